```python
import math
import jax, jax.numpy as jnp
from jax import lax
import numpy as np

D_MODEL = 1024
BATCH = 8
SEQ = 4096
DEPTH = 1

D_CONV = D_MODEL
CONV_WIDTH = 3
N_HEADS = 16
HEAD_DIM = 64
D_ATTN = N_HEADS * HEAD_DIM
ATTN_PATTERNS = ((128, 1), (512, 4), (2048, 16))
QB = 128
IN_COLS = 4 * D_CONV + 4 * D_ATTN + 2 * D_MODEL
EPS = 1e-6

kernel_name = "hybrid_gated_shortconv_dilated_alibi_attn"


def rms_norm(x, g):
    xf = x.astype(jnp.float32)
    y = xf * lax.rsqrt(jnp.mean(xf * xf, axis=-1, keepdims=True) + EPS)
    return (y * g.astype(jnp.float32)).astype(x.dtype)


def alibi_slopes(n_heads):
    return jnp.exp2(-8.0 * jnp.arange(1, n_heads + 1, dtype=jnp.float32) / n_heads)


def causal_depthwise_conv(u, w):
    return lax.conv_general_dilated(
        u, w.astype(u.dtype)[:, None, :], window_strides=(1,),
        padding=[(CONV_WIDTH - 1, 0)],
        dimension_numbers=("NWC", "WIO", "NWC"),
        feature_group_count=u.shape[-1])


def dilated_window_attention(q, k, v, window, dilation, slopes):
    B, S, H, Dh = q.shape
    d = dilation
    win = window // dilation
    assert win == QB
    L = S // d
    nb = -(-L // QB)
    Lp = nb * QB

    def to_sub(t):
        t = t.reshape(B, L, d, H, Dh)
        return jnp.pad(t, ((0, 0), (0, Lp - L), (0, 0), (0, 0), (0, 0)))

    qs = (to_sub(q) * (Dh ** -0.5)).reshape(B, nb, QB, d, H, Dh)

    def key_blocks(t):
        tp = jnp.pad(to_sub(t), ((0, 0), (QB, 0), (0, 0), (0, 0), (0, 0)))
        prev = tp[:, :Lp].reshape(B, nb, QB, d, H, Dh)
        cur = tp[:, QB:QB + Lp].reshape(B, nb, QB, d, H, Dh)
        return jnp.concatenate([prev, cur], axis=2)

    kb = key_blocks(k)
    vb = key_blocks(v)

    s = jnp.einsum("bnqrhe,bnkrhe->bnrhqk", qs, kb,
                   preferred_element_type=jnp.float32)
    q_loc = jnp.arange(QB)[:, None] + QB
    k_loc = jnp.arange(2 * QB)[None, :]
    delta = q_loc - k_loc
    key_sub = jnp.arange(nb)[:, None] * QB - QB + jnp.arange(2 * QB)[None, :]
    valid = ((delta >= 0) & (delta <= win))[None] & (key_sub >= 0)[:, None, :]
    bias = -slopes[:, None, None] * (d * delta).astype(jnp.float32)[None]
    s = jnp.where(valid[None, :, None, None], s + bias, -jnp.inf)

    m = jnp.max(s, axis=-1, keepdims=True)
    p = jnp.exp(s - m)
    den = jnp.sum(p, axis=-1)
    lse = m[..., 0] + jnp.log(den)
    o = jnp.einsum("bnrhqk,bnkrhe->bnqrhe", p, vb.astype(jnp.float32))
    o = o / jnp.transpose(den, (0, 1, 4, 2, 3))[..., None]
    o = o.reshape(B, Lp, d, H, Dh)[:, :L].reshape(B, S, H, Dh)
    lse = jnp.transpose(lse, (0, 1, 4, 2, 3)).reshape(B, Lp, d, H)[:, :L].reshape(B, S, H)
    return o, lse


def mixture_of_dilations(q, k, v, slopes):
    outs, lses = [], []
    for window, dilation in ATTN_PATTERNS:
        o, lse = dilated_window_attention(q, k, v, window, dilation, slopes)
        outs.append(o)
        lses.append(lse)
    alpha = jax.nn.softmax(jnp.stack(lses, axis=0), axis=0)
    o = jnp.sum(alpha[..., None] * jnp.stack(outs, axis=0), axis=0)
    return o.astype(q.dtype)


def hybrid_mixer(h, w_in, b_merge, conv_w, w_out_conv, w_out_attn, w_o):
    B, S, _ = h.shape
    proj = jnp.einsum("bsd,dc->bsc", h, w_in)
    splits = np.cumsum([D_CONV] * 4 + [D_ATTN] * 4 + [D_MODEL])
    xc, bg, cg, zc, q, k, v, za, g_conv, g_attn = jnp.split(proj, splits, axis=-1)

    c = causal_depthwise_conv(cg * xc, conv_w)
    y_conv = jnp.einsum("bsc,cd->bsd", jax.nn.silu(zc) * bg * c, w_out_conv)

    shp = (B, S, N_HEADS, HEAD_DIM)
    o = mixture_of_dilations(q.reshape(shp), k.reshape(shp), v.reshape(shp),
                             alibi_slopes(N_HEADS)).reshape(B, S, D_ATTN)
    y_attn = jnp.einsum("bsc,cd->bsd", jax.nn.silu(za) * o, w_out_attn)

    g_conv = jax.nn.sigmoid(g_conv + b_merge[:D_MODEL])
    g_attn = jax.nn.sigmoid(g_attn + b_merge[D_MODEL:])
    merged = g_conv * y_conv + g_attn * y_attn
    return jnp.einsum("bsd,de->bse", merged, w_o)


def _fwd_setup_inputs(seed: int = 0) -> dict:
    key = jax.random.key(seed)
    ks = jax.random.split(key, 10)
    f32 = jnp.float32
    x = jax.random.normal(ks[0], (BATCH, SEQ, D_MODEL), f32)
    norm_g = 1.0 + 0.02 * jax.random.normal(ks[1], (DEPTH, D_MODEL), f32)
    w_in = jax.random.normal(ks[2], (DEPTH, D_MODEL, IN_COLS), f32) * D_MODEL ** -0.5
    b_merge = 0.01 * jax.random.normal(ks[3], (DEPTH, 2 * D_MODEL), f32)
    conv_w = jax.random.normal(ks[4], (DEPTH, CONV_WIDTH, D_CONV), f32) * CONV_WIDTH ** -0.5
    w_out_conv = jax.random.normal(ks[5], (DEPTH, D_CONV, D_MODEL), f32) * D_CONV ** -0.5
    w_out_attn = jax.random.normal(ks[6], (DEPTH, D_ATTN, D_MODEL), f32) * D_ATTN ** -0.5
    w_o = jax.random.normal(ks[7], (DEPTH, D_MODEL, D_MODEL), f32) * D_MODEL ** -0.5
    final_g = 1.0 + 0.02 * jax.random.normal(ks[8], (D_MODEL,), f32)
    return {"x": x, "norm_g": norm_g, "w_in": w_in, "b_merge": b_merge,
            "conv_w": conv_w, "w_out_conv": w_out_conv, "w_out_attn": w_out_attn,
            "w_o": w_o, "final_g": final_g}


def _fwd_reference(x, norm_g, w_in, b_merge, conv_w, w_out_conv, w_out_attn, w_o, final_g):
    h = x
    for layer in range(DEPTH):
        u = rms_norm(h, norm_g[layer])
        h = h + hybrid_mixer(u, w_in[layer], b_merge[layer], conv_w[layer],
                             w_out_conv[layer], w_out_attn[layer], w_o[layer])
    return rms_norm(h, final_g)


import jax as _jax
import jax.numpy as _jnp

TWIN_FORMAT = 'train_step'
FWD_PARAMS = ['x', 'norm_g', 'w_in', 'b_merge', 'conv_w', 'w_out_conv', 'w_out_attn', 'w_o', 'final_g']
TWIN_WEIGHTS = ['norm_g', 'w_in', 'b_merge', 'conv_w', 'w_out_conv', 'w_out_attn', 'w_o', 'final_g']
TWIN_DIFF_INPUT = 'x'
TWIN_INPUTS = ['x', 'norm_g', 'w_in', 'b_merge', 'conv_w', 'w_out_conv', 'w_out_attn', 'w_o', 'final_g', 'loss_target', 'm_norm_g', 'm_w_in', 'm_b_merge', 'm_conv_w', 'm_w_out_conv', 'm_w_out_attn', 'm_w_o', 'm_final_g', 'v_norm_g', 'v_w_in', 'v_b_merge', 'v_conv_w', 'v_w_out_conv', 'v_w_out_attn', 'v_w_o', 'v_final_g']
TWIN_OUTPUTS = ['loss', 'grad_x', 'grad_norm_g', 'grad_w_in', 'grad_b_merge', 'grad_conv_w', 'grad_w_out_conv', 'grad_w_out_attn', 'grad_w_o', 'grad_final_g', 'delta_norm_g', 'delta_w_in', 'delta_b_merge', 'delta_conv_w', 'delta_w_out_conv', 'delta_w_out_attn', 'delta_w_o', 'delta_final_g', 'new_m_norm_g', 'new_m_w_in', 'new_m_b_merge', 'new_m_conv_w', 'new_m_w_out_conv', 'new_m_w_out_attn', 'new_m_w_o', 'new_m_final_g', 'new_v_norm_g', 'new_v_w_in', 'new_v_b_merge', 'new_v_conv_w', 'new_v_w_out_conv', 'new_v_w_out_attn', 'new_v_w_o', 'new_v_final_g']
TWIN_LEAF_KINDS = {'loss': 'loss', 'grad_x': 'grad_x', 'grad_norm_g': 'grad_w', 'grad_w_in': 'grad_w', 'grad_b_merge': 'grad_w', 'grad_conv_w': 'grad_w', 'grad_w_out_conv': 'grad_w', 'grad_w_out_attn': 'grad_w', 'grad_w_o': 'grad_w', 'grad_final_g': 'grad_w', 'delta_norm_g': 'delta_w', 'delta_w_in': 'delta_w', 'delta_b_merge': 'delta_w', 'delta_conv_w': 'delta_w', 'delta_w_out_conv': 'delta_w', 'delta_w_out_attn': 'delta_w', 'delta_w_o': 'delta_w', 'delta_final_g': 'delta_w', 'new_m_norm_g': 'new_m', 'new_m_w_in': 'new_m', 'new_m_b_merge': 'new_m', 'new_m_conv_w': 'new_m', 'new_m_w_out_conv': 'new_m', 'new_m_w_out_attn': 'new_m', 'new_m_w_o': 'new_m', 'new_m_final_g': 'new_m', 'new_v_norm_g': 'new_v', 'new_v_w_in': 'new_v', 'new_v_b_merge': 'new_v', 'new_v_conv_w': 'new_v', 'new_v_w_out_conv': 'new_v', 'new_v_w_out_attn': 'new_v', 'new_v_w_o': 'new_v', 'new_v_final_g': 'new_v'}


def _forward(args):
    return _fwd_reference(*[args[k] for k in FWD_PARAMS])


def _output_shape():
    out = _jax.eval_shape(lambda: _forward(_fwd_setup_inputs(0)))
    return out.shape, out.dtype

N_MICROBATCH = 1
ADAM_LR = 0.001
ADAM_B1 = 0.9
ADAM_B2 = 0.999
ADAM_EPS = 1e-08
ADAM_WD = 0.01
ADAM_STEP = 10
PER_EXAMPLE_BATCH_AXIS = {'x': 0, 'loss_target': 0}
SHARED_INPUTS = []
_WEIGHT_DTYPES = {'norm_g': _jnp.float32, 'w_in': _jnp.float32, 'b_merge': _jnp.float32, 'conv_w': _jnp.float32, 'w_out_conv': _jnp.float32, 'w_out_attn': _jnp.float32, 'w_o': _jnp.float32, 'final_g': _jnp.float32}
MOMENT_SCALE = {'norm_g': 1.175128e-01, 'w_in': 3.695595e-02, 'b_merge': 1.681737e-02, 'conv_w': 5.640728e-02, 'w_out_conv': 5.443986e-02, 'w_out_attn': 2.224242e-02, 'w_o': 5.774092e-02, 'final_g': 3.200220e+01}


def _to_microbatches(a, axis):
    t = _jnp.moveaxis(a, axis, 0)
    t = t.reshape((N_MICROBATCH, t.shape[0] // N_MICROBATCH) + t.shape[1:])
    return _jnp.moveaxis(t, 1, axis + 1)


def setup_inputs(seed: int = 0) -> dict:
    inp = _fwd_setup_inputs(seed)
    key = _jax.random.fold_in(_jax.random.key(seed), 7919)
    shape, _ = _output_shape()
    out = dict(inp)
    out["loss_target"] = _jax.random.normal(_jax.random.fold_in(key, 0), shape, _jnp.float32)
    for i, name in enumerate(TWIN_WEIGHTS):
        w = inp[name].astype(_jnp.float32)
        if MOMENT_SCALE is None:
            s = _jnp.sqrt(_jnp.mean(_jnp.square(w)) + 1e-30)
        else:
            s = MOMENT_SCALE[name]
        km, kv = _jax.random.split(_jax.random.fold_in(key, i + 1))
        out[name] = w
        out["m_" + name] = s * _jax.random.normal(km, w.shape, _jnp.float32)
        out["v_" + name] = (s * s) * _jax.random.uniform(kv, w.shape, _jnp.float32, 0.5, 1.5)
    if N_MICROBATCH > 1:
        for name, axis in PER_EXAMPLE_BATCH_AXIS.items():
            out[name] = _to_microbatches(out[name], axis)
    return {'x': out['x'], 'norm_g': out['norm_g'], 'w_in': out['w_in'], 'b_merge': out['b_merge'], 'conv_w': out['conv_w'], 'w_out_conv': out['w_out_conv'], 'w_out_attn': out['w_out_attn'], 'w_o': out['w_o'], 'final_g': out['final_g'], 'loss_target': out['loss_target'], 'm_norm_g': out['m_norm_g'], 'm_w_in': out['m_w_in'], 'm_b_merge': out['m_b_merge'], 'm_conv_w': out['m_conv_w'], 'm_w_out_conv': out['m_w_out_conv'], 'm_w_out_attn': out['m_w_out_attn'], 'm_w_o': out['m_w_o'], 'm_final_g': out['m_final_g'], 'v_norm_g': out['v_norm_g'], 'v_w_in': out['v_w_in'], 'v_b_merge': out['v_b_merge'], 'v_conv_w': out['v_conv_w'], 'v_w_out_conv': out['v_w_out_conv'], 'v_w_out_attn': out['v_w_out_attn'], 'v_w_o': out['v_w_o'], 'v_final_g': out['v_final_g']}


def _loss(weights, diff, rest, loss_target):
    with _jax.named_scope("forward"):
        args = {**rest, TWIN_DIFF_INPUT: diff, **{k: w.astype(_WEIGHT_DTYPES[k]) for k, w in weights.items()}}
        y = _forward(args)
    with _jax.named_scope("loss_head"):
        err = _jnp.square(y.astype(_jnp.float32) - loss_target)
        return 0.5 * _jnp.sum(_jnp.mean(err, axis=-1)) if err.ndim else 0.5 * err


def _adamw(w, g, m, v):
    m = ADAM_B1 * m + (1.0 - ADAM_B1) * g
    v = ADAM_B2 * v + (1.0 - ADAM_B2) * _jnp.square(g)
    m_hat = m / (1.0 - ADAM_B1 ** ADAM_STEP)
    v_hat = v / (1.0 - ADAM_B2 ** ADAM_STEP)
    delta = -ADAM_LR * (m_hat / (_jnp.sqrt(v_hat) + ADAM_EPS) + ADAM_WD * w)
    return delta, m, v


def reference(x, norm_g, w_in, b_merge, conv_w, w_out_conv, w_out_attn, w_o, final_g, loss_target, m_norm_g, m_w_in, m_b_merge, m_conv_w, m_w_out_conv, m_w_out_attn, m_w_o, m_final_g, v_norm_g, v_w_in, v_b_merge, v_conv_w, v_w_out_conv, v_w_out_attn, v_w_o, v_final_g):
    given = dict(x=x, norm_g=norm_g, w_in=w_in, b_merge=b_merge, conv_w=conv_w, w_out_conv=w_out_conv, w_out_attn=w_out_attn, w_o=w_o, final_g=final_g, loss_target=loss_target, m_norm_g=m_norm_g, m_w_in=m_w_in, m_b_merge=m_b_merge, m_conv_w=m_conv_w, m_w_out_conv=m_w_out_conv, m_w_out_attn=m_w_out_attn, m_w_o=m_w_o, m_final_g=m_final_g, v_norm_g=v_norm_g, v_w_in=v_w_in, v_b_merge=v_b_merge, v_conv_w=v_conv_w, v_w_out_conv=v_w_out_conv, v_w_out_attn=v_w_out_attn, v_w_o=v_w_o, v_final_g=v_final_g)
    weights = {n: given[n] for n in TWIN_WEIGHTS}
    shared = {n: given[n] for n in SHARED_INPUTS}
    per_example = {n: given[n] for n in ['x']}
    grad_fn = _jax.value_and_grad(_loss, argnums=(0, 1))

    def one_microbatch(ex, loss_target):
        ex = dict(ex)
        diff = ex.pop(TWIN_DIFF_INPUT)
        return grad_fn(weights, diff, {**shared, **ex}, loss_target)

    if N_MICROBATCH == 1:
        loss, (grad_w, grad_x) = one_microbatch(per_example, given["loss_target"])
    else:
        def body(carry, xs):
            loss_sum, grad_sum = carry
            l_k, (gw_k, gx_k) = one_microbatch(xs[0], xs[1])
            with _jax.named_scope("update"):
                return (loss_sum + l_k, _jax.tree.map(_jnp.add, grad_sum, gw_k)), gx_k

        init = (_jnp.zeros((), _jnp.float32), _jax.tree.map(_jnp.zeros_like, weights))
        (loss, grad_w), grad_x = _jax.lax.scan(body, init, (per_example, given["loss_target"]))
    with _jax.named_scope("update"):
        delta_w, new_m, new_v = {}, {}, {}
        for n in TWIN_WEIGHTS:
            delta_w[n], new_m[n], new_v[n] = _adamw(weights[n], grad_w[n], given["m_" + n], given["v_" + n])
    return (loss, grad_x, *[grad_w[n] for n in TWIN_WEIGHTS], *[delta_w[n] for n in TWIN_WEIGHTS],
            *[new_m[n] for n in TWIN_WEIGHTS], *[new_v[n] for n in TWIN_WEIGHTS])
```

```python
import functools

import numpy as np
import jax
import jax.numpy as jnp
from jax import lax
from jax.experimental import pallas as pl
from jax.experimental.pallas import tpu as pltpu

F32 = jnp.float32
BF16 = jnp.bfloat16
MXU_DTYPE = jnp.bfloat16
ACT_DTYPE = jnp.bfloat16

D_MODEL = 1024
N_HEADS = 16
HEAD_DIM = 64
QB = 128
N_RES = 16
LANES = 128
HP = N_HEADS * HEAD_DIM // LANES
IN_COLS = 10 * D_MODEL
SHARD_COLS = IN_COLS // 4
EPS = 1e-6
NEG = -1e30

ADAM_LR, ADAM_B1, ADAM_B2, ADAM_EPS, ADAM_WD, ADAM_STEP = 0.001, 0.9, 0.999, 1e-08, 0.01, 10

PATTERNS = {1: (16, 16), 4: (4, 32), 16: (1, 128)}

_NN = (((1,), (0,)), ((), ()))
_NT = (((1,), (1,)), ((), ()))


def _dot(a, b):
    return lax.dot_general(a.astype(MXU_DTYPE), b.astype(MXU_DTYPE), _NN, preferred_element_type=F32)


def _dot_nt(a, b):
    return lax.dot_general(a.astype(MXU_DTYPE), b.astype(MXU_DTYPE), _NT, preferred_element_type=F32)


def _split3(x):
    hi = x.astype(BF16)
    r1 = x - hi.astype(F32)
    mid = r1.astype(BF16)
    lo = (r1 - mid.astype(F32)).astype(BF16)
    return hi, mid, lo


def _select_rows(sel, x):
    return sum(lax.dot_general(sel, t, _NN, preferred_element_type=F32) for t in _split3(x))


def _select_cols(x, sel):
    return sum(lax.dot_general(t, sel, _NN, preferred_element_type=F32) for t in _split3(x))


def _sigmoid(z):
    return 1.0 / (1.0 + jnp.exp(-z))


def _perm_matrix():
    idx = np.arange(256)
    p = np.zeros((256, 256), np.float32)
    p[(idx % 16) * 16 + idx // 16, idx] = 1.0
    return jnp.asarray(p, BF16)


def _head_expand_matrix():
    e = np.zeros((LANES, D_MODEL), np.float32)
    for h in range(N_HEADS):
        e[8 * h, HEAD_DIM * h:HEAD_DIM * (h + 1)] = 1.0
    return jnp.asarray(e, BF16)


def _head_sum_matrix():
    e = np.zeros((D_MODEL, LANES), np.float32)
    for h in range(N_HEADS):
        e[HEAD_DIM * h:HEAD_DIM * (h + 1), 8 * h:8 * (h + 1)] = 1.0
    return jnp.asarray(e, BF16)


def _attn_tables(d):
    g_n, rq = PATTERNS[d]
    q_n = g_n * rq
    gq, iq = np.arange(q_n) // rq, np.arange(q_n) % rq

    def tab(kn, base):
        k_n = g_n * kn
        gk, jk = np.arange(k_n) // kn, np.arange(k_n) % kn
        delta = g_n * (base + iq[:, None] - jk[None, :]) + gq[:, None] - gk[None, :]
        valid = (delta >= 0) & (delta <= QB)
        dist = np.where(valid, d * delta, 0).astype(np.float32)
        madd = np.where(valid, 0.0, NEG).astype(np.float32)
        return dist, madd

    d0, m0 = tab(rq, 0)
    d1, m1 = tab(2 * rq, rq)
    return d0, m0, d1, m1


def _alibi_slopes():
    return jnp.exp2(-8.0 * jnp.arange(1, N_HEADS + 1, dtype=F32) / N_HEADS)


def _to_residue_major(x, tgt):
    s_n, c_n = x.shape
    lr = s_n // N_RES

    def body(p_ref, x_ref, t_ref, xo_ref, to_ref):
        pm = p_ref[...]
        xo_ref[...] = _select_rows(pm, x_ref[...]).reshape(16, 16, c_n)
        to_ref[...] = _select_rows(pm, t_ref[...]).reshape(16, 16, c_n)

    nat = pl.BlockSpec((256, c_n), lambda i: (i, 0))
    res = pl.BlockSpec((16, 16, c_n), lambda i: (0, i, 0))
    xo, to = pl.pallas_call(
        body, grid=(s_n // 256,),
        in_specs=[pl.BlockSpec((256, 256), lambda i: (0, 0)), nat, nat],
        out_specs=[res, res],
        out_shape=[jax.ShapeDtypeStruct((16, lr, c_n), F32)] * 2,
        name="perm_in",
    )(_perm_matrix(), x, tgt)
    return xo.reshape(s_n, c_n), to.reshape(s_n, c_n)


def _to_natural(gxp):
    s_n, c_n = gxp.shape
    lr = s_n // N_RES

    def body(p_ref, g_ref, o_ref):
        o_ref[...] = _select_rows(p_ref[...], g_ref[...].reshape(256, c_n))

    return pl.pallas_call(
        body, grid=(s_n // 256,),
        in_specs=[pl.BlockSpec((256, 256), lambda i: (0, 0)),
                  pl.BlockSpec((16, 16, c_n), lambda i: (0, i, 0))],
        out_specs=pl.BlockSpec((256, c_n), lambda i: (i, 0)),
        out_shape=jax.ShapeDtypeStruct((s_n, c_n), F32),
        name="perm_out",
    )(_perm_matrix(), gxp.reshape(16, lr, c_n))


def _rms_in(xp, norm_g):
    s_n, c_n = xp.shape
    tm = 512

    def body(x_ref, g_ref, u_ref, ut_ref):
        x = x_ref[...]
        r = lax.rsqrt(jnp.mean(x * x, axis=-1, keepdims=True) + EPS)
        u = x * r * g_ref[...]
        u_ref[...] = u.astype(u_ref.dtype)
        ut_ref[...] = u.T.astype(ut_ref.dtype)

    return pl.pallas_call(
        body, grid=(s_n // tm,),
        in_specs=[pl.BlockSpec((tm, c_n), lambda i: (i, 0)), pl.BlockSpec((1, c_n), lambda i: (0, 0))],
        out_specs=[pl.BlockSpec((tm, c_n), lambda i: (i, 0)), pl.BlockSpec((c_n, tm), lambda i: (0, i))],
        out_shape=[jax.ShapeDtypeStruct((s_n, c_n), ACT_DTYPE), jax.ShapeDtypeStruct((c_n, s_n), ACT_DTYPE)],
        name="rms_in",
    )(xp, norm_g)


def _in_proj(u, w4):
    s_n = u.shape[0]
    tn, cm = 512, 512
    per = SHARD_COLS // tn

    def body(a_ref, b_ref, o_ref):
        b = b_ref[...]
        for c in range(s_n // cm):
            o_ref[c * cm:(c + 1) * cm, :] = _dot(a_ref[c * cm:(c + 1) * cm, :], b).astype(o_ref.dtype)

    return pl.pallas_call(
        body, grid=(IN_COLS // tn,),
        in_specs=[pl.BlockSpec((s_n, D_MODEL), lambda n: (0, 0)),
                  pl.BlockSpec((None, D_MODEL, tn), lambda n: (n // per, 0, n % per))],
        out_specs=pl.BlockSpec((s_n, tn), lambda n: (0, n)),
        out_shape=jax.ShapeDtypeStruct((s_n, IN_COLS), ACT_DTYPE),
        name="in_proj",
    )(u, w4)


def _conv_terms(xc_ref, cg_ref, r, row, lr):
    def a_of(q):
        return cg_ref[q].astype(F32) * xc_ref[q].astype(F32)

    def shift_down(v):
        return jnp.where(row >= 1, pltpu.roll(v, 1, 0), 0.0)

    a = a_of(r)
    am1 = a_of(r - 1) if r >= 1 else shift_down(a_of(N_RES - 1))
    am2 = a_of(r - 2) if r >= 2 else shift_down(a_of(N_RES - 2 + r))
    return a, am1, am2


def _conv_fwd(proj, conv_w):
    s_n = proj.shape[0]
    lr = s_n // N_RES
    pv = proj.reshape(N_RES, lr, IN_COLS)

    def body(xc_ref, bg_ref, cg_ref, zc_ref, w_ref, hc_ref, hct_ref):
        w = w_ref[...]
        row = lax.broadcasted_iota(jnp.int32, (lr, LANES), 0)
        for r in range(N_RES):
            a, am1, am2 = _conv_terms(xc_ref, cg_ref, r, row, lr)
            c = w[0:1] * am2 + w[1:2] * am1 + w[2:3] * a
            z = zc_ref[r].astype(F32)
            hc = z * _sigmoid(z) * bg_ref[r].astype(F32) * c
            hc_ref[r] = hc.astype(hc_ref.dtype)
            hct_ref[:, r * lr:(r + 1) * lr] = hc.T.astype(hct_ref.dtype)

    def col(part):
        return pl.BlockSpec((N_RES, lr, LANES), lambda j: (0, 0, part * 8 + j))

    hc, hct = pl.pallas_call(
        body, grid=(D_MODEL // LANES,),
        in_specs=[col(0), col(1), col(2), col(3), pl.BlockSpec((3, LANES), lambda j: (0, j))],
        out_specs=[pl.BlockSpec((N_RES, lr, LANES), lambda j: (0, 0, j)),
                   pl.BlockSpec((LANES, s_n), lambda j: (j, 0))],
        out_shape=[jax.ShapeDtypeStruct((N_RES, lr, D_MODEL), ACT_DTYPE),
                   jax.ShapeDtypeStruct((D_MODEL, s_n), ACT_DTYPE)],
        name="conv_fwd",
    )(pv, pv, pv, pv, conv_w)
    return hc.reshape(s_n, D_MODEL), hct


def _pattern_view_shape(s_n, c_n, g_n, lead=()):
    lr = s_n // N_RES
    return (*lead, 4, 4, lr, c_n) if g_n == 4 else (*lead, N_RES, lr, c_n)


def _pattern_view(a, g_n, lead=()):
    return a.reshape(_pattern_view_shape(a.shape[-2], a.shape[-1], g_n, lead))


def _pattern_spec(g_n, lr, col_of_hp, lead=()):
    nl = len(lead)
    z = (0,) * nl
    if g_n == 16:
        return pl.BlockSpec((*lead, 16, lr, LANES), lambda r, hp: (*z, 0, 0, col_of_hp(hp)))
    if g_n == 4:
        return pl.BlockSpec((*lead, 4, None, lr, LANES), lambda r, hp: (*z, 0, r, 0, col_of_hp(hp)))
    return pl.BlockSpec((*lead, 1, lr, LANES), lambda r, hp: (*z, r, 0, col_of_hp(hp)))


def _attn_fwd(proj, slopes, d):
    g_n, rq = PATTERNS[d]
    s_n = proj.shape[0]
    lr = s_n // N_RES
    nb = lr // rq
    q_n = g_n * rq
    d0, m0, d1, m1 = _attn_tables(d)

    def body(sl_ref, q_ref, k_ref, v_ref, d0_ref, m0_ref, d1_ref, m1_ref, o_ref, lse_ref, b0_ref, b1_ref):
        hp = pl.program_id(1)

        @pl.when(hp == 0)
        def _():
            lse_ref[...] = jnp.zeros(lse_ref.shape, F32)

        for h in (0, 1):
            slope = sl_ref[2 * hp + h]
            b0_ref[h] = m0_ref[...] - slope * d0_ref[...]
            b1_ref[h] = m1_ref[...] - slope * d1_ref[...]

        lane = lax.broadcasted_iota(jnp.int32, (q_n, LANES), 1)
        low = lane < HEAD_DIM
        grp = lane // 8

        def block(n0, kb, vb, b_ref):
            qb = q_ref[:, pl.ds(n0, rq), :].reshape(q_n, LANES)
            outs = []
            upd = jnp.zeros((q_n, LANES), F32)
            for h in (0, 1):
                msk = low if h == 0 else jnp.logical_not(low)
                qm = jnp.where(msk, qb, jnp.zeros_like(qb))
                s = _dot_nt(qm, kb) * 0.125 + b_ref[h]
                m = jnp.max(s, axis=1, keepdims=True)
                p = jnp.exp(s - m)
                l = jnp.sum(p, axis=1, keepdims=True)
                outs.append(_dot(p, vb) * (1.0 / l))
                upd = upd + jnp.where(grp == 2 * hp + h, m + jnp.log(l), 0.0)
            o = jnp.where(low, outs[0], outs[1])
            o_ref[:, pl.ds(n0, rq), :] = o.reshape(g_n, rq, LANES)
            lse_ref[:, pl.ds(n0, rq), :] += upd.reshape(g_n, rq, LANES)

        block(0, k_ref[:, 0:rq, :].reshape(q_n, LANES), v_ref[:, 0:rq, :].reshape(q_n, LANES), b0_ref)

        def loop(n, carry):
            st = pl.multiple_of((n - 1) * rq, rq)
            n0 = pl.multiple_of(n * rq, rq)
            block(n0, k_ref[:, pl.ds(st, 2 * rq), :].reshape(2 * q_n, LANES),
                  v_ref[:, pl.ds(st, 2 * rq), :].reshape(2 * q_n, LANES), b1_ref)
            return carry

        lax.fori_loop(1, nb, loop, 0)

    pv = _pattern_view(proj, g_n)
    full = lambda a: pl.BlockSpec(a.shape, lambda r, hp: (0, 0))
    o, lse = pl.pallas_call(
        body, grid=(N_RES // g_n, HP),
        in_specs=[pl.BlockSpec(memory_space=pltpu.SMEM),
                  _pattern_spec(g_n, lr, lambda hp: 32 + hp),
                  _pattern_spec(g_n, lr, lambda hp: 40 + hp),
                  _pattern_spec(g_n, lr, lambda hp: 48 + hp),
                  full(d0), full(m0), full(d1), full(m1)],
        out_specs=[_pattern_spec(g_n, lr, lambda hp: hp), _pattern_spec(g_n, lr, lambda hp: 0)],
        out_shape=[jax.ShapeDtypeStruct(_pattern_view_shape(s_n, D_MODEL, g_n), F32),
                   jax.ShapeDtypeStruct(_pattern_view_shape(s_n, LANES, g_n), F32)],
        scratch_shapes=[pltpu.VMEM((2, q_n, q_n), F32), pltpu.VMEM((2, q_n, 2 * q_n), F32)],
        name=f"attn_fwd_d{d}",
    )(slopes, pv, pv, pv, d0, m0, d1, m1)
    return o.reshape(s_n, D_MODEL), lse.reshape(s_n, LANES)


def _attn_combine(outs, lses, proj):
    s_n = proj.shape[0]
    tm = 512

    def body(o1_ref, o2_ref, o3_ref, l1_ref, l2_ref, l3_ref, za_ref, e_ref, o_ref, lse_ref, ha_ref, hat_ref):
        ls = [l1_ref[...], l2_ref[...], l3_ref[...]]
        mx = jnp.maximum(jnp.maximum(ls[0], ls[1]), ls[2])
        den = sum(jnp.exp(l - mx) for l in ls)
        lse = mx + jnp.log(den)
        lse_ref[...] = lse
        o = jnp.zeros((tm, D_MODEL), F32)
        for l, oref in zip(ls, (o1_ref, o2_ref, o3_ref)):
            o = o + _select_cols(jnp.exp(l - lse), e_ref[...]) * oref[...]
        o_ref[...] = o
        z = za_ref[...].astype(F32)
        ha = z * _sigmoid(z) * o
        ha_ref[...] = ha.astype(ha_ref.dtype)
        hat_ref[...] = ha.T.astype(hat_ref.dtype)

    row = lambda w: pl.BlockSpec((tm, w), lambda i: (i, 0))
    return pl.pallas_call(
        body, grid=(s_n // tm,),
        in_specs=[row(D_MODEL)] * 3 + [row(LANES)] * 3
        + [pl.BlockSpec((tm, D_MODEL), lambda i: (i, 7)), pl.BlockSpec((LANES, D_MODEL), lambda i: (0, 0))],
        out_specs=[row(D_MODEL), row(LANES), row(D_MODEL), pl.BlockSpec((D_MODEL, tm), lambda i: (0, i))],
        out_shape=[jax.ShapeDtypeStruct((s_n, D_MODEL), F32), jax.ShapeDtypeStruct((s_n, LANES), F32),
                   jax.ShapeDtypeStruct((s_n, D_MODEL), ACT_DTYPE), jax.ShapeDtypeStruct((D_MODEL, s_n), ACT_DTYPE)],
        name="attn_combine",
    )(*outs, *lses, proj, _head_expand_matrix())


def _gates(gc_ref, ga_ref, b_ref):
    b = b_ref[...]
    gc = _sigmoid(gc_ref[...].astype(F32) + b[:, :D_MODEL])
    ga = _sigmoid(ga_ref[...].astype(F32) + b[:, D_MODEL:])
    return gc, ga


def _merge_loss(hc, ha, woc, woa, wo, proj, b_merge, xp, final_g, tgt):
    s_n = xp.shape[0]
    tm = 512

    def body(hc_ref, ha_ref, woc_ref, woa_ref, wo_ref, gc_ref, ga_ref, b_ref, x_ref, gf_ref, t_ref,
             yc_ref, ya_ref, mg_ref, mgt_ref, dh_ref, dhb_ref, dgf_ref, loss_ref):
        i = pl.program_id(0)

        @pl.when(i == 0)
        def _():
            dgf_ref[...] = jnp.zeros(dgf_ref.shape, F32)
            loss_ref[...] = jnp.zeros(loss_ref.shape, F32)

        yc = _dot(hc_ref[...], woc_ref[...])
        ya = _dot(ha_ref[...], woa_ref[...])
        gc, ga = _gates(gc_ref, ga_ref, b_ref)
        mg = gc * yc + ga * ya
        yc_ref[...] = yc.astype(yc_ref.dtype)
        ya_ref[...] = ya.astype(ya_ref.dtype)
        mg_ref[...] = mg.astype(mg_ref.dtype)
        mgt_ref[...] = mg.T.astype(mgt_ref.dtype)
        h2 = x_ref[...] + _dot(mg, wo_ref[...])
        r2 = lax.rsqrt(jnp.mean(h2 * h2, axis=-1, keepdims=True) + EPS)
        nrm = h2 * r2
        gf = gf_ref[...]
        err = nrm * gf - t_ref[...]
        e2 = (err * err).reshape(tm // 8, 8, D_MODEL).sum(axis=0)
        loss_ref[...] += sum(e2[:, c * LANES:(c + 1) * LANES] for c in range(D_MODEL // LANES))
        dy = err * (1.0 / D_MODEL)
        dgf_ref[...] += jnp.sum(dy * nrm, axis=0, keepdims=True)
        dn = dy * gf
        dh2 = r2 * (dn - nrm * jnp.mean(dn * nrm, axis=-1, keepdims=True))
        dh_ref[...] = dh2
        dhb_ref[...] = dh2.astype(dhb_ref.dtype)

    row = pl.BlockSpec((tm, D_MODEL), lambda i: (i, 0))
    wsp = pl.BlockSpec((D_MODEL, D_MODEL), lambda i: (0, 0))
    vec = lambda w: pl.BlockSpec((1, w), lambda i: (0, 0))
    act = jax.ShapeDtypeStruct((s_n, D_MODEL), ACT_DTYPE)
    return pl.pallas_call(
        body, grid=(s_n // tm,),
        in_specs=[row, row, wsp, wsp, wsp,
                  pl.BlockSpec((tm, D_MODEL), lambda i: (i, 8)), pl.BlockSpec((tm, D_MODEL), lambda i: (i, 9)),
                  vec(2 * D_MODEL), row, vec(D_MODEL), row],
        out_specs=[row, row, row, pl.BlockSpec((D_MODEL, tm), lambda i: (0, i)), row, row,
                   vec(D_MODEL), pl.BlockSpec((8, LANES), lambda i: (0, 0))],
        out_shape=[act, act, act, jax.ShapeDtypeStruct((D_MODEL, s_n), ACT_DTYPE),
                   jax.ShapeDtypeStruct((s_n, D_MODEL), F32), act,
                   jax.ShapeDtypeStruct((1, D_MODEL), F32), jax.ShapeDtypeStruct((8, LANES), F32)],
        name="merge_loss",
    )(hc, ha, woc, woa, wo, proj, proj, b_merge, xp, final_g, tgt)


def _merge_bwd(dh2b, wo, woc, woa, yc, ya, proj, b_merge, o):
    s_n = dh2b.shape[0]
    tm = 512

    def body(dh_ref, wo_ref, woc_ref, woa_ref, yc_ref, ya_ref, gc_ref, ga_ref, b_ref, o_ref, za_ref, e_ref,
             dyc_ref, dya_ref, dhc_ref, do_ref, dsum_ref, db3_ref, dbias_ref):
        i = pl.program_id(0)

        @pl.when(i == 0)
        def _():
            dbias_ref[...] = jnp.zeros(dbias_ref.shape, F32)

        dmg = _dot_nt(dh_ref[...], wo_ref[...])
        gc, ga = _gates(gc_ref, ga_ref, b_ref)
        dgc = dmg * yc_ref[...].astype(F32) * gc * (1.0 - gc)
        dga = dmg * ya_ref[...].astype(F32) * ga * (1.0 - ga)
        dbias_ref[:, :D_MODEL] += jnp.sum(dgc, axis=0, keepdims=True)
        dbias_ref[:, D_MODEL:] += jnp.sum(dga, axis=0, keepdims=True)
        dyc = dmg * gc
        dya = dmg * ga
        dyc_ref[...] = dyc.astype(dyc_ref.dtype)
        dya_ref[...] = dya.astype(dya_ref.dtype)
        dhc_ref[...] = _dot_nt(dyc, woc_ref[...]).astype(dhc_ref.dtype)
        dha = _dot_nt(dya, woa_ref[...])
        z = za_ref[...].astype(F32)
        sg = _sigmoid(z)
        ov = o_ref[...]
        dout = dha * z * sg
        do_ref[...] = dout.astype(do_ref.dtype)
        dsum_ref[...] = _select_cols(dout * ov, e_ref[...])
        db3_ref[0] = (dha * ov * sg * (1.0 + z * (1.0 - sg))).astype(db3_ref.dtype)
        db3_ref[1] = dgc.astype(db3_ref.dtype)
        db3_ref[2] = dga.astype(db3_ref.dtype)

    row = pl.BlockSpec((tm, D_MODEL), lambda i: (i, 0))
    wsp = pl.BlockSpec((D_MODEL, D_MODEL), lambda i: (0, 0))
    act = jax.ShapeDtypeStruct((s_n, D_MODEL), ACT_DTYPE)
    return pl.pallas_call(
        body, grid=(s_n // tm,),
        in_specs=[row, wsp, wsp, wsp, row, row,
                  pl.BlockSpec((tm, D_MODEL), lambda i: (i, 8)), pl.BlockSpec((tm, D_MODEL), lambda i: (i, 9)),
                  pl.BlockSpec((1, 2 * D_MODEL), lambda i: (0, 0)), row,
                  pl.BlockSpec((tm, D_MODEL), lambda i: (i, 7)), pl.BlockSpec((D_MODEL, LANES), lambda i: (0, 0))],
        out_specs=[row, row, row, row, pl.BlockSpec((tm, LANES), lambda i: (i, 0)),
                   pl.BlockSpec((3, tm, D_MODEL), lambda i: (0, i, 0)),
                   pl.BlockSpec((1, 2 * D_MODEL), lambda i: (0, 0))],
        out_shape=[act, act, act, act, jax.ShapeDtypeStruct((s_n, LANES), F32),
                   jax.ShapeDtypeStruct((3, s_n, D_MODEL), ACT_DTYPE),
                   jax.ShapeDtypeStruct((1, 2 * D_MODEL), F32)],
        name="merge_bwd",
    )(dh2b, wo, woc, woa, yc, ya, proj, proj, b_merge, o, proj, _head_sum_matrix())


def _mm_lhs_resident(a, b, tn, name):
    m_n, k_n = a.shape
    n_n = b.shape[1]

    def body(a_ref, b_ref, o_ref):
        o_ref[...] = _dot(a_ref[...], b_ref[...])

    return pl.pallas_call(
        body, grid=(n_n // tn,),
        in_specs=[pl.BlockSpec((m_n, k_n), lambda n: (0, 0)), pl.BlockSpec((k_n, tn), lambda n: (0, n))],
        out_specs=pl.BlockSpec((m_n, tn), lambda n: (0, n)),
        out_shape=jax.ShapeDtypeStruct((m_n, n_n), F32),
        name=name,
    )(a, b)


def _conv_bwd(proj, conv_w, dhc):
    s_n = proj.shape[0]
    lr = s_n // N_RES
    pv = proj.reshape(N_RES, lr, IN_COLS)

    def body(xc_ref, bg_ref, cg_ref, zc_ref, w_ref, dhc_ref, da4_ref, dw_ref, dc_ref):
        w = w_ref[...]
        row = lax.broadcasted_iota(jnp.int32, (lr, LANES), 0)
        dw = [jnp.zeros((1, LANES), F32) for _ in range(3)]
        for r in range(N_RES):
            a, am1, am2 = _conv_terms(xc_ref, cg_ref, r, row, lr)
            c = w[0:1] * am2 + w[1:2] * am1 + w[2:3] * a
            z = zc_ref[r].astype(F32)
            sg = _sigmoid(z)
            sz = z * sg
            bg = bg_ref[r].astype(F32)
            dh = dhc_ref[r].astype(F32)
            da4_ref[1, r] = (dh * sz * c).astype(da4_ref.dtype)
            da4_ref[3, r] = (dh * bg * c * sg * (1.0 + z * (1.0 - sg))).astype(da4_ref.dtype)
            dc = dh * sz * bg
            dc_ref[r] = dc
            dw[0] = dw[0] + jnp.sum(dc * am2, axis=0, keepdims=True)
            dw[1] = dw[1] + jnp.sum(dc * am1, axis=0, keepdims=True)
            dw[2] = dw[2] + jnp.sum(dc * a, axis=0, keepdims=True)
        dw_ref[0:1, :] = dw[0]
        dw_ref[1:2, :] = dw[1]
        dw_ref[2:3, :] = dw[2]

        def shift_up(v):
            return jnp.where(row < lr - 1, pltpu.roll(v, lr - 1, 0), 0.0)

        for r in range(N_RES):
            dp1 = dc_ref[r + 1] if r + 1 < N_RES else shift_up(dc_ref[0])
            dp2 = dc_ref[r + 2] if r + 2 < N_RES else shift_up(dc_ref[r + 2 - N_RES])
            da = w[2:3] * dc_ref[r] + w[1:2] * dp1 + w[0:1] * dp2
            da4_ref[0, r] = (da * cg_ref[r].astype(F32)).astype(da4_ref.dtype)
            da4_ref[2, r] = (da * xc_ref[r].astype(F32)).astype(da4_ref.dtype)

    def col(part):
        return pl.BlockSpec((N_RES, lr, LANES), lambda j: (0, 0, part * 8 + j))

    da4, dw = pl.pallas_call(
        body, grid=(D_MODEL // LANES,),
        in_specs=[col(0), col(1), col(2), col(3), pl.BlockSpec((3, LANES), lambda j: (0, j)),
                  pl.BlockSpec((N_RES, lr, LANES), lambda j: (0, 0, j))],
        out_specs=[pl.BlockSpec((4, N_RES, lr, LANES), lambda j: (0, 0, 0, j)),
                   pl.BlockSpec((3, LANES), lambda j: (0, j))],
        out_shape=[jax.ShapeDtypeStruct((4, N_RES, lr, D_MODEL), ACT_DTYPE),
                   jax.ShapeDtypeStruct((3, D_MODEL), F32)],
        scratch_shapes=[pltpu.VMEM((N_RES, lr, LANES), F32)],
        name="conv_bwd",
    )(pv, pv, pv, pv, conv_w, dhc.reshape(N_RES, lr, D_MODEL))
    return da4.reshape(4, s_n, D_MODEL), dw


def _attn_bwd(proj, dout, lse, dsum, slopes, d):
    g_n, rq = PATTERNS[d]
    s_n = proj.shape[0]
    lr = s_n // N_RES
    nb = lr // rq
    q_n = g_n * rq
    d0, m0, d1, m1 = (np.ascontiguousarray(t.T) for t in _attn_tables(d))

    def body(sl_ref, q_ref, k_ref, v_ref, do_ref, lse_ref, ds_ref, d0_ref, m0_ref, d1_ref, m1_ref, out_ref,
             b0_ref, b1_ref, lt_ref, dt_ref, dk_ref, dv_ref):
        hp = pl.program_id(1)
        for h in (0, 1):
            slope = sl_ref[2 * hp + h]
            b0_ref[h] = m0_ref[...] - slope * d0_ref[...]
            b1_ref[h] = m1_ref[...] - slope * d1_ref[...]
        dk_ref[...] = jnp.zeros(dk_ref.shape, F32)
        dv_ref[...] = jnp.zeros(dv_ref.shape, F32)
        lane = lax.broadcasted_iota(jnp.int32, (q_n, LANES), 1)
        low = lane < HEAD_DIM
        row16 = pl.multiple_of(16 * hp, 16)

        def block(n0, k0, kn, b_ref):
            k_n = g_n * kn
            qb = q_ref[:, pl.ds(n0, rq), :].reshape(q_n, LANES)
            dob = do_ref[:, pl.ds(n0, rq), :].reshape(q_n, LANES)
            kb = k_ref[:, pl.ds(k0, kn), :].reshape(k_n, LANES)
            vb = v_ref[:, pl.ds(k0, kn), :].reshape(k_n, LANES)
            lt_ref[...] = lse_ref[:, pl.ds(n0, rq), :].reshape(q_n, LANES).T
            dt_ref[...] = ds_ref[:, pl.ds(n0, rq), :].reshape(q_n, LANES).T
            l16 = lt_ref[pl.ds(row16, 16), :]
            s16 = dt_ref[pl.ds(row16, 16), :]
            dq = jnp.zeros((q_n, LANES), F32)
            for h in (0, 1):
                msk = low if h == 0 else jnp.logical_not(low)
                qm = jnp.where(msk, qb, jnp.zeros_like(qb))
                dom = jnp.where(msk, dob, jnp.zeros_like(dob))
                st = _dot_nt(kb, qm) * 0.125 + b_ref[h] - l16[8 * h:8 * h + 1, :]
                pt = jnp.exp(st)
                dpt = _dot_nt(vb, dom)
                dst = pt * (dpt - s16[8 * h:8 * h + 1, :]) * 0.125
                dv_ref[:, pl.ds(k0, kn), :] += _dot(pt, dom).reshape(g_n, kn, LANES)
                dk_ref[:, pl.ds(k0, kn), :] += _dot(dst, qm).reshape(g_n, kn, LANES)
                dq = dq + jnp.where(msk, _dot(dst.T, kb), 0.0)
            out_ref[0, :, pl.ds(n0, rq), :] = dq.reshape(g_n, rq, LANES)

        block(0, 0, rq, b0_ref)

        def loop(n, carry):
            block(pl.multiple_of(n * rq, rq), pl.multiple_of((n - 1) * rq, rq), 2 * rq, b1_ref)
            return carry

        lax.fori_loop(1, nb, loop, 0)
        out_ref[1] = dk_ref[...]
        out_ref[2] = dv_ref[...]

    pv = _pattern_view(proj, g_n)
    full = lambda a: pl.BlockSpec(a.shape, lambda r, hp: (0, 0))
    out_view = _pattern_view_shape(s_n, D_MODEL, g_n, lead=(3,))
    out = pl.pallas_call(
        body, grid=(N_RES // g_n, HP),
        in_specs=[pl.BlockSpec(memory_space=pltpu.SMEM),
                  _pattern_spec(g_n, lr, lambda hp: 32 + hp),
                  _pattern_spec(g_n, lr, lambda hp: 40 + hp),
                  _pattern_spec(g_n, lr, lambda hp: 48 + hp),
                  _pattern_spec(g_n, lr, lambda hp: hp),
                  _pattern_spec(g_n, lr, lambda hp: 0),
                  _pattern_spec(g_n, lr, lambda hp: 0),
                  full(d0), full(m0), full(d1), full(m1)],
        out_specs=_pattern_spec(g_n, lr, lambda hp: hp, lead=(3,)),
        out_shape=jax.ShapeDtypeStruct(out_view, F32),
        scratch_shapes=[pltpu.VMEM((2, q_n, q_n), F32), pltpu.VMEM((2, 2 * q_n, q_n), F32),
                        pltpu.VMEM((LANES, q_n), F32), pltpu.VMEM((LANES, q_n), F32),
                        pltpu.VMEM((g_n, lr, LANES), F32), pltpu.VMEM((g_n, lr, LANES), F32)],
        name=f"attn_bwd_d{d}",
    )(slopes, pv, pv, pv, _pattern_view(dout, g_n), _pattern_view(lse, g_n), _pattern_view(dsum, g_n),
      d0, m0, d1, m1)
    return out.reshape(3, s_n, D_MODEL)


def _sum3(a, b, c):
    _, s_n, c_n = a.shape
    tm = 512

    def body(a_ref, b_ref, c_ref, o_ref):
        o_ref[...] = (a_ref[...] + b_ref[...] + c_ref[...]).astype(o_ref.dtype)

    spec = pl.BlockSpec((1, tm, c_n), lambda p, i: (p, i, 0))
    return pl.pallas_call(
        body, grid=(3, s_n // tm), in_specs=[spec] * 3, out_specs=spec,
        out_shape=jax.ShapeDtypeStruct(a.shape, ACT_DTYPE), name="sum_dqkv",
    )(a, b, c)


def _part_index(step, per, lo, n):
    return jnp.clip(step // per - lo, 0, n - 1)


def _dw_in(ut, da4, dc3, db3):
    s_n = ut.shape[1]
    tn = 256
    per = D_MODEL // tn
    shard_blocks = SHARD_COLS // tn

    def body(a_ref, p0_ref, p1_ref, p2_ref, o_ref):
        part = pl.program_id(0) // per

        @pl.when(part < 4)
        def _():
            o_ref[...] = _dot(a_ref[...], p0_ref[...])

        @pl.when((part >= 4) & (part < 7))
        def _():
            o_ref[...] = _dot(a_ref[...], p1_ref[...])

        @pl.when(part >= 7)
        def _():
            o_ref[...] = _dot(a_ref[...], p2_ref[...])

    def pspec(lo, n):
        return pl.BlockSpec((None, s_n, tn), lambda j: (_part_index(j, per, lo, n), 0, j % per))

    return pl.pallas_call(
        body, grid=(IN_COLS // tn,),
        in_specs=[pl.BlockSpec((D_MODEL, s_n), lambda j: (0, 0)), pspec(0, 4), pspec(4, 3), pspec(7, 3)],
        out_specs=pl.BlockSpec((None, D_MODEL, tn), lambda j: (j // shard_blocks, 0, j % shard_blocks)),
        out_shape=jax.ShapeDtypeStruct((4, D_MODEL, SHARD_COLS), F32),
        name="dw_in",
    )(ut, da4, dc3, db3)


def _input_grad(da4, dc3, db3, w4, xp, norm_g, dh2):
    s_n = xp.shape[0]
    tm, tk = 1024, 512
    per = D_MODEL // tk
    nk = IN_COLS // tk
    shard_blocks = SHARD_COLS // tk

    def body(p0_ref, p1_ref, p2_ref, w_ref, x_ref, g_ref, dh_ref, gx_ref, dg_ref, acc_ref):
        m_i, k_i = pl.program_id(0), pl.program_id(1)
        part = k_i // per

        @pl.when(k_i == 0)
        def _():
            acc_ref[...] = jnp.zeros(acc_ref.shape, F32)

        @pl.when((m_i == 0) & (k_i == 0))
        def _():
            dg_ref[...] = jnp.zeros(dg_ref.shape, F32)

        @pl.when(part < 4)
        def _():
            acc_ref[...] += _dot_nt(p0_ref[...], w_ref[...])

        @pl.when((part >= 4) & (part < 7))
        def _():
            acc_ref[...] += _dot_nt(p1_ref[...], w_ref[...])

        @pl.when(part >= 7)
        def _():
            acc_ref[...] += _dot_nt(p2_ref[...], w_ref[...])

        @pl.when(k_i == nk - 1)
        def _():
            du = acc_ref[...]
            x = x_ref[...]
            r = lax.rsqrt(jnp.mean(x * x, axis=-1, keepdims=True) + EPS)
            nrm = x * r
            dg_ref[...] += jnp.sum(du * nrm, axis=0, keepdims=True)
            dn = du * g_ref[...]
            gx_ref[...] = dh_ref[...] + r * (dn - nrm * jnp.mean(dn * nrm, axis=-1, keepdims=True))

    def pspec(lo, n):
        return pl.BlockSpec((None, tm, tk), lambda m, k: (_part_index(k, per, lo, n), m, k % per))

    row = pl.BlockSpec((tm, D_MODEL), lambda m, k: (m, 0))
    vec = pl.BlockSpec((1, D_MODEL), lambda m, k: (0, 0))
    return pl.pallas_call(
        body, grid=(s_n // tm, nk),
        in_specs=[pspec(0, 4), pspec(4, 3), pspec(7, 3),
                  pl.BlockSpec((None, D_MODEL, tk), lambda m, k: (k // shard_blocks, 0, k % shard_blocks)),
                  row, vec, row],
        out_specs=[row, vec],
        out_shape=[jax.ShapeDtypeStruct((s_n, D_MODEL), F32), jax.ShapeDtypeStruct((1, D_MODEL), F32)],
        scratch_shapes=[pltpu.VMEM((tm, D_MODEL), F32)],
        name="input_grad",
    )(da4, dc3, db3, w4, xp, norm_g, dh2)


def _local_grads(x, tgt, norm_g, w4, b_merge, conv_w, woc, woa, wo, final_g):
    slopes = _alibi_slopes()
    xp, tp = _to_residue_major(x, tgt)
    u, ut = _rms_in(xp, norm_g)
    proj = _in_proj(u, w4)
    hc, hct = _conv_fwd(proj, conv_w)
    fwd = [_attn_fwd(proj, slopes, d) for d in PATTERNS]
    o, lse, ha, hat = _attn_combine([f[0] for f in fwd], [f[1] for f in fwd], proj)
    yc, ya, mg, mgt, dh2, dh2b, dgf, loss8 = _merge_loss(hc, ha, woc, woa, wo, proj, b_merge, xp, final_g, tp)

    dyc, dya, dhc, dout, dsum, db3, dbias = _merge_bwd(dh2b, wo, woc, woa, yc, ya, proj, b_merge, o)
    d_wo = _mm_lhs_resident(mgt, dh2b, 256, "dw_o")
    d_woc = _mm_lhs_resident(hct, dyc, 256, "dw_out_conv")
    d_woa = _mm_lhs_resident(hat, dya, 256, "dw_out_attn")
    da4, d_conv_w = _conv_bwd(proj, conv_w, dhc)
    dc3 = _sum3(*[_attn_bwd(proj, dout, lse, dsum, slopes, d) for d in PATTERNS])
    d_w4 = _dw_in(ut, da4, dc3, db3)
    gxp, d_norm_g = _input_grad(da4, dc3, db3, w4, xp, norm_g, dh2)
    grad_x = _to_natural(gxp)
    return loss8, grad_x, d_norm_g, d_w4, dbias, d_conv_w, d_woc, d_woa, d_wo, dgf


MESH = pl.DeviceIdType.MESH
_CHIP_FLIPS = ((1, 0), (0, 1), (1, 1))
_ANY = pl.BlockSpec(memory_space=pl.ANY)


def _place():
    return lax.axis_index("x"), lax.axis_index("y"), lax.axis_index("c")


def _flip(v, f):
    return 1 - v if f else v


def _remote(src, dst, send_sems, recv_sems, k, device):
    return pltpu.make_async_remote_copy(src_ref=src, dst_ref=dst, send_sem=send_sems.at[k], recv_sem=recv_sems.at[k],
                                        device_id=device, device_id_type=MESH)


def _gather_weights(shards):
    n = len(shards)

    def body(*refs):
        ins, outs = refs[:n], refs[n:2 * n]
        send_sems, recv_sems, local_sems = refs[2 * n:]
        x, y, c = _place()
        chip = 2 * x + y
        sibling = (x, y, 1 - c)
        locals_ = [pltpu.make_async_copy(ins[a], outs[a].at[chip], local_sems.at[a]) for a in range(n)]
        for cp in locals_:
            cp.start()
        started = []
        for a in range(n):
            h = ins[a].shape[0] // 2
            mine = pl.ds(pl.multiple_of(c * h, 8), h)
            for t, (fx, fy) in enumerate(_CHIP_FLIPS):
                cp = _remote(ins[a].at[mine], outs[a].at[chip, mine], send_sems, recv_sems, 6 * a + t,
                             (_flip(x, fx), _flip(y, fy), c))
                cp.start()
                started.append(cp)
        for a in range(n):
            h = ins[a].shape[0] // 2
            mine = pl.ds(pl.multiple_of(c * h, 8), h)
            for t, (fx, fy) in enumerate(_CHIP_FLIPS):
                landed = outs[a].at[2 * _flip(x, fx) + _flip(y, fy), mine]
                _remote(landed, landed, send_sems, recv_sems, 6 * a + t, sibling).wait_recv()
                cp = _remote(landed, landed, send_sems, recv_sems, 6 * a + 3 + t, sibling)
                cp.start()
                started.append(cp)
        for a in range(n):
            h = ins[a].shape[0] // 2
            theirs = pl.ds(pl.multiple_of((1 - c) * h, 8), h)
            for t, (fx, fy) in enumerate(_CHIP_FLIPS):
                handed = outs[a].at[2 * _flip(x, fx) + _flip(y, fy), theirs]
                _remote(handed, handed, send_sems, recv_sems, 6 * a + 3 + t, sibling).wait_recv()
        for cp in started:
            cp.wait_send()
        for cp in locals_:
            cp.wait()

    return pl.pallas_call(
        body, in_specs=[_ANY] * n, out_specs=[_ANY] * n,
        out_shape=[jax.ShapeDtypeStruct((4, *s.shape), s.dtype) for s in shards],
        scratch_shapes=[pltpu.SemaphoreType.DMA((6 * n,)), pltpu.SemaphoreType.DMA((6 * n,)),
                        pltpu.SemaphoreType.DMA((n,))],
        name="gather_weights",
    )(*shards)


def _sibling_exchange_halves(grads):
    n = len(grads)

    def body(*refs):
        ins, outs = refs[:n], refs[n:2 * n]
        send_sems, recv_sems = refs[2 * n:]
        x, y, c = _place()
        copies = []
        for a in range(n):
            h = ins[a].shape[1] // 2
            theirs = pl.ds(pl.multiple_of((1 - c) * h, 8), h)
            copies.append(_remote(ins[a].at[:, theirs], outs[a], send_sems, recv_sems, a, (x, y, 1 - c)))
        for cp in copies:
            cp.start()
        for cp in copies:
            cp.wait()

    return pl.pallas_call(
        body, in_specs=[_ANY] * n, out_specs=[_ANY] * n,
        out_shape=[jax.ShapeDtypeStruct((4, g.shape[1] // 2, g.shape[2]), g.dtype) for g in grads],
        scratch_shapes=[pltpu.SemaphoreType.DMA((n,)), pltpu.SemaphoreType.DMA((n,))],
        name="grads_to_sibling",
    )(*grads)


def _add_halves(g, r, half):
    _, rows, cols = g.shape
    h = rows // 2
    tm = min(h, 128)
    nt = h // tm

    def body(half_ref, g_ref, r_ref, f_ref, b_ref):
        s = g_ref[...] + r_ref[...]
        f_ref[...] = s
        b_ref[...] = s.astype(b_ref.dtype)

    spec = pl.BlockSpec((1, tm, cols), lambda j, i, half_ref: (j, i, 0))
    return pl.pallas_call(
        body,
        grid_spec=pltpu.PrefetchScalarGridSpec(
            num_scalar_prefetch=1, grid=(4, nt),
            in_specs=[pl.BlockSpec((1, tm, cols), lambda j, i, half_ref: (j, half_ref[0] * nt + i, 0)), spec],
            out_specs=[spec, spec]),
        out_shape=[jax.ShapeDtypeStruct((4, h, cols), F32), jax.ShapeDtypeStruct((4, h, cols), BF16)],
        name="add_sibling_grads",
    )(half, g, r)


def _chip_exchange(parts):
    n = len(parts)

    def body(*refs):
        ins, outs = refs[:n], refs[n:2 * n]
        send_sems, recv_sems = refs[2 * n:]
        x, y, c = _place()
        copies = []
        for a in range(n):
            for t, (fx, fy) in enumerate(_CHIP_FLIPS):
                tx, ty = _flip(x, fx), _flip(y, fy)
                copies.append(_remote(ins[a].at[2 * tx + ty], outs[a].at[t], send_sems, recv_sems, 3 * a + t,
                                      (tx, ty, c)))
        for cp in copies:
            cp.start()
        for cp in copies:
            cp.wait()

    return pl.pallas_call(
        body, in_specs=[_ANY] * n, out_specs=[_ANY] * n,
        out_shape=[jax.ShapeDtypeStruct((3, *p.shape[1:]), p.dtype) for p in parts],
        scratch_shapes=[pltpu.SemaphoreType.DMA((3 * n,)), pltpu.SemaphoreType.DMA((3 * n,))],
        name="grads_to_chips",
    )(*parts)


def _add_chips(own, recv, chip):
    _, h, cols = own.shape
    tm = min(h, 128)

    def body(chip_ref, o_ref, r_ref, out_ref):
        out_ref[...] = ((o_ref[0] + r_ref[0].astype(F32)) + r_ref[1].astype(F32)) + r_ref[2].astype(F32)

    return pl.pallas_call(
        body,
        grid_spec=pltpu.PrefetchScalarGridSpec(
            num_scalar_prefetch=1, grid=(h // tm,),
            in_specs=[pl.BlockSpec((1, tm, cols), lambda i, chip_ref: (chip_ref[0], i, 0)),
                      pl.BlockSpec((3, tm, cols), lambda i, chip_ref: (0, i, 0))],
            out_specs=pl.BlockSpec((tm, cols), lambda i, chip_ref: (i, 0))),
        out_shape=jax.ShapeDtypeStruct((h, cols), F32),
        name="add_chip_grads",
    )(chip, own, recv)


def _share_halves(halves):
    n = len(halves)

    def body(*refs):
        ins, outs = refs[:n], refs[n:2 * n]
        send_sems, recv_sems, local_sems = refs[2 * n:]
        x, y, c = _place()
        copies = []
        for a in range(n):
            h = ins[a].shape[0]
            mine = pl.ds(pl.multiple_of(c * h, 8), h)
            copies.append(pltpu.make_async_copy(ins[a], outs[a].at[mine], local_sems.at[a]))
            copies.append(_remote(ins[a], outs[a].at[mine], send_sems, recv_sems, a, (x, y, 1 - c)))
        for cp in copies:
            cp.start()
        for cp in copies:
            cp.wait()

    return pl.pallas_call(
        body, in_specs=[_ANY] * n, out_specs=[_ANY] * n,
        out_shape=[jax.ShapeDtypeStruct((2 * p.shape[0], p.shape[1]), p.dtype) for p in halves],
        scratch_shapes=[pltpu.SemaphoreType.DMA((n,)), pltpu.SemaphoreType.DMA((n,)), pltpu.SemaphoreType.DMA((n,))],
        name="share_reduced_halves",
    )(*halves)


def _exchange_small(rows, reduce):
    cols = rows[0].shape[1]
    n = len(rows)
    assert sum(r.shape[0] for r in rows) <= 8

    def body(*refs):
        ins, out_ref = refs[:n], refs[n]
        vec_ref, gath_ref, send_sems, recv_sems = refs[n + 1:]
        x, y, c = _place()
        me = 4 * x + 2 * y + c
        vec_ref[...] = jnp.zeros(vec_ref.shape, F32)
        at = 0
        for r in ins:
            vec_ref[at:at + r.shape[0], :] = r[...]
            at += r.shape[0]
        copies = []
        for k in range(1, 8):
            peer = (_flip(x, (k >> 2) & 1), _flip(y, (k >> 1) & 1), _flip(c, k & 1))
            copies.append(_remote(vec_ref, gath_ref.at[me], send_sems, recv_sems, k - 1, peer))
        for cp in copies:
            cp.start()
        gath_ref[me] = vec_ref[...]
        for cp in copies:
            cp.wait()
        if reduce:
            tot = gath_ref[0]
            for dev in range(1, 8):
                tot = tot + gath_ref[dev]
            out_ref[...] = tot
            out_ref[7:8, :] = jnp.zeros((1, cols), F32) + jnp.sum(tot[7:8, :])
        else:
            out_ref[...] = gath_ref[...]

    vm = pl.BlockSpec(memory_space=pltpu.VMEM)
    return pl.pallas_call(
        body, in_specs=[vm] * n, out_specs=vm,
        out_shape=jax.ShapeDtypeStruct((8, cols) if reduce else (8, 8, cols), F32),
        scratch_shapes=[pltpu.VMEM((8, cols), F32), pltpu.VMEM((8, 8, cols), F32),
                        pltpu.SemaphoreType.DMA((7,)), pltpu.SemaphoreType.DMA((7,))],
        name="reduce_small" if reduce else "gather_small",
    )(*rows)


def _adamw(w, g, m, v, name):
    rows, cols = w.shape
    tm = 128 if rows % 128 == 0 else rows

    def body(w_ref, g_ref, m_ref, v_ref, d_ref, m2_ref, v2_ref):
        gr = g_ref[...]
        m2 = ADAM_B1 * m_ref[...] + (1.0 - ADAM_B1) * gr
        v2 = ADAM_B2 * v_ref[...] + (1.0 - ADAM_B2) * (gr * gr)
        m_hat = m2 / (1.0 - ADAM_B1 ** ADAM_STEP)
        v_hat = v2 / (1.0 - ADAM_B2 ** ADAM_STEP)
        d_ref[...] = -ADAM_LR * (m_hat / (jnp.sqrt(v_hat) + ADAM_EPS) + ADAM_WD * w_ref[...])
        m2_ref[...] = m2
        v2_ref[...] = v2

    spec = pl.BlockSpec((tm, cols), lambda i: (i, 0))
    sds = jax.ShapeDtypeStruct((rows, cols), F32)
    return pl.pallas_call(body, grid=(rows // tm,), in_specs=[spec] * 4, out_specs=[spec] * 3,
                          out_shape=[sds] * 3, name=name)(w, g, m, v)


def kernel(x, norm_g, w_in, b_merge, conv_w, w_out_conv, w_out_attn, w_o, final_g, loss_target, m_norm_g, m_w_in, m_b_merge, m_conv_w, m_w_out_conv, m_w_out_attn, m_w_o, m_final_g, v_norm_g, v_w_in, v_b_merge, v_conv_w, v_w_out_conv, v_w_out_attn, v_w_o, v_final_g):
    mx, my, mc = _place()
    chip = (2 * mx + my).astype(jnp.int32)
    seq = x.shape[1]

    w4, woc4, woa4, wo4 = _gather_weights([w_in[0].astype(MXU_DTYPE), w_out_conv[0].astype(MXU_DTYPE),
                                           w_out_attn[0].astype(MXU_DTYPE), w_o[0].astype(MXU_DTYPE)])
    taps8 = _exchange_small([conv_w[0]], reduce=False)
    taps = jnp.concatenate([taps8[2 * j, :3, :] for j in range(4)], axis=1)

    loss8, grad_x, d_norm_g, d_w4, d_bias, d_taps, d_woc, d_woa, d_wo, d_final_g = _local_grads(
        x[0], loss_target[0], norm_g, w4, b_merge, taps, woc4.reshape(D_MODEL, D_MODEL),
        woa4.reshape(D_MODEL, D_MODEL), wo4.reshape(D_MODEL, D_MODEL), final_g.reshape(1, D_MODEL))

    grads = [d_w4, d_woc.reshape(4, -1, D_MODEL), d_woa.reshape(4, -1, D_MODEL), d_wo.reshape(4, -1, D_MODEL)]
    from_sibling = _sibling_exchange_halves(grads)
    half = mc.astype(jnp.int32).reshape(1)
    partial = [_add_halves(g, r, half) for g, r in zip(grads, from_sibling)]
    from_chips = _chip_exchange([p[1] for p in partial])
    reduced = [_add_chips(p[0], r, chip.reshape(1)) for p, r in zip(partial, from_chips)]
    g_w_in, g_woc, g_woa, g_wo = _share_halves(reduced)

    small = _exchange_small([d_norm_g, d_bias.reshape(2, D_MODEL), d_taps, d_final_g, loss8.reshape(1, D_MODEL)],
                            reduce=True)
    loss = (0.5 / D_MODEL) * small[7, 0]
    g_norm_g = small[0:1]
    g_bias = small[1:3].reshape(1, 2 * D_MODEL)
    g_taps = lax.dynamic_slice(small[3:6], (0, chip * (D_MODEL // 4)), (3, D_MODEL // 4))
    g_final_g = small[6:7]

    upd = [
        _adamw(norm_g, g_norm_g, m_norm_g, v_norm_g, "adamw_norm_g"),
        _adamw(w_in[0], g_w_in, m_w_in[0], v_w_in[0], "adamw_w_in"),
        _adamw(b_merge, g_bias, m_b_merge, v_b_merge, "adamw_b_merge"),
        _adamw(conv_w[0], g_taps, m_conv_w[0], v_conv_w[0], "adamw_conv_w"),
        _adamw(w_out_conv[0], g_woc, m_w_out_conv[0], v_w_out_conv[0], "adamw_w_out_conv"),
        _adamw(w_out_attn[0], g_woa, m_w_out_attn[0], v_w_out_attn[0], "adamw_w_out_attn"),
        _adamw(w_o[0], g_wo, m_w_o[0], v_w_o[0], "adamw_w_o"),
        _adamw(final_g.reshape(1, D_MODEL), g_final_g, m_final_g.reshape(1, D_MODEL),
               v_final_g.reshape(1, D_MODEL), "adamw_final_g"),
    ]
    shapes = [norm_g.shape, w_in.shape, b_merge.shape, conv_w.shape, w_out_conv.shape, w_out_attn.shape,
              w_o.shape, final_g.shape]
    grads_out = [g_norm_g, g_w_in, g_bias, g_taps, g_woc, g_woa, g_wo, g_final_g]
    outs = [loss, grad_x.reshape(1, seq, D_MODEL)]
    outs += [g.reshape(s) for g, s in zip(grads_out, shapes)]
    for k in range(3):
        outs += [u[k].reshape(s) for u, s in zip(upd, shapes)]
    return tuple(outs)
```

```python
import functools

import numpy as np
import jax
import jax.numpy as jnp
from jax import lax
from jax.experimental import pallas as pl
from jax.experimental.pallas import tpu as pltpu

F32 = jnp.float32
BF16 = jnp.bfloat16
MXU_DTYPE = jnp.bfloat16
ACT_DTYPE = jnp.bfloat16

D_MODEL = 1024
N_HEADS = 16
HEAD_DIM = 64
QB = 128
N_RES = 16
LANES = 128
HP = N_HEADS * HEAD_DIM // LANES
IN_COLS = 10 * D_MODEL
SHARD_COLS = IN_COLS // 4
EPS = 1e-6
NEG = -1e30

ADAM_LR, ADAM_B1, ADAM_B2, ADAM_EPS, ADAM_WD, ADAM_STEP = 0.001, 0.9, 0.999, 1e-08, 0.01, 10

PATTERNS = {1: (16, 16), 4: (4, 32), 16: (1, 128)}

_NN = (((1,), (0,)), ((), ()))
_NT = (((1,), (1,)), ((), ()))


def _dot(a, b):
    return lax.dot_general(a.astype(MXU_DTYPE), b.astype(MXU_DTYPE), _NN, preferred_element_type=F32)


def _dot_nt(a, b):
    return lax.dot_general(a.astype(MXU_DTYPE), b.astype(MXU_DTYPE), _NT, preferred_element_type=F32)


def _split3(x):
    hi = x.astype(BF16)
    r1 = x - hi.astype(F32)
    mid = r1.astype(BF16)
    lo = (r1 - mid.astype(F32)).astype(BF16)
    return hi, mid, lo


def _select_rows(sel, x):
    return sum(lax.dot_general(sel, t, _NN, preferred_element_type=F32) for t in _split3(x))


def _select_cols(x, sel):
    return sum(lax.dot_general(t, sel, _NN, preferred_element_type=F32) for t in _split3(x))


def _sigmoid(z):
    return 1.0 / (1.0 + jnp.exp(-z))


def _perm_matrix():
    idx = np.arange(256)
    p = np.zeros((256, 256), np.float32)
    p[(idx % 16) * 16 + idx // 16, idx] = 1.0
    return jnp.asarray(p, BF16)


def _head_expand_matrix():
    e = np.zeros((LANES, D_MODEL), np.float32)
    for h in range(N_HEADS):
        e[8 * h, HEAD_DIM * h:HEAD_DIM * (h + 1)] = 1.0
    return jnp.asarray(e, BF16)


def _head_sum_matrix():
    e = np.zeros((D_MODEL, LANES), np.float32)
    for h in range(N_HEADS):
        e[HEAD_DIM * h:HEAD_DIM * (h + 1), 8 * h:8 * (h + 1)] = 1.0
    return jnp.asarray(e, BF16)


def _attn_tables(d):
    g_n, rq = PATTERNS[d]
    q_n = g_n * rq
    gq, iq = np.arange(q_n) // rq, np.arange(q_n) % rq

    def tab(kn, base):
        k_n = g_n * kn
        gk, jk = np.arange(k_n) // kn, np.arange(k_n) % kn
        delta = g_n * (base + iq[:, None] - jk[None, :]) + gq[:, None] - gk[None, :]
        valid = (delta >= 0) & (delta <= QB)
        dist = np.where(valid, d * delta, 0).astype(np.float32)
        madd = np.where(valid, 0.0, NEG).astype(np.float32)
        return dist, madd

    d0, m0 = tab(rq, 0)
    d1, m1 = tab(2 * rq, rq)
    return d0, m0, d1, m1


def _alibi_slopes():
    return jnp.exp2(-8.0 * jnp.arange(1, N_HEADS + 1, dtype=F32) / N_HEADS)


def _to_residue_major(x, tgt):
    s_n, c_n = x.shape
    lr = s_n // N_RES

    def body(p_ref, x_ref, t_ref, xo_ref, to_ref):
        pm = p_ref[...]
        xo_ref[...] = _select_rows(pm, x_ref[...]).reshape(16, 16, c_n)
        to_ref[...] = _select_rows(pm, t_ref[...]).reshape(16, 16, c_n)

    nat = pl.BlockSpec((256, c_n), lambda i: (i, 0))
    res = pl.BlockSpec((16, 16, c_n), lambda i: (0, i, 0))
    xo, to = pl.pallas_call(
        body, grid=(s_n // 256,),
        in_specs=[pl.BlockSpec((256, 256), lambda i: (0, 0)), nat, nat],
        out_specs=[res, res],
        out_shape=[jax.ShapeDtypeStruct((16, lr, c_n), F32)] * 2,
        name="perm_in",
    )(_perm_matrix(), x, tgt)
    return xo.reshape(s_n, c_n), to.reshape(s_n, c_n)


def _to_natural(gxp):
    s_n, c_n = gxp.shape
    lr = s_n // N_RES

    def body(p_ref, g_ref, o_ref):
        o_ref[...] = _select_rows(p_ref[...], g_ref[...].reshape(256, c_n))

    return pl.pallas_call(
        body, grid=(s_n // 256,),
        in_specs=[pl.BlockSpec((256, 256), lambda i: (0, 0)),
                  pl.BlockSpec((16, 16, c_n), lambda i: (0, i, 0))],
        out_specs=pl.BlockSpec((256, c_n), lambda i: (i, 0)),
        out_shape=jax.ShapeDtypeStruct((s_n, c_n), F32),
        name="perm_out",
    )(_perm_matrix(), gxp.reshape(16, lr, c_n))


def _rms_in(xp, norm_g):
    s_n, c_n = xp.shape
    tm = 512

    def body(x_ref, g_ref, u_ref, ut_ref):
        x = x_ref[...]
        r = lax.rsqrt(jnp.mean(x * x, axis=-1, keepdims=True) + EPS)
        u = x * r * g_ref[...]
        u_ref[...] = u.astype(u_ref.dtype)
        ut_ref[...] = u.T.astype(ut_ref.dtype)

    return pl.pallas_call(
        body, grid=(s_n // tm,),
        in_specs=[pl.BlockSpec((tm, c_n), lambda i: (i, 0)), pl.BlockSpec((1, c_n), lambda i: (0, 0))],
        out_specs=[pl.BlockSpec((tm, c_n), lambda i: (i, 0)), pl.BlockSpec((c_n, tm), lambda i: (0, i))],
        out_shape=[jax.ShapeDtypeStruct((s_n, c_n), ACT_DTYPE), jax.ShapeDtypeStruct((c_n, s_n), ACT_DTYPE)],
        name="rms_in",
    )(xp, norm_g)


def _in_proj(u, w4):
    s_n = u.shape[0]
    tn, cm = 512, 512
    per = SHARD_COLS // tn

    def body(a_ref, b_ref, o_ref):
        b = b_ref[...]
        for c in range(s_n // cm):
            o_ref[c * cm:(c + 1) * cm, :] = _dot(a_ref[c * cm:(c + 1) * cm, :], b).astype(o_ref.dtype)

    return pl.pallas_call(
        body, grid=(IN_COLS // tn,),
        in_specs=[pl.BlockSpec((s_n, D_MODEL), lambda n: (0, 0)),
                  pl.BlockSpec((None, D_MODEL, tn), lambda n: (n // per, 0, n % per))],
        out_specs=pl.BlockSpec((s_n, tn), lambda n: (0, n)),
        out_shape=jax.ShapeDtypeStruct((s_n, IN_COLS), ACT_DTYPE),
        name="in_proj",
    )(u, w4)


def _conv_terms(xc_ref, cg_ref, r, row, lr):
    def a_of(q):
        return cg_ref[q].astype(F32) * xc_ref[q].astype(F32)

    def shift_down(v):
        return jnp.where(row >= 1, pltpu.roll(v, 1, 0), 0.0)

    a = a_of(r)
    am1 = a_of(r - 1) if r >= 1 else shift_down(a_of(N_RES - 1))
    am2 = a_of(r - 2) if r >= 2 else shift_down(a_of(N_RES - 2 + r))
    return a, am1, am2


def _conv_fwd(proj, conv_w):
    s_n = proj.shape[0]
    lr = s_n // N_RES
    pv = proj.reshape(N_RES, lr, IN_COLS)

    def body(xc_ref, bg_ref, cg_ref, zc_ref, w_ref, hc_ref, hct_ref):
        w = w_ref[...]
        row = lax.broadcasted_iota(jnp.int32, (lr, LANES), 0)
        for r in range(N_RES):
            a, am1, am2 = _conv_terms(xc_ref, cg_ref, r, row, lr)
            c = w[0:1] * am2 + w[1:2] * am1 + w[2:3] * a
            z = zc_ref[r].astype(F32)
            hc = z * _sigmoid(z) * bg_ref[r].astype(F32) * c
            hc_ref[r] = hc.astype(hc_ref.dtype)
            hct_ref[:, r * lr:(r + 1) * lr] = hc.T.astype(hct_ref.dtype)

    def col(part):
        return pl.BlockSpec((N_RES, lr, LANES), lambda j: (0, 0, part * 8 + j))

    hc, hct = pl.pallas_call(
        body, grid=(D_MODEL // LANES,),
        in_specs=[col(0), col(1), col(2), col(3), pl.BlockSpec((3, LANES), lambda j: (0, j))],
        out_specs=[pl.BlockSpec((N_RES, lr, LANES), lambda j: (0, 0, j)),
                   pl.BlockSpec((LANES, s_n), lambda j: (j, 0))],
        out_shape=[jax.ShapeDtypeStruct((N_RES, lr, D_MODEL), ACT_DTYPE),
                   jax.ShapeDtypeStruct((D_MODEL, s_n), ACT_DTYPE)],
        name="conv_fwd",
    )(pv, pv, pv, pv, conv_w)
    return hc.reshape(s_n, D_MODEL), hct


def _pattern_view_shape(s_n, c_n, g_n, lead=()):
    lr = s_n // N_RES
    return (*lead, 4, 4, lr, c_n) if g_n == 4 else (*lead, N_RES, lr, c_n)


def _pattern_view(a, g_n, lead=()):
    return a.reshape(_pattern_view_shape(a.shape[-2], a.shape[-1], g_n, lead))


def _pattern_spec(g_n, lr, col_of_hp, lead=()):
    nl = len(lead)
    z = (0,) * nl
    if g_n == 16:
        return pl.BlockSpec((*lead, 16, lr, LANES), lambda r, hp: (*z, 0, 0, col_of_hp(hp)))
    if g_n == 4:
        return pl.BlockSpec((*lead, 4, None, lr, LANES), lambda r, hp: (*z, 0, r, 0, col_of_hp(hp)))
    return pl.BlockSpec((*lead, 1, lr, LANES), lambda r, hp: (*z, r, 0, col_of_hp(hp)))


def _attn_fwd(proj, slopes, d):
    g_n, rq = PATTERNS[d]
    s_n = proj.shape[0]
    lr = s_n // N_RES
    nb = lr // rq
    q_n = g_n * rq
    d0, m0, d1, m1 = _attn_tables(d)

    def body(sl_ref, q_ref, k_ref, v_ref, d0_ref, m0_ref, d1_ref, m1_ref, o_ref, lse_ref, b0_ref, b1_ref):
        hp = pl.program_id(1)

        @pl.when(hp == 0)
        def _():
            lse_ref[...] = jnp.zeros(lse_ref.shape, F32)

        for h in (0, 1):
            slope = sl_ref[2 * hp + h]
            b0_ref[h] = m0_ref[...] - slope * d0_ref[...]
            b1_ref[h] = m1_ref[...] - slope * d1_ref[...]

        lane = lax.broadcasted_iota(jnp.int32, (q_n, LANES), 1)
        low = lane < HEAD_DIM
        grp = lane // 8

        def block(n0, kb, vb, b_ref):
            qb = q_ref[:, pl.ds(n0, rq), :].reshape(q_n, LANES)
            outs = []
            upd = jnp.zeros((q_n, LANES), F32)
            for h in (0, 1):
                msk = low if h == 0 else jnp.logical_not(low)
                qm = jnp.where(msk, qb, jnp.zeros_like(qb))
                s = _dot_nt(qm, kb) * 0.125 + b_ref[h]
                m = jnp.max(s, axis=1, keepdims=True)
                p = jnp.exp(s - m)
                l = jnp.sum(p, axis=1, keepdims=True)
                outs.append(_dot(p, vb) * (1.0 / l))
                upd = upd + jnp.where(grp == 2 * hp + h, m + jnp.log(l), 0.0)
            o = jnp.where(low, outs[0], outs[1])
            o_ref[:, pl.ds(n0, rq), :] = o.reshape(g_n, rq, LANES)
            lse_ref[:, pl.ds(n0, rq), :] += upd.reshape(g_n, rq, LANES)

        block(0, k_ref[:, 0:rq, :].reshape(q_n, LANES), v_ref[:, 0:rq, :].reshape(q_n, LANES), b0_ref)

        def loop(n, carry):
            st = pl.multiple_of((n - 1) * rq, rq)
            n0 = pl.multiple_of(n * rq, rq)
            block(n0, k_ref[:, pl.ds(st, 2 * rq), :].reshape(2 * q_n, LANES),
                  v_ref[:, pl.ds(st, 2 * rq), :].reshape(2 * q_n, LANES), b1_ref)
            return carry

        lax.fori_loop(1, nb, loop, 0)

    pv = _pattern_view(proj, g_n)
    full = lambda a: pl.BlockSpec(a.shape, lambda r, hp: (0, 0))
    o, lse = pl.pallas_call(
        body, grid=(N_RES // g_n, HP),
        in_specs=[pl.BlockSpec(memory_space=pltpu.SMEM),
                  _pattern_spec(g_n, lr, lambda hp: 32 + hp),
                  _pattern_spec(g_n, lr, lambda hp: 40 + hp),
                  _pattern_spec(g_n, lr, lambda hp: 48 + hp),
                  full(d0), full(m0), full(d1), full(m1)],
        out_specs=[_pattern_spec(g_n, lr, lambda hp: hp), _pattern_spec(g_n, lr, lambda hp: 0)],
        out_shape=[jax.ShapeDtypeStruct(_pattern_view_shape(s_n, D_MODEL, g_n), F32),
                   jax.ShapeDtypeStruct(_pattern_view_shape(s_n, LANES, g_n), F32)],
        scratch_shapes=[pltpu.VMEM((2, q_n, q_n), F32), pltpu.VMEM((2, q_n, 2 * q_n), F32)],
        name=f"attn_fwd_d{d}",
    )(slopes, pv, pv, pv, d0, m0, d1, m1)
    return o.reshape(s_n, D_MODEL), lse.reshape(s_n, LANES)


def _attn_combine(outs, lses, proj):
    s_n = proj.shape[0]
    tm = 512

    def body(o1_ref, o2_ref, o3_ref, l1_ref, l2_ref, l3_ref, za_ref, e_ref, o_ref, lse_ref, ha_ref, hat_ref):
        ls = [l1_ref[...], l2_ref[...], l3_ref[...]]
        mx = jnp.maximum(jnp.maximum(ls[0], ls[1]), ls[2])
        den = sum(jnp.exp(l - mx) for l in ls)
        lse = mx + jnp.log(den)
        lse_ref[...] = lse
        o = jnp.zeros((tm, D_MODEL), F32)
        for l, oref in zip(ls, (o1_ref, o2_ref, o3_ref)):
            o = o + _select_cols(jnp.exp(l - lse), e_ref[...]) * oref[...]
        o_ref[...] = o
        z = za_ref[...].astype(F32)
        ha = z * _sigmoid(z) * o
        ha_ref[...] = ha.astype(ha_ref.dtype)
        hat_ref[...] = ha.T.astype(hat_ref.dtype)

    row = lambda w: pl.BlockSpec((tm, w), lambda i: (i, 0))
    return pl.pallas_call(
        body, grid=(s_n // tm,),
        in_specs=[row(D_MODEL)] * 3 + [row(LANES)] * 3
        + [pl.BlockSpec((tm, D_MODEL), lambda i: (i, 7)), pl.BlockSpec((LANES, D_MODEL), lambda i: (0, 0))],
        out_specs=[row(D_MODEL), row(LANES), row(D_MODEL), pl.BlockSpec((D_MODEL, tm), lambda i: (0, i))],
        out_shape=[jax.ShapeDtypeStruct((s_n, D_MODEL), F32), jax.ShapeDtypeStruct((s_n, LANES), F32),
                   jax.ShapeDtypeStruct((s_n, D_MODEL), ACT_DTYPE), jax.ShapeDtypeStruct((D_MODEL, s_n), ACT_DTYPE)],
        name="attn_combine",
    )(*outs, *lses, proj, _head_expand_matrix())


def _gates(gc_ref, ga_ref, b_ref):
    b = b_ref[...]
    gc = _sigmoid(gc_ref[...].astype(F32) + b[:, :D_MODEL])
    ga = _sigmoid(ga_ref[...].astype(F32) + b[:, D_MODEL:])
    return gc, ga


def _merge_loss(hc, ha, woc, woa, wo, proj, b_merge, xp, final_g, tgt):
    s_n = xp.shape[0]
    tm = 512

    def body(hc_ref, ha_ref, woc_ref, woa_ref, wo_ref, gc_ref, ga_ref, b_ref, x_ref, gf_ref, t_ref,
             yc_ref, ya_ref, mg_ref, mgt_ref, dh_ref, dhb_ref, dgf_ref, loss_ref):
        i = pl.program_id(0)

        @pl.when(i == 0)
        def _():
            dgf_ref[...] = jnp.zeros(dgf_ref.shape, F32)
            loss_ref[...] = jnp.zeros(loss_ref.shape, F32)

        yc = _dot(hc_ref[...], woc_ref[...])
        ya = _dot(ha_ref[...], woa_ref[...])
        gc, ga = _gates(gc_ref, ga_ref, b_ref)
        mg = gc * yc + ga * ya
        yc_ref[...] = yc.astype(yc_ref.dtype)
        ya_ref[...] = ya.astype(ya_ref.dtype)
        mg_ref[...] = mg.astype(mg_ref.dtype)
        mgt_ref[...] = mg.T.astype(mgt_ref.dtype)
        h2 = x_ref[...] + _dot(mg, wo_ref[...])
        r2 = lax.rsqrt(jnp.mean(h2 * h2, axis=-1, keepdims=True) + EPS)
        nrm = h2 * r2
        gf = gf_ref[...]
        err = nrm * gf - t_ref[...]
        e2 = (err * err).reshape(tm // 8, 8, D_MODEL).sum(axis=0)
        loss_ref[...] += sum(e2[:, c * LANES:(c + 1) * LANES] for c in range(D_MODEL // LANES))
        dy = err * (1.0 / D_MODEL)
        dgf_ref[...] += jnp.sum(dy * nrm, axis=0, keepdims=True)
        dn = dy * gf
        dh2 = r2 * (dn - nrm * jnp.mean(dn * nrm, axis=-1, keepdims=True))
        dh_ref[...] = dh2
        dhb_ref[...] = dh2.astype(dhb_ref.dtype)

    row = pl.BlockSpec((tm, D_MODEL), lambda i: (i, 0))
    wsp = pl.BlockSpec((D_MODEL, D_MODEL), lambda i: (0, 0))
    vec = lambda w: pl.BlockSpec((1, w), lambda i: (0, 0))
    act = jax.ShapeDtypeStruct((s_n, D_MODEL), ACT_DTYPE)
    return pl.pallas_call(
        body, grid=(s_n // tm,),
        in_specs=[row, row, wsp, wsp, wsp,
                  pl.BlockSpec((tm, D_MODEL), lambda i: (i, 8)), pl.BlockSpec((tm, D_MODEL), lambda i: (i, 9)),
                  vec(2 * D_MODEL), row, vec(D_MODEL), row],
        out_specs=[row, row, row, pl.BlockSpec((D_MODEL, tm), lambda i: (0, i)), row, row,
                   vec(D_MODEL), pl.BlockSpec((8, LANES), lambda i: (0, 0))],
        out_shape=[act, act, act, jax.ShapeDtypeStruct((D_MODEL, s_n), ACT_DTYPE),
                   jax.ShapeDtypeStruct((s_n, D_MODEL), F32), act,
                   jax.ShapeDtypeStruct((1, D_MODEL), F32), jax.ShapeDtypeStruct((8, LANES), F32)],
        name="merge_loss",
    )(hc, ha, woc, woa, wo, proj, proj, b_merge, xp, final_g, tgt)


def _merge_bwd(dh2b, wo, woc, woa, yc, ya, proj, b_merge, o):
    s_n = dh2b.shape[0]
    tm = 512

    def body(dh_ref, wo_ref, woc_ref, woa_ref, yc_ref, ya_ref, gc_ref, ga_ref, b_ref, o_ref, za_ref, e_ref,
             dyc_ref, dya_ref, dhc_ref, do_ref, dsum_ref, db3_ref, dbias_ref):
        i = pl.program_id(0)

        @pl.when(i == 0)
        def _():
            dbias_ref[...] = jnp.zeros(dbias_ref.shape, F32)

        dmg = _dot_nt(dh_ref[...], wo_ref[...])
        gc, ga = _gates(gc_ref, ga_ref, b_ref)
        dgc = dmg * yc_ref[...].astype(F32) * gc * (1.0 - gc)
        dga = dmg * ya_ref[...].astype(F32) * ga * (1.0 - ga)
        dbias_ref[:, :D_MODEL] += jnp.sum(dgc, axis=0, keepdims=True)
        dbias_ref[:, D_MODEL:] += jnp.sum(dga, axis=0, keepdims=True)
        dyc = dmg * gc
        dya = dmg * ga
        dyc_ref[...] = dyc.astype(dyc_ref.dtype)
        dya_ref[...] = dya.astype(dya_ref.dtype)
        dhc_ref[...] = _dot_nt(dyc, woc_ref[...]).astype(dhc_ref.dtype)
        dha = _dot_nt(dya, woa_ref[...])
        z = za_ref[...].astype(F32)
        sg = _sigmoid(z)
        ov = o_ref[...]
        dout = dha * z * sg
        do_ref[...] = dout.astype(do_ref.dtype)
        dsum_ref[...] = _select_cols(dout * ov, e_ref[...])
        db3_ref[0] = (dha * ov * sg * (1.0 + z * (1.0 - sg))).astype(db3_ref.dtype)
        db3_ref[1] = dgc.astype(db3_ref.dtype)
        db3_ref[2] = dga.astype(db3_ref.dtype)

    row = pl.BlockSpec((tm, D_MODEL), lambda i: (i, 0))
    wsp = pl.BlockSpec((D_MODEL, D_MODEL), lambda i: (0, 0))
    act = jax.ShapeDtypeStruct((s_n, D_MODEL), ACT_DTYPE)
    return pl.pallas_call(
        body, grid=(s_n // tm,),
        in_specs=[row, wsp, wsp, wsp, row, row,
                  pl.BlockSpec((tm, D_MODEL), lambda i: (i, 8)), pl.BlockSpec((tm, D_MODEL), lambda i: (i, 9)),
                  pl.BlockSpec((1, 2 * D_MODEL), lambda i: (0, 0)), row,
                  pl.BlockSpec((tm, D_MODEL), lambda i: (i, 7)), pl.BlockSpec((D_MODEL, LANES), lambda i: (0, 0))],
        out_specs=[row, row, row, row, pl.BlockSpec((tm, LANES), lambda i: (i, 0)),
                   pl.BlockSpec((3, tm, D_MODEL), lambda i: (0, i, 0)),
                   pl.BlockSpec((1, 2 * D_MODEL), lambda i: (0, 0))],
        out_shape=[act, act, act, act, jax.ShapeDtypeStruct((s_n, LANES), F32),
                   jax.ShapeDtypeStruct((3, s_n, D_MODEL), ACT_DTYPE),
                   jax.ShapeDtypeStruct((1, 2 * D_MODEL), F32)],
        name="merge_bwd",
    )(dh2b, wo, woc, woa, yc, ya, proj, proj, b_merge, o, proj, _head_sum_matrix())


def _mm_lhs_resident(a, b, tn, name):
    m_n, k_n = a.shape
    n_n = b.shape[1]

    def body(a_ref, b_ref, o_ref):
        o_ref[...] = _dot(a_ref[...], b_ref[...])

    return pl.pallas_call(
        body, grid=(n_n // tn,),
        in_specs=[pl.BlockSpec((m_n, k_n), lambda n: (0, 0)), pl.BlockSpec((k_n, tn), lambda n: (0, n))],
        out_specs=pl.BlockSpec((m_n, tn), lambda n: (0, n)),
        out_shape=jax.ShapeDtypeStruct((m_n, n_n), F32),
        name=name,
    )(a, b)


def _conv_bwd(proj, conv_w, dhc):
    s_n = proj.shape[0]
    lr = s_n // N_RES
    pv = proj.reshape(N_RES, lr, IN_COLS)

    def body(xc_ref, bg_ref, cg_ref, zc_ref, w_ref, dhc_ref, da4_ref, dw_ref, dc_ref):
        w = w_ref[...]
        row = lax.broadcasted_iota(jnp.int32, (lr, LANES), 0)
        dw = [jnp.zeros((1, LANES), F32) for _ in range(3)]
        for r in range(N_RES):
            a, am1, am2 = _conv_terms(xc_ref, cg_ref, r, row, lr)
            c = w[0:1] * am2 + w[1:2] * am1 + w[2:3] * a
            z = zc_ref[r].astype(F32)
            sg = _sigmoid(z)
            sz = z * sg
            bg = bg_ref[r].astype(F32)
            dh = dhc_ref[r].astype(F32)
            da4_ref[1, r] = (dh * sz * c).astype(da4_ref.dtype)
            da4_ref[3, r] = (dh * bg * c * sg * (1.0 + z * (1.0 - sg))).astype(da4_ref.dtype)
            dc = dh * sz * bg
            dc_ref[r] = dc
            dw[0] = dw[0] + jnp.sum(dc * am2, axis=0, keepdims=True)
            dw[1] = dw[1] + jnp.sum(dc * am1, axis=0, keepdims=True)
            dw[2] = dw[2] + jnp.sum(dc * a, axis=0, keepdims=True)
        dw_ref[0:1, :] = dw[0]
        dw_ref[1:2, :] = dw[1]
        dw_ref[2:3, :] = dw[2]

        def shift_up(v):
            return jnp.where(row < lr - 1, pltpu.roll(v, lr - 1, 0), 0.0)

        for r in range(N_RES):
            dp1 = dc_ref[r + 1] if r + 1 < N_RES else shift_up(dc_ref[0])
            dp2 = dc_ref[r + 2] if r + 2 < N_RES else shift_up(dc_ref[r + 2 - N_RES])
            da = w[2:3] * dc_ref[r] + w[1:2] * dp1 + w[0:1] * dp2
            da4_ref[0, r] = (da * cg_ref[r].astype(F32)).astype(da4_ref.dtype)
            da4_ref[2, r] = (da * xc_ref[r].astype(F32)).astype(da4_ref.dtype)

    def col(part):
        return pl.BlockSpec((N_RES, lr, LANES), lambda j: (0, 0, part * 8 + j))

    da4, dw = pl.pallas_call(
        body, grid=(D_MODEL // LANES,),
        in_specs=[col(0), col(1), col(2), col(3), pl.BlockSpec((3, LANES), lambda j: (0, j)),
                  pl.BlockSpec((N_RES, lr, LANES), lambda j: (0, 0, j))],
        out_specs=[pl.BlockSpec((4, N_RES, lr, LANES), lambda j: (0, 0, 0, j)),
                   pl.BlockSpec((3, LANES), lambda j: (0, j))],
        out_shape=[jax.ShapeDtypeStruct((4, N_RES, lr, D_MODEL), ACT_DTYPE),
                   jax.ShapeDtypeStruct((3, D_MODEL), F32)],
        scratch_shapes=[pltpu.VMEM((N_RES, lr, LANES), F32)],
        name="conv_bwd",
    )(pv, pv, pv, pv, conv_w, dhc.reshape(N_RES, lr, D_MODEL))
    return da4.reshape(4, s_n, D_MODEL), dw


def _attn_bwd(proj, dout, lse, dsum, slopes, d):
    g_n, rq = PATTERNS[d]
    s_n = proj.shape[0]
    lr = s_n // N_RES
    nb = lr // rq
    q_n = g_n * rq
    d0, m0, d1, m1 = (np.ascontiguousarray(t.T) for t in _attn_tables(d))

    def body(sl_ref, q_ref, k_ref, v_ref, do_ref, lse_ref, ds_ref, d0_ref, m0_ref, d1_ref, m1_ref, out_ref,
             b0_ref, b1_ref, lt_ref, dt_ref, dk_ref, dv_ref):
        hp = pl.program_id(1)
        for h in (0, 1):
            slope = sl_ref[2 * hp + h]
            b0_ref[h] = m0_ref[...] - slope * d0_ref[...]
            b1_ref[h] = m1_ref[...] - slope * d1_ref[...]
        dk_ref[...] = jnp.zeros(dk_ref.shape, F32)
        dv_ref[...] = jnp.zeros(dv_ref.shape, F32)
        lane = lax.broadcasted_iota(jnp.int32, (q_n, LANES), 1)
        low = lane < HEAD_DIM
        row16 = pl.multiple_of(16 * hp, 16)

        def block(n0, k0, kn, b_ref):
            k_n = g_n * kn
            qb = q_ref[:, pl.ds(n0, rq), :].reshape(q_n, LANES)
            dob = do_ref[:, pl.ds(n0, rq), :].reshape(q_n, LANES)
            kb = k_ref[:, pl.ds(k0, kn), :].reshape(k_n, LANES)
            vb = v_ref[:, pl.ds(k0, kn), :].reshape(k_n, LANES)
            lt_ref[...] = lse_ref[:, pl.ds(n0, rq), :].reshape(q_n, LANES).T
            dt_ref[...] = ds_ref[:, pl.ds(n0, rq), :].reshape(q_n, LANES).T
            l16 = lt_ref[pl.ds(row16, 16), :]
            s16 = dt_ref[pl.ds(row16, 16), :]
            dq = jnp.zeros((q_n, LANES), F32)
            for h in (0, 1):
                msk = low if h == 0 else jnp.logical_not(low)
                qm = jnp.where(msk, qb, jnp.zeros_like(qb))
                dom = jnp.where(msk, dob, jnp.zeros_like(dob))
                st = _dot_nt(kb, qm) * 0.125 + b_ref[h] - l16[8 * h:8 * h + 1, :]
                pt = jnp.exp(st)
                dpt = _dot_nt(vb, dom)
                dst = pt * (dpt - s16[8 * h:8 * h + 1, :]) * 0.125
                dv_ref[:, pl.ds(k0, kn), :] += _dot(pt, dom).reshape(g_n, kn, LANES)
                dk_ref[:, pl.ds(k0, kn), :] += _dot(dst, qm).reshape(g_n, kn, LANES)
                dq = dq + jnp.where(msk, _dot(dst.T, kb), 0.0)
            out_ref[0, :, pl.ds(n0, rq), :] = dq.reshape(g_n, rq, LANES)

        block(0, 0, rq, b0_ref)

        def loop(n, carry):
            block(pl.multiple_of(n * rq, rq), pl.multiple_of((n - 1) * rq, rq), 2 * rq, b1_ref)
            return carry

        lax.fori_loop(1, nb, loop, 0)
        out_ref[1] = dk_ref[...]
        out_ref[2] = dv_ref[...]

    pv = _pattern_view(proj, g_n)
    full = lambda a: pl.BlockSpec(a.shape, lambda r, hp: (0, 0))
    out_view = _pattern_view_shape(s_n, D_MODEL, g_n, lead=(3,))
    out = pl.pallas_call(
        body, grid=(N_RES // g_n, HP),
        in_specs=[pl.BlockSpec(memory_space=pltpu.SMEM),
                  _pattern_spec(g_n, lr, lambda hp: 32 + hp),
                  _pattern_spec(g_n, lr, lambda hp: 40 + hp),
                  _pattern_spec(g_n, lr, lambda hp: 48 + hp),
                  _pattern_spec(g_n, lr, lambda hp: hp),
                  _pattern_spec(g_n, lr, lambda hp: 0),
                  _pattern_spec(g_n, lr, lambda hp: 0),
                  full(d0), full(m0), full(d1), full(m1)],
        out_specs=_pattern_spec(g_n, lr, lambda hp: hp, lead=(3,)),
        out_shape=jax.ShapeDtypeStruct(out_view, F32),
        scratch_shapes=[pltpu.VMEM((2, q_n, q_n), F32), pltpu.VMEM((2, 2 * q_n, q_n), F32),
                        pltpu.VMEM((LANES, q_n), F32), pltpu.VMEM((LANES, q_n), F32),
                        pltpu.VMEM((g_n, lr, LANES), F32), pltpu.VMEM((g_n, lr, LANES), F32)],
        name=f"attn_bwd_d{d}",
    )(slopes, pv, pv, pv, _pattern_view(dout, g_n), _pattern_view(lse, g_n), _pattern_view(dsum, g_n),
      d0, m0, d1, m1)
    return out.reshape(3, s_n, D_MODEL)


def _sum3(a, b, c):
    _, s_n, c_n = a.shape
    tm = 512

    def body(a_ref, b_ref, c_ref, o_ref):
        o_ref[...] = (a_ref[...] + b_ref[...] + c_ref[...]).astype(o_ref.dtype)

    spec = pl.BlockSpec((1, tm, c_n), lambda p, i: (p, i, 0))
    return pl.pallas_call(
        body, grid=(3, s_n // tm), in_specs=[spec] * 3, out_specs=spec,
        out_shape=jax.ShapeDtypeStruct(a.shape, ACT_DTYPE), name="sum_dqkv",
    )(a, b, c)


def _part_index(step, per, lo, n):
    return jnp.clip(step // per - lo, 0, n - 1)


def _dw_in(ut, da4, dc3, db3):
    s_n = ut.shape[1]
    tn = 256
    per = D_MODEL // tn
    shard_blocks = SHARD_COLS // tn

    def body(a_ref, p0_ref, p1_ref, p2_ref, o_ref):
        part = pl.program_id(0) // per

        @pl.when(part < 4)
        def _():
            o_ref[...] = _dot(a_ref[...], p0_ref[...])

        @pl.when((part >= 4) & (part < 7))
        def _():
            o_ref[...] = _dot(a_ref[...], p1_ref[...])

        @pl.when(part >= 7)
        def _():
            o_ref[...] = _dot(a_ref[...], p2_ref[...])

    def pspec(lo, n):
        return pl.BlockSpec((None, s_n, tn), lambda j: (_part_index(j, per, lo, n), 0, j % per))

    return pl.pallas_call(
        body, grid=(IN_COLS // tn,),
        in_specs=[pl.BlockSpec((D_MODEL, s_n), lambda j: (0, 0)), pspec(0, 4), pspec(4, 3), pspec(7, 3)],
        out_specs=pl.BlockSpec((None, D_MODEL, tn), lambda j: (j // shard_blocks, 0, j % shard_blocks)),
        out_shape=jax.ShapeDtypeStruct((4, D_MODEL, SHARD_COLS), F32),
        name="dw_in",
    )(ut, da4, dc3, db3)


def _input_grad(da4, dc3, db3, w4, xp, norm_g, dh2):
    s_n = xp.shape[0]
    tm, tk = 1024, 512
    per = D_MODEL // tk
    nk = IN_COLS // tk
    shard_blocks = SHARD_COLS // tk

    def body(p0_ref, p1_ref, p2_ref, w_ref, x_ref, g_ref, dh_ref, gx_ref, dg_ref, acc_ref):
        m_i, k_i = pl.program_id(0), pl.program_id(1)
        part = k_i // per

        @pl.when(k_i == 0)
        def _():
            acc_ref[...] = jnp.zeros(acc_ref.shape, F32)

        @pl.when((m_i == 0) & (k_i == 0))
        def _():
            dg_ref[...] = jnp.zeros(dg_ref.shape, F32)

        @pl.when(part < 4)
        def _():
            acc_ref[...] += _dot_nt(p0_ref[...], w_ref[...])

        @pl.when((part >= 4) & (part < 7))
        def _():
            acc_ref[...] += _dot_nt(p1_ref[...], w_ref[...])

        @pl.when(part >= 7)
        def _():
            acc_ref[...] += _dot_nt(p2_ref[...], w_ref[...])

        @pl.when(k_i == nk - 1)
        def _():
            du = acc_ref[...]
            x = x_ref[...]
            r = lax.rsqrt(jnp.mean(x * x, axis=-1, keepdims=True) + EPS)
            nrm = x * r
            dg_ref[...] += jnp.sum(du * nrm, axis=0, keepdims=True)
            dn = du * g_ref[...]
            gx_ref[...] = dh_ref[...] + r * (dn - nrm * jnp.mean(dn * nrm, axis=-1, keepdims=True))

    def pspec(lo, n):
        return pl.BlockSpec((None, tm, tk), lambda m, k: (_part_index(k, per, lo, n), m, k % per))

    row = pl.BlockSpec((tm, D_MODEL), lambda m, k: (m, 0))
    vec = pl.BlockSpec((1, D_MODEL), lambda m, k: (0, 0))
    return pl.pallas_call(
        body, grid=(s_n // tm, nk),
        in_specs=[pspec(0, 4), pspec(4, 3), pspec(7, 3),
                  pl.BlockSpec((None, D_MODEL, tk), lambda m, k: (k // shard_blocks, 0, k % shard_blocks)),
                  row, vec, row],
        out_specs=[row, vec],
        out_shape=[jax.ShapeDtypeStruct((s_n, D_MODEL), F32), jax.ShapeDtypeStruct((1, D_MODEL), F32)],
        scratch_shapes=[pltpu.VMEM((tm, D_MODEL), F32)],
        name="input_grad",
    )(da4, dc3, db3, w4, xp, norm_g, dh2)


def _local_grads(x, tgt, norm_g, w4, b_merge, conv_w, woc, woa, wo, final_g):
    slopes = _alibi_slopes()
    xp, tp = _to_residue_major(x, tgt)
    u, ut = _rms_in(xp, norm_g)
    proj = _in_proj(u, w4)
    hc, hct = _conv_fwd(proj, conv_w)
    fwd = [_attn_fwd(proj, slopes, d) for d in PATTERNS]
    o, lse, ha, hat = _attn_combine([f[0] for f in fwd], [f[1] for f in fwd], proj)
    yc, ya, mg, mgt, dh2, dh2b, dgf, loss8 = _merge_loss(hc, ha, woc, woa, wo, proj, b_merge, xp, final_g, tp)

    dyc, dya, dhc, dout, dsum, db3, dbias = _merge_bwd(dh2b, wo, woc, woa, yc, ya, proj, b_merge, o)
    d_wo = _mm_lhs_resident(mgt, dh2b, 256, "dw_o")
    d_woc = _mm_lhs_resident(hct, dyc, 256, "dw_out_conv")
    d_woa = _mm_lhs_resident(hat, dya, 256, "dw_out_attn")
    da4, d_conv_w = _conv_bwd(proj, conv_w, dhc)
    dc3 = _sum3(*[_attn_bwd(proj, dout, lse, dsum, slopes, d) for d in PATTERNS])
    d_w4 = _dw_in(ut, da4, dc3, db3)
    gxp, d_norm_g = _input_grad(da4, dc3, db3, w4, xp, norm_g, dh2)
    grad_x = _to_natural(gxp)
    return loss8, grad_x, d_norm_g, d_w4, dbias, d_conv_w, d_woc, d_woa, d_wo, dgf


MESH = pl.DeviceIdType.MESH
_CHIP_FLIPS = ((1, 0), (0, 1), (1, 1))
_ANY = pl.BlockSpec(memory_space=pl.ANY)


def _place():
    return lax.axis_index("x"), lax.axis_index("y"), lax.axis_index("c")


def _flip(v, f):
    return 1 - v if f else v


def _remote(src, dst, send_sems, recv_sems, k, device):
    return pltpu.make_async_remote_copy(src_ref=src, dst_ref=dst, send_sem=send_sems.at[k], recv_sem=recv_sems.at[k],
                                        device_id=device, device_id_type=MESH)


def _place_shard(w, chip):
    rows, cols = w.shape
    tm = 128

    def body(chip_ref, w_ref, o_ref):
        o_ref[0] = w_ref[...].astype(o_ref.dtype)

    return pl.pallas_call(
        body,
        grid_spec=pltpu.PrefetchScalarGridSpec(
            num_scalar_prefetch=1, grid=(rows // tm,),
            in_specs=[pl.BlockSpec((tm, cols), lambda i, chip_ref: (i, 0))],
            out_specs=pl.BlockSpec((1, tm, cols), lambda i, chip_ref: (chip_ref[0], i, 0))),
        out_shape=jax.ShapeDtypeStruct((4, rows, cols), MXU_DTYPE),
        name="place_shard",
    )(chip, w)


def _gather_weights(slots):
    n = len(slots)

    def body(*refs):
        outs = refs[n:2 * n]
        send_sems, recv_sems = refs[2 * n:]
        x, y, c = _place()
        chip = 2 * x + y
        sibling = (x, y, 1 - c)
        started = []
        for a in range(n):
            h = outs[a].shape[1] // 2
            mine = outs[a].at[chip, pl.ds(pl.multiple_of(c * h, 8), h)]
            for t, (fx, fy) in enumerate(_CHIP_FLIPS):
                cp = _remote(mine, mine, send_sems, recv_sems, 6 * a + t, (_flip(x, fx), _flip(y, fy), c))
                cp.start()
                started.append(cp)
        for a in range(n):
            h = outs[a].shape[1] // 2
            rows = pl.ds(pl.multiple_of(c * h, 8), h)
            for t, (fx, fy) in enumerate(_CHIP_FLIPS):
                landed = outs[a].at[2 * _flip(x, fx) + _flip(y, fy), rows]
                _remote(landed, landed, send_sems, recv_sems, 6 * a + t, sibling).wait_recv()
                cp = _remote(landed, landed, send_sems, recv_sems, 6 * a + 3 + t, sibling)
                cp.start()
                started.append(cp)
        for a in range(n):
            h = outs[a].shape[1] // 2
            rows = pl.ds(pl.multiple_of((1 - c) * h, 8), h)
            for t, (fx, fy) in enumerate(_CHIP_FLIPS):
                handed = outs[a].at[2 * _flip(x, fx) + _flip(y, fy), rows]
                _remote(handed, handed, send_sems, recv_sems, 6 * a + 3 + t, sibling).wait_recv()
        for cp in started:
            cp.wait_send()

    return pl.pallas_call(
        body, in_specs=[_ANY] * n, out_specs=[_ANY] * n,
        out_shape=[jax.ShapeDtypeStruct(s.shape, s.dtype) for s in slots],
        input_output_aliases={a: a for a in range(n)},
        scratch_shapes=[pltpu.SemaphoreType.DMA((6 * n,)), pltpu.SemaphoreType.DMA((6 * n,))],
        name="gather_weights",
    )(*slots)


def _sibling_exchange_halves(grads):
    n = len(grads)

    def body(*refs):
        ins, outs = refs[:n], refs[n:2 * n]
        send_sems, recv_sems = refs[2 * n:]
        x, y, c = _place()
        copies = []
        for a in range(n):
            h = ins[a].shape[1] // 2
            theirs = pl.ds(pl.multiple_of((1 - c) * h, 8), h)
            copies.append(_remote(ins[a].at[:, theirs], outs[a], send_sems, recv_sems, a, (x, y, 1 - c)))
        for cp in copies:
            cp.start()
        for cp in copies:
            cp.wait()

    return pl.pallas_call(
        body, in_specs=[_ANY] * n, out_specs=[_ANY] * n,
        out_shape=[jax.ShapeDtypeStruct((4, g.shape[1] // 2, g.shape[2]), g.dtype) for g in grads],
        scratch_shapes=[pltpu.SemaphoreType.DMA((n,)), pltpu.SemaphoreType.DMA((n,))],
        name="grads_to_sibling",
    )(*grads)


def _add_halves(g, r, half):
    _, rows, cols = g.shape
    h = rows // 2
    tm = min(h, 128)
    nt = h // tm

    def body(half_ref, g_ref, r_ref, f_ref, b_ref):
        s = g_ref[...] + r_ref[...]
        f_ref[...] = s
        b_ref[...] = s.astype(b_ref.dtype)

    spec = pl.BlockSpec((1, tm, cols), lambda j, i, half_ref: (j, i, 0))
    return pl.pallas_call(
        body,
        grid_spec=pltpu.PrefetchScalarGridSpec(
            num_scalar_prefetch=1, grid=(4, nt),
            in_specs=[pl.BlockSpec((1, tm, cols), lambda j, i, half_ref: (j, half_ref[0] * nt + i, 0)), spec],
            out_specs=[spec, spec]),
        out_shape=[jax.ShapeDtypeStruct((4, h, cols), F32), jax.ShapeDtypeStruct((4, h, cols), BF16)],
        name="add_sibling_grads",
    )(half, g, r)


def _chip_exchange(parts):
    n = len(parts)

    def body(*refs):
        ins, outs = refs[:n], refs[n:2 * n]
        send_sems, recv_sems = refs[2 * n:]
        x, y, c = _place()
        copies = []
        for a in range(n):
            for t, (fx, fy) in enumerate(_CHIP_FLIPS):
                tx, ty = _flip(x, fx), _flip(y, fy)
                copies.append(_remote(ins[a].at[2 * tx + ty], outs[a].at[t], send_sems, recv_sems, 3 * a + t,
                                      (tx, ty, c)))
        for cp in copies:
            cp.start()
        for cp in copies:
            cp.wait()

    return pl.pallas_call(
        body, in_specs=[_ANY] * n, out_specs=[_ANY] * n,
        out_shape=[jax.ShapeDtypeStruct((3, *p.shape[1:]), p.dtype) for p in parts],
        scratch_shapes=[pltpu.SemaphoreType.DMA((3 * n,)), pltpu.SemaphoreType.DMA((3 * n,))],
        name="grads_to_chips",
    )(*parts)


def _add_chips(own, recv, where):
    _, h, cols = own.shape
    tm = min(h, 128)
    nt = h // tm

    def body(where_ref, o_ref, r_ref, out_ref):
        out_ref[...] = ((o_ref[0] + r_ref[0].astype(F32)) + r_ref[1].astype(F32)) + r_ref[2].astype(F32)

    return pl.pallas_call(
        body,
        grid_spec=pltpu.PrefetchScalarGridSpec(
            num_scalar_prefetch=1, grid=(nt,),
            in_specs=[pl.BlockSpec((1, tm, cols), lambda i, where_ref: (where_ref[0], i, 0)),
                      pl.BlockSpec((3, tm, cols), lambda i, where_ref: (0, i, 0))],
            out_specs=pl.BlockSpec((tm, cols), lambda i, where_ref: (where_ref[1] * nt + i, 0))),
        out_shape=jax.ShapeDtypeStruct((2 * h, cols), F32),
        name="add_chip_grads",
    )(where, own, recv)


def _share_halves(shards):
    n = len(shards)

    def body(*refs):
        outs = refs[n:2 * n]
        send_sems, recv_sems = refs[2 * n:]
        x, y, c = _place()
        copies = []
        for a in range(n):
            h = outs[a].shape[0] // 2
            mine = outs[a].at[pl.ds(pl.multiple_of(c * h, 8), h)]
            copies.append(_remote(mine, mine, send_sems, recv_sems, a, (x, y, 1 - c)))
        for cp in copies:
            cp.start()
        for a, cp in enumerate(copies):
            cp.wait_send()
            h = outs[a].shape[0] // 2
            theirs = outs[a].at[pl.ds(pl.multiple_of((1 - c) * h, 8), h)]
            _remote(theirs, theirs, send_sems, recv_sems, a, (x, y, 1 - c)).wait_recv()

    return pl.pallas_call(
        body, in_specs=[_ANY] * n, out_specs=[_ANY] * n,
        out_shape=[jax.ShapeDtypeStruct(p.shape, p.dtype) for p in shards],
        input_output_aliases={a: a for a in range(n)},
        scratch_shapes=[pltpu.SemaphoreType.DMA((n,)), pltpu.SemaphoreType.DMA((n,))],
        name="share_reduced_halves",
    )(*shards)


def _exchange_small(rows, reduce):
    cols = rows[0].shape[1]
    n = len(rows)
    assert sum(r.shape[0] for r in rows) <= 8

    def body(*refs):
        ins, out_ref = refs[:n], refs[n]
        vec_ref, gath_ref, send_sems, recv_sems = refs[n + 1:]
        x, y, c = _place()
        me = 4 * x + 2 * y + c
        vec_ref[...] = jnp.zeros(vec_ref.shape, F32)
        at = 0
        for r in ins:
            vec_ref[at:at + r.shape[0], :] = r[...]
            at += r.shape[0]
        copies = []
        for k in range(1, 8):
            peer = (_flip(x, (k >> 2) & 1), _flip(y, (k >> 1) & 1), _flip(c, k & 1))
            copies.append(_remote(vec_ref, gath_ref.at[me], send_sems, recv_sems, k - 1, peer))
        for cp in copies:
            cp.start()
        gath_ref[me] = vec_ref[...]
        for cp in copies:
            cp.wait()
        if reduce:
            tot = gath_ref[0]
            for dev in range(1, 8):
                tot = tot + gath_ref[dev]
            out_ref[...] = tot
            out_ref[7:8, :] = jnp.zeros((1, cols), F32) + jnp.sum(tot[7:8, :])
        else:
            out_ref[...] = gath_ref[...]

    vm = pl.BlockSpec(memory_space=pltpu.VMEM)
    return pl.pallas_call(
        body, in_specs=[vm] * n, out_specs=vm,
        out_shape=jax.ShapeDtypeStruct((8, cols) if reduce else (8, 8, cols), F32),
        scratch_shapes=[pltpu.VMEM((8, cols), F32), pltpu.VMEM((8, 8, cols), F32),
                        pltpu.SemaphoreType.DMA((7,)), pltpu.SemaphoreType.DMA((7,))],
        name="reduce_small" if reduce else "gather_small",
    )(*rows)


def _adamw(w, g, m, v, name):
    rows, cols = w.shape
    tm = 128 if rows % 128 == 0 else rows

    def body(w_ref, g_ref, m_ref, v_ref, d_ref, m2_ref, v2_ref):
        gr = g_ref[...]
        m2 = ADAM_B1 * m_ref[...] + (1.0 - ADAM_B1) * gr
        v2 = ADAM_B2 * v_ref[...] + (1.0 - ADAM_B2) * (gr * gr)
        m_hat = m2 / (1.0 - ADAM_B1 ** ADAM_STEP)
        v_hat = v2 / (1.0 - ADAM_B2 ** ADAM_STEP)
        d_ref[...] = -ADAM_LR * (m_hat / (jnp.sqrt(v_hat) + ADAM_EPS) + ADAM_WD * w_ref[...])
        m2_ref[...] = m2
        v2_ref[...] = v2

    spec = pl.BlockSpec((tm, cols), lambda i: (i, 0))
    sds = jax.ShapeDtypeStruct((rows, cols), F32)
    return pl.pallas_call(body, grid=(rows // tm,), in_specs=[spec] * 4, out_specs=[spec] * 3,
                          out_shape=[sds] * 3, name=name)(w, g, m, v)


def kernel(x, norm_g, w_in, b_merge, conv_w, w_out_conv, w_out_attn, w_o, final_g, loss_target, m_norm_g, m_w_in, m_b_merge, m_conv_w, m_w_out_conv, m_w_out_attn, m_w_o, m_final_g, v_norm_g, v_w_in, v_b_merge, v_conv_w, v_w_out_conv, v_w_out_attn, v_w_o, v_final_g):
    mx, my, mc = _place()
    chip = (2 * mx + my).astype(jnp.int32)
    seq = x.shape[1]

    chip1 = chip.reshape(1)
    w4, woc4, woa4, wo4 = _gather_weights([_place_shard(w[0], chip1) for w in (w_in, w_out_conv, w_out_attn, w_o)])
    taps8 = _exchange_small([conv_w[0]], reduce=False)
    taps = jnp.concatenate([taps8[2 * j, :3, :] for j in range(4)], axis=1)

    loss8, grad_x, d_norm_g, d_w4, d_bias, d_taps, d_woc, d_woa, d_wo, d_final_g = _local_grads(
        x[0], loss_target[0], norm_g, w4, b_merge, taps, woc4.reshape(D_MODEL, D_MODEL),
        woa4.reshape(D_MODEL, D_MODEL), wo4.reshape(D_MODEL, D_MODEL), final_g.reshape(1, D_MODEL))

    grads = [d_w4, d_woc.reshape(4, -1, D_MODEL), d_woa.reshape(4, -1, D_MODEL), d_wo.reshape(4, -1, D_MODEL)]
    from_sibling = _sibling_exchange_halves(grads)
    half = mc.astype(jnp.int32).reshape(1)
    partial = [_add_halves(g, r, half) for g, r in zip(grads, from_sibling)]
    from_chips = _chip_exchange([p[1] for p in partial])
    where = jnp.stack([chip, mc.astype(jnp.int32)])
    reduced = [_add_chips(p[0], r, where) for p, r in zip(partial, from_chips)]
    g_w_in, g_woc, g_woa, g_wo = _share_halves(reduced)

    small = _exchange_small([d_norm_g, d_bias.reshape(2, D_MODEL), d_taps, d_final_g, loss8.reshape(1, D_MODEL)],
                            reduce=True)
    loss = (0.5 / D_MODEL) * small[7, 0]
    g_norm_g = small[0:1]
    g_bias = small[1:3].reshape(1, 2 * D_MODEL)
    g_taps = lax.dynamic_slice(small[3:6], (0, chip * (D_MODEL // 4)), (3, D_MODEL // 4))
    g_final_g = small[6:7]

    upd = [
        _adamw(norm_g, g_norm_g, m_norm_g, v_norm_g, "adamw_norm_g"),
        _adamw(w_in[0], g_w_in, m_w_in[0], v_w_in[0], "adamw_w_in"),
        _adamw(b_merge, g_bias, m_b_merge, v_b_merge, "adamw_b_merge"),
        _adamw(conv_w[0], g_taps, m_conv_w[0], v_conv_w[0], "adamw_conv_w"),
        _adamw(w_out_conv[0], g_woc, m_w_out_conv[0], v_w_out_conv[0], "adamw_w_out_conv"),
        _adamw(w_out_attn[0], g_woa, m_w_out_attn[0], v_w_out_attn[0], "adamw_w_out_attn"),
        _adamw(w_o[0], g_wo, m_w_o[0], v_w_o[0], "adamw_w_o"),
        _adamw(final_g.reshape(1, D_MODEL), g_final_g, m_final_g.reshape(1, D_MODEL),
               v_final_g.reshape(1, D_MODEL), "adamw_final_g"),
    ]
    shapes = [norm_g.shape, w_in.shape, b_merge.shape, conv_w.shape, w_out_conv.shape, w_out_attn.shape,
              w_o.shape, final_g.shape]
    grads_out = [g_norm_g, g_w_in, g_bias, g_taps, g_woc, g_woa, g_wo, g_final_g]
    outs = [loss, grad_x.reshape(1, seq, D_MODEL)]
    outs += [g.reshape(s) for g, s in zip(grads_out, shapes)]
    for k in range(3):
        outs += [u[k].reshape(s) for u, s in zip(upd, shapes)]
    return tuple(outs)
```

```python
import functools

import numpy as np
import jax
import jax.numpy as jnp
from jax import lax
from jax.experimental import pallas as pl
from jax.experimental.pallas import tpu as pltpu

F32 = jnp.float32
BF16 = jnp.bfloat16
MXU_DTYPE = jnp.bfloat16
ACT_DTYPE = jnp.bfloat16

D_MODEL = 1024
N_HEADS = 16
HEAD_DIM = 64
QB = 128
N_RES = 16
LANES = 128
HP = N_HEADS * HEAD_DIM // LANES
IN_COLS = 10 * D_MODEL
SHARD_COLS = IN_COLS // 4
EPS = 1e-6
NEG = -1e30

ADAM_LR, ADAM_B1, ADAM_B2, ADAM_EPS, ADAM_WD, ADAM_STEP = 0.001, 0.9, 0.999, 1e-08, 0.01, 10

PATTERNS = {1: (16, 16), 4: (4, 32), 16: (1, 128)}

_NN = (((1,), (0,)), ((), ()))
_NT = (((1,), (1,)), ((), ()))


def _dot(a, b):
    return lax.dot_general(a.astype(MXU_DTYPE), b.astype(MXU_DTYPE), _NN, preferred_element_type=F32)


def _dot_nt(a, b):
    return lax.dot_general(a.astype(MXU_DTYPE), b.astype(MXU_DTYPE), _NT, preferred_element_type=F32)


def _split3(x):
    hi = x.astype(BF16)
    r1 = x - hi.astype(F32)
    mid = r1.astype(BF16)
    lo = (r1 - mid.astype(F32)).astype(BF16)
    return hi, mid, lo


def _select_rows(sel, x):
    return sum(lax.dot_general(sel, t, _NN, preferred_element_type=F32) for t in _split3(x))


def _select_cols(x, sel):
    return sum(lax.dot_general(t, sel, _NN, preferred_element_type=F32) for t in _split3(x))


def _sigmoid(z):
    return 1.0 / (1.0 + jnp.exp(-z))


def _perm_matrix():
    idx = np.arange(256)
    p = np.zeros((256, 256), np.float32)
    p[(idx % 16) * 16 + idx // 16, idx] = 1.0
    return jnp.asarray(p, BF16)


def _head_expand_matrix():
    e = np.zeros((LANES, D_MODEL), np.float32)
    for h in range(N_HEADS):
        e[8 * h, HEAD_DIM * h:HEAD_DIM * (h + 1)] = 1.0
    return jnp.asarray(e, BF16)


def _head_sum_matrix():
    e = np.zeros((D_MODEL, LANES), np.float32)
    for h in range(N_HEADS):
        e[HEAD_DIM * h:HEAD_DIM * (h + 1), 8 * h:8 * (h + 1)] = 1.0
    return jnp.asarray(e, BF16)


def _attn_tables(d):
    g_n, rq = PATTERNS[d]
    q_n = g_n * rq
    gq, iq = np.arange(q_n) // rq, np.arange(q_n) % rq

    def tab(kn, base):
        k_n = g_n * kn
        gk, jk = np.arange(k_n) // kn, np.arange(k_n) % kn
        delta = g_n * (base + iq[:, None] - jk[None, :]) + gq[:, None] - gk[None, :]
        valid = (delta >= 0) & (delta <= QB)
        dist = np.where(valid, d * delta, 0).astype(np.float32)
        madd = np.where(valid, 0.0, NEG).astype(np.float32)
        return dist, madd

    d0, m0 = tab(rq if g_n == 1 else 2 * rq, 0)
    d1, m1 = tab(2 * rq, rq)
    return d0, m0, d1, m1


def _alibi_slopes():
    return jnp.exp2(-8.0 * jnp.arange(1, N_HEADS + 1, dtype=F32) / N_HEADS)


def _to_residue_major(x, tgt):
    s_n, c_n = x.shape
    lr = s_n // N_RES

    def body(p_ref, x_ref, t_ref, xo_ref, to_ref):
        pm = p_ref[...]
        xo_ref[...] = _select_rows(pm, x_ref[...]).reshape(16, 16, c_n)
        to_ref[...] = _select_rows(pm, t_ref[...]).reshape(16, 16, c_n)

    nat = pl.BlockSpec((256, c_n), lambda i: (i, 0))
    res = pl.BlockSpec((16, 16, c_n), lambda i: (0, i, 0))
    xo, to = pl.pallas_call(
        body, grid=(s_n // 256,),
        in_specs=[pl.BlockSpec((256, 256), lambda i: (0, 0)), nat, nat],
        out_specs=[res, res],
        out_shape=[jax.ShapeDtypeStruct((16, lr, c_n), F32)] * 2,
        name="perm_in",
    )(_perm_matrix(), x, tgt)
    return xo.reshape(s_n, c_n), to.reshape(s_n, c_n)


def _to_natural(gxp):
    s_n, c_n = gxp.shape
    lr = s_n // N_RES

    def body(p_ref, g_ref, o_ref):
        o_ref[...] = _select_rows(p_ref[...], g_ref[...].reshape(256, c_n))

    return pl.pallas_call(
        body, grid=(s_n // 256,),
        in_specs=[pl.BlockSpec((256, 256), lambda i: (0, 0)),
                  pl.BlockSpec((16, 16, c_n), lambda i: (0, i, 0))],
        out_specs=pl.BlockSpec((256, c_n), lambda i: (i, 0)),
        out_shape=jax.ShapeDtypeStruct((s_n, c_n), F32),
        name="perm_out",
    )(_perm_matrix(), gxp.reshape(16, lr, c_n))


def _rms_in(xp, norm_g):
    s_n, c_n = xp.shape
    tm = 512

    def body(x_ref, g_ref, u_ref, ut_ref):
        x = x_ref[...]
        r = lax.rsqrt(jnp.mean(x * x, axis=-1, keepdims=True) + EPS)
        u = x * r * g_ref[...]
        u_ref[...] = u.astype(u_ref.dtype)
        ut_ref[...] = u.T.astype(ut_ref.dtype)

    return pl.pallas_call(
        body, grid=(s_n // tm,),
        in_specs=[pl.BlockSpec((tm, c_n), lambda i: (i, 0)), pl.BlockSpec((1, c_n), lambda i: (0, 0))],
        out_specs=[pl.BlockSpec((tm, c_n), lambda i: (i, 0)), pl.BlockSpec((c_n, tm), lambda i: (0, i))],
        out_shape=[jax.ShapeDtypeStruct((s_n, c_n), ACT_DTYPE), jax.ShapeDtypeStruct((c_n, s_n), ACT_DTYPE)],
        name="rms_in",
    )(xp, norm_g)


def _in_proj(u, w4):
    s_n = u.shape[0]
    tn, cm = 512, 512
    per = SHARD_COLS // tn

    def body(a_ref, b_ref, o_ref):
        b = b_ref[...]
        for c in range(s_n // cm):
            o_ref[c * cm:(c + 1) * cm, :] = _dot(a_ref[c * cm:(c + 1) * cm, :], b).astype(o_ref.dtype)

    return pl.pallas_call(
        body, grid=(IN_COLS // tn,),
        in_specs=[pl.BlockSpec((s_n, D_MODEL), lambda n: (0, 0)),
                  pl.BlockSpec((None, D_MODEL, tn), lambda n: (n // per, 0, n % per))],
        out_specs=pl.BlockSpec((s_n, tn), lambda n: (0, n)),
        out_shape=jax.ShapeDtypeStruct((s_n, IN_COLS), ACT_DTYPE),
        name="in_proj",
    )(u, w4)


def _conv_terms(xc_ref, cg_ref, r, row, lr):
    def a_of(q):
        return cg_ref[q].astype(F32) * xc_ref[q].astype(F32)

    def shift_down(v):
        return jnp.where(row >= 1, pltpu.roll(v, 1, 0), 0.0)

    a = a_of(r)
    am1 = a_of(r - 1) if r >= 1 else shift_down(a_of(N_RES - 1))
    am2 = a_of(r - 2) if r >= 2 else shift_down(a_of(N_RES - 2 + r))
    return a, am1, am2


def _conv_fwd(proj, conv_w):
    s_n = proj.shape[0]
    lr = s_n // N_RES
    pv = proj.reshape(N_RES, lr, IN_COLS)

    def body(xc_ref, bg_ref, cg_ref, zc_ref, w_ref, hc_ref, hct_ref):
        w = w_ref[...]
        row = lax.broadcasted_iota(jnp.int32, (lr, LANES), 0)
        for r in range(N_RES):
            a, am1, am2 = _conv_terms(xc_ref, cg_ref, r, row, lr)
            c = w[0:1] * am2 + w[1:2] * am1 + w[2:3] * a
            z = zc_ref[r].astype(F32)
            hc = z * _sigmoid(z) * bg_ref[r].astype(F32) * c
            hc_ref[r] = hc.astype(hc_ref.dtype)
            hct_ref[:, r * lr:(r + 1) * lr] = hc.T.astype(hct_ref.dtype)

    def col(part):
        return pl.BlockSpec((N_RES, lr, LANES), lambda j: (0, 0, part * 8 + j))

    hc, hct = pl.pallas_call(
        body, grid=(D_MODEL // LANES,),
        in_specs=[col(0), col(1), col(2), col(3), pl.BlockSpec((3, LANES), lambda j: (0, j))],
        out_specs=[pl.BlockSpec((N_RES, lr, LANES), lambda j: (0, 0, j)),
                   pl.BlockSpec((LANES, s_n), lambda j: (j, 0))],
        out_shape=[jax.ShapeDtypeStruct((N_RES, lr, D_MODEL), ACT_DTYPE),
                   jax.ShapeDtypeStruct((D_MODEL, s_n), ACT_DTYPE)],
        name="conv_fwd",
    )(pv, pv, pv, pv, conv_w)
    return hc.reshape(s_n, D_MODEL), hct


RES_PER_STEP = 8
FWD_BATCH = {1: 4, 4: 8, 16: RES_PER_STEP}
BWD_BATCH = {1: 2, 4: 8, 16: RES_PER_STEP}

_BNT = (((2,), (2,)), ((0,), (0,)))
_BNN = (((2,), (1,)), ((0,), (0,)))


def _bdot(a, b, dims):
    return lax.dot_general(a.astype(MXU_DTYPE), b.astype(MXU_DTYPE), dims, preferred_element_type=F32)


def _pattern_view_shape(s_n, c_n, g_n, lead=()):
    lr = s_n // N_RES
    return (*lead, 4, 4, lr, c_n) if g_n == 4 else (*lead, N_RES, lr, c_n)


def _pattern_view(a, g_n, lead=()):
    return a.reshape(_pattern_view_shape(a.shape[-2], a.shape[-1], g_n, lead))


def _pattern_grid(g_n):
    return (N_RES // RES_PER_STEP if g_n == 1 else N_RES // g_n, HP)


def _pattern_spec(g_n, lr, col_of_hp, lead=()):
    z = (0,) * len(lead)
    if g_n == 16:
        return pl.BlockSpec((*lead, 16, lr, LANES), lambda r, hp: (*z, 0, 0, col_of_hp(hp)))
    if g_n == 4:
        return pl.BlockSpec((*lead, 4, None, lr, LANES), lambda r, hp: (*z, 0, r, 0, col_of_hp(hp)))
    return pl.BlockSpec((*lead, RES_PER_STEP, lr, LANES), lambda r, hp: (*z, r, 0, col_of_hp(hp)))


class _Units:
    def __init__(self, g_n, rq):
        self.g_n, self.rq, self.per_res = g_n, rq, g_n == 1

    def load(self, ref, starts, rows):
        if self.per_res:
            return ref[:, pl.ds(starts[0], rows), :]
        return jnp.stack([ref[:, pl.ds(s, rows), :].reshape(self.g_n * rows, LANES) for s in starts])

    def store(self, ref, starts, rows, val, add=False, lead=()):
        for b, s in enumerate(starts):
            idx = (*lead, slice(None), pl.ds(s, rows), slice(None))
            v = val if self.per_res else val[b].reshape(self.g_n, rows, LANES)
            ref[idx] = (ref[idx] + v if add else v).astype(ref.dtype)
            if self.per_res:
                break


def _stack_heads(x, low):
    zero = jnp.zeros_like(x)
    return jnp.concatenate([jnp.where(low, x, zero), jnp.where(low, zero, x)], axis=1)


def _batches(nb, rq, size, per_res):
    if per_res:
        return [(0, 0)], nb - 1
    first = [(n * rq, max(n - 1, 0) * rq) for n in range(min(size, nb))]
    assert nb % len(first) == 0
    return first, nb // len(first) - 1


def _attn_fwd(proj, slopes, d):
    g_n, rq = PATTERNS[d]
    un = _Units(g_n, rq)
    s_n = proj.shape[0]
    lr = s_n // N_RES
    nb = lr // rq
    q_n = g_n * rq
    d0, m0, d1, m1 = _attn_tables(d)
    k0_rows = d0.shape[1] // g_n
    first, n_more = _batches(nb, rq, FWD_BATCH[d], un.per_res)
    bsz = len(first)

    def body(sl_ref, q_ref, k_ref, v_ref, d0_ref, m0_ref, d1_ref, m1_ref, o_ref, lse_ref, b0_ref, b1_ref):
        hp = pl.program_id(1)

        @pl.when(hp == 0)
        def _():
            lse_ref[...] = jnp.zeros(lse_ref.shape, F32)

        for h in (0, 1):
            slope = sl_ref[2 * hp + h]
            b0_ref[h * q_n:(h + 1) * q_n, :] = m0_ref[...] - slope * d0_ref[...]
            b1_ref[h * q_n:(h + 1) * q_n, :] = m1_ref[...] - slope * d1_ref[...]

        lane = lax.broadcasted_iota(jnp.int32, (1, q_n, LANES), 2)
        low = lane < HEAD_DIM
        grp = lane // 8

        def batch(q_starts, k_starts, k_rows, bias):
            qq = _stack_heads(un.load(q_ref, q_starts, rq) * 0.125, low)
            s = _bdot(qq, un.load(k_ref, k_starts, k_rows), _BNT) + bias
            m = jnp.max(s, axis=2, keepdims=True)
            p = jnp.exp(s - m)
            l = jnp.sum(p, axis=2, keepdims=True)
            o = _bdot(p, un.load(v_ref, k_starts, k_rows), _BNN) * (1.0 / l)
            lse = m + jnp.log(l)
            un.store(o_ref, q_starts, rq, jnp.where(low, o[:, :q_n], o[:, q_n:]))
            upd = jnp.where(grp == 2 * hp, lse[:, :q_n], 0.0) + jnp.where(grp == 2 * hp + 1, lse[:, q_n:], 0.0)
            un.store(lse_ref, q_starts, rq, upd, add=True)

        if un.per_res:
            batch([0], [0], k0_rows, b0_ref[...][None])
        else:
            bias = jnp.concatenate([b0_ref[...][None]] + [b1_ref[...][None]] * (bsz - 1), axis=0)
            batch([q for q, _ in first], [k for _, k in first], 2 * rq, bias)

        def more(j, carry):
            n0 = j * bsz
            qs = [pl.multiple_of((n0 + i) * rq, rq) for i in range(bsz)]
            ks = [pl.multiple_of((n0 + i - 1) * rq, rq) for i in range(bsz)]
            batch(qs, ks, 2 * rq, b1_ref[...][None])
            return carry

        lax.fori_loop(1, 1 + n_more, more, 0)

    pv = _pattern_view(proj, g_n)
    full = lambda a: pl.BlockSpec(a.shape, lambda r, hp: (0, 0))
    o, lse = pl.pallas_call(
        body, grid=_pattern_grid(g_n),
        in_specs=[pl.BlockSpec(memory_space=pltpu.SMEM),
                  _pattern_spec(g_n, lr, lambda hp: 32 + hp),
                  _pattern_spec(g_n, lr, lambda hp: 40 + hp),
                  _pattern_spec(g_n, lr, lambda hp: 48 + hp),
                  full(d0), full(m0), full(d1), full(m1)],
        out_specs=[_pattern_spec(g_n, lr, lambda hp: hp), _pattern_spec(g_n, lr, lambda hp: 0)],
        out_shape=[jax.ShapeDtypeStruct(_pattern_view_shape(s_n, D_MODEL, g_n), F32),
                   jax.ShapeDtypeStruct(_pattern_view_shape(s_n, LANES, g_n), F32)],
        scratch_shapes=[pltpu.VMEM((2 * q_n, d0.shape[1]), F32), pltpu.VMEM((2 * q_n, 2 * q_n), F32)],
        name=f"attn_fwd_d{d}",
    )(slopes, pv, pv, pv, d0, m0, d1, m1)
    return o.reshape(s_n, D_MODEL), lse.reshape(s_n, LANES)


def _attn_combine(outs, lses, proj):
    s_n = proj.shape[0]
    tm = 512

    def body(o1_ref, o2_ref, o3_ref, l1_ref, l2_ref, l3_ref, za_ref, e_ref, o_ref, lse_ref, ha_ref, hat_ref):
        ls = [l1_ref[...], l2_ref[...], l3_ref[...]]
        mx = jnp.maximum(jnp.maximum(ls[0], ls[1]), ls[2])
        den = sum(jnp.exp(l - mx) for l in ls)
        lse = mx + jnp.log(den)
        lse_ref[...] = lse
        o = jnp.zeros((tm, D_MODEL), F32)
        for l, oref in zip(ls, (o1_ref, o2_ref, o3_ref)):
            o = o + _select_cols(jnp.exp(l - lse), e_ref[...]) * oref[...]
        o_ref[...] = o
        z = za_ref[...].astype(F32)
        ha = z * _sigmoid(z) * o
        ha_ref[...] = ha.astype(ha_ref.dtype)
        hat_ref[...] = ha.T.astype(hat_ref.dtype)

    row = lambda w: pl.BlockSpec((tm, w), lambda i: (i, 0))
    return pl.pallas_call(
        body, grid=(s_n // tm,),
        in_specs=[row(D_MODEL)] * 3 + [row(LANES)] * 3
        + [pl.BlockSpec((tm, D_MODEL), lambda i: (i, 7)), pl.BlockSpec((LANES, D_MODEL), lambda i: (0, 0))],
        out_specs=[row(D_MODEL), row(LANES), row(D_MODEL), pl.BlockSpec((D_MODEL, tm), lambda i: (0, i))],
        out_shape=[jax.ShapeDtypeStruct((s_n, D_MODEL), F32), jax.ShapeDtypeStruct((s_n, LANES), F32),
                   jax.ShapeDtypeStruct((s_n, D_MODEL), ACT_DTYPE), jax.ShapeDtypeStruct((D_MODEL, s_n), ACT_DTYPE)],
        name="attn_combine",
    )(*outs, *lses, proj, _head_expand_matrix())


def _gates(gc_ref, ga_ref, b_ref):
    b = b_ref[...]
    gc = _sigmoid(gc_ref[...].astype(F32) + b[:, :D_MODEL])
    ga = _sigmoid(ga_ref[...].astype(F32) + b[:, D_MODEL:])
    return gc, ga


def _merge_loss(hc, ha, woc, woa, wo, proj, b_merge, xp, final_g, tgt):
    s_n = xp.shape[0]
    tm = 512

    def body(hc_ref, ha_ref, woc_ref, woa_ref, wo_ref, gc_ref, ga_ref, b_ref, x_ref, gf_ref, t_ref,
             yc_ref, ya_ref, mg_ref, mgt_ref, dh_ref, dhb_ref, dgf_ref, loss_ref):
        i = pl.program_id(0)

        @pl.when(i == 0)
        def _():
            dgf_ref[...] = jnp.zeros(dgf_ref.shape, F32)
            loss_ref[...] = jnp.zeros(loss_ref.shape, F32)

        yc = _dot(hc_ref[...], woc_ref[...])
        ya = _dot(ha_ref[...], woa_ref[...])
        gc, ga = _gates(gc_ref, ga_ref, b_ref)
        mg = gc * yc + ga * ya
        yc_ref[...] = yc.astype(yc_ref.dtype)
        ya_ref[...] = ya.astype(ya_ref.dtype)
        mg_ref[...] = mg.astype(mg_ref.dtype)
        mgt_ref[...] = mg.T.astype(mgt_ref.dtype)
        h2 = x_ref[...] + _dot(mg, wo_ref[...])
        r2 = lax.rsqrt(jnp.mean(h2 * h2, axis=-1, keepdims=True) + EPS)
        nrm = h2 * r2
        gf = gf_ref[...]
        err = nrm * gf - t_ref[...]
        e2 = (err * err).reshape(tm // 8, 8, D_MODEL).sum(axis=0)
        loss_ref[...] += sum(e2[:, c * LANES:(c + 1) * LANES] for c in range(D_MODEL // LANES))
        dy = err * (1.0 / D_MODEL)
        dgf_ref[...] += jnp.sum(dy * nrm, axis=0, keepdims=True)
        dn = dy * gf
        dh2 = r2 * (dn - nrm * jnp.mean(dn * nrm, axis=-1, keepdims=True))
        dh_ref[...] = dh2
        dhb_ref[...] = dh2.astype(dhb_ref.dtype)

    row = pl.BlockSpec((tm, D_MODEL), lambda i: (i, 0))
    wsp = pl.BlockSpec((D_MODEL, D_MODEL), lambda i: (0, 0))
    vec = lambda w: pl.BlockSpec((1, w), lambda i: (0, 0))
    act = jax.ShapeDtypeStruct((s_n, D_MODEL), ACT_DTYPE)
    return pl.pallas_call(
        body, grid=(s_n // tm,),
        in_specs=[row, row, wsp, wsp, wsp,
                  pl.BlockSpec((tm, D_MODEL), lambda i: (i, 8)), pl.BlockSpec((tm, D_MODEL), lambda i: (i, 9)),
                  vec(2 * D_MODEL), row, vec(D_MODEL), row],
        out_specs=[row, row, row, pl.BlockSpec((D_MODEL, tm), lambda i: (0, i)), row, row,
                   vec(D_MODEL), pl.BlockSpec((8, LANES), lambda i: (0, 0))],
        out_shape=[act, act, act, jax.ShapeDtypeStruct((D_MODEL, s_n), ACT_DTYPE),
                   jax.ShapeDtypeStruct((s_n, D_MODEL), F32), act,
                   jax.ShapeDtypeStruct((1, D_MODEL), F32), jax.ShapeDtypeStruct((8, LANES), F32)],
        name="merge_loss",
    )(hc, ha, woc, woa, wo, proj, proj, b_merge, xp, final_g, tgt)


def _merge_bwd(dh2b, wo, woc, woa, yc, ya, proj, b_merge, o):
    s_n = dh2b.shape[0]
    tm = 512

    def body(dh_ref, wo_ref, woc_ref, woa_ref, yc_ref, ya_ref, gc_ref, ga_ref, b_ref, o_ref, za_ref, e_ref,
             dyc_ref, dya_ref, dhc_ref, do_ref, dsum_ref, db3_ref, dbias_ref):
        i = pl.program_id(0)

        @pl.when(i == 0)
        def _():
            dbias_ref[...] = jnp.zeros(dbias_ref.shape, F32)

        dmg = _dot_nt(dh_ref[...], wo_ref[...])
        gc, ga = _gates(gc_ref, ga_ref, b_ref)
        dgc = dmg * yc_ref[...].astype(F32) * gc * (1.0 - gc)
        dga = dmg * ya_ref[...].astype(F32) * ga * (1.0 - ga)
        dbias_ref[:, :D_MODEL] += jnp.sum(dgc, axis=0, keepdims=True)
        dbias_ref[:, D_MODEL:] += jnp.sum(dga, axis=0, keepdims=True)
        dyc = dmg * gc
        dya = dmg * ga
        dyc_ref[...] = dyc.astype(dyc_ref.dtype)
        dya_ref[...] = dya.astype(dya_ref.dtype)
        dhc_ref[...] = _dot_nt(dyc, woc_ref[...]).astype(dhc_ref.dtype)
        dha = _dot_nt(dya, woa_ref[...])
        z = za_ref[...].astype(F32)
        sg = _sigmoid(z)
        ov = o_ref[...]
        dout = dha * z * sg
        do_ref[...] = dout.astype(do_ref.dtype)
        dsum_ref[...] = _select_cols(dout * ov, e_ref[...])
        db3_ref[0] = (dha * ov * sg * (1.0 + z * (1.0 - sg))).astype(db3_ref.dtype)
        db3_ref[1] = dgc.astype(db3_ref.dtype)
        db3_ref[2] = dga.astype(db3_ref.dtype)

    row = pl.BlockSpec((tm, D_MODEL), lambda i: (i, 0))
    wsp = pl.BlockSpec((D_MODEL, D_MODEL), lambda i: (0, 0))
    act = jax.ShapeDtypeStruct((s_n, D_MODEL), ACT_DTYPE)
    return pl.pallas_call(
        body, grid=(s_n // tm,),
        in_specs=[row, wsp, wsp, wsp, row, row,
                  pl.BlockSpec((tm, D_MODEL), lambda i: (i, 8)), pl.BlockSpec((tm, D_MODEL), lambda i: (i, 9)),
                  pl.BlockSpec((1, 2 * D_MODEL), lambda i: (0, 0)), row,
                  pl.BlockSpec((tm, D_MODEL), lambda i: (i, 7)), pl.BlockSpec((D_MODEL, LANES), lambda i: (0, 0))],
        out_specs=[row, row, row, row, pl.BlockSpec((tm, LANES), lambda i: (i, 0)),
                   pl.BlockSpec((3, tm, D_MODEL), lambda i: (0, i, 0)),
                   pl.BlockSpec((1, 2 * D_MODEL), lambda i: (0, 0))],
        out_shape=[act, act, act, act, jax.ShapeDtypeStruct((s_n, LANES), F32),
                   jax.ShapeDtypeStruct((3, s_n, D_MODEL), ACT_DTYPE),
                   jax.ShapeDtypeStruct((1, 2 * D_MODEL), F32)],
        name="merge_bwd",
    )(dh2b, wo, woc, woa, yc, ya, proj, proj, b_merge, o, proj, _head_sum_matrix())


def _mm_lhs_resident(a, b, tn, name):
    m_n, k_n = a.shape
    n_n = b.shape[1]

    def body(a_ref, b_ref, o_ref):
        o_ref[...] = _dot(a_ref[...], b_ref[...])

    return pl.pallas_call(
        body, grid=(n_n // tn,),
        in_specs=[pl.BlockSpec((m_n, k_n), lambda n: (0, 0)), pl.BlockSpec((k_n, tn), lambda n: (0, n))],
        out_specs=pl.BlockSpec((m_n, tn), lambda n: (0, n)),
        out_shape=jax.ShapeDtypeStruct((m_n, n_n), F32),
        name=name,
    )(a, b)


def _conv_bwd(proj, conv_w, dhc):
    s_n = proj.shape[0]
    lr = s_n // N_RES
    pv = proj.reshape(N_RES, lr, IN_COLS)

    def body(xc_ref, bg_ref, cg_ref, zc_ref, w_ref, dhc_ref, da4_ref, dw_ref, dc_ref):
        w = w_ref[...]
        row = lax.broadcasted_iota(jnp.int32, (lr, LANES), 0)
        dw = [jnp.zeros((1, LANES), F32) for _ in range(3)]
        for r in range(N_RES):
            a, am1, am2 = _conv_terms(xc_ref, cg_ref, r, row, lr)
            c = w[0:1] * am2 + w[1:2] * am1 + w[2:3] * a
            z = zc_ref[r].astype(F32)
            sg = _sigmoid(z)
            sz = z * sg
            bg = bg_ref[r].astype(F32)
            dh = dhc_ref[r].astype(F32)
            da4_ref[1, r] = (dh * sz * c).astype(da4_ref.dtype)
            da4_ref[3, r] = (dh * bg * c * sg * (1.0 + z * (1.0 - sg))).astype(da4_ref.dtype)
            dc = dh * sz * bg
            dc_ref[r] = dc
            dw[0] = dw[0] + jnp.sum(dc * am2, axis=0, keepdims=True)
            dw[1] = dw[1] + jnp.sum(dc * am1, axis=0, keepdims=True)
            dw[2] = dw[2] + jnp.sum(dc * a, axis=0, keepdims=True)
        dw_ref[0:1, :] = dw[0]
        dw_ref[1:2, :] = dw[1]
        dw_ref[2:3, :] = dw[2]

        def shift_up(v):
            return jnp.where(row < lr - 1, pltpu.roll(v, lr - 1, 0), 0.0)

        for r in range(N_RES):
            dp1 = dc_ref[r + 1] if r + 1 < N_RES else shift_up(dc_ref[0])
            dp2 = dc_ref[r + 2] if r + 2 < N_RES else shift_up(dc_ref[r + 2 - N_RES])
            da = w[2:3] * dc_ref[r] + w[1:2] * dp1 + w[0:1] * dp2
            da4_ref[0, r] = (da * cg_ref[r].astype(F32)).astype(da4_ref.dtype)
            da4_ref[2, r] = (da * xc_ref[r].astype(F32)).astype(da4_ref.dtype)

    def col(part):
        return pl.BlockSpec((N_RES, lr, LANES), lambda j: (0, 0, part * 8 + j))

    da4, dw = pl.pallas_call(
        body, grid=(D_MODEL // LANES,),
        in_specs=[col(0), col(1), col(2), col(3), pl.BlockSpec((3, LANES), lambda j: (0, j)),
                  pl.BlockSpec((N_RES, lr, LANES), lambda j: (0, 0, j))],
        out_specs=[pl.BlockSpec((4, N_RES, lr, LANES), lambda j: (0, 0, 0, j)),
                   pl.BlockSpec((3, LANES), lambda j: (0, j))],
        out_shape=[jax.ShapeDtypeStruct((4, N_RES, lr, D_MODEL), ACT_DTYPE),
                   jax.ShapeDtypeStruct((3, D_MODEL), F32)],
        scratch_shapes=[pltpu.VMEM((N_RES, lr, LANES), F32)],
        name="conv_bwd",
    )(pv, pv, pv, pv, conv_w, dhc.reshape(N_RES, lr, D_MODEL))
    return da4.reshape(4, s_n, D_MODEL), dw


def _attn_bwd(proj, dout, lse, dsum, slopes, d):
    g_n, rq = PATTERNS[d]
    un = _Units(g_n, rq)
    s_n = proj.shape[0]
    lr = s_n // N_RES
    nb = lr // rq
    q_n = g_n * rq
    d0, m0, d1, m1 = (np.ascontiguousarray(t.T) for t in _attn_tables(d))
    k0_rows = d0.shape[0] // g_n
    first, n_more = _batches(nb, rq, BWD_BATCH[d], un.per_res)
    cnt = len(first)
    bsz = RES_PER_STEP if un.per_res else cnt
    gd = RES_PER_STEP if un.per_res else g_n

    def body(sl_ref, q_ref, k_ref, v_ref, do_ref, lse_ref, ds_ref, d0_ref, m0_ref, d1_ref, m1_ref, out_ref,
             b0_ref, b1_ref, lt_ref, dt_ref, dk_ref, dv_ref):
        hp = pl.program_id(1)
        for h in (0, 1):
            slope = sl_ref[2 * hp + h]
            b0_ref[:, h * q_n:(h + 1) * q_n] = m0_ref[...] - slope * d0_ref[...]
            b1_ref[:, h * q_n:(h + 1) * q_n] = m1_ref[...] - slope * d1_ref[...]
        dk_ref[...] = jnp.zeros(dk_ref.shape, F32)
        dv_ref[...] = jnp.zeros(dv_ref.shape, F32)
        low = lax.broadcasted_iota(jnp.int32, (1, q_n, LANES), 2) < HEAD_DIM
        row16 = pl.multiple_of(16 * hp, 16)

        def query_rows(stat_ref, t_ref, q_starts):
            tiles = un.load(stat_ref, q_starts, rq)
            for b in range(bsz):
                t_ref[b] = tiles[b].T
            t16 = t_ref[:, pl.ds(row16, 16), :]
            return jnp.concatenate([t16[:, 0:1, :], t16[:, 8:9, :]], axis=2)

        def batch(q_starts, k_starts, k_rows, bias):
            qq = _stack_heads(un.load(q_ref, q_starts, rq) * 0.125, low)
            dd = _stack_heads(un.load(do_ref, q_starts, rq), low)
            ks = un.load(k_ref, k_starts, k_rows)
            vs = un.load(v_ref, k_starts, k_rows)
            lrow = query_rows(lse_ref, lt_ref, q_starts)
            drow = query_rows(ds_ref, dt_ref, q_starts)
            pt = jnp.exp(_bdot(ks, qq, _BNT) + bias - lrow)
            dst = pt * (_bdot(vs, dd, _BNT) - drow)
            un.store(dv_ref, k_starts, k_rows, _bdot(pt, dd, _BNN), add=True)
            un.store(dk_ref, k_starts, k_rows, _bdot(dst, qq, _BNN), add=True)
            dq = _bdot(jnp.swapaxes(dst, 1, 2), ks, _BNN)
            un.store(out_ref, q_starts, rq, jnp.where(low, dq[:, :q_n], dq[:, q_n:]) * 0.125, lead=(0,))

        if un.per_res:
            batch([0], [0], k0_rows, b0_ref[...][None])
        else:
            bias = jnp.concatenate([b0_ref[...][None]] + [b1_ref[...][None]] * (cnt - 1), axis=0)
            batch([q for q, _ in first], [k for _, k in first], 2 * rq, bias)

        def more(j, carry):
            n0 = j * cnt
            qs = [pl.multiple_of((n0 + i) * rq, rq) for i in range(cnt)]
            ks = [pl.multiple_of((n0 + i - 1) * rq, rq) for i in range(cnt)]
            batch(qs, ks, 2 * rq, b1_ref[...][None])
            return carry

        lax.fori_loop(1, 1 + n_more, more, 0)
        out_ref[1] = dk_ref[...].astype(out_ref.dtype)
        out_ref[2] = dv_ref[...].astype(out_ref.dtype)

    pv = _pattern_view(proj, g_n)
    full = lambda a: pl.BlockSpec(a.shape, lambda r, hp: (0, 0))
    out = pl.pallas_call(
        body, grid=_pattern_grid(g_n),
        in_specs=[pl.BlockSpec(memory_space=pltpu.SMEM),
                  _pattern_spec(g_n, lr, lambda hp: 32 + hp),
                  _pattern_spec(g_n, lr, lambda hp: 40 + hp),
                  _pattern_spec(g_n, lr, lambda hp: 48 + hp),
                  _pattern_spec(g_n, lr, lambda hp: hp),
                  _pattern_spec(g_n, lr, lambda hp: 0),
                  _pattern_spec(g_n, lr, lambda hp: 0),
                  full(d0), full(m0), full(d1), full(m1)],
        out_specs=_pattern_spec(g_n, lr, lambda hp: hp, lead=(3,)),
        out_shape=jax.ShapeDtypeStruct(_pattern_view_shape(s_n, D_MODEL, g_n, lead=(3,)), ACT_DTYPE),
        scratch_shapes=[pltpu.VMEM((d0.shape[0], 2 * q_n), F32), pltpu.VMEM((2 * q_n, 2 * q_n), F32),
                        pltpu.VMEM((bsz, LANES, q_n), F32), pltpu.VMEM((bsz, LANES, q_n), F32),
                        pltpu.VMEM((gd, lr, LANES), F32), pltpu.VMEM((gd, lr, LANES), F32)],
        name=f"attn_bwd_d{d}",
    )(slopes, pv, pv, pv, _pattern_view(dout, g_n), _pattern_view(lse, g_n), _pattern_view(dsum, g_n),
      d0, m0, d1, m1)
    return out.reshape(3, s_n, D_MODEL)


def _sum3(a, b, c):
    _, s_n, c_n = a.shape
    tm = 512

    def body(a_ref, b_ref, c_ref, o_ref):
        o_ref[...] = (a_ref[...] + b_ref[...] + c_ref[...]).astype(o_ref.dtype)

    spec = pl.BlockSpec((1, tm, c_n), lambda p, i: (p, i, 0))
    return pl.pallas_call(
        body, grid=(3, s_n // tm), in_specs=[spec] * 3, out_specs=spec,
        out_shape=jax.ShapeDtypeStruct(a.shape, ACT_DTYPE), name="sum_dqkv",
    )(a, b, c)


def _part_index(step, per, lo, n):
    return jnp.clip(step // per - lo, 0, n - 1)


def _dw_in(ut, da4, dc3, db3):
    s_n = ut.shape[1]
    tn = 256
    per = D_MODEL // tn
    shard_blocks = SHARD_COLS // tn

    def body(a_ref, p0_ref, p1_ref, p2_ref, o_ref):
        part = pl.program_id(0) // per

        @pl.when(part < 4)
        def _():
            o_ref[...] = _dot(a_ref[...], p0_ref[...])

        @pl.when((part >= 4) & (part < 7))
        def _():
            o_ref[...] = _dot(a_ref[...], p1_ref[...])

        @pl.when(part >= 7)
        def _():
            o_ref[...] = _dot(a_ref[...], p2_ref[...])

    def pspec(lo, n):
        return pl.BlockSpec((None, s_n, tn), lambda j: (_part_index(j, per, lo, n), 0, j % per))

    return pl.pallas_call(
        body, grid=(IN_COLS // tn,),
        in_specs=[pl.BlockSpec((D_MODEL, s_n), lambda j: (0, 0)), pspec(0, 4), pspec(4, 3), pspec(7, 3)],
        out_specs=pl.BlockSpec((None, D_MODEL, tn), lambda j: (j // shard_blocks, 0, j % shard_blocks)),
        out_shape=jax.ShapeDtypeStruct((4, D_MODEL, SHARD_COLS), F32),
        name="dw_in",
    )(ut, da4, dc3, db3)


def _input_grad(da4, dc3, db3, w4, xp, norm_g, dh2):
    s_n = xp.shape[0]
    tm, tk = 1024, 512
    per = D_MODEL // tk
    nk = IN_COLS // tk
    shard_blocks = SHARD_COLS // tk

    def body(p0_ref, p1_ref, p2_ref, w_ref, x_ref, g_ref, dh_ref, gx_ref, dg_ref, acc_ref):
        m_i, k_i = pl.program_id(0), pl.program_id(1)
        part = k_i // per

        @pl.when(k_i == 0)
        def _():
            acc_ref[...] = jnp.zeros(acc_ref.shape, F32)

        @pl.when((m_i == 0) & (k_i == 0))
        def _():
            dg_ref[...] = jnp.zeros(dg_ref.shape, F32)

        @pl.when(part < 4)
        def _():
            acc_ref[...] += _dot_nt(p0_ref[...], w_ref[...])

        @pl.when((part >= 4) & (part < 7))
        def _():
            acc_ref[...] += _dot_nt(p1_ref[...], w_ref[...])

        @pl.when(part >= 7)
        def _():
            acc_ref[...] += _dot_nt(p2_ref[...], w_ref[...])

        @pl.when(k_i == nk - 1)
        def _():
            du = acc_ref[...]
            x = x_ref[...]
            r = lax.rsqrt(jnp.mean(x * x, axis=-1, keepdims=True) + EPS)
            nrm = x * r
            dg_ref[...] += jnp.sum(du * nrm, axis=0, keepdims=True)
            dn = du * g_ref[...]
            gx_ref[...] = dh_ref[...] + r * (dn - nrm * jnp.mean(dn * nrm, axis=-1, keepdims=True))

    def pspec(lo, n):
        return pl.BlockSpec((None, tm, tk), lambda m, k: (_part_index(k, per, lo, n), m, k % per))

    row = pl.BlockSpec((tm, D_MODEL), lambda m, k: (m, 0))
    vec = pl.BlockSpec((1, D_MODEL), lambda m, k: (0, 0))
    return pl.pallas_call(
        body, grid=(s_n // tm, nk),
        in_specs=[pspec(0, 4), pspec(4, 3), pspec(7, 3),
                  pl.BlockSpec((None, D_MODEL, tk), lambda m, k: (k // shard_blocks, 0, k % shard_blocks)),
                  row, vec, row],
        out_specs=[row, vec],
        out_shape=[jax.ShapeDtypeStruct((s_n, D_MODEL), F32), jax.ShapeDtypeStruct((1, D_MODEL), F32)],
        scratch_shapes=[pltpu.VMEM((tm, D_MODEL), F32)],
        name="input_grad",
    )(da4, dc3, db3, w4, xp, norm_g, dh2)


def _local_grads(x, tgt, norm_g, w4, b_merge, conv_w, woc, woa, wo, final_g):
    slopes = _alibi_slopes()
    xp, tp = _to_residue_major(x, tgt)
    u, ut = _rms_in(xp, norm_g)
    proj = _in_proj(u, w4)
    hc, hct = _conv_fwd(proj, conv_w)
    fwd = [_attn_fwd(proj, slopes, d) for d in PATTERNS]
    o, lse, ha, hat = _attn_combine([f[0] for f in fwd], [f[1] for f in fwd], proj)
    yc, ya, mg, mgt, dh2, dh2b, dgf, loss8 = _merge_loss(hc, ha, woc, woa, wo, proj, b_merge, xp, final_g, tp)

    dyc, dya, dhc, dout, dsum, db3, dbias = _merge_bwd(dh2b, wo, woc, woa, yc, ya, proj, b_merge, o)
    d_wo = _mm_lhs_resident(mgt, dh2b, 256, "dw_o")
    d_woc = _mm_lhs_resident(hct, dyc, 256, "dw_out_conv")
    d_woa = _mm_lhs_resident(hat, dya, 256, "dw_out_attn")
    da4, d_conv_w = _conv_bwd(proj, conv_w, dhc)
    dc3 = _sum3(*[_attn_bwd(proj, dout, lse, dsum, slopes, d) for d in PATTERNS])
    d_w4 = _dw_in(ut, da4, dc3, db3)
    gxp, d_norm_g = _input_grad(da4, dc3, db3, w4, xp, norm_g, dh2)
    grad_x = _to_natural(gxp)
    return loss8, grad_x, d_norm_g, d_w4, dbias, d_conv_w, d_woc, d_woa, d_wo, dgf


MESH = pl.DeviceIdType.MESH
_CHIP_FLIPS = ((1, 0), (0, 1), (1, 1))
_ANY = pl.BlockSpec(memory_space=pl.ANY)


def _place():
    return lax.axis_index("x"), lax.axis_index("y"), lax.axis_index("c")


def _flip(v, f):
    return 1 - v if f else v


def _remote(src, dst, send_sems, recv_sems, k, device):
    return pltpu.make_async_remote_copy(src_ref=src, dst_ref=dst, send_sem=send_sems.at[k], recv_sem=recv_sems.at[k],
                                        device_id=device, device_id_type=MESH)


def _place_shard(w, chip):
    rows, cols = w.shape
    tm = 128

    def body(chip_ref, w_ref, o_ref):
        o_ref[0] = w_ref[...].astype(o_ref.dtype)

    return pl.pallas_call(
        body,
        grid_spec=pltpu.PrefetchScalarGridSpec(
            num_scalar_prefetch=1, grid=(rows // tm,),
            in_specs=[pl.BlockSpec((tm, cols), lambda i, chip_ref: (i, 0))],
            out_specs=pl.BlockSpec((1, tm, cols), lambda i, chip_ref: (chip_ref[0], i, 0))),
        out_shape=jax.ShapeDtypeStruct((4, rows, cols), MXU_DTYPE),
        name="place_shard",
    )(chip, w)


def _gather_weights(slots):
    n = len(slots)

    def body(*refs):
        outs = refs[n:2 * n]
        send_sems, recv_sems = refs[2 * n:]
        x, y, c = _place()
        chip = 2 * x + y
        sibling = (x, y, 1 - c)
        started = []
        for a in range(n):
            h = outs[a].shape[1] // 2
            mine = outs[a].at[chip, pl.ds(pl.multiple_of(c * h, 8), h)]
            for t, (fx, fy) in enumerate(_CHIP_FLIPS):
                cp = _remote(mine, mine, send_sems, recv_sems, 6 * a + t, (_flip(x, fx), _flip(y, fy), c))
                cp.start()
                started.append(cp)
        for a in range(n):
            h = outs[a].shape[1] // 2
            rows = pl.ds(pl.multiple_of(c * h, 8), h)
            for t, (fx, fy) in enumerate(_CHIP_FLIPS):
                landed = outs[a].at[2 * _flip(x, fx) + _flip(y, fy), rows]
                _remote(landed, landed, send_sems, recv_sems, 6 * a + t, sibling).wait_recv()
                cp = _remote(landed, landed, send_sems, recv_sems, 6 * a + 3 + t, sibling)
                cp.start()
                started.append(cp)
        for a in range(n):
            h = outs[a].shape[1] // 2
            rows = pl.ds(pl.multiple_of((1 - c) * h, 8), h)
            for t, (fx, fy) in enumerate(_CHIP_FLIPS):
                handed = outs[a].at[2 * _flip(x, fx) + _flip(y, fy), rows]
                _remote(handed, handed, send_sems, recv_sems, 6 * a + 3 + t, sibling).wait_recv()
        for cp in started:
            cp.wait_send()

    return pl.pallas_call(
        body, in_specs=[_ANY] * n, out_specs=[_ANY] * n,
        out_shape=[jax.ShapeDtypeStruct(s.shape, s.dtype) for s in slots],
        input_output_aliases={a: a for a in range(n)},
        scratch_shapes=[pltpu.SemaphoreType.DMA((6 * n,)), pltpu.SemaphoreType.DMA((6 * n,))],
        name="gather_weights",
    )(*slots)


def _sibling_exchange_halves(grads):
    n = len(grads)

    def body(*refs):
        ins, outs = refs[:n], refs[n:2 * n]
        send_sems, recv_sems = refs[2 * n:]
        x, y, c = _place()
        copies = []
        for a in range(n):
            h = ins[a].shape[1] // 2
            theirs = pl.ds(pl.multiple_of((1 - c) * h, 8), h)
            copies.append(_remote(ins[a].at[:, theirs], outs[a], send_sems, recv_sems, a, (x, y, 1 - c)))
        for cp in copies:
            cp.start()
        for cp in copies:
            cp.wait()

    return pl.pallas_call(
        body, in_specs=[_ANY] * n, out_specs=[_ANY] * n,
        out_shape=[jax.ShapeDtypeStruct((4, g.shape[1] // 2, g.shape[2]), g.dtype) for g in grads],
        scratch_shapes=[pltpu.SemaphoreType.DMA((n,)), pltpu.SemaphoreType.DMA((n,))],
        name="grads_to_sibling",
    )(*grads)


def _add_halves(g, r, half):
    _, rows, cols = g.shape
    h = rows // 2
    tm = min(h, 128)
    nt = h // tm

    def body(half_ref, g_ref, r_ref, f_ref, b_ref):
        s = g_ref[...] + r_ref[...]
        f_ref[...] = s
        b_ref[...] = s.astype(b_ref.dtype)

    spec = pl.BlockSpec((1, tm, cols), lambda j, i, half_ref: (j, i, 0))
    return pl.pallas_call(
        body,
        grid_spec=pltpu.PrefetchScalarGridSpec(
            num_scalar_prefetch=1, grid=(4, nt),
            in_specs=[pl.BlockSpec((1, tm, cols), lambda j, i, half_ref: (j, half_ref[0] * nt + i, 0)), spec],
            out_specs=[spec, spec]),
        out_shape=[jax.ShapeDtypeStruct((4, h, cols), F32), jax.ShapeDtypeStruct((4, h, cols), BF16)],
        name="add_sibling_grads",
    )(half, g, r)


def _chip_exchange(parts):
    n = len(parts)

    def body(*refs):
        ins, outs = refs[:n], refs[n:2 * n]
        send_sems, recv_sems = refs[2 * n:]
        x, y, c = _place()
        copies = []
        for a in range(n):
            for t, (fx, fy) in enumerate(_CHIP_FLIPS):
                tx, ty = _flip(x, fx), _flip(y, fy)
                copies.append(_remote(ins[a].at[2 * tx + ty], outs[a].at[t], send_sems, recv_sems, 3 * a + t,
                                      (tx, ty, c)))
        for cp in copies:
            cp.start()
        for cp in copies:
            cp.wait()

    return pl.pallas_call(
        body, in_specs=[_ANY] * n, out_specs=[_ANY] * n,
        out_shape=[jax.ShapeDtypeStruct((3, *p.shape[1:]), p.dtype) for p in parts],
        scratch_shapes=[pltpu.SemaphoreType.DMA((3 * n,)), pltpu.SemaphoreType.DMA((3 * n,))],
        name="grads_to_chips",
    )(*parts)


def _add_chips(own, recv, where):
    _, h, cols = own.shape
    tm = min(h, 128)
    nt = h // tm

    def body(where_ref, o_ref, r_ref, out_ref):
        out_ref[...] = ((o_ref[0] + r_ref[0].astype(F32)) + r_ref[1].astype(F32)) + r_ref[2].astype(F32)

    return pl.pallas_call(
        body,
        grid_spec=pltpu.PrefetchScalarGridSpec(
            num_scalar_prefetch=1, grid=(nt,),
            in_specs=[pl.BlockSpec((1, tm, cols), lambda i, where_ref: (where_ref[0], i, 0)),
                      pl.BlockSpec((3, tm, cols), lambda i, where_ref: (0, i, 0))],
            out_specs=pl.BlockSpec((tm, cols), lambda i, where_ref: (where_ref[1] * nt + i, 0))),
        out_shape=jax.ShapeDtypeStruct((2 * h, cols), F32),
        name="add_chip_grads",
    )(where, own, recv)


def _share_halves(shards):
    n = len(shards)

    def body(*refs):
        outs = refs[n:2 * n]
        send_sems, recv_sems = refs[2 * n:]
        x, y, c = _place()
        copies = []
        for a in range(n):
            h = outs[a].shape[0] // 2
            mine = outs[a].at[pl.ds(pl.multiple_of(c * h, 8), h)]
            copies.append(_remote(mine, mine, send_sems, recv_sems, a, (x, y, 1 - c)))
        for cp in copies:
            cp.start()
        for a, cp in enumerate(copies):
            cp.wait_send()
            h = outs[a].shape[0] // 2
            theirs = outs[a].at[pl.ds(pl.multiple_of((1 - c) * h, 8), h)]
            _remote(theirs, theirs, send_sems, recv_sems, a, (x, y, 1 - c)).wait_recv()

    return pl.pallas_call(
        body, in_specs=[_ANY] * n, out_specs=[_ANY] * n,
        out_shape=[jax.ShapeDtypeStruct(p.shape, p.dtype) for p in shards],
        input_output_aliases={a: a for a in range(n)},
        scratch_shapes=[pltpu.SemaphoreType.DMA((n,)), pltpu.SemaphoreType.DMA((n,))],
        name="share_reduced_halves",
    )(*shards)


def _exchange_small(rows, reduce):
    cols = rows[0].shape[1]
    n = len(rows)
    assert sum(r.shape[0] for r in rows) <= 8

    def body(*refs):
        ins, out_ref = refs[:n], refs[n]
        vec_ref, gath_ref, send_sems, recv_sems = refs[n + 1:]
        x, y, c = _place()
        me = 4 * x + 2 * y + c
        vec_ref[...] = jnp.zeros(vec_ref.shape, F32)
        at = 0
        for r in ins:
            vec_ref[at:at + r.shape[0], :] = r[...]
            at += r.shape[0]
        copies = []
        for k in range(1, 8):
            peer = (_flip(x, (k >> 2) & 1), _flip(y, (k >> 1) & 1), _flip(c, k & 1))
            copies.append(_remote(vec_ref, gath_ref.at[me], send_sems, recv_sems, k - 1, peer))
        for cp in copies:
            cp.start()
        gath_ref[me] = vec_ref[...]
        for cp in copies:
            cp.wait()
        if reduce:
            tot = gath_ref[0]
            for dev in range(1, 8):
                tot = tot + gath_ref[dev]
            out_ref[...] = tot
            out_ref[7:8, :] = jnp.zeros((1, cols), F32) + jnp.sum(tot[7:8, :])
        else:
            out_ref[...] = gath_ref[...]

    vm = pl.BlockSpec(memory_space=pltpu.VMEM)
    return pl.pallas_call(
        body, in_specs=[vm] * n, out_specs=vm,
        out_shape=jax.ShapeDtypeStruct((8, cols) if reduce else (8, 8, cols), F32),
        scratch_shapes=[pltpu.VMEM((8, cols), F32), pltpu.VMEM((8, 8, cols), F32),
                        pltpu.SemaphoreType.DMA((7,)), pltpu.SemaphoreType.DMA((7,))],
        name="reduce_small" if reduce else "gather_small",
    )(*rows)


def _adamw(w, g, m, v, name):
    rows, cols = w.shape
    tm = 128 if rows % 128 == 0 else rows

    def body(w_ref, g_ref, m_ref, v_ref, d_ref, m2_ref, v2_ref):
        gr = g_ref[...]
        m2 = ADAM_B1 * m_ref[...] + (1.0 - ADAM_B1) * gr
        v2 = ADAM_B2 * v_ref[...] + (1.0 - ADAM_B2) * (gr * gr)
        m_hat = m2 / (1.0 - ADAM_B1 ** ADAM_STEP)
        v_hat = v2 / (1.0 - ADAM_B2 ** ADAM_STEP)
        d_ref[...] = -ADAM_LR * (m_hat / (jnp.sqrt(v_hat) + ADAM_EPS) + ADAM_WD * w_ref[...])
        m2_ref[...] = m2
        v2_ref[...] = v2

    spec = pl.BlockSpec((tm, cols), lambda i: (i, 0))
    sds = jax.ShapeDtypeStruct((rows, cols), F32)
    return pl.pallas_call(body, grid=(rows // tm,), in_specs=[spec] * 4, out_specs=[spec] * 3,
                          out_shape=[sds] * 3, name=name)(w, g, m, v)


def kernel(x, norm_g, w_in, b_merge, conv_w, w_out_conv, w_out_attn, w_o, final_g, loss_target, m_norm_g, m_w_in, m_b_merge, m_conv_w, m_w_out_conv, m_w_out_attn, m_w_o, m_final_g, v_norm_g, v_w_in, v_b_merge, v_conv_w, v_w_out_conv, v_w_out_attn, v_w_o, v_final_g):
    mx, my, mc = _place()
    chip = (2 * mx + my).astype(jnp.int32)
    seq = x.shape[1]

    chip1 = chip.reshape(1)
    w4, woc4, woa4, wo4 = _gather_weights([_place_shard(w[0], chip1) for w in (w_in, w_out_conv, w_out_attn, w_o)])
    taps8 = _exchange_small([conv_w[0]], reduce=False)
    taps = jnp.concatenate([taps8[2 * j, :3, :] for j in range(4)], axis=1)

    loss8, grad_x, d_norm_g, d_w4, d_bias, d_taps, d_woc, d_woa, d_wo, d_final_g = _local_grads(
        x[0], loss_target[0], norm_g, w4, b_merge, taps, woc4.reshape(D_MODEL, D_MODEL),
        woa4.reshape(D_MODEL, D_MODEL), wo4.reshape(D_MODEL, D_MODEL), final_g.reshape(1, D_MODEL))

    grads = [d_w4, d_woc.reshape(4, -1, D_MODEL), d_woa.reshape(4, -1, D_MODEL), d_wo.reshape(4, -1, D_MODEL)]
    from_sibling = _sibling_exchange_halves(grads)
    half = mc.astype(jnp.int32).reshape(1)
    partial = [_add_halves(g, r, half) for g, r in zip(grads, from_sibling)]
    from_chips = _chip_exchange([p[1] for p in partial])
    where = jnp.stack([chip, mc.astype(jnp.int32)])
    reduced = [_add_chips(p[0], r, where) for p, r in zip(partial, from_chips)]
    g_w_in, g_woc, g_woa, g_wo = _share_halves(reduced)

    small = _exchange_small([d_norm_g, d_bias.reshape(2, D_MODEL), d_taps, d_final_g, loss8.reshape(1, D_MODEL)],
                            reduce=True)
    loss = (0.5 / D_MODEL) * small[7, 0]
    g_norm_g = small[0:1]
    g_bias = small[1:3].reshape(1, 2 * D_MODEL)
    g_taps = lax.dynamic_slice(small[3:6], (0, chip * (D_MODEL // 4)), (3, D_MODEL // 4))
    g_final_g = small[6:7]

    upd = [
        _adamw(norm_g, g_norm_g, m_norm_g, v_norm_g, "adamw_norm_g"),
        _adamw(w_in[0], g_w_in, m_w_in[0], v_w_in[0], "adamw_w_in"),
        _adamw(b_merge, g_bias, m_b_merge, v_b_merge, "adamw_b_merge"),
        _adamw(conv_w[0], g_taps, m_conv_w[0], v_conv_w[0], "adamw_conv_w"),
        _adamw(w_out_conv[0], g_woc, m_w_out_conv[0], v_w_out_conv[0], "adamw_w_out_conv"),
        _adamw(w_out_attn[0], g_woa, m_w_out_attn[0], v_w_out_attn[0], "adamw_w_out_attn"),
        _adamw(w_o[0], g_wo, m_w_o[0], v_w_o[0], "adamw_w_o"),
        _adamw(final_g.reshape(1, D_MODEL), g_final_g, m_final_g.reshape(1, D_MODEL),
               v_final_g.reshape(1, D_MODEL), "adamw_final_g"),
    ]
    shapes = [norm_g.shape, w_in.shape, b_merge.shape, conv_w.shape, w_out_conv.shape, w_out_attn.shape,
              w_o.shape, final_g.shape]
    grads_out = [g_norm_g, g_w_in, g_bias, g_taps, g_woc, g_woa, g_wo, g_final_g]
    outs = [loss, grad_x.reshape(1, seq, D_MODEL)]
    outs += [g.reshape(s) for g, s in zip(grads_out, shapes)]
    for k in range(3):
        outs += [u[k].reshape(s) for u, s in zip(upd, shapes)]
    return tuple(outs)
```

```python
import functools

import numpy as np
import jax
import jax.numpy as jnp
from jax import lax
from jax.experimental import pallas as pl
from jax.experimental.pallas import tpu as pltpu

F32 = jnp.float32
BF16 = jnp.bfloat16
MXU_DTYPE = jnp.bfloat16
ACT_DTYPE = jnp.bfloat16

D_MODEL = 1024
N_HEADS = 16
HEAD_DIM = 64
QB = 128
N_RES = 16
LANES = 128
HP = N_HEADS * HEAD_DIM // LANES
IN_COLS = 10 * D_MODEL
SHARD_COLS = IN_COLS // 4
EPS = 1e-6
NEG = -1e30

ADAM_LR, ADAM_B1, ADAM_B2, ADAM_EPS, ADAM_WD, ADAM_STEP = 0.001, 0.9, 0.999, 1e-08, 0.01, 10

PATTERNS = {1: (16, 16), 4: (4, 32), 16: (1, 128)}

_NN = (((1,), (0,)), ((), ()))
_NT = (((1,), (1,)), ((), ()))


def _dot(a, b):
    return lax.dot_general(a.astype(MXU_DTYPE), b.astype(MXU_DTYPE), _NN, preferred_element_type=F32)


def _dot_nt(a, b):
    return lax.dot_general(a.astype(MXU_DTYPE), b.astype(MXU_DTYPE), _NT, preferred_element_type=F32)


def _split3(x):
    hi = x.astype(BF16)
    r1 = x - hi.astype(F32)
    mid = r1.astype(BF16)
    lo = (r1 - mid.astype(F32)).astype(BF16)
    return hi, mid, lo


def _select_rows(sel, x):
    return sum(lax.dot_general(sel, t, _NN, preferred_element_type=F32) for t in _split3(x))


def _select_cols(x, sel):
    return sum(lax.dot_general(t, sel, _NN, preferred_element_type=F32) for t in _split3(x))


def _sigmoid(z):
    return 1.0 / (1.0 + jnp.exp(-z))


def _perm_matrix():
    idx = np.arange(256)
    p = np.zeros((256, 256), np.float32)
    p[(idx % 16) * 16 + idx // 16, idx] = 1.0
    return jnp.asarray(p, BF16)


def _head_expand_matrix():
    e = np.zeros((LANES, D_MODEL), np.float32)
    for h in range(N_HEADS):
        e[8 * h, HEAD_DIM * h:HEAD_DIM * (h + 1)] = 1.0
    return jnp.asarray(e, BF16)


def _head_sum_matrix():
    e = np.zeros((D_MODEL, LANES), np.float32)
    for h in range(N_HEADS):
        e[HEAD_DIM * h:HEAD_DIM * (h + 1), 8 * h:8 * (h + 1)] = 1.0
    return jnp.asarray(e, BF16)


def _attn_tables(d):
    g_n, rq = PATTERNS[d]
    q_n = g_n * rq
    gq, iq = np.arange(q_n) // rq, np.arange(q_n) % rq

    def tab(kn, base):
        k_n = g_n * kn
        gk, jk = np.arange(k_n) // kn, np.arange(k_n) % kn
        delta = g_n * (base + iq[:, None] - jk[None, :]) + gq[:, None] - gk[None, :]
        valid = (delta >= 0) & (delta <= QB)
        dist = np.where(valid, d * delta, 0).astype(np.float32)
        madd = np.where(valid, 0.0, NEG).astype(np.float32)
        return dist, madd

    d0, m0 = tab(rq if g_n == 1 else 2 * rq, 0)
    d1, m1 = tab(2 * rq, rq)
    return d0, m0, d1, m1


def _alibi_slopes():
    return jnp.exp2(-8.0 * jnp.arange(1, N_HEADS + 1, dtype=F32) / N_HEADS)


def _to_residue_major(x, tgt):
    s_n, c_n = x.shape
    lr = s_n // N_RES

    def body(p_ref, x_ref, t_ref, xo_ref, to_ref):
        pm = p_ref[...]
        xo_ref[...] = _select_rows(pm, x_ref[...]).reshape(16, 16, c_n)
        to_ref[...] = _select_rows(pm, t_ref[...]).reshape(16, 16, c_n)

    nat = pl.BlockSpec((256, c_n), lambda i: (i, 0))
    res = pl.BlockSpec((16, 16, c_n), lambda i: (0, i, 0))
    xo, to = pl.pallas_call(
        body, grid=(s_n // 256,),
        in_specs=[pl.BlockSpec((256, 256), lambda i: (0, 0)), nat, nat],
        out_specs=[res, res],
        out_shape=[jax.ShapeDtypeStruct((16, lr, c_n), F32)] * 2,
        name="perm_in",
    )(_perm_matrix(), x, tgt)
    return xo.reshape(s_n, c_n), to.reshape(s_n, c_n)


def _to_natural(gx_lo, gx_hi):
    half_rows, c_n = gx_lo.shape
    lr = half_rows // (N_RES // 2)

    def body(p_ref, lo_ref, hi_ref, o_ref):
        g = jnp.concatenate([lo_ref[...], hi_ref[...]], axis=0)
        o_ref[...] = _select_rows(p_ref[...], g.reshape(256, c_n))

    half = pl.BlockSpec((8, 16, c_n), lambda i: (0, i, 0))
    return pl.pallas_call(
        body, grid=(lr // 16,),
        in_specs=[pl.BlockSpec((256, 256), lambda i: (0, 0)), half, half],
        out_specs=pl.BlockSpec((256, c_n), lambda i: (i, 0)),
        out_shape=jax.ShapeDtypeStruct((2 * half_rows, c_n), F32),
        name="perm_out",
    )(_perm_matrix(), gx_lo.reshape(8, lr, c_n), gx_hi.reshape(8, lr, c_n))


def _rms_in(xp, norm_g):
    s_n, c_n = xp.shape
    tm = 512

    def body(x_ref, g_ref, u_ref, ut_ref):
        x = x_ref[...]
        r = lax.rsqrt(jnp.mean(x * x, axis=-1, keepdims=True) + EPS)
        u = x * r * g_ref[...]
        u_ref[...] = u.astype(u_ref.dtype)
        ut_ref[...] = u.T.astype(ut_ref.dtype)

    return pl.pallas_call(
        body, grid=(s_n // tm,),
        in_specs=[pl.BlockSpec((tm, c_n), lambda i: (i, 0)), pl.BlockSpec((1, c_n), lambda i: (0, 0))],
        out_specs=[pl.BlockSpec((tm, c_n), lambda i: (i, 0)), pl.BlockSpec((c_n, tm), lambda i: (0, i))],
        out_shape=[jax.ShapeDtypeStruct((s_n, c_n), ACT_DTYPE), jax.ShapeDtypeStruct((c_n, s_n), ACT_DTYPE)],
        name="rms_in",
    )(xp, norm_g)


def _in_proj(u, w4):
    s_n = u.shape[0]
    tn, cm = 512, 512
    per = SHARD_COLS // tn

    def body(a_ref, b_ref, o_ref):
        b = b_ref[...]
        for c in range(s_n // cm):
            o_ref[c * cm:(c + 1) * cm, :] = _dot(a_ref[c * cm:(c + 1) * cm, :], b).astype(o_ref.dtype)

    return pl.pallas_call(
        body, grid=(IN_COLS // tn,),
        in_specs=[pl.BlockSpec((s_n, D_MODEL), lambda n: (0, 0)),
                  pl.BlockSpec((None, D_MODEL, tn), lambda n: (n // per, 0, n % per))],
        out_specs=pl.BlockSpec((s_n, tn), lambda n: (0, n)),
        out_shape=jax.ShapeDtypeStruct((s_n, IN_COLS), ACT_DTYPE),
        name="in_proj",
    )(u, w4)


def _conv_terms(xc_ref, cg_ref, r, row, lr):
    def a_of(q):
        return cg_ref[q].astype(F32) * xc_ref[q].astype(F32)

    def shift_down(v):
        return jnp.where(row >= 1, pltpu.roll(v, 1, 0), 0.0)

    a = a_of(r)
    am1 = a_of(r - 1) if r >= 1 else shift_down(a_of(N_RES - 1))
    am2 = a_of(r - 2) if r >= 2 else shift_down(a_of(N_RES - 2 + r))
    return a, am1, am2


def _conv_fwd(proj, conv_w):
    s_n = proj.shape[0]
    lr = s_n // N_RES
    pv = proj.reshape(N_RES, lr, IN_COLS)

    def body(xc_ref, bg_ref, cg_ref, zc_ref, w_ref, hc_ref, hct_ref):
        w = w_ref[...]
        row = lax.broadcasted_iota(jnp.int32, (lr, LANES), 0)
        for r in range(N_RES):
            a, am1, am2 = _conv_terms(xc_ref, cg_ref, r, row, lr)
            c = w[0:1] * am2 + w[1:2] * am1 + w[2:3] * a
            z = zc_ref[r].astype(F32)
            hc = z * _sigmoid(z) * bg_ref[r].astype(F32) * c
            hc_ref[r] = hc.astype(hc_ref.dtype)
            hct_ref[:, r * lr:(r + 1) * lr] = hc.T.astype(hct_ref.dtype)

    def col(part):
        return pl.BlockSpec((N_RES, lr, LANES), lambda j: (0, 0, part * 8 + j))

    hc, hct = pl.pallas_call(
        body, grid=(D_MODEL // LANES,),
        in_specs=[col(0), col(1), col(2), col(3), pl.BlockSpec((3, LANES), lambda j: (0, j))],
        out_specs=[pl.BlockSpec((N_RES, lr, LANES), lambda j: (0, 0, j)),
                   pl.BlockSpec((LANES, s_n), lambda j: (j, 0))],
        out_shape=[jax.ShapeDtypeStruct((N_RES, lr, D_MODEL), ACT_DTYPE),
                   jax.ShapeDtypeStruct((D_MODEL, s_n), ACT_DTYPE)],
        name="conv_fwd",
    )(pv, pv, pv, pv, conv_w)
    return hc.reshape(s_n, D_MODEL), hct


RES_PER_STEP = 8
FWD_BATCH = {1: 4, 4: 8, 16: RES_PER_STEP}
BWD_BATCH = {1: 2, 4: 8, 16: RES_PER_STEP}

_BNT = (((2,), (2,)), ((0,), (0,)))
_BNN = (((2,), (1,)), ((0,), (0,)))


def _bdot(a, b, dims):
    return lax.dot_general(a.astype(MXU_DTYPE), b.astype(MXU_DTYPE), dims, preferred_element_type=F32)


def _pattern_view_shape(s_n, c_n, g_n, lead=()):
    lr = s_n // N_RES
    return (*lead, 4, 4, lr, c_n) if g_n == 4 else (*lead, N_RES, lr, c_n)


def _pattern_view(a, g_n, lead=()):
    return a.reshape(_pattern_view_shape(a.shape[-2], a.shape[-1], g_n, lead))


def _pattern_grid(g_n):
    return (N_RES // RES_PER_STEP if g_n == 1 else N_RES // g_n, HP)


def _pattern_spec(g_n, lr, col_of_hp, lead=()):
    z = (0,) * len(lead)
    if g_n == 16:
        return pl.BlockSpec((*lead, 16, lr, LANES), lambda r, hp: (*z, 0, 0, col_of_hp(hp)))
    if g_n == 4:
        return pl.BlockSpec((*lead, 4, None, lr, LANES), lambda r, hp: (*z, 0, r, 0, col_of_hp(hp)))
    return pl.BlockSpec((*lead, RES_PER_STEP, lr, LANES), lambda r, hp: (*z, r, 0, col_of_hp(hp)))


class _Units:
    def __init__(self, g_n, rq):
        self.g_n, self.rq, self.per_res = g_n, rq, g_n == 1

    def load(self, ref, starts, rows):
        if self.per_res:
            return ref[:, pl.ds(starts[0], rows), :]
        return jnp.stack([ref[:, pl.ds(s, rows), :].reshape(self.g_n * rows, LANES) for s in starts])

    def store(self, ref, starts, rows, val, add=False, lead=()):
        for b, s in enumerate(starts):
            idx = (*lead, slice(None), pl.ds(s, rows), slice(None))
            v = val if self.per_res else val[b].reshape(self.g_n, rows, LANES)
            ref[idx] = (ref[idx] + v if add else v).astype(ref.dtype)
            if self.per_res:
                break


def _stack_heads(x, low):
    zero = jnp.zeros_like(x)
    return jnp.concatenate([jnp.where(low, x, zero), jnp.where(low, zero, x)], axis=1)


def _batches(nb, rq, size, per_res):
    if per_res:
        return [(0, 0)], nb - 1
    first = [(n * rq, max(n - 1, 0) * rq) for n in range(min(size, nb))]
    assert nb % len(first) == 0
    return first, nb // len(first) - 1


def _attn_fwd(proj, slopes, d):
    g_n, rq = PATTERNS[d]
    un = _Units(g_n, rq)
    s_n = proj.shape[0]
    lr = s_n // N_RES
    nb = lr // rq
    q_n = g_n * rq
    d0, m0, d1, m1 = _attn_tables(d)
    k0_rows = d0.shape[1] // g_n
    first, n_more = _batches(nb, rq, FWD_BATCH[d], un.per_res)
    bsz = len(first)

    def body(sl_ref, q_ref, k_ref, v_ref, d0_ref, m0_ref, d1_ref, m1_ref, o_ref, lse_ref, b0_ref, b1_ref):
        hp = pl.program_id(1)

        @pl.when(hp == 0)
        def _():
            lse_ref[...] = jnp.zeros(lse_ref.shape, F32)

        for h in (0, 1):
            slope = sl_ref[2 * hp + h]
            b0_ref[h * q_n:(h + 1) * q_n, :] = m0_ref[...] - slope * d0_ref[...]
            b1_ref[h * q_n:(h + 1) * q_n, :] = m1_ref[...] - slope * d1_ref[...]

        lane = lax.broadcasted_iota(jnp.int32, (1, q_n, LANES), 2)
        low = lane < HEAD_DIM
        grp = lane // 8

        def batch(q_starts, k_starts, k_rows, bias):
            qq = _stack_heads(un.load(q_ref, q_starts, rq) * 0.125, low)
            s = _bdot(qq, un.load(k_ref, k_starts, k_rows), _BNT) + bias
            m = jnp.max(s, axis=2, keepdims=True)
            p = jnp.exp(s - m)
            l = jnp.sum(p, axis=2, keepdims=True)
            o = _bdot(p, un.load(v_ref, k_starts, k_rows), _BNN) * (1.0 / l)
            lse = m + jnp.log(l)
            un.store(o_ref, q_starts, rq, jnp.where(low, o[:, :q_n], o[:, q_n:]))
            upd = jnp.where(grp == 2 * hp, lse[:, :q_n], 0.0) + jnp.where(grp == 2 * hp + 1, lse[:, q_n:], 0.0)
            un.store(lse_ref, q_starts, rq, upd, add=True)

        if un.per_res:
            batch([0], [0], k0_rows, b0_ref[...][None])
        else:
            bias = jnp.concatenate([b0_ref[...][None]] + [b1_ref[...][None]] * (bsz - 1), axis=0)
            batch([q for q, _ in first], [k for _, k in first], 2 * rq, bias)

        def more(j, carry):
            n0 = j * bsz
            qs = [pl.multiple_of((n0 + i) * rq, rq) for i in range(bsz)]
            ks = [pl.multiple_of((n0 + i - 1) * rq, rq) for i in range(bsz)]
            batch(qs, ks, 2 * rq, b1_ref[...][None])
            return carry

        lax.fori_loop(1, 1 + n_more, more, 0)

    pv = _pattern_view(proj, g_n)
    full = lambda a: pl.BlockSpec(a.shape, lambda r, hp: (0, 0))
    o, lse = pl.pallas_call(
        body, grid=_pattern_grid(g_n),
        in_specs=[pl.BlockSpec(memory_space=pltpu.SMEM),
                  _pattern_spec(g_n, lr, lambda hp: 32 + hp),
                  _pattern_spec(g_n, lr, lambda hp: 40 + hp),
                  _pattern_spec(g_n, lr, lambda hp: 48 + hp),
                  full(d0), full(m0), full(d1), full(m1)],
        out_specs=[_pattern_spec(g_n, lr, lambda hp: hp), _pattern_spec(g_n, lr, lambda hp: 0)],
        out_shape=[jax.ShapeDtypeStruct(_pattern_view_shape(s_n, D_MODEL, g_n), F32),
                   jax.ShapeDtypeStruct(_pattern_view_shape(s_n, LANES, g_n), F32)],
        scratch_shapes=[pltpu.VMEM((2 * q_n, d0.shape[1]), F32), pltpu.VMEM((2 * q_n, 2 * q_n), F32)],
        name=f"attn_fwd_d{d}",
    )(slopes, pv, pv, pv, d0, m0, d1, m1)
    return o.reshape(s_n, D_MODEL), lse.reshape(s_n, LANES)


def _attn_combine(outs, lses, proj):
    s_n = proj.shape[0]
    tm = 512

    def body(o1_ref, o2_ref, o3_ref, l1_ref, l2_ref, l3_ref, za_ref, e_ref, o_ref, lse_ref, ha_ref, hat_ref):
        ls = [l1_ref[...], l2_ref[...], l3_ref[...]]
        mx = jnp.maximum(jnp.maximum(ls[0], ls[1]), ls[2])
        den = sum(jnp.exp(l - mx) for l in ls)
        lse = mx + jnp.log(den)
        lse_ref[...] = lse
        o = jnp.zeros((tm, D_MODEL), F32)
        for l, oref in zip(ls, (o1_ref, o2_ref, o3_ref)):
            o = o + _select_cols(jnp.exp(l - lse), e_ref[...]) * oref[...]
        o_ref[...] = o
        z = za_ref[...].astype(F32)
        ha = z * _sigmoid(z) * o
        ha_ref[...] = ha.astype(ha_ref.dtype)
        hat_ref[...] = ha.T.astype(hat_ref.dtype)

    row = lambda w: pl.BlockSpec((tm, w), lambda i: (i, 0))
    return pl.pallas_call(
        body, grid=(s_n // tm,),
        in_specs=[row(D_MODEL)] * 3 + [row(LANES)] * 3
        + [pl.BlockSpec((tm, D_MODEL), lambda i: (i, 7)), pl.BlockSpec((LANES, D_MODEL), lambda i: (0, 0))],
        out_specs=[row(D_MODEL), row(LANES), row(D_MODEL), pl.BlockSpec((D_MODEL, tm), lambda i: (0, i))],
        out_shape=[jax.ShapeDtypeStruct((s_n, D_MODEL), F32), jax.ShapeDtypeStruct((s_n, LANES), F32),
                   jax.ShapeDtypeStruct((s_n, D_MODEL), ACT_DTYPE), jax.ShapeDtypeStruct((D_MODEL, s_n), ACT_DTYPE)],
        name="attn_combine",
    )(*outs, *lses, proj, _head_expand_matrix())


def _gates(gc_ref, ga_ref, b_ref):
    b = b_ref[...]
    gc = _sigmoid(gc_ref[...].astype(F32) + b[:, :D_MODEL])
    ga = _sigmoid(ga_ref[...].astype(F32) + b[:, D_MODEL:])
    return gc, ga


def _merge_loss(hc, ha, woc, woa, wo, proj, b_merge, xp, final_g, tgt):
    s_n = xp.shape[0]
    tm = 512

    def body(hc_ref, ha_ref, woc_ref, woa_ref, wo_ref, gc_ref, ga_ref, b_ref, x_ref, gf_ref, t_ref,
             yc_ref, ya_ref, mg_ref, mgt_ref, dh_ref, dhb_ref, dgf_ref, loss_ref):
        i = pl.program_id(0)

        @pl.when(i == 0)
        def _():
            dgf_ref[...] = jnp.zeros(dgf_ref.shape, F32)
            loss_ref[...] = jnp.zeros(loss_ref.shape, F32)

        yc = _dot(hc_ref[...], woc_ref[...])
        ya = _dot(ha_ref[...], woa_ref[...])
        gc, ga = _gates(gc_ref, ga_ref, b_ref)
        mg = gc * yc + ga * ya
        yc_ref[...] = yc.astype(yc_ref.dtype)
        ya_ref[...] = ya.astype(ya_ref.dtype)
        mg_ref[...] = mg.astype(mg_ref.dtype)
        mgt_ref[...] = mg.T.astype(mgt_ref.dtype)
        h2 = x_ref[...] + _dot(mg, wo_ref[...])
        r2 = lax.rsqrt(jnp.mean(h2 * h2, axis=-1, keepdims=True) + EPS)
        nrm = h2 * r2
        gf = gf_ref[...]
        err = nrm * gf - t_ref[...]
        e2 = (err * err).reshape(tm // 8, 8, D_MODEL).sum(axis=0)
        loss_ref[...] += sum(e2[:, c * LANES:(c + 1) * LANES] for c in range(D_MODEL // LANES))
        dy = err * (1.0 / D_MODEL)
        dgf_ref[...] += jnp.sum(dy * nrm, axis=0, keepdims=True)
        dn = dy * gf
        dh2 = r2 * (dn - nrm * jnp.mean(dn * nrm, axis=-1, keepdims=True))
        dh_ref[...] = dh2
        dhb_ref[...] = dh2.astype(dhb_ref.dtype)

    row = pl.BlockSpec((tm, D_MODEL), lambda i: (i, 0))
    wsp = pl.BlockSpec((D_MODEL, D_MODEL), lambda i: (0, 0))
    vec = lambda w: pl.BlockSpec((1, w), lambda i: (0, 0))
    act = jax.ShapeDtypeStruct((s_n, D_MODEL), ACT_DTYPE)
    return pl.pallas_call(
        body, grid=(s_n // tm,),
        in_specs=[row, row, wsp, wsp, wsp,
                  pl.BlockSpec((tm, D_MODEL), lambda i: (i, 8)), pl.BlockSpec((tm, D_MODEL), lambda i: (i, 9)),
                  vec(2 * D_MODEL), row, vec(D_MODEL), row],
        out_specs=[row, row, row, pl.BlockSpec((D_MODEL, tm), lambda i: (0, i)), row, row,
                   vec(D_MODEL), pl.BlockSpec((8, LANES), lambda i: (0, 0))],
        out_shape=[act, act, act, jax.ShapeDtypeStruct((D_MODEL, s_n), ACT_DTYPE),
                   jax.ShapeDtypeStruct((s_n, D_MODEL), F32), act,
                   jax.ShapeDtypeStruct((1, D_MODEL), F32), jax.ShapeDtypeStruct((8, LANES), F32)],
        name="merge_loss",
    )(hc, ha, woc, woa, wo, proj, proj, b_merge, xp, final_g, tgt)


def _merge_bwd(dh2b, wo, woc, woa, yc, ya, proj, b_merge, o):
    s_n = dh2b.shape[0]
    tm = 512

    def body(dh_ref, wo_ref, woc_ref, woa_ref, yc_ref, ya_ref, gc_ref, ga_ref, b_ref, o_ref, za_ref, e_ref,
             dyc_ref, dya_ref, dhc_ref, do_ref, dsum_ref, db3_ref, dbias_ref):
        i = pl.program_id(0)

        @pl.when(i == 0)
        def _():
            dbias_ref[...] = jnp.zeros(dbias_ref.shape, F32)

        dmg = _dot_nt(dh_ref[...], wo_ref[...])
        gc, ga = _gates(gc_ref, ga_ref, b_ref)
        dgc = dmg * yc_ref[...].astype(F32) * gc * (1.0 - gc)
        dga = dmg * ya_ref[...].astype(F32) * ga * (1.0 - ga)
        dbias_ref[:, :D_MODEL] += jnp.sum(dgc, axis=0, keepdims=True)
        dbias_ref[:, D_MODEL:] += jnp.sum(dga, axis=0, keepdims=True)
        dyc = dmg * gc
        dya = dmg * ga
        dyc_ref[...] = dyc.astype(dyc_ref.dtype)
        dya_ref[...] = dya.astype(dya_ref.dtype)
        dhc_ref[...] = _dot_nt(dyc, woc_ref[...]).astype(dhc_ref.dtype)
        dha = _dot_nt(dya, woa_ref[...])
        z = za_ref[...].astype(F32)
        sg = _sigmoid(z)
        ov = o_ref[...]
        dout = dha * z * sg
        do_ref[...] = dout.astype(do_ref.dtype)
        dsum_ref[...] = _select_cols(dout * ov, e_ref[...])
        db3_ref[0] = (dha * ov * sg * (1.0 + z * (1.0 - sg))).astype(db3_ref.dtype)
        db3_ref[1] = dgc.astype(db3_ref.dtype)
        db3_ref[2] = dga.astype(db3_ref.dtype)

    row = pl.BlockSpec((tm, D_MODEL), lambda i: (i, 0))
    wsp = pl.BlockSpec((D_MODEL, D_MODEL), lambda i: (0, 0))
    act = jax.ShapeDtypeStruct((s_n, D_MODEL), ACT_DTYPE)
    return pl.pallas_call(
        body, grid=(s_n // tm,),
        in_specs=[row, wsp, wsp, wsp, row, row,
                  pl.BlockSpec((tm, D_MODEL), lambda i: (i, 8)), pl.BlockSpec((tm, D_MODEL), lambda i: (i, 9)),
                  pl.BlockSpec((1, 2 * D_MODEL), lambda i: (0, 0)), row,
                  pl.BlockSpec((tm, D_MODEL), lambda i: (i, 7)), pl.BlockSpec((D_MODEL, LANES), lambda i: (0, 0))],
        out_specs=[row, row, row, row, pl.BlockSpec((tm, LANES), lambda i: (i, 0)),
                   pl.BlockSpec((3, tm, D_MODEL), lambda i: (0, i, 0)),
                   pl.BlockSpec((1, 2 * D_MODEL), lambda i: (0, 0))],
        out_shape=[act, act, act, act, jax.ShapeDtypeStruct((s_n, LANES), F32),
                   jax.ShapeDtypeStruct((3, s_n, D_MODEL), ACT_DTYPE),
                   jax.ShapeDtypeStruct((1, 2 * D_MODEL), F32)],
        name="merge_bwd",
    )(dh2b, wo, woc, woa, yc, ya, proj, proj, b_merge, o, proj, _head_sum_matrix())


def _mm_lhs_resident(a, b, tn, name):
    m_n, k_n = a.shape
    n_n = b.shape[1]

    def body(a_ref, b_ref, o_ref):
        o_ref[...] = _dot(a_ref[...], b_ref[...])

    return pl.pallas_call(
        body, grid=(n_n // tn,),
        in_specs=[pl.BlockSpec((m_n, k_n), lambda n: (0, 0)), pl.BlockSpec((k_n, tn), lambda n: (0, n))],
        out_specs=pl.BlockSpec((m_n, tn), lambda n: (0, n)),
        out_shape=jax.ShapeDtypeStruct((m_n, n_n), F32),
        name=name,
    )(a, b)


def _conv_bwd(proj, conv_w, dhc):
    s_n = proj.shape[0]
    lr = s_n // N_RES
    pv = proj.reshape(N_RES, lr, IN_COLS)

    def body(xc_ref, bg_ref, cg_ref, zc_ref, w_ref, dhc_ref, da4_ref, dw_ref, dc_ref):
        w = w_ref[...]
        row = lax.broadcasted_iota(jnp.int32, (lr, LANES), 0)
        dw = [jnp.zeros((1, LANES), F32) for _ in range(3)]
        for r in range(N_RES):
            a, am1, am2 = _conv_terms(xc_ref, cg_ref, r, row, lr)
            c = w[0:1] * am2 + w[1:2] * am1 + w[2:3] * a
            z = zc_ref[r].astype(F32)
            sg = _sigmoid(z)
            sz = z * sg
            bg = bg_ref[r].astype(F32)
            dh = dhc_ref[r].astype(F32)
            da4_ref[1, r] = (dh * sz * c).astype(da4_ref.dtype)
            da4_ref[3, r] = (dh * bg * c * sg * (1.0 + z * (1.0 - sg))).astype(da4_ref.dtype)
            dc = dh * sz * bg
            dc_ref[r] = dc
            dw[0] = dw[0] + jnp.sum(dc * am2, axis=0, keepdims=True)
            dw[1] = dw[1] + jnp.sum(dc * am1, axis=0, keepdims=True)
            dw[2] = dw[2] + jnp.sum(dc * a, axis=0, keepdims=True)
        dw_ref[0:1, :] = dw[0]
        dw_ref[1:2, :] = dw[1]
        dw_ref[2:3, :] = dw[2]

        def shift_up(v):
            return jnp.where(row < lr - 1, pltpu.roll(v, lr - 1, 0), 0.0)

        for r in range(N_RES):
            dp1 = dc_ref[r + 1] if r + 1 < N_RES else shift_up(dc_ref[0])
            dp2 = dc_ref[r + 2] if r + 2 < N_RES else shift_up(dc_ref[r + 2 - N_RES])
            da = w[2:3] * dc_ref[r] + w[1:2] * dp1 + w[0:1] * dp2
            da4_ref[0, r] = (da * cg_ref[r].astype(F32)).astype(da4_ref.dtype)
            da4_ref[2, r] = (da * xc_ref[r].astype(F32)).astype(da4_ref.dtype)

    def col(part):
        return pl.BlockSpec((N_RES, lr, LANES), lambda j: (0, 0, part * 8 + j))

    da4, dw = pl.pallas_call(
        body, grid=(D_MODEL // LANES,),
        in_specs=[col(0), col(1), col(2), col(3), pl.BlockSpec((3, LANES), lambda j: (0, j)),
                  pl.BlockSpec((N_RES, lr, LANES), lambda j: (0, 0, j))],
        out_specs=[pl.BlockSpec((4, N_RES, lr, LANES), lambda j: (0, 0, 0, j)),
                   pl.BlockSpec((3, LANES), lambda j: (0, j))],
        out_shape=[jax.ShapeDtypeStruct((4, N_RES, lr, D_MODEL), ACT_DTYPE),
                   jax.ShapeDtypeStruct((3, D_MODEL), F32)],
        scratch_shapes=[pltpu.VMEM((N_RES, lr, LANES), F32)],
        name="conv_bwd",
    )(pv, pv, pv, pv, conv_w, dhc.reshape(N_RES, lr, D_MODEL))
    return da4.reshape(4, s_n, D_MODEL), dw


def _attn_bwd(proj, dout, lse, dsum, slopes, d):
    g_n, rq = PATTERNS[d]
    un = _Units(g_n, rq)
    s_n = proj.shape[0]
    lr = s_n // N_RES
    nb = lr // rq
    q_n = g_n * rq
    d0, m0, d1, m1 = (np.ascontiguousarray(t.T) for t in _attn_tables(d))
    k0_rows = d0.shape[0] // g_n
    first, n_more = _batches(nb, rq, BWD_BATCH[d], un.per_res)
    cnt = len(first)
    bsz = RES_PER_STEP if un.per_res else cnt
    gd = RES_PER_STEP if un.per_res else g_n

    def body(sl_ref, q_ref, k_ref, v_ref, do_ref, lse_ref, ds_ref, d0_ref, m0_ref, d1_ref, m1_ref, out_ref,
             b0_ref, b1_ref, lt_ref, dt_ref, dk_ref, dv_ref):
        hp = pl.program_id(1)
        for h in (0, 1):
            slope = sl_ref[2 * hp + h]
            b0_ref[:, h * q_n:(h + 1) * q_n] = m0_ref[...] - slope * d0_ref[...]
            b1_ref[:, h * q_n:(h + 1) * q_n] = m1_ref[...] - slope * d1_ref[...]
        dk_ref[...] = jnp.zeros(dk_ref.shape, F32)
        dv_ref[...] = jnp.zeros(dv_ref.shape, F32)
        low = lax.broadcasted_iota(jnp.int32, (1, q_n, LANES), 2) < HEAD_DIM
        row16 = pl.multiple_of(16 * hp, 16)

        def query_rows(stat_ref, t_ref, q_starts):
            tiles = un.load(stat_ref, q_starts, rq)
            for b in range(bsz):
                t_ref[b] = tiles[b].T
            t16 = t_ref[:, pl.ds(row16, 16), :]
            return jnp.concatenate([t16[:, 0:1, :], t16[:, 8:9, :]], axis=2)

        def batch(q_starts, k_starts, k_rows, bias):
            qq = _stack_heads(un.load(q_ref, q_starts, rq) * 0.125, low)
            dd = _stack_heads(un.load(do_ref, q_starts, rq), low)
            ks = un.load(k_ref, k_starts, k_rows)
            vs = un.load(v_ref, k_starts, k_rows)
            lrow = query_rows(lse_ref, lt_ref, q_starts)
            drow = query_rows(ds_ref, dt_ref, q_starts)
            pt = jnp.exp(_bdot(ks, qq, _BNT) + bias - lrow)
            dst = pt * (_bdot(vs, dd, _BNT) - drow)
            un.store(dv_ref, k_starts, k_rows, _bdot(pt, dd, _BNN), add=True)
            un.store(dk_ref, k_starts, k_rows, _bdot(dst, qq, _BNN), add=True)
            dq = _bdot(jnp.swapaxes(dst, 1, 2), ks, _BNN)
            un.store(out_ref, q_starts, rq, jnp.where(low, dq[:, :q_n], dq[:, q_n:]) * 0.125, lead=(0,))

        if un.per_res:
            batch([0], [0], k0_rows, b0_ref[...][None])
        else:
            bias = jnp.concatenate([b0_ref[...][None]] + [b1_ref[...][None]] * (cnt - 1), axis=0)
            batch([q for q, _ in first], [k for _, k in first], 2 * rq, bias)

        def more(j, carry):
            n0 = j * cnt
            qs = [pl.multiple_of((n0 + i) * rq, rq) for i in range(cnt)]
            ks = [pl.multiple_of((n0 + i - 1) * rq, rq) for i in range(cnt)]
            batch(qs, ks, 2 * rq, b1_ref[...][None])
            return carry

        lax.fori_loop(1, 1 + n_more, more, 0)
        out_ref[1] = dk_ref[...].astype(out_ref.dtype)
        out_ref[2] = dv_ref[...].astype(out_ref.dtype)

    pv = _pattern_view(proj, g_n)
    full = lambda a: pl.BlockSpec(a.shape, lambda r, hp: (0, 0))
    out = pl.pallas_call(
        body, grid=_pattern_grid(g_n),
        in_specs=[pl.BlockSpec(memory_space=pltpu.SMEM),
                  _pattern_spec(g_n, lr, lambda hp: 32 + hp),
                  _pattern_spec(g_n, lr, lambda hp: 40 + hp),
                  _pattern_spec(g_n, lr, lambda hp: 48 + hp),
                  _pattern_spec(g_n, lr, lambda hp: hp),
                  _pattern_spec(g_n, lr, lambda hp: 0),
                  _pattern_spec(g_n, lr, lambda hp: 0),
                  full(d0), full(m0), full(d1), full(m1)],
        out_specs=_pattern_spec(g_n, lr, lambda hp: hp, lead=(3,)),
        out_shape=jax.ShapeDtypeStruct(_pattern_view_shape(s_n, D_MODEL, g_n, lead=(3,)), ACT_DTYPE),
        scratch_shapes=[pltpu.VMEM((d0.shape[0], 2 * q_n), F32), pltpu.VMEM((2 * q_n, 2 * q_n), F32),
                        pltpu.VMEM((bsz, LANES, q_n), F32), pltpu.VMEM((bsz, LANES, q_n), F32),
                        pltpu.VMEM((gd, lr, LANES), F32), pltpu.VMEM((gd, lr, LANES), F32)],
        name=f"attn_bwd_d{d}",
    )(slopes, pv, pv, pv, _pattern_view(dout, g_n), _pattern_view(lse, g_n), _pattern_view(dsum, g_n),
      d0, m0, d1, m1)
    return out.reshape(3, s_n, D_MODEL)


def _sum3(a, b, c):
    _, s_n, c_n = a.shape
    tm = 512

    def body(a_ref, b_ref, c_ref, o_ref):
        o_ref[...] = (a_ref[...] + b_ref[...] + c_ref[...]).astype(o_ref.dtype)

    spec = pl.BlockSpec((1, tm, c_n), lambda p, i: (p, i, 0))
    return pl.pallas_call(
        body, grid=(3, s_n // tm), in_specs=[spec] * 3, out_specs=spec,
        out_shape=jax.ShapeDtypeStruct(a.shape, ACT_DTYPE), name="sum_dqkv",
    )(a, b, c)


def _part_index(step, per, lo, n):
    return jnp.clip(step // per - lo, 0, n - 1)


def _dw_in(ut, da4, dc3, db3):
    s_n = ut.shape[1]
    tn = 256
    per = D_MODEL // tn
    shard_blocks = SHARD_COLS // tn

    def body(a_ref, p0_ref, p1_ref, p2_ref, o_ref):
        part = pl.program_id(0) // per

        @pl.when(part < 4)
        def _():
            o_ref[...] = _dot(a_ref[...], p0_ref[...])

        @pl.when((part >= 4) & (part < 7))
        def _():
            o_ref[...] = _dot(a_ref[...], p1_ref[...])

        @pl.when(part >= 7)
        def _():
            o_ref[...] = _dot(a_ref[...], p2_ref[...])

    def pspec(lo, n):
        return pl.BlockSpec((None, s_n, tn), lambda j: (_part_index(j, per, lo, n), 0, j % per))

    return pl.pallas_call(
        body, grid=(IN_COLS // tn,),
        in_specs=[pl.BlockSpec((D_MODEL, s_n), lambda j: (0, 0)), pspec(0, 4), pspec(4, 3), pspec(7, 3)],
        out_specs=pl.BlockSpec((None, D_MODEL, tn), lambda j: (j // shard_blocks, 0, j % shard_blocks)),
        out_shape=jax.ShapeDtypeStruct((4, D_MODEL, SHARD_COLS), F32),
        name="dw_in",
    )(ut, da4, dc3, db3)


def _input_grad(da4, dc3, db3, w4, xp, norm_g, dh2, row0, rows):
    tm, tk = 1024, 512
    per = D_MODEL // tk
    nk = IN_COLS // tk
    shard_blocks = SHARD_COLS // tk
    m0 = row0 // tm

    def body(p0_ref, p1_ref, p2_ref, w_ref, x_ref, g_ref, dh_ref, gx_ref, dg_ref, acc_ref):
        m_i, k_i = pl.program_id(0), pl.program_id(1)
        part = k_i // per

        @pl.when(k_i == 0)
        def _():
            acc_ref[...] = jnp.zeros(acc_ref.shape, F32)

        @pl.when((m_i == 0) & (k_i == 0))
        def _():
            dg_ref[...] = jnp.zeros(dg_ref.shape, F32)

        @pl.when(part < 4)
        def _():
            acc_ref[...] += _dot_nt(p0_ref[...], w_ref[...])

        @pl.when((part >= 4) & (part < 7))
        def _():
            acc_ref[...] += _dot_nt(p1_ref[...], w_ref[...])

        @pl.when(part >= 7)
        def _():
            acc_ref[...] += _dot_nt(p2_ref[...], w_ref[...])

        @pl.when(k_i == nk - 1)
        def _():
            du = acc_ref[...]
            x = x_ref[...]
            r = lax.rsqrt(jnp.mean(x * x, axis=-1, keepdims=True) + EPS)
            nrm = x * r
            dg_ref[...] += jnp.sum(du * nrm, axis=0, keepdims=True)
            dn = du * g_ref[...]
            gx_ref[...] = dh_ref[...] + r * (dn - nrm * jnp.mean(dn * nrm, axis=-1, keepdims=True))

    def pspec(lo, n):
        return pl.BlockSpec((None, tm, tk), lambda m, k: (_part_index(k, per, lo, n), m0 + m, k % per))

    row_in = pl.BlockSpec((tm, D_MODEL), lambda m, k: (m0 + m, 0))
    vec = pl.BlockSpec((1, D_MODEL), lambda m, k: (0, 0))
    return pl.pallas_call(
        body, grid=(rows // tm, nk),
        in_specs=[pspec(0, 4), pspec(4, 3), pspec(7, 3),
                  pl.BlockSpec((None, D_MODEL, tk), lambda m, k: (k // shard_blocks, 0, k % shard_blocks)),
                  row_in, vec, row_in],
        out_specs=[pl.BlockSpec((tm, D_MODEL), lambda m, k: (m, 0)), vec],
        out_shape=[jax.ShapeDtypeStruct((rows, D_MODEL), F32), jax.ShapeDtypeStruct((1, D_MODEL), F32)],
        scratch_shapes=[pltpu.VMEM((tm, D_MODEL), F32)],
        name="input_grad",
    )(da4, dc3, db3, w4, xp, norm_g, dh2)


class _Step:
    def __init__(self, x, tgt, norm_g, w4, b_merge, taps, woc, woa, wo, final_g):
        self.norm_g, self.w4, self.b_merge, self.taps = norm_g, w4, b_merge, taps
        self.woc, self.woa, self.wo = woc, woa, wo
        self.slopes = _alibi_slopes()
        self.xp, tp = _to_residue_major(x, tgt)
        u, self.ut = _rms_in(self.xp, norm_g)
        self.proj = _in_proj(u, w4)
        hc, self.hct = _conv_fwd(self.proj, taps)
        fwd = [_attn_fwd(self.proj, self.slopes, d) for d in PATTERNS]
        self.o, self.lse, ha, self.hat = _attn_combine([f[0] for f in fwd], [f[1] for f in fwd], self.proj)
        (self.yc, self.ya, _, self.mgt, self.dh2, self.dh2b, self.d_final_g, self.loss8) = _merge_loss(
            hc, ha, woc, woa, wo, self.proj, b_merge, self.xp, final_g, tp)

    def out_weight_grads(self):
        (dyc, dya, self.dhc, self.dout, self.dsum, self.db3, self.d_bias) = _merge_bwd(
            self.dh2b, self.wo, self.woc, self.woa, self.yc, self.ya, self.proj, self.b_merge, self.o)
        d_wo = _mm_lhs_resident(self.mgt, self.dh2b, 256, "dw_o")
        d_woc = _mm_lhs_resident(self.hct, dyc, 256, "dw_out_conv")
        d_woa = _mm_lhs_resident(self.hat, dya, 256, "dw_out_attn")
        return d_woc, d_woa, d_wo

    def conv_grads(self, after=0.0):
        self.da4, self.d_taps = _conv_bwd(self.proj, self.taps + after, self.dhc)

    def in_weight_grad(self, after=0.0):
        slopes = self.slopes + after
        self.dc3 = _sum3(*[_attn_bwd(self.proj, self.dout, self.lse, self.dsum, slopes, d) for d in PATTERNS])
        return _dw_in(self.ut, self.da4, self.dc3, self.db3)

    def input_grad(self, half, after=0.0):
        rows = self.xp.shape[0] // 2
        return _input_grad(self.da4, self.dc3, self.db3, self.w4, self.xp, self.norm_g + after, self.dh2,
                           half * rows, rows)


def _local_grads(x, tgt, norm_g, w4, b_merge, conv_w, woc, woa, wo, final_g):
    st = _Step(x, tgt, norm_g, w4, b_merge, conv_w, woc, woa, wo, final_g)
    d_woc, d_woa, d_wo = st.out_weight_grads()
    st.conv_grads()
    d_w4 = st.in_weight_grad()
    gx_lo, dg_lo = st.input_grad(0)
    gx_hi, dg_hi = st.input_grad(1)
    return (st.loss8, _to_natural(gx_lo, gx_hi), dg_lo + dg_hi, d_w4, st.d_bias, st.d_taps, d_woc, d_woa, d_wo,
            st.d_final_g)


MESH = pl.DeviceIdType.MESH
_CHIP_FLIPS = ((1, 0), (0, 1), (1, 1))
_ANY = pl.BlockSpec(memory_space=pl.ANY)


def _place():
    return lax.axis_index("x"), lax.axis_index("y"), lax.axis_index("c")


def _flip(v, f):
    return 1 - v if f else v


def _remote(src, dst, send_sems, recv_sems, k, device):
    return pltpu.make_async_remote_copy(src_ref=src, dst_ref=dst, send_sem=send_sems.at[k], recv_sem=recv_sems.at[k],
                                        device_id=device, device_id_type=MESH)


def _place_shard(w, chip):
    rows, cols = w.shape
    tm = 128

    def body(chip_ref, w_ref, o_ref):
        o_ref[0] = w_ref[...].astype(o_ref.dtype)

    return pl.pallas_call(
        body,
        grid_spec=pltpu.PrefetchScalarGridSpec(
            num_scalar_prefetch=1, grid=(rows // tm,),
            in_specs=[pl.BlockSpec((tm, cols), lambda i, chip_ref: (i, 0))],
            out_specs=pl.BlockSpec((1, tm, cols), lambda i, chip_ref: (chip_ref[0], i, 0))),
        out_shape=jax.ShapeDtypeStruct((4, rows, cols), MXU_DTYPE),
        name="place_shard",
    )(chip, w)


def _gather_weights(slots):
    n = len(slots)

    def body(*refs):
        outs = refs[n:2 * n]
        send_sems, recv_sems = refs[2 * n:]
        x, y, c = _place()
        chip = 2 * x + y
        sibling = (x, y, 1 - c)
        started = []
        for a in range(n):
            h = outs[a].shape[1] // 2
            mine = outs[a].at[chip, pl.ds(pl.multiple_of(c * h, 8), h)]
            for t, (fx, fy) in enumerate(_CHIP_FLIPS):
                cp = _remote(mine, mine, send_sems, recv_sems, 6 * a + t, (_flip(x, fx), _flip(y, fy), c))
                cp.start()
                started.append(cp)
        for a in range(n):
            h = outs[a].shape[1] // 2
            rows = pl.ds(pl.multiple_of(c * h, 8), h)
            for t, (fx, fy) in enumerate(_CHIP_FLIPS):
                landed = outs[a].at[2 * _flip(x, fx) + _flip(y, fy), rows]
                _remote(landed, landed, send_sems, recv_sems, 6 * a + t, sibling).wait_recv()
                cp = _remote(landed, landed, send_sems, recv_sems, 6 * a + 3 + t, sibling)
                cp.start()
                started.append(cp)
        for a in range(n):
            h = outs[a].shape[1] // 2
            rows = pl.ds(pl.multiple_of((1 - c) * h, 8), h)
            for t, (fx, fy) in enumerate(_CHIP_FLIPS):
                handed = outs[a].at[2 * _flip(x, fx) + _flip(y, fy), rows]
                _remote(handed, handed, send_sems, recv_sems, 6 * a + 3 + t, sibling).wait_recv()
        for cp in started:
            cp.wait_send()

    return pl.pallas_call(
        body, in_specs=[_ANY] * n, out_specs=[_ANY] * n,
        out_shape=[jax.ShapeDtypeStruct(s.shape, s.dtype) for s in slots],
        input_output_aliases={a: a for a in range(n)},
        scratch_shapes=[pltpu.SemaphoreType.DMA((6 * n,)), pltpu.SemaphoreType.DMA((6 * n,))],
        name="gather_weights",
    )(*slots)


_HBM = pl.BlockSpec(memory_space=pltpu.HBM)
_SEM = pl.BlockSpec(memory_space=pltpu.SEMAPHORE)
_EFFECT = pltpu.SideEffectType.DATAFLOW_SIDE_EFFECTING


class _SplitExchange:
    def __init__(self, name, srcs, land_shapes, n_copies, copies):
        self.name, self.n, self.copies = name, len(srcs), copies
        n = self.n
        lands = [lax.empty(s.shape, s.dtype) for s in land_shapes]
        bufs = [pltpu.with_memory_space_constraint(a, pltpu.HBM) for a in (*srcs, *lands)]

        def body(*refs):
            send_sems, recv_sems = refs[2 * n], refs[2 * n + 1]
            for cp in copies(refs[:n], refs[n:2 * n], send_sems, recv_sems):
                cp.start()
            refs[-1][...] = jnp.zeros(refs[-1].shape, F32)

        outs = pl.pallas_call(
            body, name=name + "_start",
            in_specs=[_HBM] * (2 * n),
            out_specs=[_SEM, _SEM] + [_HBM] * (2 * n) + [pl.BlockSpec(memory_space=pltpu.VMEM)],
            out_shape=[pltpu.SemaphoreType.DMA((n_copies,)), pltpu.SemaphoreType.DMA((n_copies,))]
            + [pltpu.HBM(b.shape, b.dtype) for b in bufs] + [jax.ShapeDtypeStruct((8, LANES), F32)],
            input_output_aliases={i: 2 + i for i in range(2 * n)},
            compiler_params=pltpu.CompilerParams(has_side_effects=_EFFECT),
        )(*bufs)
        self.sems, self.bufs, self.token = outs[:2], outs[2:2 + 2 * n], outs[-1]

    def after(self):
        return self.token[0, 0]

    def wait(self, done):
        n, copies = self.n, self.copies

        def body(*refs):
            send_sems, recv_sems = refs[2 * n], refs[2 * n + 1]
            for cp in copies(refs[:n], refs[n:2 * n], send_sems, recv_sems):
                cp.wait_send()
                cp.wait_recv()

        outs = pl.pallas_call(
            body, name=self.name + "_wait",
            in_specs=[_HBM] * (2 * n) + [_SEM, _SEM, _ANY],
            out_specs=[_HBM] * (2 * n),
            out_shape=[pltpu.HBM(b.shape, b.dtype) for b in self.bufs],
            input_output_aliases={i: i for i in range(2 * n)},
            compiler_params=pltpu.CompilerParams(has_side_effects=_EFFECT),
        )(*self.bufs, *self.sems, done)
        return outs[:n], outs[n:]


def _sibling_copies(srcs, lands, send_sems, recv_sems):
    x, y, c = _place()
    out = []
    for a, (src, land) in enumerate(zip(srcs, lands)):
        h = src.shape[1] // 2
        theirs = pl.ds(pl.multiple_of((1 - c) * h, 8), h)
        out.append(_remote(src.at[:, theirs], land, send_sems, recv_sems, a, (x, y, 1 - c)))
    return out


def _grads_to_sibling(name, grads):
    shapes = [jax.ShapeDtypeStruct((4, g.shape[1] // 2, g.shape[2]), g.dtype) for g in grads]
    return _SplitExchange(name, grads, shapes, len(grads), _sibling_copies)


def _chip_copies(srcs, lands, send_sems, recv_sems):
    x, y, c = _place()
    out = []
    for a, (src, land) in enumerate(zip(srcs, lands)):
        for t, (fx, fy) in enumerate(_CHIP_FLIPS):
            tx, ty = _flip(x, fx), _flip(y, fy)
            out.append(_remote(src.at[2 * tx + ty], land.at[t], send_sems, recv_sems, 3 * a + t, (tx, ty, c)))
    return out


def _grads_to_chips(name, parts):
    shapes = [jax.ShapeDtypeStruct((3, *p.shape[1:]), p.dtype) for p in parts]
    return _SplitExchange(name, parts, shapes, 3 * len(parts), _chip_copies)


def _add_halves(g, r, half):
    _, rows, cols = g.shape
    h = rows // 2
    tm = min(h, 128)
    nt = h // tm

    def body(half_ref, g_ref, r_ref, f_ref, b_ref):
        s = g_ref[...] + r_ref[...]
        f_ref[...] = s
        b_ref[...] = s.astype(b_ref.dtype)

    spec = pl.BlockSpec((1, tm, cols), lambda j, i, half_ref: (j, i, 0))
    return pl.pallas_call(
        body,
        grid_spec=pltpu.PrefetchScalarGridSpec(
            num_scalar_prefetch=1, grid=(4, nt),
            in_specs=[pl.BlockSpec((1, tm, cols), lambda j, i, half_ref: (j, half_ref[0] * nt + i, 0)), spec],
            out_specs=[spec, spec]),
        out_shape=[jax.ShapeDtypeStruct((4, h, cols), F32), jax.ShapeDtypeStruct((4, h, cols), BF16)],
        name="add_sibling_grads",
    )(half, g, r)


def _add_chips(own, recv, where):
    _, h, cols = own.shape
    tm = min(h, 128)
    nt = h // tm

    def body(where_ref, o_ref, r_ref, out_ref):
        out_ref[...] = ((o_ref[0] + r_ref[0].astype(F32)) + r_ref[1].astype(F32)) + r_ref[2].astype(F32)

    return pl.pallas_call(
        body,
        grid_spec=pltpu.PrefetchScalarGridSpec(
            num_scalar_prefetch=1, grid=(nt,),
            in_specs=[pl.BlockSpec((1, tm, cols), lambda i, where_ref: (where_ref[0], i, 0)),
                      pl.BlockSpec((3, tm, cols), lambda i, where_ref: (0, i, 0))],
            out_specs=pl.BlockSpec((tm, cols), lambda i, where_ref: (where_ref[1] * nt + i, 0))),
        out_shape=jax.ShapeDtypeStruct((2 * h, cols), F32),
        name="add_chip_grads",
    )(where, own, recv)


def _share_halves(shards):
    n = len(shards)

    def body(*refs):
        outs = refs[n:2 * n]
        send_sems, recv_sems = refs[2 * n:]
        x, y, c = _place()
        copies = []
        for a in range(n):
            h = outs[a].shape[0] // 2
            mine = outs[a].at[pl.ds(pl.multiple_of(c * h, 8), h)]
            copies.append(_remote(mine, mine, send_sems, recv_sems, a, (x, y, 1 - c)))
        for cp in copies:
            cp.start()
        for a, cp in enumerate(copies):
            cp.wait_send()
            h = outs[a].shape[0] // 2
            theirs = outs[a].at[pl.ds(pl.multiple_of((1 - c) * h, 8), h)]
            _remote(theirs, theirs, send_sems, recv_sems, a, (x, y, 1 - c)).wait_recv()

    return pl.pallas_call(
        body, in_specs=[_ANY] * n, out_specs=[_ANY] * n,
        out_shape=[jax.ShapeDtypeStruct(p.shape, p.dtype) for p in shards],
        input_output_aliases={a: a for a in range(n)},
        scratch_shapes=[pltpu.SemaphoreType.DMA((n,)), pltpu.SemaphoreType.DMA((n,))],
        name="share_reduced_halves",
    )(*shards)


def _exchange_small(rows, reduce):
    cols = rows[0].shape[1]
    n = len(rows)
    assert sum(r.shape[0] for r in rows) <= 8

    def body(*refs):
        ins, out_ref = refs[:n], refs[n]
        vec_ref, gath_ref, send_sems, recv_sems = refs[n + 1:]
        x, y, c = _place()
        me = 4 * x + 2 * y + c
        vec_ref[...] = jnp.zeros(vec_ref.shape, F32)
        at = 0
        for r in ins:
            vec_ref[at:at + r.shape[0], :] = r[...]
            at += r.shape[0]
        copies = []
        for k in range(1, 8):
            peer = (_flip(x, (k >> 2) & 1), _flip(y, (k >> 1) & 1), _flip(c, k & 1))
            copies.append(_remote(vec_ref, gath_ref.at[me], send_sems, recv_sems, k - 1, peer))
        for cp in copies:
            cp.start()
        gath_ref[me] = vec_ref[...]
        for cp in copies:
            cp.wait()
        if reduce:
            tot = gath_ref[0]
            for dev in range(1, 8):
                tot = tot + gath_ref[dev]
            out_ref[...] = tot
            out_ref[7:8, :] = jnp.zeros((1, cols), F32) + jnp.sum(tot[7:8, :])
        else:
            out_ref[...] = gath_ref[...]

    vm = pl.BlockSpec(memory_space=pltpu.VMEM)
    return pl.pallas_call(
        body, in_specs=[vm] * n, out_specs=vm,
        out_shape=jax.ShapeDtypeStruct((8, cols) if reduce else (8, 8, cols), F32),
        scratch_shapes=[pltpu.VMEM((8, cols), F32), pltpu.VMEM((8, 8, cols), F32),
                        pltpu.SemaphoreType.DMA((7,)), pltpu.SemaphoreType.DMA((7,))],
        name="reduce_small" if reduce else "gather_small",
    )(*rows)


def _adamw(w, g, m, v, name):
    rows, cols = w.shape
    tm = 128 if rows % 128 == 0 else rows

    def body(w_ref, g_ref, m_ref, v_ref, d_ref, m2_ref, v2_ref):
        gr = g_ref[...]
        m2 = ADAM_B1 * m_ref[...] + (1.0 - ADAM_B1) * gr
        v2 = ADAM_B2 * v_ref[...] + (1.0 - ADAM_B2) * (gr * gr)
        m_hat = m2 / (1.0 - ADAM_B1 ** ADAM_STEP)
        v_hat = v2 / (1.0 - ADAM_B2 ** ADAM_STEP)
        d_ref[...] = -ADAM_LR * (m_hat / (jnp.sqrt(v_hat) + ADAM_EPS) + ADAM_WD * w_ref[...])
        m2_ref[...] = m2
        v2_ref[...] = v2

    spec = pl.BlockSpec((tm, cols), lambda i: (i, 0))
    sds = jax.ShapeDtypeStruct((rows, cols), F32)
    return pl.pallas_call(body, grid=(rows // tm,), in_specs=[spec] * 4, out_specs=[spec] * 3,
                          out_shape=[sds] * 3, name=name)(w, g, m, v)


def kernel(x, norm_g, w_in, b_merge, conv_w, w_out_conv, w_out_attn, w_o, final_g, loss_target, m_norm_g, m_w_in, m_b_merge, m_conv_w, m_w_out_conv, m_w_out_attn, m_w_o, m_final_g, v_norm_g, v_w_in, v_b_merge, v_conv_w, v_w_out_conv, v_w_out_attn, v_w_o, v_final_g):
    mx, my, mc = _place()
    chip = (2 * mx + my).astype(jnp.int32)
    seq = x.shape[1]

    chip1 = chip.reshape(1)
    w4, woc4, woa4, wo4 = _gather_weights([_place_shard(w[0], chip1) for w in (w_in, w_out_conv, w_out_attn, w_o)])
    taps8 = _exchange_small([conv_w[0]], reduce=False)
    taps = jnp.concatenate([taps8[2 * j, :3, :] for j in range(4)], axis=1)

    st = _Step(x[0], loss_target[0], norm_g, w4, b_merge, taps, woc4.reshape(D_MODEL, D_MODEL),
               woa4.reshape(D_MODEL, D_MODEL), wo4.reshape(D_MODEL, D_MODEL), final_g.reshape(1, D_MODEL))

    half = mc.astype(jnp.int32).reshape(1)
    where = jnp.stack([chip, mc.astype(jnp.int32)])
    out_grads = [g.reshape(4, -1, D_MODEL) for g in st.out_weight_grads()]
    to_sibling = _grads_to_sibling("out_grads_to_sibling", out_grads)
    st.conv_grads(after=to_sibling.after())
    out_partial = [_add_halves(g, r, half) for g, r in zip(*to_sibling.wait(st.da4))]
    to_chips = _grads_to_chips("out_grads_to_chips", [p[1] for p in out_partial])
    d_w4 = st.in_weight_grad(after=to_chips.after())
    out_reduced = [_add_chips(p[0], r, where) for p, r in zip(out_partial, to_chips.wait(st.dc3)[1])]

    to_sibling = _grads_to_sibling("in_grad_to_sibling", [d_w4])
    gx_lo, dg_lo = st.input_grad(0, after=to_sibling.after())
    (d_w4,), (from_sibling,) = to_sibling.wait(gx_lo)
    in_partial = _add_halves(d_w4, from_sibling, half)
    to_chips = _grads_to_chips("in_grad_to_chips", [in_partial[1]])
    gx_hi, dg_hi = st.input_grad(1, after=to_chips.after())
    grad_x = _to_natural(gx_lo, gx_hi)
    in_reduced = _add_chips(in_partial[0], to_chips.wait(grad_x)[1][0], where)
    g_w_in, g_woc, g_woa, g_wo = _share_halves([in_reduced] + out_reduced)

    small = _exchange_small([dg_lo + dg_hi, st.d_bias.reshape(2, D_MODEL), st.d_taps, st.d_final_g,
                             st.loss8.reshape(1, D_MODEL)], reduce=True)
    loss = (0.5 / D_MODEL) * small[7, 0]
    g_norm_g = small[0:1]
    g_bias = small[1:3].reshape(1, 2 * D_MODEL)
    g_taps = lax.dynamic_slice(small[3:6], (0, chip * (D_MODEL // 4)), (3, D_MODEL // 4))
    g_final_g = small[6:7]

    upd = [
        _adamw(norm_g, g_norm_g, m_norm_g, v_norm_g, "adamw_norm_g"),
        _adamw(w_in[0], g_w_in, m_w_in[0], v_w_in[0], "adamw_w_in"),
        _adamw(b_merge, g_bias, m_b_merge, v_b_merge, "adamw_b_merge"),
        _adamw(conv_w[0], g_taps, m_conv_w[0], v_conv_w[0], "adamw_conv_w"),
        _adamw(w_out_conv[0], g_woc, m_w_out_conv[0], v_w_out_conv[0], "adamw_w_out_conv"),
        _adamw(w_out_attn[0], g_woa, m_w_out_attn[0], v_w_out_attn[0], "adamw_w_out_attn"),
        _adamw(w_o[0], g_wo, m_w_o[0], v_w_o[0], "adamw_w_o"),
        _adamw(final_g.reshape(1, D_MODEL), g_final_g, m_final_g.reshape(1, D_MODEL),
               v_final_g.reshape(1, D_MODEL), "adamw_final_g"),
    ]
    shapes = [norm_g.shape, w_in.shape, b_merge.shape, conv_w.shape, w_out_conv.shape, w_out_attn.shape,
              w_o.shape, final_g.shape]
    grads_out = [g_norm_g, g_w_in, g_bias, g_taps, g_woc, g_woa, g_wo, g_final_g]
    outs = [loss, grad_x.reshape(1, seq, D_MODEL)]
    outs += [g.reshape(s) for g, s in zip(grads_out, shapes)]
    for k in range(3):
        outs += [u[k].reshape(s) for u, s in zip(upd, shapes)]
    return tuple(outs)
```

```python
import functools

import numpy as np
import jax
import jax.numpy as jnp
from jax import lax
from jax.experimental import pallas as pl
from jax.experimental.pallas import tpu as pltpu

F32 = jnp.float32
BF16 = jnp.bfloat16
MXU_DTYPE = jnp.bfloat16
ACT_DTYPE = jnp.bfloat16

D_MODEL = 1024
N_HEADS = 16
HEAD_DIM = 64
QB = 128
N_RES = 16
LANES = 128
HP = N_HEADS * HEAD_DIM // LANES
IN_COLS = 10 * D_MODEL
SHARD_COLS = IN_COLS // 4
EPS = 1e-6
NEG = -1e30

ADAM_LR, ADAM_B1, ADAM_B2, ADAM_EPS, ADAM_WD, ADAM_STEP = 0.001, 0.9, 0.999, 1e-08, 0.01, 10

PATTERNS = {1: (16, 16), 4: (4, 32), 16: (1, 128)}

_NN = (((1,), (0,)), ((), ()))
_NT = (((1,), (1,)), ((), ()))


def _dot(a, b):
    return lax.dot_general(a.astype(MXU_DTYPE), b.astype(MXU_DTYPE), _NN, preferred_element_type=F32)


def _dot_nt(a, b):
    return lax.dot_general(a.astype(MXU_DTYPE), b.astype(MXU_DTYPE), _NT, preferred_element_type=F32)


def _split3(x):
    hi = x.astype(BF16)
    r1 = x - hi.astype(F32)
    mid = r1.astype(BF16)
    lo = (r1 - mid.astype(F32)).astype(BF16)
    return hi, mid, lo


def _select_rows(sel, x):
    return sum(lax.dot_general(sel, t, _NN, preferred_element_type=F32) for t in _split3(x))


def _select_cols(x, sel):
    return sum(lax.dot_general(t, sel, _NN, preferred_element_type=F32) for t in _split3(x))


def _sigmoid(z):
    return 1.0 / (1.0 + jnp.exp(-z))


def _perm_matrix():
    idx = np.arange(256)
    p = np.zeros((256, 256), np.float32)
    p[(idx % 16) * 16 + idx // 16, idx] = 1.0
    return jnp.asarray(p, BF16)


def _head_expand_matrix():
    e = np.zeros((LANES, D_MODEL), np.float32)
    for h in range(N_HEADS):
        e[8 * h, HEAD_DIM * h:HEAD_DIM * (h + 1)] = 1.0
    return jnp.asarray(e, BF16)


def _head_sum_matrix():
    e = np.zeros((D_MODEL, LANES), np.float32)
    for h in range(N_HEADS):
        e[HEAD_DIM * h:HEAD_DIM * (h + 1), 8 * h:8 * (h + 1)] = 1.0
    return jnp.asarray(e, BF16)


def _attn_tables(d):
    g_n, rq = PATTERNS[d]
    q_n = g_n * rq
    gq, iq = np.arange(q_n) // rq, np.arange(q_n) % rq

    def tab(kn, base):
        k_n = g_n * kn
        gk, jk = np.arange(k_n) // kn, np.arange(k_n) % kn
        delta = g_n * (base + iq[:, None] - jk[None, :]) + gq[:, None] - gk[None, :]
        valid = (delta >= 0) & (delta <= QB)
        dist = np.where(valid, d * delta, 0).astype(np.float32)
        madd = np.where(valid, 0.0, NEG).astype(np.float32)
        return dist, madd

    d0, m0 = tab(rq if g_n == 1 else 2 * rq, 0)
    d1, m1 = tab(2 * rq, rq)
    return d0, m0, d1, m1


def _alibi_slopes():
    return jnp.exp2(-8.0 * jnp.arange(1, N_HEADS + 1, dtype=F32) / N_HEADS)


def _to_residue_major(x, tgt, after=0.0):
    s_n, c_n = x.shape
    lr = s_n // N_RES
    pm = (_perm_matrix().astype(F32) + after).astype(BF16)

    def body(p_ref, x_ref, t_ref, xo_ref, to_ref):
        pm = p_ref[...]
        xo_ref[...] = _select_rows(pm, x_ref[...]).reshape(16, 16, c_n)
        to_ref[...] = _select_rows(pm, t_ref[...]).reshape(16, 16, c_n)

    nat = pl.BlockSpec((256, c_n), lambda i: (i, 0))
    res = pl.BlockSpec((16, 16, c_n), lambda i: (0, i, 0))
    xo, to = pl.pallas_call(
        body, grid=(s_n // 256,),
        in_specs=[pl.BlockSpec((256, 256), lambda i: (0, 0)), nat, nat],
        out_specs=[res, res],
        out_shape=[jax.ShapeDtypeStruct((16, lr, c_n), F32)] * 2,
        name="perm_in",
    )(pm, x, tgt)
    return xo.reshape(s_n, c_n), to.reshape(s_n, c_n)


def _to_natural(gx_lo, gx_hi):
    half_rows, c_n = gx_lo.shape
    lr = half_rows // (N_RES // 2)

    def body(p_ref, lo_ref, hi_ref, o_ref):
        g = jnp.concatenate([lo_ref[...], hi_ref[...]], axis=0)
        o_ref[...] = _select_rows(p_ref[...], g.reshape(256, c_n))

    half = pl.BlockSpec((8, 16, c_n), lambda i: (0, i, 0))
    return pl.pallas_call(
        body, grid=(lr // 16,),
        in_specs=[pl.BlockSpec((256, 256), lambda i: (0, 0)), half, half],
        out_specs=pl.BlockSpec((256, c_n), lambda i: (i, 0)),
        out_shape=jax.ShapeDtypeStruct((2 * half_rows, c_n), F32),
        name="perm_out",
    )(_perm_matrix(), gx_lo.reshape(8, lr, c_n), gx_hi.reshape(8, lr, c_n))


def _rms_in(xp, norm_g):
    s_n, c_n = xp.shape
    tm = 512

    def body(x_ref, g_ref, u_ref, ut_ref):
        x = x_ref[...]
        r = lax.rsqrt(jnp.mean(x * x, axis=-1, keepdims=True) + EPS)
        u = x * r * g_ref[...]
        u_ref[...] = u.astype(u_ref.dtype)
        ut_ref[...] = u.T.astype(ut_ref.dtype)

    return pl.pallas_call(
        body, grid=(s_n // tm,),
        in_specs=[pl.BlockSpec((tm, c_n), lambda i: (i, 0)), pl.BlockSpec((1, c_n), lambda i: (0, 0))],
        out_specs=[pl.BlockSpec((tm, c_n), lambda i: (i, 0)), pl.BlockSpec((c_n, tm), lambda i: (0, i))],
        out_shape=[jax.ShapeDtypeStruct((s_n, c_n), ACT_DTYPE), jax.ShapeDtypeStruct((c_n, s_n), ACT_DTYPE)],
        name="rms_in",
    )(xp, norm_g)


def _in_proj(u, w4):
    s_n = u.shape[0]
    tn, cm = 512, 512
    per = SHARD_COLS // tn

    def body(a_ref, b_ref, o_ref):
        b = b_ref[...]
        for c in range(s_n // cm):
            o_ref[c * cm:(c + 1) * cm, :] = _dot(a_ref[c * cm:(c + 1) * cm, :], b).astype(o_ref.dtype)

    return pl.pallas_call(
        body, grid=(IN_COLS // tn,),
        in_specs=[pl.BlockSpec((s_n, D_MODEL), lambda n: (0, 0)),
                  pl.BlockSpec((None, D_MODEL, tn), lambda n: (n // per, 0, n % per))],
        out_specs=pl.BlockSpec((s_n, tn), lambda n: (0, n)),
        out_shape=jax.ShapeDtypeStruct((s_n, IN_COLS), ACT_DTYPE),
        name="in_proj",
    )(u, w4)


def _conv_terms(xc_ref, cg_ref, r, row, lr):
    def a_of(q):
        return cg_ref[q].astype(F32) * xc_ref[q].astype(F32)

    def shift_down(v):
        return jnp.where(row >= 1, pltpu.roll(v, 1, 0), 0.0)

    a = a_of(r)
    am1 = a_of(r - 1) if r >= 1 else shift_down(a_of(N_RES - 1))
    am2 = a_of(r - 2) if r >= 2 else shift_down(a_of(N_RES - 2 + r))
    return a, am1, am2


def _conv_fwd(proj, conv_w):
    s_n = proj.shape[0]
    lr = s_n // N_RES
    pv = proj.reshape(N_RES, lr, IN_COLS)

    def body(xc_ref, bg_ref, cg_ref, zc_ref, w_ref, hc_ref, hct_ref):
        w = w_ref[...]
        row = lax.broadcasted_iota(jnp.int32, (lr, LANES), 0)
        for r in range(N_RES):
            a, am1, am2 = _conv_terms(xc_ref, cg_ref, r, row, lr)
            c = w[0:1] * am2 + w[1:2] * am1 + w[2:3] * a
            z = zc_ref[r].astype(F32)
            hc = z * _sigmoid(z) * bg_ref[r].astype(F32) * c
            hc_ref[r] = hc.astype(hc_ref.dtype)
            hct_ref[:, r * lr:(r + 1) * lr] = hc.T.astype(hct_ref.dtype)

    def col(part):
        return pl.BlockSpec((N_RES, lr, LANES), lambda j: (0, 0, part * 8 + j))

    hc, hct = pl.pallas_call(
        body, grid=(D_MODEL // LANES,),
        in_specs=[col(0), col(1), col(2), col(3), pl.BlockSpec((3, LANES), lambda j: (0, j))],
        out_specs=[pl.BlockSpec((N_RES, lr, LANES), lambda j: (0, 0, j)),
                   pl.BlockSpec((LANES, s_n), lambda j: (j, 0))],
        out_shape=[jax.ShapeDtypeStruct((N_RES, lr, D_MODEL), ACT_DTYPE),
                   jax.ShapeDtypeStruct((D_MODEL, s_n), ACT_DTYPE)],
        name="conv_fwd",
    )(pv, pv, pv, pv, conv_w)
    return hc.reshape(s_n, D_MODEL), hct


RES_PER_STEP = 8
FWD_BATCH = {1: 4, 4: 8, 16: RES_PER_STEP}
BWD_BATCH = {1: 2, 4: 8, 16: RES_PER_STEP}

_BNT = (((2,), (2,)), ((0,), (0,)))
_BNN = (((2,), (1,)), ((0,), (0,)))


def _bdot(a, b, dims):
    return lax.dot_general(a.astype(MXU_DTYPE), b.astype(MXU_DTYPE), dims, preferred_element_type=F32)


def _pattern_view_shape(s_n, c_n, g_n, lead=()):
    lr = s_n // N_RES
    return (*lead, 4, 4, lr, c_n) if g_n == 4 else (*lead, N_RES, lr, c_n)


def _pattern_view(a, g_n, lead=()):
    return a.reshape(_pattern_view_shape(a.shape[-2], a.shape[-1], g_n, lead))


def _pattern_grid(g_n):
    return (N_RES // RES_PER_STEP if g_n == 1 else N_RES // g_n, HP)


def _pattern_spec(g_n, lr, col_of_hp, lead=()):
    z = (0,) * len(lead)
    if g_n == 16:
        return pl.BlockSpec((*lead, 16, lr, LANES), lambda r, hp: (*z, 0, 0, col_of_hp(hp)))
    if g_n == 4:
        return pl.BlockSpec((*lead, 4, None, lr, LANES), lambda r, hp: (*z, 0, r, 0, col_of_hp(hp)))
    return pl.BlockSpec((*lead, RES_PER_STEP, lr, LANES), lambda r, hp: (*z, r, 0, col_of_hp(hp)))


class _Units:
    def __init__(self, g_n, rq):
        self.g_n, self.rq, self.per_res = g_n, rq, g_n == 1

    def load(self, ref, starts, rows):
        if self.per_res:
            return ref[:, pl.ds(starts[0], rows), :]
        return jnp.stack([ref[:, pl.ds(s, rows), :].reshape(self.g_n * rows, LANES) for s in starts])

    def store(self, ref, starts, rows, val, add=False, lead=()):
        for b, s in enumerate(starts):
            idx = (*lead, slice(None), pl.ds(s, rows), slice(None))
            v = val if self.per_res else val[b].reshape(self.g_n, rows, LANES)
            ref[idx] = (ref[idx] + v if add else v).astype(ref.dtype)
            if self.per_res:
                break


def _stack_heads(x, low):
    zero = jnp.zeros_like(x)
    return jnp.concatenate([jnp.where(low, x, zero), jnp.where(low, zero, x)], axis=1)


def _batches(nb, rq, size, per_res):
    if per_res:
        return [(0, 0)], nb - 1
    first = [(n * rq, max(n - 1, 0) * rq) for n in range(min(size, nb))]
    assert nb % len(first) == 0
    return first, nb // len(first) - 1


def _attn_fwd(proj, slopes, d):
    g_n, rq = PATTERNS[d]
    un = _Units(g_n, rq)
    s_n = proj.shape[0]
    lr = s_n // N_RES
    nb = lr // rq
    q_n = g_n * rq
    d0, m0, d1, m1 = _attn_tables(d)
    k0_rows = d0.shape[1] // g_n
    first, n_more = _batches(nb, rq, FWD_BATCH[d], un.per_res)
    bsz = len(first)

    def body(sl_ref, q_ref, k_ref, v_ref, d0_ref, m0_ref, d1_ref, m1_ref, o_ref, lse_ref, b0_ref, b1_ref):
        hp = pl.program_id(1)

        @pl.when(hp == 0)
        def _():
            lse_ref[...] = jnp.zeros(lse_ref.shape, F32)

        for h in (0, 1):
            slope = sl_ref[2 * hp + h]
            b0_ref[h * q_n:(h + 1) * q_n, :] = m0_ref[...] - slope * d0_ref[...]
            b1_ref[h * q_n:(h + 1) * q_n, :] = m1_ref[...] - slope * d1_ref[...]

        lane = lax.broadcasted_iota(jnp.int32, (1, q_n, LANES), 2)
        low = lane < HEAD_DIM
        grp = lane // 8

        def batch(q_starts, k_starts, k_rows, bias):
            qq = _stack_heads(un.load(q_ref, q_starts, rq) * 0.125, low)
            s = _bdot(qq, un.load(k_ref, k_starts, k_rows), _BNT) + bias
            m = jnp.max(s, axis=2, keepdims=True)
            p = jnp.exp(s - m)
            l = jnp.sum(p, axis=2, keepdims=True)
            o = _bdot(p, un.load(v_ref, k_starts, k_rows), _BNN) * (1.0 / l)
            lse = m + jnp.log(l)
            un.store(o_ref, q_starts, rq, jnp.where(low, o[:, :q_n], o[:, q_n:]))
            upd = jnp.where(grp == 2 * hp, lse[:, :q_n], 0.0) + jnp.where(grp == 2 * hp + 1, lse[:, q_n:], 0.0)
            un.store(lse_ref, q_starts, rq, upd, add=True)

        if un.per_res:
            batch([0], [0], k0_rows, b0_ref[...][None])
        else:
            bias = jnp.concatenate([b0_ref[...][None]] + [b1_ref[...][None]] * (bsz - 1), axis=0)
            batch([q for q, _ in first], [k for _, k in first], 2 * rq, bias)

        def more(j, carry):
            n0 = j * bsz
            qs = [pl.multiple_of((n0 + i) * rq, rq) for i in range(bsz)]
            ks = [pl.multiple_of((n0 + i - 1) * rq, rq) for i in range(bsz)]
            batch(qs, ks, 2 * rq, b1_ref[...][None])
            return carry

        lax.fori_loop(1, 1 + n_more, more, 0)

    pv = _pattern_view(proj, g_n)
    full = lambda a: pl.BlockSpec(a.shape, lambda r, hp: (0, 0))
    o, lse = pl.pallas_call(
        body, grid=_pattern_grid(g_n),
        in_specs=[pl.BlockSpec(memory_space=pltpu.SMEM),
                  _pattern_spec(g_n, lr, lambda hp: 32 + hp),
                  _pattern_spec(g_n, lr, lambda hp: 40 + hp),
                  _pattern_spec(g_n, lr, lambda hp: 48 + hp),
                  full(d0), full(m0), full(d1), full(m1)],
        out_specs=[_pattern_spec(g_n, lr, lambda hp: hp), _pattern_spec(g_n, lr, lambda hp: 0)],
        out_shape=[jax.ShapeDtypeStruct(_pattern_view_shape(s_n, D_MODEL, g_n), F32),
                   jax.ShapeDtypeStruct(_pattern_view_shape(s_n, LANES, g_n), F32)],
        scratch_shapes=[pltpu.VMEM((2 * q_n, d0.shape[1]), F32), pltpu.VMEM((2 * q_n, 2 * q_n), F32)],
        name=f"attn_fwd_d{d}",
    )(slopes, pv, pv, pv, d0, m0, d1, m1)
    return o.reshape(s_n, D_MODEL), lse.reshape(s_n, LANES)


def _attn_combine(outs, lses, proj):
    s_n = proj.shape[0]
    tm = 512

    def body(o1_ref, o2_ref, o3_ref, l1_ref, l2_ref, l3_ref, za_ref, e_ref, o_ref, lse_ref, ha_ref, hat_ref):
        ls = [l1_ref[...], l2_ref[...], l3_ref[...]]
        mx = jnp.maximum(jnp.maximum(ls[0], ls[1]), ls[2])
        den = sum(jnp.exp(l - mx) for l in ls)
        lse = mx + jnp.log(den)
        lse_ref[...] = lse
        o = jnp.zeros((tm, D_MODEL), F32)
        for l, oref in zip(ls, (o1_ref, o2_ref, o3_ref)):
            o = o + _select_cols(jnp.exp(l - lse), e_ref[...]) * oref[...]
        o_ref[...] = o
        z = za_ref[...].astype(F32)
        ha = z * _sigmoid(z) * o
        ha_ref[...] = ha.astype(ha_ref.dtype)
        hat_ref[...] = ha.T.astype(hat_ref.dtype)

    row = lambda w: pl.BlockSpec((tm, w), lambda i: (i, 0))
    return pl.pallas_call(
        body, grid=(s_n // tm,),
        in_specs=[row(D_MODEL)] * 3 + [row(LANES)] * 3
        + [pl.BlockSpec((tm, D_MODEL), lambda i: (i, 7)), pl.BlockSpec((LANES, D_MODEL), lambda i: (0, 0))],
        out_specs=[row(D_MODEL), row(LANES), row(D_MODEL), pl.BlockSpec((D_MODEL, tm), lambda i: (0, i))],
        out_shape=[jax.ShapeDtypeStruct((s_n, D_MODEL), F32), jax.ShapeDtypeStruct((s_n, LANES), F32),
                   jax.ShapeDtypeStruct((s_n, D_MODEL), ACT_DTYPE), jax.ShapeDtypeStruct((D_MODEL, s_n), ACT_DTYPE)],
        name="attn_combine",
    )(*outs, *lses, proj, _head_expand_matrix())


def _gates(gc_ref, ga_ref, b_ref):
    b = b_ref[...]
    gc = _sigmoid(gc_ref[...].astype(F32) + b[:, :D_MODEL])
    ga = _sigmoid(ga_ref[...].astype(F32) + b[:, D_MODEL:])
    return gc, ga


def _merge_loss(hc, ha, woc, woa, wo, proj, b_merge, xp, final_g, tgt):
    s_n = xp.shape[0]
    tm = 512

    def body(hc_ref, ha_ref, woc_ref, woa_ref, wo_ref, gc_ref, ga_ref, b_ref, x_ref, gf_ref, t_ref,
             yc_ref, ya_ref, mg_ref, mgt_ref, dh_ref, dhb_ref, dgf_ref, loss_ref):
        i = pl.program_id(0)

        @pl.when(i == 0)
        def _():
            dgf_ref[...] = jnp.zeros(dgf_ref.shape, F32)
            loss_ref[...] = jnp.zeros(loss_ref.shape, F32)

        yc = _dot(hc_ref[...], woc_ref[...])
        ya = _dot(ha_ref[...], woa_ref[...])
        gc, ga = _gates(gc_ref, ga_ref, b_ref)
        mg = gc * yc + ga * ya
        yc_ref[...] = yc.astype(yc_ref.dtype)
        ya_ref[...] = ya.astype(ya_ref.dtype)
        mg_ref[...] = mg.astype(mg_ref.dtype)
        mgt_ref[...] = mg.T.astype(mgt_ref.dtype)
        h2 = x_ref[...] + _dot(mg, wo_ref[...])
        r2 = lax.rsqrt(jnp.mean(h2 * h2, axis=-1, keepdims=True) + EPS)
        nrm = h2 * r2
        gf = gf_ref[...]
        err = nrm * gf - t_ref[...]
        e2 = (err * err).reshape(tm // 8, 8, D_MODEL).sum(axis=0)
        loss_ref[...] += sum(e2[:, c * LANES:(c + 1) * LANES] for c in range(D_MODEL // LANES))
        dy = err * (1.0 / D_MODEL)
        dgf_ref[...] += jnp.sum(dy * nrm, axis=0, keepdims=True)
        dn = dy * gf
        dh2 = r2 * (dn - nrm * jnp.mean(dn * nrm, axis=-1, keepdims=True))
        dh_ref[...] = dh2
        dhb_ref[...] = dh2.astype(dhb_ref.dtype)

    row = pl.BlockSpec((tm, D_MODEL), lambda i: (i, 0))
    wsp = pl.BlockSpec((D_MODEL, D_MODEL), lambda i: (0, 0))
    vec = lambda w: pl.BlockSpec((1, w), lambda i: (0, 0))
    act = jax.ShapeDtypeStruct((s_n, D_MODEL), ACT_DTYPE)
    return pl.pallas_call(
        body, grid=(s_n // tm,),
        in_specs=[row, row, wsp, wsp, wsp,
                  pl.BlockSpec((tm, D_MODEL), lambda i: (i, 8)), pl.BlockSpec((tm, D_MODEL), lambda i: (i, 9)),
                  vec(2 * D_MODEL), row, vec(D_MODEL), row],
        out_specs=[row, row, row, pl.BlockSpec((D_MODEL, tm), lambda i: (0, i)), row, row,
                   vec(D_MODEL), pl.BlockSpec((8, LANES), lambda i: (0, 0))],
        out_shape=[act, act, act, jax.ShapeDtypeStruct((D_MODEL, s_n), ACT_DTYPE),
                   jax.ShapeDtypeStruct((s_n, D_MODEL), F32), act,
                   jax.ShapeDtypeStruct((1, D_MODEL), F32), jax.ShapeDtypeStruct((8, LANES), F32)],
        name="merge_loss",
    )(hc, ha, woc, woa, wo, proj, proj, b_merge, xp, final_g, tgt)


def _merge_bwd(dh2b, wo, woc, woa, yc, ya, proj, b_merge, o):
    s_n = dh2b.shape[0]
    tm = 512

    def body(dh_ref, wo_ref, woc_ref, woa_ref, yc_ref, ya_ref, gc_ref, ga_ref, b_ref, o_ref, za_ref, e_ref,
             dyc_ref, dya_ref, dhc_ref, do_ref, dsum_ref, db3_ref, dbias_ref):
        i = pl.program_id(0)

        @pl.when(i == 0)
        def _():
            dbias_ref[...] = jnp.zeros(dbias_ref.shape, F32)

        dmg = _dot_nt(dh_ref[...], wo_ref[...])
        gc, ga = _gates(gc_ref, ga_ref, b_ref)
        dgc = dmg * yc_ref[...].astype(F32) * gc * (1.0 - gc)
        dga = dmg * ya_ref[...].astype(F32) * ga * (1.0 - ga)
        dbias_ref[:, :D_MODEL] += jnp.sum(dgc, axis=0, keepdims=True)
        dbias_ref[:, D_MODEL:] += jnp.sum(dga, axis=0, keepdims=True)
        dyc = dmg * gc
        dya = dmg * ga
        dyc_ref[...] = dyc.astype(dyc_ref.dtype)
        dya_ref[...] = dya.astype(dya_ref.dtype)
        dhc_ref[...] = _dot_nt(dyc, woc_ref[...]).astype(dhc_ref.dtype)
        dha = _dot_nt(dya, woa_ref[...])
        z = za_ref[...].astype(F32)
        sg = _sigmoid(z)
        ov = o_ref[...]
        dout = dha * z * sg
        do_ref[...] = dout.astype(do_ref.dtype)
        dsum_ref[...] = _select_cols(dout * ov, e_ref[...])
        db3_ref[0] = (dha * ov * sg * (1.0 + z * (1.0 - sg))).astype(db3_ref.dtype)
        db3_ref[1] = dgc.astype(db3_ref.dtype)
        db3_ref[2] = dga.astype(db3_ref.dtype)

    row = pl.BlockSpec((tm, D_MODEL), lambda i: (i, 0))
    wsp = pl.BlockSpec((D_MODEL, D_MODEL), lambda i: (0, 0))
    act = jax.ShapeDtypeStruct((s_n, D_MODEL), ACT_DTYPE)
    return pl.pallas_call(
        body, grid=(s_n // tm,),
        in_specs=[row, wsp, wsp, wsp, row, row,
                  pl.BlockSpec((tm, D_MODEL), lambda i: (i, 8)), pl.BlockSpec((tm, D_MODEL), lambda i: (i, 9)),
                  pl.BlockSpec((1, 2 * D_MODEL), lambda i: (0, 0)), row,
                  pl.BlockSpec((tm, D_MODEL), lambda i: (i, 7)), pl.BlockSpec((D_MODEL, LANES), lambda i: (0, 0))],
        out_specs=[row, row, row, row, pl.BlockSpec((tm, LANES), lambda i: (i, 0)),
                   pl.BlockSpec((3, tm, D_MODEL), lambda i: (0, i, 0)),
                   pl.BlockSpec((1, 2 * D_MODEL), lambda i: (0, 0))],
        out_shape=[act, act, act, act, jax.ShapeDtypeStruct((s_n, LANES), F32),
                   jax.ShapeDtypeStruct((3, s_n, D_MODEL), ACT_DTYPE),
                   jax.ShapeDtypeStruct((1, 2 * D_MODEL), F32)],
        name="merge_bwd",
    )(dh2b, wo, woc, woa, yc, ya, proj, proj, b_merge, o, proj, _head_sum_matrix())


def _mm_lhs_resident(a, b, tn, name):
    m_n, k_n = a.shape
    n_n = b.shape[1]

    def body(a_ref, b_ref, o_ref):
        o_ref[...] = _dot(a_ref[...], b_ref[...])

    return pl.pallas_call(
        body, grid=(n_n // tn,),
        in_specs=[pl.BlockSpec((m_n, k_n), lambda n: (0, 0)), pl.BlockSpec((k_n, tn), lambda n: (0, n))],
        out_specs=pl.BlockSpec((m_n, tn), lambda n: (0, n)),
        out_shape=jax.ShapeDtypeStruct((m_n, n_n), F32),
        name=name,
    )(a, b)


def _conv_bwd(proj, conv_w, dhc):
    s_n = proj.shape[0]
    lr = s_n // N_RES
    pv = proj.reshape(N_RES, lr, IN_COLS)

    def body(xc_ref, bg_ref, cg_ref, zc_ref, w_ref, dhc_ref, da4_ref, dw_ref, dc_ref):
        w = w_ref[...]
        row = lax.broadcasted_iota(jnp.int32, (lr, LANES), 0)
        dw = [jnp.zeros((1, LANES), F32) for _ in range(3)]
        for r in range(N_RES):
            a, am1, am2 = _conv_terms(xc_ref, cg_ref, r, row, lr)
            c = w[0:1] * am2 + w[1:2] * am1 + w[2:3] * a
            z = zc_ref[r].astype(F32)
            sg = _sigmoid(z)
            sz = z * sg
            bg = bg_ref[r].astype(F32)
            dh = dhc_ref[r].astype(F32)
            da4_ref[1, r] = (dh * sz * c).astype(da4_ref.dtype)
            da4_ref[3, r] = (dh * bg * c * sg * (1.0 + z * (1.0 - sg))).astype(da4_ref.dtype)
            dc = dh * sz * bg
            dc_ref[r] = dc
            dw[0] = dw[0] + jnp.sum(dc * am2, axis=0, keepdims=True)
            dw[1] = dw[1] + jnp.sum(dc * am1, axis=0, keepdims=True)
            dw[2] = dw[2] + jnp.sum(dc * a, axis=0, keepdims=True)
        dw_ref[0:1, :] = dw[0]
        dw_ref[1:2, :] = dw[1]
        dw_ref[2:3, :] = dw[2]

        def shift_up(v):
            return jnp.where(row < lr - 1, pltpu.roll(v, lr - 1, 0), 0.0)

        for r in range(N_RES):
            dp1 = dc_ref[r + 1] if r + 1 < N_RES else shift_up(dc_ref[0])
            dp2 = dc_ref[r + 2] if r + 2 < N_RES else shift_up(dc_ref[r + 2 - N_RES])
            da = w[2:3] * dc_ref[r] + w[1:2] * dp1 + w[0:1] * dp2
            da4_ref[0, r] = (da * cg_ref[r].astype(F32)).astype(da4_ref.dtype)
            da4_ref[2, r] = (da * xc_ref[r].astype(F32)).astype(da4_ref.dtype)

    def col(part):
        return pl.BlockSpec((N_RES, lr, LANES), lambda j: (0, 0, part * 8 + j))

    da4, dw = pl.pallas_call(
        body, grid=(D_MODEL // LANES,),
        in_specs=[col(0), col(1), col(2), col(3), pl.BlockSpec((3, LANES), lambda j: (0, j)),
                  pl.BlockSpec((N_RES, lr, LANES), lambda j: (0, 0, j))],
        out_specs=[pl.BlockSpec((4, N_RES, lr, LANES), lambda j: (0, 0, 0, j)),
                   pl.BlockSpec((3, LANES), lambda j: (0, j))],
        out_shape=[jax.ShapeDtypeStruct((4, N_RES, lr, D_MODEL), ACT_DTYPE),
                   jax.ShapeDtypeStruct((3, D_MODEL), F32)],
        scratch_shapes=[pltpu.VMEM((N_RES, lr, LANES), F32)],
        name="conv_bwd",
    )(pv, pv, pv, pv, conv_w, dhc.reshape(N_RES, lr, D_MODEL))
    return da4.reshape(4, s_n, D_MODEL), dw


def _attn_bwd(proj, dout, lse, dsum, slopes, d):
    g_n, rq = PATTERNS[d]
    un = _Units(g_n, rq)
    s_n = proj.shape[0]
    lr = s_n // N_RES
    nb = lr // rq
    q_n = g_n * rq
    d0, m0, d1, m1 = (np.ascontiguousarray(t.T) for t in _attn_tables(d))
    k0_rows = d0.shape[0] // g_n
    first, n_more = _batches(nb, rq, BWD_BATCH[d], un.per_res)
    cnt = len(first)
    bsz = RES_PER_STEP if un.per_res else cnt
    gd = RES_PER_STEP if un.per_res else g_n

    def body(sl_ref, q_ref, k_ref, v_ref, do_ref, lse_ref, ds_ref, d0_ref, m0_ref, d1_ref, m1_ref, out_ref,
             b0_ref, b1_ref, lt_ref, dt_ref, dk_ref, dv_ref):
        hp = pl.program_id(1)
        for h in (0, 1):
            slope = sl_ref[2 * hp + h]
            b0_ref[:, h * q_n:(h + 1) * q_n] = m0_ref[...] - slope * d0_ref[...]
            b1_ref[:, h * q_n:(h + 1) * q_n] = m1_ref[...] - slope * d1_ref[...]
        dk_ref[...] = jnp.zeros(dk_ref.shape, F32)
        dv_ref[...] = jnp.zeros(dv_ref.shape, F32)
        low = lax.broadcasted_iota(jnp.int32, (1, q_n, LANES), 2) < HEAD_DIM
        row16 = pl.multiple_of(16 * hp, 16)

        def query_rows(stat_ref, t_ref, q_starts):
            tiles = un.load(stat_ref, q_starts, rq)
            for b in range(bsz):
                t_ref[b] = tiles[b].T
            t16 = t_ref[:, pl.ds(row16, 16), :]
            return jnp.concatenate([t16[:, 0:1, :], t16[:, 8:9, :]], axis=2)

        def batch(q_starts, k_starts, k_rows, bias):
            qq = _stack_heads(un.load(q_ref, q_starts, rq) * 0.125, low)
            dd = _stack_heads(un.load(do_ref, q_starts, rq), low)
            ks = un.load(k_ref, k_starts, k_rows)
            vs = un.load(v_ref, k_starts, k_rows)
            lrow = query_rows(lse_ref, lt_ref, q_starts)
            drow = query_rows(ds_ref, dt_ref, q_starts)
            pt = jnp.exp(_bdot(ks, qq, _BNT) + bias - lrow)
            dst = pt * (_bdot(vs, dd, _BNT) - drow)
            un.store(dv_ref, k_starts, k_rows, _bdot(pt, dd, _BNN), add=True)
            un.store(dk_ref, k_starts, k_rows, _bdot(dst, qq, _BNN), add=True)
            dq = _bdot(jnp.swapaxes(dst, 1, 2), ks, _BNN)
            un.store(out_ref, q_starts, rq, jnp.where(low, dq[:, :q_n], dq[:, q_n:]) * 0.125, lead=(0,))

        if un.per_res:
            batch([0], [0], k0_rows, b0_ref[...][None])
        else:
            bias = jnp.concatenate([b0_ref[...][None]] + [b1_ref[...][None]] * (cnt - 1), axis=0)
            batch([q for q, _ in first], [k for _, k in first], 2 * rq, bias)

        def more(j, carry):
            n0 = j * cnt
            qs = [pl.multiple_of((n0 + i) * rq, rq) for i in range(cnt)]
            ks = [pl.multiple_of((n0 + i - 1) * rq, rq) for i in range(cnt)]
            batch(qs, ks, 2 * rq, b1_ref[...][None])
            return carry

        lax.fori_loop(1, 1 + n_more, more, 0)
        out_ref[1] = dk_ref[...].astype(out_ref.dtype)
        out_ref[2] = dv_ref[...].astype(out_ref.dtype)

    pv = _pattern_view(proj, g_n)
    full = lambda a: pl.BlockSpec(a.shape, lambda r, hp: (0, 0))
    out = pl.pallas_call(
        body, grid=_pattern_grid(g_n),
        in_specs=[pl.BlockSpec(memory_space=pltpu.SMEM),
                  _pattern_spec(g_n, lr, lambda hp: 32 + hp),
                  _pattern_spec(g_n, lr, lambda hp: 40 + hp),
                  _pattern_spec(g_n, lr, lambda hp: 48 + hp),
                  _pattern_spec(g_n, lr, lambda hp: hp),
                  _pattern_spec(g_n, lr, lambda hp: 0),
                  _pattern_spec(g_n, lr, lambda hp: 0),
                  full(d0), full(m0), full(d1), full(m1)],
        out_specs=_pattern_spec(g_n, lr, lambda hp: hp, lead=(3,)),
        out_shape=jax.ShapeDtypeStruct(_pattern_view_shape(s_n, D_MODEL, g_n, lead=(3,)), ACT_DTYPE),
        scratch_shapes=[pltpu.VMEM((d0.shape[0], 2 * q_n), F32), pltpu.VMEM((2 * q_n, 2 * q_n), F32),
                        pltpu.VMEM((bsz, LANES, q_n), F32), pltpu.VMEM((bsz, LANES, q_n), F32),
                        pltpu.VMEM((gd, lr, LANES), F32), pltpu.VMEM((gd, lr, LANES), F32)],
        name=f"attn_bwd_d{d}",
    )(slopes, pv, pv, pv, _pattern_view(dout, g_n), _pattern_view(lse, g_n), _pattern_view(dsum, g_n),
      d0, m0, d1, m1)
    return out.reshape(3, s_n, D_MODEL)


def _sum3(a, b, c):
    _, s_n, c_n = a.shape
    tm = 512

    def body(a_ref, b_ref, c_ref, o_ref):
        o_ref[...] = (a_ref[...] + b_ref[...] + c_ref[...]).astype(o_ref.dtype)

    spec = pl.BlockSpec((1, tm, c_n), lambda p, i: (p, i, 0))
    return pl.pallas_call(
        body, grid=(3, s_n // tm), in_specs=[spec] * 3, out_specs=spec,
        out_shape=jax.ShapeDtypeStruct(a.shape, ACT_DTYPE), name="sum_dqkv",
    )(a, b, c)


def _part_index(step, per, lo, n):
    return jnp.clip(step // per - lo, 0, n - 1)


def _dw_in(ut, da4, dc3, db3):
    s_n = ut.shape[1]
    tn = 256
    per = D_MODEL // tn
    shard_blocks = SHARD_COLS // tn

    def body(a_ref, p0_ref, p1_ref, p2_ref, o_ref):
        part = pl.program_id(0) // per

        @pl.when(part < 4)
        def _():
            o_ref[...] = _dot(a_ref[...], p0_ref[...])

        @pl.when((part >= 4) & (part < 7))
        def _():
            o_ref[...] = _dot(a_ref[...], p1_ref[...])

        @pl.when(part >= 7)
        def _():
            o_ref[...] = _dot(a_ref[...], p2_ref[...])

    def pspec(lo, n):
        return pl.BlockSpec((None, s_n, tn), lambda j: (_part_index(j, per, lo, n), 0, j % per))

    return pl.pallas_call(
        body, grid=(IN_COLS // tn,),
        in_specs=[pl.BlockSpec((D_MODEL, s_n), lambda j: (0, 0)), pspec(0, 4), pspec(4, 3), pspec(7, 3)],
        out_specs=pl.BlockSpec((None, D_MODEL, tn), lambda j: (j // shard_blocks, 0, j % shard_blocks)),
        out_shape=jax.ShapeDtypeStruct((4, D_MODEL, SHARD_COLS), F32),
        name="dw_in",
    )(ut, da4, dc3, db3)


def _input_grad(da4, dc3, db3, w4, xp, norm_g, dh2, row0, rows):
    tm, tk = 1024, 512
    per = D_MODEL // tk
    nk = IN_COLS // tk
    shard_blocks = SHARD_COLS // tk
    m0 = row0 // tm

    def body(p0_ref, p1_ref, p2_ref, w_ref, x_ref, g_ref, dh_ref, gx_ref, dg_ref, acc_ref):
        m_i, k_i = pl.program_id(0), pl.program_id(1)
        part = k_i // per

        @pl.when(k_i == 0)
        def _():
            acc_ref[...] = jnp.zeros(acc_ref.shape, F32)

        @pl.when((m_i == 0) & (k_i == 0))
        def _():
            dg_ref[...] = jnp.zeros(dg_ref.shape, F32)

        @pl.when(part < 4)
        def _():
            acc_ref[...] += _dot_nt(p0_ref[...], w_ref[...])

        @pl.when((part >= 4) & (part < 7))
        def _():
            acc_ref[...] += _dot_nt(p1_ref[...], w_ref[...])

        @pl.when(part >= 7)
        def _():
            acc_ref[...] += _dot_nt(p2_ref[...], w_ref[...])

        @pl.when(k_i == nk - 1)
        def _():
            du = acc_ref[...]
            x = x_ref[...]
            r = lax.rsqrt(jnp.mean(x * x, axis=-1, keepdims=True) + EPS)
            nrm = x * r
            dg_ref[...] += jnp.sum(du * nrm, axis=0, keepdims=True)
            dn = du * g_ref[...]
            gx_ref[...] = dh_ref[...] + r * (dn - nrm * jnp.mean(dn * nrm, axis=-1, keepdims=True))

    def pspec(lo, n):
        return pl.BlockSpec((None, tm, tk), lambda m, k: (_part_index(k, per, lo, n), m0 + m, k % per))

    row_in = pl.BlockSpec((tm, D_MODEL), lambda m, k: (m0 + m, 0))
    vec = pl.BlockSpec((1, D_MODEL), lambda m, k: (0, 0))
    return pl.pallas_call(
        body, grid=(rows // tm, nk),
        in_specs=[pspec(0, 4), pspec(4, 3), pspec(7, 3),
                  pl.BlockSpec((None, D_MODEL, tk), lambda m, k: (k // shard_blocks, 0, k % shard_blocks)),
                  row_in, vec, row_in],
        out_specs=[pl.BlockSpec((tm, D_MODEL), lambda m, k: (m, 0)), vec],
        out_shape=[jax.ShapeDtypeStruct((rows, D_MODEL), F32), jax.ShapeDtypeStruct((1, D_MODEL), F32)],
        scratch_shapes=[pltpu.VMEM((tm, D_MODEL), F32)],
        name="input_grad",
    )(da4, dc3, db3, w4, xp, norm_g, dh2)


class _Step:
    def __init__(self, x, tgt, norm_g, after=0.0):
        self.norm_g = norm_g
        self.slopes = _alibi_slopes()
        self.xp, self.tp = _to_residue_major(x, tgt, after)
        self.u, self.ut = _rms_in(self.xp, norm_g)

    def mixers(self, w4, taps):
        self.w4, self.taps = w4, taps
        self.proj = _in_proj(self.u, w4)
        self.hc, self.hct = _conv_fwd(self.proj, taps)
        fwd = [_attn_fwd(self.proj, self.slopes, d) for d in PATTERNS]
        self.o, self.lse, self.ha, self.hat = _attn_combine([f[0] for f in fwd], [f[1] for f in fwd], self.proj)

    def merge_and_loss(self, woc, woa, wo, b_merge, final_g):
        self.woc, self.woa, self.wo, self.b_merge = woc, woa, wo, b_merge
        (self.yc, self.ya, _, self.mgt, self.dh2, self.dh2b, self.d_final_g, self.loss8) = _merge_loss(
            self.hc, self.ha, woc, woa, wo, self.proj, b_merge, self.xp, final_g, self.tp)

    def out_weight_grads(self):
        (dyc, dya, self.dhc, self.dout, self.dsum, self.db3, self.d_bias) = _merge_bwd(
            self.dh2b, self.wo, self.woc, self.woa, self.yc, self.ya, self.proj, self.b_merge, self.o)
        d_wo = _mm_lhs_resident(self.mgt, self.dh2b, 256, "dw_o")
        d_woc = _mm_lhs_resident(self.hct, dyc, 256, "dw_out_conv")
        d_woa = _mm_lhs_resident(self.hat, dya, 256, "dw_out_attn")
        return d_woc, d_woa, d_wo

    def conv_grads(self, after=0.0):
        self.da4, self.d_taps = _conv_bwd(self.proj, self.taps + after, self.dhc)

    def in_weight_grad(self, after=0.0):
        slopes = self.slopes + after
        self.dc3 = _sum3(*[_attn_bwd(self.proj, self.dout, self.lse, self.dsum, slopes, d) for d in PATTERNS])
        return _dw_in(self.ut, self.da4, self.dc3, self.db3)

    def input_grad(self, half, after=0.0):
        rows = self.xp.shape[0] // 2
        return _input_grad(self.da4, self.dc3, self.db3, self.w4, self.xp, self.norm_g + after, self.dh2,
                           half * rows, rows)


def _local_grads(x, tgt, norm_g, w4, b_merge, conv_w, woc, woa, wo, final_g):
    st = _Step(x, tgt, norm_g)
    st.mixers(w4, conv_w)
    st.merge_and_loss(woc, woa, wo, b_merge, final_g)
    d_woc, d_woa, d_wo = st.out_weight_grads()
    st.conv_grads()
    d_w4 = st.in_weight_grad()
    gx_lo, dg_lo = st.input_grad(0)
    gx_hi, dg_hi = st.input_grad(1)
    return (st.loss8, _to_natural(gx_lo, gx_hi), dg_lo + dg_hi, d_w4, st.d_bias, st.d_taps, d_woc, d_woa, d_wo,
            st.d_final_g)


MESH = pl.DeviceIdType.MESH
_CHIP_FLIPS = ((1, 0), (0, 1), (1, 1))
_ANY = pl.BlockSpec(memory_space=pl.ANY)


def _place():
    return lax.axis_index("x"), lax.axis_index("y"), lax.axis_index("c")


def _flip(v, f):
    return 1 - v if f else v


def _remote(src, dst, send_sems, recv_sems, k, device):
    return pltpu.make_async_remote_copy(src_ref=src, dst_ref=dst, send_sem=send_sems.at[k], recv_sem=recv_sems.at[k],
                                        device_id=device, device_id_type=MESH)


def _place_shard(w, chip):
    rows, cols = w.shape
    tm = 128

    def body(chip_ref, w_ref, o_ref):
        o_ref[0] = w_ref[...].astype(o_ref.dtype)

    return pl.pallas_call(
        body,
        grid_spec=pltpu.PrefetchScalarGridSpec(
            num_scalar_prefetch=1, grid=(rows // tm,),
            in_specs=[pl.BlockSpec((tm, cols), lambda i, chip_ref: (i, 0))],
            out_specs=pl.BlockSpec((1, tm, cols), lambda i, chip_ref: (chip_ref[0], i, 0))),
        out_shape=jax.ShapeDtypeStruct((4, rows, cols), MXU_DTYPE),
        name="place_shard",
    )(chip, w)


def _gather_copies(arrs, _, send_sems, recv_sems):
    x, y, c = _place()
    out = []
    for a, arr in enumerate(arrs):
        h = arr.shape[1] // 2
        mine = arr.at[2 * x + y, pl.ds(pl.multiple_of(c * h, 8), h)]
        for t, (fx, fy) in enumerate(_CHIP_FLIPS):
            out.append(_remote(mine, mine, send_sems, recv_sems, 3 * a + t, (_flip(x, fx), _flip(y, fy), c)))
    return out


def _forward_to_sibling(arrs):
    n = len(arrs)

    def body(*refs):
        outs = refs[n:2 * n]
        send_sems, recv_sems = refs[2 * n:]
        x, y, c = _place()
        sibling = (x, y, 1 - c)
        started = []
        for a in range(n):
            h = outs[a].shape[1] // 2
            rows = pl.ds(pl.multiple_of(c * h, 8), h)
            for t, (fx, fy) in enumerate(_CHIP_FLIPS):
                landed = outs[a].at[2 * _flip(x, fx) + _flip(y, fy), rows]
                cp = _remote(landed, landed, send_sems, recv_sems, 3 * a + t, sibling)
                cp.start()
                started.append(cp)
        for a in range(n):
            h = outs[a].shape[1] // 2
            rows = pl.ds(pl.multiple_of((1 - c) * h, 8), h)
            for t, (fx, fy) in enumerate(_CHIP_FLIPS):
                handed = outs[a].at[2 * _flip(x, fx) + _flip(y, fy), rows]
                _remote(handed, handed, send_sems, recv_sems, 3 * a + t, sibling).wait_recv()
        for cp in started:
            cp.wait_send()

    return pl.pallas_call(
        body, in_specs=[_ANY] * n, out_specs=[_ANY] * n,
        out_shape=[jax.ShapeDtypeStruct(s.shape, s.dtype) for s in arrs],
        input_output_aliases={a: a for a in range(n)},
        scratch_shapes=[pltpu.SemaphoreType.DMA((3 * n,)), pltpu.SemaphoreType.DMA((3 * n,))],
        name="gathered_to_sibling",
    )(*arrs)


_HBM = pl.BlockSpec(memory_space=pltpu.HBM)
_SEM = pl.BlockSpec(memory_space=pltpu.SEMAPHORE)
_EFFECT = pltpu.SideEffectType.DATAFLOW_SIDE_EFFECTING


class _SplitExchange:
    def __init__(self, name, srcs, land_shapes, n_copies, copies, riders=()):
        self.name, self.n, self.nl, self.copies = name, len(srcs), len(land_shapes), copies
        n, nb = self.n, len(srcs) + len(land_shapes)
        lands = [lax.empty(s.shape, s.dtype) for s in land_shapes]
        bufs = [pltpu.with_memory_space_constraint(a, pltpu.HBM) for a in (*srcs, *lands, *riders)]
        na = len(bufs)

        def body(*refs):
            send_sems, recv_sems = refs[na], refs[na + 1]
            for cp in copies(refs[:n], refs[n:nb], send_sems, recv_sems):
                cp.start()
            refs[-1][...] = jnp.zeros(refs[-1].shape, F32)

        outs = pl.pallas_call(
            body, name=name + "_start",
            in_specs=[_HBM] * na,
            out_specs=[_SEM, _SEM] + [_HBM] * na + [pl.BlockSpec(memory_space=pltpu.VMEM)],
            out_shape=[pltpu.SemaphoreType.DMA((n_copies,)), pltpu.SemaphoreType.DMA((n_copies,))]
            + [pltpu.HBM(b.shape, b.dtype) for b in bufs] + [jax.ShapeDtypeStruct((8, LANES), F32)],
            input_output_aliases={i: 2 + i for i in range(na)},
            compiler_params=pltpu.CompilerParams(has_side_effects=_EFFECT),
        )(*bufs)
        self.sems, self.bufs, self.riders, self.token = outs[:2], outs[2:2 + nb], outs[2 + nb:2 + na], outs[-1]

    def after(self):
        return self.token[0, 0]

    def wait(self, done, riders=()):
        n, nb, copies = self.n, self.n + self.nl, self.copies
        bufs = [*self.bufs, *[pltpu.with_memory_space_constraint(a, pltpu.HBM) for a in riders]]
        na = len(bufs)

        def body(*refs):
            send_sems, recv_sems = refs[na], refs[na + 1]
            for cp in copies(refs[:n], refs[n:nb], send_sems, recv_sems):
                cp.wait_send()
                cp.wait_recv()

        outs = pl.pallas_call(
            body, name=self.name + "_wait",
            in_specs=[_HBM] * na + [_SEM, _SEM, _ANY],
            out_specs=[_HBM] * na,
            out_shape=[pltpu.HBM(b.shape, b.dtype) for b in bufs],
            input_output_aliases={i: i for i in range(na)},
            compiler_params=pltpu.CompilerParams(has_side_effects=_EFFECT),
        )(*bufs, *self.sems, done)
        return outs[:n], outs[n:nb], outs[nb:]


def _sibling_copies(srcs, lands, send_sems, recv_sems):
    x, y, c = _place()
    out = []
    for a, (src, land) in enumerate(zip(srcs, lands)):
        h = src.shape[1] // 2
        theirs = pl.ds(pl.multiple_of((1 - c) * h, 8), h)
        out.append(_remote(src.at[:, theirs], land, send_sems, recv_sems, a, (x, y, 1 - c)))
    return out


def _grads_to_sibling(name, grads):
    shapes = [jax.ShapeDtypeStruct((4, g.shape[1] // 2, g.shape[2]), g.dtype) for g in grads]
    return _SplitExchange(name, grads, shapes, len(grads), _sibling_copies)


def _chip_copies(srcs, lands, send_sems, recv_sems):
    x, y, c = _place()
    out = []
    for a, (src, land) in enumerate(zip(srcs, lands)):
        for t, (fx, fy) in enumerate(_CHIP_FLIPS):
            tx, ty = _flip(x, fx), _flip(y, fy)
            out.append(_remote(src.at[2 * tx + ty], land.at[t], send_sems, recv_sems, 3 * a + t, (tx, ty, c)))
    return out


def _grads_to_chips(name, parts):
    shapes = [jax.ShapeDtypeStruct((3, *p.shape[1:]), p.dtype) for p in parts]
    return _SplitExchange(name, parts, shapes, 3 * len(parts), _chip_copies)


def _add_halves(g, r, half):
    _, rows, cols = g.shape
    h = rows // 2
    tm = min(h, 128)
    nt = h // tm

    def body(half_ref, g_ref, r_ref, f_ref, b_ref):
        s = g_ref[...] + r_ref[...]
        f_ref[...] = s
        b_ref[...] = s.astype(b_ref.dtype)

    spec = pl.BlockSpec((1, tm, cols), lambda j, i, half_ref: (j, i, 0))
    return pl.pallas_call(
        body,
        grid_spec=pltpu.PrefetchScalarGridSpec(
            num_scalar_prefetch=1, grid=(4, nt),
            in_specs=[pl.BlockSpec((1, tm, cols), lambda j, i, half_ref: (j, half_ref[0] * nt + i, 0)), spec],
            out_specs=[spec, spec]),
        out_shape=[jax.ShapeDtypeStruct((4, h, cols), F32), jax.ShapeDtypeStruct((4, h, cols), BF16)],
        name="add_sibling_grads",
    )(half, g, r)


def _add_chips(own, recv, where):
    _, h, cols = own.shape
    tm = min(h, 128)
    nt = h // tm

    def body(where_ref, o_ref, r_ref, out_ref):
        out_ref[...] = ((o_ref[0] + r_ref[0].astype(F32)) + r_ref[1].astype(F32)) + r_ref[2].astype(F32)

    return pl.pallas_call(
        body,
        grid_spec=pltpu.PrefetchScalarGridSpec(
            num_scalar_prefetch=1, grid=(nt,),
            in_specs=[pl.BlockSpec((1, tm, cols), lambda i, where_ref: (where_ref[0], i, 0)),
                      pl.BlockSpec((3, tm, cols), lambda i, where_ref: (0, i, 0))],
            out_specs=pl.BlockSpec((tm, cols), lambda i, where_ref: (where_ref[1] * nt + i, 0))),
        out_shape=jax.ShapeDtypeStruct((2 * h, cols), F32),
        name="add_chip_grads",
    )(where, own, recv)


def _share_halves(shards):
    n = len(shards)

    def body(*refs):
        outs = refs[n:2 * n]
        send_sems, recv_sems = refs[2 * n:]
        x, y, c = _place()
        copies = []
        for a in range(n):
            h = outs[a].shape[0] // 2
            mine = outs[a].at[pl.ds(pl.multiple_of(c * h, 8), h)]
            copies.append(_remote(mine, mine, send_sems, recv_sems, a, (x, y, 1 - c)))
        for cp in copies:
            cp.start()
        for a, cp in enumerate(copies):
            cp.wait_send()
            h = outs[a].shape[0] // 2
            theirs = outs[a].at[pl.ds(pl.multiple_of((1 - c) * h, 8), h)]
            _remote(theirs, theirs, send_sems, recv_sems, a, (x, y, 1 - c)).wait_recv()

    return pl.pallas_call(
        body, in_specs=[_ANY] * n, out_specs=[_ANY] * n,
        out_shape=[jax.ShapeDtypeStruct(p.shape, p.dtype) for p in shards],
        input_output_aliases={a: a for a in range(n)},
        scratch_shapes=[pltpu.SemaphoreType.DMA((n,)), pltpu.SemaphoreType.DMA((n,))],
        name="share_reduced_halves",
    )(*shards)


def _exchange_small(rows, reduce):
    cols = rows[0].shape[1]
    n = len(rows)
    assert sum(r.shape[0] for r in rows) <= 8

    def body(*refs):
        ins, out_ref = refs[:n], refs[n]
        vec_ref, gath_ref, send_sems, recv_sems = refs[n + 1:]
        x, y, c = _place()
        me = 4 * x + 2 * y + c
        vec_ref[...] = jnp.zeros(vec_ref.shape, F32)
        at = 0
        for r in ins:
            vec_ref[at:at + r.shape[0], :] = r[...]
            at += r.shape[0]
        copies = []
        for k in range(1, 8):
            peer = (_flip(x, (k >> 2) & 1), _flip(y, (k >> 1) & 1), _flip(c, k & 1))
            copies.append(_remote(vec_ref, gath_ref.at[me], send_sems, recv_sems, k - 1, peer))
        for cp in copies:
            cp.start()
        gath_ref[me] = vec_ref[...]
        for cp in copies:
            cp.wait()
        if reduce:
            tot = gath_ref[0]
            for dev in range(1, 8):
                tot = tot + gath_ref[dev]
            out_ref[...] = tot
            out_ref[7:8, :] = jnp.zeros((1, cols), F32) + jnp.sum(tot[7:8, :])
        else:
            out_ref[...] = gath_ref[...]

    vm = pl.BlockSpec(memory_space=pltpu.VMEM)
    return pl.pallas_call(
        body, in_specs=[vm] * n, out_specs=vm,
        out_shape=jax.ShapeDtypeStruct((8, cols) if reduce else (8, 8, cols), F32),
        scratch_shapes=[pltpu.VMEM((8, cols), F32), pltpu.VMEM((8, 8, cols), F32),
                        pltpu.SemaphoreType.DMA((7,)), pltpu.SemaphoreType.DMA((7,))],
        name="reduce_small" if reduce else "gather_small",
    )(*rows)


def _adamw(w, g, m, v, name):
    rows, cols = w.shape
    tm = 128 if rows % 128 == 0 else rows

    def body(w_ref, g_ref, m_ref, v_ref, d_ref, m2_ref, v2_ref):
        gr = g_ref[...]
        m2 = ADAM_B1 * m_ref[...] + (1.0 - ADAM_B1) * gr
        v2 = ADAM_B2 * v_ref[...] + (1.0 - ADAM_B2) * (gr * gr)
        m_hat = m2 / (1.0 - ADAM_B1 ** ADAM_STEP)
        v_hat = v2 / (1.0 - ADAM_B2 ** ADAM_STEP)
        d_ref[...] = -ADAM_LR * (m_hat / (jnp.sqrt(v_hat) + ADAM_EPS) + ADAM_WD * w_ref[...])
        m2_ref[...] = m2
        v2_ref[...] = v2

    spec = pl.BlockSpec((tm, cols), lambda i: (i, 0))
    sds = jax.ShapeDtypeStruct((rows, cols), F32)
    return pl.pallas_call(body, grid=(rows // tm,), in_specs=[spec] * 4, out_specs=[spec] * 3,
                          out_shape=[sds] * 3, name=name)(w, g, m, v)


def kernel(x, norm_g, w_in, b_merge, conv_w, w_out_conv, w_out_attn, w_o, final_g, loss_target, m_norm_g, m_w_in, m_b_merge, m_conv_w, m_w_out_conv, m_w_out_attn, m_w_o, m_final_g, v_norm_g, v_w_in, v_b_merge, v_conv_w, v_w_out_conv, v_w_out_attn, v_w_o, v_final_g):
    mx, my, mc = _place()
    chip = (2 * mx + my).astype(jnp.int32)
    seq = x.shape[1]

    chip1 = chip.reshape(1)
    slots = [_place_shard(w[0], chip1) for w in (w_in, w_out_conv, w_out_attn, w_o)]
    taps8 = _exchange_small([conv_w[0]], reduce=False)
    taps = jnp.concatenate([taps8[2 * j, :3, :] for j in range(4)], axis=1)
    gather_in = _SplitExchange("gather_w_in", slots[:1], [], 3, _gather_copies)
    st = _Step(x[0], loss_target[0], norm_g, after=gather_in.after())
    (w4,), _, out_slots = gather_in.wait(st.ut, riders=slots[1:])
    gather_out = _SplitExchange("gather_w_out", out_slots, [], 9, _gather_copies, riders=[w4])
    (w4,) = _forward_to_sibling(gather_out.riders)
    st.mixers(w4, taps)
    out_ws, _, _ = gather_out.wait(st.o)
    woc, woa, wo = [w.reshape(D_MODEL, D_MODEL) for w in _forward_to_sibling(out_ws)]
    st.merge_and_loss(woc, woa, wo, b_merge, final_g.reshape(1, D_MODEL))

    half = mc.astype(jnp.int32).reshape(1)
    where = jnp.stack([chip, mc.astype(jnp.int32)])
    out_grads = [g.reshape(4, -1, D_MODEL) for g in st.out_weight_grads()]
    to_sibling = _grads_to_sibling("out_grads_to_sibling", out_grads)
    st.conv_grads(after=to_sibling.after())
    out_grads, from_sibling, _ = to_sibling.wait(st.da4)
    out_partial = [_add_halves(g, r, half) for g, r in zip(out_grads, from_sibling)]
    to_chips = _grads_to_chips("out_grads_to_chips", [p[1] for p in out_partial])
    d_w4 = st.in_weight_grad(after=to_chips.after())
    out_reduced = [_add_chips(p[0], r, where) for p, r in zip(out_partial, to_chips.wait(st.dc3)[1])]

    to_sibling = _grads_to_sibling("in_grad_to_sibling", [d_w4])
    gx_lo, dg_lo = st.input_grad(0, after=to_sibling.after())
    (d_w4,), (from_sibling,), _ = to_sibling.wait(gx_lo)
    in_partial = _add_halves(d_w4, from_sibling, half)
    to_chips = _grads_to_chips("in_grad_to_chips", [in_partial[1]])
    gx_hi, dg_hi = st.input_grad(1, after=to_chips.after())
    grad_x = _to_natural(gx_lo, gx_hi)
    in_reduced = _add_chips(in_partial[0], to_chips.wait(grad_x)[1][0], where)
    g_w_in, g_woc, g_woa, g_wo = _share_halves([in_reduced] + out_reduced)

    small = _exchange_small([dg_lo + dg_hi, st.d_bias.reshape(2, D_MODEL), st.d_taps, st.d_final_g,
                             st.loss8.reshape(1, D_MODEL)], reduce=True)
    loss = (0.5 / D_MODEL) * small[7, 0]
    g_norm_g = small[0:1]
    g_bias = small[1:3].reshape(1, 2 * D_MODEL)
    g_taps = lax.dynamic_slice(small[3:6], (0, chip * (D_MODEL // 4)), (3, D_MODEL // 4))
    g_final_g = small[6:7]

    upd = [
        _adamw(norm_g, g_norm_g, m_norm_g, v_norm_g, "adamw_norm_g"),
        _adamw(w_in[0], g_w_in, m_w_in[0], v_w_in[0], "adamw_w_in"),
        _adamw(b_merge, g_bias, m_b_merge, v_b_merge, "adamw_b_merge"),
        _adamw(conv_w[0], g_taps, m_conv_w[0], v_conv_w[0], "adamw_conv_w"),
        _adamw(w_out_conv[0], g_woc, m_w_out_conv[0], v_w_out_conv[0], "adamw_w_out_conv"),
        _adamw(w_out_attn[0], g_woa, m_w_out_attn[0], v_w_out_attn[0], "adamw_w_out_attn"),
        _adamw(w_o[0], g_wo, m_w_o[0], v_w_o[0], "adamw_w_o"),
        _adamw(final_g.reshape(1, D_MODEL), g_final_g, m_final_g.reshape(1, D_MODEL),
               v_final_g.reshape(1, D_MODEL), "adamw_final_g"),
    ]
    shapes = [norm_g.shape, w_in.shape, b_merge.shape, conv_w.shape, w_out_conv.shape, w_out_attn.shape,
              w_o.shape, final_g.shape]
    grads_out = [g_norm_g, g_w_in, g_bias, g_taps, g_woc, g_woa, g_wo, g_final_g]
    outs = [loss, grad_x.reshape(1, seq, D_MODEL)]
    outs += [g.reshape(s) for g, s in zip(grads_out, shapes)]
    for k in range(3):
        outs += [u[k].reshape(s) for u, s in zip(upd, shapes)]
    return tuple(outs)
```

```python
import functools

import numpy as np
import jax
import jax.numpy as jnp
from jax import lax
from jax.experimental import pallas as pl
from jax.experimental.pallas import tpu as pltpu

F32 = jnp.float32
BF16 = jnp.bfloat16
MXU_DTYPE = jnp.bfloat16
ACT_DTYPE = jnp.bfloat16

D_MODEL = 1024
N_HEADS = 16
HEAD_DIM = 64
QB = 128
N_RES = 16
LANES = 128
HP = N_HEADS * HEAD_DIM // LANES
IN_COLS = 10 * D_MODEL
SHARD_COLS = IN_COLS // 4
EPS = 1e-6
NEG = -1e30

ADAM_LR, ADAM_B1, ADAM_B2, ADAM_EPS, ADAM_WD, ADAM_STEP = 0.001, 0.9, 0.999, 1e-08, 0.01, 10

PATTERNS = {1: (16, 16), 4: (4, 32), 16: (1, 128)}

_NN = (((1,), (0,)), ((), ()))
_NT = (((1,), (1,)), ((), ()))


def _dot(a, b):
    return lax.dot_general(a.astype(MXU_DTYPE), b.astype(MXU_DTYPE), _NN, preferred_element_type=F32)


def _dot_nt(a, b):
    return lax.dot_general(a.astype(MXU_DTYPE), b.astype(MXU_DTYPE), _NT, preferred_element_type=F32)


def _split3(x):
    hi = x.astype(BF16)
    r1 = x - hi.astype(F32)
    mid = r1.astype(BF16)
    lo = (r1 - mid.astype(F32)).astype(BF16)
    return hi, mid, lo


def _select_rows(sel, x):
    return sum(lax.dot_general(sel, t, _NN, preferred_element_type=F32) for t in _split3(x))


def _select_cols(x, sel):
    return sum(lax.dot_general(t, sel, _NN, preferred_element_type=F32) for t in _split3(x))


def _sigmoid(z):
    return 1.0 / (1.0 + jnp.exp(-z))


def _perm_matrix():
    idx = np.arange(256)
    p = np.zeros((256, 256), np.float32)
    p[(idx % 16) * 16 + idx // 16, idx] = 1.0
    return jnp.asarray(p, BF16)


def _head_expand_matrix():
    e = np.zeros((LANES, D_MODEL), np.float32)
    for h in range(N_HEADS):
        e[8 * h, HEAD_DIM * h:HEAD_DIM * (h + 1)] = 1.0
    return jnp.asarray(e, BF16)


def _head_sum_matrix():
    e = np.zeros((D_MODEL, LANES), np.float32)
    for h in range(N_HEADS):
        e[HEAD_DIM * h:HEAD_DIM * (h + 1), 8 * h:8 * (h + 1)] = 1.0
    return jnp.asarray(e, BF16)


def _attn_tables(d):
    g_n, rq = PATTERNS[d]
    q_n = g_n * rq
    gq, iq = np.arange(q_n) // rq, np.arange(q_n) % rq

    def tab(kn, base):
        k_n = g_n * kn
        gk, jk = np.arange(k_n) // kn, np.arange(k_n) % kn
        delta = g_n * (base + iq[:, None] - jk[None, :]) + gq[:, None] - gk[None, :]
        valid = (delta >= 0) & (delta <= QB)
        dist = np.where(valid, d * delta, 0).astype(np.float32)
        madd = np.where(valid, 0.0, NEG).astype(np.float32)
        return dist, madd

    d0, m0 = tab(rq if g_n == 1 else 2 * rq, 0)
    d1, m1 = tab(2 * rq, rq)
    return d0, m0, d1, m1


def _alibi_slopes():
    return jnp.exp2(-8.0 * jnp.arange(1, N_HEADS + 1, dtype=F32) / N_HEADS)


def _to_residue_major(x, tgt, after=0.0):
    s_n, c_n = x.shape
    lr = s_n // N_RES
    pm = (_perm_matrix().astype(F32) + after).astype(BF16)

    def body(p_ref, x_ref, t_ref, xo_ref, to_ref):
        pm = p_ref[...]
        xo_ref[...] = _select_rows(pm, x_ref[...]).reshape(16, 16, c_n)
        to_ref[...] = _select_rows(pm, t_ref[...]).reshape(16, 16, c_n)

    nat = pl.BlockSpec((256, c_n), lambda i: (i, 0))
    res = pl.BlockSpec((16, 16, c_n), lambda i: (0, i, 0))
    xo, to = pl.pallas_call(
        body, grid=(s_n // 256,),
        in_specs=[pl.BlockSpec((256, 256), lambda i: (0, 0)), nat, nat],
        out_specs=[res, res],
        out_shape=[jax.ShapeDtypeStruct((16, lr, c_n), F32)] * 2,
        name="perm_in",
    )(pm, x, tgt)
    return xo.reshape(s_n, c_n), to.reshape(s_n, c_n)


def _to_natural(gx_lo, gx_hi):
    half_rows, c_n = gx_lo.shape
    lr = half_rows // (N_RES // 2)

    def body(p_ref, lo_ref, hi_ref, o_ref):
        g = jnp.concatenate([lo_ref[...], hi_ref[...]], axis=0)
        o_ref[...] = _select_rows(p_ref[...], g.reshape(256, c_n))

    half = pl.BlockSpec((8, 16, c_n), lambda i: (0, i, 0))
    return pl.pallas_call(
        body, grid=(lr // 16,),
        in_specs=[pl.BlockSpec((256, 256), lambda i: (0, 0)), half, half],
        out_specs=pl.BlockSpec((256, c_n), lambda i: (i, 0)),
        out_shape=jax.ShapeDtypeStruct((2 * half_rows, c_n), F32),
        name="perm_out",
    )(_perm_matrix(), gx_lo.reshape(8, lr, c_n), gx_hi.reshape(8, lr, c_n))


def _rms_in(xp, norm_g):
    s_n, c_n = xp.shape
    tm = 512

    def body(x_ref, g_ref, u_ref, ut_ref):
        x = x_ref[...]
        r = lax.rsqrt(jnp.mean(x * x, axis=-1, keepdims=True) + EPS)
        u = x * r * g_ref[...]
        u_ref[...] = u.astype(u_ref.dtype)
        ut_ref[...] = u.T.astype(ut_ref.dtype)

    return pl.pallas_call(
        body, grid=(s_n // tm,),
        in_specs=[pl.BlockSpec((tm, c_n), lambda i: (i, 0)), pl.BlockSpec((1, c_n), lambda i: (0, 0))],
        out_specs=[pl.BlockSpec((tm, c_n), lambda i: (i, 0)), pl.BlockSpec((c_n, tm), lambda i: (0, i))],
        out_shape=[jax.ShapeDtypeStruct((s_n, c_n), ACT_DTYPE), jax.ShapeDtypeStruct((c_n, s_n), ACT_DTYPE)],
        name="rms_in",
    )(xp, norm_g)


def _in_proj(u, w4):
    s_n = u.shape[0]
    tn, cm = 512, 512
    per = SHARD_COLS // tn

    def body(a_ref, b_ref, o_ref):
        b = b_ref[...]
        for c in range(s_n // cm):
            o_ref[c * cm:(c + 1) * cm, :] = _dot(a_ref[c * cm:(c + 1) * cm, :], b).astype(o_ref.dtype)

    return pl.pallas_call(
        body, grid=(IN_COLS // tn,),
        in_specs=[pl.BlockSpec((s_n, D_MODEL), lambda n: (0, 0)),
                  pl.BlockSpec((None, D_MODEL, tn), lambda n: (n // per, 0, n % per))],
        out_specs=pl.BlockSpec((s_n, tn), lambda n: (0, n)),
        out_shape=jax.ShapeDtypeStruct((s_n, IN_COLS), ACT_DTYPE),
        name="in_proj",
    )(u, w4)


def _conv_terms(xc_ref, cg_ref, r, row, lr):
    def a_of(q):
        return cg_ref[q].astype(F32) * xc_ref[q].astype(F32)

    def shift_down(v):
        return jnp.where(row >= 1, pltpu.roll(v, 1, 0), 0.0)

    a = a_of(r)
    am1 = a_of(r - 1) if r >= 1 else shift_down(a_of(N_RES - 1))
    am2 = a_of(r - 2) if r >= 2 else shift_down(a_of(N_RES - 2 + r))
    return a, am1, am2


def _conv_fwd(proj, conv_w):
    s_n = proj.shape[0]
    lr = s_n // N_RES
    pv = proj.reshape(N_RES, lr, IN_COLS)

    def body(xc_ref, bg_ref, cg_ref, zc_ref, w_ref, hc_ref, hct_ref):
        w = w_ref[...]
        row = lax.broadcasted_iota(jnp.int32, (lr, LANES), 0)
        for r in range(N_RES):
            a, am1, am2 = _conv_terms(xc_ref, cg_ref, r, row, lr)
            c = w[0:1] * am2 + w[1:2] * am1 + w[2:3] * a
            z = zc_ref[r].astype(F32)
            hc = z * _sigmoid(z) * bg_ref[r].astype(F32) * c
            hc_ref[r] = hc.astype(hc_ref.dtype)
            hct_ref[:, r * lr:(r + 1) * lr] = hc.T.astype(hct_ref.dtype)

    def col(part):
        return pl.BlockSpec((N_RES, lr, LANES), lambda j: (0, 0, part * 8 + j))

    hc, hct = pl.pallas_call(
        body, grid=(D_MODEL // LANES,),
        in_specs=[col(0), col(1), col(2), col(3), pl.BlockSpec((3, LANES), lambda j: (0, j))],
        out_specs=[pl.BlockSpec((N_RES, lr, LANES), lambda j: (0, 0, j)),
                   pl.BlockSpec((LANES, s_n), lambda j: (j, 0))],
        out_shape=[jax.ShapeDtypeStruct((N_RES, lr, D_MODEL), ACT_DTYPE),
                   jax.ShapeDtypeStruct((D_MODEL, s_n), ACT_DTYPE)],
        name="conv_fwd",
    )(pv, pv, pv, pv, conv_w)
    return hc.reshape(s_n, D_MODEL), hct


RES_PER_STEP = 8
FWD_BATCH = {1: 4, 4: 8, 16: RES_PER_STEP}
BWD_BATCH = {1: 2, 4: 8, 16: RES_PER_STEP}

_BNT = (((2,), (2,)), ((0,), (0,)))
_BNN = (((2,), (1,)), ((0,), (0,)))


def _bdot(a, b, dims):
    return lax.dot_general(a.astype(MXU_DTYPE), b.astype(MXU_DTYPE), dims, preferred_element_type=F32)


def _pattern_view_shape(s_n, c_n, g_n, lead=()):
    lr = s_n // N_RES
    return (*lead, 4, 4, lr, c_n) if g_n == 4 else (*lead, N_RES, lr, c_n)


def _pattern_view(a, g_n, lead=()):
    return a.reshape(_pattern_view_shape(a.shape[-2], a.shape[-1], g_n, lead))


def _pattern_grid(g_n):
    return (N_RES // RES_PER_STEP if g_n == 1 else N_RES // g_n, HP)


def _pattern_spec(g_n, lr, col_of_hp, lead=()):
    z = (0,) * len(lead)
    if g_n == 16:
        return pl.BlockSpec((*lead, 16, lr, LANES), lambda r, hp: (*z, 0, 0, col_of_hp(hp)))
    if g_n == 4:
        return pl.BlockSpec((*lead, 4, None, lr, LANES), lambda r, hp: (*z, 0, r, 0, col_of_hp(hp)))
    return pl.BlockSpec((*lead, RES_PER_STEP, lr, LANES), lambda r, hp: (*z, r, 0, col_of_hp(hp)))


class _Units:
    def __init__(self, g_n, rq):
        self.g_n, self.rq, self.per_res = g_n, rq, g_n == 1

    def load(self, ref, starts, rows):
        if self.per_res:
            return ref[:, pl.ds(starts[0], rows), :]
        return jnp.stack([ref[:, pl.ds(s, rows), :].reshape(self.g_n * rows, LANES) for s in starts])

    def store(self, ref, starts, rows, val, add=False, lead=()):
        for b, s in enumerate(starts):
            idx = (*lead, slice(None), pl.ds(s, rows), slice(None))
            v = val if self.per_res else val[b].reshape(self.g_n, rows, LANES)
            ref[idx] = (ref[idx] + v if add else v).astype(ref.dtype)
            if self.per_res:
                break


def _stack_heads(x, low):
    zero = jnp.zeros_like(x)
    return jnp.concatenate([jnp.where(low, x, zero), jnp.where(low, zero, x)], axis=1)


def _batches(nb, rq, size, per_res):
    if per_res:
        return [(0, 0)], nb - 1
    first = [(n * rq, max(n - 1, 0) * rq) for n in range(min(size, nb))]
    assert nb % len(first) == 0
    return first, nb // len(first) - 1


def _attn_fwd(proj, slopes, d):
    g_n, rq = PATTERNS[d]
    un = _Units(g_n, rq)
    s_n = proj.shape[0]
    lr = s_n // N_RES
    nb = lr // rq
    q_n = g_n * rq
    d0, m0, d1, m1 = _attn_tables(d)
    k0_rows = d0.shape[1] // g_n
    first, n_more = _batches(nb, rq, FWD_BATCH[d], un.per_res)
    bsz = len(first)

    def body(sl_ref, q_ref, k_ref, v_ref, d0_ref, m0_ref, d1_ref, m1_ref, o_ref, lse_ref, b0_ref, b1_ref):
        hp = pl.program_id(1)

        @pl.when(hp == 0)
        def _():
            lse_ref[...] = jnp.zeros(lse_ref.shape, F32)

        for h in (0, 1):
            slope = sl_ref[2 * hp + h]
            b0_ref[h * q_n:(h + 1) * q_n, :] = m0_ref[...] - slope * d0_ref[...]
            b1_ref[h * q_n:(h + 1) * q_n, :] = m1_ref[...] - slope * d1_ref[...]

        lane = lax.broadcasted_iota(jnp.int32, (1, q_n, LANES), 2)
        low = lane < HEAD_DIM
        grp = lane // 8

        def batch(q_starts, k_starts, k_rows, bias):
            qq = _stack_heads(un.load(q_ref, q_starts, rq) * 0.125, low)
            s = _bdot(qq, un.load(k_ref, k_starts, k_rows), _BNT) + bias
            m = jnp.max(s, axis=2, keepdims=True)
            p = jnp.exp(s - m)
            l = jnp.sum(p, axis=2, keepdims=True)
            o = _bdot(p, un.load(v_ref, k_starts, k_rows), _BNN) * (1.0 / l)
            lse = m + jnp.log(l)
            un.store(o_ref, q_starts, rq, jnp.where(low, o[:, :q_n], o[:, q_n:]))
            upd = jnp.where(grp == 2 * hp, lse[:, :q_n], 0.0) + jnp.where(grp == 2 * hp + 1, lse[:, q_n:], 0.0)
            un.store(lse_ref, q_starts, rq, upd, add=True)

        if un.per_res:
            batch([0], [0], k0_rows, b0_ref[...][None])
        else:
            bias = jnp.concatenate([b0_ref[...][None]] + [b1_ref[...][None]] * (bsz - 1), axis=0)
            batch([q for q, _ in first], [k for _, k in first], 2 * rq, bias)

        def more(j, carry):
            n0 = j * bsz
            qs = [pl.multiple_of((n0 + i) * rq, rq) for i in range(bsz)]
            ks = [pl.multiple_of((n0 + i - 1) * rq, rq) for i in range(bsz)]
            batch(qs, ks, 2 * rq, b1_ref[...][None])
            return carry

        lax.fori_loop(1, 1 + n_more, more, 0)

    pv = _pattern_view(proj, g_n)
    full = lambda a: pl.BlockSpec(a.shape, lambda r, hp: (0, 0))
    o, lse = pl.pallas_call(
        body, grid=_pattern_grid(g_n),
        in_specs=[pl.BlockSpec(memory_space=pltpu.SMEM),
                  _pattern_spec(g_n, lr, lambda hp: 32 + hp),
                  _pattern_spec(g_n, lr, lambda hp: 40 + hp),
                  _pattern_spec(g_n, lr, lambda hp: 48 + hp),
                  full(d0), full(m0), full(d1), full(m1)],
        out_specs=[_pattern_spec(g_n, lr, lambda hp: hp), _pattern_spec(g_n, lr, lambda hp: 0)],
        out_shape=[jax.ShapeDtypeStruct(_pattern_view_shape(s_n, D_MODEL, g_n), F32),
                   jax.ShapeDtypeStruct(_pattern_view_shape(s_n, LANES, g_n), F32)],
        scratch_shapes=[pltpu.VMEM((2 * q_n, d0.shape[1]), F32), pltpu.VMEM((2 * q_n, 2 * q_n), F32)],
        name=f"attn_fwd_d{d}",
    )(slopes, pv, pv, pv, d0, m0, d1, m1)
    return o.reshape(s_n, D_MODEL), lse.reshape(s_n, LANES)


def _attn_combine(outs, lses, proj):
    s_n = proj.shape[0]
    tm = 512

    def body(o1_ref, o2_ref, o3_ref, l1_ref, l2_ref, l3_ref, za_ref, e_ref, o_ref, lse_ref, ha_ref, hat_ref):
        ls = [l1_ref[...], l2_ref[...], l3_ref[...]]
        mx = jnp.maximum(jnp.maximum(ls[0], ls[1]), ls[2])
        den = sum(jnp.exp(l - mx) for l in ls)
        lse = mx + jnp.log(den)
        lse_ref[...] = lse
        o = jnp.zeros((tm, D_MODEL), F32)
        for l, oref in zip(ls, (o1_ref, o2_ref, o3_ref)):
            o = o + _select_cols(jnp.exp(l - lse), e_ref[...]) * oref[...]
        o_ref[...] = o
        z = za_ref[...].astype(F32)
        ha = z * _sigmoid(z) * o
        ha_ref[...] = ha.astype(ha_ref.dtype)
        hat_ref[...] = ha.T.astype(hat_ref.dtype)

    row = lambda w: pl.BlockSpec((tm, w), lambda i: (i, 0))
    return pl.pallas_call(
        body, grid=(s_n // tm,),
        in_specs=[row(D_MODEL)] * 3 + [row(LANES)] * 3
        + [pl.BlockSpec((tm, D_MODEL), lambda i: (i, 7)), pl.BlockSpec((LANES, D_MODEL), lambda i: (0, 0))],
        out_specs=[row(D_MODEL), row(LANES), row(D_MODEL), pl.BlockSpec((D_MODEL, tm), lambda i: (0, i))],
        out_shape=[jax.ShapeDtypeStruct((s_n, D_MODEL), F32), jax.ShapeDtypeStruct((s_n, LANES), F32),
                   jax.ShapeDtypeStruct((s_n, D_MODEL), ACT_DTYPE), jax.ShapeDtypeStruct((D_MODEL, s_n), ACT_DTYPE)],
        name="attn_combine",
    )(*outs, *lses, proj, _head_expand_matrix())


def _gates(gc_ref, ga_ref, b_ref):
    b = b_ref[...]
    gc = _sigmoid(gc_ref[...].astype(F32) + b[:, :D_MODEL])
    ga = _sigmoid(ga_ref[...].astype(F32) + b[:, D_MODEL:])
    return gc, ga


def _merge_loss(hc, ha, woc, woa, wo, proj, b_merge, xp, final_g, tgt):
    s_n = xp.shape[0]
    tm = 512

    def body(hc_ref, ha_ref, woc_ref, woa_ref, wo_ref, gc_ref, ga_ref, b_ref, x_ref, gf_ref, t_ref,
             yc_ref, ya_ref, mg_ref, mgt_ref, dh_ref, dhb_ref, dgf_ref, loss_ref):
        i = pl.program_id(0)

        @pl.when(i == 0)
        def _():
            dgf_ref[...] = jnp.zeros(dgf_ref.shape, F32)
            loss_ref[...] = jnp.zeros(loss_ref.shape, F32)

        yc = _dot(hc_ref[...], woc_ref[...])
        ya = _dot(ha_ref[...], woa_ref[...])
        gc, ga = _gates(gc_ref, ga_ref, b_ref)
        mg = gc * yc + ga * ya
        yc_ref[...] = yc.astype(yc_ref.dtype)
        ya_ref[...] = ya.astype(ya_ref.dtype)
        mg_ref[...] = mg.astype(mg_ref.dtype)
        mgt_ref[...] = mg.T.astype(mgt_ref.dtype)
        h2 = x_ref[...] + _dot(mg, wo_ref[...])
        r2 = lax.rsqrt(jnp.mean(h2 * h2, axis=-1, keepdims=True) + EPS)
        nrm = h2 * r2
        gf = gf_ref[...]
        err = nrm * gf - t_ref[...]
        e2 = (err * err).reshape(tm // 8, 8, D_MODEL).sum(axis=0)
        loss_ref[...] += sum(e2[:, c * LANES:(c + 1) * LANES] for c in range(D_MODEL // LANES))
        dy = err * (1.0 / D_MODEL)
        dgf_ref[...] += jnp.sum(dy * nrm, axis=0, keepdims=True)
        dn = dy * gf
        dh2 = r2 * (dn - nrm * jnp.mean(dn * nrm, axis=-1, keepdims=True))
        dh_ref[...] = dh2
        dhb_ref[...] = dh2.astype(dhb_ref.dtype)

    row = pl.BlockSpec((tm, D_MODEL), lambda i: (i, 0))
    wsp = pl.BlockSpec((D_MODEL, D_MODEL), lambda i: (0, 0))
    vec = lambda w: pl.BlockSpec((1, w), lambda i: (0, 0))
    act = jax.ShapeDtypeStruct((s_n, D_MODEL), ACT_DTYPE)
    return pl.pallas_call(
        body, grid=(s_n // tm,),
        in_specs=[row, row, wsp, wsp, wsp,
                  pl.BlockSpec((tm, D_MODEL), lambda i: (i, 8)), pl.BlockSpec((tm, D_MODEL), lambda i: (i, 9)),
                  vec(2 * D_MODEL), row, vec(D_MODEL), row],
        out_specs=[row, row, row, pl.BlockSpec((D_MODEL, tm), lambda i: (0, i)), row, row,
                   vec(D_MODEL), pl.BlockSpec((8, LANES), lambda i: (0, 0))],
        out_shape=[act, act, act, jax.ShapeDtypeStruct((D_MODEL, s_n), ACT_DTYPE),
                   jax.ShapeDtypeStruct((s_n, D_MODEL), F32), act,
                   jax.ShapeDtypeStruct((1, D_MODEL), F32), jax.ShapeDtypeStruct((8, LANES), F32)],
        name="merge_loss",
    )(hc, ha, woc, woa, wo, proj, proj, b_merge, xp, final_g, tgt)


def _merge_bwd(dh2b, wo, woc, woa, yc, ya, proj, b_merge, o):
    s_n = dh2b.shape[0]
    tm = 512

    def body(dh_ref, wo_ref, woc_ref, woa_ref, yc_ref, ya_ref, gc_ref, ga_ref, b_ref, o_ref, za_ref, e_ref,
             dyc_ref, dya_ref, dhc_ref, do_ref, dsum_ref, db3_ref, dbias_ref):
        i = pl.program_id(0)

        @pl.when(i == 0)
        def _():
            dbias_ref[...] = jnp.zeros(dbias_ref.shape, F32)

        dmg = _dot_nt(dh_ref[...], wo_ref[...])
        gc, ga = _gates(gc_ref, ga_ref, b_ref)
        dgc = dmg * yc_ref[...].astype(F32) * gc * (1.0 - gc)
        dga = dmg * ya_ref[...].astype(F32) * ga * (1.0 - ga)
        dbias_ref[:, :D_MODEL] += jnp.sum(dgc, axis=0, keepdims=True)
        dbias_ref[:, D_MODEL:] += jnp.sum(dga, axis=0, keepdims=True)
        dyc = dmg * gc
        dya = dmg * ga
        dyc_ref[...] = dyc.astype(dyc_ref.dtype)
        dya_ref[...] = dya.astype(dya_ref.dtype)
        dhc_ref[...] = _dot_nt(dyc, woc_ref[...]).astype(dhc_ref.dtype)
        dha = _dot_nt(dya, woa_ref[...])
        z = za_ref[...].astype(F32)
        sg = _sigmoid(z)
        ov = o_ref[...]
        dout = dha * z * sg
        do_ref[...] = dout.astype(do_ref.dtype)
        dsum_ref[...] = _select_cols(dout * ov, e_ref[...])
        db3_ref[0] = (dha * ov * sg * (1.0 + z * (1.0 - sg))).astype(db3_ref.dtype)
        db3_ref[1] = dgc.astype(db3_ref.dtype)
        db3_ref[2] = dga.astype(db3_ref.dtype)

    row = pl.BlockSpec((tm, D_MODEL), lambda i: (i, 0))
    wsp = pl.BlockSpec((D_MODEL, D_MODEL), lambda i: (0, 0))
    act = jax.ShapeDtypeStruct((s_n, D_MODEL), ACT_DTYPE)
    return pl.pallas_call(
        body, grid=(s_n // tm,),
        in_specs=[row, wsp, wsp, wsp, row, row,
                  pl.BlockSpec((tm, D_MODEL), lambda i: (i, 8)), pl.BlockSpec((tm, D_MODEL), lambda i: (i, 9)),
                  pl.BlockSpec((1, 2 * D_MODEL), lambda i: (0, 0)), row,
                  pl.BlockSpec((tm, D_MODEL), lambda i: (i, 7)), pl.BlockSpec((D_MODEL, LANES), lambda i: (0, 0))],
        out_specs=[row, row, row, row, pl.BlockSpec((tm, LANES), lambda i: (i, 0)),
                   pl.BlockSpec((3, tm, D_MODEL), lambda i: (0, i, 0)),
                   pl.BlockSpec((1, 2 * D_MODEL), lambda i: (0, 0))],
        out_shape=[act, act, act, act, jax.ShapeDtypeStruct((s_n, LANES), F32),
                   jax.ShapeDtypeStruct((3, s_n, D_MODEL), ACT_DTYPE),
                   jax.ShapeDtypeStruct((1, 2 * D_MODEL), F32)],
        name="merge_bwd",
    )(dh2b, wo, woc, woa, yc, ya, proj, proj, b_merge, o, proj, _head_sum_matrix())


def _mm_lhs_resident(a, b, tn, name):
    m_n, k_n = a.shape
    n_n = b.shape[1]

    def body(a_ref, b_ref, o_ref):
        o_ref[...] = _dot(a_ref[...], b_ref[...])

    return pl.pallas_call(
        body, grid=(n_n // tn,),
        in_specs=[pl.BlockSpec((m_n, k_n), lambda n: (0, 0)), pl.BlockSpec((k_n, tn), lambda n: (0, n))],
        out_specs=pl.BlockSpec((m_n, tn), lambda n: (0, n)),
        out_shape=jax.ShapeDtypeStruct((m_n, n_n), F32),
        name=name,
    )(a, b)


def _conv_bwd(proj, conv_w, dhc):
    s_n = proj.shape[0]
    lr = s_n // N_RES
    pv = proj.reshape(N_RES, lr, IN_COLS)

    def body(xc_ref, bg_ref, cg_ref, zc_ref, w_ref, dhc_ref, da4_ref, dw_ref, dc_ref):
        w = w_ref[...]
        row = lax.broadcasted_iota(jnp.int32, (lr, LANES), 0)
        dw = [jnp.zeros((1, LANES), F32) for _ in range(3)]
        for r in range(N_RES):
            a, am1, am2 = _conv_terms(xc_ref, cg_ref, r, row, lr)
            c = w[0:1] * am2 + w[1:2] * am1 + w[2:3] * a
            z = zc_ref[r].astype(F32)
            sg = _sigmoid(z)
            sz = z * sg
            bg = bg_ref[r].astype(F32)
            dh = dhc_ref[r].astype(F32)
            da4_ref[1, r] = (dh * sz * c).astype(da4_ref.dtype)
            da4_ref[3, r] = (dh * bg * c * sg * (1.0 + z * (1.0 - sg))).astype(da4_ref.dtype)
            dc = dh * sz * bg
            dc_ref[r] = dc
            dw[0] = dw[0] + jnp.sum(dc * am2, axis=0, keepdims=True)
            dw[1] = dw[1] + jnp.sum(dc * am1, axis=0, keepdims=True)
            dw[2] = dw[2] + jnp.sum(dc * a, axis=0, keepdims=True)
        dw_ref[0:1, :] = dw[0]
        dw_ref[1:2, :] = dw[1]
        dw_ref[2:3, :] = dw[2]

        def shift_up(v):
            return jnp.where(row < lr - 1, pltpu.roll(v, lr - 1, 0), 0.0)

        for r in range(N_RES):
            dp1 = dc_ref[r + 1] if r + 1 < N_RES else shift_up(dc_ref[0])
            dp2 = dc_ref[r + 2] if r + 2 < N_RES else shift_up(dc_ref[r + 2 - N_RES])
            da = w[2:3] * dc_ref[r] + w[1:2] * dp1 + w[0:1] * dp2
            da4_ref[0, r] = (da * cg_ref[r].astype(F32)).astype(da4_ref.dtype)
            da4_ref[2, r] = (da * xc_ref[r].astype(F32)).astype(da4_ref.dtype)

    def col(part):
        return pl.BlockSpec((N_RES, lr, LANES), lambda j: (0, 0, part * 8 + j))

    da4, dw = pl.pallas_call(
        body, grid=(D_MODEL // LANES,),
        in_specs=[col(0), col(1), col(2), col(3), pl.BlockSpec((3, LANES), lambda j: (0, j)),
                  pl.BlockSpec((N_RES, lr, LANES), lambda j: (0, 0, j))],
        out_specs=[pl.BlockSpec((4, N_RES, lr, LANES), lambda j: (0, 0, 0, j)),
                   pl.BlockSpec((3, LANES), lambda j: (0, j))],
        out_shape=[jax.ShapeDtypeStruct((4, N_RES, lr, D_MODEL), ACT_DTYPE),
                   jax.ShapeDtypeStruct((3, D_MODEL), F32)],
        scratch_shapes=[pltpu.VMEM((N_RES, lr, LANES), F32)],
        name="conv_bwd",
    )(pv, pv, pv, pv, conv_w, dhc.reshape(N_RES, lr, D_MODEL))
    return da4.reshape(4, s_n, D_MODEL), dw


def _attn_bwd(proj, dout, lse, dsum, slopes, d):
    g_n, rq = PATTERNS[d]
    un = _Units(g_n, rq)
    s_n = proj.shape[0]
    lr = s_n // N_RES
    nb = lr // rq
    q_n = g_n * rq
    d0, m0, d1, m1 = (np.ascontiguousarray(t.T) for t in _attn_tables(d))
    k0_rows = d0.shape[0] // g_n
    first, n_more = _batches(nb, rq, BWD_BATCH[d], un.per_res)
    cnt = len(first)
    bsz = RES_PER_STEP if un.per_res else cnt
    gd = RES_PER_STEP if un.per_res else g_n

    def body(sl_ref, q_ref, k_ref, v_ref, do_ref, lse_ref, ds_ref, d0_ref, m0_ref, d1_ref, m1_ref, out_ref,
             b0_ref, b1_ref, lt_ref, dt_ref, dk_ref, dv_ref):
        hp = pl.program_id(1)
        for h in (0, 1):
            slope = sl_ref[2 * hp + h]
            b0_ref[:, h * q_n:(h + 1) * q_n] = m0_ref[...] - slope * d0_ref[...]
            b1_ref[:, h * q_n:(h + 1) * q_n] = m1_ref[...] - slope * d1_ref[...]
        dk_ref[...] = jnp.zeros(dk_ref.shape, F32)
        dv_ref[...] = jnp.zeros(dv_ref.shape, F32)
        low = lax.broadcasted_iota(jnp.int32, (1, q_n, LANES), 2) < HEAD_DIM
        row16 = pl.multiple_of(16 * hp, 16)

        def query_rows(stat_ref, t_ref, q_starts):
            tiles = un.load(stat_ref, q_starts, rq)
            for b in range(bsz):
                t_ref[b] = tiles[b].T
            t16 = t_ref[:, pl.ds(row16, 16), :]
            return jnp.concatenate([t16[:, 0:1, :], t16[:, 8:9, :]], axis=2)

        def batch(q_starts, k_starts, k_rows, bias):
            qq = _stack_heads(un.load(q_ref, q_starts, rq) * 0.125, low)
            dd = _stack_heads(un.load(do_ref, q_starts, rq), low)
            ks = un.load(k_ref, k_starts, k_rows)
            vs = un.load(v_ref, k_starts, k_rows)
            lrow = query_rows(lse_ref, lt_ref, q_starts)
            drow = query_rows(ds_ref, dt_ref, q_starts)
            pt = jnp.exp(_bdot(ks, qq, _BNT) + bias - lrow)
            dst = pt * (_bdot(vs, dd, _BNT) - drow)
            un.store(dv_ref, k_starts, k_rows, _bdot(pt, dd, _BNN), add=True)
            un.store(dk_ref, k_starts, k_rows, _bdot(dst, qq, _BNN), add=True)
            dq = _bdot(jnp.swapaxes(dst, 1, 2), ks, _BNN)
            un.store(out_ref, q_starts, rq, jnp.where(low, dq[:, :q_n], dq[:, q_n:]) * 0.125, lead=(0,))

        if un.per_res:
            batch([0], [0], k0_rows, b0_ref[...][None])
        else:
            bias = jnp.concatenate([b0_ref[...][None]] + [b1_ref[...][None]] * (cnt - 1), axis=0)
            batch([q for q, _ in first], [k for _, k in first], 2 * rq, bias)

        def more(j, carry):
            n0 = j * cnt
            qs = [pl.multiple_of((n0 + i) * rq, rq) for i in range(cnt)]
            ks = [pl.multiple_of((n0 + i - 1) * rq, rq) for i in range(cnt)]
            batch(qs, ks, 2 * rq, b1_ref[...][None])
            return carry

        lax.fori_loop(1, 1 + n_more, more, 0)
        out_ref[1] = dk_ref[...].astype(out_ref.dtype)
        out_ref[2] = dv_ref[...].astype(out_ref.dtype)

    pv = _pattern_view(proj, g_n)
    full = lambda a: pl.BlockSpec(a.shape, lambda r, hp: (0, 0))
    out = pl.pallas_call(
        body, grid=_pattern_grid(g_n),
        in_specs=[pl.BlockSpec(memory_space=pltpu.SMEM),
                  _pattern_spec(g_n, lr, lambda hp: 32 + hp),
                  _pattern_spec(g_n, lr, lambda hp: 40 + hp),
                  _pattern_spec(g_n, lr, lambda hp: 48 + hp),
                  _pattern_spec(g_n, lr, lambda hp: hp),
                  _pattern_spec(g_n, lr, lambda hp: 0),
                  _pattern_spec(g_n, lr, lambda hp: 0),
                  full(d0), full(m0), full(d1), full(m1)],
        out_specs=_pattern_spec(g_n, lr, lambda hp: hp, lead=(3,)),
        out_shape=jax.ShapeDtypeStruct(_pattern_view_shape(s_n, D_MODEL, g_n, lead=(3,)), ACT_DTYPE),
        scratch_shapes=[pltpu.VMEM((d0.shape[0], 2 * q_n), F32), pltpu.VMEM((2 * q_n, 2 * q_n), F32),
                        pltpu.VMEM((bsz, LANES, q_n), F32), pltpu.VMEM((bsz, LANES, q_n), F32),
                        pltpu.VMEM((gd, lr, LANES), F32), pltpu.VMEM((gd, lr, LANES), F32)],
        name=f"attn_bwd_d{d}",
    )(slopes, pv, pv, pv, _pattern_view(dout, g_n), _pattern_view(lse, g_n), _pattern_view(dsum, g_n),
      d0, m0, d1, m1)
    return out.reshape(3, s_n, D_MODEL)


def _sum3(a, b, c):
    _, s_n, c_n = a.shape
    tm = 512

    def body(a_ref, b_ref, c_ref, o_ref):
        o_ref[...] = (a_ref[...] + b_ref[...] + c_ref[...]).astype(o_ref.dtype)

    spec = pl.BlockSpec((1, tm, c_n), lambda p, i: (p, i, 0))
    return pl.pallas_call(
        body, grid=(3, s_n // tm), in_specs=[spec] * 3, out_specs=spec,
        out_shape=jax.ShapeDtypeStruct(a.shape, ACT_DTYPE), name="sum_dqkv",
    )(a, b, c)


def _part_index(step, per, lo, n):
    return jnp.clip(step // per - lo, 0, n - 1)


def _dw_in(ut, da4, dc3, db3):
    s_n = ut.shape[1]
    tn = 512
    per = D_MODEL // tn
    shard_blocks = SHARD_COLS // tn

    def body(a_ref, p0_ref, p1_ref, p2_ref, o_ref):
        part = pl.program_id(0) // per

        @pl.when(part < 4)
        def _():
            o_ref[...] = _dot(a_ref[...], p0_ref[...])

        @pl.when((part >= 4) & (part < 7))
        def _():
            o_ref[...] = _dot(a_ref[...], p1_ref[...])

        @pl.when(part >= 7)
        def _():
            o_ref[...] = _dot(a_ref[...], p2_ref[...])

    def pspec(lo, n):
        return pl.BlockSpec((None, s_n, tn), lambda j: (_part_index(j, per, lo, n), 0, j % per))

    return pl.pallas_call(
        body, grid=(IN_COLS // tn,),
        in_specs=[pl.BlockSpec((D_MODEL, s_n), lambda j: (0, 0), pipeline_mode=pl.Buffered(1)),
                  pspec(0, 4), pspec(4, 3), pspec(7, 3)],
        out_specs=pl.BlockSpec((None, D_MODEL, tn), lambda j: (j // shard_blocks, 0, j % shard_blocks)),
        out_shape=jax.ShapeDtypeStruct((4, D_MODEL, SHARD_COLS), F32),
        name="dw_in",
    )(ut, da4, dc3, db3)


def _input_grad(da4, dc3, db3, w4, xp, norm_g, dh2, row0, rows):
    tm, tk = 256, 512
    per = D_MODEL // tk
    shard_blocks = SHARD_COLS // tk
    m0 = row0 // tm

    def body(p0_ref, p1_ref, p2_ref, w_ref, x_ref, g_ref, dh_ref, gx_ref, dg_ref):
        @pl.when(pl.program_id(0) == 0)
        def _():
            dg_ref[...] = jnp.zeros(dg_ref.shape, F32)

        du = None
        for k in range(IN_COLS // tk):
            part, cols = k // per, pl.ds((k % per) * tk, tk)
            ref, slot = (p0_ref, part) if part < 4 else (p1_ref, part - 4) if part < 7 else (p2_ref, part - 7)
            d = _dot_nt(ref[slot, :, cols], w_ref[k // shard_blocks, :, pl.ds((k % shard_blocks) * tk, tk)])
            du = d if du is None else du + d
        x = x_ref[...]
        r = lax.rsqrt(jnp.mean(x * x, axis=-1, keepdims=True) + EPS)
        nrm = x * r
        dg_ref[...] += jnp.sum(du * nrm, axis=0, keepdims=True)
        dn = du * g_ref[...]
        gx_ref[...] = dh_ref[...] + r * (dn - nrm * jnp.mean(dn * nrm, axis=-1, keepdims=True))

    def pspec(n):
        return pl.BlockSpec((n, tm, D_MODEL), lambda m: (0, m0 + m, 0))

    row_in = pl.BlockSpec((tm, D_MODEL), lambda m: (m0 + m, 0))
    vec = pl.BlockSpec((1, D_MODEL), lambda m: (0, 0))
    return pl.pallas_call(
        body, grid=(rows // tm,),
        in_specs=[pspec(4), pspec(3), pspec(3),
                  pl.BlockSpec(w4.shape, lambda m: (0, 0, 0), pipeline_mode=pl.Buffered(1)),
                  row_in, vec, row_in],
        out_specs=[pl.BlockSpec((tm, D_MODEL), lambda m: (m, 0)), vec],
        out_shape=[jax.ShapeDtypeStruct((rows, D_MODEL), F32), jax.ShapeDtypeStruct((1, D_MODEL), F32)],
        name="input_grad",
    )(da4, dc3, db3, w4, xp, norm_g, dh2)


class _Step:
    def __init__(self, x, tgt, norm_g, after=0.0):
        self.norm_g = norm_g
        self.slopes = _alibi_slopes()
        self.xp, self.tp = _to_residue_major(x, tgt, after)
        self.u, self.ut = _rms_in(self.xp, norm_g)

    def mixers(self, w4, taps):
        self.w4, self.taps = w4, taps
        self.proj = _in_proj(self.u, w4)
        self.hc, self.hct = _conv_fwd(self.proj, taps)
        fwd = [_attn_fwd(self.proj, self.slopes, d) for d in PATTERNS]
        self.o, self.lse, self.ha, self.hat = _attn_combine([f[0] for f in fwd], [f[1] for f in fwd], self.proj)

    def merge_and_loss(self, woc, woa, wo, b_merge, final_g):
        self.woc, self.woa, self.wo, self.b_merge = woc, woa, wo, b_merge
        (self.yc, self.ya, _, self.mgt, self.dh2, self.dh2b, self.d_final_g, self.loss8) = _merge_loss(
            self.hc, self.ha, woc, woa, wo, self.proj, b_merge, self.xp, final_g, self.tp)

    def out_weight_grads(self):
        (dyc, dya, self.dhc, self.dout, self.dsum, self.db3, self.d_bias) = _merge_bwd(
            self.dh2b, self.wo, self.woc, self.woa, self.yc, self.ya, self.proj, self.b_merge, self.o)
        d_wo = _mm_lhs_resident(self.mgt, self.dh2b, 256, "dw_o")
        d_woc = _mm_lhs_resident(self.hct, dyc, 256, "dw_out_conv")
        d_woa = _mm_lhs_resident(self.hat, dya, 256, "dw_out_attn")
        return d_woc, d_woa, d_wo

    def conv_grads(self, after=0.0):
        self.da4, self.d_taps = _conv_bwd(self.proj, self.taps + after, self.dhc)

    def in_weight_grad(self, after=0.0):
        slopes = self.slopes + after
        self.dc3 = _sum3(*[_attn_bwd(self.proj, self.dout, self.lse, self.dsum, slopes, d) for d in PATTERNS])
        return _dw_in(self.ut, self.da4, self.dc3, self.db3)

    def input_grad(self, half, after=0.0):
        rows = self.xp.shape[0] // 2
        return _input_grad(self.da4, self.dc3, self.db3, self.w4, self.xp, self.norm_g + after, self.dh2,
                           half * rows, rows)


def _local_grads(x, tgt, norm_g, w4, b_merge, conv_w, woc, woa, wo, final_g):
    st = _Step(x, tgt, norm_g)
    st.mixers(w4, conv_w)
    st.merge_and_loss(woc, woa, wo, b_merge, final_g)
    d_woc, d_woa, d_wo = st.out_weight_grads()
    st.conv_grads()
    d_w4 = st.in_weight_grad()
    gx_lo, dg_lo = st.input_grad(0)
    gx_hi, dg_hi = st.input_grad(1)
    return (st.loss8, _to_natural(gx_lo, gx_hi), dg_lo + dg_hi, d_w4, st.d_bias, st.d_taps, d_woc, d_woa, d_wo,
            st.d_final_g)


MESH = pl.DeviceIdType.MESH
_CHIP_FLIPS = ((1, 0), (0, 1), (1, 1))
_ANY = pl.BlockSpec(memory_space=pl.ANY)


def _place():
    return lax.axis_index("x"), lax.axis_index("y"), lax.axis_index("c")


def _flip(v, f):
    return 1 - v if f else v


def _remote(src, dst, send_sems, recv_sems, k, device):
    return pltpu.make_async_remote_copy(src_ref=src, dst_ref=dst, send_sem=send_sems.at[k], recv_sem=recv_sems.at[k],
                                        device_id=device, device_id_type=MESH)


def _place_shard(w, chip):
    rows, cols = w.shape
    tm = 128

    def body(chip_ref, w_ref, o_ref):
        o_ref[0] = w_ref[...].astype(o_ref.dtype)

    return pl.pallas_call(
        body,
        grid_spec=pltpu.PrefetchScalarGridSpec(
            num_scalar_prefetch=1, grid=(rows // tm,),
            in_specs=[pl.BlockSpec((tm, cols), lambda i, chip_ref: (i, 0))],
            out_specs=pl.BlockSpec((1, tm, cols), lambda i, chip_ref: (chip_ref[0], i, 0))),
        out_shape=jax.ShapeDtypeStruct((4, rows, cols), MXU_DTYPE),
        name="place_shard",
    )(chip, w)


def _gather_copies(arrs, _, send_sems, recv_sems):
    x, y, c = _place()
    out = []
    for a, arr in enumerate(arrs):
        h = arr.shape[1] // 2
        mine = arr.at[2 * x + y, pl.ds(pl.multiple_of(c * h, 8), h)]
        for t, (fx, fy) in enumerate(_CHIP_FLIPS):
            out.append(_remote(mine, mine, send_sems, recv_sems, 3 * a + t, (_flip(x, fx), _flip(y, fy), c)))
    return out


def _forward_to_sibling(arrs):
    n = len(arrs)

    def body(*refs):
        outs = refs[n:2 * n]
        send_sems, recv_sems = refs[2 * n:]
        x, y, c = _place()
        sibling = (x, y, 1 - c)
        started = []
        for a in range(n):
            h = outs[a].shape[1] // 2
            rows = pl.ds(pl.multiple_of(c * h, 8), h)
            for t, (fx, fy) in enumerate(_CHIP_FLIPS):
                landed = outs[a].at[2 * _flip(x, fx) + _flip(y, fy), rows]
                cp = _remote(landed, landed, send_sems, recv_sems, 3 * a + t, sibling)
                cp.start()
                started.append(cp)
        for a in range(n):
            h = outs[a].shape[1] // 2
            rows = pl.ds(pl.multiple_of((1 - c) * h, 8), h)
            for t, (fx, fy) in enumerate(_CHIP_FLIPS):
                handed = outs[a].at[2 * _flip(x, fx) + _flip(y, fy), rows]
                _remote(handed, handed, send_sems, recv_sems, 3 * a + t, sibling).wait_recv()
        for cp in started:
            cp.wait_send()

    return pl.pallas_call(
        body, in_specs=[_ANY] * n, out_specs=[_ANY] * n,
        out_shape=[jax.ShapeDtypeStruct(s.shape, s.dtype) for s in arrs],
        input_output_aliases={a: a for a in range(n)},
        scratch_shapes=[pltpu.SemaphoreType.DMA((3 * n,)), pltpu.SemaphoreType.DMA((3 * n,))],
        name="gathered_to_sibling",
    )(*arrs)


_HBM = pl.BlockSpec(memory_space=pltpu.HBM)
_SEM = pl.BlockSpec(memory_space=pltpu.SEMAPHORE)
_EFFECT = pltpu.SideEffectType.DATAFLOW_SIDE_EFFECTING


class _SplitExchange:
    def __init__(self, name, srcs, land_shapes, n_copies, copies, riders=()):
        self.name, self.n, self.nl, self.copies = name, len(srcs), len(land_shapes), copies
        n, nb = self.n, len(srcs) + len(land_shapes)
        lands = [lax.empty(s.shape, s.dtype) for s in land_shapes]
        bufs = [pltpu.with_memory_space_constraint(a, pltpu.HBM) for a in (*srcs, *lands, *riders)]
        na = len(bufs)

        def body(*refs):
            send_sems, recv_sems = refs[na], refs[na + 1]
            for cp in copies(refs[:n], refs[n:nb], send_sems, recv_sems):
                cp.start()
            refs[-1][...] = jnp.zeros(refs[-1].shape, F32)

        outs = pl.pallas_call(
            body, name=name + "_start",
            in_specs=[_HBM] * na,
            out_specs=[_SEM, _SEM] + [_HBM] * na + [pl.BlockSpec(memory_space=pltpu.VMEM)],
            out_shape=[pltpu.SemaphoreType.DMA((n_copies,)), pltpu.SemaphoreType.DMA((n_copies,))]
            + [pltpu.HBM(b.shape, b.dtype) for b in bufs] + [jax.ShapeDtypeStruct((8, LANES), F32)],
            input_output_aliases={i: 2 + i for i in range(na)},
            compiler_params=pltpu.CompilerParams(has_side_effects=_EFFECT),
        )(*bufs)
        self.sems, self.bufs, self.riders, self.token = outs[:2], outs[2:2 + nb], outs[2 + nb:2 + na], outs[-1]

    def after(self):
        return self.token[0, 0]

    def wait(self, done, riders=()):
        n, nb, copies = self.n, self.n + self.nl, self.copies
        bufs = [*self.bufs, *[pltpu.with_memory_space_constraint(a, pltpu.HBM) for a in riders]]
        na = len(bufs)

        def body(*refs):
            send_sems, recv_sems = refs[na], refs[na + 1]
            for cp in copies(refs[:n], refs[n:nb], send_sems, recv_sems):
                cp.wait_send()
                cp.wait_recv()

        outs = pl.pallas_call(
            body, name=self.name + "_wait",
            in_specs=[_HBM] * na + [_SEM, _SEM, _ANY],
            out_specs=[_HBM] * na,
            out_shape=[pltpu.HBM(b.shape, b.dtype) for b in bufs],
            input_output_aliases={i: i for i in range(na)},
            compiler_params=pltpu.CompilerParams(has_side_effects=_EFFECT),
        )(*bufs, *self.sems, done)
        return outs[:n], outs[n:nb], outs[nb:]


def _sibling_copies(srcs, lands, send_sems, recv_sems):
    x, y, c = _place()
    out = []
    for a, (src, land) in enumerate(zip(srcs, lands)):
        h = src.shape[1] // 2
        theirs = pl.ds(pl.multiple_of((1 - c) * h, 8), h)
        out.append(_remote(src.at[:, theirs], land, send_sems, recv_sems, a, (x, y, 1 - c)))
    return out


def _grads_to_sibling(name, grads):
    shapes = [jax.ShapeDtypeStruct((4, g.shape[1] // 2, g.shape[2]), g.dtype) for g in grads]
    return _SplitExchange(name, grads, shapes, len(grads), _sibling_copies)


def _chip_copies(srcs, lands, send_sems, recv_sems):
    x, y, c = _place()
    out = []
    for a, (src, land) in enumerate(zip(srcs, lands)):
        for t, (fx, fy) in enumerate(_CHIP_FLIPS):
            tx, ty = _flip(x, fx), _flip(y, fy)
            out.append(_remote(src.at[2 * tx + ty], land.at[t], send_sems, recv_sems, 3 * a + t, (tx, ty, c)))
    return out


def _grads_to_chips(name, parts):
    shapes = [jax.ShapeDtypeStruct((3, *p.shape[1:]), p.dtype) for p in parts]
    return _SplitExchange(name, parts, shapes, 3 * len(parts), _chip_copies)


def _add_halves(g, r, half):
    _, rows, cols = g.shape
    h = rows // 2
    tm = min(h, 128)
    nt = h // tm

    def body(half_ref, g_ref, r_ref, f_ref, b_ref):
        s = g_ref[...] + r_ref[...]
        f_ref[...] = s
        b_ref[...] = s.astype(b_ref.dtype)

    spec = pl.BlockSpec((1, tm, cols), lambda j, i, half_ref: (j, i, 0))
    return pl.pallas_call(
        body,
        grid_spec=pltpu.PrefetchScalarGridSpec(
            num_scalar_prefetch=1, grid=(4, nt),
            in_specs=[pl.BlockSpec((1, tm, cols), lambda j, i, half_ref: (j, half_ref[0] * nt + i, 0)), spec],
            out_specs=[spec, spec]),
        out_shape=[jax.ShapeDtypeStruct((4, h, cols), F32), jax.ShapeDtypeStruct((4, h, cols), BF16)],
        name="add_sibling_grads",
    )(half, g, r)


def _add_chips(own, recv, where):
    _, h, cols = own.shape
    tm = min(h, 128)
    nt = h // tm

    def body(where_ref, o_ref, r_ref, out_ref):
        out_ref[...] = ((o_ref[0] + r_ref[0].astype(F32)) + r_ref[1].astype(F32)) + r_ref[2].astype(F32)

    return pl.pallas_call(
        body,
        grid_spec=pltpu.PrefetchScalarGridSpec(
            num_scalar_prefetch=1, grid=(nt,),
            in_specs=[pl.BlockSpec((1, tm, cols), lambda i, where_ref: (where_ref[0], i, 0)),
                      pl.BlockSpec((3, tm, cols), lambda i, where_ref: (0, i, 0))],
            out_specs=pl.BlockSpec((tm, cols), lambda i, where_ref: (where_ref[1] * nt + i, 0))),
        out_shape=jax.ShapeDtypeStruct((2 * h, cols), F32),
        name="add_chip_grads",
    )(where, own, recv)


def _share_halves(shards):
    n = len(shards)

    def body(*refs):
        outs = refs[n:2 * n]
        send_sems, recv_sems = refs[2 * n:]
        x, y, c = _place()
        copies = []
        for a in range(n):
            h = outs[a].shape[0] // 2
            mine = outs[a].at[pl.ds(pl.multiple_of(c * h, 8), h)]
            copies.append(_remote(mine, mine, send_sems, recv_sems, a, (x, y, 1 - c)))
        for cp in copies:
            cp.start()
        for a, cp in enumerate(copies):
            cp.wait_send()
            h = outs[a].shape[0] // 2
            theirs = outs[a].at[pl.ds(pl.multiple_of((1 - c) * h, 8), h)]
            _remote(theirs, theirs, send_sems, recv_sems, a, (x, y, 1 - c)).wait_recv()

    return pl.pallas_call(
        body, in_specs=[_ANY] * n, out_specs=[_ANY] * n,
        out_shape=[jax.ShapeDtypeStruct(p.shape, p.dtype) for p in shards],
        input_output_aliases={a: a for a in range(n)},
        scratch_shapes=[pltpu.SemaphoreType.DMA((n,)), pltpu.SemaphoreType.DMA((n,))],
        name="share_reduced_halves",
    )(*shards)


def _exchange_small(rows, reduce):
    cols = rows[0].shape[1]
    n = len(rows)
    assert sum(r.shape[0] for r in rows) <= 8

    def body(*refs):
        ins, out_ref = refs[:n], refs[n]
        vec_ref, gath_ref, send_sems, recv_sems = refs[n + 1:]
        x, y, c = _place()
        me = 4 * x + 2 * y + c
        vec_ref[...] = jnp.zeros(vec_ref.shape, F32)
        at = 0
        for r in ins:
            vec_ref[at:at + r.shape[0], :] = r[...]
            at += r.shape[0]
        copies = []
        for k in range(1, 8):
            peer = (_flip(x, (k >> 2) & 1), _flip(y, (k >> 1) & 1), _flip(c, k & 1))
            copies.append(_remote(vec_ref, gath_ref.at[me], send_sems, recv_sems, k - 1, peer))
        for cp in copies:
            cp.start()
        gath_ref[me] = vec_ref[...]
        for cp in copies:
            cp.wait()
        if reduce:
            tot = gath_ref[0]
            for dev in range(1, 8):
                tot = tot + gath_ref[dev]
            out_ref[...] = tot
            out_ref[7:8, :] = jnp.zeros((1, cols), F32) + jnp.sum(tot[7:8, :])
        else:
            out_ref[...] = gath_ref[...]

    vm = pl.BlockSpec(memory_space=pltpu.VMEM)
    return pl.pallas_call(
        body, in_specs=[vm] * n, out_specs=vm,
        out_shape=jax.ShapeDtypeStruct((8, cols) if reduce else (8, 8, cols), F32),
        scratch_shapes=[pltpu.VMEM((8, cols), F32), pltpu.VMEM((8, 8, cols), F32),
                        pltpu.SemaphoreType.DMA((7,)), pltpu.SemaphoreType.DMA((7,))],
        name="reduce_small" if reduce else "gather_small",
    )(*rows)


def _adamw(w, g, m, v, name):
    rows, cols = w.shape
    tm = 128 if rows % 128 == 0 else rows

    def body(w_ref, g_ref, m_ref, v_ref, d_ref, m2_ref, v2_ref):
        gr = g_ref[...]
        m2 = ADAM_B1 * m_ref[...] + (1.0 - ADAM_B1) * gr
        v2 = ADAM_B2 * v_ref[...] + (1.0 - ADAM_B2) * (gr * gr)
        m_hat = m2 / (1.0 - ADAM_B1 ** ADAM_STEP)
        v_hat = v2 / (1.0 - ADAM_B2 ** ADAM_STEP)
        d_ref[...] = -ADAM_LR * (m_hat / (jnp.sqrt(v_hat) + ADAM_EPS) + ADAM_WD * w_ref[...])
        m2_ref[...] = m2
        v2_ref[...] = v2

    spec = pl.BlockSpec((tm, cols), lambda i: (i, 0))
    sds = jax.ShapeDtypeStruct((rows, cols), F32)
    return pl.pallas_call(body, grid=(rows // tm,), in_specs=[spec] * 4, out_specs=[spec] * 3,
                          out_shape=[sds] * 3, name=name)(w, g, m, v)


def kernel(x, norm_g, w_in, b_merge, conv_w, w_out_conv, w_out_attn, w_o, final_g, loss_target, m_norm_g, m_w_in, m_b_merge, m_conv_w, m_w_out_conv, m_w_out_attn, m_w_o, m_final_g, v_norm_g, v_w_in, v_b_merge, v_conv_w, v_w_out_conv, v_w_out_attn, v_w_o, v_final_g):
    mx, my, mc = _place()
    chip = (2 * mx + my).astype(jnp.int32)
    seq = x.shape[1]

    chip1 = chip.reshape(1)
    slots = [_place_shard(w[0], chip1) for w in (w_in, w_out_conv, w_out_attn, w_o)]
    taps8 = _exchange_small([conv_w[0]], reduce=False)
    taps = jnp.concatenate([taps8[2 * j, :3, :] for j in range(4)], axis=1)
    gather_in = _SplitExchange("gather_w_in", slots[:1], [], 3, _gather_copies)
    st = _Step(x[0], loss_target[0], norm_g, after=gather_in.after())
    (w4,), _, out_slots = gather_in.wait(st.ut, riders=slots[1:])
    gather_out = _SplitExchange("gather_w_out", out_slots, [], 9, _gather_copies, riders=[w4])
    (w4,) = _forward_to_sibling(gather_out.riders)
    st.mixers(w4, taps)
    out_ws, _, _ = gather_out.wait(st.o)
    woc, woa, wo = [w.reshape(D_MODEL, D_MODEL) for w in _forward_to_sibling(out_ws)]
    st.merge_and_loss(woc, woa, wo, b_merge, final_g.reshape(1, D_MODEL))

    half = mc.astype(jnp.int32).reshape(1)
    where = jnp.stack([chip, mc.astype(jnp.int32)])
    out_grads = [g.reshape(4, -1, D_MODEL) for g in st.out_weight_grads()]
    to_sibling = _grads_to_sibling("out_grads_to_sibling", out_grads)
    st.conv_grads(after=to_sibling.after())
    out_grads, from_sibling, _ = to_sibling.wait(st.da4)
    out_partial = [_add_halves(g, r, half) for g, r in zip(out_grads, from_sibling)]
    to_chips = _grads_to_chips("out_grads_to_chips", [p[1] for p in out_partial])
    d_w4 = st.in_weight_grad(after=to_chips.after())
    out_reduced = [_add_chips(p[0], r, where) for p, r in zip(out_partial, to_chips.wait(st.dc3)[1])]

    to_sibling = _grads_to_sibling("in_grad_to_sibling", [d_w4])
    gx_lo, dg_lo = st.input_grad(0, after=to_sibling.after())
    (d_w4,), (from_sibling,), _ = to_sibling.wait(gx_lo)
    in_partial = _add_halves(d_w4, from_sibling, half)
    to_chips = _grads_to_chips("in_grad_to_chips", [in_partial[1]])
    gx_hi, dg_hi = st.input_grad(1, after=to_chips.after())
    grad_x = _to_natural(gx_lo, gx_hi)
    in_reduced = _add_chips(in_partial[0], to_chips.wait(grad_x)[1][0], where)
    g_w_in, g_woc, g_woa, g_wo = _share_halves([in_reduced] + out_reduced)

    small = _exchange_small([dg_lo + dg_hi, st.d_bias.reshape(2, D_MODEL), st.d_taps, st.d_final_g,
                             st.loss8.reshape(1, D_MODEL)], reduce=True)
    loss = (0.5 / D_MODEL) * small[7, 0]
    g_norm_g = small[0:1]
    g_bias = small[1:3].reshape(1, 2 * D_MODEL)
    g_taps = lax.dynamic_slice(small[3:6], (0, chip * (D_MODEL // 4)), (3, D_MODEL // 4))
    g_final_g = small[6:7]

    upd = [
        _adamw(norm_g, g_norm_g, m_norm_g, v_norm_g, "adamw_norm_g"),
        _adamw(w_in[0], g_w_in, m_w_in[0], v_w_in[0], "adamw_w_in"),
        _adamw(b_merge, g_bias, m_b_merge, v_b_merge, "adamw_b_merge"),
        _adamw(conv_w[0], g_taps, m_conv_w[0], v_conv_w[0], "adamw_conv_w"),
        _adamw(w_out_conv[0], g_woc, m_w_out_conv[0], v_w_out_conv[0], "adamw_w_out_conv"),
        _adamw(w_out_attn[0], g_woa, m_w_out_attn[0], v_w_out_attn[0], "adamw_w_out_attn"),
        _adamw(w_o[0], g_wo, m_w_o[0], v_w_o[0], "adamw_w_o"),
        _adamw(final_g.reshape(1, D_MODEL), g_final_g, m_final_g.reshape(1, D_MODEL),
               v_final_g.reshape(1, D_MODEL), "adamw_final_g"),
    ]
    shapes = [norm_g.shape, w_in.shape, b_merge.shape, conv_w.shape, w_out_conv.shape, w_out_attn.shape,
              w_o.shape, final_g.shape]
    grads_out = [g_norm_g, g_w_in, g_bias, g_taps, g_woc, g_woa, g_wo, g_final_g]
    outs = [loss, grad_x.reshape(1, seq, D_MODEL)]
    outs += [g.reshape(s) for g, s in zip(grads_out, shapes)]
    for k in range(3):
        outs += [u[k].reshape(s) for u, s in zip(upd, shapes)]
    return tuple(outs)
```

```python
import functools

import numpy as np
import jax
import jax.numpy as jnp
from jax import lax
from jax.experimental import pallas as pl
from jax.experimental.pallas import tpu as pltpu

F32 = jnp.float32
BF16 = jnp.bfloat16
MXU_DTYPE = jnp.bfloat16
ACT_DTYPE = jnp.bfloat16

D_MODEL = 1024
N_HEADS = 16
HEAD_DIM = 64
QB = 128
N_RES = 16
LANES = 128
HP = N_HEADS * HEAD_DIM // LANES
IN_COLS = 10 * D_MODEL
SHARD_COLS = IN_COLS // 4
EPS = 1e-6
NEG = -1e30

ADAM_LR, ADAM_B1, ADAM_B2, ADAM_EPS, ADAM_WD, ADAM_STEP = 0.001, 0.9, 0.999, 1e-08, 0.01, 10

PATTERNS = {1: (16, 8), 4: (4, 32), 16: (1, 128)}

_NN = (((1,), (0,)), ((), ()))
_NT = (((1,), (1,)), ((), ()))


def _dot(a, b):
    return lax.dot_general(a.astype(MXU_DTYPE), b.astype(MXU_DTYPE), _NN, preferred_element_type=F32)


def _dot_nt(a, b):
    return lax.dot_general(a.astype(MXU_DTYPE), b.astype(MXU_DTYPE), _NT, preferred_element_type=F32)


def _split3(x):
    hi = x.astype(BF16)
    r1 = x - hi.astype(F32)
    mid = r1.astype(BF16)
    lo = (r1 - mid.astype(F32)).astype(BF16)
    return hi, mid, lo


def _select_rows(sel, x):
    return sum(lax.dot_general(sel, t, _NN, preferred_element_type=F32) for t in _split3(x))


def _select_cols(x, sel):
    return sum(lax.dot_general(t, sel, _NN, preferred_element_type=F32) for t in _split3(x))


def _sigmoid(z):
    return 1.0 / (1.0 + jnp.exp(-z))


def _perm_matrix():
    idx = np.arange(256)
    p = np.zeros((256, 256), np.float32)
    p[(idx % 16) * 16 + idx // 16, idx] = 1.0
    return jnp.asarray(p, BF16)


def _head_expand_matrix():
    e = np.zeros((LANES, D_MODEL), np.float32)
    for h in range(N_HEADS):
        e[8 * h, HEAD_DIM * h:HEAD_DIM * (h + 1)] = 1.0
    return jnp.asarray(e, BF16)


def _head_sum_matrix():
    e = np.zeros((D_MODEL, LANES), np.float32)
    for h in range(N_HEADS):
        e[HEAD_DIM * h:HEAD_DIM * (h + 1), 8 * h:8 * (h + 1)] = 1.0
    return jnp.asarray(e, BF16)


def _attn_tables(d):
    g_n, rq = PATTERNS[d]
    q_n = g_n * rq
    gq, iq = np.arange(q_n) // rq, np.arange(q_n) % rq

    def tab(kn, base):
        k_n = g_n * kn
        gk, jk = np.arange(k_n) // kn, np.arange(k_n) % kn
        delta = g_n * (base + iq[:, None] - jk[None, :]) + gq[:, None] - gk[None, :]
        valid = (delta >= 0) & (delta <= QB)
        dist = np.where(valid, d * delta, 0).astype(np.float32)
        madd = np.where(valid, 0.0, NEG).astype(np.float32)
        return dist, madd

    d0, m0 = tab(rq if g_n == 1 else 2 * rq, 0)
    d1, m1 = tab(2 * rq, rq)
    return d0, m0, d1, m1


def _alibi_slopes():
    return jnp.exp2(-8.0 * jnp.arange(1, N_HEADS + 1, dtype=F32) / N_HEADS)


def _to_residue_major(x, tgt, after=0.0):
    s_n, c_n = x.shape
    lr = s_n // N_RES
    pm = (_perm_matrix().astype(F32) + after).astype(BF16)

    def body(p_ref, x_ref, t_ref, xo_ref, to_ref):
        pm = p_ref[...]
        xo_ref[...] = _select_rows(pm, x_ref[...]).reshape(16, 16, c_n)
        to_ref[...] = _select_rows(pm, t_ref[...]).reshape(16, 16, c_n)

    nat = pl.BlockSpec((256, c_n), lambda i: (i, 0))
    res = pl.BlockSpec((16, 16, c_n), lambda i: (0, i, 0))
    xo, to = pl.pallas_call(
        body, grid=(s_n // 256,),
        in_specs=[pl.BlockSpec((256, 256), lambda i: (0, 0)), nat, nat],
        out_specs=[res, res],
        out_shape=[jax.ShapeDtypeStruct((16, lr, c_n), F32)] * 2,
        name="perm_in",
    )(pm, x, tgt)
    return xo.reshape(s_n, c_n), to.reshape(s_n, c_n)


def _to_natural(gx_lo, gx_hi):
    half_rows, c_n = gx_lo.shape
    lr = half_rows // (N_RES // 2)

    def body(p_ref, lo_ref, hi_ref, o_ref):
        g = jnp.concatenate([lo_ref[...], hi_ref[...]], axis=0)
        o_ref[...] = _select_rows(p_ref[...], g.reshape(256, c_n))

    half = pl.BlockSpec((8, 16, c_n), lambda i: (0, i, 0))
    return pl.pallas_call(
        body, grid=(lr // 16,),
        in_specs=[pl.BlockSpec((256, 256), lambda i: (0, 0)), half, half],
        out_specs=pl.BlockSpec((256, c_n), lambda i: (i, 0)),
        out_shape=jax.ShapeDtypeStruct((2 * half_rows, c_n), F32),
        name="perm_out",
    )(_perm_matrix(), gx_lo.reshape(8, lr, c_n), gx_hi.reshape(8, lr, c_n))


def _rms_in(xp, norm_g):
    s_n, c_n = xp.shape
    tm = 512

    def body(x_ref, g_ref, u_ref, ut_ref):
        x = x_ref[...]
        r = lax.rsqrt(jnp.mean(x * x, axis=-1, keepdims=True) + EPS)
        u = x * r * g_ref[...]
        u_ref[...] = u.astype(u_ref.dtype)
        ut_ref[...] = u.T.astype(ut_ref.dtype)

    return pl.pallas_call(
        body, grid=(s_n // tm,),
        in_specs=[pl.BlockSpec((tm, c_n), lambda i: (i, 0)), pl.BlockSpec((1, c_n), lambda i: (0, 0))],
        out_specs=[pl.BlockSpec((tm, c_n), lambda i: (i, 0)), pl.BlockSpec((c_n, tm), lambda i: (0, i))],
        out_shape=[jax.ShapeDtypeStruct((s_n, c_n), ACT_DTYPE), jax.ShapeDtypeStruct((c_n, s_n), ACT_DTYPE)],
        name="rms_in",
    )(xp, norm_g)


def _in_proj(u, w4):
    s_n = u.shape[0]
    tn, cm = 512, 512
    per = SHARD_COLS // tn

    def body(a_ref, b_ref, o_ref):
        b = b_ref[...]
        for c in range(s_n // cm):
            o_ref[c * cm:(c + 1) * cm, :] = _dot(a_ref[c * cm:(c + 1) * cm, :], b).astype(o_ref.dtype)

    return pl.pallas_call(
        body, grid=(IN_COLS // tn,),
        in_specs=[pl.BlockSpec((s_n, D_MODEL), lambda n: (0, 0)),
                  pl.BlockSpec((None, D_MODEL, tn), lambda n: (n // per, 0, n % per))],
        out_specs=pl.BlockSpec((s_n, tn), lambda n: (0, n)),
        out_shape=jax.ShapeDtypeStruct((s_n, IN_COLS), ACT_DTYPE),
        name="in_proj",
    )(u, w4)


def _conv_terms(xc_ref, cg_ref, r, row, lr):
    def a_of(q):
        return cg_ref[q].astype(F32) * xc_ref[q].astype(F32)

    def shift_down(v):
        return jnp.where(row >= 1, pltpu.roll(v, 1, 0), 0.0)

    a = a_of(r)
    am1 = a_of(r - 1) if r >= 1 else shift_down(a_of(N_RES - 1))
    am2 = a_of(r - 2) if r >= 2 else shift_down(a_of(N_RES - 2 + r))
    return a, am1, am2


def _conv_fwd(proj, conv_w):
    s_n = proj.shape[0]
    lr = s_n // N_RES
    pv = proj.reshape(N_RES, lr, IN_COLS)

    def body(xc_ref, bg_ref, cg_ref, zc_ref, w_ref, hc_ref, hct_ref):
        w = w_ref[...]
        row = lax.broadcasted_iota(jnp.int32, (lr, LANES), 0)
        for r in range(N_RES):
            a, am1, am2 = _conv_terms(xc_ref, cg_ref, r, row, lr)
            c = w[0:1] * am2 + w[1:2] * am1 + w[2:3] * a
            z = zc_ref[r].astype(F32)
            hc = z * _sigmoid(z) * bg_ref[r].astype(F32) * c
            hc_ref[r] = hc.astype(hc_ref.dtype)
            hct_ref[:, r * lr:(r + 1) * lr] = hc.T.astype(hct_ref.dtype)

    def col(part):
        return pl.BlockSpec((N_RES, lr, LANES), lambda j: (0, 0, part * 8 + j))

    hc, hct = pl.pallas_call(
        body, grid=(D_MODEL // LANES,),
        in_specs=[col(0), col(1), col(2), col(3), pl.BlockSpec((3, LANES), lambda j: (0, j))],
        out_specs=[pl.BlockSpec((N_RES, lr, LANES), lambda j: (0, 0, j)),
                   pl.BlockSpec((LANES, s_n), lambda j: (j, 0))],
        out_shape=[jax.ShapeDtypeStruct((N_RES, lr, D_MODEL), ACT_DTYPE),
                   jax.ShapeDtypeStruct((D_MODEL, s_n), ACT_DTYPE)],
        name="conv_fwd",
    )(pv, pv, pv, pv, conv_w)
    return hc.reshape(s_n, D_MODEL), hct


RES_PER_STEP = 8
FWD_BATCH = {1: 8, 4: 8, 16: RES_PER_STEP}
BWD_BATCH = {1: 8, 4: 8, 16: RES_PER_STEP}

_BNT = (((2,), (2,)), ((0,), (0,)))
_BNN = (((2,), (1,)), ((0,), (0,)))


def _bdot(a, b, dims):
    return lax.dot_general(a.astype(MXU_DTYPE), b.astype(MXU_DTYPE), dims, preferred_element_type=F32)


def _pattern_view_shape(s_n, c_n, g_n, lead=()):
    lr = s_n // N_RES
    return (*lead, 4, 4, lr, c_n) if g_n == 4 else (*lead, N_RES, lr, c_n)


def _pattern_view(a, g_n, lead=()):
    return a.reshape(_pattern_view_shape(a.shape[-2], a.shape[-1], g_n, lead))


def _pattern_grid(g_n):
    return (N_RES // RES_PER_STEP if g_n == 1 else N_RES // g_n, HP)


def _pattern_spec(g_n, lr, col_of_hp, lead=()):
    z = (0,) * len(lead)
    if g_n == 16:
        return pl.BlockSpec((*lead, 16, lr, LANES), lambda r, hp: (*z, 0, 0, col_of_hp(hp)))
    if g_n == 4:
        return pl.BlockSpec((*lead, 4, None, lr, LANES), lambda r, hp: (*z, 0, r, 0, col_of_hp(hp)))
    return pl.BlockSpec((*lead, RES_PER_STEP, lr, LANES), lambda r, hp: (*z, r, 0, col_of_hp(hp)))


def _aligned(start, m):
    return start if isinstance(start, int) else pl.multiple_of(start, m)


class _Units:
    def __init__(self, g_n, rq):
        self.g_n, self.rq = g_n, rq
        self.per_res, self.paired = g_n == 1, rq == 8

    def plan(self, lr, size):
        if self.per_res:
            return [0], lr // self.rq - 1, lambda j: [pl.multiple_of(j * self.rq, self.rq)]
        step = 16 if self.paired else self.rq
        per = min(size // 2 if self.paired else size, lr // step)
        assert (lr // step) % per == 0
        return ([i * step for i in range(per)], lr // step // per - 1,
                lambda j: [pl.multiple_of((j * per + i) * step, step) for i in range(per)])

    def count(self, qs):
        return RES_PER_STEP if self.per_res else len(qs) * (2 if self.paired else 1)

    def _split(self, tiles, lo, rows):
        return tiles[:, lo:lo + rows].reshape(self.g_n * rows, LANES)

    def load_q(self, ref, qs):
        rq = self.rq
        if self.per_res:
            return ref[:, pl.ds(qs[0], rq), :]
        if self.paired:
            tiles = [ref[:, pl.ds(q, 16), :].astype(F32) for q in qs]
            return jnp.stack([self._split(t, lo, 8) for t in tiles for lo in (0, 8)])
        return jnp.stack([ref[:, pl.ds(q, rq), :].reshape(self.g_n * rq, LANES) for q in qs])

    def _key_rows(self, q, at_start):
        return (0, 2 * self.rq) if at_start else (_aligned(q - self.rq, self.rq), 2 * self.rq)

    def load_k(self, ref, qs, first):
        rq = self.rq
        if self.per_res:
            return ref[:, pl.ds(0, rq), :] if first else ref[:, pl.ds(_aligned(qs[0] - rq, rq), 2 * rq), :]
        if self.paired:
            out = []
            for i, q in enumerate(qs):
                if first and i == 0:
                    t = ref[:, 0:16, :].astype(F32)
                    out += [self._split(t, 0, 16)] * 2
                else:
                    t = ref[:, pl.ds(_aligned(q - 16, 16), 32), :].astype(F32)
                    out += [self._split(t, 8, 16), self._split(t, 16, 16)]
            return jnp.stack(out)
        rows = [self._key_rows(q, first and i == 0) for i, q in enumerate(qs)]
        return jnp.stack([ref[:, pl.ds(k0, n), :].reshape(self.g_n * n, LANES) for k0, n in rows])

    def store_q(self, ref, qs, val, add=False, lead=()):
        if self.per_res:
            pieces = [(qs[0], self.rq, val)]
        elif self.paired:
            pieces = [(q, 16, jnp.concatenate([val[2 * i].reshape(self.g_n, 8, LANES),
                                               val[2 * i + 1].reshape(self.g_n, 8, LANES)], axis=1))
                      for i, q in enumerate(qs)]
        else:
            pieces = [(q, self.rq, val[i].reshape(self.g_n, self.rq, LANES)) for i, q in enumerate(qs)]
        for start, rows, v in pieces:
            idx = (*lead, slice(None), pl.ds(start, rows), slice(None))
            ref[idx] = (ref[idx] + v if add else v).astype(ref.dtype)

    def add_k(self, ref, qs, val, first):
        rq = self.rq
        if self.per_res:
            k0, n = (0, rq) if first else (_aligned(qs[0] - rq, rq), 2 * rq)
            ref[:, pl.ds(k0, n), :] += val
            return
        if self.paired:
            starts = [s for i, q in enumerate(qs)
                      for s in ((0, 0) if first and i == 0 else (_aligned(q - 8, 8), q))]
            rows = [(s, 16) for s in starts]
        else:
            rows = [self._key_rows(q, first and i == 0) for i, q in enumerate(qs)]
        for b, (k0, n) in enumerate(rows):
            ref[:, pl.ds(k0, n), :] += val[b].reshape(self.g_n, n, LANES)


def _batch_bias(un, qs, at_start, first_ref, general_ref):
    if not at_start:
        return general_ref[...][None]
    if un.per_res:
        return first_ref[...][None]
    return jnp.concatenate([first_ref[...][None]] + [general_ref[...][None]] * (un.count(qs) - 1), axis=0)


def _stack_heads(x, low):
    zero = jnp.zeros_like(x)
    return jnp.concatenate([jnp.where(low, x, zero), jnp.where(low, zero, x)], axis=1)


def _attn_fwd(proj, slopes, d):
    g_n, rq = PATTERNS[d]
    un = _Units(g_n, rq)
    s_n = proj.shape[0]
    lr = s_n // N_RES
    nb = lr // rq
    q_n = g_n * rq
    d0, m0, d1, m1 = _attn_tables(d)
    first, n_more, later = un.plan(lr, FWD_BATCH[d])

    def body(sl_ref, q_ref, k_ref, v_ref, d0_ref, m0_ref, d1_ref, m1_ref, o_ref, lse_ref, b0_ref, b1_ref):
        hp = pl.program_id(1)

        @pl.when(hp == 0)
        def _():
            lse_ref[...] = jnp.zeros(lse_ref.shape, F32)

        for h in (0, 1):
            slope = sl_ref[2 * hp + h]
            b0_ref[h * q_n:(h + 1) * q_n, :] = m0_ref[...] - slope * d0_ref[...]
            b1_ref[h * q_n:(h + 1) * q_n, :] = m1_ref[...] - slope * d1_ref[...]

        lane = lax.broadcasted_iota(jnp.int32, (1, q_n, LANES), 2)
        low = lane < HEAD_DIM
        grp = lane // 8

        def batch(qs, at_start):
            qq = _stack_heads(un.load_q(q_ref, qs) * 0.125, low)
            s = _bdot(qq, un.load_k(k_ref, qs, at_start), _BNT) + _batch_bias(un, qs, at_start, b0_ref, b1_ref)
            m = jnp.max(s, axis=2, keepdims=True)
            p = jnp.exp(s - m)
            l = jnp.sum(p, axis=2, keepdims=True)
            o = _bdot(p, un.load_k(v_ref, qs, at_start), _BNN) * (1.0 / l)
            lse = m + jnp.log(l)
            un.store_q(o_ref, qs, jnp.where(low, o[:, :q_n], o[:, q_n:]))
            upd = jnp.where(grp == 2 * hp, lse[:, :q_n], 0.0) + jnp.where(grp == 2 * hp + 1, lse[:, q_n:], 0.0)
            un.store_q(lse_ref, qs, upd, add=True)

        batch(first, True)

        def more(j, carry):
            batch(later(j), False)
            return carry

        lax.fori_loop(1, 1 + n_more, more, 0)

    pv = _pattern_view(proj, g_n)
    full = lambda a: pl.BlockSpec(a.shape, lambda r, hp: (0, 0))
    o, lse = pl.pallas_call(
        body, grid=_pattern_grid(g_n),
        in_specs=[pl.BlockSpec(memory_space=pltpu.SMEM),
                  _pattern_spec(g_n, lr, lambda hp: 32 + hp),
                  _pattern_spec(g_n, lr, lambda hp: 40 + hp),
                  _pattern_spec(g_n, lr, lambda hp: 48 + hp),
                  full(d0), full(m0), full(d1), full(m1)],
        out_specs=[_pattern_spec(g_n, lr, lambda hp: hp), _pattern_spec(g_n, lr, lambda hp: 0)],
        out_shape=[jax.ShapeDtypeStruct(_pattern_view_shape(s_n, D_MODEL, g_n), F32),
                   jax.ShapeDtypeStruct(_pattern_view_shape(s_n, LANES, g_n), F32)],
        scratch_shapes=[pltpu.VMEM((2 * q_n, d0.shape[1]), F32), pltpu.VMEM((2 * q_n, 2 * q_n), F32)],
        name=f"attn_fwd_d{d}",
    )(slopes, pv, pv, pv, d0, m0, d1, m1)
    return o.reshape(s_n, D_MODEL), lse.reshape(s_n, LANES)


def _attn_combine(outs, lses, proj):
    s_n = proj.shape[0]
    tm = 512

    def body(o1_ref, o2_ref, o3_ref, l1_ref, l2_ref, l3_ref, za_ref, e_ref, o_ref, lse_ref, ha_ref, hat_ref):
        ls = [l1_ref[...], l2_ref[...], l3_ref[...]]
        mx = jnp.maximum(jnp.maximum(ls[0], ls[1]), ls[2])
        den = sum(jnp.exp(l - mx) for l in ls)
        lse = mx + jnp.log(den)
        lse_ref[...] = lse
        o = jnp.zeros((tm, D_MODEL), F32)
        for l, oref in zip(ls, (o1_ref, o2_ref, o3_ref)):
            o = o + _select_cols(jnp.exp(l - lse), e_ref[...]) * oref[...]
        o_ref[...] = o
        z = za_ref[...].astype(F32)
        ha = z * _sigmoid(z) * o
        ha_ref[...] = ha.astype(ha_ref.dtype)
        hat_ref[...] = ha.T.astype(hat_ref.dtype)

    row = lambda w: pl.BlockSpec((tm, w), lambda i: (i, 0))
    return pl.pallas_call(
        body, grid=(s_n // tm,),
        in_specs=[row(D_MODEL)] * 3 + [row(LANES)] * 3
        + [pl.BlockSpec((tm, D_MODEL), lambda i: (i, 7)), pl.BlockSpec((LANES, D_MODEL), lambda i: (0, 0))],
        out_specs=[row(D_MODEL), row(LANES), row(D_MODEL), pl.BlockSpec((D_MODEL, tm), lambda i: (0, i))],
        out_shape=[jax.ShapeDtypeStruct((s_n, D_MODEL), F32), jax.ShapeDtypeStruct((s_n, LANES), F32),
                   jax.ShapeDtypeStruct((s_n, D_MODEL), ACT_DTYPE), jax.ShapeDtypeStruct((D_MODEL, s_n), ACT_DTYPE)],
        name="attn_combine",
    )(*outs, *lses, proj, _head_expand_matrix())


def _gates(gc_ref, ga_ref, b_ref):
    b = b_ref[...]
    gc = _sigmoid(gc_ref[...].astype(F32) + b[:, :D_MODEL])
    ga = _sigmoid(ga_ref[...].astype(F32) + b[:, D_MODEL:])
    return gc, ga


def _merge_loss(hc, ha, woc, woa, wo, proj, b_merge, xp, final_g, tgt):
    s_n = xp.shape[0]
    tm = 512

    def body(hc_ref, ha_ref, woc_ref, woa_ref, wo_ref, gc_ref, ga_ref, b_ref, x_ref, gf_ref, t_ref,
             yc_ref, ya_ref, mg_ref, mgt_ref, dh_ref, dhb_ref, dgf_ref, loss_ref):
        i = pl.program_id(0)

        @pl.when(i == 0)
        def _():
            dgf_ref[...] = jnp.zeros(dgf_ref.shape, F32)
            loss_ref[...] = jnp.zeros(loss_ref.shape, F32)

        yc = _dot(hc_ref[...], woc_ref[...])
        ya = _dot(ha_ref[...], woa_ref[...])
        gc, ga = _gates(gc_ref, ga_ref, b_ref)
        mg = gc * yc + ga * ya
        yc_ref[...] = yc.astype(yc_ref.dtype)
        ya_ref[...] = ya.astype(ya_ref.dtype)
        mg_ref[...] = mg.astype(mg_ref.dtype)
        mgt_ref[...] = mg.T.astype(mgt_ref.dtype)
        h2 = x_ref[...] + _dot(mg, wo_ref[...])
        r2 = lax.rsqrt(jnp.mean(h2 * h2, axis=-1, keepdims=True) + EPS)
        nrm = h2 * r2
        gf = gf_ref[...]
        err = nrm * gf - t_ref[...]
        e2 = (err * err).reshape(tm // 8, 8, D_MODEL).sum(axis=0)
        loss_ref[...] += sum(e2[:, c * LANES:(c + 1) * LANES] for c in range(D_MODEL // LANES))
        dy = err * (1.0 / D_MODEL)
        dgf_ref[...] += jnp.sum(dy * nrm, axis=0, keepdims=True)
        dn = dy * gf
        dh2 = r2 * (dn - nrm * jnp.mean(dn * nrm, axis=-1, keepdims=True))
        dh_ref[...] = dh2
        dhb_ref[...] = dh2.astype(dhb_ref.dtype)

    row = pl.BlockSpec((tm, D_MODEL), lambda i: (i, 0))
    wsp = pl.BlockSpec((D_MODEL, D_MODEL), lambda i: (0, 0))
    vec = lambda w: pl.BlockSpec((1, w), lambda i: (0, 0))
    act = jax.ShapeDtypeStruct((s_n, D_MODEL), ACT_DTYPE)
    return pl.pallas_call(
        body, grid=(s_n // tm,),
        in_specs=[row, row, wsp, wsp, wsp,
                  pl.BlockSpec((tm, D_MODEL), lambda i: (i, 8)), pl.BlockSpec((tm, D_MODEL), lambda i: (i, 9)),
                  vec(2 * D_MODEL), row, vec(D_MODEL), row],
        out_specs=[row, row, row, pl.BlockSpec((D_MODEL, tm), lambda i: (0, i)), row, row,
                   vec(D_MODEL), pl.BlockSpec((8, LANES), lambda i: (0, 0))],
        out_shape=[act, act, act, jax.ShapeDtypeStruct((D_MODEL, s_n), ACT_DTYPE),
                   jax.ShapeDtypeStruct((s_n, D_MODEL), F32), act,
                   jax.ShapeDtypeStruct((1, D_MODEL), F32), jax.ShapeDtypeStruct((8, LANES), F32)],
        name="merge_loss",
    )(hc, ha, woc, woa, wo, proj, proj, b_merge, xp, final_g, tgt)


def _merge_bwd(dh2b, wo, woc, woa, yc, ya, proj, b_merge, o):
    s_n = dh2b.shape[0]
    tm = 512

    def body(dh_ref, wo_ref, woc_ref, woa_ref, yc_ref, ya_ref, gc_ref, ga_ref, b_ref, o_ref, za_ref, e_ref,
             dyc_ref, dya_ref, dhc_ref, do_ref, dsum_ref, db3_ref, dbias_ref):
        i = pl.program_id(0)

        @pl.when(i == 0)
        def _():
            dbias_ref[...] = jnp.zeros(dbias_ref.shape, F32)

        dmg = _dot_nt(dh_ref[...], wo_ref[...])
        gc, ga = _gates(gc_ref, ga_ref, b_ref)
        dgc = dmg * yc_ref[...].astype(F32) * gc * (1.0 - gc)
        dga = dmg * ya_ref[...].astype(F32) * ga * (1.0 - ga)
        dbias_ref[:, :D_MODEL] += jnp.sum(dgc, axis=0, keepdims=True)
        dbias_ref[:, D_MODEL:] += jnp.sum(dga, axis=0, keepdims=True)
        dyc = dmg * gc
        dya = dmg * ga
        dyc_ref[...] = dyc.astype(dyc_ref.dtype)
        dya_ref[...] = dya.astype(dya_ref.dtype)
        dhc_ref[...] = _dot_nt(dyc, woc_ref[...]).astype(dhc_ref.dtype)
        dha = _dot_nt(dya, woa_ref[...])
        z = za_ref[...].astype(F32)
        sg = _sigmoid(z)
        ov = o_ref[...]
        dout = dha * z * sg
        do_ref[...] = dout.astype(do_ref.dtype)
        dsum_ref[...] = _select_cols(dout * ov, e_ref[...])
        db3_ref[0] = (dha * ov * sg * (1.0 + z * (1.0 - sg))).astype(db3_ref.dtype)
        db3_ref[1] = dgc.astype(db3_ref.dtype)
        db3_ref[2] = dga.astype(db3_ref.dtype)

    row = pl.BlockSpec((tm, D_MODEL), lambda i: (i, 0))
    wsp = pl.BlockSpec((D_MODEL, D_MODEL), lambda i: (0, 0))
    act = jax.ShapeDtypeStruct((s_n, D_MODEL), ACT_DTYPE)
    return pl.pallas_call(
        body, grid=(s_n // tm,),
        in_specs=[row, wsp, wsp, wsp, row, row,
                  pl.BlockSpec((tm, D_MODEL), lambda i: (i, 8)), pl.BlockSpec((tm, D_MODEL), lambda i: (i, 9)),
                  pl.BlockSpec((1, 2 * D_MODEL), lambda i: (0, 0)), row,
                  pl.BlockSpec((tm, D_MODEL), lambda i: (i, 7)), pl.BlockSpec((D_MODEL, LANES), lambda i: (0, 0))],
        out_specs=[row, row, row, row, pl.BlockSpec((tm, LANES), lambda i: (i, 0)),
                   pl.BlockSpec((3, tm, D_MODEL), lambda i: (0, i, 0)),
                   pl.BlockSpec((1, 2 * D_MODEL), lambda i: (0, 0))],
        out_shape=[act, act, act, act, jax.ShapeDtypeStruct((s_n, LANES), F32),
                   jax.ShapeDtypeStruct((3, s_n, D_MODEL), ACT_DTYPE),
                   jax.ShapeDtypeStruct((1, 2 * D_MODEL), F32)],
        name="merge_bwd",
    )(dh2b, wo, woc, woa, yc, ya, proj, proj, b_merge, o, proj, _head_sum_matrix())


def _mm_lhs_resident(a, b, tn, name):
    m_n, k_n = a.shape
    n_n = b.shape[1]

    def body(a_ref, b_ref, o_ref):
        o_ref[...] = _dot(a_ref[...], b_ref[...])

    return pl.pallas_call(
        body, grid=(n_n // tn,),
        in_specs=[pl.BlockSpec((m_n, k_n), lambda n: (0, 0)), pl.BlockSpec((k_n, tn), lambda n: (0, n))],
        out_specs=pl.BlockSpec((m_n, tn), lambda n: (0, n)),
        out_shape=jax.ShapeDtypeStruct((m_n, n_n), F32),
        name=name,
    )(a, b)


def _conv_bwd(proj, conv_w, dhc):
    s_n = proj.shape[0]
    lr = s_n // N_RES
    pv = proj.reshape(N_RES, lr, IN_COLS)

    def body(xc_ref, bg_ref, cg_ref, zc_ref, w_ref, dhc_ref, da4_ref, dw_ref, dc_ref):
        w = w_ref[...]
        row = lax.broadcasted_iota(jnp.int32, (lr, LANES), 0)
        dw = [jnp.zeros((1, LANES), F32) for _ in range(3)]
        for r in range(N_RES):
            a, am1, am2 = _conv_terms(xc_ref, cg_ref, r, row, lr)
            c = w[0:1] * am2 + w[1:2] * am1 + w[2:3] * a
            z = zc_ref[r].astype(F32)
            sg = _sigmoid(z)
            sz = z * sg
            bg = bg_ref[r].astype(F32)
            dh = dhc_ref[r].astype(F32)
            da4_ref[1, r] = (dh * sz * c).astype(da4_ref.dtype)
            da4_ref[3, r] = (dh * bg * c * sg * (1.0 + z * (1.0 - sg))).astype(da4_ref.dtype)
            dc = dh * sz * bg
            dc_ref[r] = dc
            dw[0] = dw[0] + jnp.sum(dc * am2, axis=0, keepdims=True)
            dw[1] = dw[1] + jnp.sum(dc * am1, axis=0, keepdims=True)
            dw[2] = dw[2] + jnp.sum(dc * a, axis=0, keepdims=True)
        dw_ref[0:1, :] = dw[0]
        dw_ref[1:2, :] = dw[1]
        dw_ref[2:3, :] = dw[2]

        def shift_up(v):
            return jnp.where(row < lr - 1, pltpu.roll(v, lr - 1, 0), 0.0)

        for r in range(N_RES):
            dp1 = dc_ref[r + 1] if r + 1 < N_RES else shift_up(dc_ref[0])
            dp2 = dc_ref[r + 2] if r + 2 < N_RES else shift_up(dc_ref[r + 2 - N_RES])
            da = w[2:3] * dc_ref[r] + w[1:2] * dp1 + w[0:1] * dp2
            da4_ref[0, r] = (da * cg_ref[r].astype(F32)).astype(da4_ref.dtype)
            da4_ref[2, r] = (da * xc_ref[r].astype(F32)).astype(da4_ref.dtype)

    def col(part):
        return pl.BlockSpec((N_RES, lr, LANES), lambda j: (0, 0, part * 8 + j))

    da4, dw = pl.pallas_call(
        body, grid=(D_MODEL // LANES,),
        in_specs=[col(0), col(1), col(2), col(3), pl.BlockSpec((3, LANES), lambda j: (0, j)),
                  pl.BlockSpec((N_RES, lr, LANES), lambda j: (0, 0, j))],
        out_specs=[pl.BlockSpec((4, N_RES, lr, LANES), lambda j: (0, 0, 0, j)),
                   pl.BlockSpec((3, LANES), lambda j: (0, j))],
        out_shape=[jax.ShapeDtypeStruct((4, N_RES, lr, D_MODEL), ACT_DTYPE),
                   jax.ShapeDtypeStruct((3, D_MODEL), F32)],
        scratch_shapes=[pltpu.VMEM((N_RES, lr, LANES), F32)],
        name="conv_bwd",
    )(pv, pv, pv, pv, conv_w, dhc.reshape(N_RES, lr, D_MODEL))
    return da4.reshape(4, s_n, D_MODEL), dw


def _attn_bwd(proj, dout, lse, dsum, slopes, d):
    g_n, rq = PATTERNS[d]
    un = _Units(g_n, rq)
    s_n = proj.shape[0]
    lr = s_n // N_RES
    nb = lr // rq
    q_n = g_n * rq
    d0, m0, d1, m1 = (np.ascontiguousarray(t.T) for t in _attn_tables(d))
    first, n_more, later = un.plan(lr, BWD_BATCH[d])
    bsz = un.count(first)
    gd = RES_PER_STEP if un.per_res else g_n

    def body(sl_ref, q_ref, k_ref, v_ref, do_ref, lse_ref, ds_ref, d0_ref, m0_ref, d1_ref, m1_ref, out_ref,
             b0_ref, b1_ref, lt_ref, dt_ref, dk_ref, dv_ref):
        hp = pl.program_id(1)
        for h in (0, 1):
            slope = sl_ref[2 * hp + h]
            b0_ref[:, h * q_n:(h + 1) * q_n] = m0_ref[...] - slope * d0_ref[...]
            b1_ref[:, h * q_n:(h + 1) * q_n] = m1_ref[...] - slope * d1_ref[...]
        dk_ref[...] = jnp.zeros(dk_ref.shape, F32)
        dv_ref[...] = jnp.zeros(dv_ref.shape, F32)
        low = lax.broadcasted_iota(jnp.int32, (1, q_n, LANES), 2) < HEAD_DIM
        row16 = pl.multiple_of(16 * hp, 16)

        def query_rows(stat_ref, t_ref, qs):
            tiles = un.load_q(stat_ref, qs)
            for b in range(bsz):
                t_ref[b] = tiles[b].T
            t16 = t_ref[:, pl.ds(row16, 16), :]
            return jnp.concatenate([t16[:, 0:1, :], t16[:, 8:9, :]], axis=2)

        def batch(qs, at_start):
            qq = _stack_heads(un.load_q(q_ref, qs) * 0.125, low)
            dd = _stack_heads(un.load_q(do_ref, qs), low)
            ks = un.load_k(k_ref, qs, at_start)
            vs = un.load_k(v_ref, qs, at_start)
            lrow = query_rows(lse_ref, lt_ref, qs)
            drow = query_rows(ds_ref, dt_ref, qs)
            pt = jnp.exp(_bdot(ks, qq, _BNT) + _batch_bias(un, qs, at_start, b0_ref, b1_ref) - lrow)
            dst = pt * (_bdot(vs, dd, _BNT) - drow)
            un.add_k(dv_ref, qs, _bdot(pt, dd, _BNN), at_start)
            un.add_k(dk_ref, qs, _bdot(dst, qq, _BNN), at_start)
            dq = _bdot(jnp.swapaxes(dst, 1, 2), ks, _BNN)
            un.store_q(out_ref, qs, jnp.where(low, dq[:, :q_n], dq[:, q_n:]) * 0.125, lead=(0,))

        batch(first, True)

        def more(j, carry):
            batch(later(j), False)
            return carry

        lax.fori_loop(1, 1 + n_more, more, 0)
        out_ref[1] = dk_ref[...].astype(out_ref.dtype)
        out_ref[2] = dv_ref[...].astype(out_ref.dtype)

    pv = _pattern_view(proj, g_n)
    full = lambda a: pl.BlockSpec(a.shape, lambda r, hp: (0, 0))
    out = pl.pallas_call(
        body, grid=_pattern_grid(g_n),
        in_specs=[pl.BlockSpec(memory_space=pltpu.SMEM),
                  _pattern_spec(g_n, lr, lambda hp: 32 + hp),
                  _pattern_spec(g_n, lr, lambda hp: 40 + hp),
                  _pattern_spec(g_n, lr, lambda hp: 48 + hp),
                  _pattern_spec(g_n, lr, lambda hp: hp),
                  _pattern_spec(g_n, lr, lambda hp: 0),
                  _pattern_spec(g_n, lr, lambda hp: 0),
                  full(d0), full(m0), full(d1), full(m1)],
        out_specs=_pattern_spec(g_n, lr, lambda hp: hp, lead=(3,)),
        out_shape=jax.ShapeDtypeStruct(_pattern_view_shape(s_n, D_MODEL, g_n, lead=(3,)), ACT_DTYPE),
        scratch_shapes=[pltpu.VMEM((d0.shape[0], 2 * q_n), F32), pltpu.VMEM((2 * q_n, 2 * q_n), F32),
                        pltpu.VMEM((bsz, LANES, q_n), F32), pltpu.VMEM((bsz, LANES, q_n), F32),
                        pltpu.VMEM((gd, lr, LANES), F32), pltpu.VMEM((gd, lr, LANES), F32)],
        name=f"attn_bwd_d{d}",
    )(slopes, pv, pv, pv, _pattern_view(dout, g_n), _pattern_view(lse, g_n), _pattern_view(dsum, g_n),
      d0, m0, d1, m1)
    return out.reshape(3, s_n, D_MODEL)


def _sum3(a, b, c):
    _, s_n, c_n = a.shape
    tm = 512

    def body(a_ref, b_ref, c_ref, o_ref):
        o_ref[...] = (a_ref[...] + b_ref[...] + c_ref[...]).astype(o_ref.dtype)

    spec = pl.BlockSpec((1, tm, c_n), lambda p, i: (p, i, 0))
    return pl.pallas_call(
        body, grid=(3, s_n // tm), in_specs=[spec] * 3, out_specs=spec,
        out_shape=jax.ShapeDtypeStruct(a.shape, ACT_DTYPE), name="sum_dqkv",
    )(a, b, c)


def _part_index(step, per, lo, n):
    return jnp.clip(step // per - lo, 0, n - 1)


def _dw_in(ut, da4, dc3, db3):
    s_n = ut.shape[1]
    tn = 512
    per = D_MODEL // tn
    shard_blocks = SHARD_COLS // tn

    def body(a_ref, p0_ref, p1_ref, p2_ref, o_ref):
        part = pl.program_id(0) // per

        @pl.when(part < 4)
        def _():
            o_ref[...] = _dot(a_ref[...], p0_ref[...])

        @pl.when((part >= 4) & (part < 7))
        def _():
            o_ref[...] = _dot(a_ref[...], p1_ref[...])

        @pl.when(part >= 7)
        def _():
            o_ref[...] = _dot(a_ref[...], p2_ref[...])

    def pspec(lo, n):
        return pl.BlockSpec((None, s_n, tn), lambda j: (_part_index(j, per, lo, n), 0, j % per))

    return pl.pallas_call(
        body, grid=(IN_COLS // tn,),
        in_specs=[pl.BlockSpec((D_MODEL, s_n), lambda j: (0, 0), pipeline_mode=pl.Buffered(1)),
                  pspec(0, 4), pspec(4, 3), pspec(7, 3)],
        out_specs=pl.BlockSpec((None, D_MODEL, tn), lambda j: (j // shard_blocks, 0, j % shard_blocks)),
        out_shape=jax.ShapeDtypeStruct((4, D_MODEL, SHARD_COLS), F32),
        name="dw_in",
    )(ut, da4, dc3, db3)


def _input_grad(da4, dc3, db3, w4, xp, norm_g, dh2, row0, rows):
    tm, tk = 256, 512
    per = D_MODEL // tk
    shard_blocks = SHARD_COLS // tk
    m0 = row0 // tm

    def body(p0_ref, p1_ref, p2_ref, w_ref, x_ref, g_ref, dh_ref, gx_ref, dg_ref):
        @pl.when(pl.program_id(0) == 0)
        def _():
            dg_ref[...] = jnp.zeros(dg_ref.shape, F32)

        du = None
        for k in range(IN_COLS // tk):
            part, cols = k // per, pl.ds((k % per) * tk, tk)
            ref, slot = (p0_ref, part) if part < 4 else (p1_ref, part - 4) if part < 7 else (p2_ref, part - 7)
            d = _dot_nt(ref[slot, :, cols], w_ref[k // shard_blocks, :, pl.ds((k % shard_blocks) * tk, tk)])
            du = d if du is None else du + d
        x = x_ref[...]
        r = lax.rsqrt(jnp.mean(x * x, axis=-1, keepdims=True) + EPS)
        nrm = x * r
        dg_ref[...] += jnp.sum(du * nrm, axis=0, keepdims=True)
        dn = du * g_ref[...]
        gx_ref[...] = dh_ref[...] + r * (dn - nrm * jnp.mean(dn * nrm, axis=-1, keepdims=True))

    def pspec(n):
        return pl.BlockSpec((n, tm, D_MODEL), lambda m: (0, m0 + m, 0))

    row_in = pl.BlockSpec((tm, D_MODEL), lambda m: (m0 + m, 0))
    vec = pl.BlockSpec((1, D_MODEL), lambda m: (0, 0))
    return pl.pallas_call(
        body, grid=(rows // tm,),
        in_specs=[pspec(4), pspec(3), pspec(3),
                  pl.BlockSpec(w4.shape, lambda m: (0, 0, 0), pipeline_mode=pl.Buffered(1)),
                  row_in, vec, row_in],
        out_specs=[pl.BlockSpec((tm, D_MODEL), lambda m: (m, 0)), vec],
        out_shape=[jax.ShapeDtypeStruct((rows, D_MODEL), F32), jax.ShapeDtypeStruct((1, D_MODEL), F32)],
        name="input_grad",
    )(da4, dc3, db3, w4, xp, norm_g, dh2)


class _Step:
    def __init__(self, x, tgt, norm_g, after=0.0):
        self.norm_g = norm_g
        self.slopes = _alibi_slopes()
        self.xp, self.tp = _to_residue_major(x, tgt, after)
        self.u, self.ut = _rms_in(self.xp, norm_g)

    def mixers(self, w4, taps):
        self.w4, self.taps = w4, taps
        self.proj = _in_proj(self.u, w4)
        self.hc, self.hct = _conv_fwd(self.proj, taps)
        fwd = [_attn_fwd(self.proj, self.slopes, d) for d in PATTERNS]
        self.o, self.lse, self.ha, self.hat = _attn_combine([f[0] for f in fwd], [f[1] for f in fwd], self.proj)

    def merge_and_loss(self, woc, woa, wo, b_merge, final_g):
        self.woc, self.woa, self.wo, self.b_merge = woc, woa, wo, b_merge
        (self.yc, self.ya, _, self.mgt, self.dh2, self.dh2b, self.d_final_g, self.loss8) = _merge_loss(
            self.hc, self.ha, woc, woa, wo, self.proj, b_merge, self.xp, final_g, self.tp)

    def out_weight_grads(self):
        (dyc, dya, self.dhc, self.dout, self.dsum, self.db3, self.d_bias) = _merge_bwd(
            self.dh2b, self.wo, self.woc, self.woa, self.yc, self.ya, self.proj, self.b_merge, self.o)
        d_wo = _mm_lhs_resident(self.mgt, self.dh2b, 256, "dw_o")
        d_woc = _mm_lhs_resident(self.hct, dyc, 256, "dw_out_conv")
        d_woa = _mm_lhs_resident(self.hat, dya, 256, "dw_out_attn")
        return d_woc, d_woa, d_wo

    def conv_grads(self, after=0.0):
        self.da4, self.d_taps = _conv_bwd(self.proj, self.taps + after, self.dhc)

    def in_weight_grad(self, after=0.0):
        slopes = self.slopes + after
        self.dc3 = _sum3(*[_attn_bwd(self.proj, self.dout, self.lse, self.dsum, slopes, d) for d in PATTERNS])
        return _dw_in(self.ut, self.da4, self.dc3, self.db3)

    def input_grad(self, half, after=0.0):
        rows = self.xp.shape[0] // 2
        return _input_grad(self.da4, self.dc3, self.db3, self.w4, self.xp, self.norm_g + after, self.dh2,
                           half * rows, rows)


def _local_grads(x, tgt, norm_g, w4, b_merge, conv_w, woc, woa, wo, final_g):
    st = _Step(x, tgt, norm_g)
    st.mixers(w4, conv_w)
    st.merge_and_loss(woc, woa, wo, b_merge, final_g)
    d_woc, d_woa, d_wo = st.out_weight_grads()
    st.conv_grads()
    d_w4 = st.in_weight_grad()
    gx_lo, dg_lo = st.input_grad(0)
    gx_hi, dg_hi = st.input_grad(1)
    return (st.loss8, _to_natural(gx_lo, gx_hi), dg_lo + dg_hi, d_w4, st.d_bias, st.d_taps, d_woc, d_woa, d_wo,
            st.d_final_g)


MESH = pl.DeviceIdType.MESH
_CHIP_FLIPS = ((1, 0), (0, 1), (1, 1))
_ANY = pl.BlockSpec(memory_space=pl.ANY)


def _place():
    return lax.axis_index("x"), lax.axis_index("y"), lax.axis_index("c")


def _flip(v, f):
    return 1 - v if f else v


def _remote(src, dst, send_sems, recv_sems, k, device):
    return pltpu.make_async_remote_copy(src_ref=src, dst_ref=dst, send_sem=send_sems.at[k], recv_sem=recv_sems.at[k],
                                        device_id=device, device_id_type=MESH)


def _place_shard(w, chip):
    rows, cols = w.shape
    tm = 128

    def body(chip_ref, w_ref, o_ref):
        o_ref[0] = w_ref[...].astype(o_ref.dtype)

    return pl.pallas_call(
        body,
        grid_spec=pltpu.PrefetchScalarGridSpec(
            num_scalar_prefetch=1, grid=(rows // tm,),
            in_specs=[pl.BlockSpec((tm, cols), lambda i, chip_ref: (i, 0))],
            out_specs=pl.BlockSpec((1, tm, cols), lambda i, chip_ref: (chip_ref[0], i, 0))),
        out_shape=jax.ShapeDtypeStruct((4, rows, cols), MXU_DTYPE),
        name="place_shard",
    )(chip, w)


def _gather_copies(arrs, _, send_sems, recv_sems):
    x, y, c = _place()
    out = []
    for a, arr in enumerate(arrs):
        h = arr.shape[1] // 2
        mine = arr.at[2 * x + y, pl.ds(pl.multiple_of(c * h, 8), h)]
        for t, (fx, fy) in enumerate(_CHIP_FLIPS):
            out.append(_remote(mine, mine, send_sems, recv_sems, 3 * a + t, (_flip(x, fx), _flip(y, fy), c)))
    return out


def _forward_to_sibling(arrs):
    n = len(arrs)

    def body(*refs):
        outs = refs[n:2 * n]
        send_sems, recv_sems = refs[2 * n:]
        x, y, c = _place()
        sibling = (x, y, 1 - c)
        started = []
        for a in range(n):
            h = outs[a].shape[1] // 2
            rows = pl.ds(pl.multiple_of(c * h, 8), h)
            for t, (fx, fy) in enumerate(_CHIP_FLIPS):
                landed = outs[a].at[2 * _flip(x, fx) + _flip(y, fy), rows]
                cp = _remote(landed, landed, send_sems, recv_sems, 3 * a + t, sibling)
                cp.start()
                started.append(cp)
        for a in range(n):
            h = outs[a].shape[1] // 2
            rows = pl.ds(pl.multiple_of((1 - c) * h, 8), h)
            for t, (fx, fy) in enumerate(_CHIP_FLIPS):
                handed = outs[a].at[2 * _flip(x, fx) + _flip(y, fy), rows]
                _remote(handed, handed, send_sems, recv_sems, 3 * a + t, sibling).wait_recv()
        for cp in started:
            cp.wait_send()

    return pl.pallas_call(
        body, in_specs=[_ANY] * n, out_specs=[_ANY] * n,
        out_shape=[jax.ShapeDtypeStruct(s.shape, s.dtype) for s in arrs],
        input_output_aliases={a: a for a in range(n)},
        scratch_shapes=[pltpu.SemaphoreType.DMA((3 * n,)), pltpu.SemaphoreType.DMA((3 * n,))],
        name="gathered_to_sibling",
    )(*arrs)


_HBM = pl.BlockSpec(memory_space=pltpu.HBM)
_SEM = pl.BlockSpec(memory_space=pltpu.SEMAPHORE)
_EFFECT = pltpu.SideEffectType.DATAFLOW_SIDE_EFFECTING


class _SplitExchange:
    def __init__(self, name, srcs, land_shapes, n_copies, copies, riders=()):
        self.name, self.n, self.nl, self.copies = name, len(srcs), len(land_shapes), copies
        n, nb = self.n, len(srcs) + len(land_shapes)
        lands = [lax.empty(s.shape, s.dtype) for s in land_shapes]
        bufs = [pltpu.with_memory_space_constraint(a, pltpu.HBM) for a in (*srcs, *lands, *riders)]
        na = len(bufs)

        def body(*refs):
            send_sems, recv_sems = refs[na], refs[na + 1]
            for cp in copies(refs[:n], refs[n:nb], send_sems, recv_sems):
                cp.start()
            refs[-1][...] = jnp.zeros(refs[-1].shape, F32)

        outs = pl.pallas_call(
            body, name=name + "_start",
            in_specs=[_HBM] * na,
            out_specs=[_SEM, _SEM] + [_HBM] * na + [pl.BlockSpec(memory_space=pltpu.VMEM)],
            out_shape=[pltpu.SemaphoreType.DMA((n_copies,)), pltpu.SemaphoreType.DMA((n_copies,))]
            + [pltpu.HBM(b.shape, b.dtype) for b in bufs] + [jax.ShapeDtypeStruct((8, LANES), F32)],
            input_output_aliases={i: 2 + i for i in range(na)},
            compiler_params=pltpu.CompilerParams(has_side_effects=_EFFECT),
        )(*bufs)
        self.sems, self.bufs, self.riders, self.token = outs[:2], outs[2:2 + nb], outs[2 + nb:2 + na], outs[-1]

    def after(self):
        return self.token[0, 0]

    def wait(self, done, riders=()):
        n, nb, copies = self.n, self.n + self.nl, self.copies
        bufs = [*self.bufs, *[pltpu.with_memory_space_constraint(a, pltpu.HBM) for a in riders]]
        na = len(bufs)

        def body(*refs):
            send_sems, recv_sems = refs[na], refs[na + 1]
            for cp in copies(refs[:n], refs[n:nb], send_sems, recv_sems):
                cp.wait_send()
                cp.wait_recv()

        outs = pl.pallas_call(
            body, name=self.name + "_wait",
            in_specs=[_HBM] * na + [_SEM, _SEM, _ANY],
            out_specs=[_HBM] * na,
            out_shape=[pltpu.HBM(b.shape, b.dtype) for b in bufs],
            input_output_aliases={i: i for i in range(na)},
            compiler_params=pltpu.CompilerParams(has_side_effects=_EFFECT),
        )(*bufs, *self.sems, done)
        return outs[:n], outs[n:nb], outs[nb:]


def _sibling_copies(srcs, lands, send_sems, recv_sems):
    x, y, c = _place()
    out = []
    for a, (src, land) in enumerate(zip(srcs, lands)):
        h = src.shape[1] // 2
        theirs = pl.ds(pl.multiple_of((1 - c) * h, 8), h)
        out.append(_remote(src.at[:, theirs], land, send_sems, recv_sems, a, (x, y, 1 - c)))
    return out


def _grads_to_sibling(name, grads):
    shapes = [jax.ShapeDtypeStruct((4, g.shape[1] // 2, g.shape[2]), g.dtype) for g in grads]
    return _SplitExchange(name, grads, shapes, len(grads), _sibling_copies)


def _chip_copies(srcs, lands, send_sems, recv_sems):
    x, y, c = _place()
    out = []
    for a, (src, land) in enumerate(zip(srcs, lands)):
        for t, (fx, fy) in enumerate(_CHIP_FLIPS):
            tx, ty = _flip(x, fx), _flip(y, fy)
            out.append(_remote(src.at[2 * tx + ty], land.at[t], send_sems, recv_sems, 3 * a + t, (tx, ty, c)))
    return out


def _grads_to_chips(name, parts):
    shapes = [jax.ShapeDtypeStruct((3, *p.shape[1:]), p.dtype) for p in parts]
    return _SplitExchange(name, parts, shapes, 3 * len(parts), _chip_copies)


def _add_halves(g, r, half):
    _, rows, cols = g.shape
    h = rows // 2
    tm = min(h, 128)
    nt = h // tm

    def body(half_ref, g_ref, r_ref, f_ref, b_ref):
        s = g_ref[...] + r_ref[...]
        f_ref[...] = s
        b_ref[...] = s.astype(b_ref.dtype)

    spec = pl.BlockSpec((1, tm, cols), lambda j, i, half_ref: (j, i, 0))
    return pl.pallas_call(
        body,
        grid_spec=pltpu.PrefetchScalarGridSpec(
            num_scalar_prefetch=1, grid=(4, nt),
            in_specs=[pl.BlockSpec((1, tm, cols), lambda j, i, half_ref: (j, half_ref[0] * nt + i, 0)), spec],
            out_specs=[spec, spec]),
        out_shape=[jax.ShapeDtypeStruct((4, h, cols), F32), jax.ShapeDtypeStruct((4, h, cols), BF16)],
        name="add_sibling_grads",
    )(half, g, r)


def _add_chips(own, recv, where):
    _, h, cols = own.shape
    tm = min(h, 128)
    nt = h // tm

    def body(where_ref, o_ref, r_ref, out_ref):
        out_ref[...] = ((o_ref[0] + r_ref[0].astype(F32)) + r_ref[1].astype(F32)) + r_ref[2].astype(F32)

    return pl.pallas_call(
        body,
        grid_spec=pltpu.PrefetchScalarGridSpec(
            num_scalar_prefetch=1, grid=(nt,),
            in_specs=[pl.BlockSpec((1, tm, cols), lambda i, where_ref: (where_ref[0], i, 0)),
                      pl.BlockSpec((3, tm, cols), lambda i, where_ref: (0, i, 0))],
            out_specs=pl.BlockSpec((tm, cols), lambda i, where_ref: (where_ref[1] * nt + i, 0))),
        out_shape=jax.ShapeDtypeStruct((2 * h, cols), F32),
        name="add_chip_grads",
    )(where, own, recv)


def _share_halves(shards):
    n = len(shards)

    def body(*refs):
        outs = refs[n:2 * n]
        send_sems, recv_sems = refs[2 * n:]
        x, y, c = _place()
        copies = []
        for a in range(n):
            h = outs[a].shape[0] // 2
            mine = outs[a].at[pl.ds(pl.multiple_of(c * h, 8), h)]
            copies.append(_remote(mine, mine, send_sems, recv_sems, a, (x, y, 1 - c)))
        for cp in copies:
            cp.start()
        for a, cp in enumerate(copies):
            cp.wait_send()
            h = outs[a].shape[0] // 2
            theirs = outs[a].at[pl.ds(pl.multiple_of((1 - c) * h, 8), h)]
            _remote(theirs, theirs, send_sems, recv_sems, a, (x, y, 1 - c)).wait_recv()

    return pl.pallas_call(
        body, in_specs=[_ANY] * n, out_specs=[_ANY] * n,
        out_shape=[jax.ShapeDtypeStruct(p.shape, p.dtype) for p in shards],
        input_output_aliases={a: a for a in range(n)},
        scratch_shapes=[pltpu.SemaphoreType.DMA((n,)), pltpu.SemaphoreType.DMA((n,))],
        name="share_reduced_halves",
    )(*shards)


def _exchange_small(rows, reduce):
    cols = rows[0].shape[1]
    n = len(rows)
    assert sum(r.shape[0] for r in rows) <= 8

    def body(*refs):
        ins, out_ref = refs[:n], refs[n]
        vec_ref, gath_ref, send_sems, recv_sems = refs[n + 1:]
        x, y, c = _place()
        me = 4 * x + 2 * y + c
        vec_ref[...] = jnp.zeros(vec_ref.shape, F32)
        at = 0
        for r in ins:
            vec_ref[at:at + r.shape[0], :] = r[...]
            at += r.shape[0]
        copies = []
        for k in range(1, 8):
            peer = (_flip(x, (k >> 2) & 1), _flip(y, (k >> 1) & 1), _flip(c, k & 1))
            copies.append(_remote(vec_ref, gath_ref.at[me], send_sems, recv_sems, k - 1, peer))
        for cp in copies:
            cp.start()
        gath_ref[me] = vec_ref[...]
        for cp in copies:
            cp.wait()
        if reduce:
            tot = gath_ref[0]
            for dev in range(1, 8):
                tot = tot + gath_ref[dev]
            out_ref[...] = tot
            out_ref[7:8, :] = jnp.zeros((1, cols), F32) + jnp.sum(tot[7:8, :])
        else:
            out_ref[...] = gath_ref[...]

    vm = pl.BlockSpec(memory_space=pltpu.VMEM)
    return pl.pallas_call(
        body, in_specs=[vm] * n, out_specs=vm,
        out_shape=jax.ShapeDtypeStruct((8, cols) if reduce else (8, 8, cols), F32),
        scratch_shapes=[pltpu.VMEM((8, cols), F32), pltpu.VMEM((8, 8, cols), F32),
                        pltpu.SemaphoreType.DMA((7,)), pltpu.SemaphoreType.DMA((7,))],
        name="reduce_small" if reduce else "gather_small",
    )(*rows)


def _adamw(w, g, m, v, name):
    rows, cols = w.shape
    tm = 128 if rows % 128 == 0 else rows

    def body(w_ref, g_ref, m_ref, v_ref, d_ref, m2_ref, v2_ref, gout_ref):
        gr = g_ref[...]
        m2 = ADAM_B1 * m_ref[...] + (1.0 - ADAM_B1) * gr
        v2 = ADAM_B2 * v_ref[...] + (1.0 - ADAM_B2) * (gr * gr)
        m_hat = m2 / (1.0 - ADAM_B1 ** ADAM_STEP)
        v_hat = v2 / (1.0 - ADAM_B2 ** ADAM_STEP)
        d_ref[...] = -ADAM_LR * (m_hat / (jnp.sqrt(v_hat) + ADAM_EPS) + ADAM_WD * w_ref[...])
        m2_ref[...] = m2
        v2_ref[...] = v2
        gout_ref[...] = gr

    spec = pl.BlockSpec((tm, cols), lambda i: (i, 0))
    sds = jax.ShapeDtypeStruct((rows, cols), F32)
    return pl.pallas_call(body, grid=(rows // tm,), in_specs=[spec] * 4, out_specs=[spec] * 4,
                          out_shape=[sds] * 4, name=name)(w, g, m, v)


def kernel(x, norm_g, w_in, b_merge, conv_w, w_out_conv, w_out_attn, w_o, final_g, loss_target, m_norm_g, m_w_in, m_b_merge, m_conv_w, m_w_out_conv, m_w_out_attn, m_w_o, m_final_g, v_norm_g, v_w_in, v_b_merge, v_conv_w, v_w_out_conv, v_w_out_attn, v_w_o, v_final_g):
    mx, my, mc = _place()
    chip = (2 * mx + my).astype(jnp.int32)
    seq = x.shape[1]

    chip1 = chip.reshape(1)
    slots = [_place_shard(w[0], chip1) for w in (w_in, w_out_conv, w_out_attn, w_o)]
    taps8 = _exchange_small([conv_w[0]], reduce=False)
    taps = jnp.concatenate([taps8[2 * j, :3, :] for j in range(4)], axis=1)
    gather_in = _SplitExchange("gather_w_in", slots[:1], [], 3, _gather_copies)
    st = _Step(x[0], loss_target[0], norm_g, after=gather_in.after())
    (w4,), _, out_slots = gather_in.wait(st.ut, riders=slots[1:])
    gather_out = _SplitExchange("gather_w_out", out_slots, [], 9, _gather_copies, riders=[w4])
    (w4,) = _forward_to_sibling(gather_out.riders)
    st.mixers(w4, taps)
    out_ws, _, _ = gather_out.wait(st.o)
    woc, woa, wo = [w.reshape(D_MODEL, D_MODEL) for w in _forward_to_sibling(out_ws)]
    st.merge_and_loss(woc, woa, wo, b_merge, final_g.reshape(1, D_MODEL))

    half = mc.astype(jnp.int32).reshape(1)
    where = jnp.stack([chip, mc.astype(jnp.int32)])
    out_grads = [g.reshape(4, -1, D_MODEL) for g in st.out_weight_grads()]
    to_sibling = _grads_to_sibling("out_grads_to_sibling", out_grads)
    st.conv_grads(after=to_sibling.after())
    out_grads, from_sibling, _ = to_sibling.wait(st.da4)
    out_partial = [_add_halves(g, r, half) for g, r in zip(out_grads, from_sibling)]
    to_chips = _grads_to_chips("out_grads_to_chips", [p[1] for p in out_partial])
    d_w4 = st.in_weight_grad(after=to_chips.after())
    from_chips = to_chips.wait(st.dc3)[1]

    to_sibling = _grads_to_sibling("in_grad_to_sibling", [d_w4])
    gx_lo, dg_lo = st.input_grad(0, after=to_sibling.after())
    (d_w4,), (from_sibling,), _ = to_sibling.wait(gx_lo)
    in_partial = _add_halves(d_w4, from_sibling, half)
    to_chips = _grads_to_chips("in_grad_to_chips", [in_partial[1]])
    gx_hi, dg_hi = st.input_grad(1, after=to_chips.after())
    grad_x = _to_natural(gx_lo, gx_hi)

    where_late = where + to_chips.after().astype(jnp.int32)
    out_reduced = [_add_chips(p[0], r, where_late) for p, r in zip(out_partial, from_chips)]
    g_woc, g_woa, g_wo = _share_halves(out_reduced)
    small = _exchange_small([dg_lo + dg_hi, st.d_bias.reshape(2, D_MODEL), st.d_taps, st.d_final_g,
                             st.loss8.reshape(1, D_MODEL)], reduce=True)
    loss = (0.5 / D_MODEL) * small[7, 0]
    g_taps = lax.dynamic_slice(small[3:6], (0, chip * (D_MODEL // 4)), (3, D_MODEL // 4))
    upd = {
        "norm_g": _adamw(norm_g, small[0:1], m_norm_g, v_norm_g, "adamw_norm_g"),
        "b_merge": _adamw(b_merge, small[1:3].reshape(1, 2 * D_MODEL), m_b_merge, v_b_merge, "adamw_b_merge"),
        "conv_w": _adamw(conv_w[0], g_taps, m_conv_w[0], v_conv_w[0], "adamw_conv_w"),
        "w_out_conv": _adamw(w_out_conv[0], g_woc, m_w_out_conv[0], v_w_out_conv[0], "adamw_w_out_conv"),
        "w_out_attn": _adamw(w_out_attn[0], g_woa, m_w_out_attn[0], v_w_out_attn[0], "adamw_w_out_attn"),
        "w_o": _adamw(w_o[0], g_wo, m_w_o[0], v_w_o[0], "adamw_w_o"),
        "final_g": _adamw(final_g.reshape(1, D_MODEL), small[6:7], m_final_g.reshape(1, D_MODEL),
                          v_final_g.reshape(1, D_MODEL), "adamw_final_g"),
    }
    in_reduced = _add_chips(in_partial[0], to_chips.wait(upd["w_o"][0])[1][0], where)
    (g_w_in,) = _share_halves([in_reduced])
    upd["w_in"] = _adamw(w_in[0], g_w_in, m_w_in[0], v_w_in[0], "adamw_w_in")

    names = ["norm_g", "w_in", "b_merge", "conv_w", "w_out_conv", "w_out_attn", "w_o", "final_g"]
    shapes = [norm_g.shape, w_in.shape, b_merge.shape, conv_w.shape, w_out_conv.shape, w_out_attn.shape,
              w_o.shape, final_g.shape]
    outs = [loss, grad_x.reshape(1, seq, D_MODEL)]
    for k in (3, 0, 1, 2):
        outs += [upd[n][k].reshape(s) for n, s in zip(names, shapes)]
    return tuple(outs)
```

```python
import functools

import numpy as np
import jax
import jax.numpy as jnp
from jax import lax
from jax.experimental import pallas as pl
from jax.experimental.pallas import tpu as pltpu

F32 = jnp.float32
BF16 = jnp.bfloat16
MXU_DTYPE = jnp.bfloat16
ACT_DTYPE = jnp.bfloat16

D_MODEL = 1024
N_HEADS = 16
HEAD_DIM = 64
QB = 128
N_RES = 16
LANES = 128
HP = N_HEADS * HEAD_DIM // LANES
IN_COLS = 10 * D_MODEL
SHARD_COLS = IN_COLS // 4
EPS = 1e-6
NEG = -1e30

ADAM_LR, ADAM_B1, ADAM_B2, ADAM_EPS, ADAM_WD, ADAM_STEP = 0.001, 0.9, 0.999, 1e-08, 0.01, 10

PATTERNS = {1: (16, 8), 4: (4, 32), 16: (1, 128)}

_NN = (((1,), (0,)), ((), ()))
_NT = (((1,), (1,)), ((), ()))


def _dot(a, b):
    return lax.dot_general(a.astype(MXU_DTYPE), b.astype(MXU_DTYPE), _NN, preferred_element_type=F32)


def _dot_nt(a, b):
    return lax.dot_general(a.astype(MXU_DTYPE), b.astype(MXU_DTYPE), _NT, preferred_element_type=F32)


def _split3(x):
    hi = x.astype(BF16)
    r1 = x - hi.astype(F32)
    mid = r1.astype(BF16)
    lo = (r1 - mid.astype(F32)).astype(BF16)
    return hi, mid, lo


def _select_rows(sel, x):
    return sum(lax.dot_general(sel, t, _NN, preferred_element_type=F32) for t in _split3(x))


def _select_cols(x, sel, terms=3):
    return sum(lax.dot_general(t, sel, _NN, preferred_element_type=F32) for t in _split3(x)[:terms])


def _sigmoid(z):
    return 1.0 / (1.0 + jnp.exp(-z))


def _perm_matrix():
    idx = np.arange(256)
    p = np.zeros((256, 256), np.float32)
    p[(idx % 16) * 16 + idx // 16, idx] = 1.0
    return jnp.asarray(p, BF16)


def _head_expand_matrix():
    e = np.zeros((LANES, D_MODEL), np.float32)
    for h in range(N_HEADS):
        e[8 * h, HEAD_DIM * h:HEAD_DIM * (h + 1)] = 1.0
    return jnp.asarray(e, BF16)


def _head_sum_matrix():
    e = np.zeros((D_MODEL, LANES), np.float32)
    for h in range(N_HEADS):
        e[HEAD_DIM * h:HEAD_DIM * (h + 1), 8 * h:8 * (h + 1)] = 1.0
    return jnp.asarray(e, BF16)


def _attn_tables(d):
    g_n, rq = PATTERNS[d]
    q_n = g_n * rq
    gq, iq = np.arange(q_n) // rq, np.arange(q_n) % rq

    def tab(kn, base):
        k_n = g_n * kn
        gk, jk = np.arange(k_n) // kn, np.arange(k_n) % kn
        delta = g_n * (base + iq[:, None] - jk[None, :]) + gq[:, None] - gk[None, :]
        valid = (delta >= 0) & (delta <= QB)
        dist = np.where(valid, d * delta, 0).astype(np.float32)
        madd = np.where(valid, 0.0, NEG).astype(np.float32)
        return dist, madd

    d0, m0 = tab(rq if g_n == 1 else 2 * rq, 0)
    d1, m1 = tab(2 * rq, rq)
    return d0, m0, d1, m1


def _alibi_slopes():
    return jnp.exp2(-8.0 * jnp.arange(1, N_HEADS + 1, dtype=F32) / N_HEADS)


def _to_residue_major(x, tgt, after=0.0):
    s_n, c_n = x.shape
    lr = s_n // N_RES
    pm = (_perm_matrix().astype(F32) + after).astype(BF16)

    def body(p_ref, x_ref, t_ref, xo_ref, to_ref):
        pm = p_ref[...]
        xo_ref[...] = _select_rows(pm, x_ref[...]).reshape(16, 16, c_n)
        to_ref[...] = _select_rows(pm, t_ref[...]).reshape(16, 16, c_n)

    nat = pl.BlockSpec((256, c_n), lambda i: (i, 0))
    res = pl.BlockSpec((16, 16, c_n), lambda i: (0, i, 0))
    xo, to = pl.pallas_call(
        body, grid=(s_n // 256,),
        in_specs=[pl.BlockSpec((256, 256), lambda i: (0, 0)), nat, nat],
        out_specs=[res, res],
        out_shape=[jax.ShapeDtypeStruct((16, lr, c_n), F32)] * 2,
        name="perm_in",
    )(pm, x, tgt)
    return xo.reshape(s_n, c_n), to.reshape(s_n, c_n)


def _to_natural(gx_lo, gx_hi):
    half_rows, c_n = gx_lo.shape
    lr = half_rows // (N_RES // 2)

    def body(p_ref, lo_ref, hi_ref, o_ref):
        g = jnp.concatenate([lo_ref[...], hi_ref[...]], axis=0)
        o_ref[...] = _select_rows(p_ref[...], g.reshape(256, c_n))

    half = pl.BlockSpec((8, 16, c_n), lambda i: (0, i, 0))
    return pl.pallas_call(
        body, grid=(lr // 16,),
        in_specs=[pl.BlockSpec((256, 256), lambda i: (0, 0)), half, half],
        out_specs=pl.BlockSpec((256, c_n), lambda i: (i, 0)),
        out_shape=jax.ShapeDtypeStruct((2 * half_rows, c_n), F32),
        name="perm_out",
    )(_perm_matrix(), gx_lo.reshape(8, lr, c_n), gx_hi.reshape(8, lr, c_n))


def _rms_in(xp, norm_g):
    s_n, c_n = xp.shape
    tm = 512

    def body(x_ref, g_ref, u_ref, ut_ref):
        x = x_ref[...]
        r = lax.rsqrt(jnp.mean(x * x, axis=-1, keepdims=True) + EPS)
        u = x * r * g_ref[...]
        u_ref[...] = u.astype(u_ref.dtype)
        ut_ref[...] = u.T.astype(ut_ref.dtype)

    return pl.pallas_call(
        body, grid=(s_n // tm,),
        in_specs=[pl.BlockSpec((tm, c_n), lambda i: (i, 0)), pl.BlockSpec((1, c_n), lambda i: (0, 0))],
        out_specs=[pl.BlockSpec((tm, c_n), lambda i: (i, 0)), pl.BlockSpec((c_n, tm), lambda i: (0, i))],
        out_shape=[jax.ShapeDtypeStruct((s_n, c_n), ACT_DTYPE), jax.ShapeDtypeStruct((c_n, s_n), ACT_DTYPE)],
        name="rms_in",
    )(xp, norm_g)


def _in_proj(u, chip, w_own=None, w4=None, partial=None):
    s_n = u.shape[0]
    tn, cm = 512, 512
    per = SHARD_COLS // tn
    own = partial is None

    def body(chip_ref, a_ref, b_ref, *rest):
        o_ref = rest[-1]
        b = b_ref[...]
        for c in range(s_n // cm):
            o_ref[c * cm:(c + 1) * cm, :] = _dot(a_ref[c * cm:(c + 1) * cm, :], b).astype(o_ref.dtype)

    def shard(n, chip_ref):
        return chip_ref[0] if own else (chip_ref[0] + 1 + n // per) % 4

    w_spec = (pl.BlockSpec((D_MODEL, tn), lambda n, c: (0, n)) if own else
              pl.BlockSpec((None, D_MODEL, tn), lambda n, c: (shard(n, c), 0, n % per)))
    return pl.pallas_call(
        body,
        grid_spec=pltpu.PrefetchScalarGridSpec(
            num_scalar_prefetch=1, grid=(per if own else 3 * per,),
            in_specs=[pl.BlockSpec((s_n, D_MODEL), lambda n, c: (0, 0)), w_spec] + ([] if own else [_ANY]),
            out_specs=pl.BlockSpec((s_n, tn), lambda n, c: (0, shard(n, c) * per + n % per))),
        out_shape=jax.ShapeDtypeStruct((s_n, IN_COLS), ACT_DTYPE),
        input_output_aliases={} if own else {3: 0},
        name="in_proj_own" if own else "in_proj",
    )(*([chip, u, w_own] if own else [chip, u, w4, partial]))


def _conv_terms(xc_ref, cg_ref, r, row, lr):
    def a_of(q):
        return cg_ref[q].astype(F32) * xc_ref[q].astype(F32)

    def shift_down(v):
        return jnp.where(row >= 1, pltpu.roll(v, 1, 0), 0.0)

    a = a_of(r)
    am1 = a_of(r - 1) if r >= 1 else shift_down(a_of(N_RES - 1))
    am2 = a_of(r - 2) if r >= 2 else shift_down(a_of(N_RES - 2 + r))
    return a, am1, am2


def _conv_fwd(proj, conv_w):
    s_n = proj.shape[0]
    lr = s_n // N_RES
    pv = proj.reshape(N_RES, lr, IN_COLS)

    def body(xc_ref, bg_ref, cg_ref, zc_ref, w_ref, hc_ref, hct_ref):
        w = w_ref[...]
        row = lax.broadcasted_iota(jnp.int32, (lr, LANES), 0)
        for r in range(N_RES):
            a, am1, am2 = _conv_terms(xc_ref, cg_ref, r, row, lr)
            c = w[0:1] * am2 + w[1:2] * am1 + w[2:3] * a
            z = zc_ref[r].astype(F32)
            hc = z * _sigmoid(z) * bg_ref[r].astype(F32) * c
            hc_ref[r] = hc.astype(hc_ref.dtype)
            hct_ref[:, r * lr:(r + 1) * lr] = hc.T.astype(hct_ref.dtype)

    def col(part):
        return pl.BlockSpec((N_RES, lr, LANES), lambda j: (0, 0, part * 8 + j))

    hc, hct = pl.pallas_call(
        body, grid=(D_MODEL // LANES,),
        in_specs=[col(0), col(1), col(2), col(3), pl.BlockSpec((3, LANES), lambda j: (0, j))],
        out_specs=[pl.BlockSpec((N_RES, lr, LANES), lambda j: (0, 0, j)),
                   pl.BlockSpec((LANES, s_n), lambda j: (j, 0))],
        out_shape=[jax.ShapeDtypeStruct((N_RES, lr, D_MODEL), ACT_DTYPE),
                   jax.ShapeDtypeStruct((D_MODEL, s_n), ACT_DTYPE)],
        name="conv_fwd",
    )(pv, pv, pv, pv, conv_w)
    return hc.reshape(s_n, D_MODEL), hct


RES_PER_STEP = 8
FWD_BATCH = {1: 8, 4: 8, 16: RES_PER_STEP}
BWD_BATCH = {1: 8, 4: 8, 16: RES_PER_STEP}

_BNT = (((2,), (2,)), ((0,), (0,)))
_BNN = (((2,), (1,)), ((0,), (0,)))


def _bdot(a, b, dims):
    return lax.dot_general(a.astype(MXU_DTYPE), b.astype(MXU_DTYPE), dims, preferred_element_type=F32)


def _pattern_view_shape(s_n, c_n, g_n, lead=()):
    lr = s_n // N_RES
    return (*lead, 4, 4, lr, c_n) if g_n == 4 else (*lead, N_RES, lr, c_n)


def _pattern_view(a, g_n, lead=()):
    return a.reshape(_pattern_view_shape(a.shape[-2], a.shape[-1], g_n, lead))


def _pattern_grid(g_n):
    return (N_RES // RES_PER_STEP if g_n == 1 else N_RES // g_n, HP)


def _pattern_spec(g_n, lr, col_of_hp, lead=()):
    z = (0,) * len(lead)
    if g_n == 16:
        return pl.BlockSpec((*lead, 16, lr, LANES), lambda r, hp: (*z, 0, 0, col_of_hp(hp)))
    if g_n == 4:
        return pl.BlockSpec((*lead, 4, None, lr, LANES), lambda r, hp: (*z, 0, r, 0, col_of_hp(hp)))
    return pl.BlockSpec((*lead, RES_PER_STEP, lr, LANES), lambda r, hp: (*z, r, 0, col_of_hp(hp)))


def _aligned(start, m):
    return start if isinstance(start, int) else pl.multiple_of(start, m)


class _Units:
    def __init__(self, g_n, rq):
        self.g_n, self.rq = g_n, rq
        self.per_res, self.paired = g_n == 1, rq == 8

    def plan(self, lr, size):
        if self.per_res:
            return [0], lr // self.rq - 1, lambda j: [pl.multiple_of(j * self.rq, self.rq)]
        step = 16 if self.paired else self.rq
        per = min(size // 2 if self.paired else size, lr // step)
        assert (lr // step) % per == 0
        return ([i * step for i in range(per)], lr // step // per - 1,
                lambda j: [pl.multiple_of((j * per + i) * step, step) for i in range(per)])

    def count(self, qs):
        return RES_PER_STEP if self.per_res else len(qs) * (2 if self.paired else 1)

    def _split(self, tiles, lo, rows):
        return tiles[:, lo:lo + rows].reshape(self.g_n * rows, LANES)

    def load_q(self, ref, qs):
        rq = self.rq
        if self.per_res:
            return ref[:, pl.ds(qs[0], rq), :]
        if self.paired:
            tiles = [ref[:, pl.ds(q, 16), :].astype(F32) for q in qs]
            return jnp.stack([self._split(t, lo, 8) for t in tiles for lo in (0, 8)])
        return jnp.stack([ref[:, pl.ds(q, rq), :].reshape(self.g_n * rq, LANES) for q in qs])

    def _key_rows(self, q, at_start):
        return (0, 2 * self.rq) if at_start else (_aligned(q - self.rq, self.rq), 2 * self.rq)

    def load_k(self, ref, qs, first):
        rq = self.rq
        if self.per_res:
            return ref[:, pl.ds(0, rq), :] if first else ref[:, pl.ds(_aligned(qs[0] - rq, rq), 2 * rq), :]
        if self.paired:
            out = []
            for i, q in enumerate(qs):
                if first and i == 0:
                    t = ref[:, 0:16, :].astype(F32)
                    out += [self._split(t, 0, 16)] * 2
                else:
                    t = ref[:, pl.ds(_aligned(q - 16, 16), 32), :].astype(F32)
                    out += [self._split(t, 8, 16), self._split(t, 16, 16)]
            return jnp.stack(out)
        rows = [self._key_rows(q, first and i == 0) for i, q in enumerate(qs)]
        return jnp.stack([ref[:, pl.ds(k0, n), :].reshape(self.g_n * n, LANES) for k0, n in rows])

    def store_q(self, ref, qs, val, add=False, lead=()):
        if self.per_res:
            pieces = [(qs[0], self.rq, val)]
        elif self.paired:
            pieces = [(q, 16, jnp.concatenate([val[2 * i].reshape(self.g_n, 8, LANES),
                                               val[2 * i + 1].reshape(self.g_n, 8, LANES)], axis=1))
                      for i, q in enumerate(qs)]
        else:
            pieces = [(q, self.rq, val[i].reshape(self.g_n, self.rq, LANES)) for i, q in enumerate(qs)]
        for start, rows, v in pieces:
            idx = (*lead, slice(None), pl.ds(start, rows), slice(None))
            ref[idx] = (ref[idx] + v if add else v).astype(ref.dtype)

    def add_k(self, ref, qs, val, first):
        rq = self.rq
        if self.per_res:
            k0, n = (0, rq) if first else (_aligned(qs[0] - rq, rq), 2 * rq)
            ref[:, pl.ds(k0, n), :] += val
            return
        if self.paired:
            starts = [s for i, q in enumerate(qs)
                      for s in ((0, 0) if first and i == 0 else (_aligned(q - 8, 8), q))]
            rows = [(s, 16) for s in starts]
        else:
            rows = [self._key_rows(q, first and i == 0) for i, q in enumerate(qs)]
        for b, (k0, n) in enumerate(rows):
            ref[:, pl.ds(k0, n), :] += val[b].reshape(self.g_n, n, LANES)


def _batch_bias(un, qs, at_start, first_ref, general_ref):
    if not at_start:
        return general_ref[...][None]
    if un.per_res:
        return first_ref[...][None]
    return jnp.concatenate([first_ref[...][None]] + [general_ref[...][None]] * (un.count(qs) - 1), axis=0)


def _stack_heads(x, low):
    zero = jnp.zeros_like(x)
    return jnp.concatenate([jnp.where(low, x, zero), jnp.where(low, zero, x)], axis=1)


def _attn_fwd(proj, slopes, d):
    g_n, rq = PATTERNS[d]
    un = _Units(g_n, rq)
    s_n = proj.shape[0]
    lr = s_n // N_RES
    nb = lr // rq
    q_n = g_n * rq
    d0, m0, d1, m1 = _attn_tables(d)
    first, n_more, later = un.plan(lr, FWD_BATCH[d])

    def body(sl_ref, q_ref, k_ref, v_ref, d0_ref, m0_ref, d1_ref, m1_ref, o_ref, lse_ref, b0_ref, b1_ref):
        hp = pl.program_id(1)

        @pl.when(hp == 0)
        def _():
            lse_ref[...] = jnp.zeros(lse_ref.shape, F32)

        for h in (0, 1):
            slope = sl_ref[2 * hp + h]
            b0_ref[h * q_n:(h + 1) * q_n, :] = m0_ref[...] - slope * d0_ref[...]
            b1_ref[h * q_n:(h + 1) * q_n, :] = m1_ref[...] - slope * d1_ref[...]

        lane = lax.broadcasted_iota(jnp.int32, (1, q_n, LANES), 2)
        low = lane < HEAD_DIM
        grp = lane // 8

        def batch(qs, at_start):
            qq = _stack_heads(un.load_q(q_ref, qs) * 0.125, low)
            s = _bdot(qq, un.load_k(k_ref, qs, at_start), _BNT) + _batch_bias(un, qs, at_start, b0_ref, b1_ref)
            m = jnp.max(s, axis=2, keepdims=True)
            p = jnp.exp(s - m)
            l = jnp.sum(p, axis=2, keepdims=True)
            o = _bdot(p, un.load_k(v_ref, qs, at_start), _BNN) * (1.0 / l)
            lse = m + jnp.log(l)
            un.store_q(o_ref, qs, jnp.where(low, o[:, :q_n], o[:, q_n:]))
            upd = jnp.where(grp == 2 * hp, lse[:, :q_n], 0.0) + jnp.where(grp == 2 * hp + 1, lse[:, q_n:], 0.0)
            un.store_q(lse_ref, qs, upd, add=True)

        batch(first, True)

        def more(j, carry):
            batch(later(j), False)
            return carry

        lax.fori_loop(1, 1 + n_more, more, 0)

    pv = _pattern_view(proj, g_n)
    full = lambda a: pl.BlockSpec(a.shape, lambda r, hp: (0, 0))
    o, lse = pl.pallas_call(
        body, grid=_pattern_grid(g_n),
        in_specs=[pl.BlockSpec(memory_space=pltpu.SMEM),
                  _pattern_spec(g_n, lr, lambda hp: 32 + hp),
                  _pattern_spec(g_n, lr, lambda hp: 40 + hp),
                  _pattern_spec(g_n, lr, lambda hp: 48 + hp),
                  full(d0), full(m0), full(d1), full(m1)],
        out_specs=[_pattern_spec(g_n, lr, lambda hp: hp), _pattern_spec(g_n, lr, lambda hp: 0)],
        out_shape=[jax.ShapeDtypeStruct(_pattern_view_shape(s_n, D_MODEL, g_n), ACT_DTYPE),
                   jax.ShapeDtypeStruct(_pattern_view_shape(s_n, LANES, g_n), F32)],
        scratch_shapes=[pltpu.VMEM((2 * q_n, d0.shape[1]), F32), pltpu.VMEM((2 * q_n, 2 * q_n), F32)],
        name=f"attn_fwd_d{d}",
    )(slopes, pv, pv, pv, d0, m0, d1, m1)
    return o.reshape(s_n, D_MODEL), lse.reshape(s_n, LANES)


def _attn_combine(outs, lses, proj):
    s_n = proj.shape[0]
    tm = 512

    def body(o1_ref, o2_ref, o3_ref, l1_ref, l2_ref, l3_ref, za_ref, e_ref, o_ref, lse_ref, ha_ref, hat_ref):
        ls = [l1_ref[...], l2_ref[...], l3_ref[...]]
        mx = jnp.maximum(jnp.maximum(ls[0], ls[1]), ls[2])
        den = sum(jnp.exp(l - mx) for l in ls)
        lse = mx + jnp.log(den)
        lse_ref[...] = lse
        o = jnp.zeros((tm, D_MODEL), F32)
        for l, oref in zip(ls, (o1_ref, o2_ref, o3_ref)):
            o = o + _select_cols(jnp.exp(l - lse), e_ref[...], terms=2) * oref[...].astype(F32)
        o_ref[...] = o
        z = za_ref[...].astype(F32)
        ha = z * _sigmoid(z) * o
        ha_ref[...] = ha.astype(ha_ref.dtype)
        hat_ref[...] = ha.T.astype(hat_ref.dtype)

    row = lambda w: pl.BlockSpec((tm, w), lambda i: (i, 0))
    return pl.pallas_call(
        body, grid=(s_n // tm,),
        in_specs=[row(D_MODEL)] * 3 + [row(LANES)] * 3
        + [pl.BlockSpec((tm, D_MODEL), lambda i: (i, 7)), pl.BlockSpec((LANES, D_MODEL), lambda i: (0, 0))],
        out_specs=[row(D_MODEL), row(LANES), row(D_MODEL), pl.BlockSpec((D_MODEL, tm), lambda i: (0, i))],
        out_shape=[jax.ShapeDtypeStruct((s_n, D_MODEL), F32), jax.ShapeDtypeStruct((s_n, LANES), F32),
                   jax.ShapeDtypeStruct((s_n, D_MODEL), ACT_DTYPE), jax.ShapeDtypeStruct((D_MODEL, s_n), ACT_DTYPE)],
        name="attn_combine",
    )(*outs, *lses, proj, _head_expand_matrix())


def _gates(gc_ref, ga_ref, b_ref):
    b = b_ref[...]
    gc = _sigmoid(gc_ref[...].astype(F32) + b[:, :D_MODEL])
    ga = _sigmoid(ga_ref[...].astype(F32) + b[:, D_MODEL:])
    return gc, ga


def _merge_loss(hc, ha, woc, woa, wo, proj, b_merge, xp, final_g, tgt):
    s_n = xp.shape[0]
    tm = 512

    def body(hc_ref, ha_ref, woc_ref, woa_ref, wo_ref, gc_ref, ga_ref, b_ref, x_ref, gf_ref, t_ref,
             yc_ref, ya_ref, mg_ref, mgt_ref, dh_ref, dhb_ref, dgf_ref, loss_ref):
        i = pl.program_id(0)

        @pl.when(i == 0)
        def _():
            dgf_ref[...] = jnp.zeros(dgf_ref.shape, F32)
            loss_ref[...] = jnp.zeros(loss_ref.shape, F32)

        yc = _dot(hc_ref[...], woc_ref[...])
        ya = _dot(ha_ref[...], woa_ref[...])
        gc, ga = _gates(gc_ref, ga_ref, b_ref)
        mg = gc * yc + ga * ya
        yc_ref[...] = yc.astype(yc_ref.dtype)
        ya_ref[...] = ya.astype(ya_ref.dtype)
        mg_ref[...] = mg.astype(mg_ref.dtype)
        mgt_ref[...] = mg.T.astype(mgt_ref.dtype)
        h2 = x_ref[...] + _dot(mg, wo_ref[...])
        r2 = lax.rsqrt(jnp.mean(h2 * h2, axis=-1, keepdims=True) + EPS)
        nrm = h2 * r2
        gf = gf_ref[...]
        err = nrm * gf - t_ref[...]
        e2 = (err * err).reshape(tm // 8, 8, D_MODEL).sum(axis=0)
        loss_ref[...] += sum(e2[:, c * LANES:(c + 1) * LANES] for c in range(D_MODEL // LANES))
        dy = err * (1.0 / D_MODEL)
        dgf_ref[...] += jnp.sum(dy * nrm, axis=0, keepdims=True)
        dn = dy * gf
        dh2 = r2 * (dn - nrm * jnp.mean(dn * nrm, axis=-1, keepdims=True))
        dh_ref[...] = dh2
        dhb_ref[...] = dh2.astype(dhb_ref.dtype)

    row = pl.BlockSpec((tm, D_MODEL), lambda i: (i, 0))
    wsp = pl.BlockSpec((D_MODEL, D_MODEL), lambda i: (0, 0))
    vec = lambda w: pl.BlockSpec((1, w), lambda i: (0, 0))
    act = jax.ShapeDtypeStruct((s_n, D_MODEL), ACT_DTYPE)
    return pl.pallas_call(
        body, grid=(s_n // tm,),
        in_specs=[row, row, wsp, wsp, wsp,
                  pl.BlockSpec((tm, D_MODEL), lambda i: (i, 8)), pl.BlockSpec((tm, D_MODEL), lambda i: (i, 9)),
                  vec(2 * D_MODEL), row, vec(D_MODEL), row],
        out_specs=[row, row, row, pl.BlockSpec((D_MODEL, tm), lambda i: (0, i)), row, row,
                   vec(D_MODEL), pl.BlockSpec((8, LANES), lambda i: (0, 0))],
        out_shape=[act, act, act, jax.ShapeDtypeStruct((D_MODEL, s_n), ACT_DTYPE),
                   jax.ShapeDtypeStruct((s_n, D_MODEL), F32), act,
                   jax.ShapeDtypeStruct((1, D_MODEL), F32), jax.ShapeDtypeStruct((8, LANES), F32)],
        name="merge_loss",
    )(hc, ha, woc, woa, wo, proj, proj, b_merge, xp, final_g, tgt)


def _merge_bwd(dh2b, wo, woc, woa, yc, ya, proj, b_merge, o):
    s_n = dh2b.shape[0]
    tm = 512

    def body(dh_ref, wo_ref, woc_ref, woa_ref, yc_ref, ya_ref, gc_ref, ga_ref, b_ref, o_ref, za_ref, e_ref,
             dyc_ref, dya_ref, dhc_ref, do_ref, dsum_ref, db3_ref, dbias_ref):
        i = pl.program_id(0)

        @pl.when(i == 0)
        def _():
            dbias_ref[...] = jnp.zeros(dbias_ref.shape, F32)

        dmg = _dot_nt(dh_ref[...], wo_ref[...])
        gc, ga = _gates(gc_ref, ga_ref, b_ref)
        dgc = dmg * yc_ref[...].astype(F32) * gc * (1.0 - gc)
        dga = dmg * ya_ref[...].astype(F32) * ga * (1.0 - ga)
        dbias_ref[:, :D_MODEL] += jnp.sum(dgc, axis=0, keepdims=True)
        dbias_ref[:, D_MODEL:] += jnp.sum(dga, axis=0, keepdims=True)
        dyc = dmg * gc
        dya = dmg * ga
        dyc_ref[...] = dyc.astype(dyc_ref.dtype)
        dya_ref[...] = dya.astype(dya_ref.dtype)
        dhc_ref[...] = _dot_nt(dyc, woc_ref[...]).astype(dhc_ref.dtype)
        dha = _dot_nt(dya, woa_ref[...])
        z = za_ref[...].astype(F32)
        sg = _sigmoid(z)
        ov = o_ref[...]
        dout = dha * z * sg
        do_ref[...] = dout.astype(do_ref.dtype)
        dsum_ref[...] = _select_cols(dout * ov, e_ref[...], terms=2)
        db3_ref[0] = (dha * ov * sg * (1.0 + z * (1.0 - sg))).astype(db3_ref.dtype)
        db3_ref[1] = dgc.astype(db3_ref.dtype)
        db3_ref[2] = dga.astype(db3_ref.dtype)

    row = pl.BlockSpec((tm, D_MODEL), lambda i: (i, 0))
    wsp = pl.BlockSpec((D_MODEL, D_MODEL), lambda i: (0, 0))
    act = jax.ShapeDtypeStruct((s_n, D_MODEL), ACT_DTYPE)
    return pl.pallas_call(
        body, grid=(s_n // tm,),
        in_specs=[row, wsp, wsp, wsp, row, row,
                  pl.BlockSpec((tm, D_MODEL), lambda i: (i, 8)), pl.BlockSpec((tm, D_MODEL), lambda i: (i, 9)),
                  pl.BlockSpec((1, 2 * D_MODEL), lambda i: (0, 0)), row,
                  pl.BlockSpec((tm, D_MODEL), lambda i: (i, 7)), pl.BlockSpec((D_MODEL, LANES), lambda i: (0, 0))],
        out_specs=[row, row, row, row, pl.BlockSpec((tm, LANES), lambda i: (i, 0)),
                   pl.BlockSpec((3, tm, D_MODEL), lambda i: (0, i, 0)),
                   pl.BlockSpec((1, 2 * D_MODEL), lambda i: (0, 0))],
        out_shape=[act, act, act, act, jax.ShapeDtypeStruct((s_n, LANES), F32),
                   jax.ShapeDtypeStruct((3, s_n, D_MODEL), ACT_DTYPE),
                   jax.ShapeDtypeStruct((1, 2 * D_MODEL), F32)],
        name="merge_bwd",
    )(dh2b, wo, woc, woa, yc, ya, proj, proj, b_merge, o, proj, _head_sum_matrix())


def _mm_lhs_resident(a, b, tn, name):
    m_n, k_n = a.shape
    n_n = b.shape[1]

    def body(a_ref, b_ref, o_ref):
        o_ref[...] = _dot(a_ref[...], b_ref[...])

    return pl.pallas_call(
        body, grid=(n_n // tn,),
        in_specs=[pl.BlockSpec((m_n, k_n), lambda n: (0, 0)), pl.BlockSpec((k_n, tn), lambda n: (0, n))],
        out_specs=pl.BlockSpec((m_n, tn), lambda n: (0, n)),
        out_shape=jax.ShapeDtypeStruct((m_n, n_n), F32),
        name=name,
    )(a, b)


def _conv_bwd(proj, conv_w, dhc):
    s_n = proj.shape[0]
    lr = s_n // N_RES
    pv = proj.reshape(N_RES, lr, IN_COLS)

    def body(xc_ref, bg_ref, cg_ref, zc_ref, w_ref, dhc_ref, da4_ref, dw_ref, dc_ref):
        w = w_ref[...]
        row = lax.broadcasted_iota(jnp.int32, (lr, LANES), 0)
        dw = [jnp.zeros((1, LANES), F32) for _ in range(3)]
        for r in range(N_RES):
            a, am1, am2 = _conv_terms(xc_ref, cg_ref, r, row, lr)
            c = w[0:1] * am2 + w[1:2] * am1 + w[2:3] * a
            z = zc_ref[r].astype(F32)
            sg = _sigmoid(z)
            sz = z * sg
            bg = bg_ref[r].astype(F32)
            dh = dhc_ref[r].astype(F32)
            da4_ref[1, r] = (dh * sz * c).astype(da4_ref.dtype)
            da4_ref[3, r] = (dh * bg * c * sg * (1.0 + z * (1.0 - sg))).astype(da4_ref.dtype)
            dc = dh * sz * bg
            dc_ref[r] = dc
            dw[0] = dw[0] + jnp.sum(dc * am2, axis=0, keepdims=True)
            dw[1] = dw[1] + jnp.sum(dc * am1, axis=0, keepdims=True)
            dw[2] = dw[2] + jnp.sum(dc * a, axis=0, keepdims=True)
        dw_ref[0:1, :] = dw[0]
        dw_ref[1:2, :] = dw[1]
        dw_ref[2:3, :] = dw[2]

        def shift_up(v):
            return jnp.where(row < lr - 1, pltpu.roll(v, lr - 1, 0), 0.0)

        for r in range(N_RES):
            dp1 = dc_ref[r + 1] if r + 1 < N_RES else shift_up(dc_ref[0])
            dp2 = dc_ref[r + 2] if r + 2 < N_RES else shift_up(dc_ref[r + 2 - N_RES])
            da = w[2:3] * dc_ref[r] + w[1:2] * dp1 + w[0:1] * dp2
            da4_ref[0, r] = (da * cg_ref[r].astype(F32)).astype(da4_ref.dtype)
            da4_ref[2, r] = (da * xc_ref[r].astype(F32)).astype(da4_ref.dtype)

    def col(part):
        return pl.BlockSpec((N_RES, lr, LANES), lambda j: (0, 0, part * 8 + j))

    da4, dw = pl.pallas_call(
        body, grid=(D_MODEL // LANES,),
        in_specs=[col(0), col(1), col(2), col(3), pl.BlockSpec((3, LANES), lambda j: (0, j)),
                  pl.BlockSpec((N_RES, lr, LANES), lambda j: (0, 0, j))],
        out_specs=[pl.BlockSpec((4, N_RES, lr, LANES), lambda j: (0, 0, 0, j)),
                   pl.BlockSpec((3, LANES), lambda j: (0, j))],
        out_shape=[jax.ShapeDtypeStruct((4, N_RES, lr, D_MODEL), ACT_DTYPE),
                   jax.ShapeDtypeStruct((3, D_MODEL), F32)],
        scratch_shapes=[pltpu.VMEM((N_RES, lr, LANES), F32)],
        name="conv_bwd",
    )(pv, pv, pv, pv, conv_w, dhc.reshape(N_RES, lr, D_MODEL))
    return da4.reshape(4, s_n, D_MODEL), dw


def _attn_bwd(proj, dout, lse, dsum, slopes, d):
    g_n, rq = PATTERNS[d]
    un = _Units(g_n, rq)
    s_n = proj.shape[0]
    lr = s_n // N_RES
    nb = lr // rq
    q_n = g_n * rq
    d0, m0, d1, m1 = (np.ascontiguousarray(t.T) for t in _attn_tables(d))
    first, n_more, later = un.plan(lr, BWD_BATCH[d])
    bsz = un.count(first)
    gd = RES_PER_STEP if un.per_res else g_n

    def body(sl_ref, q_ref, k_ref, v_ref, do_ref, lse_ref, ds_ref, d0_ref, m0_ref, d1_ref, m1_ref, out_ref,
             b0_ref, b1_ref, lt_ref, dt_ref, dk_ref, dv_ref):
        hp = pl.program_id(1)
        for h in (0, 1):
            slope = sl_ref[2 * hp + h]
            b0_ref[:, h * q_n:(h + 1) * q_n] = m0_ref[...] - slope * d0_ref[...]
            b1_ref[:, h * q_n:(h + 1) * q_n] = m1_ref[...] - slope * d1_ref[...]
        dk_ref[...] = jnp.zeros(dk_ref.shape, F32)
        dv_ref[...] = jnp.zeros(dv_ref.shape, F32)
        low = lax.broadcasted_iota(jnp.int32, (1, q_n, LANES), 2) < HEAD_DIM
        row16 = pl.multiple_of(16 * hp, 16)

        def query_rows(stat_ref, t_ref, qs):
            tiles = un.load_q(stat_ref, qs)
            for b in range(bsz):
                t_ref[b] = tiles[b].T
            t16 = t_ref[:, pl.ds(row16, 16), :]
            return jnp.concatenate([t16[:, 0:1, :], t16[:, 8:9, :]], axis=2)

        def batch(qs, at_start):
            qq = _stack_heads(un.load_q(q_ref, qs) * 0.125, low)
            dd = _stack_heads(un.load_q(do_ref, qs), low)
            ks = un.load_k(k_ref, qs, at_start)
            vs = un.load_k(v_ref, qs, at_start)
            lrow = query_rows(lse_ref, lt_ref, qs)
            drow = query_rows(ds_ref, dt_ref, qs)
            pt = jnp.exp(_bdot(ks, qq, _BNT) + _batch_bias(un, qs, at_start, b0_ref, b1_ref) - lrow)
            dst = pt * (_bdot(vs, dd, _BNT) - drow)
            un.add_k(dv_ref, qs, _bdot(pt, dd, _BNN), at_start)
            un.add_k(dk_ref, qs, _bdot(dst, qq, _BNN), at_start)
            dq = _bdot(jnp.swapaxes(dst, 1, 2), ks, _BNN)
            un.store_q(out_ref, qs, jnp.where(low, dq[:, :q_n], dq[:, q_n:]) * 0.125, lead=(0,))

        batch(first, True)

        def more(j, carry):
            batch(later(j), False)
            return carry

        lax.fori_loop(1, 1 + n_more, more, 0)
        out_ref[1] = dk_ref[...].astype(out_ref.dtype)
        out_ref[2] = dv_ref[...].astype(out_ref.dtype)

    pv = _pattern_view(proj, g_n)
    full = lambda a: pl.BlockSpec(a.shape, lambda r, hp: (0, 0))
    out = pl.pallas_call(
        body, grid=_pattern_grid(g_n),
        in_specs=[pl.BlockSpec(memory_space=pltpu.SMEM),
                  _pattern_spec(g_n, lr, lambda hp: 32 + hp),
                  _pattern_spec(g_n, lr, lambda hp: 40 + hp),
                  _pattern_spec(g_n, lr, lambda hp: 48 + hp),
                  _pattern_spec(g_n, lr, lambda hp: hp),
                  _pattern_spec(g_n, lr, lambda hp: 0),
                  _pattern_spec(g_n, lr, lambda hp: 0),
                  full(d0), full(m0), full(d1), full(m1)],
        out_specs=_pattern_spec(g_n, lr, lambda hp: hp, lead=(3,)),
        out_shape=jax.ShapeDtypeStruct(_pattern_view_shape(s_n, D_MODEL, g_n, lead=(3,)), ACT_DTYPE),
        scratch_shapes=[pltpu.VMEM((d0.shape[0], 2 * q_n), F32), pltpu.VMEM((2 * q_n, 2 * q_n), F32),
                        pltpu.VMEM((bsz, LANES, q_n), F32), pltpu.VMEM((bsz, LANES, q_n), F32),
                        pltpu.VMEM((gd, lr, LANES), F32), pltpu.VMEM((gd, lr, LANES), F32)],
        name=f"attn_bwd_d{d}",
    )(slopes, pv, pv, pv, _pattern_view(dout, g_n), _pattern_view(lse, g_n), _pattern_view(dsum, g_n),
      d0, m0, d1, m1)
    return out.reshape(3, s_n, D_MODEL)


def _sum3(a, b, c):
    _, s_n, c_n = a.shape
    tm = 512

    def body(a_ref, b_ref, c_ref, o_ref):
        o_ref[...] = (a_ref[...] + b_ref[...] + c_ref[...]).astype(o_ref.dtype)

    spec = pl.BlockSpec((1, tm, c_n), lambda p, i: (p, i, 0))
    return pl.pallas_call(
        body, grid=(3, s_n // tm), in_specs=[spec] * 3, out_specs=spec,
        out_shape=jax.ShapeDtypeStruct(a.shape, ACT_DTYPE), name="sum_dqkv",
    )(a, b, c)


def _part_index(step, per, lo, n):
    return jnp.clip(step // per - lo, 0, n - 1)


def _dw_in(ut, da4, dc3, db3):
    s_n = ut.shape[1]
    tn = 512
    per = D_MODEL // tn
    shard_blocks = SHARD_COLS // tn

    def body(a_ref, p0_ref, p1_ref, p2_ref, o_ref):
        part = pl.program_id(0) // per

        @pl.when(part < 4)
        def _():
            o_ref[...] = _dot(a_ref[...], p0_ref[...])

        @pl.when((part >= 4) & (part < 7))
        def _():
            o_ref[...] = _dot(a_ref[...], p1_ref[...])

        @pl.when(part >= 7)
        def _():
            o_ref[...] = _dot(a_ref[...], p2_ref[...])

    def pspec(lo, n):
        return pl.BlockSpec((None, s_n, tn), lambda j: (_part_index(j, per, lo, n), 0, j % per))

    return pl.pallas_call(
        body, grid=(IN_COLS // tn,),
        in_specs=[pl.BlockSpec((D_MODEL, s_n), lambda j: (0, 0), pipeline_mode=pl.Buffered(1)),
                  pspec(0, 4), pspec(4, 3), pspec(7, 3)],
        out_specs=pl.BlockSpec((None, D_MODEL, tn), lambda j: (j // shard_blocks, 0, j % shard_blocks)),
        out_shape=jax.ShapeDtypeStruct((4, D_MODEL, SHARD_COLS), F32),
        name="dw_in",
    )(ut, da4, dc3, db3)


def _input_grad(da4, dc3, db3, w4, xp, norm_g, dh2, row0, rows):
    tm, tk = 256, 512
    per = D_MODEL // tk
    shard_blocks = SHARD_COLS // tk
    m0 = row0 // tm

    def body(p0_ref, p1_ref, p2_ref, w_ref, x_ref, g_ref, dh_ref, gx_ref, dg_ref):
        @pl.when(pl.program_id(0) == 0)
        def _():
            dg_ref[...] = jnp.zeros(dg_ref.shape, F32)

        du = None
        for k in range(IN_COLS // tk):
            part, cols = k // per, pl.ds((k % per) * tk, tk)
            ref, slot = (p0_ref, part) if part < 4 else (p1_ref, part - 4) if part < 7 else (p2_ref, part - 7)
            d = _dot_nt(ref[slot, :, cols], w_ref[k // shard_blocks, :, pl.ds((k % shard_blocks) * tk, tk)])
            du = d if du is None else du + d
        x = x_ref[...]
        r = lax.rsqrt(jnp.mean(x * x, axis=-1, keepdims=True) + EPS)
        nrm = x * r
        dg_ref[...] += jnp.sum(du * nrm, axis=0, keepdims=True)
        dn = du * g_ref[...]
        gx_ref[...] = dh_ref[...] + r * (dn - nrm * jnp.mean(dn * nrm, axis=-1, keepdims=True))

    def pspec(n):
        return pl.BlockSpec((n, tm, D_MODEL), lambda m: (0, m0 + m, 0))

    row_in = pl.BlockSpec((tm, D_MODEL), lambda m: (m0 + m, 0))
    vec = pl.BlockSpec((1, D_MODEL), lambda m: (0, 0))
    return pl.pallas_call(
        body, grid=(rows // tm,),
        in_specs=[pspec(4), pspec(3), pspec(3),
                  pl.BlockSpec(w4.shape, lambda m: (0, 0, 0), pipeline_mode=pl.Buffered(1)),
                  row_in, vec, row_in],
        out_specs=[pl.BlockSpec((tm, D_MODEL), lambda m: (m, 0)), vec],
        out_shape=[jax.ShapeDtypeStruct((rows, D_MODEL), F32), jax.ShapeDtypeStruct((1, D_MODEL), F32)],
        name="input_grad",
    )(da4, dc3, db3, w4, xp, norm_g, dh2)


class _Step:
    def __init__(self, x, tgt, norm_g, chip, w_own, after=0.0):
        self.norm_g, self.chip = norm_g, chip
        self.slopes = _alibi_slopes()
        self.xp, self.tp = _to_residue_major(x, tgt, after)
        self.u, self.ut = _rms_in(self.xp, norm_g)
        self.proj_own = _in_proj(self.u, chip, w_own=w_own)

    def mixers(self, w4, taps):
        self.w4, self.taps = w4, taps
        self.proj = _in_proj(self.u, self.chip, w4=w4, partial=self.proj_own)
        self.hc, self.hct = _conv_fwd(self.proj, taps)
        fwd = [_attn_fwd(self.proj, self.slopes, d) for d in PATTERNS]
        self.o, self.lse, self.ha, self.hat = _attn_combine([f[0] for f in fwd], [f[1] for f in fwd], self.proj)

    def merge_and_loss(self, woc, woa, wo, b_merge, final_g):
        self.woc, self.woa, self.wo, self.b_merge = woc, woa, wo, b_merge
        (self.yc, self.ya, _, self.mgt, self.dh2, self.dh2b, self.d_final_g, self.loss8) = _merge_loss(
            self.hc, self.ha, woc, woa, wo, self.proj, b_merge, self.xp, final_g, self.tp)

    def out_weight_grads(self):
        (dyc, dya, self.dhc, self.dout, self.dsum, self.db3, self.d_bias) = _merge_bwd(
            self.dh2b, self.wo, self.woc, self.woa, self.yc, self.ya, self.proj, self.b_merge, self.o)
        d_wo = _mm_lhs_resident(self.mgt, self.dh2b, 256, "dw_o")
        d_woc = _mm_lhs_resident(self.hct, dyc, 256, "dw_out_conv")
        d_woa = _mm_lhs_resident(self.hat, dya, 256, "dw_out_attn")
        return d_woc, d_woa, d_wo

    def conv_grads(self, after=0.0):
        self.da4, self.d_taps = _conv_bwd(self.proj, self.taps + after, self.dhc)

    def in_weight_grad(self, after=0.0):
        slopes = self.slopes + after
        self.dc3 = _sum3(*[_attn_bwd(self.proj, self.dout, self.lse, self.dsum, slopes, d) for d in PATTERNS])
        return _dw_in(self.ut, self.da4, self.dc3, self.db3)

    def input_grad(self, half, after=0.0):
        rows = self.xp.shape[0] // 2
        return _input_grad(self.da4, self.dc3, self.db3, self.w4, self.xp, self.norm_g + after, self.dh2,
                           half * rows, rows)


def _local_grads(x, tgt, norm_g, w4, b_merge, conv_w, woc, woa, wo, final_g):
    st = _Step(x, tgt, norm_g, jnp.zeros((1,), jnp.int32), w4[0])
    st.mixers(w4, conv_w)
    st.merge_and_loss(woc, woa, wo, b_merge, final_g)
    d_woc, d_woa, d_wo = st.out_weight_grads()
    st.conv_grads()
    d_w4 = st.in_weight_grad()
    gx_lo, dg_lo = st.input_grad(0)
    gx_hi, dg_hi = st.input_grad(1)
    return (st.loss8, _to_natural(gx_lo, gx_hi), dg_lo + dg_hi, d_w4, st.d_bias, st.d_taps, d_woc, d_woa, d_wo,
            st.d_final_g)


MESH = pl.DeviceIdType.MESH
_CHIP_FLIPS = ((1, 0), (0, 1), (1, 1))
_ANY = pl.BlockSpec(memory_space=pl.ANY)


def _place():
    return lax.axis_index("x"), lax.axis_index("y"), lax.axis_index("c")


def _flip(v, f):
    return 1 - v if f else v


def _remote(src, dst, send_sems, recv_sems, k, device):
    return pltpu.make_async_remote_copy(src_ref=src, dst_ref=dst, send_sem=send_sems.at[k], recv_sem=recv_sems.at[k],
                                        device_id=device, device_id_type=MESH)


def _place_shard(w, chip):
    rows, cols = w.shape
    tm = 128

    def body(chip_ref, w_ref, o_ref):
        o_ref[0] = w_ref[...].astype(o_ref.dtype)

    return pl.pallas_call(
        body,
        grid_spec=pltpu.PrefetchScalarGridSpec(
            num_scalar_prefetch=1, grid=(rows // tm,),
            in_specs=[pl.BlockSpec((tm, cols), lambda i, chip_ref: (i, 0))],
            out_specs=pl.BlockSpec((1, tm, cols), lambda i, chip_ref: (chip_ref[0], i, 0))),
        out_shape=jax.ShapeDtypeStruct((4, rows, cols), MXU_DTYPE),
        name="place_shard",
    )(chip, w)


def _gather_copies(arrs, _, send_sems, recv_sems):
    x, y, c = _place()
    out = []
    for a, arr in enumerate(arrs):
        h = arr.shape[1] // 2
        mine = arr.at[2 * x + y, pl.ds(pl.multiple_of(c * h, 8), h)]
        for t, (fx, fy) in enumerate(_CHIP_FLIPS):
            out.append(_remote(mine, mine, send_sems, recv_sems, 3 * a + t, (_flip(x, fx), _flip(y, fy), c)))
    return out


def _forward_to_sibling(arrs):
    n = len(arrs)

    def body(*refs):
        outs = refs[n:2 * n]
        send_sems, recv_sems = refs[2 * n:]
        x, y, c = _place()
        sibling = (x, y, 1 - c)
        started = []
        for a in range(n):
            h = outs[a].shape[1] // 2
            rows = pl.ds(pl.multiple_of(c * h, 8), h)
            for t, (fx, fy) in enumerate(_CHIP_FLIPS):
                landed = outs[a].at[2 * _flip(x, fx) + _flip(y, fy), rows]
                cp = _remote(landed, landed, send_sems, recv_sems, 3 * a + t, sibling)
                cp.start()
                started.append(cp)
        for a in range(n):
            h = outs[a].shape[1] // 2
            rows = pl.ds(pl.multiple_of((1 - c) * h, 8), h)
            for t, (fx, fy) in enumerate(_CHIP_FLIPS):
                handed = outs[a].at[2 * _flip(x, fx) + _flip(y, fy), rows]
                _remote(handed, handed, send_sems, recv_sems, 3 * a + t, sibling).wait_recv()
        for cp in started:
            cp.wait_send()

    return pl.pallas_call(
        body, in_specs=[_ANY] * n, out_specs=[_ANY] * n,
        out_shape=[jax.ShapeDtypeStruct(s.shape, s.dtype) for s in arrs],
        input_output_aliases={a: a for a in range(n)},
        scratch_shapes=[pltpu.SemaphoreType.DMA((3 * n,)), pltpu.SemaphoreType.DMA((3 * n,))],
        name="gathered_to_sibling",
    )(*arrs)


_HBM = pl.BlockSpec(memory_space=pltpu.HBM)
_SEM = pl.BlockSpec(memory_space=pltpu.SEMAPHORE)
_EFFECT = pltpu.SideEffectType.DATAFLOW_SIDE_EFFECTING


class _SplitExchange:
    def __init__(self, name, srcs, land_shapes, n_copies, copies, riders=()):
        self.name, self.n, self.nl, self.copies = name, len(srcs), len(land_shapes), copies
        n, nb = self.n, len(srcs) + len(land_shapes)
        lands = [lax.empty(s.shape, s.dtype) for s in land_shapes]
        bufs = [pltpu.with_memory_space_constraint(a, pltpu.HBM) for a in (*srcs, *lands, *riders)]
        na = len(bufs)

        def body(*refs):
            send_sems, recv_sems = refs[na], refs[na + 1]
            for cp in copies(refs[:n], refs[n:nb], send_sems, recv_sems):
                cp.start()
            refs[-1][...] = jnp.zeros(refs[-1].shape, F32)

        outs = pl.pallas_call(
            body, name=name + "_start",
            in_specs=[_HBM] * na,
            out_specs=[_SEM, _SEM] + [_HBM] * na + [pl.BlockSpec(memory_space=pltpu.VMEM)],
            out_shape=[pltpu.SemaphoreType.DMA((n_copies,)), pltpu.SemaphoreType.DMA((n_copies,))]
            + [pltpu.HBM(b.shape, b.dtype) for b in bufs] + [jax.ShapeDtypeStruct((8, LANES), F32)],
            input_output_aliases={i: 2 + i for i in range(na)},
            compiler_params=pltpu.CompilerParams(has_side_effects=_EFFECT),
        )(*bufs)
        self.sems, self.bufs, self.riders, self.token = outs[:2], outs[2:2 + nb], outs[2 + nb:2 + na], outs[-1]

    def after(self):
        return self.token[0, 0]

    def wait(self, done, riders=()):
        n, nb, copies = self.n, self.n + self.nl, self.copies
        bufs = [*self.bufs, *[pltpu.with_memory_space_constraint(a, pltpu.HBM) for a in riders]]
        na = len(bufs)
        done = list(done) if isinstance(done, (list, tuple)) else [done]

        def body(*refs):
            send_sems, recv_sems = refs[na], refs[na + 1]
            for cp in copies(refs[:n], refs[n:nb], send_sems, recv_sems):
                cp.wait_send()
                cp.wait_recv()

        outs = pl.pallas_call(
            body, name=self.name + "_wait",
            in_specs=[_HBM] * na + [_SEM, _SEM] + [_ANY] * len(done),
            out_specs=[_HBM] * na,
            out_shape=[pltpu.HBM(b.shape, b.dtype) for b in bufs],
            input_output_aliases={i: i for i in range(na)},
            compiler_params=pltpu.CompilerParams(has_side_effects=_EFFECT),
        )(*bufs, *self.sems, *done)
        return outs[:n], outs[n:nb], outs[nb:]


def _sibling_copies(srcs, lands, send_sems, recv_sems):
    x, y, c = _place()
    out = []
    for a, (src, land) in enumerate(zip(srcs, lands)):
        h = src.shape[1] // 2
        theirs = pl.ds(pl.multiple_of((1 - c) * h, 8), h)
        out.append(_remote(src.at[:, theirs], land, send_sems, recv_sems, a, (x, y, 1 - c)))
    return out


def _grads_to_sibling(name, grads):
    shapes = [jax.ShapeDtypeStruct((4, g.shape[1] // 2, g.shape[2]), g.dtype) for g in grads]
    return _SplitExchange(name, grads, shapes, len(grads), _sibling_copies)


def _chip_copies(srcs, lands, send_sems, recv_sems):
    x, y, c = _place()
    out = []
    for a, (src, land) in enumerate(zip(srcs, lands)):
        for t, (fx, fy) in enumerate(_CHIP_FLIPS):
            tx, ty = _flip(x, fx), _flip(y, fy)
            out.append(_remote(src.at[2 * tx + ty], land.at[t], send_sems, recv_sems, 3 * a + t, (tx, ty, c)))
    return out


def _grads_to_chips(name, parts):
    shapes = [jax.ShapeDtypeStruct((3, *p.shape[1:]), p.dtype) for p in parts]
    return _SplitExchange(name, parts, shapes, 3 * len(parts), _chip_copies)


def _add_halves(g, r, half):
    _, rows, cols = g.shape
    h = rows // 2
    tm = min(h, 128)
    nt = h // tm

    def body(half_ref, g_ref, r_ref, f_ref, b_ref):
        s = g_ref[...] + r_ref[...]
        f_ref[...] = s
        b_ref[...] = s.astype(b_ref.dtype)

    spec = pl.BlockSpec((1, tm, cols), lambda j, i, half_ref: (j, i, 0))
    return pl.pallas_call(
        body,
        grid_spec=pltpu.PrefetchScalarGridSpec(
            num_scalar_prefetch=1, grid=(4, nt),
            in_specs=[pl.BlockSpec((1, tm, cols), lambda j, i, half_ref: (j, half_ref[0] * nt + i, 0)), spec],
            out_specs=[spec, spec]),
        out_shape=[jax.ShapeDtypeStruct((4, h, cols), F32), jax.ShapeDtypeStruct((4, h, cols), BF16)],
        name="add_sibling_grads",
    )(half, g, r)


def _add_chips(own, recv, where):
    _, h, cols = own.shape
    tm = min(h, 128)
    nt = h // tm

    def body(where_ref, o_ref, r_ref, out_ref):
        out_ref[...] = ((o_ref[0] + r_ref[0].astype(F32)) + r_ref[1].astype(F32)) + r_ref[2].astype(F32)

    return pl.pallas_call(
        body,
        grid_spec=pltpu.PrefetchScalarGridSpec(
            num_scalar_prefetch=1, grid=(nt,),
            in_specs=[pl.BlockSpec((1, tm, cols), lambda i, where_ref: (where_ref[0], i, 0)),
                      pl.BlockSpec((3, tm, cols), lambda i, where_ref: (0, i, 0))],
            out_specs=pl.BlockSpec((tm, cols), lambda i, where_ref: (where_ref[1] * nt + i, 0))),
        out_shape=jax.ShapeDtypeStruct((2 * h, cols), F32),
        name="add_chip_grads",
    )(where, own, recv)


def _share_halves(shards):
    n = len(shards)

    def body(*refs):
        outs = refs[n:2 * n]
        send_sems, recv_sems = refs[2 * n:]
        x, y, c = _place()
        copies = []
        for a in range(n):
            h = outs[a].shape[0] // 2
            mine = outs[a].at[pl.ds(pl.multiple_of(c * h, 8), h)]
            copies.append(_remote(mine, mine, send_sems, recv_sems, a, (x, y, 1 - c)))
        for cp in copies:
            cp.start()
        for a, cp in enumerate(copies):
            cp.wait_send()
            h = outs[a].shape[0] // 2
            theirs = outs[a].at[pl.ds(pl.multiple_of((1 - c) * h, 8), h)]
            _remote(theirs, theirs, send_sems, recv_sems, a, (x, y, 1 - c)).wait_recv()

    return pl.pallas_call(
        body, in_specs=[_ANY] * n, out_specs=[_ANY] * n,
        out_shape=[jax.ShapeDtypeStruct(p.shape, p.dtype) for p in shards],
        input_output_aliases={a: a for a in range(n)},
        scratch_shapes=[pltpu.SemaphoreType.DMA((n,)), pltpu.SemaphoreType.DMA((n,))],
        name="share_reduced_halves",
    )(*shards)


def _exchange_small(rows, reduce):
    cols = rows[0].shape[1]
    n = len(rows)
    assert sum(r.shape[0] for r in rows) <= 8

    def body(*refs):
        ins, out_ref = refs[:n], refs[n]
        vec_ref, gath_ref, send_sems, recv_sems = refs[n + 1:]
        x, y, c = _place()
        me = 4 * x + 2 * y + c
        vec_ref[...] = jnp.zeros(vec_ref.shape, F32)
        at = 0
        for r in ins:
            vec_ref[at:at + r.shape[0], :] = r[...]
            at += r.shape[0]
        copies = []
        for k in range(1, 8):
            peer = (_flip(x, (k >> 2) & 1), _flip(y, (k >> 1) & 1), _flip(c, k & 1))
            copies.append(_remote(vec_ref, gath_ref.at[me], send_sems, recv_sems, k - 1, peer))
        for cp in copies:
            cp.start()
        gath_ref[me] = vec_ref[...]
        for cp in copies:
            cp.wait()
        if reduce:
            tot = gath_ref[0]
            for dev in range(1, 8):
                tot = tot + gath_ref[dev]
            out_ref[...] = tot
            out_ref[7:8, :] = jnp.zeros((1, cols), F32) + jnp.sum(tot[7:8, :])
        else:
            out_ref[...] = gath_ref[...]

    vm = pl.BlockSpec(memory_space=pltpu.VMEM)
    return pl.pallas_call(
        body, in_specs=[vm] * n, out_specs=vm,
        out_shape=jax.ShapeDtypeStruct((8, cols) if reduce else (8, 8, cols), F32),
        scratch_shapes=[pltpu.VMEM((8, cols), F32), pltpu.VMEM((8, 8, cols), F32),
                        pltpu.SemaphoreType.DMA((7,)), pltpu.SemaphoreType.DMA((7,))],
        name="reduce_small" if reduce else "gather_small",
    )(*rows)


def _adamw(w, g, m, v, name):
    rows, cols = w.shape
    tm = 128 if rows % 128 == 0 else rows

    def body(w_ref, g_ref, m_ref, v_ref, d_ref, m2_ref, v2_ref, gout_ref):
        gr = g_ref[...]
        m2 = ADAM_B1 * m_ref[...] + (1.0 - ADAM_B1) * gr
        v2 = ADAM_B2 * v_ref[...] + (1.0 - ADAM_B2) * (gr * gr)
        m_hat = m2 / (1.0 - ADAM_B1 ** ADAM_STEP)
        v_hat = v2 / (1.0 - ADAM_B2 ** ADAM_STEP)
        d_ref[...] = -ADAM_LR * (m_hat / (jnp.sqrt(v_hat) + ADAM_EPS) + ADAM_WD * w_ref[...])
        m2_ref[...] = m2
        v2_ref[...] = v2
        gout_ref[...] = gr

    spec = pl.BlockSpec((tm, cols), lambda i: (i, 0))
    sds = jax.ShapeDtypeStruct((rows, cols), F32)
    return pl.pallas_call(body, grid=(rows // tm,), in_specs=[spec] * 4, out_specs=[spec] * 4,
                          out_shape=[sds] * 4, name=name)(w, g, m, v)


def kernel(x, norm_g, w_in, b_merge, conv_w, w_out_conv, w_out_attn, w_o, final_g, loss_target, m_norm_g, m_w_in, m_b_merge, m_conv_w, m_w_out_conv, m_w_out_attn, m_w_o, m_final_g, v_norm_g, v_w_in, v_b_merge, v_conv_w, v_w_out_conv, v_w_out_attn, v_w_o, v_final_g):
    mx, my, mc = _place()
    chip = (2 * mx + my).astype(jnp.int32)
    seq = x.shape[1]

    chip1 = chip.reshape(1)
    slots = [_place_shard(w[0], chip1) for w in (w_in, w_out_conv, w_out_attn, w_o)]
    taps8 = _exchange_small([conv_w[0]], reduce=False)
    taps = jnp.concatenate([taps8[2 * j, :3, :] for j in range(4)], axis=1)
    gather_in = _SplitExchange("gather_w_in", slots[:1], [], 3, _gather_copies)
    st = _Step(x[0], loss_target[0], norm_g, chip1, w_in[0], after=gather_in.after())
    (w4,), _, out_slots = gather_in.wait([st.ut, st.proj_own], riders=slots[1:])
    gather_out = _SplitExchange("gather_w_out", out_slots, [], 9, _gather_copies, riders=[w4])
    (w4,) = _forward_to_sibling(gather_out.riders)
    st.mixers(w4, taps)
    out_ws, _, _ = gather_out.wait(st.o)
    woc, woa, wo = [w.reshape(D_MODEL, D_MODEL) for w in _forward_to_sibling(out_ws)]
    st.merge_and_loss(woc, woa, wo, b_merge, final_g.reshape(1, D_MODEL))

    half = mc.astype(jnp.int32).reshape(1)
    where = jnp.stack([chip, mc.astype(jnp.int32)])
    out_grads = [g.reshape(4, -1, D_MODEL) for g in st.out_weight_grads()]
    to_sibling = _grads_to_sibling("out_grads_to_sibling", out_grads)
    st.conv_grads(after=to_sibling.after())
    out_grads, from_sibling, _ = to_sibling.wait(st.da4)
    out_partial = [_add_halves(g, r, half) for g, r in zip(out_grads, from_sibling)]
    to_chips = _grads_to_chips("out_grads_to_chips", [p[1] for p in out_partial])
    d_w4 = st.in_weight_grad(after=to_chips.after())
    from_chips = to_chips.wait(st.dc3)[1]

    to_sibling = _grads_to_sibling("in_grad_to_sibling", [d_w4])
    gx_lo, dg_lo = st.input_grad(0, after=to_sibling.after())
    (d_w4,), (from_sibling,), _ = to_sibling.wait(gx_lo)
    in_partial = _add_halves(d_w4, from_sibling, half)
    to_chips = _grads_to_chips("in_grad_to_chips", [in_partial[1]])
    gx_hi, dg_hi = st.input_grad(1, after=to_chips.after())
    grad_x = _to_natural(gx_lo, gx_hi)

    where_late = where + to_chips.after().astype(jnp.int32)
    out_reduced = [_add_chips(p[0], r, where_late) for p, r in zip(out_partial, from_chips)]
    g_woc, g_woa, g_wo = _share_halves(out_reduced)
    small = _exchange_small([dg_lo + dg_hi, st.d_bias.reshape(2, D_MODEL), st.d_taps, st.d_final_g,
                             st.loss8.reshape(1, D_MODEL)], reduce=True)
    loss = (0.5 / D_MODEL) * small[7, 0]
    g_taps = lax.dynamic_slice(small[3:6], (0, chip * (D_MODEL // 4)), (3, D_MODEL // 4))
    upd = {
        "norm_g": _adamw(norm_g, small[0:1], m_norm_g, v_norm_g, "adamw_norm_g"),
        "b_merge": _adamw(b_merge, small[1:3].reshape(1, 2 * D_MODEL), m_b_merge, v_b_merge, "adamw_b_merge"),
        "conv_w": _adamw(conv_w[0], g_taps, m_conv_w[0], v_conv_w[0], "adamw_conv_w"),
        "w_out_conv": _adamw(w_out_conv[0], g_woc, m_w_out_conv[0], v_w_out_conv[0], "adamw_w_out_conv"),
        "w_out_attn": _adamw(w_out_attn[0], g_woa, m_w_out_attn[0], v_w_out_attn[0], "adamw_w_out_attn"),
        "w_o": _adamw(w_o[0], g_wo, m_w_o[0], v_w_o[0], "adamw_w_o"),
        "final_g": _adamw(final_g.reshape(1, D_MODEL), small[6:7], m_final_g.reshape(1, D_MODEL),
                          v_final_g.reshape(1, D_MODEL), "adamw_final_g"),
    }
    behind = [grad_x] + [u[0] for u in upd.values()]
    in_reduced = _add_chips(in_partial[0], to_chips.wait(behind)[1][0], where)
    (g_w_in,) = _share_halves([in_reduced])
    upd["w_in"] = _adamw(w_in[0], g_w_in, m_w_in[0], v_w_in[0], "adamw_w_in")

    names = ["norm_g", "w_in", "b_merge", "conv_w", "w_out_conv", "w_out_attn", "w_o", "final_g"]
    shapes = [norm_g.shape, w_in.shape, b_merge.shape, conv_w.shape, w_out_conv.shape, w_out_attn.shape,
              w_o.shape, final_g.shape]
    outs = [loss, grad_x.reshape(1, seq, D_MODEL)]
    for k in (3, 0, 1, 2):
        outs += [upd[n][k].reshape(s) for n, s in zip(names, shapes)]
    return tuple(outs)
```

```python
import functools

import numpy as np
import jax
import jax.numpy as jnp
from jax import lax
from jax.experimental import pallas as pl
from jax.experimental.pallas import tpu as pltpu

F32 = jnp.float32
BF16 = jnp.bfloat16
MXU_DTYPE = jnp.bfloat16
ACT_DTYPE = jnp.bfloat16

D_MODEL = 1024
N_HEADS = 16
HEAD_DIM = 64
QB = 128
N_RES = 16
LANES = 128
HP = N_HEADS * HEAD_DIM // LANES
IN_COLS = 10 * D_MODEL
SHARD_COLS = IN_COLS // 4
EPS = 1e-6
NEG = -1e30

ADAM_LR, ADAM_B1, ADAM_B2, ADAM_EPS, ADAM_WD, ADAM_STEP = 0.001, 0.9, 0.999, 1e-08, 0.01, 10

PATTERNS = {1: (16, 8), 4: (4, 32), 16: (1, 128)}

_NN = (((1,), (0,)), ((), ()))
_NT = (((1,), (1,)), ((), ()))


def _dot(a, b):
    return lax.dot_general(a.astype(MXU_DTYPE), b.astype(MXU_DTYPE), _NN, preferred_element_type=F32)


def _dot_nt(a, b):
    return lax.dot_general(a.astype(MXU_DTYPE), b.astype(MXU_DTYPE), _NT, preferred_element_type=F32)


def _split3(x):
    hi = x.astype(BF16)
    r1 = x - hi.astype(F32)
    mid = r1.astype(BF16)
    lo = (r1 - mid.astype(F32)).astype(BF16)
    return hi, mid, lo


def _select_rows(sel, x):
    return sum(lax.dot_general(sel, t, _NN, preferred_element_type=F32) for t in _split3(x))


def _select_cols(x, sel, terms=3):
    return sum(lax.dot_general(t, sel, _NN, preferred_element_type=F32) for t in _split3(x)[:terms])


def _sigmoid(z):
    return 1.0 / (1.0 + jnp.exp(-z))


def _perm_matrix():
    idx = np.arange(256)
    p = np.zeros((256, 256), np.float32)
    p[(idx % 16) * 16 + idx // 16, idx] = 1.0
    return jnp.asarray(p, BF16)


def _head_expand_matrix():
    e = np.zeros((LANES, D_MODEL), np.float32)
    for h in range(N_HEADS):
        e[8 * h, HEAD_DIM * h:HEAD_DIM * (h + 1)] = 1.0
    return jnp.asarray(e, BF16)


def _head_sum_matrix():
    e = np.zeros((D_MODEL, LANES), np.float32)
    for h in range(N_HEADS):
        e[HEAD_DIM * h:HEAD_DIM * (h + 1), 8 * h:8 * (h + 1)] = 1.0
    return jnp.asarray(e, BF16)


def _attn_tables(d):
    g_n, rq = PATTERNS[d]
    q_n = g_n * rq
    gq, iq = np.arange(q_n) // rq, np.arange(q_n) % rq

    def tab(kn, base):
        k_n = g_n * kn
        gk, jk = np.arange(k_n) // kn, np.arange(k_n) % kn
        delta = g_n * (base + iq[:, None] - jk[None, :]) + gq[:, None] - gk[None, :]
        valid = (delta >= 0) & (delta <= QB)
        dist = np.where(valid, d * delta, 0).astype(np.float32)
        madd = np.where(valid, 0.0, NEG).astype(np.float32)
        return dist, madd

    d0, m0 = tab(rq if g_n == 1 else 2 * rq, 0)
    d1, m1 = tab(2 * rq, rq)
    return d0, m0, d1, m1


def _alibi_slopes():
    return jnp.exp2(-8.0 * jnp.arange(1, N_HEADS + 1, dtype=F32) / N_HEADS)


def _to_residue_major(x, tgt, after=0.0):
    s_n, c_n = x.shape
    lr = s_n // N_RES
    pm = (_perm_matrix().astype(F32) + after).astype(BF16)

    def body(p_ref, x_ref, t_ref, xo_ref, to_ref):
        pm = p_ref[...]
        xo_ref[...] = _select_rows(pm, x_ref[...]).reshape(16, 16, c_n)
        to_ref[...] = _select_rows(pm, t_ref[...]).reshape(16, 16, c_n)

    nat = pl.BlockSpec((256, c_n), lambda i: (i, 0))
    res = pl.BlockSpec((16, 16, c_n), lambda i: (0, i, 0))
    xo, to = pl.pallas_call(
        body, grid=(s_n // 256,),
        in_specs=[pl.BlockSpec((256, 256), lambda i: (0, 0)), nat, nat],
        out_specs=[res, res],
        out_shape=[jax.ShapeDtypeStruct((16, lr, c_n), F32)] * 2,
        name="perm_in",
    )(pm, x, tgt)
    return xo.reshape(s_n, c_n), to.reshape(s_n, c_n)


def _to_natural(gx_lo, gx_hi):
    half_rows, c_n = gx_lo.shape
    lr = half_rows // (N_RES // 2)

    def body(p_ref, lo_ref, hi_ref, o_ref):
        g = jnp.concatenate([lo_ref[...], hi_ref[...]], axis=0)
        o_ref[...] = _select_rows(p_ref[...], g.reshape(256, c_n))

    half = pl.BlockSpec((8, 16, c_n), lambda i: (0, i, 0))
    return pl.pallas_call(
        body, grid=(lr // 16,),
        in_specs=[pl.BlockSpec((256, 256), lambda i: (0, 0)), half, half],
        out_specs=pl.BlockSpec((256, c_n), lambda i: (i, 0)),
        out_shape=jax.ShapeDtypeStruct((2 * half_rows, c_n), F32),
        name="perm_out",
    )(_perm_matrix(), gx_lo.reshape(8, lr, c_n), gx_hi.reshape(8, lr, c_n))


def _rms_in(xp, norm_g):
    s_n, c_n = xp.shape
    tm = 512

    def body(x_ref, g_ref, u_ref, ut_ref):
        x = x_ref[...]
        r = lax.rsqrt(jnp.mean(x * x, axis=-1, keepdims=True) + EPS)
        u = x * r * g_ref[...]
        u_ref[...] = u.astype(u_ref.dtype)
        ut_ref[...] = u.T.astype(ut_ref.dtype)

    return pl.pallas_call(
        body, grid=(s_n // tm,),
        in_specs=[pl.BlockSpec((tm, c_n), lambda i: (i, 0)), pl.BlockSpec((1, c_n), lambda i: (0, 0))],
        out_specs=[pl.BlockSpec((tm, c_n), lambda i: (i, 0)), pl.BlockSpec((c_n, tm), lambda i: (0, i))],
        out_shape=[jax.ShapeDtypeStruct((s_n, c_n), ACT_DTYPE), jax.ShapeDtypeStruct((c_n, s_n), ACT_DTYPE)],
        name="rms_in",
    )(xp, norm_g)


def _in_proj(u, chip, w_own=None, w4=None, partial=None):
    s_n = u.shape[0]
    tn, cm = 512, 512
    per = SHARD_COLS // tn
    own = partial is None

    def body(chip_ref, a_ref, b_ref, *rest):
        o_ref = rest[-1]
        b = b_ref[...]
        for c in range(s_n // cm):
            o_ref[c * cm:(c + 1) * cm, :] = _dot(a_ref[c * cm:(c + 1) * cm, :], b).astype(o_ref.dtype)

    def shard(n, chip_ref):
        return chip_ref[0] if own else (chip_ref[0] + 1 + n // per) % 4

    w_spec = (pl.BlockSpec((D_MODEL, tn), lambda n, c: (0, n)) if own else
              pl.BlockSpec((None, D_MODEL, tn), lambda n, c: (shard(n, c), 0, n % per)))
    return pl.pallas_call(
        body,
        grid_spec=pltpu.PrefetchScalarGridSpec(
            num_scalar_prefetch=1, grid=(per if own else 3 * per,),
            in_specs=[pl.BlockSpec((s_n, D_MODEL), lambda n, c: (0, 0)), w_spec] + ([] if own else [_ANY]),
            out_specs=pl.BlockSpec((s_n, tn), lambda n, c: (0, shard(n, c) * per + n % per))),
        out_shape=jax.ShapeDtypeStruct((s_n, IN_COLS), ACT_DTYPE),
        input_output_aliases={} if own else {3: 0},
        name="in_proj_own" if own else "in_proj",
    )(*([chip, u, w_own] if own else [chip, u, w4, partial]))


def _conv_terms(xc_ref, cg_ref, r, row, lr, cache):
    def a_of(q):
        if q not in cache:
            cache[q] = cg_ref[q].astype(F32) * xc_ref[q].astype(F32)
        return cache[q]

    def shift_down(v):
        return jnp.where(row >= 1, pltpu.roll(v, 1, 0), 0.0)

    a = a_of(r)
    am1 = a_of(r - 1) if r >= 1 else shift_down(a_of(N_RES - 1))
    am2 = a_of(r - 2) if r >= 2 else shift_down(a_of(N_RES - 2 + r))
    return a, am1, am2


def _conv_fwd(proj, conv_w):
    s_n = proj.shape[0]
    lr = s_n // N_RES
    pv = proj.reshape(N_RES, lr, IN_COLS)

    def body(xc_ref, bg_ref, cg_ref, zc_ref, w_ref, hc_ref, hct_ref):
        w = w_ref[...]
        row = lax.broadcasted_iota(jnp.int32, (lr, LANES), 0)
        products = {}
        for r in range(N_RES):
            a, am1, am2 = _conv_terms(xc_ref, cg_ref, r, row, lr, products)
            c = w[0:1] * am2 + w[1:2] * am1 + w[2:3] * a
            z = zc_ref[r].astype(F32)
            hc = z * _sigmoid(z) * bg_ref[r].astype(F32) * c
            hc_ref[r] = hc.astype(hc_ref.dtype)
            hct_ref[:, r * lr:(r + 1) * lr] = hc.T.astype(hct_ref.dtype)

    def col(part):
        return pl.BlockSpec((N_RES, lr, LANES), lambda j: (0, 0, part * 8 + j))

    hc, hct = pl.pallas_call(
        body, grid=(D_MODEL // LANES,),
        in_specs=[col(0), col(1), col(2), col(3), pl.BlockSpec((3, LANES), lambda j: (0, j))],
        out_specs=[pl.BlockSpec((N_RES, lr, LANES), lambda j: (0, 0, j)),
                   pl.BlockSpec((LANES, s_n), lambda j: (j, 0))],
        out_shape=[jax.ShapeDtypeStruct((N_RES, lr, D_MODEL), ACT_DTYPE),
                   jax.ShapeDtypeStruct((D_MODEL, s_n), ACT_DTYPE)],
        name="conv_fwd",
    )(pv, pv, pv, pv, conv_w)
    return hc.reshape(s_n, D_MODEL), hct


RES_PER_STEP = 8
FWD_BATCH = {1: 8, 4: 8, 16: RES_PER_STEP}
BWD_BATCH = {1: 8, 4: 8, 16: RES_PER_STEP}

_BNT = (((2,), (2,)), ((0,), (0,)))
_BNN = (((2,), (1,)), ((0,), (0,)))


def _bdot(a, b, dims):
    return lax.dot_general(a.astype(MXU_DTYPE), b.astype(MXU_DTYPE), dims, preferred_element_type=F32)


def _pattern_view_shape(s_n, c_n, g_n, lead=()):
    lr = s_n // N_RES
    return (*lead, 4, 4, lr, c_n) if g_n == 4 else (*lead, N_RES, lr, c_n)


def _pattern_view(a, g_n, lead=()):
    return a.reshape(_pattern_view_shape(a.shape[-2], a.shape[-1], g_n, lead))


def _pattern_grid(g_n):
    return (N_RES // RES_PER_STEP if g_n == 1 else N_RES // g_n, HP)


def _pattern_spec(g_n, lr, col_of_hp, lead=()):
    z = (0,) * len(lead)
    if g_n == 16:
        return pl.BlockSpec((*lead, 16, lr, LANES), lambda r, hp: (*z, 0, 0, col_of_hp(hp)))
    if g_n == 4:
        return pl.BlockSpec((*lead, 4, None, lr, LANES), lambda r, hp: (*z, 0, r, 0, col_of_hp(hp)))
    return pl.BlockSpec((*lead, RES_PER_STEP, lr, LANES), lambda r, hp: (*z, r, 0, col_of_hp(hp)))


def _aligned(start, m):
    return start if isinstance(start, int) else pl.multiple_of(start, m)


class _Units:
    def __init__(self, g_n, rq):
        self.g_n, self.rq = g_n, rq
        self.per_res, self.paired = g_n == 1, rq == 8

    def plan(self, lr, size):
        if self.per_res:
            return [0], lr // self.rq - 1, lambda j: [pl.multiple_of(j * self.rq, self.rq)]
        step = 16 if self.paired else self.rq
        per = min(size // 2 if self.paired else size, lr // step)
        assert (lr // step) % per == 0
        return ([i * step for i in range(per)], lr // step // per - 1,
                lambda j: [pl.multiple_of((j * per + i) * step, step) for i in range(per)])

    def count(self, qs):
        return RES_PER_STEP if self.per_res else len(qs) * (2 if self.paired else 1)

    def _split(self, tiles, lo, rows):
        return tiles[:, lo:lo + rows].reshape(self.g_n * rows, LANES)

    def load_q(self, ref, qs):
        rq = self.rq
        if self.per_res:
            return ref[:, pl.ds(qs[0], rq), :]
        if self.paired:
            tiles = [ref[:, pl.ds(q, 16), :].astype(F32) for q in qs]
            return jnp.stack([self._split(t, lo, 8) for t in tiles for lo in (0, 8)])
        return jnp.stack([ref[:, pl.ds(q, rq), :].reshape(self.g_n * rq, LANES) for q in qs])

    def _key_rows(self, q, at_start):
        return (0, 2 * self.rq) if at_start else (_aligned(q - self.rq, self.rq), 2 * self.rq)

    def load_k(self, ref, qs, first):
        rq = self.rq
        if self.per_res:
            return ref[:, pl.ds(0, rq), :] if first else ref[:, pl.ds(_aligned(qs[0] - rq, rq), 2 * rq), :]
        if self.paired:
            out = []
            for i, q in enumerate(qs):
                if first and i == 0:
                    t = ref[:, 0:16, :].astype(F32)
                    out += [self._split(t, 0, 16)] * 2
                else:
                    t = ref[:, pl.ds(_aligned(q - 16, 16), 32), :].astype(F32)
                    out += [self._split(t, 8, 16), self._split(t, 16, 16)]
            return jnp.stack(out)
        rows = [self._key_rows(q, first and i == 0) for i, q in enumerate(qs)]
        return jnp.stack([ref[:, pl.ds(k0, n), :].reshape(self.g_n * n, LANES) for k0, n in rows])

    def store_q(self, ref, qs, val, add=False, lead=()):
        if self.per_res:
            pieces = [(qs[0], self.rq, val)]
        elif self.paired:
            pieces = [(q, 16, jnp.concatenate([val[2 * i].reshape(self.g_n, 8, LANES),
                                               val[2 * i + 1].reshape(self.g_n, 8, LANES)], axis=1))
                      for i, q in enumerate(qs)]
        else:
            pieces = [(q, self.rq, val[i].reshape(self.g_n, self.rq, LANES)) for i, q in enumerate(qs)]
        for start, rows, v in pieces:
            idx = (*lead, slice(None), pl.ds(start, rows), slice(None))
            ref[idx] = (ref[idx] + v if add else v).astype(ref.dtype)

    def add_k(self, ref, qs, val, first):
        rq = self.rq
        if self.per_res:
            k0, n = (0, rq) if first else (_aligned(qs[0] - rq, rq), 2 * rq)
            ref[:, pl.ds(k0, n), :] += val
            return
        if self.paired:
            starts = [s for i, q in enumerate(qs)
                      for s in ((0, 0) if first and i == 0 else (_aligned(q - 8, 8), q))]
            rows = [(s, 16) for s in starts]
        else:
            rows = [self._key_rows(q, first and i == 0) for i, q in enumerate(qs)]
        for b, (k0, n) in enumerate(rows):
            ref[:, pl.ds(k0, n), :] += val[b].reshape(self.g_n, n, LANES)


def _batch_bias(un, qs, at_start, first_ref, general_ref):
    if not at_start:
        return general_ref[...][None]
    if un.per_res:
        return first_ref[...][None]
    return jnp.concatenate([first_ref[...][None]] + [general_ref[...][None]] * (un.count(qs) - 1), axis=0)


def _stack_heads(x, low):
    zero = jnp.zeros_like(x)
    return jnp.concatenate([jnp.where(low, x, zero), jnp.where(low, zero, x)], axis=1)


def _attn_fwd(proj, slopes, d):
    g_n, rq = PATTERNS[d]
    un = _Units(g_n, rq)
    s_n = proj.shape[0]
    lr = s_n // N_RES
    nb = lr // rq
    q_n = g_n * rq
    d0, m0, d1, m1 = _attn_tables(d)
    first, n_more, later = un.plan(lr, FWD_BATCH[d])

    def body(sl_ref, q_ref, k_ref, v_ref, d0_ref, m0_ref, d1_ref, m1_ref, o_ref, lse_ref, b0_ref, b1_ref):
        hp = pl.program_id(1)

        @pl.when(hp == 0)
        def _():
            lse_ref[...] = jnp.zeros(lse_ref.shape, F32)

        for h in (0, 1):
            slope = sl_ref[2 * hp + h]
            b0_ref[h * q_n:(h + 1) * q_n, :] = m0_ref[...] - slope * d0_ref[...]
            b1_ref[h * q_n:(h + 1) * q_n, :] = m1_ref[...] - slope * d1_ref[...]

        lane = lax.broadcasted_iota(jnp.int32, (1, q_n, LANES), 2)
        low = lane < HEAD_DIM
        grp = lane // 8

        def batch(qs, at_start):
            qq = _stack_heads(un.load_q(q_ref, qs) * 0.125, low)
            s = _bdot(qq, un.load_k(k_ref, qs, at_start), _BNT) + _batch_bias(un, qs, at_start, b0_ref, b1_ref)
            m = jnp.max(s, axis=2, keepdims=True)
            p = jnp.exp(s - m)
            l = jnp.sum(p, axis=2, keepdims=True)
            o = _bdot(p, un.load_k(v_ref, qs, at_start), _BNN) * (1.0 / l)
            lse = m + jnp.log(l)
            un.store_q(o_ref, qs, jnp.where(low, o[:, :q_n], o[:, q_n:]))
            upd = jnp.where(grp == 2 * hp, lse[:, :q_n], 0.0) + jnp.where(grp == 2 * hp + 1, lse[:, q_n:], 0.0)
            un.store_q(lse_ref, qs, upd, add=True)

        batch(first, True)

        def more(j, carry):
            batch(later(j), False)
            return carry

        lax.fori_loop(1, 1 + n_more, more, 0)

    pv = _pattern_view(proj, g_n)
    full = lambda a: pl.BlockSpec(a.shape, lambda r, hp: (0, 0))
    o, lse = pl.pallas_call(
        body, grid=_pattern_grid(g_n),
        in_specs=[pl.BlockSpec(memory_space=pltpu.SMEM),
                  _pattern_spec(g_n, lr, lambda hp: 32 + hp),
                  _pattern_spec(g_n, lr, lambda hp: 40 + hp),
                  _pattern_spec(g_n, lr, lambda hp: 48 + hp),
                  full(d0), full(m0), full(d1), full(m1)],
        out_specs=[_pattern_spec(g_n, lr, lambda hp: hp), _pattern_spec(g_n, lr, lambda hp: 0)],
        out_shape=[jax.ShapeDtypeStruct(_pattern_view_shape(s_n, D_MODEL, g_n), ACT_DTYPE),
                   jax.ShapeDtypeStruct(_pattern_view_shape(s_n, LANES, g_n), F32)],
        scratch_shapes=[pltpu.VMEM((2 * q_n, d0.shape[1]), F32), pltpu.VMEM((2 * q_n, 2 * q_n), F32)],
        name=f"attn_fwd_d{d}",
    )(slopes, pv, pv, pv, d0, m0, d1, m1)
    return o.reshape(s_n, D_MODEL), lse.reshape(s_n, LANES)


def _attn_combine(outs, lses, proj):
    s_n = proj.shape[0]
    tm = 512

    def body(o1_ref, o2_ref, o3_ref, l1_ref, l2_ref, l3_ref, za_ref, e_ref, o_ref, lse_ref, ha_ref, hat_ref):
        ls = [l1_ref[...], l2_ref[...], l3_ref[...]]
        mx = jnp.maximum(jnp.maximum(ls[0], ls[1]), ls[2])
        den = sum(jnp.exp(l - mx) for l in ls)
        lse = mx + jnp.log(den)
        lse_ref[...] = lse
        o = jnp.zeros((tm, D_MODEL), F32)
        for l, oref in zip(ls, (o1_ref, o2_ref, o3_ref)):
            o = o + _select_cols(jnp.exp(l - lse), e_ref[...], terms=2) * oref[...].astype(F32)
        o_ref[...] = o
        z = za_ref[...].astype(F32)
        ha = z * _sigmoid(z) * o
        ha_ref[...] = ha.astype(ha_ref.dtype)
        hat_ref[...] = ha.T.astype(hat_ref.dtype)

    row = lambda w: pl.BlockSpec((tm, w), lambda i: (i, 0))
    return pl.pallas_call(
        body, grid=(s_n // tm,),
        in_specs=[row(D_MODEL)] * 3 + [row(LANES)] * 3
        + [pl.BlockSpec((tm, D_MODEL), lambda i: (i, 7)), pl.BlockSpec((LANES, D_MODEL), lambda i: (0, 0))],
        out_specs=[row(D_MODEL), row(LANES), row(D_MODEL), pl.BlockSpec((D_MODEL, tm), lambda i: (0, i))],
        out_shape=[jax.ShapeDtypeStruct((s_n, D_MODEL), F32), jax.ShapeDtypeStruct((s_n, LANES), F32),
                   jax.ShapeDtypeStruct((s_n, D_MODEL), ACT_DTYPE), jax.ShapeDtypeStruct((D_MODEL, s_n), ACT_DTYPE)],
        name="attn_combine",
    )(*outs, *lses, proj, _head_expand_matrix())


def _gates(gc_ref, ga_ref, b_ref):
    b = b_ref[...]
    gc = _sigmoid(gc_ref[...].astype(F32) + b[:, :D_MODEL])
    ga = _sigmoid(ga_ref[...].astype(F32) + b[:, D_MODEL:])
    return gc, ga


def _merge_loss(hc, ha, woc, woa, wo, proj, b_merge, xp, final_g, tgt):
    s_n = xp.shape[0]
    tm = 512

    def body(hc_ref, ha_ref, woc_ref, woa_ref, wo_ref, gc_ref, ga_ref, b_ref, x_ref, gf_ref, t_ref,
             yc_ref, ya_ref, mg_ref, mgt_ref, dh_ref, dhb_ref, dgf_ref, loss_ref):
        i = pl.program_id(0)

        @pl.when(i == 0)
        def _():
            dgf_ref[...] = jnp.zeros(dgf_ref.shape, F32)
            loss_ref[...] = jnp.zeros(loss_ref.shape, F32)

        yc = _dot(hc_ref[...], woc_ref[...])
        ya = _dot(ha_ref[...], woa_ref[...])
        gc, ga = _gates(gc_ref, ga_ref, b_ref)
        mg = gc * yc + ga * ya
        yc_ref[...] = yc.astype(yc_ref.dtype)
        ya_ref[...] = ya.astype(ya_ref.dtype)
        mg_ref[...] = mg.astype(mg_ref.dtype)
        mgt_ref[...] = mg.T.astype(mgt_ref.dtype)
        h2 = x_ref[...] + _dot(mg, wo_ref[...])
        r2 = lax.rsqrt(jnp.mean(h2 * h2, axis=-1, keepdims=True) + EPS)
        nrm = h2 * r2
        gf = gf_ref[...]
        err = nrm * gf - t_ref[...]
        e2 = (err * err).reshape(tm // 8, 8, D_MODEL).sum(axis=0)
        loss_ref[...] += sum(e2[:, c * LANES:(c + 1) * LANES] for c in range(D_MODEL // LANES))
        dy = err * (1.0 / D_MODEL)
        dgf_ref[...] += jnp.sum(dy * nrm, axis=0, keepdims=True)
        dn = dy * gf
        dh2 = r2 * (dn - nrm * jnp.mean(dn * nrm, axis=-1, keepdims=True))
        dh_ref[...] = dh2
        dhb_ref[...] = dh2.astype(dhb_ref.dtype)

    row = pl.BlockSpec((tm, D_MODEL), lambda i: (i, 0))
    wsp = pl.BlockSpec((D_MODEL, D_MODEL), lambda i: (0, 0))
    vec = lambda w: pl.BlockSpec((1, w), lambda i: (0, 0))
    act = jax.ShapeDtypeStruct((s_n, D_MODEL), ACT_DTYPE)
    return pl.pallas_call(
        body, grid=(s_n // tm,),
        in_specs=[row, row, wsp, wsp, wsp,
                  pl.BlockSpec((tm, D_MODEL), lambda i: (i, 8)), pl.BlockSpec((tm, D_MODEL), lambda i: (i, 9)),
                  vec(2 * D_MODEL), row, vec(D_MODEL), row],
        out_specs=[row, row, row, pl.BlockSpec((D_MODEL, tm), lambda i: (0, i)), row, row,
                   vec(D_MODEL), pl.BlockSpec((8, LANES), lambda i: (0, 0))],
        out_shape=[act, act, act, jax.ShapeDtypeStruct((D_MODEL, s_n), ACT_DTYPE),
                   jax.ShapeDtypeStruct((s_n, D_MODEL), F32), act,
                   jax.ShapeDtypeStruct((1, D_MODEL), F32), jax.ShapeDtypeStruct((8, LANES), F32)],
        name="merge_loss",
    )(hc, ha, woc, woa, wo, proj, proj, b_merge, xp, final_g, tgt)


def _merge_bwd(dh2b, wo, woc, woa, yc, ya, proj, b_merge, o):
    s_n = dh2b.shape[0]
    tm = 512

    def body(dh_ref, wo_ref, woc_ref, woa_ref, yc_ref, ya_ref, gc_ref, ga_ref, b_ref, o_ref, za_ref, e_ref,
             dyc_ref, dya_ref, dhc_ref, do_ref, dsum_ref, db3_ref, dbias_ref):
        i = pl.program_id(0)

        @pl.when(i == 0)
        def _():
            dbias_ref[...] = jnp.zeros(dbias_ref.shape, F32)

        dmg = _dot_nt(dh_ref[...], wo_ref[...])
        gc, ga = _gates(gc_ref, ga_ref, b_ref)
        dgc = dmg * yc_ref[...].astype(F32) * gc * (1.0 - gc)
        dga = dmg * ya_ref[...].astype(F32) * ga * (1.0 - ga)
        dbias_ref[:, :D_MODEL] += jnp.sum(dgc, axis=0, keepdims=True)
        dbias_ref[:, D_MODEL:] += jnp.sum(dga, axis=0, keepdims=True)
        dyc = dmg * gc
        dya = dmg * ga
        dyc_ref[...] = dyc.astype(dyc_ref.dtype)
        dya_ref[...] = dya.astype(dya_ref.dtype)
        dhc_ref[...] = _dot_nt(dyc, woc_ref[...]).astype(dhc_ref.dtype)
        dha = _dot_nt(dya, woa_ref[...])
        z = za_ref[...].astype(F32)
        sg = _sigmoid(z)
        ov = o_ref[...]
        dout = dha * z * sg
        do_ref[...] = dout.astype(do_ref.dtype)
        dsum_ref[...] = _select_cols(dout * ov, e_ref[...], terms=2)
        db3_ref[0] = (dha * ov * sg * (1.0 + z * (1.0 - sg))).astype(db3_ref.dtype)
        db3_ref[1] = dgc.astype(db3_ref.dtype)
        db3_ref[2] = dga.astype(db3_ref.dtype)

    row = pl.BlockSpec((tm, D_MODEL), lambda i: (i, 0))
    wsp = pl.BlockSpec((D_MODEL, D_MODEL), lambda i: (0, 0))
    act = jax.ShapeDtypeStruct((s_n, D_MODEL), ACT_DTYPE)
    return pl.pallas_call(
        body, grid=(s_n // tm,),
        in_specs=[row, wsp, wsp, wsp, row, row,
                  pl.BlockSpec((tm, D_MODEL), lambda i: (i, 8)), pl.BlockSpec((tm, D_MODEL), lambda i: (i, 9)),
                  pl.BlockSpec((1, 2 * D_MODEL), lambda i: (0, 0)), row,
                  pl.BlockSpec((tm, D_MODEL), lambda i: (i, 7)), pl.BlockSpec((D_MODEL, LANES), lambda i: (0, 0))],
        out_specs=[row, row, row, row, pl.BlockSpec((tm, LANES), lambda i: (i, 0)),
                   pl.BlockSpec((3, tm, D_MODEL), lambda i: (0, i, 0)),
                   pl.BlockSpec((1, 2 * D_MODEL), lambda i: (0, 0))],
        out_shape=[act, act, act, act, jax.ShapeDtypeStruct((s_n, LANES), F32),
                   jax.ShapeDtypeStruct((3, s_n, D_MODEL), ACT_DTYPE),
                   jax.ShapeDtypeStruct((1, 2 * D_MODEL), F32)],
        name="merge_bwd",
    )(dh2b, wo, woc, woa, yc, ya, proj, proj, b_merge, o, proj, _head_sum_matrix())


def _mm_lhs_resident(a, b, tn, name):
    m_n, k_n = a.shape
    n_n = b.shape[1]

    def body(a_ref, b_ref, o_ref):
        o_ref[...] = _dot(a_ref[...], b_ref[...])

    return pl.pallas_call(
        body, grid=(n_n // tn,),
        in_specs=[pl.BlockSpec((m_n, k_n), lambda n: (0, 0)), pl.BlockSpec((k_n, tn), lambda n: (0, n))],
        out_specs=pl.BlockSpec((m_n, tn), lambda n: (0, n)),
        out_shape=jax.ShapeDtypeStruct((m_n, n_n), F32),
        name=name,
    )(a, b)


def _conv_bwd(proj, conv_w, dhc):
    s_n = proj.shape[0]
    lr = s_n // N_RES
    pv = proj.reshape(N_RES, lr, IN_COLS)

    def body(xc_ref, bg_ref, cg_ref, zc_ref, w_ref, dhc_ref, da4_ref, dw_ref, dc_ref):
        w = w_ref[...]
        row = lax.broadcasted_iota(jnp.int32, (lr, LANES), 0)
        dw = [jnp.zeros((1, LANES), F32) for _ in range(3)]
        products = {}
        for r in range(N_RES):
            a, am1, am2 = _conv_terms(xc_ref, cg_ref, r, row, lr, products)
            c = w[0:1] * am2 + w[1:2] * am1 + w[2:3] * a
            z = zc_ref[r].astype(F32)
            sg = _sigmoid(z)
            sz = z * sg
            bg = bg_ref[r].astype(F32)
            dh = dhc_ref[r].astype(F32)
            da4_ref[1, r] = (dh * sz * c).astype(da4_ref.dtype)
            da4_ref[3, r] = (dh * bg * c * sg * (1.0 + z * (1.0 - sg))).astype(da4_ref.dtype)
            dc = dh * sz * bg
            dc_ref[r] = dc
            dw[0] = dw[0] + jnp.sum(dc * am2, axis=0, keepdims=True)
            dw[1] = dw[1] + jnp.sum(dc * am1, axis=0, keepdims=True)
            dw[2] = dw[2] + jnp.sum(dc * a, axis=0, keepdims=True)
        dw_ref[0:1, :] = dw[0]
        dw_ref[1:2, :] = dw[1]
        dw_ref[2:3, :] = dw[2]

        def shift_up(v):
            return jnp.where(row < lr - 1, pltpu.roll(v, lr - 1, 0), 0.0)

        for r in range(N_RES):
            dp1 = dc_ref[r + 1] if r + 1 < N_RES else shift_up(dc_ref[0])
            dp2 = dc_ref[r + 2] if r + 2 < N_RES else shift_up(dc_ref[r + 2 - N_RES])
            da = w[2:3] * dc_ref[r] + w[1:2] * dp1 + w[0:1] * dp2
            da4_ref[0, r] = (da * cg_ref[r].astype(F32)).astype(da4_ref.dtype)
            da4_ref[2, r] = (da * xc_ref[r].astype(F32)).astype(da4_ref.dtype)

    def col(part):
        return pl.BlockSpec((N_RES, lr, LANES), lambda j: (0, 0, part * 8 + j))

    da4, dw = pl.pallas_call(
        body, grid=(D_MODEL // LANES,),
        in_specs=[col(0), col(1), col(2), col(3), pl.BlockSpec((3, LANES), lambda j: (0, j)),
                  pl.BlockSpec((N_RES, lr, LANES), lambda j: (0, 0, j))],
        out_specs=[pl.BlockSpec((4, N_RES, lr, LANES), lambda j: (0, 0, 0, j)),
                   pl.BlockSpec((3, LANES), lambda j: (0, j))],
        out_shape=[jax.ShapeDtypeStruct((4, N_RES, lr, D_MODEL), ACT_DTYPE),
                   jax.ShapeDtypeStruct((3, D_MODEL), F32)],
        scratch_shapes=[pltpu.VMEM((N_RES, lr, LANES), F32)],
        name="conv_bwd",
    )(pv, pv, pv, pv, conv_w, dhc.reshape(N_RES, lr, D_MODEL))
    return da4.reshape(4, s_n, D_MODEL), dw


def _attn_bwd(proj, dout, lse, dsum, slopes, d, prev=None):
    g_n, rq = PATTERNS[d]
    un = _Units(g_n, rq)
    s_n = proj.shape[0]
    lr = s_n // N_RES
    nb = lr // rq
    q_n = g_n * rq
    d0, m0, d1, m1 = (np.ascontiguousarray(t.T) for t in _attn_tables(d))
    first, n_more, later = un.plan(lr, BWD_BATCH[d])
    bsz = un.count(first)
    gd = RES_PER_STEP if un.per_res else g_n

    def body(sl_ref, q_ref, k_ref, v_ref, do_ref, lse_ref, ds_ref, d0_ref, m0_ref, d1_ref, m1_ref, *rest):
        prev_ref = rest[0] if prev is not None else None
        out_ref, b0_ref, b1_ref, lt_ref, dt_ref, dk_ref, dv_ref = rest[-7:]
        hp = pl.program_id(1)
        for h in (0, 1):
            slope = sl_ref[2 * hp + h]
            b0_ref[:, h * q_n:(h + 1) * q_n] = m0_ref[...] - slope * d0_ref[...]
            b1_ref[:, h * q_n:(h + 1) * q_n] = m1_ref[...] - slope * d1_ref[...]
        if prev is None:
            dk_ref[...] = jnp.zeros(dk_ref.shape, F32)
            dv_ref[...] = jnp.zeros(dv_ref.shape, F32)
        else:
            out_ref[0] = prev_ref[0]
            dk_ref[...] = prev_ref[1].astype(F32)
            dv_ref[...] = prev_ref[2].astype(F32)
        low = lax.broadcasted_iota(jnp.int32, (1, q_n, LANES), 2) < HEAD_DIM
        row16 = pl.multiple_of(16 * hp, 16)

        def query_rows(stat_ref, t_ref, qs):
            tiles = un.load_q(stat_ref, qs)
            for b in range(bsz):
                t_ref[b] = tiles[b].T
            t16 = t_ref[:, pl.ds(row16, 16), :]
            return jnp.concatenate([t16[:, 0:1, :], t16[:, 8:9, :]], axis=2)

        def batch(qs, at_start):
            qq = _stack_heads(un.load_q(q_ref, qs) * 0.125, low)
            dd = _stack_heads(un.load_q(do_ref, qs), low)
            ks = un.load_k(k_ref, qs, at_start)
            vs = un.load_k(v_ref, qs, at_start)
            lrow = query_rows(lse_ref, lt_ref, qs)
            drow = query_rows(ds_ref, dt_ref, qs)
            pt = jnp.exp(_bdot(ks, qq, _BNT) + _batch_bias(un, qs, at_start, b0_ref, b1_ref) - lrow)
            dst = pt * (_bdot(vs, dd, _BNT) - drow)
            un.add_k(dv_ref, qs, _bdot(pt, dd, _BNN), at_start)
            un.add_k(dk_ref, qs, _bdot(dst, qq, _BNN), at_start)
            dq = _bdot(jnp.swapaxes(dst, 1, 2), ks, _BNN)
            un.store_q(out_ref, qs, jnp.where(low, dq[:, :q_n], dq[:, q_n:]) * 0.125, add=prev is not None,
                       lead=(0,))

        batch(first, True)

        def more(j, carry):
            batch(later(j), False)
            return carry

        lax.fori_loop(1, 1 + n_more, more, 0)
        out_ref[1] = dk_ref[...].astype(out_ref.dtype)
        out_ref[2] = dv_ref[...].astype(out_ref.dtype)

    pv = _pattern_view(proj, g_n)
    full = lambda a: pl.BlockSpec(a.shape, lambda r, hp: (0, 0))
    whole = _pattern_spec(g_n, lr, lambda hp: hp, lead=(3,))
    out = pl.pallas_call(
        body, grid=_pattern_grid(g_n),
        in_specs=[pl.BlockSpec(memory_space=pltpu.SMEM),
                  _pattern_spec(g_n, lr, lambda hp: 32 + hp),
                  _pattern_spec(g_n, lr, lambda hp: 40 + hp),
                  _pattern_spec(g_n, lr, lambda hp: 48 + hp),
                  _pattern_spec(g_n, lr, lambda hp: hp),
                  _pattern_spec(g_n, lr, lambda hp: 0),
                  _pattern_spec(g_n, lr, lambda hp: 0),
                  full(d0), full(m0), full(d1), full(m1)] + ([] if prev is None else [whole]),
        out_specs=whole,
        out_shape=jax.ShapeDtypeStruct(_pattern_view_shape(s_n, D_MODEL, g_n, lead=(3,)), ACT_DTYPE),
        scratch_shapes=[pltpu.VMEM((d0.shape[0], 2 * q_n), F32), pltpu.VMEM((2 * q_n, 2 * q_n), F32),
                        pltpu.VMEM((bsz, LANES, q_n), F32), pltpu.VMEM((bsz, LANES, q_n), F32),
                        pltpu.VMEM((gd, lr, LANES), F32), pltpu.VMEM((gd, lr, LANES), F32)],
        name=f"attn_bwd_d{d}",
    )(slopes, pv, pv, pv, _pattern_view(dout, g_n), _pattern_view(lse, g_n), _pattern_view(dsum, g_n),
      d0, m0, d1, m1, *([] if prev is None else [_pattern_view(prev, g_n, lead=(3,))]))
    return out.reshape(3, s_n, D_MODEL)


def _part_index(step, per, lo, n):
    return jnp.clip(step // per - lo, 0, n - 1)


def _dw_in(ut, da4, dc3, db3):
    s_n = ut.shape[1]
    tn = 512
    per = D_MODEL // tn
    shard_blocks = SHARD_COLS // tn

    def body(a_ref, p0_ref, p1_ref, p2_ref, o_ref):
        part = pl.program_id(0) // per

        @pl.when(part < 4)
        def _():
            o_ref[...] = _dot(a_ref[...], p0_ref[...])

        @pl.when((part >= 4) & (part < 7))
        def _():
            o_ref[...] = _dot(a_ref[...], p1_ref[...])

        @pl.when(part >= 7)
        def _():
            o_ref[...] = _dot(a_ref[...], p2_ref[...])

    def pspec(lo, n):
        return pl.BlockSpec((None, s_n, tn), lambda j: (_part_index(j, per, lo, n), 0, j % per))

    return pl.pallas_call(
        body, grid=(IN_COLS // tn,),
        in_specs=[pl.BlockSpec((D_MODEL, s_n), lambda j: (0, 0), pipeline_mode=pl.Buffered(1)),
                  pspec(0, 4), pspec(4, 3), pspec(7, 3)],
        out_specs=pl.BlockSpec((None, D_MODEL, tn), lambda j: (j // shard_blocks, 0, j % shard_blocks)),
        out_shape=jax.ShapeDtypeStruct((4, D_MODEL, SHARD_COLS), F32),
        name="dw_in",
    )(ut, da4, dc3, db3)


def _input_grad(da4, dc3, db3, w4, xp, norm_g, dh2, row0, rows):
    tm, tk = 256, 512
    per = D_MODEL // tk
    shard_blocks = SHARD_COLS // tk
    m0 = row0 // tm

    def body(p0_ref, p1_ref, p2_ref, w_ref, x_ref, g_ref, dh_ref, gx_ref, dg_ref):
        @pl.when(pl.program_id(0) == 0)
        def _():
            dg_ref[...] = jnp.zeros(dg_ref.shape, F32)

        du = None
        for k in range(IN_COLS // tk):
            part, cols = k // per, pl.ds((k % per) * tk, tk)
            ref, slot = (p0_ref, part) if part < 4 else (p1_ref, part - 4) if part < 7 else (p2_ref, part - 7)
            d = _dot_nt(ref[slot, :, cols], w_ref[k // shard_blocks, :, pl.ds((k % shard_blocks) * tk, tk)])
            du = d if du is None else du + d
        x = x_ref[...]
        r = lax.rsqrt(jnp.mean(x * x, axis=-1, keepdims=True) + EPS)
        nrm = x * r
        dg_ref[...] += jnp.sum(du * nrm, axis=0, keepdims=True)
        dn = du * g_ref[...]
        gx_ref[...] = dh_ref[...] + r * (dn - nrm * jnp.mean(dn * nrm, axis=-1, keepdims=True))

    def pspec(n):
        return pl.BlockSpec((n, tm, D_MODEL), lambda m: (0, m0 + m, 0))

    row_in = pl.BlockSpec((tm, D_MODEL), lambda m: (m0 + m, 0))
    vec = pl.BlockSpec((1, D_MODEL), lambda m: (0, 0))
    return pl.pallas_call(
        body, grid=(rows // tm,),
        in_specs=[pspec(4), pspec(3), pspec(3),
                  pl.BlockSpec(w4.shape, lambda m: (0, 0, 0), pipeline_mode=pl.Buffered(1)),
                  row_in, vec, row_in],
        out_specs=[pl.BlockSpec((tm, D_MODEL), lambda m: (m, 0)), vec],
        out_shape=[jax.ShapeDtypeStruct((rows, D_MODEL), F32), jax.ShapeDtypeStruct((1, D_MODEL), F32)],
        name="input_grad",
    )(da4, dc3, db3, w4, xp, norm_g, dh2)


class _Step:
    def __init__(self, x, tgt, norm_g, chip, w_own, after=0.0):
        self.norm_g, self.chip = norm_g, chip
        self.slopes = _alibi_slopes()
        self.xp, self.tp = _to_residue_major(x, tgt, after)
        self.u, self.ut = _rms_in(self.xp, norm_g)
        self.proj_own = _in_proj(self.u, chip, w_own=w_own)

    def mixers(self, w4, taps):
        self.w4, self.taps = w4, taps
        self.proj = _in_proj(self.u, self.chip, w4=w4, partial=self.proj_own)
        self.hc, self.hct = _conv_fwd(self.proj, taps)
        fwd = [_attn_fwd(self.proj, self.slopes, d) for d in PATTERNS]
        self.o, self.lse, self.ha, self.hat = _attn_combine([f[0] for f in fwd], [f[1] for f in fwd], self.proj)

    def merge_and_loss(self, woc, woa, wo, b_merge, final_g):
        self.woc, self.woa, self.wo, self.b_merge = woc, woa, wo, b_merge
        (self.yc, self.ya, _, self.mgt, self.dh2, self.dh2b, self.d_final_g, self.loss8) = _merge_loss(
            self.hc, self.ha, woc, woa, wo, self.proj, b_merge, self.xp, final_g, self.tp)

    def out_weight_grads(self):
        (dyc, dya, self.dhc, self.dout, self.dsum, self.db3, self.d_bias) = _merge_bwd(
            self.dh2b, self.wo, self.woc, self.woa, self.yc, self.ya, self.proj, self.b_merge, self.o)
        d_wo = _mm_lhs_resident(self.mgt, self.dh2b, 256, "dw_o")
        d_woc = _mm_lhs_resident(self.hct, dyc, 256, "dw_out_conv")
        d_woa = _mm_lhs_resident(self.hat, dya, 256, "dw_out_attn")
        return d_woc, d_woa, d_wo

    def conv_grads(self, after=0.0):
        self.da4, self.d_taps = _conv_bwd(self.proj, self.taps + after, self.dhc)

    def in_weight_grad(self, after=0.0):
        slopes = self.slopes + after
        self.dc3 = None
        for d in PATTERNS:
            self.dc3 = _attn_bwd(self.proj, self.dout, self.lse, self.dsum, slopes, d, prev=self.dc3)
        return _dw_in(self.ut, self.da4, self.dc3, self.db3)

    def input_grad(self, half, after=0.0):
        rows = self.xp.shape[0] // 2
        return _input_grad(self.da4, self.dc3, self.db3, self.w4, self.xp, self.norm_g + after, self.dh2,
                           half * rows, rows)


def _local_grads(x, tgt, norm_g, w4, b_merge, conv_w, woc, woa, wo, final_g):
    st = _Step(x, tgt, norm_g, jnp.zeros((1,), jnp.int32), w4[0])
    st.mixers(w4, conv_w)
    st.merge_and_loss(woc, woa, wo, b_merge, final_g)
    d_woc, d_woa, d_wo = st.out_weight_grads()
    st.conv_grads()
    d_w4 = st.in_weight_grad()
    gx_lo, dg_lo = st.input_grad(0)
    gx_hi, dg_hi = st.input_grad(1)
    return (st.loss8, _to_natural(gx_lo, gx_hi), dg_lo + dg_hi, d_w4, st.d_bias, st.d_taps, d_woc, d_woa, d_wo,
            st.d_final_g)


MESH = pl.DeviceIdType.MESH
_CHIP_FLIPS = ((1, 0), (0, 1), (1, 1))
_ANY = pl.BlockSpec(memory_space=pl.ANY)


def _place():
    return lax.axis_index("x"), lax.axis_index("y"), lax.axis_index("c")


def _flip(v, f):
    return 1 - v if f else v


def _remote(src, dst, send_sems, recv_sems, k, device):
    return pltpu.make_async_remote_copy(src_ref=src, dst_ref=dst, send_sem=send_sems.at[k], recv_sem=recv_sems.at[k],
                                        device_id=device, device_id_type=MESH)


def _place_shard(w, chip):
    rows, cols = w.shape
    tm = 128

    def body(chip_ref, w_ref, o_ref):
        o_ref[0] = w_ref[...].astype(o_ref.dtype)

    return pl.pallas_call(
        body,
        grid_spec=pltpu.PrefetchScalarGridSpec(
            num_scalar_prefetch=1, grid=(rows // tm,),
            in_specs=[pl.BlockSpec((tm, cols), lambda i, chip_ref: (i, 0))],
            out_specs=pl.BlockSpec((1, tm, cols), lambda i, chip_ref: (chip_ref[0], i, 0))),
        out_shape=jax.ShapeDtypeStruct((4, rows, cols), MXU_DTYPE),
        name="place_shard",
    )(chip, w)


def _gather_copies(arrs, _, send_sems, recv_sems):
    x, y, c = _place()
    out = []
    for a, arr in enumerate(arrs):
        h = arr.shape[1] // 2
        mine = arr.at[2 * x + y, pl.ds(pl.multiple_of(c * h, 8), h)]
        for t, (fx, fy) in enumerate(_CHIP_FLIPS):
            out.append(_remote(mine, mine, send_sems, recv_sems, 3 * a + t, (_flip(x, fx), _flip(y, fy), c)))
    return out


def _forward_to_sibling(arrs):
    n = len(arrs)

    def body(*refs):
        outs = refs[n:2 * n]
        send_sems, recv_sems = refs[2 * n:]
        x, y, c = _place()
        sibling = (x, y, 1 - c)
        started = []
        for a in range(n):
            h = outs[a].shape[1] // 2
            rows = pl.ds(pl.multiple_of(c * h, 8), h)
            for t, (fx, fy) in enumerate(_CHIP_FLIPS):
                landed = outs[a].at[2 * _flip(x, fx) + _flip(y, fy), rows]
                cp = _remote(landed, landed, send_sems, recv_sems, 3 * a + t, sibling)
                cp.start()
                started.append(cp)
        for a in range(n):
            h = outs[a].shape[1] // 2
            rows = pl.ds(pl.multiple_of((1 - c) * h, 8), h)
            for t, (fx, fy) in enumerate(_CHIP_FLIPS):
                handed = outs[a].at[2 * _flip(x, fx) + _flip(y, fy), rows]
                _remote(handed, handed, send_sems, recv_sems, 3 * a + t, sibling).wait_recv()
        for cp in started:
            cp.wait_send()

    return pl.pallas_call(
        body, in_specs=[_ANY] * n, out_specs=[_ANY] * n,
        out_shape=[jax.ShapeDtypeStruct(s.shape, s.dtype) for s in arrs],
        input_output_aliases={a: a for a in range(n)},
        scratch_shapes=[pltpu.SemaphoreType.DMA((3 * n,)), pltpu.SemaphoreType.DMA((3 * n,))],
        name="gathered_to_sibling",
    )(*arrs)


_HBM = pl.BlockSpec(memory_space=pltpu.HBM)
_SEM = pl.BlockSpec(memory_space=pltpu.SEMAPHORE)
_EFFECT = pltpu.SideEffectType.DATAFLOW_SIDE_EFFECTING


class _SplitExchange:
    def __init__(self, name, srcs, land_shapes, n_copies, copies, riders=()):
        self.name, self.n, self.nl, self.copies = name, len(srcs), len(land_shapes), copies
        n, nb = self.n, len(srcs) + len(land_shapes)
        lands = [lax.empty(s.shape, s.dtype) for s in land_shapes]
        bufs = [pltpu.with_memory_space_constraint(a, pltpu.HBM) for a in (*srcs, *lands, *riders)]
        na = len(bufs)

        def body(*refs):
            send_sems, recv_sems = refs[na], refs[na + 1]
            for cp in copies(refs[:n], refs[n:nb], send_sems, recv_sems):
                cp.start()
            refs[-1][...] = jnp.zeros(refs[-1].shape, F32)

        outs = pl.pallas_call(
            body, name=name + "_start",
            in_specs=[_HBM] * na,
            out_specs=[_SEM, _SEM] + [_HBM] * na + [pl.BlockSpec(memory_space=pltpu.VMEM)],
            out_shape=[pltpu.SemaphoreType.DMA((n_copies,)), pltpu.SemaphoreType.DMA((n_copies,))]
            + [pltpu.HBM(b.shape, b.dtype) for b in bufs] + [jax.ShapeDtypeStruct((8, LANES), F32)],
            input_output_aliases={i: 2 + i for i in range(na)},
            compiler_params=pltpu.CompilerParams(has_side_effects=_EFFECT),
        )(*bufs)
        self.sems, self.bufs, self.riders, self.token = outs[:2], outs[2:2 + nb], outs[2 + nb:2 + na], outs[-1]

    def after(self):
        return self.token[0, 0]

    def wait(self, done, riders=()):
        n, nb, copies = self.n, self.n + self.nl, self.copies
        bufs = [*self.bufs, *[pltpu.with_memory_space_constraint(a, pltpu.HBM) for a in riders]]
        na = len(bufs)
        done = list(done) if isinstance(done, (list, tuple)) else [done]

        def body(*refs):
            send_sems, recv_sems = refs[na], refs[na + 1]
            for cp in copies(refs[:n], refs[n:nb], send_sems, recv_sems):
                cp.wait_send()
                cp.wait_recv()

        outs = pl.pallas_call(
            body, name=self.name + "_wait",
            in_specs=[_HBM] * na + [_SEM, _SEM] + [_ANY] * len(done),
            out_specs=[_HBM] * na,
            out_shape=[pltpu.HBM(b.shape, b.dtype) for b in bufs],
            input_output_aliases={i: i for i in range(na)},
            compiler_params=pltpu.CompilerParams(has_side_effects=_EFFECT),
        )(*bufs, *self.sems, *done)
        return outs[:n], outs[n:nb], outs[nb:]


def _sibling_copies(srcs, lands, send_sems, recv_sems):
    x, y, c = _place()
    out = []
    for a, (src, land) in enumerate(zip(srcs, lands)):
        h = src.shape[1] // 2
        theirs = pl.ds(pl.multiple_of((1 - c) * h, 8), h)
        out.append(_remote(src.at[:, theirs], land, send_sems, recv_sems, a, (x, y, 1 - c)))
    return out


def _grads_to_sibling(name, grads):
    shapes = [jax.ShapeDtypeStruct((4, g.shape[1] // 2, g.shape[2]), g.dtype) for g in grads]
    return _SplitExchange(name, grads, shapes, len(grads), _sibling_copies)


def _chip_copies(srcs, lands, send_sems, recv_sems):
    x, y, c = _place()
    out = []
    for a, (src, land) in enumerate(zip(srcs, lands)):
        for t, (fx, fy) in enumerate(_CHIP_FLIPS):
            tx, ty = _flip(x, fx), _flip(y, fy)
            out.append(_remote(src.at[2 * tx + ty], land.at[t], send_sems, recv_sems, 3 * a + t, (tx, ty, c)))
    return out


def _grads_to_chips(name, parts):
    shapes = [jax.ShapeDtypeStruct((3, *p.shape[1:]), p.dtype) for p in parts]
    return _SplitExchange(name, parts, shapes, 3 * len(parts), _chip_copies)


def _add_halves(g, r, half):
    _, rows, cols = g.shape
    h = rows // 2
    tm = min(h, 128)
    nt = h // tm

    def body(half_ref, g_ref, r_ref, b_ref):
        b_ref[...] = (g_ref[...] + r_ref[...]).astype(b_ref.dtype)

    spec = pl.BlockSpec((1, tm, cols), lambda j, i, half_ref: (j, i, 0))
    return pl.pallas_call(
        body,
        grid_spec=pltpu.PrefetchScalarGridSpec(
            num_scalar_prefetch=1, grid=(4, nt),
            in_specs=[pl.BlockSpec((1, tm, cols), lambda j, i, half_ref: (j, half_ref[0] * nt + i, 0)), spec],
            out_specs=spec),
        out_shape=jax.ShapeDtypeStruct((4, h, cols), BF16),
        name="add_sibling_grads",
    )(half, g, r)


def _add_chips(g, r, recv, where):
    _, h, cols = r.shape
    tm = min(h, 128)
    nt = h // tm

    def body(where_ref, g_ref, r_ref, recv_ref, out_ref):
        own = g_ref[0] + r_ref[0]
        out_ref[...] = ((own + recv_ref[0].astype(F32)) + recv_ref[1].astype(F32)) + recv_ref[2].astype(F32)

    return pl.pallas_call(
        body,
        grid_spec=pltpu.PrefetchScalarGridSpec(
            num_scalar_prefetch=1, grid=(nt,),
            in_specs=[pl.BlockSpec((1, tm, cols), lambda i, w: (w[0], w[1] * nt + i, 0)),
                      pl.BlockSpec((1, tm, cols), lambda i, w: (w[0], i, 0)),
                      pl.BlockSpec((3, tm, cols), lambda i, w: (0, i, 0))],
            out_specs=pl.BlockSpec((tm, cols), lambda i, w: (w[1] * nt + i, 0))),
        out_shape=jax.ShapeDtypeStruct((2 * h, cols), F32),
        name="add_chip_grads",
    )(where, g, r, recv)


def _share_halves(shards):
    n = len(shards)

    def body(*refs):
        outs = refs[n:2 * n]
        send_sems, recv_sems = refs[2 * n:]
        x, y, c = _place()
        copies = []
        for a in range(n):
            h = outs[a].shape[0] // 2
            mine = outs[a].at[pl.ds(pl.multiple_of(c * h, 8), h)]
            copies.append(_remote(mine, mine, send_sems, recv_sems, a, (x, y, 1 - c)))
        for cp in copies:
            cp.start()
        for a, cp in enumerate(copies):
            cp.wait_send()
            h = outs[a].shape[0] // 2
            theirs = outs[a].at[pl.ds(pl.multiple_of((1 - c) * h, 8), h)]
            _remote(theirs, theirs, send_sems, recv_sems, a, (x, y, 1 - c)).wait_recv()

    return pl.pallas_call(
        body, in_specs=[_ANY] * n, out_specs=[_ANY] * n,
        out_shape=[jax.ShapeDtypeStruct(p.shape, p.dtype) for p in shards],
        input_output_aliases={a: a for a in range(n)},
        scratch_shapes=[pltpu.SemaphoreType.DMA((n,)), pltpu.SemaphoreType.DMA((n,))],
        name="share_reduced_halves",
    )(*shards)


def _exchange_small(rows, reduce):
    cols = rows[0].shape[1]
    n = len(rows)
    assert sum(r.shape[0] for r in rows) <= 8

    def body(*refs):
        ins, out_ref = refs[:n], refs[n]
        vec_ref, gath_ref, send_sems, recv_sems = refs[n + 1:]
        x, y, c = _place()
        me = 4 * x + 2 * y + c
        vec_ref[...] = jnp.zeros(vec_ref.shape, F32)
        at = 0
        for r in ins:
            vec_ref[at:at + r.shape[0], :] = r[...]
            at += r.shape[0]
        copies = []
        for k in range(1, 8):
            peer = (_flip(x, (k >> 2) & 1), _flip(y, (k >> 1) & 1), _flip(c, k & 1))
            copies.append(_remote(vec_ref, gath_ref.at[me], send_sems, recv_sems, k - 1, peer))
        for cp in copies:
            cp.start()
        gath_ref[me] = vec_ref[...]
        for cp in copies:
            cp.wait()
        if reduce:
            tot = gath_ref[0]
            for dev in range(1, 8):
                tot = tot + gath_ref[dev]
            out_ref[...] = tot
            out_ref[7:8, :] = jnp.zeros((1, cols), F32) + jnp.sum(tot[7:8, :])
        else:
            out_ref[...] = gath_ref[...]

    vm = pl.BlockSpec(memory_space=pltpu.VMEM)
    return pl.pallas_call(
        body, in_specs=[vm] * n, out_specs=vm,
        out_shape=jax.ShapeDtypeStruct((8, cols) if reduce else (8, 8, cols), F32),
        scratch_shapes=[pltpu.VMEM((8, cols), F32), pltpu.VMEM((8, 8, cols), F32),
                        pltpu.SemaphoreType.DMA((7,)), pltpu.SemaphoreType.DMA((7,))],
        name="reduce_small" if reduce else "gather_small",
    )(*rows)


def _adamw(w, g, m, v, name):
    rows, cols = w.shape
    tm = 128 if rows % 128 == 0 else rows

    def body(w_ref, g_ref, m_ref, v_ref, d_ref, m2_ref, v2_ref, gout_ref):
        gr = g_ref[...]
        m2 = ADAM_B1 * m_ref[...] + (1.0 - ADAM_B1) * gr
        v2 = ADAM_B2 * v_ref[...] + (1.0 - ADAM_B2) * (gr * gr)
        m_hat = m2 / (1.0 - ADAM_B1 ** ADAM_STEP)
        v_hat = v2 / (1.0 - ADAM_B2 ** ADAM_STEP)
        d_ref[...] = -ADAM_LR * (m_hat / (jnp.sqrt(v_hat) + ADAM_EPS) + ADAM_WD * w_ref[...])
        m2_ref[...] = m2
        v2_ref[...] = v2
        gout_ref[...] = gr

    spec = pl.BlockSpec((tm, cols), lambda i: (i, 0))
    sds = jax.ShapeDtypeStruct((rows, cols), F32)
    return pl.pallas_call(body, grid=(rows // tm,), in_specs=[spec] * 4, out_specs=[spec] * 4,
                          out_shape=[sds] * 4, name=name)(w, g, m, v)


def kernel(x, norm_g, w_in, b_merge, conv_w, w_out_conv, w_out_attn, w_o, final_g, loss_target, m_norm_g, m_w_in, m_b_merge, m_conv_w, m_w_out_conv, m_w_out_attn, m_w_o, m_final_g, v_norm_g, v_w_in, v_b_merge, v_conv_w, v_w_out_conv, v_w_out_attn, v_w_o, v_final_g):
    mx, my, mc = _place()
    chip = (2 * mx + my).astype(jnp.int32)
    seq = x.shape[1]

    chip1 = chip.reshape(1)
    slots = [_place_shard(w[0], chip1) for w in (w_in, w_out_conv, w_out_attn, w_o)]
    taps8 = _exchange_small([conv_w[0]], reduce=False)
    taps = jnp.concatenate([taps8[2 * j, :3, :] for j in range(4)], axis=1)
    gather_in = _SplitExchange("gather_w_in", slots[:1], [], 3, _gather_copies)
    st = _Step(x[0], loss_target[0], norm_g, chip1, w_in[0], after=gather_in.after())
    (w4,), _, out_slots = gather_in.wait([st.ut, st.proj_own, taps8], riders=slots[1:])
    gather_out = _SplitExchange("gather_w_out", out_slots, [], 9, _gather_copies, riders=[w4])
    (w4,) = _forward_to_sibling(gather_out.riders)
    st.mixers(w4, taps)
    out_ws, _, _ = gather_out.wait(st.o)
    woc, woa, wo = [w.reshape(D_MODEL, D_MODEL) for w in _forward_to_sibling(out_ws)]
    st.merge_and_loss(woc, woa, wo, b_merge, final_g.reshape(1, D_MODEL))

    half = mc.astype(jnp.int32).reshape(1)
    where = jnp.stack([chip, mc.astype(jnp.int32)])
    out_grads = [g.reshape(4, -1, D_MODEL) for g in st.out_weight_grads()]
    to_sibling = _grads_to_sibling("out_grads_to_sibling", out_grads)
    st.conv_grads(after=to_sibling.after())
    out_grads, out_from_sibling, _ = to_sibling.wait(st.da4)
    to_chips = _grads_to_chips("out_grads_to_chips",
                               [_add_halves(g, r, half) for g, r in zip(out_grads, out_from_sibling)])
    d_w4 = st.in_weight_grad(after=to_chips.after())
    out_from_chips = to_chips.wait(st.dc3)[1]

    to_sibling = _grads_to_sibling("in_grad_to_sibling", [d_w4])
    gx_lo, dg_lo = st.input_grad(0, after=to_sibling.after())
    (d_w4,), (from_sibling,), _ = to_sibling.wait(gx_lo)
    to_chips = _grads_to_chips("in_grad_to_chips", [_add_halves(d_w4, from_sibling, half)])
    gx_hi, dg_hi = st.input_grad(1, after=to_chips.after())
    grad_x = _to_natural(gx_lo, gx_hi)

    where_late = where + to_chips.after().astype(jnp.int32)
    out_reduced = [_add_chips(g, r, recv, where_late)
                   for g, r, recv in zip(out_grads, out_from_sibling, out_from_chips)]
    g_woc, g_woa, g_wo = _share_halves(out_reduced)
    small = _exchange_small([dg_lo + dg_hi, st.d_bias.reshape(2, D_MODEL), st.d_taps, st.d_final_g,
                             st.loss8.reshape(1, D_MODEL)], reduce=True)
    loss = (0.5 / D_MODEL) * small[7, 0]
    g_taps = lax.dynamic_slice(small[3:6], (0, chip * (D_MODEL // 4)), (3, D_MODEL // 4))
    upd = {
        "norm_g": _adamw(norm_g, small[0:1], m_norm_g, v_norm_g, "adamw_norm_g"),
        "b_merge": _adamw(b_merge, small[1:3].reshape(1, 2 * D_MODEL), m_b_merge, v_b_merge, "adamw_b_merge"),
        "conv_w": _adamw(conv_w[0], g_taps, m_conv_w[0], v_conv_w[0], "adamw_conv_w"),
        "w_out_conv": _adamw(w_out_conv[0], g_woc, m_w_out_conv[0], v_w_out_conv[0], "adamw_w_out_conv"),
        "w_out_attn": _adamw(w_out_attn[0], g_woa, m_w_out_attn[0], v_w_out_attn[0], "adamw_w_out_attn"),
        "w_o": _adamw(w_o[0], g_wo, m_w_o[0], v_w_o[0], "adamw_w_o"),
        "final_g": _adamw(final_g.reshape(1, D_MODEL), small[6:7], m_final_g.reshape(1, D_MODEL),
                          v_final_g.reshape(1, D_MODEL), "adamw_final_g"),
    }
    behind = [grad_x] + [u[0] for u in upd.values()]
    in_reduced = _add_chips(d_w4, from_sibling, to_chips.wait(behind)[1][0], where)
    (g_w_in,) = _share_halves([in_reduced])
    upd["w_in"] = _adamw(w_in[0], g_w_in, m_w_in[0], v_w_in[0], "adamw_w_in")

    names = ["norm_g", "w_in", "b_merge", "conv_w", "w_out_conv", "w_out_attn", "w_o", "final_g"]
    shapes = [norm_g.shape, w_in.shape, b_merge.shape, conv_w.shape, w_out_conv.shape, w_out_attn.shape,
              w_o.shape, final_g.shape]
    outs = [loss, grad_x.reshape(1, seq, D_MODEL)]
    for k in (3, 0, 1, 2):
        outs += [upd[n][k].reshape(s) for n, s in zip(names, shapes)]
    return tuple(outs)
```

```python
import functools

import numpy as np
import jax
import jax.numpy as jnp
from jax import lax
from jax.experimental import pallas as pl
from jax.experimental.pallas import tpu as pltpu

F32 = jnp.float32
BF16 = jnp.bfloat16
MXU_DTYPE = jnp.bfloat16
ACT_DTYPE = jnp.bfloat16

D_MODEL = 1024
N_HEADS = 16
HEAD_DIM = 64
QB = 128
N_RES = 16
LANES = 128
HP = N_HEADS * HEAD_DIM // LANES
IN_COLS = 10 * D_MODEL
SHARD_COLS = IN_COLS // 4
EPS = 1e-6
NEG = -1e30

ADAM_LR, ADAM_B1, ADAM_B2, ADAM_EPS, ADAM_WD, ADAM_STEP = 0.001, 0.9, 0.999, 1e-08, 0.01, 10

PATTERNS = {1: (16, 8), 4: (4, 32), 16: (1, 128)}

_NN = (((1,), (0,)), ((), ()))
_NT = (((1,), (1,)), ((), ()))


def _dot(a, b):
    return lax.dot_general(a.astype(MXU_DTYPE), b.astype(MXU_DTYPE), _NN, preferred_element_type=F32)


def _dot_nt(a, b):
    return lax.dot_general(a.astype(MXU_DTYPE), b.astype(MXU_DTYPE), _NT, preferred_element_type=F32)


def _split3(x):
    hi = x.astype(BF16)
    r1 = x - hi.astype(F32)
    mid = r1.astype(BF16)
    lo = (r1 - mid.astype(F32)).astype(BF16)
    return hi, mid, lo


def _select_rows(sel, x):
    return sum(lax.dot_general(sel, t, _NN, preferred_element_type=F32) for t in _split3(x))


def _select_cols(x, sel, terms=3):
    return sum(lax.dot_general(t, sel, _NN, preferred_element_type=F32) for t in _split3(x)[:terms])


def _sigmoid(z):
    return 1.0 / (1.0 + jnp.exp(-z))


def _perm_matrix():
    idx = np.arange(256)
    p = np.zeros((256, 256), np.float32)
    p[(idx % 16) * 16 + idx // 16, idx] = 1.0
    return jnp.asarray(p, BF16)


def _head_expand_matrix():
    e = np.zeros((LANES, D_MODEL), np.float32)
    for h in range(N_HEADS):
        e[8 * h, HEAD_DIM * h:HEAD_DIM * (h + 1)] = 1.0
    return jnp.asarray(e, BF16)


def _head_sum_matrix():
    e = np.zeros((D_MODEL, LANES), np.float32)
    for h in range(N_HEADS):
        e[HEAD_DIM * h:HEAD_DIM * (h + 1), 8 * h:8 * (h + 1)] = 1.0
    return jnp.asarray(e, BF16)


def _attn_tables(d):
    g_n, rq = PATTERNS[d]
    q_n = g_n * rq
    gq, iq = np.arange(q_n) // rq, np.arange(q_n) % rq

    def tab(kn, base):
        k_n = g_n * kn
        gk, jk = np.arange(k_n) // kn, np.arange(k_n) % kn
        delta = g_n * (base + iq[:, None] - jk[None, :]) + gq[:, None] - gk[None, :]
        valid = (delta >= 0) & (delta <= QB)
        dist = np.where(valid, d * delta, 0).astype(np.float32)
        madd = np.where(valid, 0.0, NEG).astype(np.float32)
        return dist, madd

    d0, m0 = tab(rq if g_n == 1 else 2 * rq, 0)
    d1, m1 = tab(2 * rq, rq)
    return d0, m0, d1, m1


def _alibi_slopes():
    return jnp.exp2(-8.0 * jnp.arange(1, N_HEADS + 1, dtype=F32) / N_HEADS)


def _to_residue_major(x, tgt, after=0.0):
    s_n, c_n = x.shape
    lr = s_n // N_RES
    pm = (_perm_matrix().astype(F32) + after).astype(BF16)

    def body(p_ref, x_ref, t_ref, xo_ref, to_ref):
        pm = p_ref[...]
        xo_ref[...] = _select_rows(pm, x_ref[...]).reshape(16, 16, c_n)
        to_ref[...] = _select_rows(pm, t_ref[...]).reshape(16, 16, c_n)

    nat = pl.BlockSpec((256, c_n), lambda i: (i, 0))
    res = pl.BlockSpec((16, 16, c_n), lambda i: (0, i, 0))
    xo, to = pl.pallas_call(
        body, grid=(s_n // 256,),
        in_specs=[pl.BlockSpec((256, 256), lambda i: (0, 0)), nat, nat],
        out_specs=[res, res],
        out_shape=[jax.ShapeDtypeStruct((16, lr, c_n), F32)] * 2,
        name="perm_in",
    )(pm, x, tgt)
    return xo.reshape(s_n, c_n), to.reshape(s_n, c_n)


def _to_natural(gx_lo, gx_hi):
    half_rows, c_n = gx_lo.shape
    lr = half_rows // (N_RES // 2)

    def body(p_ref, lo_ref, hi_ref, o_ref):
        g = jnp.concatenate([lo_ref[...], hi_ref[...]], axis=0)
        o_ref[...] = _select_rows(p_ref[...], g.reshape(256, c_n))

    half = pl.BlockSpec((8, 16, c_n), lambda i: (0, i, 0))
    return pl.pallas_call(
        body, grid=(lr // 16,),
        in_specs=[pl.BlockSpec((256, 256), lambda i: (0, 0)), half, half],
        out_specs=pl.BlockSpec((256, c_n), lambda i: (i, 0)),
        out_shape=jax.ShapeDtypeStruct((2 * half_rows, c_n), F32),
        name="perm_out",
    )(_perm_matrix(), gx_lo.reshape(8, lr, c_n), gx_hi.reshape(8, lr, c_n))


def _rms_in(xp, norm_g):
    s_n, c_n = xp.shape
    tm = 512

    def body(x_ref, g_ref, u_ref, ut_ref):
        x = x_ref[...]
        r = lax.rsqrt(jnp.mean(x * x, axis=-1, keepdims=True) + EPS)
        u = x * r * g_ref[...]
        u_ref[...] = u.astype(u_ref.dtype)
        ut_ref[...] = u.T.astype(ut_ref.dtype)

    return pl.pallas_call(
        body, grid=(s_n // tm,),
        in_specs=[pl.BlockSpec((tm, c_n), lambda i: (i, 0)), pl.BlockSpec((1, c_n), lambda i: (0, 0))],
        out_specs=[pl.BlockSpec((tm, c_n), lambda i: (i, 0)), pl.BlockSpec((c_n, tm), lambda i: (0, i))],
        out_shape=[jax.ShapeDtypeStruct((s_n, c_n), ACT_DTYPE), jax.ShapeDtypeStruct((c_n, s_n), ACT_DTYPE)],
        name="rms_in",
    )(xp, norm_g)


def _in_proj(u, chip, w_own=None, w4=None, partial=None, others=()):
    s_n = u.shape[0]
    tn, cm = 512, 512
    per = SHARD_COLS // tn
    own = partial is None

    def body(chip_ref, a_ref, b_ref, *rest):
        o_ref = rest[-1]
        b = b_ref[...]
        for c in range(s_n // cm):
            o_ref[c * cm:(c + 1) * cm, :] = _dot(a_ref[c * cm:(c + 1) * cm, :], b).astype(o_ref.dtype)

    def shard(n, chip_ref):
        if own:
            return chip_ref[0]
        mask = others[-1]
        for i, m in enumerate(others[:-1]):
            mask = jnp.where(n // per == i, m, mask)
        return jnp.bitwise_xor(chip_ref[0], mask)

    w_spec = (pl.BlockSpec((D_MODEL, tn), lambda n, c: (0, n)) if own else
              pl.BlockSpec((None, D_MODEL, tn), lambda n, c: (shard(n, c), 0, n % per)))
    return pl.pallas_call(
        body,
        grid_spec=pltpu.PrefetchScalarGridSpec(
            num_scalar_prefetch=1, grid=(per if own else len(others) * per,),
            in_specs=[pl.BlockSpec((s_n, D_MODEL), lambda n, c: (0, 0)), w_spec] + ([] if own else [_ANY]),
            out_specs=pl.BlockSpec((s_n, tn), lambda n, c: (0, shard(n, c) * per + n % per))),
        out_shape=jax.ShapeDtypeStruct((s_n, IN_COLS), ACT_DTYPE),
        input_output_aliases={} if own else {3: 0},
        name="in_proj_own" if own else "in_proj_" + "_".join(str(m) for m in others),
    )(*([chip, u, w_own] if own else [chip, u, w4, partial]))


def _conv_terms(xc_ref, cg_ref, r, row, lr, cache):
    def a_of(q):
        if q not in cache:
            cache[q] = cg_ref[q].astype(F32) * xc_ref[q].astype(F32)
        return cache[q]

    def shift_down(v):
        return jnp.where(row >= 1, pltpu.roll(v, 1, 0), 0.0)

    a = a_of(r)
    am1 = a_of(r - 1) if r >= 1 else shift_down(a_of(N_RES - 1))
    am2 = a_of(r - 2) if r >= 2 else shift_down(a_of(N_RES - 2 + r))
    return a, am1, am2


def _conv_fwd(proj, conv_w):
    s_n = proj.shape[0]
    lr = s_n // N_RES
    pv = proj.reshape(N_RES, lr, IN_COLS)

    def body(xc_ref, bg_ref, cg_ref, zc_ref, w_ref, hc_ref, hct_ref):
        w = w_ref[...]
        row = lax.broadcasted_iota(jnp.int32, (lr, LANES), 0)
        products = {}
        for r in range(N_RES):
            a, am1, am2 = _conv_terms(xc_ref, cg_ref, r, row, lr, products)
            c = w[0:1] * am2 + w[1:2] * am1 + w[2:3] * a
            z = zc_ref[r].astype(F32)
            hc = z * _sigmoid(z) * bg_ref[r].astype(F32) * c
            hc_ref[r] = hc.astype(hc_ref.dtype)
            hct_ref[:, r * lr:(r + 1) * lr] = hc.T.astype(hct_ref.dtype)

    def col(part):
        return pl.BlockSpec((N_RES, lr, LANES), lambda j: (0, 0, part * 8 + j))

    hc, hct = pl.pallas_call(
        body, grid=(D_MODEL // LANES,),
        in_specs=[col(0), col(1), col(2), col(3), pl.BlockSpec((3, LANES), lambda j: (0, j))],
        out_specs=[pl.BlockSpec((N_RES, lr, LANES), lambda j: (0, 0, j)),
                   pl.BlockSpec((LANES, s_n), lambda j: (j, 0))],
        out_shape=[jax.ShapeDtypeStruct((N_RES, lr, D_MODEL), ACT_DTYPE),
                   jax.ShapeDtypeStruct((D_MODEL, s_n), ACT_DTYPE)],
        name="conv_fwd",
    )(pv, pv, pv, pv, conv_w)
    return hc.reshape(s_n, D_MODEL), hct


RES_PER_STEP = 8
FWD_BATCH = {1: 8, 4: 8, 16: RES_PER_STEP}
BWD_BATCH = {1: 8, 4: 8, 16: RES_PER_STEP}

_BNT = (((2,), (2,)), ((0,), (0,)))
_BNN = (((2,), (1,)), ((0,), (0,)))


def _bdot(a, b, dims):
    return lax.dot_general(a.astype(MXU_DTYPE), b.astype(MXU_DTYPE), dims, preferred_element_type=F32)


def _pattern_view_shape(s_n, c_n, g_n, lead=()):
    lr = s_n // N_RES
    return (*lead, 4, 4, lr, c_n) if g_n == 4 else (*lead, N_RES, lr, c_n)


def _pattern_view(a, g_n, lead=()):
    return a.reshape(_pattern_view_shape(a.shape[-2], a.shape[-1], g_n, lead))


def _pattern_grid(g_n):
    return (N_RES // RES_PER_STEP if g_n == 1 else N_RES // g_n, HP)


def _pattern_spec(g_n, lr, col_of_hp, lead=()):
    z = (0,) * len(lead)
    if g_n == 16:
        return pl.BlockSpec((*lead, 16, lr, LANES), lambda r, hp: (*z, 0, 0, col_of_hp(hp)))
    if g_n == 4:
        return pl.BlockSpec((*lead, 4, None, lr, LANES), lambda r, hp: (*z, 0, r, 0, col_of_hp(hp)))
    return pl.BlockSpec((*lead, RES_PER_STEP, lr, LANES), lambda r, hp: (*z, r, 0, col_of_hp(hp)))


def _aligned(start, m):
    return start if isinstance(start, int) else pl.multiple_of(start, m)


class _Units:
    def __init__(self, g_n, rq):
        self.g_n, self.rq = g_n, rq
        self.per_res, self.paired = g_n == 1, rq == 8

    def plan(self, lr, size):
        if self.per_res:
            return [0], lr // self.rq - 1, lambda j: [pl.multiple_of(j * self.rq, self.rq)]
        step = 16 if self.paired else self.rq
        per = min(size // 2 if self.paired else size, lr // step)
        assert (lr // step) % per == 0
        return ([i * step for i in range(per)], lr // step // per - 1,
                lambda j: [pl.multiple_of((j * per + i) * step, step) for i in range(per)])

    def count(self, qs):
        return RES_PER_STEP if self.per_res else len(qs) * (2 if self.paired else 1)

    def _split(self, tiles, lo, rows):
        return tiles[:, lo:lo + rows].reshape(self.g_n * rows, LANES)

    def load_q(self, ref, qs):
        rq = self.rq
        if self.per_res:
            return ref[:, pl.ds(qs[0], rq), :]
        if self.paired:
            tiles = [ref[:, pl.ds(q, 16), :].astype(F32) for q in qs]
            return jnp.stack([self._split(t, lo, 8) for t in tiles for lo in (0, 8)])
        return jnp.stack([ref[:, pl.ds(q, rq), :].reshape(self.g_n * rq, LANES) for q in qs])

    def _key_rows(self, q, at_start):
        return (0, 2 * self.rq) if at_start else (_aligned(q - self.rq, self.rq), 2 * self.rq)

    def load_k(self, ref, qs, first):
        rq = self.rq
        if self.per_res:
            return ref[:, pl.ds(0, rq), :] if first else ref[:, pl.ds(_aligned(qs[0] - rq, rq), 2 * rq), :]
        if self.paired:
            out = []
            for i, q in enumerate(qs):
                if first and i == 0:
                    t = ref[:, 0:16, :].astype(F32)
                    out += [self._split(t, 0, 16)] * 2
                else:
                    t = ref[:, pl.ds(_aligned(q - 16, 16), 32), :].astype(F32)
                    out += [self._split(t, 8, 16), self._split(t, 16, 16)]
            return jnp.stack(out)
        rows = [self._key_rows(q, first and i == 0) for i, q in enumerate(qs)]
        return jnp.stack([ref[:, pl.ds(k0, n), :].reshape(self.g_n * n, LANES) for k0, n in rows])

    def store_q(self, ref, qs, val, add=False, lead=()):
        if self.per_res:
            pieces = [(qs[0], self.rq, val)]
        elif self.paired:
            pieces = [(q, 16, jnp.concatenate([val[2 * i].reshape(self.g_n, 8, LANES),
                                               val[2 * i + 1].reshape(self.g_n, 8, LANES)], axis=1))
                      for i, q in enumerate(qs)]
        else:
            pieces = [(q, self.rq, val[i].reshape(self.g_n, self.rq, LANES)) for i, q in enumerate(qs)]
        for start, rows, v in pieces:
            idx = (*lead, slice(None), pl.ds(start, rows), slice(None))
            ref[idx] = (ref[idx] + v if add else v).astype(ref.dtype)

    def add_k(self, ref, qs, val, first):
        rq = self.rq
        if self.per_res:
            k0, n = (0, rq) if first else (_aligned(qs[0] - rq, rq), 2 * rq)
            ref[:, pl.ds(k0, n), :] += val
            return
        if self.paired:
            starts = [s for i, q in enumerate(qs)
                      for s in ((0, 0) if first and i == 0 else (_aligned(q - 8, 8), q))]
            rows = [(s, 16) for s in starts]
        else:
            rows = [self._key_rows(q, first and i == 0) for i, q in enumerate(qs)]
        for b, (k0, n) in enumerate(rows):
            ref[:, pl.ds(k0, n), :] += val[b].reshape(self.g_n, n, LANES)


def _batch_bias(un, qs, at_start, first_ref, general_ref):
    if not at_start:
        return general_ref[...][None]
    if un.per_res:
        return first_ref[...][None]
    return jnp.concatenate([first_ref[...][None]] + [general_ref[...][None]] * (un.count(qs) - 1), axis=0)


def _stack_heads(x, low):
    zero = jnp.zeros_like(x)
    return jnp.concatenate([jnp.where(low, x, zero), jnp.where(low, zero, x)], axis=1)


def _attn_fwd(proj, slopes, d):
    g_n, rq = PATTERNS[d]
    un = _Units(g_n, rq)
    s_n = proj.shape[0]
    lr = s_n // N_RES
    nb = lr // rq
    q_n = g_n * rq
    d0, m0, d1, m1 = _attn_tables(d)
    first, n_more, later = un.plan(lr, FWD_BATCH[d])

    def body(sl_ref, q_ref, k_ref, v_ref, d0_ref, m0_ref, d1_ref, m1_ref, o_ref, lse_ref, b0_ref, b1_ref):
        hp = pl.program_id(1)

        @pl.when(hp == 0)
        def _():
            lse_ref[...] = jnp.zeros(lse_ref.shape, F32)

        for h in (0, 1):
            slope = sl_ref[2 * hp + h]
            b0_ref[h * q_n:(h + 1) * q_n, :] = m0_ref[...] - slope * d0_ref[...]
            b1_ref[h * q_n:(h + 1) * q_n, :] = m1_ref[...] - slope * d1_ref[...]

        lane = lax.broadcasted_iota(jnp.int32, (1, q_n, LANES), 2)
        low = lane < HEAD_DIM
        grp = lane // 8

        def batch(qs, at_start):
            qq = _stack_heads(un.load_q(q_ref, qs) * 0.125, low)
            s = _bdot(qq, un.load_k(k_ref, qs, at_start), _BNT) + _batch_bias(un, qs, at_start, b0_ref, b1_ref)
            m = jnp.max(s, axis=2, keepdims=True)
            p = jnp.exp(s - m)
            l = jnp.sum(p, axis=2, keepdims=True)
            o = _bdot(p, un.load_k(v_ref, qs, at_start), _BNN) * (1.0 / l)
            lse = m + jnp.log(l)
            un.store_q(o_ref, qs, jnp.where(low, o[:, :q_n], o[:, q_n:]))
            upd = jnp.where(grp == 2 * hp, lse[:, :q_n], 0.0) + jnp.where(grp == 2 * hp + 1, lse[:, q_n:], 0.0)
            un.store_q(lse_ref, qs, upd, add=True)

        batch(first, True)

        def more(j, carry):
            batch(later(j), False)
            return carry

        lax.fori_loop(1, 1 + n_more, more, 0)

    pv = _pattern_view(proj, g_n)
    full = lambda a: pl.BlockSpec(a.shape, lambda r, hp: (0, 0))
    o, lse = pl.pallas_call(
        body, grid=_pattern_grid(g_n),
        in_specs=[pl.BlockSpec(memory_space=pltpu.SMEM),
                  _pattern_spec(g_n, lr, lambda hp: 32 + hp),
                  _pattern_spec(g_n, lr, lambda hp: 40 + hp),
                  _pattern_spec(g_n, lr, lambda hp: 48 + hp),
                  full(d0), full(m0), full(d1), full(m1)],
        out_specs=[_pattern_spec(g_n, lr, lambda hp: hp), _pattern_spec(g_n, lr, lambda hp: 0)],
        out_shape=[jax.ShapeDtypeStruct(_pattern_view_shape(s_n, D_MODEL, g_n), ACT_DTYPE),
                   jax.ShapeDtypeStruct(_pattern_view_shape(s_n, LANES, g_n), F32)],
        scratch_shapes=[pltpu.VMEM((2 * q_n, d0.shape[1]), F32), pltpu.VMEM((2 * q_n, 2 * q_n), F32)],
        name=f"attn_fwd_d{d}",
    )(slopes, pv, pv, pv, d0, m0, d1, m1)
    return o.reshape(s_n, D_MODEL), lse.reshape(s_n, LANES)


def _attn_combine(outs, lses, proj):
    s_n = proj.shape[0]
    tm = 512

    def body(o1_ref, o2_ref, o3_ref, l1_ref, l2_ref, l3_ref, za_ref, e_ref, o_ref, lse_ref, ha_ref, hat_ref):
        ls = [l1_ref[...], l2_ref[...], l3_ref[...]]
        mx = jnp.maximum(jnp.maximum(ls[0], ls[1]), ls[2])
        den = sum(jnp.exp(l - mx) for l in ls)
        lse = mx + jnp.log(den)
        lse_ref[...] = lse
        o = jnp.zeros((tm, D_MODEL), F32)
        for l, oref in zip(ls, (o1_ref, o2_ref, o3_ref)):
            o = o + _select_cols(jnp.exp(l - lse), e_ref[...], terms=2) * oref[...].astype(F32)
        o_ref[...] = o.astype(o_ref.dtype)
        z = za_ref[...].astype(F32)
        ha = z * _sigmoid(z) * o
        ha_ref[...] = ha.astype(ha_ref.dtype)
        hat_ref[...] = ha.T.astype(hat_ref.dtype)

    row = lambda w: pl.BlockSpec((tm, w), lambda i: (i, 0))
    return pl.pallas_call(
        body, grid=(s_n // tm,),
        in_specs=[row(D_MODEL)] * 3 + [row(LANES)] * 3
        + [pl.BlockSpec((tm, D_MODEL), lambda i: (i, 7)), pl.BlockSpec((LANES, D_MODEL), lambda i: (0, 0))],
        out_specs=[row(D_MODEL), row(LANES), row(D_MODEL), pl.BlockSpec((D_MODEL, tm), lambda i: (0, i))],
        out_shape=[jax.ShapeDtypeStruct((s_n, D_MODEL), ACT_DTYPE), jax.ShapeDtypeStruct((s_n, LANES), F32),
                   jax.ShapeDtypeStruct((s_n, D_MODEL), ACT_DTYPE), jax.ShapeDtypeStruct((D_MODEL, s_n), ACT_DTYPE)],
        name="attn_combine",
    )(*outs, *lses, proj, _head_expand_matrix())


def _gates(gc_ref, ga_ref, b_ref):
    b = b_ref[...]
    gc = _sigmoid(gc_ref[...].astype(F32) + b[:, :D_MODEL])
    ga = _sigmoid(ga_ref[...].astype(F32) + b[:, D_MODEL:])
    return gc, ga


def _merge_loss(hc, ha, woc, woa, wo, proj, b_merge, xp, final_g, tgt):
    s_n = xp.shape[0]
    tm = 512

    def body(hc_ref, ha_ref, woc_ref, woa_ref, wo_ref, gc_ref, ga_ref, b_ref, x_ref, gf_ref, t_ref,
             yc_ref, ya_ref, mgt_ref, dhb_ref, dgf_ref, loss_ref):
        i = pl.program_id(0)

        @pl.when(i == 0)
        def _():
            dgf_ref[...] = jnp.zeros(dgf_ref.shape, F32)
            loss_ref[...] = jnp.zeros(loss_ref.shape, F32)

        yc = _dot(hc_ref[...], woc_ref[...])
        ya = _dot(ha_ref[...], woa_ref[...])
        gc, ga = _gates(gc_ref, ga_ref, b_ref)
        mg = gc * yc + ga * ya
        yc_ref[...] = yc.astype(yc_ref.dtype)
        ya_ref[...] = ya.astype(ya_ref.dtype)
        mgt_ref[...] = mg.T.astype(mgt_ref.dtype)
        h2 = x_ref[...] + _dot(mg, wo_ref[...])
        r2 = lax.rsqrt(jnp.mean(h2 * h2, axis=-1, keepdims=True) + EPS)
        nrm = h2 * r2
        gf = gf_ref[...]
        err = nrm * gf - t_ref[...]
        e2 = (err * err).reshape(tm // 8, 8, D_MODEL).sum(axis=0)
        loss_ref[...] += sum(e2[:, c * LANES:(c + 1) * LANES] for c in range(D_MODEL // LANES))
        dy = err * (1.0 / D_MODEL)
        dgf_ref[...] += jnp.sum(dy * nrm, axis=0, keepdims=True)
        dn = dy * gf
        dh2 = r2 * (dn - nrm * jnp.mean(dn * nrm, axis=-1, keepdims=True))
        dhb_ref[...] = dh2.astype(dhb_ref.dtype)

    row = pl.BlockSpec((tm, D_MODEL), lambda i: (i, 0))
    wsp = pl.BlockSpec((D_MODEL, D_MODEL), lambda i: (0, 0))
    vec = lambda w: pl.BlockSpec((1, w), lambda i: (0, 0))
    act = jax.ShapeDtypeStruct((s_n, D_MODEL), ACT_DTYPE)
    return pl.pallas_call(
        body, grid=(s_n // tm,),
        in_specs=[row, row, wsp, wsp, wsp,
                  pl.BlockSpec((tm, D_MODEL), lambda i: (i, 8)), pl.BlockSpec((tm, D_MODEL), lambda i: (i, 9)),
                  vec(2 * D_MODEL), row, vec(D_MODEL), row],
        out_specs=[row, row, pl.BlockSpec((D_MODEL, tm), lambda i: (0, i)), row,
                   vec(D_MODEL), pl.BlockSpec((8, LANES), lambda i: (0, 0))],
        out_shape=[act, act, jax.ShapeDtypeStruct((D_MODEL, s_n), ACT_DTYPE), act,
                   jax.ShapeDtypeStruct((1, D_MODEL), F32), jax.ShapeDtypeStruct((8, LANES), F32)],
        name="merge_loss",
    )(hc, ha, woc, woa, wo, proj, proj, b_merge, xp, final_g, tgt)


def _merge_bwd(dh2b, wo, woc, woa, yc, ya, proj, b_merge, o):
    s_n = dh2b.shape[0]
    tm = 512

    def body(dh_ref, wo_ref, woc_ref, woa_ref, yc_ref, ya_ref, gc_ref, ga_ref, b_ref, o_ref, za_ref, e_ref,
             dyc_ref, dya_ref, dhc_ref, do_ref, dsum_ref, db3_ref, dbias_ref):
        i = pl.program_id(0)

        @pl.when(i == 0)
        def _():
            dbias_ref[...] = jnp.zeros(dbias_ref.shape, F32)

        dmg = _dot_nt(dh_ref[...], wo_ref[...])
        gc, ga = _gates(gc_ref, ga_ref, b_ref)
        dgc = dmg * yc_ref[...].astype(F32) * gc * (1.0 - gc)
        dga = dmg * ya_ref[...].astype(F32) * ga * (1.0 - ga)
        dbias_ref[:, :D_MODEL] += jnp.sum(dgc, axis=0, keepdims=True)
        dbias_ref[:, D_MODEL:] += jnp.sum(dga, axis=0, keepdims=True)
        dyc = dmg * gc
        dya = dmg * ga
        dyc_ref[...] = dyc.astype(dyc_ref.dtype)
        dya_ref[...] = dya.astype(dya_ref.dtype)
        dhc_ref[...] = _dot_nt(dyc, woc_ref[...]).astype(dhc_ref.dtype)
        dha = _dot_nt(dya, woa_ref[...])
        z = za_ref[...].astype(F32)
        sg = _sigmoid(z)
        ov = o_ref[...].astype(F32)
        dout = dha * z * sg
        do_ref[...] = dout.astype(do_ref.dtype)
        dsum_ref[...] = _select_cols(dout * ov, e_ref[...], terms=2)
        db3_ref[0] = (dha * ov * sg * (1.0 + z * (1.0 - sg))).astype(db3_ref.dtype)
        db3_ref[1] = dgc.astype(db3_ref.dtype)
        db3_ref[2] = dga.astype(db3_ref.dtype)

    row = pl.BlockSpec((tm, D_MODEL), lambda i: (i, 0))
    wsp = pl.BlockSpec((D_MODEL, D_MODEL), lambda i: (0, 0))
    act = jax.ShapeDtypeStruct((s_n, D_MODEL), ACT_DTYPE)
    return pl.pallas_call(
        body, grid=(s_n // tm,),
        in_specs=[row, wsp, wsp, wsp, row, row,
                  pl.BlockSpec((tm, D_MODEL), lambda i: (i, 8)), pl.BlockSpec((tm, D_MODEL), lambda i: (i, 9)),
                  pl.BlockSpec((1, 2 * D_MODEL), lambda i: (0, 0)), row,
                  pl.BlockSpec((tm, D_MODEL), lambda i: (i, 7)), pl.BlockSpec((D_MODEL, LANES), lambda i: (0, 0))],
        out_specs=[row, row, row, row, pl.BlockSpec((tm, LANES), lambda i: (i, 0)),
                   pl.BlockSpec((3, tm, D_MODEL), lambda i: (0, i, 0)),
                   pl.BlockSpec((1, 2 * D_MODEL), lambda i: (0, 0))],
        out_shape=[act, act, act, act, jax.ShapeDtypeStruct((s_n, LANES), F32),
                   jax.ShapeDtypeStruct((3, s_n, D_MODEL), ACT_DTYPE),
                   jax.ShapeDtypeStruct((1, 2 * D_MODEL), F32)],
        name="merge_bwd",
    )(dh2b, wo, woc, woa, yc, ya, proj, proj, b_merge, o, proj, _head_sum_matrix())


def _mm_lhs_resident(a, b, tn, name):
    m_n, k_n = a.shape
    n_n = b.shape[1]

    def body(a_ref, b_ref, o_ref):
        o_ref[...] = _dot(a_ref[...], b_ref[...])

    return pl.pallas_call(
        body, grid=(n_n // tn,),
        in_specs=[pl.BlockSpec((m_n, k_n), lambda n: (0, 0)), pl.BlockSpec((k_n, tn), lambda n: (0, n))],
        out_specs=pl.BlockSpec((m_n, tn), lambda n: (0, n)),
        out_shape=jax.ShapeDtypeStruct((m_n, n_n), F32),
        name=name,
    )(a, b)


def _conv_bwd(proj, conv_w, dhc):
    s_n = proj.shape[0]
    lr = s_n // N_RES
    pv = proj.reshape(N_RES, lr, IN_COLS)

    def body(xc_ref, bg_ref, cg_ref, zc_ref, w_ref, dhc_ref, da4_ref, dw_ref, dc_ref):
        w = w_ref[...]
        row = lax.broadcasted_iota(jnp.int32, (lr, LANES), 0)
        dw = [jnp.zeros((1, LANES), F32) for _ in range(3)]
        products = {}
        for r in range(N_RES):
            a, am1, am2 = _conv_terms(xc_ref, cg_ref, r, row, lr, products)
            c = w[0:1] * am2 + w[1:2] * am1 + w[2:3] * a
            z = zc_ref[r].astype(F32)
            sg = _sigmoid(z)
            sz = z * sg
            bg = bg_ref[r].astype(F32)
            dh = dhc_ref[r].astype(F32)
            da4_ref[1, r] = (dh * sz * c).astype(da4_ref.dtype)
            da4_ref[3, r] = (dh * bg * c * sg * (1.0 + z * (1.0 - sg))).astype(da4_ref.dtype)
            dc = dh * sz * bg
            dc_ref[r] = dc
            dw[0] = dw[0] + jnp.sum(dc * am2, axis=0, keepdims=True)
            dw[1] = dw[1] + jnp.sum(dc * am1, axis=0, keepdims=True)
            dw[2] = dw[2] + jnp.sum(dc * a, axis=0, keepdims=True)
        dw_ref[0:1, :] = dw[0]
        dw_ref[1:2, :] = dw[1]
        dw_ref[2:3, :] = dw[2]

        def shift_up(v):
            return jnp.where(row < lr - 1, pltpu.roll(v, lr - 1, 0), 0.0)

        for r in range(N_RES):
            dp1 = dc_ref[r + 1] if r + 1 < N_RES else shift_up(dc_ref[0])
            dp2 = dc_ref[r + 2] if r + 2 < N_RES else shift_up(dc_ref[r + 2 - N_RES])
            da = w[2:3] * dc_ref[r] + w[1:2] * dp1 + w[0:1] * dp2
            da4_ref[0, r] = (da * cg_ref[r].astype(F32)).astype(da4_ref.dtype)
            da4_ref[2, r] = (da * xc_ref[r].astype(F32)).astype(da4_ref.dtype)

    def col(part):
        return pl.BlockSpec((N_RES, lr, LANES), lambda j: (0, 0, part * 8 + j))

    da4, dw = pl.pallas_call(
        body, grid=(D_MODEL // LANES,),
        in_specs=[col(0), col(1), col(2), col(3), pl.BlockSpec((3, LANES), lambda j: (0, j)),
                  pl.BlockSpec((N_RES, lr, LANES), lambda j: (0, 0, j))],
        out_specs=[pl.BlockSpec((4, N_RES, lr, LANES), lambda j: (0, 0, 0, j)),
                   pl.BlockSpec((3, LANES), lambda j: (0, j))],
        out_shape=[jax.ShapeDtypeStruct((4, N_RES, lr, D_MODEL), ACT_DTYPE),
                   jax.ShapeDtypeStruct((3, D_MODEL), F32)],
        scratch_shapes=[pltpu.VMEM((N_RES, lr, LANES), F32)],
        name="conv_bwd",
    )(pv, pv, pv, pv, conv_w, dhc.reshape(N_RES, lr, D_MODEL))
    return da4.reshape(4, s_n, D_MODEL), dw


def _attn_bwd(proj, dout, lse, dsum, slopes, d, prev=None):
    g_n, rq = PATTERNS[d]
    un = _Units(g_n, rq)
    s_n = proj.shape[0]
    lr = s_n // N_RES
    nb = lr // rq
    q_n = g_n * rq
    d0, m0, d1, m1 = (np.ascontiguousarray(t.T) for t in _attn_tables(d))
    first, n_more, later = un.plan(lr, BWD_BATCH[d])
    bsz = un.count(first)
    gd = RES_PER_STEP if un.per_res else g_n

    def body(sl_ref, q_ref, k_ref, v_ref, do_ref, lse_ref, ds_ref, d0_ref, m0_ref, d1_ref, m1_ref, *rest):
        prev_ref = rest[0] if prev is not None else None
        out_ref, b0_ref, b1_ref, lt_ref, dt_ref, dk_ref, dv_ref = rest[-7:]
        hp = pl.program_id(1)
        for h in (0, 1):
            slope = sl_ref[2 * hp + h]
            b0_ref[:, h * q_n:(h + 1) * q_n] = m0_ref[...] - slope * d0_ref[...]
            b1_ref[:, h * q_n:(h + 1) * q_n] = m1_ref[...] - slope * d1_ref[...]
        if prev is None:
            dk_ref[...] = jnp.zeros(dk_ref.shape, F32)
            dv_ref[...] = jnp.zeros(dv_ref.shape, F32)
        else:
            out_ref[0] = prev_ref[0]
            dk_ref[...] = prev_ref[1].astype(F32)
            dv_ref[...] = prev_ref[2].astype(F32)
        low = lax.broadcasted_iota(jnp.int32, (1, q_n, LANES), 2) < HEAD_DIM
        row16 = pl.multiple_of(16 * hp, 16)

        def query_rows(stat_ref, t_ref, qs):
            tiles = un.load_q(stat_ref, qs)
            for b in range(bsz):
                t_ref[b] = tiles[b].T
            t16 = t_ref[:, pl.ds(row16, 16), :]
            return jnp.concatenate([t16[:, 0:1, :], t16[:, 8:9, :]], axis=2)

        def batch(qs, at_start):
            qq = _stack_heads(un.load_q(q_ref, qs) * 0.125, low)
            dd = _stack_heads(un.load_q(do_ref, qs), low)
            ks = un.load_k(k_ref, qs, at_start)
            vs = un.load_k(v_ref, qs, at_start)
            lrow = query_rows(lse_ref, lt_ref, qs)
            drow = query_rows(ds_ref, dt_ref, qs)
            pt = jnp.exp(_bdot(ks, qq, _BNT) + _batch_bias(un, qs, at_start, b0_ref, b1_ref) - lrow)
            dst = pt * (_bdot(vs, dd, _BNT) - drow)
            un.add_k(dv_ref, qs, _bdot(pt, dd, _BNN), at_start)
            un.add_k(dk_ref, qs, _bdot(dst, qq, _BNN), at_start)
            dq = _bdot(jnp.swapaxes(dst, 1, 2), ks, _BNN)
            un.store_q(out_ref, qs, jnp.where(low, dq[:, :q_n], dq[:, q_n:]) * 0.125, add=prev is not None,
                       lead=(0,))

        batch(first, True)

        def more(j, carry):
            batch(later(j), False)
            return carry

        lax.fori_loop(1, 1 + n_more, more, 0)
        out_ref[1] = dk_ref[...].astype(out_ref.dtype)
        out_ref[2] = dv_ref[...].astype(out_ref.dtype)

    pv = _pattern_view(proj, g_n)
    full = lambda a: pl.BlockSpec(a.shape, lambda r, hp: (0, 0))
    whole = _pattern_spec(g_n, lr, lambda hp: hp, lead=(3,))
    out = pl.pallas_call(
        body, grid=_pattern_grid(g_n),
        in_specs=[pl.BlockSpec(memory_space=pltpu.SMEM),
                  _pattern_spec(g_n, lr, lambda hp: 32 + hp),
                  _pattern_spec(g_n, lr, lambda hp: 40 + hp),
                  _pattern_spec(g_n, lr, lambda hp: 48 + hp),
                  _pattern_spec(g_n, lr, lambda hp: hp),
                  _pattern_spec(g_n, lr, lambda hp: 0),
                  _pattern_spec(g_n, lr, lambda hp: 0),
                  full(d0), full(m0), full(d1), full(m1)] + ([] if prev is None else [whole]),
        out_specs=whole,
        out_shape=jax.ShapeDtypeStruct(_pattern_view_shape(s_n, D_MODEL, g_n, lead=(3,)), ACT_DTYPE),
        scratch_shapes=[pltpu.VMEM((d0.shape[0], 2 * q_n), F32), pltpu.VMEM((2 * q_n, 2 * q_n), F32),
                        pltpu.VMEM((bsz, LANES, q_n), F32), pltpu.VMEM((bsz, LANES, q_n), F32),
                        pltpu.VMEM((gd, lr, LANES), F32), pltpu.VMEM((gd, lr, LANES), F32)],
        name=f"attn_bwd_d{d}",
    )(slopes, pv, pv, pv, _pattern_view(dout, g_n), _pattern_view(lse, g_n), _pattern_view(dsum, g_n),
      d0, m0, d1, m1, *([] if prev is None else [_pattern_view(prev, g_n, lead=(3,))]))
    return out.reshape(3, s_n, D_MODEL)


def _part_index(step, per, lo, n):
    return jnp.clip(step // per - lo, 0, n - 1)


def _dw_in(ut, da4, dc3, db3):
    s_n = ut.shape[1]
    tn = 512
    per = D_MODEL // tn
    shard_blocks = SHARD_COLS // tn

    def body(a_ref, p0_ref, p1_ref, p2_ref, o_ref):
        part = pl.program_id(0) // per

        @pl.when(part < 4)
        def _():
            o_ref[...] = _dot(a_ref[...], p0_ref[...])

        @pl.when((part >= 4) & (part < 7))
        def _():
            o_ref[...] = _dot(a_ref[...], p1_ref[...])

        @pl.when(part >= 7)
        def _():
            o_ref[...] = _dot(a_ref[...], p2_ref[...])

    def pspec(lo, n):
        return pl.BlockSpec((None, s_n, tn), lambda j: (_part_index(j, per, lo, n), 0, j % per))

    return pl.pallas_call(
        body, grid=(IN_COLS // tn,),
        in_specs=[pl.BlockSpec((D_MODEL, s_n), lambda j: (0, 0), pipeline_mode=pl.Buffered(1)),
                  pspec(0, 4), pspec(4, 3), pspec(7, 3)],
        out_specs=pl.BlockSpec((None, D_MODEL, tn), lambda j: (j // shard_blocks, 0, j % shard_blocks)),
        out_shape=jax.ShapeDtypeStruct((4, D_MODEL, SHARD_COLS), F32),
        name="dw_in",
    )(ut, da4, dc3, db3)


def _input_grad(da4, dc3, db3, w4, xp, norm_g, dh2, row0, rows):
    tm, tk = 256, 512
    per = D_MODEL // tk
    shard_blocks = SHARD_COLS // tk
    m0 = row0 // tm

    def body(p0_ref, p1_ref, p2_ref, w_ref, x_ref, g_ref, dh_ref, gx_ref, dg_ref):
        @pl.when(pl.program_id(0) == 0)
        def _():
            dg_ref[...] = jnp.zeros(dg_ref.shape, F32)

        du = None
        for k in range(IN_COLS // tk):
            part, cols = k // per, pl.ds((k % per) * tk, tk)
            ref, slot = (p0_ref, part) if part < 4 else (p1_ref, part - 4) if part < 7 else (p2_ref, part - 7)
            d = _dot_nt(ref[slot, :, cols], w_ref[k // shard_blocks, :, pl.ds((k % shard_blocks) * tk, tk)])
            du = d if du is None else du + d
        x = x_ref[...]
        r = lax.rsqrt(jnp.mean(x * x, axis=-1, keepdims=True) + EPS)
        nrm = x * r
        dg_ref[...] += jnp.sum(du * nrm, axis=0, keepdims=True)
        dn = du * g_ref[...]
        gx_ref[...] = dh_ref[...].astype(F32) + r * (dn - nrm * jnp.mean(dn * nrm, axis=-1, keepdims=True))

    def pspec(n):
        return pl.BlockSpec((n, tm, D_MODEL), lambda m: (0, m0 + m, 0))

    row_in = pl.BlockSpec((tm, D_MODEL), lambda m: (m0 + m, 0))
    vec = pl.BlockSpec((1, D_MODEL), lambda m: (0, 0))
    return pl.pallas_call(
        body, grid=(rows // tm,),
        in_specs=[pspec(4), pspec(3), pspec(3),
                  pl.BlockSpec(w4.shape, lambda m: (0, 0, 0), pipeline_mode=pl.Buffered(1)),
                  row_in, vec, row_in],
        out_specs=[pl.BlockSpec((tm, D_MODEL), lambda m: (m, 0)), vec],
        out_shape=[jax.ShapeDtypeStruct((rows, D_MODEL), F32), jax.ShapeDtypeStruct((1, D_MODEL), F32)],
        name="input_grad",
    )(da4, dc3, db3, w4, xp, norm_g, dh2)


class _Step:
    def __init__(self, x, tgt, norm_g, chip, w_own, after=0.0):
        self.norm_g, self.chip = norm_g, chip
        self.slopes = _alibi_slopes()
        self.xp, self.tp = _to_residue_major(x, tgt, after)
        self.u, self.ut = _rms_in(self.xp, norm_g)
        self.proj_own = _in_proj(self.u, chip, w_own=w_own)

    def project(self, w4, others):
        self.proj_own = _in_proj(self.u, self.chip, w4=w4, partial=self.proj_own, others=others)

    def mixers(self, w4, taps):
        self.w4, self.taps, self.proj = w4, taps, self.proj_own
        self.hc, self.hct = _conv_fwd(self.proj, taps)
        fwd = [_attn_fwd(self.proj, self.slopes, d) for d in PATTERNS]
        self.o, self.lse, self.ha, self.hat = _attn_combine([f[0] for f in fwd], [f[1] for f in fwd], self.proj)

    def merge_and_loss(self, woc, woa, wo, b_merge, final_g):
        self.woc, self.woa, self.wo, self.b_merge = woc, woa, wo, b_merge
        (self.yc, self.ya, self.mgt, self.dh2b, self.d_final_g, self.loss8) = _merge_loss(
            self.hc, self.ha, woc, woa, wo, self.proj, b_merge, self.xp, final_g, self.tp)

    def out_weight_grads(self):
        (dyc, dya, self.dhc, self.dout, self.dsum, self.db3, self.d_bias) = _merge_bwd(
            self.dh2b, self.wo, self.woc, self.woa, self.yc, self.ya, self.proj, self.b_merge, self.o)
        d_wo = _mm_lhs_resident(self.mgt, self.dh2b, 256, "dw_o")
        d_woc = _mm_lhs_resident(self.hct, dyc, 256, "dw_out_conv")
        d_woa = _mm_lhs_resident(self.hat, dya, 256, "dw_out_attn")
        return d_woc, d_woa, d_wo

    def conv_grads(self, after=0.0):
        self.da4, self.d_taps = _conv_bwd(self.proj, self.taps + after, self.dhc)

    def in_weight_grad(self, after=0.0):
        slopes = self.slopes + after
        self.dc3 = None
        for d in PATTERNS:
            self.dc3 = _attn_bwd(self.proj, self.dout, self.lse, self.dsum, slopes, d, prev=self.dc3)
        return _dw_in(self.ut, self.da4, self.dc3, self.db3)

    def input_grad(self, half, after=0.0):
        rows = self.xp.shape[0] // 2
        return _input_grad(self.da4, self.dc3, self.db3, self.w4, self.xp, self.norm_g + after, self.dh2b,
                           half * rows, rows)


def _local_grads(x, tgt, norm_g, w4, b_merge, conv_w, woc, woa, wo, final_g):
    st = _Step(x, tgt, norm_g, jnp.zeros((1,), jnp.int32), w4[0])
    st.project(w4, (2, 1))
    st.project(w4, (3,))
    st.mixers(w4, conv_w)
    st.merge_and_loss(woc, woa, wo, b_merge, final_g)
    d_woc, d_woa, d_wo = st.out_weight_grads()
    st.conv_grads()
    d_w4 = st.in_weight_grad()
    gx_lo, dg_lo = st.input_grad(0)
    gx_hi, dg_hi = st.input_grad(1)
    return (st.loss8, _to_natural(gx_lo, gx_hi), dg_lo + dg_hi, d_w4, st.d_bias, st.d_taps, d_woc, d_woa, d_wo,
            st.d_final_g)


MESH = pl.DeviceIdType.MESH
_CHIP_FLIPS = ((1, 0), (0, 1), (1, 1))
_ANY = pl.BlockSpec(memory_space=pl.ANY)


def _place():
    return lax.axis_index("x"), lax.axis_index("y"), lax.axis_index("c")


def _flip(v, f):
    return 1 - v if f else v


def _remote(src, dst, send_sems, recv_sems, k, device):
    return pltpu.make_async_remote_copy(src_ref=src, dst_ref=dst, send_sem=send_sems.at[k], recv_sem=recv_sems.at[k],
                                        device_id=device, device_id_type=MESH)


def _place_shard(w, chip):
    rows, cols = w.shape
    tm = 128

    def body(chip_ref, w_ref, o_ref):
        o_ref[0] = w_ref[...].astype(o_ref.dtype)

    return pl.pallas_call(
        body,
        grid_spec=pltpu.PrefetchScalarGridSpec(
            num_scalar_prefetch=1, grid=(rows // tm,),
            in_specs=[pl.BlockSpec((tm, cols), lambda i, chip_ref: (i, 0))],
            out_specs=pl.BlockSpec((1, tm, cols), lambda i, chip_ref: (chip_ref[0], i, 0))),
        out_shape=jax.ShapeDtypeStruct((4, rows, cols), MXU_DTYPE),
        name="place_shard",
    )(chip, w)


def _gather_copies(arrs, _, send_sems, recv_sems):
    x, y, c = _place()
    out = []
    for a, arr in enumerate(arrs):
        h = arr.shape[1] // 2
        mine = arr.at[2 * x + y, pl.ds(pl.multiple_of(c * h, 8), h)]
        for t, (fx, fy) in enumerate(_CHIP_FLIPS):
            out.append(_remote(mine, mine, send_sems, recv_sems, 3 * a + t, (_flip(x, fx), _flip(y, fy), c)))
    return out


def _forward_to_sibling(arrs, flips=(0, 1, 2)):
    n = len(arrs)

    def body(*refs):
        outs = refs[n:2 * n]
        send_sems, recv_sems = refs[2 * n:]
        x, y, c = _place()
        sibling = (x, y, 1 - c)
        started = []
        for a in range(n):
            h = outs[a].shape[1] // 2
            rows = pl.ds(pl.multiple_of(c * h, 8), h)
            for t in flips:
                fx, fy = _CHIP_FLIPS[t]
                landed = outs[a].at[2 * _flip(x, fx) + _flip(y, fy), rows]
                cp = _remote(landed, landed, send_sems, recv_sems, 3 * a + t, sibling)
                cp.start()
                started.append(cp)
        for a in range(n):
            h = outs[a].shape[1] // 2
            rows = pl.ds(pl.multiple_of((1 - c) * h, 8), h)
            for t in flips:
                fx, fy = _CHIP_FLIPS[t]
                handed = outs[a].at[2 * _flip(x, fx) + _flip(y, fy), rows]
                _remote(handed, handed, send_sems, recv_sems, 3 * a + t, sibling).wait_recv()
        for cp in started:
            cp.wait_send()

    return pl.pallas_call(
        body, in_specs=[_ANY] * n, out_specs=[_ANY] * n,
        out_shape=[jax.ShapeDtypeStruct(s.shape, s.dtype) for s in arrs],
        input_output_aliases={a: a for a in range(n)},
        scratch_shapes=[pltpu.SemaphoreType.DMA((3 * n,)), pltpu.SemaphoreType.DMA((3 * n,))],
        name="gathered_to_sibling_" + "".join(str(t) for t in flips),
    )(*arrs)


_HBM = pl.BlockSpec(memory_space=pltpu.HBM)
_SEM = pl.BlockSpec(memory_space=pltpu.SEMAPHORE)
_EFFECT = pltpu.SideEffectType.DATAFLOW_SIDE_EFFECTING


class _SplitExchange:
    def __init__(self, name, srcs, land_shapes, n_copies, copies, riders=()):
        self.name, self.n, self.nl, self.copies = name, len(srcs), len(land_shapes), copies
        n, nb = self.n, len(srcs) + len(land_shapes)
        lands = [lax.empty(s.shape, s.dtype) for s in land_shapes]
        bufs = [pltpu.with_memory_space_constraint(a, pltpu.HBM) for a in (*srcs, *lands, *riders)]
        na = len(bufs)

        def body(*refs):
            send_sems, recv_sems = refs[na], refs[na + 1]
            for cp in copies(refs[:n], refs[n:nb], send_sems, recv_sems):
                cp.start()
            refs[-1][...] = jnp.zeros(refs[-1].shape, F32)

        outs = pl.pallas_call(
            body, name=name + "_start",
            in_specs=[_HBM] * na,
            out_specs=[_SEM, _SEM] + [_HBM] * na + [pl.BlockSpec(memory_space=pltpu.VMEM)],
            out_shape=[pltpu.SemaphoreType.DMA((n_copies,)), pltpu.SemaphoreType.DMA((n_copies,))]
            + [pltpu.HBM(b.shape, b.dtype) for b in bufs] + [jax.ShapeDtypeStruct((8, LANES), F32)],
            input_output_aliases={i: 2 + i for i in range(na)},
            compiler_params=pltpu.CompilerParams(has_side_effects=_EFFECT),
        )(*bufs)
        self.sems, self.bufs, self.riders, self.token = outs[:2], outs[2:2 + nb], outs[2 + nb:2 + na], outs[-1]

    def after(self):
        return self.token[0, 0]

    def wait(self, done, riders=(), only=None, bufs=None):
        n, nb, copies = self.n, self.n + self.nl, self.copies
        bufs = [*(self.bufs if bufs is None else bufs),
                *[pltpu.with_memory_space_constraint(a, pltpu.HBM) for a in riders]]
        na = len(bufs)
        done = list(done) if isinstance(done, (list, tuple)) else [done]

        def body(*refs):
            send_sems, recv_sems = refs[na], refs[na + 1]
            for k, cp in enumerate(copies(refs[:n], refs[n:nb], send_sems, recv_sems)):
                if only is None or k in only:
                    cp.wait_send()
                    cp.wait_recv()

        outs = pl.pallas_call(
            body, name=self.name + "_wait" + ("" if only is None else "_" + "".join(str(k) for k in only)),
            in_specs=[_HBM] * na + [_SEM, _SEM] + [_ANY] * len(done),
            out_specs=[_HBM] * na,
            out_shape=[pltpu.HBM(b.shape, b.dtype) for b in bufs],
            input_output_aliases={i: i for i in range(na)},
            compiler_params=pltpu.CompilerParams(has_side_effects=_EFFECT),
        )(*bufs, *self.sems, *done)
        return outs[:n], outs[n:nb], outs[nb:]


def _sibling_copies(srcs, lands, send_sems, recv_sems):
    x, y, c = _place()
    out = []
    for a, (src, land) in enumerate(zip(srcs, lands)):
        h = src.shape[1] // 2
        theirs = pl.ds(pl.multiple_of((1 - c) * h, 8), h)
        out.append(_remote(src.at[:, theirs], land, send_sems, recv_sems, a, (x, y, 1 - c)))
    return out


def _grads_to_sibling(name, grads):
    shapes = [jax.ShapeDtypeStruct((4, g.shape[1] // 2, g.shape[2]), g.dtype) for g in grads]
    return _SplitExchange(name, grads, shapes, len(grads), _sibling_copies)


def _chip_copies(srcs, lands, send_sems, recv_sems):
    x, y, c = _place()
    out = []
    for a, (src, land) in enumerate(zip(srcs, lands)):
        for t, (fx, fy) in enumerate(_CHIP_FLIPS):
            tx, ty = _flip(x, fx), _flip(y, fy)
            out.append(_remote(src.at[2 * tx + ty], land.at[t], send_sems, recv_sems, 3 * a + t, (tx, ty, c)))
    return out


def _grads_to_chips(name, parts):
    shapes = [jax.ShapeDtypeStruct((3, *p.shape[1:]), p.dtype) for p in parts]
    return _SplitExchange(name, parts, shapes, 3 * len(parts), _chip_copies)


def _add_halves(g, r, half):
    _, rows, cols = g.shape
    h = rows // 2
    tm = min(h, 128)
    nt = h // tm

    def body(half_ref, g_ref, r_ref, b_ref):
        b_ref[...] = (g_ref[...] + r_ref[...]).astype(b_ref.dtype)

    spec = pl.BlockSpec((1, tm, cols), lambda j, i, half_ref: (j, i, 0))
    return pl.pallas_call(
        body,
        grid_spec=pltpu.PrefetchScalarGridSpec(
            num_scalar_prefetch=1, grid=(4, nt),
            in_specs=[pl.BlockSpec((1, tm, cols), lambda j, i, half_ref: (j, half_ref[0] * nt + i, 0)), spec],
            out_specs=spec),
        out_shape=jax.ShapeDtypeStruct((4, h, cols), BF16),
        name="add_sibling_grads",
    )(half, g, r)


def _add_chips(g, r, recv, where):
    _, h, cols = r.shape
    tm = min(h, 128)
    nt = h // tm

    def body(where_ref, g_ref, r_ref, recv_ref, out_ref):
        own = g_ref[0] + r_ref[0]
        out_ref[...] = ((own + recv_ref[0].astype(F32)) + recv_ref[1].astype(F32)) + recv_ref[2].astype(F32)

    return pl.pallas_call(
        body,
        grid_spec=pltpu.PrefetchScalarGridSpec(
            num_scalar_prefetch=1, grid=(nt,),
            in_specs=[pl.BlockSpec((1, tm, cols), lambda i, w: (w[0], w[1] * nt + i, 0)),
                      pl.BlockSpec((1, tm, cols), lambda i, w: (w[0], i, 0)),
                      pl.BlockSpec((3, tm, cols), lambda i, w: (0, i, 0))],
            out_specs=pl.BlockSpec((tm, cols), lambda i, w: (w[1] * nt + i, 0))),
        out_shape=jax.ShapeDtypeStruct((2 * h, cols), F32),
        name="add_chip_grads",
    )(where, g, r, recv)


def _share_halves(shards):
    n = len(shards)

    def body(*refs):
        outs = refs[n:2 * n]
        send_sems, recv_sems = refs[2 * n:]
        x, y, c = _place()
        copies = []
        for a in range(n):
            h = outs[a].shape[0] // 2
            mine = outs[a].at[pl.ds(pl.multiple_of(c * h, 8), h)]
            copies.append(_remote(mine, mine, send_sems, recv_sems, a, (x, y, 1 - c)))
        for cp in copies:
            cp.start()
        for a, cp in enumerate(copies):
            cp.wait_send()
            h = outs[a].shape[0] // 2
            theirs = outs[a].at[pl.ds(pl.multiple_of((1 - c) * h, 8), h)]
            _remote(theirs, theirs, send_sems, recv_sems, a, (x, y, 1 - c)).wait_recv()

    return pl.pallas_call(
        body, in_specs=[_ANY] * n, out_specs=[_ANY] * n,
        out_shape=[jax.ShapeDtypeStruct(p.shape, p.dtype) for p in shards],
        input_output_aliases={a: a for a in range(n)},
        scratch_shapes=[pltpu.SemaphoreType.DMA((n,)), pltpu.SemaphoreType.DMA((n,))],
        name="share_reduced_halves",
    )(*shards)


def _exchange_small(rows, reduce):
    cols = rows[0].shape[1]
    n = len(rows)
    assert sum(r.shape[0] for r in rows) <= 8

    def body(*refs):
        ins, out_ref = refs[:n], refs[n]
        vec_ref, gath_ref, send_sems, recv_sems = refs[n + 1:]
        x, y, c = _place()
        me = 4 * x + 2 * y + c
        vec_ref[...] = jnp.zeros(vec_ref.shape, F32)
        at = 0
        for r in ins:
            vec_ref[at:at + r.shape[0], :] = r[...]
            at += r.shape[0]
        copies = []
        for k in range(1, 8):
            peer = (_flip(x, (k >> 2) & 1), _flip(y, (k >> 1) & 1), _flip(c, k & 1))
            copies.append(_remote(vec_ref, gath_ref.at[me], send_sems, recv_sems, k - 1, peer))
        for cp in copies:
            cp.start()
        gath_ref[me] = vec_ref[...]
        for cp in copies:
            cp.wait()
        if reduce:
            tot = gath_ref[0]
            for dev in range(1, 8):
                tot = tot + gath_ref[dev]
            out_ref[...] = tot
            out_ref[7:8, :] = jnp.zeros((1, cols), F32) + jnp.sum(tot[7:8, :])
        else:
            out_ref[...] = gath_ref[...]

    vm = pl.BlockSpec(memory_space=pltpu.VMEM)
    return pl.pallas_call(
        body, in_specs=[vm] * n, out_specs=vm,
        out_shape=jax.ShapeDtypeStruct((8, cols) if reduce else (8, 8, cols), F32),
        scratch_shapes=[pltpu.VMEM((8, cols), F32), pltpu.VMEM((8, 8, cols), F32),
                        pltpu.SemaphoreType.DMA((7,)), pltpu.SemaphoreType.DMA((7,))],
        name="reduce_small" if reduce else "gather_small",
    )(*rows)


def _adamw(w, g, m, v, name):
    rows, cols = w.shape
    tm = 128 if rows % 128 == 0 else rows

    def body(w_ref, g_ref, m_ref, v_ref, d_ref, m2_ref, v2_ref, gout_ref):
        gr = g_ref[...]
        m2 = ADAM_B1 * m_ref[...] + (1.0 - ADAM_B1) * gr
        v2 = ADAM_B2 * v_ref[...] + (1.0 - ADAM_B2) * (gr * gr)
        m_hat = m2 / (1.0 - ADAM_B1 ** ADAM_STEP)
        v_hat = v2 / (1.0 - ADAM_B2 ** ADAM_STEP)
        d_ref[...] = -ADAM_LR * (m_hat / (jnp.sqrt(v_hat) + ADAM_EPS) + ADAM_WD * w_ref[...])
        m2_ref[...] = m2
        v2_ref[...] = v2
        gout_ref[...] = gr

    spec = pl.BlockSpec((tm, cols), lambda i: (i, 0))
    sds = jax.ShapeDtypeStruct((rows, cols), F32)
    return pl.pallas_call(body, grid=(rows // tm,), in_specs=[spec] * 4, out_specs=[spec] * 4,
                          out_shape=[sds] * 4, name=name)(w, g, m, v)


def kernel(x, norm_g, w_in, b_merge, conv_w, w_out_conv, w_out_attn, w_o, final_g, loss_target, m_norm_g, m_w_in, m_b_merge, m_conv_w, m_w_out_conv, m_w_out_attn, m_w_o, m_final_g, v_norm_g, v_w_in, v_b_merge, v_conv_w, v_w_out_conv, v_w_out_attn, v_w_o, v_final_g):
    mx, my, mc = _place()
    chip = (2 * mx + my).astype(jnp.int32)
    seq = x.shape[1]

    chip1 = chip.reshape(1)
    slots = [_place_shard(w[0], chip1) for w in (w_in, w_out_conv, w_out_attn, w_o)]
    taps8 = _exchange_small([conv_w[0]], reduce=False)
    taps = jnp.concatenate([taps8[2 * j, :3, :] for j in range(4)], axis=1)
    gather_in = _SplitExchange("gather_w_in", slots[:1], [], 3, _gather_copies)
    st = _Step(x[0], loss_target[0], norm_g, chip1, w_in[0], after=gather_in.after())
    w4, _, _ = gather_in.wait([st.ut, st.proj_own, taps8], only=(0, 1))
    (w4,) = _forward_to_sibling(w4, flips=(0, 1))
    st.project(w4, (2, 1))
    (w4,), _, out_slots = gather_in.wait([st.proj_own], riders=slots[1:], only=(2,), bufs=[w4])
    gather_out = _SplitExchange("gather_w_out", out_slots, [], 9, _gather_copies, riders=[w4])
    (w4,) = _forward_to_sibling(gather_out.riders, flips=(2,))
    st.project(w4, (3,))
    st.mixers(w4, taps)
    out_ws, _, _ = gather_out.wait(st.o)
    woc, woa, wo = [w.reshape(D_MODEL, D_MODEL) for w in _forward_to_sibling(out_ws)]
    st.merge_and_loss(woc, woa, wo, b_merge, final_g.reshape(1, D_MODEL))

    half = mc.astype(jnp.int32).reshape(1)
    where = jnp.stack([chip, mc.astype(jnp.int32)])
    out_grads = [g.reshape(4, -1, D_MODEL) for g in st.out_weight_grads()]
    to_sibling = _grads_to_sibling("out_grads_to_sibling", out_grads)
    st.conv_grads(after=to_sibling.after())
    out_grads, out_from_sibling, _ = to_sibling.wait(st.da4)
    to_chips = _grads_to_chips("out_grads_to_chips",
                               [_add_halves(g, r, half) for g, r in zip(out_grads, out_from_sibling)])
    d_w4 = st.in_weight_grad(after=to_chips.after())
    out_from_chips = to_chips.wait(st.dc3)[1]

    to_sibling = _grads_to_sibling("in_grad_to_sibling", [d_w4])
    gx_lo, dg_lo = st.input_grad(0, after=to_sibling.after())
    (d_w4,), (from_sibling,), _ = to_sibling.wait(gx_lo)
    to_chips = _grads_to_chips("in_grad_to_chips", [_add_halves(d_w4, from_sibling, half)])
    gx_hi, dg_hi = st.input_grad(1, after=to_chips.after())
    grad_x = _to_natural(gx_lo, gx_hi)

    where_late = where + to_chips.after().astype(jnp.int32)
    out_reduced = [_add_chips(g, r, recv, where_late)
                   for g, r, recv in zip(out_grads, out_from_sibling, out_from_chips)]
    g_woc, g_woa, g_wo = _share_halves(out_reduced)
    small = _exchange_small([dg_lo + dg_hi, st.d_bias.reshape(2, D_MODEL), st.d_taps, st.d_final_g,
                             st.loss8.reshape(1, D_MODEL)], reduce=True)
    loss = (0.5 / D_MODEL) * small[7, 0]
    g_taps = lax.dynamic_slice(small[3:6], (0, chip * (D_MODEL // 4)), (3, D_MODEL // 4))
    upd = {
        "norm_g": _adamw(norm_g, small[0:1], m_norm_g, v_norm_g, "adamw_norm_g"),
        "b_merge": _adamw(b_merge, small[1:3].reshape(1, 2 * D_MODEL), m_b_merge, v_b_merge, "adamw_b_merge"),
        "conv_w": _adamw(conv_w[0], g_taps, m_conv_w[0], v_conv_w[0], "adamw_conv_w"),
        "w_out_conv": _adamw(w_out_conv[0], g_woc, m_w_out_conv[0], v_w_out_conv[0], "adamw_w_out_conv"),
        "w_out_attn": _adamw(w_out_attn[0], g_woa, m_w_out_attn[0], v_w_out_attn[0], "adamw_w_out_attn"),
        "w_o": _adamw(w_o[0], g_wo, m_w_o[0], v_w_o[0], "adamw_w_o"),
        "final_g": _adamw(final_g.reshape(1, D_MODEL), small[6:7], m_final_g.reshape(1, D_MODEL),
                          v_final_g.reshape(1, D_MODEL), "adamw_final_g"),
    }
    behind = [grad_x] + [u[0] for u in upd.values()]
    in_reduced = _add_chips(d_w4, from_sibling, to_chips.wait(behind)[1][0], where)
    (g_w_in,) = _share_halves([in_reduced])
    upd["w_in"] = _adamw(w_in[0], g_w_in, m_w_in[0], v_w_in[0], "adamw_w_in")

    names = ["norm_g", "w_in", "b_merge", "conv_w", "w_out_conv", "w_out_attn", "w_o", "final_g"]
    shapes = [norm_g.shape, w_in.shape, b_merge.shape, conv_w.shape, w_out_conv.shape, w_out_attn.shape,
              w_o.shape, final_g.shape]
    outs = [loss, grad_x.reshape(1, seq, D_MODEL)]
    for k in (3, 0, 1, 2):
        outs += [upd[n][k].reshape(s) for n, s in zip(names, shapes)]
    return tuple(outs)
```

```python
import functools

import numpy as np
import jax
import jax.numpy as jnp
from jax import lax
from jax.experimental import pallas as pl
from jax.experimental.pallas import tpu as pltpu

F32 = jnp.float32
BF16 = jnp.bfloat16
MXU_DTYPE = jnp.bfloat16
ACT_DTYPE = jnp.bfloat16

D_MODEL = 1024
N_HEADS = 16
HEAD_DIM = 64
QB = 128
N_RES = 16
LANES = 128
HP = N_HEADS * HEAD_DIM // LANES
IN_COLS = 10 * D_MODEL
SHARD_COLS = IN_COLS // 4
EPS = 1e-6
NEG = -1e30

ADAM_LR, ADAM_B1, ADAM_B2, ADAM_EPS, ADAM_WD, ADAM_STEP = 0.001, 0.9, 0.999, 1e-08, 0.01, 10

PATTERNS = {1: (16, 8), 4: (4, 32), 16: (1, 128)}

_NN = (((1,), (0,)), ((), ()))
_NT = (((1,), (1,)), ((), ()))


def _dot(a, b):
    return lax.dot_general(a.astype(MXU_DTYPE), b.astype(MXU_DTYPE), _NN, preferred_element_type=F32)


def _dot_nt(a, b):
    return lax.dot_general(a.astype(MXU_DTYPE), b.astype(MXU_DTYPE), _NT, preferred_element_type=F32)


def _split3(x):
    hi = x.astype(BF16)
    r1 = x - hi.astype(F32)
    mid = r1.astype(BF16)
    lo = (r1 - mid.astype(F32)).astype(BF16)
    return hi, mid, lo


def _select_rows(sel, x):
    return sum(lax.dot_general(sel, t, _NN, preferred_element_type=F32) for t in _split3(x))


def _select_cols(x, sel, terms=3):
    return sum(lax.dot_general(t, sel, _NN, preferred_element_type=F32) for t in _split3(x)[:terms])


def _sigmoid(z):
    return 1.0 / (1.0 + jnp.exp(-z))


def _perm_matrix():
    idx = np.arange(256)
    p = np.zeros((256, 256), np.float32)
    p[(idx % 16) * 16 + idx // 16, idx] = 1.0
    return jnp.asarray(p, BF16)


def _head_expand_matrix():
    e = np.zeros((LANES, D_MODEL), np.float32)
    for h in range(N_HEADS):
        e[8 * h, HEAD_DIM * h:HEAD_DIM * (h + 1)] = 1.0
    return jnp.asarray(e, BF16)


def _head_sum_matrix():
    e = np.zeros((D_MODEL, LANES), np.float32)
    for h in range(N_HEADS):
        e[HEAD_DIM * h:HEAD_DIM * (h + 1), 8 * h:8 * (h + 1)] = 1.0
    return jnp.asarray(e, BF16)


def _attn_tables(d):
    g_n, rq = PATTERNS[d]
    q_n = g_n * rq
    gq, iq = np.arange(q_n) // rq, np.arange(q_n) % rq

    def tab(kn, base):
        k_n = g_n * kn
        gk, jk = np.arange(k_n) // kn, np.arange(k_n) % kn
        delta = g_n * (base + iq[:, None] - jk[None, :]) + gq[:, None] - gk[None, :]
        valid = (delta >= 0) & (delta <= QB)
        dist = np.where(valid, d * delta, 0).astype(np.float32)
        madd = np.where(valid, 0.0, NEG).astype(np.float32)
        return dist, madd

    d0, m0 = tab(rq if g_n == 1 else 2 * rq, 0)
    d1, m1 = tab(2 * rq, rq)
    return d0, m0, d1, m1


def _alibi_slopes():
    return jnp.exp2(-8.0 * jnp.arange(1, N_HEADS + 1, dtype=F32) / N_HEADS)


def _to_residue_major(x, tgt, after=0.0):
    s_n, c_n = x.shape
    lr = s_n // N_RES
    pm = (_perm_matrix().astype(F32) + after).astype(BF16)

    def body(p_ref, x_ref, t_ref, xo_ref, to_ref):
        pm = p_ref[...]
        xo_ref[...] = _select_rows(pm, x_ref[...]).reshape(16, 16, c_n)
        to_ref[...] = _select_rows(pm, t_ref[...]).reshape(16, 16, c_n)

    nat = pl.BlockSpec((256, c_n), lambda i: (i, 0))
    res = pl.BlockSpec((16, 16, c_n), lambda i: (0, i, 0))
    xo, to = pl.pallas_call(
        body, grid=(s_n // 256,),
        in_specs=[pl.BlockSpec((256, 256), lambda i: (0, 0)), nat, nat],
        out_specs=[res, res],
        out_shape=[jax.ShapeDtypeStruct((16, lr, c_n), F32)] * 2,
        name="perm_in",
    )(pm, x, tgt)
    return xo.reshape(s_n, c_n), to.reshape(s_n, c_n)


def _to_natural(gx_lo, gx_hi):
    half_rows, c_n = gx_lo.shape
    lr = half_rows // (N_RES // 2)

    def body(p_ref, lo_ref, hi_ref, o_ref):
        g = jnp.concatenate([lo_ref[...], hi_ref[...]], axis=0)
        o_ref[...] = _select_rows(p_ref[...], g.reshape(256, c_n))

    half = pl.BlockSpec((8, 16, c_n), lambda i: (0, i, 0))
    return pl.pallas_call(
        body, grid=(lr // 16,),
        in_specs=[pl.BlockSpec((256, 256), lambda i: (0, 0)), half, half],
        out_specs=pl.BlockSpec((256, c_n), lambda i: (i, 0)),
        out_shape=jax.ShapeDtypeStruct((2 * half_rows, c_n), F32),
        name="perm_out",
    )(_perm_matrix(), gx_lo.reshape(8, lr, c_n), gx_hi.reshape(8, lr, c_n))


def _rms_in(xp, norm_g):
    s_n, c_n = xp.shape
    tm = 512

    def body(x_ref, g_ref, u_ref, ut_ref):
        x = x_ref[...]
        r = lax.rsqrt(jnp.mean(x * x, axis=-1, keepdims=True) + EPS)
        u = x * r * g_ref[...]
        u_ref[...] = u.astype(u_ref.dtype)
        ut_ref[...] = u.T.astype(ut_ref.dtype)

    return pl.pallas_call(
        body, grid=(s_n // tm,),
        in_specs=[pl.BlockSpec((tm, c_n), lambda i: (i, 0)), pl.BlockSpec((1, c_n), lambda i: (0, 0))],
        out_specs=[pl.BlockSpec((tm, c_n), lambda i: (i, 0)), pl.BlockSpec((c_n, tm), lambda i: (0, i))],
        out_shape=[jax.ShapeDtypeStruct((s_n, c_n), ACT_DTYPE), jax.ShapeDtypeStruct((c_n, s_n), ACT_DTYPE)],
        name="rms_in",
    )(xp, norm_g)


def _in_proj(u, chip, w_own=None, w4=None, partial=None, others=(), after=None):
    s_n = u.shape[0]
    tn, cm = 512, 512
    per = SHARD_COLS // tn
    own = partial is None
    extra = [] if after is None else [after]

    def body(chip_ref, a_ref, b_ref, *rest):
        o_ref = rest[-1]
        b = b_ref[...]
        for c in range(s_n // cm):
            o_ref[c * cm:(c + 1) * cm, :] = _dot(a_ref[c * cm:(c + 1) * cm, :], b).astype(o_ref.dtype)

    def shard(n, chip_ref):
        if own:
            return chip_ref[0]
        mask = others[-1]
        for i, m in enumerate(others[:-1]):
            mask = jnp.where(n // per == i, m, mask)
        return jnp.bitwise_xor(chip_ref[0], mask)

    w_spec = (pl.BlockSpec((D_MODEL, tn), lambda n, c: (0, n)) if own else
              pl.BlockSpec((None, D_MODEL, tn), lambda n, c: (shard(n, c), 0, n % per)))
    return pl.pallas_call(
        body,
        grid_spec=pltpu.PrefetchScalarGridSpec(
            num_scalar_prefetch=1, grid=(per if own else len(others) * per,),
            in_specs=[pl.BlockSpec((s_n, D_MODEL), lambda n, c: (0, 0)), w_spec] + ([] if own else [_ANY])
            + [pl.BlockSpec((8, LANES), lambda n, c: (0, 0))] * len(extra),
            out_specs=pl.BlockSpec((s_n, tn), lambda n, c: (0, shard(n, c) * per + n % per))),
        out_shape=jax.ShapeDtypeStruct((s_n, IN_COLS), ACT_DTYPE),
        input_output_aliases={} if own else {3: 0},
        name="in_proj_own" if own else "in_proj_" + "_".join(str(m) for m in others),
    )(*([chip, u, w_own] if own else [chip, u, w4, partial]), *extra)


def _conv_terms(xc_ref, cg_ref, r, row, lr, cache):
    def a_of(q):
        if q not in cache:
            cache[q] = cg_ref[q].astype(F32) * xc_ref[q].astype(F32)
        return cache[q]

    def shift_down(v):
        return jnp.where(row >= 1, pltpu.roll(v, 1, 0), 0.0)

    a = a_of(r)
    am1 = a_of(r - 1) if r >= 1 else shift_down(a_of(N_RES - 1))
    am2 = a_of(r - 2) if r >= 2 else shift_down(a_of(N_RES - 2 + r))
    return a, am1, am2


def _conv_fwd(proj, conv_w):
    s_n = proj.shape[0]
    lr = s_n // N_RES
    pv = proj.reshape(N_RES, lr, IN_COLS)

    def body(xc_ref, bg_ref, cg_ref, zc_ref, w_ref, hc_ref, hct_ref):
        w = w_ref[...]
        row = lax.broadcasted_iota(jnp.int32, (lr, LANES), 0)
        products = {}
        for r in range(N_RES):
            a, am1, am2 = _conv_terms(xc_ref, cg_ref, r, row, lr, products)
            c = w[0:1] * am2 + w[1:2] * am1 + w[2:3] * a
            z = zc_ref[r].astype(F32)
            hc = z * _sigmoid(z) * bg_ref[r].astype(F32) * c
            hc_ref[r] = hc.astype(hc_ref.dtype)
            hct_ref[:, r * lr:(r + 1) * lr] = hc.T.astype(hct_ref.dtype)

    def col(part):
        return pl.BlockSpec((N_RES, lr, LANES), lambda j: (0, 0, part * 8 + j))

    hc, hct = pl.pallas_call(
        body, grid=(D_MODEL // LANES,),
        in_specs=[col(0), col(1), col(2), col(3), pl.BlockSpec((3, LANES), lambda j: (0, j))],
        out_specs=[pl.BlockSpec((N_RES, lr, LANES), lambda j: (0, 0, j)),
                   pl.BlockSpec((LANES, s_n), lambda j: (j, 0))],
        out_shape=[jax.ShapeDtypeStruct((N_RES, lr, D_MODEL), ACT_DTYPE),
                   jax.ShapeDtypeStruct((D_MODEL, s_n), ACT_DTYPE)],
        name="conv_fwd",
    )(pv, pv, pv, pv, conv_w)
    return hc.reshape(s_n, D_MODEL), hct


RES_PER_STEP = 8
FWD_BATCH = {1: 8, 4: 8, 16: RES_PER_STEP}
BWD_BATCH = {1: 8, 4: 8, 16: RES_PER_STEP}

_BNT = (((2,), (2,)), ((0,), (0,)))
_BNN = (((2,), (1,)), ((0,), (0,)))


def _bdot(a, b, dims):
    return lax.dot_general(a.astype(MXU_DTYPE), b.astype(MXU_DTYPE), dims, preferred_element_type=F32)


def _pattern_view_shape(s_n, c_n, g_n, lead=()):
    lr = s_n // N_RES
    return (*lead, 4, 4, lr, c_n) if g_n == 4 else (*lead, N_RES, lr, c_n)


def _pattern_view(a, g_n, lead=()):
    return a.reshape(_pattern_view_shape(a.shape[-2], a.shape[-1], g_n, lead))


def _pattern_grid(g_n):
    return (N_RES // RES_PER_STEP if g_n == 1 else N_RES // g_n, HP)


def _pattern_spec(g_n, lr, col_of_hp, lead=()):
    z = (0,) * len(lead)
    if g_n == 16:
        return pl.BlockSpec((*lead, 16, lr, LANES), lambda r, hp: (*z, 0, 0, col_of_hp(hp)))
    if g_n == 4:
        return pl.BlockSpec((*lead, 4, None, lr, LANES), lambda r, hp: (*z, 0, r, 0, col_of_hp(hp)))
    return pl.BlockSpec((*lead, RES_PER_STEP, lr, LANES), lambda r, hp: (*z, r, 0, col_of_hp(hp)))


def _aligned(start, m):
    return start if isinstance(start, int) else pl.multiple_of(start, m)


class _Units:
    def __init__(self, g_n, rq):
        self.g_n, self.rq = g_n, rq
        self.per_res, self.paired = g_n == 1, rq == 8

    def plan(self, lr, size):
        if self.per_res:
            return [0], lr // self.rq - 1, lambda j: [pl.multiple_of(j * self.rq, self.rq)]
        step = 16 if self.paired else self.rq
        per = min(size // 2 if self.paired else size, lr // step)
        assert (lr // step) % per == 0
        return ([i * step for i in range(per)], lr // step // per - 1,
                lambda j: [pl.multiple_of((j * per + i) * step, step) for i in range(per)])

    def count(self, qs):
        return RES_PER_STEP if self.per_res else len(qs) * (2 if self.paired else 1)

    def _split(self, tiles, lo, rows):
        return tiles[:, lo:lo + rows].reshape(self.g_n * rows, LANES)

    def load_q(self, ref, qs):
        rq = self.rq
        if self.per_res:
            return ref[:, pl.ds(qs[0], rq), :]
        if self.paired:
            tiles = [ref[:, pl.ds(q, 16), :].astype(F32) for q in qs]
            return jnp.stack([self._split(t, lo, 8) for t in tiles for lo in (0, 8)])
        return jnp.stack([ref[:, pl.ds(q, rq), :].reshape(self.g_n * rq, LANES) for q in qs])

    def _key_rows(self, q, at_start):
        return (0, 2 * self.rq) if at_start else (_aligned(q - self.rq, self.rq), 2 * self.rq)

    def load_k(self, ref, qs, first):
        rq = self.rq
        if self.per_res:
            return ref[:, pl.ds(0, rq), :] if first else ref[:, pl.ds(_aligned(qs[0] - rq, rq), 2 * rq), :]
        if self.paired:
            out = []
            for i, q in enumerate(qs):
                if first and i == 0:
                    t = ref[:, 0:16, :].astype(F32)
                    out += [self._split(t, 0, 16)] * 2
                else:
                    t = ref[:, pl.ds(_aligned(q - 16, 16), 32), :].astype(F32)
                    out += [self._split(t, 8, 16), self._split(t, 16, 16)]
            return jnp.stack(out)
        rows = [self._key_rows(q, first and i == 0) for i, q in enumerate(qs)]
        return jnp.stack([ref[:, pl.ds(k0, n), :].reshape(self.g_n * n, LANES) for k0, n in rows])

    def store_q(self, ref, qs, val, add=False, lead=()):
        if self.per_res:
            pieces = [(qs[0], self.rq, val)]
        elif self.paired:
            pieces = [(q, 16, jnp.concatenate([val[2 * i].reshape(self.g_n, 8, LANES),
                                               val[2 * i + 1].reshape(self.g_n, 8, LANES)], axis=1))
                      for i, q in enumerate(qs)]
        else:
            pieces = [(q, self.rq, val[i].reshape(self.g_n, self.rq, LANES)) for i, q in enumerate(qs)]
        for start, rows, v in pieces:
            idx = (*lead, slice(None), pl.ds(start, rows), slice(None))
            ref[idx] = (ref[idx] + v if add else v).astype(ref.dtype)

    def add_k(self, ref, qs, val, first):
        rq = self.rq
        if self.per_res:
            k0, n = (0, rq) if first else (_aligned(qs[0] - rq, rq), 2 * rq)
            ref[:, pl.ds(k0, n), :] += val
            return
        if self.paired:
            starts = [s for i, q in enumerate(qs)
                      for s in ((0, 0) if first and i == 0 else (_aligned(q - 8, 8), q))]
            rows = [(s, 16) for s in starts]
        else:
            rows = [self._key_rows(q, first and i == 0) for i, q in enumerate(qs)]
        for b, (k0, n) in enumerate(rows):
            ref[:, pl.ds(k0, n), :] += val[b].reshape(self.g_n, n, LANES)


def _batch_bias(un, qs, at_start, first_ref, general_ref):
    if not at_start:
        return general_ref[...][None]
    if un.per_res:
        return first_ref[...][None]
    return jnp.concatenate([first_ref[...][None]] + [general_ref[...][None]] * (un.count(qs) - 1), axis=0)


def _stack_heads(x, low):
    zero = jnp.zeros_like(x)
    return jnp.concatenate([jnp.where(low, x, zero), jnp.where(low, zero, x)], axis=1)


def _attn_fwd(proj, slopes, d):
    g_n, rq = PATTERNS[d]
    un = _Units(g_n, rq)
    s_n = proj.shape[0]
    lr = s_n // N_RES
    nb = lr // rq
    q_n = g_n * rq
    d0, m0, d1, m1 = _attn_tables(d)
    first, n_more, later = un.plan(lr, FWD_BATCH[d])

    def body(sl_ref, q_ref, k_ref, v_ref, d0_ref, m0_ref, d1_ref, m1_ref, o_ref, lse_ref, b0_ref, b1_ref):
        hp = pl.program_id(1)

        @pl.when(hp == 0)
        def _():
            lse_ref[...] = jnp.zeros(lse_ref.shape, F32)

        for h in (0, 1):
            slope = sl_ref[2 * hp + h]
            b0_ref[h * q_n:(h + 1) * q_n, :] = m0_ref[...] - slope * d0_ref[...]
            b1_ref[h * q_n:(h + 1) * q_n, :] = m1_ref[...] - slope * d1_ref[...]

        lane = lax.broadcasted_iota(jnp.int32, (1, q_n, LANES), 2)
        low = lane < HEAD_DIM
        grp = lane // 8

        def batch(qs, at_start):
            qq = _stack_heads(un.load_q(q_ref, qs) * 0.125, low)
            s = _bdot(qq, un.load_k(k_ref, qs, at_start), _BNT) + _batch_bias(un, qs, at_start, b0_ref, b1_ref)
            m = jnp.max(s, axis=2, keepdims=True)
            p = jnp.exp(s - m)
            l = jnp.sum(p, axis=2, keepdims=True)
            o = _bdot(p, un.load_k(v_ref, qs, at_start), _BNN) * (1.0 / l)
            lse = m + jnp.log(l)
            un.store_q(o_ref, qs, jnp.where(low, o[:, :q_n], o[:, q_n:]))
            upd = jnp.where(grp == 2 * hp, lse[:, :q_n], 0.0) + jnp.where(grp == 2 * hp + 1, lse[:, q_n:], 0.0)
            un.store_q(lse_ref, qs, upd, add=True)

        batch(first, True)

        def more(j, carry):
            batch(later(j), False)
            return carry

        lax.fori_loop(1, 1 + n_more, more, 0)

    pv = _pattern_view(proj, g_n)
    full = lambda a: pl.BlockSpec(a.shape, lambda r, hp: (0, 0))
    o, lse = pl.pallas_call(
        body, grid=_pattern_grid(g_n),
        in_specs=[pl.BlockSpec(memory_space=pltpu.SMEM),
                  _pattern_spec(g_n, lr, lambda hp: 32 + hp),
                  _pattern_spec(g_n, lr, lambda hp: 40 + hp),
                  _pattern_spec(g_n, lr, lambda hp: 48 + hp),
                  full(d0), full(m0), full(d1), full(m1)],
        out_specs=[_pattern_spec(g_n, lr, lambda hp: hp), _pattern_spec(g_n, lr, lambda hp: 0)],
        out_shape=[jax.ShapeDtypeStruct(_pattern_view_shape(s_n, D_MODEL, g_n), ACT_DTYPE),
                   jax.ShapeDtypeStruct(_pattern_view_shape(s_n, LANES, g_n), F32)],
        scratch_shapes=[pltpu.VMEM((2 * q_n, d0.shape[1]), F32), pltpu.VMEM((2 * q_n, 2 * q_n), F32)],
        name=f"attn_fwd_d{d}",
    )(slopes, pv, pv, pv, d0, m0, d1, m1)
    return o.reshape(s_n, D_MODEL), lse.reshape(s_n, LANES)


def _attn_combine(outs, lses, proj):
    s_n = proj.shape[0]
    tm = 512

    def body(o1_ref, o2_ref, o3_ref, l1_ref, l2_ref, l3_ref, za_ref, e_ref, o_ref, lse_ref, ha_ref, hat_ref):
        ls = [l1_ref[...], l2_ref[...], l3_ref[...]]
        mx = jnp.maximum(jnp.maximum(ls[0], ls[1]), ls[2])
        den = sum(jnp.exp(l - mx) for l in ls)
        lse = mx + jnp.log(den)
        lse_ref[...] = lse
        o = jnp.zeros((tm, D_MODEL), F32)
        for l, oref in zip(ls, (o1_ref, o2_ref, o3_ref)):
            o = o + _select_cols(jnp.exp(l - lse), e_ref[...], terms=2) * oref[...].astype(F32)
        o_ref[...] = o.astype(o_ref.dtype)
        z = za_ref[...].astype(F32)
        ha = z * _sigmoid(z) * o
        ha_ref[...] = ha.astype(ha_ref.dtype)
        hat_ref[...] = ha.T.astype(hat_ref.dtype)

    row = lambda w: pl.BlockSpec((tm, w), lambda i: (i, 0))
    return pl.pallas_call(
        body, grid=(s_n // tm,),
        in_specs=[row(D_MODEL)] * 3 + [row(LANES)] * 3
        + [pl.BlockSpec((tm, D_MODEL), lambda i: (i, 7)), pl.BlockSpec((LANES, D_MODEL), lambda i: (0, 0))],
        out_specs=[row(D_MODEL), row(LANES), row(D_MODEL), pl.BlockSpec((D_MODEL, tm), lambda i: (0, i))],
        out_shape=[jax.ShapeDtypeStruct((s_n, D_MODEL), ACT_DTYPE), jax.ShapeDtypeStruct((s_n, LANES), F32),
                   jax.ShapeDtypeStruct((s_n, D_MODEL), ACT_DTYPE), jax.ShapeDtypeStruct((D_MODEL, s_n), ACT_DTYPE)],
        name="attn_combine",
    )(*outs, *lses, proj, _head_expand_matrix())


def _gates(gc_ref, ga_ref, b_ref):
    b = b_ref[...]
    gc = _sigmoid(gc_ref[...].astype(F32) + b[:, :D_MODEL])
    ga = _sigmoid(ga_ref[...].astype(F32) + b[:, D_MODEL:])
    return gc, ga


def _merge_loss(hc, ha, woc, woa, wo, proj, b_merge, xp, final_g, tgt):
    s_n = xp.shape[0]
    tm = 512

    def body(hc_ref, ha_ref, woc_ref, woa_ref, wo_ref, gc_ref, ga_ref, b_ref, x_ref, gf_ref, t_ref,
             yc_ref, ya_ref, mgt_ref, dhb_ref, dgf_ref, loss_ref):
        i = pl.program_id(0)

        @pl.when(i == 0)
        def _():
            dgf_ref[...] = jnp.zeros(dgf_ref.shape, F32)
            loss_ref[...] = jnp.zeros(loss_ref.shape, F32)

        yc = _dot(hc_ref[...], woc_ref[...])
        ya = _dot(ha_ref[...], woa_ref[...])
        gc, ga = _gates(gc_ref, ga_ref, b_ref)
        mg = gc * yc + ga * ya
        yc_ref[...] = yc.astype(yc_ref.dtype)
        ya_ref[...] = ya.astype(ya_ref.dtype)
        mgt_ref[...] = mg.T.astype(mgt_ref.dtype)
        h2 = x_ref[...] + _dot(mg, wo_ref[...])
        r2 = lax.rsqrt(jnp.mean(h2 * h2, axis=-1, keepdims=True) + EPS)
        nrm = h2 * r2
        gf = gf_ref[...]
        err = nrm * gf - t_ref[...]
        e2 = (err * err).reshape(tm // 8, 8, D_MODEL).sum(axis=0)
        loss_ref[...] += sum(e2[:, c * LANES:(c + 1) * LANES] for c in range(D_MODEL // LANES))
        dy = err * (1.0 / D_MODEL)
        dgf_ref[...] += jnp.sum(dy * nrm, axis=0, keepdims=True)
        dn = dy * gf
        dh2 = r2 * (dn - nrm * jnp.mean(dn * nrm, axis=-1, keepdims=True))
        dhb_ref[...] = dh2.astype(dhb_ref.dtype)

    row = pl.BlockSpec((tm, D_MODEL), lambda i: (i, 0))
    wsp = pl.BlockSpec((D_MODEL, D_MODEL), lambda i: (0, 0))
    vec = lambda w: pl.BlockSpec((1, w), lambda i: (0, 0))
    act = jax.ShapeDtypeStruct((s_n, D_MODEL), ACT_DTYPE)
    return pl.pallas_call(
        body, grid=(s_n // tm,),
        in_specs=[row, row, wsp, wsp, wsp,
                  pl.BlockSpec((tm, D_MODEL), lambda i: (i, 8)), pl.BlockSpec((tm, D_MODEL), lambda i: (i, 9)),
                  vec(2 * D_MODEL), row, vec(D_MODEL), row],
        out_specs=[row, row, pl.BlockSpec((D_MODEL, tm), lambda i: (0, i)), row,
                   vec(D_MODEL), pl.BlockSpec((8, LANES), lambda i: (0, 0))],
        out_shape=[act, act, jax.ShapeDtypeStruct((D_MODEL, s_n), ACT_DTYPE), act,
                   jax.ShapeDtypeStruct((1, D_MODEL), F32), jax.ShapeDtypeStruct((8, LANES), F32)],
        name="merge_loss",
    )(hc, ha, woc, woa, wo, proj, proj, b_merge, xp, final_g, tgt)


def _merge_bwd(dh2b, wo, woc, woa, yc, ya, proj, b_merge, o):
    s_n = dh2b.shape[0]
    tm = 512

    def body(dh_ref, wo_ref, woc_ref, woa_ref, yc_ref, ya_ref, gc_ref, ga_ref, b_ref, o_ref, za_ref, e_ref,
             dyc_ref, dya_ref, dhc_ref, do_ref, dsum_ref, db3_ref, dbias_ref):
        i = pl.program_id(0)

        @pl.when(i == 0)
        def _():
            dbias_ref[...] = jnp.zeros(dbias_ref.shape, F32)

        dmg = _dot_nt(dh_ref[...], wo_ref[...])
        gc, ga = _gates(gc_ref, ga_ref, b_ref)
        dgc = dmg * yc_ref[...].astype(F32) * gc * (1.0 - gc)
        dga = dmg * ya_ref[...].astype(F32) * ga * (1.0 - ga)
        dbias_ref[:, :D_MODEL] += jnp.sum(dgc, axis=0, keepdims=True)
        dbias_ref[:, D_MODEL:] += jnp.sum(dga, axis=0, keepdims=True)
        dyc = dmg * gc
        dya = dmg * ga
        dyc_ref[...] = dyc.astype(dyc_ref.dtype)
        dya_ref[...] = dya.astype(dya_ref.dtype)
        dhc_ref[...] = _dot_nt(dyc, woc_ref[...]).astype(dhc_ref.dtype)
        dha = _dot_nt(dya, woa_ref[...])
        z = za_ref[...].astype(F32)
        sg = _sigmoid(z)
        ov = o_ref[...].astype(F32)
        dout = dha * z * sg
        do_ref[...] = dout.astype(do_ref.dtype)
        dsum_ref[...] = _select_cols(dout * ov, e_ref[...], terms=2)
        db3_ref[0] = (dha * ov * sg * (1.0 + z * (1.0 - sg))).astype(db3_ref.dtype)
        db3_ref[1] = dgc.astype(db3_ref.dtype)
        db3_ref[2] = dga.astype(db3_ref.dtype)

    row = pl.BlockSpec((tm, D_MODEL), lambda i: (i, 0))
    wsp = pl.BlockSpec((D_MODEL, D_MODEL), lambda i: (0, 0))
    act = jax.ShapeDtypeStruct((s_n, D_MODEL), ACT_DTYPE)
    return pl.pallas_call(
        body, grid=(s_n // tm,),
        in_specs=[row, wsp, wsp, wsp, row, row,
                  pl.BlockSpec((tm, D_MODEL), lambda i: (i, 8)), pl.BlockSpec((tm, D_MODEL), lambda i: (i, 9)),
                  pl.BlockSpec((1, 2 * D_MODEL), lambda i: (0, 0)), row,
                  pl.BlockSpec((tm, D_MODEL), lambda i: (i, 7)), pl.BlockSpec((D_MODEL, LANES), lambda i: (0, 0))],
        out_specs=[row, row, row, row, pl.BlockSpec((tm, LANES), lambda i: (i, 0)),
                   pl.BlockSpec((3, tm, D_MODEL), lambda i: (0, i, 0)),
                   pl.BlockSpec((1, 2 * D_MODEL), lambda i: (0, 0))],
        out_shape=[act, act, act, act, jax.ShapeDtypeStruct((s_n, LANES), F32),
                   jax.ShapeDtypeStruct((3, s_n, D_MODEL), ACT_DTYPE),
                   jax.ShapeDtypeStruct((1, 2 * D_MODEL), F32)],
        name="merge_bwd",
    )(dh2b, wo, woc, woa, yc, ya, proj, proj, b_merge, o, proj, _head_sum_matrix())


def _mm_lhs_resident(a, b, tn, name):
    m_n, k_n = a.shape
    n_n = b.shape[1]

    def body(a_ref, b_ref, o_ref):
        o_ref[...] = _dot(a_ref[...], b_ref[...])

    return pl.pallas_call(
        body, grid=(n_n // tn,),
        in_specs=[pl.BlockSpec((m_n, k_n), lambda n: (0, 0)), pl.BlockSpec((k_n, tn), lambda n: (0, n))],
        out_specs=pl.BlockSpec((m_n, tn), lambda n: (0, n)),
        out_shape=jax.ShapeDtypeStruct((m_n, n_n), F32),
        name=name,
    )(a, b)


def _conv_bwd(proj, conv_w, dhc):
    s_n = proj.shape[0]
    lr = s_n // N_RES
    pv = proj.reshape(N_RES, lr, IN_COLS)

    def body(xc_ref, bg_ref, cg_ref, zc_ref, w_ref, dhc_ref, da4_ref, dw_ref, dc_ref):
        w = w_ref[...]
        row = lax.broadcasted_iota(jnp.int32, (lr, LANES), 0)
        dw = [jnp.zeros((1, LANES), F32) for _ in range(3)]
        products = {}
        for r in range(N_RES):
            a, am1, am2 = _conv_terms(xc_ref, cg_ref, r, row, lr, products)
            c = w[0:1] * am2 + w[1:2] * am1 + w[2:3] * a
            z = zc_ref[r].astype(F32)
            sg = _sigmoid(z)
            sz = z * sg
            bg = bg_ref[r].astype(F32)
            dh = dhc_ref[r].astype(F32)
            da4_ref[1, r] = (dh * sz * c).astype(da4_ref.dtype)
            da4_ref[3, r] = (dh * bg * c * sg * (1.0 + z * (1.0 - sg))).astype(da4_ref.dtype)
            dc = dh * sz * bg
            dc_ref[r] = dc
            dw[0] = dw[0] + jnp.sum(dc * am2, axis=0, keepdims=True)
            dw[1] = dw[1] + jnp.sum(dc * am1, axis=0, keepdims=True)
            dw[2] = dw[2] + jnp.sum(dc * a, axis=0, keepdims=True)
        dw_ref[0:1, :] = dw[0]
        dw_ref[1:2, :] = dw[1]
        dw_ref[2:3, :] = dw[2]

        def shift_up(v):
            return jnp.where(row < lr - 1, pltpu.roll(v, lr - 1, 0), 0.0)

        for r in range(N_RES):
            dp1 = dc_ref[r + 1] if r + 1 < N_RES else shift_up(dc_ref[0])
            dp2 = dc_ref[r + 2] if r + 2 < N_RES else shift_up(dc_ref[r + 2 - N_RES])
            da = w[2:3] * dc_ref[r] + w[1:2] * dp1 + w[0:1] * dp2
            da4_ref[0, r] = (da * cg_ref[r].astype(F32)).astype(da4_ref.dtype)
            da4_ref[2, r] = (da * xc_ref[r].astype(F32)).astype(da4_ref.dtype)

    def col(part):
        return pl.BlockSpec((N_RES, lr, LANES), lambda j: (0, 0, part * 8 + j))

    da4, dw = pl.pallas_call(
        body, grid=(D_MODEL // LANES,),
        in_specs=[col(0), col(1), col(2), col(3), pl.BlockSpec((3, LANES), lambda j: (0, j)),
                  pl.BlockSpec((N_RES, lr, LANES), lambda j: (0, 0, j))],
        out_specs=[pl.BlockSpec((4, N_RES, lr, LANES), lambda j: (0, 0, 0, j)),
                   pl.BlockSpec((3, LANES), lambda j: (0, j))],
        out_shape=[jax.ShapeDtypeStruct((4, N_RES, lr, D_MODEL), ACT_DTYPE),
                   jax.ShapeDtypeStruct((3, D_MODEL), F32)],
        scratch_shapes=[pltpu.VMEM((N_RES, lr, LANES), F32)],
        name="conv_bwd",
    )(pv, pv, pv, pv, conv_w, dhc.reshape(N_RES, lr, D_MODEL))
    return da4.reshape(4, s_n, D_MODEL), dw


def _attn_bwd(proj, dout, lse, dsum, slopes, d, prev=None):
    g_n, rq = PATTERNS[d]
    un = _Units(g_n, rq)
    s_n = proj.shape[0]
    lr = s_n // N_RES
    nb = lr // rq
    q_n = g_n * rq
    d0, m0, d1, m1 = (np.ascontiguousarray(t.T) for t in _attn_tables(d))
    first, n_more, later = un.plan(lr, BWD_BATCH[d])
    bsz = un.count(first)
    gd = RES_PER_STEP if un.per_res else g_n

    def body(sl_ref, q_ref, k_ref, v_ref, do_ref, lse_ref, ds_ref, d0_ref, m0_ref, d1_ref, m1_ref, *rest):
        prev_ref = rest[0] if prev is not None else None
        out_ref, b0_ref, b1_ref, lt_ref, dt_ref, dk_ref, dv_ref = rest[-7:]
        hp = pl.program_id(1)
        for h in (0, 1):
            slope = sl_ref[2 * hp + h]
            b0_ref[:, h * q_n:(h + 1) * q_n] = m0_ref[...] - slope * d0_ref[...]
            b1_ref[:, h * q_n:(h + 1) * q_n] = m1_ref[...] - slope * d1_ref[...]
        if prev is None:
            dk_ref[...] = jnp.zeros(dk_ref.shape, F32)
            dv_ref[...] = jnp.zeros(dv_ref.shape, F32)
        else:
            out_ref[0] = prev_ref[0]
            dk_ref[...] = prev_ref[1].astype(F32)
            dv_ref[...] = prev_ref[2].astype(F32)
        low = lax.broadcasted_iota(jnp.int32, (1, q_n, LANES), 2) < HEAD_DIM
        row16 = pl.multiple_of(16 * hp, 16)

        def query_rows(stat_ref, t_ref, qs):
            tiles = un.load_q(stat_ref, qs)
            for b in range(bsz):
                t_ref[b] = tiles[b].T
            t16 = t_ref[:, pl.ds(row16, 16), :]
            return jnp.concatenate([t16[:, 0:1, :], t16[:, 8:9, :]], axis=2)

        def batch(qs, at_start):
            qq = _stack_heads(un.load_q(q_ref, qs) * 0.125, low)
            dd = _stack_heads(un.load_q(do_ref, qs), low)
            ks = un.load_k(k_ref, qs, at_start)
            vs = un.load_k(v_ref, qs, at_start)
            lrow = query_rows(lse_ref, lt_ref, qs)
            drow = query_rows(ds_ref, dt_ref, qs)
            pt = jnp.exp(_bdot(ks, qq, _BNT) + _batch_bias(un, qs, at_start, b0_ref, b1_ref) - lrow)
            dst = pt * (_bdot(vs, dd, _BNT) - drow)
            un.add_k(dv_ref, qs, _bdot(pt, dd, _BNN), at_start)
            un.add_k(dk_ref, qs, _bdot(dst, qq, _BNN), at_start)
            dq = _bdot(jnp.swapaxes(dst, 1, 2), ks, _BNN)
            un.store_q(out_ref, qs, jnp.where(low, dq[:, :q_n], dq[:, q_n:]) * 0.125, add=prev is not None,
                       lead=(0,))

        batch(first, True)

        def more(j, carry):
            batch(later(j), False)
            return carry

        lax.fori_loop(1, 1 + n_more, more, 0)
        out_ref[1] = dk_ref[...].astype(out_ref.dtype)
        out_ref[2] = dv_ref[...].astype(out_ref.dtype)

    pv = _pattern_view(proj, g_n)
    full = lambda a: pl.BlockSpec(a.shape, lambda r, hp: (0, 0))
    whole = _pattern_spec(g_n, lr, lambda hp: hp, lead=(3,))
    out = pl.pallas_call(
        body, grid=_pattern_grid(g_n),
        in_specs=[pl.BlockSpec(memory_space=pltpu.SMEM),
                  _pattern_spec(g_n, lr, lambda hp: 32 + hp),
                  _pattern_spec(g_n, lr, lambda hp: 40 + hp),
                  _pattern_spec(g_n, lr, lambda hp: 48 + hp),
                  _pattern_spec(g_n, lr, lambda hp: hp),
                  _pattern_spec(g_n, lr, lambda hp: 0),
                  _pattern_spec(g_n, lr, lambda hp: 0),
                  full(d0), full(m0), full(d1), full(m1)] + ([] if prev is None else [whole]),
        out_specs=whole,
        out_shape=jax.ShapeDtypeStruct(_pattern_view_shape(s_n, D_MODEL, g_n, lead=(3,)), ACT_DTYPE),
        scratch_shapes=[pltpu.VMEM((d0.shape[0], 2 * q_n), F32), pltpu.VMEM((2 * q_n, 2 * q_n), F32),
                        pltpu.VMEM((bsz, LANES, q_n), F32), pltpu.VMEM((bsz, LANES, q_n), F32),
                        pltpu.VMEM((gd, lr, LANES), F32), pltpu.VMEM((gd, lr, LANES), F32)],
        name=f"attn_bwd_d{d}",
    )(slopes, pv, pv, pv, _pattern_view(dout, g_n), _pattern_view(lse, g_n), _pattern_view(dsum, g_n),
      d0, m0, d1, m1, *([] if prev is None else [_pattern_view(prev, g_n, lead=(3,))]))
    return out.reshape(3, s_n, D_MODEL)


def _part_index(step, per, lo, n):
    return jnp.clip(step // per - lo, 0, n - 1)


def _dw_in(ut, da4, dc3, db3):
    s_n = ut.shape[1]
    tn = 512
    per = D_MODEL // tn
    shard_blocks = SHARD_COLS // tn

    def body(a_ref, p0_ref, p1_ref, p2_ref, o_ref):
        part = pl.program_id(0) // per

        @pl.when(part < 4)
        def _():
            o_ref[...] = _dot(a_ref[...], p0_ref[...])

        @pl.when((part >= 4) & (part < 7))
        def _():
            o_ref[...] = _dot(a_ref[...], p1_ref[...])

        @pl.when(part >= 7)
        def _():
            o_ref[...] = _dot(a_ref[...], p2_ref[...])

    def pspec(lo, n):
        return pl.BlockSpec((None, s_n, tn), lambda j: (_part_index(j, per, lo, n), 0, j % per))

    return pl.pallas_call(
        body, grid=(IN_COLS // tn,),
        in_specs=[pl.BlockSpec((D_MODEL, s_n), lambda j: (0, 0), pipeline_mode=pl.Buffered(1)),
                  pspec(0, 4), pspec(4, 3), pspec(7, 3)],
        out_specs=pl.BlockSpec((None, D_MODEL, tn), lambda j: (j // shard_blocks, 0, j % shard_blocks)),
        out_shape=jax.ShapeDtypeStruct((4, D_MODEL, SHARD_COLS), F32),
        name="dw_in",
    )(ut, da4, dc3, db3)


def _input_grad(da4, dc3, db3, w4, xp, norm_g, dh2, row0, rows):
    tm, tk = 256, 512
    per = D_MODEL // tk
    shard_blocks = SHARD_COLS // tk
    m0 = row0 // tm

    def body(p0_ref, p1_ref, p2_ref, w_ref, x_ref, g_ref, dh_ref, gx_ref, dg_ref):
        @pl.when(pl.program_id(0) == 0)
        def _():
            dg_ref[...] = jnp.zeros(dg_ref.shape, F32)

        du = None
        for k in range(IN_COLS // tk):
            part, cols = k // per, pl.ds((k % per) * tk, tk)
            ref, slot = (p0_ref, part) if part < 4 else (p1_ref, part - 4) if part < 7 else (p2_ref, part - 7)
            d = _dot_nt(ref[slot, :, cols], w_ref[k // shard_blocks, :, pl.ds((k % shard_blocks) * tk, tk)])
            du = d if du is None else du + d
        x = x_ref[...]
        r = lax.rsqrt(jnp.mean(x * x, axis=-1, keepdims=True) + EPS)
        nrm = x * r
        dg_ref[...] += jnp.sum(du * nrm, axis=0, keepdims=True)
        dn = du * g_ref[...]
        gx_ref[...] = dh_ref[...].astype(F32) + r * (dn - nrm * jnp.mean(dn * nrm, axis=-1, keepdims=True))

    def pspec(n):
        return pl.BlockSpec((n, tm, D_MODEL), lambda m: (0, m0 + m, 0))

    row_in = pl.BlockSpec((tm, D_MODEL), lambda m: (m0 + m, 0))
    vec = pl.BlockSpec((1, D_MODEL), lambda m: (0, 0))
    return pl.pallas_call(
        body, grid=(rows // tm,),
        in_specs=[pspec(4), pspec(3), pspec(3),
                  pl.BlockSpec(w4.shape, lambda m: (0, 0, 0), pipeline_mode=pl.Buffered(1)),
                  row_in, vec, row_in],
        out_specs=[pl.BlockSpec((tm, D_MODEL), lambda m: (m, 0)), vec],
        out_shape=[jax.ShapeDtypeStruct((rows, D_MODEL), F32), jax.ShapeDtypeStruct((1, D_MODEL), F32)],
        name="input_grad",
    )(da4, dc3, db3, w4, xp, norm_g, dh2)


class _Step:
    def __init__(self, x, tgt, norm_g, chip, after=0.0):
        self.norm_g, self.chip = norm_g, chip
        self.slopes = _alibi_slopes()
        self.xp, self.tp = _to_residue_major(x, tgt, after)
        self.u, self.ut = _rms_in(self.xp, norm_g)

    def project_own(self, w_own, after=None):
        self.proj_own = _in_proj(self.u, self.chip, w_own=w_own, after=after)

    def project(self, w4, others):
        self.proj_own = _in_proj(self.u, self.chip, w4=w4, partial=self.proj_own, others=others)

    def mixers(self, w4, taps):
        self.w4, self.taps, self.proj = w4, taps, self.proj_own
        self.hc, self.hct = _conv_fwd(self.proj, taps)
        fwd = [_attn_fwd(self.proj, self.slopes, d) for d in PATTERNS]
        self.o, self.lse, self.ha, self.hat = _attn_combine([f[0] for f in fwd], [f[1] for f in fwd], self.proj)

    def merge_and_loss(self, woc, woa, wo, b_merge, final_g):
        self.woc, self.woa, self.wo, self.b_merge = woc, woa, wo, b_merge
        (self.yc, self.ya, self.mgt, self.dh2b, self.d_final_g, self.loss8) = _merge_loss(
            self.hc, self.ha, woc, woa, wo, self.proj, b_merge, self.xp, final_g, self.tp)

    def out_weight_grads(self):
        (dyc, dya, self.dhc, self.dout, self.dsum, self.db3, self.d_bias) = _merge_bwd(
            self.dh2b, self.wo, self.woc, self.woa, self.yc, self.ya, self.proj, self.b_merge, self.o)
        d_wo = _mm_lhs_resident(self.mgt, self.dh2b, 256, "dw_o")
        d_woc = _mm_lhs_resident(self.hct, dyc, 256, "dw_out_conv")
        d_woa = _mm_lhs_resident(self.hat, dya, 256, "dw_out_attn")
        return d_woc, d_woa, d_wo

    def conv_grads(self, after=0.0):
        self.da4, self.d_taps = _conv_bwd(self.proj, self.taps + after, self.dhc)

    def in_weight_grad(self, after=0.0):
        slopes = self.slopes + after
        self.dc3 = None
        for d in PATTERNS:
            self.dc3 = _attn_bwd(self.proj, self.dout, self.lse, self.dsum, slopes, d, prev=self.dc3)
        return _dw_in(self.ut, self.da4, self.dc3, self.db3)

    def input_grad(self, half, after=0.0):
        rows = self.xp.shape[0] // 2
        return _input_grad(self.da4, self.dc3, self.db3, self.w4, self.xp, self.norm_g + after, self.dh2b,
                           half * rows, rows)


def _local_grads(x, tgt, norm_g, w4, b_merge, conv_w, woc, woa, wo, final_g):
    st = _Step(x, tgt, norm_g, jnp.zeros((1,), jnp.int32))
    st.project_own(w4[0])
    st.project(w4, (2, 1))
    st.project(w4, (3,))
    st.mixers(w4, conv_w)
    st.merge_and_loss(woc, woa, wo, b_merge, final_g)
    d_woc, d_woa, d_wo = st.out_weight_grads()
    st.conv_grads()
    d_w4 = st.in_weight_grad()
    gx_lo, dg_lo = st.input_grad(0)
    gx_hi, dg_hi = st.input_grad(1)
    return (st.loss8, _to_natural(gx_lo, gx_hi), dg_lo + dg_hi, d_w4, st.d_bias, st.d_taps, d_woc, d_woa, d_wo,
            st.d_final_g)


MESH = pl.DeviceIdType.MESH
_CHIP_FLIPS = ((1, 0), (0, 1), (1, 1))
_ANY = pl.BlockSpec(memory_space=pl.ANY)


def _place():
    return lax.axis_index("x"), lax.axis_index("y"), lax.axis_index("c")


def _flip(v, f):
    return 1 - v if f else v


def _remote(src, dst, send_sems, recv_sems, k, device):
    return pltpu.make_async_remote_copy(src_ref=src, dst_ref=dst, send_sem=send_sems.at[k], recv_sem=recv_sems.at[k],
                                        device_id=device, device_id_type=MESH)


def _place_shard(w, chip):
    rows, cols = w.shape
    tm = 128

    def body(chip_ref, w_ref, o_ref):
        o_ref[0] = w_ref[...].astype(o_ref.dtype)

    return pl.pallas_call(
        body,
        grid_spec=pltpu.PrefetchScalarGridSpec(
            num_scalar_prefetch=1, grid=(rows // tm,),
            in_specs=[pl.BlockSpec((tm, cols), lambda i, chip_ref: (i, 0))],
            out_specs=pl.BlockSpec((1, tm, cols), lambda i, chip_ref: (chip_ref[0], i, 0))),
        out_shape=jax.ShapeDtypeStruct((4, rows, cols), MXU_DTYPE),
        name="place_shard",
    )(chip, w)


def _gather_copies_to(flips):
    def copies(arrs, _, send_sems, recv_sems):
        x, y, c = _place()
        out = []
        for a, arr in enumerate(arrs):
            h = arr.shape[1] // 2
            mine = arr.at[2 * x + y, pl.ds(pl.multiple_of(c * h, 8), h)]
            for i, t in enumerate(flips):
                fx, fy = _CHIP_FLIPS[t]
                out.append(_remote(mine, mine, send_sems, recv_sems, len(flips) * a + i,
                                   (_flip(x, fx), _flip(y, fy), c)))
        return out
    return copies


def _forward_to_sibling(arrs, flips=(0, 1, 2)):
    n = len(arrs)

    def body(*refs):
        outs = refs[n:2 * n]
        send_sems, recv_sems = refs[2 * n:]
        x, y, c = _place()
        sibling = (x, y, 1 - c)
        started = []
        for a in range(n):
            h = outs[a].shape[1] // 2
            rows = pl.ds(pl.multiple_of(c * h, 8), h)
            for t in flips:
                fx, fy = _CHIP_FLIPS[t]
                landed = outs[a].at[2 * _flip(x, fx) + _flip(y, fy), rows]
                cp = _remote(landed, landed, send_sems, recv_sems, 3 * a + t, sibling)
                cp.start()
                started.append(cp)
        for a in range(n):
            h = outs[a].shape[1] // 2
            rows = pl.ds(pl.multiple_of((1 - c) * h, 8), h)
            for t in flips:
                fx, fy = _CHIP_FLIPS[t]
                handed = outs[a].at[2 * _flip(x, fx) + _flip(y, fy), rows]
                _remote(handed, handed, send_sems, recv_sems, 3 * a + t, sibling).wait_recv()
        for cp in started:
            cp.wait_send()

    return pl.pallas_call(
        body, in_specs=[_ANY] * n, out_specs=[_ANY] * n,
        out_shape=[jax.ShapeDtypeStruct(s.shape, s.dtype) for s in arrs],
        input_output_aliases={a: a for a in range(n)},
        scratch_shapes=[pltpu.SemaphoreType.DMA((3 * n,)), pltpu.SemaphoreType.DMA((3 * n,))],
        name="gathered_to_sibling_" + "".join(str(t) for t in flips),
    )(*arrs)


_HBM = pl.BlockSpec(memory_space=pltpu.HBM)
_SEM = pl.BlockSpec(memory_space=pltpu.SEMAPHORE)
_EFFECT = pltpu.SideEffectType.DATAFLOW_SIDE_EFFECTING


class _SplitExchange:
    def __init__(self, name, srcs, land_shapes, n_copies, copies, riders=()):
        self.name, self.n, self.nl, self.copies = name, len(srcs), len(land_shapes), copies
        n, nb = self.n, len(srcs) + len(land_shapes)
        lands = [lax.empty(s.shape, s.dtype) for s in land_shapes]
        bufs = [pltpu.with_memory_space_constraint(a, pltpu.HBM) for a in (*srcs, *lands, *riders)]
        na = len(bufs)

        def body(*refs):
            send_sems, recv_sems = refs[na], refs[na + 1]
            for cp in copies(refs[:n], refs[n:nb], send_sems, recv_sems):
                cp.start()
            refs[-1][...] = jnp.zeros(refs[-1].shape, F32)

        outs = pl.pallas_call(
            body, name=name + "_start",
            in_specs=[_HBM] * na,
            out_specs=[_SEM, _SEM] + [_HBM] * na + [pl.BlockSpec(memory_space=pltpu.VMEM)],
            out_shape=[pltpu.SemaphoreType.DMA((n_copies,)), pltpu.SemaphoreType.DMA((n_copies,))]
            + [pltpu.HBM(b.shape, b.dtype) for b in bufs] + [jax.ShapeDtypeStruct((8, LANES), F32)],
            input_output_aliases={i: 2 + i for i in range(na)},
            compiler_params=pltpu.CompilerParams(has_side_effects=_EFFECT),
        )(*bufs)
        self.sems, self.bufs, self.riders, self.token = outs[:2], outs[2:2 + nb], outs[2 + nb:2 + na], outs[-1]

    def after(self):
        return self.token[0, 0]

    def wait(self, done, riders=(), bufs=None):
        n, nb, copies = self.n, self.n + self.nl, self.copies
        bufs = [*(self.bufs if bufs is None else bufs),
                *[pltpu.with_memory_space_constraint(a, pltpu.HBM) for a in riders]]
        na = len(bufs)
        done = list(done) if isinstance(done, (list, tuple)) else [done]

        def body(*refs):
            send_sems, recv_sems = refs[na], refs[na + 1]
            for cp in copies(refs[:n], refs[n:nb], send_sems, recv_sems):
                cp.wait_send()
                cp.wait_recv()

        outs = pl.pallas_call(
            body, name=self.name + "_wait",
            in_specs=[_HBM] * na + [_SEM, _SEM] + [_ANY] * len(done),
            out_specs=[_HBM] * na,
            out_shape=[pltpu.HBM(b.shape, b.dtype) for b in bufs],
            input_output_aliases={i: i for i in range(na)},
            compiler_params=pltpu.CompilerParams(has_side_effects=_EFFECT),
        )(*bufs, *self.sems, *done)
        return outs[:n], outs[n:nb], outs[nb:]


def _sibling_copies(srcs, lands, send_sems, recv_sems):
    x, y, c = _place()
    out = []
    for a, (src, land) in enumerate(zip(srcs, lands)):
        h = src.shape[1] // 2
        theirs = pl.ds(pl.multiple_of((1 - c) * h, 8), h)
        out.append(_remote(src.at[:, theirs], land, send_sems, recv_sems, a, (x, y, 1 - c)))
    return out


def _grads_to_sibling(name, grads):
    shapes = [jax.ShapeDtypeStruct((4, g.shape[1] // 2, g.shape[2]), g.dtype) for g in grads]
    return _SplitExchange(name, grads, shapes, len(grads), _sibling_copies)


def _chip_copies(srcs, lands, send_sems, recv_sems):
    x, y, c = _place()
    out = []
    for a, (src, land) in enumerate(zip(srcs, lands)):
        for t, (fx, fy) in enumerate(_CHIP_FLIPS):
            tx, ty = _flip(x, fx), _flip(y, fy)
            out.append(_remote(src.at[2 * tx + ty], land.at[t], send_sems, recv_sems, 3 * a + t, (tx, ty, c)))
    return out


def _grads_to_chips(name, parts):
    shapes = [jax.ShapeDtypeStruct((3, *p.shape[1:]), p.dtype) for p in parts]
    return _SplitExchange(name, parts, shapes, 3 * len(parts), _chip_copies)


def _add_halves(g, r, half):
    _, rows, cols = g.shape
    h = rows // 2
    tm = min(h, 128)
    nt = h // tm

    def body(half_ref, g_ref, r_ref, b_ref):
        b_ref[...] = (g_ref[...] + r_ref[...]).astype(b_ref.dtype)

    spec = pl.BlockSpec((1, tm, cols), lambda j, i, half_ref: (j, i, 0))
    return pl.pallas_call(
        body,
        grid_spec=pltpu.PrefetchScalarGridSpec(
            num_scalar_prefetch=1, grid=(4, nt),
            in_specs=[pl.BlockSpec((1, tm, cols), lambda j, i, half_ref: (j, half_ref[0] * nt + i, 0)), spec],
            out_specs=spec),
        out_shape=jax.ShapeDtypeStruct((4, h, cols), BF16),
        name="add_sibling_grads",
    )(half, g, r)


def _add_chips(g, r, recv, where):
    _, h, cols = r.shape
    tm = min(h, 128)
    nt = h // tm

    def body(where_ref, g_ref, r_ref, recv_ref, out_ref):
        own = g_ref[0] + r_ref[0]
        out_ref[...] = ((own + recv_ref[0].astype(F32)) + recv_ref[1].astype(F32)) + recv_ref[2].astype(F32)

    return pl.pallas_call(
        body,
        grid_spec=pltpu.PrefetchScalarGridSpec(
            num_scalar_prefetch=1, grid=(nt,),
            in_specs=[pl.BlockSpec((1, tm, cols), lambda i, w: (w[0], w[1] * nt + i, 0)),
                      pl.BlockSpec((1, tm, cols), lambda i, w: (w[0], i, 0)),
                      pl.BlockSpec((3, tm, cols), lambda i, w: (0, i, 0))],
            out_specs=pl.BlockSpec((tm, cols), lambda i, w: (w[1] * nt + i, 0))),
        out_shape=jax.ShapeDtypeStruct((2 * h, cols), F32),
        name="add_chip_grads",
    )(where, g, r, recv)


def _share_halves(shards):
    n = len(shards)

    def body(*refs):
        outs = refs[n:2 * n]
        send_sems, recv_sems = refs[2 * n:]
        x, y, c = _place()
        copies = []
        for a in range(n):
            h = outs[a].shape[0] // 2
            mine = outs[a].at[pl.ds(pl.multiple_of(c * h, 8), h)]
            copies.append(_remote(mine, mine, send_sems, recv_sems, a, (x, y, 1 - c)))
        for cp in copies:
            cp.start()
        for a, cp in enumerate(copies):
            cp.wait_send()
            h = outs[a].shape[0] // 2
            theirs = outs[a].at[pl.ds(pl.multiple_of((1 - c) * h, 8), h)]
            _remote(theirs, theirs, send_sems, recv_sems, a, (x, y, 1 - c)).wait_recv()

    return pl.pallas_call(
        body, in_specs=[_ANY] * n, out_specs=[_ANY] * n,
        out_shape=[jax.ShapeDtypeStruct(p.shape, p.dtype) for p in shards],
        input_output_aliases={a: a for a in range(n)},
        scratch_shapes=[pltpu.SemaphoreType.DMA((n,)), pltpu.SemaphoreType.DMA((n,))],
        name="share_reduced_halves",
    )(*shards)


def _exchange_small(rows, reduce):
    cols = rows[0].shape[1]
    n = len(rows)
    assert sum(r.shape[0] for r in rows) <= 8

    def body(*refs):
        ins, out_ref = refs[:n], refs[n]
        vec_ref, gath_ref, send_sems, recv_sems = refs[n + 1:]
        x, y, c = _place()
        me = 4 * x + 2 * y + c
        vec_ref[...] = jnp.zeros(vec_ref.shape, F32)
        at = 0
        for r in ins:
            vec_ref[at:at + r.shape[0], :] = r[...]
            at += r.shape[0]
        copies = []
        for k in range(1, 8):
            peer = (_flip(x, (k >> 2) & 1), _flip(y, (k >> 1) & 1), _flip(c, k & 1))
            copies.append(_remote(vec_ref, gath_ref.at[me], send_sems, recv_sems, k - 1, peer))
        for cp in copies:
            cp.start()
        gath_ref[me] = vec_ref[...]
        for cp in copies:
            cp.wait()
        if reduce:
            tot = gath_ref[0]
            for dev in range(1, 8):
                tot = tot + gath_ref[dev]
            out_ref[...] = tot
            out_ref[7:8, :] = jnp.zeros((1, cols), F32) + jnp.sum(tot[7:8, :])
        else:
            out_ref[...] = gath_ref[...]

    vm = pl.BlockSpec(memory_space=pltpu.VMEM)
    return pl.pallas_call(
        body, in_specs=[vm] * n, out_specs=vm,
        out_shape=jax.ShapeDtypeStruct((8, cols) if reduce else (8, 8, cols), F32),
        scratch_shapes=[pltpu.VMEM((8, cols), F32), pltpu.VMEM((8, 8, cols), F32),
                        pltpu.SemaphoreType.DMA((7,)), pltpu.SemaphoreType.DMA((7,))],
        name="reduce_small" if reduce else "gather_small",
    )(*rows)


def _adamw(w, g, m, v, name):
    rows, cols = w.shape
    tm = 128 if rows % 128 == 0 else rows

    def body(w_ref, g_ref, m_ref, v_ref, d_ref, m2_ref, v2_ref, gout_ref):
        gr = g_ref[...]
        m2 = ADAM_B1 * m_ref[...] + (1.0 - ADAM_B1) * gr
        v2 = ADAM_B2 * v_ref[...] + (1.0 - ADAM_B2) * (gr * gr)
        m_hat = m2 / (1.0 - ADAM_B1 ** ADAM_STEP)
        v_hat = v2 / (1.0 - ADAM_B2 ** ADAM_STEP)
        d_ref[...] = -ADAM_LR * (m_hat / (jnp.sqrt(v_hat) + ADAM_EPS) + ADAM_WD * w_ref[...])
        m2_ref[...] = m2
        v2_ref[...] = v2
        gout_ref[...] = gr

    spec = pl.BlockSpec((tm, cols), lambda i: (i, 0))
    sds = jax.ShapeDtypeStruct((rows, cols), F32)
    return pl.pallas_call(body, grid=(rows // tm,), in_specs=[spec] * 4, out_specs=[spec] * 4,
                          out_shape=[sds] * 4, name=name)(w, g, m, v)


def kernel(x, norm_g, w_in, b_merge, conv_w, w_out_conv, w_out_attn, w_o, final_g, loss_target, m_norm_g, m_w_in, m_b_merge, m_conv_w, m_w_out_conv, m_w_out_attn, m_w_o, m_final_g, v_norm_g, v_w_in, v_b_merge, v_conv_w, v_w_out_conv, v_w_out_attn, v_w_o, v_final_g):
    mx, my, mc = _place()
    chip = (2 * mx + my).astype(jnp.int32)
    seq = x.shape[1]

    chip1 = chip.reshape(1)
    slots = [_place_shard(w[0], chip1) for w in (w_in, w_out_conv, w_out_attn, w_o)]
    taps8 = _exchange_small([conv_w[0]], reduce=False)
    taps = jnp.concatenate([taps8[2 * j, :3, :] for j in range(4)], axis=1)
    gather_near = _SplitExchange("gather_w_in_near", slots[:1], [], 2, _gather_copies_to((0, 1)))
    st = _Step(x[0], loss_target[0], norm_g, chip1, after=gather_near.after())
    w4, _, _ = gather_near.wait([st.ut, taps8])
    gather_far = _SplitExchange("gather_w_in_far", w4, [], 1, _gather_copies_to((2,)))
    st.project_own(w_in[0], after=gather_far.token)
    (w4,) = _forward_to_sibling(gather_far.bufs, flips=(0, 1))
    st.project(w4, (2, 1))
    (w4,), _, out_slots = gather_far.wait([st.proj_own], riders=slots[1:], bufs=[w4])
    gather_out = _SplitExchange("gather_w_out", out_slots, [], 9, _gather_copies_to((0, 1, 2)), riders=[w4])
    (w4,) = _forward_to_sibling(gather_out.riders, flips=(2,))
    st.project(w4, (3,))
    st.mixers(w4, taps)
    out_ws, _, _ = gather_out.wait(st.o)
    woc, woa, wo = [w.reshape(D_MODEL, D_MODEL) for w in _forward_to_sibling(out_ws)]
    st.merge_and_loss(woc, woa, wo, b_merge, final_g.reshape(1, D_MODEL))

    half = mc.astype(jnp.int32).reshape(1)
    where = jnp.stack([chip, mc.astype(jnp.int32)])
    out_grads = [g.reshape(4, -1, D_MODEL) for g in st.out_weight_grads()]
    to_sibling = _grads_to_sibling("out_grads_to_sibling", out_grads)
    st.conv_grads(after=to_sibling.after())
    out_grads, out_from_sibling, _ = to_sibling.wait(st.da4)
    to_chips = _grads_to_chips("out_grads_to_chips",
                               [_add_halves(g, r, half) for g, r in zip(out_grads, out_from_sibling)])
    d_w4 = st.in_weight_grad(after=to_chips.after())
    out_from_chips = to_chips.wait(st.dc3)[1]

    to_sibling = _grads_to_sibling("in_grad_to_sibling", [d_w4])
    gx_lo, dg_lo = st.input_grad(0, after=to_sibling.after())
    (d_w4,), (from_sibling,), _ = to_sibling.wait(gx_lo)
    to_chips = _grads_to_chips("in_grad_to_chips", [_add_halves(d_w4, from_sibling, half)])
    gx_hi, dg_hi = st.input_grad(1, after=to_chips.after())
    grad_x = _to_natural(gx_lo, gx_hi)

    where_late = where + to_chips.after().astype(jnp.int32)
    out_reduced = [_add_chips(g, r, recv, where_late)
                   for g, r, recv in zip(out_grads, out_from_sibling, out_from_chips)]
    g_woc, g_woa, g_wo = _share_halves(out_reduced)
    small = _exchange_small([dg_lo + dg_hi, st.d_bias.reshape(2, D_MODEL), st.d_taps, st.d_final_g,
                             st.loss8.reshape(1, D_MODEL)], reduce=True)
    loss = (0.5 / D_MODEL) * small[7, 0]
    g_taps = lax.dynamic_slice(small[3:6], (0, chip * (D_MODEL // 4)), (3, D_MODEL // 4))
    upd = {
        "norm_g": _adamw(norm_g, small[0:1], m_norm_g, v_norm_g, "adamw_norm_g"),
        "b_merge": _adamw(b_merge, small[1:3].reshape(1, 2 * D_MODEL), m_b_merge, v_b_merge, "adamw_b_merge"),
        "conv_w": _adamw(conv_w[0], g_taps, m_conv_w[0], v_conv_w[0], "adamw_conv_w"),
        "w_out_conv": _adamw(w_out_conv[0], g_woc, m_w_out_conv[0], v_w_out_conv[0], "adamw_w_out_conv"),
        "w_out_attn": _adamw(w_out_attn[0], g_woa, m_w_out_attn[0], v_w_out_attn[0], "adamw_w_out_attn"),
        "w_o": _adamw(w_o[0], g_wo, m_w_o[0], v_w_o[0], "adamw_w_o"),
        "final_g": _adamw(final_g.reshape(1, D_MODEL), small[6:7], m_final_g.reshape(1, D_MODEL),
                          v_final_g.reshape(1, D_MODEL), "adamw_final_g"),
    }
    behind = [grad_x] + [u[0] for u in upd.values()]
    in_reduced = _add_chips(d_w4, from_sibling, to_chips.wait(behind)[1][0], where)
    (g_w_in,) = _share_halves([in_reduced])
    upd["w_in"] = _adamw(w_in[0], g_w_in, m_w_in[0], v_w_in[0], "adamw_w_in")

    names = ["norm_g", "w_in", "b_merge", "conv_w", "w_out_conv", "w_out_attn", "w_o", "final_g"]
    shapes = [norm_g.shape, w_in.shape, b_merge.shape, conv_w.shape, w_out_conv.shape, w_out_attn.shape,
              w_o.shape, final_g.shape]
    outs = [loss, grad_x.reshape(1, seq, D_MODEL)]
    for k in (3, 0, 1, 2):
        outs += [upd[n][k].reshape(s) for n, s in zip(names, shapes)]
    return tuple(outs)
```

```python
import functools

import numpy as np
import jax
import jax.numpy as jnp
from jax import lax
from jax.experimental import pallas as pl
from jax.experimental.pallas import tpu as pltpu

F32 = jnp.float32
BF16 = jnp.bfloat16
MXU_DTYPE = jnp.bfloat16
ACT_DTYPE = jnp.bfloat16

D_MODEL = 1024
N_HEADS = 16
HEAD_DIM = 64
QB = 128
N_RES = 16
LANES = 128
HP = N_HEADS * HEAD_DIM // LANES
IN_COLS = 10 * D_MODEL
SHARD_COLS = IN_COLS // 4
EPS = 1e-6
NEG = -1e30

ADAM_LR, ADAM_B1, ADAM_B2, ADAM_EPS, ADAM_WD, ADAM_STEP = 0.001, 0.9, 0.999, 1e-08, 0.01, 10

PATTERNS = {1: (16, 8), 4: (4, 32), 16: (1, 128)}

_NN = (((1,), (0,)), ((), ()))
_NT = (((1,), (1,)), ((), ()))


def _dot(a, b):
    return lax.dot_general(a.astype(MXU_DTYPE), b.astype(MXU_DTYPE), _NN, preferred_element_type=F32)


def _dot_nt(a, b):
    return lax.dot_general(a.astype(MXU_DTYPE), b.astype(MXU_DTYPE), _NT, preferred_element_type=F32)


def _split3(x):
    hi = x.astype(BF16)
    r1 = x - hi.astype(F32)
    mid = r1.astype(BF16)
    lo = (r1 - mid.astype(F32)).astype(BF16)
    return hi, mid, lo


def _select_rows(sel, x):
    return sum(lax.dot_general(sel, t, _NN, preferred_element_type=F32) for t in _split3(x))


def _select_cols(x, sel, terms=3):
    return sum(lax.dot_general(t, sel, _NN, preferred_element_type=F32) for t in _split3(x)[:terms])


def _sigmoid(z):
    return 1.0 / (1.0 + jnp.exp(-z))


def _perm_matrix():
    idx = np.arange(256)
    p = np.zeros((256, 256), np.float32)
    p[(idx % 16) * 16 + idx // 16, idx] = 1.0
    return jnp.asarray(p, BF16)


def _head_expand_matrix():
    e = np.zeros((LANES, D_MODEL), np.float32)
    for h in range(N_HEADS):
        e[8 * h, HEAD_DIM * h:HEAD_DIM * (h + 1)] = 1.0
    return jnp.asarray(e, BF16)


def _head_sum_matrix():
    e = np.zeros((D_MODEL, LANES), np.float32)
    for h in range(N_HEADS):
        e[HEAD_DIM * h:HEAD_DIM * (h + 1), 8 * h:8 * (h + 1)] = 1.0
    return jnp.asarray(e, BF16)


def _attn_tables(d):
    g_n, rq = PATTERNS[d]
    q_n = g_n * rq
    gq, iq = np.arange(q_n) // rq, np.arange(q_n) % rq

    def tab(kn, base):
        k_n = g_n * kn
        gk, jk = np.arange(k_n) // kn, np.arange(k_n) % kn
        delta = g_n * (base + iq[:, None] - jk[None, :]) + gq[:, None] - gk[None, :]
        valid = (delta >= 0) & (delta <= QB)
        dist = np.where(valid, d * delta, 0).astype(np.float32)
        madd = np.where(valid, 0.0, NEG).astype(np.float32)
        return dist, madd

    d0, m0 = tab(rq if g_n == 1 else 2 * rq, 0)
    d1, m1 = tab(2 * rq, rq)
    return d0, m0, d1, m1


def _alibi_slopes():
    return jnp.exp2(-8.0 * jnp.arange(1, N_HEADS + 1, dtype=F32) / N_HEADS)


def _to_residue_major(x, tgt, after=0.0):
    s_n, c_n = x.shape
    lr = s_n // N_RES
    pm = (_perm_matrix().astype(F32) + after).astype(BF16)

    def body(p_ref, x_ref, t_ref, xo_ref, to_ref):
        pm = p_ref[...]
        xo_ref[...] = _select_rows(pm, x_ref[...]).reshape(16, 16, c_n)
        to_ref[...] = _select_rows(pm, t_ref[...]).reshape(16, 16, c_n)

    nat = pl.BlockSpec((256, c_n), lambda i: (i, 0))
    res = pl.BlockSpec((16, 16, c_n), lambda i: (0, i, 0))
    xo, to = pl.pallas_call(
        body, grid=(s_n // 256,),
        in_specs=[pl.BlockSpec((256, 256), lambda i: (0, 0)), nat, nat],
        out_specs=[res, res],
        out_shape=[jax.ShapeDtypeStruct((16, lr, c_n), F32)] * 2,
        name="perm_in",
    )(pm, x, tgt)
    return xo.reshape(s_n, c_n), to.reshape(s_n, c_n)


def _to_natural(gx_lo, gx_hi):
    half_rows, c_n = gx_lo.shape
    lr = half_rows // (N_RES // 2)

    def body(p_ref, lo_ref, hi_ref, o_ref):
        g = jnp.concatenate([lo_ref[...], hi_ref[...]], axis=0)
        o_ref[...] = _select_rows(p_ref[...], g.reshape(256, c_n))

    half = pl.BlockSpec((8, 16, c_n), lambda i: (0, i, 0))
    return pl.pallas_call(
        body, grid=(lr // 16,),
        in_specs=[pl.BlockSpec((256, 256), lambda i: (0, 0)), half, half],
        out_specs=pl.BlockSpec((256, c_n), lambda i: (i, 0)),
        out_shape=jax.ShapeDtypeStruct((2 * half_rows, c_n), F32),
        name="perm_out",
    )(_perm_matrix(), gx_lo.reshape(8, lr, c_n), gx_hi.reshape(8, lr, c_n))


def _rms_in(xp, norm_g):
    s_n, c_n = xp.shape
    tm = 512

    def body(x_ref, g_ref, u_ref, ut_ref):
        x = x_ref[...]
        r = lax.rsqrt(jnp.mean(x * x, axis=-1, keepdims=True) + EPS)
        u = x * r * g_ref[...]
        u_ref[...] = u.astype(u_ref.dtype)
        ut_ref[...] = u.T.astype(ut_ref.dtype)

    return pl.pallas_call(
        body, grid=(s_n // tm,),
        in_specs=[pl.BlockSpec((tm, c_n), lambda i: (i, 0)), pl.BlockSpec((1, c_n), lambda i: (0, 0))],
        out_specs=[pl.BlockSpec((tm, c_n), lambda i: (i, 0)), pl.BlockSpec((c_n, tm), lambda i: (0, i))],
        out_shape=[jax.ShapeDtypeStruct((s_n, c_n), ACT_DTYPE), jax.ShapeDtypeStruct((c_n, s_n), ACT_DTYPE)],
        name="rms_in",
    )(xp, norm_g)


def _in_proj(u, chip, w_own=None, w4=None, partial=None, others=(), after=None):
    s_n = u.shape[0]
    tn, cm = 512, 512
    per = SHARD_COLS // tn
    own = partial is None
    extra = [] if after is None else [after]

    def body(chip_ref, a_ref, b_ref, *rest):
        o_ref = rest[-1]
        b = b_ref[...]
        for c in range(s_n // cm):
            o_ref[c * cm:(c + 1) * cm, :] = _dot(a_ref[c * cm:(c + 1) * cm, :], b).astype(o_ref.dtype)

    def shard(n, chip_ref):
        if own:
            return chip_ref[0]
        mask = others[-1]
        for i, m in enumerate(others[:-1]):
            mask = jnp.where(n // per == i, m, mask)
        return jnp.bitwise_xor(chip_ref[0], mask)

    w_spec = (pl.BlockSpec((D_MODEL, tn), lambda n, c: (0, n)) if own else
              pl.BlockSpec((None, D_MODEL, tn), lambda n, c: (shard(n, c), 0, n % per)))
    return pl.pallas_call(
        body,
        grid_spec=pltpu.PrefetchScalarGridSpec(
            num_scalar_prefetch=1, grid=(per if own else len(others) * per,),
            in_specs=[pl.BlockSpec((s_n, D_MODEL), lambda n, c: (0, 0)), w_spec] + ([] if own else [_ANY])
            + [pl.BlockSpec((8, LANES), lambda n, c: (0, 0))] * len(extra),
            out_specs=pl.BlockSpec((s_n, tn), lambda n, c: (0, shard(n, c) * per + n % per))),
        out_shape=jax.ShapeDtypeStruct((s_n, IN_COLS), ACT_DTYPE),
        input_output_aliases={} if own else {3: 0},
        name="in_proj_own" if own else "in_proj_" + "_".join(str(m) for m in others),
    )(*([chip, u, w_own] if own else [chip, u, w4, partial]), *extra)


def _conv_terms(xc_ref, cg_ref, r, row, lr, cache):
    def a_of(q):
        if q not in cache:
            cache[q] = cg_ref[q].astype(F32) * xc_ref[q].astype(F32)
        return cache[q]

    def shift_down(v):
        return jnp.where(row >= 1, pltpu.roll(v, 1, 0), 0.0)

    a = a_of(r)
    am1 = a_of(r - 1) if r >= 1 else shift_down(a_of(N_RES - 1))
    am2 = a_of(r - 2) if r >= 2 else shift_down(a_of(N_RES - 2 + r))
    return a, am1, am2


def _conv_fwd(proj, conv_w):
    s_n = proj.shape[0]
    lr = s_n // N_RES
    pv = proj.reshape(N_RES, lr, IN_COLS)

    def body(xc_ref, bg_ref, cg_ref, zc_ref, w_ref, hc_ref, hct_ref):
        w = w_ref[...]
        row = lax.broadcasted_iota(jnp.int32, (lr, LANES), 0)
        products = {}
        for r in range(N_RES):
            a, am1, am2 = _conv_terms(xc_ref, cg_ref, r, row, lr, products)
            c = w[0:1] * am2 + w[1:2] * am1 + w[2:3] * a
            z = zc_ref[r].astype(F32)
            hc = z * _sigmoid(z) * bg_ref[r].astype(F32) * c
            hc_ref[r] = hc.astype(hc_ref.dtype)
            hct_ref[:, r * lr:(r + 1) * lr] = hc.T.astype(hct_ref.dtype)

    def col(part):
        return pl.BlockSpec((N_RES, lr, LANES), lambda j: (0, 0, part * 8 + j))

    hc, hct = pl.pallas_call(
        body, grid=(D_MODEL // LANES,),
        in_specs=[col(0), col(1), col(2), col(3), pl.BlockSpec((3, LANES), lambda j: (0, j))],
        out_specs=[pl.BlockSpec((N_RES, lr, LANES), lambda j: (0, 0, j)),
                   pl.BlockSpec((LANES, s_n), lambda j: (j, 0))],
        out_shape=[jax.ShapeDtypeStruct((N_RES, lr, D_MODEL), ACT_DTYPE),
                   jax.ShapeDtypeStruct((D_MODEL, s_n), ACT_DTYPE)],
        name="conv_fwd",
    )(pv, pv, pv, pv, conv_w)
    return hc.reshape(s_n, D_MODEL), hct


RES_PER_STEP = 8
FWD_BATCH = {1: 8, 4: 8, 16: RES_PER_STEP}
BWD_BATCH = {1: 8, 4: 8, 16: RES_PER_STEP}

_BNT = (((2,), (2,)), ((0,), (0,)))
_BNN = (((2,), (1,)), ((0,), (0,)))


def _bdot(a, b, dims):
    return lax.dot_general(a.astype(MXU_DTYPE), b.astype(MXU_DTYPE), dims, preferred_element_type=F32)


def _pattern_view_shape(s_n, c_n, g_n, lead=()):
    lr = s_n // N_RES
    return (*lead, 4, 4, lr, c_n) if g_n == 4 else (*lead, N_RES, lr, c_n)


def _pattern_view(a, g_n, lead=()):
    return a.reshape(_pattern_view_shape(a.shape[-2], a.shape[-1], g_n, lead))


def _pattern_grid(g_n):
    return (N_RES // RES_PER_STEP if g_n == 1 else N_RES // g_n, HP)


def _pattern_spec(g_n, lr, col_of_hp, lead=()):
    z = (0,) * len(lead)
    if g_n == 16:
        return pl.BlockSpec((*lead, 16, lr, LANES), lambda r, hp: (*z, 0, 0, col_of_hp(hp)))
    if g_n == 4:
        return pl.BlockSpec((*lead, 4, None, lr, LANES), lambda r, hp: (*z, 0, r, 0, col_of_hp(hp)))
    return pl.BlockSpec((*lead, RES_PER_STEP, lr, LANES), lambda r, hp: (*z, r, 0, col_of_hp(hp)))


def _aligned(start, m):
    return start if isinstance(start, int) else pl.multiple_of(start, m)


class _Units:
    def __init__(self, g_n, rq):
        self.g_n, self.rq = g_n, rq
        self.per_res, self.paired = g_n == 1, rq == 8

    def plan(self, lr, size):
        if self.per_res:
            return [0], lr // self.rq - 1, lambda j: [pl.multiple_of(j * self.rq, self.rq)]
        step = 16 if self.paired else self.rq
        per = min(size // 2 if self.paired else size, lr // step)
        assert (lr // step) % per == 0
        return ([i * step for i in range(per)], lr // step // per - 1,
                lambda j: [pl.multiple_of((j * per + i) * step, step) for i in range(per)])

    def count(self, qs):
        return RES_PER_STEP if self.per_res else len(qs) * (2 if self.paired else 1)

    def _split(self, tiles, lo, rows):
        return tiles[:, lo:lo + rows].reshape(self.g_n * rows, LANES)

    def load_q(self, ref, qs):
        rq = self.rq
        if self.per_res:
            return ref[:, pl.ds(qs[0], rq), :]
        if self.paired:
            tiles = [ref[:, pl.ds(q, 16), :].astype(F32) for q in qs]
            return jnp.stack([self._split(t, lo, 8) for t in tiles for lo in (0, 8)])
        return jnp.stack([ref[:, pl.ds(q, rq), :].reshape(self.g_n * rq, LANES) for q in qs])

    def _key_rows(self, q, at_start):
        return (0, 2 * self.rq) if at_start else (_aligned(q - self.rq, self.rq), 2 * self.rq)

    def load_k(self, ref, qs, first):
        rq = self.rq
        if self.per_res:
            return ref[:, pl.ds(0, rq), :] if first else ref[:, pl.ds(_aligned(qs[0] - rq, rq), 2 * rq), :]
        if self.paired:
            out = []
            for i, q in enumerate(qs):
                if first and i == 0:
                    t = ref[:, 0:16, :].astype(F32)
                    out += [self._split(t, 0, 16)] * 2
                else:
                    t = ref[:, pl.ds(_aligned(q - 16, 16), 32), :].astype(F32)
                    out += [self._split(t, 8, 16), self._split(t, 16, 16)]
            return jnp.stack(out)
        rows = [self._key_rows(q, first and i == 0) for i, q in enumerate(qs)]
        return jnp.stack([ref[:, pl.ds(k0, n), :].reshape(self.g_n * n, LANES) for k0, n in rows])

    def store_q(self, ref, qs, val, add=False, lead=()):
        if self.per_res:
            pieces = [(qs[0], self.rq, val)]
        elif self.paired:
            pieces = [(q, 16, jnp.concatenate([val[2 * i].reshape(self.g_n, 8, LANES),
                                               val[2 * i + 1].reshape(self.g_n, 8, LANES)], axis=1))
                      for i, q in enumerate(qs)]
        else:
            pieces = [(q, self.rq, val[i].reshape(self.g_n, self.rq, LANES)) for i, q in enumerate(qs)]
        for start, rows, v in pieces:
            idx = (*lead, slice(None), pl.ds(start, rows), slice(None))
            ref[idx] = (ref[idx] + v if add else v).astype(ref.dtype)

    def add_k(self, ref, qs, val, first):
        rq = self.rq
        if self.per_res:
            k0, n = (0, rq) if first else (_aligned(qs[0] - rq, rq), 2 * rq)
            ref[:, pl.ds(k0, n), :] += val
            return
        if self.paired:
            starts = [s for i, q in enumerate(qs)
                      for s in ((0, 0) if first and i == 0 else (_aligned(q - 8, 8), q))]
            rows = [(s, 16) for s in starts]
        else:
            rows = [self._key_rows(q, first and i == 0) for i, q in enumerate(qs)]
        for b, (k0, n) in enumerate(rows):
            ref[:, pl.ds(k0, n), :] += val[b].reshape(self.g_n, n, LANES)


def _batch_bias(un, qs, at_start, first_ref, general_ref):
    if not at_start:
        return general_ref[...][None]
    if un.per_res:
        return first_ref[...][None]
    return jnp.concatenate([first_ref[...][None]] + [general_ref[...][None]] * (un.count(qs) - 1), axis=0)


def _stack_heads(x, low):
    zero = jnp.zeros_like(x)
    return jnp.concatenate([jnp.where(low, x, zero), jnp.where(low, zero, x)], axis=1)


def _attn_fwd(proj, slopes, d):
    g_n, rq = PATTERNS[d]
    un = _Units(g_n, rq)
    s_n = proj.shape[0]
    lr = s_n // N_RES
    nb = lr // rq
    q_n = g_n * rq
    d0, m0, d1, m1 = _attn_tables(d)
    first, n_more, later = un.plan(lr, FWD_BATCH[d])

    def body(sl_ref, q_ref, k_ref, v_ref, d0_ref, m0_ref, d1_ref, m1_ref, o_ref, lse_ref, b0_ref, b1_ref):
        hp = pl.program_id(1)

        @pl.when(hp == 0)
        def _():
            lse_ref[...] = jnp.zeros(lse_ref.shape, F32)

        for h in (0, 1):
            slope = sl_ref[2 * hp + h]
            b0_ref[h * q_n:(h + 1) * q_n, :] = m0_ref[...] - slope * d0_ref[...]
            b1_ref[h * q_n:(h + 1) * q_n, :] = m1_ref[...] - slope * d1_ref[...]

        lane = lax.broadcasted_iota(jnp.int32, (1, q_n, LANES), 2)
        low = lane < HEAD_DIM
        grp = lane // 8

        def batch(qs, at_start):
            qq = _stack_heads(un.load_q(q_ref, qs) * 0.125, low)
            s = _bdot(qq, un.load_k(k_ref, qs, at_start), _BNT) + _batch_bias(un, qs, at_start, b0_ref, b1_ref)
            m = jnp.max(s, axis=2, keepdims=True)
            p = jnp.exp(s - m)
            l = jnp.sum(p, axis=2, keepdims=True)
            o = _bdot(p, un.load_k(v_ref, qs, at_start), _BNN) * (1.0 / l)
            lse = m + jnp.log(l)
            un.store_q(o_ref, qs, jnp.where(low, o[:, :q_n], o[:, q_n:]))
            upd = jnp.where(grp == 2 * hp, lse[:, :q_n], 0.0) + jnp.where(grp == 2 * hp + 1, lse[:, q_n:], 0.0)
            un.store_q(lse_ref, qs, upd, add=True)

        batch(first, True)

        def more(j, carry):
            batch(later(j), False)
            return carry

        lax.fori_loop(1, 1 + n_more, more, 0)

    pv = _pattern_view(proj, g_n)
    full = lambda a: pl.BlockSpec(a.shape, lambda r, hp: (0, 0))
    o, lse = pl.pallas_call(
        body, grid=_pattern_grid(g_n),
        in_specs=[pl.BlockSpec(memory_space=pltpu.SMEM),
                  _pattern_spec(g_n, lr, lambda hp: 32 + hp),
                  _pattern_spec(g_n, lr, lambda hp: 40 + hp),
                  _pattern_spec(g_n, lr, lambda hp: 48 + hp),
                  full(d0), full(m0), full(d1), full(m1)],
        out_specs=[_pattern_spec(g_n, lr, lambda hp: hp), _pattern_spec(g_n, lr, lambda hp: 0)],
        out_shape=[jax.ShapeDtypeStruct(_pattern_view_shape(s_n, D_MODEL, g_n), ACT_DTYPE),
                   jax.ShapeDtypeStruct(_pattern_view_shape(s_n, LANES, g_n), F32)],
        scratch_shapes=[pltpu.VMEM((2 * q_n, d0.shape[1]), F32), pltpu.VMEM((2 * q_n, 2 * q_n), F32)],
        name=f"attn_fwd_d{d}",
    )(slopes, pv, pv, pv, d0, m0, d1, m1)
    return o.reshape(s_n, D_MODEL), lse.reshape(s_n, LANES)


def _attn_combine(outs, lses, proj):
    s_n = proj.shape[0]
    tm = 512

    def body(o1_ref, o2_ref, o3_ref, l1_ref, l2_ref, l3_ref, za_ref, e_ref, o_ref, lse_ref, ha_ref, hat_ref):
        ls = [l1_ref[...], l2_ref[...], l3_ref[...]]
        mx = jnp.maximum(jnp.maximum(ls[0], ls[1]), ls[2])
        den = sum(jnp.exp(l - mx) for l in ls)
        lse = mx + jnp.log(den)
        lse_ref[...] = lse
        o = jnp.zeros((tm, D_MODEL), F32)
        for l, oref in zip(ls, (o1_ref, o2_ref, o3_ref)):
            o = o + _select_cols(jnp.exp(l - lse), e_ref[...], terms=2) * oref[...].astype(F32)
        o_ref[...] = o.astype(o_ref.dtype)
        z = za_ref[...].astype(F32)
        ha = z * _sigmoid(z) * o
        ha_ref[...] = ha.astype(ha_ref.dtype)
        hat_ref[...] = ha.T.astype(hat_ref.dtype)

    row = lambda w: pl.BlockSpec((tm, w), lambda i: (i, 0))
    return pl.pallas_call(
        body, grid=(s_n // tm,),
        in_specs=[row(D_MODEL)] * 3 + [row(LANES)] * 3
        + [pl.BlockSpec((tm, D_MODEL), lambda i: (i, 7)), pl.BlockSpec((LANES, D_MODEL), lambda i: (0, 0))],
        out_specs=[row(D_MODEL), row(LANES), row(D_MODEL), pl.BlockSpec((D_MODEL, tm), lambda i: (0, i))],
        out_shape=[jax.ShapeDtypeStruct((s_n, D_MODEL), ACT_DTYPE), jax.ShapeDtypeStruct((s_n, LANES), F32),
                   jax.ShapeDtypeStruct((s_n, D_MODEL), ACT_DTYPE), jax.ShapeDtypeStruct((D_MODEL, s_n), ACT_DTYPE)],
        name="attn_combine",
    )(*outs, *lses, proj, _head_expand_matrix())


CHAIN_ROWS = 256


def _row_chains(tm):
    return [slice(r, r + CHAIN_ROWS) for r in range(0, tm, CHAIN_ROWS)]


def _gates(gc_ref, ga_ref, b_ref, rows):
    b = b_ref[...]
    gc = _sigmoid(gc_ref[rows, :].astype(F32) + b[:, :D_MODEL])
    ga = _sigmoid(ga_ref[rows, :].astype(F32) + b[:, D_MODEL:])
    return gc, ga


def _merge_loss(hc, ha, woc, woa, wo, proj, b_merge, xp, final_g, tgt):
    s_n = xp.shape[0]
    tm = 512

    def body(hc_ref, ha_ref, woc_ref, woa_ref, wo_ref, gc_ref, ga_ref, b_ref, x_ref, gf_ref, t_ref,
             yc_ref, ya_ref, mgt_ref, dhb_ref, dgf_ref, loss_ref):
        i = pl.program_id(0)

        @pl.when(i == 0)
        def _():
            dgf_ref[...] = jnp.zeros(dgf_ref.shape, F32)
            loss_ref[...] = jnp.zeros(loss_ref.shape, F32)

        gf = gf_ref[...]
        for rows in _row_chains(tm):
            yc = _dot(hc_ref[rows, :], woc_ref[...])
            ya = _dot(ha_ref[rows, :], woa_ref[...])
            gc, ga = _gates(gc_ref, ga_ref, b_ref, rows)
            mg = gc * yc + ga * ya
            yc_ref[rows, :] = yc.astype(yc_ref.dtype)
            ya_ref[rows, :] = ya.astype(ya_ref.dtype)
            mgt_ref[:, rows] = mg.T.astype(mgt_ref.dtype)
            h2 = x_ref[rows, :] + _dot(mg, wo_ref[...])
            r2 = lax.rsqrt(jnp.mean(h2 * h2, axis=-1, keepdims=True) + EPS)
            nrm = h2 * r2
            err = nrm * gf - t_ref[rows, :]
            e2 = (err * err).reshape(-1, 8, D_MODEL).sum(axis=0)
            loss_ref[...] += sum(e2[:, c * LANES:(c + 1) * LANES] for c in range(D_MODEL // LANES))
            dy = err * (1.0 / D_MODEL)
            dgf_ref[...] += jnp.sum(dy * nrm, axis=0, keepdims=True)
            dn = dy * gf
            dh2 = r2 * (dn - nrm * jnp.mean(dn * nrm, axis=-1, keepdims=True))
            dhb_ref[rows, :] = dh2.astype(dhb_ref.dtype)

    row = pl.BlockSpec((tm, D_MODEL), lambda i: (i, 0))
    wsp = pl.BlockSpec((D_MODEL, D_MODEL), lambda i: (0, 0))
    vec = lambda w: pl.BlockSpec((1, w), lambda i: (0, 0))
    act = jax.ShapeDtypeStruct((s_n, D_MODEL), ACT_DTYPE)
    return pl.pallas_call(
        body, grid=(s_n // tm,),
        in_specs=[row, row, wsp, wsp, wsp,
                  pl.BlockSpec((tm, D_MODEL), lambda i: (i, 8)), pl.BlockSpec((tm, D_MODEL), lambda i: (i, 9)),
                  vec(2 * D_MODEL), row, vec(D_MODEL), row],
        out_specs=[row, row, pl.BlockSpec((D_MODEL, tm), lambda i: (0, i)), row,
                   vec(D_MODEL), pl.BlockSpec((8, LANES), lambda i: (0, 0))],
        out_shape=[act, act, jax.ShapeDtypeStruct((D_MODEL, s_n), ACT_DTYPE), act,
                   jax.ShapeDtypeStruct((1, D_MODEL), F32), jax.ShapeDtypeStruct((8, LANES), F32)],
        name="merge_loss",
    )(hc, ha, woc, woa, wo, proj, proj, b_merge, xp, final_g, tgt)


def _merge_bwd(dh2b, wo, woc, woa, yc, ya, proj, b_merge, o):
    s_n = dh2b.shape[0]
    tm = 512

    def body(dh_ref, wo_ref, woc_ref, woa_ref, yc_ref, ya_ref, gc_ref, ga_ref, b_ref, o_ref, za_ref, e_ref,
             dyc_ref, dya_ref, dhc_ref, do_ref, dsum_ref, db3_ref, dbias_ref):
        i = pl.program_id(0)

        @pl.when(i == 0)
        def _():
            dbias_ref[...] = jnp.zeros(dbias_ref.shape, F32)

        for rows in _row_chains(tm):
            dmg = _dot_nt(dh_ref[rows, :], wo_ref[...])
            gc, ga = _gates(gc_ref, ga_ref, b_ref, rows)
            dgc = dmg * yc_ref[rows, :].astype(F32) * gc * (1.0 - gc)
            dga = dmg * ya_ref[rows, :].astype(F32) * ga * (1.0 - ga)
            dbias_ref[:, :D_MODEL] += jnp.sum(dgc, axis=0, keepdims=True)
            dbias_ref[:, D_MODEL:] += jnp.sum(dga, axis=0, keepdims=True)
            dyc = dmg * gc
            dya = dmg * ga
            dyc_ref[rows, :] = dyc.astype(dyc_ref.dtype)
            dya_ref[rows, :] = dya.astype(dya_ref.dtype)
            dhc_ref[rows, :] = _dot_nt(dyc, woc_ref[...]).astype(dhc_ref.dtype)
            dha = _dot_nt(dya, woa_ref[...])
            z = za_ref[rows, :].astype(F32)
            sg = _sigmoid(z)
            ov = o_ref[rows, :].astype(F32)
            dout = dha * z * sg
            do_ref[rows, :] = dout.astype(do_ref.dtype)
            dsum_ref[rows, :] = _select_cols(dout * ov, e_ref[...], terms=2)
            db3_ref[0, rows, :] = (dha * ov * sg * (1.0 + z * (1.0 - sg))).astype(db3_ref.dtype)
            db3_ref[1, rows, :] = dgc.astype(db3_ref.dtype)
            db3_ref[2, rows, :] = dga.astype(db3_ref.dtype)

    row = pl.BlockSpec((tm, D_MODEL), lambda i: (i, 0))
    wsp = pl.BlockSpec((D_MODEL, D_MODEL), lambda i: (0, 0))
    act = jax.ShapeDtypeStruct((s_n, D_MODEL), ACT_DTYPE)
    return pl.pallas_call(
        body, grid=(s_n // tm,),
        in_specs=[row, wsp, wsp, wsp, row, row,
                  pl.BlockSpec((tm, D_MODEL), lambda i: (i, 8)), pl.BlockSpec((tm, D_MODEL), lambda i: (i, 9)),
                  pl.BlockSpec((1, 2 * D_MODEL), lambda i: (0, 0)), row,
                  pl.BlockSpec((tm, D_MODEL), lambda i: (i, 7)), pl.BlockSpec((D_MODEL, LANES), lambda i: (0, 0))],
        out_specs=[row, row, row, row, pl.BlockSpec((tm, LANES), lambda i: (i, 0)),
                   pl.BlockSpec((3, tm, D_MODEL), lambda i: (0, i, 0)),
                   pl.BlockSpec((1, 2 * D_MODEL), lambda i: (0, 0))],
        out_shape=[act, act, act, act, jax.ShapeDtypeStruct((s_n, LANES), F32),
                   jax.ShapeDtypeStruct((3, s_n, D_MODEL), ACT_DTYPE),
                   jax.ShapeDtypeStruct((1, 2 * D_MODEL), F32)],
        name="merge_bwd",
    )(dh2b, wo, woc, woa, yc, ya, proj, proj, b_merge, o, proj, _head_sum_matrix())


def _mm_lhs_resident(a, b, tn, name):
    m_n, k_n = a.shape
    n_n = b.shape[1]

    def body(a_ref, b_ref, o_ref):
        o_ref[...] = _dot(a_ref[...], b_ref[...])

    return pl.pallas_call(
        body, grid=(n_n // tn,),
        in_specs=[pl.BlockSpec((m_n, k_n), lambda n: (0, 0)), pl.BlockSpec((k_n, tn), lambda n: (0, n))],
        out_specs=pl.BlockSpec((m_n, tn), lambda n: (0, n)),
        out_shape=jax.ShapeDtypeStruct((m_n, n_n), F32),
        name=name,
    )(a, b)


def _conv_bwd(proj, conv_w, dhc):
    s_n = proj.shape[0]
    lr = s_n // N_RES
    pv = proj.reshape(N_RES, lr, IN_COLS)

    def body(xc_ref, bg_ref, cg_ref, zc_ref, w_ref, dhc_ref, da4_ref, dw_ref, dc_ref):
        w = w_ref[...]
        row = lax.broadcasted_iota(jnp.int32, (lr, LANES), 0)
        dw = [jnp.zeros((1, LANES), F32) for _ in range(3)]
        products = {}
        for r in range(N_RES):
            a, am1, am2 = _conv_terms(xc_ref, cg_ref, r, row, lr, products)
            c = w[0:1] * am2 + w[1:2] * am1 + w[2:3] * a
            z = zc_ref[r].astype(F32)
            sg = _sigmoid(z)
            sz = z * sg
            bg = bg_ref[r].astype(F32)
            dh = dhc_ref[r].astype(F32)
            da4_ref[1, r] = (dh * sz * c).astype(da4_ref.dtype)
            da4_ref[3, r] = (dh * bg * c * sg * (1.0 + z * (1.0 - sg))).astype(da4_ref.dtype)
            dc = dh * sz * bg
            dc_ref[r] = dc
            dw[0] = dw[0] + jnp.sum(dc * am2, axis=0, keepdims=True)
            dw[1] = dw[1] + jnp.sum(dc * am1, axis=0, keepdims=True)
            dw[2] = dw[2] + jnp.sum(dc * a, axis=0, keepdims=True)
        dw_ref[0:1, :] = dw[0]
        dw_ref[1:2, :] = dw[1]
        dw_ref[2:3, :] = dw[2]

        def shift_up(v):
            return jnp.where(row < lr - 1, pltpu.roll(v, lr - 1, 0), 0.0)

        for r in range(N_RES):
            dp1 = dc_ref[r + 1] if r + 1 < N_RES else shift_up(dc_ref[0])
            dp2 = dc_ref[r + 2] if r + 2 < N_RES else shift_up(dc_ref[r + 2 - N_RES])
            da = w[2:3] * dc_ref[r] + w[1:2] * dp1 + w[0:1] * dp2
            da4_ref[0, r] = (da * cg_ref[r].astype(F32)).astype(da4_ref.dtype)
            da4_ref[2, r] = (da * xc_ref[r].astype(F32)).astype(da4_ref.dtype)

    def col(part):
        return pl.BlockSpec((N_RES, lr, LANES), lambda j: (0, 0, part * 8 + j))

    da4, dw = pl.pallas_call(
        body, grid=(D_MODEL // LANES,),
        in_specs=[col(0), col(1), col(2), col(3), pl.BlockSpec((3, LANES), lambda j: (0, j)),
                  pl.BlockSpec((N_RES, lr, LANES), lambda j: (0, 0, j))],
        out_specs=[pl.BlockSpec((4, N_RES, lr, LANES), lambda j: (0, 0, 0, j)),
                   pl.BlockSpec((3, LANES), lambda j: (0, j))],
        out_shape=[jax.ShapeDtypeStruct((4, N_RES, lr, D_MODEL), ACT_DTYPE),
                   jax.ShapeDtypeStruct((3, D_MODEL), F32)],
        scratch_shapes=[pltpu.VMEM((N_RES, lr, LANES), F32)],
        name="conv_bwd",
    )(pv, pv, pv, pv, conv_w, dhc.reshape(N_RES, lr, D_MODEL))
    return da4.reshape(4, s_n, D_MODEL), dw


def _attn_bwd(proj, dout, lse, dsum, slopes, d, prev=None):
    g_n, rq = PATTERNS[d]
    un = _Units(g_n, rq)
    s_n = proj.shape[0]
    lr = s_n // N_RES
    nb = lr // rq
    q_n = g_n * rq
    d0, m0, d1, m1 = (np.ascontiguousarray(t.T) for t in _attn_tables(d))
    first, n_more, later = un.plan(lr, BWD_BATCH[d])
    bsz = un.count(first)
    gd = RES_PER_STEP if un.per_res else g_n

    def body(sl_ref, q_ref, k_ref, v_ref, do_ref, lse_ref, ds_ref, d0_ref, m0_ref, d1_ref, m1_ref, *rest):
        prev_ref = rest[0] if prev is not None else None
        out_ref, b0_ref, b1_ref, lt_ref, dt_ref, dk_ref, dv_ref = rest[-7:]
        hp = pl.program_id(1)
        for h in (0, 1):
            slope = sl_ref[2 * hp + h]
            b0_ref[:, h * q_n:(h + 1) * q_n] = m0_ref[...] - slope * d0_ref[...]
            b1_ref[:, h * q_n:(h + 1) * q_n] = m1_ref[...] - slope * d1_ref[...]
        if prev is None:
            dk_ref[...] = jnp.zeros(dk_ref.shape, F32)
            dv_ref[...] = jnp.zeros(dv_ref.shape, F32)
        else:
            out_ref[0] = prev_ref[0]
            dk_ref[...] = prev_ref[1].astype(F32)
            dv_ref[...] = prev_ref[2].astype(F32)
        low = lax.broadcasted_iota(jnp.int32, (1, q_n, LANES), 2) < HEAD_DIM
        row16 = pl.multiple_of(16 * hp, 16)

        def query_rows(stat_ref, t_ref, qs):
            tiles = un.load_q(stat_ref, qs)
            for b in range(bsz):
                t_ref[b] = tiles[b].T
            t16 = t_ref[:, pl.ds(row16, 16), :]
            return jnp.concatenate([t16[:, 0:1, :], t16[:, 8:9, :]], axis=2)

        def batch(qs, at_start):
            qq = _stack_heads(un.load_q(q_ref, qs) * 0.125, low)
            dd = _stack_heads(un.load_q(do_ref, qs), low)
            ks = un.load_k(k_ref, qs, at_start)
            vs = un.load_k(v_ref, qs, at_start)
            lrow = query_rows(lse_ref, lt_ref, qs)
            drow = query_rows(ds_ref, dt_ref, qs)
            pt = jnp.exp(_bdot(ks, qq, _BNT) + _batch_bias(un, qs, at_start, b0_ref, b1_ref) - lrow)
            dst = pt * (_bdot(vs, dd, _BNT) - drow)
            un.add_k(dv_ref, qs, _bdot(pt, dd, _BNN), at_start)
            un.add_k(dk_ref, qs, _bdot(dst, qq, _BNN), at_start)
            dq = _bdot(jnp.swapaxes(dst, 1, 2), ks, _BNN)
            un.store_q(out_ref, qs, jnp.where(low, dq[:, :q_n], dq[:, q_n:]) * 0.125, add=prev is not None,
                       lead=(0,))

        batch(first, True)

        def more(j, carry):
            batch(later(j), False)
            return carry

        lax.fori_loop(1, 1 + n_more, more, 0)
        out_ref[1] = dk_ref[...].astype(out_ref.dtype)
        out_ref[2] = dv_ref[...].astype(out_ref.dtype)

    pv = _pattern_view(proj, g_n)
    full = lambda a: pl.BlockSpec(a.shape, lambda r, hp: (0, 0))
    whole = _pattern_spec(g_n, lr, lambda hp: hp, lead=(3,))
    out = pl.pallas_call(
        body, grid=_pattern_grid(g_n),
        in_specs=[pl.BlockSpec(memory_space=pltpu.SMEM),
                  _pattern_spec(g_n, lr, lambda hp: 32 + hp),
                  _pattern_spec(g_n, lr, lambda hp: 40 + hp),
                  _pattern_spec(g_n, lr, lambda hp: 48 + hp),
                  _pattern_spec(g_n, lr, lambda hp: hp),
                  _pattern_spec(g_n, lr, lambda hp: 0),
                  _pattern_spec(g_n, lr, lambda hp: 0),
                  full(d0), full(m0), full(d1), full(m1)] + ([] if prev is None else [whole]),
        out_specs=whole,
        out_shape=jax.ShapeDtypeStruct(_pattern_view_shape(s_n, D_MODEL, g_n, lead=(3,)), ACT_DTYPE),
        scratch_shapes=[pltpu.VMEM((d0.shape[0], 2 * q_n), F32), pltpu.VMEM((2 * q_n, 2 * q_n), F32),
                        pltpu.VMEM((bsz, LANES, q_n), F32), pltpu.VMEM((bsz, LANES, q_n), F32),
                        pltpu.VMEM((gd, lr, LANES), F32), pltpu.VMEM((gd, lr, LANES), F32)],
        name=f"attn_bwd_d{d}",
    )(slopes, pv, pv, pv, _pattern_view(dout, g_n), _pattern_view(lse, g_n), _pattern_view(dsum, g_n),
      d0, m0, d1, m1, *([] if prev is None else [_pattern_view(prev, g_n, lead=(3,))]))
    return out.reshape(3, s_n, D_MODEL)


def _part_index(step, per, lo, n):
    return jnp.clip(step // per - lo, 0, n - 1)


def _dw_in(ut, da4, dc3, db3):
    s_n = ut.shape[1]
    tn = 512
    per = D_MODEL // tn
    shard_blocks = SHARD_COLS // tn

    def body(a_ref, p0_ref, p1_ref, p2_ref, o_ref):
        part = pl.program_id(0) // per

        @pl.when(part < 4)
        def _():
            o_ref[...] = _dot(a_ref[...], p0_ref[...])

        @pl.when((part >= 4) & (part < 7))
        def _():
            o_ref[...] = _dot(a_ref[...], p1_ref[...])

        @pl.when(part >= 7)
        def _():
            o_ref[...] = _dot(a_ref[...], p2_ref[...])

    def pspec(lo, n):
        def index(j):
            part = j // per
            col = jnp.where(part < lo, 0, jnp.where(part >= lo + n, per - 1, j % per))
            return _part_index(j, per, lo, n), 0, col
        return pl.BlockSpec((None, s_n, tn), index)

    return pl.pallas_call(
        body, grid=(IN_COLS // tn,),
        in_specs=[pl.BlockSpec((D_MODEL, s_n), lambda j: (0, 0), pipeline_mode=pl.Buffered(1)),
                  pspec(0, 4), pspec(4, 3), pspec(7, 3)],
        out_specs=pl.BlockSpec((None, D_MODEL, tn), lambda j: (j // shard_blocks, 0, j % shard_blocks)),
        out_shape=jax.ShapeDtypeStruct((4, D_MODEL, SHARD_COLS), F32),
        name="dw_in",
    )(ut, da4, dc3, db3)


def _input_grad(da4, dc3, db3, w4, xp, norm_g, dh2, row0, rows):
    tm, tk = 256, 512
    per = D_MODEL // tk
    shard_blocks = SHARD_COLS // tk
    m0 = row0 // tm

    def body(p0_ref, p1_ref, p2_ref, w_ref, x_ref, g_ref, dh_ref, gx_ref, dg_ref):
        @pl.when(pl.program_id(0) == 0)
        def _():
            dg_ref[...] = jnp.zeros(dg_ref.shape, F32)

        du = None
        for k in range(IN_COLS // tk):
            part, cols = k // per, pl.ds((k % per) * tk, tk)
            ref, slot = (p0_ref, part) if part < 4 else (p1_ref, part - 4) if part < 7 else (p2_ref, part - 7)
            d = _dot_nt(ref[slot, :, cols], w_ref[k // shard_blocks, :, pl.ds((k % shard_blocks) * tk, tk)])
            du = d if du is None else du + d
        x = x_ref[...]
        r = lax.rsqrt(jnp.mean(x * x, axis=-1, keepdims=True) + EPS)
        nrm = x * r
        dg_ref[...] += jnp.sum(du * nrm, axis=0, keepdims=True)
        dn = du * g_ref[...]
        gx_ref[...] = dh_ref[...].astype(F32) + r * (dn - nrm * jnp.mean(dn * nrm, axis=-1, keepdims=True))

    def pspec(n):
        return pl.BlockSpec((n, tm, D_MODEL), lambda m: (0, m0 + m, 0))

    row_in = pl.BlockSpec((tm, D_MODEL), lambda m: (m0 + m, 0))
    vec = pl.BlockSpec((1, D_MODEL), lambda m: (0, 0))
    return pl.pallas_call(
        body, grid=(rows // tm,),
        in_specs=[pspec(4), pspec(3), pspec(3),
                  pl.BlockSpec(w4.shape, lambda m: (0, 0, 0), pipeline_mode=pl.Buffered(1)),
                  row_in, vec, row_in],
        out_specs=[pl.BlockSpec((tm, D_MODEL), lambda m: (m, 0)), vec],
        out_shape=[jax.ShapeDtypeStruct((rows, D_MODEL), F32), jax.ShapeDtypeStruct((1, D_MODEL), F32)],
        name="input_grad",
    )(da4, dc3, db3, w4, xp, norm_g, dh2)


class _Step:
    def __init__(self, x, tgt, norm_g, chip, after=0.0):
        self.norm_g, self.chip = norm_g, chip
        self.slopes = _alibi_slopes()
        self.xp, self.tp = _to_residue_major(x, tgt, after)
        self.u, self.ut = _rms_in(self.xp, norm_g)

    def project_own(self, w_own, after=None):
        self.proj_own = _in_proj(self.u, self.chip, w_own=w_own, after=after)

    def project(self, w4, others):
        self.proj_own = _in_proj(self.u, self.chip, w4=w4, partial=self.proj_own, others=others)

    def mixers(self, w4, taps):
        self.w4, self.taps, self.proj = w4, taps, self.proj_own
        self.hc, self.hct = _conv_fwd(self.proj, taps)
        fwd = [_attn_fwd(self.proj, self.slopes, d) for d in PATTERNS]
        self.o, self.lse, self.ha, self.hat = _attn_combine([f[0] for f in fwd], [f[1] for f in fwd], self.proj)

    def merge_and_loss(self, woc, woa, wo, b_merge, final_g):
        self.woc, self.woa, self.wo, self.b_merge = woc, woa, wo, b_merge
        (self.yc, self.ya, self.mgt, self.dh2b, self.d_final_g, self.loss8) = _merge_loss(
            self.hc, self.ha, woc, woa, wo, self.proj, b_merge, self.xp, final_g, self.tp)

    def out_weight_grads(self):
        (dyc, dya, self.dhc, self.dout, self.dsum, self.db3, self.d_bias) = _merge_bwd(
            self.dh2b, self.wo, self.woc, self.woa, self.yc, self.ya, self.proj, self.b_merge, self.o)
        d_wo = _mm_lhs_resident(self.mgt, self.dh2b, 256, "dw_o")
        d_woc = _mm_lhs_resident(self.hct, dyc, 256, "dw_out_conv")
        d_woa = _mm_lhs_resident(self.hat, dya, 256, "dw_out_attn")
        return d_woc, d_woa, d_wo

    def conv_grads(self, after=0.0):
        self.da4, self.d_taps = _conv_bwd(self.proj, self.taps + after, self.dhc)

    def in_weight_grad(self, after=0.0):
        slopes = self.slopes + after
        self.dc3 = None
        for d in PATTERNS:
            self.dc3 = _attn_bwd(self.proj, self.dout, self.lse, self.dsum, slopes, d, prev=self.dc3)
        return _dw_in(self.ut, self.da4, self.dc3, self.db3)

    def input_grad(self, half, after=0.0):
        rows = self.xp.shape[0] // 2
        return _input_grad(self.da4, self.dc3, self.db3, self.w4, self.xp, self.norm_g + after, self.dh2b,
                           half * rows, rows)


def _local_grads(x, tgt, norm_g, w4, b_merge, conv_w, woc, woa, wo, final_g):
    st = _Step(x, tgt, norm_g, jnp.zeros((1,), jnp.int32))
    st.project_own(w4[0])
    st.project(w4, (2, 1))
    st.project(w4, (3,))
    st.mixers(w4, conv_w)
    st.merge_and_loss(woc, woa, wo, b_merge, final_g)
    d_woc, d_woa, d_wo = st.out_weight_grads()
    st.conv_grads()
    d_w4 = st.in_weight_grad()
    gx_lo, dg_lo = st.input_grad(0)
    gx_hi, dg_hi = st.input_grad(1)
    return (st.loss8, _to_natural(gx_lo, gx_hi), dg_lo + dg_hi, d_w4, st.d_bias, st.d_taps, d_woc, d_woa, d_wo,
            st.d_final_g)


MESH = pl.DeviceIdType.MESH
_CHIP_FLIPS = ((1, 0), (0, 1), (1, 1))
_ANY = pl.BlockSpec(memory_space=pl.ANY)


def _place():
    return lax.axis_index("x"), lax.axis_index("y"), lax.axis_index("c")


def _flip(v, f):
    return 1 - v if f else v


def _remote(src, dst, send_sems, recv_sems, k, device):
    return pltpu.make_async_remote_copy(src_ref=src, dst_ref=dst, send_sem=send_sems.at[k], recv_sem=recv_sems.at[k],
                                        device_id=device, device_id_type=MESH)


def _place_shard(w, chip):
    rows, cols = w.shape
    tm = 128

    def body(chip_ref, w_ref, o_ref):
        o_ref[0] = w_ref[...].astype(o_ref.dtype)

    return pl.pallas_call(
        body,
        grid_spec=pltpu.PrefetchScalarGridSpec(
            num_scalar_prefetch=1, grid=(rows // tm,),
            in_specs=[pl.BlockSpec((tm, cols), lambda i, chip_ref: (i, 0))],
            out_specs=pl.BlockSpec((1, tm, cols), lambda i, chip_ref: (chip_ref[0], i, 0))),
        out_shape=jax.ShapeDtypeStruct((4, rows, cols), MXU_DTYPE),
        name="place_shard",
    )(chip, w)


def _gather_copies_to(flips):
    def copies(arrs, _, send_sems, recv_sems):
        x, y, c = _place()
        out = []
        for a, arr in enumerate(arrs):
            h = arr.shape[1] // 2
            mine = arr.at[2 * x + y, pl.ds(pl.multiple_of(c * h, 8), h)]
            for i, t in enumerate(flips):
                fx, fy = _CHIP_FLIPS[t]
                out.append(_remote(mine, mine, send_sems, recv_sems, len(flips) * a + i,
                                   (_flip(x, fx), _flip(y, fy), c)))
        return out
    return copies


def _forward_to_sibling(arrs, flips=(0, 1, 2)):
    n = len(arrs)

    def body(*refs):
        outs = refs[n:2 * n]
        send_sems, recv_sems = refs[2 * n:]
        x, y, c = _place()
        sibling = (x, y, 1 - c)
        started = []
        for a in range(n):
            h = outs[a].shape[1] // 2
            rows = pl.ds(pl.multiple_of(c * h, 8), h)
            for t in flips:
                fx, fy = _CHIP_FLIPS[t]
                landed = outs[a].at[2 * _flip(x, fx) + _flip(y, fy), rows]
                cp = _remote(landed, landed, send_sems, recv_sems, 3 * a + t, sibling)
                cp.start()
                started.append(cp)
        for a in range(n):
            h = outs[a].shape[1] // 2
            rows = pl.ds(pl.multiple_of((1 - c) * h, 8), h)
            for t in flips:
                fx, fy = _CHIP_FLIPS[t]
                handed = outs[a].at[2 * _flip(x, fx) + _flip(y, fy), rows]
                _remote(handed, handed, send_sems, recv_sems, 3 * a + t, sibling).wait_recv()
        for cp in started:
            cp.wait_send()

    return pl.pallas_call(
        body, in_specs=[_ANY] * n, out_specs=[_ANY] * n,
        out_shape=[jax.ShapeDtypeStruct(s.shape, s.dtype) for s in arrs],
        input_output_aliases={a: a for a in range(n)},
        scratch_shapes=[pltpu.SemaphoreType.DMA((3 * n,)), pltpu.SemaphoreType.DMA((3 * n,))],
        name="gathered_to_sibling_" + "".join(str(t) for t in flips),
    )(*arrs)


_HBM = pl.BlockSpec(memory_space=pltpu.HBM)
_SEM = pl.BlockSpec(memory_space=pltpu.SEMAPHORE)
_EFFECT = pltpu.SideEffectType.DATAFLOW_SIDE_EFFECTING


class _SplitExchange:
    def __init__(self, name, srcs, land_shapes, n_copies, copies, riders=()):
        self.name, self.n, self.nl, self.copies = name, len(srcs), len(land_shapes), copies
        n, nb = self.n, len(srcs) + len(land_shapes)
        lands = [lax.empty(s.shape, s.dtype) for s in land_shapes]
        bufs = [pltpu.with_memory_space_constraint(a, pltpu.HBM) for a in (*srcs, *lands, *riders)]
        na = len(bufs)

        def body(*refs):
            send_sems, recv_sems = refs[na], refs[na + 1]
            for cp in copies(refs[:n], refs[n:nb], send_sems, recv_sems):
                cp.start()
            refs[-1][...] = jnp.zeros(refs[-1].shape, F32)

        outs = pl.pallas_call(
            body, name=name + "_start",
            in_specs=[_HBM] * na,
            out_specs=[_SEM, _SEM] + [_HBM] * na + [pl.BlockSpec(memory_space=pltpu.VMEM)],
            out_shape=[pltpu.SemaphoreType.DMA((n_copies,)), pltpu.SemaphoreType.DMA((n_copies,))]
            + [pltpu.HBM(b.shape, b.dtype) for b in bufs] + [jax.ShapeDtypeStruct((8, LANES), F32)],
            input_output_aliases={i: 2 + i for i in range(na)},
            compiler_params=pltpu.CompilerParams(has_side_effects=_EFFECT),
        )(*bufs)
        self.sems, self.bufs, self.riders, self.token = outs[:2], outs[2:2 + nb], outs[2 + nb:2 + na], outs[-1]

    def after(self):
        return self.token[0, 0]

    def wait(self, done, riders=(), bufs=None):
        n, nb, copies = self.n, self.n + self.nl, self.copies
        bufs = [*(self.bufs if bufs is None else bufs),
                *[pltpu.with_memory_space_constraint(a, pltpu.HBM) for a in riders]]
        na = len(bufs)
        done = list(done) if isinstance(done, (list, tuple)) else [done]

        def body(*refs):
            send_sems, recv_sems = refs[na], refs[na + 1]
            for cp in copies(refs[:n], refs[n:nb], send_sems, recv_sems):
                cp.wait_send()
                cp.wait_recv()

        outs = pl.pallas_call(
            body, name=self.name + "_wait",
            in_specs=[_HBM] * na + [_SEM, _SEM] + [_ANY] * len(done),
            out_specs=[_HBM] * na,
            out_shape=[pltpu.HBM(b.shape, b.dtype) for b in bufs],
            input_output_aliases={i: i for i in range(na)},
            compiler_params=pltpu.CompilerParams(has_side_effects=_EFFECT),
        )(*bufs, *self.sems, *done)
        return outs[:n], outs[n:nb], outs[nb:]


def _sibling_copies(srcs, lands, send_sems, recv_sems):
    x, y, c = _place()
    out = []
    for a, (src, land) in enumerate(zip(srcs, lands)):
        h = src.shape[1] // 2
        theirs = pl.ds(pl.multiple_of((1 - c) * h, 8), h)
        out.append(_remote(src.at[:, theirs], land, send_sems, recv_sems, a, (x, y, 1 - c)))
    return out


def _grads_to_sibling(name, grads):
    shapes = [jax.ShapeDtypeStruct((4, g.shape[1] // 2, g.shape[2]), g.dtype) for g in grads]
    return _SplitExchange(name, grads, shapes, len(grads), _sibling_copies)


def _chip_copies(srcs, lands, send_sems, recv_sems):
    x, y, c = _place()
    out = []
    for a, (src, land) in enumerate(zip(srcs, lands)):
        for t, (fx, fy) in enumerate(_CHIP_FLIPS):
            tx, ty = _flip(x, fx), _flip(y, fy)
            out.append(_remote(src.at[2 * tx + ty], land.at[t], send_sems, recv_sems, 3 * a + t, (tx, ty, c)))
    return out


def _grads_to_chips(name, parts):
    shapes = [jax.ShapeDtypeStruct((3, *p.shape[1:]), p.dtype) for p in parts]
    return _SplitExchange(name, parts, shapes, 3 * len(parts), _chip_copies)


def _add_halves(g, r, half):
    _, rows, cols = g.shape
    h = rows // 2
    tm = min(h, 128)
    nt = h // tm

    def body(half_ref, g_ref, r_ref, b_ref):
        b_ref[...] = (g_ref[...] + r_ref[...]).astype(b_ref.dtype)

    spec = pl.BlockSpec((1, tm, cols), lambda j, i, half_ref: (j, i, 0))
    return pl.pallas_call(
        body,
        grid_spec=pltpu.PrefetchScalarGridSpec(
            num_scalar_prefetch=1, grid=(4, nt),
            in_specs=[pl.BlockSpec((1, tm, cols), lambda j, i, half_ref: (j, half_ref[0] * nt + i, 0)), spec],
            out_specs=spec),
        out_shape=jax.ShapeDtypeStruct((4, h, cols), BF16),
        name="add_sibling_grads",
    )(half, g, r)


def _add_chips(g, r, recv, where):
    _, h, cols = r.shape
    tm = min(h, 128)
    nt = h // tm

    def body(where_ref, g_ref, r_ref, recv_ref, out_ref):
        own = g_ref[0] + r_ref[0]
        out_ref[...] = ((own + recv_ref[0].astype(F32)) + recv_ref[1].astype(F32)) + recv_ref[2].astype(F32)

    return pl.pallas_call(
        body,
        grid_spec=pltpu.PrefetchScalarGridSpec(
            num_scalar_prefetch=1, grid=(nt,),
            in_specs=[pl.BlockSpec((1, tm, cols), lambda i, w: (w[0], w[1] * nt + i, 0)),
                      pl.BlockSpec((1, tm, cols), lambda i, w: (w[0], i, 0)),
                      pl.BlockSpec((3, tm, cols), lambda i, w: (0, i, 0))],
            out_specs=pl.BlockSpec((tm, cols), lambda i, w: (w[1] * nt + i, 0))),
        out_shape=jax.ShapeDtypeStruct((2 * h, cols), F32),
        name="add_chip_grads",
    )(where, g, r, recv)


def _share_halves(shards):
    n = len(shards)

    def body(*refs):
        outs = refs[n:2 * n]
        send_sems, recv_sems = refs[2 * n:]
        x, y, c = _place()
        copies = []
        for a in range(n):
            h = outs[a].shape[0] // 2
            mine = outs[a].at[pl.ds(pl.multiple_of(c * h, 8), h)]
            copies.append(_remote(mine, mine, send_sems, recv_sems, a, (x, y, 1 - c)))
        for cp in copies:
            cp.start()
        for a, cp in enumerate(copies):
            cp.wait_send()
            h = outs[a].shape[0] // 2
            theirs = outs[a].at[pl.ds(pl.multiple_of((1 - c) * h, 8), h)]
            _remote(theirs, theirs, send_sems, recv_sems, a, (x, y, 1 - c)).wait_recv()

    return pl.pallas_call(
        body, in_specs=[_ANY] * n, out_specs=[_ANY] * n,
        out_shape=[jax.ShapeDtypeStruct(p.shape, p.dtype) for p in shards],
        input_output_aliases={a: a for a in range(n)},
        scratch_shapes=[pltpu.SemaphoreType.DMA((n,)), pltpu.SemaphoreType.DMA((n,))],
        name="share_reduced_halves",
    )(*shards)


def _exchange_small(rows, reduce):
    cols = rows[0].shape[1]
    n = len(rows)
    assert sum(r.shape[0] for r in rows) <= 8

    def body(*refs):
        ins, out_ref = refs[:n], refs[n]
        vec_ref, gath_ref, send_sems, recv_sems = refs[n + 1:]
        x, y, c = _place()
        me = 4 * x + 2 * y + c
        vec_ref[...] = jnp.zeros(vec_ref.shape, F32)
        at = 0
        for r in ins:
            vec_ref[at:at + r.shape[0], :] = r[...]
            at += r.shape[0]
        copies = []
        for k in range(1, 8):
            peer = (_flip(x, (k >> 2) & 1), _flip(y, (k >> 1) & 1), _flip(c, k & 1))
            copies.append(_remote(vec_ref, gath_ref.at[me], send_sems, recv_sems, k - 1, peer))
        for cp in copies:
            cp.start()
        gath_ref[me] = vec_ref[...]
        for cp in copies:
            cp.wait()
        if reduce:
            tot = gath_ref[0]
            for dev in range(1, 8):
                tot = tot + gath_ref[dev]
            out_ref[...] = tot
            out_ref[7:8, :] = jnp.zeros((1, cols), F32) + jnp.sum(tot[7:8, :])
        else:
            out_ref[...] = gath_ref[...]

    vm = pl.BlockSpec(memory_space=pltpu.VMEM)
    return pl.pallas_call(
        body, in_specs=[vm] * n, out_specs=vm,
        out_shape=jax.ShapeDtypeStruct((8, cols) if reduce else (8, 8, cols), F32),
        scratch_shapes=[pltpu.VMEM((8, cols), F32), pltpu.VMEM((8, 8, cols), F32),
                        pltpu.SemaphoreType.DMA((7,)), pltpu.SemaphoreType.DMA((7,))],
        name="reduce_small" if reduce else "gather_small",
    )(*rows)


def _adamw(w, g, m, v, name):
    rows, cols = w.shape
    tm = 128 if rows % 128 == 0 else rows

    def body(w_ref, g_ref, m_ref, v_ref, d_ref, m2_ref, v2_ref, gout_ref):
        gr = g_ref[...]
        m2 = ADAM_B1 * m_ref[...] + (1.0 - ADAM_B1) * gr
        v2 = ADAM_B2 * v_ref[...] + (1.0 - ADAM_B2) * (gr * gr)
        m_hat = m2 / (1.0 - ADAM_B1 ** ADAM_STEP)
        v_hat = v2 / (1.0 - ADAM_B2 ** ADAM_STEP)
        d_ref[...] = -ADAM_LR * (m_hat / (jnp.sqrt(v_hat) + ADAM_EPS) + ADAM_WD * w_ref[...])
        m2_ref[...] = m2
        v2_ref[...] = v2
        gout_ref[...] = gr

    spec = pl.BlockSpec((tm, cols), lambda i: (i, 0))
    sds = jax.ShapeDtypeStruct((rows, cols), F32)
    return pl.pallas_call(body, grid=(rows // tm,), in_specs=[spec] * 4, out_specs=[spec] * 4,
                          out_shape=[sds] * 4, name=name)(w, g, m, v)


def kernel(x, norm_g, w_in, b_merge, conv_w, w_out_conv, w_out_attn, w_o, final_g, loss_target, m_norm_g, m_w_in, m_b_merge, m_conv_w, m_w_out_conv, m_w_out_attn, m_w_o, m_final_g, v_norm_g, v_w_in, v_b_merge, v_conv_w, v_w_out_conv, v_w_out_attn, v_w_o, v_final_g):
    mx, my, mc = _place()
    chip = (2 * mx + my).astype(jnp.int32)
    seq = x.shape[1]

    chip1 = chip.reshape(1)
    slots = [_place_shard(w[0], chip1) for w in (w_in, w_out_conv, w_out_attn, w_o)]
    taps8 = _exchange_small([conv_w[0]], reduce=False)
    taps = jnp.concatenate([taps8[2 * j, :3, :] for j in range(4)], axis=1)
    gather_near = _SplitExchange("gather_w_in_near", slots[:1], [], 2, _gather_copies_to((0, 1)))
    st = _Step(x[0], loss_target[0], norm_g, chip1, after=gather_near.after())
    w4, _, _ = gather_near.wait([st.ut, taps8])
    gather_far = _SplitExchange("gather_w_in_far", w4, [], 1, _gather_copies_to((2,)))
    st.project_own(w_in[0], after=gather_far.token)
    (w4,) = _forward_to_sibling(gather_far.bufs, flips=(0, 1))
    st.project(w4, (2, 1))
    (w4,), _, out_slots = gather_far.wait([st.proj_own], riders=slots[1:], bufs=[w4])
    gather_out = _SplitExchange("gather_w_out", out_slots, [], 9, _gather_copies_to((0, 1, 2)), riders=[w4])
    (w4,) = _forward_to_sibling(gather_out.riders, flips=(2,))
    st.project(w4, (3,))
    st.mixers(w4, taps)
    out_ws, _, _ = gather_out.wait(st.o)
    woc, woa, wo = [w.reshape(D_MODEL, D_MODEL) for w in _forward_to_sibling(out_ws)]
    st.merge_and_loss(woc, woa, wo, b_merge, final_g.reshape(1, D_MODEL))

    half = mc.astype(jnp.int32).reshape(1)
    where = jnp.stack([chip, mc.astype(jnp.int32)])
    out_grads = [g.reshape(4, -1, D_MODEL) for g in st.out_weight_grads()]
    to_sibling = _grads_to_sibling("out_grads_to_sibling", out_grads)
    st.conv_grads(after=to_sibling.after())
    out_grads, out_from_sibling, _ = to_sibling.wait(st.da4)
    to_chips = _grads_to_chips("out_grads_to_chips",
                               [_add_halves(g, r, half) for g, r in zip(out_grads, out_from_sibling)])
    d_w4 = st.in_weight_grad(after=to_chips.after())
    out_from_chips = to_chips.wait(st.dc3)[1]

    to_sibling = _grads_to_sibling("in_grad_to_sibling", [d_w4])
    gx_lo, dg_lo = st.input_grad(0, after=to_sibling.after())
    (d_w4,), (from_sibling,), _ = to_sibling.wait(gx_lo)
    to_chips = _grads_to_chips("in_grad_to_chips", [_add_halves(d_w4, from_sibling, half)])
    gx_hi, dg_hi = st.input_grad(1, after=to_chips.after())
    grad_x = _to_natural(gx_lo, gx_hi)

    where_late = where + to_chips.after().astype(jnp.int32)
    out_reduced = [_add_chips(g, r, recv, where_late)
                   for g, r, recv in zip(out_grads, out_from_sibling, out_from_chips)]
    g_woc, g_woa, g_wo = _share_halves(out_reduced)
    small = _exchange_small([dg_lo + dg_hi, st.d_bias.reshape(2, D_MODEL), st.d_taps, st.d_final_g,
                             st.loss8.reshape(1, D_MODEL)], reduce=True)
    loss = (0.5 / D_MODEL) * small[7, 0]
    g_taps = lax.dynamic_slice(small[3:6], (0, chip * (D_MODEL // 4)), (3, D_MODEL // 4))
    upd = {
        "norm_g": _adamw(norm_g, small[0:1], m_norm_g, v_norm_g, "adamw_norm_g"),
        "b_merge": _adamw(b_merge, small[1:3].reshape(1, 2 * D_MODEL), m_b_merge, v_b_merge, "adamw_b_merge"),
        "conv_w": _adamw(conv_w[0], g_taps, m_conv_w[0], v_conv_w[0], "adamw_conv_w"),
        "w_out_conv": _adamw(w_out_conv[0], g_woc, m_w_out_conv[0], v_w_out_conv[0], "adamw_w_out_conv"),
        "w_out_attn": _adamw(w_out_attn[0], g_woa, m_w_out_attn[0], v_w_out_attn[0], "adamw_w_out_attn"),
        "w_o": _adamw(w_o[0], g_wo, m_w_o[0], v_w_o[0], "adamw_w_o"),
        "final_g": _adamw(final_g.reshape(1, D_MODEL), small[6:7], m_final_g.reshape(1, D_MODEL),
                          v_final_g.reshape(1, D_MODEL), "adamw_final_g"),
    }
    behind = [grad_x] + [u[0] for u in upd.values()]
    in_reduced = _add_chips(d_w4, from_sibling, to_chips.wait(behind)[1][0], where)
    (g_w_in,) = _share_halves([in_reduced])
    upd["w_in"] = _adamw(w_in[0], g_w_in, m_w_in[0], v_w_in[0], "adamw_w_in")

    names = ["norm_g", "w_in", "b_merge", "conv_w", "w_out_conv", "w_out_attn", "w_o", "final_g"]
    shapes = [norm_g.shape, w_in.shape, b_merge.shape, conv_w.shape, w_out_conv.shape, w_out_attn.shape,
              w_o.shape, final_g.shape]
    outs = [loss, grad_x.reshape(1, seq, D_MODEL)]
    for k in (3, 0, 1, 2):
        outs += [upd[n][k].reshape(s) for n, s in zip(names, shapes)]
    return tuple(outs)
```

```python
import functools

import numpy as np
import jax
import jax.numpy as jnp
from jax import lax
from jax.experimental import pallas as pl
from jax.experimental.pallas import tpu as pltpu

F32 = jnp.float32
BF16 = jnp.bfloat16
MXU_DTYPE = jnp.bfloat16
ACT_DTYPE = jnp.bfloat16

D_MODEL = 1024
N_HEADS = 16
HEAD_DIM = 64
QB = 128
N_RES = 16
LANES = 128
HP = N_HEADS * HEAD_DIM // LANES
IN_COLS = 10 * D_MODEL
SHARD_COLS = IN_COLS // 4
EPS = 1e-6
NEG = -1e30

ADAM_LR, ADAM_B1, ADAM_B2, ADAM_EPS, ADAM_WD, ADAM_STEP = 0.001, 0.9, 0.999, 1e-08, 0.01, 10

PATTERNS = {1: (16, 8), 4: (4, 32), 16: (1, 128)}

_NN = (((1,), (0,)), ((), ()))
_NT = (((1,), (1,)), ((), ()))


def _dot(a, b):
    return lax.dot_general(a.astype(MXU_DTYPE), b.astype(MXU_DTYPE), _NN, preferred_element_type=F32)


def _dot_nt(a, b):
    return lax.dot_general(a.astype(MXU_DTYPE), b.astype(MXU_DTYPE), _NT, preferred_element_type=F32)


def _split3(x):
    hi = x.astype(BF16)
    r1 = x - hi.astype(F32)
    mid = r1.astype(BF16)
    lo = (r1 - mid.astype(F32)).astype(BF16)
    return hi, mid, lo


def _select_cols(x, sel, terms):
    return sum(lax.dot_general(t, sel, _NN, preferred_element_type=F32) for t in _split3(x)[:terms])


def _sigmoid(z):
    return 1.0 / (1.0 + jnp.exp(-z))


def _head_expand_matrix():
    e = np.zeros((LANES, D_MODEL), np.float32)
    for h in range(N_HEADS):
        e[8 * h, HEAD_DIM * h:HEAD_DIM * (h + 1)] = 1.0
    return jnp.asarray(e, BF16)


def _head_sum_matrix():
    e = np.zeros((D_MODEL, LANES), np.float32)
    for h in range(N_HEADS):
        e[HEAD_DIM * h:HEAD_DIM * (h + 1), 8 * h:8 * (h + 1)] = 1.0
    return jnp.asarray(e, BF16)


def _attn_tables(d):
    g_n, rq = PATTERNS[d]
    q_n = g_n * rq
    gq, iq = np.arange(q_n) // rq, np.arange(q_n) % rq

    def tab(kn, base):
        k_n = g_n * kn
        gk, jk = np.arange(k_n) // kn, np.arange(k_n) % kn
        delta = g_n * (base + iq[:, None] - jk[None, :]) + gq[:, None] - gk[None, :]
        valid = (delta >= 0) & (delta <= QB)
        dist = np.where(valid, d * delta, 0).astype(np.float32)
        madd = np.where(valid, 0.0, NEG).astype(np.float32)
        return dist, madd

    d0, m0 = tab(rq if g_n == 1 else 2 * rq, 0)
    d1, m1 = tab(2 * rq, rq)
    return d0, m0, d1, m1


def _alibi_slopes():
    return jnp.exp2(-8.0 * jnp.arange(1, N_HEADS + 1, dtype=F32) / N_HEADS)


def _to_residue_major(x, tgt, after=None):
    s_n, c_n = x.shape
    lr = s_n // N_RES
    extra = [] if after is None else [after]

    def body(x_ref, t_ref, *rest):
        xo_ref, to_ref = rest[-2:]
        for r in range(N_RES):
            xo_ref[r] = x_ref[pl.ds(r, lr, stride=N_RES), :]
            to_ref[r] = t_ref[pl.ds(r, lr, stride=N_RES), :]

    nat = pl.BlockSpec((s_n, LANES), lambda j: (0, j))
    res = pl.BlockSpec((N_RES, lr, LANES), lambda j: (0, 0, j))
    xo, to = pl.pallas_call(
        body, grid=(c_n // LANES,),
        in_specs=[nat, nat] + [pl.BlockSpec((8, LANES), lambda j: (0, 0))] * len(extra),
        out_specs=[res, res],
        out_shape=[jax.ShapeDtypeStruct((N_RES, lr, c_n), F32)] * 2,
        name="perm_in",
    )(x, tgt, *extra)
    return xo.reshape(s_n, c_n), to.reshape(s_n, c_n)


def _to_natural(gx_lo, gx_hi):
    half_rows, c_n = gx_lo.shape
    lr = half_rows // (N_RES // 2)

    def body(lo_ref, hi_ref, o_ref):
        for r in range(N_RES):
            o_ref[pl.ds(r, lr, stride=N_RES), :] = lo_ref[r] if r < N_RES // 2 else hi_ref[r - N_RES // 2]

    half = pl.BlockSpec((N_RES // 2, lr, LANES), lambda j: (0, 0, j))
    return pl.pallas_call(
        body, grid=(c_n // LANES,),
        in_specs=[half, half],
        out_specs=pl.BlockSpec((2 * half_rows, LANES), lambda j: (0, j)),
        out_shape=jax.ShapeDtypeStruct((2 * half_rows, c_n), F32),
        name="perm_out",
    )(gx_lo.reshape(N_RES // 2, lr, c_n), gx_hi.reshape(N_RES // 2, lr, c_n))


def _rms_in(xp, norm_g):
    s_n, c_n = xp.shape
    tm = 512

    def body(x_ref, g_ref, u_ref, ut_ref):
        x = x_ref[...]
        r = lax.rsqrt(jnp.mean(x * x, axis=-1, keepdims=True) + EPS)
        u = x * r * g_ref[...]
        u_ref[...] = u.astype(u_ref.dtype)
        ut_ref[...] = u.T.astype(ut_ref.dtype)

    return pl.pallas_call(
        body, grid=(s_n // tm,),
        in_specs=[pl.BlockSpec((tm, c_n), lambda i: (i, 0)), pl.BlockSpec((1, c_n), lambda i: (0, 0))],
        out_specs=[pl.BlockSpec((tm, c_n), lambda i: (i, 0)), pl.BlockSpec((c_n, tm), lambda i: (0, i))],
        out_shape=[jax.ShapeDtypeStruct((s_n, c_n), ACT_DTYPE), jax.ShapeDtypeStruct((c_n, s_n), ACT_DTYPE)],
        name="rms_in",
    )(xp, norm_g)


def _in_proj(u, chip, w_own=None, w4=None, partial=None, others=(), after=None):
    s_n = u.shape[0]
    tn, cm = 512, 512
    per = SHARD_COLS // tn
    own = partial is None
    extra = [] if after is None else [after]

    def body(chip_ref, a_ref, b_ref, *rest):
        o_ref = rest[-1]
        b = b_ref[...]
        for c in range(s_n // cm):
            o_ref[c * cm:(c + 1) * cm, :] = _dot(a_ref[c * cm:(c + 1) * cm, :], b).astype(o_ref.dtype)

    def shard(n, chip_ref):
        if own:
            return chip_ref[0]
        mask = others[-1]
        for i, m in enumerate(others[:-1]):
            mask = jnp.where(n // per == i, m, mask)
        return jnp.bitwise_xor(chip_ref[0], mask)

    w_spec = (pl.BlockSpec((D_MODEL, tn), lambda n, c: (0, n)) if own else
              pl.BlockSpec((None, D_MODEL, tn), lambda n, c: (shard(n, c), 0, n % per)))
    return pl.pallas_call(
        body,
        grid_spec=pltpu.PrefetchScalarGridSpec(
            num_scalar_prefetch=1, grid=(per if own else len(others) * per,),
            in_specs=[pl.BlockSpec((s_n, D_MODEL), lambda n, c: (0, 0)), w_spec] + ([] if own else [_ANY])
            + [pl.BlockSpec((8, LANES), lambda n, c: (0, 0))] * len(extra),
            out_specs=pl.BlockSpec((s_n, tn), lambda n, c: (0, shard(n, c) * per + n % per))),
        out_shape=jax.ShapeDtypeStruct((s_n, IN_COLS), ACT_DTYPE),
        input_output_aliases={} if own else {3: 0},
        name="in_proj_own" if own else "in_proj_" + "_".join(str(m) for m in others),
    )(*([chip, u, w_own] if own else [chip, u, w4, partial]), *extra)


def _conv_terms(xc_ref, cg_ref, r, row, lr, cache):
    def a_of(q):
        if q not in cache:
            cache[q] = cg_ref[q].astype(F32) * xc_ref[q].astype(F32)
        return cache[q]

    def shift_down(v):
        return jnp.where(row >= 1, pltpu.roll(v, 1, 0), 0.0)

    a = a_of(r)
    am1 = a_of(r - 1) if r >= 1 else shift_down(a_of(N_RES - 1))
    am2 = a_of(r - 2) if r >= 2 else shift_down(a_of(N_RES - 2 + r))
    return a, am1, am2


def _conv_fwd(proj, conv_w):
    s_n = proj.shape[0]
    lr = s_n // N_RES
    pv = proj.reshape(N_RES, lr, IN_COLS)

    def body(xc_ref, bg_ref, cg_ref, zc_ref, w_ref, hc_ref, hct_ref):
        w = w_ref[...]
        row = lax.broadcasted_iota(jnp.int32, (lr, LANES), 0)
        products = {}
        for r in range(N_RES):
            a, am1, am2 = _conv_terms(xc_ref, cg_ref, r, row, lr, products)
            c = w[0:1] * am2 + w[1:2] * am1 + w[2:3] * a
            z = zc_ref[r].astype(F32)
            hc = z * _sigmoid(z) * bg_ref[r].astype(F32) * c
            hc_ref[r] = hc.astype(hc_ref.dtype)
            hct_ref[:, r * lr:(r + 1) * lr] = hc.T.astype(hct_ref.dtype)

    def col(part):
        return pl.BlockSpec((N_RES, lr, LANES), lambda j: (0, 0, part * 8 + j))

    hc, hct = pl.pallas_call(
        body, grid=(D_MODEL // LANES,),
        in_specs=[col(0), col(1), col(2), col(3), pl.BlockSpec((3, LANES), lambda j: (0, j))],
        out_specs=[pl.BlockSpec((N_RES, lr, LANES), lambda j: (0, 0, j)),
                   pl.BlockSpec((LANES, s_n), lambda j: (j, 0))],
        out_shape=[jax.ShapeDtypeStruct((N_RES, lr, D_MODEL), ACT_DTYPE),
                   jax.ShapeDtypeStruct((D_MODEL, s_n), ACT_DTYPE)],
        name="conv_fwd",
    )(pv, pv, pv, pv, conv_w)
    return hc.reshape(s_n, D_MODEL), hct


RES_PER_STEP = 8
ATTN_BATCH = 8

_BNT = (((2,), (2,)), ((0,), (0,)))
_BNN = (((2,), (1,)), ((0,), (0,)))


def _bdot(a, b, dims):
    return lax.dot_general(a.astype(MXU_DTYPE), b.astype(MXU_DTYPE), dims, preferred_element_type=F32)


def _pattern_view_shape(s_n, c_n, g_n, lead=()):
    lr = s_n // N_RES
    return (*lead, 4, 4, lr, c_n) if g_n == 4 else (*lead, N_RES, lr, c_n)


def _pattern_view(a, g_n, lead=()):
    return a.reshape(_pattern_view_shape(a.shape[-2], a.shape[-1], g_n, lead))


def _pattern_grid(g_n):
    return (N_RES // RES_PER_STEP if g_n == 1 else N_RES // g_n, HP)


def _pattern_spec(g_n, lr, col_of_hp, lead=()):
    z = (0,) * len(lead)
    if g_n == 16:
        return pl.BlockSpec((*lead, 16, lr, LANES), lambda r, hp: (*z, 0, 0, col_of_hp(hp)))
    if g_n == 4:
        return pl.BlockSpec((*lead, 4, None, lr, LANES), lambda r, hp: (*z, 0, r, 0, col_of_hp(hp)))
    return pl.BlockSpec((*lead, RES_PER_STEP, lr, LANES), lambda r, hp: (*z, r, 0, col_of_hp(hp)))


def _aligned(start, m):
    return start if isinstance(start, int) else pl.multiple_of(start, m)


class _Units:
    def __init__(self, g_n, rq):
        self.g_n, self.rq = g_n, rq
        self.per_res, self.paired = g_n == 1, rq == 8

    def plan(self, lr, size):
        if self.per_res:
            return [0], lr // self.rq - 1, lambda j: [pl.multiple_of(j * self.rq, self.rq)]
        step = 16 if self.paired else self.rq
        per = min(size // 2 if self.paired else size, lr // step)
        assert (lr // step) % per == 0
        return ([i * step for i in range(per)], lr // step // per - 1,
                lambda j: [pl.multiple_of((j * per + i) * step, step) for i in range(per)])

    def count(self, qs):
        return RES_PER_STEP if self.per_res else len(qs) * (2 if self.paired else 1)

    def _split(self, tiles, lo, rows):
        return tiles[:, lo:lo + rows].reshape(self.g_n * rows, LANES)

    def load_q(self, ref, qs):
        rq = self.rq
        if self.per_res:
            return ref[:, pl.ds(qs[0], rq), :]
        if self.paired:
            tiles = [ref[:, pl.ds(q, 16), :].astype(F32) for q in qs]
            return jnp.stack([self._split(t, lo, 8) for t in tiles for lo in (0, 8)])
        return jnp.stack([ref[:, pl.ds(q, rq), :].reshape(self.g_n * rq, LANES) for q in qs])

    def _key_rows(self, q, at_start):
        return (0, 2 * self.rq) if at_start else (_aligned(q - self.rq, self.rq), 2 * self.rq)

    def load_k(self, ref, qs, first):
        rq = self.rq
        if self.per_res:
            return ref[:, pl.ds(0, rq), :] if first else ref[:, pl.ds(_aligned(qs[0] - rq, rq), 2 * rq), :]
        if self.paired:
            out = []
            for i, q in enumerate(qs):
                if first and i == 0:
                    t = ref[:, 0:16, :].astype(F32)
                    out += [self._split(t, 0, 16)] * 2
                else:
                    t = ref[:, pl.ds(_aligned(q - 16, 16), 32), :].astype(F32)
                    out += [self._split(t, 8, 16), self._split(t, 16, 16)]
            return jnp.stack(out)
        rows = [self._key_rows(q, first and i == 0) for i, q in enumerate(qs)]
        return jnp.stack([ref[:, pl.ds(k0, n), :].reshape(self.g_n * n, LANES) for k0, n in rows])

    def store_q(self, ref, qs, val, add=False, lead=()):
        if self.per_res:
            pieces = [(qs[0], self.rq, val)]
        elif self.paired:
            pieces = [(q, 16, jnp.concatenate([val[2 * i].reshape(self.g_n, 8, LANES),
                                               val[2 * i + 1].reshape(self.g_n, 8, LANES)], axis=1))
                      for i, q in enumerate(qs)]
        else:
            pieces = [(q, self.rq, val[i].reshape(self.g_n, self.rq, LANES)) for i, q in enumerate(qs)]
        for start, rows, v in pieces:
            idx = (*lead, slice(None), pl.ds(start, rows), slice(None))
            ref[idx] = (ref[idx] + v if add else v).astype(ref.dtype)

    def add_k(self, ref, qs, val, first):
        rq = self.rq
        if self.per_res:
            k0, n = (0, rq) if first else (_aligned(qs[0] - rq, rq), 2 * rq)
            ref[:, pl.ds(k0, n), :] += val
            return
        if self.paired:
            starts = [s for i, q in enumerate(qs)
                      for s in ((0, 0) if first and i == 0 else (_aligned(q - 8, 8), q))]
            rows = [(s, 16) for s in starts]
        else:
            rows = [self._key_rows(q, first and i == 0) for i, q in enumerate(qs)]
        for b, (k0, n) in enumerate(rows):
            ref[:, pl.ds(k0, n), :] += val[b].reshape(self.g_n, n, LANES)


def _batch_bias(un, qs, at_start, first_ref, general_ref):
    if not at_start:
        return general_ref[...][None]
    if un.per_res:
        return first_ref[...][None]
    return jnp.concatenate([first_ref[...][None]] + [general_ref[...][None]] * (un.count(qs) - 1), axis=0)


def _stack_heads(x, low):
    zero = jnp.zeros_like(x)
    return jnp.concatenate([jnp.where(low, x, zero), jnp.where(low, zero, x)], axis=1)


def _attn_fwd(proj, slopes, d):
    g_n, rq = PATTERNS[d]
    un = _Units(g_n, rq)
    s_n = proj.shape[0]
    lr = s_n // N_RES
    nb = lr // rq
    q_n = g_n * rq
    d0, m0, d1, m1 = _attn_tables(d)
    first, n_more, later = un.plan(lr, ATTN_BATCH)

    def body(sl_ref, q_ref, k_ref, v_ref, d0_ref, m0_ref, d1_ref, m1_ref, o_ref, lse_ref, b0_ref, b1_ref):
        hp = pl.program_id(1)

        @pl.when(hp == 0)
        def _():
            lse_ref[...] = jnp.zeros(lse_ref.shape, F32)

        for h in (0, 1):
            slope = sl_ref[2 * hp + h]
            b0_ref[h * q_n:(h + 1) * q_n, :] = m0_ref[...] - slope * d0_ref[...]
            b1_ref[h * q_n:(h + 1) * q_n, :] = m1_ref[...] - slope * d1_ref[...]

        lane = lax.broadcasted_iota(jnp.int32, (1, q_n, LANES), 2)
        low = lane < HEAD_DIM
        grp = lane // 8

        def batch(qs, at_start):
            qq = _stack_heads(un.load_q(q_ref, qs) * 0.125, low)
            s = _bdot(qq, un.load_k(k_ref, qs, at_start), _BNT) + _batch_bias(un, qs, at_start, b0_ref, b1_ref)
            m = jnp.max(s, axis=2, keepdims=True)
            p = jnp.exp(s - m)
            l = jnp.sum(p, axis=2, keepdims=True)
            o = _bdot(p, un.load_k(v_ref, qs, at_start), _BNN) * (1.0 / l)
            lse = m + jnp.log(l)
            un.store_q(o_ref, qs, jnp.where(low, o[:, :q_n], o[:, q_n:]))
            upd = jnp.where(grp == 2 * hp, lse[:, :q_n], 0.0) + jnp.where(grp == 2 * hp + 1, lse[:, q_n:], 0.0)
            un.store_q(lse_ref, qs, upd, add=True)

        batch(first, True)

        def more(j, carry):
            batch(later(j), False)
            return carry

        lax.fori_loop(1, 1 + n_more, more, 0)

    pv = _pattern_view(proj, g_n)
    full = lambda a: pl.BlockSpec(a.shape, lambda r, hp: (0, 0))
    o, lse = pl.pallas_call(
        body, grid=_pattern_grid(g_n),
        in_specs=[pl.BlockSpec(memory_space=pltpu.SMEM),
                  _pattern_spec(g_n, lr, lambda hp: 32 + hp),
                  _pattern_spec(g_n, lr, lambda hp: 40 + hp),
                  _pattern_spec(g_n, lr, lambda hp: 48 + hp),
                  full(d0), full(m0), full(d1), full(m1)],
        out_specs=[_pattern_spec(g_n, lr, lambda hp: hp), _pattern_spec(g_n, lr, lambda hp: 0)],
        out_shape=[jax.ShapeDtypeStruct(_pattern_view_shape(s_n, D_MODEL, g_n), ACT_DTYPE),
                   jax.ShapeDtypeStruct(_pattern_view_shape(s_n, LANES, g_n), F32)],
        scratch_shapes=[pltpu.VMEM((2 * q_n, d0.shape[1]), F32), pltpu.VMEM((2 * q_n, 2 * q_n), F32)],
        name=f"attn_fwd_d{d}",
    )(slopes, pv, pv, pv, d0, m0, d1, m1)
    return o.reshape(s_n, D_MODEL), lse.reshape(s_n, LANES)


def _attn_combine(outs, lses, proj):
    s_n = proj.shape[0]
    tm = 512

    def body(o1_ref, o2_ref, o3_ref, l1_ref, l2_ref, l3_ref, za_ref, e_ref, o_ref, lse_ref, ha_ref, hat_ref):
        ls = [l1_ref[...], l2_ref[...], l3_ref[...]]
        mx = jnp.maximum(jnp.maximum(ls[0], ls[1]), ls[2])
        den = sum(jnp.exp(l - mx) for l in ls)
        lse = mx + jnp.log(den)
        lse_ref[...] = lse
        o = jnp.zeros((tm, D_MODEL), F32)
        for l, oref in zip(ls, (o1_ref, o2_ref, o3_ref)):
            o = o + _select_cols(jnp.exp(l - lse), e_ref[...], terms=2) * oref[...].astype(F32)
        o_ref[...] = o.astype(o_ref.dtype)
        z = za_ref[...].astype(F32)
        ha = z * _sigmoid(z) * o
        ha_ref[...] = ha.astype(ha_ref.dtype)
        hat_ref[...] = ha.T.astype(hat_ref.dtype)

    row = lambda w: pl.BlockSpec((tm, w), lambda i: (i, 0))
    return pl.pallas_call(
        body, grid=(s_n // tm,),
        in_specs=[row(D_MODEL)] * 3 + [row(LANES)] * 3
        + [pl.BlockSpec((tm, D_MODEL), lambda i: (i, 7)), pl.BlockSpec((LANES, D_MODEL), lambda i: (0, 0))],
        out_specs=[row(D_MODEL), row(LANES), row(D_MODEL), pl.BlockSpec((D_MODEL, tm), lambda i: (0, i))],
        out_shape=[jax.ShapeDtypeStruct((s_n, D_MODEL), ACT_DTYPE), jax.ShapeDtypeStruct((s_n, LANES), F32),
                   jax.ShapeDtypeStruct((s_n, D_MODEL), ACT_DTYPE), jax.ShapeDtypeStruct((D_MODEL, s_n), ACT_DTYPE)],
        name="attn_combine",
    )(*outs, *lses, proj, _head_expand_matrix())


CHAIN_ROWS = 256


def _row_chains(tm):
    return [slice(r, r + CHAIN_ROWS) for r in range(0, tm, CHAIN_ROWS)]


def _gates(gc_ref, ga_ref, b_ref, rows):
    b = b_ref[...]
    gc = _sigmoid(gc_ref[rows, :].astype(F32) + b[:, :D_MODEL])
    ga = _sigmoid(ga_ref[rows, :].astype(F32) + b[:, D_MODEL:])
    return gc, ga


def _merge_loss(hc, ha, woc, woa, wo, proj, b_merge, xp, final_g, tgt):
    s_n = xp.shape[0]
    tm = 512

    def body(hc_ref, ha_ref, woc_ref, woa_ref, wo_ref, gc_ref, ga_ref, b_ref, x_ref, gf_ref, t_ref,
             yc_ref, ya_ref, mgt_ref, dhb_ref, dgf_ref, loss_ref):
        i = pl.program_id(0)

        @pl.when(i == 0)
        def _():
            dgf_ref[...] = jnp.zeros(dgf_ref.shape, F32)
            loss_ref[...] = jnp.zeros(loss_ref.shape, F32)

        gf = gf_ref[...]
        for rows in _row_chains(tm):
            yc = _dot(hc_ref[rows, :], woc_ref[...])
            ya = _dot(ha_ref[rows, :], woa_ref[...])
            gc, ga = _gates(gc_ref, ga_ref, b_ref, rows)
            mg = gc * yc + ga * ya
            yc_ref[rows, :] = yc.astype(yc_ref.dtype)
            ya_ref[rows, :] = ya.astype(ya_ref.dtype)
            mgt_ref[:, rows] = mg.T.astype(mgt_ref.dtype)
            h2 = x_ref[rows, :] + _dot(mg, wo_ref[...])
            r2 = lax.rsqrt(jnp.mean(h2 * h2, axis=-1, keepdims=True) + EPS)
            nrm = h2 * r2
            err = nrm * gf - t_ref[rows, :]
            e2 = (err * err).reshape(-1, 8, D_MODEL).sum(axis=0)
            loss_ref[...] += sum(e2[:, c * LANES:(c + 1) * LANES] for c in range(D_MODEL // LANES))
            dy = err * (1.0 / D_MODEL)
            dgf_ref[...] += jnp.sum(dy * nrm, axis=0, keepdims=True)
            dn = dy * gf
            dh2 = r2 * (dn - nrm * jnp.mean(dn * nrm, axis=-1, keepdims=True))
            dhb_ref[rows, :] = dh2.astype(dhb_ref.dtype)

    row = pl.BlockSpec((tm, D_MODEL), lambda i: (i, 0))
    wsp = pl.BlockSpec((D_MODEL, D_MODEL), lambda i: (0, 0))
    vec = lambda w: pl.BlockSpec((1, w), lambda i: (0, 0))
    act = jax.ShapeDtypeStruct((s_n, D_MODEL), ACT_DTYPE)
    return pl.pallas_call(
        body, grid=(s_n // tm,),
        in_specs=[row, row, wsp, wsp, wsp,
                  pl.BlockSpec((tm, D_MODEL), lambda i: (i, 8)), pl.BlockSpec((tm, D_MODEL), lambda i: (i, 9)),
                  vec(2 * D_MODEL), row, vec(D_MODEL), row],
        out_specs=[row, row, pl.BlockSpec((D_MODEL, tm), lambda i: (0, i)), row,
                   vec(D_MODEL), pl.BlockSpec((8, LANES), lambda i: (0, 0))],
        out_shape=[act, act, jax.ShapeDtypeStruct((D_MODEL, s_n), ACT_DTYPE), act,
                   jax.ShapeDtypeStruct((1, D_MODEL), F32), jax.ShapeDtypeStruct((8, LANES), F32)],
        name="merge_loss",
    )(hc, ha, woc, woa, wo, proj, proj, b_merge, xp, final_g, tgt)


def _merge_bwd(dh2b, wo, woc, woa, yc, ya, proj, b_merge, o):
    s_n = dh2b.shape[0]
    tm = 512

    def body(dh_ref, wo_ref, woc_ref, woa_ref, yc_ref, ya_ref, gc_ref, ga_ref, b_ref, o_ref, za_ref, e_ref,
             dyc_ref, dya_ref, dhc_ref, do_ref, dsum_ref, db3_ref, dbias_ref):
        i = pl.program_id(0)

        @pl.when(i == 0)
        def _():
            dbias_ref[...] = jnp.zeros(dbias_ref.shape, F32)

        for rows in _row_chains(tm):
            dmg = _dot_nt(dh_ref[rows, :], wo_ref[...])
            gc, ga = _gates(gc_ref, ga_ref, b_ref, rows)
            dgc = dmg * yc_ref[rows, :].astype(F32) * gc * (1.0 - gc)
            dga = dmg * ya_ref[rows, :].astype(F32) * ga * (1.0 - ga)
            dbias_ref[:, :D_MODEL] += jnp.sum(dgc, axis=0, keepdims=True)
            dbias_ref[:, D_MODEL:] += jnp.sum(dga, axis=0, keepdims=True)
            dyc = dmg * gc
            dya = dmg * ga
            dyc_ref[rows, :] = dyc.astype(dyc_ref.dtype)
            dya_ref[rows, :] = dya.astype(dya_ref.dtype)
            dhc_ref[rows, :] = _dot_nt(dyc, woc_ref[...]).astype(dhc_ref.dtype)
            dha = _dot_nt(dya, woa_ref[...])
            z = za_ref[rows, :].astype(F32)
            sg = _sigmoid(z)
            ov = o_ref[rows, :].astype(F32)
            dout = dha * z * sg
            do_ref[rows, :] = dout.astype(do_ref.dtype)
            dsum_ref[rows, :] = _select_cols(dout * ov, e_ref[...], terms=2)
            db3_ref[0, rows, :] = (dha * ov * sg * (1.0 + z * (1.0 - sg))).astype(db3_ref.dtype)
            db3_ref[1, rows, :] = dgc.astype(db3_ref.dtype)
            db3_ref[2, rows, :] = dga.astype(db3_ref.dtype)

    row = pl.BlockSpec((tm, D_MODEL), lambda i: (i, 0))
    wsp = pl.BlockSpec((D_MODEL, D_MODEL), lambda i: (0, 0))
    act = jax.ShapeDtypeStruct((s_n, D_MODEL), ACT_DTYPE)
    return pl.pallas_call(
        body, grid=(s_n // tm,),
        in_specs=[row, wsp, wsp, wsp, row, row,
                  pl.BlockSpec((tm, D_MODEL), lambda i: (i, 8)), pl.BlockSpec((tm, D_MODEL), lambda i: (i, 9)),
                  pl.BlockSpec((1, 2 * D_MODEL), lambda i: (0, 0)), row,
                  pl.BlockSpec((tm, D_MODEL), lambda i: (i, 7)), pl.BlockSpec((D_MODEL, LANES), lambda i: (0, 0))],
        out_specs=[row, row, row, row, pl.BlockSpec((tm, LANES), lambda i: (i, 0)),
                   pl.BlockSpec((3, tm, D_MODEL), lambda i: (0, i, 0)),
                   pl.BlockSpec((1, 2 * D_MODEL), lambda i: (0, 0))],
        out_shape=[act, act, act, act, jax.ShapeDtypeStruct((s_n, LANES), F32),
                   jax.ShapeDtypeStruct((3, s_n, D_MODEL), ACT_DTYPE),
                   jax.ShapeDtypeStruct((1, 2 * D_MODEL), F32)],
        name="merge_bwd",
    )(dh2b, wo, woc, woa, yc, ya, proj, proj, b_merge, o, proj, _head_sum_matrix())


def _mm_lhs_resident(a, b, tn, name):
    m_n, k_n = a.shape
    n_n = b.shape[1]

    def body(a_ref, b_ref, o_ref):
        o_ref[...] = _dot(a_ref[...], b_ref[...])

    return pl.pallas_call(
        body, grid=(n_n // tn,),
        in_specs=[pl.BlockSpec((m_n, k_n), lambda n: (0, 0)), pl.BlockSpec((k_n, tn), lambda n: (0, n))],
        out_specs=pl.BlockSpec((m_n, tn), lambda n: (0, n)),
        out_shape=jax.ShapeDtypeStruct((m_n, n_n), F32),
        name=name,
    )(a, b)


def _conv_bwd(proj, conv_w, dhc):
    s_n = proj.shape[0]
    lr = s_n // N_RES
    pv = proj.reshape(N_RES, lr, IN_COLS)

    def body(xc_ref, bg_ref, cg_ref, zc_ref, w_ref, dhc_ref, da4_ref, dw_ref, dc_ref):
        w = w_ref[...]
        row = lax.broadcasted_iota(jnp.int32, (lr, LANES), 0)
        dw = [jnp.zeros((1, LANES), F32) for _ in range(3)]
        products = {}
        for r in range(N_RES):
            a, am1, am2 = _conv_terms(xc_ref, cg_ref, r, row, lr, products)
            c = w[0:1] * am2 + w[1:2] * am1 + w[2:3] * a
            z = zc_ref[r].astype(F32)
            sg = _sigmoid(z)
            sz = z * sg
            bg = bg_ref[r].astype(F32)
            dh = dhc_ref[r].astype(F32)
            da4_ref[1, r] = (dh * sz * c).astype(da4_ref.dtype)
            da4_ref[3, r] = (dh * bg * c * sg * (1.0 + z * (1.0 - sg))).astype(da4_ref.dtype)
            dc = dh * sz * bg
            dc_ref[r] = dc
            dw[0] = dw[0] + jnp.sum(dc * am2, axis=0, keepdims=True)
            dw[1] = dw[1] + jnp.sum(dc * am1, axis=0, keepdims=True)
            dw[2] = dw[2] + jnp.sum(dc * a, axis=0, keepdims=True)
        dw_ref[0:1, :] = dw[0]
        dw_ref[1:2, :] = dw[1]
        dw_ref[2:3, :] = dw[2]

        def shift_up(v):
            return jnp.where(row < lr - 1, pltpu.roll(v, lr - 1, 0), 0.0)

        for r in range(N_RES):
            dp1 = dc_ref[r + 1] if r + 1 < N_RES else shift_up(dc_ref[0])
            dp2 = dc_ref[r + 2] if r + 2 < N_RES else shift_up(dc_ref[r + 2 - N_RES])
            da = w[2:3] * dc_ref[r] + w[1:2] * dp1 + w[0:1] * dp2
            da4_ref[0, r] = (da * cg_ref[r].astype(F32)).astype(da4_ref.dtype)
            da4_ref[2, r] = (da * xc_ref[r].astype(F32)).astype(da4_ref.dtype)

    def col(part):
        return pl.BlockSpec((N_RES, lr, LANES), lambda j: (0, 0, part * 8 + j))

    da4, dw = pl.pallas_call(
        body, grid=(D_MODEL // LANES,),
        in_specs=[col(0), col(1), col(2), col(3), pl.BlockSpec((3, LANES), lambda j: (0, j)),
                  pl.BlockSpec((N_RES, lr, LANES), lambda j: (0, 0, j))],
        out_specs=[pl.BlockSpec((4, N_RES, lr, LANES), lambda j: (0, 0, 0, j)),
                   pl.BlockSpec((3, LANES), lambda j: (0, j))],
        out_shape=[jax.ShapeDtypeStruct((4, N_RES, lr, D_MODEL), ACT_DTYPE),
                   jax.ShapeDtypeStruct((3, D_MODEL), F32)],
        scratch_shapes=[pltpu.VMEM((N_RES, lr, LANES), F32)],
        name="conv_bwd",
    )(pv, pv, pv, pv, conv_w, dhc.reshape(N_RES, lr, D_MODEL))
    return da4.reshape(4, s_n, D_MODEL), dw


def _attn_bwd(proj, dout, lse, dsum, slopes, d, prev=None):
    g_n, rq = PATTERNS[d]
    un = _Units(g_n, rq)
    s_n = proj.shape[0]
    lr = s_n // N_RES
    nb = lr // rq
    q_n = g_n * rq
    d0, m0, d1, m1 = (np.ascontiguousarray(t.T) for t in _attn_tables(d))
    first, n_more, later = un.plan(lr, ATTN_BATCH)
    bsz = un.count(first)
    gd = RES_PER_STEP if un.per_res else g_n

    def body(sl_ref, q_ref, k_ref, v_ref, do_ref, lse_ref, ds_ref, d0_ref, m0_ref, d1_ref, m1_ref, *rest):
        prev_ref = rest[0] if prev is not None else None
        out_ref, b0_ref, b1_ref, lt_ref, dt_ref, dk_ref, dv_ref = rest[-7:]
        hp = pl.program_id(1)
        for h in (0, 1):
            slope = sl_ref[2 * hp + h]
            b0_ref[:, h * q_n:(h + 1) * q_n] = m0_ref[...] - slope * d0_ref[...]
            b1_ref[:, h * q_n:(h + 1) * q_n] = m1_ref[...] - slope * d1_ref[...]
        if prev is None:
            dk_ref[...] = jnp.zeros(dk_ref.shape, F32)
            dv_ref[...] = jnp.zeros(dv_ref.shape, F32)
        else:
            out_ref[0] = prev_ref[0]
            dk_ref[...] = prev_ref[1].astype(F32)
            dv_ref[...] = prev_ref[2].astype(F32)
        low = lax.broadcasted_iota(jnp.int32, (1, q_n, LANES), 2) < HEAD_DIM
        row16 = pl.multiple_of(16 * hp, 16)

        def query_rows(stat_ref, t_ref, qs):
            tiles = un.load_q(stat_ref, qs)
            for b in range(bsz):
                t_ref[b] = tiles[b].T
            t16 = t_ref[:, pl.ds(row16, 16), :]
            return jnp.concatenate([t16[:, 0:1, :], t16[:, 8:9, :]], axis=2)

        def batch(qs, at_start):
            qq = _stack_heads(un.load_q(q_ref, qs) * 0.125, low)
            dd = _stack_heads(un.load_q(do_ref, qs), low)
            ks = un.load_k(k_ref, qs, at_start)
            vs = un.load_k(v_ref, qs, at_start)
            lrow = query_rows(lse_ref, lt_ref, qs)
            drow = query_rows(ds_ref, dt_ref, qs)
            pt = jnp.exp(_bdot(ks, qq, _BNT) + _batch_bias(un, qs, at_start, b0_ref, b1_ref) - lrow)
            dst = pt * (_bdot(vs, dd, _BNT) - drow)
            un.add_k(dv_ref, qs, _bdot(pt, dd, _BNN), at_start)
            un.add_k(dk_ref, qs, _bdot(dst, qq, _BNN), at_start)
            dq = _bdot(jnp.swapaxes(dst, 1, 2), ks, _BNN)
            un.store_q(out_ref, qs, jnp.where(low, dq[:, :q_n], dq[:, q_n:]) * 0.125, add=prev is not None,
                       lead=(0,))

        batch(first, True)

        def more(j, carry):
            batch(later(j), False)
            return carry

        lax.fori_loop(1, 1 + n_more, more, 0)
        out_ref[1] = dk_ref[...].astype(out_ref.dtype)
        out_ref[2] = dv_ref[...].astype(out_ref.dtype)

    pv = _pattern_view(proj, g_n)
    full = lambda a: pl.BlockSpec(a.shape, lambda r, hp: (0, 0))
    whole = _pattern_spec(g_n, lr, lambda hp: hp, lead=(3,))
    out = pl.pallas_call(
        body, grid=_pattern_grid(g_n),
        in_specs=[pl.BlockSpec(memory_space=pltpu.SMEM),
                  _pattern_spec(g_n, lr, lambda hp: 32 + hp),
                  _pattern_spec(g_n, lr, lambda hp: 40 + hp),
                  _pattern_spec(g_n, lr, lambda hp: 48 + hp),
                  _pattern_spec(g_n, lr, lambda hp: hp),
                  _pattern_spec(g_n, lr, lambda hp: 0),
                  _pattern_spec(g_n, lr, lambda hp: 0),
                  full(d0), full(m0), full(d1), full(m1)] + ([] if prev is None else [whole]),
        out_specs=whole,
        out_shape=jax.ShapeDtypeStruct(_pattern_view_shape(s_n, D_MODEL, g_n, lead=(3,)), ACT_DTYPE),
        scratch_shapes=[pltpu.VMEM((d0.shape[0], 2 * q_n), F32), pltpu.VMEM((2 * q_n, 2 * q_n), F32),
                        pltpu.VMEM((bsz, LANES, q_n), F32), pltpu.VMEM((bsz, LANES, q_n), F32),
                        pltpu.VMEM((gd, lr, LANES), F32), pltpu.VMEM((gd, lr, LANES), F32)],
        name=f"attn_bwd_d{d}",
    )(slopes, pv, pv, pv, _pattern_view(dout, g_n), _pattern_view(lse, g_n), _pattern_view(dsum, g_n),
      d0, m0, d1, m1, *([] if prev is None else [_pattern_view(prev, g_n, lead=(3,))]))
    return out.reshape(3, s_n, D_MODEL)


def _part_index(step, per, lo, n):
    return jnp.clip(step // per - lo, 0, n - 1)


def _dw_in(ut, da4, dc3, db3):
    s_n = ut.shape[1]
    tn = 512
    per = D_MODEL // tn
    shard_blocks = SHARD_COLS // tn

    def body(a_ref, p0_ref, p1_ref, p2_ref, o_ref):
        part = pl.program_id(0) // per

        @pl.when(part < 4)
        def _():
            o_ref[...] = _dot(a_ref[...], p0_ref[...])

        @pl.when((part >= 4) & (part < 7))
        def _():
            o_ref[...] = _dot(a_ref[...], p1_ref[...])

        @pl.when(part >= 7)
        def _():
            o_ref[...] = _dot(a_ref[...], p2_ref[...])

    def pspec(lo, n):
        def index(j):
            part = j // per
            col = jnp.where(part < lo, 0, jnp.where(part >= lo + n, per - 1, j % per))
            return _part_index(j, per, lo, n), 0, col
        return pl.BlockSpec((None, s_n, tn), index)

    return pl.pallas_call(
        body, grid=(IN_COLS // tn,),
        in_specs=[pl.BlockSpec((D_MODEL, s_n), lambda j: (0, 0), pipeline_mode=pl.Buffered(1)),
                  pspec(0, 4), pspec(4, 3), pspec(7, 3)],
        out_specs=pl.BlockSpec((None, D_MODEL, tn), lambda j: (j // shard_blocks, 0, j % shard_blocks)),
        out_shape=jax.ShapeDtypeStruct((4, D_MODEL, SHARD_COLS), F32),
        name="dw_in",
    )(ut, da4, dc3, db3)


def _input_grad(da4, dc3, db3, w4, xp, norm_g, dh2, row0, rows):
    tm, tk = 256, 512
    per = D_MODEL // tk
    shard_blocks = SHARD_COLS // tk
    m0 = row0 // tm

    def body(p0_ref, p1_ref, p2_ref, w_ref, x_ref, g_ref, dh_ref, gx_ref, dg_ref):
        @pl.when(pl.program_id(0) == 0)
        def _():
            dg_ref[...] = jnp.zeros(dg_ref.shape, F32)

        du = None
        for k in range(IN_COLS // tk):
            part, cols = k // per, pl.ds((k % per) * tk, tk)
            ref, slot = (p0_ref, part) if part < 4 else (p1_ref, part - 4) if part < 7 else (p2_ref, part - 7)
            d = _dot_nt(ref[slot, :, cols], w_ref[k // shard_blocks, :, pl.ds((k % shard_blocks) * tk, tk)])
            du = d if du is None else du + d
        x = x_ref[...]
        r = lax.rsqrt(jnp.mean(x * x, axis=-1, keepdims=True) + EPS)
        nrm = x * r
        dg_ref[...] += jnp.sum(du * nrm, axis=0, keepdims=True)
        dn = du * g_ref[...]
        gx_ref[...] = dh_ref[...].astype(F32) + r * (dn - nrm * jnp.mean(dn * nrm, axis=-1, keepdims=True))

    def pspec(n):
        return pl.BlockSpec((n, tm, D_MODEL), lambda m: (0, m0 + m, 0))

    row_in = pl.BlockSpec((tm, D_MODEL), lambda m: (m0 + m, 0))
    vec = pl.BlockSpec((1, D_MODEL), lambda m: (0, 0))
    return pl.pallas_call(
        body, grid=(rows // tm,),
        in_specs=[pspec(4), pspec(3), pspec(3),
                  pl.BlockSpec(w4.shape, lambda m: (0, 0, 0), pipeline_mode=pl.Buffered(1)),
                  row_in, vec, row_in],
        out_specs=[pl.BlockSpec((tm, D_MODEL), lambda m: (m, 0)), vec],
        out_shape=[jax.ShapeDtypeStruct((rows, D_MODEL), F32), jax.ShapeDtypeStruct((1, D_MODEL), F32)],
        name="input_grad",
    )(da4, dc3, db3, w4, xp, norm_g, dh2)


class _Step:
    def __init__(self, x, tgt, norm_g, chip, after=None):
        self.norm_g, self.chip = norm_g, chip
        self.slopes = _alibi_slopes()
        self.xp, self.tp = _to_residue_major(x, tgt, after)
        self.u, self.ut = _rms_in(self.xp, norm_g)

    def project_own(self, w_own, after=None):
        self.proj_own = _in_proj(self.u, self.chip, w_own=w_own, after=after)

    def project(self, w4, others):
        self.proj_own = _in_proj(self.u, self.chip, w4=w4, partial=self.proj_own, others=others)

    def mixers(self, w4, taps):
        self.w4, self.taps, self.proj = w4, taps, self.proj_own
        self.hc, self.hct = _conv_fwd(self.proj, taps)
        fwd = [_attn_fwd(self.proj, self.slopes, d) for d in PATTERNS]
        self.o, self.lse, self.ha, self.hat = _attn_combine([f[0] for f in fwd], [f[1] for f in fwd], self.proj)

    def merge_and_loss(self, woc, woa, wo, b_merge, final_g):
        self.woc, self.woa, self.wo, self.b_merge = woc, woa, wo, b_merge
        (self.yc, self.ya, self.mgt, self.dh2b, self.d_final_g, self.loss8) = _merge_loss(
            self.hc, self.ha, woc, woa, wo, self.proj, b_merge, self.xp, final_g, self.tp)

    def out_weight_grads(self):
        (dyc, dya, self.dhc, self.dout, self.dsum, self.db3, self.d_bias) = _merge_bwd(
            self.dh2b, self.wo, self.woc, self.woa, self.yc, self.ya, self.proj, self.b_merge, self.o)
        d_wo = _mm_lhs_resident(self.mgt, self.dh2b, 256, "dw_o")
        d_woc = _mm_lhs_resident(self.hct, dyc, 256, "dw_out_conv")
        d_woa = _mm_lhs_resident(self.hat, dya, 256, "dw_out_attn")
        return d_woc, d_woa, d_wo

    def conv_grads(self, after=0.0):
        self.da4, self.d_taps = _conv_bwd(self.proj, self.taps + after, self.dhc)

    def in_weight_grad(self, after=0.0):
        slopes = self.slopes + after
        self.dc3 = None
        for d in PATTERNS:
            self.dc3 = _attn_bwd(self.proj, self.dout, self.lse, self.dsum, slopes, d, prev=self.dc3)
        return _dw_in(self.ut, self.da4, self.dc3, self.db3)

    def input_grad(self, half, after=0.0):
        rows = self.xp.shape[0] // 2
        return _input_grad(self.da4, self.dc3, self.db3, self.w4, self.xp, self.norm_g + after, self.dh2b,
                           half * rows, rows)


def _local_grads(x, tgt, norm_g, w4, b_merge, conv_w, woc, woa, wo, final_g):
    st = _Step(x, tgt, norm_g, jnp.zeros((1,), jnp.int32))
    st.project_own(w4[0])
    st.project(w4, (2, 1))
    st.project(w4, (3,))
    st.mixers(w4, conv_w)
    st.merge_and_loss(woc, woa, wo, b_merge, final_g)
    d_woc, d_woa, d_wo = st.out_weight_grads()
    st.conv_grads()
    d_w4 = st.in_weight_grad()
    gx_lo, dg_lo = st.input_grad(0)
    gx_hi, dg_hi = st.input_grad(1)
    return (st.loss8, _to_natural(gx_lo, gx_hi), dg_lo + dg_hi, d_w4, st.d_bias, st.d_taps, d_woc, d_woa, d_wo,
            st.d_final_g)


MESH = pl.DeviceIdType.MESH
_CHIP_FLIPS = ((1, 0), (0, 1), (1, 1))
_ANY = pl.BlockSpec(memory_space=pl.ANY)


def _place():
    return lax.axis_index("x"), lax.axis_index("y"), lax.axis_index("c")


def _flip(v, f):
    return 1 - v if f else v


def _remote(src, dst, send_sems, recv_sems, k, device):
    return pltpu.make_async_remote_copy(src_ref=src, dst_ref=dst, send_sem=send_sems.at[k], recv_sem=recv_sems.at[k],
                                        device_id=device, device_id_type=MESH)


def _place_shard(w, chip):
    rows, cols = w.shape
    tm = 128

    def body(chip_ref, w_ref, o_ref):
        o_ref[0] = w_ref[...].astype(o_ref.dtype)

    return pl.pallas_call(
        body,
        grid_spec=pltpu.PrefetchScalarGridSpec(
            num_scalar_prefetch=1, grid=(rows // tm,),
            in_specs=[pl.BlockSpec((tm, cols), lambda i, chip_ref: (i, 0))],
            out_specs=pl.BlockSpec((1, tm, cols), lambda i, chip_ref: (chip_ref[0], i, 0))),
        out_shape=jax.ShapeDtypeStruct((4, rows, cols), MXU_DTYPE),
        name="place_shard",
    )(chip, w)


def _gather_copies_to(flips):
    def copies(arrs, _, send_sems, recv_sems):
        x, y, c = _place()
        out = []
        for a, arr in enumerate(arrs):
            h = arr.shape[1] // 2
            mine = arr.at[2 * x + y, pl.ds(pl.multiple_of(c * h, 8), h)]
            for i, t in enumerate(flips):
                fx, fy = _CHIP_FLIPS[t]
                out.append(_remote(mine, mine, send_sems, recv_sems, len(flips) * a + i,
                                   (_flip(x, fx), _flip(y, fy), c)))
        return out
    return copies


def _forward_to_sibling(arrs, flips=(0, 1, 2)):
    n = len(arrs)

    def body(*refs):
        outs = refs[n:2 * n]
        send_sems, recv_sems = refs[2 * n:]
        x, y, c = _place()
        sibling = (x, y, 1 - c)
        started = []
        for a in range(n):
            h = outs[a].shape[1] // 2
            rows = pl.ds(pl.multiple_of(c * h, 8), h)
            for t in flips:
                fx, fy = _CHIP_FLIPS[t]
                landed = outs[a].at[2 * _flip(x, fx) + _flip(y, fy), rows]
                cp = _remote(landed, landed, send_sems, recv_sems, 3 * a + t, sibling)
                cp.start()
                started.append(cp)
        for a in range(n):
            h = outs[a].shape[1] // 2
            rows = pl.ds(pl.multiple_of((1 - c) * h, 8), h)
            for t in flips:
                fx, fy = _CHIP_FLIPS[t]
                handed = outs[a].at[2 * _flip(x, fx) + _flip(y, fy), rows]
                _remote(handed, handed, send_sems, recv_sems, 3 * a + t, sibling).wait_recv()
        for cp in started:
            cp.wait_send()

    return pl.pallas_call(
        body, in_specs=[_ANY] * n, out_specs=[_ANY] * n,
        out_shape=[jax.ShapeDtypeStruct(s.shape, s.dtype) for s in arrs],
        input_output_aliases={a: a for a in range(n)},
        scratch_shapes=[pltpu.SemaphoreType.DMA((3 * n,)), pltpu.SemaphoreType.DMA((3 * n,))],
        name="gathered_to_sibling_" + "".join(str(t) for t in flips),
    )(*arrs)


_HBM = pl.BlockSpec(memory_space=pltpu.HBM)
_SEM = pl.BlockSpec(memory_space=pltpu.SEMAPHORE)
_EFFECT = pltpu.SideEffectType.DATAFLOW_SIDE_EFFECTING


class _SplitExchange:
    def __init__(self, name, srcs, land_shapes, n_copies, copies, riders=()):
        self.name, self.n, self.nl, self.copies = name, len(srcs), len(land_shapes), copies
        n, nb = self.n, len(srcs) + len(land_shapes)
        lands = [lax.empty(s.shape, s.dtype) for s in land_shapes]
        bufs = [pltpu.with_memory_space_constraint(a, pltpu.HBM) for a in (*srcs, *lands, *riders)]
        na = len(bufs)

        def body(*refs):
            send_sems, recv_sems = refs[na], refs[na + 1]
            for cp in copies(refs[:n], refs[n:nb], send_sems, recv_sems):
                cp.start()
            refs[-1][...] = jnp.zeros(refs[-1].shape, F32)

        outs = pl.pallas_call(
            body, name=name + "_start",
            in_specs=[_HBM] * na,
            out_specs=[_SEM, _SEM] + [_HBM] * na + [pl.BlockSpec(memory_space=pltpu.VMEM)],
            out_shape=[pltpu.SemaphoreType.DMA((n_copies,)), pltpu.SemaphoreType.DMA((n_copies,))]
            + [pltpu.HBM(b.shape, b.dtype) for b in bufs] + [jax.ShapeDtypeStruct((8, LANES), F32)],
            input_output_aliases={i: 2 + i for i in range(na)},
            compiler_params=pltpu.CompilerParams(has_side_effects=_EFFECT),
        )(*bufs)
        self.sems, self.bufs, self.riders, self.token = outs[:2], outs[2:2 + nb], outs[2 + nb:2 + na], outs[-1]

    def after(self):
        return self.token[0, 0]

    def wait(self, done, riders=(), bufs=None):
        n, nb, copies = self.n, self.n + self.nl, self.copies
        bufs = [*(self.bufs if bufs is None else bufs),
                *[pltpu.with_memory_space_constraint(a, pltpu.HBM) for a in riders]]
        na = len(bufs)
        done = list(done) if isinstance(done, (list, tuple)) else [done]

        def body(*refs):
            send_sems, recv_sems = refs[na], refs[na + 1]
            for cp in copies(refs[:n], refs[n:nb], send_sems, recv_sems):
                cp.wait_send()
                cp.wait_recv()

        outs = pl.pallas_call(
            body, name=self.name + "_wait",
            in_specs=[_HBM] * na + [_SEM, _SEM] + [_ANY] * len(done),
            out_specs=[_HBM] * na,
            out_shape=[pltpu.HBM(b.shape, b.dtype) for b in bufs],
            input_output_aliases={i: i for i in range(na)},
            compiler_params=pltpu.CompilerParams(has_side_effects=_EFFECT),
        )(*bufs, *self.sems, *done)
        return outs[:n], outs[n:nb], outs[nb:]


def _sibling_copies(srcs, lands, send_sems, recv_sems):
    x, y, c = _place()
    out = []
    for a, (src, land) in enumerate(zip(srcs, lands)):
        h = src.shape[1] // 2
        theirs = pl.ds(pl.multiple_of((1 - c) * h, 8), h)
        out.append(_remote(src.at[:, theirs], land, send_sems, recv_sems, a, (x, y, 1 - c)))
    return out


def _grads_to_sibling(name, grads):
    shapes = [jax.ShapeDtypeStruct((4, g.shape[1] // 2, g.shape[2]), g.dtype) for g in grads]
    return _SplitExchange(name, grads, shapes, len(grads), _sibling_copies)


def _chip_copies(srcs, lands, send_sems, recv_sems):
    x, y, c = _place()
    out = []
    for a, (src, land) in enumerate(zip(srcs, lands)):
        for t, (fx, fy) in enumerate(_CHIP_FLIPS):
            tx, ty = _flip(x, fx), _flip(y, fy)
            out.append(_remote(src.at[2 * tx + ty], land.at[t], send_sems, recv_sems, 3 * a + t, (tx, ty, c)))
    return out


def _grads_to_chips(name, parts):
    shapes = [jax.ShapeDtypeStruct((3, *p.shape[1:]), p.dtype) for p in parts]
    return _SplitExchange(name, parts, shapes, 3 * len(parts), _chip_copies)


def _add_halves(g, r, half):
    _, rows, cols = g.shape
    h = rows // 2
    tm = min(h, 128)
    nt = h // tm

    def body(half_ref, g_ref, r_ref, b_ref):
        b_ref[...] = (g_ref[...] + r_ref[...]).astype(b_ref.dtype)

    spec = pl.BlockSpec((1, tm, cols), lambda j, i, half_ref: (j, i, 0))
    return pl.pallas_call(
        body,
        grid_spec=pltpu.PrefetchScalarGridSpec(
            num_scalar_prefetch=1, grid=(4, nt),
            in_specs=[pl.BlockSpec((1, tm, cols), lambda j, i, half_ref: (j, half_ref[0] * nt + i, 0)), spec],
            out_specs=spec),
        out_shape=jax.ShapeDtypeStruct((4, h, cols), BF16),
        name="add_sibling_grads",
    )(half, g, r)


def _add_chips(g, r, recv, where):
    _, h, cols = r.shape
    tm = min(h, 128)
    nt = h // tm

    def body(where_ref, g_ref, r_ref, recv_ref, out_ref):
        own = g_ref[0] + r_ref[0]
        out_ref[...] = ((own + recv_ref[0].astype(F32)) + recv_ref[1].astype(F32)) + recv_ref[2].astype(F32)

    return pl.pallas_call(
        body,
        grid_spec=pltpu.PrefetchScalarGridSpec(
            num_scalar_prefetch=1, grid=(nt,),
            in_specs=[pl.BlockSpec((1, tm, cols), lambda i, w: (w[0], w[1] * nt + i, 0)),
                      pl.BlockSpec((1, tm, cols), lambda i, w: (w[0], i, 0)),
                      pl.BlockSpec((3, tm, cols), lambda i, w: (0, i, 0))],
            out_specs=pl.BlockSpec((tm, cols), lambda i, w: (w[1] * nt + i, 0))),
        out_shape=jax.ShapeDtypeStruct((2 * h, cols), F32),
        name="add_chip_grads",
    )(where, g, r, recv)


def _share_halves(shards):
    n = len(shards)

    def body(*refs):
        outs = refs[n:2 * n]
        send_sems, recv_sems = refs[2 * n:]
        x, y, c = _place()
        copies = []
        for a in range(n):
            h = outs[a].shape[0] // 2
            mine = outs[a].at[pl.ds(pl.multiple_of(c * h, 8), h)]
            copies.append(_remote(mine, mine, send_sems, recv_sems, a, (x, y, 1 - c)))
        for cp in copies:
            cp.start()
        for a, cp in enumerate(copies):
            cp.wait_send()
            h = outs[a].shape[0] // 2
            theirs = outs[a].at[pl.ds(pl.multiple_of((1 - c) * h, 8), h)]
            _remote(theirs, theirs, send_sems, recv_sems, a, (x, y, 1 - c)).wait_recv()

    return pl.pallas_call(
        body, in_specs=[_ANY] * n, out_specs=[_ANY] * n,
        out_shape=[jax.ShapeDtypeStruct(p.shape, p.dtype) for p in shards],
        input_output_aliases={a: a for a in range(n)},
        scratch_shapes=[pltpu.SemaphoreType.DMA((n,)), pltpu.SemaphoreType.DMA((n,))],
        name="share_reduced_halves",
    )(*shards)


def _exchange_small(rows, reduce):
    cols = rows[0].shape[1]
    n = len(rows)
    assert sum(r.shape[0] for r in rows) <= 8

    def body(*refs):
        ins, out_ref = refs[:n], refs[n]
        vec_ref, gath_ref, send_sems, recv_sems = refs[n + 1:]
        x, y, c = _place()
        me = 4 * x + 2 * y + c
        vec_ref[...] = jnp.zeros(vec_ref.shape, F32)
        at = 0
        for r in ins:
            vec_ref[at:at + r.shape[0], :] = r[...]
            at += r.shape[0]
        copies = []
        for k in range(1, 8):
            peer = (_flip(x, (k >> 2) & 1), _flip(y, (k >> 1) & 1), _flip(c, k & 1))
            copies.append(_remote(vec_ref, gath_ref.at[me], send_sems, recv_sems, k - 1, peer))
        for cp in copies:
            cp.start()
        gath_ref[me] = vec_ref[...]
        for cp in copies:
            cp.wait()
        if reduce:
            tot = gath_ref[0]
            for dev in range(1, 8):
                tot = tot + gath_ref[dev]
            out_ref[...] = tot
            out_ref[7:8, :] = jnp.zeros((1, cols), F32) + jnp.sum(tot[7:8, :])
        else:
            out_ref[...] = gath_ref[...]

    vm = pl.BlockSpec(memory_space=pltpu.VMEM)
    return pl.pallas_call(
        body, in_specs=[vm] * n, out_specs=vm,
        out_shape=jax.ShapeDtypeStruct((8, cols) if reduce else (8, 8, cols), F32),
        scratch_shapes=[pltpu.VMEM((8, cols), F32), pltpu.VMEM((8, 8, cols), F32),
                        pltpu.SemaphoreType.DMA((7,)), pltpu.SemaphoreType.DMA((7,))],
        name="reduce_small" if reduce else "gather_small",
    )(*rows)


def _adamw(w, g, m, v, name):
    rows, cols = w.shape
    tm = 128 if rows % 128 == 0 else rows

    def body(w_ref, g_ref, m_ref, v_ref, d_ref, m2_ref, v2_ref, gout_ref):
        gr = g_ref[...]
        m2 = ADAM_B1 * m_ref[...] + (1.0 - ADAM_B1) * gr
        v2 = ADAM_B2 * v_ref[...] + (1.0 - ADAM_B2) * (gr * gr)
        m_hat = m2 / (1.0 - ADAM_B1 ** ADAM_STEP)
        v_hat = v2 / (1.0 - ADAM_B2 ** ADAM_STEP)
        d_ref[...] = -ADAM_LR * (m_hat / (jnp.sqrt(v_hat) + ADAM_EPS) + ADAM_WD * w_ref[...])
        m2_ref[...] = m2
        v2_ref[...] = v2
        gout_ref[...] = gr

    spec = pl.BlockSpec((tm, cols), lambda i: (i, 0))
    sds = jax.ShapeDtypeStruct((rows, cols), F32)
    return pl.pallas_call(body, grid=(rows // tm,), in_specs=[spec] * 4, out_specs=[spec] * 4,
                          out_shape=[sds] * 4, name=name)(w, g, m, v)


def kernel(x, norm_g, w_in, b_merge, conv_w, w_out_conv, w_out_attn, w_o, final_g, loss_target, m_norm_g, m_w_in, m_b_merge, m_conv_w, m_w_out_conv, m_w_out_attn, m_w_o, m_final_g, v_norm_g, v_w_in, v_b_merge, v_conv_w, v_w_out_conv, v_w_out_attn, v_w_o, v_final_g):
    mx, my, mc = _place()
    chip = (2 * mx + my).astype(jnp.int32)
    seq = x.shape[1]

    chip1 = chip.reshape(1)
    slots = [_place_shard(w[0], chip1) for w in (w_in, w_out_conv, w_out_attn, w_o)]
    taps8 = _exchange_small([conv_w[0]], reduce=False)
    taps = jnp.concatenate([taps8[2 * j, :3, :] for j in range(4)], axis=1)
    gather_near = _SplitExchange("gather_w_in_near", slots[:1], [], 2, _gather_copies_to((0, 1)))
    st = _Step(x[0], loss_target[0], norm_g, chip1, after=gather_near.token)
    w4, _, _ = gather_near.wait([st.ut, taps8])
    gather_far = _SplitExchange("gather_w_in_far", w4, [], 1, _gather_copies_to((2,)))
    st.project_own(w_in[0], after=gather_far.token)
    (w4,) = _forward_to_sibling(gather_far.bufs, flips=(0, 1))
    st.project(w4, (2, 1))
    (w4,), _, out_slots = gather_far.wait([st.proj_own], riders=slots[1:], bufs=[w4])
    gather_out = _SplitExchange("gather_w_out", out_slots, [], 9, _gather_copies_to((0, 1, 2)), riders=[w4])
    (w4,) = _forward_to_sibling(gather_out.riders, flips=(2,))
    st.project(w4, (3,))
    st.mixers(w4, taps)
    out_ws, _, _ = gather_out.wait(st.o)
    woc, woa, wo = [w.reshape(D_MODEL, D_MODEL) for w in _forward_to_sibling(out_ws)]
    st.merge_and_loss(woc, woa, wo, b_merge, final_g.reshape(1, D_MODEL))

    half = mc.astype(jnp.int32).reshape(1)
    where = jnp.stack([chip, mc.astype(jnp.int32)])
    out_grads = [g.reshape(4, -1, D_MODEL) for g in st.out_weight_grads()]
    to_sibling = _grads_to_sibling("out_grads_to_sibling", out_grads)
    st.conv_grads(after=to_sibling.after())
    out_grads, out_from_sibling, _ = to_sibling.wait(st.da4)
    to_chips = _grads_to_chips("out_grads_to_chips",
                               [_add_halves(g, r, half) for g, r in zip(out_grads, out_from_sibling)])
    d_w4 = st.in_weight_grad(after=to_chips.after())
    out_from_chips = to_chips.wait(st.dc3)[1]

    to_sibling = _grads_to_sibling("in_grad_to_sibling", [d_w4])
    gx_lo, dg_lo = st.input_grad(0, after=to_sibling.after())
    (d_w4,), (from_sibling,), _ = to_sibling.wait(gx_lo)
    to_chips = _grads_to_chips("in_grad_to_chips", [_add_halves(d_w4, from_sibling, half)])
    gx_hi, dg_hi = st.input_grad(1, after=to_chips.after())
    grad_x = _to_natural(gx_lo, gx_hi)

    where_late = where + to_chips.after().astype(jnp.int32)
    out_reduced = [_add_chips(g, r, recv, where_late)
                   for g, r, recv in zip(out_grads, out_from_sibling, out_from_chips)]
    g_woc, g_woa, g_wo = _share_halves(out_reduced)
    small = _exchange_small([dg_lo + dg_hi, st.d_bias.reshape(2, D_MODEL), st.d_taps, st.d_final_g,
                             st.loss8.reshape(1, D_MODEL)], reduce=True)
    loss = (0.5 / D_MODEL) * small[7, 0]
    g_taps = lax.dynamic_slice(small[3:6], (0, chip * (D_MODEL // 4)), (3, D_MODEL // 4))
    upd = {
        "norm_g": _adamw(norm_g, small[0:1], m_norm_g, v_norm_g, "adamw_norm_g"),
        "b_merge": _adamw(b_merge, small[1:3].reshape(1, 2 * D_MODEL), m_b_merge, v_b_merge, "adamw_b_merge"),
        "conv_w": _adamw(conv_w[0], g_taps, m_conv_w[0], v_conv_w[0], "adamw_conv_w"),
        "w_out_conv": _adamw(w_out_conv[0], g_woc, m_w_out_conv[0], v_w_out_conv[0], "adamw_w_out_conv"),
        "w_out_attn": _adamw(w_out_attn[0], g_woa, m_w_out_attn[0], v_w_out_attn[0], "adamw_w_out_attn"),
        "w_o": _adamw(w_o[0], g_wo, m_w_o[0], v_w_o[0], "adamw_w_o"),
        "final_g": _adamw(final_g.reshape(1, D_MODEL), small[6:7], m_final_g.reshape(1, D_MODEL),
                          v_final_g.reshape(1, D_MODEL), "adamw_final_g"),
    }
    behind = [grad_x] + [u[0] for u in upd.values()]
    in_reduced = _add_chips(d_w4, from_sibling, to_chips.wait(behind)[1][0], where)
    (g_w_in,) = _share_halves([in_reduced])
    upd["w_in"] = _adamw(w_in[0], g_w_in, m_w_in[0], v_w_in[0], "adamw_w_in")

    names = ["norm_g", "w_in", "b_merge", "conv_w", "w_out_conv", "w_out_attn", "w_o", "final_g"]
    shapes = [norm_g.shape, w_in.shape, b_merge.shape, conv_w.shape, w_out_conv.shape, w_out_attn.shape,
              w_o.shape, final_g.shape]
    outs = [loss, grad_x.reshape(1, seq, D_MODEL)]
    for k in (3, 0, 1, 2):
        outs += [upd[n][k].reshape(s) for n, s in zip(names, shapes)]
    return tuple(outs)
```

```python
import functools

import numpy as np
import jax
import jax.numpy as jnp
from jax import lax
from jax.experimental import pallas as pl
from jax.experimental.pallas import tpu as pltpu

F32 = jnp.float32
BF16 = jnp.bfloat16
MXU_DTYPE = jnp.bfloat16
ACT_DTYPE = jnp.bfloat16

D_MODEL = 1024
N_HEADS = 16
HEAD_DIM = 64
QB = 128
N_RES = 16
LANES = 128
HP = N_HEADS * HEAD_DIM // LANES
IN_COLS = 10 * D_MODEL
SHARD_COLS = IN_COLS // 4
EPS = 1e-6
NEG = -1e30

ADAM_LR, ADAM_B1, ADAM_B2, ADAM_EPS, ADAM_WD, ADAM_STEP = 0.001, 0.9, 0.999, 1e-08, 0.01, 10

PATTERNS = {1: (16, 8), 4: (4, 32), 16: (1, 128)}

_NN = (((1,), (0,)), ((), ()))
_NT = (((1,), (1,)), ((), ()))


def _dot(a, b):
    return lax.dot_general(a.astype(MXU_DTYPE), b.astype(MXU_DTYPE), _NN, preferred_element_type=F32)


def _dot_nt(a, b):
    return lax.dot_general(a.astype(MXU_DTYPE), b.astype(MXU_DTYPE), _NT, preferred_element_type=F32)


def _split3(x):
    hi = x.astype(BF16)
    r1 = x - hi.astype(F32)
    mid = r1.astype(BF16)
    lo = (r1 - mid.astype(F32)).astype(BF16)
    return hi, mid, lo


def _select_cols(x, sel, terms):
    return sum(lax.dot_general(t, sel, _NN, preferred_element_type=F32) for t in _split3(x)[:terms])


def _sigmoid(z):
    return 1.0 / (1.0 + jnp.exp(-z))


def _head_expand_matrix():
    e = np.zeros((LANES, D_MODEL), np.float32)
    for h in range(N_HEADS):
        e[8 * h, HEAD_DIM * h:HEAD_DIM * (h + 1)] = 1.0
    return jnp.asarray(e, BF16)


def _head_sum_matrix():
    e = np.zeros((D_MODEL, LANES), np.float32)
    for h in range(N_HEADS):
        e[HEAD_DIM * h:HEAD_DIM * (h + 1), 8 * h:8 * (h + 1)] = 1.0
    return jnp.asarray(e, BF16)


def _attn_tables(d):
    g_n, rq = PATTERNS[d]
    q_n = g_n * rq
    gq, iq = np.arange(q_n) // rq, np.arange(q_n) % rq

    def tab(kn, base):
        k_n = g_n * kn
        gk, jk = np.arange(k_n) // kn, np.arange(k_n) % kn
        delta = g_n * (base + iq[:, None] - jk[None, :]) + gq[:, None] - gk[None, :]
        valid = (delta >= 0) & (delta <= QB)
        dist = np.where(valid, d * delta, 0).astype(np.float32)
        madd = np.where(valid, 0.0, NEG).astype(np.float32)
        return dist, madd

    d0, m0 = tab(rq if g_n == 1 else 2 * rq, 0)
    d1, m1 = tab(2 * rq, rq)
    return d0, m0, d1, m1


def _alibi_slopes():
    return jnp.exp2(-8.0 * jnp.arange(1, N_HEADS + 1, dtype=F32) / N_HEADS)


def _to_residue_major(x, tgt, after=None):
    s_n, c_n = x.shape
    lr = s_n // N_RES
    extra = [] if after is None else [after]

    def body(x_ref, t_ref, *rest):
        xo_ref, to_ref = rest[-2:]
        for r in range(N_RES):
            xo_ref[r] = x_ref[pl.ds(r, lr, stride=N_RES), :]
            to_ref[r] = t_ref[pl.ds(r, lr, stride=N_RES), :]

    nat = pl.BlockSpec((s_n, LANES), lambda j: (0, j))
    res = pl.BlockSpec((N_RES, lr, LANES), lambda j: (0, 0, j))
    xo, to = pl.pallas_call(
        body, grid=(c_n // LANES,),
        in_specs=[nat, nat] + [pl.BlockSpec((8, LANES), lambda j: (0, 0))] * len(extra),
        out_specs=[res, res],
        out_shape=[jax.ShapeDtypeStruct((N_RES, lr, c_n), F32)] * 2,
        name="perm_in",
    )(x, tgt, *extra)
    return xo.reshape(s_n, c_n), to.reshape(s_n, c_n)


def _to_natural(gx_lo, gx_hi):
    half_rows, c_n = gx_lo.shape
    lr = half_rows // (N_RES // 2)

    def body(lo_ref, hi_ref, o_ref):
        for r in range(N_RES):
            o_ref[pl.ds(r, lr, stride=N_RES), :] = lo_ref[r] if r < N_RES // 2 else hi_ref[r - N_RES // 2]

    half = pl.BlockSpec((N_RES // 2, lr, LANES), lambda j: (0, 0, j))
    return pl.pallas_call(
        body, grid=(c_n // LANES,),
        in_specs=[half, half],
        out_specs=pl.BlockSpec((2 * half_rows, LANES), lambda j: (0, j)),
        out_shape=jax.ShapeDtypeStruct((2 * half_rows, c_n), F32),
        name="perm_out",
    )(gx_lo.reshape(N_RES // 2, lr, c_n), gx_hi.reshape(N_RES // 2, lr, c_n))


def _rms_in(xp, norm_g):
    s_n, c_n = xp.shape
    tm = 512

    def body(x_ref, g_ref, u_ref, ut_ref):
        x = x_ref[...]
        r = lax.rsqrt(jnp.mean(x * x, axis=-1, keepdims=True) + EPS)
        u = x * r * g_ref[...]
        u_ref[...] = u.astype(u_ref.dtype)
        ut_ref[...] = u.T.astype(ut_ref.dtype)

    return pl.pallas_call(
        body, grid=(s_n // tm,),
        in_specs=[pl.BlockSpec((tm, c_n), lambda i: (i, 0)), pl.BlockSpec((1, c_n), lambda i: (0, 0))],
        out_specs=[pl.BlockSpec((tm, c_n), lambda i: (i, 0)), pl.BlockSpec((c_n, tm), lambda i: (0, i))],
        out_shape=[jax.ShapeDtypeStruct((s_n, c_n), ACT_DTYPE), jax.ShapeDtypeStruct((c_n, s_n), ACT_DTYPE)],
        name="rms_in",
    )(xp, norm_g)


def _in_proj(u, chip, w_own=None, w4=None, partial=None, others=()):
    s_n = u.shape[0]
    tn, cm = 512, 512
    per = SHARD_COLS // tn
    own = partial is None

    def body(chip_ref, a_ref, b_ref, *rest):
        o_ref = rest[-1]
        b = b_ref[...]
        for c in range(s_n // cm):
            o_ref[c * cm:(c + 1) * cm, :] = _dot(a_ref[c * cm:(c + 1) * cm, :], b).astype(o_ref.dtype)

    def shard(n, chip_ref):
        if own:
            return chip_ref[0]
        mask = others[-1]
        for i, m in enumerate(others[:-1]):
            mask = jnp.where(n // per == i, m, mask)
        return jnp.bitwise_xor(chip_ref[0], mask)

    w_spec = (pl.BlockSpec((D_MODEL, tn), lambda n, c: (0, n)) if own else
              pl.BlockSpec((None, D_MODEL, tn), lambda n, c: (shard(n, c), 0, n % per)))
    return pl.pallas_call(
        body,
        grid_spec=pltpu.PrefetchScalarGridSpec(
            num_scalar_prefetch=1, grid=(per if own else len(others) * per,),
            in_specs=[pl.BlockSpec((s_n, D_MODEL), lambda n, c: (0, 0)), w_spec] + ([] if own else [_ANY]),
            out_specs=pl.BlockSpec((s_n, tn), lambda n, c: (0, shard(n, c) * per + n % per))),
        out_shape=jax.ShapeDtypeStruct((s_n, IN_COLS), ACT_DTYPE),
        input_output_aliases={} if own else {3: 0},
        name="in_proj_own" if own else "in_proj_" + "_".join(str(m) for m in others),
    )(*([chip, u, w_own] if own else [chip, u, w4, partial]))


def _conv_terms(xc_ref, cg_ref, r, row, lr, cache):
    def a_of(q):
        if q not in cache:
            cache[q] = cg_ref[q].astype(F32) * xc_ref[q].astype(F32)
        return cache[q]

    def shift_down(v):
        return jnp.where(row >= 1, pltpu.roll(v, 1, 0), 0.0)

    a = a_of(r)
    am1 = a_of(r - 1) if r >= 1 else shift_down(a_of(N_RES - 1))
    am2 = a_of(r - 2) if r >= 2 else shift_down(a_of(N_RES - 2 + r))
    return a, am1, am2


def _conv_fwd(proj, conv_w):
    s_n = proj.shape[0]
    lr = s_n // N_RES
    pv = proj.reshape(N_RES, lr, IN_COLS)

    def body(xc_ref, bg_ref, cg_ref, zc_ref, w_ref, hc_ref, hct_ref):
        w = w_ref[...]
        row = lax.broadcasted_iota(jnp.int32, (lr, LANES), 0)
        products = {}
        for r in range(N_RES):
            a, am1, am2 = _conv_terms(xc_ref, cg_ref, r, row, lr, products)
            c = w[0:1] * am2 + w[1:2] * am1 + w[2:3] * a
            z = zc_ref[r].astype(F32)
            hc = z * _sigmoid(z) * bg_ref[r].astype(F32) * c
            hc_ref[r] = hc.astype(hc_ref.dtype)
            hct_ref[:, r * lr:(r + 1) * lr] = hc.T.astype(hct_ref.dtype)

    def col(part):
        return pl.BlockSpec((N_RES, lr, LANES), lambda j: (0, 0, part * 8 + j))

    hc, hct = pl.pallas_call(
        body, grid=(D_MODEL // LANES,),
        in_specs=[col(0), col(1), col(2), col(3), pl.BlockSpec((3, LANES), lambda j: (0, j))],
        out_specs=[pl.BlockSpec((N_RES, lr, LANES), lambda j: (0, 0, j)),
                   pl.BlockSpec((LANES, s_n), lambda j: (j, 0))],
        out_shape=[jax.ShapeDtypeStruct((N_RES, lr, D_MODEL), ACT_DTYPE),
                   jax.ShapeDtypeStruct((D_MODEL, s_n), ACT_DTYPE)],
        name="conv_fwd",
    )(pv, pv, pv, pv, conv_w)
    return hc.reshape(s_n, D_MODEL), hct


RES_PER_STEP = 8
ATTN_BATCH = 8

_BNT = (((2,), (2,)), ((0,), (0,)))
_BNN = (((2,), (1,)), ((0,), (0,)))


def _bdot(a, b, dims):
    return lax.dot_general(a.astype(MXU_DTYPE), b.astype(MXU_DTYPE), dims, preferred_element_type=F32)


def _pattern_view_shape(s_n, c_n, g_n, lead=()):
    lr = s_n // N_RES
    return (*lead, 4, 4, lr, c_n) if g_n == 4 else (*lead, N_RES, lr, c_n)


def _pattern_view(a, g_n, lead=()):
    return a.reshape(_pattern_view_shape(a.shape[-2], a.shape[-1], g_n, lead))


def _pattern_grid(g_n):
    return (N_RES // RES_PER_STEP if g_n == 1 else N_RES // g_n, HP)


def _pattern_spec(g_n, lr, col_of_hp, lead=()):
    z = (0,) * len(lead)
    if g_n == 16:
        return pl.BlockSpec((*lead, 16, lr, LANES), lambda r, hp: (*z, 0, 0, col_of_hp(hp)))
    if g_n == 4:
        return pl.BlockSpec((*lead, 4, None, lr, LANES), lambda r, hp: (*z, 0, r, 0, col_of_hp(hp)))
    return pl.BlockSpec((*lead, RES_PER_STEP, lr, LANES), lambda r, hp: (*z, r, 0, col_of_hp(hp)))


def _aligned(start, m):
    return start if isinstance(start, int) else pl.multiple_of(start, m)


class _Units:
    def __init__(self, g_n, rq):
        self.g_n, self.rq = g_n, rq
        self.per_res, self.paired = g_n == 1, rq == 8

    def plan(self, lr, size):
        if self.per_res:
            return [0], lr // self.rq - 1, lambda j: [pl.multiple_of(j * self.rq, self.rq)]
        step = 16 if self.paired else self.rq
        per = min(size // 2 if self.paired else size, lr // step)
        assert (lr // step) % per == 0
        return ([i * step for i in range(per)], lr // step // per - 1,
                lambda j: [pl.multiple_of((j * per + i) * step, step) for i in range(per)])

    def count(self, qs):
        return RES_PER_STEP if self.per_res else len(qs) * (2 if self.paired else 1)

    def _split(self, tiles, lo, rows):
        return tiles[:, lo:lo + rows].reshape(self.g_n * rows, LANES)

    def load_q(self, ref, qs):
        rq = self.rq
        if self.per_res:
            return ref[:, pl.ds(qs[0], rq), :]
        if self.paired:
            tiles = [ref[:, pl.ds(q, 16), :].astype(F32) for q in qs]
            return jnp.stack([self._split(t, lo, 8) for t in tiles for lo in (0, 8)])
        return jnp.stack([ref[:, pl.ds(q, rq), :].reshape(self.g_n * rq, LANES) for q in qs])

    def _key_rows(self, q, at_start):
        return (0, 2 * self.rq) if at_start else (_aligned(q - self.rq, self.rq), 2 * self.rq)

    def load_k(self, ref, qs, first):
        rq = self.rq
        if self.per_res:
            return ref[:, pl.ds(0, rq), :] if first else ref[:, pl.ds(_aligned(qs[0] - rq, rq), 2 * rq), :]
        if self.paired:
            out = []
            for i, q in enumerate(qs):
                if first and i == 0:
                    t = ref[:, 0:16, :].astype(F32)
                    out += [self._split(t, 0, 16)] * 2
                else:
                    t = ref[:, pl.ds(_aligned(q - 16, 16), 32), :].astype(F32)
                    out += [self._split(t, 8, 16), self._split(t, 16, 16)]
            return jnp.stack(out)
        rows = [self._key_rows(q, first and i == 0) for i, q in enumerate(qs)]
        return jnp.stack([ref[:, pl.ds(k0, n), :].reshape(self.g_n * n, LANES) for k0, n in rows])

    def store_q(self, ref, qs, val, add=False, lead=()):
        if self.per_res:
            pieces = [(qs[0], self.rq, val)]
        elif self.paired:
            pieces = [(q, 16, jnp.concatenate([val[2 * i].reshape(self.g_n, 8, LANES),
                                               val[2 * i + 1].reshape(self.g_n, 8, LANES)], axis=1))
                      for i, q in enumerate(qs)]
        else:
            pieces = [(q, self.rq, val[i].reshape(self.g_n, self.rq, LANES)) for i, q in enumerate(qs)]
        for start, rows, v in pieces:
            idx = (*lead, slice(None), pl.ds(start, rows), slice(None))
            ref[idx] = (ref[idx] + v if add else v).astype(ref.dtype)

    def add_k(self, ref, qs, val, first):
        rq = self.rq
        if self.per_res:
            k0, n = (0, rq) if first else (_aligned(qs[0] - rq, rq), 2 * rq)
            ref[:, pl.ds(k0, n), :] += val
            return
        if self.paired:
            starts = [s for i, q in enumerate(qs)
                      for s in ((0, 0) if first and i == 0 else (_aligned(q - 8, 8), q))]
            rows = [(s, 16) for s in starts]
        else:
            rows = [self._key_rows(q, first and i == 0) for i, q in enumerate(qs)]
        for b, (k0, n) in enumerate(rows):
            ref[:, pl.ds(k0, n), :] += val[b].reshape(self.g_n, n, LANES)


def _batch_bias(un, qs, at_start, first_ref, general_ref):
    if not at_start:
        return general_ref[...][None]
    if un.per_res:
        return first_ref[...][None]
    return jnp.concatenate([first_ref[...][None]] + [general_ref[...][None]] * (un.count(qs) - 1), axis=0)


def _stack_heads(x, low):
    zero = jnp.zeros_like(x)
    return jnp.concatenate([jnp.where(low, x, zero), jnp.where(low, zero, x)], axis=1)


def _attn_fwd(proj, slopes, d):
    g_n, rq = PATTERNS[d]
    un = _Units(g_n, rq)
    s_n = proj.shape[0]
    lr = s_n // N_RES
    nb = lr // rq
    q_n = g_n * rq
    d0, m0, d1, m1 = _attn_tables(d)
    first, n_more, later = un.plan(lr, ATTN_BATCH)

    def body(sl_ref, q_ref, k_ref, v_ref, d0_ref, m0_ref, d1_ref, m1_ref, o_ref, lse_ref, b0_ref, b1_ref):
        hp = pl.program_id(1)

        @pl.when(hp == 0)
        def _():
            lse_ref[...] = jnp.zeros(lse_ref.shape, F32)

        for h in (0, 1):
            slope = sl_ref[2 * hp + h]
            b0_ref[h * q_n:(h + 1) * q_n, :] = m0_ref[...] - slope * d0_ref[...]
            b1_ref[h * q_n:(h + 1) * q_n, :] = m1_ref[...] - slope * d1_ref[...]

        lane = lax.broadcasted_iota(jnp.int32, (1, q_n, LANES), 2)
        low = lane < HEAD_DIM
        grp = lane // 8

        def batch(qs, at_start):
            qq = _stack_heads(un.load_q(q_ref, qs) * 0.125, low)
            s = _bdot(qq, un.load_k(k_ref, qs, at_start), _BNT) + _batch_bias(un, qs, at_start, b0_ref, b1_ref)
            m = jnp.max(s, axis=2, keepdims=True)
            p = jnp.exp(s - m)
            l = jnp.sum(p, axis=2, keepdims=True)
            o = _bdot(p, un.load_k(v_ref, qs, at_start), _BNN) * (1.0 / l)
            lse = m + jnp.log(l)
            un.store_q(o_ref, qs, jnp.where(low, o[:, :q_n], o[:, q_n:]))
            upd = jnp.where(grp == 2 * hp, lse[:, :q_n], 0.0) + jnp.where(grp == 2 * hp + 1, lse[:, q_n:], 0.0)
            un.store_q(lse_ref, qs, upd, add=True)

        batch(first, True)

        def more(j, carry):
            batch(later(j), False)
            return carry

        lax.fori_loop(1, 1 + n_more, more, 0)

    pv = _pattern_view(proj, g_n)
    full = lambda a: pl.BlockSpec(a.shape, lambda r, hp: (0, 0))
    o, lse = pl.pallas_call(
        body, grid=_pattern_grid(g_n),
        in_specs=[pl.BlockSpec(memory_space=pltpu.SMEM),
                  _pattern_spec(g_n, lr, lambda hp: 32 + hp),
                  _pattern_spec(g_n, lr, lambda hp: 40 + hp),
                  _pattern_spec(g_n, lr, lambda hp: 48 + hp),
                  full(d0), full(m0), full(d1), full(m1)],
        out_specs=[_pattern_spec(g_n, lr, lambda hp: hp), _pattern_spec(g_n, lr, lambda hp: 0)],
        out_shape=[jax.ShapeDtypeStruct(_pattern_view_shape(s_n, D_MODEL, g_n), ACT_DTYPE),
                   jax.ShapeDtypeStruct(_pattern_view_shape(s_n, LANES, g_n), F32)],
        scratch_shapes=[pltpu.VMEM((2 * q_n, d0.shape[1]), F32), pltpu.VMEM((2 * q_n, 2 * q_n), F32)],
        name=f"attn_fwd_d{d}",
    )(slopes, pv, pv, pv, d0, m0, d1, m1)
    return o.reshape(s_n, D_MODEL), lse.reshape(s_n, LANES)


def _attn_combine(outs, lses, proj):
    s_n = proj.shape[0]
    tm = 512

    def body(o1_ref, o2_ref, o3_ref, l1_ref, l2_ref, l3_ref, za_ref, e_ref, o_ref, lse_ref, ha_ref, hat_ref):
        ls = [l1_ref[...], l2_ref[...], l3_ref[...]]
        mx = jnp.maximum(jnp.maximum(ls[0], ls[1]), ls[2])
        den = sum(jnp.exp(l - mx) for l in ls)
        lse = mx + jnp.log(den)
        lse_ref[...] = lse
        o = jnp.zeros((tm, D_MODEL), F32)
        for l, oref in zip(ls, (o1_ref, o2_ref, o3_ref)):
            o = o + _select_cols(jnp.exp(l - lse), e_ref[...], terms=2) * oref[...].astype(F32)
        o_ref[...] = o.astype(o_ref.dtype)
        z = za_ref[...].astype(F32)
        ha = z * _sigmoid(z) * o
        ha_ref[...] = ha.astype(ha_ref.dtype)
        hat_ref[...] = ha.T.astype(hat_ref.dtype)

    row = lambda w: pl.BlockSpec((tm, w), lambda i: (i, 0))
    return pl.pallas_call(
        body, grid=(s_n // tm,),
        in_specs=[row(D_MODEL)] * 3 + [row(LANES)] * 3
        + [pl.BlockSpec((tm, D_MODEL), lambda i: (i, 7)), pl.BlockSpec((LANES, D_MODEL), lambda i: (0, 0))],
        out_specs=[row(D_MODEL), row(LANES), row(D_MODEL), pl.BlockSpec((D_MODEL, tm), lambda i: (0, i))],
        out_shape=[jax.ShapeDtypeStruct((s_n, D_MODEL), ACT_DTYPE), jax.ShapeDtypeStruct((s_n, LANES), F32),
                   jax.ShapeDtypeStruct((s_n, D_MODEL), ACT_DTYPE), jax.ShapeDtypeStruct((D_MODEL, s_n), ACT_DTYPE)],
        name="attn_combine",
    )(*outs, *lses, proj, _head_expand_matrix())


CHAIN_ROWS = 256


def _row_chains(tm):
    return [slice(r, r + CHAIN_ROWS) for r in range(0, tm, CHAIN_ROWS)]


def _gates(gc_ref, ga_ref, b_ref, rows):
    b = b_ref[...]
    gc = _sigmoid(gc_ref[rows, :].astype(F32) + b[:, :D_MODEL])
    ga = _sigmoid(ga_ref[rows, :].astype(F32) + b[:, D_MODEL:])
    return gc, ga


def _merge_loss(hc, ha, woc, woa, wo, proj, b_merge, xp, final_g, tgt):
    s_n = xp.shape[0]
    tm = 512

    def body(hc_ref, ha_ref, woc_ref, woa_ref, wo_ref, gc_ref, ga_ref, b_ref, x_ref, gf_ref, t_ref,
             yc_ref, ya_ref, mgt_ref, dhb_ref, dgf_ref, loss_ref):
        i = pl.program_id(0)

        @pl.when(i == 0)
        def _():
            dgf_ref[...] = jnp.zeros(dgf_ref.shape, F32)
            loss_ref[...] = jnp.zeros(loss_ref.shape, F32)

        gf = gf_ref[...]
        for rows in _row_chains(tm):
            yc = _dot(hc_ref[rows, :], woc_ref[...])
            ya = _dot(ha_ref[rows, :], woa_ref[...])
            gc, ga = _gates(gc_ref, ga_ref, b_ref, rows)
            mg = gc * yc + ga * ya
            yc_ref[rows, :] = yc.astype(yc_ref.dtype)
            ya_ref[rows, :] = ya.astype(ya_ref.dtype)
            mgt_ref[:, rows] = mg.T.astype(mgt_ref.dtype)
            h2 = x_ref[rows, :] + _dot(mg, wo_ref[...])
            r2 = lax.rsqrt(jnp.mean(h2 * h2, axis=-1, keepdims=True) + EPS)
            nrm = h2 * r2
            err = nrm * gf - t_ref[rows, :]
            e2 = (err * err).reshape(-1, 8, D_MODEL).sum(axis=0)
            loss_ref[...] += sum(e2[:, c * LANES:(c + 1) * LANES] for c in range(D_MODEL // LANES))
            dy = err * (1.0 / D_MODEL)
            dgf_ref[...] += jnp.sum(dy * nrm, axis=0, keepdims=True)
            dn = dy * gf
            dh2 = r2 * (dn - nrm * jnp.mean(dn * nrm, axis=-1, keepdims=True))
            dhb_ref[rows, :] = dh2.astype(dhb_ref.dtype)

    row = pl.BlockSpec((tm, D_MODEL), lambda i: (i, 0))
    wsp = pl.BlockSpec((D_MODEL, D_MODEL), lambda i: (0, 0))
    vec = lambda w: pl.BlockSpec((1, w), lambda i: (0, 0))
    act = jax.ShapeDtypeStruct((s_n, D_MODEL), ACT_DTYPE)
    return pl.pallas_call(
        body, grid=(s_n // tm,),
        in_specs=[row, row, wsp, wsp, wsp,
                  pl.BlockSpec((tm, D_MODEL), lambda i: (i, 8)), pl.BlockSpec((tm, D_MODEL), lambda i: (i, 9)),
                  vec(2 * D_MODEL), row, vec(D_MODEL), row],
        out_specs=[row, row, pl.BlockSpec((D_MODEL, tm), lambda i: (0, i)), row,
                   vec(D_MODEL), pl.BlockSpec((8, LANES), lambda i: (0, 0))],
        out_shape=[act, act, jax.ShapeDtypeStruct((D_MODEL, s_n), ACT_DTYPE), act,
                   jax.ShapeDtypeStruct((1, D_MODEL), F32), jax.ShapeDtypeStruct((8, LANES), F32)],
        name="merge_loss",
    )(hc, ha, woc, woa, wo, proj, proj, b_merge, xp, final_g, tgt)


def _merge_bwd(dh2b, wo, woc, woa, yc, ya, proj, b_merge, o):
    s_n = dh2b.shape[0]
    tm = 512

    def body(dh_ref, wo_ref, woc_ref, woa_ref, yc_ref, ya_ref, gc_ref, ga_ref, b_ref, o_ref, za_ref, e_ref,
             dyc_ref, dya_ref, dhc_ref, do_ref, dsum_ref, db3_ref, dbias_ref):
        i = pl.program_id(0)

        @pl.when(i == 0)
        def _():
            dbias_ref[...] = jnp.zeros(dbias_ref.shape, F32)

        for rows in _row_chains(tm):
            dmg = _dot_nt(dh_ref[rows, :], wo_ref[...])
            gc, ga = _gates(gc_ref, ga_ref, b_ref, rows)
            dgc = dmg * yc_ref[rows, :].astype(F32) * gc * (1.0 - gc)
            dga = dmg * ya_ref[rows, :].astype(F32) * ga * (1.0 - ga)
            dbias_ref[:, :D_MODEL] += jnp.sum(dgc, axis=0, keepdims=True)
            dbias_ref[:, D_MODEL:] += jnp.sum(dga, axis=0, keepdims=True)
            dyc = dmg * gc
            dya = dmg * ga
            dyc_ref[rows, :] = dyc.astype(dyc_ref.dtype)
            dya_ref[rows, :] = dya.astype(dya_ref.dtype)
            dhc_ref[rows, :] = _dot_nt(dyc, woc_ref[...]).astype(dhc_ref.dtype)
            dha = _dot_nt(dya, woa_ref[...])
            z = za_ref[rows, :].astype(F32)
            sg = _sigmoid(z)
            ov = o_ref[rows, :].astype(F32)
            dout = dha * z * sg
            do_ref[rows, :] = dout.astype(do_ref.dtype)
            dsum_ref[rows, :] = _select_cols(dout * ov, e_ref[...], terms=2)
            db3_ref[0, rows, :] = (dha * ov * sg * (1.0 + z * (1.0 - sg))).astype(db3_ref.dtype)
            db3_ref[1, rows, :] = dgc.astype(db3_ref.dtype)
            db3_ref[2, rows, :] = dga.astype(db3_ref.dtype)

    row = pl.BlockSpec((tm, D_MODEL), lambda i: (i, 0))
    wsp = pl.BlockSpec((D_MODEL, D_MODEL), lambda i: (0, 0))
    act = jax.ShapeDtypeStruct((s_n, D_MODEL), ACT_DTYPE)
    return pl.pallas_call(
        body, grid=(s_n // tm,),
        in_specs=[row, wsp, wsp, wsp, row, row,
                  pl.BlockSpec((tm, D_MODEL), lambda i: (i, 8)), pl.BlockSpec((tm, D_MODEL), lambda i: (i, 9)),
                  pl.BlockSpec((1, 2 * D_MODEL), lambda i: (0, 0)), row,
                  pl.BlockSpec((tm, D_MODEL), lambda i: (i, 7)), pl.BlockSpec((D_MODEL, LANES), lambda i: (0, 0))],
        out_specs=[row, row, row, row, pl.BlockSpec((tm, LANES), lambda i: (i, 0)),
                   pl.BlockSpec((3, tm, D_MODEL), lambda i: (0, i, 0)),
                   pl.BlockSpec((1, 2 * D_MODEL), lambda i: (0, 0))],
        out_shape=[act, act, act, act, jax.ShapeDtypeStruct((s_n, LANES), F32),
                   jax.ShapeDtypeStruct((3, s_n, D_MODEL), ACT_DTYPE),
                   jax.ShapeDtypeStruct((1, 2 * D_MODEL), F32)],
        name="merge_bwd",
    )(dh2b, wo, woc, woa, yc, ya, proj, proj, b_merge, o, proj, _head_sum_matrix())


def _mm_lhs_resident(a, b, tn, name):
    m_n, k_n = a.shape
    n_n = b.shape[1]

    def body(a_ref, b_ref, o_ref):
        o_ref[...] = _dot(a_ref[...], b_ref[...])

    return pl.pallas_call(
        body, grid=(n_n // tn,),
        in_specs=[pl.BlockSpec((m_n, k_n), lambda n: (0, 0)), pl.BlockSpec((k_n, tn), lambda n: (0, n))],
        out_specs=pl.BlockSpec((m_n, tn), lambda n: (0, n)),
        out_shape=jax.ShapeDtypeStruct((m_n, n_n), F32),
        name=name,
    )(a, b)


def _conv_bwd(proj, conv_w, dhc):
    s_n = proj.shape[0]
    lr = s_n // N_RES
    pv = proj.reshape(N_RES, lr, IN_COLS)

    def body(xc_ref, bg_ref, cg_ref, zc_ref, w_ref, dhc_ref, da4_ref, dw_ref, dc_ref):
        w = w_ref[...]
        row = lax.broadcasted_iota(jnp.int32, (lr, LANES), 0)
        dw = [jnp.zeros((1, LANES), F32) for _ in range(3)]
        products = {}
        for r in range(N_RES):
            a, am1, am2 = _conv_terms(xc_ref, cg_ref, r, row, lr, products)
            c = w[0:1] * am2 + w[1:2] * am1 + w[2:3] * a
            z = zc_ref[r].astype(F32)
            sg = _sigmoid(z)
            sz = z * sg
            bg = bg_ref[r].astype(F32)
            dh = dhc_ref[r].astype(F32)
            da4_ref[1, r] = (dh * sz * c).astype(da4_ref.dtype)
            da4_ref[3, r] = (dh * bg * c * sg * (1.0 + z * (1.0 - sg))).astype(da4_ref.dtype)
            dc = dh * sz * bg
            dc_ref[r] = dc
            dw[0] = dw[0] + jnp.sum(dc * am2, axis=0, keepdims=True)
            dw[1] = dw[1] + jnp.sum(dc * am1, axis=0, keepdims=True)
            dw[2] = dw[2] + jnp.sum(dc * a, axis=0, keepdims=True)
        dw_ref[0:1, :] = dw[0]
        dw_ref[1:2, :] = dw[1]
        dw_ref[2:3, :] = dw[2]

        def shift_up(v):
            return jnp.where(row < lr - 1, pltpu.roll(v, lr - 1, 0), 0.0)

        for r in range(N_RES):
            dp1 = dc_ref[r + 1] if r + 1 < N_RES else shift_up(dc_ref[0])
            dp2 = dc_ref[r + 2] if r + 2 < N_RES else shift_up(dc_ref[r + 2 - N_RES])
            da = w[2:3] * dc_ref[r] + w[1:2] * dp1 + w[0:1] * dp2
            da4_ref[0, r] = (da * cg_ref[r].astype(F32)).astype(da4_ref.dtype)
            da4_ref[2, r] = (da * xc_ref[r].astype(F32)).astype(da4_ref.dtype)

    def col(part):
        return pl.BlockSpec((N_RES, lr, LANES), lambda j: (0, 0, part * 8 + j))

    da4, dw = pl.pallas_call(
        body, grid=(D_MODEL // LANES,),
        in_specs=[col(0), col(1), col(2), col(3), pl.BlockSpec((3, LANES), lambda j: (0, j)),
                  pl.BlockSpec((N_RES, lr, LANES), lambda j: (0, 0, j))],
        out_specs=[pl.BlockSpec((4, N_RES, lr, LANES), lambda j: (0, 0, 0, j)),
                   pl.BlockSpec((3, LANES), lambda j: (0, j))],
        out_shape=[jax.ShapeDtypeStruct((4, N_RES, lr, D_MODEL), ACT_DTYPE),
                   jax.ShapeDtypeStruct((3, D_MODEL), F32)],
        scratch_shapes=[pltpu.VMEM((N_RES, lr, LANES), F32)],
        name="conv_bwd",
    )(pv, pv, pv, pv, conv_w, dhc.reshape(N_RES, lr, D_MODEL))
    return da4.reshape(4, s_n, D_MODEL), dw


def _attn_bwd(proj, dout, lse, dsum, slopes, d, prev=None):
    g_n, rq = PATTERNS[d]
    un = _Units(g_n, rq)
    s_n = proj.shape[0]
    lr = s_n // N_RES
    nb = lr // rq
    q_n = g_n * rq
    d0, m0, d1, m1 = (np.ascontiguousarray(t.T) for t in _attn_tables(d))
    first, n_more, later = un.plan(lr, ATTN_BATCH)
    bsz = un.count(first)
    gd = RES_PER_STEP if un.per_res else g_n

    def body(sl_ref, q_ref, k_ref, v_ref, do_ref, lse_ref, ds_ref, d0_ref, m0_ref, d1_ref, m1_ref, *rest):
        prev_ref = rest[0] if prev is not None else None
        out_ref, b0_ref, b1_ref, lt_ref, dt_ref, dk_ref, dv_ref = rest[-7:]
        hp = pl.program_id(1)
        for h in (0, 1):
            slope = sl_ref[2 * hp + h]
            b0_ref[:, h * q_n:(h + 1) * q_n] = m0_ref[...] - slope * d0_ref[...]
            b1_ref[:, h * q_n:(h + 1) * q_n] = m1_ref[...] - slope * d1_ref[...]
        if prev is None:
            dk_ref[...] = jnp.zeros(dk_ref.shape, F32)
            dv_ref[...] = jnp.zeros(dv_ref.shape, F32)
        else:
            out_ref[0] = prev_ref[0]
            dk_ref[...] = prev_ref[1].astype(F32)
            dv_ref[...] = prev_ref[2].astype(F32)
        low = lax.broadcasted_iota(jnp.int32, (1, q_n, LANES), 2) < HEAD_DIM
        row16 = pl.multiple_of(16 * hp, 16)

        def query_rows(stat_ref, t_ref, qs):
            tiles = un.load_q(stat_ref, qs)
            for b in range(bsz):
                t_ref[b] = tiles[b].T
            t16 = t_ref[:, pl.ds(row16, 16), :]
            return jnp.concatenate([t16[:, 0:1, :], t16[:, 8:9, :]], axis=2)

        def batch(qs, at_start):
            qq = _stack_heads(un.load_q(q_ref, qs) * 0.125, low)
            dd = _stack_heads(un.load_q(do_ref, qs), low)
            ks = un.load_k(k_ref, qs, at_start)
            vs = un.load_k(v_ref, qs, at_start)
            lrow = query_rows(lse_ref, lt_ref, qs)
            drow = query_rows(ds_ref, dt_ref, qs)
            pt = jnp.exp(_bdot(ks, qq, _BNT) + _batch_bias(un, qs, at_start, b0_ref, b1_ref) - lrow)
            dst = pt * (_bdot(vs, dd, _BNT) - drow)
            un.add_k(dv_ref, qs, _bdot(pt, dd, _BNN), at_start)
            un.add_k(dk_ref, qs, _bdot(dst, qq, _BNN), at_start)
            dq = _bdot(jnp.swapaxes(dst, 1, 2), ks, _BNN)
            un.store_q(out_ref, qs, jnp.where(low, dq[:, :q_n], dq[:, q_n:]) * 0.125, add=prev is not None,
                       lead=(0,))

        batch(first, True)

        def more(j, carry):
            batch(later(j), False)
            return carry

        lax.fori_loop(1, 1 + n_more, more, 0)
        out_ref[1] = dk_ref[...].astype(out_ref.dtype)
        out_ref[2] = dv_ref[...].astype(out_ref.dtype)

    pv = _pattern_view(proj, g_n)
    full = lambda a: pl.BlockSpec(a.shape, lambda r, hp: (0, 0))
    whole = _pattern_spec(g_n, lr, lambda hp: hp, lead=(3,))
    out = pl.pallas_call(
        body, grid=_pattern_grid(g_n),
        in_specs=[pl.BlockSpec(memory_space=pltpu.SMEM),
                  _pattern_spec(g_n, lr, lambda hp: 32 + hp),
                  _pattern_spec(g_n, lr, lambda hp: 40 + hp),
                  _pattern_spec(g_n, lr, lambda hp: 48 + hp),
                  _pattern_spec(g_n, lr, lambda hp: hp),
                  _pattern_spec(g_n, lr, lambda hp: 0),
                  _pattern_spec(g_n, lr, lambda hp: 0),
                  full(d0), full(m0), full(d1), full(m1)] + ([] if prev is None else [whole]),
        out_specs=whole,
        out_shape=jax.ShapeDtypeStruct(_pattern_view_shape(s_n, D_MODEL, g_n, lead=(3,)), ACT_DTYPE),
        scratch_shapes=[pltpu.VMEM((d0.shape[0], 2 * q_n), F32), pltpu.VMEM((2 * q_n, 2 * q_n), F32),
                        pltpu.VMEM((bsz, LANES, q_n), F32), pltpu.VMEM((bsz, LANES, q_n), F32),
                        pltpu.VMEM((gd, lr, LANES), F32), pltpu.VMEM((gd, lr, LANES), F32)],
        name=f"attn_bwd_d{d}",
    )(slopes, pv, pv, pv, _pattern_view(dout, g_n), _pattern_view(lse, g_n), _pattern_view(dsum, g_n),
      d0, m0, d1, m1, *([] if prev is None else [_pattern_view(prev, g_n, lead=(3,))]))
    return out.reshape(3, s_n, D_MODEL)


def _part_index(step, per, lo, n):
    return jnp.clip(step // per - lo, 0, n - 1)


def _dw_in(ut, da4, dc3, db3):
    s_n = ut.shape[1]
    tn = 512
    per = D_MODEL // tn
    shard_blocks = SHARD_COLS // tn

    def body(a_ref, p0_ref, p1_ref, p2_ref, o_ref):
        part = pl.program_id(0) // per

        @pl.when(part < 4)
        def _():
            o_ref[...] = _dot(a_ref[...], p0_ref[...])

        @pl.when((part >= 4) & (part < 7))
        def _():
            o_ref[...] = _dot(a_ref[...], p1_ref[...])

        @pl.when(part >= 7)
        def _():
            o_ref[...] = _dot(a_ref[...], p2_ref[...])

    def pspec(lo, n):
        def index(j):
            part = j // per
            col = jnp.where(part < lo, 0, jnp.where(part >= lo + n, per - 1, j % per))
            return _part_index(j, per, lo, n), 0, col
        return pl.BlockSpec((None, s_n, tn), index)

    return pl.pallas_call(
        body, grid=(IN_COLS // tn,),
        in_specs=[pl.BlockSpec((D_MODEL, s_n), lambda j: (0, 0), pipeline_mode=pl.Buffered(1)),
                  pspec(0, 4), pspec(4, 3), pspec(7, 3)],
        out_specs=pl.BlockSpec((None, D_MODEL, tn), lambda j: (j // shard_blocks, 0, j % shard_blocks)),
        out_shape=jax.ShapeDtypeStruct((4, D_MODEL, SHARD_COLS), F32),
        name="dw_in",
    )(ut, da4, dc3, db3)


def _input_grad(da4, dc3, db3, w4, xp, norm_g, dh2, row0, rows):
    tm, tk = 256, 512
    per = D_MODEL // tk
    shard_blocks = SHARD_COLS // tk
    m0 = row0 // tm

    def body(p0_ref, p1_ref, p2_ref, w_ref, x_ref, g_ref, dh_ref, gx_ref, dg_ref):
        @pl.when(pl.program_id(0) == 0)
        def _():
            dg_ref[...] = jnp.zeros(dg_ref.shape, F32)

        du = None
        for k in range(IN_COLS // tk):
            part, cols = k // per, pl.ds((k % per) * tk, tk)
            ref, slot = (p0_ref, part) if part < 4 else (p1_ref, part - 4) if part < 7 else (p2_ref, part - 7)
            d = _dot_nt(ref[slot, :, cols], w_ref[k // shard_blocks, :, pl.ds((k % shard_blocks) * tk, tk)])
            du = d if du is None else du + d
        x = x_ref[...]
        r = lax.rsqrt(jnp.mean(x * x, axis=-1, keepdims=True) + EPS)
        nrm = x * r
        dg_ref[...] += jnp.sum(du * nrm, axis=0, keepdims=True)
        dn = du * g_ref[...]
        gx_ref[...] = dh_ref[...].astype(F32) + r * (dn - nrm * jnp.mean(dn * nrm, axis=-1, keepdims=True))

    def pspec(n):
        return pl.BlockSpec((n, tm, D_MODEL), lambda m: (0, m0 + m, 0))

    row_in = pl.BlockSpec((tm, D_MODEL), lambda m: (m0 + m, 0))
    vec = pl.BlockSpec((1, D_MODEL), lambda m: (0, 0))
    return pl.pallas_call(
        body, grid=(rows // tm,),
        in_specs=[pspec(4), pspec(3), pspec(3),
                  pl.BlockSpec(w4.shape, lambda m: (0, 0, 0), pipeline_mode=pl.Buffered(1)),
                  row_in, vec, row_in],
        out_specs=[pl.BlockSpec((tm, D_MODEL), lambda m: (m, 0)), vec],
        out_shape=[jax.ShapeDtypeStruct((rows, D_MODEL), F32), jax.ShapeDtypeStruct((1, D_MODEL), F32)],
        name="input_grad",
    )(da4, dc3, db3, w4, xp, norm_g, dh2)


class _Step:
    def __init__(self, x, tgt, norm_g, chip, after=None):
        self.norm_g, self.chip = norm_g, chip
        self.slopes = _alibi_slopes()
        self.xp, self.tp = _to_residue_major(x, tgt, after)
        self.u, self.ut = _rms_in(self.xp, norm_g)

    def project_own(self, w_own):
        self.proj_own = _in_proj(self.u, self.chip, w_own=w_own)

    def project(self, w4, others):
        self.proj_own = _in_proj(self.u, self.chip, w4=w4, partial=self.proj_own, others=others)

    def mixers(self, w4, taps):
        self.w4, self.taps, self.proj = w4, taps, self.proj_own
        self.hc, self.hct = _conv_fwd(self.proj, taps)
        fwd = [_attn_fwd(self.proj, self.slopes, d) for d in PATTERNS]
        self.o, self.lse, self.ha, self.hat = _attn_combine([f[0] for f in fwd], [f[1] for f in fwd], self.proj)

    def merge_and_loss(self, woc, woa, wo, b_merge, final_g):
        self.woc, self.woa, self.wo, self.b_merge = woc, woa, wo, b_merge
        (self.yc, self.ya, self.mgt, self.dh2b, self.d_final_g, self.loss8) = _merge_loss(
            self.hc, self.ha, woc, woa, wo, self.proj, b_merge, self.xp, final_g, self.tp)

    def out_weight_grads(self):
        (dyc, dya, self.dhc, self.dout, self.dsum, self.db3, self.d_bias) = _merge_bwd(
            self.dh2b, self.wo, self.woc, self.woa, self.yc, self.ya, self.proj, self.b_merge, self.o)
        d_wo = _mm_lhs_resident(self.mgt, self.dh2b, 256, "dw_o")
        d_woc = _mm_lhs_resident(self.hct, dyc, 256, "dw_out_conv")
        d_woa = _mm_lhs_resident(self.hat, dya, 256, "dw_out_attn")
        return d_woc, d_woa, d_wo

    def conv_grads(self, after=0.0):
        self.da4, self.d_taps = _conv_bwd(self.proj, self.taps + after, self.dhc)

    def in_weight_grad(self, after=0.0):
        slopes = self.slopes + after
        self.dc3 = None
        for d in PATTERNS:
            self.dc3 = _attn_bwd(self.proj, self.dout, self.lse, self.dsum, slopes, d, prev=self.dc3)
        return _dw_in(self.ut, self.da4, self.dc3, self.db3)

    def input_grad(self, half, after=0.0):
        rows = self.xp.shape[0] // 2
        return _input_grad(self.da4, self.dc3, self.db3, self.w4, self.xp, self.norm_g + after, self.dh2b,
                           half * rows, rows)


def _local_grads(x, tgt, norm_g, w4, b_merge, conv_w, woc, woa, wo, final_g):
    st = _Step(x, tgt, norm_g, jnp.zeros((1,), jnp.int32))
    st.project_own(w4[0])
    st.project(w4, (2, 1))
    st.project(w4, (3,))
    st.mixers(w4, conv_w)
    st.merge_and_loss(woc, woa, wo, b_merge, final_g)
    d_woc, d_woa, d_wo = st.out_weight_grads()
    st.conv_grads()
    d_w4 = st.in_weight_grad()
    gx_lo, dg_lo = st.input_grad(0)
    gx_hi, dg_hi = st.input_grad(1)
    return (st.loss8, _to_natural(gx_lo, gx_hi), dg_lo + dg_hi, d_w4, st.d_bias, st.d_taps, d_woc, d_woa, d_wo,
            st.d_final_g)


MESH = pl.DeviceIdType.MESH
_CHIP_FLIPS = ((1, 0), (0, 1), (1, 1))
_ANY = pl.BlockSpec(memory_space=pl.ANY)


def _place():
    return lax.axis_index("x"), lax.axis_index("y"), lax.axis_index("c")


def _flip(v, f):
    return 1 - v if f else v


def _remote(src, dst, send_sems, recv_sems, k, device):
    return pltpu.make_async_remote_copy(src_ref=src, dst_ref=dst, send_sem=send_sems.at[k], recv_sem=recv_sems.at[k],
                                        device_id=device, device_id_type=MESH)


def _place_shard(w, chip, dtype):
    rows, cols = w.shape
    tm = min(rows, 128)

    def body(chip_ref, w_ref, o_ref):
        o_ref[0] = w_ref[...].astype(o_ref.dtype)

    return pl.pallas_call(
        body,
        grid_spec=pltpu.PrefetchScalarGridSpec(
            num_scalar_prefetch=1, grid=(rows // tm,),
            in_specs=[pl.BlockSpec((tm, cols), lambda i, chip_ref: (i, 0))],
            out_specs=pl.BlockSpec((1, tm, cols), lambda i, chip_ref: (chip_ref[0], i, 0))),
        out_shape=jax.ShapeDtypeStruct((4, rows, cols), dtype),
        name="place_shard",
    )(chip, w)


def _gather_copies_to(flips, whole=()):
    def copies(arrs, _, send_sems, recv_sems):
        x, y, c = _place()
        out = []
        for a, arr in enumerate(arrs):
            h = arr.shape[1] // 2
            mine = arr.at[2 * x + y] if a in whole else arr.at[2 * x + y, pl.ds(pl.multiple_of(c * h, 8), h)]
            for i, t in enumerate(flips):
                fx, fy = _CHIP_FLIPS[t]
                out.append(_remote(mine, mine, send_sems, recv_sems, len(flips) * a + i,
                                   (_flip(x, fx), _flip(y, fy), c)))
        return out
    return copies


def _forward_to_sibling(arrs, flips=(0, 1, 2)):
    n = len(arrs)

    def body(*refs):
        outs = refs[n:2 * n]
        send_sems, recv_sems = refs[2 * n:]
        x, y, c = _place()
        sibling = (x, y, 1 - c)
        started = []
        for a in range(n):
            h = outs[a].shape[1] // 2
            rows = pl.ds(pl.multiple_of(c * h, 8), h)
            for t in flips:
                fx, fy = _CHIP_FLIPS[t]
                landed = outs[a].at[2 * _flip(x, fx) + _flip(y, fy), rows]
                cp = _remote(landed, landed, send_sems, recv_sems, 3 * a + t, sibling)
                cp.start()
                started.append(cp)
        for a in range(n):
            h = outs[a].shape[1] // 2
            rows = pl.ds(pl.multiple_of((1 - c) * h, 8), h)
            for t in flips:
                fx, fy = _CHIP_FLIPS[t]
                handed = outs[a].at[2 * _flip(x, fx) + _flip(y, fy), rows]
                _remote(handed, handed, send_sems, recv_sems, 3 * a + t, sibling).wait_recv()
        for cp in started:
            cp.wait_send()

    return pl.pallas_call(
        body, in_specs=[_ANY] * n, out_specs=[_ANY] * n,
        out_shape=[jax.ShapeDtypeStruct(s.shape, s.dtype) for s in arrs],
        input_output_aliases={a: a for a in range(n)},
        scratch_shapes=[pltpu.SemaphoreType.DMA((3 * n,)), pltpu.SemaphoreType.DMA((3 * n,))],
        name="gathered_to_sibling_" + "".join(str(t) for t in flips),
    )(*arrs)


_HBM = pl.BlockSpec(memory_space=pltpu.HBM)
_SEM = pl.BlockSpec(memory_space=pltpu.SEMAPHORE)
_EFFECT = pltpu.SideEffectType.DATAFLOW_SIDE_EFFECTING


class _SplitExchange:
    def __init__(self, name, srcs, land_shapes, n_copies, copies, riders=()):
        self.name, self.n, self.nl, self.copies = name, len(srcs), len(land_shapes), copies
        n, nb = self.n, len(srcs) + len(land_shapes)
        lands = [lax.empty(s.shape, s.dtype) for s in land_shapes]
        bufs = [pltpu.with_memory_space_constraint(a, pltpu.HBM) for a in (*srcs, *lands, *riders)]
        na = len(bufs)

        def body(*refs):
            send_sems, recv_sems = refs[na], refs[na + 1]
            for cp in copies(refs[:n], refs[n:nb], send_sems, recv_sems):
                cp.start()
            refs[-1][...] = jnp.zeros(refs[-1].shape, F32)

        outs = pl.pallas_call(
            body, name=name + "_start",
            in_specs=[_HBM] * na,
            out_specs=[_SEM, _SEM] + [_HBM] * na + [pl.BlockSpec(memory_space=pltpu.VMEM)],
            out_shape=[pltpu.SemaphoreType.DMA((n_copies,)), pltpu.SemaphoreType.DMA((n_copies,))]
            + [pltpu.HBM(b.shape, b.dtype) for b in bufs] + [jax.ShapeDtypeStruct((8, LANES), F32)],
            input_output_aliases={i: 2 + i for i in range(na)},
            compiler_params=pltpu.CompilerParams(has_side_effects=_EFFECT),
        )(*bufs)
        self.sems, self.bufs, self.riders, self.token = outs[:2], outs[2:2 + nb], outs[2 + nb:2 + na], outs[-1]

    def after(self):
        return self.token[0, 0]

    def wait(self, done, riders=(), bufs=None):
        n, nb, copies = self.n, self.n + self.nl, self.copies
        bufs = [*(self.bufs if bufs is None else bufs),
                *[pltpu.with_memory_space_constraint(a, pltpu.HBM) for a in riders]]
        na = len(bufs)
        done = list(done) if isinstance(done, (list, tuple)) else [done]

        def body(*refs):
            send_sems, recv_sems = refs[na], refs[na + 1]
            for cp in copies(refs[:n], refs[n:nb], send_sems, recv_sems):
                cp.wait_send()
                cp.wait_recv()

        outs = pl.pallas_call(
            body, name=self.name + "_wait",
            in_specs=[_HBM] * na + [_SEM, _SEM] + [_ANY] * len(done),
            out_specs=[_HBM] * na,
            out_shape=[pltpu.HBM(b.shape, b.dtype) for b in bufs],
            input_output_aliases={i: i for i in range(na)},
            compiler_params=pltpu.CompilerParams(has_side_effects=_EFFECT),
        )(*bufs, *self.sems, *done)
        return outs[:n], outs[n:nb], outs[nb:]


def _sibling_copies(srcs, lands, send_sems, recv_sems):
    x, y, c = _place()
    out = []
    for a, (src, land) in enumerate(zip(srcs, lands)):
        h = src.shape[1] // 2
        theirs = pl.ds(pl.multiple_of((1 - c) * h, 8), h)
        out.append(_remote(src.at[:, theirs], land, send_sems, recv_sems, a, (x, y, 1 - c)))
    return out


def _grads_to_sibling(name, grads):
    shapes = [jax.ShapeDtypeStruct((4, g.shape[1] // 2, g.shape[2]), g.dtype) for g in grads]
    return _SplitExchange(name, grads, shapes, len(grads), _sibling_copies)


def _chip_copies(srcs, lands, send_sems, recv_sems):
    x, y, c = _place()
    out = []
    for a, (src, land) in enumerate(zip(srcs, lands)):
        for t, (fx, fy) in enumerate(_CHIP_FLIPS):
            tx, ty = _flip(x, fx), _flip(y, fy)
            out.append(_remote(src.at[2 * tx + ty], land.at[t], send_sems, recv_sems, 3 * a + t, (tx, ty, c)))
    return out


def _grads_to_chips(name, parts):
    shapes = [jax.ShapeDtypeStruct((3, *p.shape[1:]), p.dtype) for p in parts]
    return _SplitExchange(name, parts, shapes, 3 * len(parts), _chip_copies)


def _add_halves(g, r, half):
    _, rows, cols = g.shape
    h = rows // 2
    tm = min(h, 128)
    nt = h // tm

    def body(half_ref, g_ref, r_ref, b_ref):
        b_ref[...] = (g_ref[...] + r_ref[...]).astype(b_ref.dtype)

    spec = pl.BlockSpec((1, tm, cols), lambda j, i, half_ref: (j, i, 0))
    return pl.pallas_call(
        body,
        grid_spec=pltpu.PrefetchScalarGridSpec(
            num_scalar_prefetch=1, grid=(4, nt),
            in_specs=[pl.BlockSpec((1, tm, cols), lambda j, i, half_ref: (j, half_ref[0] * nt + i, 0)), spec],
            out_specs=spec),
        out_shape=jax.ShapeDtypeStruct((4, h, cols), BF16),
        name="add_sibling_grads",
    )(half, g, r)


def _add_chips(g, r, recv, where):
    _, h, cols = r.shape
    tm = min(h, 128)
    nt = h // tm

    def body(where_ref, g_ref, r_ref, recv_ref, out_ref):
        own = g_ref[0] + r_ref[0]
        out_ref[...] = ((own + recv_ref[0].astype(F32)) + recv_ref[1].astype(F32)) + recv_ref[2].astype(F32)

    return pl.pallas_call(
        body,
        grid_spec=pltpu.PrefetchScalarGridSpec(
            num_scalar_prefetch=1, grid=(nt,),
            in_specs=[pl.BlockSpec((1, tm, cols), lambda i, w: (w[0], w[1] * nt + i, 0)),
                      pl.BlockSpec((1, tm, cols), lambda i, w: (w[0], i, 0)),
                      pl.BlockSpec((3, tm, cols), lambda i, w: (0, i, 0))],
            out_specs=pl.BlockSpec((tm, cols), lambda i, w: (w[1] * nt + i, 0))),
        out_shape=jax.ShapeDtypeStruct((2 * h, cols), F32),
        name="add_chip_grads",
    )(where, g, r, recv)


def _share_halves(shards):
    n = len(shards)

    def body(*refs):
        outs = refs[n:2 * n]
        send_sems, recv_sems = refs[2 * n:]
        x, y, c = _place()
        copies = []
        for a in range(n):
            h = outs[a].shape[0] // 2
            mine = outs[a].at[pl.ds(pl.multiple_of(c * h, 8), h)]
            copies.append(_remote(mine, mine, send_sems, recv_sems, a, (x, y, 1 - c)))
        for cp in copies:
            cp.start()
        for a, cp in enumerate(copies):
            cp.wait_send()
            h = outs[a].shape[0] // 2
            theirs = outs[a].at[pl.ds(pl.multiple_of((1 - c) * h, 8), h)]
            _remote(theirs, theirs, send_sems, recv_sems, a, (x, y, 1 - c)).wait_recv()

    return pl.pallas_call(
        body, in_specs=[_ANY] * n, out_specs=[_ANY] * n,
        out_shape=[jax.ShapeDtypeStruct(p.shape, p.dtype) for p in shards],
        input_output_aliases={a: a for a in range(n)},
        scratch_shapes=[pltpu.SemaphoreType.DMA((n,)), pltpu.SemaphoreType.DMA((n,))],
        name="share_reduced_halves",
    )(*shards)


def _reduce_small(rows):
    cols = rows[0].shape[1]
    n = len(rows)
    assert sum(r.shape[0] for r in rows) <= 8

    def body(*refs):
        ins, out_ref = refs[:n], refs[n]
        vec_ref, gath_ref, send_sems, recv_sems = refs[n + 1:]
        x, y, c = _place()
        me = 4 * x + 2 * y + c
        vec_ref[...] = jnp.zeros(vec_ref.shape, F32)
        at = 0
        for r in ins:
            vec_ref[at:at + r.shape[0], :] = r[...]
            at += r.shape[0]
        copies = []
        for k in range(1, 8):
            peer = (_flip(x, (k >> 2) & 1), _flip(y, (k >> 1) & 1), _flip(c, k & 1))
            copies.append(_remote(vec_ref, gath_ref.at[me], send_sems, recv_sems, k - 1, peer))
        for cp in copies:
            cp.start()
        gath_ref[me] = vec_ref[...]
        for cp in copies:
            cp.wait()
        tot = gath_ref[0]
        for dev in range(1, 8):
            tot = tot + gath_ref[dev]
        out_ref[...] = tot
        out_ref[7:8, :] = jnp.zeros((1, cols), F32) + jnp.sum(tot[7:8, :])

    vm = pl.BlockSpec(memory_space=pltpu.VMEM)
    return pl.pallas_call(
        body, in_specs=[vm] * n, out_specs=vm,
        out_shape=jax.ShapeDtypeStruct((8, cols), F32),
        scratch_shapes=[pltpu.VMEM((8, cols), F32), pltpu.VMEM((8, 8, cols), F32),
                        pltpu.SemaphoreType.DMA((7,)), pltpu.SemaphoreType.DMA((7,))],
        name="reduce_small",
    )(*rows)


def _adamw(w, g, m, v, name):
    rows, cols = w.shape
    tm = 128 if rows % 128 == 0 else rows

    def body(w_ref, g_ref, m_ref, v_ref, d_ref, m2_ref, v2_ref, gout_ref):
        gr = g_ref[...]
        m2 = ADAM_B1 * m_ref[...] + (1.0 - ADAM_B1) * gr
        v2 = ADAM_B2 * v_ref[...] + (1.0 - ADAM_B2) * (gr * gr)
        m_hat = m2 / (1.0 - ADAM_B1 ** ADAM_STEP)
        v_hat = v2 / (1.0 - ADAM_B2 ** ADAM_STEP)
        d_ref[...] = -ADAM_LR * (m_hat / (jnp.sqrt(v_hat) + ADAM_EPS) + ADAM_WD * w_ref[...])
        m2_ref[...] = m2
        v2_ref[...] = v2
        gout_ref[...] = gr

    spec = pl.BlockSpec((tm, cols), lambda i: (i, 0))
    sds = jax.ShapeDtypeStruct((rows, cols), F32)
    return pl.pallas_call(body, grid=(rows // tm,), in_specs=[spec] * 4, out_specs=[spec] * 4,
                          out_shape=[sds] * 4, name=name)(w, g, m, v)


def kernel(x, norm_g, w_in, b_merge, conv_w, w_out_conv, w_out_attn, w_o, final_g, loss_target, m_norm_g, m_w_in, m_b_merge, m_conv_w, m_w_out_conv, m_w_out_attn, m_w_o, m_final_g, v_norm_g, v_w_in, v_b_merge, v_conv_w, v_w_out_conv, v_w_out_attn, v_w_o, v_final_g):
    mx, my, mc = _place()
    chip = (2 * mx + my).astype(jnp.int32)
    seq = x.shape[1]

    chip1 = chip.reshape(1)
    slots = [_place_shard(w[0], chip1, MXU_DTYPE) for w in (w_in, w_out_conv, w_out_attn, w_o)]
    taps_slot = _place_shard(jnp.pad(conv_w[0], ((0, 5), (0, 0))), chip1, F32)
    gather_near = _SplitExchange("gather_w_in_near", [slots[0], taps_slot], [], 4,
                                 _gather_copies_to((0, 1), whole=(1,)))
    st = _Step(x[0], loss_target[0], norm_g, chip1, after=gather_near.token)
    st.project_own(w_in[0])
    near, _, _ = gather_near.wait([st.ut, st.proj_own])
    gather_far = _SplitExchange("gather_w_in_far", near, [], 2, _gather_copies_to((2,), whole=(1,)))
    (w4,) = _forward_to_sibling(gather_far.bufs[:1], flips=(0, 1))
    st.project(w4, (2, 1))
    (w4, taps4), _, out_slots = gather_far.wait([st.proj_own], riders=slots[1:], bufs=[w4, gather_far.bufs[1]])
    gather_out = _SplitExchange("gather_w_out", out_slots, [], 9, _gather_copies_to((0, 1, 2)), riders=[w4])
    (w4,) = _forward_to_sibling(gather_out.riders, flips=(2,))
    st.project(w4, (3,))
    st.mixers(w4, jnp.concatenate([taps4[j, :3, :] for j in range(4)], axis=1))
    out_ws, _, _ = gather_out.wait(st.o)
    woc, woa, wo = [w.reshape(D_MODEL, D_MODEL) for w in _forward_to_sibling(out_ws)]
    st.merge_and_loss(woc, woa, wo, b_merge, final_g.reshape(1, D_MODEL))

    half = mc.astype(jnp.int32).reshape(1)
    where = jnp.stack([chip, mc.astype(jnp.int32)])
    out_grads = [g.reshape(4, -1, D_MODEL) for g in st.out_weight_grads()]
    to_sibling = _grads_to_sibling("out_grads_to_sibling", out_grads)
    st.conv_grads(after=to_sibling.after())
    out_grads, out_from_sibling, _ = to_sibling.wait(st.da4)
    to_chips = _grads_to_chips("out_grads_to_chips",
                               [_add_halves(g, r, half) for g, r in zip(out_grads, out_from_sibling)])
    d_w4 = st.in_weight_grad(after=to_chips.after())
    out_from_chips = to_chips.wait(st.dc3)[1]

    to_sibling = _grads_to_sibling("in_grad_to_sibling", [d_w4])
    gx_lo, dg_lo = st.input_grad(0, after=to_sibling.after())
    (d_w4,), (from_sibling,), _ = to_sibling.wait(gx_lo)
    to_chips = _grads_to_chips("in_grad_to_chips", [_add_halves(d_w4, from_sibling, half)])
    gx_hi, dg_hi = st.input_grad(1, after=to_chips.after())
    grad_x = _to_natural(gx_lo, gx_hi)

    where_late = where + to_chips.after().astype(jnp.int32)
    out_reduced = [_add_chips(g, r, recv, where_late)
                   for g, r, recv in zip(out_grads, out_from_sibling, out_from_chips)]
    g_woc, g_woa, g_wo = _share_halves(out_reduced)
    small = _reduce_small([dg_lo + dg_hi, st.d_bias.reshape(2, D_MODEL), st.d_taps, st.d_final_g,
                           st.loss8.reshape(1, D_MODEL)])
    loss = (0.5 / D_MODEL) * small[7, 0]
    g_taps = lax.dynamic_slice(small[3:6], (0, chip * (D_MODEL // 4)), (3, D_MODEL // 4))
    upd = {
        "norm_g": _adamw(norm_g, small[0:1], m_norm_g, v_norm_g, "adamw_norm_g"),
        "b_merge": _adamw(b_merge, small[1:3].reshape(1, 2 * D_MODEL), m_b_merge, v_b_merge, "adamw_b_merge"),
        "conv_w": _adamw(conv_w[0], g_taps, m_conv_w[0], v_conv_w[0], "adamw_conv_w"),
        "w_out_conv": _adamw(w_out_conv[0], g_woc, m_w_out_conv[0], v_w_out_conv[0], "adamw_w_out_conv"),
        "w_out_attn": _adamw(w_out_attn[0], g_woa, m_w_out_attn[0], v_w_out_attn[0], "adamw_w_out_attn"),
        "w_o": _adamw(w_o[0], g_wo, m_w_o[0], v_w_o[0], "adamw_w_o"),
        "final_g": _adamw(final_g.reshape(1, D_MODEL), small[6:7], m_final_g.reshape(1, D_MODEL),
                          v_final_g.reshape(1, D_MODEL), "adamw_final_g"),
    }
    behind = [grad_x] + [u[0] for u in upd.values()]
    in_reduced = _add_chips(d_w4, from_sibling, to_chips.wait(behind)[1][0], where)
    (g_w_in,) = _share_halves([in_reduced])
    upd["w_in"] = _adamw(w_in[0], g_w_in, m_w_in[0], v_w_in[0], "adamw_w_in")

    names = ["norm_g", "w_in", "b_merge", "conv_w", "w_out_conv", "w_out_attn", "w_o", "final_g"]
    shapes = [norm_g.shape, w_in.shape, b_merge.shape, conv_w.shape, w_out_conv.shape, w_out_attn.shape,
              w_o.shape, final_g.shape]
    outs = [loss, grad_x.reshape(1, seq, D_MODEL)]
    for k in (3, 0, 1, 2):
        outs += [upd[n][k].reshape(s) for n, s in zip(names, shapes)]
    return tuple(outs)
```

```python
import functools

import numpy as np
import jax
import jax.numpy as jnp
from jax import lax
from jax.experimental import pallas as pl
from jax.experimental.pallas import tpu as pltpu

F32 = jnp.float32
BF16 = jnp.bfloat16
MXU_DTYPE = jnp.bfloat16
ACT_DTYPE = jnp.bfloat16

D_MODEL = 1024
N_HEADS = 16
HEAD_DIM = 64
QB = 128
N_RES = 16
LANES = 128
HP = N_HEADS * HEAD_DIM // LANES
IN_COLS = 10 * D_MODEL
SHARD_COLS = IN_COLS // 4
EPS = 1e-6
NEG = -1e30

ADAM_LR, ADAM_B1, ADAM_B2, ADAM_EPS, ADAM_WD, ADAM_STEP = 0.001, 0.9, 0.999, 1e-08, 0.01, 10

PATTERNS = {1: (16, 8), 4: (4, 32), 16: (1, 128)}

_NN = (((1,), (0,)), ((), ()))
_NT = (((1,), (1,)), ((), ()))


def _dot(a, b):
    return lax.dot_general(a.astype(MXU_DTYPE), b.astype(MXU_DTYPE), _NN, preferred_element_type=F32)


def _dot_nt(a, b):
    return lax.dot_general(a.astype(MXU_DTYPE), b.astype(MXU_DTYPE), _NT, preferred_element_type=F32)


def _split3(x):
    hi = x.astype(BF16)
    r1 = x - hi.astype(F32)
    mid = r1.astype(BF16)
    lo = (r1 - mid.astype(F32)).astype(BF16)
    return hi, mid, lo


def _select_cols(x, sel, terms):
    return sum(lax.dot_general(t, sel, _NN, preferred_element_type=F32) for t in _split3(x)[:terms])


def _sigmoid(z):
    return 1.0 / (1.0 + jnp.exp(-z))


def _head_expand_matrix():
    e = np.zeros((LANES, D_MODEL), np.float32)
    for h in range(N_HEADS):
        e[8 * h, HEAD_DIM * h:HEAD_DIM * (h + 1)] = 1.0
    return jnp.asarray(e, BF16)


def _head_sum_matrix():
    e = np.zeros((D_MODEL, LANES), np.float32)
    for h in range(N_HEADS):
        e[HEAD_DIM * h:HEAD_DIM * (h + 1), 8 * h:8 * (h + 1)] = 1.0
    return jnp.asarray(e, BF16)


def _attn_tables(d):
    g_n, rq = PATTERNS[d]
    q_n = g_n * rq
    gq, iq = np.arange(q_n) // rq, np.arange(q_n) % rq

    def tab(kn, base):
        k_n = g_n * kn
        gk, jk = np.arange(k_n) // kn, np.arange(k_n) % kn
        delta = g_n * (base + iq[:, None] - jk[None, :]) + gq[:, None] - gk[None, :]
        valid = (delta >= 0) & (delta <= QB)
        dist = np.where(valid, d * delta, 0).astype(np.float32)
        madd = np.where(valid, 0.0, NEG).astype(np.float32)
        return dist, madd

    d0, m0 = tab(rq if g_n == 1 else 2 * rq, 0)
    d1, m1 = tab(2 * rq, rq)
    return d0, m0, d1, m1


def _alibi_slopes():
    return jnp.exp2(-8.0 * jnp.arange(1, N_HEADS + 1, dtype=F32) / N_HEADS)


def _to_residue_major(x, tgt, after=None):
    s_n, c_n = x.shape
    lr = s_n // N_RES
    extra = [] if after is None else [after]

    def body(x_ref, t_ref, *rest):
        xo_ref, to_ref = rest[-2:]
        for r in range(N_RES):
            xo_ref[r] = x_ref[pl.ds(r, lr, stride=N_RES), :]
            to_ref[r] = t_ref[pl.ds(r, lr, stride=N_RES), :]

    nat = pl.BlockSpec((s_n, LANES), lambda j: (0, j))
    res = pl.BlockSpec((N_RES, lr, LANES), lambda j: (0, 0, j))
    xo, to = pl.pallas_call(
        body, grid=(c_n // LANES,),
        in_specs=[nat, nat] + [pl.BlockSpec((8, LANES), lambda j: (0, 0))] * len(extra),
        out_specs=[res, res],
        out_shape=[jax.ShapeDtypeStruct((N_RES, lr, c_n), F32)] * 2,
        name="perm_in",
    )(x, tgt, *extra)
    return xo.reshape(s_n, c_n), to.reshape(s_n, c_n)


def _to_natural(gx_lo, gx_hi):
    half_rows, c_n = gx_lo.shape
    lr = half_rows // (N_RES // 2)

    def body(lo_ref, hi_ref, o_ref):
        for r in range(N_RES):
            o_ref[pl.ds(r, lr, stride=N_RES), :] = lo_ref[r] if r < N_RES // 2 else hi_ref[r - N_RES // 2]

    half = pl.BlockSpec((N_RES // 2, lr, LANES), lambda j: (0, 0, j))
    return pl.pallas_call(
        body, grid=(c_n // LANES,),
        in_specs=[half, half],
        out_specs=pl.BlockSpec((2 * half_rows, LANES), lambda j: (0, j)),
        out_shape=jax.ShapeDtypeStruct((2 * half_rows, c_n), F32),
        name="perm_out",
    )(gx_lo.reshape(N_RES // 2, lr, c_n), gx_hi.reshape(N_RES // 2, lr, c_n))


def _rms_in(xp, norm_g):
    s_n, c_n = xp.shape
    tm = 512

    def body(x_ref, g_ref, u_ref, ut_ref):
        x = x_ref[...]
        r = lax.rsqrt(jnp.mean(x * x, axis=-1, keepdims=True) + EPS)
        u = x * r * g_ref[...]
        u_ref[...] = u.astype(u_ref.dtype)
        ut_ref[...] = u.T.astype(ut_ref.dtype)

    return pl.pallas_call(
        body, grid=(s_n // tm,),
        in_specs=[pl.BlockSpec((tm, c_n), lambda i: (i, 0)), pl.BlockSpec((1, c_n), lambda i: (0, 0))],
        out_specs=[pl.BlockSpec((tm, c_n), lambda i: (i, 0)), pl.BlockSpec((c_n, tm), lambda i: (0, i))],
        out_shape=[jax.ShapeDtypeStruct((s_n, c_n), ACT_DTYPE), jax.ShapeDtypeStruct((c_n, s_n), ACT_DTYPE)],
        name="rms_in",
    )(xp, norm_g)


def _in_proj(u, chip, w_own=None, w4=None, partial=None, others=()):
    s_n = u.shape[0]
    tn, cm = SHARD_COLS // 2, 512
    per = SHARD_COLS // tn
    own = partial is None

    def body(chip_ref, a_ref, b_ref, *rest):
        o_ref = rest[-1]
        b = b_ref[...]
        for c in range(s_n // cm):
            o_ref[c * cm:(c + 1) * cm, :] = _dot(a_ref[c * cm:(c + 1) * cm, :], b).astype(o_ref.dtype)

    def shard(n, chip_ref):
        if own:
            return chip_ref[0]
        mask = others[-1]
        for i, m in enumerate(others[:-1]):
            mask = jnp.where(n // per == i, m, mask)
        return jnp.bitwise_xor(chip_ref[0], mask)

    w_spec = (pl.BlockSpec((D_MODEL, tn), lambda n, c: (0, n)) if own else
              pl.BlockSpec((None, D_MODEL, tn), lambda n, c: (shard(n, c), 0, n % per)))
    return pl.pallas_call(
        body,
        grid_spec=pltpu.PrefetchScalarGridSpec(
            num_scalar_prefetch=1, grid=(per if own else len(others) * per,),
            in_specs=[pl.BlockSpec((s_n, D_MODEL), lambda n, c: (0, 0), pipeline_mode=pl.Buffered(1)), w_spec]
            + ([] if own else [_ANY]),
            out_specs=pl.BlockSpec((s_n, tn), lambda n, c: (0, shard(n, c) * per + n % per))),
        out_shape=jax.ShapeDtypeStruct((s_n, IN_COLS), ACT_DTYPE),
        input_output_aliases={} if own else {3: 0},
        name="in_proj_own" if own else "in_proj_" + "_".join(str(m) for m in others),
    )(*([chip, u, w_own] if own else [chip, u, w4, partial]))


def _conv_terms(xc_ref, cg_ref, r, row, lr, cache):
    def a_of(q):
        if q not in cache:
            cache[q] = cg_ref[q].astype(F32) * xc_ref[q].astype(F32)
        return cache[q]

    def shift_down(v):
        return jnp.where(row >= 1, pltpu.roll(v, 1, 0), 0.0)

    a = a_of(r)
    am1 = a_of(r - 1) if r >= 1 else shift_down(a_of(N_RES - 1))
    am2 = a_of(r - 2) if r >= 2 else shift_down(a_of(N_RES - 2 + r))
    return a, am1, am2


def _conv_fwd(proj, conv_w):
    s_n = proj.shape[0]
    lr = s_n // N_RES
    pv = proj.reshape(N_RES, lr, IN_COLS)

    def body(xc_ref, bg_ref, cg_ref, zc_ref, w_ref, hc_ref, hct_ref):
        w = w_ref[...]
        row = lax.broadcasted_iota(jnp.int32, (lr, LANES), 0)
        products = {}
        for r in range(N_RES):
            a, am1, am2 = _conv_terms(xc_ref, cg_ref, r, row, lr, products)
            c = w[0:1] * am2 + w[1:2] * am1 + w[2:3] * a
            z = zc_ref[r].astype(F32)
            hc = z * _sigmoid(z) * bg_ref[r].astype(F32) * c
            hc_ref[r] = hc.astype(hc_ref.dtype)
            hct_ref[:, r * lr:(r + 1) * lr] = hc.T.astype(hct_ref.dtype)

    def col(part):
        return pl.BlockSpec((N_RES, lr, LANES), lambda j: (0, 0, part * 8 + j))

    hc, hct = pl.pallas_call(
        body, grid=(D_MODEL // LANES,),
        in_specs=[col(0), col(1), col(2), col(3), pl.BlockSpec((3, LANES), lambda j: (0, j))],
        out_specs=[pl.BlockSpec((N_RES, lr, LANES), lambda j: (0, 0, j)),
                   pl.BlockSpec((LANES, s_n), lambda j: (j, 0))],
        out_shape=[jax.ShapeDtypeStruct((N_RES, lr, D_MODEL), ACT_DTYPE),
                   jax.ShapeDtypeStruct((D_MODEL, s_n), ACT_DTYPE)],
        name="conv_fwd",
    )(pv, pv, pv, pv, conv_w)
    return hc.reshape(s_n, D_MODEL), hct


RES_PER_STEP = 8
ATTN_BATCH = 8

_BNT = (((2,), (2,)), ((0,), (0,)))
_BNN = (((2,), (1,)), ((0,), (0,)))


def _bdot(a, b, dims):
    return lax.dot_general(a.astype(MXU_DTYPE), b.astype(MXU_DTYPE), dims, preferred_element_type=F32)


def _pattern_view_shape(s_n, c_n, g_n, lead=()):
    lr = s_n // N_RES
    return (*lead, 4, 4, lr, c_n) if g_n == 4 else (*lead, N_RES, lr, c_n)


def _pattern_view(a, g_n, lead=()):
    return a.reshape(_pattern_view_shape(a.shape[-2], a.shape[-1], g_n, lead))


def _pattern_grid(g_n):
    return (N_RES // RES_PER_STEP if g_n == 1 else N_RES // g_n, HP)


def _pattern_spec(g_n, lr, col_of_hp, lead=()):
    z = (0,) * len(lead)
    if g_n == 16:
        return pl.BlockSpec((*lead, 16, lr, LANES), lambda r, hp: (*z, 0, 0, col_of_hp(hp)))
    if g_n == 4:
        return pl.BlockSpec((*lead, 4, None, lr, LANES), lambda r, hp: (*z, 0, r, 0, col_of_hp(hp)))
    return pl.BlockSpec((*lead, RES_PER_STEP, lr, LANES), lambda r, hp: (*z, r, 0, col_of_hp(hp)))


def _aligned(start, m):
    return start if isinstance(start, int) else pl.multiple_of(start, m)


class _Units:
    def __init__(self, g_n, rq):
        self.g_n, self.rq = g_n, rq
        self.per_res, self.paired = g_n == 1, rq == 8

    def plan(self, lr, size):
        if self.per_res:
            return [0], lr // self.rq - 1, lambda j: [pl.multiple_of(j * self.rq, self.rq)]
        step = 16 if self.paired else self.rq
        per = min(size // 2 if self.paired else size, lr // step)
        assert (lr // step) % per == 0
        return ([i * step for i in range(per)], lr // step // per - 1,
                lambda j: [pl.multiple_of((j * per + i) * step, step) for i in range(per)])

    def count(self, qs):
        return RES_PER_STEP if self.per_res else len(qs) * (2 if self.paired else 1)

    def _split(self, tiles, lo, rows):
        return tiles[:, lo:lo + rows].reshape(self.g_n * rows, LANES)

    def load_q(self, ref, qs):
        rq = self.rq
        if self.per_res:
            return ref[:, pl.ds(qs[0], rq), :]
        if self.paired:
            tiles = [ref[:, pl.ds(q, 16), :].astype(F32) for q in qs]
            return jnp.stack([self._split(t, lo, 8) for t in tiles for lo in (0, 8)])
        return jnp.stack([ref[:, pl.ds(q, rq), :].reshape(self.g_n * rq, LANES) for q in qs])

    def _key_rows(self, q, at_start):
        return (0, 2 * self.rq) if at_start else (_aligned(q - self.rq, self.rq), 2 * self.rq)

    def load_k(self, ref, qs, first):
        rq = self.rq
        if self.per_res:
            return ref[:, pl.ds(0, rq), :] if first else ref[:, pl.ds(_aligned(qs[0] - rq, rq), 2 * rq), :]
        if self.paired:
            out = []
            for i, q in enumerate(qs):
                if first and i == 0:
                    t = ref[:, 0:16, :].astype(F32)
                    out += [self._split(t, 0, 16)] * 2
                else:
                    t = ref[:, pl.ds(_aligned(q - 16, 16), 32), :].astype(F32)
                    out += [self._split(t, 8, 16), self._split(t, 16, 16)]
            return jnp.stack(out)
        rows = [self._key_rows(q, first and i == 0) for i, q in enumerate(qs)]
        return jnp.stack([ref[:, pl.ds(k0, n), :].reshape(self.g_n * n, LANES) for k0, n in rows])

    def store_q(self, ref, qs, val, add=False, lead=()):
        if self.per_res:
            pieces = [(qs[0], self.rq, val)]
        elif self.paired:
            pieces = [(q, 16, jnp.concatenate([val[2 * i].reshape(self.g_n, 8, LANES),
                                               val[2 * i + 1].reshape(self.g_n, 8, LANES)], axis=1))
                      for i, q in enumerate(qs)]
        else:
            pieces = [(q, self.rq, val[i].reshape(self.g_n, self.rq, LANES)) for i, q in enumerate(qs)]
        for start, rows, v in pieces:
            idx = (*lead, slice(None), pl.ds(start, rows), slice(None))
            ref[idx] = (ref[idx] + v if add else v).astype(ref.dtype)

    def add_k(self, ref, qs, val, first):
        rq = self.rq
        if self.per_res:
            k0, n = (0, rq) if first else (_aligned(qs[0] - rq, rq), 2 * rq)
            ref[:, pl.ds(k0, n), :] += val
            return
        if self.paired:
            starts = [s for i, q in enumerate(qs)
                      for s in ((0, 0) if first and i == 0 else (_aligned(q - 8, 8), q))]
            rows = [(s, 16) for s in starts]
        else:
            rows = [self._key_rows(q, first and i == 0) for i, q in enumerate(qs)]
        for b, (k0, n) in enumerate(rows):
            ref[:, pl.ds(k0, n), :] += val[b].reshape(self.g_n, n, LANES)


def _batch_bias(un, qs, at_start, first_ref, general_ref):
    if not at_start:
        return general_ref[...][None]
    if un.per_res:
        return first_ref[...][None]
    return jnp.concatenate([first_ref[...][None]] + [general_ref[...][None]] * (un.count(qs) - 1), axis=0)


def _stack_heads(x, low):
    zero = jnp.zeros_like(x)
    return jnp.concatenate([jnp.where(low, x, zero), jnp.where(low, zero, x)], axis=1)


def _attn_fwd(proj, slopes, d):
    g_n, rq = PATTERNS[d]
    un = _Units(g_n, rq)
    s_n = proj.shape[0]
    lr = s_n // N_RES
    nb = lr // rq
    q_n = g_n * rq
    d0, m0, d1, m1 = _attn_tables(d)
    first, n_more, later = un.plan(lr, ATTN_BATCH)

    def body(sl_ref, q_ref, k_ref, v_ref, d0_ref, m0_ref, d1_ref, m1_ref, o_ref, lse_ref, b0_ref, b1_ref):
        hp = pl.program_id(1)

        @pl.when(hp == 0)
        def _():
            lse_ref[...] = jnp.zeros(lse_ref.shape, F32)

        for h in (0, 1):
            slope = sl_ref[2 * hp + h]
            b0_ref[h * q_n:(h + 1) * q_n, :] = m0_ref[...] - slope * d0_ref[...]
            b1_ref[h * q_n:(h + 1) * q_n, :] = m1_ref[...] - slope * d1_ref[...]

        lane = lax.broadcasted_iota(jnp.int32, (1, q_n, LANES), 2)
        low = lane < HEAD_DIM
        grp = lane // 8

        def batch(qs, at_start):
            qq = _stack_heads(un.load_q(q_ref, qs) * 0.125, low)
            s = _bdot(qq, un.load_k(k_ref, qs, at_start), _BNT) + _batch_bias(un, qs, at_start, b0_ref, b1_ref)
            m = jnp.max(s, axis=2, keepdims=True)
            p = jnp.exp(s - m)
            l = jnp.sum(p, axis=2, keepdims=True)
            o = _bdot(p, un.load_k(v_ref, qs, at_start), _BNN) * (1.0 / l)
            lse = m + jnp.log(l)
            un.store_q(o_ref, qs, jnp.where(low, o[:, :q_n], o[:, q_n:]))
            upd = jnp.where(grp == 2 * hp, lse[:, :q_n], 0.0) + jnp.where(grp == 2 * hp + 1, lse[:, q_n:], 0.0)
            un.store_q(lse_ref, qs, upd, add=True)

        batch(first, True)

        def more(j, carry):
            batch(later(j), False)
            return carry

        lax.fori_loop(1, 1 + n_more, more, 0)

    pv = _pattern_view(proj, g_n)
    full = lambda a: pl.BlockSpec(a.shape, lambda r, hp: (0, 0))
    o, lse = pl.pallas_call(
        body, grid=_pattern_grid(g_n),
        in_specs=[pl.BlockSpec(memory_space=pltpu.SMEM),
                  _pattern_spec(g_n, lr, lambda hp: 32 + hp),
                  _pattern_spec(g_n, lr, lambda hp: 40 + hp),
                  _pattern_spec(g_n, lr, lambda hp: 48 + hp),
                  full(d0), full(m0), full(d1), full(m1)],
        out_specs=[_pattern_spec(g_n, lr, lambda hp: hp), _pattern_spec(g_n, lr, lambda hp: 0)],
        out_shape=[jax.ShapeDtypeStruct(_pattern_view_shape(s_n, D_MODEL, g_n), ACT_DTYPE),
                   jax.ShapeDtypeStruct(_pattern_view_shape(s_n, LANES, g_n), F32)],
        scratch_shapes=[pltpu.VMEM((2 * q_n, d0.shape[1]), F32), pltpu.VMEM((2 * q_n, 2 * q_n), F32)],
        name=f"attn_fwd_d{d}",
    )(slopes, pv, pv, pv, d0, m0, d1, m1)
    return o.reshape(s_n, D_MODEL), lse.reshape(s_n, LANES)


def _attn_combine(outs, lses, proj):
    s_n = proj.shape[0]
    tm = 512

    def body(o1_ref, o2_ref, o3_ref, l1_ref, l2_ref, l3_ref, za_ref, e_ref, o_ref, lse_ref, ha_ref, hat_ref):
        ls = [l1_ref[...], l2_ref[...], l3_ref[...]]
        mx = jnp.maximum(jnp.maximum(ls[0], ls[1]), ls[2])
        den = sum(jnp.exp(l - mx) for l in ls)
        lse = mx + jnp.log(den)
        lse_ref[...] = lse
        o = jnp.zeros((tm, D_MODEL), F32)
        for l, oref in zip(ls, (o1_ref, o2_ref, o3_ref)):
            o = o + _select_cols(jnp.exp(l - lse), e_ref[...], terms=2) * oref[...].astype(F32)
        o_ref[...] = o.astype(o_ref.dtype)
        z = za_ref[...].astype(F32)
        ha = z * _sigmoid(z) * o
        ha_ref[...] = ha.astype(ha_ref.dtype)
        hat_ref[...] = ha.T.astype(hat_ref.dtype)

    row = lambda w: pl.BlockSpec((tm, w), lambda i: (i, 0))
    return pl.pallas_call(
        body, grid=(s_n // tm,),
        in_specs=[row(D_MODEL)] * 3 + [row(LANES)] * 3
        + [pl.BlockSpec((tm, D_MODEL), lambda i: (i, 7)), pl.BlockSpec((LANES, D_MODEL), lambda i: (0, 0))],
        out_specs=[row(D_MODEL), row(LANES), row(D_MODEL), pl.BlockSpec((D_MODEL, tm), lambda i: (0, i))],
        out_shape=[jax.ShapeDtypeStruct((s_n, D_MODEL), ACT_DTYPE), jax.ShapeDtypeStruct((s_n, LANES), F32),
                   jax.ShapeDtypeStruct((s_n, D_MODEL), ACT_DTYPE), jax.ShapeDtypeStruct((D_MODEL, s_n), ACT_DTYPE)],
        name="attn_combine",
    )(*outs, *lses, proj, _head_expand_matrix())


CHAIN_ROWS = 256


def _row_chains(tm):
    return [slice(r, r + CHAIN_ROWS) for r in range(0, tm, CHAIN_ROWS)]


def _gates(gc_ref, ga_ref, b_ref, rows):
    b = b_ref[...]
    gc = _sigmoid(gc_ref[rows, :].astype(F32) + b[:, :D_MODEL])
    ga = _sigmoid(ga_ref[rows, :].astype(F32) + b[:, D_MODEL:])
    return gc, ga


def _merge_loss(hc, ha, woc, woa, wo, proj, b_merge, xp, final_g, tgt):
    s_n = xp.shape[0]
    tm = 512

    def body(hc_ref, ha_ref, woc_ref, woa_ref, wo_ref, gc_ref, ga_ref, b_ref, x_ref, gf_ref, t_ref,
             yc_ref, ya_ref, mgt_ref, dhb_ref, dgf_ref, loss_ref):
        i = pl.program_id(0)

        @pl.when(i == 0)
        def _():
            dgf_ref[...] = jnp.zeros(dgf_ref.shape, F32)
            loss_ref[...] = jnp.zeros(loss_ref.shape, F32)

        gf = gf_ref[...]
        for rows in _row_chains(tm):
            yc = _dot(hc_ref[rows, :], woc_ref[...])
            ya = _dot(ha_ref[rows, :], woa_ref[...])
            gc, ga = _gates(gc_ref, ga_ref, b_ref, rows)
            mg = gc * yc + ga * ya
            yc_ref[rows, :] = yc.astype(yc_ref.dtype)
            ya_ref[rows, :] = ya.astype(ya_ref.dtype)
            mgt_ref[:, rows] = mg.T.astype(mgt_ref.dtype)
            h2 = x_ref[rows, :] + _dot(mg, wo_ref[...])
            r2 = lax.rsqrt(jnp.mean(h2 * h2, axis=-1, keepdims=True) + EPS)
            nrm = h2 * r2
            err = nrm * gf - t_ref[rows, :]
            e2 = (err * err).reshape(-1, 8, D_MODEL).sum(axis=0)
            loss_ref[...] += sum(e2[:, c * LANES:(c + 1) * LANES] for c in range(D_MODEL // LANES))
            dy = err * (1.0 / D_MODEL)
            dgf_ref[...] += jnp.sum(dy * nrm, axis=0, keepdims=True)
            dn = dy * gf
            dh2 = r2 * (dn - nrm * jnp.mean(dn * nrm, axis=-1, keepdims=True))
            dhb_ref[rows, :] = dh2.astype(dhb_ref.dtype)

    row = pl.BlockSpec((tm, D_MODEL), lambda i: (i, 0))
    wsp = pl.BlockSpec((D_MODEL, D_MODEL), lambda i: (0, 0))
    vec = lambda w: pl.BlockSpec((1, w), lambda i: (0, 0))
    act = jax.ShapeDtypeStruct((s_n, D_MODEL), ACT_DTYPE)
    return pl.pallas_call(
        body, grid=(s_n // tm,),
        in_specs=[row, row, wsp, wsp, wsp,
                  pl.BlockSpec((tm, D_MODEL), lambda i: (i, 8)), pl.BlockSpec((tm, D_MODEL), lambda i: (i, 9)),
                  vec(2 * D_MODEL), row, vec(D_MODEL), row],
        out_specs=[row, row, pl.BlockSpec((D_MODEL, tm), lambda i: (0, i)), row,
                   vec(D_MODEL), pl.BlockSpec((8, LANES), lambda i: (0, 0))],
        out_shape=[act, act, jax.ShapeDtypeStruct((D_MODEL, s_n), ACT_DTYPE), act,
                   jax.ShapeDtypeStruct((1, D_MODEL), F32), jax.ShapeDtypeStruct((8, LANES), F32)],
        name="merge_loss",
    )(hc, ha, woc, woa, wo, proj, proj, b_merge, xp, final_g, tgt)


def _merge_bwd(dh2b, wo, woc, woa, yc, ya, proj, b_merge, o):
    s_n = dh2b.shape[0]
    tm = 512

    def body(dh_ref, wo_ref, woc_ref, woa_ref, yc_ref, ya_ref, gc_ref, ga_ref, b_ref, o_ref, za_ref, e_ref,
             dyc_ref, dya_ref, dhc_ref, do_ref, dsum_ref, db3_ref, dbias_ref):
        i = pl.program_id(0)

        @pl.when(i == 0)
        def _():
            dbias_ref[...] = jnp.zeros(dbias_ref.shape, F32)

        for rows in _row_chains(tm):
            dmg = _dot_nt(dh_ref[rows, :], wo_ref[...])
            gc, ga = _gates(gc_ref, ga_ref, b_ref, rows)
            dgc = dmg * yc_ref[rows, :].astype(F32) * gc * (1.0 - gc)
            dga = dmg * ya_ref[rows, :].astype(F32) * ga * (1.0 - ga)
            dbias_ref[:, :D_MODEL] += jnp.sum(dgc, axis=0, keepdims=True)
            dbias_ref[:, D_MODEL:] += jnp.sum(dga, axis=0, keepdims=True)
            dyc = dmg * gc
            dya = dmg * ga
            dyc_ref[rows, :] = dyc.astype(dyc_ref.dtype)
            dya_ref[rows, :] = dya.astype(dya_ref.dtype)
            dhc_ref[rows, :] = _dot_nt(dyc, woc_ref[...]).astype(dhc_ref.dtype)
            dha = _dot_nt(dya, woa_ref[...])
            z = za_ref[rows, :].astype(F32)
            sg = _sigmoid(z)
            ov = o_ref[rows, :].astype(F32)
            dout = dha * z * sg
            do_ref[rows, :] = dout.astype(do_ref.dtype)
            dsum_ref[rows, :] = _select_cols(dout * ov, e_ref[...], terms=2)
            db3_ref[0, rows, :] = (dha * ov * sg * (1.0 + z * (1.0 - sg))).astype(db3_ref.dtype)
            db3_ref[1, rows, :] = dgc.astype(db3_ref.dtype)
            db3_ref[2, rows, :] = dga.astype(db3_ref.dtype)

    row = pl.BlockSpec((tm, D_MODEL), lambda i: (i, 0))
    wsp = pl.BlockSpec((D_MODEL, D_MODEL), lambda i: (0, 0))
    act = jax.ShapeDtypeStruct((s_n, D_MODEL), ACT_DTYPE)
    return pl.pallas_call(
        body, grid=(s_n // tm,),
        in_specs=[row, wsp, wsp, wsp, row, row,
                  pl.BlockSpec((tm, D_MODEL), lambda i: (i, 8)), pl.BlockSpec((tm, D_MODEL), lambda i: (i, 9)),
                  pl.BlockSpec((1, 2 * D_MODEL), lambda i: (0, 0)), row,
                  pl.BlockSpec((tm, D_MODEL), lambda i: (i, 7)), pl.BlockSpec((D_MODEL, LANES), lambda i: (0, 0))],
        out_specs=[row, row, row, row, pl.BlockSpec((tm, LANES), lambda i: (i, 0)),
                   pl.BlockSpec((3, tm, D_MODEL), lambda i: (0, i, 0)),
                   pl.BlockSpec((1, 2 * D_MODEL), lambda i: (0, 0))],
        out_shape=[act, act, act, act, jax.ShapeDtypeStruct((s_n, LANES), F32),
                   jax.ShapeDtypeStruct((3, s_n, D_MODEL), ACT_DTYPE),
                   jax.ShapeDtypeStruct((1, 2 * D_MODEL), F32)],
        name="merge_bwd",
    )(dh2b, wo, woc, woa, yc, ya, proj, proj, b_merge, o, proj, _head_sum_matrix())


def _mm_lhs_resident(a, b, tn, name):
    m_n, k_n = a.shape
    n_n = b.shape[1]

    def body(a_ref, b_ref, o_ref):
        o_ref[...] = _dot(a_ref[...], b_ref[...])

    return pl.pallas_call(
        body, grid=(n_n // tn,),
        in_specs=[pl.BlockSpec((m_n, k_n), lambda n: (0, 0)), pl.BlockSpec((k_n, tn), lambda n: (0, n))],
        out_specs=pl.BlockSpec((m_n, tn), lambda n: (0, n)),
        out_shape=jax.ShapeDtypeStruct((m_n, n_n), F32),
        name=name,
    )(a, b)


def _conv_bwd(proj, conv_w, dhc):
    s_n = proj.shape[0]
    lr = s_n // N_RES
    pv = proj.reshape(N_RES, lr, IN_COLS)

    def body(xc_ref, bg_ref, cg_ref, zc_ref, w_ref, dhc_ref, da4_ref, dw_ref, dc_ref):
        w = w_ref[...]
        row = lax.broadcasted_iota(jnp.int32, (lr, LANES), 0)
        dw = [jnp.zeros((1, LANES), F32) for _ in range(3)]
        products = {}
        for r in range(N_RES):
            a, am1, am2 = _conv_terms(xc_ref, cg_ref, r, row, lr, products)
            c = w[0:1] * am2 + w[1:2] * am1 + w[2:3] * a
            z = zc_ref[r].astype(F32)
            sg = _sigmoid(z)
            sz = z * sg
            bg = bg_ref[r].astype(F32)
            dh = dhc_ref[r].astype(F32)
            da4_ref[1, r] = (dh * sz * c).astype(da4_ref.dtype)
            da4_ref[3, r] = (dh * bg * c * sg * (1.0 + z * (1.0 - sg))).astype(da4_ref.dtype)
            dc = dh * sz * bg
            dc_ref[r] = dc
            dw[0] = dw[0] + jnp.sum(dc * am2, axis=0, keepdims=True)
            dw[1] = dw[1] + jnp.sum(dc * am1, axis=0, keepdims=True)
            dw[2] = dw[2] + jnp.sum(dc * a, axis=0, keepdims=True)
        dw_ref[0:1, :] = dw[0]
        dw_ref[1:2, :] = dw[1]
        dw_ref[2:3, :] = dw[2]

        def shift_up(v):
            return jnp.where(row < lr - 1, pltpu.roll(v, lr - 1, 0), 0.0)

        for r in range(N_RES):
            dp1 = dc_ref[r + 1] if r + 1 < N_RES else shift_up(dc_ref[0])
            dp2 = dc_ref[r + 2] if r + 2 < N_RES else shift_up(dc_ref[r + 2 - N_RES])
            da = w[2:3] * dc_ref[r] + w[1:2] * dp1 + w[0:1] * dp2
            da4_ref[0, r] = (da * cg_ref[r].astype(F32)).astype(da4_ref.dtype)
            da4_ref[2, r] = (da * xc_ref[r].astype(F32)).astype(da4_ref.dtype)

    def col(part):
        return pl.BlockSpec((N_RES, lr, LANES), lambda j: (0, 0, part * 8 + j))

    da4, dw = pl.pallas_call(
        body, grid=(D_MODEL // LANES,),
        in_specs=[col(0), col(1), col(2), col(3), pl.BlockSpec((3, LANES), lambda j: (0, j)),
                  pl.BlockSpec((N_RES, lr, LANES), lambda j: (0, 0, j))],
        out_specs=[pl.BlockSpec((4, N_RES, lr, LANES), lambda j: (0, 0, 0, j)),
                   pl.BlockSpec((3, LANES), lambda j: (0, j))],
        out_shape=[jax.ShapeDtypeStruct((4, N_RES, lr, D_MODEL), ACT_DTYPE),
                   jax.ShapeDtypeStruct((3, D_MODEL), F32)],
        scratch_shapes=[pltpu.VMEM((N_RES, lr, LANES), F32)],
        name="conv_bwd",
    )(pv, pv, pv, pv, conv_w, dhc.reshape(N_RES, lr, D_MODEL))
    return da4.reshape(4, s_n, D_MODEL), dw


def _attn_bwd(proj, dout, lse, dsum, slopes, d, prev=None):
    g_n, rq = PATTERNS[d]
    un = _Units(g_n, rq)
    s_n = proj.shape[0]
    lr = s_n // N_RES
    nb = lr // rq
    q_n = g_n * rq
    d0, m0, d1, m1 = (np.ascontiguousarray(t.T) for t in _attn_tables(d))
    first, n_more, later = un.plan(lr, ATTN_BATCH)
    bsz = un.count(first)
    gd = RES_PER_STEP if un.per_res else g_n

    def body(sl_ref, q_ref, k_ref, v_ref, do_ref, lse_ref, ds_ref, d0_ref, m0_ref, d1_ref, m1_ref, *rest):
        prev_ref = rest[0] if prev is not None else None
        out_ref, b0_ref, b1_ref, lt_ref, dt_ref, dk_ref, dv_ref = rest[-7:]
        hp = pl.program_id(1)
        for h in (0, 1):
            slope = sl_ref[2 * hp + h]
            b0_ref[:, h * q_n:(h + 1) * q_n] = m0_ref[...] - slope * d0_ref[...]
            b1_ref[:, h * q_n:(h + 1) * q_n] = m1_ref[...] - slope * d1_ref[...]
        if prev is None:
            dk_ref[...] = jnp.zeros(dk_ref.shape, F32)
            dv_ref[...] = jnp.zeros(dv_ref.shape, F32)
        else:
            out_ref[0] = prev_ref[0]
            dk_ref[...] = prev_ref[1].astype(F32)
            dv_ref[...] = prev_ref[2].astype(F32)
        low = lax.broadcasted_iota(jnp.int32, (1, q_n, LANES), 2) < HEAD_DIM
        row16 = pl.multiple_of(16 * hp, 16)

        def query_rows(stat_ref, t_ref, qs):
            tiles = un.load_q(stat_ref, qs)
            for b in range(bsz):
                t_ref[b] = tiles[b].T
            t16 = t_ref[:, pl.ds(row16, 16), :]
            return jnp.concatenate([t16[:, 0:1, :], t16[:, 8:9, :]], axis=2)

        def batch(qs, at_start):
            qq = _stack_heads(un.load_q(q_ref, qs) * 0.125, low)
            dd = _stack_heads(un.load_q(do_ref, qs), low)
            ks = un.load_k(k_ref, qs, at_start)
            vs = un.load_k(v_ref, qs, at_start)
            lrow = query_rows(lse_ref, lt_ref, qs)
            drow = query_rows(ds_ref, dt_ref, qs)
            pt = jnp.exp(_bdot(ks, qq, _BNT) + _batch_bias(un, qs, at_start, b0_ref, b1_ref) - lrow)
            dst = pt * (_bdot(vs, dd, _BNT) - drow)
            un.add_k(dv_ref, qs, _bdot(pt, dd, _BNN), at_start)
            un.add_k(dk_ref, qs, _bdot(dst, qq, _BNN), at_start)
            dq = _bdot(jnp.swapaxes(dst, 1, 2), ks, _BNN)
            un.store_q(out_ref, qs, jnp.where(low, dq[:, :q_n], dq[:, q_n:]) * 0.125, add=prev is not None,
                       lead=(0,))

        batch(first, True)

        def more(j, carry):
            batch(later(j), False)
            return carry

        lax.fori_loop(1, 1 + n_more, more, 0)
        out_ref[1] = dk_ref[...].astype(out_ref.dtype)
        out_ref[2] = dv_ref[...].astype(out_ref.dtype)

    pv = _pattern_view(proj, g_n)
    full = lambda a: pl.BlockSpec(a.shape, lambda r, hp: (0, 0))
    whole = _pattern_spec(g_n, lr, lambda hp: hp, lead=(3,))
    out = pl.pallas_call(
        body, grid=_pattern_grid(g_n),
        in_specs=[pl.BlockSpec(memory_space=pltpu.SMEM),
                  _pattern_spec(g_n, lr, lambda hp: 32 + hp),
                  _pattern_spec(g_n, lr, lambda hp: 40 + hp),
                  _pattern_spec(g_n, lr, lambda hp: 48 + hp),
                  _pattern_spec(g_n, lr, lambda hp: hp),
                  _pattern_spec(g_n, lr, lambda hp: 0),
                  _pattern_spec(g_n, lr, lambda hp: 0),
                  full(d0), full(m0), full(d1), full(m1)] + ([] if prev is None else [whole]),
        out_specs=whole,
        out_shape=jax.ShapeDtypeStruct(_pattern_view_shape(s_n, D_MODEL, g_n, lead=(3,)), ACT_DTYPE),
        scratch_shapes=[pltpu.VMEM((d0.shape[0], 2 * q_n), F32), pltpu.VMEM((2 * q_n, 2 * q_n), F32),
                        pltpu.VMEM((bsz, LANES, q_n), F32), pltpu.VMEM((bsz, LANES, q_n), F32),
                        pltpu.VMEM((gd, lr, LANES), F32), pltpu.VMEM((gd, lr, LANES), F32)],
        name=f"attn_bwd_d{d}",
    )(slopes, pv, pv, pv, _pattern_view(dout, g_n), _pattern_view(lse, g_n), _pattern_view(dsum, g_n),
      d0, m0, d1, m1, *([] if prev is None else [_pattern_view(prev, g_n, lead=(3,))]))
    return out.reshape(3, s_n, D_MODEL)


def _part_index(step, per, lo, n):
    return jnp.clip(step // per - lo, 0, n - 1)


def _dw_in(ut, da4, dc3, db3):
    s_n = ut.shape[1]
    tn = 512
    per = D_MODEL // tn
    shard_blocks = SHARD_COLS // tn

    def body(a_ref, p0_ref, p1_ref, p2_ref, o_ref):
        part = pl.program_id(0) // per

        @pl.when(part < 4)
        def _():
            o_ref[...] = _dot(a_ref[...], p0_ref[...])

        @pl.when((part >= 4) & (part < 7))
        def _():
            o_ref[...] = _dot(a_ref[...], p1_ref[...])

        @pl.when(part >= 7)
        def _():
            o_ref[...] = _dot(a_ref[...], p2_ref[...])

    def pspec(lo, n):
        def index(j):
            part = j // per
            col = jnp.where(part < lo, 0, jnp.where(part >= lo + n, per - 1, j % per))
            return _part_index(j, per, lo, n), 0, col
        return pl.BlockSpec((None, s_n, tn), index)

    return pl.pallas_call(
        body, grid=(IN_COLS // tn,),
        in_specs=[pl.BlockSpec((D_MODEL, s_n), lambda j: (0, 0), pipeline_mode=pl.Buffered(1)),
                  pspec(0, 4), pspec(4, 3), pspec(7, 3)],
        out_specs=pl.BlockSpec((None, D_MODEL, tn), lambda j: (j // shard_blocks, 0, j % shard_blocks)),
        out_shape=jax.ShapeDtypeStruct((4, D_MODEL, SHARD_COLS), F32),
        name="dw_in",
    )(ut, da4, dc3, db3)


def _input_grad(da4, dc3, db3, w4, xp, norm_g, dh2, row0, rows):
    tm, tk = 256, 512
    per = D_MODEL // tk
    shard_blocks = SHARD_COLS // tk
    m0 = row0 // tm

    def body(p0_ref, p1_ref, p2_ref, w_ref, x_ref, g_ref, dh_ref, gx_ref, dg_ref):
        @pl.when(pl.program_id(0) == 0)
        def _():
            dg_ref[...] = jnp.zeros(dg_ref.shape, F32)

        du = None
        for k in range(IN_COLS // tk):
            part, cols = k // per, pl.ds((k % per) * tk, tk)
            ref, slot = (p0_ref, part) if part < 4 else (p1_ref, part - 4) if part < 7 else (p2_ref, part - 7)
            d = _dot_nt(ref[slot, :, cols], w_ref[k // shard_blocks, :, pl.ds((k % shard_blocks) * tk, tk)])
            du = d if du is None else du + d
        x = x_ref[...]
        r = lax.rsqrt(jnp.mean(x * x, axis=-1, keepdims=True) + EPS)
        nrm = x * r
        dg_ref[...] += jnp.sum(du * nrm, axis=0, keepdims=True)
        dn = du * g_ref[...]
        gx_ref[...] = dh_ref[...].astype(F32) + r * (dn - nrm * jnp.mean(dn * nrm, axis=-1, keepdims=True))

    def pspec(n):
        return pl.BlockSpec((n, tm, D_MODEL), lambda m: (0, m0 + m, 0))

    row_in = pl.BlockSpec((tm, D_MODEL), lambda m: (m0 + m, 0))
    vec = pl.BlockSpec((1, D_MODEL), lambda m: (0, 0))
    return pl.pallas_call(
        body, grid=(rows // tm,),
        in_specs=[pspec(4), pspec(3), pspec(3),
                  pl.BlockSpec(w4.shape, lambda m: (0, 0, 0), pipeline_mode=pl.Buffered(1)),
                  row_in, vec, row_in],
        out_specs=[pl.BlockSpec((tm, D_MODEL), lambda m: (m, 0)), vec],
        out_shape=[jax.ShapeDtypeStruct((rows, D_MODEL), F32), jax.ShapeDtypeStruct((1, D_MODEL), F32)],
        name="input_grad",
    )(da4, dc3, db3, w4, xp, norm_g, dh2)


class _Step:
    def __init__(self, x, tgt, norm_g, chip, after=None):
        self.norm_g, self.chip = norm_g, chip
        self.slopes = _alibi_slopes()
        self.xp, self.tp = _to_residue_major(x, tgt, after)
        self.u, self.ut = _rms_in(self.xp, norm_g)

    def project_own(self, w_own):
        self.proj_own = _in_proj(self.u, self.chip, w_own=w_own)

    def project(self, w4, others):
        self.proj_own = _in_proj(self.u, self.chip, w4=w4, partial=self.proj_own, others=others)

    def mixers(self, w4, taps):
        self.w4, self.taps, self.proj = w4, taps, self.proj_own
        self.hc, self.hct = _conv_fwd(self.proj, taps)
        fwd = [_attn_fwd(self.proj, self.slopes, d) for d in PATTERNS]
        self.o, self.lse, self.ha, self.hat = _attn_combine([f[0] for f in fwd], [f[1] for f in fwd], self.proj)

    def merge_and_loss(self, woc, woa, wo, b_merge, final_g):
        self.woc, self.woa, self.wo, self.b_merge = woc, woa, wo, b_merge
        (self.yc, self.ya, self.mgt, self.dh2b, self.d_final_g, self.loss8) = _merge_loss(
            self.hc, self.ha, woc, woa, wo, self.proj, b_merge, self.xp, final_g, self.tp)

    def out_weight_grads(self):
        (dyc, dya, self.dhc, self.dout, self.dsum, self.db3, self.d_bias) = _merge_bwd(
            self.dh2b, self.wo, self.woc, self.woa, self.yc, self.ya, self.proj, self.b_merge, self.o)
        d_wo = _mm_lhs_resident(self.mgt, self.dh2b, 256, "dw_o")
        d_woc = _mm_lhs_resident(self.hct, dyc, 256, "dw_out_conv")
        d_woa = _mm_lhs_resident(self.hat, dya, 256, "dw_out_attn")
        return d_woc, d_woa, d_wo

    def conv_grads(self, after=0.0):
        self.da4, self.d_taps = _conv_bwd(self.proj, self.taps + after, self.dhc)

    def in_weight_grad(self, after=0.0):
        slopes = self.slopes + after
        self.dc3 = None
        for d in PATTERNS:
            self.dc3 = _attn_bwd(self.proj, self.dout, self.lse, self.dsum, slopes, d, prev=self.dc3)
        return _dw_in(self.ut, self.da4, self.dc3, self.db3)

    def input_grad(self, half, after=0.0):
        rows = self.xp.shape[0] // 2
        return _input_grad(self.da4, self.dc3, self.db3, self.w4, self.xp, self.norm_g + after, self.dh2b,
                           half * rows, rows)


def _local_grads(x, tgt, norm_g, w4, b_merge, conv_w, woc, woa, wo, final_g):
    st = _Step(x, tgt, norm_g, jnp.zeros((1,), jnp.int32))
    st.project_own(w4[0])
    st.project(w4, (2, 1))
    st.project(w4, (3,))
    st.mixers(w4, conv_w)
    st.merge_and_loss(woc, woa, wo, b_merge, final_g)
    d_woc, d_woa, d_wo = st.out_weight_grads()
    st.conv_grads()
    d_w4 = st.in_weight_grad()
    gx_lo, dg_lo = st.input_grad(0)
    gx_hi, dg_hi = st.input_grad(1)
    return (st.loss8, _to_natural(gx_lo, gx_hi), dg_lo + dg_hi, d_w4, st.d_bias, st.d_taps, d_woc, d_woa, d_wo,
            st.d_final_g)


MESH = pl.DeviceIdType.MESH
_CHIP_FLIPS = ((1, 0), (0, 1), (1, 1))
_ANY = pl.BlockSpec(memory_space=pl.ANY)


def _place():
    return lax.axis_index("x"), lax.axis_index("y"), lax.axis_index("c")


def _flip(v, f):
    return 1 - v if f else v


def _remote(src, dst, send_sems, recv_sems, k, device):
    return pltpu.make_async_remote_copy(src_ref=src, dst_ref=dst, send_sem=send_sems.at[k], recv_sem=recv_sems.at[k],
                                        device_id=device, device_id_type=MESH)


def _place_shard(w, chip, dtype):
    rows, cols = w.shape
    tm = min(rows, 128)

    def body(chip_ref, w_ref, o_ref):
        o_ref[0] = w_ref[...].astype(o_ref.dtype)

    return pl.pallas_call(
        body,
        grid_spec=pltpu.PrefetchScalarGridSpec(
            num_scalar_prefetch=1, grid=(rows // tm,),
            in_specs=[pl.BlockSpec((tm, cols), lambda i, chip_ref: (i, 0))],
            out_specs=pl.BlockSpec((1, tm, cols), lambda i, chip_ref: (chip_ref[0], i, 0))),
        out_shape=jax.ShapeDtypeStruct((4, rows, cols), dtype),
        name="place_shard",
    )(chip, w)


def _gather_copies_to(flips, whole=()):
    def copies(arrs, _, send_sems, recv_sems):
        x, y, c = _place()
        out = []
        for a, arr in enumerate(arrs):
            h = arr.shape[1] // 2
            mine = arr.at[2 * x + y] if a in whole else arr.at[2 * x + y, pl.ds(pl.multiple_of(c * h, 8), h)]
            for i, t in enumerate(flips):
                fx, fy = _CHIP_FLIPS[t]
                out.append(_remote(mine, mine, send_sems, recv_sems, len(flips) * a + i,
                                   (_flip(x, fx), _flip(y, fy), c)))
        return out
    return copies


def _forward_to_sibling(arrs, flips=(0, 1, 2)):
    n = len(arrs)

    def body(*refs):
        outs = refs[n:2 * n]
        send_sems, recv_sems = refs[2 * n:]
        x, y, c = _place()
        sibling = (x, y, 1 - c)
        started = []
        for a in range(n):
            h = outs[a].shape[1] // 2
            rows = pl.ds(pl.multiple_of(c * h, 8), h)
            for t in flips:
                fx, fy = _CHIP_FLIPS[t]
                landed = outs[a].at[2 * _flip(x, fx) + _flip(y, fy), rows]
                cp = _remote(landed, landed, send_sems, recv_sems, 3 * a + t, sibling)
                cp.start()
                started.append(cp)
        for a in range(n):
            h = outs[a].shape[1] // 2
            rows = pl.ds(pl.multiple_of((1 - c) * h, 8), h)
            for t in flips:
                fx, fy = _CHIP_FLIPS[t]
                handed = outs[a].at[2 * _flip(x, fx) + _flip(y, fy), rows]
                _remote(handed, handed, send_sems, recv_sems, 3 * a + t, sibling).wait_recv()
        for cp in started:
            cp.wait_send()

    return pl.pallas_call(
        body, in_specs=[_ANY] * n, out_specs=[_ANY] * n,
        out_shape=[jax.ShapeDtypeStruct(s.shape, s.dtype) for s in arrs],
        input_output_aliases={a: a for a in range(n)},
        scratch_shapes=[pltpu.SemaphoreType.DMA((3 * n,)), pltpu.SemaphoreType.DMA((3 * n,))],
        name="gathered_to_sibling_" + "".join(str(t) for t in flips),
    )(*arrs)


_HBM = pl.BlockSpec(memory_space=pltpu.HBM)
_SEM = pl.BlockSpec(memory_space=pltpu.SEMAPHORE)
_EFFECT = pltpu.SideEffectType.DATAFLOW_SIDE_EFFECTING


class _SplitExchange:
    def __init__(self, name, srcs, land_shapes, n_copies, copies, riders=()):
        self.name, self.n, self.nl, self.copies = name, len(srcs), len(land_shapes), copies
        n, nb = self.n, len(srcs) + len(land_shapes)
        lands = [lax.empty(s.shape, s.dtype) for s in land_shapes]
        bufs = [pltpu.with_memory_space_constraint(a, pltpu.HBM) for a in (*srcs, *lands, *riders)]
        na = len(bufs)

        def body(*refs):
            send_sems, recv_sems = refs[na], refs[na + 1]
            for cp in copies(refs[:n], refs[n:nb], send_sems, recv_sems):
                cp.start()
            refs[-1][...] = jnp.zeros(refs[-1].shape, F32)

        outs = pl.pallas_call(
            body, name=name + "_start",
            in_specs=[_HBM] * na,
            out_specs=[_SEM, _SEM] + [_HBM] * na + [pl.BlockSpec(memory_space=pltpu.VMEM)],
            out_shape=[pltpu.SemaphoreType.DMA((n_copies,)), pltpu.SemaphoreType.DMA((n_copies,))]
            + [pltpu.HBM(b.shape, b.dtype) for b in bufs] + [jax.ShapeDtypeStruct((8, LANES), F32)],
            input_output_aliases={i: 2 + i for i in range(na)},
            compiler_params=pltpu.CompilerParams(has_side_effects=_EFFECT),
        )(*bufs)
        self.sems, self.bufs, self.riders, self.token = outs[:2], outs[2:2 + nb], outs[2 + nb:2 + na], outs[-1]

    def after(self):
        return self.token[0, 0]

    def wait(self, done, riders=(), bufs=None):
        n, nb, copies = self.n, self.n + self.nl, self.copies
        bufs = [*(self.bufs if bufs is None else bufs),
                *[pltpu.with_memory_space_constraint(a, pltpu.HBM) for a in riders]]
        na = len(bufs)
        done = list(done) if isinstance(done, (list, tuple)) else [done]

        def body(*refs):
            send_sems, recv_sems = refs[na], refs[na + 1]
            for cp in copies(refs[:n], refs[n:nb], send_sems, recv_sems):
                cp.wait_send()
                cp.wait_recv()

        outs = pl.pallas_call(
            body, name=self.name + "_wait",
            in_specs=[_HBM] * na + [_SEM, _SEM] + [_ANY] * len(done),
            out_specs=[_HBM] * na,
            out_shape=[pltpu.HBM(b.shape, b.dtype) for b in bufs],
            input_output_aliases={i: i for i in range(na)},
            compiler_params=pltpu.CompilerParams(has_side_effects=_EFFECT),
        )(*bufs, *self.sems, *done)
        return outs[:n], outs[n:nb], outs[nb:]


def _sibling_copies(srcs, lands, send_sems, recv_sems):
    x, y, c = _place()
    out = []
    for a, (src, land) in enumerate(zip(srcs, lands)):
        h = src.shape[1] // 2
        theirs = pl.ds(pl.multiple_of((1 - c) * h, 8), h)
        out.append(_remote(src.at[:, theirs], land, send_sems, recv_sems, a, (x, y, 1 - c)))
    return out


def _grads_to_sibling(name, grads):
    shapes = [jax.ShapeDtypeStruct((4, g.shape[1] // 2, g.shape[2]), g.dtype) for g in grads]
    return _SplitExchange(name, grads, shapes, len(grads), _sibling_copies)


def _chip_copies(srcs, lands, send_sems, recv_sems):
    x, y, c = _place()
    out = []
    for a, (src, land) in enumerate(zip(srcs, lands)):
        for t, (fx, fy) in enumerate(_CHIP_FLIPS):
            tx, ty = _flip(x, fx), _flip(y, fy)
            out.append(_remote(src.at[2 * tx + ty], land.at[t], send_sems, recv_sems, 3 * a + t, (tx, ty, c)))
    return out


def _grads_to_chips(name, parts):
    shapes = [jax.ShapeDtypeStruct((3, *p.shape[1:]), p.dtype) for p in parts]
    return _SplitExchange(name, parts, shapes, 3 * len(parts), _chip_copies)


def _add_halves(g, r, half):
    _, rows, cols = g.shape
    h = rows // 2
    tm = min(h, 128)
    nt = h // tm

    def body(half_ref, g_ref, r_ref, b_ref):
        b_ref[...] = (g_ref[...] + r_ref[...]).astype(b_ref.dtype)

    spec = pl.BlockSpec((1, tm, cols), lambda j, i, half_ref: (j, i, 0))
    return pl.pallas_call(
        body,
        grid_spec=pltpu.PrefetchScalarGridSpec(
            num_scalar_prefetch=1, grid=(4, nt),
            in_specs=[pl.BlockSpec((1, tm, cols), lambda j, i, half_ref: (j, half_ref[0] * nt + i, 0)), spec],
            out_specs=spec),
        out_shape=jax.ShapeDtypeStruct((4, h, cols), BF16),
        name="add_sibling_grads",
    )(half, g, r)


def _add_chips(g, r, recv, where):
    _, h, cols = r.shape
    tm = min(h, 128)
    nt = h // tm

    def body(where_ref, g_ref, r_ref, recv_ref, out_ref):
        own = g_ref[0] + r_ref[0]
        out_ref[...] = ((own + recv_ref[0].astype(F32)) + recv_ref[1].astype(F32)) + recv_ref[2].astype(F32)

    return pl.pallas_call(
        body,
        grid_spec=pltpu.PrefetchScalarGridSpec(
            num_scalar_prefetch=1, grid=(nt,),
            in_specs=[pl.BlockSpec((1, tm, cols), lambda i, w: (w[0], w[1] * nt + i, 0)),
                      pl.BlockSpec((1, tm, cols), lambda i, w: (w[0], i, 0)),
                      pl.BlockSpec((3, tm, cols), lambda i, w: (0, i, 0))],
            out_specs=pl.BlockSpec((tm, cols), lambda i, w: (w[1] * nt + i, 0))),
        out_shape=jax.ShapeDtypeStruct((2 * h, cols), F32),
        name="add_chip_grads",
    )(where, g, r, recv)


def _share_halves(shards):
    n = len(shards)

    def body(*refs):
        outs = refs[n:2 * n]
        send_sems, recv_sems = refs[2 * n:]
        x, y, c = _place()
        copies = []
        for a in range(n):
            h = outs[a].shape[0] // 2
            mine = outs[a].at[pl.ds(pl.multiple_of(c * h, 8), h)]
            copies.append(_remote(mine, mine, send_sems, recv_sems, a, (x, y, 1 - c)))
        for cp in copies:
            cp.start()
        for a, cp in enumerate(copies):
            cp.wait_send()
            h = outs[a].shape[0] // 2
            theirs = outs[a].at[pl.ds(pl.multiple_of((1 - c) * h, 8), h)]
            _remote(theirs, theirs, send_sems, recv_sems, a, (x, y, 1 - c)).wait_recv()

    return pl.pallas_call(
        body, in_specs=[_ANY] * n, out_specs=[_ANY] * n,
        out_shape=[jax.ShapeDtypeStruct(p.shape, p.dtype) for p in shards],
        input_output_aliases={a: a for a in range(n)},
        scratch_shapes=[pltpu.SemaphoreType.DMA((n,)), pltpu.SemaphoreType.DMA((n,))],
        name="share_reduced_halves",
    )(*shards)


def _reduce_small(rows):
    cols = rows[0].shape[1]
    n = len(rows)
    assert sum(r.shape[0] for r in rows) <= 8

    def body(*refs):
        ins, out_ref = refs[:n], refs[n]
        vec_ref, gath_ref, send_sems, recv_sems = refs[n + 1:]
        x, y, c = _place()
        me = 4 * x + 2 * y + c
        vec_ref[...] = jnp.zeros(vec_ref.shape, F32)
        at = 0
        for r in ins:
            vec_ref[at:at + r.shape[0], :] = r[...]
            at += r.shape[0]
        copies = []
        for k in range(1, 8):
            peer = (_flip(x, (k >> 2) & 1), _flip(y, (k >> 1) & 1), _flip(c, k & 1))
            copies.append(_remote(vec_ref, gath_ref.at[me], send_sems, recv_sems, k - 1, peer))
        for cp in copies:
            cp.start()
        gath_ref[me] = vec_ref[...]
        for cp in copies:
            cp.wait()
        tot = gath_ref[0]
        for dev in range(1, 8):
            tot = tot + gath_ref[dev]
        out_ref[...] = tot
        out_ref[7:8, :] = jnp.zeros((1, cols), F32) + jnp.sum(tot[7:8, :])

    vm = pl.BlockSpec(memory_space=pltpu.VMEM)
    return pl.pallas_call(
        body, in_specs=[vm] * n, out_specs=vm,
        out_shape=jax.ShapeDtypeStruct((8, cols), F32),
        scratch_shapes=[pltpu.VMEM((8, cols), F32), pltpu.VMEM((8, 8, cols), F32),
                        pltpu.SemaphoreType.DMA((7,)), pltpu.SemaphoreType.DMA((7,))],
        name="reduce_small",
    )(*rows)


def _adamw(w, g, m, v, name):
    rows, cols = w.shape
    tm = 128 if rows % 128 == 0 else rows

    def body(w_ref, g_ref, m_ref, v_ref, d_ref, m2_ref, v2_ref, gout_ref):
        gr = g_ref[...]
        m2 = ADAM_B1 * m_ref[...] + (1.0 - ADAM_B1) * gr
        v2 = ADAM_B2 * v_ref[...] + (1.0 - ADAM_B2) * (gr * gr)
        m_hat = m2 / (1.0 - ADAM_B1 ** ADAM_STEP)
        v_hat = v2 / (1.0 - ADAM_B2 ** ADAM_STEP)
        d_ref[...] = -ADAM_LR * (m_hat / (jnp.sqrt(v_hat) + ADAM_EPS) + ADAM_WD * w_ref[...])
        m2_ref[...] = m2
        v2_ref[...] = v2
        gout_ref[...] = gr

    spec = pl.BlockSpec((tm, cols), lambda i: (i, 0))
    sds = jax.ShapeDtypeStruct((rows, cols), F32)
    return pl.pallas_call(body, grid=(rows // tm,), in_specs=[spec] * 4, out_specs=[spec] * 4,
                          out_shape=[sds] * 4, name=name)(w, g, m, v)


def kernel(x, norm_g, w_in, b_merge, conv_w, w_out_conv, w_out_attn, w_o, final_g, loss_target, m_norm_g, m_w_in, m_b_merge, m_conv_w, m_w_out_conv, m_w_out_attn, m_w_o, m_final_g, v_norm_g, v_w_in, v_b_merge, v_conv_w, v_w_out_conv, v_w_out_attn, v_w_o, v_final_g):
    mx, my, mc = _place()
    chip = (2 * mx + my).astype(jnp.int32)
    seq = x.shape[1]

    chip1 = chip.reshape(1)
    slots = [_place_shard(w[0], chip1, MXU_DTYPE) for w in (w_in, w_out_conv, w_out_attn, w_o)]
    taps_slot = _place_shard(jnp.pad(conv_w[0], ((0, 5), (0, 0))), chip1, F32)
    gather_near = _SplitExchange("gather_w_in_near", [slots[0], taps_slot], [], 4,
                                 _gather_copies_to((0, 1), whole=(1,)))
    st = _Step(x[0], loss_target[0], norm_g, chip1, after=gather_near.token)
    st.project_own(w_in[0])
    near, _, _ = gather_near.wait([st.ut, st.proj_own])
    gather_far = _SplitExchange("gather_w_in_far", near, [], 2, _gather_copies_to((2,), whole=(1,)))
    (w4,) = _forward_to_sibling(gather_far.bufs[:1], flips=(0, 1))
    st.project(w4, (2, 1))
    (w4, taps4), _, out_slots = gather_far.wait([st.proj_own], riders=slots[1:], bufs=[w4, gather_far.bufs[1]])
    gather_out = _SplitExchange("gather_w_out", out_slots, [], 9, _gather_copies_to((0, 1, 2)), riders=[w4])
    (w4,) = _forward_to_sibling(gather_out.riders, flips=(2,))
    st.project(w4, (3,))
    st.mixers(w4, jnp.concatenate([taps4[j, :3, :] for j in range(4)], axis=1))
    out_ws, _, _ = gather_out.wait(st.o)
    woc, woa, wo = [w.reshape(D_MODEL, D_MODEL) for w in _forward_to_sibling(out_ws)]
    st.merge_and_loss(woc, woa, wo, b_merge, final_g.reshape(1, D_MODEL))

    half = mc.astype(jnp.int32).reshape(1)
    where = jnp.stack([chip, mc.astype(jnp.int32)])
    out_grads = [g.reshape(4, -1, D_MODEL) for g in st.out_weight_grads()]
    to_sibling = _grads_to_sibling("out_grads_to_sibling", out_grads)
    st.conv_grads(after=to_sibling.after())
    out_grads, out_from_sibling, _ = to_sibling.wait(st.da4)
    to_chips = _grads_to_chips("out_grads_to_chips",
                               [_add_halves(g, r, half) for g, r in zip(out_grads, out_from_sibling)])
    d_w4 = st.in_weight_grad(after=to_chips.after())
    out_from_chips = to_chips.wait(st.dc3)[1]

    to_sibling = _grads_to_sibling("in_grad_to_sibling", [d_w4])
    gx_lo, dg_lo = st.input_grad(0, after=to_sibling.after())
    (d_w4,), (from_sibling,), _ = to_sibling.wait(gx_lo)
    to_chips = _grads_to_chips("in_grad_to_chips", [_add_halves(d_w4, from_sibling, half)])
    gx_hi, dg_hi = st.input_grad(1, after=to_chips.after())
    grad_x = _to_natural(gx_lo, gx_hi)

    where_late = where + to_chips.after().astype(jnp.int32)
    out_reduced = [_add_chips(g, r, recv, where_late)
                   for g, r, recv in zip(out_grads, out_from_sibling, out_from_chips)]
    g_woc, g_woa, g_wo = _share_halves(out_reduced)
    small = _reduce_small([dg_lo + dg_hi, st.d_bias.reshape(2, D_MODEL), st.d_taps, st.d_final_g,
                           st.loss8.reshape(1, D_MODEL)])
    loss = (0.5 / D_MODEL) * small[7, 0]
    g_taps = lax.dynamic_slice(small[3:6], (0, chip * (D_MODEL // 4)), (3, D_MODEL // 4))
    upd = {
        "norm_g": _adamw(norm_g, small[0:1], m_norm_g, v_norm_g, "adamw_norm_g"),
        "b_merge": _adamw(b_merge, small[1:3].reshape(1, 2 * D_MODEL), m_b_merge, v_b_merge, "adamw_b_merge"),
        "conv_w": _adamw(conv_w[0], g_taps, m_conv_w[0], v_conv_w[0], "adamw_conv_w"),
        "w_out_conv": _adamw(w_out_conv[0], g_woc, m_w_out_conv[0], v_w_out_conv[0], "adamw_w_out_conv"),
        "w_out_attn": _adamw(w_out_attn[0], g_woa, m_w_out_attn[0], v_w_out_attn[0], "adamw_w_out_attn"),
        "w_o": _adamw(w_o[0], g_wo, m_w_o[0], v_w_o[0], "adamw_w_o"),
        "final_g": _adamw(final_g.reshape(1, D_MODEL), small[6:7], m_final_g.reshape(1, D_MODEL),
                          v_final_g.reshape(1, D_MODEL), "adamw_final_g"),
    }
    behind = [grad_x] + [u[0] for u in upd.values()]
    in_reduced = _add_chips(d_w4, from_sibling, to_chips.wait(behind)[1][0], where)
    (g_w_in,) = _share_halves([in_reduced])
    upd["w_in"] = _adamw(w_in[0], g_w_in, m_w_in[0], v_w_in[0], "adamw_w_in")

    names = ["norm_g", "w_in", "b_merge", "conv_w", "w_out_conv", "w_out_attn", "w_o", "final_g"]
    shapes = [norm_g.shape, w_in.shape, b_merge.shape, conv_w.shape, w_out_conv.shape, w_out_attn.shape,
              w_o.shape, final_g.shape]
    outs = [loss, grad_x.reshape(1, seq, D_MODEL)]
    for k in (3, 0, 1, 2):
        outs += [upd[n][k].reshape(s) for n, s in zip(names, shapes)]
    return tuple(outs)
```

```python
import functools

import numpy as np
import jax
import jax.numpy as jnp
from jax import lax
from jax.experimental import pallas as pl
from jax.experimental.pallas import tpu as pltpu

F32 = jnp.float32
BF16 = jnp.bfloat16
MXU_DTYPE = jnp.bfloat16
ACT_DTYPE = jnp.bfloat16

D_MODEL = 1024
N_HEADS = 16
HEAD_DIM = 64
QB = 128
N_RES = 16
LANES = 128
HP = N_HEADS * HEAD_DIM // LANES
IN_COLS = 10 * D_MODEL
SHARD_COLS = IN_COLS // 4
EPS = 1e-6
NEG = -1e30

ADAM_LR, ADAM_B1, ADAM_B2, ADAM_EPS, ADAM_WD, ADAM_STEP = 0.001, 0.9, 0.999, 1e-08, 0.01, 10

PATTERNS = {1: (16, 8), 4: (4, 32), 16: (1, 128)}

_NN = (((1,), (0,)), ((), ()))
_NT = (((1,), (1,)), ((), ()))


def _dot(a, b):
    return lax.dot_general(a.astype(MXU_DTYPE), b.astype(MXU_DTYPE), _NN, preferred_element_type=F32)


def _dot_nt(a, b):
    return lax.dot_general(a.astype(MXU_DTYPE), b.astype(MXU_DTYPE), _NT, preferred_element_type=F32)


def _split3(x):
    hi = x.astype(BF16)
    r1 = x - hi.astype(F32)
    mid = r1.astype(BF16)
    lo = (r1 - mid.astype(F32)).astype(BF16)
    return hi, mid, lo


def _select_cols(x, sel, terms):
    return sum(lax.dot_general(t, sel, _NN, preferred_element_type=F32) for t in _split3(x)[:terms])


def _sigmoid(z):
    return 1.0 / (1.0 + jnp.exp(-z))


def _head_expand_matrix():
    e = np.zeros((LANES, D_MODEL), np.float32)
    for h in range(N_HEADS):
        e[8 * h, HEAD_DIM * h:HEAD_DIM * (h + 1)] = 1.0
    return jnp.asarray(e, BF16)


def _head_sum_matrix():
    e = np.zeros((D_MODEL, LANES), np.float32)
    for h in range(N_HEADS):
        e[HEAD_DIM * h:HEAD_DIM * (h + 1), 8 * h:8 * (h + 1)] = 1.0
    return jnp.asarray(e, BF16)


def _attn_tables(d):
    g_n, rq = PATTERNS[d]
    q_n = g_n * rq
    gq, iq = np.arange(q_n) // rq, np.arange(q_n) % rq

    def tab(kn, base):
        k_n = g_n * kn
        gk, jk = np.arange(k_n) // kn, np.arange(k_n) % kn
        delta = g_n * (base + iq[:, None] - jk[None, :]) + gq[:, None] - gk[None, :]
        valid = (delta >= 0) & (delta <= QB)
        dist = np.where(valid, d * delta, 0).astype(np.float32)
        madd = np.where(valid, 0.0, NEG).astype(np.float32)
        return dist, madd

    d0, m0 = tab(rq if g_n == 1 else 2 * rq, 0)
    d1, m1 = tab(2 * rq, rq)
    return d0, m0, d1, m1


def _alibi_slopes():
    return jnp.exp2(-8.0 * jnp.arange(1, N_HEADS + 1, dtype=F32) / N_HEADS)


def _to_residue_major(x, tgt, after=None):
    s_n, c_n = x.shape
    lr = s_n // N_RES
    extra = [] if after is None else [after]

    def body(x_ref, t_ref, *rest):
        xo_ref, to_ref = rest[-2:]
        for r in range(N_RES):
            xo_ref[r] = x_ref[pl.ds(r, lr, stride=N_RES), :]
            to_ref[r] = t_ref[pl.ds(r, lr, stride=N_RES), :]

    nat = pl.BlockSpec((s_n, LANES), lambda j: (0, j))
    res = pl.BlockSpec((N_RES, lr, LANES), lambda j: (0, 0, j))
    xo, to = pl.pallas_call(
        body, grid=(c_n // LANES,),
        in_specs=[nat, nat] + [pl.BlockSpec((8, LANES), lambda j: (0, 0))] * len(extra),
        out_specs=[res, res],
        out_shape=[jax.ShapeDtypeStruct((N_RES, lr, c_n), F32)] * 2,
        name="perm_in",
    )(x, tgt, *extra)
    return xo.reshape(s_n, c_n), to.reshape(s_n, c_n)


def _to_natural(gx_lo, gx_hi):
    half_rows, c_n = gx_lo.shape
    lr = half_rows // (N_RES // 2)

    def body(lo_ref, hi_ref, o_ref):
        for r in range(N_RES):
            o_ref[pl.ds(r, lr, stride=N_RES), :] = lo_ref[r] if r < N_RES // 2 else hi_ref[r - N_RES // 2]

    half = pl.BlockSpec((N_RES // 2, lr, LANES), lambda j: (0, 0, j))
    return pl.pallas_call(
        body, grid=(c_n // LANES,),
        in_specs=[half, half],
        out_specs=pl.BlockSpec((2 * half_rows, LANES), lambda j: (0, j)),
        out_shape=jax.ShapeDtypeStruct((2 * half_rows, c_n), F32),
        name="perm_out",
    )(gx_lo.reshape(N_RES // 2, lr, c_n), gx_hi.reshape(N_RES // 2, lr, c_n))


def _rms_in(xp, norm_g):
    s_n, c_n = xp.shape
    tm = 512

    def body(x_ref, g_ref, u_ref, ut_ref):
        x = x_ref[...]
        r = lax.rsqrt(jnp.mean(x * x, axis=-1, keepdims=True) + EPS)
        u = x * r * g_ref[...]
        u_ref[...] = u.astype(u_ref.dtype)
        ut_ref[...] = u.T.astype(ut_ref.dtype)

    return pl.pallas_call(
        body, grid=(s_n // tm,),
        in_specs=[pl.BlockSpec((tm, c_n), lambda i: (i, 0)), pl.BlockSpec((1, c_n), lambda i: (0, 0))],
        out_specs=[pl.BlockSpec((tm, c_n), lambda i: (i, 0)), pl.BlockSpec((c_n, tm), lambda i: (0, i))],
        out_shape=[jax.ShapeDtypeStruct((s_n, c_n), ACT_DTYPE), jax.ShapeDtypeStruct((c_n, s_n), ACT_DTYPE)],
        name="rms_in",
    )(xp, norm_g)


def _in_proj(u, chip, w_own=None, w4=None, partial=None, others=()):
    s_n = u.shape[0]
    tn, cm = 512, 512
    per = SHARD_COLS // tn
    own = partial is None

    def body(chip_ref, a_ref, b_ref, *rest):
        o_ref = rest[-1]
        b = b_ref[...]
        for c in range(s_n // cm):
            o_ref[c * cm:(c + 1) * cm, :] = _dot(a_ref[c * cm:(c + 1) * cm, :], b).astype(o_ref.dtype)

    def shard(n, chip_ref):
        if own:
            return chip_ref[0]
        mask = others[-1]
        for i, m in enumerate(others[:-1]):
            mask = jnp.where(n // per == i, m, mask)
        return jnp.bitwise_xor(chip_ref[0], mask)

    w_spec = (pl.BlockSpec((D_MODEL, tn), lambda n, c: (0, n)) if own else
              pl.BlockSpec((None, D_MODEL, tn), lambda n, c: (shard(n, c), 0, n % per)))
    return pl.pallas_call(
        body,
        grid_spec=pltpu.PrefetchScalarGridSpec(
            num_scalar_prefetch=1, grid=(per if own else len(others) * per,),
            in_specs=[pl.BlockSpec((s_n, D_MODEL), lambda n, c: (0, 0)), w_spec] + ([] if own else [_ANY]),
            out_specs=pl.BlockSpec((s_n, tn), lambda n, c: (0, shard(n, c) * per + n % per))),
        out_shape=jax.ShapeDtypeStruct((s_n, IN_COLS), ACT_DTYPE),
        input_output_aliases={} if own else {3: 0},
        name="in_proj_own" if own else "in_proj_" + "_".join(str(m) for m in others),
    )(*([chip, u, w_own] if own else [chip, u, w4, partial]))


def _conv_terms(xc_ref, cg_ref, r, row, lr, cache):
    def a_of(q):
        if q not in cache:
            cache[q] = cg_ref[q].astype(F32) * xc_ref[q].astype(F32)
        return cache[q]

    def shift_down(v):
        return jnp.where(row >= 1, pltpu.roll(v, 1, 0), 0.0)

    a = a_of(r)
    am1 = a_of(r - 1) if r >= 1 else shift_down(a_of(N_RES - 1))
    am2 = a_of(r - 2) if r >= 2 else shift_down(a_of(N_RES - 2 + r))
    return a, am1, am2


def _conv_fwd(proj, conv_w):
    s_n = proj.shape[0]
    lr = s_n // N_RES
    pv = proj.reshape(N_RES, lr, IN_COLS)

    def body(xc_ref, bg_ref, cg_ref, zc_ref, w_ref, hc_ref, hct_ref):
        w = w_ref[...]
        row = lax.broadcasted_iota(jnp.int32, (lr, LANES), 0)
        products = {}
        for r in range(N_RES):
            a, am1, am2 = _conv_terms(xc_ref, cg_ref, r, row, lr, products)
            c = w[0:1] * am2 + w[1:2] * am1 + w[2:3] * a
            z = zc_ref[r].astype(F32)
            hc = z * _sigmoid(z) * bg_ref[r].astype(F32) * c
            hc_ref[r] = hc.astype(hc_ref.dtype)
            hct_ref[:, r * lr:(r + 1) * lr] = hc.T.astype(hct_ref.dtype)

    def col(part):
        return pl.BlockSpec((N_RES, lr, LANES), lambda j: (0, 0, part * 8 + j))

    hc, hct = pl.pallas_call(
        body, grid=(D_MODEL // LANES,),
        in_specs=[col(0), col(1), col(2), col(3), pl.BlockSpec((3, LANES), lambda j: (0, j))],
        out_specs=[pl.BlockSpec((N_RES, lr, LANES), lambda j: (0, 0, j)),
                   pl.BlockSpec((LANES, s_n), lambda j: (j, 0))],
        out_shape=[jax.ShapeDtypeStruct((N_RES, lr, D_MODEL), ACT_DTYPE),
                   jax.ShapeDtypeStruct((D_MODEL, s_n), ACT_DTYPE)],
        name="conv_fwd",
    )(pv, pv, pv, pv, conv_w)
    return hc.reshape(s_n, D_MODEL), hct


RES_PER_STEP = 8
ATTN_BATCH = 8

_BNT = (((2,), (2,)), ((0,), (0,)))
_BNN = (((2,), (1,)), ((0,), (0,)))


def _bdot(a, b, dims):
    return lax.dot_general(a.astype(MXU_DTYPE), b.astype(MXU_DTYPE), dims, preferred_element_type=F32)


def _pattern_view_shape(s_n, c_n, g_n, lead=()):
    lr = s_n // N_RES
    return (*lead, 4, 4, lr, c_n) if g_n == 4 else (*lead, N_RES, lr, c_n)


def _pattern_view(a, g_n, lead=()):
    return a.reshape(_pattern_view_shape(a.shape[-2], a.shape[-1], g_n, lead))


def _pattern_grid(g_n):
    return (N_RES // RES_PER_STEP if g_n == 1 else N_RES // g_n, HP)


def _pattern_spec(g_n, lr, col_of_hp, lead=()):
    z = (0,) * len(lead)
    if g_n == 16:
        return pl.BlockSpec((*lead, 16, lr, LANES), lambda r, hp: (*z, 0, 0, col_of_hp(hp)))
    if g_n == 4:
        return pl.BlockSpec((*lead, 4, None, lr, LANES), lambda r, hp: (*z, 0, r, 0, col_of_hp(hp)))
    return pl.BlockSpec((*lead, RES_PER_STEP, lr, LANES), lambda r, hp: (*z, r, 0, col_of_hp(hp)))


def _aligned(start, m):
    return start if isinstance(start, int) else pl.multiple_of(start, m)


class _Units:
    def __init__(self, g_n, rq):
        self.g_n, self.rq = g_n, rq
        self.per_res, self.paired = g_n == 1, rq == 8

    def plan(self, lr, size):
        if self.per_res:
            return [0], lr // self.rq - 1, lambda j: [pl.multiple_of(j * self.rq, self.rq)]
        step = 16 if self.paired else self.rq
        per = min(size // 2 if self.paired else size, lr // step)
        assert (lr // step) % per == 0
        return ([i * step for i in range(per)], lr // step // per - 1,
                lambda j: [pl.multiple_of((j * per + i) * step, step) for i in range(per)])

    def count(self, qs):
        return RES_PER_STEP if self.per_res else len(qs) * (2 if self.paired else 1)

    def _split(self, tiles, lo, rows):
        return tiles[:, lo:lo + rows].reshape(self.g_n * rows, LANES)

    def load_q(self, ref, qs):
        rq = self.rq
        if self.per_res:
            return ref[:, pl.ds(qs[0], rq), :]
        if self.paired:
            tiles = [ref[:, pl.ds(q, 16), :].astype(F32) for q in qs]
            return jnp.stack([self._split(t, lo, 8) for t in tiles for lo in (0, 8)])
        return jnp.stack([ref[:, pl.ds(q, rq), :].reshape(self.g_n * rq, LANES) for q in qs])

    def _key_rows(self, q, at_start):
        return (0, 2 * self.rq) if at_start else (_aligned(q - self.rq, self.rq), 2 * self.rq)

    def load_k(self, ref, qs, first):
        rq = self.rq
        if self.per_res:
            return ref[:, pl.ds(0, rq), :] if first else ref[:, pl.ds(_aligned(qs[0] - rq, rq), 2 * rq), :]
        if self.paired:
            out = []
            for i, q in enumerate(qs):
                if first and i == 0:
                    t = ref[:, 0:16, :].astype(F32)
                    out += [self._split(t, 0, 16)] * 2
                else:
                    t = ref[:, pl.ds(_aligned(q - 16, 16), 32), :].astype(F32)
                    out += [self._split(t, 8, 16), self._split(t, 16, 16)]
            return jnp.stack(out)
        rows = [self._key_rows(q, first and i == 0) for i, q in enumerate(qs)]
        return jnp.stack([ref[:, pl.ds(k0, n), :].reshape(self.g_n * n, LANES) for k0, n in rows])

    def store_q(self, ref, qs, val, add=False, lead=()):
        if self.per_res:
            pieces = [(qs[0], self.rq, val)]
        elif self.paired:
            pieces = [(q, 16, jnp.concatenate([val[2 * i].reshape(self.g_n, 8, LANES),
                                               val[2 * i + 1].reshape(self.g_n, 8, LANES)], axis=1))
                      for i, q in enumerate(qs)]
        else:
            pieces = [(q, self.rq, val[i].reshape(self.g_n, self.rq, LANES)) for i, q in enumerate(qs)]
        for start, rows, v in pieces:
            idx = (*lead, slice(None), pl.ds(start, rows), slice(None))
            ref[idx] = (ref[idx] + v if add else v).astype(ref.dtype)

    def add_k(self, ref, qs, val, first):
        rq = self.rq
        if self.per_res:
            k0, n = (0, rq) if first else (_aligned(qs[0] - rq, rq), 2 * rq)
            ref[:, pl.ds(k0, n), :] += val
            return
        if self.paired:
            starts = [s for i, q in enumerate(qs)
                      for s in ((0, 0) if first and i == 0 else (_aligned(q - 8, 8), q))]
            rows = [(s, 16) for s in starts]
        else:
            rows = [self._key_rows(q, first and i == 0) for i, q in enumerate(qs)]
        for b, (k0, n) in enumerate(rows):
            ref[:, pl.ds(k0, n), :] += val[b].reshape(self.g_n, n, LANES)


def _batch_bias(un, qs, at_start, first_ref, general_ref):
    if not at_start:
        return general_ref[...][None]
    if un.per_res:
        return first_ref[...][None]
    return jnp.concatenate([first_ref[...][None]] + [general_ref[...][None]] * (un.count(qs) - 1), axis=0)


def _stack_heads(x, low):
    zero = jnp.zeros_like(x)
    return jnp.concatenate([jnp.where(low, x, zero), jnp.where(low, zero, x)], axis=1)


def _attn_fwd(proj, slopes, d):
    g_n, rq = PATTERNS[d]
    un = _Units(g_n, rq)
    s_n = proj.shape[0]
    lr = s_n // N_RES
    nb = lr // rq
    q_n = g_n * rq
    d0, m0, d1, m1 = _attn_tables(d)
    first, n_more, later = un.plan(lr, ATTN_BATCH)

    def body(sl_ref, q_ref, k_ref, v_ref, d0_ref, m0_ref, d1_ref, m1_ref, o_ref, lse_ref, b0_ref, b1_ref):
        hp = pl.program_id(1)

        @pl.when(hp == 0)
        def _():
            lse_ref[...] = jnp.zeros(lse_ref.shape, F32)

        for h in (0, 1):
            slope = sl_ref[2 * hp + h]
            b0_ref[h * q_n:(h + 1) * q_n, :] = m0_ref[...] - slope * d0_ref[...]
            b1_ref[h * q_n:(h + 1) * q_n, :] = m1_ref[...] - slope * d1_ref[...]

        lane = lax.broadcasted_iota(jnp.int32, (1, q_n, LANES), 2)
        low = lane < HEAD_DIM
        grp = lane // 8

        def batch(qs, at_start):
            qq = _stack_heads(un.load_q(q_ref, qs) * 0.125, low)
            s = _bdot(qq, un.load_k(k_ref, qs, at_start), _BNT) + _batch_bias(un, qs, at_start, b0_ref, b1_ref)
            m = jnp.max(s, axis=2, keepdims=True)
            p = jnp.exp(s - m)
            l = jnp.sum(p, axis=2, keepdims=True)
            o = _bdot(p, un.load_k(v_ref, qs, at_start), _BNN) * (1.0 / l)
            lse = m + jnp.log(l)
            un.store_q(o_ref, qs, jnp.where(low, o[:, :q_n], o[:, q_n:]))
            upd = jnp.where(grp == 2 * hp, lse[:, :q_n], 0.0) + jnp.where(grp == 2 * hp + 1, lse[:, q_n:], 0.0)
            un.store_q(lse_ref, qs, upd, add=True)

        batch(first, True)

        def more(j, carry):
            batch(later(j), False)
            return carry

        lax.fori_loop(1, 1 + n_more, more, 0)

    pv = _pattern_view(proj, g_n)
    full = lambda a: pl.BlockSpec(a.shape, lambda r, hp: (0, 0))
    o, lse = pl.pallas_call(
        body, grid=_pattern_grid(g_n),
        in_specs=[pl.BlockSpec(memory_space=pltpu.SMEM),
                  _pattern_spec(g_n, lr, lambda hp: 32 + hp),
                  _pattern_spec(g_n, lr, lambda hp: 40 + hp),
                  _pattern_spec(g_n, lr, lambda hp: 48 + hp),
                  full(d0), full(m0), full(d1), full(m1)],
        out_specs=[_pattern_spec(g_n, lr, lambda hp: hp), _pattern_spec(g_n, lr, lambda hp: 0)],
        out_shape=[jax.ShapeDtypeStruct(_pattern_view_shape(s_n, D_MODEL, g_n), ACT_DTYPE),
                   jax.ShapeDtypeStruct(_pattern_view_shape(s_n, LANES, g_n), F32)],
        scratch_shapes=[pltpu.VMEM((2 * q_n, d0.shape[1]), F32), pltpu.VMEM((2 * q_n, 2 * q_n), F32)],
        name=f"attn_fwd_d{d}",
    )(slopes, pv, pv, pv, d0, m0, d1, m1)
    return o.reshape(s_n, D_MODEL), lse.reshape(s_n, LANES)


def _attn_combine(outs, lses, proj):
    s_n = proj.shape[0]
    tm = 512

    def body(o1_ref, o2_ref, o3_ref, l1_ref, l2_ref, l3_ref, za_ref, e_ref, o_ref, lse_ref, ha_ref, hat_ref):
        ls = [l1_ref[...], l2_ref[...], l3_ref[...]]
        mx = jnp.maximum(jnp.maximum(ls[0], ls[1]), ls[2])
        den = sum(jnp.exp(l - mx) for l in ls)
        lse = mx + jnp.log(den)
        lse_ref[...] = lse
        o = jnp.zeros((tm, D_MODEL), F32)
        for l, oref in zip(ls, (o1_ref, o2_ref, o3_ref)):
            o = o + _select_cols(jnp.exp(l - lse), e_ref[...], terms=2) * oref[...].astype(F32)
        o_ref[...] = o.astype(o_ref.dtype)
        z = za_ref[...].astype(F32)
        ha = z * _sigmoid(z) * o
        ha_ref[...] = ha.astype(ha_ref.dtype)
        hat_ref[...] = ha.T.astype(hat_ref.dtype)

    row = lambda w: pl.BlockSpec((tm, w), lambda i: (i, 0))
    return pl.pallas_call(
        body, grid=(s_n // tm,),
        in_specs=[row(D_MODEL)] * 3 + [row(LANES)] * 3
        + [pl.BlockSpec((tm, D_MODEL), lambda i: (i, 7)), pl.BlockSpec((LANES, D_MODEL), lambda i: (0, 0))],
        out_specs=[row(D_MODEL), row(LANES), row(D_MODEL), pl.BlockSpec((D_MODEL, tm), lambda i: (0, i))],
        out_shape=[jax.ShapeDtypeStruct((s_n, D_MODEL), ACT_DTYPE), jax.ShapeDtypeStruct((s_n, LANES), F32),
                   jax.ShapeDtypeStruct((s_n, D_MODEL), ACT_DTYPE), jax.ShapeDtypeStruct((D_MODEL, s_n), ACT_DTYPE)],
        name="attn_combine",
    )(*outs, *lses, proj, _head_expand_matrix())


CHAIN_ROWS = 256


def _row_chains(tm):
    return [slice(r, r + CHAIN_ROWS) for r in range(0, tm, CHAIN_ROWS)]


def _gates(gc_ref, ga_ref, b_ref, rows):
    b = b_ref[...]
    gc = _sigmoid(gc_ref[rows, :].astype(F32) + b[:, :D_MODEL])
    ga = _sigmoid(ga_ref[rows, :].astype(F32) + b[:, D_MODEL:])
    return gc, ga


def _merge_loss(hc, ha, woc, woa, wo, proj, b_merge, xp, final_g, tgt):
    s_n = xp.shape[0]
    tm = 512

    def body(hc_ref, ha_ref, woc_ref, woa_ref, wo_ref, gc_ref, ga_ref, b_ref, x_ref, gf_ref, t_ref,
             yc_ref, ya_ref, dhb_ref, dgf_ref, loss_ref, dwo_ref, mgt_ref):
        i = pl.program_id(0)

        @pl.when(i == 0)
        def _():
            dgf_ref[...] = jnp.zeros(dgf_ref.shape, F32)
            loss_ref[...] = jnp.zeros(loss_ref.shape, F32)
            dwo_ref[...] = jnp.zeros(dwo_ref.shape, F32)

        gf = gf_ref[...]
        for rows in _row_chains(tm):
            yc = _dot(hc_ref[rows, :], woc_ref[...])
            ya = _dot(ha_ref[rows, :], woa_ref[...])
            gc, ga = _gates(gc_ref, ga_ref, b_ref, rows)
            mg = gc * yc + ga * ya
            yc_ref[rows, :] = yc.astype(yc_ref.dtype)
            ya_ref[rows, :] = ya.astype(ya_ref.dtype)
            mgt_ref[:, rows] = mg.T.astype(mgt_ref.dtype)
            h2 = x_ref[rows, :] + _dot(mg, wo_ref[...])
            r2 = lax.rsqrt(jnp.mean(h2 * h2, axis=-1, keepdims=True) + EPS)
            nrm = h2 * r2
            err = nrm * gf - t_ref[rows, :]
            e2 = (err * err).reshape(-1, 8, D_MODEL).sum(axis=0)
            loss_ref[...] += sum(e2[:, c * LANES:(c + 1) * LANES] for c in range(D_MODEL // LANES))
            dy = err * (1.0 / D_MODEL)
            dgf_ref[...] += jnp.sum(dy * nrm, axis=0, keepdims=True)
            dn = dy * gf
            dh2 = r2 * (dn - nrm * jnp.mean(dn * nrm, axis=-1, keepdims=True))
            dhb_ref[rows, :] = dh2.astype(dhb_ref.dtype)
        dwo_ref[...] += _dot(mgt_ref[...], dhb_ref[...])

    row = pl.BlockSpec((tm, D_MODEL), lambda i: (i, 0))
    wsp = pl.BlockSpec((D_MODEL, D_MODEL), lambda i: (0, 0), pipeline_mode=pl.Buffered(1))
    vec = lambda w: pl.BlockSpec((1, w), lambda i: (0, 0))
    act = jax.ShapeDtypeStruct((s_n, D_MODEL), ACT_DTYPE)
    return pl.pallas_call(
        body, grid=(s_n // tm,),
        in_specs=[row, row, wsp, wsp, wsp,
                  pl.BlockSpec((tm, D_MODEL), lambda i: (i, 8)), pl.BlockSpec((tm, D_MODEL), lambda i: (i, 9)),
                  vec(2 * D_MODEL), row, vec(D_MODEL), row],
        out_specs=[row, row, row, vec(D_MODEL), pl.BlockSpec((8, LANES), lambda i: (0, 0)),
                   pl.BlockSpec((D_MODEL, D_MODEL), lambda i: (0, 0))],
        out_shape=[act, act, act, jax.ShapeDtypeStruct((1, D_MODEL), F32), jax.ShapeDtypeStruct((8, LANES), F32),
                   jax.ShapeDtypeStruct((D_MODEL, D_MODEL), F32)],
        scratch_shapes=[pltpu.VMEM((D_MODEL, tm), MXU_DTYPE)],
        name="merge_loss",
    )(hc, ha, woc, woa, wo, proj, proj, b_merge, xp, final_g, tgt)


def _merge_bwd(dh2b, wo, woc, woa, yc, ya, proj, b_merge, o, hct, hat):
    s_n = dh2b.shape[0]
    tm = 512

    def body(dh_ref, wo_ref, woc_ref, woa_ref, yc_ref, ya_ref, gc_ref, ga_ref, b_ref, o_ref, za_ref, e_ref,
             hct_ref, hat_ref, dhc_ref, do_ref, dsum_ref, db3_ref, dbias_ref, dwoc_ref, dwoa_ref,
             dyc_ref, dya_ref):
        i = pl.program_id(0)

        @pl.when(i == 0)
        def _():
            dbias_ref[...] = jnp.zeros(dbias_ref.shape, F32)
            dwoc_ref[...] = jnp.zeros(dwoc_ref.shape, F32)
            dwoa_ref[...] = jnp.zeros(dwoa_ref.shape, F32)

        for rows in _row_chains(tm):
            dmg = _dot_nt(dh_ref[rows, :], wo_ref[...])
            gc, ga = _gates(gc_ref, ga_ref, b_ref, rows)
            dgc = dmg * yc_ref[rows, :].astype(F32) * gc * (1.0 - gc)
            dga = dmg * ya_ref[rows, :].astype(F32) * ga * (1.0 - ga)
            dbias_ref[:, :D_MODEL] += jnp.sum(dgc, axis=0, keepdims=True)
            dbias_ref[:, D_MODEL:] += jnp.sum(dga, axis=0, keepdims=True)
            dyc = dmg * gc
            dya = dmg * ga
            dyc_ref[rows, :] = dyc.astype(dyc_ref.dtype)
            dya_ref[rows, :] = dya.astype(dya_ref.dtype)
            dhc_ref[rows, :] = _dot_nt(dyc, woc_ref[...]).astype(dhc_ref.dtype)
            dha = _dot_nt(dya, woa_ref[...])
            z = za_ref[rows, :].astype(F32)
            sg = _sigmoid(z)
            ov = o_ref[rows, :].astype(F32)
            dout = dha * z * sg
            do_ref[rows, :] = dout.astype(do_ref.dtype)
            dsum_ref[rows, :] = _select_cols(dout * ov, e_ref[...], terms=2)
            db3_ref[0, rows, :] = (dha * ov * sg * (1.0 + z * (1.0 - sg))).astype(db3_ref.dtype)
            db3_ref[1, rows, :] = dgc.astype(db3_ref.dtype)
            db3_ref[2, rows, :] = dga.astype(db3_ref.dtype)
        dwoc_ref[...] += _dot(hct_ref[...], dyc_ref[...])
        dwoa_ref[...] += _dot(hat_ref[...], dya_ref[...])

    row = pl.BlockSpec((tm, D_MODEL), lambda i: (i, 0))
    col = pl.BlockSpec((D_MODEL, tm), lambda i: (0, i))
    wsp = pl.BlockSpec((D_MODEL, D_MODEL), lambda i: (0, 0), pipeline_mode=pl.Buffered(1))
    acc = pl.BlockSpec((D_MODEL, D_MODEL), lambda i: (0, 0))
    act = jax.ShapeDtypeStruct((s_n, D_MODEL), ACT_DTYPE)
    grad = jax.ShapeDtypeStruct((D_MODEL, D_MODEL), F32)
    return pl.pallas_call(
        body, grid=(s_n // tm,),
        in_specs=[row, wsp, wsp, wsp, row, row,
                  pl.BlockSpec((tm, D_MODEL), lambda i: (i, 8)), pl.BlockSpec((tm, D_MODEL), lambda i: (i, 9)),
                  pl.BlockSpec((1, 2 * D_MODEL), lambda i: (0, 0)), row,
                  pl.BlockSpec((tm, D_MODEL), lambda i: (i, 7)), pl.BlockSpec((D_MODEL, LANES), lambda i: (0, 0)),
                  col, col],
        out_specs=[row, row, pl.BlockSpec((tm, LANES), lambda i: (i, 0)),
                   pl.BlockSpec((3, tm, D_MODEL), lambda i: (0, i, 0)),
                   pl.BlockSpec((1, 2 * D_MODEL), lambda i: (0, 0)), acc, acc],
        out_shape=[act, act, jax.ShapeDtypeStruct((s_n, LANES), F32),
                   jax.ShapeDtypeStruct((3, s_n, D_MODEL), ACT_DTYPE),
                   jax.ShapeDtypeStruct((1, 2 * D_MODEL), F32), grad, grad],
        scratch_shapes=[pltpu.VMEM((tm, D_MODEL), MXU_DTYPE), pltpu.VMEM((tm, D_MODEL), MXU_DTYPE)],
        name="merge_bwd",
    )(dh2b, wo, woc, woa, yc, ya, proj, proj, b_merge, o, proj, _head_sum_matrix(), hct, hat)


def _conv_bwd(proj, conv_w, dhc):
    s_n = proj.shape[0]
    lr = s_n // N_RES
    pv = proj.reshape(N_RES, lr, IN_COLS)

    def body(xc_ref, bg_ref, cg_ref, zc_ref, w_ref, dhc_ref, da4_ref, dw_ref, dc_ref):
        w = w_ref[...]
        row = lax.broadcasted_iota(jnp.int32, (lr, LANES), 0)
        dw = [jnp.zeros((1, LANES), F32) for _ in range(3)]
        products = {}
        for r in range(N_RES):
            a, am1, am2 = _conv_terms(xc_ref, cg_ref, r, row, lr, products)
            c = w[0:1] * am2 + w[1:2] * am1 + w[2:3] * a
            z = zc_ref[r].astype(F32)
            sg = _sigmoid(z)
            sz = z * sg
            bg = bg_ref[r].astype(F32)
            dh = dhc_ref[r].astype(F32)
            da4_ref[1, r] = (dh * sz * c).astype(da4_ref.dtype)
            da4_ref[3, r] = (dh * bg * c * sg * (1.0 + z * (1.0 - sg))).astype(da4_ref.dtype)
            dc = dh * sz * bg
            dc_ref[r] = dc
            dw[0] = dw[0] + jnp.sum(dc * am2, axis=0, keepdims=True)
            dw[1] = dw[1] + jnp.sum(dc * am1, axis=0, keepdims=True)
            dw[2] = dw[2] + jnp.sum(dc * a, axis=0, keepdims=True)
        dw_ref[0:1, :] = dw[0]
        dw_ref[1:2, :] = dw[1]
        dw_ref[2:3, :] = dw[2]

        def shift_up(v):
            return jnp.where(row < lr - 1, pltpu.roll(v, lr - 1, 0), 0.0)

        for r in range(N_RES):
            dp1 = dc_ref[r + 1] if r + 1 < N_RES else shift_up(dc_ref[0])
            dp2 = dc_ref[r + 2] if r + 2 < N_RES else shift_up(dc_ref[r + 2 - N_RES])
            da = w[2:3] * dc_ref[r] + w[1:2] * dp1 + w[0:1] * dp2
            da4_ref[0, r] = (da * cg_ref[r].astype(F32)).astype(da4_ref.dtype)
            da4_ref[2, r] = (da * xc_ref[r].astype(F32)).astype(da4_ref.dtype)

    def col(part):
        return pl.BlockSpec((N_RES, lr, LANES), lambda j: (0, 0, part * 8 + j))

    da4, dw = pl.pallas_call(
        body, grid=(D_MODEL // LANES,),
        in_specs=[col(0), col(1), col(2), col(3), pl.BlockSpec((3, LANES), lambda j: (0, j)),
                  pl.BlockSpec((N_RES, lr, LANES), lambda j: (0, 0, j))],
        out_specs=[pl.BlockSpec((4, N_RES, lr, LANES), lambda j: (0, 0, 0, j)),
                   pl.BlockSpec((3, LANES), lambda j: (0, j))],
        out_shape=[jax.ShapeDtypeStruct((4, N_RES, lr, D_MODEL), ACT_DTYPE),
                   jax.ShapeDtypeStruct((3, D_MODEL), F32)],
        scratch_shapes=[pltpu.VMEM((N_RES, lr, LANES), F32)],
        name="conv_bwd",
    )(pv, pv, pv, pv, conv_w, dhc.reshape(N_RES, lr, D_MODEL))
    return da4.reshape(4, s_n, D_MODEL), dw


def _attn_bwd(proj, dout, lse, dsum, slopes, d, prev=None):
    g_n, rq = PATTERNS[d]
    un = _Units(g_n, rq)
    s_n = proj.shape[0]
    lr = s_n // N_RES
    nb = lr // rq
    q_n = g_n * rq
    d0, m0, d1, m1 = (np.ascontiguousarray(t.T) for t in _attn_tables(d))
    first, n_more, later = un.plan(lr, ATTN_BATCH)
    bsz = un.count(first)
    gd = RES_PER_STEP if un.per_res else g_n

    def body(sl_ref, q_ref, k_ref, v_ref, do_ref, lse_ref, ds_ref, d0_ref, m0_ref, d1_ref, m1_ref, *rest):
        prev_ref = rest[0] if prev is not None else None
        out_ref, b0_ref, b1_ref, lt_ref, dt_ref, dk_ref, dv_ref = rest[-7:]
        hp = pl.program_id(1)
        for h in (0, 1):
            slope = sl_ref[2 * hp + h]
            b0_ref[:, h * q_n:(h + 1) * q_n] = m0_ref[...] - slope * d0_ref[...]
            b1_ref[:, h * q_n:(h + 1) * q_n] = m1_ref[...] - slope * d1_ref[...]
        if prev is None:
            dk_ref[...] = jnp.zeros(dk_ref.shape, F32)
            dv_ref[...] = jnp.zeros(dv_ref.shape, F32)
        else:
            out_ref[0] = prev_ref[0]
            dk_ref[...] = prev_ref[1].astype(F32)
            dv_ref[...] = prev_ref[2].astype(F32)
        low = lax.broadcasted_iota(jnp.int32, (1, q_n, LANES), 2) < HEAD_DIM
        row16 = pl.multiple_of(16 * hp, 16)

        def query_rows(stat_ref, t_ref, qs):
            tiles = un.load_q(stat_ref, qs)
            for b in range(bsz):
                t_ref[b] = tiles[b].T
            t16 = t_ref[:, pl.ds(row16, 16), :]
            return jnp.concatenate([t16[:, 0:1, :], t16[:, 8:9, :]], axis=2)

        def batch(qs, at_start):
            qq = _stack_heads(un.load_q(q_ref, qs) * 0.125, low)
            dd = _stack_heads(un.load_q(do_ref, qs), low)
            ks = un.load_k(k_ref, qs, at_start)
            vs = un.load_k(v_ref, qs, at_start)
            lrow = query_rows(lse_ref, lt_ref, qs)
            drow = query_rows(ds_ref, dt_ref, qs)
            pt = jnp.exp(_bdot(ks, qq, _BNT) + _batch_bias(un, qs, at_start, b0_ref, b1_ref) - lrow)
            dst = pt * (_bdot(vs, dd, _BNT) - drow)
            un.add_k(dv_ref, qs, _bdot(pt, dd, _BNN), at_start)
            un.add_k(dk_ref, qs, _bdot(dst, qq, _BNN), at_start)
            dq = _bdot(jnp.swapaxes(dst, 1, 2), ks, _BNN)
            un.store_q(out_ref, qs, jnp.where(low, dq[:, :q_n], dq[:, q_n:]) * 0.125, add=prev is not None,
                       lead=(0,))

        batch(first, True)

        def more(j, carry):
            batch(later(j), False)
            return carry

        lax.fori_loop(1, 1 + n_more, more, 0)
        out_ref[1] = dk_ref[...].astype(out_ref.dtype)
        out_ref[2] = dv_ref[...].astype(out_ref.dtype)

    pv = _pattern_view(proj, g_n)
    full = lambda a: pl.BlockSpec(a.shape, lambda r, hp: (0, 0))
    whole = _pattern_spec(g_n, lr, lambda hp: hp, lead=(3,))
    out = pl.pallas_call(
        body, grid=_pattern_grid(g_n),
        in_specs=[pl.BlockSpec(memory_space=pltpu.SMEM),
                  _pattern_spec(g_n, lr, lambda hp: 32 + hp),
                  _pattern_spec(g_n, lr, lambda hp: 40 + hp),
                  _pattern_spec(g_n, lr, lambda hp: 48 + hp),
                  _pattern_spec(g_n, lr, lambda hp: hp),
                  _pattern_spec(g_n, lr, lambda hp: 0),
                  _pattern_spec(g_n, lr, lambda hp: 0),
                  full(d0), full(m0), full(d1), full(m1)] + ([] if prev is None else [whole]),
        out_specs=whole,
        out_shape=jax.ShapeDtypeStruct(_pattern_view_shape(s_n, D_MODEL, g_n, lead=(3,)), ACT_DTYPE),
        scratch_shapes=[pltpu.VMEM((d0.shape[0], 2 * q_n), F32), pltpu.VMEM((2 * q_n, 2 * q_n), F32),
                        pltpu.VMEM((bsz, LANES, q_n), F32), pltpu.VMEM((bsz, LANES, q_n), F32),
                        pltpu.VMEM((gd, lr, LANES), F32), pltpu.VMEM((gd, lr, LANES), F32)],
        name=f"attn_bwd_d{d}",
    )(slopes, pv, pv, pv, _pattern_view(dout, g_n), _pattern_view(lse, g_n), _pattern_view(dsum, g_n),
      d0, m0, d1, m1, *([] if prev is None else [_pattern_view(prev, g_n, lead=(3,))]))
    return out.reshape(3, s_n, D_MODEL)


def _part_index(step, per, lo, n):
    return jnp.clip(step // per - lo, 0, n - 1)


def _dw_in(ut, da4, dc3, db3):
    s_n = ut.shape[1]
    tn = 512
    per = D_MODEL // tn
    shard_blocks = SHARD_COLS // tn

    def body(a_ref, p0_ref, p1_ref, p2_ref, o_ref):
        part = pl.program_id(0) // per

        @pl.when(part < 4)
        def _():
            o_ref[...] = _dot(a_ref[...], p0_ref[...])

        @pl.when((part >= 4) & (part < 7))
        def _():
            o_ref[...] = _dot(a_ref[...], p1_ref[...])

        @pl.when(part >= 7)
        def _():
            o_ref[...] = _dot(a_ref[...], p2_ref[...])

    def pspec(lo, n):
        def index(j):
            part = j // per
            col = jnp.where(part < lo, 0, jnp.where(part >= lo + n, per - 1, j % per))
            return _part_index(j, per, lo, n), 0, col
        return pl.BlockSpec((None, s_n, tn), index)

    return pl.pallas_call(
        body, grid=(IN_COLS // tn,),
        in_specs=[pl.BlockSpec((D_MODEL, s_n), lambda j: (0, 0), pipeline_mode=pl.Buffered(1)),
                  pspec(0, 4), pspec(4, 3), pspec(7, 3)],
        out_specs=pl.BlockSpec((None, D_MODEL, tn), lambda j: (j // shard_blocks, 0, j % shard_blocks)),
        out_shape=jax.ShapeDtypeStruct((4, D_MODEL, SHARD_COLS), F32),
        name="dw_in",
    )(ut, da4, dc3, db3)


def _input_grad(da4, dc3, db3, w4, xp, norm_g, dh2, row0, rows):
    tm, tk = 256, 512
    per = D_MODEL // tk
    shard_blocks = SHARD_COLS // tk
    m0 = row0 // tm

    def body(p0_ref, p1_ref, p2_ref, w_ref, x_ref, g_ref, dh_ref, gx_ref, dg_ref):
        @pl.when(pl.program_id(0) == 0)
        def _():
            dg_ref[...] = jnp.zeros(dg_ref.shape, F32)

        du = None
        for k in range(IN_COLS // tk):
            part, cols = k // per, pl.ds((k % per) * tk, tk)
            ref, slot = (p0_ref, part) if part < 4 else (p1_ref, part - 4) if part < 7 else (p2_ref, part - 7)
            d = _dot_nt(ref[slot, :, cols], w_ref[k // shard_blocks, :, pl.ds((k % shard_blocks) * tk, tk)])
            du = d if du is None else du + d
        x = x_ref[...]
        r = lax.rsqrt(jnp.mean(x * x, axis=-1, keepdims=True) + EPS)
        nrm = x * r
        dg_ref[...] += jnp.sum(du * nrm, axis=0, keepdims=True)
        dn = du * g_ref[...]
        gx_ref[...] = dh_ref[...].astype(F32) + r * (dn - nrm * jnp.mean(dn * nrm, axis=-1, keepdims=True))

    def pspec(n):
        return pl.BlockSpec((n, tm, D_MODEL), lambda m: (0, m0 + m, 0))

    row_in = pl.BlockSpec((tm, D_MODEL), lambda m: (m0 + m, 0))
    vec = pl.BlockSpec((1, D_MODEL), lambda m: (0, 0))
    return pl.pallas_call(
        body, grid=(rows // tm,),
        in_specs=[pspec(4), pspec(3), pspec(3),
                  pl.BlockSpec(w4.shape, lambda m: (0, 0, 0), pipeline_mode=pl.Buffered(1)),
                  row_in, vec, row_in],
        out_specs=[pl.BlockSpec((tm, D_MODEL), lambda m: (m, 0)), vec],
        out_shape=[jax.ShapeDtypeStruct((rows, D_MODEL), F32), jax.ShapeDtypeStruct((1, D_MODEL), F32)],
        name="input_grad",
    )(da4, dc3, db3, w4, xp, norm_g, dh2)


class _Step:
    def __init__(self, x, tgt, norm_g, chip, after=None):
        self.norm_g, self.chip = norm_g, chip
        self.slopes = _alibi_slopes()
        self.xp, self.tp = _to_residue_major(x, tgt, after)
        self.u, self.ut = _rms_in(self.xp, norm_g)

    def project_own(self, w_own):
        self.proj_own = _in_proj(self.u, self.chip, w_own=w_own)

    def project(self, w4, others):
        self.proj_own = _in_proj(self.u, self.chip, w4=w4, partial=self.proj_own, others=others)

    def mixers(self, w4, taps):
        self.w4, self.taps, self.proj = w4, taps, self.proj_own
        self.hc, self.hct = _conv_fwd(self.proj, taps)
        fwd = [_attn_fwd(self.proj, self.slopes, d) for d in PATTERNS]
        self.o, self.lse, self.ha, self.hat = _attn_combine([f[0] for f in fwd], [f[1] for f in fwd], self.proj)

    def merge_and_loss(self, woc, woa, wo, b_merge, final_g):
        self.woc, self.woa, self.wo, self.b_merge = woc, woa, wo, b_merge
        (self.yc, self.ya, self.dh2b, self.d_final_g, self.loss8, self.d_wo) = _merge_loss(
            self.hc, self.ha, woc, woa, wo, self.proj, b_merge, self.xp, final_g, self.tp)

    def out_weight_grads(self):
        (self.dhc, self.dout, self.dsum, self.db3, self.d_bias, d_woc, d_woa) = _merge_bwd(
            self.dh2b, self.wo, self.woc, self.woa, self.yc, self.ya, self.proj, self.b_merge, self.o,
            self.hct, self.hat)
        return d_woc, d_woa, self.d_wo

    def conv_grads(self, after=0.0):
        self.da4, self.d_taps = _conv_bwd(self.proj, self.taps + after, self.dhc)

    def in_weight_grad(self, after=0.0):
        slopes = self.slopes + after
        self.dc3 = None
        for d in PATTERNS:
            self.dc3 = _attn_bwd(self.proj, self.dout, self.lse, self.dsum, slopes, d, prev=self.dc3)
        return _dw_in(self.ut, self.da4, self.dc3, self.db3)

    def input_grad(self, half, after=0.0):
        rows = self.xp.shape[0] // 2
        return _input_grad(self.da4, self.dc3, self.db3, self.w4, self.xp, self.norm_g + after, self.dh2b,
                           half * rows, rows)


def _local_grads(x, tgt, norm_g, w4, b_merge, conv_w, woc, woa, wo, final_g):
    st = _Step(x, tgt, norm_g, jnp.zeros((1,), jnp.int32))
    st.project_own(w4[0])
    st.project(w4, (2, 1))
    st.project(w4, (3,))
    st.mixers(w4, conv_w)
    st.merge_and_loss(woc, woa, wo, b_merge, final_g)
    d_woc, d_woa, d_wo = st.out_weight_grads()
    st.conv_grads()
    d_w4 = st.in_weight_grad()
    gx_lo, dg_lo = st.input_grad(0)
    gx_hi, dg_hi = st.input_grad(1)
    return (st.loss8, _to_natural(gx_lo, gx_hi), dg_lo + dg_hi, d_w4, st.d_bias, st.d_taps, d_woc, d_woa, d_wo,
            st.d_final_g)


MESH = pl.DeviceIdType.MESH
_CHIP_FLIPS = ((1, 0), (0, 1), (1, 1))
_ANY = pl.BlockSpec(memory_space=pl.ANY)


def _place():
    return lax.axis_index("x"), lax.axis_index("y"), lax.axis_index("c")


def _flip(v, f):
    return 1 - v if f else v


def _remote(src, dst, send_sems, recv_sems, k, device):
    return pltpu.make_async_remote_copy(src_ref=src, dst_ref=dst, send_sem=send_sems.at[k], recv_sem=recv_sems.at[k],
                                        device_id=device, device_id_type=MESH)


def _place_shard(w, chip, dtype):
    rows, cols = w.shape
    tm = min(rows, 128)

    def body(chip_ref, w_ref, o_ref):
        o_ref[0] = w_ref[...].astype(o_ref.dtype)

    return pl.pallas_call(
        body,
        grid_spec=pltpu.PrefetchScalarGridSpec(
            num_scalar_prefetch=1, grid=(rows // tm,),
            in_specs=[pl.BlockSpec((tm, cols), lambda i, chip_ref: (i, 0))],
            out_specs=pl.BlockSpec((1, tm, cols), lambda i, chip_ref: (chip_ref[0], i, 0))),
        out_shape=jax.ShapeDtypeStruct((4, rows, cols), dtype),
        name="place_shard",
    )(chip, w)


def _gather_copies_to(flips, whole=()):
    def copies(arrs, _, send_sems, recv_sems):
        x, y, c = _place()
        out = []
        for a, arr in enumerate(arrs):
            h = arr.shape[1] // 2
            mine = arr.at[2 * x + y] if a in whole else arr.at[2 * x + y, pl.ds(pl.multiple_of(c * h, 8), h)]
            for i, t in enumerate(flips):
                fx, fy = _CHIP_FLIPS[t]
                out.append(_remote(mine, mine, send_sems, recv_sems, len(flips) * a + i,
                                   (_flip(x, fx), _flip(y, fy), c)))
        return out
    return copies


def _forward_to_sibling(arrs, flips=(0, 1, 2)):
    n = len(arrs)

    def body(*refs):
        outs = refs[n:2 * n]
        send_sems, recv_sems = refs[2 * n:]
        x, y, c = _place()
        sibling = (x, y, 1 - c)
        started = []
        for a in range(n):
            h = outs[a].shape[1] // 2
            rows = pl.ds(pl.multiple_of(c * h, 8), h)
            for t in flips:
                fx, fy = _CHIP_FLIPS[t]
                landed = outs[a].at[2 * _flip(x, fx) + _flip(y, fy), rows]
                cp = _remote(landed, landed, send_sems, recv_sems, 3 * a + t, sibling)
                cp.start()
                started.append(cp)
        for a in range(n):
            h = outs[a].shape[1] // 2
            rows = pl.ds(pl.multiple_of((1 - c) * h, 8), h)
            for t in flips:
                fx, fy = _CHIP_FLIPS[t]
                handed = outs[a].at[2 * _flip(x, fx) + _flip(y, fy), rows]
                _remote(handed, handed, send_sems, recv_sems, 3 * a + t, sibling).wait_recv()
        for cp in started:
            cp.wait_send()

    return pl.pallas_call(
        body, in_specs=[_ANY] * n, out_specs=[_ANY] * n,
        out_shape=[jax.ShapeDtypeStruct(s.shape, s.dtype) for s in arrs],
        input_output_aliases={a: a for a in range(n)},
        scratch_shapes=[pltpu.SemaphoreType.DMA((3 * n,)), pltpu.SemaphoreType.DMA((3 * n,))],
        name="gathered_to_sibling_" + "".join(str(t) for t in flips),
    )(*arrs)


_HBM = pl.BlockSpec(memory_space=pltpu.HBM)
_SEM = pl.BlockSpec(memory_space=pltpu.SEMAPHORE)
_EFFECT = pltpu.SideEffectType.DATAFLOW_SIDE_EFFECTING


class _SplitExchange:
    def __init__(self, name, srcs, land_shapes, n_copies, copies, riders=()):
        self.name, self.n, self.nl, self.copies = name, len(srcs), len(land_shapes), copies
        n, nb = self.n, len(srcs) + len(land_shapes)
        lands = [lax.empty(s.shape, s.dtype) for s in land_shapes]
        bufs = [pltpu.with_memory_space_constraint(a, pltpu.HBM) for a in (*srcs, *lands, *riders)]
        na = len(bufs)

        def body(*refs):
            send_sems, recv_sems = refs[na], refs[na + 1]
            for cp in copies(refs[:n], refs[n:nb], send_sems, recv_sems):
                cp.start()
            refs[-1][...] = jnp.zeros(refs[-1].shape, F32)

        outs = pl.pallas_call(
            body, name=name + "_start",
            in_specs=[_HBM] * na,
            out_specs=[_SEM, _SEM] + [_HBM] * na + [pl.BlockSpec(memory_space=pltpu.VMEM)],
            out_shape=[pltpu.SemaphoreType.DMA((n_copies,)), pltpu.SemaphoreType.DMA((n_copies,))]
            + [pltpu.HBM(b.shape, b.dtype) for b in bufs] + [jax.ShapeDtypeStruct((8, LANES), F32)],
            input_output_aliases={i: 2 + i for i in range(na)},
            compiler_params=pltpu.CompilerParams(has_side_effects=_EFFECT),
        )(*bufs)
        self.sems, self.bufs, self.riders, self.token = outs[:2], outs[2:2 + nb], outs[2 + nb:2 + na], outs[-1]

    def after(self):
        return self.token[0, 0]

    def wait(self, done, riders=(), bufs=None):
        n, nb, copies = self.n, self.n + self.nl, self.copies
        bufs = [*(self.bufs if bufs is None else bufs),
                *[pltpu.with_memory_space_constraint(a, pltpu.HBM) for a in riders]]
        na = len(bufs)
        done = list(done) if isinstance(done, (list, tuple)) else [done]

        def body(*refs):
            send_sems, recv_sems = refs[na], refs[na + 1]
            for cp in copies(refs[:n], refs[n:nb], send_sems, recv_sems):
                cp.wait_send()
                cp.wait_recv()

        outs = pl.pallas_call(
            body, name=self.name + "_wait",
            in_specs=[_HBM] * na + [_SEM, _SEM] + [_ANY] * len(done),
            out_specs=[_HBM] * na,
            out_shape=[pltpu.HBM(b.shape, b.dtype) for b in bufs],
            input_output_aliases={i: i for i in range(na)},
            compiler_params=pltpu.CompilerParams(has_side_effects=_EFFECT),
        )(*bufs, *self.sems, *done)
        return outs[:n], outs[n:nb], outs[nb:]


def _sibling_copies(srcs, lands, send_sems, recv_sems):
    x, y, c = _place()
    out = []
    for a, (src, land) in enumerate(zip(srcs, lands)):
        h = src.shape[1] // 2
        theirs = pl.ds(pl.multiple_of((1 - c) * h, 8), h)
        out.append(_remote(src.at[:, theirs], land, send_sems, recv_sems, a, (x, y, 1 - c)))
    return out


def _grads_to_sibling(name, grads):
    shapes = [jax.ShapeDtypeStruct((4, g.shape[1] // 2, g.shape[2]), g.dtype) for g in grads]
    return _SplitExchange(name, grads, shapes, len(grads), _sibling_copies)


def _chip_copies(srcs, lands, send_sems, recv_sems):
    x, y, c = _place()
    out = []
    for a, (src, land) in enumerate(zip(srcs, lands)):
        for t, (fx, fy) in enumerate(_CHIP_FLIPS):
            tx, ty = _flip(x, fx), _flip(y, fy)
            out.append(_remote(src.at[2 * tx + ty], land.at[t], send_sems, recv_sems, 3 * a + t, (tx, ty, c)))
    return out


def _grads_to_chips(name, parts):
    shapes = [jax.ShapeDtypeStruct((3, *p.shape[1:]), p.dtype) for p in parts]
    return _SplitExchange(name, parts, shapes, 3 * len(parts), _chip_copies)


def _add_halves(g, r, half):
    _, rows, cols = g.shape
    h = rows // 2
    tm = min(h, 128)
    nt = h // tm

    def body(half_ref, g_ref, r_ref, b_ref):
        b_ref[...] = (g_ref[...] + r_ref[...]).astype(b_ref.dtype)

    spec = pl.BlockSpec((1, tm, cols), lambda j, i, half_ref: (j, i, 0))
    return pl.pallas_call(
        body,
        grid_spec=pltpu.PrefetchScalarGridSpec(
            num_scalar_prefetch=1, grid=(4, nt),
            in_specs=[pl.BlockSpec((1, tm, cols), lambda j, i, half_ref: (j, half_ref[0] * nt + i, 0)), spec],
            out_specs=spec),
        out_shape=jax.ShapeDtypeStruct((4, h, cols), BF16),
        name="add_sibling_grads",
    )(half, g, r)


def _add_chips(g, r, recv, where):
    _, h, cols = r.shape
    tm = min(h, 128)
    nt = h // tm

    def body(where_ref, g_ref, r_ref, recv_ref, out_ref):
        own = g_ref[0] + r_ref[0]
        out_ref[...] = ((own + recv_ref[0].astype(F32)) + recv_ref[1].astype(F32)) + recv_ref[2].astype(F32)

    return pl.pallas_call(
        body,
        grid_spec=pltpu.PrefetchScalarGridSpec(
            num_scalar_prefetch=1, grid=(nt,),
            in_specs=[pl.BlockSpec((1, tm, cols), lambda i, w: (w[0], w[1] * nt + i, 0)),
                      pl.BlockSpec((1, tm, cols), lambda i, w: (w[0], i, 0)),
                      pl.BlockSpec((3, tm, cols), lambda i, w: (0, i, 0))],
            out_specs=pl.BlockSpec((tm, cols), lambda i, w: (w[1] * nt + i, 0))),
        out_shape=jax.ShapeDtypeStruct((2 * h, cols), F32),
        name="add_chip_grads",
    )(where, g, r, recv)


def _share_halves(shards):
    n = len(shards)

    def body(*refs):
        outs = refs[n:2 * n]
        send_sems, recv_sems = refs[2 * n:]
        x, y, c = _place()
        copies = []
        for a in range(n):
            h = outs[a].shape[0] // 2
            mine = outs[a].at[pl.ds(pl.multiple_of(c * h, 8), h)]
            copies.append(_remote(mine, mine, send_sems, recv_sems, a, (x, y, 1 - c)))
        for cp in copies:
            cp.start()
        for a, cp in enumerate(copies):
            cp.wait_send()
            h = outs[a].shape[0] // 2
            theirs = outs[a].at[pl.ds(pl.multiple_of((1 - c) * h, 8), h)]
            _remote(theirs, theirs, send_sems, recv_sems, a, (x, y, 1 - c)).wait_recv()

    return pl.pallas_call(
        body, in_specs=[_ANY] * n, out_specs=[_ANY] * n,
        out_shape=[jax.ShapeDtypeStruct(p.shape, p.dtype) for p in shards],
        input_output_aliases={a: a for a in range(n)},
        scratch_shapes=[pltpu.SemaphoreType.DMA((n,)), pltpu.SemaphoreType.DMA((n,))],
        name="share_reduced_halves",
    )(*shards)


def _reduce_small(rows):
    cols = rows[0].shape[1]
    n = len(rows)
    assert sum(r.shape[0] for r in rows) <= 8

    def body(*refs):
        ins, out_ref = refs[:n], refs[n]
        vec_ref, gath_ref, send_sems, recv_sems = refs[n + 1:]
        x, y, c = _place()
        me = 4 * x + 2 * y + c
        vec_ref[...] = jnp.zeros(vec_ref.shape, F32)
        at = 0
        for r in ins:
            vec_ref[at:at + r.shape[0], :] = r[...]
            at += r.shape[0]
        copies = []
        for k in range(1, 8):
            peer = (_flip(x, (k >> 2) & 1), _flip(y, (k >> 1) & 1), _flip(c, k & 1))
            copies.append(_remote(vec_ref, gath_ref.at[me], send_sems, recv_sems, k - 1, peer))
        for cp in copies:
            cp.start()
        gath_ref[me] = vec_ref[...]
        for cp in copies:
            cp.wait()
        tot = gath_ref[0]
        for dev in range(1, 8):
            tot = tot + gath_ref[dev]
        out_ref[...] = tot
        out_ref[7:8, :] = jnp.zeros((1, cols), F32) + jnp.sum(tot[7:8, :])

    vm = pl.BlockSpec(memory_space=pltpu.VMEM)
    return pl.pallas_call(
        body, in_specs=[vm] * n, out_specs=vm,
        out_shape=jax.ShapeDtypeStruct((8, cols), F32),
        scratch_shapes=[pltpu.VMEM((8, cols), F32), pltpu.VMEM((8, 8, cols), F32),
                        pltpu.SemaphoreType.DMA((7,)), pltpu.SemaphoreType.DMA((7,))],
        name="reduce_small",
    )(*rows)


def _adamw(w, g, m, v, name):
    rows, cols = w.shape
    tm = 128 if rows % 128 == 0 else rows

    def body(w_ref, g_ref, m_ref, v_ref, d_ref, m2_ref, v2_ref, gout_ref):
        gr = g_ref[...]
        m2 = ADAM_B1 * m_ref[...] + (1.0 - ADAM_B1) * gr
        v2 = ADAM_B2 * v_ref[...] + (1.0 - ADAM_B2) * (gr * gr)
        m_hat = m2 / (1.0 - ADAM_B1 ** ADAM_STEP)
        v_hat = v2 / (1.0 - ADAM_B2 ** ADAM_STEP)
        d_ref[...] = -ADAM_LR * (m_hat / (jnp.sqrt(v_hat) + ADAM_EPS) + ADAM_WD * w_ref[...])
        m2_ref[...] = m2
        v2_ref[...] = v2
        gout_ref[...] = gr

    spec = pl.BlockSpec((tm, cols), lambda i: (i, 0))
    sds = jax.ShapeDtypeStruct((rows, cols), F32)
    return pl.pallas_call(body, grid=(rows // tm,), in_specs=[spec] * 4, out_specs=[spec] * 4,
                          out_shape=[sds] * 4, name=name)(w, g, m, v)


def kernel(x, norm_g, w_in, b_merge, conv_w, w_out_conv, w_out_attn, w_o, final_g, loss_target, m_norm_g, m_w_in, m_b_merge, m_conv_w, m_w_out_conv, m_w_out_attn, m_w_o, m_final_g, v_norm_g, v_w_in, v_b_merge, v_conv_w, v_w_out_conv, v_w_out_attn, v_w_o, v_final_g):
    mx, my, mc = _place()
    chip = (2 * mx + my).astype(jnp.int32)
    seq = x.shape[1]

    chip1 = chip.reshape(1)
    slots = [_place_shard(w[0], chip1, MXU_DTYPE) for w in (w_in, w_out_conv, w_out_attn, w_o)]
    taps_slot = _place_shard(jnp.pad(conv_w[0], ((0, 5), (0, 0))), chip1, F32)
    gather_near = _SplitExchange("gather_w_in_near", [slots[0], taps_slot], [], 4,
                                 _gather_copies_to((0, 1), whole=(1,)))
    st = _Step(x[0], loss_target[0], norm_g, chip1, after=gather_near.token)
    st.project_own(w_in[0])
    near, _, _ = gather_near.wait([st.ut, st.proj_own])
    gather_far = _SplitExchange("gather_w_in_far", near, [], 2, _gather_copies_to((2,), whole=(1,)))
    (w4,) = _forward_to_sibling(gather_far.bufs[:1], flips=(0, 1))
    st.project(w4, (2, 1))
    (w4, taps4), _, out_slots = gather_far.wait([st.proj_own], riders=slots[1:], bufs=[w4, gather_far.bufs[1]])
    gather_out = _SplitExchange("gather_w_out", out_slots, [], 9, _gather_copies_to((0, 1, 2)), riders=[w4])
    (w4,) = _forward_to_sibling(gather_out.riders, flips=(2,))
    st.project(w4, (3,))
    st.mixers(w4, jnp.concatenate([taps4[j, :3, :] for j in range(4)], axis=1))
    out_ws, _, _ = gather_out.wait(st.o)
    woc, woa, wo = [w.reshape(D_MODEL, D_MODEL) for w in _forward_to_sibling(out_ws)]
    st.merge_and_loss(woc, woa, wo, b_merge, final_g.reshape(1, D_MODEL))

    half = mc.astype(jnp.int32).reshape(1)
    where = jnp.stack([chip, mc.astype(jnp.int32)])
    out_grads = [g.reshape(4, -1, D_MODEL) for g in st.out_weight_grads()]
    to_sibling = _grads_to_sibling("out_grads_to_sibling", out_grads)
    st.conv_grads(after=to_sibling.after())
    out_grads, out_from_sibling, _ = to_sibling.wait(st.da4)
    to_chips = _grads_to_chips("out_grads_to_chips",
                               [_add_halves(g, r, half) for g, r in zip(out_grads, out_from_sibling)])
    d_w4 = st.in_weight_grad(after=to_chips.after())
    out_from_chips = to_chips.wait(st.dc3)[1]

    to_sibling = _grads_to_sibling("in_grad_to_sibling", [d_w4])
    gx_lo, dg_lo = st.input_grad(0, after=to_sibling.after())
    (d_w4,), (from_sibling,), _ = to_sibling.wait(gx_lo)
    to_chips = _grads_to_chips("in_grad_to_chips", [_add_halves(d_w4, from_sibling, half)])
    gx_hi, dg_hi = st.input_grad(1, after=to_chips.after())
    grad_x = _to_natural(gx_lo, gx_hi)

    where_late = where + to_chips.after().astype(jnp.int32)
    out_reduced = [_add_chips(g, r, recv, where_late)
                   for g, r, recv in zip(out_grads, out_from_sibling, out_from_chips)]
    g_woc, g_woa, g_wo = _share_halves(out_reduced)
    small = _reduce_small([dg_lo + dg_hi, st.d_bias.reshape(2, D_MODEL), st.d_taps, st.d_final_g,
                           st.loss8.reshape(1, D_MODEL)])
    loss = (0.5 / D_MODEL) * small[7, 0]
    g_taps = lax.dynamic_slice(small[3:6], (0, chip * (D_MODEL // 4)), (3, D_MODEL // 4))
    upd = {
        "norm_g": _adamw(norm_g, small[0:1], m_norm_g, v_norm_g, "adamw_norm_g"),
        "b_merge": _adamw(b_merge, small[1:3].reshape(1, 2 * D_MODEL), m_b_merge, v_b_merge, "adamw_b_merge"),
        "conv_w": _adamw(conv_w[0], g_taps, m_conv_w[0], v_conv_w[0], "adamw_conv_w"),
        "w_out_conv": _adamw(w_out_conv[0], g_woc, m_w_out_conv[0], v_w_out_conv[0], "adamw_w_out_conv"),
        "w_out_attn": _adamw(w_out_attn[0], g_woa, m_w_out_attn[0], v_w_out_attn[0], "adamw_w_out_attn"),
        "w_o": _adamw(w_o[0], g_wo, m_w_o[0], v_w_o[0], "adamw_w_o"),
        "final_g": _adamw(final_g.reshape(1, D_MODEL), small[6:7], m_final_g.reshape(1, D_MODEL),
                          v_final_g.reshape(1, D_MODEL), "adamw_final_g"),
    }
    behind = [grad_x] + [u[0] for u in upd.values()]
    in_reduced = _add_chips(d_w4, from_sibling, to_chips.wait(behind)[1][0], where)
    (g_w_in,) = _share_halves([in_reduced])
    upd["w_in"] = _adamw(w_in[0], g_w_in, m_w_in[0], v_w_in[0], "adamw_w_in")

    names = ["norm_g", "w_in", "b_merge", "conv_w", "w_out_conv", "w_out_attn", "w_o", "final_g"]
    shapes = [norm_g.shape, w_in.shape, b_merge.shape, conv_w.shape, w_out_conv.shape, w_out_attn.shape,
              w_o.shape, final_g.shape]
    outs = [loss, grad_x.reshape(1, seq, D_MODEL)]
    for k in (3, 0, 1, 2):
        outs += [upd[n][k].reshape(s) for n, s in zip(names, shapes)]
    return tuple(outs)
```

```python
import functools

import numpy as np
import jax
import jax.numpy as jnp
from jax import lax
from jax.experimental import pallas as pl
from jax.experimental.pallas import tpu as pltpu

F32 = jnp.float32
BF16 = jnp.bfloat16
MXU_DTYPE = jnp.bfloat16
ACT_DTYPE = jnp.bfloat16

D_MODEL = 1024
N_HEADS = 16
HEAD_DIM = 64
QB = 128
N_RES = 16
LANES = 128
HP = N_HEADS * HEAD_DIM // LANES
IN_COLS = 10 * D_MODEL
SHARD_COLS = IN_COLS // 4
EPS = 1e-6
NEG = -1e30

ADAM_LR, ADAM_B1, ADAM_B2, ADAM_EPS, ADAM_WD, ADAM_STEP = 0.001, 0.9, 0.999, 1e-08, 0.01, 10

PATTERNS = {1: (16, 8), 4: (4, 32), 16: (1, 128)}

_NN = (((1,), (0,)), ((), ()))
_NT = (((1,), (1,)), ((), ()))


def _dot(a, b):
    return lax.dot_general(a.astype(MXU_DTYPE), b.astype(MXU_DTYPE), _NN, preferred_element_type=F32)


def _dot_nt(a, b):
    return lax.dot_general(a.astype(MXU_DTYPE), b.astype(MXU_DTYPE), _NT, preferred_element_type=F32)


def _split3(x):
    hi = x.astype(BF16)
    r1 = x - hi.astype(F32)
    mid = r1.astype(BF16)
    lo = (r1 - mid.astype(F32)).astype(BF16)
    return hi, mid, lo


def _select_cols(x, sel, terms):
    return sum(lax.dot_general(t, sel, _NN, preferred_element_type=F32) for t in _split3(x)[:terms])


def _sigmoid(z):
    return 1.0 / (1.0 + jnp.exp(-z))


def _head_expand_matrix():
    e = np.zeros((LANES, D_MODEL), np.float32)
    for h in range(N_HEADS):
        e[8 * h, HEAD_DIM * h:HEAD_DIM * (h + 1)] = 1.0
    return jnp.asarray(e, BF16)


def _head_sum_matrix():
    e = np.zeros((D_MODEL, LANES), np.float32)
    for h in range(N_HEADS):
        e[HEAD_DIM * h:HEAD_DIM * (h + 1), 8 * h:8 * (h + 1)] = 1.0
    return jnp.asarray(e, BF16)


def _attn_tables(d):
    g_n, rq = PATTERNS[d]
    q_n = g_n * rq
    gq, iq = np.arange(q_n) // rq, np.arange(q_n) % rq

    def tab(kn, base):
        k_n = g_n * kn
        gk, jk = np.arange(k_n) // kn, np.arange(k_n) % kn
        delta = g_n * (base + iq[:, None] - jk[None, :]) + gq[:, None] - gk[None, :]
        valid = (delta >= 0) & (delta <= QB)
        dist = np.where(valid, d * delta, 0).astype(np.float32)
        madd = np.where(valid, 0.0, NEG).astype(np.float32)
        return dist, madd

    d0, m0 = tab(rq if g_n == 1 else 2 * rq, 0)
    d1, m1 = tab(2 * rq, rq)
    return d0, m0, d1, m1


def _alibi_slopes():
    return jnp.exp2(-8.0 * jnp.arange(1, N_HEADS + 1, dtype=F32) / N_HEADS)


def _to_residue_major(x, tgt, after=None):
    s_n, c_n = x.shape
    lr = s_n // N_RES
    extra = [] if after is None else [after]

    def body(x_ref, t_ref, *rest):
        xo_ref, to_ref = rest[-2:]
        for r in range(N_RES):
            xo_ref[r] = x_ref[pl.ds(r, lr, stride=N_RES), :]
            to_ref[r] = t_ref[pl.ds(r, lr, stride=N_RES), :]

    nat = pl.BlockSpec((s_n, LANES), lambda j: (0, j))
    res = pl.BlockSpec((N_RES, lr, LANES), lambda j: (0, 0, j))
    xo, to = pl.pallas_call(
        body, grid=(c_n // LANES,),
        in_specs=[nat, nat] + [pl.BlockSpec((8, LANES), lambda j: (0, 0))] * len(extra),
        out_specs=[res, res],
        out_shape=[jax.ShapeDtypeStruct((N_RES, lr, c_n), F32)] * 2,
        name="perm_in",
    )(x, tgt, *extra)
    return xo.reshape(s_n, c_n), to.reshape(s_n, c_n)


def _to_natural(gx_lo, gx_hi):
    half_rows, c_n = gx_lo.shape
    lr = half_rows // (N_RES // 2)

    def body(lo_ref, hi_ref, o_ref):
        for r in range(N_RES):
            o_ref[pl.ds(r, lr, stride=N_RES), :] = lo_ref[r] if r < N_RES // 2 else hi_ref[r - N_RES // 2]

    half = pl.BlockSpec((N_RES // 2, lr, LANES), lambda j: (0, 0, j))
    return pl.pallas_call(
        body, grid=(c_n // LANES,),
        in_specs=[half, half],
        out_specs=pl.BlockSpec((2 * half_rows, LANES), lambda j: (0, j)),
        out_shape=jax.ShapeDtypeStruct((2 * half_rows, c_n), F32),
        name="perm_out",
    )(gx_lo.reshape(N_RES // 2, lr, c_n), gx_hi.reshape(N_RES // 2, lr, c_n))


def _rms_in(xp, norm_g):
    s_n, c_n = xp.shape
    tm = 512

    def body(x_ref, g_ref, u_ref, ut_ref):
        x = x_ref[...]
        r = lax.rsqrt(jnp.mean(x * x, axis=-1, keepdims=True) + EPS)
        u = x * r * g_ref[...]
        u_ref[...] = u.astype(u_ref.dtype)
        ut_ref[...] = u.T.astype(ut_ref.dtype)

    return pl.pallas_call(
        body, grid=(s_n // tm,),
        in_specs=[pl.BlockSpec((tm, c_n), lambda i: (i, 0)), pl.BlockSpec((1, c_n), lambda i: (0, 0))],
        out_specs=[pl.BlockSpec((tm, c_n), lambda i: (i, 0)), pl.BlockSpec((c_n, tm), lambda i: (0, i))],
        out_shape=[jax.ShapeDtypeStruct((s_n, c_n), ACT_DTYPE), jax.ShapeDtypeStruct((c_n, s_n), ACT_DTYPE)],
        name="rms_in",
    )(xp, norm_g)


def _in_proj(u, chip, w_own=None, w4=None, partial=None, others=()):
    s_n = u.shape[0]
    tn, cm = 512, 512
    per = SHARD_COLS // tn
    own = partial is None

    def body(chip_ref, a_hbm, b_ref, *rest):
        o_ref, a_ref, sems = rest[-3:]
        first = pl.program_id(0) == 0
        chunks = [pl.ds(c * cm, cm) for c in range(s_n // cm)]
        loads = [pltpu.make_async_copy(a_hbm.at[rows], a_ref.at[rows], sems.at[c]) for c, rows in enumerate(chunks)]

        @pl.when(first)
        def _():
            for cp in loads:
                cp.start()

        b = b_ref[...]
        for rows, cp in zip(chunks, loads):
            pl.when(first)(cp.wait)
            o_ref[rows, :] = _dot(a_ref[rows, :], b).astype(o_ref.dtype)

    def shard(n, chip_ref):
        if own:
            return chip_ref[0]
        mask = others[-1]
        for i, m in enumerate(others[:-1]):
            mask = jnp.where(n // per == i, m, mask)
        return jnp.bitwise_xor(chip_ref[0], mask)

    w_spec = (pl.BlockSpec((D_MODEL, tn), lambda n, c: (0, n)) if own else
              pl.BlockSpec((None, D_MODEL, tn), lambda n, c: (shard(n, c), 0, n % per)))
    return pl.pallas_call(
        body,
        grid_spec=pltpu.PrefetchScalarGridSpec(
            num_scalar_prefetch=1, grid=(per if own else len(others) * per,),
            in_specs=[_ANY, w_spec] + ([] if own else [_ANY]),
            out_specs=pl.BlockSpec((s_n, tn), lambda n, c: (0, shard(n, c) * per + n % per)),
            scratch_shapes=[pltpu.VMEM((s_n, D_MODEL), u.dtype), pltpu.SemaphoreType.DMA((s_n // cm,))]),
        out_shape=jax.ShapeDtypeStruct((s_n, IN_COLS), ACT_DTYPE),
        input_output_aliases={} if own else {3: 0},
        name="in_proj_own" if own else "in_proj_" + "_".join(str(m) for m in others),
    )(*([chip, u, w_own] if own else [chip, u, w4, partial]))


def _conv_terms(xc_ref, cg_ref, r, row, lr, cache):
    def a_of(q):
        if q not in cache:
            cache[q] = cg_ref[q].astype(F32) * xc_ref[q].astype(F32)
        return cache[q]

    def shift_down(v):
        return jnp.where(row >= 1, pltpu.roll(v, 1, 0), 0.0)

    a = a_of(r)
    am1 = a_of(r - 1) if r >= 1 else shift_down(a_of(N_RES - 1))
    am2 = a_of(r - 2) if r >= 2 else shift_down(a_of(N_RES - 2 + r))
    return a, am1, am2


def _conv_fwd(proj, conv_w):
    s_n = proj.shape[0]
    lr = s_n // N_RES
    pv = proj.reshape(N_RES, lr, IN_COLS)

    def body(xc_ref, bg_ref, cg_ref, zc_ref, w_ref, hc_ref, hct_ref):
        w = w_ref[...]
        row = lax.broadcasted_iota(jnp.int32, (lr, LANES), 0)
        products = {}
        for r in range(N_RES):
            a, am1, am2 = _conv_terms(xc_ref, cg_ref, r, row, lr, products)
            c = w[0:1] * am2 + w[1:2] * am1 + w[2:3] * a
            z = zc_ref[r].astype(F32)
            hc = z * _sigmoid(z) * bg_ref[r].astype(F32) * c
            hc_ref[r] = hc.astype(hc_ref.dtype)
            hct_ref[:, r * lr:(r + 1) * lr] = hc.T.astype(hct_ref.dtype)

    def col(part):
        return pl.BlockSpec((N_RES, lr, LANES), lambda j: (0, 0, part * 8 + j))

    hc, hct = pl.pallas_call(
        body, grid=(D_MODEL // LANES,),
        in_specs=[col(0), col(1), col(2), col(3), pl.BlockSpec((3, LANES), lambda j: (0, j))],
        out_specs=[pl.BlockSpec((N_RES, lr, LANES), lambda j: (0, 0, j)),
                   pl.BlockSpec((LANES, s_n), lambda j: (j, 0))],
        out_shape=[jax.ShapeDtypeStruct((N_RES, lr, D_MODEL), ACT_DTYPE),
                   jax.ShapeDtypeStruct((D_MODEL, s_n), ACT_DTYPE)],
        name="conv_fwd",
    )(pv, pv, pv, pv, conv_w)
    return hc.reshape(s_n, D_MODEL), hct


RES_PER_STEP = 8
ATTN_BATCH = 8

_BNT = (((2,), (2,)), ((0,), (0,)))
_BNN = (((2,), (1,)), ((0,), (0,)))


def _bdot(a, b, dims):
    return lax.dot_general(a.astype(MXU_DTYPE), b.astype(MXU_DTYPE), dims, preferred_element_type=F32)


def _pattern_view_shape(s_n, c_n, g_n, lead=()):
    lr = s_n // N_RES
    return (*lead, 4, 4, lr, c_n) if g_n == 4 else (*lead, N_RES, lr, c_n)


def _pattern_view(a, g_n, lead=()):
    return a.reshape(_pattern_view_shape(a.shape[-2], a.shape[-1], g_n, lead))


def _pattern_grid(g_n):
    return (N_RES // RES_PER_STEP if g_n == 1 else N_RES // g_n, HP)


def _pattern_spec(g_n, lr, col_of_hp, lead=()):
    z = (0,) * len(lead)
    if g_n == 16:
        return pl.BlockSpec((*lead, 16, lr, LANES), lambda r, hp: (*z, 0, 0, col_of_hp(hp)))
    if g_n == 4:
        return pl.BlockSpec((*lead, 4, None, lr, LANES), lambda r, hp: (*z, 0, r, 0, col_of_hp(hp)))
    return pl.BlockSpec((*lead, RES_PER_STEP, lr, LANES), lambda r, hp: (*z, r, 0, col_of_hp(hp)))


def _aligned(start, m):
    return start if isinstance(start, int) else pl.multiple_of(start, m)


class _Units:
    def __init__(self, g_n, rq):
        self.g_n, self.rq = g_n, rq
        self.per_res, self.paired = g_n == 1, rq == 8

    def plan(self, lr, size):
        if self.per_res:
            return [0], lr // self.rq - 1, lambda j: [pl.multiple_of(j * self.rq, self.rq)]
        step = 16 if self.paired else self.rq
        per = min(size // 2 if self.paired else size, lr // step)
        assert (lr // step) % per == 0
        return ([i * step for i in range(per)], lr // step // per - 1,
                lambda j: [pl.multiple_of((j * per + i) * step, step) for i in range(per)])

    def count(self, qs):
        return RES_PER_STEP if self.per_res else len(qs) * (2 if self.paired else 1)

    def _split(self, tiles, lo, rows):
        return tiles[:, lo:lo + rows].reshape(self.g_n * rows, LANES)

    def load_q(self, ref, qs):
        rq = self.rq
        if self.per_res:
            return ref[:, pl.ds(qs[0], rq), :]
        if self.paired:
            tiles = [ref[:, pl.ds(q, 16), :].astype(F32) for q in qs]
            return jnp.stack([self._split(t, lo, 8) for t in tiles for lo in (0, 8)])
        return jnp.stack([ref[:, pl.ds(q, rq), :].reshape(self.g_n * rq, LANES) for q in qs])

    def _key_rows(self, q, at_start):
        return (0, 2 * self.rq) if at_start else (_aligned(q - self.rq, self.rq), 2 * self.rq)

    def load_k(self, ref, qs, first):
        rq = self.rq
        if self.per_res:
            return ref[:, pl.ds(0, rq), :] if first else ref[:, pl.ds(_aligned(qs[0] - rq, rq), 2 * rq), :]
        if self.paired:
            out = []
            for i, q in enumerate(qs):
                if first and i == 0:
                    t = ref[:, 0:16, :].astype(F32)
                    out += [self._split(t, 0, 16)] * 2
                else:
                    t = ref[:, pl.ds(_aligned(q - 16, 16), 32), :].astype(F32)
                    out += [self._split(t, 8, 16), self._split(t, 16, 16)]
            return jnp.stack(out)
        rows = [self._key_rows(q, first and i == 0) for i, q in enumerate(qs)]
        return jnp.stack([ref[:, pl.ds(k0, n), :].reshape(self.g_n * n, LANES) for k0, n in rows])

    def store_q(self, ref, qs, val, add=False, lead=()):
        if self.per_res:
            pieces = [(qs[0], self.rq, val)]
        elif self.paired:
            pieces = [(q, 16, jnp.concatenate([val[2 * i].reshape(self.g_n, 8, LANES),
                                               val[2 * i + 1].reshape(self.g_n, 8, LANES)], axis=1))
                      for i, q in enumerate(qs)]
        else:
            pieces = [(q, self.rq, val[i].reshape(self.g_n, self.rq, LANES)) for i, q in enumerate(qs)]
        for start, rows, v in pieces:
            idx = (*lead, slice(None), pl.ds(start, rows), slice(None))
            ref[idx] = (ref[idx] + v if add else v).astype(ref.dtype)

    def add_k(self, ref, qs, val, first):
        rq = self.rq
        if self.per_res:
            k0, n = (0, rq) if first else (_aligned(qs[0] - rq, rq), 2 * rq)
            ref[:, pl.ds(k0, n), :] += val
            return
        if self.paired:
            starts = [s for i, q in enumerate(qs)
                      for s in ((0, 0) if first and i == 0 else (_aligned(q - 8, 8), q))]
            rows = [(s, 16) for s in starts]
        else:
            rows = [self._key_rows(q, first and i == 0) for i, q in enumerate(qs)]
        for b, (k0, n) in enumerate(rows):
            ref[:, pl.ds(k0, n), :] += val[b].reshape(self.g_n, n, LANES)


def _batch_bias(un, qs, at_start, first_ref, general_ref):
    if not at_start:
        return general_ref[...][None]
    if un.per_res:
        return first_ref[...][None]
    return jnp.concatenate([first_ref[...][None]] + [general_ref[...][None]] * (un.count(qs) - 1), axis=0)


def _stack_heads(x, low):
    zero = jnp.zeros_like(x)
    return jnp.concatenate([jnp.where(low, x, zero), jnp.where(low, zero, x)], axis=1)


def _attn_fwd(proj, slopes, d):
    g_n, rq = PATTERNS[d]
    un = _Units(g_n, rq)
    s_n = proj.shape[0]
    lr = s_n // N_RES
    q_n = g_n * rq
    d0, m0, d1, m1 = _attn_tables(d)
    first, n_more, later = un.plan(lr, ATTN_BATCH)

    def body(sl_ref, q_ref, k_ref, v_ref, d0_ref, m0_ref, d1_ref, m1_ref, o_ref, lse_ref, b0_ref, b1_ref):
        hp = pl.program_id(1)

        @pl.when(hp == 0)
        def _():
            lse_ref[...] = jnp.zeros(lse_ref.shape, F32)

        for h in (0, 1):
            slope = sl_ref[2 * hp + h]
            b0_ref[h * q_n:(h + 1) * q_n, :] = m0_ref[...] - slope * d0_ref[...]
            b1_ref[h * q_n:(h + 1) * q_n, :] = m1_ref[...] - slope * d1_ref[...]

        lane = lax.broadcasted_iota(jnp.int32, (1, q_n, LANES), 2)
        low = lane < HEAD_DIM
        grp = lane // 8

        def batch(qs, at_start):
            qq = _stack_heads(un.load_q(q_ref, qs) * 0.125, low)
            s = _bdot(qq, un.load_k(k_ref, qs, at_start), _BNT) + _batch_bias(un, qs, at_start, b0_ref, b1_ref)
            m = jnp.max(s, axis=2, keepdims=True)
            p = jnp.exp(s - m)
            l = jnp.sum(p, axis=2, keepdims=True)
            o = _bdot(p, un.load_k(v_ref, qs, at_start), _BNN) * (1.0 / l)
            lse = m + jnp.log(l)
            un.store_q(o_ref, qs, jnp.where(low, o[:, :q_n], o[:, q_n:]))
            upd = jnp.where(grp == 2 * hp, lse[:, :q_n], 0.0) + jnp.where(grp == 2 * hp + 1, lse[:, q_n:], 0.0)
            un.store_q(lse_ref, qs, upd, add=True)

        batch(first, True)

        def more(j, carry):
            batch(later(j), False)
            return carry

        lax.fori_loop(1, 1 + n_more, more, 0)

    pv = _pattern_view(proj, g_n)
    full = lambda a: pl.BlockSpec(a.shape, lambda r, hp: (0, 0))
    o, lse = pl.pallas_call(
        body, grid=_pattern_grid(g_n),
        in_specs=[pl.BlockSpec(memory_space=pltpu.SMEM),
                  _pattern_spec(g_n, lr, lambda hp: 32 + hp),
                  _pattern_spec(g_n, lr, lambda hp: 40 + hp),
                  _pattern_spec(g_n, lr, lambda hp: 48 + hp),
                  full(d0), full(m0), full(d1), full(m1)],
        out_specs=[_pattern_spec(g_n, lr, lambda hp: hp), _pattern_spec(g_n, lr, lambda hp: 0)],
        out_shape=[jax.ShapeDtypeStruct(_pattern_view_shape(s_n, D_MODEL, g_n), ACT_DTYPE),
                   jax.ShapeDtypeStruct(_pattern_view_shape(s_n, LANES, g_n), F32)],
        scratch_shapes=[pltpu.VMEM((2 * q_n, d0.shape[1]), F32), pltpu.VMEM((2 * q_n, 2 * q_n), F32)],
        name=f"attn_fwd_d{d}",
    )(slopes, pv, pv, pv, d0, m0, d1, m1)
    return o.reshape(s_n, D_MODEL), lse.reshape(s_n, LANES)


def _attn_combine(outs, lses, proj):
    s_n = proj.shape[0]
    tm = 512

    def body(o1_ref, o2_ref, o3_ref, l1_ref, l2_ref, l3_ref, za_ref, e_ref, o_ref, lse_ref, ha_ref, hat_ref):
        ls = [l1_ref[...], l2_ref[...], l3_ref[...]]
        mx = jnp.maximum(jnp.maximum(ls[0], ls[1]), ls[2])
        den = sum(jnp.exp(l - mx) for l in ls)
        lse = mx + jnp.log(den)
        lse_ref[...] = lse
        o = jnp.zeros((tm, D_MODEL), F32)
        for l, oref in zip(ls, (o1_ref, o2_ref, o3_ref)):
            o = o + _select_cols(jnp.exp(l - lse), e_ref[...], terms=2) * oref[...].astype(F32)
        o_ref[...] = o.astype(o_ref.dtype)
        z = za_ref[...].astype(F32)
        ha = z * _sigmoid(z) * o
        ha_ref[...] = ha.astype(ha_ref.dtype)
        hat_ref[...] = ha.T.astype(hat_ref.dtype)

    row = lambda w: pl.BlockSpec((tm, w), lambda i: (i, 0))
    return pl.pallas_call(
        body, grid=(s_n // tm,),
        in_specs=[row(D_MODEL)] * 3 + [row(LANES)] * 3
        + [pl.BlockSpec((tm, D_MODEL), lambda i: (i, 7)), pl.BlockSpec((LANES, D_MODEL), lambda i: (0, 0))],
        out_specs=[row(D_MODEL), row(LANES), row(D_MODEL), pl.BlockSpec((D_MODEL, tm), lambda i: (0, i))],
        out_shape=[jax.ShapeDtypeStruct((s_n, D_MODEL), ACT_DTYPE), jax.ShapeDtypeStruct((s_n, LANES), F32),
                   jax.ShapeDtypeStruct((s_n, D_MODEL), ACT_DTYPE), jax.ShapeDtypeStruct((D_MODEL, s_n), ACT_DTYPE)],
        name="attn_combine",
    )(*outs, *lses, proj, _head_expand_matrix())


CHAIN_ROWS = 256


def _row_chains(tm):
    return [slice(r, r + CHAIN_ROWS) for r in range(0, tm, CHAIN_ROWS)]


def _gates(gc_ref, ga_ref, b_ref, rows):
    b = b_ref[...]
    gc = _sigmoid(gc_ref[rows, :].astype(F32) + b[:, :D_MODEL])
    ga = _sigmoid(ga_ref[rows, :].astype(F32) + b[:, D_MODEL:])
    return gc, ga


def _merge_loss(hc, ha, woc, woa, wo, proj, b_merge, xp, final_g, tgt):
    s_n = xp.shape[0]
    tm = 512

    def body(hc_ref, ha_ref, woc_ref, woa_ref, wo_ref, gc_ref, ga_ref, b_ref, x_ref, gf_ref, t_ref,
             yc_ref, ya_ref, dhb_ref, dgf_ref, loss_ref, dwo_ref, mgt_ref):
        i = pl.program_id(0)

        @pl.when(i == 0)
        def _():
            dgf_ref[...] = jnp.zeros(dgf_ref.shape, F32)
            loss_ref[...] = jnp.zeros(loss_ref.shape, F32)
            dwo_ref[...] = jnp.zeros(dwo_ref.shape, F32)

        gf = gf_ref[...]
        for rows in _row_chains(tm):
            yc = _dot(hc_ref[rows, :], woc_ref[...])
            ya = _dot(ha_ref[rows, :], woa_ref[...])
            gc, ga = _gates(gc_ref, ga_ref, b_ref, rows)
            mg = gc * yc + ga * ya
            yc_ref[rows, :] = yc.astype(yc_ref.dtype)
            ya_ref[rows, :] = ya.astype(ya_ref.dtype)
            mgt_ref[:, rows] = mg.T.astype(mgt_ref.dtype)
            h2 = x_ref[rows, :] + _dot(mg, wo_ref[...])
            r2 = lax.rsqrt(jnp.mean(h2 * h2, axis=-1, keepdims=True) + EPS)
            nrm = h2 * r2
            err = nrm * gf - t_ref[rows, :]
            e2 = (err * err).reshape(-1, 8, D_MODEL).sum(axis=0)
            loss_ref[...] += sum(e2[:, c * LANES:(c + 1) * LANES] for c in range(D_MODEL // LANES))
            dy = err * (1.0 / D_MODEL)
            dgf_ref[...] += jnp.sum(dy * nrm, axis=0, keepdims=True)
            dn = dy * gf
            dh2 = r2 * (dn - nrm * jnp.mean(dn * nrm, axis=-1, keepdims=True))
            dhb_ref[rows, :] = dh2.astype(dhb_ref.dtype)
        dwo_ref[...] += _dot(mgt_ref[...], dhb_ref[...])

    row = pl.BlockSpec((tm, D_MODEL), lambda i: (i, 0))
    wsp = pl.BlockSpec((D_MODEL, D_MODEL), lambda i: (0, 0), pipeline_mode=pl.Buffered(1))
    vec = lambda w: pl.BlockSpec((1, w), lambda i: (0, 0))
    act = jax.ShapeDtypeStruct((s_n, D_MODEL), ACT_DTYPE)
    return pl.pallas_call(
        body, grid=(s_n // tm,),
        in_specs=[row, row, wsp, wsp, wsp,
                  pl.BlockSpec((tm, D_MODEL), lambda i: (i, 8)), pl.BlockSpec((tm, D_MODEL), lambda i: (i, 9)),
                  vec(2 * D_MODEL), row, vec(D_MODEL), row],
        out_specs=[row, row, row, vec(D_MODEL), pl.BlockSpec((8, LANES), lambda i: (0, 0)),
                   pl.BlockSpec((D_MODEL, D_MODEL), lambda i: (0, 0))],
        out_shape=[act, act, act, jax.ShapeDtypeStruct((1, D_MODEL), F32), jax.ShapeDtypeStruct((8, LANES), F32),
                   jax.ShapeDtypeStruct((D_MODEL, D_MODEL), F32)],
        scratch_shapes=[pltpu.VMEM((D_MODEL, tm), MXU_DTYPE)],
        name="merge_loss",
    )(hc, ha, woc, woa, wo, proj, proj, b_merge, xp, final_g, tgt)


def _merge_bwd(dh2b, wo, woc, woa, yc, ya, proj, b_merge, o, hct, hat):
    s_n = dh2b.shape[0]
    tm = 512

    def body(dh_ref, wo_ref, woc_ref, woa_ref, yc_ref, ya_ref, gc_ref, ga_ref, b_ref, o_ref, za_ref, e_ref,
             hct_ref, hat_ref, dhc_ref, do_ref, dsum_ref, db3_ref, dbias_ref, dwoc_ref, dwoa_ref,
             dyc_ref, dya_ref):
        i = pl.program_id(0)

        @pl.when(i == 0)
        def _():
            dbias_ref[...] = jnp.zeros(dbias_ref.shape, F32)
            dwoc_ref[...] = jnp.zeros(dwoc_ref.shape, F32)
            dwoa_ref[...] = jnp.zeros(dwoa_ref.shape, F32)

        for rows in _row_chains(tm):
            dmg = _dot_nt(dh_ref[rows, :], wo_ref[...])
            gc, ga = _gates(gc_ref, ga_ref, b_ref, rows)
            dgc = dmg * yc_ref[rows, :].astype(F32) * gc * (1.0 - gc)
            dga = dmg * ya_ref[rows, :].astype(F32) * ga * (1.0 - ga)
            dbias_ref[:, :D_MODEL] += jnp.sum(dgc, axis=0, keepdims=True)
            dbias_ref[:, D_MODEL:] += jnp.sum(dga, axis=0, keepdims=True)
            dyc = dmg * gc
            dya = dmg * ga
            dyc_ref[rows, :] = dyc.astype(dyc_ref.dtype)
            dya_ref[rows, :] = dya.astype(dya_ref.dtype)
            dhc_ref[rows, :] = _dot_nt(dyc, woc_ref[...]).astype(dhc_ref.dtype)
            dha = _dot_nt(dya, woa_ref[...])
            z = za_ref[rows, :].astype(F32)
            sg = _sigmoid(z)
            ov = o_ref[rows, :].astype(F32)
            dout = dha * z * sg
            do_ref[rows, :] = dout.astype(do_ref.dtype)
            dsum_ref[rows, :] = _select_cols(dout * ov, e_ref[...], terms=2)
            db3_ref[0, rows, :] = (dha * ov * sg * (1.0 + z * (1.0 - sg))).astype(db3_ref.dtype)
            db3_ref[1, rows, :] = dgc.astype(db3_ref.dtype)
            db3_ref[2, rows, :] = dga.astype(db3_ref.dtype)
        dwoc_ref[...] += _dot(hct_ref[...], dyc_ref[...])
        dwoa_ref[...] += _dot(hat_ref[...], dya_ref[...])

    row = pl.BlockSpec((tm, D_MODEL), lambda i: (i, 0))
    col = pl.BlockSpec((D_MODEL, tm), lambda i: (0, i))
    wsp = pl.BlockSpec((D_MODEL, D_MODEL), lambda i: (0, 0), pipeline_mode=pl.Buffered(1))
    acc = pl.BlockSpec((D_MODEL, D_MODEL), lambda i: (0, 0))
    act = jax.ShapeDtypeStruct((s_n, D_MODEL), ACT_DTYPE)
    grad = jax.ShapeDtypeStruct((D_MODEL, D_MODEL), F32)
    return pl.pallas_call(
        body, grid=(s_n // tm,),
        in_specs=[row, wsp, wsp, wsp, row, row,
                  pl.BlockSpec((tm, D_MODEL), lambda i: (i, 8)), pl.BlockSpec((tm, D_MODEL), lambda i: (i, 9)),
                  pl.BlockSpec((1, 2 * D_MODEL), lambda i: (0, 0)), row,
                  pl.BlockSpec((tm, D_MODEL), lambda i: (i, 7)), pl.BlockSpec((D_MODEL, LANES), lambda i: (0, 0)),
                  col, col],
        out_specs=[row, row, pl.BlockSpec((tm, LANES), lambda i: (i, 0)),
                   pl.BlockSpec((3, tm, D_MODEL), lambda i: (0, i, 0)),
                   pl.BlockSpec((1, 2 * D_MODEL), lambda i: (0, 0)), acc, acc],
        out_shape=[act, act, jax.ShapeDtypeStruct((s_n, LANES), F32),
                   jax.ShapeDtypeStruct((3, s_n, D_MODEL), ACT_DTYPE),
                   jax.ShapeDtypeStruct((1, 2 * D_MODEL), F32), grad, grad],
        scratch_shapes=[pltpu.VMEM((tm, D_MODEL), MXU_DTYPE), pltpu.VMEM((tm, D_MODEL), MXU_DTYPE)],
        name="merge_bwd",
    )(dh2b, wo, woc, woa, yc, ya, proj, proj, b_merge, o, proj, _head_sum_matrix(), hct, hat)


def _conv_bwd(proj, conv_w, dhc):
    s_n = proj.shape[0]
    lr = s_n // N_RES
    pv = proj.reshape(N_RES, lr, IN_COLS)

    def body(xc_ref, bg_ref, cg_ref, zc_ref, w_ref, dhc_ref, da4_ref, dw_ref, dc_ref):
        w = w_ref[...]
        row = lax.broadcasted_iota(jnp.int32, (lr, LANES), 0)
        dw = [jnp.zeros((1, LANES), F32) for _ in range(3)]
        products = {}
        for r in range(N_RES):
            a, am1, am2 = _conv_terms(xc_ref, cg_ref, r, row, lr, products)
            c = w[0:1] * am2 + w[1:2] * am1 + w[2:3] * a
            z = zc_ref[r].astype(F32)
            sg = _sigmoid(z)
            sz = z * sg
            bg = bg_ref[r].astype(F32)
            dh = dhc_ref[r].astype(F32)
            da4_ref[1, r] = (dh * sz * c).astype(da4_ref.dtype)
            da4_ref[3, r] = (dh * bg * c * sg * (1.0 + z * (1.0 - sg))).astype(da4_ref.dtype)
            dc = dh * sz * bg
            dc_ref[r] = dc
            dw[0] = dw[0] + jnp.sum(dc * am2, axis=0, keepdims=True)
            dw[1] = dw[1] + jnp.sum(dc * am1, axis=0, keepdims=True)
            dw[2] = dw[2] + jnp.sum(dc * a, axis=0, keepdims=True)
        dw_ref[0:1, :] = dw[0]
        dw_ref[1:2, :] = dw[1]
        dw_ref[2:3, :] = dw[2]

        def shift_up(v):
            return jnp.where(row < lr - 1, pltpu.roll(v, lr - 1, 0), 0.0)

        for r in range(N_RES):
            dp1 = dc_ref[r + 1] if r + 1 < N_RES else shift_up(dc_ref[0])
            dp2 = dc_ref[r + 2] if r + 2 < N_RES else shift_up(dc_ref[r + 2 - N_RES])
            da = w[2:3] * dc_ref[r] + w[1:2] * dp1 + w[0:1] * dp2
            da4_ref[0, r] = (da * cg_ref[r].astype(F32)).astype(da4_ref.dtype)
            da4_ref[2, r] = (da * xc_ref[r].astype(F32)).astype(da4_ref.dtype)

    def col(part):
        return pl.BlockSpec((N_RES, lr, LANES), lambda j: (0, 0, part * 8 + j))

    da4, dw = pl.pallas_call(
        body, grid=(D_MODEL // LANES,),
        in_specs=[col(0), col(1), col(2), col(3), pl.BlockSpec((3, LANES), lambda j: (0, j)),
                  pl.BlockSpec((N_RES, lr, LANES), lambda j: (0, 0, j))],
        out_specs=[pl.BlockSpec((4, N_RES, lr, LANES), lambda j: (0, 0, 0, j)),
                   pl.BlockSpec((3, LANES), lambda j: (0, j))],
        out_shape=[jax.ShapeDtypeStruct((4, N_RES, lr, D_MODEL), ACT_DTYPE),
                   jax.ShapeDtypeStruct((3, D_MODEL), F32)],
        scratch_shapes=[pltpu.VMEM((N_RES, lr, LANES), F32)],
        name="conv_bwd",
    )(pv, pv, pv, pv, conv_w, dhc.reshape(N_RES, lr, D_MODEL))
    return da4.reshape(4, s_n, D_MODEL), dw


def _attn_bwd(proj, dout, lse, dsum, slopes, d, prev=None):
    g_n, rq = PATTERNS[d]
    un = _Units(g_n, rq)
    s_n = proj.shape[0]
    lr = s_n // N_RES
    q_n = g_n * rq
    d0, m0, d1, m1 = (np.ascontiguousarray(t.T) for t in _attn_tables(d))
    first, n_more, later = un.plan(lr, ATTN_BATCH)
    bsz = un.count(first)
    gd = RES_PER_STEP if un.per_res else g_n

    def body(sl_ref, q_ref, k_ref, v_ref, do_ref, lse_ref, ds_ref, d0_ref, m0_ref, d1_ref, m1_ref, *rest):
        prev_ref = rest[0] if prev is not None else None
        out_ref, b0_ref, b1_ref, lt_ref, dt_ref, dk_ref, dv_ref = rest[-7:]
        hp = pl.program_id(1)
        for h in (0, 1):
            slope = sl_ref[2 * hp + h]
            b0_ref[:, h * q_n:(h + 1) * q_n] = m0_ref[...] - slope * d0_ref[...]
            b1_ref[:, h * q_n:(h + 1) * q_n] = m1_ref[...] - slope * d1_ref[...]
        if prev is None:
            dk_ref[...] = jnp.zeros(dk_ref.shape, F32)
            dv_ref[...] = jnp.zeros(dv_ref.shape, F32)
        else:
            out_ref[0] = prev_ref[0]
            dk_ref[...] = prev_ref[1].astype(F32)
            dv_ref[...] = prev_ref[2].astype(F32)
        low = lax.broadcasted_iota(jnp.int32, (1, q_n, LANES), 2) < HEAD_DIM
        row16 = pl.multiple_of(16 * hp, 16)

        def query_rows(stat_ref, t_ref, qs):
            tiles = un.load_q(stat_ref, qs)
            for b in range(bsz):
                t_ref[b] = tiles[b].T
            t16 = t_ref[:, pl.ds(row16, 16), :]
            return jnp.concatenate([t16[:, 0:1, :], t16[:, 8:9, :]], axis=2)

        def batch(qs, at_start):
            qq = _stack_heads(un.load_q(q_ref, qs) * 0.125, low)
            dd = _stack_heads(un.load_q(do_ref, qs), low)
            ks = un.load_k(k_ref, qs, at_start)
            vs = un.load_k(v_ref, qs, at_start)
            lrow = query_rows(lse_ref, lt_ref, qs)
            drow = query_rows(ds_ref, dt_ref, qs)
            pt = jnp.exp(_bdot(ks, qq, _BNT) + _batch_bias(un, qs, at_start, b0_ref, b1_ref) - lrow)
            dst = pt * (_bdot(vs, dd, _BNT) - drow)
            un.add_k(dv_ref, qs, _bdot(pt, dd, _BNN), at_start)
            un.add_k(dk_ref, qs, _bdot(dst, qq, _BNN), at_start)
            dq = _bdot(jnp.swapaxes(dst, 1, 2), ks, _BNN)
            un.store_q(out_ref, qs, jnp.where(low, dq[:, :q_n], dq[:, q_n:]) * 0.125, add=prev is not None,
                       lead=(0,))

        batch(first, True)

        def more(j, carry):
            batch(later(j), False)
            return carry

        lax.fori_loop(1, 1 + n_more, more, 0)
        out_ref[1] = dk_ref[...].astype(out_ref.dtype)
        out_ref[2] = dv_ref[...].astype(out_ref.dtype)

    pv = _pattern_view(proj, g_n)
    full = lambda a: pl.BlockSpec(a.shape, lambda r, hp: (0, 0))
    whole = _pattern_spec(g_n, lr, lambda hp: hp, lead=(3,))
    out = pl.pallas_call(
        body, grid=_pattern_grid(g_n),
        in_specs=[pl.BlockSpec(memory_space=pltpu.SMEM),
                  _pattern_spec(g_n, lr, lambda hp: 32 + hp),
                  _pattern_spec(g_n, lr, lambda hp: 40 + hp),
                  _pattern_spec(g_n, lr, lambda hp: 48 + hp),
                  _pattern_spec(g_n, lr, lambda hp: hp),
                  _pattern_spec(g_n, lr, lambda hp: 0),
                  _pattern_spec(g_n, lr, lambda hp: 0),
                  full(d0), full(m0), full(d1), full(m1)] + ([] if prev is None else [whole]),
        out_specs=whole,
        out_shape=jax.ShapeDtypeStruct(_pattern_view_shape(s_n, D_MODEL, g_n, lead=(3,)), ACT_DTYPE),
        scratch_shapes=[pltpu.VMEM((d0.shape[0], 2 * q_n), F32), pltpu.VMEM((2 * q_n, 2 * q_n), F32),
                        pltpu.VMEM((bsz, LANES, q_n), F32), pltpu.VMEM((bsz, LANES, q_n), F32),
                        pltpu.VMEM((gd, lr, LANES), F32), pltpu.VMEM((gd, lr, LANES), F32)],
        name=f"attn_bwd_d{d}",
    )(slopes, pv, pv, pv, _pattern_view(dout, g_n), _pattern_view(lse, g_n), _pattern_view(dsum, g_n),
      d0, m0, d1, m1, *([] if prev is None else [_pattern_view(prev, g_n, lead=(3,))]))
    return out.reshape(3, s_n, D_MODEL)


def _part_index(step, per, lo, n):
    return jnp.clip(step // per - lo, 0, n - 1)


def _dw_in(ut, da4, dc3, db3):
    s_n = ut.shape[1]
    tn = 512
    per = D_MODEL // tn
    shard_blocks = SHARD_COLS // tn

    def body(a_ref, p0_ref, p1_ref, p2_ref, o_ref):
        part = pl.program_id(0) // per

        @pl.when(part < 4)
        def _():
            o_ref[...] = _dot(a_ref[...], p0_ref[...])

        @pl.when((part >= 4) & (part < 7))
        def _():
            o_ref[...] = _dot(a_ref[...], p1_ref[...])

        @pl.when(part >= 7)
        def _():
            o_ref[...] = _dot(a_ref[...], p2_ref[...])

    def pspec(lo, n):
        def index(j):
            part = j // per
            col = jnp.where(part < lo, 0, jnp.where(part >= lo + n, per - 1, j % per))
            return _part_index(j, per, lo, n), 0, col
        return pl.BlockSpec((None, s_n, tn), index)

    return pl.pallas_call(
        body, grid=(IN_COLS // tn,),
        in_specs=[pl.BlockSpec((D_MODEL, s_n), lambda j: (0, 0), pipeline_mode=pl.Buffered(1)),
                  pspec(0, 4), pspec(4, 3), pspec(7, 3)],
        out_specs=pl.BlockSpec((None, D_MODEL, tn), lambda j: (j // shard_blocks, 0, j % shard_blocks)),
        out_shape=jax.ShapeDtypeStruct((4, D_MODEL, SHARD_COLS), F32),
        name="dw_in",
    )(ut, da4, dc3, db3)


def _input_grad(da4, dc3, db3, w4, xp, norm_g, dh2, row0, rows):
    tm, tk = 256, 512
    per = D_MODEL // tk
    shard_blocks = SHARD_COLS // tk
    m0 = row0 // tm

    def body(p0_ref, p1_ref, p2_ref, w_hbm, x_ref, g_ref, dh_ref, gx_ref, dg_ref, w_ref, sems):
        first = pl.program_id(0) == 0
        loads = [pltpu.make_async_copy(w_hbm.at[j], w_ref.at[j], sems.at[j]) for j in range(4)]

        @pl.when(first)
        def _():
            dg_ref[...] = jnp.zeros(dg_ref.shape, F32)
            for cp in loads:
                cp.start()

        du = None
        for k in range(IN_COLS // tk):
            if k % shard_blocks == 0:
                pl.when(first)(loads[k // shard_blocks].wait)
            part, cols = k // per, pl.ds((k % per) * tk, tk)
            ref, slot = (p0_ref, part) if part < 4 else (p1_ref, part - 4) if part < 7 else (p2_ref, part - 7)
            d = _dot_nt(ref[slot, :, cols], w_ref[k // shard_blocks, :, pl.ds((k % shard_blocks) * tk, tk)])
            du = d if du is None else du + d
        x = x_ref[...]
        r = lax.rsqrt(jnp.mean(x * x, axis=-1, keepdims=True) + EPS)
        nrm = x * r
        dg_ref[...] += jnp.sum(du * nrm, axis=0, keepdims=True)
        dn = du * g_ref[...]
        gx_ref[...] = dh_ref[...].astype(F32) + r * (dn - nrm * jnp.mean(dn * nrm, axis=-1, keepdims=True))

    def pspec(n):
        return pl.BlockSpec((n, tm, D_MODEL), lambda m: (0, m0 + m, 0))

    row_in = pl.BlockSpec((tm, D_MODEL), lambda m: (m0 + m, 0))
    vec = pl.BlockSpec((1, D_MODEL), lambda m: (0, 0))
    return pl.pallas_call(
        body, grid=(rows // tm,),
        in_specs=[pspec(4), pspec(3), pspec(3), _ANY, row_in, vec, row_in],
        out_specs=[pl.BlockSpec((tm, D_MODEL), lambda m: (m, 0)), vec],
        out_shape=[jax.ShapeDtypeStruct((rows, D_MODEL), F32), jax.ShapeDtypeStruct((1, D_MODEL), F32)],
        scratch_shapes=[pltpu.VMEM(w4.shape, w4.dtype), pltpu.SemaphoreType.DMA((4,))],
        name="input_grad",
    )(da4, dc3, db3, w4, xp, norm_g, dh2)


class _Step:
    def __init__(self, x, tgt, norm_g, chip, after=None):
        self.norm_g, self.chip = norm_g, chip
        self.slopes = _alibi_slopes()
        self.xp, self.tp = _to_residue_major(x, tgt, after)
        self.u, self.ut = _rms_in(self.xp, norm_g)

    def project_own(self, w_own):
        self.proj_own = _in_proj(self.u, self.chip, w_own=w_own)

    def project(self, w4, others):
        self.proj_own = _in_proj(self.u, self.chip, w4=w4, partial=self.proj_own, others=others)

    def mixers(self, w4, taps):
        self.w4, self.taps, self.proj = w4, taps, self.proj_own
        self.hc, self.hct = _conv_fwd(self.proj, taps)
        fwd = [_attn_fwd(self.proj, self.slopes, d) for d in PATTERNS]
        self.o, self.lse, self.ha, self.hat = _attn_combine([f[0] for f in fwd], [f[1] for f in fwd], self.proj)

    def merge_and_loss(self, woc, woa, wo, b_merge, final_g):
        self.woc, self.woa, self.wo, self.b_merge = woc, woa, wo, b_merge
        (self.yc, self.ya, self.dh2b, self.d_final_g, self.loss8, self.d_wo) = _merge_loss(
            self.hc, self.ha, woc, woa, wo, self.proj, b_merge, self.xp, final_g, self.tp)

    def out_weight_grads(self):
        (self.dhc, self.dout, self.dsum, self.db3, self.d_bias, d_woc, d_woa) = _merge_bwd(
            self.dh2b, self.wo, self.woc, self.woa, self.yc, self.ya, self.proj, self.b_merge, self.o,
            self.hct, self.hat)
        return d_woc, d_woa, self.d_wo

    def conv_grads(self, after=0.0):
        self.da4, self.d_taps = _conv_bwd(self.proj, self.taps + after, self.dhc)

    def in_weight_grad(self, after=0.0):
        slopes = self.slopes + after
        self.dc3 = None
        for d in PATTERNS:
            self.dc3 = _attn_bwd(self.proj, self.dout, self.lse, self.dsum, slopes, d, prev=self.dc3)
        return _dw_in(self.ut, self.da4, self.dc3, self.db3)

    def input_grad(self, half, after=0.0):
        rows = self.xp.shape[0] // 2
        return _input_grad(self.da4, self.dc3, self.db3, self.w4, self.xp, self.norm_g + after, self.dh2b,
                           half * rows, rows)


def _local_grads(x, tgt, norm_g, w4, b_merge, conv_w, woc, woa, wo, final_g):
    st = _Step(x, tgt, norm_g, jnp.zeros((1,), jnp.int32))
    st.project_own(w4[0])
    st.project(w4, (2, 1))
    st.project(w4, (3,))
    st.mixers(w4, conv_w)
    st.merge_and_loss(woc, woa, wo, b_merge, final_g)
    d_woc, d_woa, d_wo = st.out_weight_grads()
    st.conv_grads()
    d_w4 = st.in_weight_grad()
    gx_lo, dg_lo = st.input_grad(0)
    gx_hi, dg_hi = st.input_grad(1)
    return (st.loss8, _to_natural(gx_lo, gx_hi), dg_lo + dg_hi, d_w4, st.d_bias, st.d_taps, d_woc, d_woa, d_wo,
            st.d_final_g)


MESH = pl.DeviceIdType.MESH
_CHIP_FLIPS = ((1, 0), (0, 1), (1, 1))
_ANY = pl.BlockSpec(memory_space=pl.ANY)


def _place():
    return lax.axis_index("x"), lax.axis_index("y"), lax.axis_index("c")


def _flip(v, f):
    return 1 - v if f else v


def _remote(src, dst, send_sems, recv_sems, k, device):
    return pltpu.make_async_remote_copy(src_ref=src, dst_ref=dst, send_sem=send_sems.at[k], recv_sem=recv_sems.at[k],
                                        device_id=device, device_id_type=MESH)


def _place_shard(w, chip, dtype):
    rows, cols = w.shape
    tm = min(rows, 128)

    def body(chip_ref, w_ref, o_ref):
        o_ref[0] = w_ref[...].astype(o_ref.dtype)

    return pl.pallas_call(
        body,
        grid_spec=pltpu.PrefetchScalarGridSpec(
            num_scalar_prefetch=1, grid=(rows // tm,),
            in_specs=[pl.BlockSpec((tm, cols), lambda i, chip_ref: (i, 0))],
            out_specs=pl.BlockSpec((1, tm, cols), lambda i, chip_ref: (chip_ref[0], i, 0))),
        out_shape=jax.ShapeDtypeStruct((4, rows, cols), dtype),
        name="place_shard",
    )(chip, w)


def _gather_copies_to(flips, whole=()):
    def copies(arrs, _, send_sems, recv_sems):
        x, y, c = _place()
        out = []
        for a, arr in enumerate(arrs):
            h = arr.shape[1] // 2
            mine = arr.at[2 * x + y] if a in whole else arr.at[2 * x + y, pl.ds(pl.multiple_of(c * h, 8), h)]
            for i, t in enumerate(flips):
                fx, fy = _CHIP_FLIPS[t]
                out.append(_remote(mine, mine, send_sems, recv_sems, len(flips) * a + i,
                                   (_flip(x, fx), _flip(y, fy), c)))
        return out
    return copies


def _forward_to_sibling(arrs, flips=(0, 1, 2)):
    n = len(arrs)

    def body(*refs):
        outs = refs[n:2 * n]
        send_sems, recv_sems = refs[2 * n:]
        x, y, c = _place()
        sibling = (x, y, 1 - c)
        started = []
        for a in range(n):
            h = outs[a].shape[1] // 2
            rows = pl.ds(pl.multiple_of(c * h, 8), h)
            for t in flips:
                fx, fy = _CHIP_FLIPS[t]
                landed = outs[a].at[2 * _flip(x, fx) + _flip(y, fy), rows]
                cp = _remote(landed, landed, send_sems, recv_sems, 3 * a + t, sibling)
                cp.start()
                started.append(cp)
        for a in range(n):
            h = outs[a].shape[1] // 2
            rows = pl.ds(pl.multiple_of((1 - c) * h, 8), h)
            for t in flips:
                fx, fy = _CHIP_FLIPS[t]
                handed = outs[a].at[2 * _flip(x, fx) + _flip(y, fy), rows]
                _remote(handed, handed, send_sems, recv_sems, 3 * a + t, sibling).wait_recv()
        for cp in started:
            cp.wait_send()

    return pl.pallas_call(
        body, in_specs=[_ANY] * n, out_specs=[_ANY] * n,
        out_shape=[jax.ShapeDtypeStruct(s.shape, s.dtype) for s in arrs],
        input_output_aliases={a: a for a in range(n)},
        scratch_shapes=[pltpu.SemaphoreType.DMA((3 * n,)), pltpu.SemaphoreType.DMA((3 * n,))],
        name="gathered_to_sibling_" + "".join(str(t) for t in flips),
    )(*arrs)


_HBM = pl.BlockSpec(memory_space=pltpu.HBM)
_SEM = pl.BlockSpec(memory_space=pltpu.SEMAPHORE)
_EFFECT = pltpu.SideEffectType.DATAFLOW_SIDE_EFFECTING


class _SplitExchange:
    def __init__(self, name, srcs, land_shapes, n_copies, copies, riders=()):
        self.name, self.n, self.nl, self.copies = name, len(srcs), len(land_shapes), copies
        n, nb = self.n, len(srcs) + len(land_shapes)
        lands = [lax.empty(s.shape, s.dtype) for s in land_shapes]
        bufs = [pltpu.with_memory_space_constraint(a, pltpu.HBM) for a in (*srcs, *lands, *riders)]
        na = len(bufs)

        def body(*refs):
            send_sems, recv_sems = refs[na], refs[na + 1]
            for cp in copies(refs[:n], refs[n:nb], send_sems, recv_sems):
                cp.start()
            refs[-1][...] = jnp.zeros(refs[-1].shape, F32)

        outs = pl.pallas_call(
            body, name=name + "_start",
            in_specs=[_HBM] * na,
            out_specs=[_SEM, _SEM] + [_HBM] * na + [pl.BlockSpec(memory_space=pltpu.VMEM)],
            out_shape=[pltpu.SemaphoreType.DMA((n_copies,)), pltpu.SemaphoreType.DMA((n_copies,))]
            + [pltpu.HBM(b.shape, b.dtype) for b in bufs] + [jax.ShapeDtypeStruct((8, LANES), F32)],
            input_output_aliases={i: 2 + i for i in range(na)},
            compiler_params=pltpu.CompilerParams(has_side_effects=_EFFECT),
        )(*bufs)
        self.sems, self.bufs, self.riders, self.token = outs[:2], outs[2:2 + nb], outs[2 + nb:2 + na], outs[-1]

    def after(self):
        return self.token[0, 0]

    def wait(self, done, riders=(), bufs=None):
        n, nb, copies = self.n, self.n + self.nl, self.copies
        bufs = [*(self.bufs if bufs is None else bufs),
                *[pltpu.with_memory_space_constraint(a, pltpu.HBM) for a in riders]]
        na = len(bufs)
        done = list(done) if isinstance(done, (list, tuple)) else [done]

        def body(*refs):
            send_sems, recv_sems = refs[na], refs[na + 1]
            for cp in copies(refs[:n], refs[n:nb], send_sems, recv_sems):
                cp.wait_send()
                cp.wait_recv()

        outs = pl.pallas_call(
            body, name=self.name + "_wait",
            in_specs=[_HBM] * na + [_SEM, _SEM] + [_ANY] * len(done),
            out_specs=[_HBM] * na,
            out_shape=[pltpu.HBM(b.shape, b.dtype) for b in bufs],
            input_output_aliases={i: i for i in range(na)},
            compiler_params=pltpu.CompilerParams(has_side_effects=_EFFECT),
        )(*bufs, *self.sems, *done)
        return outs[:n], outs[n:nb], outs[nb:]


def _sibling_copies(srcs, lands, send_sems, recv_sems):
    x, y, c = _place()
    out = []
    for a, (src, land) in enumerate(zip(srcs, lands)):
        h = src.shape[1] // 2
        theirs = pl.ds(pl.multiple_of((1 - c) * h, 8), h)
        out.append(_remote(src.at[:, theirs], land, send_sems, recv_sems, a, (x, y, 1 - c)))
    return out


def _grads_to_sibling(name, grads):
    shapes = [jax.ShapeDtypeStruct((4, g.shape[1] // 2, g.shape[2]), g.dtype) for g in grads]
    return _SplitExchange(name, grads, shapes, len(grads), _sibling_copies)


def _chip_copies(srcs, lands, send_sems, recv_sems):
    x, y, c = _place()
    out = []
    for a, (src, land) in enumerate(zip(srcs, lands)):
        for t, (fx, fy) in enumerate(_CHIP_FLIPS):
            tx, ty = _flip(x, fx), _flip(y, fy)
            out.append(_remote(src.at[2 * tx + ty], land.at[t], send_sems, recv_sems, 3 * a + t, (tx, ty, c)))
    return out


def _grads_to_chips(name, parts):
    shapes = [jax.ShapeDtypeStruct((3, *p.shape[1:]), p.dtype) for p in parts]
    return _SplitExchange(name, parts, shapes, 3 * len(parts), _chip_copies)


def _add_halves(g, r, half):
    _, rows, cols = g.shape
    h = rows // 2
    tm = min(h, 128)
    nt = h // tm

    def body(half_ref, g_ref, r_ref, b_ref):
        b_ref[...] = (g_ref[...] + r_ref[...]).astype(b_ref.dtype)

    spec = pl.BlockSpec((1, tm, cols), lambda j, i, half_ref: (j, i, 0))
    return pl.pallas_call(
        body,
        grid_spec=pltpu.PrefetchScalarGridSpec(
            num_scalar_prefetch=1, grid=(4, nt),
            in_specs=[pl.BlockSpec((1, tm, cols), lambda j, i, half_ref: (j, half_ref[0] * nt + i, 0)), spec],
            out_specs=spec),
        out_shape=jax.ShapeDtypeStruct((4, h, cols), BF16),
        name="add_sibling_grads",
    )(half, g, r)


def _add_chips(g, r, recv, where):
    _, h, cols = r.shape
    tm = min(h, 128)
    nt = h // tm

    def body(where_ref, g_ref, r_ref, recv_ref, out_ref):
        own = g_ref[0] + r_ref[0]
        out_ref[...] = ((own + recv_ref[0].astype(F32)) + recv_ref[1].astype(F32)) + recv_ref[2].astype(F32)

    return pl.pallas_call(
        body,
        grid_spec=pltpu.PrefetchScalarGridSpec(
            num_scalar_prefetch=1, grid=(nt,),
            in_specs=[pl.BlockSpec((1, tm, cols), lambda i, w: (w[0], w[1] * nt + i, 0)),
                      pl.BlockSpec((1, tm, cols), lambda i, w: (w[0], i, 0)),
                      pl.BlockSpec((3, tm, cols), lambda i, w: (0, i, 0))],
            out_specs=pl.BlockSpec((tm, cols), lambda i, w: (w[1] * nt + i, 0))),
        out_shape=jax.ShapeDtypeStruct((2 * h, cols), F32),
        name="add_chip_grads",
    )(where, g, r, recv)


def _share_halves(shards):
    n = len(shards)

    def body(*refs):
        outs = refs[n:2 * n]
        send_sems, recv_sems = refs[2 * n:]
        x, y, c = _place()
        copies = []
        for a in range(n):
            h = outs[a].shape[0] // 2
            mine = outs[a].at[pl.ds(pl.multiple_of(c * h, 8), h)]
            copies.append(_remote(mine, mine, send_sems, recv_sems, a, (x, y, 1 - c)))
        for cp in copies:
            cp.start()
        for a, cp in enumerate(copies):
            cp.wait_send()
            h = outs[a].shape[0] // 2
            theirs = outs[a].at[pl.ds(pl.multiple_of((1 - c) * h, 8), h)]
            _remote(theirs, theirs, send_sems, recv_sems, a, (x, y, 1 - c)).wait_recv()

    return pl.pallas_call(
        body, in_specs=[_ANY] * n, out_specs=[_ANY] * n,
        out_shape=[jax.ShapeDtypeStruct(p.shape, p.dtype) for p in shards],
        input_output_aliases={a: a for a in range(n)},
        scratch_shapes=[pltpu.SemaphoreType.DMA((n,)), pltpu.SemaphoreType.DMA((n,))],
        name="share_reduced_halves",
    )(*shards)


def _reduce_small(rows):
    cols = rows[0].shape[1]
    n = len(rows)
    assert sum(r.shape[0] for r in rows) <= 8

    def body(*refs):
        ins, out_ref = refs[:n], refs[n]
        vec_ref, gath_ref, send_sems, recv_sems = refs[n + 1:]
        x, y, c = _place()
        me = 4 * x + 2 * y + c
        vec_ref[...] = jnp.zeros(vec_ref.shape, F32)
        at = 0
        for r in ins:
            vec_ref[at:at + r.shape[0], :] = r[...]
            at += r.shape[0]
        copies = []
        for k in range(1, 8):
            peer = (_flip(x, (k >> 2) & 1), _flip(y, (k >> 1) & 1), _flip(c, k & 1))
            copies.append(_remote(vec_ref, gath_ref.at[me], send_sems, recv_sems, k - 1, peer))
        for cp in copies:
            cp.start()
        gath_ref[me] = vec_ref[...]
        for cp in copies:
            cp.wait()
        tot = gath_ref[0]
        for dev in range(1, 8):
            tot = tot + gath_ref[dev]
        out_ref[...] = tot
        out_ref[7:8, :] = jnp.zeros((1, cols), F32) + jnp.sum(tot[7:8, :])

    vm = pl.BlockSpec(memory_space=pltpu.VMEM)
    return pl.pallas_call(
        body, in_specs=[vm] * n, out_specs=vm,
        out_shape=jax.ShapeDtypeStruct((8, cols), F32),
        scratch_shapes=[pltpu.VMEM((8, cols), F32), pltpu.VMEM((8, 8, cols), F32),
                        pltpu.SemaphoreType.DMA((7,)), pltpu.SemaphoreType.DMA((7,))],
        name="reduce_small",
    )(*rows)


def _adamw(w, g, m, v, name):
    rows, cols = w.shape
    tm = 128 if rows % 128 == 0 else rows

    def body(w_ref, g_ref, m_ref, v_ref, d_ref, m2_ref, v2_ref, gout_ref):
        gr = g_ref[...]
        m2 = ADAM_B1 * m_ref[...] + (1.0 - ADAM_B1) * gr
        v2 = ADAM_B2 * v_ref[...] + (1.0 - ADAM_B2) * (gr * gr)
        m_hat = m2 / (1.0 - ADAM_B1 ** ADAM_STEP)
        v_hat = v2 / (1.0 - ADAM_B2 ** ADAM_STEP)
        d_ref[...] = -ADAM_LR * (m_hat / (jnp.sqrt(v_hat) + ADAM_EPS) + ADAM_WD * w_ref[...])
        m2_ref[...] = m2
        v2_ref[...] = v2
        gout_ref[...] = gr

    spec = pl.BlockSpec((tm, cols), lambda i: (i, 0))
    sds = jax.ShapeDtypeStruct((rows, cols), F32)
    return pl.pallas_call(body, grid=(rows // tm,), in_specs=[spec] * 4, out_specs=[spec] * 4,
                          out_shape=[sds] * 4, name=name)(w, g, m, v)


def kernel(x, norm_g, w_in, b_merge, conv_w, w_out_conv, w_out_attn, w_o, final_g, loss_target, m_norm_g, m_w_in, m_b_merge, m_conv_w, m_w_out_conv, m_w_out_attn, m_w_o, m_final_g, v_norm_g, v_w_in, v_b_merge, v_conv_w, v_w_out_conv, v_w_out_attn, v_w_o, v_final_g):
    mx, my, mc = _place()
    chip = (2 * mx + my).astype(jnp.int32)
    seq = x.shape[1]

    chip1 = chip.reshape(1)
    slots = [_place_shard(w[0], chip1, MXU_DTYPE) for w in (w_in, w_out_conv, w_out_attn, w_o)]
    taps_slot = _place_shard(jnp.pad(conv_w[0], ((0, 5), (0, 0))), chip1, F32)
    gather_near = _SplitExchange("gather_w_in_near", [slots[0], taps_slot], [], 4,
                                 _gather_copies_to((0, 1), whole=(1,)))
    st = _Step(x[0], loss_target[0], norm_g, chip1, after=gather_near.token)
    st.project_own(w_in[0])
    near, _, _ = gather_near.wait([st.ut, st.proj_own])
    gather_far = _SplitExchange("gather_w_in_far", near, [], 2, _gather_copies_to((2,), whole=(1,)))
    (w4,) = _forward_to_sibling(gather_far.bufs[:1], flips=(0, 1))
    st.project(w4, (2, 1))
    (w4, taps4), _, out_slots = gather_far.wait([st.proj_own], riders=slots[1:], bufs=[w4, gather_far.bufs[1]])
    gather_out = _SplitExchange("gather_w_out", out_slots, [], 9, _gather_copies_to((0, 1, 2)), riders=[w4])
    (w4,) = _forward_to_sibling(gather_out.riders, flips=(2,))
    st.project(w4, (3,))
    st.mixers(w4, jnp.concatenate([taps4[j, :3, :] for j in range(4)], axis=1))
    out_ws, _, _ = gather_out.wait(st.o)
    woc, woa, wo = [w.reshape(D_MODEL, D_MODEL) for w in _forward_to_sibling(out_ws)]
    st.merge_and_loss(woc, woa, wo, b_merge, final_g.reshape(1, D_MODEL))

    half = mc.astype(jnp.int32).reshape(1)
    where = jnp.stack([chip, mc.astype(jnp.int32)])
    out_grads = [g.reshape(4, -1, D_MODEL) for g in st.out_weight_grads()]
    to_sibling = _grads_to_sibling("out_grads_to_sibling", out_grads)
    st.conv_grads(after=to_sibling.after())
    out_grads, out_from_sibling, _ = to_sibling.wait(st.da4)
    to_chips = _grads_to_chips("out_grads_to_chips",
                               [_add_halves(g, r, half) for g, r in zip(out_grads, out_from_sibling)])
    d_w4 = st.in_weight_grad(after=to_chips.after())
    out_from_chips = to_chips.wait(st.dc3)[1]

    to_sibling = _grads_to_sibling("in_grad_to_sibling", [d_w4])
    gx_lo, dg_lo = st.input_grad(0, after=to_sibling.after())
    (d_w4,), (from_sibling,), _ = to_sibling.wait(gx_lo)
    to_chips = _grads_to_chips("in_grad_to_chips", [_add_halves(d_w4, from_sibling, half)])
    gx_hi, dg_hi = st.input_grad(1, after=to_chips.after())
    grad_x = _to_natural(gx_lo, gx_hi)

    where_late = where + to_chips.after().astype(jnp.int32)
    out_reduced = [_add_chips(g, r, recv, where_late)
                   for g, r, recv in zip(out_grads, out_from_sibling, out_from_chips)]
    g_woc, g_woa, g_wo = _share_halves(out_reduced)
    small = _reduce_small([dg_lo + dg_hi, st.d_bias.reshape(2, D_MODEL), st.d_taps, st.d_final_g,
                           st.loss8.reshape(1, D_MODEL)])
    loss = (0.5 / D_MODEL) * small[7, 0]
    g_taps = lax.dynamic_slice(small[3:6], (0, chip * (D_MODEL // 4)), (3, D_MODEL // 4))
    upd = {
        "norm_g": _adamw(norm_g, small[0:1], m_norm_g, v_norm_g, "adamw_norm_g"),
        "b_merge": _adamw(b_merge, small[1:3].reshape(1, 2 * D_MODEL), m_b_merge, v_b_merge, "adamw_b_merge"),
        "conv_w": _adamw(conv_w[0], g_taps, m_conv_w[0], v_conv_w[0], "adamw_conv_w"),
        "w_out_conv": _adamw(w_out_conv[0], g_woc, m_w_out_conv[0], v_w_out_conv[0], "adamw_w_out_conv"),
        "w_out_attn": _adamw(w_out_attn[0], g_woa, m_w_out_attn[0], v_w_out_attn[0], "adamw_w_out_attn"),
        "w_o": _adamw(w_o[0], g_wo, m_w_o[0], v_w_o[0], "adamw_w_o"),
        "final_g": _adamw(final_g.reshape(1, D_MODEL), small[6:7], m_final_g.reshape(1, D_MODEL),
                          v_final_g.reshape(1, D_MODEL), "adamw_final_g"),
    }
    behind = [grad_x] + [u[0] for u in upd.values()]
    in_reduced = _add_chips(d_w4, from_sibling, to_chips.wait(behind)[1][0], where)
    (g_w_in,) = _share_halves([in_reduced])
    upd["w_in"] = _adamw(w_in[0], g_w_in, m_w_in[0], v_w_in[0], "adamw_w_in")

    names = ["norm_g", "w_in", "b_merge", "conv_w", "w_out_conv", "w_out_attn", "w_o", "final_g"]
    shapes = [norm_g.shape, w_in.shape, b_merge.shape, conv_w.shape, w_out_conv.shape, w_out_attn.shape,
              w_o.shape, final_g.shape]
    outs = [loss, grad_x.reshape(1, seq, D_MODEL)]
    for k in (3, 0, 1, 2):
        outs += [upd[n][k].reshape(s) for n, s in zip(names, shapes)]
    return tuple(outs)
```

```python
import functools

import numpy as np
import jax
import jax.numpy as jnp
from jax import lax
from jax.experimental import pallas as pl
from jax.experimental.pallas import tpu as pltpu

F32 = jnp.float32
BF16 = jnp.bfloat16
MXU_DTYPE = jnp.bfloat16
ACT_DTYPE = jnp.bfloat16

D_MODEL = 1024
N_HEADS = 16
HEAD_DIM = 64
QB = 128
N_RES = 16
LANES = 128
HP = N_HEADS * HEAD_DIM // LANES
IN_COLS = 10 * D_MODEL
SHARD_COLS = IN_COLS // 4
EPS = 1e-6
NEG = -1e30

ADAM_LR, ADAM_B1, ADAM_B2, ADAM_EPS, ADAM_WD, ADAM_STEP = 0.001, 0.9, 0.999, 1e-08, 0.01, 10

PATTERNS = {1: (16, 8), 4: (4, 32), 16: (1, 128)}

_NN = (((1,), (0,)), ((), ()))
_NT = (((1,), (1,)), ((), ()))


def _dot(a, b):
    return lax.dot_general(a.astype(MXU_DTYPE), b.astype(MXU_DTYPE), _NN, preferred_element_type=F32)


def _dot_nt(a, b):
    return lax.dot_general(a.astype(MXU_DTYPE), b.astype(MXU_DTYPE), _NT, preferred_element_type=F32)


def _split3(x):
    hi = x.astype(BF16)
    r1 = x - hi.astype(F32)
    mid = r1.astype(BF16)
    lo = (r1 - mid.astype(F32)).astype(BF16)
    return hi, mid, lo


def _select_cols(x, sel, terms):
    return sum(lax.dot_general(t, sel, _NN, preferred_element_type=F32) for t in _split3(x)[:terms])


def _sigmoid(z):
    return 1.0 / (1.0 + jnp.exp(-z))


def _head_expand_matrix():
    e = np.zeros((LANES, D_MODEL), np.float32)
    for h in range(N_HEADS):
        e[8 * h, HEAD_DIM * h:HEAD_DIM * (h + 1)] = 1.0
    return jnp.asarray(e, BF16)


def _head_sum_matrix():
    e = np.zeros((D_MODEL, LANES), np.float32)
    for h in range(N_HEADS):
        e[HEAD_DIM * h:HEAD_DIM * (h + 1), 8 * h:8 * (h + 1)] = 1.0
    return jnp.asarray(e, BF16)


def _attn_tables(d):
    g_n, rq = PATTERNS[d]
    q_n = g_n * rq
    gq, iq = np.arange(q_n) // rq, np.arange(q_n) % rq

    def tab(kn, base):
        k_n = g_n * kn
        gk, jk = np.arange(k_n) // kn, np.arange(k_n) % kn
        delta = g_n * (base + iq[:, None] - jk[None, :]) + gq[:, None] - gk[None, :]
        valid = (delta >= 0) & (delta <= QB)
        dist = np.where(valid, d * delta, 0).astype(np.float32)
        madd = np.where(valid, 0.0, NEG).astype(np.float32)
        return dist, madd

    d0, m0 = tab(rq if g_n == 1 else 2 * rq, 0)
    d1, m1 = tab(2 * rq, rq)
    return d0, m0, d1, m1


def _alibi_slopes():
    return jnp.exp2(-8.0 * jnp.arange(1, N_HEADS + 1, dtype=F32) / N_HEADS)


def _to_residue_major(x, tgt, after=None):
    s_n, c_n = x.shape
    lr = s_n // N_RES
    extra = [] if after is None else [after]

    def body(x_ref, t_ref, *rest):
        xo_ref, to_ref = rest[-2:]
        for r in range(N_RES):
            xo_ref[r] = x_ref[pl.ds(r, lr, stride=N_RES), :]
            to_ref[r] = t_ref[pl.ds(r, lr, stride=N_RES), :]

    nat = pl.BlockSpec((s_n, LANES), lambda j: (0, j))
    res = pl.BlockSpec((N_RES, lr, LANES), lambda j: (0, 0, j))
    xo, to = pl.pallas_call(
        body, grid=(c_n // LANES,),
        in_specs=[nat, nat] + [pl.BlockSpec((8, LANES), lambda j: (0, 0))] * len(extra),
        out_specs=[res, res],
        out_shape=[jax.ShapeDtypeStruct((N_RES, lr, c_n), F32)] * 2,
        name="perm_in",
    )(x, tgt, *extra)
    return xo.reshape(s_n, c_n), to.reshape(s_n, c_n)


def _to_natural(gx_lo, gx_hi):
    half_rows, c_n = gx_lo.shape
    lr = half_rows // (N_RES // 2)

    def body(lo_ref, hi_ref, o_ref):
        for r in range(N_RES):
            o_ref[pl.ds(r, lr, stride=N_RES), :] = lo_ref[r] if r < N_RES // 2 else hi_ref[r - N_RES // 2]

    half = pl.BlockSpec((N_RES // 2, lr, LANES), lambda j: (0, 0, j))
    return pl.pallas_call(
        body, grid=(c_n // LANES,),
        in_specs=[half, half],
        out_specs=pl.BlockSpec((2 * half_rows, LANES), lambda j: (0, j)),
        out_shape=jax.ShapeDtypeStruct((2 * half_rows, c_n), F32),
        name="perm_out",
    )(gx_lo.reshape(N_RES // 2, lr, c_n), gx_hi.reshape(N_RES // 2, lr, c_n))


def _rms_in(xp, norm_g):
    s_n, c_n = xp.shape
    tm = 512

    def body(x_ref, g_ref, u_ref, ut_ref):
        x = x_ref[...]
        r = lax.rsqrt(jnp.mean(x * x, axis=-1, keepdims=True) + EPS)
        u = x * r * g_ref[...]
        u_ref[...] = u.astype(u_ref.dtype)
        ut_ref[...] = u.T.astype(ut_ref.dtype)

    return pl.pallas_call(
        body, grid=(s_n // tm,),
        in_specs=[pl.BlockSpec((tm, c_n), lambda i: (i, 0)), pl.BlockSpec((1, c_n), lambda i: (0, 0))],
        out_specs=[pl.BlockSpec((tm, c_n), lambda i: (i, 0)), pl.BlockSpec((c_n, tm), lambda i: (0, i))],
        out_shape=[jax.ShapeDtypeStruct((s_n, c_n), ACT_DTYPE), jax.ShapeDtypeStruct((c_n, s_n), ACT_DTYPE)],
        name="rms_in",
    )(xp, norm_g)


def _in_proj(u, chip, w_own=None, w4=None, partial=None, others=()):
    s_n = u.shape[0]
    tn, cm = 512, 512
    per = SHARD_COLS // tn
    own = partial is None

    def body(chip_ref, a_ref, b_ref, *rest):
        o_ref = rest[-1]
        b = b_ref[...]
        for c in range(s_n // cm):
            o_ref[c * cm:(c + 1) * cm, :] = _dot(a_ref[c * cm:(c + 1) * cm, :], b).astype(o_ref.dtype)

    def shard(n, chip_ref):
        if own:
            return chip_ref[0]
        mask = others[-1]
        for i, m in enumerate(others[:-1]):
            mask = jnp.where(n // per == i, m, mask)
        return jnp.bitwise_xor(chip_ref[0], mask)

    w_spec = (pl.BlockSpec((D_MODEL, tn), lambda n, c: (0, n)) if own else
              pl.BlockSpec((None, D_MODEL, tn), lambda n, c: (shard(n, c), 0, n % per)))
    return pl.pallas_call(
        body,
        grid_spec=pltpu.PrefetchScalarGridSpec(
            num_scalar_prefetch=1, grid=(per if own else len(others) * per,),
            in_specs=[pl.BlockSpec((s_n, D_MODEL), lambda n, c: (0, 0)), w_spec] + ([] if own else [_ANY]),
            out_specs=pl.BlockSpec((s_n, tn), lambda n, c: (0, shard(n, c) * per + n % per))),
        out_shape=jax.ShapeDtypeStruct((s_n, IN_COLS), ACT_DTYPE),
        input_output_aliases={} if own else {3: 0},
        name="in_proj_own" if own else "in_proj_" + "_".join(str(m) for m in others),
    )(*([chip, u, w_own] if own else [chip, u, w4, partial]))


def _conv_terms(xc_ref, cg_ref, r, row, lr, cache):
    def a_of(q):
        if q not in cache:
            cache[q] = cg_ref[q].astype(F32) * xc_ref[q].astype(F32)
        return cache[q]

    def shift_down(v):
        return jnp.where(row >= 1, pltpu.roll(v, 1, 0), 0.0)

    a = a_of(r)
    am1 = a_of(r - 1) if r >= 1 else shift_down(a_of(N_RES - 1))
    am2 = a_of(r - 2) if r >= 2 else shift_down(a_of(N_RES - 2 + r))
    return a, am1, am2


def _conv_fwd(proj, conv_w):
    s_n = proj.shape[0]
    lr = s_n // N_RES
    pv = proj.reshape(N_RES, lr, IN_COLS)

    def body(xc_ref, bg_ref, cg_ref, zc_ref, w_ref, hc_ref, hct_ref):
        w = w_ref[...]
        row = lax.broadcasted_iota(jnp.int32, (lr, LANES), 0)
        products = {}
        for r in range(N_RES):
            a, am1, am2 = _conv_terms(xc_ref, cg_ref, r, row, lr, products)
            c = w[0:1] * am2 + w[1:2] * am1 + w[2:3] * a
            z = zc_ref[r].astype(F32)
            hc = z * _sigmoid(z) * bg_ref[r].astype(F32) * c
            hc_ref[r] = hc.astype(hc_ref.dtype)
            hct_ref[:, r * lr:(r + 1) * lr] = hc.T.astype(hct_ref.dtype)

    def col(part):
        return pl.BlockSpec((N_RES, lr, LANES), lambda j: (0, 0, part * 8 + j))

    hc, hct = pl.pallas_call(
        body, grid=(D_MODEL // LANES,),
        in_specs=[col(0), col(1), col(2), col(3), pl.BlockSpec((3, LANES), lambda j: (0, j))],
        out_specs=[pl.BlockSpec((N_RES, lr, LANES), lambda j: (0, 0, j)),
                   pl.BlockSpec((LANES, s_n), lambda j: (j, 0))],
        out_shape=[jax.ShapeDtypeStruct((N_RES, lr, D_MODEL), ACT_DTYPE),
                   jax.ShapeDtypeStruct((D_MODEL, s_n), ACT_DTYPE)],
        name="conv_fwd",
    )(pv, pv, pv, pv, conv_w)
    return hc.reshape(s_n, D_MODEL), hct


RES_PER_STEP = 8
ATTN_BATCH = 16

_BNT = (((2,), (2,)), ((0,), (0,)))
_BNN = (((2,), (1,)), ((0,), (0,)))


def _bdot(a, b, dims):
    return lax.dot_general(a.astype(MXU_DTYPE), b.astype(MXU_DTYPE), dims, preferred_element_type=F32)


def _pattern_view_shape(s_n, c_n, g_n, lead=()):
    lr = s_n // N_RES
    return (*lead, 4, 4, lr, c_n) if g_n == 4 else (*lead, N_RES, lr, c_n)


def _pattern_view(a, g_n, lead=()):
    return a.reshape(_pattern_view_shape(a.shape[-2], a.shape[-1], g_n, lead))


def _pattern_grid(g_n):
    return (N_RES // RES_PER_STEP if g_n == 1 else N_RES // g_n, HP)


def _pattern_spec(g_n, lr, col_of_hp, lead=()):
    z = (0,) * len(lead)
    if g_n == 16:
        return pl.BlockSpec((*lead, 16, lr, LANES), lambda r, hp: (*z, 0, 0, col_of_hp(hp)))
    if g_n == 4:
        return pl.BlockSpec((*lead, 4, None, lr, LANES), lambda r, hp: (*z, 0, r, 0, col_of_hp(hp)))
    return pl.BlockSpec((*lead, RES_PER_STEP, lr, LANES), lambda r, hp: (*z, r, 0, col_of_hp(hp)))


def _aligned(start, m):
    return start if isinstance(start, int) else pl.multiple_of(start, m)


class _Units:
    def __init__(self, g_n, rq):
        self.g_n, self.rq = g_n, rq
        self.per_res, self.paired = g_n == 1, rq == 8

    def plan(self, lr, size):
        if self.per_res:
            return [0], lr // self.rq - 1, lambda j: [pl.multiple_of(j * self.rq, self.rq)]
        step = 16 if self.paired else self.rq
        per = min(size // 2 if self.paired else size, lr // step)
        assert (lr // step) % per == 0
        return ([i * step for i in range(per)], lr // step // per - 1,
                lambda j: [pl.multiple_of((j * per + i) * step, step) for i in range(per)])

    def count(self, qs):
        return RES_PER_STEP if self.per_res else len(qs) * (2 if self.paired else 1)

    def _split(self, tiles, lo, rows):
        return tiles[:, lo:lo + rows].reshape(self.g_n * rows, LANES)

    def load_q(self, ref, qs):
        rq = self.rq
        if self.per_res:
            return ref[:, pl.ds(qs[0], rq), :]
        if self.paired:
            tiles = [ref[:, pl.ds(q, 16), :].astype(F32) for q in qs]
            return jnp.stack([self._split(t, lo, 8) for t in tiles for lo in (0, 8)])
        return jnp.stack([ref[:, pl.ds(q, rq), :].reshape(self.g_n * rq, LANES) for q in qs])

    def _key_rows(self, q, at_start):
        return (0, 2 * self.rq) if at_start else (_aligned(q - self.rq, self.rq), 2 * self.rq)

    def load_k(self, ref, qs, first):
        rq = self.rq
        if self.per_res:
            return ref[:, pl.ds(0, rq), :] if first else ref[:, pl.ds(_aligned(qs[0] - rq, rq), 2 * rq), :]
        if self.paired:
            out = []
            for i, q in enumerate(qs):
                if first and i == 0:
                    t = ref[:, 0:16, :].astype(F32)
                    out += [self._split(t, 0, 16)] * 2
                else:
                    t = ref[:, pl.ds(_aligned(q - 16, 16), 32), :].astype(F32)
                    out += [self._split(t, 8, 16), self._split(t, 16, 16)]
            return jnp.stack(out)
        rows = [self._key_rows(q, first and i == 0) for i, q in enumerate(qs)]
        return jnp.stack([ref[:, pl.ds(k0, n), :].reshape(self.g_n * n, LANES) for k0, n in rows])

    def store_q(self, ref, qs, val, add=False, lead=()):
        if self.per_res:
            pieces = [(qs[0], self.rq, val)]
        elif self.paired:
            pieces = [(q, 16, jnp.concatenate([val[2 * i].reshape(self.g_n, 8, LANES),
                                               val[2 * i + 1].reshape(self.g_n, 8, LANES)], axis=1))
                      for i, q in enumerate(qs)]
        else:
            pieces = [(q, self.rq, val[i].reshape(self.g_n, self.rq, LANES)) for i, q in enumerate(qs)]
        for start, rows, v in pieces:
            idx = (*lead, slice(None), pl.ds(start, rows), slice(None))
            ref[idx] = (ref[idx] + v if add else v).astype(ref.dtype)

    def add_k(self, ref, qs, val, first):
        rq = self.rq
        if self.per_res:
            k0, n = (0, rq) if first else (_aligned(qs[0] - rq, rq), 2 * rq)
            ref[:, pl.ds(k0, n), :] += val
            return
        if self.paired:
            starts = [s for i, q in enumerate(qs)
                      for s in ((0, 0) if first and i == 0 else (_aligned(q - 8, 8), q))]
            rows = [(s, 16) for s in starts]
        else:
            rows = [self._key_rows(q, first and i == 0) for i, q in enumerate(qs)]
        for b, (k0, n) in enumerate(rows):
            ref[:, pl.ds(k0, n), :] += val[b].reshape(self.g_n, n, LANES)


def _batch_bias(un, qs, at_start, first_ref, general_ref):
    if not at_start:
        return general_ref[...][None]
    if un.per_res:
        return first_ref[...][None]
    return jnp.concatenate([first_ref[...][None]] + [general_ref[...][None]] * (un.count(qs) - 1), axis=0)


def _stack_heads(x, low):
    zero = jnp.zeros_like(x)
    return jnp.concatenate([jnp.where(low, x, zero), jnp.where(low, zero, x)], axis=1)


def _attn_fwd(proj, slopes, d):
    g_n, rq = PATTERNS[d]
    un = _Units(g_n, rq)
    s_n = proj.shape[0]
    lr = s_n // N_RES
    q_n = g_n * rq
    d0, m0, d1, m1 = _attn_tables(d)
    first, n_more, later = un.plan(lr, ATTN_BATCH)

    def body(sl_ref, q_ref, k_ref, v_ref, d0_ref, m0_ref, d1_ref, m1_ref, o_ref, lse_ref, b0_ref, b1_ref):
        hp = pl.program_id(1)

        @pl.when(hp == 0)
        def _():
            lse_ref[...] = jnp.zeros(lse_ref.shape, F32)

        for h in (0, 1):
            slope = sl_ref[2 * hp + h]
            b0_ref[h * q_n:(h + 1) * q_n, :] = m0_ref[...] - slope * d0_ref[...]
            b1_ref[h * q_n:(h + 1) * q_n, :] = m1_ref[...] - slope * d1_ref[...]

        lane = lax.broadcasted_iota(jnp.int32, (1, q_n, LANES), 2)
        low = lane < HEAD_DIM
        grp = lane // 8

        def batch(qs, at_start):
            qq = _stack_heads(un.load_q(q_ref, qs) * 0.125, low)
            s = _bdot(qq, un.load_k(k_ref, qs, at_start), _BNT) + _batch_bias(un, qs, at_start, b0_ref, b1_ref)
            m = jnp.max(s, axis=2, keepdims=True)
            p = jnp.exp(s - m)
            l = jnp.sum(p, axis=2, keepdims=True)
            o = _bdot(p, un.load_k(v_ref, qs, at_start), _BNN) * (1.0 / l)
            lse = m + jnp.log(l)
            un.store_q(o_ref, qs, jnp.where(low, o[:, :q_n], o[:, q_n:]))
            upd = jnp.where(grp == 2 * hp, lse[:, :q_n], 0.0) + jnp.where(grp == 2 * hp + 1, lse[:, q_n:], 0.0)
            un.store_q(lse_ref, qs, upd, add=True)

        batch(first, True)

        def more(j, carry):
            batch(later(j), False)
            return carry

        lax.fori_loop(1, 1 + n_more, more, 0)

    pv = _pattern_view(proj, g_n)
    full = lambda a: pl.BlockSpec(a.shape, lambda r, hp: (0, 0))
    o, lse = pl.pallas_call(
        body, grid=_pattern_grid(g_n),
        in_specs=[pl.BlockSpec(memory_space=pltpu.SMEM),
                  _pattern_spec(g_n, lr, lambda hp: 32 + hp),
                  _pattern_spec(g_n, lr, lambda hp: 40 + hp),
                  _pattern_spec(g_n, lr, lambda hp: 48 + hp),
                  full(d0), full(m0), full(d1), full(m1)],
        out_specs=[_pattern_spec(g_n, lr, lambda hp: hp), _pattern_spec(g_n, lr, lambda hp: 0)],
        out_shape=[jax.ShapeDtypeStruct(_pattern_view_shape(s_n, D_MODEL, g_n), ACT_DTYPE),
                   jax.ShapeDtypeStruct(_pattern_view_shape(s_n, LANES, g_n), F32)],
        scratch_shapes=[pltpu.VMEM((2 * q_n, d0.shape[1]), F32), pltpu.VMEM((2 * q_n, 2 * q_n), F32)],
        name=f"attn_fwd_d{d}",
    )(slopes, pv, pv, pv, d0, m0, d1, m1)
    return o.reshape(s_n, D_MODEL), lse.reshape(s_n, LANES)


def _attn_combine(outs, lses, proj):
    s_n = proj.shape[0]
    tm = 512

    def body(o1_ref, o2_ref, o3_ref, l1_ref, l2_ref, l3_ref, za_ref, e_ref, o_ref, lse_ref, ha_ref, hat_ref):
        ls = [l1_ref[...], l2_ref[...], l3_ref[...]]
        mx = jnp.maximum(jnp.maximum(ls[0], ls[1]), ls[2])
        den = sum(jnp.exp(l - mx) for l in ls)
        lse = mx + jnp.log(den)
        lse_ref[...] = lse
        o = jnp.zeros((tm, D_MODEL), F32)
        for l, oref in zip(ls, (o1_ref, o2_ref, o3_ref)):
            o = o + _select_cols(jnp.exp(l - lse), e_ref[...], terms=2) * oref[...].astype(F32)
        o_ref[...] = o.astype(o_ref.dtype)
        z = za_ref[...].astype(F32)
        ha = z * _sigmoid(z) * o
        ha_ref[...] = ha.astype(ha_ref.dtype)
        hat_ref[...] = ha.T.astype(hat_ref.dtype)

    row = lambda w: pl.BlockSpec((tm, w), lambda i: (i, 0))
    return pl.pallas_call(
        body, grid=(s_n // tm,),
        in_specs=[row(D_MODEL)] * 3 + [row(LANES)] * 3
        + [pl.BlockSpec((tm, D_MODEL), lambda i: (i, 7)), pl.BlockSpec((LANES, D_MODEL), lambda i: (0, 0))],
        out_specs=[row(D_MODEL), row(LANES), row(D_MODEL), pl.BlockSpec((D_MODEL, tm), lambda i: (0, i))],
        out_shape=[jax.ShapeDtypeStruct((s_n, D_MODEL), ACT_DTYPE), jax.ShapeDtypeStruct((s_n, LANES), F32),
                   jax.ShapeDtypeStruct((s_n, D_MODEL), ACT_DTYPE), jax.ShapeDtypeStruct((D_MODEL, s_n), ACT_DTYPE)],
        name="attn_combine",
    )(*outs, *lses, proj, _head_expand_matrix())


CHAIN_ROWS = 256


def _row_chains(tm):
    return [slice(r, r + CHAIN_ROWS) for r in range(0, tm, CHAIN_ROWS)]


def _gates(gc_ref, ga_ref, b_ref, rows):
    b = b_ref[...]
    gc = _sigmoid(gc_ref[rows, :].astype(F32) + b[:, :D_MODEL])
    ga = _sigmoid(ga_ref[rows, :].astype(F32) + b[:, D_MODEL:])
    return gc, ga


def _merge_loss(hc, ha, woc, woa, wo, proj, b_merge, xp, final_g, tgt):
    s_n = xp.shape[0]
    tm = 512

    def body(hc_ref, ha_ref, woc_ref, woa_ref, wo_ref, gc_ref, ga_ref, b_ref, x_ref, gf_ref, t_ref,
             yc_ref, ya_ref, dhb_ref, dgf_ref, loss_ref, dwo_ref, mgt_ref):
        i = pl.program_id(0)

        @pl.when(i == 0)
        def _():
            dgf_ref[...] = jnp.zeros(dgf_ref.shape, F32)
            loss_ref[...] = jnp.zeros(loss_ref.shape, F32)
            dwo_ref[...] = jnp.zeros(dwo_ref.shape, F32)

        gf = gf_ref[...]
        for rows in _row_chains(tm):
            yc = _dot(hc_ref[rows, :], woc_ref[...])
            ya = _dot(ha_ref[rows, :], woa_ref[...])
            gc, ga = _gates(gc_ref, ga_ref, b_ref, rows)
            mg = gc * yc + ga * ya
            yc_ref[rows, :] = yc.astype(yc_ref.dtype)
            ya_ref[rows, :] = ya.astype(ya_ref.dtype)
            mgt_ref[:, rows] = mg.T.astype(mgt_ref.dtype)
            h2 = x_ref[rows, :] + _dot(mg, wo_ref[...])
            r2 = lax.rsqrt(jnp.mean(h2 * h2, axis=-1, keepdims=True) + EPS)
            nrm = h2 * r2
            err = nrm * gf - t_ref[rows, :]
            e2 = (err * err).reshape(-1, 8, D_MODEL).sum(axis=0)
            loss_ref[...] += sum(e2[:, c * LANES:(c + 1) * LANES] for c in range(D_MODEL // LANES))
            dy = err * (1.0 / D_MODEL)
            dgf_ref[...] += jnp.sum(dy * nrm, axis=0, keepdims=True)
            dn = dy * gf
            dh2 = r2 * (dn - nrm * jnp.mean(dn * nrm, axis=-1, keepdims=True))
            dhb_ref[rows, :] = dh2.astype(dhb_ref.dtype)
        dwo_ref[...] += _dot(mgt_ref[...], dhb_ref[...])

    row = pl.BlockSpec((tm, D_MODEL), lambda i: (i, 0))
    wsp = pl.BlockSpec((D_MODEL, D_MODEL), lambda i: (0, 0), pipeline_mode=pl.Buffered(1))
    vec = lambda w: pl.BlockSpec((1, w), lambda i: (0, 0))
    act = jax.ShapeDtypeStruct((s_n, D_MODEL), ACT_DTYPE)
    return pl.pallas_call(
        body, grid=(s_n // tm,),
        in_specs=[row, row, wsp, wsp, wsp,
                  pl.BlockSpec((tm, D_MODEL), lambda i: (i, 8)), pl.BlockSpec((tm, D_MODEL), lambda i: (i, 9)),
                  vec(2 * D_MODEL), row, vec(D_MODEL), row],
        out_specs=[row, row, row, vec(D_MODEL), pl.BlockSpec((8, LANES), lambda i: (0, 0)),
                   pl.BlockSpec((D_MODEL, D_MODEL), lambda i: (0, 0))],
        out_shape=[act, act, act, jax.ShapeDtypeStruct((1, D_MODEL), F32), jax.ShapeDtypeStruct((8, LANES), F32),
                   jax.ShapeDtypeStruct((D_MODEL, D_MODEL), F32)],
        scratch_shapes=[pltpu.VMEM((D_MODEL, tm), MXU_DTYPE)],
        name="merge_loss",
    )(hc, ha, woc, woa, wo, proj, proj, b_merge, xp, final_g, tgt)


def _merge_bwd(dh2b, wo, woc, woa, yc, ya, proj, b_merge, o, hct, hat):
    s_n = dh2b.shape[0]
    tm = 512

    def body(dh_ref, wo_ref, woc_ref, woa_ref, yc_ref, ya_ref, gc_ref, ga_ref, b_ref, o_ref, za_ref, e_ref,
             hct_ref, hat_ref, dhc_ref, do_ref, dsum_ref, db3_ref, dbias_ref, dwoc_ref, dwoa_ref,
             dyc_ref, dya_ref):
        i = pl.program_id(0)

        @pl.when(i == 0)
        def _():
            dbias_ref[...] = jnp.zeros(dbias_ref.shape, F32)
            dwoc_ref[...] = jnp.zeros(dwoc_ref.shape, F32)
            dwoa_ref[...] = jnp.zeros(dwoa_ref.shape, F32)

        for rows in _row_chains(tm):
            dmg = _dot_nt(dh_ref[rows, :], wo_ref[...])
            gc, ga = _gates(gc_ref, ga_ref, b_ref, rows)
            dgc = dmg * yc_ref[rows, :].astype(F32) * gc * (1.0 - gc)
            dga = dmg * ya_ref[rows, :].astype(F32) * ga * (1.0 - ga)
            dbias_ref[:, :D_MODEL] += jnp.sum(dgc, axis=0, keepdims=True)
            dbias_ref[:, D_MODEL:] += jnp.sum(dga, axis=0, keepdims=True)
            dyc = dmg * gc
            dya = dmg * ga
            dyc_ref[rows, :] = dyc.astype(dyc_ref.dtype)
            dya_ref[rows, :] = dya.astype(dya_ref.dtype)
            dhc_ref[rows, :] = _dot_nt(dyc, woc_ref[...]).astype(dhc_ref.dtype)
            dha = _dot_nt(dya, woa_ref[...])
            z = za_ref[rows, :].astype(F32)
            sg = _sigmoid(z)
            ov = o_ref[rows, :].astype(F32)
            dout = dha * z * sg
            do_ref[rows, :] = dout.astype(do_ref.dtype)
            dsum_ref[rows, :] = _select_cols(dout * ov, e_ref[...], terms=2)
            db3_ref[0, rows, :] = (dha * ov * sg * (1.0 + z * (1.0 - sg))).astype(db3_ref.dtype)
            db3_ref[1, rows, :] = dgc.astype(db3_ref.dtype)
            db3_ref[2, rows, :] = dga.astype(db3_ref.dtype)
        dwoc_ref[...] += _dot(hct_ref[...], dyc_ref[...])
        dwoa_ref[...] += _dot(hat_ref[...], dya_ref[...])

    row = pl.BlockSpec((tm, D_MODEL), lambda i: (i, 0))
    col = pl.BlockSpec((D_MODEL, tm), lambda i: (0, i))
    wsp = pl.BlockSpec((D_MODEL, D_MODEL), lambda i: (0, 0), pipeline_mode=pl.Buffered(1))
    acc = pl.BlockSpec((D_MODEL, D_MODEL), lambda i: (0, 0))
    act = jax.ShapeDtypeStruct((s_n, D_MODEL), ACT_DTYPE)
    grad = jax.ShapeDtypeStruct((D_MODEL, D_MODEL), F32)
    return pl.pallas_call(
        body, grid=(s_n // tm,),
        in_specs=[row, wsp, wsp, wsp, row, row,
                  pl.BlockSpec((tm, D_MODEL), lambda i: (i, 8)), pl.BlockSpec((tm, D_MODEL), lambda i: (i, 9)),
                  pl.BlockSpec((1, 2 * D_MODEL), lambda i: (0, 0)), row,
                  pl.BlockSpec((tm, D_MODEL), lambda i: (i, 7)), pl.BlockSpec((D_MODEL, LANES), lambda i: (0, 0)),
                  col, col],
        out_specs=[row, row, pl.BlockSpec((tm, LANES), lambda i: (i, 0)),
                   pl.BlockSpec((3, tm, D_MODEL), lambda i: (0, i, 0)),
                   pl.BlockSpec((1, 2 * D_MODEL), lambda i: (0, 0)), acc, acc],
        out_shape=[act, act, jax.ShapeDtypeStruct((s_n, LANES), F32),
                   jax.ShapeDtypeStruct((3, s_n, D_MODEL), ACT_DTYPE),
                   jax.ShapeDtypeStruct((1, 2 * D_MODEL), F32), grad, grad],
        scratch_shapes=[pltpu.VMEM((tm, D_MODEL), MXU_DTYPE), pltpu.VMEM((tm, D_MODEL), MXU_DTYPE)],
        name="merge_bwd",
    )(dh2b, wo, woc, woa, yc, ya, proj, proj, b_merge, o, proj, _head_sum_matrix(), hct, hat)


def _conv_bwd(proj, conv_w, dhc):
    s_n = proj.shape[0]
    lr = s_n // N_RES
    pv = proj.reshape(N_RES, lr, IN_COLS)

    def body(xc_ref, bg_ref, cg_ref, zc_ref, w_ref, dhc_ref, da4_ref, dw_ref, dc_ref):
        w = w_ref[...]
        row = lax.broadcasted_iota(jnp.int32, (lr, LANES), 0)
        dw = [jnp.zeros((1, LANES), F32) for _ in range(3)]
        products = {}
        for r in range(N_RES):
            a, am1, am2 = _conv_terms(xc_ref, cg_ref, r, row, lr, products)
            c = w[0:1] * am2 + w[1:2] * am1 + w[2:3] * a
            z = zc_ref[r].astype(F32)
            sg = _sigmoid(z)
            sz = z * sg
            bg = bg_ref[r].astype(F32)
            dh = dhc_ref[r].astype(F32)
            da4_ref[1, r] = (dh * sz * c).astype(da4_ref.dtype)
            da4_ref[3, r] = (dh * bg * c * sg * (1.0 + z * (1.0 - sg))).astype(da4_ref.dtype)
            dc = dh * sz * bg
            dc_ref[r] = dc
            dw[0] = dw[0] + jnp.sum(dc * am2, axis=0, keepdims=True)
            dw[1] = dw[1] + jnp.sum(dc * am1, axis=0, keepdims=True)
            dw[2] = dw[2] + jnp.sum(dc * a, axis=0, keepdims=True)
        dw_ref[0:1, :] = dw[0]
        dw_ref[1:2, :] = dw[1]
        dw_ref[2:3, :] = dw[2]

        def shift_up(v):
            return jnp.where(row < lr - 1, pltpu.roll(v, lr - 1, 0), 0.0)

        for r in range(N_RES):
            dp1 = dc_ref[r + 1] if r + 1 < N_RES else shift_up(dc_ref[0])
            dp2 = dc_ref[r + 2] if r + 2 < N_RES else shift_up(dc_ref[r + 2 - N_RES])
            da = w[2:3] * dc_ref[r] + w[1:2] * dp1 + w[0:1] * dp2
            da4_ref[0, r] = (da * cg_ref[r].astype(F32)).astype(da4_ref.dtype)
            da4_ref[2, r] = (da * xc_ref[r].astype(F32)).astype(da4_ref.dtype)

    def col(part):
        return pl.BlockSpec((N_RES, lr, LANES), lambda j: (0, 0, part * 8 + j))

    da4, dw = pl.pallas_call(
        body, grid=(D_MODEL // LANES,),
        in_specs=[col(0), col(1), col(2), col(3), pl.BlockSpec((3, LANES), lambda j: (0, j)),
                  pl.BlockSpec((N_RES, lr, LANES), lambda j: (0, 0, j))],
        out_specs=[pl.BlockSpec((4, N_RES, lr, LANES), lambda j: (0, 0, 0, j)),
                   pl.BlockSpec((3, LANES), lambda j: (0, j))],
        out_shape=[jax.ShapeDtypeStruct((4, N_RES, lr, D_MODEL), ACT_DTYPE),
                   jax.ShapeDtypeStruct((3, D_MODEL), F32)],
        scratch_shapes=[pltpu.VMEM((N_RES, lr, LANES), F32)],
        name="conv_bwd",
    )(pv, pv, pv, pv, conv_w, dhc.reshape(N_RES, lr, D_MODEL))
    return da4.reshape(4, s_n, D_MODEL), dw


def _attn_bwd(proj, dout, lse, dsum, slopes, d, prev=None):
    g_n, rq = PATTERNS[d]
    un = _Units(g_n, rq)
    s_n = proj.shape[0]
    lr = s_n // N_RES
    q_n = g_n * rq
    d0, m0, d1, m1 = (np.ascontiguousarray(t.T) for t in _attn_tables(d))
    first, n_more, later = un.plan(lr, ATTN_BATCH)
    bsz = un.count(first)
    gd = RES_PER_STEP if un.per_res else g_n

    def body(sl_ref, q_ref, k_ref, v_ref, do_ref, lse_ref, ds_ref, d0_ref, m0_ref, d1_ref, m1_ref, *rest):
        prev_ref = rest[0] if prev is not None else None
        out_ref, b0_ref, b1_ref, lt_ref, dt_ref, dk_ref, dv_ref = rest[-7:]
        hp = pl.program_id(1)
        for h in (0, 1):
            slope = sl_ref[2 * hp + h]
            b0_ref[:, h * q_n:(h + 1) * q_n] = m0_ref[...] - slope * d0_ref[...]
            b1_ref[:, h * q_n:(h + 1) * q_n] = m1_ref[...] - slope * d1_ref[...]
        if prev is None:
            dk_ref[...] = jnp.zeros(dk_ref.shape, F32)
            dv_ref[...] = jnp.zeros(dv_ref.shape, F32)
        else:
            out_ref[0] = prev_ref[0]
            dk_ref[...] = prev_ref[1].astype(F32)
            dv_ref[...] = prev_ref[2].astype(F32)
        low = lax.broadcasted_iota(jnp.int32, (1, q_n, LANES), 2) < HEAD_DIM
        row16 = pl.multiple_of(16 * hp, 16)

        def query_rows(stat_ref, t_ref, qs):
            tiles = un.load_q(stat_ref, qs)
            for b in range(bsz):
                t_ref[b] = tiles[b].T
            t16 = t_ref[:, pl.ds(row16, 16), :]
            return jnp.concatenate([t16[:, 0:1, :], t16[:, 8:9, :]], axis=2)

        def batch(qs, at_start):
            qq = _stack_heads(un.load_q(q_ref, qs) * 0.125, low)
            dd = _stack_heads(un.load_q(do_ref, qs), low)
            ks = un.load_k(k_ref, qs, at_start)
            vs = un.load_k(v_ref, qs, at_start)
            lrow = query_rows(lse_ref, lt_ref, qs)
            drow = query_rows(ds_ref, dt_ref, qs)
            pt = jnp.exp(_bdot(ks, qq, _BNT) + _batch_bias(un, qs, at_start, b0_ref, b1_ref) - lrow)
            dst = pt * (_bdot(vs, dd, _BNT) - drow)
            un.add_k(dv_ref, qs, _bdot(pt, dd, _BNN), at_start)
            un.add_k(dk_ref, qs, _bdot(dst, qq, _BNN), at_start)
            dq = _bdot(jnp.swapaxes(dst, 1, 2), ks, _BNN)
            un.store_q(out_ref, qs, jnp.where(low, dq[:, :q_n], dq[:, q_n:]) * 0.125, add=prev is not None,
                       lead=(0,))

        batch(first, True)

        def more(j, carry):
            batch(later(j), False)
            return carry

        lax.fori_loop(1, 1 + n_more, more, 0)
        out_ref[1] = dk_ref[...].astype(out_ref.dtype)
        out_ref[2] = dv_ref[...].astype(out_ref.dtype)

    pv = _pattern_view(proj, g_n)
    full = lambda a: pl.BlockSpec(a.shape, lambda r, hp: (0, 0))
    whole = _pattern_spec(g_n, lr, lambda hp: hp, lead=(3,))
    out = pl.pallas_call(
        body, grid=_pattern_grid(g_n),
        in_specs=[pl.BlockSpec(memory_space=pltpu.SMEM),
                  _pattern_spec(g_n, lr, lambda hp: 32 + hp),
                  _pattern_spec(g_n, lr, lambda hp: 40 + hp),
                  _pattern_spec(g_n, lr, lambda hp: 48 + hp),
                  _pattern_spec(g_n, lr, lambda hp: hp),
                  _pattern_spec(g_n, lr, lambda hp: 0),
                  _pattern_spec(g_n, lr, lambda hp: 0),
                  full(d0), full(m0), full(d1), full(m1)] + ([] if prev is None else [whole]),
        out_specs=whole,
        out_shape=jax.ShapeDtypeStruct(_pattern_view_shape(s_n, D_MODEL, g_n, lead=(3,)), ACT_DTYPE),
        scratch_shapes=[pltpu.VMEM((d0.shape[0], 2 * q_n), F32), pltpu.VMEM((2 * q_n, 2 * q_n), F32),
                        pltpu.VMEM((bsz, LANES, q_n), F32), pltpu.VMEM((bsz, LANES, q_n), F32),
                        pltpu.VMEM((gd, lr, LANES), F32), pltpu.VMEM((gd, lr, LANES), F32)],
        name=f"attn_bwd_d{d}",
    )(slopes, pv, pv, pv, _pattern_view(dout, g_n), _pattern_view(lse, g_n), _pattern_view(dsum, g_n),
      d0, m0, d1, m1, *([] if prev is None else [_pattern_view(prev, g_n, lead=(3,))]))
    return out.reshape(3, s_n, D_MODEL)


def _part_index(step, per, lo, n):
    return jnp.clip(step // per - lo, 0, n - 1)


def _dw_in(ut, da4, dc3, db3):
    s_n = ut.shape[1]
    tn = 512
    per = D_MODEL // tn
    shard_blocks = SHARD_COLS // tn

    def body(a_ref, p0_ref, p1_ref, p2_ref, o_ref):
        part = pl.program_id(0) // per

        @pl.when(part < 4)
        def _():
            o_ref[...] = _dot(a_ref[...], p0_ref[...])

        @pl.when((part >= 4) & (part < 7))
        def _():
            o_ref[...] = _dot(a_ref[...], p1_ref[...])

        @pl.when(part >= 7)
        def _():
            o_ref[...] = _dot(a_ref[...], p2_ref[...])

    def pspec(lo, n):
        def index(j):
            part = j // per
            col = jnp.where(part < lo, 0, jnp.where(part >= lo + n, per - 1, j % per))
            return _part_index(j, per, lo, n), 0, col
        return pl.BlockSpec((None, s_n, tn), index)

    return pl.pallas_call(
        body, grid=(IN_COLS // tn,),
        in_specs=[pl.BlockSpec((D_MODEL, s_n), lambda j: (0, 0), pipeline_mode=pl.Buffered(1)),
                  pspec(0, 4), pspec(4, 3), pspec(7, 3)],
        out_specs=pl.BlockSpec((None, D_MODEL, tn), lambda j: (j // shard_blocks, 0, j % shard_blocks)),
        out_shape=jax.ShapeDtypeStruct((4, D_MODEL, SHARD_COLS), F32),
        name="dw_in",
    )(ut, da4, dc3, db3)


def _input_grad(da4, dc3, db3, w4, xp, norm_g, dh2, row0, rows):
    tm, tk = 256, 512
    per = D_MODEL // tk
    shard_blocks = SHARD_COLS // tk
    m0 = row0 // tm

    def body(p0_ref, p1_ref, p2_ref, w_ref, x_ref, g_ref, dh_ref, gx_ref, dg_ref):
        @pl.when(pl.program_id(0) == 0)
        def _():
            dg_ref[...] = jnp.zeros(dg_ref.shape, F32)

        du = None
        for k in range(IN_COLS // tk):
            part, cols = k // per, pl.ds((k % per) * tk, tk)
            ref, slot = (p0_ref, part) if part < 4 else (p1_ref, part - 4) if part < 7 else (p2_ref, part - 7)
            d = _dot_nt(ref[slot, :, cols], w_ref[k // shard_blocks, :, pl.ds((k % shard_blocks) * tk, tk)])
            du = d if du is None else du + d
        x = x_ref[...]
        r = lax.rsqrt(jnp.mean(x * x, axis=-1, keepdims=True) + EPS)
        nrm = x * r
        dg_ref[...] += jnp.sum(du * nrm, axis=0, keepdims=True)
        dn = du * g_ref[...]
        gx_ref[...] = dh_ref[...].astype(F32) + r * (dn - nrm * jnp.mean(dn * nrm, axis=-1, keepdims=True))

    def pspec(n):
        return pl.BlockSpec((n, tm, D_MODEL), lambda m: (0, m0 + m, 0))

    row_in = pl.BlockSpec((tm, D_MODEL), lambda m: (m0 + m, 0))
    vec = pl.BlockSpec((1, D_MODEL), lambda m: (0, 0))
    return pl.pallas_call(
        body, grid=(rows // tm,),
        in_specs=[pspec(4), pspec(3), pspec(3),
                  pl.BlockSpec(w4.shape, lambda m: (0, 0, 0), pipeline_mode=pl.Buffered(1)),
                  row_in, vec, row_in],
        out_specs=[pl.BlockSpec((tm, D_MODEL), lambda m: (m, 0)), vec],
        out_shape=[jax.ShapeDtypeStruct((rows, D_MODEL), F32), jax.ShapeDtypeStruct((1, D_MODEL), F32)],
        name="input_grad",
    )(da4, dc3, db3, w4, xp, norm_g, dh2)


class _Step:
    def __init__(self, x, tgt, norm_g, chip, after=None):
        self.norm_g, self.chip = norm_g, chip
        self.slopes = _alibi_slopes()
        self.xp, self.tp = _to_residue_major(x, tgt, after)
        self.u, self.ut = _rms_in(self.xp, norm_g)

    def project_own(self, w_own):
        self.proj_own = _in_proj(self.u, self.chip, w_own=w_own)

    def project(self, w4, others):
        self.proj_own = _in_proj(self.u, self.chip, w4=w4, partial=self.proj_own, others=others)

    def mixers(self, w4, taps):
        self.w4, self.taps, self.proj = w4, taps, self.proj_own
        self.hc, self.hct = _conv_fwd(self.proj, taps)
        fwd = [_attn_fwd(self.proj, self.slopes, d) for d in PATTERNS]
        self.o, self.lse, self.ha, self.hat = _attn_combine([f[0] for f in fwd], [f[1] for f in fwd], self.proj)

    def merge_and_loss(self, woc, woa, wo, b_merge, final_g):
        self.woc, self.woa, self.wo, self.b_merge = woc, woa, wo, b_merge
        (self.yc, self.ya, self.dh2b, self.d_final_g, self.loss8, self.d_wo) = _merge_loss(
            self.hc, self.ha, woc, woa, wo, self.proj, b_merge, self.xp, final_g, self.tp)

    def out_weight_grads(self):
        (self.dhc, self.dout, self.dsum, self.db3, self.d_bias, d_woc, d_woa) = _merge_bwd(
            self.dh2b, self.wo, self.woc, self.woa, self.yc, self.ya, self.proj, self.b_merge, self.o,
            self.hct, self.hat)
        return d_woc, d_woa, self.d_wo

    def conv_grads(self, after=0.0):
        self.da4, self.d_taps = _conv_bwd(self.proj, self.taps + after, self.dhc)

    def in_weight_grad(self, after=0.0):
        slopes = self.slopes + after
        self.dc3 = None
        for d in PATTERNS:
            self.dc3 = _attn_bwd(self.proj, self.dout, self.lse, self.dsum, slopes, d, prev=self.dc3)
        return _dw_in(self.ut, self.da4, self.dc3, self.db3)

    def input_grad(self, half, after=0.0):
        rows = self.xp.shape[0] // 2
        return _input_grad(self.da4, self.dc3, self.db3, self.w4, self.xp, self.norm_g + after, self.dh2b,
                           half * rows, rows)


def _local_grads(x, tgt, norm_g, w4, b_merge, conv_w, woc, woa, wo, final_g):
    st = _Step(x, tgt, norm_g, jnp.zeros((1,), jnp.int32))
    st.project_own(w4[0])
    st.project(w4, (2, 1))
    st.project(w4, (3,))
    st.mixers(w4, conv_w)
    st.merge_and_loss(woc, woa, wo, b_merge, final_g)
    d_woc, d_woa, d_wo = st.out_weight_grads()
    st.conv_grads()
    d_w4 = st.in_weight_grad()
    gx_lo, dg_lo = st.input_grad(0)
    gx_hi, dg_hi = st.input_grad(1)
    return (st.loss8, _to_natural(gx_lo, gx_hi), dg_lo + dg_hi, d_w4, st.d_bias, st.d_taps, d_woc, d_woa, d_wo,
            st.d_final_g)


MESH = pl.DeviceIdType.MESH
_CHIP_FLIPS = ((1, 0), (0, 1), (1, 1))
_ANY = pl.BlockSpec(memory_space=pl.ANY)


def _place():
    return lax.axis_index("x"), lax.axis_index("y"), lax.axis_index("c")


def _flip(v, f):
    return 1 - v if f else v


def _remote(src, dst, send_sems, recv_sems, k, device):
    return pltpu.make_async_remote_copy(src_ref=src, dst_ref=dst, send_sem=send_sems.at[k], recv_sem=recv_sems.at[k],
                                        device_id=device, device_id_type=MESH)


def _place_shard(w, chip, dtype):
    rows, cols = w.shape
    tm = min(rows, 128)

    def body(chip_ref, w_ref, o_ref):
        o_ref[0] = w_ref[...].astype(o_ref.dtype)

    return pl.pallas_call(
        body,
        grid_spec=pltpu.PrefetchScalarGridSpec(
            num_scalar_prefetch=1, grid=(rows // tm,),
            in_specs=[pl.BlockSpec((tm, cols), lambda i, chip_ref: (i, 0))],
            out_specs=pl.BlockSpec((1, tm, cols), lambda i, chip_ref: (chip_ref[0], i, 0))),
        out_shape=jax.ShapeDtypeStruct((4, rows, cols), dtype),
        name="place_shard",
    )(chip, w)


def _gather_copies_to(flips, whole=()):
    def copies(arrs, _, send_sems, recv_sems):
        x, y, c = _place()
        out = []
        for a, arr in enumerate(arrs):
            h = arr.shape[1] // 2
            mine = arr.at[2 * x + y] if a in whole else arr.at[2 * x + y, pl.ds(pl.multiple_of(c * h, 8), h)]
            for i, t in enumerate(flips):
                fx, fy = _CHIP_FLIPS[t]
                out.append(_remote(mine, mine, send_sems, recv_sems, len(flips) * a + i,
                                   (_flip(x, fx), _flip(y, fy), c)))
        return out
    return copies


def _forward_to_sibling(arrs, flips=(0, 1, 2)):
    n = len(arrs)

    def body(*refs):
        outs = refs[n:2 * n]
        send_sems, recv_sems = refs[2 * n:]
        x, y, c = _place()
        sibling = (x, y, 1 - c)
        started = []
        for a in range(n):
            h = outs[a].shape[1] // 2
            rows = pl.ds(pl.multiple_of(c * h, 8), h)
            for t in flips:
                fx, fy = _CHIP_FLIPS[t]
                landed = outs[a].at[2 * _flip(x, fx) + _flip(y, fy), rows]
                cp = _remote(landed, landed, send_sems, recv_sems, 3 * a + t, sibling)
                cp.start()
                started.append(cp)
        for a in range(n):
            h = outs[a].shape[1] // 2
            rows = pl.ds(pl.multiple_of((1 - c) * h, 8), h)
            for t in flips:
                fx, fy = _CHIP_FLIPS[t]
                handed = outs[a].at[2 * _flip(x, fx) + _flip(y, fy), rows]
                _remote(handed, handed, send_sems, recv_sems, 3 * a + t, sibling).wait_recv()
        for cp in started:
            cp.wait_send()

    return pl.pallas_call(
        body, in_specs=[_ANY] * n, out_specs=[_ANY] * n,
        out_shape=[jax.ShapeDtypeStruct(s.shape, s.dtype) for s in arrs],
        input_output_aliases={a: a for a in range(n)},
        scratch_shapes=[pltpu.SemaphoreType.DMA((3 * n,)), pltpu.SemaphoreType.DMA((3 * n,))],
        name="gathered_to_sibling_" + "".join(str(t) for t in flips),
    )(*arrs)


_HBM = pl.BlockSpec(memory_space=pltpu.HBM)
_SEM = pl.BlockSpec(memory_space=pltpu.SEMAPHORE)
_EFFECT = pltpu.SideEffectType.DATAFLOW_SIDE_EFFECTING


class _SplitExchange:
    def __init__(self, name, srcs, land_shapes, n_copies, copies, riders=()):
        self.name, self.n, self.nl, self.copies = name, len(srcs), len(land_shapes), copies
        n, nb = self.n, len(srcs) + len(land_shapes)
        lands = [lax.empty(s.shape, s.dtype) for s in land_shapes]
        bufs = [pltpu.with_memory_space_constraint(a, pltpu.HBM) for a in (*srcs, *lands, *riders)]
        na = len(bufs)

        def body(*refs):
            send_sems, recv_sems = refs[na], refs[na + 1]
            for cp in copies(refs[:n], refs[n:nb], send_sems, recv_sems):
                cp.start()
            refs[-1][...] = jnp.zeros(refs[-1].shape, F32)

        outs = pl.pallas_call(
            body, name=name + "_start",
            in_specs=[_HBM] * na,
            out_specs=[_SEM, _SEM] + [_HBM] * na + [pl.BlockSpec(memory_space=pltpu.VMEM)],
            out_shape=[pltpu.SemaphoreType.DMA((n_copies,)), pltpu.SemaphoreType.DMA((n_copies,))]
            + [pltpu.HBM(b.shape, b.dtype) for b in bufs] + [jax.ShapeDtypeStruct((8, LANES), F32)],
            input_output_aliases={i: 2 + i for i in range(na)},
            compiler_params=pltpu.CompilerParams(has_side_effects=_EFFECT),
        )(*bufs)
        self.sems, self.bufs, self.riders, self.token = outs[:2], outs[2:2 + nb], outs[2 + nb:2 + na], outs[-1]

    def after(self):
        return self.token[0, 0]

    def wait(self, done, riders=(), bufs=None):
        n, nb, copies = self.n, self.n + self.nl, self.copies
        bufs = [*(self.bufs if bufs is None else bufs),
                *[pltpu.with_memory_space_constraint(a, pltpu.HBM) for a in riders]]
        na = len(bufs)
        done = list(done) if isinstance(done, (list, tuple)) else [done]

        def body(*refs):
            send_sems, recv_sems = refs[na], refs[na + 1]
            for cp in copies(refs[:n], refs[n:nb], send_sems, recv_sems):
                cp.wait_send()
                cp.wait_recv()

        outs = pl.pallas_call(
            body, name=self.name + "_wait",
            in_specs=[_HBM] * na + [_SEM, _SEM] + [_ANY] * len(done),
            out_specs=[_HBM] * na,
            out_shape=[pltpu.HBM(b.shape, b.dtype) for b in bufs],
            input_output_aliases={i: i for i in range(na)},
            compiler_params=pltpu.CompilerParams(has_side_effects=_EFFECT),
        )(*bufs, *self.sems, *done)
        return outs[:n], outs[n:nb], outs[nb:]


def _sibling_copies(srcs, lands, send_sems, recv_sems):
    x, y, c = _place()
    out = []
    for a, (src, land) in enumerate(zip(srcs, lands)):
        h = src.shape[1] // 2
        theirs = pl.ds(pl.multiple_of((1 - c) * h, 8), h)
        out.append(_remote(src.at[:, theirs], land, send_sems, recv_sems, a, (x, y, 1 - c)))
    return out


def _grads_to_sibling(name, grads):
    shapes = [jax.ShapeDtypeStruct((4, g.shape[1] // 2, g.shape[2]), g.dtype) for g in grads]
    return _SplitExchange(name, grads, shapes, len(grads), _sibling_copies)


def _chip_copies(srcs, lands, send_sems, recv_sems):
    x, y, c = _place()
    out = []
    for a, (src, land) in enumerate(zip(srcs, lands)):
        for t, (fx, fy) in enumerate(_CHIP_FLIPS):
            tx, ty = _flip(x, fx), _flip(y, fy)
            out.append(_remote(src.at[2 * tx + ty], land.at[t], send_sems, recv_sems, 3 * a + t, (tx, ty, c)))
    return out


def _grads_to_chips(name, parts):
    shapes = [jax.ShapeDtypeStruct((3, *p.shape[1:]), p.dtype) for p in parts]
    return _SplitExchange(name, parts, shapes, 3 * len(parts), _chip_copies)


def _add_halves(g, r, half):
    _, rows, cols = g.shape
    h = rows // 2
    tm = min(h, 128)
    nt = h // tm

    def body(half_ref, g_ref, r_ref, b_ref):
        b_ref[...] = (g_ref[...] + r_ref[...]).astype(b_ref.dtype)

    spec = pl.BlockSpec((1, tm, cols), lambda j, i, half_ref: (j, i, 0))
    return pl.pallas_call(
        body,
        grid_spec=pltpu.PrefetchScalarGridSpec(
            num_scalar_prefetch=1, grid=(4, nt),
            in_specs=[pl.BlockSpec((1, tm, cols), lambda j, i, half_ref: (j, half_ref[0] * nt + i, 0)), spec],
            out_specs=spec),
        out_shape=jax.ShapeDtypeStruct((4, h, cols), BF16),
        name="add_sibling_grads",
    )(half, g, r)


def _add_chips(g, r, recv, where):
    _, h, cols = r.shape
    tm = min(h, 128)
    nt = h // tm

    def body(where_ref, g_ref, r_ref, recv_ref, out_ref):
        own = g_ref[0] + r_ref[0]
        out_ref[...] = ((own + recv_ref[0].astype(F32)) + recv_ref[1].astype(F32)) + recv_ref[2].astype(F32)

    return pl.pallas_call(
        body,
        grid_spec=pltpu.PrefetchScalarGridSpec(
            num_scalar_prefetch=1, grid=(nt,),
            in_specs=[pl.BlockSpec((1, tm, cols), lambda i, w: (w[0], w[1] * nt + i, 0)),
                      pl.BlockSpec((1, tm, cols), lambda i, w: (w[0], i, 0)),
                      pl.BlockSpec((3, tm, cols), lambda i, w: (0, i, 0))],
            out_specs=pl.BlockSpec((tm, cols), lambda i, w: (w[1] * nt + i, 0))),
        out_shape=jax.ShapeDtypeStruct((2 * h, cols), F32),
        name="add_chip_grads",
    )(where, g, r, recv)


def _share_halves(shards):
    n = len(shards)

    def body(*refs):
        outs = refs[n:2 * n]
        send_sems, recv_sems = refs[2 * n:]
        x, y, c = _place()
        copies = []
        for a in range(n):
            h = outs[a].shape[0] // 2
            mine = outs[a].at[pl.ds(pl.multiple_of(c * h, 8), h)]
            copies.append(_remote(mine, mine, send_sems, recv_sems, a, (x, y, 1 - c)))
        for cp in copies:
            cp.start()
        for a, cp in enumerate(copies):
            cp.wait_send()
            h = outs[a].shape[0] // 2
            theirs = outs[a].at[pl.ds(pl.multiple_of((1 - c) * h, 8), h)]
            _remote(theirs, theirs, send_sems, recv_sems, a, (x, y, 1 - c)).wait_recv()

    return pl.pallas_call(
        body, in_specs=[_ANY] * n, out_specs=[_ANY] * n,
        out_shape=[jax.ShapeDtypeStruct(p.shape, p.dtype) for p in shards],
        input_output_aliases={a: a for a in range(n)},
        scratch_shapes=[pltpu.SemaphoreType.DMA((n,)), pltpu.SemaphoreType.DMA((n,))],
        name="share_reduced_halves",
    )(*shards)


def _reduce_small(rows):
    cols = rows[0].shape[1]
    n = len(rows)
    assert sum(r.shape[0] for r in rows) <= 8

    def body(*refs):
        ins, out_ref = refs[:n], refs[n]
        vec_ref, gath_ref, send_sems, recv_sems = refs[n + 1:]
        x, y, c = _place()
        me = 4 * x + 2 * y + c
        vec_ref[...] = jnp.zeros(vec_ref.shape, F32)
        at = 0
        for r in ins:
            vec_ref[at:at + r.shape[0], :] = r[...]
            at += r.shape[0]
        copies = []
        for k in range(1, 8):
            peer = (_flip(x, (k >> 2) & 1), _flip(y, (k >> 1) & 1), _flip(c, k & 1))
            copies.append(_remote(vec_ref, gath_ref.at[me], send_sems, recv_sems, k - 1, peer))
        for cp in copies:
            cp.start()
        gath_ref[me] = vec_ref[...]
        for cp in copies:
            cp.wait()
        tot = gath_ref[0]
        for dev in range(1, 8):
            tot = tot + gath_ref[dev]
        out_ref[...] = tot
        out_ref[7:8, :] = jnp.zeros((1, cols), F32) + jnp.sum(tot[7:8, :])

    vm = pl.BlockSpec(memory_space=pltpu.VMEM)
    return pl.pallas_call(
        body, in_specs=[vm] * n, out_specs=vm,
        out_shape=jax.ShapeDtypeStruct((8, cols), F32),
        scratch_shapes=[pltpu.VMEM((8, cols), F32), pltpu.VMEM((8, 8, cols), F32),
                        pltpu.SemaphoreType.DMA((7,)), pltpu.SemaphoreType.DMA((7,))],
        name="reduce_small",
    )(*rows)


def _adamw(w, g, m, v, name):
    rows, cols = w.shape
    tm = 128 if rows % 128 == 0 else rows

    def body(w_ref, g_ref, m_ref, v_ref, d_ref, m2_ref, v2_ref, gout_ref):
        gr = g_ref[...]
        m2 = ADAM_B1 * m_ref[...] + (1.0 - ADAM_B1) * gr
        v2 = ADAM_B2 * v_ref[...] + (1.0 - ADAM_B2) * (gr * gr)
        m_hat = m2 / (1.0 - ADAM_B1 ** ADAM_STEP)
        v_hat = v2 / (1.0 - ADAM_B2 ** ADAM_STEP)
        d_ref[...] = -ADAM_LR * (m_hat / (jnp.sqrt(v_hat) + ADAM_EPS) + ADAM_WD * w_ref[...])
        m2_ref[...] = m2
        v2_ref[...] = v2
        gout_ref[...] = gr

    spec = pl.BlockSpec((tm, cols), lambda i: (i, 0))
    sds = jax.ShapeDtypeStruct((rows, cols), F32)
    return pl.pallas_call(body, grid=(rows // tm,), in_specs=[spec] * 4, out_specs=[spec] * 4,
                          out_shape=[sds] * 4, name=name)(w, g, m, v)


def kernel(x, norm_g, w_in, b_merge, conv_w, w_out_conv, w_out_attn, w_o, final_g, loss_target, m_norm_g, m_w_in, m_b_merge, m_conv_w, m_w_out_conv, m_w_out_attn, m_w_o, m_final_g, v_norm_g, v_w_in, v_b_merge, v_conv_w, v_w_out_conv, v_w_out_attn, v_w_o, v_final_g):
    mx, my, mc = _place()
    chip = (2 * mx + my).astype(jnp.int32)
    seq = x.shape[1]

    chip1 = chip.reshape(1)
    slots = [_place_shard(w[0], chip1, MXU_DTYPE) for w in (w_in, w_out_conv, w_out_attn, w_o)]
    taps_slot = _place_shard(jnp.pad(conv_w[0], ((0, 5), (0, 0))), chip1, F32)
    gather_near = _SplitExchange("gather_w_in_near", [slots[0], taps_slot], [], 4,
                                 _gather_copies_to((0, 1), whole=(1,)))
    st = _Step(x[0], loss_target[0], norm_g, chip1, after=gather_near.token)
    st.project_own(w_in[0])
    near, _, _ = gather_near.wait([st.ut, st.proj_own])
    gather_far = _SplitExchange("gather_w_in_far", near, [], 2, _gather_copies_to((2,), whole=(1,)))
    (w4,) = _forward_to_sibling(gather_far.bufs[:1], flips=(0, 1))
    st.project(w4, (2, 1))
    (w4, taps4), _, out_slots = gather_far.wait([st.proj_own], riders=slots[1:], bufs=[w4, gather_far.bufs[1]])
    gather_out = _SplitExchange("gather_w_out", out_slots, [], 9, _gather_copies_to((0, 1, 2)), riders=[w4])
    (w4,) = _forward_to_sibling(gather_out.riders, flips=(2,))
    st.project(w4, (3,))
    st.mixers(w4, jnp.concatenate([taps4[j, :3, :] for j in range(4)], axis=1))
    out_ws, _, _ = gather_out.wait(st.o)
    woc, woa, wo = [w.reshape(D_MODEL, D_MODEL) for w in _forward_to_sibling(out_ws)]
    st.merge_and_loss(woc, woa, wo, b_merge, final_g.reshape(1, D_MODEL))

    half = mc.astype(jnp.int32).reshape(1)
    where = jnp.stack([chip, mc.astype(jnp.int32)])
    out_grads = [g.reshape(4, -1, D_MODEL) for g in st.out_weight_grads()]
    to_sibling = _grads_to_sibling("out_grads_to_sibling", out_grads)
    st.conv_grads(after=to_sibling.after())
    out_grads, out_from_sibling, _ = to_sibling.wait(st.da4)
    to_chips = _grads_to_chips("out_grads_to_chips",
                               [_add_halves(g, r, half) for g, r in zip(out_grads, out_from_sibling)])
    d_w4 = st.in_weight_grad(after=to_chips.after())
    out_from_chips = to_chips.wait(st.dc3)[1]

    to_sibling = _grads_to_sibling("in_grad_to_sibling", [d_w4])
    gx_lo, dg_lo = st.input_grad(0, after=to_sibling.after())
    (d_w4,), (from_sibling,), _ = to_sibling.wait(gx_lo)
    to_chips = _grads_to_chips("in_grad_to_chips", [_add_halves(d_w4, from_sibling, half)])
    gx_hi, dg_hi = st.input_grad(1, after=to_chips.after())
    grad_x = _to_natural(gx_lo, gx_hi)

    where_late = where + to_chips.after().astype(jnp.int32)
    out_reduced = [_add_chips(g, r, recv, where_late)
                   for g, r, recv in zip(out_grads, out_from_sibling, out_from_chips)]
    g_woc, g_woa, g_wo = _share_halves(out_reduced)
    small = _reduce_small([dg_lo + dg_hi, st.d_bias.reshape(2, D_MODEL), st.d_taps, st.d_final_g,
                           st.loss8.reshape(1, D_MODEL)])
    loss = (0.5 / D_MODEL) * small[7, 0]
    g_taps = lax.dynamic_slice(small[3:6], (0, chip * (D_MODEL // 4)), (3, D_MODEL // 4))
    upd = {
        "norm_g": _adamw(norm_g, small[0:1], m_norm_g, v_norm_g, "adamw_norm_g"),
        "b_merge": _adamw(b_merge, small[1:3].reshape(1, 2 * D_MODEL), m_b_merge, v_b_merge, "adamw_b_merge"),
        "conv_w": _adamw(conv_w[0], g_taps, m_conv_w[0], v_conv_w[0], "adamw_conv_w"),
        "w_out_conv": _adamw(w_out_conv[0], g_woc, m_w_out_conv[0], v_w_out_conv[0], "adamw_w_out_conv"),
        "w_out_attn": _adamw(w_out_attn[0], g_woa, m_w_out_attn[0], v_w_out_attn[0], "adamw_w_out_attn"),
        "w_o": _adamw(w_o[0], g_wo, m_w_o[0], v_w_o[0], "adamw_w_o"),
        "final_g": _adamw(final_g.reshape(1, D_MODEL), small[6:7], m_final_g.reshape(1, D_MODEL),
                          v_final_g.reshape(1, D_MODEL), "adamw_final_g"),
    }
    behind = [grad_x] + [u[0] for u in upd.values()]
    in_reduced = _add_chips(d_w4, from_sibling, to_chips.wait(behind)[1][0], where)
    (g_w_in,) = _share_halves([in_reduced])
    upd["w_in"] = _adamw(w_in[0], g_w_in, m_w_in[0], v_w_in[0], "adamw_w_in")

    names = ["norm_g", "w_in", "b_merge", "conv_w", "w_out_conv", "w_out_attn", "w_o", "final_g"]
    shapes = [norm_g.shape, w_in.shape, b_merge.shape, conv_w.shape, w_out_conv.shape, w_out_attn.shape,
              w_o.shape, final_g.shape]
    outs = [loss, grad_x.reshape(1, seq, D_MODEL)]
    for k in (3, 0, 1, 2):
        outs += [upd[n][k].reshape(s) for n, s in zip(names, shapes)]
    return tuple(outs)
```

```python
import functools

import numpy as np
import jax
import jax.numpy as jnp
from jax import lax
from jax.experimental import pallas as pl
from jax.experimental.pallas import tpu as pltpu

F32 = jnp.float32
BF16 = jnp.bfloat16
MXU_DTYPE = jnp.bfloat16
ACT_DTYPE = jnp.bfloat16

D_MODEL = 1024
N_HEADS = 16
HEAD_DIM = 64
QB = 128
N_RES = 16
LANES = 128
HP = N_HEADS * HEAD_DIM // LANES
IN_COLS = 10 * D_MODEL
SHARD_COLS = IN_COLS // 4
EPS = 1e-6
NEG = -1e30

ADAM_LR, ADAM_B1, ADAM_B2, ADAM_EPS, ADAM_WD, ADAM_STEP = 0.001, 0.9, 0.999, 1e-08, 0.01, 10

PATTERNS = {1: (16, 8), 4: (4, 32), 16: (1, 128)}

_NN = (((1,), (0,)), ((), ()))
_NT = (((1,), (1,)), ((), ()))


def _dot(a, b):
    return lax.dot_general(a.astype(MXU_DTYPE), b.astype(MXU_DTYPE), _NN, preferred_element_type=F32)


def _dot_nt(a, b):
    return lax.dot_general(a.astype(MXU_DTYPE), b.astype(MXU_DTYPE), _NT, preferred_element_type=F32)


def _split3(x):
    hi = x.astype(BF16)
    r1 = x - hi.astype(F32)
    mid = r1.astype(BF16)
    lo = (r1 - mid.astype(F32)).astype(BF16)
    return hi, mid, lo


def _select_cols(x, sel, terms):
    return sum(lax.dot_general(t, sel, _NN, preferred_element_type=F32) for t in _split3(x)[:terms])


def _sigmoid(z):
    return 1.0 / (1.0 + jnp.exp(-z))


def _head_expand_matrix():
    e = np.zeros((LANES, D_MODEL), np.float32)
    for h in range(N_HEADS):
        e[8 * h, HEAD_DIM * h:HEAD_DIM * (h + 1)] = 1.0
    return jnp.asarray(e, BF16)


def _head_sum_matrix():
    e = np.zeros((D_MODEL, LANES), np.float32)
    for h in range(N_HEADS):
        e[HEAD_DIM * h:HEAD_DIM * (h + 1), 8 * h:8 * (h + 1)] = 1.0
    return jnp.asarray(e, BF16)


def _attn_tables(d):
    g_n, rq = PATTERNS[d]
    q_n = g_n * rq
    gq, iq = np.arange(q_n) // rq, np.arange(q_n) % rq

    def tab(kn, base):
        k_n = g_n * kn
        gk, jk = np.arange(k_n) // kn, np.arange(k_n) % kn
        delta = g_n * (base + iq[:, None] - jk[None, :]) + gq[:, None] - gk[None, :]
        valid = (delta >= 0) & (delta <= QB)
        dist = np.where(valid, d * delta, 0).astype(np.float32)
        madd = np.where(valid, 0.0, NEG).astype(np.float32)
        return dist, madd

    d0, m0 = tab(rq if g_n == 1 else 2 * rq, 0)
    d1, m1 = tab(2 * rq, rq)
    return d0, m0, d1, m1


def _alibi_slopes():
    return jnp.exp2(-8.0 * jnp.arange(1, N_HEADS + 1, dtype=F32) / N_HEADS)


def _to_residue_major(x, tgt, after=None):
    s_n, c_n = x.shape
    lr = s_n // N_RES
    extra = [] if after is None else [after]

    def body(x_ref, t_ref, *rest):
        xo_ref, to_ref = rest[-2:]
        for r in range(N_RES):
            xo_ref[r] = x_ref[pl.ds(r, lr, stride=N_RES), :]
            to_ref[r] = t_ref[pl.ds(r, lr, stride=N_RES), :]

    nat = pl.BlockSpec((s_n, LANES), lambda j: (0, j))
    res = pl.BlockSpec((N_RES, lr, LANES), lambda j: (0, 0, j))
    xo, to = pl.pallas_call(
        body, grid=(c_n // LANES,),
        in_specs=[nat, nat] + [pl.BlockSpec((8, LANES), lambda j: (0, 0))] * len(extra),
        out_specs=[res, res],
        out_shape=[jax.ShapeDtypeStruct((N_RES, lr, c_n), F32)] * 2,
        name="perm_in",
    )(x, tgt, *extra)
    return xo.reshape(s_n, c_n), to.reshape(s_n, c_n)


def _to_natural(gx_lo, gx_hi):
    half_rows, c_n = gx_lo.shape
    lr = half_rows // (N_RES // 2)

    def body(lo_ref, hi_ref, o_ref):
        for r in range(N_RES):
            o_ref[pl.ds(r, lr, stride=N_RES), :] = lo_ref[r] if r < N_RES // 2 else hi_ref[r - N_RES // 2]

    half = pl.BlockSpec((N_RES // 2, lr, LANES), lambda j: (0, 0, j))
    return pl.pallas_call(
        body, grid=(c_n // LANES,),
        in_specs=[half, half],
        out_specs=pl.BlockSpec((2 * half_rows, LANES), lambda j: (0, j)),
        out_shape=jax.ShapeDtypeStruct((2 * half_rows, c_n), F32),
        name="perm_out",
    )(gx_lo.reshape(N_RES // 2, lr, c_n), gx_hi.reshape(N_RES // 2, lr, c_n))


def _rms_in(xp, norm_g):
    s_n, c_n = xp.shape
    tm = 512

    def body(x_ref, g_ref, u_ref, ut_ref):
        x = x_ref[...]
        r = lax.rsqrt(jnp.mean(x * x, axis=-1, keepdims=True) + EPS)
        u = x * r * g_ref[...]
        u_ref[...] = u.astype(u_ref.dtype)
        ut_ref[...] = u.T.astype(ut_ref.dtype)

    return pl.pallas_call(
        body, grid=(s_n // tm,),
        in_specs=[pl.BlockSpec((tm, c_n), lambda i: (i, 0)), pl.BlockSpec((1, c_n), lambda i: (0, 0))],
        out_specs=[pl.BlockSpec((tm, c_n), lambda i: (i, 0)), pl.BlockSpec((c_n, tm), lambda i: (0, i))],
        out_shape=[jax.ShapeDtypeStruct((s_n, c_n), ACT_DTYPE), jax.ShapeDtypeStruct((c_n, s_n), ACT_DTYPE)],
        name="rms_in",
    )(xp, norm_g)


def _in_proj(u, chip, w_own=None, w4=None, partial=None, others=()):
    s_n = u.shape[0]
    tn, cm = 512, 512
    per = SHARD_COLS // tn
    own = partial is None

    def body(chip_ref, a_ref, b_ref, *rest):
        o_ref = rest[-1]
        b = b_ref[...]
        for c in range(s_n // cm):
            o_ref[c * cm:(c + 1) * cm, :] = _dot(a_ref[c * cm:(c + 1) * cm, :], b).astype(o_ref.dtype)

    def shard(n, chip_ref):
        if own:
            return chip_ref[0]
        mask = others[-1]
        for i, m in enumerate(others[:-1]):
            mask = jnp.where(n // per == i, m, mask)
        return jnp.bitwise_xor(chip_ref[0], mask)

    w_spec = (pl.BlockSpec((D_MODEL, tn), lambda n, c: (0, n)) if own else
              pl.BlockSpec((None, D_MODEL, tn), lambda n, c: (shard(n, c), 0, n % per)))
    return pl.pallas_call(
        body,
        grid_spec=pltpu.PrefetchScalarGridSpec(
            num_scalar_prefetch=1, grid=(per if own else len(others) * per,),
            in_specs=[pl.BlockSpec((s_n, D_MODEL), lambda n, c: (0, 0)), w_spec] + ([] if own else [_ANY]),
            out_specs=pl.BlockSpec((s_n, tn), lambda n, c: (0, shard(n, c) * per + n % per))),
        out_shape=jax.ShapeDtypeStruct((s_n, IN_COLS), ACT_DTYPE),
        input_output_aliases={} if own else {3: 0},
        name="in_proj_own" if own else "in_proj_" + "_".join(str(m) for m in others),
    )(*([chip, u, w_own] if own else [chip, u, w4, partial]))


def _conv_terms(xc_ref, cg_ref, r, row, lr, cache):
    def a_of(q):
        if q not in cache:
            cache[q] = cg_ref[q].astype(F32) * xc_ref[q].astype(F32)
        return cache[q]

    def shift_down(v):
        return jnp.where(row >= 1, pltpu.roll(v, 1, 0), 0.0)

    a = a_of(r)
    am1 = a_of(r - 1) if r >= 1 else shift_down(a_of(N_RES - 1))
    am2 = a_of(r - 2) if r >= 2 else shift_down(a_of(N_RES - 2 + r))
    return a, am1, am2


def _conv_fwd(proj, conv_w):
    s_n = proj.shape[0]
    lr = s_n // N_RES
    pv = proj.reshape(N_RES, lr, IN_COLS)

    def body(xc_ref, bg_ref, cg_ref, zc_ref, w_ref, hc_ref, hct_ref):
        w = w_ref[...]
        row = lax.broadcasted_iota(jnp.int32, (lr, LANES), 0)
        products = {}
        for r in range(N_RES):
            a, am1, am2 = _conv_terms(xc_ref, cg_ref, r, row, lr, products)
            c = w[0:1] * am2 + w[1:2] * am1 + w[2:3] * a
            z = zc_ref[r].astype(F32)
            hc = z * _sigmoid(z) * bg_ref[r].astype(F32) * c
            hc_ref[r] = hc.astype(hc_ref.dtype)
            hct_ref[:, r * lr:(r + 1) * lr] = hc.T.astype(hct_ref.dtype)

    def col(part):
        return pl.BlockSpec((N_RES, lr, LANES), lambda j: (0, 0, part * 8 + j))

    hc, hct = pl.pallas_call(
        body, grid=(D_MODEL // LANES,),
        in_specs=[col(0), col(1), col(2), col(3), pl.BlockSpec((3, LANES), lambda j: (0, j))],
        out_specs=[pl.BlockSpec((N_RES, lr, LANES), lambda j: (0, 0, j)),
                   pl.BlockSpec((LANES, s_n), lambda j: (j, 0))],
        out_shape=[jax.ShapeDtypeStruct((N_RES, lr, D_MODEL), ACT_DTYPE),
                   jax.ShapeDtypeStruct((D_MODEL, s_n), ACT_DTYPE)],
        name="conv_fwd",
    )(pv, pv, pv, pv, conv_w)
    return hc.reshape(s_n, D_MODEL), hct


RES_PER_STEP = 16
ATTN_BATCH = 16

_BNT = (((2,), (2,)), ((0,), (0,)))
_BNN = (((2,), (1,)), ((0,), (0,)))


def _bdot(a, b, dims):
    return lax.dot_general(a.astype(MXU_DTYPE), b.astype(MXU_DTYPE), dims, preferred_element_type=F32)


def _pattern_view_shape(s_n, c_n, g_n, lead=()):
    lr = s_n // N_RES
    return (*lead, 4, 4, lr, c_n) if g_n == 4 else (*lead, N_RES, lr, c_n)


def _pattern_view(a, g_n, lead=()):
    return a.reshape(_pattern_view_shape(a.shape[-2], a.shape[-1], g_n, lead))


def _pattern_grid(g_n):
    return (N_RES // RES_PER_STEP if g_n == 1 else N_RES // g_n, HP)


def _pattern_spec(g_n, lr, col_of_hp, lead=()):
    z = (0,) * len(lead)
    if g_n == 16:
        return pl.BlockSpec((*lead, 16, lr, LANES), lambda r, hp: (*z, 0, 0, col_of_hp(hp)))
    if g_n == 4:
        return pl.BlockSpec((*lead, 4, None, lr, LANES), lambda r, hp: (*z, 0, r, 0, col_of_hp(hp)))
    return pl.BlockSpec((*lead, RES_PER_STEP, lr, LANES), lambda r, hp: (*z, r, 0, col_of_hp(hp)))


def _aligned(start, m):
    return start if isinstance(start, int) else pl.multiple_of(start, m)


class _Units:
    def __init__(self, g_n, rq):
        self.g_n, self.rq = g_n, rq
        self.per_res, self.paired = g_n == 1, rq == 8

    def plan(self, lr, size):
        if self.per_res:
            return [0], lr // self.rq - 1, lambda j: [pl.multiple_of(j * self.rq, self.rq)]
        step = 16 if self.paired else self.rq
        per = min(size // 2 if self.paired else size, lr // step)
        assert (lr // step) % per == 0
        return ([i * step for i in range(per)], lr // step // per - 1,
                lambda j: [pl.multiple_of((j * per + i) * step, step) for i in range(per)])

    def count(self, qs):
        return RES_PER_STEP if self.per_res else len(qs) * (2 if self.paired else 1)

    def _split(self, tiles, lo, rows):
        return tiles[:, lo:lo + rows].reshape(self.g_n * rows, LANES)

    def load_q(self, ref, qs):
        rq = self.rq
        if self.per_res:
            return ref[:, pl.ds(qs[0], rq), :]
        if self.paired:
            tiles = [ref[:, pl.ds(q, 16), :].astype(F32) for q in qs]
            return jnp.stack([self._split(t, lo, 8) for t in tiles for lo in (0, 8)])
        return jnp.stack([ref[:, pl.ds(q, rq), :].reshape(self.g_n * rq, LANES) for q in qs])

    def _key_rows(self, q, at_start):
        return (0, 2 * self.rq) if at_start else (_aligned(q - self.rq, self.rq), 2 * self.rq)

    def load_k(self, ref, qs, first):
        rq = self.rq
        if self.per_res:
            return ref[:, pl.ds(0, rq), :] if first else ref[:, pl.ds(_aligned(qs[0] - rq, rq), 2 * rq), :]
        if self.paired:
            out = []
            for i, q in enumerate(qs):
                if first and i == 0:
                    t = ref[:, 0:16, :].astype(F32)
                    out += [self._split(t, 0, 16)] * 2
                else:
                    t = ref[:, pl.ds(_aligned(q - 16, 16), 32), :].astype(F32)
                    out += [self._split(t, 8, 16), self._split(t, 16, 16)]
            return jnp.stack(out)
        rows = [self._key_rows(q, first and i == 0) for i, q in enumerate(qs)]
        return jnp.stack([ref[:, pl.ds(k0, n), :].reshape(self.g_n * n, LANES) for k0, n in rows])

    def store_q(self, ref, qs, val, add=False, lead=()):
        if self.per_res:
            pieces = [(qs[0], self.rq, val)]
        elif self.paired:
            pieces = [(q, 16, jnp.concatenate([val[2 * i].reshape(self.g_n, 8, LANES),
                                               val[2 * i + 1].reshape(self.g_n, 8, LANES)], axis=1))
                      for i, q in enumerate(qs)]
        else:
            pieces = [(q, self.rq, val[i].reshape(self.g_n, self.rq, LANES)) for i, q in enumerate(qs)]
        for start, rows, v in pieces:
            idx = (*lead, slice(None), pl.ds(start, rows), slice(None))
            ref[idx] = (ref[idx] + v if add else v).astype(ref.dtype)

    def add_k(self, ref, qs, val, first):
        rq = self.rq
        if self.per_res:
            k0, n = (0, rq) if first else (_aligned(qs[0] - rq, rq), 2 * rq)
            ref[:, pl.ds(k0, n), :] += val
            return
        if self.paired:
            starts = [s for i, q in enumerate(qs)
                      for s in ((0, 0) if first and i == 0 else (_aligned(q - 8, 8), q))]
            rows = [(s, 16) for s in starts]
        else:
            rows = [self._key_rows(q, first and i == 0) for i, q in enumerate(qs)]
        for b, (k0, n) in enumerate(rows):
            ref[:, pl.ds(k0, n), :] += val[b].reshape(self.g_n, n, LANES)


def _batch_bias(un, qs, at_start, first_ref, general_ref):
    if not at_start:
        return general_ref[...][None]
    if un.per_res:
        return first_ref[...][None]
    return jnp.concatenate([first_ref[...][None]] + [general_ref[...][None]] * (un.count(qs) - 1), axis=0)


def _stack_heads(x, low):
    zero = jnp.zeros_like(x)
    return jnp.concatenate([jnp.where(low, x, zero), jnp.where(low, zero, x)], axis=1)


def _attn_fwd(proj, slopes, d):
    g_n, rq = PATTERNS[d]
    un = _Units(g_n, rq)
    s_n = proj.shape[0]
    lr = s_n // N_RES
    q_n = g_n * rq
    d0, m0, d1, m1 = _attn_tables(d)
    first, n_more, later = un.plan(lr, ATTN_BATCH)

    def body(sl_ref, q_ref, k_ref, v_ref, d0_ref, m0_ref, d1_ref, m1_ref, o_ref, lse_ref, b0_ref, b1_ref):
        hp = pl.program_id(1)

        @pl.when(hp == 0)
        def _():
            lse_ref[...] = jnp.zeros(lse_ref.shape, F32)

        for h in (0, 1):
            slope = sl_ref[2 * hp + h]
            b0_ref[h * q_n:(h + 1) * q_n, :] = m0_ref[...] - slope * d0_ref[...]
            b1_ref[h * q_n:(h + 1) * q_n, :] = m1_ref[...] - slope * d1_ref[...]

        lane = lax.broadcasted_iota(jnp.int32, (1, q_n, LANES), 2)
        low = lane < HEAD_DIM
        grp = lane // 8

        def batch(qs, at_start):
            qq = _stack_heads(un.load_q(q_ref, qs) * 0.125, low)
            s = _bdot(qq, un.load_k(k_ref, qs, at_start), _BNT) + _batch_bias(un, qs, at_start, b0_ref, b1_ref)
            m = jnp.max(s, axis=2, keepdims=True)
            p = jnp.exp(s - m)
            l = jnp.sum(p, axis=2, keepdims=True)
            o = _bdot(p, un.load_k(v_ref, qs, at_start), _BNN) * (1.0 / l)
            lse = m + jnp.log(l)
            un.store_q(o_ref, qs, jnp.where(low, o[:, :q_n], o[:, q_n:]))
            upd = jnp.where(grp == 2 * hp, lse[:, :q_n], 0.0) + jnp.where(grp == 2 * hp + 1, lse[:, q_n:], 0.0)
            un.store_q(lse_ref, qs, upd, add=True)

        batch(first, True)

        def more(j, carry):
            batch(later(j), False)
            return carry

        lax.fori_loop(1, 1 + n_more, more, 0)

    pv = _pattern_view(proj, g_n)
    full = lambda a: pl.BlockSpec(a.shape, lambda r, hp: (0, 0))
    o, lse = pl.pallas_call(
        body, grid=_pattern_grid(g_n),
        in_specs=[pl.BlockSpec(memory_space=pltpu.SMEM),
                  _pattern_spec(g_n, lr, lambda hp: 32 + hp),
                  _pattern_spec(g_n, lr, lambda hp: 40 + hp),
                  _pattern_spec(g_n, lr, lambda hp: 48 + hp),
                  full(d0), full(m0), full(d1), full(m1)],
        out_specs=[_pattern_spec(g_n, lr, lambda hp: hp), _pattern_spec(g_n, lr, lambda hp: 0)],
        out_shape=[jax.ShapeDtypeStruct(_pattern_view_shape(s_n, D_MODEL, g_n), ACT_DTYPE),
                   jax.ShapeDtypeStruct(_pattern_view_shape(s_n, LANES, g_n), F32)],
        scratch_shapes=[pltpu.VMEM((2 * q_n, d0.shape[1]), F32), pltpu.VMEM((2 * q_n, 2 * q_n), F32)],
        name=f"attn_fwd_d{d}",
    )(slopes, pv, pv, pv, d0, m0, d1, m1)
    return o.reshape(s_n, D_MODEL), lse.reshape(s_n, LANES)


def _attn_combine(outs, lses, proj):
    s_n = proj.shape[0]
    tm = 512

    def body(o1_ref, o2_ref, o3_ref, l1_ref, l2_ref, l3_ref, za_ref, e_ref, o_ref, lse_ref, ha_ref, hat_ref):
        ls = [l1_ref[...], l2_ref[...], l3_ref[...]]
        mx = jnp.maximum(jnp.maximum(ls[0], ls[1]), ls[2])
        den = sum(jnp.exp(l - mx) for l in ls)
        lse = mx + jnp.log(den)
        lse_ref[...] = lse
        o = jnp.zeros((tm, D_MODEL), F32)
        for l, oref in zip(ls, (o1_ref, o2_ref, o3_ref)):
            o = o + _select_cols(jnp.exp(l - lse), e_ref[...], terms=2) * oref[...].astype(F32)
        o_ref[...] = o.astype(o_ref.dtype)
        z = za_ref[...].astype(F32)
        ha = z * _sigmoid(z) * o
        ha_ref[...] = ha.astype(ha_ref.dtype)
        hat_ref[...] = ha.T.astype(hat_ref.dtype)

    row = lambda w: pl.BlockSpec((tm, w), lambda i: (i, 0))
    return pl.pallas_call(
        body, grid=(s_n // tm,),
        in_specs=[row(D_MODEL)] * 3 + [row(LANES)] * 3
        + [pl.BlockSpec((tm, D_MODEL), lambda i: (i, 7)), pl.BlockSpec((LANES, D_MODEL), lambda i: (0, 0))],
        out_specs=[row(D_MODEL), row(LANES), row(D_MODEL), pl.BlockSpec((D_MODEL, tm), lambda i: (0, i))],
        out_shape=[jax.ShapeDtypeStruct((s_n, D_MODEL), ACT_DTYPE), jax.ShapeDtypeStruct((s_n, LANES), F32),
                   jax.ShapeDtypeStruct((s_n, D_MODEL), ACT_DTYPE), jax.ShapeDtypeStruct((D_MODEL, s_n), ACT_DTYPE)],
        name="attn_combine",
    )(*outs, *lses, proj, _head_expand_matrix())


CHAIN_ROWS = 256


def _row_chains(tm):
    return [slice(r, r + CHAIN_ROWS) for r in range(0, tm, CHAIN_ROWS)]


def _gates(gc_ref, ga_ref, b_ref, rows):
    b = b_ref[...]
    gc = _sigmoid(gc_ref[rows, :].astype(F32) + b[:, :D_MODEL])
    ga = _sigmoid(ga_ref[rows, :].astype(F32) + b[:, D_MODEL:])
    return gc, ga


def _merge_loss(hc, ha, woc, woa, wo, proj, b_merge, xp, final_g, tgt):
    s_n = xp.shape[0]
    tm = 512

    def body(hc_ref, ha_ref, woc_ref, woa_ref, wo_ref, gc_ref, ga_ref, b_ref, x_ref, gf_ref, t_ref,
             yc_ref, ya_ref, dhb_ref, dgf_ref, loss_ref, dwo_ref, mgt_ref):
        i = pl.program_id(0)

        @pl.when(i == 0)
        def _():
            dgf_ref[...] = jnp.zeros(dgf_ref.shape, F32)
            loss_ref[...] = jnp.zeros(loss_ref.shape, F32)
            dwo_ref[...] = jnp.zeros(dwo_ref.shape, F32)

        gf = gf_ref[...]
        for rows in _row_chains(tm):
            yc = _dot(hc_ref[rows, :], woc_ref[...])
            ya = _dot(ha_ref[rows, :], woa_ref[...])
            gc, ga = _gates(gc_ref, ga_ref, b_ref, rows)
            mg = gc * yc + ga * ya
            yc_ref[rows, :] = yc.astype(yc_ref.dtype)
            ya_ref[rows, :] = ya.astype(ya_ref.dtype)
            mgt_ref[:, rows] = mg.T.astype(mgt_ref.dtype)
            h2 = x_ref[rows, :] + _dot(mg, wo_ref[...])
            r2 = lax.rsqrt(jnp.mean(h2 * h2, axis=-1, keepdims=True) + EPS)
            nrm = h2 * r2
            err = nrm * gf - t_ref[rows, :]
            e2 = (err * err).reshape(-1, 8, D_MODEL).sum(axis=0)
            loss_ref[...] += sum(e2[:, c * LANES:(c + 1) * LANES] for c in range(D_MODEL // LANES))
            dy = err * (1.0 / D_MODEL)
            dgf_ref[...] += jnp.sum(dy * nrm, axis=0, keepdims=True)
            dn = dy * gf
            dh2 = r2 * (dn - nrm * jnp.mean(dn * nrm, axis=-1, keepdims=True))
            dhb_ref[rows, :] = dh2.astype(dhb_ref.dtype)
        dwo_ref[...] += _dot(mgt_ref[...], dhb_ref[...])

    row = pl.BlockSpec((tm, D_MODEL), lambda i: (i, 0))
    wsp = pl.BlockSpec((D_MODEL, D_MODEL), lambda i: (0, 0), pipeline_mode=pl.Buffered(1))
    vec = lambda w: pl.BlockSpec((1, w), lambda i: (0, 0))
    act = jax.ShapeDtypeStruct((s_n, D_MODEL), ACT_DTYPE)
    return pl.pallas_call(
        body, grid=(s_n // tm,),
        in_specs=[row, row, wsp, wsp, wsp,
                  pl.BlockSpec((tm, D_MODEL), lambda i: (i, 8)), pl.BlockSpec((tm, D_MODEL), lambda i: (i, 9)),
                  vec(2 * D_MODEL), row, vec(D_MODEL), row],
        out_specs=[row, row, row, vec(D_MODEL), pl.BlockSpec((8, LANES), lambda i: (0, 0)),
                   pl.BlockSpec((D_MODEL, D_MODEL), lambda i: (0, 0))],
        out_shape=[act, act, act, jax.ShapeDtypeStruct((1, D_MODEL), F32), jax.ShapeDtypeStruct((8, LANES), F32),
                   jax.ShapeDtypeStruct((D_MODEL, D_MODEL), F32)],
        scratch_shapes=[pltpu.VMEM((D_MODEL, tm), MXU_DTYPE)],
        name="merge_loss",
    )(hc, ha, woc, woa, wo, proj, proj, b_merge, xp, final_g, tgt)


def _merge_bwd(dh2b, wo, woc, woa, yc, ya, proj, b_merge, o, hct, hat):
    s_n = dh2b.shape[0]
    tm = 512

    def body(dh_ref, wo_ref, woc_ref, woa_ref, yc_ref, ya_ref, gc_ref, ga_ref, b_ref, o_ref, za_ref, e_ref,
             hct_ref, hat_ref, dhc_ref, do_ref, dsum_ref, db3_ref, dbias_ref, dwoc_ref, dwoa_ref,
             dyc_ref, dya_ref):
        i = pl.program_id(0)

        @pl.when(i == 0)
        def _():
            dbias_ref[...] = jnp.zeros(dbias_ref.shape, F32)
            dwoc_ref[...] = jnp.zeros(dwoc_ref.shape, F32)
            dwoa_ref[...] = jnp.zeros(dwoa_ref.shape, F32)

        for rows in _row_chains(tm):
            dmg = _dot_nt(dh_ref[rows, :], wo_ref[...])
            gc, ga = _gates(gc_ref, ga_ref, b_ref, rows)
            dgc = dmg * yc_ref[rows, :].astype(F32) * gc * (1.0 - gc)
            dga = dmg * ya_ref[rows, :].astype(F32) * ga * (1.0 - ga)
            dbias_ref[:, :D_MODEL] += jnp.sum(dgc, axis=0, keepdims=True)
            dbias_ref[:, D_MODEL:] += jnp.sum(dga, axis=0, keepdims=True)
            dyc = dmg * gc
            dya = dmg * ga
            dyc_ref[rows, :] = dyc.astype(dyc_ref.dtype)
            dya_ref[rows, :] = dya.astype(dya_ref.dtype)
            dhc_ref[rows, :] = _dot_nt(dyc, woc_ref[...]).astype(dhc_ref.dtype)
            dha = _dot_nt(dya, woa_ref[...])
            z = za_ref[rows, :].astype(F32)
            sg = _sigmoid(z)
            ov = o_ref[rows, :].astype(F32)
            dout = dha * z * sg
            do_ref[rows, :] = dout.astype(do_ref.dtype)
            dsum_ref[rows, :] = _select_cols(dout * ov, e_ref[...], terms=2)
            db3_ref[0, rows, :] = (dha * ov * sg * (1.0 + z * (1.0 - sg))).astype(db3_ref.dtype)
            db3_ref[1, rows, :] = dgc.astype(db3_ref.dtype)
            db3_ref[2, rows, :] = dga.astype(db3_ref.dtype)
        dwoc_ref[...] += _dot(hct_ref[...], dyc_ref[...])
        dwoa_ref[...] += _dot(hat_ref[...], dya_ref[...])

    row = pl.BlockSpec((tm, D_MODEL), lambda i: (i, 0))
    col = pl.BlockSpec((D_MODEL, tm), lambda i: (0, i))
    wsp = pl.BlockSpec((D_MODEL, D_MODEL), lambda i: (0, 0), pipeline_mode=pl.Buffered(1))
    acc = pl.BlockSpec((D_MODEL, D_MODEL), lambda i: (0, 0))
    act = jax.ShapeDtypeStruct((s_n, D_MODEL), ACT_DTYPE)
    grad = jax.ShapeDtypeStruct((D_MODEL, D_MODEL), F32)
    return pl.pallas_call(
        body, grid=(s_n // tm,),
        in_specs=[row, wsp, wsp, wsp, row, row,
                  pl.BlockSpec((tm, D_MODEL), lambda i: (i, 8)), pl.BlockSpec((tm, D_MODEL), lambda i: (i, 9)),
                  pl.BlockSpec((1, 2 * D_MODEL), lambda i: (0, 0)), row,
                  pl.BlockSpec((tm, D_MODEL), lambda i: (i, 7)), pl.BlockSpec((D_MODEL, LANES), lambda i: (0, 0)),
                  col, col],
        out_specs=[row, row, pl.BlockSpec((tm, LANES), lambda i: (i, 0)),
                   pl.BlockSpec((3, tm, D_MODEL), lambda i: (0, i, 0)),
                   pl.BlockSpec((1, 2 * D_MODEL), lambda i: (0, 0)), acc, acc],
        out_shape=[act, act, jax.ShapeDtypeStruct((s_n, LANES), F32),
                   jax.ShapeDtypeStruct((3, s_n, D_MODEL), ACT_DTYPE),
                   jax.ShapeDtypeStruct((1, 2 * D_MODEL), F32), grad, grad],
        scratch_shapes=[pltpu.VMEM((tm, D_MODEL), MXU_DTYPE), pltpu.VMEM((tm, D_MODEL), MXU_DTYPE)],
        name="merge_bwd",
    )(dh2b, wo, woc, woa, yc, ya, proj, proj, b_merge, o, proj, _head_sum_matrix(), hct, hat)


def _conv_bwd(proj, conv_w, dhc):
    s_n = proj.shape[0]
    lr = s_n // N_RES
    pv = proj.reshape(N_RES, lr, IN_COLS)

    def body(xc_ref, bg_ref, cg_ref, zc_ref, w_ref, dhc_ref, da4_ref, dw_ref, dc_ref):
        w = w_ref[...]
        row = lax.broadcasted_iota(jnp.int32, (lr, LANES), 0)
        dw = [jnp.zeros((1, LANES), F32) for _ in range(3)]
        products = {}
        for r in range(N_RES):
            a, am1, am2 = _conv_terms(xc_ref, cg_ref, r, row, lr, products)
            c = w[0:1] * am2 + w[1:2] * am1 + w[2:3] * a
            z = zc_ref[r].astype(F32)
            sg = _sigmoid(z)
            sz = z * sg
            bg = bg_ref[r].astype(F32)
            dh = dhc_ref[r].astype(F32)
            da4_ref[1, r] = (dh * sz * c).astype(da4_ref.dtype)
            da4_ref[3, r] = (dh * bg * c * sg * (1.0 + z * (1.0 - sg))).astype(da4_ref.dtype)
            dc = dh * sz * bg
            dc_ref[r] = dc
            dw[0] = dw[0] + jnp.sum(dc * am2, axis=0, keepdims=True)
            dw[1] = dw[1] + jnp.sum(dc * am1, axis=0, keepdims=True)
            dw[2] = dw[2] + jnp.sum(dc * a, axis=0, keepdims=True)
        dw_ref[0:1, :] = dw[0]
        dw_ref[1:2, :] = dw[1]
        dw_ref[2:3, :] = dw[2]

        def shift_up(v):
            return jnp.where(row < lr - 1, pltpu.roll(v, lr - 1, 0), 0.0)

        for r in range(N_RES):
            dp1 = dc_ref[r + 1] if r + 1 < N_RES else shift_up(dc_ref[0])
            dp2 = dc_ref[r + 2] if r + 2 < N_RES else shift_up(dc_ref[r + 2 - N_RES])
            da = w[2:3] * dc_ref[r] + w[1:2] * dp1 + w[0:1] * dp2
            da4_ref[0, r] = (da * cg_ref[r].astype(F32)).astype(da4_ref.dtype)
            da4_ref[2, r] = (da * xc_ref[r].astype(F32)).astype(da4_ref.dtype)

    def col(part):
        return pl.BlockSpec((N_RES, lr, LANES), lambda j: (0, 0, part * 8 + j))

    da4, dw = pl.pallas_call(
        body, grid=(D_MODEL // LANES,),
        in_specs=[col(0), col(1), col(2), col(3), pl.BlockSpec((3, LANES), lambda j: (0, j)),
                  pl.BlockSpec((N_RES, lr, LANES), lambda j: (0, 0, j))],
        out_specs=[pl.BlockSpec((4, N_RES, lr, LANES), lambda j: (0, 0, 0, j)),
                   pl.BlockSpec((3, LANES), lambda j: (0, j))],
        out_shape=[jax.ShapeDtypeStruct((4, N_RES, lr, D_MODEL), ACT_DTYPE),
                   jax.ShapeDtypeStruct((3, D_MODEL), F32)],
        scratch_shapes=[pltpu.VMEM((N_RES, lr, LANES), F32)],
        name="conv_bwd",
    )(pv, pv, pv, pv, conv_w, dhc.reshape(N_RES, lr, D_MODEL))
    return da4.reshape(4, s_n, D_MODEL), dw


def _attn_bwd(proj, dout, lse, dsum, slopes, d, prev=None):
    g_n, rq = PATTERNS[d]
    un = _Units(g_n, rq)
    s_n = proj.shape[0]
    lr = s_n // N_RES
    q_n = g_n * rq
    d0, m0, d1, m1 = (np.ascontiguousarray(t.T) for t in _attn_tables(d))
    first, n_more, later = un.plan(lr, ATTN_BATCH)
    bsz = un.count(first)
    gd = RES_PER_STEP if un.per_res else g_n

    def body(sl_ref, q_ref, k_ref, v_ref, do_ref, lse_ref, ds_ref, d0_ref, m0_ref, d1_ref, m1_ref, *rest):
        prev_ref = rest[0] if prev is not None else None
        out_ref, b0_ref, b1_ref, lt_ref, dt_ref, dk_ref, dv_ref = rest[-7:]
        hp = pl.program_id(1)
        for h in (0, 1):
            slope = sl_ref[2 * hp + h]
            b0_ref[:, h * q_n:(h + 1) * q_n] = m0_ref[...] - slope * d0_ref[...]
            b1_ref[:, h * q_n:(h + 1) * q_n] = m1_ref[...] - slope * d1_ref[...]
        if prev is None:
            dk_ref[...] = jnp.zeros(dk_ref.shape, F32)
            dv_ref[...] = jnp.zeros(dv_ref.shape, F32)
        else:
            out_ref[0] = prev_ref[0]
            dk_ref[...] = prev_ref[1].astype(F32)
            dv_ref[...] = prev_ref[2].astype(F32)
        low = lax.broadcasted_iota(jnp.int32, (1, q_n, LANES), 2) < HEAD_DIM
        row16 = pl.multiple_of(16 * hp, 16)

        def query_rows(stat_ref, t_ref, qs):
            tiles = un.load_q(stat_ref, qs)
            for b in range(bsz):
                t_ref[b] = tiles[b].T
            t16 = t_ref[:, pl.ds(row16, 16), :]
            return jnp.concatenate([t16[:, 0:1, :], t16[:, 8:9, :]], axis=2)

        def batch(qs, at_start):
            qq = _stack_heads(un.load_q(q_ref, qs) * 0.125, low)
            dd = _stack_heads(un.load_q(do_ref, qs), low)
            ks = un.load_k(k_ref, qs, at_start)
            vs = un.load_k(v_ref, qs, at_start)
            lrow = query_rows(lse_ref, lt_ref, qs)
            drow = query_rows(ds_ref, dt_ref, qs)
            pt = jnp.exp(_bdot(ks, qq, _BNT) + _batch_bias(un, qs, at_start, b0_ref, b1_ref) - lrow)
            dst = pt * (_bdot(vs, dd, _BNT) - drow)
            un.add_k(dv_ref, qs, _bdot(pt, dd, _BNN), at_start)
            un.add_k(dk_ref, qs, _bdot(dst, qq, _BNN), at_start)
            dq = _bdot(jnp.swapaxes(dst, 1, 2), ks, _BNN)
            un.store_q(out_ref, qs, jnp.where(low, dq[:, :q_n], dq[:, q_n:]) * 0.125, add=prev is not None,
                       lead=(0,))

        batch(first, True)

        def more(j, carry):
            batch(later(j), False)
            return carry

        lax.fori_loop(1, 1 + n_more, more, 0)
        out_ref[1] = dk_ref[...].astype(out_ref.dtype)
        out_ref[2] = dv_ref[...].astype(out_ref.dtype)

    pv = _pattern_view(proj, g_n)
    full = lambda a: pl.BlockSpec(a.shape, lambda r, hp: (0, 0))
    whole = _pattern_spec(g_n, lr, lambda hp: hp, lead=(3,))
    out = pl.pallas_call(
        body, grid=_pattern_grid(g_n),
        in_specs=[pl.BlockSpec(memory_space=pltpu.SMEM),
                  _pattern_spec(g_n, lr, lambda hp: 32 + hp),
                  _pattern_spec(g_n, lr, lambda hp: 40 + hp),
                  _pattern_spec(g_n, lr, lambda hp: 48 + hp),
                  _pattern_spec(g_n, lr, lambda hp: hp),
                  _pattern_spec(g_n, lr, lambda hp: 0),
                  _pattern_spec(g_n, lr, lambda hp: 0),
                  full(d0), full(m0), full(d1), full(m1)] + ([] if prev is None else [whole]),
        out_specs=whole,
        out_shape=jax.ShapeDtypeStruct(_pattern_view_shape(s_n, D_MODEL, g_n, lead=(3,)), ACT_DTYPE),
        scratch_shapes=[pltpu.VMEM((d0.shape[0], 2 * q_n), F32), pltpu.VMEM((2 * q_n, 2 * q_n), F32),
                        pltpu.VMEM((bsz, LANES, q_n), F32), pltpu.VMEM((bsz, LANES, q_n), F32),
                        pltpu.VMEM((gd, lr, LANES), F32), pltpu.VMEM((gd, lr, LANES), F32)],
        name=f"attn_bwd_d{d}",
    )(slopes, pv, pv, pv, _pattern_view(dout, g_n), _pattern_view(lse, g_n), _pattern_view(dsum, g_n),
      d0, m0, d1, m1, *([] if prev is None else [_pattern_view(prev, g_n, lead=(3,))]))
    return out.reshape(3, s_n, D_MODEL)


def _part_index(step, per, lo, n):
    return jnp.clip(step // per - lo, 0, n - 1)


def _dw_in(ut, da4, dc3, db3):
    s_n = ut.shape[1]
    tn = 512
    per = D_MODEL // tn
    shard_blocks = SHARD_COLS // tn

    def body(a_ref, p0_ref, p1_ref, p2_ref, o_ref):
        part = pl.program_id(0) // per

        @pl.when(part < 4)
        def _():
            o_ref[...] = _dot(a_ref[...], p0_ref[...])

        @pl.when((part >= 4) & (part < 7))
        def _():
            o_ref[...] = _dot(a_ref[...], p1_ref[...])

        @pl.when(part >= 7)
        def _():
            o_ref[...] = _dot(a_ref[...], p2_ref[...])

    def pspec(lo, n):
        def index(j):
            part = j // per
            col = jnp.where(part < lo, 0, jnp.where(part >= lo + n, per - 1, j % per))
            return _part_index(j, per, lo, n), 0, col
        return pl.BlockSpec((None, s_n, tn), index)

    return pl.pallas_call(
        body, grid=(IN_COLS // tn,),
        in_specs=[pl.BlockSpec((D_MODEL, s_n), lambda j: (0, 0), pipeline_mode=pl.Buffered(1)),
                  pspec(0, 4), pspec(4, 3), pspec(7, 3)],
        out_specs=pl.BlockSpec((None, D_MODEL, tn), lambda j: (j // shard_blocks, 0, j % shard_blocks)),
        out_shape=jax.ShapeDtypeStruct((4, D_MODEL, SHARD_COLS), F32),
        name="dw_in",
    )(ut, da4, dc3, db3)


def _input_grad(da4, dc3, db3, w4, xp, norm_g, dh2, row0, rows):
    tm, tk = 256, 512
    per = D_MODEL // tk
    shard_blocks = SHARD_COLS // tk
    m0 = row0 // tm

    def body(p0_ref, p1_ref, p2_ref, w_ref, x_ref, g_ref, dh_ref, gx_ref, dg_ref):
        @pl.when(pl.program_id(0) == 0)
        def _():
            dg_ref[...] = jnp.zeros(dg_ref.shape, F32)

        du = None
        for k in range(IN_COLS // tk):
            part, cols = k // per, pl.ds((k % per) * tk, tk)
            ref, slot = (p0_ref, part) if part < 4 else (p1_ref, part - 4) if part < 7 else (p2_ref, part - 7)
            d = _dot_nt(ref[slot, :, cols], w_ref[k // shard_blocks, :, pl.ds((k % shard_blocks) * tk, tk)])
            du = d if du is None else du + d
        x = x_ref[...]
        r = lax.rsqrt(jnp.mean(x * x, axis=-1, keepdims=True) + EPS)
        nrm = x * r
        dg_ref[...] += jnp.sum(du * nrm, axis=0, keepdims=True)
        dn = du * g_ref[...]
        gx_ref[...] = dh_ref[...].astype(F32) + r * (dn - nrm * jnp.mean(dn * nrm, axis=-1, keepdims=True))

    def pspec(n):
        return pl.BlockSpec((n, tm, D_MODEL), lambda m: (0, m0 + m, 0))

    row_in = pl.BlockSpec((tm, D_MODEL), lambda m: (m0 + m, 0))
    vec = pl.BlockSpec((1, D_MODEL), lambda m: (0, 0))
    return pl.pallas_call(
        body, grid=(rows // tm,),
        in_specs=[pspec(4), pspec(3), pspec(3),
                  pl.BlockSpec(w4.shape, lambda m: (0, 0, 0), pipeline_mode=pl.Buffered(1)),
                  row_in, vec, row_in],
        out_specs=[pl.BlockSpec((tm, D_MODEL), lambda m: (m, 0)), vec],
        out_shape=[jax.ShapeDtypeStruct((rows, D_MODEL), F32), jax.ShapeDtypeStruct((1, D_MODEL), F32)],
        name="input_grad",
    )(da4, dc3, db3, w4, xp, norm_g, dh2)


class _Step:
    def __init__(self, x, tgt, norm_g, chip, after=None):
        self.norm_g, self.chip = norm_g, chip
        self.slopes = _alibi_slopes()
        self.xp, self.tp = _to_residue_major(x, tgt, after)
        self.u, self.ut = _rms_in(self.xp, norm_g)

    def project_own(self, w_own):
        self.proj_own = _in_proj(self.u, self.chip, w_own=w_own)

    def project(self, w4, others):
        self.proj_own = _in_proj(self.u, self.chip, w4=w4, partial=self.proj_own, others=others)

    def mixers(self, w4, taps):
        self.w4, self.taps, self.proj = w4, taps, self.proj_own
        self.hc, self.hct = _conv_fwd(self.proj, taps)
        fwd = [_attn_fwd(self.proj, self.slopes, d) for d in PATTERNS]
        self.o, self.lse, self.ha, self.hat = _attn_combine([f[0] for f in fwd], [f[1] for f in fwd], self.proj)

    def merge_and_loss(self, woc, woa, wo, b_merge, final_g):
        self.woc, self.woa, self.wo, self.b_merge = woc, woa, wo, b_merge
        (self.yc, self.ya, self.dh2b, self.d_final_g, self.loss8, self.d_wo) = _merge_loss(
            self.hc, self.ha, woc, woa, wo, self.proj, b_merge, self.xp, final_g, self.tp)

    def out_weight_grads(self):
        (self.dhc, self.dout, self.dsum, self.db3, self.d_bias, d_woc, d_woa) = _merge_bwd(
            self.dh2b, self.wo, self.woc, self.woa, self.yc, self.ya, self.proj, self.b_merge, self.o,
            self.hct, self.hat)
        return d_woc, d_woa, self.d_wo

    def conv_grads(self, after=0.0):
        self.da4, self.d_taps = _conv_bwd(self.proj, self.taps + after, self.dhc)

    def in_weight_grad(self, after=0.0):
        slopes = self.slopes + after
        self.dc3 = None
        for d in PATTERNS:
            self.dc3 = _attn_bwd(self.proj, self.dout, self.lse, self.dsum, slopes, d, prev=self.dc3)
        return _dw_in(self.ut, self.da4, self.dc3, self.db3)

    def input_grad(self, half, after=0.0):
        rows = self.xp.shape[0] // 2
        return _input_grad(self.da4, self.dc3, self.db3, self.w4, self.xp, self.norm_g + after, self.dh2b,
                           half * rows, rows)


def _local_grads(x, tgt, norm_g, w4, b_merge, conv_w, woc, woa, wo, final_g):
    st = _Step(x, tgt, norm_g, jnp.zeros((1,), jnp.int32))
    st.project_own(w4[0])
    st.project(w4, (2, 1))
    st.project(w4, (3,))
    st.mixers(w4, conv_w)
    st.merge_and_loss(woc, woa, wo, b_merge, final_g)
    d_woc, d_woa, d_wo = st.out_weight_grads()
    st.conv_grads()
    d_w4 = st.in_weight_grad()
    gx_lo, dg_lo = st.input_grad(0)
    gx_hi, dg_hi = st.input_grad(1)
    return (st.loss8, _to_natural(gx_lo, gx_hi), dg_lo + dg_hi, d_w4, st.d_bias, st.d_taps, d_woc, d_woa, d_wo,
            st.d_final_g)


MESH = pl.DeviceIdType.MESH
_CHIP_FLIPS = ((1, 0), (0, 1), (1, 1))
_ANY = pl.BlockSpec(memory_space=pl.ANY)


def _place():
    return lax.axis_index("x"), lax.axis_index("y"), lax.axis_index("c")


def _flip(v, f):
    return 1 - v if f else v


def _remote(src, dst, send_sems, recv_sems, k, device):
    return pltpu.make_async_remote_copy(src_ref=src, dst_ref=dst, send_sem=send_sems.at[k], recv_sem=recv_sems.at[k],
                                        device_id=device, device_id_type=MESH)


def _place_shard(w, chip, dtype):
    rows, cols = w.shape
    tm = min(rows, 128)

    def body(chip_ref, w_ref, o_ref):
        o_ref[0] = w_ref[...].astype(o_ref.dtype)

    return pl.pallas_call(
        body,
        grid_spec=pltpu.PrefetchScalarGridSpec(
            num_scalar_prefetch=1, grid=(rows // tm,),
            in_specs=[pl.BlockSpec((tm, cols), lambda i, chip_ref: (i, 0))],
            out_specs=pl.BlockSpec((1, tm, cols), lambda i, chip_ref: (chip_ref[0], i, 0))),
        out_shape=jax.ShapeDtypeStruct((4, rows, cols), dtype),
        name="place_shard",
    )(chip, w)


def _gather_copies_to(flips, whole=()):
    def copies(arrs, _, send_sems, recv_sems):
        x, y, c = _place()
        out = []
        for a, arr in enumerate(arrs):
            h = arr.shape[1] // 2
            mine = arr.at[2 * x + y] if a in whole else arr.at[2 * x + y, pl.ds(pl.multiple_of(c * h, 8), h)]
            for i, t in enumerate(flips):
                fx, fy = _CHIP_FLIPS[t]
                out.append(_remote(mine, mine, send_sems, recv_sems, len(flips) * a + i,
                                   (_flip(x, fx), _flip(y, fy), c)))
        return out
    return copies


def _forward_to_sibling(arrs, flips=(0, 1, 2)):
    n = len(arrs)

    def body(*refs):
        outs = refs[n:2 * n]
        send_sems, recv_sems = refs[2 * n:]
        x, y, c = _place()
        sibling = (x, y, 1 - c)
        started = []
        for a in range(n):
            h = outs[a].shape[1] // 2
            rows = pl.ds(pl.multiple_of(c * h, 8), h)
            for t in flips:
                fx, fy = _CHIP_FLIPS[t]
                landed = outs[a].at[2 * _flip(x, fx) + _flip(y, fy), rows]
                cp = _remote(landed, landed, send_sems, recv_sems, 3 * a + t, sibling)
                cp.start()
                started.append(cp)
        for a in range(n):
            h = outs[a].shape[1] // 2
            rows = pl.ds(pl.multiple_of((1 - c) * h, 8), h)
            for t in flips:
                fx, fy = _CHIP_FLIPS[t]
                handed = outs[a].at[2 * _flip(x, fx) + _flip(y, fy), rows]
                _remote(handed, handed, send_sems, recv_sems, 3 * a + t, sibling).wait_recv()
        for cp in started:
            cp.wait_send()

    return pl.pallas_call(
        body, in_specs=[_ANY] * n, out_specs=[_ANY] * n,
        out_shape=[jax.ShapeDtypeStruct(s.shape, s.dtype) for s in arrs],
        input_output_aliases={a: a for a in range(n)},
        scratch_shapes=[pltpu.SemaphoreType.DMA((3 * n,)), pltpu.SemaphoreType.DMA((3 * n,))],
        name="gathered_to_sibling_" + "".join(str(t) for t in flips),
    )(*arrs)


_HBM = pl.BlockSpec(memory_space=pltpu.HBM)
_SEM = pl.BlockSpec(memory_space=pltpu.SEMAPHORE)
_EFFECT = pltpu.SideEffectType.DATAFLOW_SIDE_EFFECTING


class _SplitExchange:
    def __init__(self, name, srcs, land_shapes, n_copies, copies, riders=()):
        self.name, self.n, self.nl, self.copies = name, len(srcs), len(land_shapes), copies
        n, nb = self.n, len(srcs) + len(land_shapes)
        lands = [lax.empty(s.shape, s.dtype) for s in land_shapes]
        bufs = [pltpu.with_memory_space_constraint(a, pltpu.HBM) for a in (*srcs, *lands, *riders)]
        na = len(bufs)

        def body(*refs):
            send_sems, recv_sems = refs[na], refs[na + 1]
            for cp in copies(refs[:n], refs[n:nb], send_sems, recv_sems):
                cp.start()
            refs[-1][...] = jnp.zeros(refs[-1].shape, F32)

        outs = pl.pallas_call(
            body, name=name + "_start",
            in_specs=[_HBM] * na,
            out_specs=[_SEM, _SEM] + [_HBM] * na + [pl.BlockSpec(memory_space=pltpu.VMEM)],
            out_shape=[pltpu.SemaphoreType.DMA((n_copies,)), pltpu.SemaphoreType.DMA((n_copies,))]
            + [pltpu.HBM(b.shape, b.dtype) for b in bufs] + [jax.ShapeDtypeStruct((8, LANES), F32)],
            input_output_aliases={i: 2 + i for i in range(na)},
            compiler_params=pltpu.CompilerParams(has_side_effects=_EFFECT),
        )(*bufs)
        self.sems, self.bufs, self.riders, self.token = outs[:2], outs[2:2 + nb], outs[2 + nb:2 + na], outs[-1]

    def after(self):
        return self.token[0, 0]

    def wait(self, done, riders=(), bufs=None):
        n, nb, copies = self.n, self.n + self.nl, self.copies
        bufs = [*(self.bufs if bufs is None else bufs),
                *[pltpu.with_memory_space_constraint(a, pltpu.HBM) for a in riders]]
        na = len(bufs)
        done = list(done) if isinstance(done, (list, tuple)) else [done]

        def body(*refs):
            send_sems, recv_sems = refs[na], refs[na + 1]
            for cp in copies(refs[:n], refs[n:nb], send_sems, recv_sems):
                cp.wait_send()
                cp.wait_recv()

        outs = pl.pallas_call(
            body, name=self.name + "_wait",
            in_specs=[_HBM] * na + [_SEM, _SEM] + [_ANY] * len(done),
            out_specs=[_HBM] * na,
            out_shape=[pltpu.HBM(b.shape, b.dtype) for b in bufs],
            input_output_aliases={i: i for i in range(na)},
            compiler_params=pltpu.CompilerParams(has_side_effects=_EFFECT),
        )(*bufs, *self.sems, *done)
        return outs[:n], outs[n:nb], outs[nb:]


def _sibling_copies(srcs, lands, send_sems, recv_sems):
    x, y, c = _place()
    out = []
    for a, (src, land) in enumerate(zip(srcs, lands)):
        h = src.shape[1] // 2
        theirs = pl.ds(pl.multiple_of((1 - c) * h, 8), h)
        out.append(_remote(src.at[:, theirs], land, send_sems, recv_sems, a, (x, y, 1 - c)))
    return out


def _grads_to_sibling(name, grads):
    shapes = [jax.ShapeDtypeStruct((4, g.shape[1] // 2, g.shape[2]), g.dtype) for g in grads]
    return _SplitExchange(name, grads, shapes, len(grads), _sibling_copies)


def _chip_copies(srcs, lands, send_sems, recv_sems):
    x, y, c = _place()
    out = []
    for a, (src, land) in enumerate(zip(srcs, lands)):
        for t, (fx, fy) in enumerate(_CHIP_FLIPS):
            tx, ty = _flip(x, fx), _flip(y, fy)
            out.append(_remote(src.at[2 * tx + ty], land.at[t], send_sems, recv_sems, 3 * a + t, (tx, ty, c)))
    return out


def _grads_to_chips(name, parts):
    shapes = [jax.ShapeDtypeStruct((3, *p.shape[1:]), p.dtype) for p in parts]
    return _SplitExchange(name, parts, shapes, 3 * len(parts), _chip_copies)


def _add_halves(g, r, half):
    _, rows, cols = g.shape
    h = rows // 2
    tm = min(h, 128)
    nt = h // tm

    def body(half_ref, g_ref, r_ref, b_ref):
        b_ref[...] = (g_ref[...] + r_ref[...]).astype(b_ref.dtype)

    spec = pl.BlockSpec((1, tm, cols), lambda j, i, half_ref: (j, i, 0))
    return pl.pallas_call(
        body,
        grid_spec=pltpu.PrefetchScalarGridSpec(
            num_scalar_prefetch=1, grid=(4, nt),
            in_specs=[pl.BlockSpec((1, tm, cols), lambda j, i, half_ref: (j, half_ref[0] * nt + i, 0)), spec],
            out_specs=spec),
        out_shape=jax.ShapeDtypeStruct((4, h, cols), BF16),
        name="add_sibling_grads",
    )(half, g, r)


def _add_chips(g, r, recv, where):
    _, h, cols = r.shape
    tm = min(h, 128)
    nt = h // tm

    def body(where_ref, g_ref, r_ref, recv_ref, out_ref):
        own = g_ref[0] + r_ref[0]
        out_ref[...] = ((own + recv_ref[0].astype(F32)) + recv_ref[1].astype(F32)) + recv_ref[2].astype(F32)

    return pl.pallas_call(
        body,
        grid_spec=pltpu.PrefetchScalarGridSpec(
            num_scalar_prefetch=1, grid=(nt,),
            in_specs=[pl.BlockSpec((1, tm, cols), lambda i, w: (w[0], w[1] * nt + i, 0)),
                      pl.BlockSpec((1, tm, cols), lambda i, w: (w[0], i, 0)),
                      pl.BlockSpec((3, tm, cols), lambda i, w: (0, i, 0))],
            out_specs=pl.BlockSpec((tm, cols), lambda i, w: (w[1] * nt + i, 0))),
        out_shape=jax.ShapeDtypeStruct((2 * h, cols), F32),
        name="add_chip_grads",
    )(where, g, r, recv)


def _share_halves(shards):
    n = len(shards)

    def body(*refs):
        outs = refs[n:2 * n]
        send_sems, recv_sems = refs[2 * n:]
        x, y, c = _place()
        copies = []
        for a in range(n):
            h = outs[a].shape[0] // 2
            mine = outs[a].at[pl.ds(pl.multiple_of(c * h, 8), h)]
            copies.append(_remote(mine, mine, send_sems, recv_sems, a, (x, y, 1 - c)))
        for cp in copies:
            cp.start()
        for a, cp in enumerate(copies):
            cp.wait_send()
            h = outs[a].shape[0] // 2
            theirs = outs[a].at[pl.ds(pl.multiple_of((1 - c) * h, 8), h)]
            _remote(theirs, theirs, send_sems, recv_sems, a, (x, y, 1 - c)).wait_recv()

    return pl.pallas_call(
        body, in_specs=[_ANY] * n, out_specs=[_ANY] * n,
        out_shape=[jax.ShapeDtypeStruct(p.shape, p.dtype) for p in shards],
        input_output_aliases={a: a for a in range(n)},
        scratch_shapes=[pltpu.SemaphoreType.DMA((n,)), pltpu.SemaphoreType.DMA((n,))],
        name="share_reduced_halves",
    )(*shards)


def _reduce_small(rows):
    cols = rows[0].shape[1]
    n = len(rows)
    assert sum(r.shape[0] for r in rows) <= 8

    def body(*refs):
        ins, out_ref = refs[:n], refs[n]
        vec_ref, gath_ref, send_sems, recv_sems = refs[n + 1:]
        x, y, c = _place()
        me = 4 * x + 2 * y + c
        vec_ref[...] = jnp.zeros(vec_ref.shape, F32)
        at = 0
        for r in ins:
            vec_ref[at:at + r.shape[0], :] = r[...]
            at += r.shape[0]
        copies = []
        for k in range(1, 8):
            peer = (_flip(x, (k >> 2) & 1), _flip(y, (k >> 1) & 1), _flip(c, k & 1))
            copies.append(_remote(vec_ref, gath_ref.at[me], send_sems, recv_sems, k - 1, peer))
        for cp in copies:
            cp.start()
        gath_ref[me] = vec_ref[...]
        for cp in copies:
            cp.wait()
        tot = gath_ref[0]
        for dev in range(1, 8):
            tot = tot + gath_ref[dev]
        out_ref[...] = tot
        out_ref[7:8, :] = jnp.zeros((1, cols), F32) + jnp.sum(tot[7:8, :])

    vm = pl.BlockSpec(memory_space=pltpu.VMEM)
    return pl.pallas_call(
        body, in_specs=[vm] * n, out_specs=vm,
        out_shape=jax.ShapeDtypeStruct((8, cols), F32),
        scratch_shapes=[pltpu.VMEM((8, cols), F32), pltpu.VMEM((8, 8, cols), F32),
                        pltpu.SemaphoreType.DMA((7,)), pltpu.SemaphoreType.DMA((7,))],
        name="reduce_small",
    )(*rows)


def _adamw(w, g, m, v, name):
    rows, cols = w.shape
    tm = 128 if rows % 128 == 0 else rows

    def body(w_ref, g_ref, m_ref, v_ref, d_ref, m2_ref, v2_ref, gout_ref):
        gr = g_ref[...]
        m2 = ADAM_B1 * m_ref[...] + (1.0 - ADAM_B1) * gr
        v2 = ADAM_B2 * v_ref[...] + (1.0 - ADAM_B2) * (gr * gr)
        m_hat = m2 / (1.0 - ADAM_B1 ** ADAM_STEP)
        v_hat = v2 / (1.0 - ADAM_B2 ** ADAM_STEP)
        d_ref[...] = -ADAM_LR * (m_hat / (jnp.sqrt(v_hat) + ADAM_EPS) + ADAM_WD * w_ref[...])
        m2_ref[...] = m2
        v2_ref[...] = v2
        gout_ref[...] = gr

    spec = pl.BlockSpec((tm, cols), lambda i: (i, 0))
    sds = jax.ShapeDtypeStruct((rows, cols), F32)
    return pl.pallas_call(body, grid=(rows // tm,), in_specs=[spec] * 4, out_specs=[spec] * 4,
                          out_shape=[sds] * 4, name=name)(w, g, m, v)


def kernel(x, norm_g, w_in, b_merge, conv_w, w_out_conv, w_out_attn, w_o, final_g, loss_target, m_norm_g, m_w_in, m_b_merge, m_conv_w, m_w_out_conv, m_w_out_attn, m_w_o, m_final_g, v_norm_g, v_w_in, v_b_merge, v_conv_w, v_w_out_conv, v_w_out_attn, v_w_o, v_final_g):
    mx, my, mc = _place()
    chip = (2 * mx + my).astype(jnp.int32)
    seq = x.shape[1]

    chip1 = chip.reshape(1)
    slots = [_place_shard(w[0], chip1, MXU_DTYPE) for w in (w_in, w_out_conv, w_out_attn, w_o)]
    taps_slot = _place_shard(jnp.pad(conv_w[0], ((0, 5), (0, 0))), chip1, F32)
    gather_near = _SplitExchange("gather_w_in_near", [slots[0], taps_slot], [], 4,
                                 _gather_copies_to((0, 1), whole=(1,)))
    st = _Step(x[0], loss_target[0], norm_g, chip1, after=gather_near.token)
    st.project_own(w_in[0])
    near, _, _ = gather_near.wait([st.ut, st.proj_own])
    gather_far = _SplitExchange("gather_w_in_far", near, [], 2, _gather_copies_to((2,), whole=(1,)))
    (w4,) = _forward_to_sibling(gather_far.bufs[:1], flips=(0, 1))
    st.project(w4, (2, 1))
    (w4, taps4), _, out_slots = gather_far.wait([st.proj_own], riders=slots[1:], bufs=[w4, gather_far.bufs[1]])
    gather_out = _SplitExchange("gather_w_out", out_slots, [], 9, _gather_copies_to((0, 1, 2)), riders=[w4])
    (w4,) = _forward_to_sibling(gather_out.riders, flips=(2,))
    st.project(w4, (3,))
    st.mixers(w4, jnp.concatenate([taps4[j, :3, :] for j in range(4)], axis=1))
    out_ws, _, _ = gather_out.wait(st.o)
    woc, woa, wo = [w.reshape(D_MODEL, D_MODEL) for w in _forward_to_sibling(out_ws)]
    st.merge_and_loss(woc, woa, wo, b_merge, final_g.reshape(1, D_MODEL))

    half = mc.astype(jnp.int32).reshape(1)
    where = jnp.stack([chip, mc.astype(jnp.int32)])
    out_grads = [g.reshape(4, -1, D_MODEL) for g in st.out_weight_grads()]
    to_sibling = _grads_to_sibling("out_grads_to_sibling", out_grads)
    st.conv_grads(after=to_sibling.after())
    out_grads, out_from_sibling, _ = to_sibling.wait(st.da4)
    to_chips = _grads_to_chips("out_grads_to_chips",
                               [_add_halves(g, r, half) for g, r in zip(out_grads, out_from_sibling)])
    d_w4 = st.in_weight_grad(after=to_chips.after())
    out_from_chips = to_chips.wait(st.dc3)[1]

    to_sibling = _grads_to_sibling("in_grad_to_sibling", [d_w4])
    gx_lo, dg_lo = st.input_grad(0, after=to_sibling.after())
    (d_w4,), (from_sibling,), _ = to_sibling.wait(gx_lo)
    to_chips = _grads_to_chips("in_grad_to_chips", [_add_halves(d_w4, from_sibling, half)])
    gx_hi, dg_hi = st.input_grad(1, after=to_chips.after())
    grad_x = _to_natural(gx_lo, gx_hi)

    where_late = where + to_chips.after().astype(jnp.int32)
    out_reduced = [_add_chips(g, r, recv, where_late)
                   for g, r, recv in zip(out_grads, out_from_sibling, out_from_chips)]
    g_woc, g_woa, g_wo = _share_halves(out_reduced)
    small = _reduce_small([dg_lo + dg_hi, st.d_bias.reshape(2, D_MODEL), st.d_taps, st.d_final_g,
                           st.loss8.reshape(1, D_MODEL)])
    loss = (0.5 / D_MODEL) * small[7, 0]
    g_taps = lax.dynamic_slice(small[3:6], (0, chip * (D_MODEL // 4)), (3, D_MODEL // 4))
    upd = {
        "norm_g": _adamw(norm_g, small[0:1], m_norm_g, v_norm_g, "adamw_norm_g"),
        "b_merge": _adamw(b_merge, small[1:3].reshape(1, 2 * D_MODEL), m_b_merge, v_b_merge, "adamw_b_merge"),
        "conv_w": _adamw(conv_w[0], g_taps, m_conv_w[0], v_conv_w[0], "adamw_conv_w"),
        "w_out_conv": _adamw(w_out_conv[0], g_woc, m_w_out_conv[0], v_w_out_conv[0], "adamw_w_out_conv"),
        "w_out_attn": _adamw(w_out_attn[0], g_woa, m_w_out_attn[0], v_w_out_attn[0], "adamw_w_out_attn"),
        "w_o": _adamw(w_o[0], g_wo, m_w_o[0], v_w_o[0], "adamw_w_o"),
        "final_g": _adamw(final_g.reshape(1, D_MODEL), small[6:7], m_final_g.reshape(1, D_MODEL),
                          v_final_g.reshape(1, D_MODEL), "adamw_final_g"),
    }
    behind = [grad_x] + [u[0] for u in upd.values()]
    in_reduced = _add_chips(d_w4, from_sibling, to_chips.wait(behind)[1][0], where)
    (g_w_in,) = _share_halves([in_reduced])
    upd["w_in"] = _adamw(w_in[0], g_w_in, m_w_in[0], v_w_in[0], "adamw_w_in")

    names = ["norm_g", "w_in", "b_merge", "conv_w", "w_out_conv", "w_out_attn", "w_o", "final_g"]
    shapes = [norm_g.shape, w_in.shape, b_merge.shape, conv_w.shape, w_out_conv.shape, w_out_attn.shape,
              w_o.shape, final_g.shape]
    outs = [loss, grad_x.reshape(1, seq, D_MODEL)]
    for k in (3, 0, 1, 2):
        outs += [upd[n][k].reshape(s) for n, s in zip(names, shapes)]
    return tuple(outs)
```

```python
import functools

import numpy as np
import jax
import jax.numpy as jnp
from jax import lax
from jax.experimental import pallas as pl
from jax.experimental.pallas import tpu as pltpu

F32 = jnp.float32
BF16 = jnp.bfloat16
MXU_DTYPE = jnp.bfloat16
ACT_DTYPE = jnp.bfloat16

D_MODEL = 1024
N_HEADS = 16
HEAD_DIM = 64
QB = 128
N_RES = 16
LANES = 128
HP = N_HEADS * HEAD_DIM // LANES
IN_COLS = 10 * D_MODEL
SHARD_COLS = IN_COLS // 4
EPS = 1e-6
NEG = -1e30

ADAM_LR, ADAM_B1, ADAM_B2, ADAM_EPS, ADAM_WD, ADAM_STEP = 0.001, 0.9, 0.999, 1e-08, 0.01, 10

PATTERNS = {1: (16, 8), 4: (4, 32), 16: (1, 128)}

_NN = (((1,), (0,)), ((), ()))
_NT = (((1,), (1,)), ((), ()))


def _dot(a, b):
    return lax.dot_general(a.astype(MXU_DTYPE), b.astype(MXU_DTYPE), _NN, preferred_element_type=F32)


def _dot_nt(a, b):
    return lax.dot_general(a.astype(MXU_DTYPE), b.astype(MXU_DTYPE), _NT, preferred_element_type=F32)


def _split3(x):
    hi = x.astype(BF16)
    r1 = x - hi.astype(F32)
    mid = r1.astype(BF16)
    lo = (r1 - mid.astype(F32)).astype(BF16)
    return hi, mid, lo


def _select_cols(x, sel, terms):
    return sum(lax.dot_general(t, sel, _NN, preferred_element_type=F32) for t in _split3(x)[:terms])


def _sigmoid(z):
    return 1.0 / (1.0 + jnp.exp(-z))


def _head_expand_matrix():
    e = np.zeros((LANES, D_MODEL), np.float32)
    for h in range(N_HEADS):
        e[8 * h, HEAD_DIM * h:HEAD_DIM * (h + 1)] = 1.0
    return jnp.asarray(e, BF16)


def _head_sum_matrix():
    e = np.zeros((D_MODEL, LANES), np.float32)
    for h in range(N_HEADS):
        e[HEAD_DIM * h:HEAD_DIM * (h + 1), 8 * h:8 * (h + 1)] = 1.0
    return jnp.asarray(e, BF16)


def _attn_tables(d):
    g_n, rq = PATTERNS[d]
    q_n = g_n * rq
    gq, iq = np.arange(q_n) // rq, np.arange(q_n) % rq

    def tab(kn, base):
        k_n = g_n * kn
        gk, jk = np.arange(k_n) // kn, np.arange(k_n) % kn
        delta = g_n * (base + iq[:, None] - jk[None, :]) + gq[:, None] - gk[None, :]
        valid = (delta >= 0) & (delta <= QB)
        dist = np.where(valid, d * delta, 0).astype(np.float32)
        madd = np.where(valid, 0.0, NEG).astype(np.float32)
        return dist, madd

    d0, m0 = tab(rq if g_n == 1 else 2 * rq, 0)
    d1, m1 = tab(2 * rq, rq)
    return d0, m0, d1, m1


def _alibi_slopes():
    return jnp.exp2(-8.0 * jnp.arange(1, N_HEADS + 1, dtype=F32) / N_HEADS)


def _to_residue_major(x, tgt, after=None):
    s_n, c_n = x.shape
    lr = s_n // N_RES
    extra = [] if after is None else [after]

    def body(x_ref, t_ref, *rest):
        xo_ref, to_ref = rest[-2:]
        for r in range(N_RES):
            xo_ref[r] = x_ref[pl.ds(r, lr, stride=N_RES), :]
            to_ref[r] = t_ref[pl.ds(r, lr, stride=N_RES), :]

    nat = pl.BlockSpec((s_n, LANES), lambda j: (0, j))
    res = pl.BlockSpec((N_RES, lr, LANES), lambda j: (0, 0, j))
    xo, to = pl.pallas_call(
        body, grid=(c_n // LANES,),
        in_specs=[nat, nat] + [pl.BlockSpec((8, LANES), lambda j: (0, 0))] * len(extra),
        out_specs=[res, res],
        out_shape=[jax.ShapeDtypeStruct((N_RES, lr, c_n), F32)] * 2,
        name="perm_in",
    )(x, tgt, *extra)
    return xo.reshape(s_n, c_n), to.reshape(s_n, c_n)


def _to_natural(gx_lo, gx_hi):
    half_rows, c_n = gx_lo.shape
    lr = half_rows // (N_RES // 2)

    def body(lo_ref, hi_ref, o_ref):
        for r in range(N_RES):
            o_ref[pl.ds(r, lr, stride=N_RES), :] = lo_ref[r] if r < N_RES // 2 else hi_ref[r - N_RES // 2]

    half = pl.BlockSpec((N_RES // 2, lr, LANES), lambda j: (0, 0, j))
    return pl.pallas_call(
        body, grid=(c_n // LANES,),
        in_specs=[half, half],
        out_specs=pl.BlockSpec((2 * half_rows, LANES), lambda j: (0, j)),
        out_shape=jax.ShapeDtypeStruct((2 * half_rows, c_n), F32),
        name="perm_out",
    )(gx_lo.reshape(N_RES // 2, lr, c_n), gx_hi.reshape(N_RES // 2, lr, c_n))


def _rms_in(xp, norm_g):
    s_n, c_n = xp.shape
    tm = 512

    def body(x_ref, g_ref, u_ref, ut_ref):
        x = x_ref[...]
        r = lax.rsqrt(jnp.mean(x * x, axis=-1, keepdims=True) + EPS)
        u = x * r * g_ref[...]
        u_ref[...] = u.astype(u_ref.dtype)
        ut_ref[...] = u.T.astype(ut_ref.dtype)

    return pl.pallas_call(
        body, grid=(s_n // tm,),
        in_specs=[pl.BlockSpec((tm, c_n), lambda i: (i, 0)), pl.BlockSpec((1, c_n), lambda i: (0, 0))],
        out_specs=[pl.BlockSpec((tm, c_n), lambda i: (i, 0)), pl.BlockSpec((c_n, tm), lambda i: (0, i))],
        out_shape=[jax.ShapeDtypeStruct((s_n, c_n), ACT_DTYPE), jax.ShapeDtypeStruct((c_n, s_n), ACT_DTYPE)],
        name="rms_in",
    )(xp, norm_g)


def _in_proj(u, chip, w_own=None, w4=None, partial=None, others=()):
    s_n = u.shape[0]
    tn, cm = 512, 512
    per = SHARD_COLS // tn
    own = partial is None

    def body(chip_ref, a_ref, b_ref, *rest):
        o_ref = rest[-1]
        b = b_ref[...]
        for c in range(s_n // cm):
            o_ref[c * cm:(c + 1) * cm, :] = _dot(a_ref[c * cm:(c + 1) * cm, :], b).astype(o_ref.dtype)

    def shard(n, chip_ref):
        if own:
            return chip_ref[0]
        mask = others[-1]
        for i, m in enumerate(others[:-1]):
            mask = jnp.where(n // per == i, m, mask)
        return jnp.bitwise_xor(chip_ref[0], mask)

    w_spec = (pl.BlockSpec((D_MODEL, tn), lambda n, c: (0, n)) if own else
              pl.BlockSpec((None, D_MODEL, tn), lambda n, c: (shard(n, c), 0, n % per)))
    return pl.pallas_call(
        body,
        grid_spec=pltpu.PrefetchScalarGridSpec(
            num_scalar_prefetch=1, grid=(per if own else len(others) * per,),
            in_specs=[pl.BlockSpec((s_n, D_MODEL), lambda n, c: (0, 0)), w_spec] + ([] if own else [_ANY]),
            out_specs=pl.BlockSpec((s_n, tn), lambda n, c: (0, shard(n, c) * per + n % per))),
        out_shape=jax.ShapeDtypeStruct((s_n, IN_COLS), ACT_DTYPE),
        input_output_aliases={} if own else {3: 0},
        name="in_proj_own" if own else "in_proj_" + "_".join(str(m) for m in others),
    )(*([chip, u, w_own] if own else [chip, u, w4, partial]))


def _conv_terms(xc_ref, cg_ref, r, row, lr, cache):
    def a_of(q):
        if q not in cache:
            cache[q] = cg_ref[q].astype(F32) * xc_ref[q].astype(F32)
        return cache[q]

    def shift_down(v):
        return jnp.where(row >= 1, pltpu.roll(v, 1, 0), 0.0)

    a = a_of(r)
    am1 = a_of(r - 1) if r >= 1 else shift_down(a_of(N_RES - 1))
    am2 = a_of(r - 2) if r >= 2 else shift_down(a_of(N_RES - 2 + r))
    return a, am1, am2


def _conv_fwd(proj, conv_w):
    s_n = proj.shape[0]
    lr = s_n // N_RES
    pv = proj.reshape(N_RES, lr, IN_COLS)

    def body(xc_ref, bg_ref, cg_ref, zc_ref, w_ref, hc_ref, hct_ref):
        w = w_ref[...]
        row = lax.broadcasted_iota(jnp.int32, (lr, LANES), 0)
        products = {}
        for r in range(N_RES):
            a, am1, am2 = _conv_terms(xc_ref, cg_ref, r, row, lr, products)
            c = w[0:1] * am2 + w[1:2] * am1 + w[2:3] * a
            z = zc_ref[r].astype(F32)
            hc = z * _sigmoid(z) * bg_ref[r].astype(F32) * c
            hc_ref[r] = hc.astype(hc_ref.dtype)
            hct_ref[:, r * lr:(r + 1) * lr] = hc.T.astype(hct_ref.dtype)

    def col(part):
        return pl.BlockSpec((N_RES, lr, LANES), lambda j: (0, 0, part * 8 + j))

    hc, hct = pl.pallas_call(
        body, grid=(D_MODEL // LANES,),
        in_specs=[col(0), col(1), col(2), col(3), pl.BlockSpec((3, LANES), lambda j: (0, j))],
        out_specs=[pl.BlockSpec((N_RES, lr, LANES), lambda j: (0, 0, j)),
                   pl.BlockSpec((LANES, s_n), lambda j: (j, 0))],
        out_shape=[jax.ShapeDtypeStruct((N_RES, lr, D_MODEL), ACT_DTYPE),
                   jax.ShapeDtypeStruct((D_MODEL, s_n), ACT_DTYPE)],
        name="conv_fwd",
    )(pv, pv, pv, pv, conv_w)
    return hc.reshape(s_n, D_MODEL), hct


RES_PER_STEP = 8
ATTN_BATCH = 16

_BNT = (((2,), (2,)), ((0,), (0,)))
_BNN = (((2,), (1,)), ((0,), (0,)))


def _bdot(a, b, dims):
    return lax.dot_general(a.astype(MXU_DTYPE), b.astype(MXU_DTYPE), dims, preferred_element_type=F32)


def _pattern_view_shape(s_n, c_n, g_n, lead=()):
    lr = s_n // N_RES
    return (*lead, 4, 4, lr, c_n) if g_n == 4 else (*lead, N_RES, lr, c_n)


def _pattern_view(a, g_n, lead=()):
    return a.reshape(_pattern_view_shape(a.shape[-2], a.shape[-1], g_n, lead))


def _pattern_grid(g_n):
    return (N_RES // RES_PER_STEP if g_n == 1 else N_RES // g_n, HP)


def _pattern_spec(g_n, lr, col_of_hp, lead=()):
    z = (0,) * len(lead)
    if g_n == 16:
        return pl.BlockSpec((*lead, 16, lr, LANES), lambda r, hp: (*z, 0, 0, col_of_hp(hp)))
    if g_n == 4:
        return pl.BlockSpec((*lead, 4, None, lr, LANES), lambda r, hp: (*z, 0, r, 0, col_of_hp(hp)))
    return pl.BlockSpec((*lead, RES_PER_STEP, lr, LANES), lambda r, hp: (*z, r, 0, col_of_hp(hp)))


def _aligned(start, m):
    return start if isinstance(start, int) else pl.multiple_of(start, m)


class _Units:
    def __init__(self, g_n, rq):
        self.g_n, self.rq = g_n, rq
        self.per_res, self.paired = g_n == 1, rq == 8

    def plan(self, lr, size):
        if self.per_res:
            return [0], lr // self.rq - 1, lambda j: [pl.multiple_of(j * self.rq, self.rq)]
        step = 16 if self.paired else self.rq
        per = min(size // 2 if self.paired else size, lr // step)
        assert (lr // step) % per == 0
        return ([i * step for i in range(per)], lr // step // per - 1,
                lambda j: [pl.multiple_of((j * per + i) * step, step) for i in range(per)])

    def count(self, qs):
        return RES_PER_STEP if self.per_res else len(qs) * (2 if self.paired else 1)

    def _split(self, tiles, lo, rows):
        return tiles[:, lo:lo + rows].reshape(self.g_n * rows, LANES)

    def load_q(self, ref, qs):
        rq = self.rq
        if self.per_res:
            return ref[:, pl.ds(qs[0], rq), :]
        if self.paired:
            tiles = [ref[:, pl.ds(q, 16), :].astype(F32) for q in qs]
            return jnp.stack([self._split(t, lo, 8) for t in tiles for lo in (0, 8)])
        return jnp.stack([ref[:, pl.ds(q, rq), :].reshape(self.g_n * rq, LANES) for q in qs])

    def _key_rows(self, q, at_start):
        return (0, 2 * self.rq) if at_start else (_aligned(q - self.rq, self.rq), 2 * self.rq)

    def load_k(self, ref, qs, first):
        rq = self.rq
        if self.per_res:
            return ref[:, pl.ds(0, rq), :] if first else ref[:, pl.ds(_aligned(qs[0] - rq, rq), 2 * rq), :]
        if self.paired:
            out = []
            for i, q in enumerate(qs):
                if first and i == 0:
                    t = ref[:, 0:16, :].astype(F32)
                    out += [self._split(t, 0, 16)] * 2
                else:
                    t = ref[:, pl.ds(_aligned(q - 16, 16), 32), :].astype(F32)
                    out += [self._split(t, 8, 16), self._split(t, 16, 16)]
            return jnp.stack(out)
        rows = [self._key_rows(q, first and i == 0) for i, q in enumerate(qs)]
        return jnp.stack([ref[:, pl.ds(k0, n), :].reshape(self.g_n * n, LANES) for k0, n in rows])

    def store_q(self, ref, qs, val, add=False, lead=()):
        if self.per_res:
            pieces = [(qs[0], self.rq, val)]
        elif self.paired:
            pieces = [(q, 16, jnp.concatenate([val[2 * i].reshape(self.g_n, 8, LANES),
                                               val[2 * i + 1].reshape(self.g_n, 8, LANES)], axis=1))
                      for i, q in enumerate(qs)]
        else:
            pieces = [(q, self.rq, val[i].reshape(self.g_n, self.rq, LANES)) for i, q in enumerate(qs)]
        for start, rows, v in pieces:
            idx = (*lead, slice(None), pl.ds(start, rows), slice(None))
            ref[idx] = (ref[idx] + v if add else v).astype(ref.dtype)

    def add_k(self, ref, qs, val, first):
        rq = self.rq
        if self.per_res:
            k0, n = (0, rq) if first else (_aligned(qs[0] - rq, rq), 2 * rq)
            ref[:, pl.ds(k0, n), :] += val
            return
        if self.paired:
            starts = [s for i, q in enumerate(qs)
                      for s in ((0, 0) if first and i == 0 else (_aligned(q - 8, 8), q))]
            rows = [(s, 16) for s in starts]
        else:
            rows = [self._key_rows(q, first and i == 0) for i, q in enumerate(qs)]
        for b, (k0, n) in enumerate(rows):
            ref[:, pl.ds(k0, n), :] += val[b].reshape(self.g_n, n, LANES)


def _batch_bias(un, qs, at_start, first_ref, general_ref):
    if not at_start:
        return general_ref[...][None]
    if un.per_res:
        return first_ref[...][None]
    return jnp.concatenate([first_ref[...][None]] + [general_ref[...][None]] * (un.count(qs) - 1), axis=0)


def _stack_heads(x, low):
    zero = jnp.zeros_like(x)
    return jnp.concatenate([jnp.where(low, x, zero), jnp.where(low, zero, x)], axis=1)


def _attn_fwd(proj, slopes, d):
    g_n, rq = PATTERNS[d]
    un = _Units(g_n, rq)
    s_n = proj.shape[0]
    lr = s_n // N_RES
    q_n = g_n * rq
    d0, m0, d1, m1 = _attn_tables(d)
    first, n_more, later = un.plan(lr, ATTN_BATCH)

    def body(sl_ref, q_ref, k_ref, v_ref, d0_ref, m0_ref, d1_ref, m1_ref, o_ref, lse_ref, b0_ref, b1_ref):
        hp = pl.program_id(1)

        @pl.when(hp == 0)
        def _():
            lse_ref[...] = jnp.zeros(lse_ref.shape, F32)

        for h in (0, 1):
            slope = sl_ref[2 * hp + h]
            b0_ref[h * q_n:(h + 1) * q_n, :] = m0_ref[...] - slope * d0_ref[...]
            b1_ref[h * q_n:(h + 1) * q_n, :] = m1_ref[...] - slope * d1_ref[...]

        lane = lax.broadcasted_iota(jnp.int32, (1, q_n, LANES), 2)
        low = lane < HEAD_DIM
        grp = lane // 8

        def batch(qs, at_start):
            qq = _stack_heads(un.load_q(q_ref, qs) * 0.125, low)
            s = _bdot(qq, un.load_k(k_ref, qs, at_start), _BNT) + _batch_bias(un, qs, at_start, b0_ref, b1_ref)
            m = jnp.max(s, axis=2, keepdims=True)
            p = jnp.exp(s - m)
            l = jnp.sum(p, axis=2, keepdims=True)
            o = _bdot(p, un.load_k(v_ref, qs, at_start), _BNN) * (1.0 / l)
            lse = m + jnp.log(l)
            un.store_q(o_ref, qs, jnp.where(low, o[:, :q_n], o[:, q_n:]))
            upd = jnp.where(grp == 2 * hp, lse[:, :q_n], 0.0) + jnp.where(grp == 2 * hp + 1, lse[:, q_n:], 0.0)
            un.store_q(lse_ref, qs, upd, add=True)

        batch(first, True)

        def more(j, carry):
            batch(later(j), False)
            return carry

        lax.fori_loop(1, 1 + n_more, more, 0)

    pv = _pattern_view(proj, g_n)
    full = lambda a: pl.BlockSpec(a.shape, lambda r, hp: (0, 0))
    o, lse = pl.pallas_call(
        body, grid=_pattern_grid(g_n),
        in_specs=[pl.BlockSpec(memory_space=pltpu.SMEM),
                  _pattern_spec(g_n, lr, lambda hp: 32 + hp),
                  _pattern_spec(g_n, lr, lambda hp: 40 + hp),
                  _pattern_spec(g_n, lr, lambda hp: 48 + hp),
                  full(d0), full(m0), full(d1), full(m1)],
        out_specs=[_pattern_spec(g_n, lr, lambda hp: hp), _pattern_spec(g_n, lr, lambda hp: 0)],
        out_shape=[jax.ShapeDtypeStruct(_pattern_view_shape(s_n, D_MODEL, g_n), ACT_DTYPE),
                   jax.ShapeDtypeStruct(_pattern_view_shape(s_n, LANES, g_n), F32)],
        scratch_shapes=[pltpu.VMEM((2 * q_n, d0.shape[1]), F32), pltpu.VMEM((2 * q_n, 2 * q_n), F32)],
        name=f"attn_fwd_d{d}",
    )(slopes, pv, pv, pv, d0, m0, d1, m1)
    return o.reshape(s_n, D_MODEL), lse.reshape(s_n, LANES)


def _attn_combine(outs, lses, proj):
    s_n = proj.shape[0]
    tm = 512

    def body(o1_ref, o2_ref, o3_ref, l1_ref, l2_ref, l3_ref, za_ref, e_ref, o_ref, lse_ref, ha_ref, hat_ref):
        ls = [l1_ref[...], l2_ref[...], l3_ref[...]]
        mx = jnp.maximum(jnp.maximum(ls[0], ls[1]), ls[2])
        den = sum(jnp.exp(l - mx) for l in ls)
        lse = mx + jnp.log(den)
        lse_ref[...] = lse
        o = jnp.zeros((tm, D_MODEL), F32)
        for l, oref in zip(ls, (o1_ref, o2_ref, o3_ref)):
            o = o + _select_cols(jnp.exp(l - lse), e_ref[...], terms=2) * oref[...].astype(F32)
        o_ref[...] = o.astype(o_ref.dtype)
        z = za_ref[...].astype(F32)
        ha = z * _sigmoid(z) * o
        ha_ref[...] = ha.astype(ha_ref.dtype)
        hat_ref[...] = ha.T.astype(hat_ref.dtype)

    row = lambda w: pl.BlockSpec((tm, w), lambda i: (i, 0))
    return pl.pallas_call(
        body, grid=(s_n // tm,),
        in_specs=[row(D_MODEL)] * 3 + [row(LANES)] * 3
        + [pl.BlockSpec((tm, D_MODEL), lambda i: (i, 7)), pl.BlockSpec((LANES, D_MODEL), lambda i: (0, 0))],
        out_specs=[row(D_MODEL), row(LANES), row(D_MODEL), pl.BlockSpec((D_MODEL, tm), lambda i: (0, i))],
        out_shape=[jax.ShapeDtypeStruct((s_n, D_MODEL), ACT_DTYPE), jax.ShapeDtypeStruct((s_n, LANES), F32),
                   jax.ShapeDtypeStruct((s_n, D_MODEL), ACT_DTYPE), jax.ShapeDtypeStruct((D_MODEL, s_n), ACT_DTYPE)],
        name="attn_combine",
    )(*outs, *lses, proj, _head_expand_matrix())


CHAIN_ROWS = 256


def _row_chains(tm):
    return [slice(r, r + CHAIN_ROWS) for r in range(0, tm, CHAIN_ROWS)]


def _gates(gc_ref, ga_ref, b_ref, rows):
    b = b_ref[...]
    gc = _sigmoid(gc_ref[rows, :].astype(F32) + b[:, :D_MODEL])
    ga = _sigmoid(ga_ref[rows, :].astype(F32) + b[:, D_MODEL:])
    return gc, ga


def _merge_loss(hc, ha, woc, woa, wo, proj, b_merge, xp, final_g, tgt):
    s_n = xp.shape[0]
    tm = 512

    def body(hc_ref, ha_ref, woc_ref, woa_ref, wo_ref, gc_ref, ga_ref, b_ref, x_ref, gf_ref, t_ref,
             yc_ref, ya_ref, dhb_ref, dgf_ref, loss_ref, dwo_ref, mgt_ref):
        i = pl.program_id(0)

        @pl.when(i == 0)
        def _():
            dgf_ref[...] = jnp.zeros(dgf_ref.shape, F32)
            loss_ref[...] = jnp.zeros(loss_ref.shape, F32)
            dwo_ref[...] = jnp.zeros(dwo_ref.shape, F32)

        gf = gf_ref[...]
        for rows in _row_chains(tm):
            yc = _dot(hc_ref[rows, :], woc_ref[...])
            ya = _dot(ha_ref[rows, :], woa_ref[...])
            gc, ga = _gates(gc_ref, ga_ref, b_ref, rows)
            mg = gc * yc + ga * ya
            yc_ref[rows, :] = yc.astype(yc_ref.dtype)
            ya_ref[rows, :] = ya.astype(ya_ref.dtype)
            mgt_ref[:, rows] = mg.T.astype(mgt_ref.dtype)
            h2 = x_ref[rows, :] + _dot(mg, wo_ref[...])
            r2 = lax.rsqrt(jnp.mean(h2 * h2, axis=-1, keepdims=True) + EPS)
            nrm = h2 * r2
            err = nrm * gf - t_ref[rows, :]
            e2 = (err * err).reshape(-1, 8, D_MODEL).sum(axis=0)
            loss_ref[...] += sum(e2[:, c * LANES:(c + 1) * LANES] for c in range(D_MODEL // LANES))
            dy = err * (1.0 / D_MODEL)
            dgf_ref[...] += jnp.sum(dy * nrm, axis=0, keepdims=True)
            dn = dy * gf
            dh2 = r2 * (dn - nrm * jnp.mean(dn * nrm, axis=-1, keepdims=True))
            dhb_ref[rows, :] = dh2.astype(dhb_ref.dtype)
        dwo_ref[...] += _dot(mgt_ref[...], dhb_ref[...])

    row = pl.BlockSpec((tm, D_MODEL), lambda i: (i, 0))
    wsp = pl.BlockSpec((D_MODEL, D_MODEL), lambda i: (0, 0), pipeline_mode=pl.Buffered(1))
    vec = lambda w: pl.BlockSpec((1, w), lambda i: (0, 0))
    act = jax.ShapeDtypeStruct((s_n, D_MODEL), ACT_DTYPE)
    return pl.pallas_call(
        body, grid=(s_n // tm,),
        in_specs=[row, row, wsp, wsp, wsp,
                  pl.BlockSpec((tm, D_MODEL), lambda i: (i, 8)), pl.BlockSpec((tm, D_MODEL), lambda i: (i, 9)),
                  vec(2 * D_MODEL), row, vec(D_MODEL), row],
        out_specs=[row, row, row, vec(D_MODEL), pl.BlockSpec((8, LANES), lambda i: (0, 0)),
                   pl.BlockSpec((D_MODEL, D_MODEL), lambda i: (0, 0))],
        out_shape=[act, act, act, jax.ShapeDtypeStruct((1, D_MODEL), F32), jax.ShapeDtypeStruct((8, LANES), F32),
                   jax.ShapeDtypeStruct((D_MODEL, D_MODEL), F32)],
        scratch_shapes=[pltpu.VMEM((D_MODEL, tm), MXU_DTYPE)],
        name="merge_loss",
    )(hc, ha, woc, woa, wo, proj, proj, b_merge, xp, final_g, tgt)


def _merge_bwd(dh2b, wo, woc, woa, yc, ya, proj, b_merge, o, hct, hat):
    s_n = dh2b.shape[0]
    tm = 512

    def body(dh_ref, wo_ref, woc_ref, woa_ref, yc_ref, ya_ref, gc_ref, ga_ref, b_ref, o_ref, za_ref, e_ref,
             hct_ref, hat_ref, dhc_ref, do_ref, dsum_ref, db3_ref, dbias_ref, dwoc_ref, dwoa_ref,
             dyc_ref, dya_ref):
        i = pl.program_id(0)

        @pl.when(i == 0)
        def _():
            dbias_ref[...] = jnp.zeros(dbias_ref.shape, F32)
            dwoc_ref[...] = jnp.zeros(dwoc_ref.shape, F32)
            dwoa_ref[...] = jnp.zeros(dwoa_ref.shape, F32)

        for rows in _row_chains(tm):
            dmg = _dot_nt(dh_ref[rows, :], wo_ref[...])
            gc, ga = _gates(gc_ref, ga_ref, b_ref, rows)
            dgc = dmg * yc_ref[rows, :].astype(F32) * gc * (1.0 - gc)
            dga = dmg * ya_ref[rows, :].astype(F32) * ga * (1.0 - ga)
            dbias_ref[:, :D_MODEL] += jnp.sum(dgc, axis=0, keepdims=True)
            dbias_ref[:, D_MODEL:] += jnp.sum(dga, axis=0, keepdims=True)
            dyc = dmg * gc
            dya = dmg * ga
            dyc_ref[rows, :] = dyc.astype(dyc_ref.dtype)
            dya_ref[rows, :] = dya.astype(dya_ref.dtype)
            dhc_ref[rows, :] = _dot_nt(dyc, woc_ref[...]).astype(dhc_ref.dtype)
            dha = _dot_nt(dya, woa_ref[...])
            z = za_ref[rows, :].astype(F32)
            sg = _sigmoid(z)
            ov = o_ref[rows, :].astype(F32)
            dout = dha * z * sg
            do_ref[rows, :] = dout.astype(do_ref.dtype)
            dsum_ref[rows, :] = _select_cols(dout * ov, e_ref[...], terms=2)
            db3_ref[0, rows, :] = (dha * ov * sg * (1.0 + z * (1.0 - sg))).astype(db3_ref.dtype)
            db3_ref[1, rows, :] = dgc.astype(db3_ref.dtype)
            db3_ref[2, rows, :] = dga.astype(db3_ref.dtype)
        dwoc_ref[...] += _dot(hct_ref[...], dyc_ref[...])
        dwoa_ref[...] += _dot(hat_ref[...], dya_ref[...])

    row = pl.BlockSpec((tm, D_MODEL), lambda i: (i, 0))
    col = pl.BlockSpec((D_MODEL, tm), lambda i: (0, i))
    wsp = pl.BlockSpec((D_MODEL, D_MODEL), lambda i: (0, 0), pipeline_mode=pl.Buffered(1))
    acc = pl.BlockSpec((D_MODEL, D_MODEL), lambda i: (0, 0))
    act = jax.ShapeDtypeStruct((s_n, D_MODEL), ACT_DTYPE)
    grad = jax.ShapeDtypeStruct((D_MODEL, D_MODEL), F32)
    return pl.pallas_call(
        body, grid=(s_n // tm,),
        in_specs=[row, wsp, wsp, wsp, row, row,
                  pl.BlockSpec((tm, D_MODEL), lambda i: (i, 8)), pl.BlockSpec((tm, D_MODEL), lambda i: (i, 9)),
                  pl.BlockSpec((1, 2 * D_MODEL), lambda i: (0, 0)), row,
                  pl.BlockSpec((tm, D_MODEL), lambda i: (i, 7)), pl.BlockSpec((D_MODEL, LANES), lambda i: (0, 0)),
                  col, col],
        out_specs=[row, row, pl.BlockSpec((tm, LANES), lambda i: (i, 0)),
                   pl.BlockSpec((3, tm, D_MODEL), lambda i: (0, i, 0)),
                   pl.BlockSpec((1, 2 * D_MODEL), lambda i: (0, 0)), acc, acc],
        out_shape=[act, act, jax.ShapeDtypeStruct((s_n, LANES), F32),
                   jax.ShapeDtypeStruct((3, s_n, D_MODEL), ACT_DTYPE),
                   jax.ShapeDtypeStruct((1, 2 * D_MODEL), F32), grad, grad],
        scratch_shapes=[pltpu.VMEM((tm, D_MODEL), MXU_DTYPE), pltpu.VMEM((tm, D_MODEL), MXU_DTYPE)],
        name="merge_bwd",
    )(dh2b, wo, woc, woa, yc, ya, proj, proj, b_merge, o, proj, _head_sum_matrix(), hct, hat)


def _conv_bwd(proj, conv_w, dhc):
    s_n = proj.shape[0]
    lr = s_n // N_RES
    pv = proj.reshape(N_RES, lr, IN_COLS)

    def body(xc_ref, bg_ref, cg_ref, zc_ref, w_ref, dhc_ref, da4_ref, dw_ref, dc_ref):
        w = w_ref[...]
        row = lax.broadcasted_iota(jnp.int32, (lr, LANES), 0)
        dw = [jnp.zeros((1, LANES), F32) for _ in range(3)]
        products = {}
        for r in range(N_RES):
            a, am1, am2 = _conv_terms(xc_ref, cg_ref, r, row, lr, products)
            c = w[0:1] * am2 + w[1:2] * am1 + w[2:3] * a
            z = zc_ref[r].astype(F32)
            sg = _sigmoid(z)
            sz = z * sg
            bg = bg_ref[r].astype(F32)
            dh = dhc_ref[r].astype(F32)
            da4_ref[1, r] = (dh * sz * c).astype(da4_ref.dtype)
            da4_ref[3, r] = (dh * bg * c * sg * (1.0 + z * (1.0 - sg))).astype(da4_ref.dtype)
            dc = dh * sz * bg
            dc_ref[r] = dc
            dw[0] = dw[0] + jnp.sum(dc * am2, axis=0, keepdims=True)
            dw[1] = dw[1] + jnp.sum(dc * am1, axis=0, keepdims=True)
            dw[2] = dw[2] + jnp.sum(dc * a, axis=0, keepdims=True)
        dw_ref[0:1, :] = dw[0]
        dw_ref[1:2, :] = dw[1]
        dw_ref[2:3, :] = dw[2]

        def shift_up(v):
            return jnp.where(row < lr - 1, pltpu.roll(v, lr - 1, 0), 0.0)

        for r in range(N_RES):
            dp1 = dc_ref[r + 1] if r + 1 < N_RES else shift_up(dc_ref[0])
            dp2 = dc_ref[r + 2] if r + 2 < N_RES else shift_up(dc_ref[r + 2 - N_RES])
            da = w[2:3] * dc_ref[r] + w[1:2] * dp1 + w[0:1] * dp2
            da4_ref[0, r] = (da * cg_ref[r].astype(F32)).astype(da4_ref.dtype)
            da4_ref[2, r] = (da * xc_ref[r].astype(F32)).astype(da4_ref.dtype)

    def col(part):
        return pl.BlockSpec((N_RES, lr, LANES), lambda j: (0, 0, part * 8 + j))

    da4, dw = pl.pallas_call(
        body, grid=(D_MODEL // LANES,),
        in_specs=[col(0), col(1), col(2), col(3), pl.BlockSpec((3, LANES), lambda j: (0, j)),
                  pl.BlockSpec((N_RES, lr, LANES), lambda j: (0, 0, j))],
        out_specs=[pl.BlockSpec((4, N_RES, lr, LANES), lambda j: (0, 0, 0, j)),
                   pl.BlockSpec((3, LANES), lambda j: (0, j))],
        out_shape=[jax.ShapeDtypeStruct((4, N_RES, lr, D_MODEL), ACT_DTYPE),
                   jax.ShapeDtypeStruct((3, D_MODEL), F32)],
        scratch_shapes=[pltpu.VMEM((N_RES, lr, LANES), F32)],
        name="conv_bwd",
    )(pv, pv, pv, pv, conv_w, dhc.reshape(N_RES, lr, D_MODEL))
    return da4.reshape(4, s_n, D_MODEL), dw


def _attn_bwd(proj, dout, lse, dsum, slopes, d, prev=None):
    g_n, rq = PATTERNS[d]
    un = _Units(g_n, rq)
    s_n = proj.shape[0]
    lr = s_n // N_RES
    q_n = g_n * rq
    d0, m0, d1, m1 = (np.ascontiguousarray(t.T) for t in _attn_tables(d))
    first, n_more, later = un.plan(lr, ATTN_BATCH)
    bsz = un.count(first)
    gd = RES_PER_STEP if un.per_res else g_n

    def body(sl_ref, q_ref, k_ref, v_ref, do_ref, lse_ref, ds_ref, d0_ref, m0_ref, d1_ref, m1_ref, *rest):
        prev_ref = rest[0] if prev is not None else None
        out_ref, b0_ref, b1_ref, lt_ref, dt_ref, dk_ref, dv_ref = rest[-7:]
        hp = pl.program_id(1)
        for h in (0, 1):
            slope = sl_ref[2 * hp + h]
            b0_ref[:, h * q_n:(h + 1) * q_n] = m0_ref[...] - slope * d0_ref[...]
            b1_ref[:, h * q_n:(h + 1) * q_n] = m1_ref[...] - slope * d1_ref[...]
        if prev is None:
            dk_ref[...] = jnp.zeros(dk_ref.shape, F32)
            dv_ref[...] = jnp.zeros(dv_ref.shape, F32)
        else:
            out_ref[0] = prev_ref[0]
            dk_ref[...] = prev_ref[1].astype(F32)
            dv_ref[...] = prev_ref[2].astype(F32)
        low = lax.broadcasted_iota(jnp.int32, (1, q_n, LANES), 2) < HEAD_DIM
        row16 = pl.multiple_of(16 * hp, 16)

        def query_rows(stat_ref, t_ref, qs):
            tiles = un.load_q(stat_ref, qs)
            for b in range(bsz):
                t_ref[b] = tiles[b].T
            t16 = t_ref[:, pl.ds(row16, 16), :]
            return jnp.concatenate([t16[:, 0:1, :], t16[:, 8:9, :]], axis=2)

        def batch(qs, at_start):
            qq = _stack_heads(un.load_q(q_ref, qs) * 0.125, low)
            dd = _stack_heads(un.load_q(do_ref, qs), low)
            ks = un.load_k(k_ref, qs, at_start)
            vs = un.load_k(v_ref, qs, at_start)
            lrow = query_rows(lse_ref, lt_ref, qs)
            drow = query_rows(ds_ref, dt_ref, qs)
            pt = jnp.exp(_bdot(ks, qq, _BNT) + _batch_bias(un, qs, at_start, b0_ref, b1_ref) - lrow)
            dst = pt * (_bdot(vs, dd, _BNT) - drow)
            un.add_k(dv_ref, qs, _bdot(pt, dd, _BNN), at_start)
            un.add_k(dk_ref, qs, _bdot(dst, qq, _BNN), at_start)
            dq = _bdot(jnp.swapaxes(dst, 1, 2), ks, _BNN)
            un.store_q(out_ref, qs, jnp.where(low, dq[:, :q_n], dq[:, q_n:]) * 0.125, add=prev is not None,
                       lead=(0,))

        batch(first, True)

        def more(j, carry):
            batch(later(j), False)
            return carry

        lax.fori_loop(1, 1 + n_more, more, 0)
        out_ref[1] = dk_ref[...].astype(out_ref.dtype)
        out_ref[2] = dv_ref[...].astype(out_ref.dtype)

    pv = _pattern_view(proj, g_n)
    full = lambda a: pl.BlockSpec(a.shape, lambda r, hp: (0, 0))
    whole = _pattern_spec(g_n, lr, lambda hp: hp, lead=(3,))
    out = pl.pallas_call(
        body, grid=_pattern_grid(g_n),
        in_specs=[pl.BlockSpec(memory_space=pltpu.SMEM),
                  _pattern_spec(g_n, lr, lambda hp: 32 + hp),
                  _pattern_spec(g_n, lr, lambda hp: 40 + hp),
                  _pattern_spec(g_n, lr, lambda hp: 48 + hp),
                  _pattern_spec(g_n, lr, lambda hp: hp),
                  _pattern_spec(g_n, lr, lambda hp: 0),
                  _pattern_spec(g_n, lr, lambda hp: 0),
                  full(d0), full(m0), full(d1), full(m1)] + ([] if prev is None else [whole]),
        out_specs=whole,
        out_shape=jax.ShapeDtypeStruct(_pattern_view_shape(s_n, D_MODEL, g_n, lead=(3,)), ACT_DTYPE),
        scratch_shapes=[pltpu.VMEM((d0.shape[0], 2 * q_n), F32), pltpu.VMEM((2 * q_n, 2 * q_n), F32),
                        pltpu.VMEM((bsz, LANES, q_n), F32), pltpu.VMEM((bsz, LANES, q_n), F32),
                        pltpu.VMEM((gd, lr, LANES), F32), pltpu.VMEM((gd, lr, LANES), F32)],
        name=f"attn_bwd_d{d}",
    )(slopes, pv, pv, pv, _pattern_view(dout, g_n), _pattern_view(lse, g_n), _pattern_view(dsum, g_n),
      d0, m0, d1, m1, *([] if prev is None else [_pattern_view(prev, g_n, lead=(3,))]))
    return out.reshape(3, s_n, D_MODEL)


def _part_index(step, per, lo, n):
    return jnp.clip(step // per - lo, 0, n - 1)


def _dw_in(ut, da4, dc3, db3):
    s_n = ut.shape[1]
    tn = 512
    per = D_MODEL // tn
    shard_blocks = SHARD_COLS // tn

    def body(a_ref, p0_ref, p1_ref, p2_ref, o_ref):
        part = pl.program_id(0) // per

        @pl.when(part < 4)
        def _():
            o_ref[...] = _dot(a_ref[...], p0_ref[...])

        @pl.when((part >= 4) & (part < 7))
        def _():
            o_ref[...] = _dot(a_ref[...], p1_ref[...])

        @pl.when(part >= 7)
        def _():
            o_ref[...] = _dot(a_ref[...], p2_ref[...])

    def pspec(lo, n):
        def index(j):
            part = j // per
            col = jnp.where(part < lo, 0, jnp.where(part >= lo + n, per - 1, j % per))
            return _part_index(j, per, lo, n), 0, col
        return pl.BlockSpec((None, s_n, tn), index)

    return pl.pallas_call(
        body, grid=(IN_COLS // tn,),
        in_specs=[pl.BlockSpec((D_MODEL, s_n), lambda j: (0, 0), pipeline_mode=pl.Buffered(1)),
                  pspec(0, 4), pspec(4, 3), pspec(7, 3)],
        out_specs=pl.BlockSpec((None, D_MODEL, tn), lambda j: (j // shard_blocks, 0, j % shard_blocks)),
        out_shape=jax.ShapeDtypeStruct((4, D_MODEL, SHARD_COLS), F32),
        name="dw_in",
    )(ut, da4, dc3, db3)


def _input_grad(da4, dc3, db3, w4, xp, norm_g, dh2, row0, rows):
    tm, tk = 256, 512
    per = D_MODEL // tk
    shard_blocks = SHARD_COLS // tk
    m0 = row0 // tm

    def body(p0_ref, p1_ref, p2_ref, w_ref, x_ref, g_ref, dh_ref, gx_ref, dg_ref):
        @pl.when(pl.program_id(0) == 0)
        def _():
            dg_ref[...] = jnp.zeros(dg_ref.shape, F32)

        du = None
        for k in range(IN_COLS // tk):
            part, cols = k // per, pl.ds((k % per) * tk, tk)
            ref, slot = (p0_ref, part) if part < 4 else (p1_ref, part - 4) if part < 7 else (p2_ref, part - 7)
            d = _dot_nt(ref[slot, :, cols], w_ref[k // shard_blocks, :, pl.ds((k % shard_blocks) * tk, tk)])
            du = d if du is None else du + d
        x = x_ref[...]
        r = lax.rsqrt(jnp.mean(x * x, axis=-1, keepdims=True) + EPS)
        nrm = x * r
        dg_ref[...] += jnp.sum(du * nrm, axis=0, keepdims=True)
        dn = du * g_ref[...]
        gx_ref[...] = dh_ref[...].astype(F32) + r * (dn - nrm * jnp.mean(dn * nrm, axis=-1, keepdims=True))

    def pspec(n):
        return pl.BlockSpec((n, tm, D_MODEL), lambda m: (0, m0 + m, 0))

    row_in = pl.BlockSpec((tm, D_MODEL), lambda m: (m0 + m, 0))
    vec = pl.BlockSpec((1, D_MODEL), lambda m: (0, 0))
    return pl.pallas_call(
        body, grid=(rows // tm,),
        in_specs=[pspec(4), pspec(3), pspec(3),
                  pl.BlockSpec(w4.shape, lambda m: (0, 0, 0), pipeline_mode=pl.Buffered(1)),
                  row_in, vec, row_in],
        out_specs=[pl.BlockSpec((tm, D_MODEL), lambda m: (m, 0)), vec],
        out_shape=[jax.ShapeDtypeStruct((rows, D_MODEL), F32), jax.ShapeDtypeStruct((1, D_MODEL), F32)],
        name="input_grad",
    )(da4, dc3, db3, w4, xp, norm_g, dh2)


class _Step:
    def __init__(self, x, tgt, norm_g, chip, after=None):
        self.norm_g, self.chip = norm_g, chip
        self.slopes = _alibi_slopes()
        self.xp, self.tp = _to_residue_major(x, tgt, after)
        self.u, self.ut = _rms_in(self.xp, norm_g)

    def project_own(self, w_own):
        self.proj_own = _in_proj(self.u, self.chip, w_own=w_own)

    def project(self, w4, others):
        self.proj_own = _in_proj(self.u, self.chip, w4=w4, partial=self.proj_own, others=others)

    def mixers(self, w4, taps):
        self.w4, self.taps, self.proj = w4, taps, self.proj_own
        self.hc, self.hct = _conv_fwd(self.proj, taps)
        fwd = [_attn_fwd(self.proj, self.slopes, d) for d in PATTERNS]
        self.o, self.lse, self.ha, self.hat = _attn_combine([f[0] for f in fwd], [f[1] for f in fwd], self.proj)

    def merge_and_loss(self, woc, woa, wo, b_merge, final_g):
        self.woc, self.woa, self.wo, self.b_merge = woc, woa, wo, b_merge
        (self.yc, self.ya, self.dh2b, self.d_final_g, self.loss8, self.d_wo) = _merge_loss(
            self.hc, self.ha, woc, woa, wo, self.proj, b_merge, self.xp, final_g, self.tp)

    def out_weight_grads(self):
        (self.dhc, self.dout, self.dsum, self.db3, self.d_bias, d_woc, d_woa) = _merge_bwd(
            self.dh2b, self.wo, self.woc, self.woa, self.yc, self.ya, self.proj, self.b_merge, self.o,
            self.hct, self.hat)
        return d_woc, d_woa, self.d_wo

    def conv_grads(self, after=0.0):
        self.da4, self.d_taps = _conv_bwd(self.proj, self.taps + after, self.dhc)

    def in_weight_grad(self, after=0.0):
        slopes = self.slopes + after
        self.dc3 = None
        for d in PATTERNS:
            self.dc3 = _attn_bwd(self.proj, self.dout, self.lse, self.dsum, slopes, d, prev=self.dc3)
        return _dw_in(self.ut, self.da4, self.dc3, self.db3)

    def input_grad(self, half, after=0.0):
        rows = self.xp.shape[0] // 2
        return _input_grad(self.da4, self.dc3, self.db3, self.w4, self.xp, self.norm_g + after, self.dh2b,
                           half * rows, rows)


def _local_grads(x, tgt, norm_g, w4, b_merge, conv_w, woc, woa, wo, final_g):
    st = _Step(x, tgt, norm_g, jnp.zeros((1,), jnp.int32))
    st.project_own(w4[0])
    st.project(w4, (2, 1))
    st.project(w4, (3,))
    st.mixers(w4, conv_w)
    st.merge_and_loss(woc, woa, wo, b_merge, final_g)
    d_woc, d_woa, d_wo = st.out_weight_grads()
    st.conv_grads()
    d_w4 = st.in_weight_grad()
    gx_lo, dg_lo = st.input_grad(0)
    gx_hi, dg_hi = st.input_grad(1)
    return (st.loss8, _to_natural(gx_lo, gx_hi), dg_lo + dg_hi, d_w4, st.d_bias, st.d_taps, d_woc, d_woa, d_wo,
            st.d_final_g)


MESH = pl.DeviceIdType.MESH
_CHIP_FLIPS = ((1, 0), (0, 1), (1, 1))
_ANY = pl.BlockSpec(memory_space=pl.ANY)


def _place():
    return lax.axis_index("x"), lax.axis_index("y"), lax.axis_index("c")


def _flip(v, f):
    return 1 - v if f else v


def _remote(src, dst, send_sems, recv_sems, k, device):
    return pltpu.make_async_remote_copy(src_ref=src, dst_ref=dst, send_sem=send_sems.at[k], recv_sem=recv_sems.at[k],
                                        device_id=device, device_id_type=MESH)


def _place_shard(w, chip, dtype):
    rows, cols = w.shape
    tm = min(rows, 128)

    def body(chip_ref, w_ref, o_ref):
        o_ref[0] = w_ref[...].astype(o_ref.dtype)

    return pl.pallas_call(
        body,
        grid_spec=pltpu.PrefetchScalarGridSpec(
            num_scalar_prefetch=1, grid=(rows // tm,),
            in_specs=[pl.BlockSpec((tm, cols), lambda i, chip_ref: (i, 0))],
            out_specs=pl.BlockSpec((1, tm, cols), lambda i, chip_ref: (chip_ref[0], i, 0))),
        out_shape=jax.ShapeDtypeStruct((4, rows, cols), dtype),
        name="place_shard",
    )(chip, w)


def _gather_copies_to(flips, whole=()):
    def copies(arrs, _, send_sems, recv_sems):
        x, y, c = _place()
        out = []
        for a, arr in enumerate(arrs):
            h = arr.shape[1] // 2
            mine = arr.at[2 * x + y] if a in whole else arr.at[2 * x + y, pl.ds(pl.multiple_of(c * h, 8), h)]
            for i, t in enumerate(flips):
                fx, fy = _CHIP_FLIPS[t]
                out.append(_remote(mine, mine, send_sems, recv_sems, len(flips) * a + i,
                                   (_flip(x, fx), _flip(y, fy), c)))
        return out
    return copies


def _forward_to_sibling(arrs, flips=(0, 1, 2)):
    n = len(arrs)

    def body(*refs):
        outs = refs[n:2 * n]
        send_sems, recv_sems = refs[2 * n:]
        x, y, c = _place()
        sibling = (x, y, 1 - c)
        started = []
        for a in range(n):
            h = outs[a].shape[1] // 2
            rows = pl.ds(pl.multiple_of(c * h, 8), h)
            for t in flips:
                fx, fy = _CHIP_FLIPS[t]
                landed = outs[a].at[2 * _flip(x, fx) + _flip(y, fy), rows]
                cp = _remote(landed, landed, send_sems, recv_sems, 3 * a + t, sibling)
                cp.start()
                started.append(cp)
        for a in range(n):
            h = outs[a].shape[1] // 2
            rows = pl.ds(pl.multiple_of((1 - c) * h, 8), h)
            for t in flips:
                fx, fy = _CHIP_FLIPS[t]
                handed = outs[a].at[2 * _flip(x, fx) + _flip(y, fy), rows]
                _remote(handed, handed, send_sems, recv_sems, 3 * a + t, sibling).wait_recv()
        for cp in started:
            cp.wait_send()

    return pl.pallas_call(
        body, in_specs=[_ANY] * n, out_specs=[_ANY] * n,
        out_shape=[jax.ShapeDtypeStruct(s.shape, s.dtype) for s in arrs],
        input_output_aliases={a: a for a in range(n)},
        scratch_shapes=[pltpu.SemaphoreType.DMA((3 * n,)), pltpu.SemaphoreType.DMA((3 * n,))],
        name="gathered_to_sibling_" + "".join(str(t) for t in flips),
    )(*arrs)


_HBM = pl.BlockSpec(memory_space=pltpu.HBM)
_SEM = pl.BlockSpec(memory_space=pltpu.SEMAPHORE)
_EFFECT = pltpu.SideEffectType.DATAFLOW_SIDE_EFFECTING


class _SplitExchange:
    def __init__(self, name, srcs, land_shapes, n_copies, copies, riders=()):
        self.name, self.n, self.nl, self.copies = name, len(srcs), len(land_shapes), copies
        n, nb = self.n, len(srcs) + len(land_shapes)
        lands = [lax.empty(s.shape, s.dtype) for s in land_shapes]
        bufs = [pltpu.with_memory_space_constraint(a, pltpu.HBM) for a in (*srcs, *lands, *riders)]
        na = len(bufs)

        def body(*refs):
            send_sems, recv_sems = refs[na], refs[na + 1]
            for cp in copies(refs[:n], refs[n:nb], send_sems, recv_sems):
                cp.start()
            refs[-1][...] = jnp.zeros(refs[-1].shape, F32)

        outs = pl.pallas_call(
            body, name=name + "_start",
            in_specs=[_HBM] * na,
            out_specs=[_SEM, _SEM] + [_HBM] * na + [pl.BlockSpec(memory_space=pltpu.VMEM)],
            out_shape=[pltpu.SemaphoreType.DMA((n_copies,)), pltpu.SemaphoreType.DMA((n_copies,))]
            + [pltpu.HBM(b.shape, b.dtype) for b in bufs] + [jax.ShapeDtypeStruct((8, LANES), F32)],
            input_output_aliases={i: 2 + i for i in range(na)},
            compiler_params=pltpu.CompilerParams(has_side_effects=_EFFECT),
        )(*bufs)
        self.sems, self.bufs, self.riders, self.token = outs[:2], outs[2:2 + nb], outs[2 + nb:2 + na], outs[-1]

    def after(self):
        return self.token[0, 0]

    def wait(self, done, riders=(), bufs=None):
        n, nb, copies = self.n, self.n + self.nl, self.copies
        bufs = [*(self.bufs if bufs is None else bufs),
                *[pltpu.with_memory_space_constraint(a, pltpu.HBM) for a in riders]]
        na = len(bufs)
        done = list(done) if isinstance(done, (list, tuple)) else [done]

        def body(*refs):
            send_sems, recv_sems = refs[na], refs[na + 1]
            for cp in copies(refs[:n], refs[n:nb], send_sems, recv_sems):
                cp.wait_send()
                cp.wait_recv()

        outs = pl.pallas_call(
            body, name=self.name + "_wait",
            in_specs=[_HBM] * na + [_SEM, _SEM] + [_ANY] * len(done),
            out_specs=[_HBM] * na,
            out_shape=[pltpu.HBM(b.shape, b.dtype) for b in bufs],
            input_output_aliases={i: i for i in range(na)},
            compiler_params=pltpu.CompilerParams(has_side_effects=_EFFECT),
        )(*bufs, *self.sems, *done)
        return outs[:n], outs[n:nb], outs[nb:]


def _sibling_copies(srcs, lands, send_sems, recv_sems):
    x, y, c = _place()
    out = []
    for a, (src, land) in enumerate(zip(srcs, lands)):
        h = src.shape[1] // 2
        theirs = pl.ds(pl.multiple_of((1 - c) * h, 8), h)
        out.append(_remote(src.at[:, theirs], land, send_sems, recv_sems, a, (x, y, 1 - c)))
    return out


def _grads_to_sibling(name, grads):
    shapes = [jax.ShapeDtypeStruct((4, g.shape[1] // 2, g.shape[2]), g.dtype) for g in grads]
    return _SplitExchange(name, grads, shapes, len(grads), _sibling_copies)


def _chip_copies(srcs, lands, send_sems, recv_sems):
    x, y, c = _place()
    out = []
    for a, (src, land) in enumerate(zip(srcs, lands)):
        for t, (fx, fy) in enumerate(_CHIP_FLIPS):
            tx, ty = _flip(x, fx), _flip(y, fy)
            out.append(_remote(src.at[2 * tx + ty], land.at[t], send_sems, recv_sems, 3 * a + t, (tx, ty, c)))
    return out


def _grads_to_chips(name, parts):
    shapes = [jax.ShapeDtypeStruct((3, *p.shape[1:]), p.dtype) for p in parts]
    return _SplitExchange(name, parts, shapes, 3 * len(parts), _chip_copies)


def _add_halves(g, r, half):
    _, rows, cols = g.shape
    h = rows // 2
    tm = min(h, 128)
    nt = h // tm

    def body(half_ref, g_ref, r_ref, b_ref):
        b_ref[...] = (g_ref[...] + r_ref[...]).astype(b_ref.dtype)

    spec = pl.BlockSpec((1, tm, cols), lambda j, i, half_ref: (j, i, 0))
    return pl.pallas_call(
        body,
        grid_spec=pltpu.PrefetchScalarGridSpec(
            num_scalar_prefetch=1, grid=(4, nt),
            in_specs=[pl.BlockSpec((1, tm, cols), lambda j, i, half_ref: (j, half_ref[0] * nt + i, 0)), spec],
            out_specs=spec),
        out_shape=jax.ShapeDtypeStruct((4, h, cols), BF16),
        name="add_sibling_grads",
    )(half, g, r)


def _add_chips(g, r, recv, where):
    _, h, cols = r.shape
    tm = min(h, 128)
    nt = h // tm

    def body(where_ref, g_ref, r_ref, recv_ref, out_ref):
        own = g_ref[0] + r_ref[0]
        out_ref[...] = ((own + recv_ref[0].astype(F32)) + recv_ref[1].astype(F32)) + recv_ref[2].astype(F32)

    return pl.pallas_call(
        body,
        grid_spec=pltpu.PrefetchScalarGridSpec(
            num_scalar_prefetch=1, grid=(nt,),
            in_specs=[pl.BlockSpec((1, tm, cols), lambda i, w: (w[0], w[1] * nt + i, 0)),
                      pl.BlockSpec((1, tm, cols), lambda i, w: (w[0], i, 0)),
                      pl.BlockSpec((3, tm, cols), lambda i, w: (0, i, 0))],
            out_specs=pl.BlockSpec((tm, cols), lambda i, w: (w[1] * nt + i, 0))),
        out_shape=jax.ShapeDtypeStruct((2 * h, cols), F32),
        name="add_chip_grads",
    )(where, g, r, recv)


def _half_copies(srcs, _, send_sems, recv_sems):
    x, y, c = _place()
    out = []
    for a, src in enumerate(srcs):
        h = src.shape[0] // 2
        mine = src.at[pl.ds(pl.multiple_of(c * h, 8), h)]
        out.append(_remote(mine, mine, send_sems, recv_sems, a, (x, y, 1 - c)))
    return out


def _share_halves(name, shards):
    return _SplitExchange(name, shards, [], len(shards), _half_copies)


def _reduce_small(rows):
    cols = rows[0].shape[1]
    n = len(rows)
    assert sum(r.shape[0] for r in rows) <= 8

    def body(*refs):
        ins, out_ref = refs[:n], refs[n]
        vec_ref, gath_ref, send_sems, recv_sems = refs[n + 1:]
        x, y, c = _place()
        me = 4 * x + 2 * y + c
        vec_ref[...] = jnp.zeros(vec_ref.shape, F32)
        at = 0
        for r in ins:
            vec_ref[at:at + r.shape[0], :] = r[...]
            at += r.shape[0]
        copies = []
        for k in range(1, 8):
            peer = (_flip(x, (k >> 2) & 1), _flip(y, (k >> 1) & 1), _flip(c, k & 1))
            copies.append(_remote(vec_ref, gath_ref.at[me], send_sems, recv_sems, k - 1, peer))
        for cp in copies:
            cp.start()
        gath_ref[me] = vec_ref[...]
        for cp in copies:
            cp.wait()
        tot = gath_ref[0]
        for dev in range(1, 8):
            tot = tot + gath_ref[dev]
        out_ref[...] = tot
        out_ref[7:8, :] = jnp.zeros((1, cols), F32) + jnp.sum(tot[7:8, :])

    vm = pl.BlockSpec(memory_space=pltpu.VMEM)
    return pl.pallas_call(
        body, in_specs=[vm] * n, out_specs=vm,
        out_shape=jax.ShapeDtypeStruct((8, cols), F32),
        scratch_shapes=[pltpu.VMEM((8, cols), F32), pltpu.VMEM((8, 8, cols), F32),
                        pltpu.SemaphoreType.DMA((7,)), pltpu.SemaphoreType.DMA((7,))],
        name="reduce_small",
    )(*rows)


def _adamw(w, g, m, v, name):
    rows, cols = w.shape
    tm = 128 if rows % 128 == 0 else rows

    def body(w_ref, g_ref, m_ref, v_ref, d_ref, m2_ref, v2_ref, gout_ref):
        gr = g_ref[...]
        m2 = ADAM_B1 * m_ref[...] + (1.0 - ADAM_B1) * gr
        v2 = ADAM_B2 * v_ref[...] + (1.0 - ADAM_B2) * (gr * gr)
        m_hat = m2 / (1.0 - ADAM_B1 ** ADAM_STEP)
        v_hat = v2 / (1.0 - ADAM_B2 ** ADAM_STEP)
        d_ref[...] = -ADAM_LR * (m_hat / (jnp.sqrt(v_hat) + ADAM_EPS) + ADAM_WD * w_ref[...])
        m2_ref[...] = m2
        v2_ref[...] = v2
        gout_ref[...] = gr

    spec = pl.BlockSpec((tm, cols), lambda i: (i, 0))
    sds = jax.ShapeDtypeStruct((rows, cols), F32)
    return pl.pallas_call(body, grid=(rows // tm,), in_specs=[spec] * 4, out_specs=[spec] * 4,
                          out_shape=[sds] * 4, name=name)(w, g, m, v)


def kernel(x, norm_g, w_in, b_merge, conv_w, w_out_conv, w_out_attn, w_o, final_g, loss_target, m_norm_g, m_w_in, m_b_merge, m_conv_w, m_w_out_conv, m_w_out_attn, m_w_o, m_final_g, v_norm_g, v_w_in, v_b_merge, v_conv_w, v_w_out_conv, v_w_out_attn, v_w_o, v_final_g):
    mx, my, mc = _place()
    chip = (2 * mx + my).astype(jnp.int32)
    seq = x.shape[1]

    chip1 = chip.reshape(1)
    slots = [_place_shard(w[0], chip1, MXU_DTYPE) for w in (w_in, w_out_conv, w_out_attn, w_o)]
    taps_slot = _place_shard(jnp.pad(conv_w[0], ((0, 5), (0, 0))), chip1, F32)
    gather_near = _SplitExchange("gather_w_in_near", [slots[0], taps_slot], [], 4,
                                 _gather_copies_to((0, 1), whole=(1,)))
    st = _Step(x[0], loss_target[0], norm_g, chip1, after=gather_near.token)
    st.project_own(w_in[0])
    near, _, _ = gather_near.wait([st.ut, st.proj_own])
    gather_far = _SplitExchange("gather_w_in_far", near, [], 2, _gather_copies_to((2,), whole=(1,)))
    (w4,) = _forward_to_sibling(gather_far.bufs[:1], flips=(0, 1))
    st.project(w4, (2, 1))
    (w4, taps4), _, out_slots = gather_far.wait([st.proj_own], riders=slots[1:], bufs=[w4, gather_far.bufs[1]])
    gather_out = _SplitExchange("gather_w_out", out_slots, [], 9, _gather_copies_to((0, 1, 2)), riders=[w4])
    (w4,) = _forward_to_sibling(gather_out.riders, flips=(2,))
    st.project(w4, (3,))
    st.mixers(w4, jnp.concatenate([taps4[j, :3, :] for j in range(4)], axis=1))
    out_ws, _, _ = gather_out.wait(st.o)
    woc, woa, wo = [w.reshape(D_MODEL, D_MODEL) for w in _forward_to_sibling(out_ws)]
    st.merge_and_loss(woc, woa, wo, b_merge, final_g.reshape(1, D_MODEL))

    half = mc.astype(jnp.int32).reshape(1)
    where = jnp.stack([chip, mc.astype(jnp.int32)])
    out_grads = [g.reshape(4, -1, D_MODEL) for g in st.out_weight_grads()]
    to_sibling = _grads_to_sibling("out_grads_to_sibling", out_grads)
    st.conv_grads(after=to_sibling.after())
    out_grads, out_from_sibling, _ = to_sibling.wait(st.da4)
    to_chips = _grads_to_chips("out_grads_to_chips",
                               [_add_halves(g, r, half) for g, r in zip(out_grads, out_from_sibling)])
    d_w4 = st.in_weight_grad(after=to_chips.after())
    out_from_chips = to_chips.wait(st.dc3)[1]

    to_sibling = _grads_to_sibling("in_grad_to_sibling", [d_w4])
    gx_lo, dg_lo = st.input_grad(0, after=to_sibling.after())
    (d_w4,), (from_sibling,), _ = to_sibling.wait(gx_lo)
    to_chips = _grads_to_chips("in_grad_to_chips", [_add_halves(d_w4, from_sibling, half)])

    where_late = where + to_chips.after().astype(jnp.int32)
    out_reduced = [_add_chips(g, r, recv, where_late)
                   for g, r, recv in zip(out_grads, out_from_sibling, out_from_chips)]
    share_out = _share_halves("share_out_grads", out_reduced)
    gx_hi, dg_hi = st.input_grad(1, after=share_out.after())
    grad_x = _to_natural(gx_lo, gx_hi)
    g_woc, g_woa, g_wo = share_out.wait(gx_hi)[0]

    in_reduced = _add_chips(d_w4, from_sibling, to_chips.wait(gx_hi)[1][0], where)
    share_in = _share_halves("share_in_grad", [in_reduced])
    small = _reduce_small([dg_lo + dg_hi + share_in.after(), st.d_bias.reshape(2, D_MODEL), st.d_taps,
                           st.d_final_g, st.loss8.reshape(1, D_MODEL)])
    loss = (0.5 / D_MODEL) * small[7, 0]
    g_taps = lax.dynamic_slice(small[3:6], (0, chip * (D_MODEL // 4)), (3, D_MODEL // 4))
    upd = {
        "norm_g": _adamw(norm_g, small[0:1], m_norm_g, v_norm_g, "adamw_norm_g"),
        "b_merge": _adamw(b_merge, small[1:3].reshape(1, 2 * D_MODEL), m_b_merge, v_b_merge, "adamw_b_merge"),
        "conv_w": _adamw(conv_w[0], g_taps, m_conv_w[0], v_conv_w[0], "adamw_conv_w"),
        "w_out_conv": _adamw(w_out_conv[0], g_woc, m_w_out_conv[0], v_w_out_conv[0], "adamw_w_out_conv"),
        "w_out_attn": _adamw(w_out_attn[0], g_woa, m_w_out_attn[0], v_w_out_attn[0], "adamw_w_out_attn"),
        "w_o": _adamw(w_o[0], g_wo, m_w_o[0], v_w_o[0], "adamw_w_o"),
        "final_g": _adamw(final_g.reshape(1, D_MODEL), small[6:7], m_final_g.reshape(1, D_MODEL),
                          v_final_g.reshape(1, D_MODEL), "adamw_final_g"),
    }
    behind = [grad_x] + [u[0] for u in upd.values()]
    (g_w_in,) = share_in.wait(behind)[0]
    upd["w_in"] = _adamw(w_in[0], g_w_in, m_w_in[0], v_w_in[0], "adamw_w_in")

    names = ["norm_g", "w_in", "b_merge", "conv_w", "w_out_conv", "w_out_attn", "w_o", "final_g"]
    shapes = [norm_g.shape, w_in.shape, b_merge.shape, conv_w.shape, w_out_conv.shape, w_out_attn.shape,
              w_o.shape, final_g.shape]
    outs = [loss, grad_x.reshape(1, seq, D_MODEL)]
    for k in (3, 0, 1, 2):
        outs += [upd[n][k].reshape(s) for n, s in zip(names, shapes)]
    return tuple(outs)
```

```python
import functools

import numpy as np
import jax
import jax.numpy as jnp
from jax import lax
from jax.experimental import pallas as pl
from jax.experimental.pallas import tpu as pltpu

F32 = jnp.float32
BF16 = jnp.bfloat16
MXU_DTYPE = jnp.bfloat16
ACT_DTYPE = jnp.bfloat16

D_MODEL = 1024
N_HEADS = 16
HEAD_DIM = 64
QB = 128
N_RES = 16
LANES = 128
HP = N_HEADS * HEAD_DIM // LANES
IN_COLS = 10 * D_MODEL
SHARD_COLS = IN_COLS // 4
EPS = 1e-6
NEG = -1e30

ADAM_LR, ADAM_B1, ADAM_B2, ADAM_EPS, ADAM_WD, ADAM_STEP = 0.001, 0.9, 0.999, 1e-08, 0.01, 10

PATTERNS = {1: (16, 8), 4: (4, 32), 16: (1, 128)}

_NN = (((1,), (0,)), ((), ()))
_NT = (((1,), (1,)), ((), ()))


def _dot(a, b):
    return lax.dot_general(a.astype(MXU_DTYPE), b.astype(MXU_DTYPE), _NN, preferred_element_type=F32)


def _dot_nt(a, b):
    return lax.dot_general(a.astype(MXU_DTYPE), b.astype(MXU_DTYPE), _NT, preferred_element_type=F32)


def _split3(x):
    hi = x.astype(BF16)
    r1 = x - hi.astype(F32)
    mid = r1.astype(BF16)
    lo = (r1 - mid.astype(F32)).astype(BF16)
    return hi, mid, lo


def _select_cols(x, sel, terms):
    return sum(lax.dot_general(t, sel, _NN, preferred_element_type=F32) for t in _split3(x)[:terms])


def _sigmoid(z):
    return 1.0 / (1.0 + jnp.exp(-z))


def _head_expand_matrix():
    e = np.zeros((LANES, D_MODEL), np.float32)
    for h in range(N_HEADS):
        e[8 * h, HEAD_DIM * h:HEAD_DIM * (h + 1)] = 1.0
    return jnp.asarray(e, BF16)


def _head_sum_matrix():
    e = np.zeros((D_MODEL, LANES), np.float32)
    for h in range(N_HEADS):
        e[HEAD_DIM * h:HEAD_DIM * (h + 1), 8 * h:8 * (h + 1)] = 1.0
    return jnp.asarray(e, BF16)


def _attn_tables(d):
    g_n, rq = PATTERNS[d]
    q_n = g_n * rq
    gq, iq = np.arange(q_n) // rq, np.arange(q_n) % rq

    def tab(kn, base):
        k_n = g_n * kn
        gk, jk = np.arange(k_n) // kn, np.arange(k_n) % kn
        delta = g_n * (base + iq[:, None] - jk[None, :]) + gq[:, None] - gk[None, :]
        valid = (delta >= 0) & (delta <= QB)
        dist = np.where(valid, d * delta, 0).astype(np.float32)
        madd = np.where(valid, 0.0, NEG).astype(np.float32)
        return dist, madd

    d0, m0 = tab(rq if g_n == 1 else 2 * rq, 0)
    d1, m1 = tab(2 * rq, rq)
    return d0, m0, d1, m1


def _alibi_slopes():
    return jnp.exp2(-8.0 * jnp.arange(1, N_HEADS + 1, dtype=F32) / N_HEADS)


def _to_residue_major(x, tgt, after=None):
    s_n, c_n = x.shape
    lr = s_n // N_RES
    extra = [] if after is None else [after]

    def body(x_ref, t_ref, *rest):
        xo_ref, to_ref = rest[-2:]
        for r in range(N_RES):
            xo_ref[r] = x_ref[pl.ds(r, lr, stride=N_RES), :]
            to_ref[r] = t_ref[pl.ds(r, lr, stride=N_RES), :]

    nat = pl.BlockSpec((s_n, LANES), lambda j: (0, j))
    res = pl.BlockSpec((N_RES, lr, LANES), lambda j: (0, 0, j))
    xo, to = pl.pallas_call(
        body, grid=(c_n // LANES,),
        in_specs=[nat, nat] + [pl.BlockSpec((8, LANES), lambda j: (0, 0))] * len(extra),
        out_specs=[res, res],
        out_shape=[jax.ShapeDtypeStruct((N_RES, lr, c_n), F32)] * 2,
        name="perm_in",
    )(x, tgt, *extra)
    return xo.reshape(s_n, c_n), to.reshape(s_n, c_n)


def _to_natural(gx_lo, gx_hi):
    half_rows, c_n = gx_lo.shape
    lr = half_rows // (N_RES // 2)

    def body(lo_ref, hi_ref, o_ref):
        for r in range(N_RES):
            o_ref[pl.ds(r, lr, stride=N_RES), :] = lo_ref[r] if r < N_RES // 2 else hi_ref[r - N_RES // 2]

    half = pl.BlockSpec((N_RES // 2, lr, LANES), lambda j: (0, 0, j))
    return pl.pallas_call(
        body, grid=(c_n // LANES,),
        in_specs=[half, half],
        out_specs=pl.BlockSpec((2 * half_rows, LANES), lambda j: (0, j)),
        out_shape=jax.ShapeDtypeStruct((2 * half_rows, c_n), F32),
        name="perm_out",
    )(gx_lo.reshape(N_RES // 2, lr, c_n), gx_hi.reshape(N_RES // 2, lr, c_n))


def _rms_in(xp, norm_g):
    s_n, c_n = xp.shape
    tm = 512

    def body(x_ref, g_ref, u_ref, ut_ref):
        x = x_ref[...]
        r = lax.rsqrt(jnp.mean(x * x, axis=-1, keepdims=True) + EPS)
        u = x * r * g_ref[...]
        u_ref[...] = u.astype(u_ref.dtype)
        ut_ref[...] = u.T.astype(ut_ref.dtype)

    return pl.pallas_call(
        body, grid=(s_n // tm,),
        in_specs=[pl.BlockSpec((tm, c_n), lambda i: (i, 0)), pl.BlockSpec((1, c_n), lambda i: (0, 0))],
        out_specs=[pl.BlockSpec((tm, c_n), lambda i: (i, 0)), pl.BlockSpec((c_n, tm), lambda i: (0, i))],
        out_shape=[jax.ShapeDtypeStruct((s_n, c_n), ACT_DTYPE), jax.ShapeDtypeStruct((c_n, s_n), ACT_DTYPE)],
        name="rms_in",
    )(xp, norm_g)


def _in_proj(u, chip, w_own=None, w4=None, partial=None, others=()):
    s_n = u.shape[0]
    tn, cm = 512, 512
    per = SHARD_COLS // tn
    own = partial is None

    def body(chip_ref, a_ref, b_ref, *rest):
        o_ref = rest[-1]
        b = b_ref[...]
        for c in range(s_n // cm):
            o_ref[c * cm:(c + 1) * cm, :] = _dot(a_ref[c * cm:(c + 1) * cm, :], b).astype(o_ref.dtype)

    def shard(n, chip_ref):
        if own:
            return chip_ref[0]
        mask = others[-1]
        for i, m in enumerate(others[:-1]):
            mask = jnp.where(n // per == i, m, mask)
        return jnp.bitwise_xor(chip_ref[0], mask)

    w_spec = (pl.BlockSpec((D_MODEL, tn), lambda n, c: (0, n)) if own else
              pl.BlockSpec((None, D_MODEL, tn), lambda n, c: (shard(n, c), 0, n % per)))
    return pl.pallas_call(
        body,
        grid_spec=pltpu.PrefetchScalarGridSpec(
            num_scalar_prefetch=1, grid=(per if own else len(others) * per,),
            in_specs=[pl.BlockSpec((s_n, D_MODEL), lambda n, c: (0, 0)), w_spec] + ([] if own else [_ANY]),
            out_specs=pl.BlockSpec((s_n, tn), lambda n, c: (0, shard(n, c) * per + n % per))),
        out_shape=jax.ShapeDtypeStruct((s_n, IN_COLS), ACT_DTYPE),
        input_output_aliases={} if own else {3: 0},
        name="in_proj_own" if own else "in_proj_" + "_".join(str(m) for m in others),
    )(*([chip, u, w_own] if own else [chip, u, w4, partial]))


def _conv_terms(xc_ref, cg_ref, r, row, lr, cache):
    def a_of(q):
        if q not in cache:
            cache[q] = cg_ref[q].astype(F32) * xc_ref[q].astype(F32)
        return cache[q]

    def shift_down(v):
        return jnp.where(row >= 1, pltpu.roll(v, 1, 0), 0.0)

    a = a_of(r)
    am1 = a_of(r - 1) if r >= 1 else shift_down(a_of(N_RES - 1))
    am2 = a_of(r - 2) if r >= 2 else shift_down(a_of(N_RES - 2 + r))
    return a, am1, am2


def _conv_fwd(proj, conv_w):
    s_n = proj.shape[0]
    lr = s_n // N_RES
    pv = proj.reshape(N_RES, lr, IN_COLS)

    def body(xc_ref, bg_ref, cg_ref, zc_ref, w_ref, hc_ref, hct_ref):
        w = w_ref[...]
        row = lax.broadcasted_iota(jnp.int32, (lr, LANES), 0)
        products = {}
        for r in range(N_RES):
            a, am1, am2 = _conv_terms(xc_ref, cg_ref, r, row, lr, products)
            c = w[0:1] * am2 + w[1:2] * am1 + w[2:3] * a
            z = zc_ref[r].astype(F32)
            hc = z * _sigmoid(z) * bg_ref[r].astype(F32) * c
            hc_ref[r] = hc.astype(hc_ref.dtype)
            hct_ref[:, r * lr:(r + 1) * lr] = hc.T.astype(hct_ref.dtype)

    def col(part):
        return pl.BlockSpec((N_RES, lr, LANES), lambda j: (0, 0, part * 8 + j))

    hc, hct = pl.pallas_call(
        body, grid=(D_MODEL // LANES,),
        in_specs=[col(0), col(1), col(2), col(3), pl.BlockSpec((3, LANES), lambda j: (0, j))],
        out_specs=[pl.BlockSpec((N_RES, lr, LANES), lambda j: (0, 0, j)),
                   pl.BlockSpec((LANES, s_n), lambda j: (j, 0))],
        out_shape=[jax.ShapeDtypeStruct((N_RES, lr, D_MODEL), ACT_DTYPE),
                   jax.ShapeDtypeStruct((D_MODEL, s_n), ACT_DTYPE)],
        name="conv_fwd",
    )(pv, pv, pv, pv, conv_w)
    return hc.reshape(s_n, D_MODEL), hct


RES_PER_STEP = 8
ATTN_BATCH = 16

_BNT = (((2,), (2,)), ((0,), (0,)))
_BNN = (((2,), (1,)), ((0,), (0,)))


def _bdot(a, b, dims):
    return lax.dot_general(a.astype(MXU_DTYPE), b.astype(MXU_DTYPE), dims, preferred_element_type=F32)


def _pattern_view_shape(s_n, c_n, g_n, lead=()):
    lr = s_n // N_RES
    return (*lead, 4, 4, lr, c_n) if g_n == 4 else (*lead, N_RES, lr, c_n)


def _pattern_view(a, g_n, lead=()):
    return a.reshape(_pattern_view_shape(a.shape[-2], a.shape[-1], g_n, lead))


def _pattern_grid(g_n):
    return (N_RES // RES_PER_STEP if g_n == 1 else N_RES // g_n, HP)


def _pattern_spec(g_n, lr, col_of_hp, lead=()):
    z = (0,) * len(lead)
    if g_n == 16:
        return pl.BlockSpec((*lead, 16, lr, LANES), lambda r, hp: (*z, 0, 0, col_of_hp(hp)))
    if g_n == 4:
        return pl.BlockSpec((*lead, 4, None, lr, LANES), lambda r, hp: (*z, 0, r, 0, col_of_hp(hp)))
    return pl.BlockSpec((*lead, RES_PER_STEP, lr, LANES), lambda r, hp: (*z, r, 0, col_of_hp(hp)))


def _aligned(start, m):
    return start if isinstance(start, int) else pl.multiple_of(start, m)


class _Units:
    def __init__(self, g_n, rq):
        self.g_n, self.rq = g_n, rq
        self.per_res, self.paired = g_n == 1, rq == 8

    def plan(self, lr, size):
        if self.per_res:
            return [0], lr // self.rq - 1, lambda j: [pl.multiple_of(j * self.rq, self.rq)]
        step = 16 if self.paired else self.rq
        per = min(size // 2 if self.paired else size, lr // step)
        assert (lr // step) % per == 0
        return ([i * step for i in range(per)], lr // step // per - 1,
                lambda j: [pl.multiple_of((j * per + i) * step, step) for i in range(per)])

    def count(self, qs):
        return RES_PER_STEP if self.per_res else len(qs) * (2 if self.paired else 1)

    def _split(self, tiles, lo, rows):
        return tiles[:, lo:lo + rows].reshape(self.g_n * rows, LANES)

    def load_q(self, ref, qs):
        rq = self.rq
        if self.per_res:
            return ref[:, pl.ds(qs[0], rq), :]
        if self.paired:
            tiles = [ref[:, pl.ds(q, 16), :].astype(F32) for q in qs]
            return jnp.stack([self._split(t, lo, 8) for t in tiles for lo in (0, 8)])
        return jnp.stack([ref[:, pl.ds(q, rq), :].reshape(self.g_n * rq, LANES) for q in qs])

    def _key_rows(self, q, at_start):
        return (0, 2 * self.rq) if at_start else (_aligned(q - self.rq, self.rq), 2 * self.rq)

    def load_k(self, ref, qs, first):
        rq = self.rq
        if self.per_res:
            return ref[:, pl.ds(0, rq), :] if first else ref[:, pl.ds(_aligned(qs[0] - rq, rq), 2 * rq), :]
        if self.paired:
            out = []
            for i, q in enumerate(qs):
                if first and i == 0:
                    t = ref[:, 0:16, :].astype(F32)
                    out += [self._split(t, 0, 16)] * 2
                else:
                    t = ref[:, pl.ds(_aligned(q - 16, 16), 32), :].astype(F32)
                    out += [self._split(t, 8, 16), self._split(t, 16, 16)]
            return jnp.stack(out)
        rows = [self._key_rows(q, first and i == 0) for i, q in enumerate(qs)]
        return jnp.stack([ref[:, pl.ds(k0, n), :].reshape(self.g_n * n, LANES) for k0, n in rows])

    def store_q(self, ref, qs, val, add=False, lead=()):
        if self.per_res:
            pieces = [(qs[0], self.rq, val)]
        elif self.paired:
            pieces = [(q, 16, jnp.concatenate([val[2 * i].reshape(self.g_n, 8, LANES),
                                               val[2 * i + 1].reshape(self.g_n, 8, LANES)], axis=1))
                      for i, q in enumerate(qs)]
        else:
            pieces = [(q, self.rq, val[i].reshape(self.g_n, self.rq, LANES)) for i, q in enumerate(qs)]
        for start, rows, v in pieces:
            idx = (*lead, slice(None), pl.ds(start, rows), slice(None))
            ref[idx] = (ref[idx] + v if add else v).astype(ref.dtype)

    def add_k(self, ref, qs, val, first):
        rq = self.rq
        if self.per_res:
            k0, n = (0, rq) if first else (_aligned(qs[0] - rq, rq), 2 * rq)
            ref[:, pl.ds(k0, n), :] += val
            return
        if self.paired:
            starts = [s for i, q in enumerate(qs)
                      for s in ((0, 0) if first and i == 0 else (_aligned(q - 8, 8), q))]
            rows = [(s, 16) for s in starts]
        else:
            rows = [self._key_rows(q, first and i == 0) for i, q in enumerate(qs)]
        for b, (k0, n) in enumerate(rows):
            ref[:, pl.ds(k0, n), :] += val[b].reshape(self.g_n, n, LANES)


def _batch_bias(un, qs, at_start, first_ref, general_ref):
    if not at_start:
        return general_ref[...][None]
    if un.per_res:
        return first_ref[...][None]
    return jnp.concatenate([first_ref[...][None]] + [general_ref[...][None]] * (un.count(qs) - 1), axis=0)


def _stack_heads(x, low):
    zero = jnp.zeros_like(x)
    return jnp.concatenate([jnp.where(low, x, zero), jnp.where(low, zero, x)], axis=1)


def _attn_fwd(proj, slopes, d):
    g_n, rq = PATTERNS[d]
    un = _Units(g_n, rq)
    s_n = proj.shape[0]
    lr = s_n // N_RES
    q_n = g_n * rq
    d0, m0, d1, m1 = _attn_tables(d)
    first, n_more, later = un.plan(lr, ATTN_BATCH)

    def body(sl_ref, q_ref, k_ref, v_ref, d0_ref, m0_ref, d1_ref, m1_ref, o_ref, lse_ref, b0_ref, b1_ref):
        hp = pl.program_id(1)

        @pl.when(hp == 0)
        def _():
            lse_ref[...] = jnp.zeros(lse_ref.shape, F32)

        for h in (0, 1):
            slope = sl_ref[2 * hp + h]
            b0_ref[h * q_n:(h + 1) * q_n, :] = m0_ref[...] - slope * d0_ref[...]
            b1_ref[h * q_n:(h + 1) * q_n, :] = m1_ref[...] - slope * d1_ref[...]

        lane = lax.broadcasted_iota(jnp.int32, (1, q_n, LANES), 2)
        low = lane < HEAD_DIM
        grp = lane // 8

        def batch(qs, at_start):
            qq = _stack_heads(un.load_q(q_ref, qs) * 0.125, low)
            s = _bdot(qq, un.load_k(k_ref, qs, at_start), _BNT) + _batch_bias(un, qs, at_start, b0_ref, b1_ref)
            m = jnp.max(s, axis=2, keepdims=True)
            p = jnp.exp(s - m)
            l = jnp.sum(p, axis=2, keepdims=True)
            o = _bdot(p, un.load_k(v_ref, qs, at_start), _BNN) * (1.0 / l)
            lse = m + jnp.log(l)
            un.store_q(o_ref, qs, jnp.where(low, o[:, :q_n], o[:, q_n:]))
            upd = jnp.where(grp == 2 * hp, lse[:, :q_n], 0.0) + jnp.where(grp == 2 * hp + 1, lse[:, q_n:], 0.0)
            un.store_q(lse_ref, qs, upd, add=True)

        batch(first, True)

        def more(j, carry):
            batch(later(j), False)
            return carry

        lax.fori_loop(1, 1 + n_more, more, 0)

    pv = _pattern_view(proj, g_n)
    full = lambda a: pl.BlockSpec(a.shape, lambda r, hp: (0, 0))
    o, lse = pl.pallas_call(
        body, grid=_pattern_grid(g_n),
        in_specs=[pl.BlockSpec(memory_space=pltpu.SMEM),
                  _pattern_spec(g_n, lr, lambda hp: 32 + hp),
                  _pattern_spec(g_n, lr, lambda hp: 40 + hp),
                  _pattern_spec(g_n, lr, lambda hp: 48 + hp),
                  full(d0), full(m0), full(d1), full(m1)],
        out_specs=[_pattern_spec(g_n, lr, lambda hp: hp), _pattern_spec(g_n, lr, lambda hp: 0)],
        out_shape=[jax.ShapeDtypeStruct(_pattern_view_shape(s_n, D_MODEL, g_n), ACT_DTYPE),
                   jax.ShapeDtypeStruct(_pattern_view_shape(s_n, LANES, g_n), F32)],
        scratch_shapes=[pltpu.VMEM((2 * q_n, d0.shape[1]), F32), pltpu.VMEM((2 * q_n, 2 * q_n), F32)],
        name=f"attn_fwd_d{d}",
    )(slopes, pv, pv, pv, d0, m0, d1, m1)
    return o.reshape(s_n, D_MODEL), lse.reshape(s_n, LANES)


def _attn_combine(outs, lses, proj):
    s_n = proj.shape[0]
    tm = 512

    def body(o1_ref, o2_ref, o3_ref, l1_ref, l2_ref, l3_ref, za_ref, e_ref, o_ref, lse_ref, ha_ref, hat_ref):
        ls = [l1_ref[...], l2_ref[...], l3_ref[...]]
        mx = jnp.maximum(jnp.maximum(ls[0], ls[1]), ls[2])
        den = sum(jnp.exp(l - mx) for l in ls)
        lse = mx + jnp.log(den)
        lse_ref[...] = lse
        o = jnp.zeros((tm, D_MODEL), F32)
        for l, oref in zip(ls, (o1_ref, o2_ref, o3_ref)):
            o = o + _select_cols(jnp.exp(l - lse), e_ref[...], terms=2) * oref[...].astype(F32)
        o_ref[...] = o.astype(o_ref.dtype)
        z = za_ref[...].astype(F32)
        ha = z * _sigmoid(z) * o
        ha_ref[...] = ha.astype(ha_ref.dtype)
        hat_ref[...] = ha.T.astype(hat_ref.dtype)

    row = lambda w: pl.BlockSpec((tm, w), lambda i: (i, 0))
    return pl.pallas_call(
        body, grid=(s_n // tm,),
        in_specs=[row(D_MODEL)] * 3 + [row(LANES)] * 3
        + [pl.BlockSpec((tm, D_MODEL), lambda i: (i, 7)), pl.BlockSpec((LANES, D_MODEL), lambda i: (0, 0))],
        out_specs=[row(D_MODEL), row(LANES), row(D_MODEL), pl.BlockSpec((D_MODEL, tm), lambda i: (0, i))],
        out_shape=[jax.ShapeDtypeStruct((s_n, D_MODEL), ACT_DTYPE), jax.ShapeDtypeStruct((s_n, LANES), F32),
                   jax.ShapeDtypeStruct((s_n, D_MODEL), ACT_DTYPE), jax.ShapeDtypeStruct((D_MODEL, s_n), ACT_DTYPE)],
        name="attn_combine",
    )(*outs, *lses, proj, _head_expand_matrix())


CHAIN_ROWS = 256


def _row_chains(tm):
    return [slice(r, r + CHAIN_ROWS) for r in range(0, tm, CHAIN_ROWS)]


def _gates(gc_ref, ga_ref, b_ref, rows):
    b = b_ref[...]
    gc = _sigmoid(gc_ref[rows, :].astype(F32) + b[:, :D_MODEL])
    ga = _sigmoid(ga_ref[rows, :].astype(F32) + b[:, D_MODEL:])
    return gc, ga


def _merge_loss(hc, ha, woc, woa, wo, proj, b_merge, xp, final_g, tgt):
    s_n = xp.shape[0]
    tm = 512

    def body(hc_ref, ha_ref, woc_ref, woa_ref, wo_ref, gc_ref, ga_ref, b_ref, x_ref, gf_ref, t_ref,
             yc_ref, ya_ref, dhb_ref, dgf_ref, loss_ref, dwo_ref, mgt_ref):
        i = pl.program_id(0)

        @pl.when(i == 0)
        def _():
            dgf_ref[...] = jnp.zeros(dgf_ref.shape, F32)
            loss_ref[...] = jnp.zeros(loss_ref.shape, F32)
            dwo_ref[...] = jnp.zeros(dwo_ref.shape, F32)

        gf = gf_ref[...]
        for rows in _row_chains(tm):
            yc = _dot(hc_ref[rows, :], woc_ref[...])
            ya = _dot(ha_ref[rows, :], woa_ref[...])
            gc, ga = _gates(gc_ref, ga_ref, b_ref, rows)
            mg = gc * yc + ga * ya
            yc_ref[rows, :] = yc.astype(yc_ref.dtype)
            ya_ref[rows, :] = ya.astype(ya_ref.dtype)
            mgt_ref[:, rows] = mg.T.astype(mgt_ref.dtype)
            h2 = x_ref[rows, :] + _dot(mg, wo_ref[...])
            r2 = lax.rsqrt(jnp.mean(h2 * h2, axis=-1, keepdims=True) + EPS)
            nrm = h2 * r2
            err = nrm * gf - t_ref[rows, :]
            e2 = (err * err).reshape(-1, 8, D_MODEL).sum(axis=0)
            loss_ref[...] += sum(e2[:, c * LANES:(c + 1) * LANES] for c in range(D_MODEL // LANES))
            dy = err * (1.0 / D_MODEL)
            dgf_ref[...] += jnp.sum(dy * nrm, axis=0, keepdims=True)
            dn = dy * gf
            dh2 = r2 * (dn - nrm * jnp.mean(dn * nrm, axis=-1, keepdims=True))
            dhb_ref[rows, :] = dh2.astype(dhb_ref.dtype)
        dwo_ref[...] += _dot(mgt_ref[...], dhb_ref[...])

    row = pl.BlockSpec((tm, D_MODEL), lambda i: (i, 0))
    wsp = pl.BlockSpec((D_MODEL, D_MODEL), lambda i: (0, 0), pipeline_mode=pl.Buffered(1))
    vec = lambda w: pl.BlockSpec((1, w), lambda i: (0, 0))
    act = jax.ShapeDtypeStruct((s_n, D_MODEL), ACT_DTYPE)
    return pl.pallas_call(
        body, grid=(s_n // tm,),
        in_specs=[row, row, wsp, wsp, wsp,
                  pl.BlockSpec((tm, D_MODEL), lambda i: (i, 8)), pl.BlockSpec((tm, D_MODEL), lambda i: (i, 9)),
                  vec(2 * D_MODEL), row, vec(D_MODEL), row],
        out_specs=[row, row, row, vec(D_MODEL), pl.BlockSpec((8, LANES), lambda i: (0, 0)),
                   pl.BlockSpec((D_MODEL, D_MODEL), lambda i: (0, 0))],
        out_shape=[act, act, act, jax.ShapeDtypeStruct((1, D_MODEL), F32), jax.ShapeDtypeStruct((8, LANES), F32),
                   jax.ShapeDtypeStruct((D_MODEL, D_MODEL), F32)],
        scratch_shapes=[pltpu.VMEM((D_MODEL, tm), MXU_DTYPE)],
        name="merge_loss",
    )(hc, ha, woc, woa, wo, proj, proj, b_merge, xp, final_g, tgt)


def _merge_bwd(dh2b, wo, woc, woa, yc, ya, proj, b_merge, o, hct, hat):
    s_n = dh2b.shape[0]
    tm = 512

    def body(dh_ref, wo_ref, woc_ref, woa_ref, yc_ref, ya_ref, gc_ref, ga_ref, b_ref, o_ref, za_ref, e_ref,
             hct_ref, hat_ref, dhc_ref, do_ref, dsum_ref, db3_ref, dbias_ref, dwoc_ref, dwoa_ref,
             dyc_ref, dya_ref):
        i = pl.program_id(0)

        @pl.when(i == 0)
        def _():
            dbias_ref[...] = jnp.zeros(dbias_ref.shape, F32)
            dwoc_ref[...] = jnp.zeros(dwoc_ref.shape, F32)
            dwoa_ref[...] = jnp.zeros(dwoa_ref.shape, F32)

        for rows in _row_chains(tm):
            dmg = _dot_nt(dh_ref[rows, :], wo_ref[...])
            gc, ga = _gates(gc_ref, ga_ref, b_ref, rows)
            dgc = dmg * yc_ref[rows, :].astype(F32) * gc * (1.0 - gc)
            dga = dmg * ya_ref[rows, :].astype(F32) * ga * (1.0 - ga)
            dbias_ref[:, :D_MODEL] += jnp.sum(dgc, axis=0, keepdims=True)
            dbias_ref[:, D_MODEL:] += jnp.sum(dga, axis=0, keepdims=True)
            dyc = dmg * gc
            dya = dmg * ga
            dyc_ref[rows, :] = dyc.astype(dyc_ref.dtype)
            dya_ref[rows, :] = dya.astype(dya_ref.dtype)
            dhc_ref[rows, :] = _dot_nt(dyc, woc_ref[...]).astype(dhc_ref.dtype)
            dha = _dot_nt(dya, woa_ref[...])
            z = za_ref[rows, :].astype(F32)
            sg = _sigmoid(z)
            ov = o_ref[rows, :].astype(F32)
            dout = dha * z * sg
            do_ref[rows, :] = dout.astype(do_ref.dtype)
            dsum_ref[rows, :] = _select_cols(dout * ov, e_ref[...], terms=2)
            db3_ref[0, rows, :] = (dha * ov * sg * (1.0 + z * (1.0 - sg))).astype(db3_ref.dtype)
            db3_ref[1, rows, :] = dgc.astype(db3_ref.dtype)
            db3_ref[2, rows, :] = dga.astype(db3_ref.dtype)
        dwoc_ref[...] += _dot(hct_ref[...], dyc_ref[...])
        dwoa_ref[...] += _dot(hat_ref[...], dya_ref[...])

    row = pl.BlockSpec((tm, D_MODEL), lambda i: (i, 0))
    col = pl.BlockSpec((D_MODEL, tm), lambda i: (0, i))
    wsp = pl.BlockSpec((D_MODEL, D_MODEL), lambda i: (0, 0), pipeline_mode=pl.Buffered(1))
    acc = pl.BlockSpec((D_MODEL, D_MODEL), lambda i: (0, 0))
    act = jax.ShapeDtypeStruct((s_n, D_MODEL), ACT_DTYPE)
    grad = jax.ShapeDtypeStruct((D_MODEL, D_MODEL), F32)
    return pl.pallas_call(
        body, grid=(s_n // tm,),
        in_specs=[row, wsp, wsp, wsp, row, row,
                  pl.BlockSpec((tm, D_MODEL), lambda i: (i, 8)), pl.BlockSpec((tm, D_MODEL), lambda i: (i, 9)),
                  pl.BlockSpec((1, 2 * D_MODEL), lambda i: (0, 0)), row,
                  pl.BlockSpec((tm, D_MODEL), lambda i: (i, 7)), pl.BlockSpec((D_MODEL, LANES), lambda i: (0, 0)),
                  col, col],
        out_specs=[row, row, pl.BlockSpec((tm, LANES), lambda i: (i, 0)),
                   pl.BlockSpec((3, tm, D_MODEL), lambda i: (0, i, 0)),
                   pl.BlockSpec((1, 2 * D_MODEL), lambda i: (0, 0)), acc, acc],
        out_shape=[act, act, jax.ShapeDtypeStruct((s_n, LANES), F32),
                   jax.ShapeDtypeStruct((3, s_n, D_MODEL), ACT_DTYPE),
                   jax.ShapeDtypeStruct((1, 2 * D_MODEL), F32), grad, grad],
        scratch_shapes=[pltpu.VMEM((tm, D_MODEL), MXU_DTYPE), pltpu.VMEM((tm, D_MODEL), MXU_DTYPE)],
        name="merge_bwd",
    )(dh2b, wo, woc, woa, yc, ya, proj, proj, b_merge, o, proj, _head_sum_matrix(), hct, hat)


def _conv_bwd(proj, conv_w, dhc):
    s_n = proj.shape[0]
    lr = s_n // N_RES
    pv = proj.reshape(N_RES, lr, IN_COLS)

    def body(xc_ref, bg_ref, cg_ref, zc_ref, w_ref, dhc_ref, da4_ref, dw_ref, dc_ref):
        w = w_ref[...]
        row = lax.broadcasted_iota(jnp.int32, (lr, LANES), 0)
        dw = [jnp.zeros((1, LANES), F32) for _ in range(3)]
        products = {}
        for r in range(N_RES):
            a, am1, am2 = _conv_terms(xc_ref, cg_ref, r, row, lr, products)
            c = w[0:1] * am2 + w[1:2] * am1 + w[2:3] * a
            z = zc_ref[r].astype(F32)
            sg = _sigmoid(z)
            sz = z * sg
            bg = bg_ref[r].astype(F32)
            dh = dhc_ref[r].astype(F32)
            da4_ref[1, r] = (dh * sz * c).astype(da4_ref.dtype)
            da4_ref[3, r] = (dh * bg * c * sg * (1.0 + z * (1.0 - sg))).astype(da4_ref.dtype)
            dc = dh * sz * bg
            dc_ref[r] = dc
            dw[0] = dw[0] + jnp.sum(dc * am2, axis=0, keepdims=True)
            dw[1] = dw[1] + jnp.sum(dc * am1, axis=0, keepdims=True)
            dw[2] = dw[2] + jnp.sum(dc * a, axis=0, keepdims=True)
        dw_ref[0:1, :] = dw[0]
        dw_ref[1:2, :] = dw[1]
        dw_ref[2:3, :] = dw[2]

        def shift_up(v):
            return jnp.where(row < lr - 1, pltpu.roll(v, lr - 1, 0), 0.0)

        for r in range(N_RES):
            dp1 = dc_ref[r + 1] if r + 1 < N_RES else shift_up(dc_ref[0])
            dp2 = dc_ref[r + 2] if r + 2 < N_RES else shift_up(dc_ref[r + 2 - N_RES])
            da = w[2:3] * dc_ref[r] + w[1:2] * dp1 + w[0:1] * dp2
            da4_ref[0, r] = (da * cg_ref[r].astype(F32)).astype(da4_ref.dtype)
            da4_ref[2, r] = (da * xc_ref[r].astype(F32)).astype(da4_ref.dtype)

    def col(part):
        return pl.BlockSpec((N_RES, lr, LANES), lambda j: (0, 0, part * 8 + j))

    da4, dw = pl.pallas_call(
        body, grid=(D_MODEL // LANES,),
        in_specs=[col(0), col(1), col(2), col(3), pl.BlockSpec((3, LANES), lambda j: (0, j)),
                  pl.BlockSpec((N_RES, lr, LANES), lambda j: (0, 0, j))],
        out_specs=[pl.BlockSpec((4, N_RES, lr, LANES), lambda j: (0, 0, 0, j)),
                   pl.BlockSpec((3, LANES), lambda j: (0, j))],
        out_shape=[jax.ShapeDtypeStruct((4, N_RES, lr, D_MODEL), ACT_DTYPE),
                   jax.ShapeDtypeStruct((3, D_MODEL), F32)],
        scratch_shapes=[pltpu.VMEM((N_RES, lr, LANES), F32)],
        name="conv_bwd",
    )(pv, pv, pv, pv, conv_w, dhc.reshape(N_RES, lr, D_MODEL))
    return da4.reshape(4, s_n, D_MODEL), dw


def _attn_bwd(proj, dout, lse, dsum, slopes, d, prev=None):
    g_n, rq = PATTERNS[d]
    un = _Units(g_n, rq)
    s_n = proj.shape[0]
    lr = s_n // N_RES
    q_n = g_n * rq
    d0, m0, d1, m1 = (np.ascontiguousarray(t.T) for t in _attn_tables(d))
    first, n_more, later = un.plan(lr, ATTN_BATCH)
    bsz = un.count(first)
    gd = RES_PER_STEP if un.per_res else g_n

    def body(sl_ref, q_ref, k_ref, v_ref, do_ref, lse_ref, ds_ref, d0_ref, m0_ref, d1_ref, m1_ref, *rest):
        prev_ref = rest[0] if prev is not None else None
        out_ref, b0_ref, b1_ref, lt_ref, dt_ref, dk_ref, dv_ref = rest[-7:]
        hp = pl.program_id(1)
        for h in (0, 1):
            slope = sl_ref[2 * hp + h]
            b0_ref[:, h * q_n:(h + 1) * q_n] = m0_ref[...] - slope * d0_ref[...]
            b1_ref[:, h * q_n:(h + 1) * q_n] = m1_ref[...] - slope * d1_ref[...]
        if prev is None:
            dk_ref[...] = jnp.zeros(dk_ref.shape, F32)
            dv_ref[...] = jnp.zeros(dv_ref.shape, F32)
        else:
            out_ref[0] = prev_ref[0]
            dk_ref[...] = prev_ref[1].astype(F32)
            dv_ref[...] = prev_ref[2].astype(F32)
        low = lax.broadcasted_iota(jnp.int32, (1, q_n, LANES), 2) < HEAD_DIM
        row16 = pl.multiple_of(16 * hp, 16)

        def query_rows(stat_ref, t_ref, qs):
            tiles = un.load_q(stat_ref, qs)
            for b in range(bsz):
                t_ref[b] = tiles[b].T
            t16 = t_ref[:, pl.ds(row16, 16), :]
            return jnp.concatenate([t16[:, 0:1, :], t16[:, 8:9, :]], axis=2)

        def batch(qs, at_start):
            qq = _stack_heads(un.load_q(q_ref, qs) * 0.125, low)
            dd = _stack_heads(un.load_q(do_ref, qs), low)
            ks = un.load_k(k_ref, qs, at_start)
            vs = un.load_k(v_ref, qs, at_start)
            lrow = query_rows(lse_ref, lt_ref, qs)
            drow = query_rows(ds_ref, dt_ref, qs)
            pt = jnp.exp(_bdot(ks, qq, _BNT) + _batch_bias(un, qs, at_start, b0_ref, b1_ref) - lrow)
            dst = pt * (_bdot(vs, dd, _BNT) - drow)
            un.add_k(dv_ref, qs, _bdot(pt, dd, _BNN), at_start)
            un.add_k(dk_ref, qs, _bdot(dst, qq, _BNN), at_start)
            dq = _bdot(jnp.swapaxes(dst, 1, 2), ks, _BNN)
            un.store_q(out_ref, qs, jnp.where(low, dq[:, :q_n], dq[:, q_n:]) * 0.125, add=prev is not None,
                       lead=(0,))

        batch(first, True)

        def more(j, carry):
            batch(later(j), False)
            return carry

        lax.fori_loop(1, 1 + n_more, more, 0)
        out_ref[1] = dk_ref[...].astype(out_ref.dtype)
        out_ref[2] = dv_ref[...].astype(out_ref.dtype)

    pv = _pattern_view(proj, g_n)
    full = lambda a: pl.BlockSpec(a.shape, lambda r, hp: (0, 0))
    whole = _pattern_spec(g_n, lr, lambda hp: hp, lead=(3,))
    out = pl.pallas_call(
        body, grid=_pattern_grid(g_n),
        in_specs=[pl.BlockSpec(memory_space=pltpu.SMEM),
                  _pattern_spec(g_n, lr, lambda hp: 32 + hp),
                  _pattern_spec(g_n, lr, lambda hp: 40 + hp),
                  _pattern_spec(g_n, lr, lambda hp: 48 + hp),
                  _pattern_spec(g_n, lr, lambda hp: hp),
                  _pattern_spec(g_n, lr, lambda hp: 0),
                  _pattern_spec(g_n, lr, lambda hp: 0),
                  full(d0), full(m0), full(d1), full(m1)] + ([] if prev is None else [whole]),
        out_specs=whole,
        out_shape=jax.ShapeDtypeStruct(_pattern_view_shape(s_n, D_MODEL, g_n, lead=(3,)), ACT_DTYPE),
        scratch_shapes=[pltpu.VMEM((d0.shape[0], 2 * q_n), F32), pltpu.VMEM((2 * q_n, 2 * q_n), F32),
                        pltpu.VMEM((bsz, LANES, q_n), F32), pltpu.VMEM((bsz, LANES, q_n), F32),
                        pltpu.VMEM((gd, lr, LANES), F32), pltpu.VMEM((gd, lr, LANES), F32)],
        name=f"attn_bwd_d{d}",
    )(slopes, pv, pv, pv, _pattern_view(dout, g_n), _pattern_view(lse, g_n), _pattern_view(dsum, g_n),
      d0, m0, d1, m1, *([] if prev is None else [_pattern_view(prev, g_n, lead=(3,))]))
    return out.reshape(3, s_n, D_MODEL)


def _part_index(step, per, lo, n):
    return jnp.clip(step // per - lo, 0, n - 1)


def _dw_in(ut, da4, dc3, db3):
    s_n = ut.shape[1]
    tn = 512
    per = D_MODEL // tn
    shard_blocks = SHARD_COLS // tn

    def body(a_ref, p0_ref, p1_ref, p2_ref, o_ref):
        part = pl.program_id(0) // per

        @pl.when(part < 4)
        def _():
            o_ref[...] = _dot(a_ref[...], p0_ref[...])

        @pl.when((part >= 4) & (part < 7))
        def _():
            o_ref[...] = _dot(a_ref[...], p1_ref[...])

        @pl.when(part >= 7)
        def _():
            o_ref[...] = _dot(a_ref[...], p2_ref[...])

    def pspec(lo, n):
        def index(j):
            part = j // per
            col = jnp.where(part < lo, 0, jnp.where(part >= lo + n, per - 1, j % per))
            return _part_index(j, per, lo, n), 0, col
        return pl.BlockSpec((None, s_n, tn), index)

    return pl.pallas_call(
        body, grid=(IN_COLS // tn,),
        in_specs=[pl.BlockSpec((D_MODEL, s_n), lambda j: (0, 0), pipeline_mode=pl.Buffered(1)),
                  pspec(0, 4), pspec(4, 3), pspec(7, 3)],
        out_specs=pl.BlockSpec((None, D_MODEL, tn), lambda j: (j // shard_blocks, 0, j % shard_blocks)),
        out_shape=jax.ShapeDtypeStruct((4, D_MODEL, SHARD_COLS), F32),
        name="dw_in",
    )(ut, da4, dc3, db3)


def _input_grad(da4, dc3, db3, w4, xp, norm_g, dh2, row0, rows):
    tm, tk = 256, 512
    per = D_MODEL // tk
    shard_blocks = SHARD_COLS // tk
    m0 = row0 // tm

    def body(p0_ref, p1_ref, p2_ref, w_ref, x_ref, g_ref, dh_ref, gx_ref, dg_ref):
        @pl.when(pl.program_id(0) == 0)
        def _():
            dg_ref[...] = jnp.zeros(dg_ref.shape, F32)

        du = None
        for k in range(IN_COLS // tk):
            part, cols = k // per, pl.ds((k % per) * tk, tk)
            ref, slot = (p0_ref, part) if part < 4 else (p1_ref, part - 4) if part < 7 else (p2_ref, part - 7)
            d = _dot_nt(ref[slot, :, cols], w_ref[k // shard_blocks, :, pl.ds((k % shard_blocks) * tk, tk)])
            du = d if du is None else du + d
        x = x_ref[...]
        r = lax.rsqrt(jnp.mean(x * x, axis=-1, keepdims=True) + EPS)
        nrm = x * r
        dg_ref[...] += jnp.sum(du * nrm, axis=0, keepdims=True)
        dn = du * g_ref[...]
        gx_ref[...] = dh_ref[...].astype(F32) + r * (dn - nrm * jnp.mean(dn * nrm, axis=-1, keepdims=True))

    def pspec(n):
        return pl.BlockSpec((n, tm, D_MODEL), lambda m: (0, m0 + m, 0))

    row_in = pl.BlockSpec((tm, D_MODEL), lambda m: (m0 + m, 0))
    vec = pl.BlockSpec((1, D_MODEL), lambda m: (0, 0))
    return pl.pallas_call(
        body, grid=(rows // tm,),
        in_specs=[pspec(4), pspec(3), pspec(3),
                  pl.BlockSpec(w4.shape, lambda m: (0, 0, 0), pipeline_mode=pl.Buffered(1)),
                  row_in, vec, row_in],
        out_specs=[pl.BlockSpec((tm, D_MODEL), lambda m: (m, 0)), vec],
        out_shape=[jax.ShapeDtypeStruct((rows, D_MODEL), F32), jax.ShapeDtypeStruct((1, D_MODEL), F32)],
        name="input_grad",
    )(da4, dc3, db3, w4, xp, norm_g, dh2)


class _Step:
    def __init__(self, x, tgt, norm_g, chip, after=None):
        self.norm_g, self.chip = norm_g, chip
        self.slopes = _alibi_slopes()
        self.xp, self.tp = _to_residue_major(x, tgt, after)
        self.u, self.ut = _rms_in(self.xp, norm_g)

    def project_own(self, w_own):
        self.proj_own = _in_proj(self.u, self.chip, w_own=w_own)

    def project(self, w4, others):
        self.proj_own = _in_proj(self.u, self.chip, w4=w4, partial=self.proj_own, others=others)

    def mixers(self, w4, taps):
        self.w4, self.taps, self.proj = w4, taps, self.proj_own
        self.hc, self.hct = _conv_fwd(self.proj, taps)
        fwd = [_attn_fwd(self.proj, self.slopes, d) for d in PATTERNS]
        self.o, self.lse, self.ha, self.hat = _attn_combine([f[0] for f in fwd], [f[1] for f in fwd], self.proj)

    def merge_and_loss(self, woc, woa, wo, b_merge, final_g):
        self.woc, self.woa, self.wo, self.b_merge = woc, woa, wo, b_merge
        (self.yc, self.ya, self.dh2b, self.d_final_g, self.loss8, self.d_wo) = _merge_loss(
            self.hc, self.ha, woc, woa, wo, self.proj, b_merge, self.xp, final_g, self.tp)

    def out_weight_grads(self):
        (self.dhc, self.dout, self.dsum, self.db3, self.d_bias, d_woc, d_woa) = _merge_bwd(
            self.dh2b, self.wo, self.woc, self.woa, self.yc, self.ya, self.proj, self.b_merge, self.o,
            self.hct, self.hat)
        return d_woc, d_woa, self.d_wo

    def conv_grads(self, after=0.0):
        self.da4, self.d_taps = _conv_bwd(self.proj, self.taps + after, self.dhc)

    def in_weight_grad(self, after=0.0):
        slopes = self.slopes + after
        self.dc3 = None
        for d in PATTERNS:
            self.dc3 = _attn_bwd(self.proj, self.dout, self.lse, self.dsum, slopes, d, prev=self.dc3)
        return _dw_in(self.ut, self.da4, self.dc3, self.db3)

    def input_grad(self, half, after=0.0):
        rows = self.xp.shape[0] // 2
        return _input_grad(self.da4, self.dc3, self.db3, self.w4, self.xp, self.norm_g + after, self.dh2b,
                           half * rows, rows)


def _local_grads(x, tgt, norm_g, w4, b_merge, conv_w, woc, woa, wo, final_g):
    st = _Step(x, tgt, norm_g, jnp.zeros((1,), jnp.int32))
    st.project_own(w4[0])
    st.project(w4, (2, 1))
    st.project(w4, (3,))
    st.mixers(w4, conv_w)
    st.merge_and_loss(woc, woa, wo, b_merge, final_g)
    d_woc, d_woa, d_wo = st.out_weight_grads()
    st.conv_grads()
    d_w4 = st.in_weight_grad()
    gx_lo, dg_lo = st.input_grad(0)
    gx_hi, dg_hi = st.input_grad(1)
    return (st.loss8, _to_natural(gx_lo, gx_hi), dg_lo + dg_hi, d_w4, st.d_bias, st.d_taps, d_woc, d_woa, d_wo,
            st.d_final_g)


MESH = pl.DeviceIdType.MESH
_CHIP_FLIPS = ((1, 0), (0, 1), (1, 1))
_ANY = pl.BlockSpec(memory_space=pl.ANY)


def _place():
    return lax.axis_index("x"), lax.axis_index("y"), lax.axis_index("c")


def _flip(v, f):
    return 1 - v if f else v


def _remote(src, dst, send_sems, recv_sems, k, device):
    return pltpu.make_async_remote_copy(src_ref=src, dst_ref=dst, send_sem=send_sems.at[k], recv_sem=recv_sems.at[k],
                                        device_id=device, device_id_type=MESH)


def _place_shard(w, chip, dtype):
    rows, cols = w.shape
    tm = min(rows, 128)

    def body(chip_ref, w_ref, o_ref):
        o_ref[0] = w_ref[...].astype(o_ref.dtype)

    return pl.pallas_call(
        body,
        grid_spec=pltpu.PrefetchScalarGridSpec(
            num_scalar_prefetch=1, grid=(rows // tm,),
            in_specs=[pl.BlockSpec((tm, cols), lambda i, chip_ref: (i, 0))],
            out_specs=pl.BlockSpec((1, tm, cols), lambda i, chip_ref: (chip_ref[0], i, 0))),
        out_shape=jax.ShapeDtypeStruct((4, rows, cols), dtype),
        name="place_shard",
    )(chip, w)


def _gather_copies_to(flips, whole=()):
    def copies(arrs, _, send_sems, recv_sems):
        x, y, c = _place()
        out = []
        for a, arr in enumerate(arrs):
            h = arr.shape[1] // 2
            mine = arr.at[2 * x + y] if a in whole else arr.at[2 * x + y, pl.ds(pl.multiple_of(c * h, 8), h)]
            for i, t in enumerate(flips):
                fx, fy = _CHIP_FLIPS[t]
                out.append(_remote(mine, mine, send_sems, recv_sems, len(flips) * a + i,
                                   (_flip(x, fx), _flip(y, fy), c)))
        return out
    return copies


def _forward_to_sibling(arrs, flips=(0, 1, 2)):
    n = len(arrs)

    def body(*refs):
        outs = refs[n:2 * n]
        send_sems, recv_sems = refs[2 * n:]
        x, y, c = _place()
        sibling = (x, y, 1 - c)
        started = []
        for a in range(n):
            h = outs[a].shape[1] // 2
            rows = pl.ds(pl.multiple_of(c * h, 8), h)
            for t in flips:
                fx, fy = _CHIP_FLIPS[t]
                landed = outs[a].at[2 * _flip(x, fx) + _flip(y, fy), rows]
                cp = _remote(landed, landed, send_sems, recv_sems, 3 * a + t, sibling)
                cp.start()
                started.append(cp)
        for a in range(n):
            h = outs[a].shape[1] // 2
            rows = pl.ds(pl.multiple_of((1 - c) * h, 8), h)
            for t in flips:
                fx, fy = _CHIP_FLIPS[t]
                handed = outs[a].at[2 * _flip(x, fx) + _flip(y, fy), rows]
                _remote(handed, handed, send_sems, recv_sems, 3 * a + t, sibling).wait_recv()
        for cp in started:
            cp.wait_send()

    return pl.pallas_call(
        body, in_specs=[_ANY] * n, out_specs=[_ANY] * n,
        out_shape=[jax.ShapeDtypeStruct(s.shape, s.dtype) for s in arrs],
        input_output_aliases={a: a for a in range(n)},
        scratch_shapes=[pltpu.SemaphoreType.DMA((3 * n,)), pltpu.SemaphoreType.DMA((3 * n,))],
        name="gathered_to_sibling_" + "".join(str(t) for t in flips),
    )(*arrs)


_HBM = pl.BlockSpec(memory_space=pltpu.HBM)
_SEM = pl.BlockSpec(memory_space=pltpu.SEMAPHORE)
_EFFECT = pltpu.SideEffectType.DATAFLOW_SIDE_EFFECTING


class _SplitExchange:
    def __init__(self, name, srcs, land_shapes, n_copies, copies, riders=()):
        self.name, self.n, self.nl, self.copies = name, len(srcs), len(land_shapes), copies
        n, nb = self.n, len(srcs) + len(land_shapes)
        lands = [lax.empty(s.shape, s.dtype) for s in land_shapes]
        bufs = [pltpu.with_memory_space_constraint(a, pltpu.HBM) for a in (*srcs, *lands, *riders)]
        na = len(bufs)

        def body(*refs):
            send_sems, recv_sems = refs[na], refs[na + 1]
            for cp in copies(refs[:n], refs[n:nb], send_sems, recv_sems):
                cp.start()
            refs[-1][...] = jnp.zeros(refs[-1].shape, F32)

        outs = pl.pallas_call(
            body, name=name + "_start",
            in_specs=[_HBM] * na,
            out_specs=[_SEM, _SEM] + [_HBM] * na + [pl.BlockSpec(memory_space=pltpu.VMEM)],
            out_shape=[pltpu.SemaphoreType.DMA((n_copies,)), pltpu.SemaphoreType.DMA((n_copies,))]
            + [pltpu.HBM(b.shape, b.dtype) for b in bufs] + [jax.ShapeDtypeStruct((8, LANES), F32)],
            input_output_aliases={i: 2 + i for i in range(na)},
            compiler_params=pltpu.CompilerParams(has_side_effects=_EFFECT),
        )(*bufs)
        self.sems, self.bufs, self.riders, self.token = outs[:2], outs[2:2 + nb], outs[2 + nb:2 + na], outs[-1]

    def after(self):
        return self.token[0, 0]

    def wait(self, done, riders=(), bufs=None):
        n, nb, copies = self.n, self.n + self.nl, self.copies
        bufs = [*(self.bufs if bufs is None else bufs),
                *[pltpu.with_memory_space_constraint(a, pltpu.HBM) for a in riders]]
        na = len(bufs)
        done = list(done) if isinstance(done, (list, tuple)) else [done]

        def body(*refs):
            send_sems, recv_sems = refs[na], refs[na + 1]
            for cp in copies(refs[:n], refs[n:nb], send_sems, recv_sems):
                cp.wait_send()
                cp.wait_recv()

        outs = pl.pallas_call(
            body, name=self.name + "_wait",
            in_specs=[_HBM] * na + [_SEM, _SEM] + [_ANY] * len(done),
            out_specs=[_HBM] * na,
            out_shape=[pltpu.HBM(b.shape, b.dtype) for b in bufs],
            input_output_aliases={i: i for i in range(na)},
            compiler_params=pltpu.CompilerParams(has_side_effects=_EFFECT),
        )(*bufs, *self.sems, *done)
        return outs[:n], outs[n:nb], outs[nb:]


def _sibling_copies(srcs, lands, send_sems, recv_sems):
    x, y, c = _place()
    out = []
    for a, (src, land) in enumerate(zip(srcs, lands)):
        h = src.shape[1] // 2
        theirs = pl.ds(pl.multiple_of((1 - c) * h, 8), h)
        out.append(_remote(src.at[:, theirs], land, send_sems, recv_sems, a, (x, y, 1 - c)))
    return out


def _grads_to_sibling(name, grads):
    shapes = [jax.ShapeDtypeStruct((4, g.shape[1] // 2, g.shape[2]), g.dtype) for g in grads]
    return _SplitExchange(name, grads, shapes, len(grads), _sibling_copies)


def _chip_copies(srcs, lands, send_sems, recv_sems):
    x, y, c = _place()
    out = []
    for a, (src, land) in enumerate(zip(srcs, lands)):
        for t, (fx, fy) in enumerate(_CHIP_FLIPS):
            tx, ty = _flip(x, fx), _flip(y, fy)
            out.append(_remote(src.at[2 * tx + ty], land.at[t], send_sems, recv_sems, 3 * a + t, (tx, ty, c)))
    return out


def _grads_to_chips(name, parts):
    shapes = [jax.ShapeDtypeStruct((3, *p.shape[1:]), p.dtype) for p in parts]
    return _SplitExchange(name, parts, shapes, 3 * len(parts), _chip_copies)


def _add_halves(g, r, half):
    _, rows, cols = g.shape
    h = rows // 2
    tm = min(h, 128)
    nt = h // tm

    def body(half_ref, g_ref, r_ref, b_ref):
        b_ref[...] = (g_ref[...] + r_ref[...]).astype(b_ref.dtype)

    spec = pl.BlockSpec((1, tm, cols), lambda j, i, half_ref: (j, i, 0))
    return pl.pallas_call(
        body,
        grid_spec=pltpu.PrefetchScalarGridSpec(
            num_scalar_prefetch=1, grid=(4, nt),
            in_specs=[pl.BlockSpec((1, tm, cols), lambda j, i, half_ref: (j, half_ref[0] * nt + i, 0)), spec],
            out_specs=spec),
        out_shape=jax.ShapeDtypeStruct((4, h, cols), BF16),
        name="add_sibling_grads",
    )(half, g, r)


def _add_chips(g, r, recv, where):
    _, h, cols = r.shape
    tm = min(h, 128)
    nt = h // tm

    def body(where_ref, g_ref, r_ref, recv_ref, out_ref):
        own = g_ref[0] + r_ref[0]
        out_ref[...] = ((own + recv_ref[0].astype(F32)) + recv_ref[1].astype(F32)) + recv_ref[2].astype(F32)

    return pl.pallas_call(
        body,
        grid_spec=pltpu.PrefetchScalarGridSpec(
            num_scalar_prefetch=1, grid=(nt,),
            in_specs=[pl.BlockSpec((1, tm, cols), lambda i, w: (w[0], w[1] * nt + i, 0)),
                      pl.BlockSpec((1, tm, cols), lambda i, w: (w[0], i, 0)),
                      pl.BlockSpec((3, tm, cols), lambda i, w: (0, i, 0))],
            out_specs=pl.BlockSpec((tm, cols), lambda i, w: (w[1] * nt + i, 0))),
        out_shape=jax.ShapeDtypeStruct((2 * h, cols), F32),
        name="add_chip_grads",
    )(where, g, r, recv)


def _half_copies(srcs, _, send_sems, recv_sems):
    x, y, c = _place()
    out = []
    for a, src in enumerate(srcs):
        h = src.shape[0] // 2
        mine = src.at[pl.ds(pl.multiple_of(c * h, 8), h)]
        out.append(_remote(mine, mine, send_sems, recv_sems, a, (x, y, 1 - c)))
    return out


def _share_halves(name, shards):
    return _SplitExchange(name, shards, [], len(shards), _half_copies)


def _reduce_small(rows):
    cols = rows[0].shape[1]
    n = len(rows)
    assert sum(r.shape[0] for r in rows) <= 8

    def body(*refs):
        ins, out_ref = refs[:n], refs[n]
        vec_ref, gath_ref, send_sems, recv_sems = refs[n + 1:]
        x, y, c = _place()
        me = 4 * x + 2 * y + c
        vec_ref[...] = jnp.zeros(vec_ref.shape, F32)
        at = 0
        for r in ins:
            vec_ref[at:at + r.shape[0], :] = r[...]
            at += r.shape[0]
        copies = []
        for k in range(1, 8):
            peer = (_flip(x, (k >> 2) & 1), _flip(y, (k >> 1) & 1), _flip(c, k & 1))
            copies.append(_remote(vec_ref, gath_ref.at[me], send_sems, recv_sems, k - 1, peer))
        for cp in copies:
            cp.start()
        gath_ref[me] = vec_ref[...]
        for cp in copies:
            cp.wait()
        tot = gath_ref[0]
        for dev in range(1, 8):
            tot = tot + gath_ref[dev]
        out_ref[...] = tot
        out_ref[7:8, :] = jnp.zeros((1, cols), F32) + jnp.sum(tot[7:8, :])

    vm = pl.BlockSpec(memory_space=pltpu.VMEM)
    return pl.pallas_call(
        body, in_specs=[vm] * n, out_specs=vm,
        out_shape=jax.ShapeDtypeStruct((8, cols), F32),
        scratch_shapes=[pltpu.VMEM((8, cols), F32), pltpu.VMEM((8, 8, cols), F32),
                        pltpu.SemaphoreType.DMA((7,)), pltpu.SemaphoreType.DMA((7,))],
        name="reduce_small",
    )(*rows)


def _adamw_tile(w_ref, g_ref, m_ref, v_ref, d_ref, m2_ref, v2_ref, gout_ref):
    gr = g_ref[...]
    m2 = ADAM_B1 * m_ref[...] + (1.0 - ADAM_B1) * gr
    v2 = ADAM_B2 * v_ref[...] + (1.0 - ADAM_B2) * (gr * gr)
    m_hat = m2 / (1.0 - ADAM_B1 ** ADAM_STEP)
    v_hat = v2 / (1.0 - ADAM_B2 ** ADAM_STEP)
    d_ref[...] = -ADAM_LR * (m_hat / (jnp.sqrt(v_hat) + ADAM_EPS) + ADAM_WD * w_ref[...])
    m2_ref[...] = m2
    v2_ref[...] = v2
    gout_ref[...] = gr


def _adamw(w, g, m, v, name):
    rows, cols = w.shape
    tm = 128 if rows % 128 == 0 else rows

    def body(*refs):
        _adamw_tile(*refs)

    spec = pl.BlockSpec((tm, cols), lambda i: (i, 0))
    sds = jax.ShapeDtypeStruct((rows, cols), F32)
    return pl.pallas_call(body, grid=(rows // tm,), in_specs=[spec] * 4, out_specs=[spec] * 4,
                          out_shape=[sds] * 4, name=name)(w, g, m, v)


def _adamw_half(w, g, m, v, half, name, other=()):
    rows, cols = w.shape
    tm = 128
    nt = rows // 2 // tm

    def body(half_ref, *refs):
        _adamw_tile(*refs[:4], *refs[4 + len(other):])

    spec = pl.BlockSpec((tm, cols), lambda i, h: (h[0] * nt + i, 0))
    sds = jax.ShapeDtypeStruct((rows, cols), F32)
    return pl.pallas_call(
        body,
        grid_spec=pltpu.PrefetchScalarGridSpec(
            num_scalar_prefetch=1, grid=(nt,), in_specs=[spec] * 4 + [_ANY] * len(other), out_specs=[spec] * 4),
        out_shape=[sds] * 4, input_output_aliases={5 + k: k for k in range(len(other))}, name=name,
    )(half, w, g, m, v, *other)


def kernel(x, norm_g, w_in, b_merge, conv_w, w_out_conv, w_out_attn, w_o, final_g, loss_target, m_norm_g, m_w_in, m_b_merge, m_conv_w, m_w_out_conv, m_w_out_attn, m_w_o, m_final_g, v_norm_g, v_w_in, v_b_merge, v_conv_w, v_w_out_conv, v_w_out_attn, v_w_o, v_final_g):
    mx, my, mc = _place()
    chip = (2 * mx + my).astype(jnp.int32)
    seq = x.shape[1]

    chip1 = chip.reshape(1)
    slots = [_place_shard(w[0], chip1, MXU_DTYPE) for w in (w_in, w_out_conv, w_out_attn, w_o)]
    taps_slot = _place_shard(jnp.pad(conv_w[0], ((0, 5), (0, 0))), chip1, F32)
    gather_near = _SplitExchange("gather_w_in_near", [slots[0], taps_slot], [], 4,
                                 _gather_copies_to((0, 1), whole=(1,)))
    st = _Step(x[0], loss_target[0], norm_g, chip1, after=gather_near.token)
    st.project_own(w_in[0])
    near, _, _ = gather_near.wait([st.ut, st.proj_own])
    gather_far = _SplitExchange("gather_w_in_far", near, [], 2, _gather_copies_to((2,), whole=(1,)))
    (w4,) = _forward_to_sibling(gather_far.bufs[:1], flips=(0, 1))
    st.project(w4, (2, 1))
    (w4, taps4), _, out_slots = gather_far.wait([st.proj_own], riders=slots[1:], bufs=[w4, gather_far.bufs[1]])
    gather_out = _SplitExchange("gather_w_out", out_slots, [], 9, _gather_copies_to((0, 1, 2)), riders=[w4])
    (w4,) = _forward_to_sibling(gather_out.riders, flips=(2,))
    st.project(w4, (3,))
    st.mixers(w4, jnp.concatenate([taps4[j, :3, :] for j in range(4)], axis=1))
    out_ws, _, _ = gather_out.wait(st.o)
    woc, woa, wo = [w.reshape(D_MODEL, D_MODEL) for w in _forward_to_sibling(out_ws)]
    st.merge_and_loss(woc, woa, wo, b_merge, final_g.reshape(1, D_MODEL))

    half = mc.astype(jnp.int32).reshape(1)
    where = jnp.stack([chip, mc.astype(jnp.int32)])
    out_grads = [g.reshape(4, -1, D_MODEL) for g in st.out_weight_grads()]
    to_sibling = _grads_to_sibling("out_grads_to_sibling", out_grads)
    st.conv_grads(after=to_sibling.after())
    out_grads, out_from_sibling, _ = to_sibling.wait(st.da4)
    to_chips = _grads_to_chips("out_grads_to_chips",
                               [_add_halves(g, r, half) for g, r in zip(out_grads, out_from_sibling)])
    d_w4 = st.in_weight_grad(after=to_chips.after())
    out_from_chips = to_chips.wait(st.dc3)[1]

    to_sibling = _grads_to_sibling("in_grad_to_sibling", [d_w4])
    gx_lo, dg_lo = st.input_grad(0, after=to_sibling.after())
    (d_w4,), (from_sibling,), _ = to_sibling.wait(gx_lo)
    to_chips = _grads_to_chips("in_grad_to_chips", [_add_halves(d_w4, from_sibling, half)])

    where_late = where + to_chips.after().astype(jnp.int32)
    out_reduced = [_add_chips(g, r, recv, where_late)
                   for g, r, recv in zip(out_grads, out_from_sibling, out_from_chips)]
    share_out = _share_halves("share_out_grads", out_reduced)
    gx_hi, dg_hi = st.input_grad(1, after=share_out.after())
    grad_x = _to_natural(gx_lo, gx_hi)
    g_woc, g_woa, g_wo = share_out.wait(gx_hi)[0]

    small = _reduce_small([dg_lo + dg_hi, st.d_bias.reshape(2, D_MODEL), st.d_taps, st.d_final_g,
                           st.loss8.reshape(1, D_MODEL)])
    loss = (0.5 / D_MODEL) * small[7, 0]
    g_taps = lax.dynamic_slice(small[3:6], (0, chip * (D_MODEL // 4)), (3, D_MODEL // 4))
    upd = {
        "norm_g": _adamw(norm_g, small[0:1], m_norm_g, v_norm_g, "adamw_norm_g"),
        "b_merge": _adamw(b_merge, small[1:3].reshape(1, 2 * D_MODEL), m_b_merge, v_b_merge, "adamw_b_merge"),
        "conv_w": _adamw(conv_w[0], g_taps, m_conv_w[0], v_conv_w[0], "adamw_conv_w"),
        "w_out_conv": _adamw(w_out_conv[0], g_woc, m_w_out_conv[0], v_w_out_conv[0], "adamw_w_out_conv"),
        "w_out_attn": _adamw(w_out_attn[0], g_woa, m_w_out_attn[0], v_w_out_attn[0], "adamw_w_out_attn"),
        "w_o": _adamw(w_o[0], g_wo, m_w_o[0], v_w_o[0], "adamw_w_o"),
        "final_g": _adamw(final_g.reshape(1, D_MODEL), small[6:7], m_final_g.reshape(1, D_MODEL),
                          v_final_g.reshape(1, D_MODEL), "adamw_final_g"),
    }
    behind = [grad_x] + [u[0] for u in upd.values()]
    in_reduced = _add_chips(d_w4, from_sibling, to_chips.wait(behind)[1][0], where)

    share_in = _share_halves("share_in_grad", [in_reduced])
    w_in_args = (w_in[0], m_w_in[0], v_w_in[0])
    own_rows = _adamw_half(w_in_args[0], share_in.bufs[0], *w_in_args[1:], half, "adamw_w_in_own_rows")
    (g_w_in,) = share_in.wait(own_rows[0])[0]
    upd["w_in"] = _adamw_half(w_in_args[0], g_w_in, *w_in_args[1:], 1 - half, "adamw_w_in_sibling_rows",
                              other=own_rows)

    names = ["norm_g", "w_in", "b_merge", "conv_w", "w_out_conv", "w_out_attn", "w_o", "final_g"]
    shapes = [norm_g.shape, w_in.shape, b_merge.shape, conv_w.shape, w_out_conv.shape, w_out_attn.shape,
              w_o.shape, final_g.shape]
    outs = [loss, grad_x.reshape(1, seq, D_MODEL)]
    for k in (3, 0, 1, 2):
        outs += [upd[n][k].reshape(s) for n, s in zip(names, shapes)]
    return tuple(outs)
```

```python
import functools

import numpy as np
import jax
import jax.numpy as jnp
from jax import lax
from jax.experimental import pallas as pl
from jax.experimental.pallas import tpu as pltpu

F32 = jnp.float32
BF16 = jnp.bfloat16
MXU_DTYPE = jnp.bfloat16
ACT_DTYPE = jnp.bfloat16

D_MODEL = 1024
N_HEADS = 16
HEAD_DIM = 64
QB = 128
N_RES = 16
LANES = 128
HP = N_HEADS * HEAD_DIM // LANES
IN_COLS = 10 * D_MODEL
SHARD_COLS = IN_COLS // 4
EPS = 1e-6
NEG = -1e30

ADAM_LR, ADAM_B1, ADAM_B2, ADAM_EPS, ADAM_WD, ADAM_STEP = 0.001, 0.9, 0.999, 1e-08, 0.01, 10

PATTERNS = {1: (16, 8), 4: (4, 32), 16: (1, 128)}

_NN = (((1,), (0,)), ((), ()))
_NT = (((1,), (1,)), ((), ()))


def _dot(a, b):
    return lax.dot_general(a.astype(MXU_DTYPE), b.astype(MXU_DTYPE), _NN, preferred_element_type=F32)


def _dot_nt(a, b):
    return lax.dot_general(a.astype(MXU_DTYPE), b.astype(MXU_DTYPE), _NT, preferred_element_type=F32)


def _split3(x):
    hi = x.astype(BF16)
    r1 = x - hi.astype(F32)
    mid = r1.astype(BF16)
    lo = (r1 - mid.astype(F32)).astype(BF16)
    return hi, mid, lo


def _select_cols(x, sel, terms):
    return sum(lax.dot_general(t, sel, _NN, preferred_element_type=F32) for t in _split3(x)[:terms])


def _sigmoid(z):
    return 1.0 / (1.0 + jnp.exp(-z))


def _head_expand_matrix():
    e = np.zeros((LANES, D_MODEL), np.float32)
    for h in range(N_HEADS):
        e[8 * h, HEAD_DIM * h:HEAD_DIM * (h + 1)] = 1.0
    return jnp.asarray(e, BF16)


def _head_sum_matrix():
    e = np.zeros((D_MODEL, LANES), np.float32)
    for h in range(N_HEADS):
        e[HEAD_DIM * h:HEAD_DIM * (h + 1), 8 * h:8 * (h + 1)] = 1.0
    return jnp.asarray(e, BF16)


def _attn_tables(d):
    g_n, rq = PATTERNS[d]
    q_n = g_n * rq
    gq, iq = np.arange(q_n) // rq, np.arange(q_n) % rq

    def tab(kn, base):
        k_n = g_n * kn
        gk, jk = np.arange(k_n) // kn, np.arange(k_n) % kn
        delta = g_n * (base + iq[:, None] - jk[None, :]) + gq[:, None] - gk[None, :]
        valid = (delta >= 0) & (delta <= QB)
        dist = np.where(valid, d * delta, 0).astype(np.float32)
        madd = np.where(valid, 0.0, NEG).astype(np.float32)
        return dist, madd

    d0, m0 = tab(rq if g_n == 1 else 2 * rq, 0)
    d1, m1 = tab(2 * rq, rq)
    return d0, m0, d1, m1


def _alibi_slopes():
    return jnp.exp2(-8.0 * jnp.arange(1, N_HEADS + 1, dtype=F32) / N_HEADS)


def _to_residue_major(x, tgt, after=None):
    s_n, c_n = x.shape
    lr = s_n // N_RES
    extra = [] if after is None else [after]

    def body(x_ref, t_ref, *rest):
        xo_ref, to_ref = rest[-2:]
        for r in range(N_RES):
            xo_ref[r] = x_ref[pl.ds(r, lr, stride=N_RES), :]
            to_ref[r] = t_ref[pl.ds(r, lr, stride=N_RES), :]

    nat = pl.BlockSpec((s_n, LANES), lambda j: (0, j))
    res = pl.BlockSpec((N_RES, lr, LANES), lambda j: (0, 0, j))
    xo, to = pl.pallas_call(
        body, grid=(c_n // LANES,),
        in_specs=[nat, nat] + [pl.BlockSpec((8, LANES), lambda j: (0, 0))] * len(extra),
        out_specs=[res, res],
        out_shape=[jax.ShapeDtypeStruct((N_RES, lr, c_n), F32)] * 2,
        name="perm_in",
    )(x, tgt, *extra)
    return xo.reshape(s_n, c_n), to.reshape(s_n, c_n)


def _to_natural(gx_lo, gx_hi):
    half_rows, c_n = gx_lo.shape
    lr = half_rows // (N_RES // 2)

    def body(lo_ref, hi_ref, o_ref):
        for r in range(N_RES):
            o_ref[pl.ds(r, lr, stride=N_RES), :] = lo_ref[r] if r < N_RES // 2 else hi_ref[r - N_RES // 2]

    half = pl.BlockSpec((N_RES // 2, lr, LANES), lambda j: (0, 0, j))
    return pl.pallas_call(
        body, grid=(c_n // LANES,),
        in_specs=[half, half],
        out_specs=pl.BlockSpec((2 * half_rows, LANES), lambda j: (0, j)),
        out_shape=jax.ShapeDtypeStruct((2 * half_rows, c_n), F32),
        name="perm_out",
    )(gx_lo.reshape(N_RES // 2, lr, c_n), gx_hi.reshape(N_RES // 2, lr, c_n))


def _rms_in(xp, norm_g):
    s_n, c_n = xp.shape
    tm = 512

    def body(x_ref, g_ref, u_ref, ut_ref):
        x = x_ref[...]
        r = lax.rsqrt(jnp.mean(x * x, axis=-1, keepdims=True) + EPS)
        u = x * r * g_ref[...]
        u_ref[...] = u.astype(u_ref.dtype)
        ut_ref[...] = u.T.astype(ut_ref.dtype)

    return pl.pallas_call(
        body, grid=(s_n // tm,),
        in_specs=[pl.BlockSpec((tm, c_n), lambda i: (i, 0)), pl.BlockSpec((1, c_n), lambda i: (0, 0))],
        out_specs=[pl.BlockSpec((tm, c_n), lambda i: (i, 0)), pl.BlockSpec((c_n, tm), lambda i: (0, i))],
        out_shape=[jax.ShapeDtypeStruct((s_n, c_n), ACT_DTYPE), jax.ShapeDtypeStruct((c_n, s_n), ACT_DTYPE)],
        name="rms_in",
    )(xp, norm_g)


def _in_proj(u, chip, w_own=None, w4=None, partial=None, others=()):
    s_n = u.shape[0]
    tn, cm = 512, 512
    per = SHARD_COLS // tn
    own = partial is None

    def body(chip_ref, a_ref, b_ref, *rest):
        o_ref = rest[-1]
        b = b_ref[...]
        for c in range(s_n // cm):
            o_ref[c * cm:(c + 1) * cm, :] = _dot(a_ref[c * cm:(c + 1) * cm, :], b).astype(o_ref.dtype)

    def shard(n, chip_ref):
        if own:
            return chip_ref[0]
        mask = others[-1]
        for i, m in enumerate(others[:-1]):
            mask = jnp.where(n // per == i, m, mask)
        return jnp.bitwise_xor(chip_ref[0], mask)

    w_spec = (pl.BlockSpec((D_MODEL, tn), lambda n, c: (0, n)) if own else
              pl.BlockSpec((None, D_MODEL, tn), lambda n, c: (shard(n, c), 0, n % per)))
    return pl.pallas_call(
        body,
        grid_spec=pltpu.PrefetchScalarGridSpec(
            num_scalar_prefetch=1, grid=(per if own else len(others) * per,),
            in_specs=[pl.BlockSpec((s_n, D_MODEL), lambda n, c: (0, 0)), w_spec] + ([] if own else [_ANY]),
            out_specs=pl.BlockSpec((s_n, tn), lambda n, c: (0, shard(n, c) * per + n % per))),
        out_shape=jax.ShapeDtypeStruct((s_n, IN_COLS), ACT_DTYPE),
        input_output_aliases={} if own else {3: 0},
        name="in_proj_own" if own else "in_proj_" + "_".join(str(m) for m in others),
    )(*([chip, u, w_own] if own else [chip, u, w4, partial]))


def _conv_terms(xc_ref, cg_ref, r, row, lr, cache):
    def a_of(q):
        if q not in cache:
            cache[q] = cg_ref[q].astype(F32) * xc_ref[q].astype(F32)
        return cache[q]

    def shift_down(v):
        return jnp.where(row >= 1, pltpu.roll(v, 1, 0), 0.0)

    a = a_of(r)
    am1 = a_of(r - 1) if r >= 1 else shift_down(a_of(N_RES - 1))
    am2 = a_of(r - 2) if r >= 2 else shift_down(a_of(N_RES - 2 + r))
    return a, am1, am2


def _conv_fwd(proj, conv_w):
    s_n = proj.shape[0]
    lr = s_n // N_RES
    pv = proj.reshape(N_RES, lr, IN_COLS)

    def body(xc_ref, bg_ref, cg_ref, zc_ref, w_ref, hc_ref, hct_ref):
        w = w_ref[...]
        row = lax.broadcasted_iota(jnp.int32, (lr, LANES), 0)
        products = {}
        for r in range(N_RES):
            a, am1, am2 = _conv_terms(xc_ref, cg_ref, r, row, lr, products)
            c = w[0:1] * am2 + w[1:2] * am1 + w[2:3] * a
            z = zc_ref[r].astype(F32)
            hc = z * _sigmoid(z) * bg_ref[r].astype(F32) * c
            hc_ref[r] = hc.astype(hc_ref.dtype)
            hct_ref[:, r * lr:(r + 1) * lr] = hc.T.astype(hct_ref.dtype)

    def col(part):
        return pl.BlockSpec((N_RES, lr, LANES), lambda j: (0, 0, part * 8 + j))

    hc, hct = pl.pallas_call(
        body, grid=(D_MODEL // LANES,),
        in_specs=[col(0), col(1), col(2), col(3), pl.BlockSpec((3, LANES), lambda j: (0, j))],
        out_specs=[pl.BlockSpec((N_RES, lr, LANES), lambda j: (0, 0, j)),
                   pl.BlockSpec((LANES, s_n), lambda j: (j, 0))],
        out_shape=[jax.ShapeDtypeStruct((N_RES, lr, D_MODEL), ACT_DTYPE),
                   jax.ShapeDtypeStruct((D_MODEL, s_n), ACT_DTYPE)],
        name="conv_fwd",
    )(pv, pv, pv, pv, conv_w)
    return hc.reshape(s_n, D_MODEL), hct


RES_PER_STEP = 8
CLASSES_PER_STEP = 2
ATTN_BATCH = 16

_BNT = (((2,), (2,)), ((0,), (0,)))
_BNN = (((2,), (1,)), ((0,), (0,)))


def _bdot(a, b, dims):
    return lax.dot_general(a.astype(MXU_DTYPE), b.astype(MXU_DTYPE), dims, preferred_element_type=F32)


def _pattern_view_shape(s_n, c_n, g_n, lead=()):
    lr = s_n // N_RES
    return (*lead, 4, 4, lr, c_n) if g_n == 4 else (*lead, N_RES, lr, c_n)


def _pattern_view(a, g_n, lead=()):
    return a.reshape(_pattern_view_shape(a.shape[-2], a.shape[-1], g_n, lead))


def _pattern_block(g_n, lr):
    if g_n == 4:
        return (4, CLASSES_PER_STEP, lr, LANES)
    return (16 if g_n == 16 else RES_PER_STEP, lr, LANES)


def _pattern_grid(g_n):
    return ({1: N_RES // RES_PER_STEP, 4: 4 // CLASSES_PER_STEP, 16: 1}[g_n], HP)


def _pattern_spec(g_n, lr, col_of_hp, lead=()):
    z = (0,) * len(lead)
    block = (*lead, *_pattern_block(g_n, lr))
    if g_n == 16:
        return pl.BlockSpec(block, lambda r, hp: (*z, 0, 0, col_of_hp(hp)))
    if g_n == 4:
        return pl.BlockSpec(block, lambda r, hp: (*z, 0, r, 0, col_of_hp(hp)))
    return pl.BlockSpec(block, lambda r, hp: (*z, r, 0, col_of_hp(hp)))


def _aligned(start, m):
    return start if isinstance(start, int) else pl.multiple_of(start, m)


class _Units:
    def __init__(self, g_n, rq):
        self.g_n, self.rq = g_n, rq
        self.per_res, self.paired = g_n == 1, rq == 8

    def plan(self, lr, size):
        if self.per_res:
            return [0], lr // self.rq - 1, lambda j: [pl.multiple_of(j * self.rq, self.rq)]
        if self.paired:
            per = min(size // 2, lr // 16)
            assert (lr // 16) % per == 0
            return ([i * 16 for i in range(per)], lr // 16 // per - 1,
                    lambda j: [pl.multiple_of((j * per + i) * 16, 16) for i in range(per)])
        step, classes = self.rq, range(CLASSES_PER_STEP)
        per = min(size // CLASSES_PER_STEP, lr // step)
        assert (lr // step) % per == 0
        return ([(c, i * step) for c in classes for i in range(per)], lr // step // per - 1,
                lambda j: [(c, pl.multiple_of((j * per + i) * step, step)) for c in classes for i in range(per)])

    def count(self, qs):
        return RES_PER_STEP if self.per_res else len(qs) * (2 if self.paired else 1)

    def _split(self, tiles, lo, rows):
        return tiles[:, lo:lo + rows].reshape(self.g_n * rows, LANES)

    def load_q(self, ref, qs):
        rq = self.rq
        if self.per_res:
            return ref[:, pl.ds(qs[0], rq), :]
        if self.paired:
            tiles = [ref[:, pl.ds(q, 16), :].astype(F32) for q in qs]
            return jnp.stack([self._split(t, lo, 8) for t in tiles for lo in (0, 8)])
        return jnp.stack([ref[:, c, pl.ds(q, rq), :].reshape(self.g_n * rq, LANES) for c, q in qs])

    def _key_rows(self, q, at_start):
        return (0, 2 * self.rq) if at_start and q == 0 else (_aligned(q - self.rq, self.rq), 2 * self.rq)

    def load_k(self, ref, qs, first):
        rq = self.rq
        if self.per_res:
            return ref[:, pl.ds(0, rq), :] if first else ref[:, pl.ds(_aligned(qs[0] - rq, rq), 2 * rq), :]
        if self.paired:
            out = []
            for i, q in enumerate(qs):
                if first and i == 0:
                    t = ref[:, 0:16, :].astype(F32)
                    out += [self._split(t, 0, 16)] * 2
                else:
                    t = ref[:, pl.ds(_aligned(q - 16, 16), 32), :].astype(F32)
                    out += [self._split(t, 8, 16), self._split(t, 16, 16)]
            return jnp.stack(out)
        rows = [(c, *self._key_rows(q, first)) for c, q in qs]
        return jnp.stack([ref[:, c, pl.ds(k0, n), :].reshape(self.g_n * n, LANES) for c, k0, n in rows])

    def store_q(self, ref, qs, val, add=False, lead=()):
        if self.per_res:
            pieces = [((), qs[0], self.rq, val)]
        elif self.paired:
            pieces = [((), q, 16, jnp.concatenate([val[2 * i].reshape(self.g_n, 8, LANES),
                                                   val[2 * i + 1].reshape(self.g_n, 8, LANES)], axis=1))
                      for i, q in enumerate(qs)]
        else:
            pieces = [((c,), q, self.rq, val[i].reshape(self.g_n, self.rq, LANES)) for i, (c, q) in enumerate(qs)]
        for cls, start, rows, v in pieces:
            idx = (*lead, slice(None), *cls, pl.ds(start, rows), slice(None))
            ref[idx] = (ref[idx] + v if add else v).astype(ref.dtype)

    def add_k(self, ref, qs, val, first):
        rq = self.rq
        if self.per_res:
            k0, n = (0, rq) if first else (_aligned(qs[0] - rq, rq), 2 * rq)
            ref[:, pl.ds(k0, n), :] += val
            return
        if self.paired:
            starts = [s for i, q in enumerate(qs)
                      for s in ((0, 0) if first and i == 0 else (_aligned(q - 8, 8), q))]
            rows = [((), s, 16) for s in starts]
        else:
            rows = [((c,), *self._key_rows(q, first)) for c, q in qs]
        for b, (cls, k0, n) in enumerate(rows):
            idx = (slice(None), *cls, pl.ds(k0, n), slice(None))
            ref[idx] += val[b].reshape(self.g_n, n, LANES)


def _batch_bias(un, qs, at_start, first_ref, general_ref):
    if not at_start:
        return general_ref[...][None]
    if un.per_res:
        return first_ref[...][None]
    if un.paired:
        return jnp.concatenate([first_ref[...][None]] + [general_ref[...][None]] * (un.count(qs) - 1), axis=0)
    return jnp.stack([(first_ref if q == 0 else general_ref)[...] for _, q in qs])


def _stack_heads(x, low):
    zero = jnp.zeros_like(x)
    return jnp.concatenate([jnp.where(low, x, zero), jnp.where(low, zero, x)], axis=1)


def _attn_fwd(proj, slopes, d):
    g_n, rq = PATTERNS[d]
    un = _Units(g_n, rq)
    s_n = proj.shape[0]
    lr = s_n // N_RES
    q_n = g_n * rq
    d0, m0, d1, m1 = _attn_tables(d)
    first, n_more, later = un.plan(lr, ATTN_BATCH)

    def body(sl_ref, q_ref, k_ref, v_ref, d0_ref, m0_ref, d1_ref, m1_ref, o_ref, lse_ref, b0_ref, b1_ref):
        hp = pl.program_id(1)

        @pl.when(hp == 0)
        def _():
            lse_ref[...] = jnp.zeros(lse_ref.shape, F32)

        for h in (0, 1):
            slope = sl_ref[2 * hp + h]
            b0_ref[h * q_n:(h + 1) * q_n, :] = m0_ref[...] - slope * d0_ref[...]
            b1_ref[h * q_n:(h + 1) * q_n, :] = m1_ref[...] - slope * d1_ref[...]

        lane = lax.broadcasted_iota(jnp.int32, (1, q_n, LANES), 2)
        low = lane < HEAD_DIM
        grp = lane // 8

        def batch(qs, at_start):
            qq = _stack_heads(un.load_q(q_ref, qs) * 0.125, low)
            s = _bdot(qq, un.load_k(k_ref, qs, at_start), _BNT) + _batch_bias(un, qs, at_start, b0_ref, b1_ref)
            m = jnp.max(s, axis=2, keepdims=True)
            p = jnp.exp(s - m)
            l = jnp.sum(p, axis=2, keepdims=True)
            o = _bdot(p, un.load_k(v_ref, qs, at_start), _BNN) * (1.0 / l)
            lse = m + jnp.log(l)
            un.store_q(o_ref, qs, jnp.where(low, o[:, :q_n], o[:, q_n:]))
            upd = jnp.where(grp == 2 * hp, lse[:, :q_n], 0.0) + jnp.where(grp == 2 * hp + 1, lse[:, q_n:], 0.0)
            un.store_q(lse_ref, qs, upd, add=True)

        batch(first, True)

        def more(j, carry):
            batch(later(j), False)
            return carry

        lax.fori_loop(1, 1 + n_more, more, 0)

    pv = _pattern_view(proj, g_n)
    full = lambda a: pl.BlockSpec(a.shape, lambda r, hp: (0, 0))
    o, lse = pl.pallas_call(
        body, grid=_pattern_grid(g_n),
        in_specs=[pl.BlockSpec(memory_space=pltpu.SMEM),
                  _pattern_spec(g_n, lr, lambda hp: 32 + hp),
                  _pattern_spec(g_n, lr, lambda hp: 40 + hp),
                  _pattern_spec(g_n, lr, lambda hp: 48 + hp),
                  full(d0), full(m0), full(d1), full(m1)],
        out_specs=[_pattern_spec(g_n, lr, lambda hp: hp), _pattern_spec(g_n, lr, lambda hp: 0)],
        out_shape=[jax.ShapeDtypeStruct(_pattern_view_shape(s_n, D_MODEL, g_n), ACT_DTYPE),
                   jax.ShapeDtypeStruct(_pattern_view_shape(s_n, LANES, g_n), F32)],
        scratch_shapes=[pltpu.VMEM((2 * q_n, d0.shape[1]), F32), pltpu.VMEM((2 * q_n, 2 * q_n), F32)],
        name=f"attn_fwd_d{d}",
    )(slopes, pv, pv, pv, d0, m0, d1, m1)
    return o.reshape(s_n, D_MODEL), lse.reshape(s_n, LANES)


def _attn_combine(outs, lses, proj):
    s_n = proj.shape[0]
    tm = 512

    def body(o1_ref, o2_ref, o3_ref, l1_ref, l2_ref, l3_ref, za_ref, e_ref, o_ref, lse_ref, ha_ref, hat_ref):
        ls = [l1_ref[...], l2_ref[...], l3_ref[...]]
        mx = jnp.maximum(jnp.maximum(ls[0], ls[1]), ls[2])
        den = sum(jnp.exp(l - mx) for l in ls)
        lse = mx + jnp.log(den)
        lse_ref[...] = lse
        o = jnp.zeros((tm, D_MODEL), F32)
        for l, oref in zip(ls, (o1_ref, o2_ref, o3_ref)):
            o = o + _select_cols(jnp.exp(l - lse), e_ref[...], terms=2) * oref[...].astype(F32)
        o_ref[...] = o.astype(o_ref.dtype)
        z = za_ref[...].astype(F32)
        ha = z * _sigmoid(z) * o
        ha_ref[...] = ha.astype(ha_ref.dtype)
        hat_ref[...] = ha.T.astype(hat_ref.dtype)

    row = lambda w: pl.BlockSpec((tm, w), lambda i: (i, 0))
    return pl.pallas_call(
        body, grid=(s_n // tm,),
        in_specs=[row(D_MODEL)] * 3 + [row(LANES)] * 3
        + [pl.BlockSpec((tm, D_MODEL), lambda i: (i, 7)), pl.BlockSpec((LANES, D_MODEL), lambda i: (0, 0))],
        out_specs=[row(D_MODEL), row(LANES), row(D_MODEL), pl.BlockSpec((D_MODEL, tm), lambda i: (0, i))],
        out_shape=[jax.ShapeDtypeStruct((s_n, D_MODEL), ACT_DTYPE), jax.ShapeDtypeStruct((s_n, LANES), F32),
                   jax.ShapeDtypeStruct((s_n, D_MODEL), ACT_DTYPE), jax.ShapeDtypeStruct((D_MODEL, s_n), ACT_DTYPE)],
        name="attn_combine",
    )(*outs, *lses, proj, _head_expand_matrix())


CHAIN_ROWS = 256


def _row_chains(tm):
    return [slice(r, r + CHAIN_ROWS) for r in range(0, tm, CHAIN_ROWS)]


def _gates(gc_ref, ga_ref, b_ref, rows):
    b = b_ref[...]
    gc = _sigmoid(gc_ref[rows, :].astype(F32) + b[:, :D_MODEL])
    ga = _sigmoid(ga_ref[rows, :].astype(F32) + b[:, D_MODEL:])
    return gc, ga


def _merge_loss(hc, ha, woc, woa, wo, proj, b_merge, xp, final_g, tgt):
    s_n = xp.shape[0]
    tm = 512

    def body(hc_ref, ha_ref, woc_ref, woa_ref, wo_ref, gc_ref, ga_ref, b_ref, x_ref, gf_ref, t_ref,
             yc_ref, ya_ref, dhb_ref, dgf_ref, loss_ref, dwo_ref, mgt_ref):
        i = pl.program_id(0)

        @pl.when(i == 0)
        def _():
            dgf_ref[...] = jnp.zeros(dgf_ref.shape, F32)
            loss_ref[...] = jnp.zeros(loss_ref.shape, F32)
            dwo_ref[...] = jnp.zeros(dwo_ref.shape, F32)

        gf = gf_ref[...]
        for rows in _row_chains(tm):
            yc = _dot(hc_ref[rows, :], woc_ref[...])
            ya = _dot(ha_ref[rows, :], woa_ref[...])
            gc, ga = _gates(gc_ref, ga_ref, b_ref, rows)
            mg = gc * yc + ga * ya
            yc_ref[rows, :] = yc.astype(yc_ref.dtype)
            ya_ref[rows, :] = ya.astype(ya_ref.dtype)
            mgt_ref[:, rows] = mg.T.astype(mgt_ref.dtype)
            h2 = x_ref[rows, :] + _dot(mg, wo_ref[...])
            r2 = lax.rsqrt(jnp.mean(h2 * h2, axis=-1, keepdims=True) + EPS)
            nrm = h2 * r2
            err = nrm * gf - t_ref[rows, :]
            e2 = (err * err).reshape(-1, 8, D_MODEL).sum(axis=0)
            loss_ref[...] += sum(e2[:, c * LANES:(c + 1) * LANES] for c in range(D_MODEL // LANES))
            dy = err * (1.0 / D_MODEL)
            dgf_ref[...] += jnp.sum(dy * nrm, axis=0, keepdims=True)
            dn = dy * gf
            dh2 = r2 * (dn - nrm * jnp.mean(dn * nrm, axis=-1, keepdims=True))
            dhb_ref[rows, :] = dh2.astype(dhb_ref.dtype)
        dwo_ref[...] += _dot(mgt_ref[...], dhb_ref[...])

    row = pl.BlockSpec((tm, D_MODEL), lambda i: (i, 0))
    wsp = pl.BlockSpec((D_MODEL, D_MODEL), lambda i: (0, 0), pipeline_mode=pl.Buffered(1))
    vec = lambda w: pl.BlockSpec((1, w), lambda i: (0, 0))
    act = jax.ShapeDtypeStruct((s_n, D_MODEL), ACT_DTYPE)
    return pl.pallas_call(
        body, grid=(s_n // tm,),
        in_specs=[row, row, wsp, wsp, wsp,
                  pl.BlockSpec((tm, D_MODEL), lambda i: (i, 8)), pl.BlockSpec((tm, D_MODEL), lambda i: (i, 9)),
                  vec(2 * D_MODEL), row, vec(D_MODEL), row],
        out_specs=[row, row, row, vec(D_MODEL), pl.BlockSpec((8, LANES), lambda i: (0, 0)),
                   pl.BlockSpec((D_MODEL, D_MODEL), lambda i: (0, 0))],
        out_shape=[act, act, act, jax.ShapeDtypeStruct((1, D_MODEL), F32), jax.ShapeDtypeStruct((8, LANES), F32),
                   jax.ShapeDtypeStruct((D_MODEL, D_MODEL), F32)],
        scratch_shapes=[pltpu.VMEM((D_MODEL, tm), MXU_DTYPE)],
        name="merge_loss",
    )(hc, ha, woc, woa, wo, proj, proj, b_merge, xp, final_g, tgt)


def _merge_bwd(dh2b, wo, woc, woa, yc, ya, proj, b_merge, o, hct, hat):
    s_n = dh2b.shape[0]
    tm = 512

    def body(dh_ref, wo_ref, woc_ref, woa_ref, yc_ref, ya_ref, gc_ref, ga_ref, b_ref, o_ref, za_ref, e_ref,
             hct_ref, hat_ref, dhc_ref, do_ref, dsum_ref, db3_ref, dbias_ref, dwoc_ref, dwoa_ref,
             dyc_ref, dya_ref):
        i = pl.program_id(0)

        @pl.when(i == 0)
        def _():
            dbias_ref[...] = jnp.zeros(dbias_ref.shape, F32)
            dwoc_ref[...] = jnp.zeros(dwoc_ref.shape, F32)
            dwoa_ref[...] = jnp.zeros(dwoa_ref.shape, F32)

        for rows in _row_chains(tm):
            dmg = _dot_nt(dh_ref[rows, :], wo_ref[...])
            gc, ga = _gates(gc_ref, ga_ref, b_ref, rows)
            dgc = dmg * yc_ref[rows, :].astype(F32) * gc * (1.0 - gc)
            dga = dmg * ya_ref[rows, :].astype(F32) * ga * (1.0 - ga)
            dbias_ref[:, :D_MODEL] += jnp.sum(dgc, axis=0, keepdims=True)
            dbias_ref[:, D_MODEL:] += jnp.sum(dga, axis=0, keepdims=True)
            dyc = dmg * gc
            dya = dmg * ga
            dyc_ref[rows, :] = dyc.astype(dyc_ref.dtype)
            dya_ref[rows, :] = dya.astype(dya_ref.dtype)
            dhc_ref[rows, :] = _dot_nt(dyc, woc_ref[...]).astype(dhc_ref.dtype)
            dha = _dot_nt(dya, woa_ref[...])
            z = za_ref[rows, :].astype(F32)
            sg = _sigmoid(z)
            ov = o_ref[rows, :].astype(F32)
            dout = dha * z * sg
            do_ref[rows, :] = dout.astype(do_ref.dtype)
            dsum_ref[rows, :] = _select_cols(dout * ov, e_ref[...], terms=2)
            db3_ref[0, rows, :] = (dha * ov * sg * (1.0 + z * (1.0 - sg))).astype(db3_ref.dtype)
            db3_ref[1, rows, :] = dgc.astype(db3_ref.dtype)
            db3_ref[2, rows, :] = dga.astype(db3_ref.dtype)
        dwoc_ref[...] += _dot(hct_ref[...], dyc_ref[...])
        dwoa_ref[...] += _dot(hat_ref[...], dya_ref[...])

    row = pl.BlockSpec((tm, D_MODEL), lambda i: (i, 0))
    col = pl.BlockSpec((D_MODEL, tm), lambda i: (0, i))
    wsp = pl.BlockSpec((D_MODEL, D_MODEL), lambda i: (0, 0), pipeline_mode=pl.Buffered(1))
    acc = pl.BlockSpec((D_MODEL, D_MODEL), lambda i: (0, 0))
    act = jax.ShapeDtypeStruct((s_n, D_MODEL), ACT_DTYPE)
    grad = jax.ShapeDtypeStruct((D_MODEL, D_MODEL), F32)
    return pl.pallas_call(
        body, grid=(s_n // tm,),
        in_specs=[row, wsp, wsp, wsp, row, row,
                  pl.BlockSpec((tm, D_MODEL), lambda i: (i, 8)), pl.BlockSpec((tm, D_MODEL), lambda i: (i, 9)),
                  pl.BlockSpec((1, 2 * D_MODEL), lambda i: (0, 0)), row,
                  pl.BlockSpec((tm, D_MODEL), lambda i: (i, 7)), pl.BlockSpec((D_MODEL, LANES), lambda i: (0, 0)),
                  col, col],
        out_specs=[row, row, pl.BlockSpec((tm, LANES), lambda i: (i, 0)),
                   pl.BlockSpec((3, tm, D_MODEL), lambda i: (0, i, 0)),
                   pl.BlockSpec((1, 2 * D_MODEL), lambda i: (0, 0)), acc, acc],
        out_shape=[act, act, jax.ShapeDtypeStruct((s_n, LANES), F32),
                   jax.ShapeDtypeStruct((3, s_n, D_MODEL), ACT_DTYPE),
                   jax.ShapeDtypeStruct((1, 2 * D_MODEL), F32), grad, grad],
        scratch_shapes=[pltpu.VMEM((tm, D_MODEL), MXU_DTYPE), pltpu.VMEM((tm, D_MODEL), MXU_DTYPE)],
        name="merge_bwd",
    )(dh2b, wo, woc, woa, yc, ya, proj, proj, b_merge, o, proj, _head_sum_matrix(), hct, hat)


def _conv_bwd(proj, conv_w, dhc):
    s_n = proj.shape[0]
    lr = s_n // N_RES
    pv = proj.reshape(N_RES, lr, IN_COLS)

    def body(xc_ref, bg_ref, cg_ref, zc_ref, w_ref, dhc_ref, da4_ref, dw_ref, dc_ref):
        w = w_ref[...]
        row = lax.broadcasted_iota(jnp.int32, (lr, LANES), 0)
        dw = [jnp.zeros((1, LANES), F32) for _ in range(3)]
        products = {}
        for r in range(N_RES):
            a, am1, am2 = _conv_terms(xc_ref, cg_ref, r, row, lr, products)
            c = w[0:1] * am2 + w[1:2] * am1 + w[2:3] * a
            z = zc_ref[r].astype(F32)
            sg = _sigmoid(z)
            sz = z * sg
            bg = bg_ref[r].astype(F32)
            dh = dhc_ref[r].astype(F32)
            da4_ref[1, r] = (dh * sz * c).astype(da4_ref.dtype)
            da4_ref[3, r] = (dh * bg * c * sg * (1.0 + z * (1.0 - sg))).astype(da4_ref.dtype)
            dc = dh * sz * bg
            dc_ref[r] = dc
            dw[0] = dw[0] + jnp.sum(dc * am2, axis=0, keepdims=True)
            dw[1] = dw[1] + jnp.sum(dc * am1, axis=0, keepdims=True)
            dw[2] = dw[2] + jnp.sum(dc * a, axis=0, keepdims=True)
        dw_ref[0:1, :] = dw[0]
        dw_ref[1:2, :] = dw[1]
        dw_ref[2:3, :] = dw[2]

        def shift_up(v):
            return jnp.where(row < lr - 1, pltpu.roll(v, lr - 1, 0), 0.0)

        for r in range(N_RES):
            dp1 = dc_ref[r + 1] if r + 1 < N_RES else shift_up(dc_ref[0])
            dp2 = dc_ref[r + 2] if r + 2 < N_RES else shift_up(dc_ref[r + 2 - N_RES])
            da = w[2:3] * dc_ref[r] + w[1:2] * dp1 + w[0:1] * dp2
            da4_ref[0, r] = (da * cg_ref[r].astype(F32)).astype(da4_ref.dtype)
            da4_ref[2, r] = (da * xc_ref[r].astype(F32)).astype(da4_ref.dtype)

    def col(part):
        return pl.BlockSpec((N_RES, lr, LANES), lambda j: (0, 0, part * 8 + j))

    da4, dw = pl.pallas_call(
        body, grid=(D_MODEL // LANES,),
        in_specs=[col(0), col(1), col(2), col(3), pl.BlockSpec((3, LANES), lambda j: (0, j)),
                  pl.BlockSpec((N_RES, lr, LANES), lambda j: (0, 0, j))],
        out_specs=[pl.BlockSpec((4, N_RES, lr, LANES), lambda j: (0, 0, 0, j)),
                   pl.BlockSpec((3, LANES), lambda j: (0, j))],
        out_shape=[jax.ShapeDtypeStruct((4, N_RES, lr, D_MODEL), ACT_DTYPE),
                   jax.ShapeDtypeStruct((3, D_MODEL), F32)],
        scratch_shapes=[pltpu.VMEM((N_RES, lr, LANES), F32)],
        name="conv_bwd",
    )(pv, pv, pv, pv, conv_w, dhc.reshape(N_RES, lr, D_MODEL))
    return da4.reshape(4, s_n, D_MODEL), dw


def _attn_bwd(proj, dout, lse, dsum, slopes, d, prev=None):
    g_n, rq = PATTERNS[d]
    un = _Units(g_n, rq)
    s_n = proj.shape[0]
    lr = s_n // N_RES
    q_n = g_n * rq
    d0, m0, d1, m1 = (np.ascontiguousarray(t.T) for t in _attn_tables(d))
    first, n_more, later = un.plan(lr, ATTN_BATCH)
    bsz = un.count(first)

    def body(sl_ref, q_ref, k_ref, v_ref, do_ref, lse_ref, ds_ref, d0_ref, m0_ref, d1_ref, m1_ref, *rest):
        prev_ref = rest[0] if prev is not None else None
        out_ref, b0_ref, b1_ref, lt_ref, dt_ref, dk_ref, dv_ref = rest[-7:]
        hp = pl.program_id(1)
        for h in (0, 1):
            slope = sl_ref[2 * hp + h]
            b0_ref[:, h * q_n:(h + 1) * q_n] = m0_ref[...] - slope * d0_ref[...]
            b1_ref[:, h * q_n:(h + 1) * q_n] = m1_ref[...] - slope * d1_ref[...]
        if prev is None:
            dk_ref[...] = jnp.zeros(dk_ref.shape, F32)
            dv_ref[...] = jnp.zeros(dv_ref.shape, F32)
        else:
            out_ref[0] = prev_ref[0]
            dk_ref[...] = prev_ref[1].astype(F32)
            dv_ref[...] = prev_ref[2].astype(F32)
        low = lax.broadcasted_iota(jnp.int32, (1, q_n, LANES), 2) < HEAD_DIM
        row16 = pl.multiple_of(16 * hp, 16)

        def query_rows(stat_ref, t_ref, qs):
            tiles = un.load_q(stat_ref, qs)
            for b in range(bsz):
                t_ref[b] = tiles[b].T
            t16 = t_ref[:, pl.ds(row16, 16), :]
            return jnp.concatenate([t16[:, 0:1, :], t16[:, 8:9, :]], axis=2)

        def batch(qs, at_start):
            qq = _stack_heads(un.load_q(q_ref, qs) * 0.125, low)
            dd = _stack_heads(un.load_q(do_ref, qs), low)
            ks = un.load_k(k_ref, qs, at_start)
            vs = un.load_k(v_ref, qs, at_start)
            lrow = query_rows(lse_ref, lt_ref, qs)
            drow = query_rows(ds_ref, dt_ref, qs)
            pt = jnp.exp(_bdot(ks, qq, _BNT) + _batch_bias(un, qs, at_start, b0_ref, b1_ref) - lrow)
            dst = pt * (_bdot(vs, dd, _BNT) - drow)
            un.add_k(dv_ref, qs, _bdot(pt, dd, _BNN), at_start)
            un.add_k(dk_ref, qs, _bdot(dst, qq, _BNN), at_start)
            dq = _bdot(jnp.swapaxes(dst, 1, 2), ks, _BNN)
            un.store_q(out_ref, qs, jnp.where(low, dq[:, :q_n], dq[:, q_n:]) * 0.125, add=prev is not None,
                       lead=(0,))

        batch(first, True)

        def more(j, carry):
            batch(later(j), False)
            return carry

        lax.fori_loop(1, 1 + n_more, more, 0)
        out_ref[1] = dk_ref[...].astype(out_ref.dtype)
        out_ref[2] = dv_ref[...].astype(out_ref.dtype)

    pv = _pattern_view(proj, g_n)
    full = lambda a: pl.BlockSpec(a.shape, lambda r, hp: (0, 0))
    whole = _pattern_spec(g_n, lr, lambda hp: hp, lead=(3,))
    out = pl.pallas_call(
        body, grid=_pattern_grid(g_n),
        in_specs=[pl.BlockSpec(memory_space=pltpu.SMEM),
                  _pattern_spec(g_n, lr, lambda hp: 32 + hp),
                  _pattern_spec(g_n, lr, lambda hp: 40 + hp),
                  _pattern_spec(g_n, lr, lambda hp: 48 + hp),
                  _pattern_spec(g_n, lr, lambda hp: hp),
                  _pattern_spec(g_n, lr, lambda hp: 0),
                  _pattern_spec(g_n, lr, lambda hp: 0),
                  full(d0), full(m0), full(d1), full(m1)] + ([] if prev is None else [whole]),
        out_specs=whole,
        out_shape=jax.ShapeDtypeStruct(_pattern_view_shape(s_n, D_MODEL, g_n, lead=(3,)), ACT_DTYPE),
        scratch_shapes=[pltpu.VMEM((d0.shape[0], 2 * q_n), F32), pltpu.VMEM((2 * q_n, 2 * q_n), F32),
                        pltpu.VMEM((bsz, LANES, q_n), F32), pltpu.VMEM((bsz, LANES, q_n), F32),
                        pltpu.VMEM(_pattern_block(g_n, lr), F32), pltpu.VMEM(_pattern_block(g_n, lr), F32)],
        name=f"attn_bwd_d{d}",
    )(slopes, pv, pv, pv, _pattern_view(dout, g_n), _pattern_view(lse, g_n), _pattern_view(dsum, g_n),
      d0, m0, d1, m1, *([] if prev is None else [_pattern_view(prev, g_n, lead=(3,))]))
    return out.reshape(3, s_n, D_MODEL)


def _part_index(step, per, lo, n):
    return jnp.clip(step // per - lo, 0, n - 1)


def _dw_in(ut, da4, dc3, db3):
    s_n = ut.shape[1]
    tn = 512
    per = D_MODEL // tn
    shard_blocks = SHARD_COLS // tn

    def body(a_ref, p0_ref, p1_ref, p2_ref, o_ref):
        part = pl.program_id(0) // per

        @pl.when(part < 4)
        def _():
            o_ref[...] = _dot(a_ref[...], p0_ref[...])

        @pl.when((part >= 4) & (part < 7))
        def _():
            o_ref[...] = _dot(a_ref[...], p1_ref[...])

        @pl.when(part >= 7)
        def _():
            o_ref[...] = _dot(a_ref[...], p2_ref[...])

    def pspec(lo, n):
        def index(j):
            part = j // per
            col = jnp.where(part < lo, 0, jnp.where(part >= lo + n, per - 1, j % per))
            return _part_index(j, per, lo, n), 0, col
        return pl.BlockSpec((None, s_n, tn), index)

    return pl.pallas_call(
        body, grid=(IN_COLS // tn,),
        in_specs=[pl.BlockSpec((D_MODEL, s_n), lambda j: (0, 0), pipeline_mode=pl.Buffered(1)),
                  pspec(0, 4), pspec(4, 3), pspec(7, 3)],
        out_specs=pl.BlockSpec((None, D_MODEL, tn), lambda j: (j // shard_blocks, 0, j % shard_blocks)),
        out_shape=jax.ShapeDtypeStruct((4, D_MODEL, SHARD_COLS), F32),
        name="dw_in",
    )(ut, da4, dc3, db3)


def _input_grad(da4, dc3, db3, w4, xp, norm_g, dh2, row0, rows):
    tm, tk = 256, 512
    per = D_MODEL // tk
    shard_blocks = SHARD_COLS // tk
    m0 = row0 // tm

    def body(p0_ref, p1_ref, p2_ref, w_ref, x_ref, g_ref, dh_ref, gx_ref, dg_ref):
        @pl.when(pl.program_id(0) == 0)
        def _():
            dg_ref[...] = jnp.zeros(dg_ref.shape, F32)

        du = None
        for k in range(IN_COLS // tk):
            part, cols = k // per, pl.ds((k % per) * tk, tk)
            ref, slot = (p0_ref, part) if part < 4 else (p1_ref, part - 4) if part < 7 else (p2_ref, part - 7)
            d = _dot_nt(ref[slot, :, cols], w_ref[k // shard_blocks, :, pl.ds((k % shard_blocks) * tk, tk)])
            du = d if du is None else du + d
        x = x_ref[...]
        r = lax.rsqrt(jnp.mean(x * x, axis=-1, keepdims=True) + EPS)
        nrm = x * r
        dg_ref[...] += jnp.sum(du * nrm, axis=0, keepdims=True)
        dn = du * g_ref[...]
        gx_ref[...] = dh_ref[...].astype(F32) + r * (dn - nrm * jnp.mean(dn * nrm, axis=-1, keepdims=True))

    def pspec(n):
        return pl.BlockSpec((n, tm, D_MODEL), lambda m: (0, m0 + m, 0))

    row_in = pl.BlockSpec((tm, D_MODEL), lambda m: (m0 + m, 0))
    vec = pl.BlockSpec((1, D_MODEL), lambda m: (0, 0))
    return pl.pallas_call(
        body, grid=(rows // tm,),
        in_specs=[pspec(4), pspec(3), pspec(3),
                  pl.BlockSpec(w4.shape, lambda m: (0, 0, 0), pipeline_mode=pl.Buffered(1)),
                  row_in, vec, row_in],
        out_specs=[pl.BlockSpec((tm, D_MODEL), lambda m: (m, 0)), vec],
        out_shape=[jax.ShapeDtypeStruct((rows, D_MODEL), F32), jax.ShapeDtypeStruct((1, D_MODEL), F32)],
        name="input_grad",
    )(da4, dc3, db3, w4, xp, norm_g, dh2)


class _Step:
    def __init__(self, x, tgt, norm_g, chip, after=None):
        self.norm_g, self.chip = norm_g, chip
        self.slopes = _alibi_slopes()
        self.xp, self.tp = _to_residue_major(x, tgt, after)
        self.u, self.ut = _rms_in(self.xp, norm_g)

    def project_own(self, w_own):
        self.proj_own = _in_proj(self.u, self.chip, w_own=w_own)

    def project(self, w4, others):
        self.proj_own = _in_proj(self.u, self.chip, w4=w4, partial=self.proj_own, others=others)

    def mixers(self, w4, taps):
        self.w4, self.taps, self.proj = w4, taps, self.proj_own
        self.hc, self.hct = _conv_fwd(self.proj, taps)
        fwd = [_attn_fwd(self.proj, self.slopes, d) for d in PATTERNS]
        self.o, self.lse, self.ha, self.hat = _attn_combine([f[0] for f in fwd], [f[1] for f in fwd], self.proj)

    def merge_and_loss(self, woc, woa, wo, b_merge, final_g):
        self.woc, self.woa, self.wo, self.b_merge = woc, woa, wo, b_merge
        (self.yc, self.ya, self.dh2b, self.d_final_g, self.loss8, self.d_wo) = _merge_loss(
            self.hc, self.ha, woc, woa, wo, self.proj, b_merge, self.xp, final_g, self.tp)

    def out_weight_grads(self):
        (self.dhc, self.dout, self.dsum, self.db3, self.d_bias, d_woc, d_woa) = _merge_bwd(
            self.dh2b, self.wo, self.woc, self.woa, self.yc, self.ya, self.proj, self.b_merge, self.o,
            self.hct, self.hat)
        return d_woc, d_woa, self.d_wo

    def conv_grads(self, after=0.0):
        self.da4, self.d_taps = _conv_bwd(self.proj, self.taps + after, self.dhc)

    def in_weight_grad(self, after=0.0):
        slopes = self.slopes + after
        self.dc3 = None
        for d in PATTERNS:
            self.dc3 = _attn_bwd(self.proj, self.dout, self.lse, self.dsum, slopes, d, prev=self.dc3)
        return _dw_in(self.ut, self.da4, self.dc3, self.db3)

    def input_grad(self, half, after=0.0):
        rows = self.xp.shape[0] // 2
        return _input_grad(self.da4, self.dc3, self.db3, self.w4, self.xp, self.norm_g + after, self.dh2b,
                           half * rows, rows)


def _local_grads(x, tgt, norm_g, w4, b_merge, conv_w, woc, woa, wo, final_g):
    st = _Step(x, tgt, norm_g, jnp.zeros((1,), jnp.int32))
    st.project_own(w4[0])
    st.project(w4, (2, 1))
    st.project(w4, (3,))
    st.mixers(w4, conv_w)
    st.merge_and_loss(woc, woa, wo, b_merge, final_g)
    d_woc, d_woa, d_wo = st.out_weight_grads()
    st.conv_grads()
    d_w4 = st.in_weight_grad()
    gx_lo, dg_lo = st.input_grad(0)
    gx_hi, dg_hi = st.input_grad(1)
    return (st.loss8, _to_natural(gx_lo, gx_hi), dg_lo + dg_hi, d_w4, st.d_bias, st.d_taps, d_woc, d_woa, d_wo,
            st.d_final_g)


MESH = pl.DeviceIdType.MESH
_CHIP_FLIPS = ((1, 0), (0, 1), (1, 1))
_ANY = pl.BlockSpec(memory_space=pl.ANY)


def _place():
    return lax.axis_index("x"), lax.axis_index("y"), lax.axis_index("c")


def _flip(v, f):
    return 1 - v if f else v


def _remote(src, dst, send_sems, recv_sems, k, device):
    return pltpu.make_async_remote_copy(src_ref=src, dst_ref=dst, send_sem=send_sems.at[k], recv_sem=recv_sems.at[k],
                                        device_id=device, device_id_type=MESH)


def _place_shard(w, chip, dtype):
    rows, cols = w.shape
    tm = min(rows, 128)

    def body(chip_ref, w_ref, o_ref):
        o_ref[0] = w_ref[...].astype(o_ref.dtype)

    return pl.pallas_call(
        body,
        grid_spec=pltpu.PrefetchScalarGridSpec(
            num_scalar_prefetch=1, grid=(rows // tm,),
            in_specs=[pl.BlockSpec((tm, cols), lambda i, chip_ref: (i, 0))],
            out_specs=pl.BlockSpec((1, tm, cols), lambda i, chip_ref: (chip_ref[0], i, 0))),
        out_shape=jax.ShapeDtypeStruct((4, rows, cols), dtype),
        name="place_shard",
    )(chip, w)


def _gather_copies_to(flips, whole=()):
    def copies(arrs, _, send_sems, recv_sems):
        x, y, c = _place()
        out = []
        for a, arr in enumerate(arrs):
            h = arr.shape[1] // 2
            mine = arr.at[2 * x + y] if a in whole else arr.at[2 * x + y, pl.ds(pl.multiple_of(c * h, 8), h)]
            for i, t in enumerate(flips):
                fx, fy = _CHIP_FLIPS[t]
                out.append(_remote(mine, mine, send_sems, recv_sems, len(flips) * a + i,
                                   (_flip(x, fx), _flip(y, fy), c)))
        return out
    return copies


def _forward_to_sibling(arrs, flips=(0, 1, 2)):
    n = len(arrs)

    def body(*refs):
        outs = refs[n:2 * n]
        send_sems, recv_sems = refs[2 * n:]
        x, y, c = _place()
        sibling = (x, y, 1 - c)
        started = []
        for a in range(n):
            h = outs[a].shape[1] // 2
            rows = pl.ds(pl.multiple_of(c * h, 8), h)
            for t in flips:
                fx, fy = _CHIP_FLIPS[t]
                landed = outs[a].at[2 * _flip(x, fx) + _flip(y, fy), rows]
                cp = _remote(landed, landed, send_sems, recv_sems, 3 * a + t, sibling)
                cp.start()
                started.append(cp)
        for a in range(n):
            h = outs[a].shape[1] // 2
            rows = pl.ds(pl.multiple_of((1 - c) * h, 8), h)
            for t in flips:
                fx, fy = _CHIP_FLIPS[t]
                handed = outs[a].at[2 * _flip(x, fx) + _flip(y, fy), rows]
                _remote(handed, handed, send_sems, recv_sems, 3 * a + t, sibling).wait_recv()
        for cp in started:
            cp.wait_send()

    return pl.pallas_call(
        body, in_specs=[_ANY] * n, out_specs=[_ANY] * n,
        out_shape=[jax.ShapeDtypeStruct(s.shape, s.dtype) for s in arrs],
        input_output_aliases={a: a for a in range(n)},
        scratch_shapes=[pltpu.SemaphoreType.DMA((3 * n,)), pltpu.SemaphoreType.DMA((3 * n,))],
        name="gathered_to_sibling_" + "".join(str(t) for t in flips),
    )(*arrs)


_HBM = pl.BlockSpec(memory_space=pltpu.HBM)
_SEM = pl.BlockSpec(memory_space=pltpu.SEMAPHORE)
_EFFECT = pltpu.SideEffectType.DATAFLOW_SIDE_EFFECTING


class _SplitExchange:
    def __init__(self, name, srcs, land_shapes, n_copies, copies, riders=()):
        self.name, self.n, self.nl, self.copies = name, len(srcs), len(land_shapes), copies
        n, nb = self.n, len(srcs) + len(land_shapes)
        lands = [lax.empty(s.shape, s.dtype) for s in land_shapes]
        bufs = [pltpu.with_memory_space_constraint(a, pltpu.HBM) for a in (*srcs, *lands, *riders)]
        na = len(bufs)

        def body(*refs):
            send_sems, recv_sems = refs[na], refs[na + 1]
            for cp in copies(refs[:n], refs[n:nb], send_sems, recv_sems):
                cp.start()
            refs[-1][...] = jnp.zeros(refs[-1].shape, F32)

        outs = pl.pallas_call(
            body, name=name + "_start",
            in_specs=[_HBM] * na,
            out_specs=[_SEM, _SEM] + [_HBM] * na + [pl.BlockSpec(memory_space=pltpu.VMEM)],
            out_shape=[pltpu.SemaphoreType.DMA((n_copies,)), pltpu.SemaphoreType.DMA((n_copies,))]
            + [pltpu.HBM(b.shape, b.dtype) for b in bufs] + [jax.ShapeDtypeStruct((8, LANES), F32)],
            input_output_aliases={i: 2 + i for i in range(na)},
            compiler_params=pltpu.CompilerParams(has_side_effects=_EFFECT),
        )(*bufs)
        self.sems, self.bufs, self.riders, self.token = outs[:2], outs[2:2 + nb], outs[2 + nb:2 + na], outs[-1]

    def after(self):
        return self.token[0, 0]

    def wait(self, done, riders=(), bufs=None):
        n, nb, copies = self.n, self.n + self.nl, self.copies
        bufs = [*(self.bufs if bufs is None else bufs),
                *[pltpu.with_memory_space_constraint(a, pltpu.HBM) for a in riders]]
        na = len(bufs)
        done = list(done) if isinstance(done, (list, tuple)) else [done]

        def body(*refs):
            send_sems, recv_sems = refs[na], refs[na + 1]
            for cp in copies(refs[:n], refs[n:nb], send_sems, recv_sems):
                cp.wait_send()
                cp.wait_recv()

        outs = pl.pallas_call(
            body, name=self.name + "_wait",
            in_specs=[_HBM] * na + [_SEM, _SEM] + [_ANY] * len(done),
            out_specs=[_HBM] * na,
            out_shape=[pltpu.HBM(b.shape, b.dtype) for b in bufs],
            input_output_aliases={i: i for i in range(na)},
            compiler_params=pltpu.CompilerParams(has_side_effects=_EFFECT),
        )(*bufs, *self.sems, *done)
        return outs[:n], outs[n:nb], outs[nb:]


def _sibling_copies(srcs, lands, send_sems, recv_sems):
    x, y, c = _place()
    out = []
    for a, (src, land) in enumerate(zip(srcs, lands)):
        h = src.shape[1] // 2
        theirs = pl.ds(pl.multiple_of((1 - c) * h, 8), h)
        out.append(_remote(src.at[:, theirs], land, send_sems, recv_sems, a, (x, y, 1 - c)))
    return out


def _grads_to_sibling(name, grads):
    shapes = [jax.ShapeDtypeStruct((4, g.shape[1] // 2, g.shape[2]), g.dtype) for g in grads]
    return _SplitExchange(name, grads, shapes, len(grads), _sibling_copies)


def _chip_copies(srcs, lands, send_sems, recv_sems):
    x, y, c = _place()
    out = []
    for a, (src, land) in enumerate(zip(srcs, lands)):
        for t, (fx, fy) in enumerate(_CHIP_FLIPS):
            tx, ty = _flip(x, fx), _flip(y, fy)
            out.append(_remote(src.at[2 * tx + ty], land.at[t], send_sems, recv_sems, 3 * a + t, (tx, ty, c)))
    return out


def _grads_to_chips(name, parts):
    shapes = [jax.ShapeDtypeStruct((3, *p.shape[1:]), p.dtype) for p in parts]
    return _SplitExchange(name, parts, shapes, 3 * len(parts), _chip_copies)


def _add_halves(g, r, half):
    _, rows, cols = g.shape
    h = rows // 2
    tm = min(h, 128)
    nt = h // tm

    def body(half_ref, g_ref, r_ref, b_ref):
        b_ref[...] = (g_ref[...] + r_ref[...]).astype(b_ref.dtype)

    spec = pl.BlockSpec((1, tm, cols), lambda j, i, half_ref: (j, i, 0))
    return pl.pallas_call(
        body,
        grid_spec=pltpu.PrefetchScalarGridSpec(
            num_scalar_prefetch=1, grid=(4, nt),
            in_specs=[pl.BlockSpec((1, tm, cols), lambda j, i, half_ref: (j, half_ref[0] * nt + i, 0)), spec],
            out_specs=spec),
        out_shape=jax.ShapeDtypeStruct((4, h, cols), BF16),
        name="add_sibling_grads",
    )(half, g, r)


def _add_chips(g, r, recv, where):
    _, h, cols = r.shape
    tm = min(h, 128)
    nt = h // tm

    def body(where_ref, g_ref, r_ref, recv_ref, out_ref):
        own = g_ref[0] + r_ref[0]
        out_ref[...] = ((own + recv_ref[0].astype(F32)) + recv_ref[1].astype(F32)) + recv_ref[2].astype(F32)

    return pl.pallas_call(
        body,
        grid_spec=pltpu.PrefetchScalarGridSpec(
            num_scalar_prefetch=1, grid=(nt,),
            in_specs=[pl.BlockSpec((1, tm, cols), lambda i, w: (w[0], w[1] * nt + i, 0)),
                      pl.BlockSpec((1, tm, cols), lambda i, w: (w[0], i, 0)),
                      pl.BlockSpec((3, tm, cols), lambda i, w: (0, i, 0))],
            out_specs=pl.BlockSpec((tm, cols), lambda i, w: (w[1] * nt + i, 0))),
        out_shape=jax.ShapeDtypeStruct((2 * h, cols), F32),
        name="add_chip_grads",
    )(where, g, r, recv)


def _half_copies(srcs, _, send_sems, recv_sems):
    x, y, c = _place()
    out = []
    for a, src in enumerate(srcs):
        h = src.shape[0] // 2
        mine = src.at[pl.ds(pl.multiple_of(c * h, 8), h)]
        out.append(_remote(mine, mine, send_sems, recv_sems, a, (x, y, 1 - c)))
    return out


def _share_halves(name, shards):
    return _SplitExchange(name, shards, [], len(shards), _half_copies)


def _reduce_small(rows):
    cols = rows[0].shape[1]
    n = len(rows)
    assert sum(r.shape[0] for r in rows) <= 8

    def body(*refs):
        ins, out_ref = refs[:n], refs[n]
        vec_ref, gath_ref, send_sems, recv_sems = refs[n + 1:]
        x, y, c = _place()
        me = 4 * x + 2 * y + c
        vec_ref[...] = jnp.zeros(vec_ref.shape, F32)
        at = 0
        for r in ins:
            vec_ref[at:at + r.shape[0], :] = r[...]
            at += r.shape[0]
        copies = []
        for k in range(1, 8):
            peer = (_flip(x, (k >> 2) & 1), _flip(y, (k >> 1) & 1), _flip(c, k & 1))
            copies.append(_remote(vec_ref, gath_ref.at[me], send_sems, recv_sems, k - 1, peer))
        for cp in copies:
            cp.start()
        gath_ref[me] = vec_ref[...]
        for cp in copies:
            cp.wait()
        tot = gath_ref[0]
        for dev in range(1, 8):
            tot = tot + gath_ref[dev]
        out_ref[...] = tot
        out_ref[7:8, :] = jnp.zeros((1, cols), F32) + jnp.sum(tot[7:8, :])

    vm = pl.BlockSpec(memory_space=pltpu.VMEM)
    return pl.pallas_call(
        body, in_specs=[vm] * n, out_specs=vm,
        out_shape=jax.ShapeDtypeStruct((8, cols), F32),
        scratch_shapes=[pltpu.VMEM((8, cols), F32), pltpu.VMEM((8, 8, cols), F32),
                        pltpu.SemaphoreType.DMA((7,)), pltpu.SemaphoreType.DMA((7,))],
        name="reduce_small",
    )(*rows)


def _adamw_tile(w_ref, g_ref, m_ref, v_ref, d_ref, m2_ref, v2_ref, gout_ref):
    gr = g_ref[...]
    m2 = ADAM_B1 * m_ref[...] + (1.0 - ADAM_B1) * gr
    v2 = ADAM_B2 * v_ref[...] + (1.0 - ADAM_B2) * (gr * gr)
    m_hat = m2 / (1.0 - ADAM_B1 ** ADAM_STEP)
    v_hat = v2 / (1.0 - ADAM_B2 ** ADAM_STEP)
    d_ref[...] = -ADAM_LR * (m_hat / (jnp.sqrt(v_hat) + ADAM_EPS) + ADAM_WD * w_ref[...])
    m2_ref[...] = m2
    v2_ref[...] = v2
    gout_ref[...] = gr


def _adamw(w, g, m, v, name):
    rows, cols = w.shape
    tm = 128 if rows % 128 == 0 else rows

    def body(*refs):
        _adamw_tile(*refs)

    spec = pl.BlockSpec((tm, cols), lambda i: (i, 0))
    sds = jax.ShapeDtypeStruct((rows, cols), F32)
    return pl.pallas_call(body, grid=(rows // tm,), in_specs=[spec] * 4, out_specs=[spec] * 4,
                          out_shape=[sds] * 4, name=name)(w, g, m, v)


def _adamw_half(w, g, m, v, half, name, other=()):
    rows, cols = w.shape
    tm = 128
    nt = rows // 2 // tm

    def body(half_ref, *refs):
        _adamw_tile(*refs[:4], *refs[4 + len(other):])

    spec = pl.BlockSpec((tm, cols), lambda i, h: (h[0] * nt + i, 0))
    sds = jax.ShapeDtypeStruct((rows, cols), F32)
    return pl.pallas_call(
        body,
        grid_spec=pltpu.PrefetchScalarGridSpec(
            num_scalar_prefetch=1, grid=(nt,), in_specs=[spec] * 4 + [_ANY] * len(other), out_specs=[spec] * 4),
        out_shape=[sds] * 4, input_output_aliases={5 + k: k for k in range(len(other))}, name=name,
    )(half, w, g, m, v, *other)


def kernel(x, norm_g, w_in, b_merge, conv_w, w_out_conv, w_out_attn, w_o, final_g, loss_target, m_norm_g, m_w_in, m_b_merge, m_conv_w, m_w_out_conv, m_w_out_attn, m_w_o, m_final_g, v_norm_g, v_w_in, v_b_merge, v_conv_w, v_w_out_conv, v_w_out_attn, v_w_o, v_final_g):
    mx, my, mc = _place()
    chip = (2 * mx + my).astype(jnp.int32)
    seq = x.shape[1]

    chip1 = chip.reshape(1)
    slots = [_place_shard(w[0], chip1, MXU_DTYPE) for w in (w_in, w_out_conv, w_out_attn, w_o)]
    taps_slot = _place_shard(jnp.pad(conv_w[0], ((0, 5), (0, 0))), chip1, F32)
    gather_near = _SplitExchange("gather_w_in_near", [slots[0], taps_slot], [], 4,
                                 _gather_copies_to((0, 1), whole=(1,)))
    st = _Step(x[0], loss_target[0], norm_g, chip1, after=gather_near.token)
    st.project_own(w_in[0])
    near, _, _ = gather_near.wait([st.ut, st.proj_own])
    gather_far = _SplitExchange("gather_w_in_far", near, [], 2, _gather_copies_to((2,), whole=(1,)))
    (w4,) = _forward_to_sibling(gather_far.bufs[:1], flips=(0, 1))
    st.project(w4, (2, 1))
    (w4, taps4), _, out_slots = gather_far.wait([st.proj_own], riders=slots[1:], bufs=[w4, gather_far.bufs[1]])
    gather_out = _SplitExchange("gather_w_out", out_slots, [], 9, _gather_copies_to((0, 1, 2)), riders=[w4])
    (w4,) = _forward_to_sibling(gather_out.riders, flips=(2,))
    st.project(w4, (3,))
    st.mixers(w4, jnp.concatenate([taps4[j, :3, :] for j in range(4)], axis=1))
    out_ws, _, _ = gather_out.wait(st.o)
    woc, woa, wo = [w.reshape(D_MODEL, D_MODEL) for w in _forward_to_sibling(out_ws)]
    st.merge_and_loss(woc, woa, wo, b_merge, final_g.reshape(1, D_MODEL))

    half = mc.astype(jnp.int32).reshape(1)
    where = jnp.stack([chip, mc.astype(jnp.int32)])
    out_grads = [g.reshape(4, -1, D_MODEL) for g in st.out_weight_grads()]
    to_sibling = _grads_to_sibling("out_grads_to_sibling", out_grads)
    st.conv_grads(after=to_sibling.after())
    out_grads, out_from_sibling, _ = to_sibling.wait(st.da4)
    to_chips = _grads_to_chips("out_grads_to_chips",
                               [_add_halves(g, r, half) for g, r in zip(out_grads, out_from_sibling)])
    d_w4 = st.in_weight_grad(after=to_chips.after())
    out_from_chips = to_chips.wait(st.dc3)[1]

    to_sibling = _grads_to_sibling("in_grad_to_sibling", [d_w4])
    gx_lo, dg_lo = st.input_grad(0, after=to_sibling.after())
    (d_w4,), (from_sibling,), _ = to_sibling.wait(gx_lo)
    to_chips = _grads_to_chips("in_grad_to_chips", [_add_halves(d_w4, from_sibling, half)])

    where_late = where + to_chips.after().astype(jnp.int32)
    out_reduced = [_add_chips(g, r, recv, where_late)
                   for g, r, recv in zip(out_grads, out_from_sibling, out_from_chips)]
    share_out = _share_halves("share_out_grads", out_reduced)
    gx_hi, dg_hi = st.input_grad(1, after=share_out.after())
    grad_x = _to_natural(gx_lo, gx_hi)
    g_woc, g_woa, g_wo = share_out.wait(gx_hi)[0]

    small = _reduce_small([dg_lo + dg_hi, st.d_bias.reshape(2, D_MODEL), st.d_taps, st.d_final_g,
                           st.loss8.reshape(1, D_MODEL)])
    loss = (0.5 / D_MODEL) * small[7, 0]
    g_taps = lax.dynamic_slice(small[3:6], (0, chip * (D_MODEL // 4)), (3, D_MODEL // 4))
    upd = {
        "norm_g": _adamw(norm_g, small[0:1], m_norm_g, v_norm_g, "adamw_norm_g"),
        "b_merge": _adamw(b_merge, small[1:3].reshape(1, 2 * D_MODEL), m_b_merge, v_b_merge, "adamw_b_merge"),
        "conv_w": _adamw(conv_w[0], g_taps, m_conv_w[0], v_conv_w[0], "adamw_conv_w"),
        "w_out_conv": _adamw(w_out_conv[0], g_woc, m_w_out_conv[0], v_w_out_conv[0], "adamw_w_out_conv"),
        "w_out_attn": _adamw(w_out_attn[0], g_woa, m_w_out_attn[0], v_w_out_attn[0], "adamw_w_out_attn"),
        "w_o": _adamw(w_o[0], g_wo, m_w_o[0], v_w_o[0], "adamw_w_o"),
        "final_g": _adamw(final_g.reshape(1, D_MODEL), small[6:7], m_final_g.reshape(1, D_MODEL),
                          v_final_g.reshape(1, D_MODEL), "adamw_final_g"),
    }
    behind = [grad_x] + [u[0] for u in upd.values()]
    in_reduced = _add_chips(d_w4, from_sibling, to_chips.wait(behind)[1][0], where)

    share_in = _share_halves("share_in_grad", [in_reduced])
    w_in_args = (w_in[0], m_w_in[0], v_w_in[0])
    own_rows = _adamw_half(w_in_args[0], share_in.bufs[0], *w_in_args[1:], half, "adamw_w_in_own_rows")
    (g_w_in,) = share_in.wait(own_rows[0])[0]
    upd["w_in"] = _adamw_half(w_in_args[0], g_w_in, *w_in_args[1:], 1 - half, "adamw_w_in_sibling_rows",
                              other=own_rows)

    names = ["norm_g", "w_in", "b_merge", "conv_w", "w_out_conv", "w_out_attn", "w_o", "final_g"]
    shapes = [norm_g.shape, w_in.shape, b_merge.shape, conv_w.shape, w_out_conv.shape, w_out_attn.shape,
              w_o.shape, final_g.shape]
    outs = [loss, grad_x.reshape(1, seq, D_MODEL)]
    for k in (3, 0, 1, 2):
        outs += [upd[n][k].reshape(s) for n, s in zip(names, shapes)]
    return tuple(outs)
```

```python
import functools

import numpy as np
import jax
import jax.numpy as jnp
from jax import lax
from jax.experimental import pallas as pl
from jax.experimental.pallas import tpu as pltpu

F32 = jnp.float32
BF16 = jnp.bfloat16
MXU_DTYPE = jnp.bfloat16
ACT_DTYPE = jnp.bfloat16

D_MODEL = 1024
N_HEADS = 16
HEAD_DIM = 64
QB = 128
N_RES = 16
LANES = 128
HP = N_HEADS * HEAD_DIM // LANES
IN_COLS = 10 * D_MODEL
SHARD_COLS = IN_COLS // 4
EPS = 1e-6
NEG = -1e30

ADAM_LR, ADAM_B1, ADAM_B2, ADAM_EPS, ADAM_WD, ADAM_STEP = 0.001, 0.9, 0.999, 1e-08, 0.01, 10

PATTERNS = {1: (16, 8), 4: (4, 32), 16: (1, 128)}

_NN = (((1,), (0,)), ((), ()))
_NT = (((1,), (1,)), ((), ()))


def _dot(a, b):
    return lax.dot_general(a.astype(MXU_DTYPE), b.astype(MXU_DTYPE), _NN, preferred_element_type=F32)


def _dot_nt(a, b):
    return lax.dot_general(a.astype(MXU_DTYPE), b.astype(MXU_DTYPE), _NT, preferred_element_type=F32)


def _split3(x):
    hi = x.astype(BF16)
    r1 = x - hi.astype(F32)
    mid = r1.astype(BF16)
    lo = (r1 - mid.astype(F32)).astype(BF16)
    return hi, mid, lo


def _select_cols(x, sel, terms):
    return sum(lax.dot_general(t, sel, _NN, preferred_element_type=F32) for t in _split3(x)[:terms])


def _sigmoid(z):
    return 1.0 / (1.0 + jnp.exp(-z))


def _head_expand_matrix():
    e = np.zeros((LANES, D_MODEL), np.float32)
    for h in range(N_HEADS):
        e[8 * h, HEAD_DIM * h:HEAD_DIM * (h + 1)] = 1.0
    return jnp.asarray(e, BF16)


def _head_sum_matrix():
    e = np.zeros((D_MODEL, LANES), np.float32)
    for h in range(N_HEADS):
        e[HEAD_DIM * h:HEAD_DIM * (h + 1), 8 * h:8 * (h + 1)] = 1.0
    return jnp.asarray(e, BF16)


def _attn_tables(d):
    g_n, rq = PATTERNS[d]
    q_n = g_n * rq
    gq, iq = np.arange(q_n) // rq, np.arange(q_n) % rq

    def tab(kn, base):
        k_n = g_n * kn
        gk, jk = np.arange(k_n) // kn, np.arange(k_n) % kn
        delta = g_n * (base + iq[:, None] - jk[None, :]) + gq[:, None] - gk[None, :]
        valid = (delta >= 0) & (delta <= QB)
        dist = np.where(valid, d * delta, 0).astype(np.float32)
        madd = np.where(valid, 0.0, NEG).astype(np.float32)
        return dist, madd

    d0, m0 = tab(rq if g_n == 1 else 2 * rq, 0)
    d1, m1 = tab(2 * rq, rq)
    return d0, m0, d1, m1


def _alibi_slopes():
    return jnp.exp2(-8.0 * jnp.arange(1, N_HEADS + 1, dtype=F32) / N_HEADS)


def _to_residue_major(x, tgt, after=None):
    s_n, c_n = x.shape
    lr = s_n // N_RES
    extra = [] if after is None else [after]

    def body(x_ref, t_ref, *rest):
        xo_ref, to_ref = rest[-2:]
        for r in range(N_RES):
            xo_ref[r] = x_ref[pl.ds(r, lr, stride=N_RES), :]
            to_ref[r] = t_ref[pl.ds(r, lr, stride=N_RES), :]

    nat = pl.BlockSpec((s_n, LANES), lambda j: (0, j))
    res = pl.BlockSpec((N_RES, lr, LANES), lambda j: (0, 0, j))
    xo, to = pl.pallas_call(
        body, grid=(c_n // LANES,),
        in_specs=[nat, nat] + [pl.BlockSpec((8, LANES), lambda j: (0, 0))] * len(extra),
        out_specs=[res, res],
        out_shape=[jax.ShapeDtypeStruct((N_RES, lr, c_n), F32)] * 2,
        name="perm_in",
    )(x, tgt, *extra)
    return xo.reshape(s_n, c_n), to.reshape(s_n, c_n)


def _to_natural(gx_lo, gx_hi):
    half_rows, c_n = gx_lo.shape
    lr = half_rows // (N_RES // 2)

    def body(lo_ref, hi_ref, o_ref):
        for r in range(N_RES):
            o_ref[pl.ds(r, lr, stride=N_RES), :] = lo_ref[r] if r < N_RES // 2 else hi_ref[r - N_RES // 2]

    half = pl.BlockSpec((N_RES // 2, lr, LANES), lambda j: (0, 0, j))
    return pl.pallas_call(
        body, grid=(c_n // LANES,),
        in_specs=[half, half],
        out_specs=pl.BlockSpec((2 * half_rows, LANES), lambda j: (0, j)),
        out_shape=jax.ShapeDtypeStruct((2 * half_rows, c_n), F32),
        name="perm_out",
    )(gx_lo.reshape(N_RES // 2, lr, c_n), gx_hi.reshape(N_RES // 2, lr, c_n))


def _rms_in(xp, norm_g):
    s_n, c_n = xp.shape
    tm = 512

    def body(x_ref, g_ref, u_ref, ut_ref):
        x = x_ref[...]
        r = lax.rsqrt(jnp.mean(x * x, axis=-1, keepdims=True) + EPS)
        u = x * r * g_ref[...]
        u_ref[...] = u.astype(u_ref.dtype)
        ut_ref[...] = u.T.astype(ut_ref.dtype)

    return pl.pallas_call(
        body, grid=(s_n // tm,),
        in_specs=[pl.BlockSpec((tm, c_n), lambda i: (i, 0)), pl.BlockSpec((1, c_n), lambda i: (0, 0))],
        out_specs=[pl.BlockSpec((tm, c_n), lambda i: (i, 0)), pl.BlockSpec((c_n, tm), lambda i: (0, i))],
        out_shape=[jax.ShapeDtypeStruct((s_n, c_n), ACT_DTYPE), jax.ShapeDtypeStruct((c_n, s_n), ACT_DTYPE)],
        name="rms_in",
    )(xp, norm_g)


def _in_proj(u, chip, w_own=None, w4=None, partial=None, others=()):
    s_n = u.shape[0]
    tn, cm = 512, 512
    per = SHARD_COLS // tn
    own = partial is None

    def body(chip_ref, a_ref, b_ref, *rest):
        o_ref = rest[-1]
        b = b_ref[...]
        for c in range(s_n // cm):
            o_ref[c * cm:(c + 1) * cm, :] = _dot(a_ref[c * cm:(c + 1) * cm, :], b).astype(o_ref.dtype)

    def shard(n, chip_ref):
        if own:
            return chip_ref[0]
        mask = others[-1]
        for i, m in enumerate(others[:-1]):
            mask = jnp.where(n // per == i, m, mask)
        return jnp.bitwise_xor(chip_ref[0], mask)

    w_spec = (pl.BlockSpec((D_MODEL, tn), lambda n, c: (0, n)) if own else
              pl.BlockSpec((None, D_MODEL, tn), lambda n, c: (shard(n, c), 0, n % per)))
    return pl.pallas_call(
        body,
        grid_spec=pltpu.PrefetchScalarGridSpec(
            num_scalar_prefetch=1, grid=(per if own else len(others) * per,),
            in_specs=[pl.BlockSpec((s_n, D_MODEL), lambda n, c: (0, 0)), w_spec] + ([] if own else [_ANY]),
            out_specs=pl.BlockSpec((s_n, tn), lambda n, c: (0, shard(n, c) * per + n % per))),
        out_shape=jax.ShapeDtypeStruct((s_n, IN_COLS), ACT_DTYPE),
        input_output_aliases={} if own else {3: 0},
        name="in_proj_own" if own else "in_proj_" + "_".join(str(m) for m in others),
    )(*([chip, u, w_own] if own else [chip, u, w4, partial]))


def _conv_terms(xc_ref, cg_ref, r, row, lr, cache):
    def a_of(q):
        if q not in cache:
            cache[q] = cg_ref[q].astype(F32) * xc_ref[q].astype(F32)
        return cache[q]

    def shift_down(v):
        return jnp.where(row >= 1, pltpu.roll(v, 1, 0), 0.0)

    a = a_of(r)
    am1 = a_of(r - 1) if r >= 1 else shift_down(a_of(N_RES - 1))
    am2 = a_of(r - 2) if r >= 2 else shift_down(a_of(N_RES - 2 + r))
    return a, am1, am2


def _conv_fwd(proj, conv_w):
    s_n = proj.shape[0]
    lr = s_n // N_RES
    pv = proj.reshape(N_RES, lr, IN_COLS)

    def body(xc_ref, bg_ref, cg_ref, zc_ref, w_ref, hc_ref, hct_ref):
        w = w_ref[...]
        row = lax.broadcasted_iota(jnp.int32, (lr, LANES), 0)
        products = {}
        for r in range(N_RES):
            a, am1, am2 = _conv_terms(xc_ref, cg_ref, r, row, lr, products)
            c = w[0:1] * am2 + w[1:2] * am1 + w[2:3] * a
            z = zc_ref[r].astype(F32)
            hc = z * _sigmoid(z) * bg_ref[r].astype(F32) * c
            hc_ref[r] = hc.astype(hc_ref.dtype)
            hct_ref[:, r * lr:(r + 1) * lr] = hc.T.astype(hct_ref.dtype)

    def col(part):
        return pl.BlockSpec((N_RES, lr, LANES), lambda j: (0, 0, part * 8 + j))

    hc, hct = pl.pallas_call(
        body, grid=(D_MODEL // LANES,),
        in_specs=[col(0), col(1), col(2), col(3), pl.BlockSpec((3, LANES), lambda j: (0, j))],
        out_specs=[pl.BlockSpec((N_RES, lr, LANES), lambda j: (0, 0, j)),
                   pl.BlockSpec((LANES, s_n), lambda j: (j, 0))],
        out_shape=[jax.ShapeDtypeStruct((N_RES, lr, D_MODEL), ACT_DTYPE),
                   jax.ShapeDtypeStruct((D_MODEL, s_n), ACT_DTYPE)],
        name="conv_fwd",
    )(pv, pv, pv, pv, conv_w)
    return hc.reshape(s_n, D_MODEL), hct


RES_PER_STEP = 8
CLASSES_PER_STEP = 2
ATTN_BATCH = 16

_BNT = (((2,), (2,)), ((0,), (0,)))
_BNN = (((2,), (1,)), ((0,), (0,)))


def _bdot(a, b, dims):
    return lax.dot_general(a.astype(MXU_DTYPE), b.astype(MXU_DTYPE), dims, preferred_element_type=F32)


def _pattern_view_shape(s_n, c_n, g_n, lead=()):
    lr = s_n // N_RES
    return (*lead, 4, 4, lr, c_n) if g_n == 4 else (*lead, N_RES, lr, c_n)


def _pattern_view(a, g_n, lead=()):
    return a.reshape(_pattern_view_shape(a.shape[-2], a.shape[-1], g_n, lead))


def _pattern_block(g_n, lr):
    if g_n == 4:
        return (4, CLASSES_PER_STEP, lr, LANES)
    return (16 if g_n == 16 else RES_PER_STEP, lr, LANES)


def _pattern_grid(g_n):
    return ({1: N_RES // RES_PER_STEP, 4: 4 // CLASSES_PER_STEP, 16: 1}[g_n], HP)


def _pattern_spec(g_n, lr, col_of_hp, lead=()):
    z = (0,) * len(lead)
    block = (*lead, *_pattern_block(g_n, lr))
    if g_n == 16:
        return pl.BlockSpec(block, lambda r, hp: (*z, 0, 0, col_of_hp(hp)))
    if g_n == 4:
        return pl.BlockSpec(block, lambda r, hp: (*z, 0, r, 0, col_of_hp(hp)))
    return pl.BlockSpec(block, lambda r, hp: (*z, r, 0, col_of_hp(hp)))


def _aligned(start, m):
    return start if isinstance(start, int) else pl.multiple_of(start, m)


class _Units:
    def __init__(self, g_n, rq):
        self.g_n, self.rq = g_n, rq
        self.per_res, self.paired = g_n == 1, rq == 8

    def plan(self, lr, size):
        if self.per_res:
            return [0], lr // self.rq - 1, lambda j: [pl.multiple_of(j * self.rq, self.rq)]
        if self.paired:
            per = min(size // 2, lr // 16)
            assert (lr // 16) % per == 0
            return ([i * 16 for i in range(per)], lr // 16 // per - 1,
                    lambda j: [pl.multiple_of((j * per + i) * 16, 16) for i in range(per)])
        step, classes = self.rq, range(CLASSES_PER_STEP)
        per = min(size // CLASSES_PER_STEP, lr // step)
        assert (lr // step) % per == 0
        return ([(c, i * step) for c in classes for i in range(per)], lr // step // per - 1,
                lambda j: [(c, pl.multiple_of((j * per + i) * step, step)) for c in classes for i in range(per)])

    def count(self, qs):
        return RES_PER_STEP if self.per_res else len(qs) * (2 if self.paired else 1)

    def _split(self, tiles, lo, rows):
        return tiles[:, lo:lo + rows].reshape(self.g_n * rows, LANES)

    def load_q(self, ref, qs):
        rq = self.rq
        if self.per_res:
            return ref[:, pl.ds(qs[0], rq), :]
        if self.paired:
            tiles = [ref[:, pl.ds(q, 16), :].astype(F32) for q in qs]
            return jnp.stack([self._split(t, lo, 8) for t in tiles for lo in (0, 8)])
        return jnp.stack([ref[:, c, pl.ds(q, rq), :].reshape(self.g_n * rq, LANES) for c, q in qs])

    def _key_rows(self, q, at_start):
        return (0, 2 * self.rq) if at_start and q == 0 else (_aligned(q - self.rq, self.rq), 2 * self.rq)

    def load_k(self, ref, qs, first):
        rq = self.rq
        if self.per_res:
            return ref[:, pl.ds(0, rq), :] if first else ref[:, pl.ds(_aligned(qs[0] - rq, rq), 2 * rq), :]
        if self.paired:
            out = []
            for i, q in enumerate(qs):
                if first and i == 0:
                    t = ref[:, 0:16, :].astype(F32)
                    out += [self._split(t, 0, 16)] * 2
                else:
                    t = ref[:, pl.ds(_aligned(q - 16, 16), 32), :].astype(F32)
                    out += [self._split(t, 8, 16), self._split(t, 16, 16)]
            return jnp.stack(out)
        rows = [(c, *self._key_rows(q, first)) for c, q in qs]
        return jnp.stack([ref[:, c, pl.ds(k0, n), :].reshape(self.g_n * n, LANES) for c, k0, n in rows])

    def store_q(self, ref, qs, val, add=False, lead=()):
        if self.per_res:
            pieces = [((), qs[0], self.rq, val)]
        elif self.paired:
            pieces = [((), q, 16, jnp.concatenate([val[2 * i].reshape(self.g_n, 8, LANES),
                                                   val[2 * i + 1].reshape(self.g_n, 8, LANES)], axis=1))
                      for i, q in enumerate(qs)]
        else:
            pieces = [((c,), q, self.rq, val[i].reshape(self.g_n, self.rq, LANES)) for i, (c, q) in enumerate(qs)]
        for cls, start, rows, v in pieces:
            idx = (*lead, slice(None), *cls, pl.ds(start, rows), slice(None))
            ref[idx] = (ref[idx] + v if add else v).astype(ref.dtype)

    def add_k(self, ref, qs, val, first):
        rq = self.rq
        if self.per_res:
            k0, n = (0, rq) if first else (_aligned(qs[0] - rq, rq), 2 * rq)
            ref[:, pl.ds(k0, n), :] += val
            return
        if self.paired:
            starts = [s for i, q in enumerate(qs)
                      for s in ((0, 0) if first and i == 0 else (_aligned(q - 8, 8), q))]
            rows = [((), s, 16) for s in starts]
        else:
            rows = [((c,), *self._key_rows(q, first)) for c, q in qs]
        for b, (cls, k0, n) in enumerate(rows):
            idx = (slice(None), *cls, pl.ds(k0, n), slice(None))
            ref[idx] += val[b].reshape(self.g_n, n, LANES)


def _batch_bias(un, qs, at_start, first_ref, general_ref):
    if not at_start:
        return general_ref[...][None]
    if un.per_res:
        return first_ref[...][None]
    if un.paired:
        return jnp.concatenate([first_ref[...][None]] + [general_ref[...][None]] * (un.count(qs) - 1), axis=0)
    return jnp.stack([(first_ref if q == 0 else general_ref)[...] for _, q in qs])


def _stack_heads(x, low):
    zero = jnp.zeros_like(x)
    return jnp.concatenate([jnp.where(low, x, zero), jnp.where(low, zero, x)], axis=1)


def _attn_fwd(proj, slopes, d):
    g_n, rq = PATTERNS[d]
    un = _Units(g_n, rq)
    s_n = proj.shape[0]
    lr = s_n // N_RES
    q_n = g_n * rq
    d0, m0, d1, m1 = _attn_tables(d)
    first, n_more, later = un.plan(lr, ATTN_BATCH)

    def body(sl_ref, q_ref, k_ref, v_ref, d0_ref, m0_ref, d1_ref, m1_ref, o_ref, lse_ref, b0_ref, b1_ref):
        hp = pl.program_id(1)

        @pl.when(hp == 0)
        def _():
            lse_ref[...] = jnp.zeros(lse_ref.shape, F32)

        for h in (0, 1):
            slope = sl_ref[2 * hp + h]
            b0_ref[h * q_n:(h + 1) * q_n, :] = m0_ref[...] - slope * d0_ref[...]
            b1_ref[h * q_n:(h + 1) * q_n, :] = m1_ref[...] - slope * d1_ref[...]

        lane = lax.broadcasted_iota(jnp.int32, (1, q_n, LANES), 2)
        low = lane < HEAD_DIM
        grp = lane // 8

        def batch(qs, at_start):
            qq = _stack_heads(un.load_q(q_ref, qs) * 0.125, low)
            s = _bdot(qq, un.load_k(k_ref, qs, at_start), _BNT) + _batch_bias(un, qs, at_start, b0_ref, b1_ref)
            m = jnp.max(s, axis=2, keepdims=True)
            p = jnp.exp(s - m)
            l = jnp.sum(p, axis=2, keepdims=True)
            o = _bdot(p, un.load_k(v_ref, qs, at_start), _BNN) * (1.0 / l)
            lse = m + jnp.log(l)
            un.store_q(o_ref, qs, jnp.where(low, o[:, :q_n], o[:, q_n:]))
            upd = jnp.where(grp == 2 * hp, lse[:, :q_n], 0.0) + jnp.where(grp == 2 * hp + 1, lse[:, q_n:], 0.0)
            un.store_q(lse_ref, qs, upd, add=True)

        batch(first, True)

        def more(j, carry):
            batch(later(j), False)
            return carry

        lax.fori_loop(1, 1 + n_more, more, 0)

    pv = _pattern_view(proj, g_n)
    full = lambda a: pl.BlockSpec(a.shape, lambda r, hp: (0, 0))
    o, lse = pl.pallas_call(
        body, grid=_pattern_grid(g_n),
        in_specs=[pl.BlockSpec(memory_space=pltpu.SMEM),
                  _pattern_spec(g_n, lr, lambda hp: 32 + hp),
                  _pattern_spec(g_n, lr, lambda hp: 40 + hp),
                  _pattern_spec(g_n, lr, lambda hp: 48 + hp),
                  full(d0), full(m0), full(d1), full(m1)],
        out_specs=[_pattern_spec(g_n, lr, lambda hp: hp), _pattern_spec(g_n, lr, lambda hp: 0)],
        out_shape=[jax.ShapeDtypeStruct(_pattern_view_shape(s_n, D_MODEL, g_n), ACT_DTYPE),
                   jax.ShapeDtypeStruct(_pattern_view_shape(s_n, LANES, g_n), F32)],
        scratch_shapes=[pltpu.VMEM((2 * q_n, d0.shape[1]), F32), pltpu.VMEM((2 * q_n, 2 * q_n), F32)],
        name=f"attn_fwd_d{d}",
    )(slopes, pv, pv, pv, d0, m0, d1, m1)
    return o.reshape(s_n, D_MODEL), lse.reshape(s_n, LANES)


def _attn_combine(outs, lses, proj):
    s_n = proj.shape[0]
    tm = 512

    def body(o1_ref, o2_ref, o3_ref, l1_ref, l2_ref, l3_ref, za_ref, e_ref, o_ref, lse_ref, ha_ref, hat_ref):
        ls = [l1_ref[...], l2_ref[...], l3_ref[...]]
        mx = jnp.maximum(jnp.maximum(ls[0], ls[1]), ls[2])
        den = sum(jnp.exp(l - mx) for l in ls)
        lse = mx + jnp.log(den)
        lse_ref[...] = lse
        o = jnp.zeros((tm, D_MODEL), F32)
        for l, oref in zip(ls, (o1_ref, o2_ref, o3_ref)):
            o = o + _select_cols(jnp.exp(l - lse), e_ref[...], terms=2) * oref[...].astype(F32)
        o_ref[...] = o.astype(o_ref.dtype)
        z = za_ref[...].astype(F32)
        ha = z * _sigmoid(z) * o
        ha_ref[...] = ha.astype(ha_ref.dtype)
        hat_ref[...] = ha.T.astype(hat_ref.dtype)

    row = lambda w: pl.BlockSpec((tm, w), lambda i: (i, 0))
    return pl.pallas_call(
        body, grid=(s_n // tm,),
        in_specs=[row(D_MODEL)] * 3 + [row(LANES)] * 3
        + [pl.BlockSpec((tm, D_MODEL), lambda i: (i, 7)), pl.BlockSpec((LANES, D_MODEL), lambda i: (0, 0))],
        out_specs=[row(D_MODEL), row(LANES), row(D_MODEL), pl.BlockSpec((D_MODEL, tm), lambda i: (0, i))],
        out_shape=[jax.ShapeDtypeStruct((s_n, D_MODEL), ACT_DTYPE), jax.ShapeDtypeStruct((s_n, LANES), F32),
                   jax.ShapeDtypeStruct((s_n, D_MODEL), ACT_DTYPE), jax.ShapeDtypeStruct((D_MODEL, s_n), ACT_DTYPE)],
        name="attn_combine",
    )(*outs, *lses, proj, _head_expand_matrix())


CHAIN_ROWS = 256


def _row_chains(tm):
    return [slice(r, r + CHAIN_ROWS) for r in range(0, tm, CHAIN_ROWS)]


def _gates(gc_ref, ga_ref, b_ref, rows):
    b = b_ref[...]
    gc = _sigmoid(gc_ref[rows, :].astype(F32) + b[:, :D_MODEL])
    ga = _sigmoid(ga_ref[rows, :].astype(F32) + b[:, D_MODEL:])
    return gc, ga


def _merge_loss(hc, ha, woc, woa, wo, proj, b_merge, xp, final_g, tgt):
    s_n = xp.shape[0]
    tm = 512

    def body(hc_ref, ha_ref, woc_ref, woa_ref, wo_ref, gc_ref, ga_ref, b_ref, x_ref, gf_ref, t_ref,
             yc_ref, ya_ref, dhb_ref, dgf_ref, loss_ref, dwo_ref, mgt_ref):
        i = pl.program_id(0)

        @pl.when(i == 0)
        def _():
            dgf_ref[...] = jnp.zeros(dgf_ref.shape, F32)
            loss_ref[...] = jnp.zeros(loss_ref.shape, F32)
            dwo_ref[...] = jnp.zeros(dwo_ref.shape, F32)

        gf = gf_ref[...]
        for rows in _row_chains(tm):
            yc = _dot(hc_ref[rows, :], woc_ref[...])
            ya = _dot(ha_ref[rows, :], woa_ref[...])
            gc, ga = _gates(gc_ref, ga_ref, b_ref, rows)
            mg = gc * yc + ga * ya
            yc_ref[rows, :] = yc.astype(yc_ref.dtype)
            ya_ref[rows, :] = ya.astype(ya_ref.dtype)
            mgt_ref[:, rows] = mg.T.astype(mgt_ref.dtype)
            h2 = x_ref[rows, :] + _dot(mg, wo_ref[...])
            r2 = lax.rsqrt(jnp.mean(h2 * h2, axis=-1, keepdims=True) + EPS)
            nrm = h2 * r2
            err = nrm * gf - t_ref[rows, :]
            e2 = (err * err).reshape(-1, 8, D_MODEL).sum(axis=0)
            loss_ref[...] += sum(e2[:, c * LANES:(c + 1) * LANES] for c in range(D_MODEL // LANES))
            dy = err * (1.0 / D_MODEL)
            dgf_ref[...] += jnp.sum(dy * nrm, axis=0, keepdims=True)
            dn = dy * gf
            dh2 = r2 * (dn - nrm * jnp.mean(dn * nrm, axis=-1, keepdims=True))
            dhb_ref[rows, :] = dh2.astype(dhb_ref.dtype)
        dwo_ref[...] += _dot(mgt_ref[...], dhb_ref[...])

    row = pl.BlockSpec((tm, D_MODEL), lambda i: (i, 0))
    wsp = pl.BlockSpec((D_MODEL, D_MODEL), lambda i: (0, 0), pipeline_mode=pl.Buffered(1))
    vec = lambda w: pl.BlockSpec((1, w), lambda i: (0, 0))
    act = jax.ShapeDtypeStruct((s_n, D_MODEL), ACT_DTYPE)
    return pl.pallas_call(
        body, grid=(s_n // tm,),
        in_specs=[row, row, wsp, wsp, wsp,
                  pl.BlockSpec((tm, D_MODEL), lambda i: (i, 8)), pl.BlockSpec((tm, D_MODEL), lambda i: (i, 9)),
                  vec(2 * D_MODEL), row, vec(D_MODEL), row],
        out_specs=[row, row, row, vec(D_MODEL), pl.BlockSpec((8, LANES), lambda i: (0, 0)),
                   pl.BlockSpec((D_MODEL, D_MODEL), lambda i: (0, 0))],
        out_shape=[act, act, act, jax.ShapeDtypeStruct((1, D_MODEL), F32), jax.ShapeDtypeStruct((8, LANES), F32),
                   jax.ShapeDtypeStruct((D_MODEL, D_MODEL), F32)],
        scratch_shapes=[pltpu.VMEM((D_MODEL, tm), MXU_DTYPE)],
        name="merge_loss",
    )(hc, ha, woc, woa, wo, proj, proj, b_merge, xp, final_g, tgt)


def _merge_bwd(dh2b, wo, woc, woa, yc, ya, proj, b_merge, o, hct, hat):
    s_n = dh2b.shape[0]
    tm = 512

    def body(dh_ref, wo_ref, woc_ref, woa_ref, yc_ref, ya_ref, gc_ref, ga_ref, b_ref, o_ref, za_ref, e_ref,
             hct_ref, hat_ref, dhc_ref, do_ref, dsum_ref, db3_ref, dbias_ref, dwoc_ref, dwoa_ref,
             dyc_ref, dya_ref):
        i = pl.program_id(0)

        @pl.when(i == 0)
        def _():
            dbias_ref[...] = jnp.zeros(dbias_ref.shape, F32)
            dwoc_ref[...] = jnp.zeros(dwoc_ref.shape, F32)
            dwoa_ref[...] = jnp.zeros(dwoa_ref.shape, F32)

        for rows in _row_chains(tm):
            dmg = _dot_nt(dh_ref[rows, :], wo_ref[...])
            gc, ga = _gates(gc_ref, ga_ref, b_ref, rows)
            dgc = dmg * yc_ref[rows, :].astype(F32) * gc * (1.0 - gc)
            dga = dmg * ya_ref[rows, :].astype(F32) * ga * (1.0 - ga)
            dbias_ref[:, :D_MODEL] += jnp.sum(dgc, axis=0, keepdims=True)
            dbias_ref[:, D_MODEL:] += jnp.sum(dga, axis=0, keepdims=True)
            dyc = dmg * gc
            dya = dmg * ga
            dyc_ref[rows, :] = dyc.astype(dyc_ref.dtype)
            dya_ref[rows, :] = dya.astype(dya_ref.dtype)
            dhc_ref[rows, :] = _dot_nt(dyc, woc_ref[...]).astype(dhc_ref.dtype)
            dha = _dot_nt(dya, woa_ref[...])
            z = za_ref[rows, :].astype(F32)
            sg = _sigmoid(z)
            ov = o_ref[rows, :].astype(F32)
            dout = dha * z * sg
            do_ref[rows, :] = dout.astype(do_ref.dtype)
            dsum_ref[rows, :] = _select_cols(dout * ov, e_ref[...], terms=2)
            db3_ref[0, rows, :] = (dha * ov * sg * (1.0 + z * (1.0 - sg))).astype(db3_ref.dtype)
            db3_ref[1, rows, :] = dgc.astype(db3_ref.dtype)
            db3_ref[2, rows, :] = dga.astype(db3_ref.dtype)
        dwoc_ref[...] += _dot(hct_ref[...], dyc_ref[...])
        dwoa_ref[...] += _dot(hat_ref[...], dya_ref[...])

    row = pl.BlockSpec((tm, D_MODEL), lambda i: (i, 0))
    col = pl.BlockSpec((D_MODEL, tm), lambda i: (0, i))
    wsp = pl.BlockSpec((D_MODEL, D_MODEL), lambda i: (0, 0), pipeline_mode=pl.Buffered(1))
    acc = pl.BlockSpec((D_MODEL, D_MODEL), lambda i: (0, 0))
    act = jax.ShapeDtypeStruct((s_n, D_MODEL), ACT_DTYPE)
    grad = jax.ShapeDtypeStruct((D_MODEL, D_MODEL), F32)
    return pl.pallas_call(
        body, grid=(s_n // tm,),
        in_specs=[row, wsp, wsp, wsp, row, row,
                  pl.BlockSpec((tm, D_MODEL), lambda i: (i, 8)), pl.BlockSpec((tm, D_MODEL), lambda i: (i, 9)),
                  pl.BlockSpec((1, 2 * D_MODEL), lambda i: (0, 0)), row,
                  pl.BlockSpec((tm, D_MODEL), lambda i: (i, 7)), pl.BlockSpec((D_MODEL, LANES), lambda i: (0, 0)),
                  col, col],
        out_specs=[row, row, pl.BlockSpec((tm, LANES), lambda i: (i, 0)),
                   pl.BlockSpec((3, tm, D_MODEL), lambda i: (0, i, 0)),
                   pl.BlockSpec((1, 2 * D_MODEL), lambda i: (0, 0)), acc, acc],
        out_shape=[act, act, jax.ShapeDtypeStruct((s_n, LANES), F32),
                   jax.ShapeDtypeStruct((3, s_n, D_MODEL), ACT_DTYPE),
                   jax.ShapeDtypeStruct((1, 2 * D_MODEL), F32), grad, grad],
        scratch_shapes=[pltpu.VMEM((tm, D_MODEL), MXU_DTYPE), pltpu.VMEM((tm, D_MODEL), MXU_DTYPE)],
        name="merge_bwd",
    )(dh2b, wo, woc, woa, yc, ya, proj, proj, b_merge, o, proj, _head_sum_matrix(), hct, hat)


def _behind(body, after, n_prefetch=0):
    if after is None:
        return body, [], []

    def ordered(*refs):
        body(*refs[:n_prefetch], *refs[n_prefetch + 1:])

    return ordered, [_ANY], [after]


def _conv_bwd(proj, conv_w, dhc, after=None):
    s_n = proj.shape[0]
    lr = s_n // N_RES
    pv = proj.reshape(N_RES, lr, IN_COLS)

    def body(xc_ref, bg_ref, cg_ref, zc_ref, w_ref, dhc_ref, da4_ref, dw_ref, dc_ref):
        w = w_ref[...]
        row = lax.broadcasted_iota(jnp.int32, (lr, LANES), 0)
        dw = [jnp.zeros((1, LANES), F32) for _ in range(3)]
        products = {}
        for r in range(N_RES):
            a, am1, am2 = _conv_terms(xc_ref, cg_ref, r, row, lr, products)
            c = w[0:1] * am2 + w[1:2] * am1 + w[2:3] * a
            z = zc_ref[r].astype(F32)
            sg = _sigmoid(z)
            sz = z * sg
            bg = bg_ref[r].astype(F32)
            dh = dhc_ref[r].astype(F32)
            da4_ref[1, r] = (dh * sz * c).astype(da4_ref.dtype)
            da4_ref[3, r] = (dh * bg * c * sg * (1.0 + z * (1.0 - sg))).astype(da4_ref.dtype)
            dc = dh * sz * bg
            dc_ref[r] = dc
            dw[0] = dw[0] + jnp.sum(dc * am2, axis=0, keepdims=True)
            dw[1] = dw[1] + jnp.sum(dc * am1, axis=0, keepdims=True)
            dw[2] = dw[2] + jnp.sum(dc * a, axis=0, keepdims=True)
        dw_ref[0:1, :] = dw[0]
        dw_ref[1:2, :] = dw[1]
        dw_ref[2:3, :] = dw[2]

        def shift_up(v):
            return jnp.where(row < lr - 1, pltpu.roll(v, lr - 1, 0), 0.0)

        for r in range(N_RES):
            dp1 = dc_ref[r + 1] if r + 1 < N_RES else shift_up(dc_ref[0])
            dp2 = dc_ref[r + 2] if r + 2 < N_RES else shift_up(dc_ref[r + 2 - N_RES])
            da = w[2:3] * dc_ref[r] + w[1:2] * dp1 + w[0:1] * dp2
            da4_ref[0, r] = (da * cg_ref[r].astype(F32)).astype(da4_ref.dtype)
            da4_ref[2, r] = (da * xc_ref[r].astype(F32)).astype(da4_ref.dtype)

    def col(part):
        return pl.BlockSpec((N_RES, lr, LANES), lambda j: (0, 0, part * 8 + j))

    kern, token_spec, token = _behind(body, after)
    da4, dw = pl.pallas_call(
        kern, grid=(D_MODEL // LANES,),
        in_specs=token_spec + [col(0), col(1), col(2), col(3), pl.BlockSpec((3, LANES), lambda j: (0, j)),
                               pl.BlockSpec((N_RES, lr, LANES), lambda j: (0, 0, j))],
        out_specs=[pl.BlockSpec((4, N_RES, lr, LANES), lambda j: (0, 0, 0, j)),
                   pl.BlockSpec((3, LANES), lambda j: (0, j))],
        out_shape=[jax.ShapeDtypeStruct((4, N_RES, lr, D_MODEL), ACT_DTYPE),
                   jax.ShapeDtypeStruct((3, D_MODEL), F32)],
        scratch_shapes=[pltpu.VMEM((N_RES, lr, LANES), F32)],
        name="conv_bwd",
    )(*token, pv, pv, pv, pv, conv_w, dhc.reshape(N_RES, lr, D_MODEL))
    return da4.reshape(4, s_n, D_MODEL), dw


def _attn_bwd(proj, dout, lse, dsum, slopes, d, prev=None, after=None):
    g_n, rq = PATTERNS[d]
    un = _Units(g_n, rq)
    s_n = proj.shape[0]
    lr = s_n // N_RES
    q_n = g_n * rq
    d0, m0, d1, m1 = (np.ascontiguousarray(t.T) for t in _attn_tables(d))
    first, n_more, later = un.plan(lr, ATTN_BATCH)
    bsz = un.count(first)

    def body(sl_ref, q_ref, k_ref, v_ref, do_ref, lse_ref, ds_ref, d0_ref, m0_ref, d1_ref, m1_ref, *rest):
        prev_ref = rest[0] if prev is not None else None
        out_ref, b0_ref, b1_ref, lt_ref, dt_ref, dk_ref, dv_ref = rest[-7:]
        hp = pl.program_id(1)
        for h in (0, 1):
            slope = sl_ref[2 * hp + h]
            b0_ref[:, h * q_n:(h + 1) * q_n] = m0_ref[...] - slope * d0_ref[...]
            b1_ref[:, h * q_n:(h + 1) * q_n] = m1_ref[...] - slope * d1_ref[...]
        if prev is None:
            dk_ref[...] = jnp.zeros(dk_ref.shape, F32)
            dv_ref[...] = jnp.zeros(dv_ref.shape, F32)
        else:
            out_ref[0] = prev_ref[0]
            dk_ref[...] = prev_ref[1].astype(F32)
            dv_ref[...] = prev_ref[2].astype(F32)
        low = lax.broadcasted_iota(jnp.int32, (1, q_n, LANES), 2) < HEAD_DIM
        row16 = pl.multiple_of(16 * hp, 16)

        def query_rows(stat_ref, t_ref, qs):
            tiles = un.load_q(stat_ref, qs)
            for b in range(bsz):
                t_ref[b] = tiles[b].T
            t16 = t_ref[:, pl.ds(row16, 16), :]
            return jnp.concatenate([t16[:, 0:1, :], t16[:, 8:9, :]], axis=2)

        def batch(qs, at_start):
            qq = _stack_heads(un.load_q(q_ref, qs) * 0.125, low)
            dd = _stack_heads(un.load_q(do_ref, qs), low)
            ks = un.load_k(k_ref, qs, at_start)
            vs = un.load_k(v_ref, qs, at_start)
            lrow = query_rows(lse_ref, lt_ref, qs)
            drow = query_rows(ds_ref, dt_ref, qs)
            pt = jnp.exp(_bdot(ks, qq, _BNT) + _batch_bias(un, qs, at_start, b0_ref, b1_ref) - lrow)
            dst = pt * (_bdot(vs, dd, _BNT) - drow)
            un.add_k(dv_ref, qs, _bdot(pt, dd, _BNN), at_start)
            un.add_k(dk_ref, qs, _bdot(dst, qq, _BNN), at_start)
            dq = _bdot(jnp.swapaxes(dst, 1, 2), ks, _BNN)
            un.store_q(out_ref, qs, jnp.where(low, dq[:, :q_n], dq[:, q_n:]) * 0.125, add=prev is not None,
                       lead=(0,))

        batch(first, True)

        def more(j, carry):
            batch(later(j), False)
            return carry

        lax.fori_loop(1, 1 + n_more, more, 0)
        out_ref[1] = dk_ref[...].astype(out_ref.dtype)
        out_ref[2] = dv_ref[...].astype(out_ref.dtype)

    pv = _pattern_view(proj, g_n)
    full = lambda a: pl.BlockSpec(a.shape, lambda r, hp: (0, 0))
    whole = _pattern_spec(g_n, lr, lambda hp: hp, lead=(3,))
    kern, token_spec, token = _behind(body, after)
    out = pl.pallas_call(
        kern, grid=_pattern_grid(g_n),
        in_specs=token_spec + [pl.BlockSpec(memory_space=pltpu.SMEM),
                               _pattern_spec(g_n, lr, lambda hp: 32 + hp),
                               _pattern_spec(g_n, lr, lambda hp: 40 + hp),
                               _pattern_spec(g_n, lr, lambda hp: 48 + hp),
                               _pattern_spec(g_n, lr, lambda hp: hp),
                               _pattern_spec(g_n, lr, lambda hp: 0),
                               _pattern_spec(g_n, lr, lambda hp: 0),
                               full(d0), full(m0), full(d1), full(m1)] + ([] if prev is None else [whole]),
        out_specs=whole,
        out_shape=jax.ShapeDtypeStruct(_pattern_view_shape(s_n, D_MODEL, g_n, lead=(3,)), ACT_DTYPE),
        scratch_shapes=[pltpu.VMEM((d0.shape[0], 2 * q_n), F32), pltpu.VMEM((2 * q_n, 2 * q_n), F32),
                        pltpu.VMEM((bsz, LANES, q_n), F32), pltpu.VMEM((bsz, LANES, q_n), F32),
                        pltpu.VMEM(_pattern_block(g_n, lr), F32), pltpu.VMEM(_pattern_block(g_n, lr), F32)],
        name=f"attn_bwd_d{d}",
    )(*token, slopes, pv, pv, pv, _pattern_view(dout, g_n), _pattern_view(lse, g_n), _pattern_view(dsum, g_n),
      d0, m0, d1, m1, *([] if prev is None else [_pattern_view(prev, g_n, lead=(3,))]))
    return out.reshape(3, s_n, D_MODEL)


def _part_index(step, per, lo, n):
    return jnp.clip(step // per - lo, 0, n - 1)


def _dw_in(ut, da4, dc3, db3):
    s_n = ut.shape[1]
    tn = 512
    per = D_MODEL // tn
    shard_blocks = SHARD_COLS // tn

    def body(a_ref, p0_ref, p1_ref, p2_ref, o_ref):
        part = pl.program_id(0) // per

        @pl.when(part < 4)
        def _():
            o_ref[...] = _dot(a_ref[...], p0_ref[...])

        @pl.when((part >= 4) & (part < 7))
        def _():
            o_ref[...] = _dot(a_ref[...], p1_ref[...])

        @pl.when(part >= 7)
        def _():
            o_ref[...] = _dot(a_ref[...], p2_ref[...])

    def pspec(lo, n):
        def index(j):
            part = j // per
            col = jnp.where(part < lo, 0, jnp.where(part >= lo + n, per - 1, j % per))
            return _part_index(j, per, lo, n), 0, col
        return pl.BlockSpec((None, s_n, tn), index)

    return pl.pallas_call(
        body, grid=(IN_COLS // tn,),
        in_specs=[pl.BlockSpec((D_MODEL, s_n), lambda j: (0, 0), pipeline_mode=pl.Buffered(1)),
                  pspec(0, 4), pspec(4, 3), pspec(7, 3)],
        out_specs=pl.BlockSpec((None, D_MODEL, tn), lambda j: (j // shard_blocks, 0, j % shard_blocks)),
        out_shape=jax.ShapeDtypeStruct((4, D_MODEL, SHARD_COLS), F32),
        name="dw_in",
    )(ut, da4, dc3, db3)


def _input_grad(da4, dc3, db3, w4, xp, norm_g, dh2, row0, rows, after=None):
    tm, tk = 256, 512
    per = D_MODEL // tk
    shard_blocks = SHARD_COLS // tk
    m0 = row0 // tm

    def body(p0_ref, p1_ref, p2_ref, w_ref, x_ref, g_ref, dh_ref, gx_ref, dg_ref):
        @pl.when(pl.program_id(0) == 0)
        def _():
            dg_ref[...] = jnp.zeros(dg_ref.shape, F32)

        du = None
        for k in range(IN_COLS // tk):
            part, cols = k // per, pl.ds((k % per) * tk, tk)
            ref, slot = (p0_ref, part) if part < 4 else (p1_ref, part - 4) if part < 7 else (p2_ref, part - 7)
            d = _dot_nt(ref[slot, :, cols], w_ref[k // shard_blocks, :, pl.ds((k % shard_blocks) * tk, tk)])
            du = d if du is None else du + d
        x = x_ref[...]
        r = lax.rsqrt(jnp.mean(x * x, axis=-1, keepdims=True) + EPS)
        nrm = x * r
        dg_ref[...] += jnp.sum(du * nrm, axis=0, keepdims=True)
        dn = du * g_ref[...]
        gx_ref[...] = dh_ref[...].astype(F32) + r * (dn - nrm * jnp.mean(dn * nrm, axis=-1, keepdims=True))

    def pspec(n):
        return pl.BlockSpec((n, tm, D_MODEL), lambda m: (0, m0 + m, 0))

    row_in = pl.BlockSpec((tm, D_MODEL), lambda m: (m0 + m, 0))
    vec = pl.BlockSpec((1, D_MODEL), lambda m: (0, 0))
    kern, token_spec, token = _behind(body, after)
    return pl.pallas_call(
        kern, grid=(rows // tm,),
        in_specs=token_spec + [pspec(4), pspec(3), pspec(3),
                               pl.BlockSpec(w4.shape, lambda m: (0, 0, 0), pipeline_mode=pl.Buffered(1)),
                               row_in, vec, row_in],
        out_specs=[pl.BlockSpec((tm, D_MODEL), lambda m: (m, 0)), vec],
        out_shape=[jax.ShapeDtypeStruct((rows, D_MODEL), F32), jax.ShapeDtypeStruct((1, D_MODEL), F32)],
        name="input_grad",
    )(*token, da4, dc3, db3, w4, xp, norm_g, dh2)


class _Step:
    def __init__(self, x, tgt, norm_g, chip, after=None):
        self.norm_g, self.chip = norm_g, chip
        self.slopes = _alibi_slopes()
        self.xp, self.tp = _to_residue_major(x, tgt, after)
        self.u, self.ut = _rms_in(self.xp, norm_g)

    def project_own(self, w_own):
        self.proj_own = _in_proj(self.u, self.chip, w_own=w_own)

    def project(self, w4, others):
        self.proj_own = _in_proj(self.u, self.chip, w4=w4, partial=self.proj_own, others=others)

    def mixers(self, w4, taps):
        self.w4, self.taps, self.proj = w4, taps, self.proj_own
        self.hc, self.hct = _conv_fwd(self.proj, taps)
        fwd = [_attn_fwd(self.proj, self.slopes, d) for d in PATTERNS]
        self.o, self.lse, self.ha, self.hat = _attn_combine([f[0] for f in fwd], [f[1] for f in fwd], self.proj)

    def merge_and_loss(self, woc, woa, wo, b_merge, final_g):
        self.woc, self.woa, self.wo, self.b_merge = woc, woa, wo, b_merge
        (self.yc, self.ya, self.dh2b, self.d_final_g, self.loss8, self.d_wo) = _merge_loss(
            self.hc, self.ha, woc, woa, wo, self.proj, b_merge, self.xp, final_g, self.tp)

    def out_weight_grads(self):
        (self.dhc, self.dout, self.dsum, self.db3, self.d_bias, d_woc, d_woa) = _merge_bwd(
            self.dh2b, self.wo, self.woc, self.woa, self.yc, self.ya, self.proj, self.b_merge, self.o,
            self.hct, self.hat)
        return d_woc, d_woa, self.d_wo

    def conv_grads(self, after=None):
        self.da4, self.d_taps = _conv_bwd(self.proj, self.taps, self.dhc, after)

    def in_weight_grad(self, after=None):
        self.dc3 = None
        for d in PATTERNS:
            self.dc3 = _attn_bwd(self.proj, self.dout, self.lse, self.dsum, self.slopes, d, prev=self.dc3,
                                 after=after if self.dc3 is None else None)
        return _dw_in(self.ut, self.da4, self.dc3, self.db3)

    def input_grad(self, half, after=None):
        rows = self.xp.shape[0] // 2
        return _input_grad(self.da4, self.dc3, self.db3, self.w4, self.xp, self.norm_g, self.dh2b,
                           half * rows, rows, after)


def _local_grads(x, tgt, norm_g, w4, b_merge, conv_w, woc, woa, wo, final_g):
    st = _Step(x, tgt, norm_g, jnp.zeros((1,), jnp.int32))
    st.project_own(w4[0])
    st.project(w4, (2, 1))
    st.project(w4, (3,))
    st.mixers(w4, conv_w)
    st.merge_and_loss(woc, woa, wo, b_merge, final_g)
    d_woc, d_woa, d_wo = st.out_weight_grads()
    st.conv_grads()
    d_w4 = st.in_weight_grad()
    gx_lo, dg_lo = st.input_grad(0)
    gx_hi, dg_hi = st.input_grad(1)
    return (st.loss8, _to_natural(gx_lo, gx_hi), dg_lo + dg_hi, d_w4, st.d_bias, st.d_taps, d_woc, d_woa, d_wo,
            st.d_final_g)


MESH = pl.DeviceIdType.MESH
_CHIP_FLIPS = ((1, 0), (0, 1), (1, 1))
_ANY = pl.BlockSpec(memory_space=pl.ANY)


def _place():
    return lax.axis_index("x"), lax.axis_index("y"), lax.axis_index("c")


def _flip(v, f):
    return 1 - v if f else v


def _remote(src, dst, send_sems, recv_sems, k, device):
    return pltpu.make_async_remote_copy(src_ref=src, dst_ref=dst, send_sem=send_sems.at[k], recv_sem=recv_sems.at[k],
                                        device_id=device, device_id_type=MESH)


def _place_shard(w, chip, dtype):
    rows, cols = w.shape
    tm = min(rows, 128)

    def body(chip_ref, w_ref, o_ref):
        o_ref[0] = w_ref[...].astype(o_ref.dtype)

    return pl.pallas_call(
        body,
        grid_spec=pltpu.PrefetchScalarGridSpec(
            num_scalar_prefetch=1, grid=(rows // tm,),
            in_specs=[pl.BlockSpec((tm, cols), lambda i, chip_ref: (i, 0))],
            out_specs=pl.BlockSpec((1, tm, cols), lambda i, chip_ref: (chip_ref[0], i, 0))),
        out_shape=jax.ShapeDtypeStruct((4, rows, cols), dtype),
        name="place_shard",
    )(chip, w)


def _gather_copies_to(flips, whole=()):
    def copies(arrs, _, send_sems, recv_sems):
        x, y, c = _place()
        out = []
        for a, arr in enumerate(arrs):
            h = arr.shape[1] // 2
            mine = arr.at[2 * x + y] if a in whole else arr.at[2 * x + y, pl.ds(pl.multiple_of(c * h, 8), h)]
            for i, t in enumerate(flips):
                fx, fy = _CHIP_FLIPS[t]
                out.append(_remote(mine, mine, send_sems, recv_sems, len(flips) * a + i,
                                   (_flip(x, fx), _flip(y, fy), c)))
        return out
    return copies


def _forward_to_sibling(arrs, flips=(0, 1, 2)):
    n = len(arrs)

    def body(*refs):
        outs = refs[n:2 * n]
        send_sems, recv_sems = refs[2 * n:]
        x, y, c = _place()
        sibling = (x, y, 1 - c)
        started = []
        for a in range(n):
            h = outs[a].shape[1] // 2
            rows = pl.ds(pl.multiple_of(c * h, 8), h)
            for t in flips:
                fx, fy = _CHIP_FLIPS[t]
                landed = outs[a].at[2 * _flip(x, fx) + _flip(y, fy), rows]
                cp = _remote(landed, landed, send_sems, recv_sems, 3 * a + t, sibling)
                cp.start()
                started.append(cp)
        for a in range(n):
            h = outs[a].shape[1] // 2
            rows = pl.ds(pl.multiple_of((1 - c) * h, 8), h)
            for t in flips:
                fx, fy = _CHIP_FLIPS[t]
                handed = outs[a].at[2 * _flip(x, fx) + _flip(y, fy), rows]
                _remote(handed, handed, send_sems, recv_sems, 3 * a + t, sibling).wait_recv()
        for cp in started:
            cp.wait_send()

    return pl.pallas_call(
        body, in_specs=[_ANY] * n, out_specs=[_ANY] * n,
        out_shape=[jax.ShapeDtypeStruct(s.shape, s.dtype) for s in arrs],
        input_output_aliases={a: a for a in range(n)},
        scratch_shapes=[pltpu.SemaphoreType.DMA((3 * n,)), pltpu.SemaphoreType.DMA((3 * n,))],
        name="gathered_to_sibling_" + "".join(str(t) for t in flips),
    )(*arrs)


_HBM = pl.BlockSpec(memory_space=pltpu.HBM)
_SEM = pl.BlockSpec(memory_space=pltpu.SEMAPHORE)
_EFFECT = pltpu.SideEffectType.DATAFLOW_SIDE_EFFECTING


class _SplitExchange:
    def __init__(self, name, srcs, land_shapes, n_copies, copies, riders=()):
        self.name, self.n, self.nl, self.copies = name, len(srcs), len(land_shapes), copies
        n, nb = self.n, len(srcs) + len(land_shapes)
        lands = [lax.empty(s.shape, s.dtype) for s in land_shapes]
        bufs = [pltpu.with_memory_space_constraint(a, pltpu.HBM) for a in (*srcs, *lands, *riders)]
        na = len(bufs)

        def body(*refs):
            send_sems, recv_sems = refs[na], refs[na + 1]
            for cp in copies(refs[:n], refs[n:nb], send_sems, recv_sems):
                cp.start()
            refs[-1][...] = jnp.zeros(refs[-1].shape, F32)

        outs = pl.pallas_call(
            body, name=name + "_start",
            in_specs=[_HBM] * na,
            out_specs=[_SEM, _SEM] + [_HBM] * na + [pl.BlockSpec(memory_space=pltpu.VMEM)],
            out_shape=[pltpu.SemaphoreType.DMA((n_copies,)), pltpu.SemaphoreType.DMA((n_copies,))]
            + [pltpu.HBM(b.shape, b.dtype) for b in bufs] + [jax.ShapeDtypeStruct((8, LANES), F32)],
            input_output_aliases={i: 2 + i for i in range(na)},
            compiler_params=pltpu.CompilerParams(has_side_effects=_EFFECT),
        )(*bufs)
        self.sems, self.bufs, self.riders, self.token = outs[:2], outs[2:2 + nb], outs[2 + nb:2 + na], outs[-1]

    def wait(self, done, riders=(), bufs=None):
        n, nb, copies = self.n, self.n + self.nl, self.copies
        bufs = [*(self.bufs if bufs is None else bufs),
                *[pltpu.with_memory_space_constraint(a, pltpu.HBM) for a in riders]]
        na = len(bufs)
        done = list(done) if isinstance(done, (list, tuple)) else [done]

        def body(*refs):
            send_sems, recv_sems = refs[na], refs[na + 1]
            for cp in copies(refs[:n], refs[n:nb], send_sems, recv_sems):
                cp.wait_send()
                cp.wait_recv()

        outs = pl.pallas_call(
            body, name=self.name + "_wait",
            in_specs=[_HBM] * na + [_SEM, _SEM] + [_ANY] * len(done),
            out_specs=[_HBM] * na,
            out_shape=[pltpu.HBM(b.shape, b.dtype) for b in bufs],
            input_output_aliases={i: i for i in range(na)},
            compiler_params=pltpu.CompilerParams(has_side_effects=_EFFECT),
        )(*bufs, *self.sems, *done)
        return outs[:n], outs[n:nb], outs[nb:]


def _sibling_copies(srcs, lands, send_sems, recv_sems):
    x, y, c = _place()
    out = []
    for a, (src, land) in enumerate(zip(srcs, lands)):
        h = src.shape[1] // 2
        theirs = pl.ds(pl.multiple_of((1 - c) * h, 8), h)
        out.append(_remote(src.at[:, theirs], land, send_sems, recv_sems, a, (x, y, 1 - c)))
    return out


def _grads_to_sibling(name, grads):
    shapes = [jax.ShapeDtypeStruct((4, g.shape[1] // 2, g.shape[2]), g.dtype) for g in grads]
    return _SplitExchange(name, grads, shapes, len(grads), _sibling_copies)


def _chip_copies(srcs, lands, send_sems, recv_sems):
    x, y, c = _place()
    out = []
    for a, (src, land) in enumerate(zip(srcs, lands)):
        for t, (fx, fy) in enumerate(_CHIP_FLIPS):
            tx, ty = _flip(x, fx), _flip(y, fy)
            out.append(_remote(src.at[2 * tx + ty], land.at[t], send_sems, recv_sems, 3 * a + t, (tx, ty, c)))
    return out


def _grads_to_chips(name, parts):
    shapes = [jax.ShapeDtypeStruct((3, *p.shape[1:]), p.dtype) for p in parts]
    return _SplitExchange(name, parts, shapes, 3 * len(parts), _chip_copies)


def _add_halves(g, r, half):
    _, rows, cols = g.shape
    h = rows // 2
    tm = min(h, 128)
    nt = h // tm

    def body(half_ref, g_ref, r_ref, b_ref):
        b_ref[...] = (g_ref[...] + r_ref[...]).astype(b_ref.dtype)

    spec = pl.BlockSpec((1, tm, cols), lambda j, i, half_ref: (j, i, 0))
    return pl.pallas_call(
        body,
        grid_spec=pltpu.PrefetchScalarGridSpec(
            num_scalar_prefetch=1, grid=(4, nt),
            in_specs=[pl.BlockSpec((1, tm, cols), lambda j, i, half_ref: (j, half_ref[0] * nt + i, 0)), spec],
            out_specs=spec),
        out_shape=jax.ShapeDtypeStruct((4, h, cols), BF16),
        name="add_sibling_grads",
    )(half, g, r)


def _add_chips(g, r, recv, where, after=None):
    _, h, cols = r.shape
    tm = min(h, 128)
    nt = h // tm

    def body(where_ref, g_ref, r_ref, recv_ref, out_ref):
        own = g_ref[0] + r_ref[0]
        out_ref[...] = ((own + recv_ref[0].astype(F32)) + recv_ref[1].astype(F32)) + recv_ref[2].astype(F32)

    kern, token_spec, token = _behind(body, after, n_prefetch=1)
    return pl.pallas_call(
        kern,
        grid_spec=pltpu.PrefetchScalarGridSpec(
            num_scalar_prefetch=1, grid=(nt,),
            in_specs=token_spec + [pl.BlockSpec((1, tm, cols), lambda i, w: (w[0], w[1] * nt + i, 0)),
                                   pl.BlockSpec((1, tm, cols), lambda i, w: (w[0], i, 0)),
                                   pl.BlockSpec((3, tm, cols), lambda i, w: (0, i, 0))],
            out_specs=pl.BlockSpec((tm, cols), lambda i, w: (w[1] * nt + i, 0))),
        out_shape=jax.ShapeDtypeStruct((2 * h, cols), F32),
        name="add_chip_grads",
    )(where, *token, g, r, recv)


def _half_copies(srcs, _, send_sems, recv_sems):
    x, y, c = _place()
    out = []
    for a, src in enumerate(srcs):
        h = src.shape[0] // 2
        mine = src.at[pl.ds(pl.multiple_of(c * h, 8), h)]
        out.append(_remote(mine, mine, send_sems, recv_sems, a, (x, y, 1 - c)))
    return out


def _share_halves(name, shards):
    return _SplitExchange(name, shards, [], len(shards), _half_copies)


def _reduce_small(rows):
    cols = rows[0].shape[1]
    n = len(rows)
    assert sum(r.shape[0] for r in rows) <= 8

    def body(*refs):
        ins, out_ref = refs[:n], refs[n]
        vec_ref, gath_ref, send_sems, recv_sems = refs[n + 1:]
        x, y, c = _place()
        me = 4 * x + 2 * y + c
        vec_ref[...] = jnp.zeros(vec_ref.shape, F32)
        at = 0
        for r in ins:
            vec_ref[at:at + r.shape[0], :] = r[...]
            at += r.shape[0]
        copies = []
        for k in range(1, 8):
            peer = (_flip(x, (k >> 2) & 1), _flip(y, (k >> 1) & 1), _flip(c, k & 1))
            copies.append(_remote(vec_ref, gath_ref.at[me], send_sems, recv_sems, k - 1, peer))
        for cp in copies:
            cp.start()
        gath_ref[me] = vec_ref[...]
        for cp in copies:
            cp.wait()
        tot = gath_ref[0]
        for dev in range(1, 8):
            tot = tot + gath_ref[dev]
        out_ref[...] = tot
        out_ref[7:8, :] = jnp.zeros((1, cols), F32) + jnp.sum(tot[7:8, :])

    vm = pl.BlockSpec(memory_space=pltpu.VMEM)
    return pl.pallas_call(
        body, in_specs=[vm] * n, out_specs=vm,
        out_shape=jax.ShapeDtypeStruct((8, cols), F32),
        scratch_shapes=[pltpu.VMEM((8, cols), F32), pltpu.VMEM((8, 8, cols), F32),
                        pltpu.SemaphoreType.DMA((7,)), pltpu.SemaphoreType.DMA((7,))],
        name="reduce_small",
    )(*rows)


def _adamw_tile(w_ref, g_ref, m_ref, v_ref, d_ref, m2_ref, v2_ref, gout_ref):
    gr = g_ref[...]
    m2 = ADAM_B1 * m_ref[...] + (1.0 - ADAM_B1) * gr
    v2 = ADAM_B2 * v_ref[...] + (1.0 - ADAM_B2) * (gr * gr)
    m_hat = m2 / (1.0 - ADAM_B1 ** ADAM_STEP)
    v_hat = v2 / (1.0 - ADAM_B2 ** ADAM_STEP)
    d_ref[...] = -ADAM_LR * (m_hat / (jnp.sqrt(v_hat) + ADAM_EPS) + ADAM_WD * w_ref[...])
    m2_ref[...] = m2
    v2_ref[...] = v2
    gout_ref[...] = gr


def _adamw(w, g, m, v, name):
    rows, cols = w.shape
    tm = 128 if rows % 128 == 0 else rows

    def body(*refs):
        _adamw_tile(*refs)

    spec = pl.BlockSpec((tm, cols), lambda i: (i, 0))
    sds = jax.ShapeDtypeStruct((rows, cols), F32)
    return pl.pallas_call(body, grid=(rows // tm,), in_specs=[spec] * 4, out_specs=[spec] * 4,
                          out_shape=[sds] * 4, name=name)(w, g, m, v)


def _adamw_half(w, g, m, v, half, name, other=()):
    rows, cols = w.shape
    tm = 128
    nt = rows // 2 // tm

    def body(half_ref, *refs):
        _adamw_tile(*refs[:4], *refs[4 + len(other):])

    spec = pl.BlockSpec((tm, cols), lambda i, h: (h[0] * nt + i, 0))
    sds = jax.ShapeDtypeStruct((rows, cols), F32)
    return pl.pallas_call(
        body,
        grid_spec=pltpu.PrefetchScalarGridSpec(
            num_scalar_prefetch=1, grid=(nt,), in_specs=[spec] * 4 + [_ANY] * len(other), out_specs=[spec] * 4),
        out_shape=[sds] * 4, input_output_aliases={5 + k: k for k in range(len(other))}, name=name,
    )(half, w, g, m, v, *other)


def kernel(x, norm_g, w_in, b_merge, conv_w, w_out_conv, w_out_attn, w_o, final_g, loss_target, m_norm_g, m_w_in, m_b_merge, m_conv_w, m_w_out_conv, m_w_out_attn, m_w_o, m_final_g, v_norm_g, v_w_in, v_b_merge, v_conv_w, v_w_out_conv, v_w_out_attn, v_w_o, v_final_g):
    mx, my, mc = _place()
    chip = (2 * mx + my).astype(jnp.int32)
    seq = x.shape[1]

    chip1 = chip.reshape(1)
    slots = [_place_shard(w[0], chip1, MXU_DTYPE) for w in (w_in, w_out_conv, w_out_attn, w_o)]
    taps_slot = _place_shard(jnp.pad(conv_w[0], ((0, 5), (0, 0))), chip1, F32)
    gather_near = _SplitExchange("gather_w_in_near", [slots[0], taps_slot], [], 4,
                                 _gather_copies_to((0, 1), whole=(1,)))
    st = _Step(x[0], loss_target[0], norm_g, chip1, after=gather_near.token)
    st.project_own(w_in[0])
    near, _, _ = gather_near.wait([st.ut, st.proj_own])
    gather_far = _SplitExchange("gather_w_in_far", near, [], 2, _gather_copies_to((2,), whole=(1,)))
    (w4,) = _forward_to_sibling(gather_far.bufs[:1], flips=(0, 1))
    st.project(w4, (2, 1))
    (w4, taps4), _, out_slots = gather_far.wait([st.proj_own], riders=slots[1:], bufs=[w4, gather_far.bufs[1]])
    gather_out = _SplitExchange("gather_w_out", out_slots, [], 9, _gather_copies_to((0, 1, 2)), riders=[w4])
    (w4,) = _forward_to_sibling(gather_out.riders, flips=(2,))
    st.project(w4, (3,))
    st.mixers(w4, jnp.concatenate([taps4[j, :3, :] for j in range(4)], axis=1))
    out_ws, _, _ = gather_out.wait(st.o)
    woc, woa, wo = [w.reshape(D_MODEL, D_MODEL) for w in _forward_to_sibling(out_ws)]
    st.merge_and_loss(woc, woa, wo, b_merge, final_g.reshape(1, D_MODEL))

    half = mc.astype(jnp.int32).reshape(1)
    where = jnp.stack([chip, mc.astype(jnp.int32)])
    out_grads = [g.reshape(4, -1, D_MODEL) for g in st.out_weight_grads()]
    to_sibling = _grads_to_sibling("out_grads_to_sibling", out_grads)
    st.conv_grads(after=to_sibling.token)
    out_grads, out_from_sibling, _ = to_sibling.wait(st.da4)
    to_chips = _grads_to_chips("out_grads_to_chips",
                               [_add_halves(g, r, half) for g, r in zip(out_grads, out_from_sibling)])
    d_w4 = st.in_weight_grad(after=to_chips.token)
    out_from_chips = to_chips.wait(st.dc3)[1]

    to_sibling = _grads_to_sibling("in_grad_to_sibling", [d_w4])
    gx_lo, dg_lo = st.input_grad(0, after=to_sibling.token)
    (d_w4,), (from_sibling,), _ = to_sibling.wait(gx_lo)
    to_chips = _grads_to_chips("in_grad_to_chips", [_add_halves(d_w4, from_sibling, half)])

    out_reduced = [_add_chips(g, r, recv, where, after=to_chips.token)
                   for g, r, recv in zip(out_grads, out_from_sibling, out_from_chips)]
    share_out = _share_halves("share_out_grads", out_reduced)
    gx_hi, dg_hi = st.input_grad(1, after=share_out.token)
    grad_x = _to_natural(gx_lo, gx_hi)
    g_woc, g_woa, g_wo = share_out.wait(gx_hi)[0]

    small = _reduce_small([dg_lo + dg_hi, st.d_bias.reshape(2, D_MODEL), st.d_taps, st.d_final_g,
                           st.loss8.reshape(1, D_MODEL)])
    loss = (0.5 / D_MODEL) * small[7, 0]
    g_taps = lax.dynamic_slice(small[3:6], (0, chip * (D_MODEL // 4)), (3, D_MODEL // 4))
    upd = {
        "norm_g": _adamw(norm_g, small[0:1], m_norm_g, v_norm_g, "adamw_norm_g"),
        "b_merge": _adamw(b_merge, small[1:3].reshape(1, 2 * D_MODEL), m_b_merge, v_b_merge, "adamw_b_merge"),
        "conv_w": _adamw(conv_w[0], g_taps, m_conv_w[0], v_conv_w[0], "adamw_conv_w"),
        "w_out_conv": _adamw(w_out_conv[0], g_woc, m_w_out_conv[0], v_w_out_conv[0], "adamw_w_out_conv"),
        "w_out_attn": _adamw(w_out_attn[0], g_woa, m_w_out_attn[0], v_w_out_attn[0], "adamw_w_out_attn"),
        "w_o": _adamw(w_o[0], g_wo, m_w_o[0], v_w_o[0], "adamw_w_o"),
        "final_g": _adamw(final_g.reshape(1, D_MODEL), small[6:7], m_final_g.reshape(1, D_MODEL),
                          v_final_g.reshape(1, D_MODEL), "adamw_final_g"),
    }
    behind = [grad_x] + [u[0] for u in upd.values()]
    in_reduced = _add_chips(d_w4, from_sibling, to_chips.wait(behind)[1][0], where)

    share_in = _share_halves("share_in_grad", [in_reduced])
    w_in_args = (w_in[0], m_w_in[0], v_w_in[0])
    own_rows = _adamw_half(w_in_args[0], share_in.bufs[0], *w_in_args[1:], half, "adamw_w_in_own_rows")
    (g_w_in,) = share_in.wait(own_rows[0])[0]
    upd["w_in"] = _adamw_half(w_in_args[0], g_w_in, *w_in_args[1:], 1 - half, "adamw_w_in_sibling_rows",
                              other=own_rows)

    names = ["norm_g", "w_in", "b_merge", "conv_w", "w_out_conv", "w_out_attn", "w_o", "final_g"]
    shapes = [norm_g.shape, w_in.shape, b_merge.shape, conv_w.shape, w_out_conv.shape, w_out_attn.shape,
              w_o.shape, final_g.shape]
    outs = [loss, grad_x.reshape(1, seq, D_MODEL)]
    for k in (3, 0, 1, 2):
        outs += [upd[n][k].reshape(s) for n, s in zip(names, shapes)]
    return tuple(outs)
```

```python
import functools

import numpy as np
import jax
import jax.numpy as jnp
from jax import lax
from jax.experimental import pallas as pl
from jax.experimental.pallas import tpu as pltpu

F32 = jnp.float32
BF16 = jnp.bfloat16
MXU_DTYPE = jnp.bfloat16
ACT_DTYPE = jnp.bfloat16

D_MODEL = 1024
N_HEADS = 16
HEAD_DIM = 64
QB = 128
N_RES = 16
LANES = 128
HP = N_HEADS * HEAD_DIM // LANES
IN_COLS = 10 * D_MODEL
SHARD_COLS = IN_COLS // 4
EPS = 1e-6
NEG = -1e30

ADAM_LR, ADAM_B1, ADAM_B2, ADAM_EPS, ADAM_WD, ADAM_STEP = 0.001, 0.9, 0.999, 1e-08, 0.01, 10

PATTERNS = {1: (16, 8), 4: (4, 32), 16: (1, 128)}

_NN = (((1,), (0,)), ((), ()))
_NT = (((1,), (1,)), ((), ()))


def _dot(a, b):
    return lax.dot_general(a.astype(MXU_DTYPE), b.astype(MXU_DTYPE), _NN, preferred_element_type=F32)


def _dot_nt(a, b):
    return lax.dot_general(a.astype(MXU_DTYPE), b.astype(MXU_DTYPE), _NT, preferred_element_type=F32)


def _split3(x):
    hi = x.astype(BF16)
    r1 = x - hi.astype(F32)
    mid = r1.astype(BF16)
    lo = (r1 - mid.astype(F32)).astype(BF16)
    return hi, mid, lo


def _select_cols(x, sel, terms):
    return sum(lax.dot_general(t, sel, _NN, preferred_element_type=F32) for t in _split3(x)[:terms])


def _sigmoid(z):
    return 1.0 / (1.0 + jnp.exp(-z))


def _head_expand_matrix():
    e = np.zeros((LANES, D_MODEL), np.float32)
    for h in range(N_HEADS):
        e[8 * h, HEAD_DIM * h:HEAD_DIM * (h + 1)] = 1.0
    return jnp.asarray(e, BF16)


def _head_sum_matrix():
    e = np.zeros((D_MODEL, LANES), np.float32)
    for h in range(N_HEADS):
        e[HEAD_DIM * h:HEAD_DIM * (h + 1), 8 * h:8 * (h + 1)] = 1.0
    return jnp.asarray(e, BF16)


def _attn_tables(d):
    g_n, rq = PATTERNS[d]
    q_n = g_n * rq
    gq, iq = np.arange(q_n) // rq, np.arange(q_n) % rq

    def tab(kn, base):
        k_n = g_n * kn
        gk, jk = np.arange(k_n) // kn, np.arange(k_n) % kn
        delta = g_n * (base + iq[:, None] - jk[None, :]) + gq[:, None] - gk[None, :]
        valid = (delta >= 0) & (delta <= QB)
        dist = np.where(valid, d * delta, 0).astype(np.float32)
        madd = np.where(valid, 0.0, NEG).astype(np.float32)
        return dist, madd

    d0, m0 = tab(rq if g_n == 1 else 2 * rq, 0)
    d1, m1 = tab(2 * rq, rq)
    return d0, m0, d1, m1


def _alibi_slopes():
    return jnp.exp2(-8.0 * jnp.arange(1, N_HEADS + 1, dtype=F32) / N_HEADS)


def _to_residue_major(x, tgt, after=None):
    s_n, c_n = x.shape
    lr = s_n // N_RES
    extra = [] if after is None else [after]

    def body(x_ref, t_ref, *rest):
        xo_ref, to_ref = rest[-2:]
        for r in range(N_RES):
            xo_ref[r] = x_ref[pl.ds(r, lr, stride=N_RES), :]
            to_ref[r] = t_ref[pl.ds(r, lr, stride=N_RES), :]

    nat = pl.BlockSpec((s_n, LANES), lambda j: (0, j))
    res = pl.BlockSpec((N_RES, lr, LANES), lambda j: (0, 0, j))
    xo, to = pl.pallas_call(
        body, grid=(c_n // LANES,),
        in_specs=[nat, nat] + [pl.BlockSpec((8, LANES), lambda j: (0, 0))] * len(extra),
        out_specs=[res, res],
        out_shape=[jax.ShapeDtypeStruct((N_RES, lr, c_n), F32)] * 2,
        name="perm_in",
    )(x, tgt, *extra)
    return xo.reshape(s_n, c_n), to.reshape(s_n, c_n)


def _to_natural(gx_lo, gx_hi):
    half_rows, c_n = gx_lo.shape
    lr = half_rows // (N_RES // 2)

    def body(lo_ref, hi_ref, o_ref):
        for r in range(N_RES):
            o_ref[pl.ds(r, lr, stride=N_RES), :] = lo_ref[r] if r < N_RES // 2 else hi_ref[r - N_RES // 2]

    half = pl.BlockSpec((N_RES // 2, lr, LANES), lambda j: (0, 0, j))
    return pl.pallas_call(
        body, grid=(c_n // LANES,),
        in_specs=[half, half],
        out_specs=pl.BlockSpec((2 * half_rows, LANES), lambda j: (0, j)),
        out_shape=jax.ShapeDtypeStruct((2 * half_rows, c_n), F32),
        name="perm_out",
    )(gx_lo.reshape(N_RES // 2, lr, c_n), gx_hi.reshape(N_RES // 2, lr, c_n))


def _rms_in(xp, norm_g):
    s_n, c_n = xp.shape
    tm = 512

    def body(x_ref, g_ref, u_ref, ut_ref):
        x = x_ref[...]
        r = lax.rsqrt(jnp.mean(x * x, axis=-1, keepdims=True) + EPS)
        u = x * r * g_ref[...]
        u_ref[...] = u.astype(u_ref.dtype)
        ut_ref[...] = u.T.astype(ut_ref.dtype)

    return pl.pallas_call(
        body, grid=(s_n // tm,),
        in_specs=[pl.BlockSpec((tm, c_n), lambda i: (i, 0)), pl.BlockSpec((1, c_n), lambda i: (0, 0))],
        out_specs=[pl.BlockSpec((tm, c_n), lambda i: (i, 0)), pl.BlockSpec((c_n, tm), lambda i: (0, i))],
        out_shape=[jax.ShapeDtypeStruct((s_n, c_n), ACT_DTYPE), jax.ShapeDtypeStruct((c_n, s_n), ACT_DTYPE)],
        name="rms_in",
    )(xp, norm_g)


def _in_proj(u, chip, w_own=None, w4=None, partial=None, others=()):
    s_n = u.shape[0]
    tn, cm = 512, 512
    per = SHARD_COLS // tn
    own = partial is None

    def body(chip_ref, a_ref, b_ref, *rest):
        o_ref = rest[-1]
        b = b_ref[...]
        for c in range(s_n // cm):
            o_ref[c * cm:(c + 1) * cm, :] = _dot(a_ref[c * cm:(c + 1) * cm, :], b).astype(o_ref.dtype)

    def shard(n, chip_ref):
        if own:
            return chip_ref[0]
        mask = others[-1]
        for i, m in enumerate(others[:-1]):
            mask = jnp.where(n // per == i, m, mask)
        return jnp.bitwise_xor(chip_ref[0], mask)

    w_spec = (pl.BlockSpec((D_MODEL, tn), lambda n, c: (0, n)) if own else
              pl.BlockSpec((None, D_MODEL, tn), lambda n, c: (shard(n, c), 0, n % per)))
    return pl.pallas_call(
        body,
        grid_spec=pltpu.PrefetchScalarGridSpec(
            num_scalar_prefetch=1, grid=(per if own else len(others) * per,),
            in_specs=[pl.BlockSpec((s_n, D_MODEL), lambda n, c: (0, 0)), w_spec] + ([] if own else [_ANY]),
            out_specs=pl.BlockSpec((s_n, tn), lambda n, c: (0, shard(n, c) * per + n % per))),
        out_shape=jax.ShapeDtypeStruct((s_n, IN_COLS), ACT_DTYPE),
        input_output_aliases={} if own else {3: 0},
        name="in_proj_own" if own else "in_proj_" + "_".join(str(m) for m in others),
    )(*([chip, u, w_own] if own else [chip, u, w4, partial]))


def _conv_terms(xc_ref, cg_ref, r, row, lr, cache):
    def a_of(q):
        if q not in cache:
            cache[q] = cg_ref[q].astype(F32) * xc_ref[q].astype(F32)
        return cache[q]

    def shift_down(v):
        return jnp.where(row >= 1, pltpu.roll(v, 1, 0), 0.0)

    a = a_of(r)
    am1 = a_of(r - 1) if r >= 1 else shift_down(a_of(N_RES - 1))
    am2 = a_of(r - 2) if r >= 2 else shift_down(a_of(N_RES - 2 + r))
    return a, am1, am2


def _conv_fwd(proj, conv_w):
    s_n = proj.shape[0]
    lr = s_n // N_RES
    pv = proj.reshape(N_RES, lr, IN_COLS)

    def body(xc_ref, bg_ref, cg_ref, zc_ref, w_ref, hc_ref, hct_ref):
        w = w_ref[...]
        row = lax.broadcasted_iota(jnp.int32, (lr, LANES), 0)
        products = {}
        for r in range(N_RES):
            a, am1, am2 = _conv_terms(xc_ref, cg_ref, r, row, lr, products)
            c = w[0:1] * am2 + w[1:2] * am1 + w[2:3] * a
            z = zc_ref[r].astype(F32)
            hc = z * _sigmoid(z) * bg_ref[r].astype(F32) * c
            hc_ref[r] = hc.astype(hc_ref.dtype)
            hct_ref[:, r * lr:(r + 1) * lr] = hc.T.astype(hct_ref.dtype)

    def col(part):
        return pl.BlockSpec((N_RES, lr, LANES), lambda j: (0, 0, part * 8 + j))

    hc, hct = pl.pallas_call(
        body, grid=(D_MODEL // LANES,),
        in_specs=[col(0), col(1), col(2), col(3), pl.BlockSpec((3, LANES), lambda j: (0, j))],
        out_specs=[pl.BlockSpec((N_RES, lr, LANES), lambda j: (0, 0, j)),
                   pl.BlockSpec((LANES, s_n), lambda j: (j, 0))],
        out_shape=[jax.ShapeDtypeStruct((N_RES, lr, D_MODEL), ACT_DTYPE),
                   jax.ShapeDtypeStruct((D_MODEL, s_n), ACT_DTYPE)],
        name="conv_fwd",
    )(pv, pv, pv, pv, conv_w)
    return hc.reshape(s_n, D_MODEL), hct


RES_PER_STEP = 8
CLASSES_PER_STEP = 2
ATTN_BATCH = 16

_BNT = (((2,), (2,)), ((0,), (0,)))
_BNN = (((2,), (1,)), ((0,), (0,)))


def _bdot(a, b, dims):
    return lax.dot_general(a.astype(MXU_DTYPE), b.astype(MXU_DTYPE), dims, preferred_element_type=F32)


def _pattern_view_shape(s_n, c_n, g_n, lead=()):
    lr = s_n // N_RES
    return (*lead, 4, 4, lr, c_n) if g_n == 4 else (*lead, N_RES, lr, c_n)


def _pattern_view(a, g_n, lead=()):
    return a.reshape(_pattern_view_shape(a.shape[-2], a.shape[-1], g_n, lead))


def _pattern_block(g_n, lr):
    if g_n == 4:
        return (4, CLASSES_PER_STEP, lr, LANES)
    return (16 if g_n == 16 else RES_PER_STEP, lr, LANES)


def _pattern_grid(g_n):
    return ({1: N_RES // RES_PER_STEP, 4: 4 // CLASSES_PER_STEP, 16: 1}[g_n], HP)


def _pattern_spec(g_n, lr, col_of_hp, lead=()):
    z = (0,) * len(lead)
    block = (*lead, *_pattern_block(g_n, lr))
    if g_n == 16:
        return pl.BlockSpec(block, lambda r, hp: (*z, 0, 0, col_of_hp(hp)))
    if g_n == 4:
        return pl.BlockSpec(block, lambda r, hp: (*z, 0, r, 0, col_of_hp(hp)))
    return pl.BlockSpec(block, lambda r, hp: (*z, r, 0, col_of_hp(hp)))


def _aligned(start, m):
    return start if isinstance(start, int) else pl.multiple_of(start, m)


class _Units:
    def __init__(self, g_n, rq):
        self.g_n, self.rq = g_n, rq
        self.per_res, self.paired = g_n == 1, rq == 8

    def plan(self, lr, size):
        if self.per_res:
            return [0], lr // self.rq - 1, lambda j: [pl.multiple_of(j * self.rq, self.rq)]
        if self.paired:
            per = min(size // 2, lr // 16)
            assert (lr // 16) % per == 0
            return ([i * 16 for i in range(per)], lr // 16 // per - 1,
                    lambda j: [pl.multiple_of((j * per + i) * 16, 16) for i in range(per)])
        step, classes = self.rq, range(CLASSES_PER_STEP)
        per = min(size // CLASSES_PER_STEP, lr // step)
        assert (lr // step) % per == 0
        return ([(c, i * step) for c in classes for i in range(per)], lr // step // per - 1,
                lambda j: [(c, pl.multiple_of((j * per + i) * step, step)) for c in classes for i in range(per)])

    def count(self, qs):
        return RES_PER_STEP if self.per_res else len(qs) * (2 if self.paired else 1)

    def _split(self, tiles, lo, rows):
        return tiles[:, lo:lo + rows].reshape(self.g_n * rows, LANES)

    def load_q(self, ref, qs):
        rq = self.rq
        if self.per_res:
            return ref[:, pl.ds(qs[0], rq), :]
        if self.paired:
            tiles = [ref[:, pl.ds(q, 16), :].astype(F32) for q in qs]
            return jnp.stack([self._split(t, lo, 8) for t in tiles for lo in (0, 8)])
        return jnp.stack([ref[:, c, pl.ds(q, rq), :].reshape(self.g_n * rq, LANES) for c, q in qs])

    def _key_rows(self, q, at_start):
        return (0, 2 * self.rq) if at_start and q == 0 else (_aligned(q - self.rq, self.rq), 2 * self.rq)

    def load_k(self, ref, qs, first):
        rq = self.rq
        if self.per_res:
            return ref[:, pl.ds(0, rq), :] if first else ref[:, pl.ds(_aligned(qs[0] - rq, rq), 2 * rq), :]
        if self.paired:
            out = []
            for i, q in enumerate(qs):
                if first and i == 0:
                    t = ref[:, 0:16, :].astype(F32)
                    out += [self._split(t, 0, 16)] * 2
                else:
                    t = ref[:, pl.ds(_aligned(q - 16, 16), 32), :].astype(F32)
                    out += [self._split(t, 8, 16), self._split(t, 16, 16)]
            return jnp.stack(out)
        rows = [(c, *self._key_rows(q, first)) for c, q in qs]
        return jnp.stack([ref[:, c, pl.ds(k0, n), :].reshape(self.g_n * n, LANES) for c, k0, n in rows])

    def store_q(self, ref, qs, val, add=False, lead=()):
        if self.per_res:
            pieces = [((), qs[0], self.rq, val)]
        elif self.paired:
            pieces = [((), q, 16, jnp.concatenate([val[2 * i].reshape(self.g_n, 8, LANES),
                                                   val[2 * i + 1].reshape(self.g_n, 8, LANES)], axis=1))
                      for i, q in enumerate(qs)]
        else:
            pieces = [((c,), q, self.rq, val[i].reshape(self.g_n, self.rq, LANES)) for i, (c, q) in enumerate(qs)]
        for cls, start, rows, v in pieces:
            idx = (*lead, slice(None), *cls, pl.ds(start, rows), slice(None))
            ref[idx] = (ref[idx] + v if add else v).astype(ref.dtype)

    def add_k(self, ref, qs, val, first):
        rq = self.rq
        if self.per_res:
            k0, n = (0, rq) if first else (_aligned(qs[0] - rq, rq), 2 * rq)
            ref[:, pl.ds(k0, n), :] += val
            return
        if self.paired:
            starts = [s for i, q in enumerate(qs)
                      for s in ((0, 0) if first and i == 0 else (_aligned(q - 8, 8), q))]
            rows = [((), s, 16) for s in starts]
        else:
            rows = [((c,), *self._key_rows(q, first)) for c, q in qs]
        for b, (cls, k0, n) in enumerate(rows):
            idx = (slice(None), *cls, pl.ds(k0, n), slice(None))
            ref[idx] += val[b].reshape(self.g_n, n, LANES)


def _batch_bias(un, qs, at_start, first_ref, general_ref):
    if not at_start:
        return general_ref[...][None]
    if un.per_res:
        return first_ref[...][None]
    if un.paired:
        return jnp.concatenate([first_ref[...][None]] + [general_ref[...][None]] * (un.count(qs) - 1), axis=0)
    return jnp.stack([(first_ref if q == 0 else general_ref)[...] for _, q in qs])


def _stack_heads(x, low):
    zero = jnp.zeros_like(x)
    return jnp.concatenate([jnp.where(low, x, zero), jnp.where(low, zero, x)], axis=1)


def _attn_fwd(proj, slopes, d):
    g_n, rq = PATTERNS[d]
    un = _Units(g_n, rq)
    s_n = proj.shape[0]
    lr = s_n // N_RES
    q_n = g_n * rq
    d0, m0, d1, m1 = _attn_tables(d)
    first, n_more, later = un.plan(lr, ATTN_BATCH)

    def body(sl_ref, q_ref, k_ref, v_ref, d0_ref, m0_ref, d1_ref, m1_ref, o_ref, lse_ref, b0_ref, b1_ref):
        hp = pl.program_id(1)

        @pl.when(hp == 0)
        def _():
            lse_ref[...] = jnp.zeros(lse_ref.shape, F32)

        for h in (0, 1):
            slope = sl_ref[2 * hp + h]
            b0_ref[h * q_n:(h + 1) * q_n, :] = m0_ref[...] - slope * d0_ref[...]
            b1_ref[h * q_n:(h + 1) * q_n, :] = m1_ref[...] - slope * d1_ref[...]

        lane = lax.broadcasted_iota(jnp.int32, (1, q_n, LANES), 2)
        low = lane < HEAD_DIM
        grp = lane // 8

        def batch(qs, at_start):
            qq = _stack_heads(un.load_q(q_ref, qs) * 0.125, low)
            s = _bdot(qq, un.load_k(k_ref, qs, at_start), _BNT) + _batch_bias(un, qs, at_start, b0_ref, b1_ref)
            m = jnp.max(s, axis=2, keepdims=True)
            p = jnp.exp(s - m)
            l = jnp.sum(p, axis=2, keepdims=True)
            o = _bdot(p, un.load_k(v_ref, qs, at_start), _BNN) * (1.0 / l)
            lse = m + jnp.log(l)
            un.store_q(o_ref, qs, jnp.where(low, o[:, :q_n], o[:, q_n:]))
            upd = jnp.where(grp == 2 * hp, lse[:, :q_n], 0.0) + jnp.where(grp == 2 * hp + 1, lse[:, q_n:], 0.0)
            un.store_q(lse_ref, qs, upd, add=True)

        batch(first, True)

        def more(j, carry):
            batch(later(j), False)
            return carry

        lax.fori_loop(1, 1 + n_more, more, 0)

    pv = _pattern_view(proj, g_n)
    full = lambda a: pl.BlockSpec(a.shape, lambda r, hp: (0, 0))
    o, lse = pl.pallas_call(
        body, grid=_pattern_grid(g_n),
        in_specs=[pl.BlockSpec(memory_space=pltpu.SMEM),
                  _pattern_spec(g_n, lr, lambda hp: 32 + hp),
                  _pattern_spec(g_n, lr, lambda hp: 40 + hp),
                  _pattern_spec(g_n, lr, lambda hp: 48 + hp),
                  full(d0), full(m0), full(d1), full(m1)],
        out_specs=[_pattern_spec(g_n, lr, lambda hp: hp), _pattern_spec(g_n, lr, lambda hp: 0)],
        out_shape=[jax.ShapeDtypeStruct(_pattern_view_shape(s_n, D_MODEL, g_n), ACT_DTYPE),
                   jax.ShapeDtypeStruct(_pattern_view_shape(s_n, LANES, g_n), F32)],
        scratch_shapes=[pltpu.VMEM((2 * q_n, d0.shape[1]), F32), pltpu.VMEM((2 * q_n, 2 * q_n), F32)],
        name=f"attn_fwd_d{d}",
    )(slopes, pv, pv, pv, d0, m0, d1, m1)
    return o.reshape(s_n, D_MODEL), lse.reshape(s_n, LANES)


def _attn_combine(outs, lses, proj):
    s_n = proj.shape[0]
    tm = 512

    def body(o1_ref, o2_ref, o3_ref, l1_ref, l2_ref, l3_ref, za_ref, e_ref, o_ref, lse_ref, ha_ref, hat_ref):
        ls = [l1_ref[...], l2_ref[...], l3_ref[...]]
        mx = jnp.maximum(jnp.maximum(ls[0], ls[1]), ls[2])
        den = sum(jnp.exp(l - mx) for l in ls)
        lse = mx + jnp.log(den)
        lse_ref[...] = lse
        o = jnp.zeros((tm, D_MODEL), F32)
        for l, oref in zip(ls, (o1_ref, o2_ref, o3_ref)):
            o = o + _select_cols(jnp.exp(l - lse), e_ref[...], terms=2) * oref[...].astype(F32)
        o_ref[...] = o.astype(o_ref.dtype)
        z = za_ref[...].astype(F32)
        ha = z * _sigmoid(z) * o
        ha_ref[...] = ha.astype(ha_ref.dtype)
        hat_ref[...] = ha.T.astype(hat_ref.dtype)

    row = lambda w: pl.BlockSpec((tm, w), lambda i: (i, 0))
    return pl.pallas_call(
        body, grid=(s_n // tm,),
        in_specs=[row(D_MODEL)] * 3 + [row(LANES)] * 3
        + [pl.BlockSpec((tm, D_MODEL), lambda i: (i, 7)), pl.BlockSpec((LANES, D_MODEL), lambda i: (0, 0))],
        out_specs=[row(D_MODEL), row(LANES), row(D_MODEL), pl.BlockSpec((D_MODEL, tm), lambda i: (0, i))],
        out_shape=[jax.ShapeDtypeStruct((s_n, D_MODEL), ACT_DTYPE), jax.ShapeDtypeStruct((s_n, LANES), F32),
                   jax.ShapeDtypeStruct((s_n, D_MODEL), ACT_DTYPE), jax.ShapeDtypeStruct((D_MODEL, s_n), ACT_DTYPE)],
        name="attn_combine",
    )(*outs, *lses, proj, _head_expand_matrix())


CHAIN_ROWS = 256


def _row_chains(tm):
    return [slice(r, r + CHAIN_ROWS) for r in range(0, tm, CHAIN_ROWS)]


def _gates(gc_ref, ga_ref, b_ref, rows):
    b = b_ref[...]
    gc = _sigmoid(gc_ref[rows, :].astype(F32) + b[:, :D_MODEL])
    ga = _sigmoid(ga_ref[rows, :].astype(F32) + b[:, D_MODEL:])
    return gc, ga


def _merge_loss(hc, ha, woc, woa, wo, proj, b_merge, xp, final_g, tgt):
    s_n = xp.shape[0]
    tm = 512

    def body(hc_ref, ha_ref, woc_ref, woa_ref, wo_ref, gc_ref, ga_ref, b_ref, x_ref, gf_ref, t_ref,
             yc_ref, ya_ref, dhb_ref, dgf_ref, loss_ref, dwo_ref, mgt_ref):
        i = pl.program_id(0)

        @pl.when(i == 0)
        def _():
            dgf_ref[...] = jnp.zeros(dgf_ref.shape, F32)
            loss_ref[...] = jnp.zeros(loss_ref.shape, F32)
            dwo_ref[...] = jnp.zeros(dwo_ref.shape, F32)

        gf = gf_ref[...]
        for rows in _row_chains(tm):
            yc = _dot(hc_ref[rows, :], woc_ref[...])
            ya = _dot(ha_ref[rows, :], woa_ref[...])
            gc, ga = _gates(gc_ref, ga_ref, b_ref, rows)
            mg = gc * yc + ga * ya
            yc_ref[rows, :] = yc.astype(yc_ref.dtype)
            ya_ref[rows, :] = ya.astype(ya_ref.dtype)
            mgt_ref[:, rows] = mg.T.astype(mgt_ref.dtype)
            h2 = x_ref[rows, :] + _dot(mg, wo_ref[...])
            r2 = lax.rsqrt(jnp.mean(h2 * h2, axis=-1, keepdims=True) + EPS)
            nrm = h2 * r2
            err = nrm * gf - t_ref[rows, :]
            e2 = (err * err).reshape(-1, 8, D_MODEL).sum(axis=0)
            loss_ref[...] += sum(e2[:, c * LANES:(c + 1) * LANES] for c in range(D_MODEL // LANES))
            dy = err * (1.0 / D_MODEL)
            dgf_ref[...] += jnp.sum(dy * nrm, axis=0, keepdims=True)
            dn = dy * gf
            dh2 = r2 * (dn - nrm * jnp.mean(dn * nrm, axis=-1, keepdims=True))
            dhb_ref[rows, :] = dh2.astype(dhb_ref.dtype)
        dwo_ref[...] += _dot(mgt_ref[...], dhb_ref[...])

    row = pl.BlockSpec((tm, D_MODEL), lambda i: (i, 0))
    wsp = pl.BlockSpec((D_MODEL, D_MODEL), lambda i: (0, 0), pipeline_mode=pl.Buffered(1))
    vec = lambda w: pl.BlockSpec((1, w), lambda i: (0, 0))
    act = jax.ShapeDtypeStruct((s_n, D_MODEL), ACT_DTYPE)
    return pl.pallas_call(
        body, grid=(s_n // tm,),
        in_specs=[row, row, wsp, wsp, wsp,
                  pl.BlockSpec((tm, D_MODEL), lambda i: (i, 8)), pl.BlockSpec((tm, D_MODEL), lambda i: (i, 9)),
                  vec(2 * D_MODEL), row, vec(D_MODEL), row],
        out_specs=[row, row, row, vec(D_MODEL), pl.BlockSpec((8, LANES), lambda i: (0, 0)),
                   pl.BlockSpec((D_MODEL, D_MODEL), lambda i: (0, 0))],
        out_shape=[act, act, act, jax.ShapeDtypeStruct((1, D_MODEL), F32), jax.ShapeDtypeStruct((8, LANES), F32),
                   jax.ShapeDtypeStruct((D_MODEL, D_MODEL), F32)],
        scratch_shapes=[pltpu.VMEM((D_MODEL, tm), MXU_DTYPE)],
        name="merge_loss",
    )(hc, ha, woc, woa, wo, proj, proj, b_merge, xp, final_g, tgt)


def _merge_bwd(dh2b, wo, woc, woa, yc, ya, proj, b_merge, o, hct, hat):
    s_n = dh2b.shape[0]
    tm = 512

    def body(dh_ref, wo_ref, woc_ref, woa_ref, yc_ref, ya_ref, gc_ref, ga_ref, b_ref, o_ref, za_ref, e_ref,
             hct_ref, hat_ref, dhc_ref, do_ref, dsum_ref, db3_ref, dbias_ref, dwoc_ref, dwoa_ref,
             dyc_ref, dya_ref):
        i = pl.program_id(0)

        @pl.when(i == 0)
        def _():
            dbias_ref[...] = jnp.zeros(dbias_ref.shape, F32)
            dwoc_ref[...] = jnp.zeros(dwoc_ref.shape, F32)
            dwoa_ref[...] = jnp.zeros(dwoa_ref.shape, F32)

        for rows in _row_chains(tm):
            dmg = _dot_nt(dh_ref[rows, :], wo_ref[...])
            gc, ga = _gates(gc_ref, ga_ref, b_ref, rows)
            dgc = dmg * yc_ref[rows, :].astype(F32) * gc * (1.0 - gc)
            dga = dmg * ya_ref[rows, :].astype(F32) * ga * (1.0 - ga)
            dbias_ref[:, :D_MODEL] += jnp.sum(dgc, axis=0, keepdims=True)
            dbias_ref[:, D_MODEL:] += jnp.sum(dga, axis=0, keepdims=True)
            dyc = dmg * gc
            dya = dmg * ga
            dyc_ref[rows, :] = dyc.astype(dyc_ref.dtype)
            dya_ref[rows, :] = dya.astype(dya_ref.dtype)
            dhc_ref[rows, :] = _dot_nt(dyc, woc_ref[...]).astype(dhc_ref.dtype)
            dha = _dot_nt(dya, woa_ref[...])
            z = za_ref[rows, :].astype(F32)
            sg = _sigmoid(z)
            ov = o_ref[rows, :].astype(F32)
            dout = dha * z * sg
            do_ref[rows, :] = dout.astype(do_ref.dtype)
            dsum_ref[rows, :] = _select_cols(dout * ov, e_ref[...], terms=2)
            db3_ref[0, rows, :] = (dha * ov * sg * (1.0 + z * (1.0 - sg))).astype(db3_ref.dtype)
            db3_ref[1, rows, :] = dgc.astype(db3_ref.dtype)
            db3_ref[2, rows, :] = dga.astype(db3_ref.dtype)
        dwoc_ref[...] += _dot(hct_ref[...], dyc_ref[...])
        dwoa_ref[...] += _dot(hat_ref[...], dya_ref[...])

    row = pl.BlockSpec((tm, D_MODEL), lambda i: (i, 0))
    col = pl.BlockSpec((D_MODEL, tm), lambda i: (0, i))
    wsp = pl.BlockSpec((D_MODEL, D_MODEL), lambda i: (0, 0), pipeline_mode=pl.Buffered(1))
    acc = pl.BlockSpec((D_MODEL, D_MODEL), lambda i: (0, 0))
    act = jax.ShapeDtypeStruct((s_n, D_MODEL), ACT_DTYPE)
    grad = jax.ShapeDtypeStruct((D_MODEL, D_MODEL), F32)
    return pl.pallas_call(
        body, grid=(s_n // tm,),
        in_specs=[row, wsp, wsp, wsp, row, row,
                  pl.BlockSpec((tm, D_MODEL), lambda i: (i, 8)), pl.BlockSpec((tm, D_MODEL), lambda i: (i, 9)),
                  pl.BlockSpec((1, 2 * D_MODEL), lambda i: (0, 0)), row,
                  pl.BlockSpec((tm, D_MODEL), lambda i: (i, 7)), pl.BlockSpec((D_MODEL, LANES), lambda i: (0, 0)),
                  col, col],
        out_specs=[row, row, pl.BlockSpec((tm, LANES), lambda i: (i, 0)),
                   pl.BlockSpec((3, tm, D_MODEL), lambda i: (0, i, 0)),
                   pl.BlockSpec((1, 2 * D_MODEL), lambda i: (0, 0)), acc, acc],
        out_shape=[act, act, jax.ShapeDtypeStruct((s_n, LANES), F32),
                   jax.ShapeDtypeStruct((3, s_n, D_MODEL), ACT_DTYPE),
                   jax.ShapeDtypeStruct((1, 2 * D_MODEL), F32), grad, grad],
        scratch_shapes=[pltpu.VMEM((tm, D_MODEL), MXU_DTYPE), pltpu.VMEM((tm, D_MODEL), MXU_DTYPE)],
        name="merge_bwd",
    )(dh2b, wo, woc, woa, yc, ya, proj, proj, b_merge, o, proj, _head_sum_matrix(), hct, hat)


def _behind(body, after, n_prefetch=0):
    if after is None:
        return body, [], []

    def ordered(*refs):
        body(*refs[:n_prefetch], *refs[n_prefetch + 1:])

    return ordered, [_ANY], [after]


def _conv_bwd(proj, conv_w, dhc, after=None):
    s_n = proj.shape[0]
    lr = s_n // N_RES
    pv = proj.reshape(N_RES, lr, IN_COLS)

    def body(xc_ref, bg_ref, cg_ref, zc_ref, w_ref, dhc_ref, da4_ref, dw_ref, dc_ref):
        w = w_ref[...]
        row = lax.broadcasted_iota(jnp.int32, (lr, LANES), 0)
        dw = [jnp.zeros((1, LANES), F32) for _ in range(3)]
        products = {}
        for r in range(N_RES):
            a, am1, am2 = _conv_terms(xc_ref, cg_ref, r, row, lr, products)
            c = w[0:1] * am2 + w[1:2] * am1 + w[2:3] * a
            z = zc_ref[r].astype(F32)
            sg = _sigmoid(z)
            sz = z * sg
            bg = bg_ref[r].astype(F32)
            dh = dhc_ref[r].astype(F32)
            da4_ref[1, r] = (dh * sz * c).astype(da4_ref.dtype)
            da4_ref[3, r] = (dh * bg * c * sg * (1.0 + z * (1.0 - sg))).astype(da4_ref.dtype)
            dc = dh * sz * bg
            dc_ref[r] = dc
            dw[0] = dw[0] + jnp.sum(dc * am2, axis=0, keepdims=True)
            dw[1] = dw[1] + jnp.sum(dc * am1, axis=0, keepdims=True)
            dw[2] = dw[2] + jnp.sum(dc * a, axis=0, keepdims=True)
        dw_ref[0:1, :] = dw[0]
        dw_ref[1:2, :] = dw[1]
        dw_ref[2:3, :] = dw[2]

        def shift_up(v):
            return jnp.where(row < lr - 1, pltpu.roll(v, lr - 1, 0), 0.0)

        for r in range(N_RES):
            dp1 = dc_ref[r + 1] if r + 1 < N_RES else shift_up(dc_ref[0])
            dp2 = dc_ref[r + 2] if r + 2 < N_RES else shift_up(dc_ref[r + 2 - N_RES])
            da = w[2:3] * dc_ref[r] + w[1:2] * dp1 + w[0:1] * dp2
            da4_ref[0, r] = (da * cg_ref[r].astype(F32)).astype(da4_ref.dtype)
            da4_ref[2, r] = (da * xc_ref[r].astype(F32)).astype(da4_ref.dtype)

    def col(part):
        return pl.BlockSpec((N_RES, lr, LANES), lambda j: (0, 0, part * 8 + j))

    kern, token_spec, token = _behind(body, after)
    da4, dw = pl.pallas_call(
        kern, grid=(D_MODEL // LANES,),
        in_specs=token_spec + [col(0), col(1), col(2), col(3), pl.BlockSpec((3, LANES), lambda j: (0, j)),
                               pl.BlockSpec((N_RES, lr, LANES), lambda j: (0, 0, j))],
        out_specs=[pl.BlockSpec((4, N_RES, lr, LANES), lambda j: (0, 0, 0, j)),
                   pl.BlockSpec((3, LANES), lambda j: (0, j))],
        out_shape=[jax.ShapeDtypeStruct((4, N_RES, lr, D_MODEL), ACT_DTYPE),
                   jax.ShapeDtypeStruct((3, D_MODEL), F32)],
        scratch_shapes=[pltpu.VMEM((N_RES, lr, LANES), F32)],
        name="conv_bwd",
    )(*token, pv, pv, pv, pv, conv_w, dhc.reshape(N_RES, lr, D_MODEL))
    return da4.reshape(4, s_n, D_MODEL), dw


def _attn_bwd(proj, dout, lse, dsum, slopes, d, prev=None, after=None):
    g_n, rq = PATTERNS[d]
    un = _Units(g_n, rq)
    s_n = proj.shape[0]
    lr = s_n // N_RES
    q_n = g_n * rq
    d0, m0, d1, m1 = (np.ascontiguousarray(t.T) for t in _attn_tables(d))
    first, n_more, later = un.plan(lr, ATTN_BATCH)
    bsz = un.count(first)

    def body(sl_ref, q_ref, k_ref, v_ref, do_ref, lse_ref, ds_ref, d0_ref, m0_ref, d1_ref, m1_ref, *rest):
        prev_ref = rest[0] if prev is not None else None
        out_ref, b0_ref, b1_ref, lt_ref, dt_ref, dk_ref, dv_ref = rest[-7:]
        hp = pl.program_id(1)
        for h in (0, 1):
            slope = sl_ref[2 * hp + h]
            b0_ref[:, h * q_n:(h + 1) * q_n] = m0_ref[...] - slope * d0_ref[...]
            b1_ref[:, h * q_n:(h + 1) * q_n] = m1_ref[...] - slope * d1_ref[...]
        if prev is None:
            dk_ref[...] = jnp.zeros(dk_ref.shape, F32)
            dv_ref[...] = jnp.zeros(dv_ref.shape, F32)
        else:
            out_ref[0] = prev_ref[0]
            dk_ref[...] = prev_ref[1].astype(F32)
            dv_ref[...] = prev_ref[2].astype(F32)
        low = lax.broadcasted_iota(jnp.int32, (1, q_n, LANES), 2) < HEAD_DIM
        row16 = pl.multiple_of(16 * hp, 16)

        def query_rows(stat_ref, t_ref, qs):
            tiles = un.load_q(stat_ref, qs)
            for b in range(bsz):
                t_ref[b] = tiles[b].T
            t16 = t_ref[:, pl.ds(row16, 16), :]
            return jnp.concatenate([t16[:, 0:1, :], t16[:, 8:9, :]], axis=2)

        def batch(qs, at_start):
            qq = _stack_heads(un.load_q(q_ref, qs) * 0.125, low)
            dd = _stack_heads(un.load_q(do_ref, qs), low)
            ks = un.load_k(k_ref, qs, at_start)
            vs = un.load_k(v_ref, qs, at_start)
            lrow = query_rows(lse_ref, lt_ref, qs)
            drow = query_rows(ds_ref, dt_ref, qs)
            pt = jnp.exp(_bdot(ks, qq, _BNT) + _batch_bias(un, qs, at_start, b0_ref, b1_ref) - lrow)
            dst = pt * (_bdot(vs, dd, _BNT) - drow)
            un.add_k(dv_ref, qs, _bdot(pt, dd, _BNN), at_start)
            un.add_k(dk_ref, qs, _bdot(dst, qq, _BNN), at_start)
            dq = _bdot(jnp.swapaxes(dst, 1, 2), ks, _BNN)
            un.store_q(out_ref, qs, jnp.where(low, dq[:, :q_n], dq[:, q_n:]) * 0.125, add=prev is not None,
                       lead=(0,))

        batch(first, True)

        def more(j, carry):
            batch(later(j), False)
            return carry

        lax.fori_loop(1, 1 + n_more, more, 0)
        out_ref[1] = dk_ref[...].astype(out_ref.dtype)
        out_ref[2] = dv_ref[...].astype(out_ref.dtype)

    pv = _pattern_view(proj, g_n)
    full = lambda a: pl.BlockSpec(a.shape, lambda r, hp: (0, 0))
    whole = _pattern_spec(g_n, lr, lambda hp: hp, lead=(3,))
    kern, token_spec, token = _behind(body, after)
    out = pl.pallas_call(
        kern, grid=_pattern_grid(g_n),
        in_specs=token_spec + [pl.BlockSpec(memory_space=pltpu.SMEM),
                               _pattern_spec(g_n, lr, lambda hp: 32 + hp),
                               _pattern_spec(g_n, lr, lambda hp: 40 + hp),
                               _pattern_spec(g_n, lr, lambda hp: 48 + hp),
                               _pattern_spec(g_n, lr, lambda hp: hp),
                               _pattern_spec(g_n, lr, lambda hp: 0),
                               _pattern_spec(g_n, lr, lambda hp: 0),
                               full(d0), full(m0), full(d1), full(m1)] + ([] if prev is None else [whole]),
        out_specs=whole,
        out_shape=jax.ShapeDtypeStruct(_pattern_view_shape(s_n, D_MODEL, g_n, lead=(3,)), ACT_DTYPE),
        scratch_shapes=[pltpu.VMEM((d0.shape[0], 2 * q_n), F32), pltpu.VMEM((2 * q_n, 2 * q_n), F32),
                        pltpu.VMEM((bsz, LANES, q_n), F32), pltpu.VMEM((bsz, LANES, q_n), F32),
                        pltpu.VMEM(_pattern_block(g_n, lr), F32), pltpu.VMEM(_pattern_block(g_n, lr), F32)],
        name=f"attn_bwd_d{d}",
    )(*token, slopes, pv, pv, pv, _pattern_view(dout, g_n), _pattern_view(lse, g_n), _pattern_view(dsum, g_n),
      d0, m0, d1, m1, *([] if prev is None else [_pattern_view(prev, g_n, lead=(3,))]))
    return out.reshape(3, s_n, D_MODEL)


def _part_index(step, per, lo, n):
    return jnp.clip(step // per - lo, 0, n - 1)


def _dw_in(ut, da4, dc3, db3):
    s_n = ut.shape[1]
    tn = 512
    per = D_MODEL // tn
    shard_blocks = SHARD_COLS // tn

    def body(a_ref, p0_ref, p1_ref, p2_ref, o_ref):
        part = pl.program_id(0) // per

        @pl.when(part < 4)
        def _():
            o_ref[...] = _dot(a_ref[...], p0_ref[...])

        @pl.when((part >= 4) & (part < 7))
        def _():
            o_ref[...] = _dot(a_ref[...], p1_ref[...])

        @pl.when(part >= 7)
        def _():
            o_ref[...] = _dot(a_ref[...], p2_ref[...])

    def pspec(lo, n):
        def index(j):
            part = j // per
            col = jnp.where(part < lo, 0, jnp.where(part >= lo + n, per - 1, j % per))
            return _part_index(j, per, lo, n), 0, col
        return pl.BlockSpec((None, s_n, tn), index)

    return pl.pallas_call(
        body, grid=(IN_COLS // tn,),
        in_specs=[pl.BlockSpec((D_MODEL, s_n), lambda j: (0, 0), pipeline_mode=pl.Buffered(1)),
                  pspec(0, 4), pspec(4, 3), pspec(7, 3)],
        out_specs=pl.BlockSpec((None, D_MODEL, tn), lambda j: (j // shard_blocks, 0, j % shard_blocks)),
        out_shape=jax.ShapeDtypeStruct((4, D_MODEL, SHARD_COLS), F32),
        name="dw_in",
    )(ut, da4, dc3, db3)


def _input_grad(da4, dc3, db3, w4, xp, norm_g, dh2, row0, rows, after=None):
    tm, tk = 256, 512
    per = D_MODEL // tk
    shard_blocks = SHARD_COLS // tk
    m0 = row0 // tm

    def body(p0_ref, p1_ref, p2_ref, w_ref, x_ref, g_ref, dh_ref, gx_ref, dg_ref):
        @pl.when(pl.program_id(0) == 0)
        def _():
            dg_ref[...] = jnp.zeros(dg_ref.shape, F32)

        du = None
        for k in range(IN_COLS // tk):
            part, cols = k // per, pl.ds((k % per) * tk, tk)
            ref, slot = (p0_ref, part) if part < 4 else (p1_ref, part - 4) if part < 7 else (p2_ref, part - 7)
            d = _dot_nt(ref[slot, :, cols], w_ref[k // shard_blocks, :, pl.ds((k % shard_blocks) * tk, tk)])
            du = d if du is None else du + d
        x = x_ref[...]
        r = lax.rsqrt(jnp.mean(x * x, axis=-1, keepdims=True) + EPS)
        nrm = x * r
        dg_ref[...] += jnp.sum(du * nrm, axis=0, keepdims=True)
        dn = du * g_ref[...]
        gx_ref[...] = dh_ref[...].astype(F32) + r * (dn - nrm * jnp.mean(dn * nrm, axis=-1, keepdims=True))

    def pspec(n):
        return pl.BlockSpec((n, tm, D_MODEL), lambda m: (0, m0 + m, 0))

    row_in = pl.BlockSpec((tm, D_MODEL), lambda m: (m0 + m, 0))
    vec = pl.BlockSpec((1, D_MODEL), lambda m: (0, 0))
    kern, token_spec, token = _behind(body, after)
    return pl.pallas_call(
        kern, grid=(rows // tm,),
        in_specs=token_spec + [pspec(4), pspec(3), pspec(3),
                               pl.BlockSpec(w4.shape, lambda m: (0, 0, 0), pipeline_mode=pl.Buffered(1)),
                               row_in, vec, row_in],
        out_specs=[pl.BlockSpec((tm, D_MODEL), lambda m: (m, 0)), vec],
        out_shape=[jax.ShapeDtypeStruct((rows, D_MODEL), F32), jax.ShapeDtypeStruct((1, D_MODEL), F32)],
        name="input_grad",
    )(*token, da4, dc3, db3, w4, xp, norm_g, dh2)


class _Step:
    def __init__(self, x, tgt, norm_g, chip, after=None):
        self.norm_g, self.chip = norm_g, chip
        self.slopes = _alibi_slopes()
        self.xp, self.tp = _to_residue_major(x, tgt, after)
        self.u, self.ut = _rms_in(self.xp, norm_g)

    def project_own(self, w_own):
        self.proj_own = _in_proj(self.u, self.chip, w_own=w_own)

    def project(self, w4, others):
        self.proj_own = _in_proj(self.u, self.chip, w4=w4, partial=self.proj_own, others=others)

    def mixers(self, w4, taps):
        self.w4, self.taps, self.proj = w4, taps, self.proj_own
        self.hc, self.hct = _conv_fwd(self.proj, taps)
        fwd = [_attn_fwd(self.proj, self.slopes, d) for d in PATTERNS]
        self.o, self.lse, self.ha, self.hat = _attn_combine([f[0] for f in fwd], [f[1] for f in fwd], self.proj)

    def merge_and_loss(self, woc, woa, wo, b_merge, final_g):
        self.woc, self.woa, self.wo, self.b_merge = woc, woa, wo, b_merge
        (self.yc, self.ya, self.dh2b, self.d_final_g, self.loss8, self.d_wo) = _merge_loss(
            self.hc, self.ha, woc, woa, wo, self.proj, b_merge, self.xp, final_g, self.tp)

    def out_weight_grads(self):
        (self.dhc, self.dout, self.dsum, self.db3, self.d_bias, d_woc, d_woa) = _merge_bwd(
            self.dh2b, self.wo, self.woc, self.woa, self.yc, self.ya, self.proj, self.b_merge, self.o,
            self.hct, self.hat)
        return d_woc, d_woa, self.d_wo

    def conv_grads(self, after=None):
        self.da4, self.d_taps = _conv_bwd(self.proj, self.taps, self.dhc, after)

    def in_weight_grad(self, after=None):
        self.dc3 = None
        for d in PATTERNS:
            self.dc3 = _attn_bwd(self.proj, self.dout, self.lse, self.dsum, self.slopes, d, prev=self.dc3,
                                 after=after if self.dc3 is None else None)
        return _dw_in(self.ut, self.da4, self.dc3, self.db3)

    def input_grad(self, half, after=None):
        rows = self.xp.shape[0] // 2
        return _input_grad(self.da4, self.dc3, self.db3, self.w4, self.xp, self.norm_g, self.dh2b,
                           half * rows, rows, after)


def _local_grads(x, tgt, norm_g, w4, b_merge, conv_w, woc, woa, wo, final_g):
    st = _Step(x, tgt, norm_g, jnp.zeros((1,), jnp.int32))
    st.project_own(w4[0])
    st.project(w4, (2, 1))
    st.project(w4, (3,))
    st.mixers(w4, conv_w)
    st.merge_and_loss(woc, woa, wo, b_merge, final_g)
    d_woc, d_woa, d_wo = st.out_weight_grads()
    st.conv_grads()
    d_w4 = st.in_weight_grad()
    gx_lo, dg_lo = st.input_grad(0)
    gx_hi, dg_hi = st.input_grad(1)
    return (st.loss8, _to_natural(gx_lo, gx_hi), dg_lo + dg_hi, d_w4, st.d_bias, st.d_taps, d_woc, d_woa, d_wo,
            st.d_final_g)


MESH = pl.DeviceIdType.MESH
_CHIP_FLIPS = ((1, 0), (0, 1), (1, 1))
_ANY = pl.BlockSpec(memory_space=pl.ANY)


def _place():
    return lax.axis_index("x"), lax.axis_index("y"), lax.axis_index("c")


def _flip(v, f):
    return 1 - v if f else v


def _remote(src, dst, send_sems, recv_sems, k, device):
    return pltpu.make_async_remote_copy(src_ref=src, dst_ref=dst, send_sem=send_sems.at[k], recv_sem=recv_sems.at[k],
                                        device_id=device, device_id_type=MESH)


def _place_shard(w, chip, dtype):
    rows, cols = w.shape
    tm = min(rows, 128)

    def body(chip_ref, w_ref, o_ref):
        o_ref[0] = w_ref[...].astype(o_ref.dtype)

    return pl.pallas_call(
        body,
        grid_spec=pltpu.PrefetchScalarGridSpec(
            num_scalar_prefetch=1, grid=(rows // tm,),
            in_specs=[pl.BlockSpec((tm, cols), lambda i, chip_ref: (i, 0))],
            out_specs=pl.BlockSpec((1, tm, cols), lambda i, chip_ref: (chip_ref[0], i, 0))),
        out_shape=jax.ShapeDtypeStruct((4, rows, cols), dtype),
        name="place_shard",
    )(chip, w)


def _gather_copies_to(flips, whole=()):
    def copies(arrs, _, send_sems, recv_sems):
        x, y, c = _place()
        out = []
        for a, arr in enumerate(arrs):
            h = arr.shape[1] // 2
            mine = arr.at[2 * x + y] if a in whole else arr.at[2 * x + y, pl.ds(pl.multiple_of(c * h, 8), h)]
            for i, t in enumerate(flips):
                fx, fy = _CHIP_FLIPS[t]
                out.append(_remote(mine, mine, send_sems, recv_sems, len(flips) * a + i,
                                   (_flip(x, fx), _flip(y, fy), c)))
        return out
    return copies


def _forward_to_sibling(arrs, flips=(0, 1, 2)):
    n = len(arrs)

    def body(*refs):
        outs = refs[n:2 * n]
        send_sems, recv_sems = refs[2 * n:]
        x, y, c = _place()
        sibling = (x, y, 1 - c)
        started = []
        for a in range(n):
            h = outs[a].shape[1] // 2
            rows = pl.ds(pl.multiple_of(c * h, 8), h)
            for t in flips:
                fx, fy = _CHIP_FLIPS[t]
                landed = outs[a].at[2 * _flip(x, fx) + _flip(y, fy), rows]
                cp = _remote(landed, landed, send_sems, recv_sems, 3 * a + t, sibling)
                cp.start()
                started.append(cp)
        for a in range(n):
            h = outs[a].shape[1] // 2
            rows = pl.ds(pl.multiple_of((1 - c) * h, 8), h)
            for t in flips:
                fx, fy = _CHIP_FLIPS[t]
                handed = outs[a].at[2 * _flip(x, fx) + _flip(y, fy), rows]
                _remote(handed, handed, send_sems, recv_sems, 3 * a + t, sibling).wait_recv()
        for cp in started:
            cp.wait_send()

    return pl.pallas_call(
        body, in_specs=[_ANY] * n, out_specs=[_ANY] * n,
        out_shape=[jax.ShapeDtypeStruct(s.shape, s.dtype) for s in arrs],
        input_output_aliases={a: a for a in range(n)},
        scratch_shapes=[pltpu.SemaphoreType.DMA((3 * n,)), pltpu.SemaphoreType.DMA((3 * n,))],
        name="gathered_to_sibling_" + "".join(str(t) for t in flips),
    )(*arrs)


_HBM = pl.BlockSpec(memory_space=pltpu.HBM)
_SEM = pl.BlockSpec(memory_space=pltpu.SEMAPHORE)
_EFFECT = pltpu.SideEffectType.DATAFLOW_SIDE_EFFECTING


class _SplitExchange:
    def __init__(self, name, srcs, land_shapes, n_copies, copies, riders=()):
        self.name, self.n, self.nl, self.copies = name, len(srcs), len(land_shapes), copies
        n, nb = self.n, len(srcs) + len(land_shapes)
        lands = [lax.empty(s.shape, s.dtype) for s in land_shapes]
        bufs = [pltpu.with_memory_space_constraint(a, pltpu.HBM) for a in (*srcs, *lands, *riders)]
        na = len(bufs)

        def body(*refs):
            send_sems, recv_sems = refs[na], refs[na + 1]
            for cp in copies(refs[:n], refs[n:nb], send_sems, recv_sems):
                cp.start()
            refs[-1][...] = jnp.zeros(refs[-1].shape, F32)

        outs = pl.pallas_call(
            body, name=name + "_start",
            in_specs=[_HBM] * na,
            out_specs=[_SEM, _SEM] + [_HBM] * na + [pl.BlockSpec(memory_space=pltpu.VMEM)],
            out_shape=[pltpu.SemaphoreType.DMA((n_copies,)), pltpu.SemaphoreType.DMA((n_copies,))]
            + [pltpu.HBM(b.shape, b.dtype) for b in bufs] + [jax.ShapeDtypeStruct((8, LANES), F32)],
            input_output_aliases={i: 2 + i for i in range(na)},
            compiler_params=pltpu.CompilerParams(has_side_effects=_EFFECT),
        )(*bufs)
        self.sems, self.bufs, self.riders, self.token = outs[:2], outs[2:2 + nb], outs[2 + nb:2 + na], outs[-1]

    def wait(self, done, riders=(), bufs=None):
        n, nb, copies = self.n, self.n + self.nl, self.copies
        bufs = [*(self.bufs if bufs is None else bufs),
                *[pltpu.with_memory_space_constraint(a, pltpu.HBM) for a in riders]]
        na = len(bufs)
        done = list(done) if isinstance(done, (list, tuple)) else [done]

        def body(*refs):
            send_sems, recv_sems = refs[na], refs[na + 1]
            for cp in copies(refs[:n], refs[n:nb], send_sems, recv_sems):
                cp.wait_send()
                cp.wait_recv()

        outs = pl.pallas_call(
            body, name=self.name + "_wait",
            in_specs=[_HBM] * na + [_SEM, _SEM] + [_ANY] * len(done),
            out_specs=[_HBM] * na,
            out_shape=[pltpu.HBM(b.shape, b.dtype) for b in bufs],
            input_output_aliases={i: i for i in range(na)},
            compiler_params=pltpu.CompilerParams(has_side_effects=_EFFECT),
        )(*bufs, *self.sems, *done)
        return outs[:n], outs[n:nb], outs[nb:]


def _sibling_copies(srcs, lands, send_sems, recv_sems):
    x, y, c = _place()
    out = []
    for a, (src, land) in enumerate(zip(srcs, lands)):
        h = src.shape[1] // 2
        theirs = pl.ds(pl.multiple_of((1 - c) * h, 8), h)
        out.append(_remote(src.at[:, theirs], land, send_sems, recv_sems, a, (x, y, 1 - c)))
    return out


def _grads_to_sibling(name, grads):
    shapes = [jax.ShapeDtypeStruct((4, g.shape[1] // 2, g.shape[2]), g.dtype) for g in grads]
    return _SplitExchange(name, grads, shapes, len(grads), _sibling_copies)


def _chip_copies(srcs, lands, send_sems, recv_sems):
    x, y, c = _place()
    out = []
    for a, (src, land) in enumerate(zip(srcs, lands)):
        for t, (fx, fy) in enumerate(_CHIP_FLIPS):
            tx, ty = _flip(x, fx), _flip(y, fy)
            out.append(_remote(src.at[2 * tx + ty], land.at[t], send_sems, recv_sems, 3 * a + t, (tx, ty, c)))
    return out


def _grads_to_chips(name, parts):
    shapes = [jax.ShapeDtypeStruct((3, *p.shape[1:]), p.dtype) for p in parts]
    return _SplitExchange(name, parts, shapes, 3 * len(parts), _chip_copies)


def _add_halves(g, r, half):
    _, rows, cols = g.shape
    h = rows // 2
    tm = min(h, 128)
    nt = h // tm

    def body(half_ref, g_ref, r_ref, b_ref):
        b_ref[...] = (g_ref[...] + r_ref[...]).astype(b_ref.dtype)

    spec = pl.BlockSpec((1, tm, cols), lambda j, i, half_ref: (j, i, 0))
    return pl.pallas_call(
        body,
        grid_spec=pltpu.PrefetchScalarGridSpec(
            num_scalar_prefetch=1, grid=(4, nt),
            in_specs=[pl.BlockSpec((1, tm, cols), lambda j, i, half_ref: (j, half_ref[0] * nt + i, 0)), spec],
            out_specs=spec),
        out_shape=jax.ShapeDtypeStruct((4, h, cols), BF16),
        name="add_sibling_grads",
    )(half, g, r)


def _add_chips(g, r, recv, where, after=None):
    n = len(g)
    _, h, cols = r[0].shape
    tm = min(h, 128)
    nt = h // tm

    def body(where_ref, *refs):
        for a in range(n):
            g_ref, r_ref, recv_ref, out_ref = refs[a], refs[n + a], refs[2 * n + a], refs[3 * n + a]
            own = g_ref[0] + r_ref[0]
            out_ref[...] = ((own + recv_ref[0].astype(F32)) + recv_ref[1].astype(F32)) + recv_ref[2].astype(F32)

    kern, token_spec, token = _behind(body, after, n_prefetch=1)
    return pl.pallas_call(
        kern,
        grid_spec=pltpu.PrefetchScalarGridSpec(
            num_scalar_prefetch=1, grid=(nt,),
            in_specs=token_spec + [pl.BlockSpec((1, tm, cols), lambda i, w: (w[0], w[1] * nt + i, 0))] * n
            + [pl.BlockSpec((1, tm, cols), lambda i, w: (w[0], i, 0))] * n
            + [pl.BlockSpec((3, tm, cols), lambda i, w: (0, i, 0))] * n,
            out_specs=[pl.BlockSpec((tm, cols), lambda i, w: (w[1] * nt + i, 0))] * n),
        out_shape=[jax.ShapeDtypeStruct((2 * h, cols), F32)] * n,
        name="add_chip_grads",
    )(where, *token, *g, *r, *recv)


def _half_copies(srcs, _, send_sems, recv_sems):
    x, y, c = _place()
    out = []
    for a, src in enumerate(srcs):
        h = src.shape[0] // 2
        mine = src.at[pl.ds(pl.multiple_of(c * h, 8), h)]
        out.append(_remote(mine, mine, send_sems, recv_sems, a, (x, y, 1 - c)))
    return out


def _share_halves(name, shards):
    return _SplitExchange(name, shards, [], len(shards), _half_copies)


def _reduce_small(rows):
    cols = rows[0].shape[1]
    n = len(rows)
    assert sum(r.shape[0] for r in rows) <= 8

    def body(*refs):
        ins, out_ref = refs[:n], refs[n]
        vec_ref, gath_ref, send_sems, recv_sems = refs[n + 1:]
        x, y, c = _place()
        me = 4 * x + 2 * y + c
        vec_ref[...] = jnp.zeros(vec_ref.shape, F32)
        at = 0
        for r in ins:
            vec_ref[at:at + r.shape[0], :] = r[...]
            at += r.shape[0]
        copies = []
        for k in range(1, 8):
            peer = (_flip(x, (k >> 2) & 1), _flip(y, (k >> 1) & 1), _flip(c, k & 1))
            copies.append(_remote(vec_ref, gath_ref.at[me], send_sems, recv_sems, k - 1, peer))
        for cp in copies:
            cp.start()
        gath_ref[me] = vec_ref[...]
        for cp in copies:
            cp.wait()
        tot = gath_ref[0]
        for dev in range(1, 8):
            tot = tot + gath_ref[dev]
        out_ref[...] = tot
        out_ref[7:8, :] = jnp.zeros((1, cols), F32) + jnp.sum(tot[7:8, :])

    vm = pl.BlockSpec(memory_space=pltpu.VMEM)
    return pl.pallas_call(
        body, in_specs=[vm] * n, out_specs=vm,
        out_shape=jax.ShapeDtypeStruct((8, cols), F32),
        scratch_shapes=[pltpu.VMEM((8, cols), F32), pltpu.VMEM((8, 8, cols), F32),
                        pltpu.SemaphoreType.DMA((7,)), pltpu.SemaphoreType.DMA((7,))],
        name="reduce_small",
    )(*rows)


def _adamw_tile(w_ref, g_ref, m_ref, v_ref, d_ref, m2_ref, v2_ref, gout_ref):
    gr = g_ref[...]
    m2 = ADAM_B1 * m_ref[...] + (1.0 - ADAM_B1) * gr
    v2 = ADAM_B2 * v_ref[...] + (1.0 - ADAM_B2) * (gr * gr)
    m_hat = m2 / (1.0 - ADAM_B1 ** ADAM_STEP)
    v_hat = v2 / (1.0 - ADAM_B2 ** ADAM_STEP)
    d_ref[...] = -ADAM_LR * (m_hat / (jnp.sqrt(v_hat) + ADAM_EPS) + ADAM_WD * w_ref[...])
    m2_ref[...] = m2
    v2_ref[...] = v2
    gout_ref[...] = gr


def _adamw(w, g, m, v, name):
    rows, cols = w.shape
    tm = 128 if rows % 128 == 0 else rows

    def body(*refs):
        _adamw_tile(*refs)

    spec = pl.BlockSpec((tm, cols), lambda i: (i, 0))
    sds = jax.ShapeDtypeStruct((rows, cols), F32)
    return pl.pallas_call(body, grid=(rows // tm,), in_specs=[spec] * 4, out_specs=[spec] * 4,
                          out_shape=[sds] * 4, name=name)(w, g, m, v)


def _adamw_each(params, name):
    n = len(params)
    rows, cols = params[0][0].shape
    tm = 128

    def body(*refs):
        for a in range(n):
            _adamw_tile(*refs[4 * a:4 * a + 4], *refs[4 * (n + a):4 * (n + a) + 4])

    spec = pl.BlockSpec((tm, cols), lambda i: (i, 0))
    sds = jax.ShapeDtypeStruct((rows, cols), F32)
    outs = pl.pallas_call(body, grid=(rows // tm,), in_specs=[spec] * (4 * n), out_specs=[spec] * (4 * n),
                          out_shape=[sds] * (4 * n), name=name)(*[a for p in params for a in p])
    return [outs[4 * a:4 * a + 4] for a in range(n)]


def _adamw_half(w, g, m, v, half, name, other=()):
    rows, cols = w.shape
    tm = 128
    nt = rows // 2 // tm

    def body(half_ref, *refs):
        _adamw_tile(*refs[:4], *refs[4 + len(other):])

    spec = pl.BlockSpec((tm, cols), lambda i, h: (h[0] * nt + i, 0))
    sds = jax.ShapeDtypeStruct((rows, cols), F32)
    return pl.pallas_call(
        body,
        grid_spec=pltpu.PrefetchScalarGridSpec(
            num_scalar_prefetch=1, grid=(nt,), in_specs=[spec] * 4 + [_ANY] * len(other), out_specs=[spec] * 4),
        out_shape=[sds] * 4, input_output_aliases={5 + k: k for k in range(len(other))}, name=name,
    )(half, w, g, m, v, *other)


def kernel(x, norm_g, w_in, b_merge, conv_w, w_out_conv, w_out_attn, w_o, final_g, loss_target, m_norm_g, m_w_in, m_b_merge, m_conv_w, m_w_out_conv, m_w_out_attn, m_w_o, m_final_g, v_norm_g, v_w_in, v_b_merge, v_conv_w, v_w_out_conv, v_w_out_attn, v_w_o, v_final_g):
    mx, my, mc = _place()
    chip = (2 * mx + my).astype(jnp.int32)
    seq = x.shape[1]

    chip1 = chip.reshape(1)
    slots = [_place_shard(w[0], chip1, MXU_DTYPE) for w in (w_in, w_out_conv, w_out_attn, w_o)]
    taps_slot = _place_shard(jnp.pad(conv_w[0], ((0, 5), (0, 0))), chip1, F32)
    gather_near = _SplitExchange("gather_w_in_near", [slots[0], taps_slot], [], 4,
                                 _gather_copies_to((0, 1), whole=(1,)))
    st = _Step(x[0], loss_target[0], norm_g, chip1, after=gather_near.token)
    st.project_own(w_in[0])
    near, _, _ = gather_near.wait([st.ut, st.proj_own])
    gather_far = _SplitExchange("gather_w_in_far", near, [], 2, _gather_copies_to((2,), whole=(1,)))
    (w4,) = _forward_to_sibling(gather_far.bufs[:1], flips=(0, 1))
    st.project(w4, (2, 1))
    (w4, taps4), _, out_slots = gather_far.wait([st.proj_own], riders=slots[1:], bufs=[w4, gather_far.bufs[1]])
    gather_out = _SplitExchange("gather_w_out", out_slots, [], 9, _gather_copies_to((0, 1, 2)), riders=[w4])
    (w4,) = _forward_to_sibling(gather_out.riders, flips=(2,))
    st.project(w4, (3,))
    st.mixers(w4, jnp.concatenate([taps4[j, :3, :] for j in range(4)], axis=1))
    out_ws, _, _ = gather_out.wait(st.o)
    woc, woa, wo = [w.reshape(D_MODEL, D_MODEL) for w in _forward_to_sibling(out_ws)]
    st.merge_and_loss(woc, woa, wo, b_merge, final_g.reshape(1, D_MODEL))

    half = mc.astype(jnp.int32).reshape(1)
    where = jnp.stack([chip, mc.astype(jnp.int32)])
    out_grads = [g.reshape(4, -1, D_MODEL) for g in st.out_weight_grads()]
    to_sibling = _grads_to_sibling("out_grads_to_sibling", out_grads)
    st.conv_grads(after=to_sibling.token)
    out_grads, out_from_sibling, _ = to_sibling.wait(st.da4)
    to_chips = _grads_to_chips("out_grads_to_chips",
                               [_add_halves(g, r, half) for g, r in zip(out_grads, out_from_sibling)])
    d_w4 = st.in_weight_grad(after=to_chips.token)
    out_from_chips = to_chips.wait(st.dc3)[1]

    to_sibling = _grads_to_sibling("in_grad_to_sibling", [d_w4])
    gx_lo, dg_lo = st.input_grad(0, after=to_sibling.token)
    (d_w4,), (from_sibling,), _ = to_sibling.wait(gx_lo)
    to_chips = _grads_to_chips("in_grad_to_chips", [_add_halves(d_w4, from_sibling, half)])

    out_reduced = _add_chips(out_grads, out_from_sibling, out_from_chips, where, after=to_chips.token)
    share_out = _share_halves("share_out_grads", out_reduced)
    gx_hi, dg_hi = st.input_grad(1, after=share_out.token)
    grad_x = _to_natural(gx_lo, gx_hi)
    g_woc, g_woa, g_wo = share_out.wait(gx_hi)[0]

    small = _reduce_small([dg_lo + dg_hi, st.d_bias.reshape(2, D_MODEL), st.d_taps, st.d_final_g,
                           st.loss8.reshape(1, D_MODEL)])
    loss = (0.5 / D_MODEL) * small[7, 0]
    g_taps = lax.dynamic_slice(small[3:6], (0, chip * (D_MODEL // 4)), (3, D_MODEL // 4))
    upd = {
        "norm_g": _adamw(norm_g, small[0:1], m_norm_g, v_norm_g, "adamw_norm_g"),
        "b_merge": _adamw(b_merge, small[1:3].reshape(1, 2 * D_MODEL), m_b_merge, v_b_merge, "adamw_b_merge"),
        "conv_w": _adamw(conv_w[0], g_taps, m_conv_w[0], v_conv_w[0], "adamw_conv_w"),
        "final_g": _adamw(final_g.reshape(1, D_MODEL), small[6:7], m_final_g.reshape(1, D_MODEL),
                          v_final_g.reshape(1, D_MODEL), "adamw_final_g"),
    }
    upd["w_out_conv"], upd["w_out_attn"], upd["w_o"] = _adamw_each(
        [(w_out_conv[0], g_woc, m_w_out_conv[0], v_w_out_conv[0]),
         (w_out_attn[0], g_woa, m_w_out_attn[0], v_w_out_attn[0]), (w_o[0], g_wo, m_w_o[0], v_w_o[0])],
        "adamw_w_out")
    behind = [grad_x] + [u[0] for u in upd.values()]
    (in_reduced,) = _add_chips([d_w4], [from_sibling], [to_chips.wait(behind)[1][0]], where)

    share_in = _share_halves("share_in_grad", [in_reduced])
    w_in_args = (w_in[0], m_w_in[0], v_w_in[0])
    own_rows = _adamw_half(w_in_args[0], share_in.bufs[0], *w_in_args[1:], half, "adamw_w_in_own_rows")
    (g_w_in,) = share_in.wait(own_rows[0])[0]
    upd["w_in"] = _adamw_half(w_in_args[0], g_w_in, *w_in_args[1:], 1 - half, "adamw_w_in_sibling_rows",
                              other=own_rows)

    names = ["norm_g", "w_in", "b_merge", "conv_w", "w_out_conv", "w_out_attn", "w_o", "final_g"]
    shapes = [norm_g.shape, w_in.shape, b_merge.shape, conv_w.shape, w_out_conv.shape, w_out_attn.shape,
              w_o.shape, final_g.shape]
    outs = [loss, grad_x.reshape(1, seq, D_MODEL)]
    for k in (3, 0, 1, 2):
        outs += [upd[n][k].reshape(s) for n, s in zip(names, shapes)]
    return tuple(outs)
```

```python
import functools

import numpy as np
import jax
import jax.numpy as jnp
from jax import lax
from jax.experimental import pallas as pl
from jax.experimental.pallas import tpu as pltpu

F32 = jnp.float32
BF16 = jnp.bfloat16
MXU_DTYPE = jnp.bfloat16
ACT_DTYPE = jnp.bfloat16

D_MODEL = 1024
N_HEADS = 16
HEAD_DIM = 64
QB = 128
N_RES = 16
LANES = 128
HP = N_HEADS * HEAD_DIM // LANES
IN_COLS = 10 * D_MODEL
SHARD_COLS = IN_COLS // 4
EPS = 1e-6
NEG = -1e30

ADAM_LR, ADAM_B1, ADAM_B2, ADAM_EPS, ADAM_WD, ADAM_STEP = 0.001, 0.9, 0.999, 1e-08, 0.01, 10

PATTERNS = {1: (16, 8), 4: (4, 32), 16: (1, 128)}

_NN = (((1,), (0,)), ((), ()))
_NT = (((1,), (1,)), ((), ()))


def _dot(a, b):
    return lax.dot_general(a.astype(MXU_DTYPE), b.astype(MXU_DTYPE), _NN, preferred_element_type=F32)


def _dot_nt(a, b):
    return lax.dot_general(a.astype(MXU_DTYPE), b.astype(MXU_DTYPE), _NT, preferred_element_type=F32)


def _split3(x):
    hi = x.astype(BF16)
    r1 = x - hi.astype(F32)
    mid = r1.astype(BF16)
    lo = (r1 - mid.astype(F32)).astype(BF16)
    return hi, mid, lo


def _select_cols(x, sel, terms):
    return sum(lax.dot_general(t, sel, _NN, preferred_element_type=F32) for t in _split3(x)[:terms])


def _sigmoid(z):
    return 1.0 / (1.0 + jnp.exp(-z))


def _head_expand_matrix():
    e = np.zeros((LANES, D_MODEL), np.float32)
    for h in range(N_HEADS):
        e[8 * h, HEAD_DIM * h:HEAD_DIM * (h + 1)] = 1.0
    return jnp.asarray(e, BF16)


def _head_sum_matrix():
    e = np.zeros((D_MODEL, LANES), np.float32)
    for h in range(N_HEADS):
        e[HEAD_DIM * h:HEAD_DIM * (h + 1), 8 * h:8 * (h + 1)] = 1.0
    return jnp.asarray(e, BF16)


def _attn_tables(d):
    g_n, rq = PATTERNS[d]
    q_n = g_n * rq
    gq, iq = np.arange(q_n) // rq, np.arange(q_n) % rq

    def tab(kn, base):
        k_n = g_n * kn
        gk, jk = np.arange(k_n) // kn, np.arange(k_n) % kn
        delta = g_n * (base + iq[:, None] - jk[None, :]) + gq[:, None] - gk[None, :]
        valid = (delta >= 0) & (delta <= QB)
        dist = np.where(valid, d * delta, 0).astype(np.float32)
        madd = np.where(valid, 0.0, NEG).astype(np.float32)
        return dist, madd

    d0, m0 = tab(rq if g_n == 1 else 2 * rq, 0)
    d1, m1 = tab(2 * rq, rq)
    return d0, m0, d1, m1


def _alibi_slopes():
    return jnp.exp2(-8.0 * jnp.arange(1, N_HEADS + 1, dtype=F32) / N_HEADS)


def _to_residue_major(x, tgt, after=None):
    s_n, c_n = x.shape
    lr = s_n // N_RES
    extra = [] if after is None else [after]

    def body(x_ref, t_ref, *rest):
        xo_ref, to_ref = rest[-2:]
        for r in range(N_RES):
            xo_ref[r] = x_ref[pl.ds(r, lr, stride=N_RES), :]
            to_ref[r] = t_ref[pl.ds(r, lr, stride=N_RES), :]

    nat = pl.BlockSpec((s_n, LANES), lambda j: (0, j))
    res = pl.BlockSpec((N_RES, lr, LANES), lambda j: (0, 0, j))
    xo, to = pl.pallas_call(
        body, grid=(c_n // LANES,),
        in_specs=[nat, nat] + [pl.BlockSpec((8, LANES), lambda j: (0, 0))] * len(extra),
        out_specs=[res, res],
        out_shape=[jax.ShapeDtypeStruct((N_RES, lr, c_n), F32)] * 2,
        name="perm_in",
    )(x, tgt, *extra)
    return xo.reshape(s_n, c_n), to.reshape(s_n, c_n)


def _to_natural(gx_lo, gx_hi):
    half_rows, c_n = gx_lo.shape
    lr = half_rows // (N_RES // 2)

    def body(lo_ref, hi_ref, o_ref):
        for r in range(N_RES):
            o_ref[pl.ds(r, lr, stride=N_RES), :] = lo_ref[r] if r < N_RES // 2 else hi_ref[r - N_RES // 2]

    half = pl.BlockSpec((N_RES // 2, lr, LANES), lambda j: (0, 0, j))
    return pl.pallas_call(
        body, grid=(c_n // LANES,),
        in_specs=[half, half],
        out_specs=pl.BlockSpec((2 * half_rows, LANES), lambda j: (0, j)),
        out_shape=jax.ShapeDtypeStruct((2 * half_rows, c_n), F32),
        name="perm_out",
    )(gx_lo.reshape(N_RES // 2, lr, c_n), gx_hi.reshape(N_RES // 2, lr, c_n))


def _rms_in(xp, norm_g):
    s_n, c_n = xp.shape
    tm = 512

    def body(x_ref, g_ref, u_ref, ut_ref):
        x = x_ref[...]
        r = lax.rsqrt(jnp.mean(x * x, axis=-1, keepdims=True) + EPS)
        u = x * r * g_ref[...]
        u_ref[...] = u.astype(u_ref.dtype)
        ut_ref[...] = u.T.astype(ut_ref.dtype)

    return pl.pallas_call(
        body, grid=(s_n // tm,),
        in_specs=[pl.BlockSpec((tm, c_n), lambda i: (i, 0)), pl.BlockSpec((1, c_n), lambda i: (0, 0))],
        out_specs=[pl.BlockSpec((tm, c_n), lambda i: (i, 0)), pl.BlockSpec((c_n, tm), lambda i: (0, i))],
        out_shape=[jax.ShapeDtypeStruct((s_n, c_n), ACT_DTYPE), jax.ShapeDtypeStruct((c_n, s_n), ACT_DTYPE)],
        name="rms_in",
    )(xp, norm_g)


def _in_proj(u, chip, w_own=None, w4=None, partial=None, others=()):
    s_n = u.shape[0]
    tn, cm = 512, 512
    per = SHARD_COLS // tn
    own = partial is None

    def body(chip_ref, a_ref, b_ref, *rest):
        o_ref = rest[-1]
        b = b_ref[...]
        for c in range(s_n // cm):
            o_ref[c * cm:(c + 1) * cm, :] = _dot(a_ref[c * cm:(c + 1) * cm, :], b).astype(o_ref.dtype)

    def shard(n, chip_ref):
        if own:
            return chip_ref[0]
        mask = others[-1]
        for i, m in enumerate(others[:-1]):
            mask = jnp.where(n // per == i, m, mask)
        return jnp.bitwise_xor(chip_ref[0], mask)

    w_spec = (pl.BlockSpec((D_MODEL, tn), lambda n, c: (0, n)) if own else
              pl.BlockSpec((None, D_MODEL, tn), lambda n, c: (shard(n, c), 0, n % per)))
    return pl.pallas_call(
        body,
        grid_spec=pltpu.PrefetchScalarGridSpec(
            num_scalar_prefetch=1, grid=(per if own else len(others) * per,),
            in_specs=[pl.BlockSpec((s_n, D_MODEL), lambda n, c: (0, 0)), w_spec] + ([] if own else [_ANY]),
            out_specs=pl.BlockSpec((s_n, tn), lambda n, c: (0, shard(n, c) * per + n % per))),
        out_shape=jax.ShapeDtypeStruct((s_n, IN_COLS), ACT_DTYPE),
        input_output_aliases={} if own else {3: 0},
        name="in_proj_own" if own else "in_proj_" + "_".join(str(m) for m in others),
    )(*([chip, u, w_own] if own else [chip, u, w4, partial]))


def _conv_terms(xc_ref, cg_ref, r, row, lr, cache):
    def a_of(q):
        if q not in cache:
            cache[q] = cg_ref[q].astype(F32) * xc_ref[q].astype(F32)
        return cache[q]

    def shift_down(v):
        return jnp.where(row >= 1, pltpu.roll(v, 1, 0), 0.0)

    a = a_of(r)
    am1 = a_of(r - 1) if r >= 1 else shift_down(a_of(N_RES - 1))
    am2 = a_of(r - 2) if r >= 2 else shift_down(a_of(N_RES - 2 + r))
    return a, am1, am2


def _conv_fwd(proj, conv_w):
    s_n = proj.shape[0]
    lr = s_n // N_RES
    pv = proj.reshape(N_RES, lr, IN_COLS)

    def body(xc_ref, bg_ref, cg_ref, zc_ref, w_ref, hc_ref, hct_ref):
        w = w_ref[...]
        row = lax.broadcasted_iota(jnp.int32, (lr, LANES), 0)
        products = {}
        for r in range(N_RES):
            a, am1, am2 = _conv_terms(xc_ref, cg_ref, r, row, lr, products)
            c = w[0:1] * am2 + w[1:2] * am1 + w[2:3] * a
            z = zc_ref[r].astype(F32)
            hc = z * _sigmoid(z) * bg_ref[r].astype(F32) * c
            hc_ref[r] = hc.astype(hc_ref.dtype)
            hct_ref[:, r * lr:(r + 1) * lr] = hc.T.astype(hct_ref.dtype)

    def col(part):
        return pl.BlockSpec((N_RES, lr, LANES), lambda j: (0, 0, part * 8 + j))

    hc, hct = pl.pallas_call(
        body, grid=(D_MODEL // LANES,),
        in_specs=[col(0), col(1), col(2), col(3), pl.BlockSpec((3, LANES), lambda j: (0, j))],
        out_specs=[pl.BlockSpec((N_RES, lr, LANES), lambda j: (0, 0, j)),
                   pl.BlockSpec((LANES, s_n), lambda j: (j, 0))],
        out_shape=[jax.ShapeDtypeStruct((N_RES, lr, D_MODEL), ACT_DTYPE),
                   jax.ShapeDtypeStruct((D_MODEL, s_n), ACT_DTYPE)],
        name="conv_fwd",
    )(pv, pv, pv, pv, conv_w)
    return hc.reshape(s_n, D_MODEL), hct


RES_PER_STEP = 8
CLASSES_PER_STEP = 2
ATTN_BATCH = 16

_BNT = (((2,), (2,)), ((0,), (0,)))
_BNN = (((2,), (1,)), ((0,), (0,)))


def _bdot(a, b, dims):
    return lax.dot_general(a.astype(MXU_DTYPE), b.astype(MXU_DTYPE), dims, preferred_element_type=F32)


def _pattern_view_shape(s_n, c_n, g_n, lead=()):
    lr = s_n // N_RES
    return (*lead, 4, 4, lr, c_n) if g_n == 4 else (*lead, N_RES, lr, c_n)


def _pattern_view(a, g_n, lead=()):
    return a.reshape(_pattern_view_shape(a.shape[-2], a.shape[-1], g_n, lead))


def _pattern_block(g_n, lr):
    if g_n == 4:
        return (4, CLASSES_PER_STEP, lr, LANES)
    return (16 if g_n == 16 else RES_PER_STEP, lr, LANES)


def _pattern_grid(g_n):
    return ({1: N_RES // RES_PER_STEP, 4: 4 // CLASSES_PER_STEP, 16: 1}[g_n], HP)


def _pattern_spec(g_n, lr, col_of_hp, lead=()):
    z = (0,) * len(lead)
    block = (*lead, *_pattern_block(g_n, lr))
    if g_n == 16:
        return pl.BlockSpec(block, lambda r, hp: (*z, 0, 0, col_of_hp(hp)))
    if g_n == 4:
        return pl.BlockSpec(block, lambda r, hp: (*z, 0, r, 0, col_of_hp(hp)))
    return pl.BlockSpec(block, lambda r, hp: (*z, r, 0, col_of_hp(hp)))


def _aligned(start, m):
    return start if isinstance(start, int) else pl.multiple_of(start, m)


class _Units:
    def __init__(self, g_n, rq):
        self.g_n, self.rq = g_n, rq
        self.per_res, self.paired = g_n == 1, rq == 8

    def plan(self, lr, size):
        if self.per_res:
            return [0], lr // self.rq - 1, lambda j: [pl.multiple_of(j * self.rq, self.rq)]
        if self.paired:
            per = min(size // 2, lr // 16)
            assert (lr // 16) % per == 0
            return ([i * 16 for i in range(per)], lr // 16 // per - 1,
                    lambda j: [pl.multiple_of((j * per + i) * 16, 16) for i in range(per)])
        step, classes = self.rq, range(CLASSES_PER_STEP)
        per = min(size // CLASSES_PER_STEP, lr // step)
        assert (lr // step) % per == 0
        return ([(c, i * step) for c in classes for i in range(per)], lr // step // per - 1,
                lambda j: [(c, pl.multiple_of((j * per + i) * step, step)) for c in classes for i in range(per)])

    def count(self, qs):
        return RES_PER_STEP if self.per_res else len(qs) * (2 if self.paired else 1)

    def _split(self, tiles, lo, rows):
        return tiles[:, lo:lo + rows].reshape(self.g_n * rows, LANES)

    def load_q(self, ref, qs):
        rq = self.rq
        if self.per_res:
            return ref[:, pl.ds(qs[0], rq), :]
        if self.paired:
            tiles = [ref[:, pl.ds(q, 16), :].astype(F32) for q in qs]
            return jnp.stack([self._split(t, lo, 8) for t in tiles for lo in (0, 8)])
        return jnp.stack([ref[:, c, pl.ds(q, rq), :].reshape(self.g_n * rq, LANES) for c, q in qs])

    def _key_rows(self, q, at_start):
        return (0, 2 * self.rq) if at_start and q == 0 else (_aligned(q - self.rq, self.rq), 2 * self.rq)

    def load_k(self, ref, qs, first):
        rq = self.rq
        if self.per_res:
            return ref[:, pl.ds(0, rq), :] if first else ref[:, pl.ds(_aligned(qs[0] - rq, rq), 2 * rq), :]
        if self.paired:
            out = []
            for i, q in enumerate(qs):
                if first and i == 0:
                    t = ref[:, 0:16, :].astype(F32)
                    out += [self._split(t, 0, 16)] * 2
                else:
                    t = ref[:, pl.ds(_aligned(q - 16, 16), 32), :].astype(F32)
                    out += [self._split(t, 8, 16), self._split(t, 16, 16)]
            return jnp.stack(out)
        rows = [(c, *self._key_rows(q, first)) for c, q in qs]
        return jnp.stack([ref[:, c, pl.ds(k0, n), :].reshape(self.g_n * n, LANES) for c, k0, n in rows])

    def store_q(self, ref, qs, val, add=False, lead=()):
        if self.per_res:
            pieces = [((), qs[0], self.rq, val)]
        elif self.paired:
            pieces = [((), q, 16, jnp.concatenate([val[2 * i].reshape(self.g_n, 8, LANES),
                                                   val[2 * i + 1].reshape(self.g_n, 8, LANES)], axis=1))
                      for i, q in enumerate(qs)]
        else:
            pieces = [((c,), q, self.rq, val[i].reshape(self.g_n, self.rq, LANES)) for i, (c, q) in enumerate(qs)]
        for cls, start, rows, v in pieces:
            idx = (*lead, slice(None), *cls, pl.ds(start, rows), slice(None))
            ref[idx] = (ref[idx] + v if add else v).astype(ref.dtype)

    def add_k(self, ref, qs, val, first):
        rq = self.rq
        if self.per_res:
            k0, n = (0, rq) if first else (_aligned(qs[0] - rq, rq), 2 * rq)
            ref[:, pl.ds(k0, n), :] += val
            return
        if self.paired:
            starts = [s for i, q in enumerate(qs)
                      for s in ((0, 0) if first and i == 0 else (_aligned(q - 8, 8), q))]
            rows = [((), s, 16) for s in starts]
        else:
            rows = [((c,), *self._key_rows(q, first)) for c, q in qs]
        for b, (cls, k0, n) in enumerate(rows):
            idx = (slice(None), *cls, pl.ds(k0, n), slice(None))
            ref[idx] += val[b].reshape(self.g_n, n, LANES)


def _batch_bias(un, qs, at_start, first_ref, general_ref):
    if not at_start:
        return general_ref[...][None]
    if un.per_res:
        return first_ref[...][None]
    if un.paired:
        return jnp.concatenate([first_ref[...][None]] + [general_ref[...][None]] * (un.count(qs) - 1), axis=0)
    return jnp.stack([(first_ref if q == 0 else general_ref)[...] for _, q in qs])


def _stack_heads(x, low):
    zero = jnp.zeros_like(x)
    return jnp.concatenate([jnp.where(low, x, zero), jnp.where(low, zero, x)], axis=1)


def _attn_fwd(proj, slopes, d):
    g_n, rq = PATTERNS[d]
    un = _Units(g_n, rq)
    s_n = proj.shape[0]
    lr = s_n // N_RES
    q_n = g_n * rq
    d0, m0, d1, m1 = _attn_tables(d)
    first, n_more, later = un.plan(lr, ATTN_BATCH)

    def body(sl_ref, q_ref, k_ref, v_ref, d0_ref, m0_ref, d1_ref, m1_ref, o_ref, lse_ref, b0_ref, b1_ref):
        hp = pl.program_id(1)

        @pl.when(hp == 0)
        def _():
            lse_ref[...] = jnp.zeros(lse_ref.shape, F32)

        for h in (0, 1):
            slope = sl_ref[2 * hp + h]
            b0_ref[h * q_n:(h + 1) * q_n, :] = m0_ref[...] - slope * d0_ref[...]
            b1_ref[h * q_n:(h + 1) * q_n, :] = m1_ref[...] - slope * d1_ref[...]

        lane = lax.broadcasted_iota(jnp.int32, (1, q_n, LANES), 2)
        low = lane < HEAD_DIM
        grp = lane // 8

        def batch(qs, at_start):
            qq = _stack_heads(un.load_q(q_ref, qs) * 0.125, low)
            s = _bdot(qq, un.load_k(k_ref, qs, at_start), _BNT) + _batch_bias(un, qs, at_start, b0_ref, b1_ref)
            m = jnp.max(s, axis=2, keepdims=True)
            p = jnp.exp(s - m)
            l = jnp.sum(p, axis=2, keepdims=True)
            o = _bdot(p, un.load_k(v_ref, qs, at_start), _BNN) * (1.0 / l)
            lse = m + jnp.log(l)
            un.store_q(o_ref, qs, jnp.where(low, o[:, :q_n], o[:, q_n:]))
            upd = jnp.where(grp == 2 * hp, lse[:, :q_n], 0.0) + jnp.where(grp == 2 * hp + 1, lse[:, q_n:], 0.0)
            un.store_q(lse_ref, qs, upd, add=True)

        batch(first, True)

        def more(j, carry):
            batch(later(j), False)
            return carry

        lax.fori_loop(1, 1 + n_more, more, 0)

    pv = _pattern_view(proj, g_n)
    full = lambda a: pl.BlockSpec(a.shape, lambda r, hp: (0, 0))
    o, lse = pl.pallas_call(
        body, grid=_pattern_grid(g_n),
        in_specs=[pl.BlockSpec(memory_space=pltpu.SMEM),
                  _pattern_spec(g_n, lr, lambda hp: 32 + hp),
                  _pattern_spec(g_n, lr, lambda hp: 40 + hp),
                  _pattern_spec(g_n, lr, lambda hp: 48 + hp),
                  full(d0), full(m0), full(d1), full(m1)],
        out_specs=[_pattern_spec(g_n, lr, lambda hp: hp), _pattern_spec(g_n, lr, lambda hp: 0)],
        out_shape=[jax.ShapeDtypeStruct(_pattern_view_shape(s_n, D_MODEL, g_n), ACT_DTYPE),
                   jax.ShapeDtypeStruct(_pattern_view_shape(s_n, LANES, g_n), F32)],
        scratch_shapes=[pltpu.VMEM((2 * q_n, d0.shape[1]), F32), pltpu.VMEM((2 * q_n, 2 * q_n), F32)],
        name=f"attn_fwd_d{d}",
    )(slopes, pv, pv, pv, d0, m0, d1, m1)
    return o.reshape(s_n, D_MODEL), lse.reshape(s_n, LANES)


def _attn_combine(outs, lses, proj):
    s_n = proj.shape[0]
    tm = 512

    def body(o1_ref, o2_ref, o3_ref, l1_ref, l2_ref, l3_ref, za_ref, e_ref, o_ref, lse_ref, ha_ref, hat_ref):
        ls = [l1_ref[...], l2_ref[...], l3_ref[...]]
        mx = jnp.maximum(jnp.maximum(ls[0], ls[1]), ls[2])
        den = sum(jnp.exp(l - mx) for l in ls)
        lse = mx + jnp.log(den)
        lse_ref[...] = lse
        o = jnp.zeros((tm, D_MODEL), F32)
        for l, oref in zip(ls, (o1_ref, o2_ref, o3_ref)):
            o = o + _select_cols(jnp.exp(l - lse), e_ref[...], terms=2) * oref[...].astype(F32)
        o_ref[...] = o.astype(o_ref.dtype)
        z = za_ref[...].astype(F32)
        ha = z * _sigmoid(z) * o
        ha_ref[...] = ha.astype(ha_ref.dtype)
        hat_ref[...] = ha.T.astype(hat_ref.dtype)

    row = lambda w: pl.BlockSpec((tm, w), lambda i: (i, 0))
    return pl.pallas_call(
        body, grid=(s_n // tm,),
        in_specs=[row(D_MODEL)] * 3 + [row(LANES)] * 3
        + [pl.BlockSpec((tm, D_MODEL), lambda i: (i, 7)), pl.BlockSpec((LANES, D_MODEL), lambda i: (0, 0))],
        out_specs=[row(D_MODEL), row(LANES), row(D_MODEL), pl.BlockSpec((D_MODEL, tm), lambda i: (0, i))],
        out_shape=[jax.ShapeDtypeStruct((s_n, D_MODEL), ACT_DTYPE), jax.ShapeDtypeStruct((s_n, LANES), F32),
                   jax.ShapeDtypeStruct((s_n, D_MODEL), ACT_DTYPE), jax.ShapeDtypeStruct((D_MODEL, s_n), ACT_DTYPE)],
        name="attn_combine",
    )(*outs, *lses, proj, _head_expand_matrix())


CHAIN_ROWS = 256


def _row_chains(tm):
    return [slice(r, r + CHAIN_ROWS) for r in range(0, tm, CHAIN_ROWS)]


def _gates(gc_ref, ga_ref, b_ref, rows):
    b = b_ref[...]
    gc = _sigmoid(gc_ref[rows, :].astype(F32) + b[:, :D_MODEL])
    ga = _sigmoid(ga_ref[rows, :].astype(F32) + b[:, D_MODEL:])
    return gc, ga


def _merge_loss(hc, ha, woc, woa, wo, proj, b_merge, xp, final_g, tgt):
    s_n = xp.shape[0]
    tm = 512

    def body(hc_ref, ha_ref, woc_ref, woa_ref, wo_ref, gc_ref, ga_ref, b_ref, x_ref, gf_ref, t_ref,
             yc_ref, ya_ref, dhb_ref, dgf_ref, loss_ref, dwo_ref, mgt_ref):
        i = pl.program_id(0)

        @pl.when(i == 0)
        def _():
            dgf_ref[...] = jnp.zeros(dgf_ref.shape, F32)
            loss_ref[...] = jnp.zeros(loss_ref.shape, F32)
            dwo_ref[...] = jnp.zeros(dwo_ref.shape, F32)

        gf = gf_ref[...]
        for rows in _row_chains(tm):
            yc = _dot(hc_ref[rows, :], woc_ref[...])
            ya = _dot(ha_ref[rows, :], woa_ref[...])
            gc, ga = _gates(gc_ref, ga_ref, b_ref, rows)
            mg = gc * yc + ga * ya
            yc_ref[rows, :] = yc.astype(yc_ref.dtype)
            ya_ref[rows, :] = ya.astype(ya_ref.dtype)
            mgt_ref[:, rows] = mg.T.astype(mgt_ref.dtype)
            h2 = x_ref[rows, :] + _dot(mg, wo_ref[...])
            r2 = lax.rsqrt(jnp.mean(h2 * h2, axis=-1, keepdims=True) + EPS)
            nrm = h2 * r2
            err = nrm * gf - t_ref[rows, :]
            e2 = (err * err).reshape(-1, 8, D_MODEL).sum(axis=0)
            loss_ref[...] += sum(e2[:, c * LANES:(c + 1) * LANES] for c in range(D_MODEL // LANES))
            dy = err * (1.0 / D_MODEL)
            dgf_ref[...] += jnp.sum(dy * nrm, axis=0, keepdims=True)
            dn = dy * gf
            dh2 = r2 * (dn - nrm * jnp.mean(dn * nrm, axis=-1, keepdims=True))
            dhb_ref[rows, :] = dh2.astype(dhb_ref.dtype)
        dwo_ref[...] += _dot(mgt_ref[...], dhb_ref[...])

    row = pl.BlockSpec((tm, D_MODEL), lambda i: (i, 0))
    wsp = pl.BlockSpec((D_MODEL, D_MODEL), lambda i: (0, 0), pipeline_mode=pl.Buffered(1))
    vec = lambda w: pl.BlockSpec((1, w), lambda i: (0, 0))
    act = jax.ShapeDtypeStruct((s_n, D_MODEL), ACT_DTYPE)
    return pl.pallas_call(
        body, grid=(s_n // tm,),
        in_specs=[row, row, wsp, wsp, wsp,
                  pl.BlockSpec((tm, D_MODEL), lambda i: (i, 8)), pl.BlockSpec((tm, D_MODEL), lambda i: (i, 9)),
                  vec(2 * D_MODEL), row, vec(D_MODEL), row],
        out_specs=[row, row, row, vec(D_MODEL), pl.BlockSpec((8, LANES), lambda i: (0, 0)),
                   pl.BlockSpec((D_MODEL, D_MODEL), lambda i: (0, 0))],
        out_shape=[act, act, act, jax.ShapeDtypeStruct((1, D_MODEL), F32), jax.ShapeDtypeStruct((8, LANES), F32),
                   jax.ShapeDtypeStruct((D_MODEL, D_MODEL), F32)],
        scratch_shapes=[pltpu.VMEM((D_MODEL, tm), MXU_DTYPE)],
        name="merge_loss",
    )(hc, ha, woc, woa, wo, proj, proj, b_merge, xp, final_g, tgt)


def _merge_bwd(dh2b, wo, woc, woa, yc, ya, proj, b_merge, o, hct, hat):
    s_n = dh2b.shape[0]
    tm = 512

    def body(dh_ref, wo_ref, woc_ref, woa_ref, yc_ref, ya_ref, gc_ref, ga_ref, b_ref, o_ref, za_ref, e_ref,
             hct_ref, hat_ref, dhc_ref, do_ref, dsum_ref, db3_ref, dbias_ref, dwoc_ref, dwoa_ref,
             dyc_ref, dya_ref):
        i = pl.program_id(0)

        @pl.when(i == 0)
        def _():
            dbias_ref[...] = jnp.zeros(dbias_ref.shape, F32)
            dwoc_ref[...] = jnp.zeros(dwoc_ref.shape, F32)
            dwoa_ref[...] = jnp.zeros(dwoa_ref.shape, F32)

        for rows in _row_chains(tm):
            dmg = _dot_nt(dh_ref[rows, :], wo_ref[...])
            gc, ga = _gates(gc_ref, ga_ref, b_ref, rows)
            dgc = dmg * yc_ref[rows, :].astype(F32) * gc * (1.0 - gc)
            dga = dmg * ya_ref[rows, :].astype(F32) * ga * (1.0 - ga)
            dbias_ref[:, :D_MODEL] += jnp.sum(dgc, axis=0, keepdims=True)
            dbias_ref[:, D_MODEL:] += jnp.sum(dga, axis=0, keepdims=True)
            dyc = dmg * gc
            dya = dmg * ga
            dyc_ref[rows, :] = dyc.astype(dyc_ref.dtype)
            dya_ref[rows, :] = dya.astype(dya_ref.dtype)
            dhc_ref[rows, :] = _dot_nt(dyc, woc_ref[...]).astype(dhc_ref.dtype)
            dha = _dot_nt(dya, woa_ref[...])
            z = za_ref[rows, :].astype(F32)
            sg = _sigmoid(z)
            ov = o_ref[rows, :].astype(F32)
            dout = dha * z * sg
            do_ref[rows, :] = dout.astype(do_ref.dtype)
            dsum_ref[rows, :] = _select_cols(dout * ov, e_ref[...], terms=2)
            db3_ref[0, rows, :] = (dha * ov * sg * (1.0 + z * (1.0 - sg))).astype(db3_ref.dtype)
            db3_ref[1, rows, :] = dgc.astype(db3_ref.dtype)
            db3_ref[2, rows, :] = dga.astype(db3_ref.dtype)
        dwoc_ref[...] += _dot(hct_ref[...], dyc_ref[...])
        dwoa_ref[...] += _dot(hat_ref[...], dya_ref[...])

    row = pl.BlockSpec((tm, D_MODEL), lambda i: (i, 0))
    col = pl.BlockSpec((D_MODEL, tm), lambda i: (0, i))
    wsp = pl.BlockSpec((D_MODEL, D_MODEL), lambda i: (0, 0), pipeline_mode=pl.Buffered(1))
    acc = pl.BlockSpec((D_MODEL, D_MODEL), lambda i: (0, 0))
    act = jax.ShapeDtypeStruct((s_n, D_MODEL), ACT_DTYPE)
    grad = jax.ShapeDtypeStruct((D_MODEL, D_MODEL), F32)
    return pl.pallas_call(
        body, grid=(s_n // tm,),
        in_specs=[row, wsp, wsp, wsp, row, row,
                  pl.BlockSpec((tm, D_MODEL), lambda i: (i, 8)), pl.BlockSpec((tm, D_MODEL), lambda i: (i, 9)),
                  pl.BlockSpec((1, 2 * D_MODEL), lambda i: (0, 0)), row,
                  pl.BlockSpec((tm, D_MODEL), lambda i: (i, 7)), pl.BlockSpec((D_MODEL, LANES), lambda i: (0, 0)),
                  col, col],
        out_specs=[row, row, pl.BlockSpec((tm, LANES), lambda i: (i, 0)),
                   pl.BlockSpec((3, tm, D_MODEL), lambda i: (0, i, 0)),
                   pl.BlockSpec((1, 2 * D_MODEL), lambda i: (0, 0)), acc, acc],
        out_shape=[act, act, jax.ShapeDtypeStruct((s_n, LANES), F32),
                   jax.ShapeDtypeStruct((3, s_n, D_MODEL), ACT_DTYPE),
                   jax.ShapeDtypeStruct((1, 2 * D_MODEL), F32), grad, grad],
        scratch_shapes=[pltpu.VMEM((tm, D_MODEL), MXU_DTYPE), pltpu.VMEM((tm, D_MODEL), MXU_DTYPE)],
        name="merge_bwd",
    )(dh2b, wo, woc, woa, yc, ya, proj, proj, b_merge, o, proj, _head_sum_matrix(), hct, hat)


def _behind(body, after, n_prefetch=0):
    if after is None:
        return body, [], []

    def ordered(*refs):
        body(*refs[:n_prefetch], *refs[n_prefetch + 1:])

    return ordered, [_ANY], [after]


def _conv_bwd(proj, conv_w, dhc, after=None):
    s_n = proj.shape[0]
    lr = s_n // N_RES
    pv = proj.reshape(N_RES, lr, IN_COLS)

    def body(xc_ref, bg_ref, cg_ref, zc_ref, w_ref, dhc_ref, da4_ref, dw_ref, dc_ref):
        w = w_ref[...]
        row = lax.broadcasted_iota(jnp.int32, (lr, LANES), 0)
        dw = [jnp.zeros((1, LANES), F32) for _ in range(3)]
        products = {}
        for r in range(N_RES):
            a, am1, am2 = _conv_terms(xc_ref, cg_ref, r, row, lr, products)
            c = w[0:1] * am2 + w[1:2] * am1 + w[2:3] * a
            z = zc_ref[r].astype(F32)
            sg = _sigmoid(z)
            sz = z * sg
            bg = bg_ref[r].astype(F32)
            dh = dhc_ref[r].astype(F32)
            da4_ref[1, r] = (dh * sz * c).astype(da4_ref.dtype)
            da4_ref[3, r] = (dh * bg * c * sg * (1.0 + z * (1.0 - sg))).astype(da4_ref.dtype)
            dc = dh * sz * bg
            dc_ref[r] = dc
            dw[0] = dw[0] + jnp.sum(dc * am2, axis=0, keepdims=True)
            dw[1] = dw[1] + jnp.sum(dc * am1, axis=0, keepdims=True)
            dw[2] = dw[2] + jnp.sum(dc * a, axis=0, keepdims=True)
        dw_ref[0:1, :] = dw[0]
        dw_ref[1:2, :] = dw[1]
        dw_ref[2:3, :] = dw[2]

        def shift_up(v):
            return jnp.where(row < lr - 1, pltpu.roll(v, lr - 1, 0), 0.0)

        for r in range(N_RES):
            dp1 = dc_ref[r + 1] if r + 1 < N_RES else shift_up(dc_ref[0])
            dp2 = dc_ref[r + 2] if r + 2 < N_RES else shift_up(dc_ref[r + 2 - N_RES])
            da = w[2:3] * dc_ref[r] + w[1:2] * dp1 + w[0:1] * dp2
            da4_ref[0, r] = (da * cg_ref[r].astype(F32)).astype(da4_ref.dtype)
            da4_ref[2, r] = (da * xc_ref[r].astype(F32)).astype(da4_ref.dtype)

    def col(part):
        return pl.BlockSpec((N_RES, lr, LANES), lambda j: (0, 0, part * 8 + j))

    kern, token_spec, token = _behind(body, after)
    da4, dw = pl.pallas_call(
        kern, grid=(D_MODEL // LANES,),
        in_specs=token_spec + [col(0), col(1), col(2), col(3), pl.BlockSpec((3, LANES), lambda j: (0, j)),
                               pl.BlockSpec((N_RES, lr, LANES), lambda j: (0, 0, j))],
        out_specs=[pl.BlockSpec((4, N_RES, lr, LANES), lambda j: (0, 0, 0, j)),
                   pl.BlockSpec((3, LANES), lambda j: (0, j))],
        out_shape=[jax.ShapeDtypeStruct((4, N_RES, lr, D_MODEL), ACT_DTYPE),
                   jax.ShapeDtypeStruct((3, D_MODEL), F32)],
        scratch_shapes=[pltpu.VMEM((N_RES, lr, LANES), F32)],
        name="conv_bwd",
    )(*token, pv, pv, pv, pv, conv_w, dhc.reshape(N_RES, lr, D_MODEL))
    return da4.reshape(4, s_n, D_MODEL), dw


def _attn_bwd(proj, dout, lse, dsum, slopes, d, prev=None, after=None):
    g_n, rq = PATTERNS[d]
    un = _Units(g_n, rq)
    s_n = proj.shape[0]
    lr = s_n // N_RES
    q_n = g_n * rq
    d0, m0, d1, m1 = (np.ascontiguousarray(t.T) for t in _attn_tables(d))
    first, n_more, later = un.plan(lr, ATTN_BATCH)
    bsz = un.count(first)

    def body(sl_ref, q_ref, k_ref, v_ref, do_ref, lse_ref, ds_ref, d0_ref, m0_ref, d1_ref, m1_ref, *rest):
        prev_ref = rest[0] if prev is not None else None
        out_ref, b0_ref, b1_ref, lt_ref, dt_ref, dk_ref, dv_ref = rest[-7:]
        hp = pl.program_id(1)
        for h in (0, 1):
            slope = sl_ref[2 * hp + h]
            b0_ref[:, h * q_n:(h + 1) * q_n] = m0_ref[...] - slope * d0_ref[...]
            b1_ref[:, h * q_n:(h + 1) * q_n] = m1_ref[...] - slope * d1_ref[...]
        if prev is None:
            dk_ref[...] = jnp.zeros(dk_ref.shape, F32)
            dv_ref[...] = jnp.zeros(dv_ref.shape, F32)
        else:
            out_ref[0] = prev_ref[0]
            dk_ref[...] = prev_ref[1].astype(F32)
            dv_ref[...] = prev_ref[2].astype(F32)
        low = lax.broadcasted_iota(jnp.int32, (1, q_n, LANES), 2) < HEAD_DIM
        row16 = pl.multiple_of(16 * hp, 16)

        def query_rows(stat_ref, t_ref, qs):
            tiles = un.load_q(stat_ref, qs)
            for b in range(bsz):
                t_ref[b] = tiles[b].T
            t16 = t_ref[:, pl.ds(row16, 16), :]
            return jnp.concatenate([t16[:, 0:1, :], t16[:, 8:9, :]], axis=2)

        def batch(qs, at_start):
            qq = _stack_heads(un.load_q(q_ref, qs) * 0.125, low)
            dd = _stack_heads(un.load_q(do_ref, qs), low)
            ks = un.load_k(k_ref, qs, at_start)
            vs = un.load_k(v_ref, qs, at_start)
            lrow = query_rows(lse_ref, lt_ref, qs)
            drow = query_rows(ds_ref, dt_ref, qs)
            pt = jnp.exp(_bdot(ks, qq, _BNT) + _batch_bias(un, qs, at_start, b0_ref, b1_ref) - lrow)
            dst = pt * (_bdot(vs, dd, _BNT) - drow)
            un.add_k(dv_ref, qs, _bdot(pt, dd, _BNN), at_start)
            un.add_k(dk_ref, qs, _bdot(dst, qq, _BNN), at_start)
            dq = _bdot(jnp.swapaxes(dst, 1, 2), ks, _BNN)
            un.store_q(out_ref, qs, jnp.where(low, dq[:, :q_n], dq[:, q_n:]) * 0.125, add=prev is not None,
                       lead=(0,))

        batch(first, True)

        def more(j, carry):
            batch(later(j), False)
            return carry

        lax.fori_loop(1, 1 + n_more, more, 0)
        out_ref[1] = dk_ref[...].astype(out_ref.dtype)
        out_ref[2] = dv_ref[...].astype(out_ref.dtype)

    pv = _pattern_view(proj, g_n)
    full = lambda a: pl.BlockSpec(a.shape, lambda r, hp: (0, 0))
    whole = _pattern_spec(g_n, lr, lambda hp: hp, lead=(3,))
    kern, token_spec, token = _behind(body, after)
    out = pl.pallas_call(
        kern, grid=_pattern_grid(g_n),
        in_specs=token_spec + [pl.BlockSpec(memory_space=pltpu.SMEM),
                               _pattern_spec(g_n, lr, lambda hp: 32 + hp),
                               _pattern_spec(g_n, lr, lambda hp: 40 + hp),
                               _pattern_spec(g_n, lr, lambda hp: 48 + hp),
                               _pattern_spec(g_n, lr, lambda hp: hp),
                               _pattern_spec(g_n, lr, lambda hp: 0),
                               _pattern_spec(g_n, lr, lambda hp: 0),
                               full(d0), full(m0), full(d1), full(m1)] + ([] if prev is None else [whole]),
        out_specs=whole,
        out_shape=jax.ShapeDtypeStruct(_pattern_view_shape(s_n, D_MODEL, g_n, lead=(3,)), ACT_DTYPE),
        scratch_shapes=[pltpu.VMEM((d0.shape[0], 2 * q_n), F32), pltpu.VMEM((2 * q_n, 2 * q_n), F32),
                        pltpu.VMEM((bsz, LANES, q_n), F32), pltpu.VMEM((bsz, LANES, q_n), F32),
                        pltpu.VMEM(_pattern_block(g_n, lr), F32), pltpu.VMEM(_pattern_block(g_n, lr), F32)],
        name=f"attn_bwd_d{d}",
    )(*token, slopes, pv, pv, pv, _pattern_view(dout, g_n), _pattern_view(lse, g_n), _pattern_view(dsum, g_n),
      d0, m0, d1, m1, *([] if prev is None else [_pattern_view(prev, g_n, lead=(3,))]))
    return out.reshape(3, s_n, D_MODEL)


def _part_index(step, per, lo, n):
    return jnp.clip(step // per - lo, 0, n - 1)


def _dw_in(ut, da4, dc3, db3):
    s_n = ut.shape[1]
    tn = 512
    per = D_MODEL // tn
    shard_blocks = SHARD_COLS // tn

    def body(a_ref, p0_ref, p1_ref, p2_ref, o_ref):
        part = pl.program_id(0) // per

        @pl.when(part < 4)
        def _():
            o_ref[...] = _dot(a_ref[...], p0_ref[...])

        @pl.when((part >= 4) & (part < 7))
        def _():
            o_ref[...] = _dot(a_ref[...], p1_ref[...])

        @pl.when(part >= 7)
        def _():
            o_ref[...] = _dot(a_ref[...], p2_ref[...])

    def pspec(lo, n):
        def index(j):
            part = j // per
            col = jnp.where(part < lo, 0, jnp.where(part >= lo + n, per - 1, j % per))
            return _part_index(j, per, lo, n), 0, col
        return pl.BlockSpec((None, s_n, tn), index)

    return pl.pallas_call(
        body, grid=(IN_COLS // tn,),
        in_specs=[pl.BlockSpec((D_MODEL, s_n), lambda j: (0, 0), pipeline_mode=pl.Buffered(1)),
                  pspec(0, 4), pspec(4, 3), pspec(7, 3)],
        out_specs=pl.BlockSpec((None, D_MODEL, tn), lambda j: (j // shard_blocks, 0, j % shard_blocks)),
        out_shape=jax.ShapeDtypeStruct((4, D_MODEL, SHARD_COLS), F32),
        name="dw_in",
    )(ut, da4, dc3, db3)


def _input_grad(da4, dc3, db3, w4, xp, norm_g, dh2, row0, rows, after=None):
    tm, tk = 256, 512
    per = D_MODEL // tk
    shard_blocks = SHARD_COLS // tk
    m0 = row0 // tm

    def body(p0_ref, p1_ref, p2_ref, w_ref, x_ref, g_ref, dh_ref, gx_ref, dg_ref):
        @pl.when(pl.program_id(0) == 0)
        def _():
            dg_ref[...] = jnp.zeros(dg_ref.shape, F32)

        du = None
        for k in range(IN_COLS // tk):
            part, cols = k // per, pl.ds((k % per) * tk, tk)
            ref, slot = (p0_ref, part) if part < 4 else (p1_ref, part - 4) if part < 7 else (p2_ref, part - 7)
            d = _dot_nt(ref[slot, :, cols], w_ref[k // shard_blocks, :, pl.ds((k % shard_blocks) * tk, tk)])
            du = d if du is None else du + d
        x = x_ref[...]
        r = lax.rsqrt(jnp.mean(x * x, axis=-1, keepdims=True) + EPS)
        nrm = x * r
        dg_ref[...] += jnp.sum(du * nrm, axis=0, keepdims=True)
        dn = du * g_ref[...]
        gx_ref[...] = dh_ref[...].astype(F32) + r * (dn - nrm * jnp.mean(dn * nrm, axis=-1, keepdims=True))

    def pspec(n):
        return pl.BlockSpec((n, tm, D_MODEL), lambda m: (0, m0 + m, 0))

    row_in = pl.BlockSpec((tm, D_MODEL), lambda m: (m0 + m, 0))
    vec = pl.BlockSpec((1, D_MODEL), lambda m: (0, 0))
    kern, token_spec, token = _behind(body, after)
    return pl.pallas_call(
        kern, grid=(rows // tm,),
        in_specs=token_spec + [pspec(4), pspec(3), pspec(3),
                               pl.BlockSpec(w4.shape, lambda m: (0, 0, 0), pipeline_mode=pl.Buffered(1)),
                               row_in, vec, row_in],
        out_specs=[pl.BlockSpec((tm, D_MODEL), lambda m: (m, 0)), vec],
        out_shape=[jax.ShapeDtypeStruct((rows, D_MODEL), F32), jax.ShapeDtypeStruct((1, D_MODEL), F32)],
        name="input_grad",
    )(*token, da4, dc3, db3, w4, xp, norm_g, dh2)


class _Step:
    def __init__(self, x, tgt, norm_g, chip, after=None):
        self.norm_g, self.chip = norm_g, chip
        self.slopes = _alibi_slopes()
        self.xp, self.tp = _to_residue_major(x, tgt, after)
        self.u, self.ut = _rms_in(self.xp, norm_g)

    def project_own(self, w_own):
        self.proj_own = _in_proj(self.u, self.chip, w_own=w_own)

    def project(self, w4, others):
        self.proj_own = _in_proj(self.u, self.chip, w4=w4, partial=self.proj_own, others=others)

    def mixers(self, w4, taps):
        self.w4, self.taps, self.proj = w4, taps, self.proj_own
        self.hc, self.hct = _conv_fwd(self.proj, taps)
        fwd = [_attn_fwd(self.proj, self.slopes, d) for d in PATTERNS]
        self.o, self.lse, self.ha, self.hat = _attn_combine([f[0] for f in fwd], [f[1] for f in fwd], self.proj)

    def merge_and_loss(self, woc, woa, wo, b_merge, final_g):
        self.woc, self.woa, self.wo, self.b_merge = woc, woa, wo, b_merge
        (self.yc, self.ya, self.dh2b, self.d_final_g, self.loss8, self.d_wo) = _merge_loss(
            self.hc, self.ha, woc, woa, wo, self.proj, b_merge, self.xp, final_g, self.tp)

    def out_weight_grads(self):
        (self.dhc, self.dout, self.dsum, self.db3, self.d_bias, d_woc, d_woa) = _merge_bwd(
            self.dh2b, self.wo, self.woc, self.woa, self.yc, self.ya, self.proj, self.b_merge, self.o,
            self.hct, self.hat)
        return d_woc, d_woa, self.d_wo

    def conv_grads(self, after=None):
        self.da4, self.d_taps = _conv_bwd(self.proj, self.taps, self.dhc, after)

    def in_weight_grad(self, after=None):
        self.dc3 = None
        for d in PATTERNS:
            self.dc3 = _attn_bwd(self.proj, self.dout, self.lse, self.dsum, self.slopes, d, prev=self.dc3,
                                 after=after if self.dc3 is None else None)
        return _dw_in(self.ut, self.da4, self.dc3, self.db3)

    def input_grad(self, half, after=None):
        rows = self.xp.shape[0] // 2
        return _input_grad(self.da4, self.dc3, self.db3, self.w4, self.xp, self.norm_g, self.dh2b,
                           half * rows, rows, after)


def _local_grads(x, tgt, norm_g, w4, b_merge, conv_w, woc, woa, wo, final_g):
    st = _Step(x, tgt, norm_g, jnp.zeros((1,), jnp.int32))
    st.project_own(w4[0])
    st.project(w4, (2, 1))
    st.project(w4, (3,))
    st.mixers(w4, conv_w)
    st.merge_and_loss(woc, woa, wo, b_merge, final_g)
    d_woc, d_woa, d_wo = st.out_weight_grads()
    st.conv_grads()
    d_w4 = st.in_weight_grad()
    gx_lo, dg_lo = st.input_grad(0)
    gx_hi, dg_hi = st.input_grad(1)
    return (st.loss8, _to_natural(gx_lo, gx_hi), dg_lo + dg_hi, d_w4, st.d_bias, st.d_taps, d_woc, d_woa, d_wo,
            st.d_final_g)


MESH = pl.DeviceIdType.MESH
_CHIP_FLIPS = ((1, 0), (0, 1), (1, 1))
_ANY = pl.BlockSpec(memory_space=pl.ANY)


def _place():
    return lax.axis_index("x"), lax.axis_index("y"), lax.axis_index("c")


def _flip(v, f):
    return 1 - v if f else v


def _remote(src, dst, send_sems, recv_sems, k, device):
    return pltpu.make_async_remote_copy(src_ref=src, dst_ref=dst, send_sem=send_sems.at[k], recv_sem=recv_sems.at[k],
                                        device_id=device, device_id_type=MESH)


def _place_shard(w, chip, dtype):
    rows, cols = w.shape
    tm = min(rows, 128)

    def body(chip_ref, w_ref, o_ref):
        o_ref[0] = w_ref[...].astype(o_ref.dtype)

    return pl.pallas_call(
        body,
        grid_spec=pltpu.PrefetchScalarGridSpec(
            num_scalar_prefetch=1, grid=(rows // tm,),
            in_specs=[pl.BlockSpec((tm, cols), lambda i, chip_ref: (i, 0))],
            out_specs=pl.BlockSpec((1, tm, cols), lambda i, chip_ref: (chip_ref[0], i, 0))),
        out_shape=jax.ShapeDtypeStruct((4, rows, cols), dtype),
        name="place_shard",
    )(chip, w)


def _gather_copies_to(flips, whole=()):
    def copies(arrs, _, send_sems, recv_sems):
        x, y, c = _place()
        out = []
        for a, arr in enumerate(arrs):
            h = arr.shape[1] // 2
            mine = arr.at[2 * x + y] if a in whole else arr.at[2 * x + y, pl.ds(pl.multiple_of(c * h, 8), h)]
            for i, t in enumerate(flips):
                fx, fy = _CHIP_FLIPS[t]
                out.append(_remote(mine, mine, send_sems, recv_sems, len(flips) * a + i,
                                   (_flip(x, fx), _flip(y, fy), c)))
        return out
    return copies


def _forward_to_sibling(arrs, flips=(0, 1, 2)):
    n = len(arrs)

    def body(*refs):
        outs = refs[n:2 * n]
        send_sems, recv_sems = refs[2 * n:]
        x, y, c = _place()
        sibling = (x, y, 1 - c)
        started = []
        for a in range(n):
            h = outs[a].shape[1] // 2
            rows = pl.ds(pl.multiple_of(c * h, 8), h)
            for t in flips:
                fx, fy = _CHIP_FLIPS[t]
                landed = outs[a].at[2 * _flip(x, fx) + _flip(y, fy), rows]
                cp = _remote(landed, landed, send_sems, recv_sems, 3 * a + t, sibling)
                cp.start()
                started.append(cp)
        for a in range(n):
            h = outs[a].shape[1] // 2
            rows = pl.ds(pl.multiple_of((1 - c) * h, 8), h)
            for t in flips:
                fx, fy = _CHIP_FLIPS[t]
                handed = outs[a].at[2 * _flip(x, fx) + _flip(y, fy), rows]
                _remote(handed, handed, send_sems, recv_sems, 3 * a + t, sibling).wait_recv()
        for cp in started:
            cp.wait_send()

    return pl.pallas_call(
        body, in_specs=[_ANY] * n, out_specs=[_ANY] * n,
        out_shape=[jax.ShapeDtypeStruct(s.shape, s.dtype) for s in arrs],
        input_output_aliases={a: a for a in range(n)},
        scratch_shapes=[pltpu.SemaphoreType.DMA((3 * n,)), pltpu.SemaphoreType.DMA((3 * n,))],
        name="gathered_to_sibling_" + "".join(str(t) for t in flips),
    )(*arrs)


_HBM = pl.BlockSpec(memory_space=pltpu.HBM)
_SEM = pl.BlockSpec(memory_space=pltpu.SEMAPHORE)
_EFFECT = pltpu.SideEffectType.DATAFLOW_SIDE_EFFECTING


class _SplitExchange:
    def __init__(self, name, srcs, land_shapes, n_copies, copies, riders=()):
        self.name, self.n, self.nl, self.copies = name, len(srcs), len(land_shapes), copies
        n, nb = self.n, len(srcs) + len(land_shapes)
        lands = [lax.empty(s.shape, s.dtype) for s in land_shapes]
        bufs = [pltpu.with_memory_space_constraint(a, pltpu.HBM) for a in (*srcs, *lands, *riders)]
        na = len(bufs)

        def body(*refs):
            send_sems, recv_sems = refs[na], refs[na + 1]
            for cp in copies(refs[:n], refs[n:nb], send_sems, recv_sems):
                cp.start()
            refs[-1][...] = jnp.zeros(refs[-1].shape, F32)

        outs = pl.pallas_call(
            body, name=name + "_start",
            in_specs=[_HBM] * na,
            out_specs=[_SEM, _SEM] + [_HBM] * na + [pl.BlockSpec(memory_space=pltpu.VMEM)],
            out_shape=[pltpu.SemaphoreType.DMA((n_copies,)), pltpu.SemaphoreType.DMA((n_copies,))]
            + [pltpu.HBM(b.shape, b.dtype) for b in bufs] + [jax.ShapeDtypeStruct((8, LANES), F32)],
            input_output_aliases={i: 2 + i for i in range(na)},
            compiler_params=pltpu.CompilerParams(has_side_effects=_EFFECT),
        )(*bufs)
        self.sems, self.bufs, self.riders, self.token = outs[:2], outs[2:2 + nb], outs[2 + nb:2 + na], outs[-1]

    def wait(self, done, riders=(), bufs=None):
        n, nb, copies = self.n, self.n + self.nl, self.copies
        bufs = [*(self.bufs if bufs is None else bufs),
                *[pltpu.with_memory_space_constraint(a, pltpu.HBM) for a in riders]]
        na = len(bufs)
        done = list(done) if isinstance(done, (list, tuple)) else [done]

        def body(*refs):
            send_sems, recv_sems = refs[na], refs[na + 1]
            for cp in copies(refs[:n], refs[n:nb], send_sems, recv_sems):
                cp.wait_send()
                cp.wait_recv()

        outs = pl.pallas_call(
            body, name=self.name + "_wait",
            in_specs=[_HBM] * na + [_SEM, _SEM] + [_ANY] * len(done),
            out_specs=[_HBM] * na,
            out_shape=[pltpu.HBM(b.shape, b.dtype) for b in bufs],
            input_output_aliases={i: i for i in range(na)},
            compiler_params=pltpu.CompilerParams(has_side_effects=_EFFECT),
        )(*bufs, *self.sems, *done)
        return outs[:n], outs[n:nb], outs[nb:]


def _sibling_copies(srcs, lands, send_sems, recv_sems):
    x, y, c = _place()
    out = []
    for a, (src, land) in enumerate(zip(srcs, lands)):
        h = src.shape[1] // 2
        theirs = pl.ds(pl.multiple_of((1 - c) * h, 8), h)
        out.append(_remote(src.at[:, theirs], land, send_sems, recv_sems, a, (x, y, 1 - c)))
    return out


def _grads_to_sibling(name, grads):
    shapes = [jax.ShapeDtypeStruct((4, g.shape[1] // 2, g.shape[2]), g.dtype) for g in grads]
    return _SplitExchange(name, grads, shapes, len(grads), _sibling_copies)


def _chip_copies(srcs, lands, send_sems, recv_sems):
    x, y, c = _place()
    out = []
    for a, (src, land) in enumerate(zip(srcs, lands)):
        for t, (fx, fy) in enumerate(_CHIP_FLIPS):
            tx, ty = _flip(x, fx), _flip(y, fy)
            out.append(_remote(src.at[2 * tx + ty], land.at[t], send_sems, recv_sems, 3 * a + t, (tx, ty, c)))
    return out


def _grads_to_chips(name, parts):
    shapes = [jax.ShapeDtypeStruct((3, *p.shape[1:]), p.dtype) for p in parts]
    return _SplitExchange(name, parts, shapes, 3 * len(parts), _chip_copies)


def _add_halves(g, r, where):
    _, rows, cols = g.shape
    h = rows // 2
    tm = min(h, 128)
    nt = h // tm

    def body(where_ref, g_ref, r_ref, b_ref):
        b_ref[...] = (g_ref[...] + r_ref[...]).astype(b_ref.dtype)

    def other(j, w):
        return jnp.bitwise_xor(w[0], j + 1)

    spec = pl.BlockSpec((1, tm, cols), lambda j, i, w: (other(j, w), i, 0))
    return pl.pallas_call(
        body,
        grid_spec=pltpu.PrefetchScalarGridSpec(
            num_scalar_prefetch=1, grid=(3, nt),
            in_specs=[pl.BlockSpec((1, tm, cols), lambda j, i, w: (other(j, w), w[1] * nt + i, 0)), spec],
            out_specs=spec),
        out_shape=jax.ShapeDtypeStruct((4, h, cols), BF16),
        name="add_sibling_grads",
    )(where, g, r)


def _add_chips(g, r, recv, where, after=None):
    _, h, cols = r.shape
    tm = min(h, 128)
    nt = h // tm

    def body(where_ref, g_ref, r_ref, recv_ref, out_ref):
        own = g_ref[0] + r_ref[0]
        out_ref[...] = ((own + recv_ref[0].astype(F32)) + recv_ref[1].astype(F32)) + recv_ref[2].astype(F32)

    kern, token_spec, token = _behind(body, after, n_prefetch=1)
    return pl.pallas_call(
        kern,
        grid_spec=pltpu.PrefetchScalarGridSpec(
            num_scalar_prefetch=1, grid=(nt,),
            in_specs=token_spec + [pl.BlockSpec((1, tm, cols), lambda i, w: (w[0], w[1] * nt + i, 0)),
                                   pl.BlockSpec((1, tm, cols), lambda i, w: (w[0], i, 0)),
                                   pl.BlockSpec((3, tm, cols), lambda i, w: (0, i, 0))],
            out_specs=pl.BlockSpec((tm, cols), lambda i, w: (w[1] * nt + i, 0))),
        out_shape=jax.ShapeDtypeStruct((2 * h, cols), F32),
        name="add_chip_grads",
    )(where, *token, g, r, recv)


def _half_copies(srcs, _, send_sems, recv_sems):
    x, y, c = _place()
    out = []
    for a, src in enumerate(srcs):
        h = src.shape[0] // 2
        mine = src.at[pl.ds(pl.multiple_of(c * h, 8), h)]
        out.append(_remote(mine, mine, send_sems, recv_sems, a, (x, y, 1 - c)))
    return out


def _share_halves(name, shards):
    return _SplitExchange(name, shards, [], len(shards), _half_copies)


def _reduce_small(rows):
    cols = rows[0].shape[1]
    n = len(rows)
    assert sum(r.shape[0] for r in rows) <= 8

    def body(*refs):
        ins, out_ref = refs[:n], refs[n]
        vec_ref, gath_ref, send_sems, recv_sems = refs[n + 1:]
        x, y, c = _place()
        me = 4 * x + 2 * y + c
        vec_ref[...] = jnp.zeros(vec_ref.shape, F32)
        at = 0
        for r in ins:
            vec_ref[at:at + r.shape[0], :] = r[...]
            at += r.shape[0]
        copies = []
        for k in range(1, 8):
            peer = (_flip(x, (k >> 2) & 1), _flip(y, (k >> 1) & 1), _flip(c, k & 1))
            copies.append(_remote(vec_ref, gath_ref.at[me], send_sems, recv_sems, k - 1, peer))
        for cp in copies:
            cp.start()
        gath_ref[me] = vec_ref[...]
        for cp in copies:
            cp.wait()
        tot = gath_ref[0]
        for dev in range(1, 8):
            tot = tot + gath_ref[dev]
        out_ref[...] = tot
        out_ref[7:8, :] = jnp.zeros((1, cols), F32) + jnp.sum(tot[7:8, :])

    vm = pl.BlockSpec(memory_space=pltpu.VMEM)
    return pl.pallas_call(
        body, in_specs=[vm] * n, out_specs=vm,
        out_shape=jax.ShapeDtypeStruct((8, cols), F32),
        scratch_shapes=[pltpu.VMEM((8, cols), F32), pltpu.VMEM((8, 8, cols), F32),
                        pltpu.SemaphoreType.DMA((7,)), pltpu.SemaphoreType.DMA((7,))],
        name="reduce_small",
    )(*rows)


def _adamw_tile(w_ref, g_ref, m_ref, v_ref, d_ref, m2_ref, v2_ref, gout_ref):
    gr = g_ref[...]
    m2 = ADAM_B1 * m_ref[...] + (1.0 - ADAM_B1) * gr
    v2 = ADAM_B2 * v_ref[...] + (1.0 - ADAM_B2) * (gr * gr)
    m_hat = m2 / (1.0 - ADAM_B1 ** ADAM_STEP)
    v_hat = v2 / (1.0 - ADAM_B2 ** ADAM_STEP)
    d_ref[...] = -ADAM_LR * (m_hat / (jnp.sqrt(v_hat) + ADAM_EPS) + ADAM_WD * w_ref[...])
    m2_ref[...] = m2
    v2_ref[...] = v2
    gout_ref[...] = gr


def _adamw(w, g, m, v, name):
    rows, cols = w.shape
    tm = 128 if rows % 128 == 0 else rows

    def body(*refs):
        _adamw_tile(*refs)

    spec = pl.BlockSpec((tm, cols), lambda i: (i, 0))
    sds = jax.ShapeDtypeStruct((rows, cols), F32)
    return pl.pallas_call(body, grid=(rows // tm,), in_specs=[spec] * 4, out_specs=[spec] * 4,
                          out_shape=[sds] * 4, name=name)(w, g, m, v)


def _adamw_half(w, g, m, v, half, name, other=()):
    rows, cols = w.shape
    tm = 128
    nt = rows // 2 // tm

    def body(half_ref, *refs):
        _adamw_tile(*refs[:4], *refs[4 + len(other):])

    spec = pl.BlockSpec((tm, cols), lambda i, h: (h[0] * nt + i, 0))
    sds = jax.ShapeDtypeStruct((rows, cols), F32)
    return pl.pallas_call(
        body,
        grid_spec=pltpu.PrefetchScalarGridSpec(
            num_scalar_prefetch=1, grid=(nt,), in_specs=[spec] * 4 + [_ANY] * len(other), out_specs=[spec] * 4),
        out_shape=[sds] * 4, input_output_aliases={5 + k: k for k in range(len(other))}, name=name,
    )(half, w, g, m, v, *other)


def kernel(x, norm_g, w_in, b_merge, conv_w, w_out_conv, w_out_attn, w_o, final_g, loss_target, m_norm_g, m_w_in, m_b_merge, m_conv_w, m_w_out_conv, m_w_out_attn, m_w_o, m_final_g, v_norm_g, v_w_in, v_b_merge, v_conv_w, v_w_out_conv, v_w_out_attn, v_w_o, v_final_g):
    mx, my, mc = _place()
    chip = (2 * mx + my).astype(jnp.int32)
    seq = x.shape[1]

    chip1 = chip.reshape(1)
    slots = [_place_shard(w[0], chip1, MXU_DTYPE) for w in (w_in, w_out_conv, w_out_attn, w_o)]
    taps_slot = _place_shard(jnp.pad(conv_w[0], ((0, 5), (0, 0))), chip1, F32)
    gather_near = _SplitExchange("gather_w_in_near", [slots[0], taps_slot], [], 4,
                                 _gather_copies_to((0, 1), whole=(1,)))
    st = _Step(x[0], loss_target[0], norm_g, chip1, after=gather_near.token)
    st.project_own(w_in[0])
    near, _, _ = gather_near.wait([st.ut, st.proj_own])
    gather_far = _SplitExchange("gather_w_in_far", near, [], 2, _gather_copies_to((2,), whole=(1,)))
    (w4,) = _forward_to_sibling(gather_far.bufs[:1], flips=(0, 1))
    st.project(w4, (2, 1))
    (w4, taps4), _, out_slots = gather_far.wait([st.proj_own], riders=slots[1:], bufs=[w4, gather_far.bufs[1]])
    gather_out = _SplitExchange("gather_w_out", out_slots, [], 9, _gather_copies_to((0, 1, 2)), riders=[w4])
    (w4,) = _forward_to_sibling(gather_out.riders, flips=(2,))
    st.project(w4, (3,))
    st.mixers(w4, jnp.concatenate([taps4[j, :3, :] for j in range(4)], axis=1))
    out_ws, _, _ = gather_out.wait(st.o)
    woc, woa, wo = [w.reshape(D_MODEL, D_MODEL) for w in _forward_to_sibling(out_ws)]
    st.merge_and_loss(woc, woa, wo, b_merge, final_g.reshape(1, D_MODEL))

    half = mc.astype(jnp.int32).reshape(1)
    where = jnp.stack([chip, mc.astype(jnp.int32)])
    out_grads = [g.reshape(4, -1, D_MODEL) for g in st.out_weight_grads()]
    to_sibling = _grads_to_sibling("out_grads_to_sibling", out_grads)
    st.conv_grads(after=to_sibling.token)
    out_grads, out_from_sibling, _ = to_sibling.wait(st.da4)
    to_chips = _grads_to_chips("out_grads_to_chips",
                               [_add_halves(g, r, where) for g, r in zip(out_grads, out_from_sibling)])
    d_w4 = st.in_weight_grad(after=to_chips.token)
    out_from_chips = to_chips.wait(st.dc3)[1]

    to_sibling = _grads_to_sibling("in_grad_to_sibling", [d_w4])
    gx_lo, dg_lo = st.input_grad(0, after=to_sibling.token)
    (d_w4,), (from_sibling,), _ = to_sibling.wait(gx_lo)
    to_chips = _grads_to_chips("in_grad_to_chips", [_add_halves(d_w4, from_sibling, where)])

    out_reduced = [_add_chips(g, r, recv, where, after=to_chips.token)
                   for g, r, recv in zip(out_grads, out_from_sibling, out_from_chips)]
    share_out = _share_halves("share_out_grads", out_reduced)
    gx_hi, dg_hi = st.input_grad(1, after=share_out.token)
    grad_x = _to_natural(gx_lo, gx_hi)
    g_woc, g_woa, g_wo = share_out.wait(gx_hi)[0]

    small = _reduce_small([dg_lo + dg_hi, st.d_bias.reshape(2, D_MODEL), st.d_taps, st.d_final_g,
                           st.loss8.reshape(1, D_MODEL)])
    loss = (0.5 / D_MODEL) * small[7, 0]
    g_taps = lax.dynamic_slice(small[3:6], (0, chip * (D_MODEL // 4)), (3, D_MODEL // 4))
    upd = {
        "norm_g": _adamw(norm_g, small[0:1], m_norm_g, v_norm_g, "adamw_norm_g"),
        "b_merge": _adamw(b_merge, small[1:3].reshape(1, 2 * D_MODEL), m_b_merge, v_b_merge, "adamw_b_merge"),
        "conv_w": _adamw(conv_w[0], g_taps, m_conv_w[0], v_conv_w[0], "adamw_conv_w"),
        "w_out_conv": _adamw(w_out_conv[0], g_woc, m_w_out_conv[0], v_w_out_conv[0], "adamw_w_out_conv"),
        "w_out_attn": _adamw(w_out_attn[0], g_woa, m_w_out_attn[0], v_w_out_attn[0], "adamw_w_out_attn"),
        "w_o": _adamw(w_o[0], g_wo, m_w_o[0], v_w_o[0], "adamw_w_o"),
        "final_g": _adamw(final_g.reshape(1, D_MODEL), small[6:7], m_final_g.reshape(1, D_MODEL),
                          v_final_g.reshape(1, D_MODEL), "adamw_final_g"),
    }
    behind = [grad_x] + [u[0] for u in upd.values()]
    in_reduced = _add_chips(d_w4, from_sibling, to_chips.wait(behind)[1][0], where)

    share_in = _share_halves("share_in_grad", [in_reduced])
    w_in_args = (w_in[0], m_w_in[0], v_w_in[0])
    own_rows = _adamw_half(w_in_args[0], share_in.bufs[0], *w_in_args[1:], half, "adamw_w_in_own_rows")
    (g_w_in,) = share_in.wait(own_rows[0])[0]
    upd["w_in"] = _adamw_half(w_in_args[0], g_w_in, *w_in_args[1:], 1 - half, "adamw_w_in_sibling_rows",
                              other=own_rows)

    names = ["norm_g", "w_in", "b_merge", "conv_w", "w_out_conv", "w_out_attn", "w_o", "final_g"]
    shapes = [norm_g.shape, w_in.shape, b_merge.shape, conv_w.shape, w_out_conv.shape, w_out_attn.shape,
              w_o.shape, final_g.shape]
    outs = [loss, grad_x.reshape(1, seq, D_MODEL)]
    for k in (3, 0, 1, 2):
        outs += [upd[n][k].reshape(s) for n, s in zip(names, shapes)]
    return tuple(outs)
```

```python
import functools

import numpy as np
import jax
import jax.numpy as jnp
from jax import lax
from jax.experimental import pallas as pl
from jax.experimental.pallas import tpu as pltpu

F32 = jnp.float32
BF16 = jnp.bfloat16
MXU_DTYPE = jnp.bfloat16
ACT_DTYPE = jnp.bfloat16

D_MODEL = 1024
N_HEADS = 16
HEAD_DIM = 64
QB = 128
N_RES = 16
LANES = 128
HP = N_HEADS * HEAD_DIM // LANES
IN_COLS = 10 * D_MODEL
SHARD_COLS = IN_COLS // 4
EPS = 1e-6
NEG = -1e30

ADAM_LR, ADAM_B1, ADAM_B2, ADAM_EPS, ADAM_WD, ADAM_STEP = 0.001, 0.9, 0.999, 1e-08, 0.01, 10

PATTERNS = {1: (16, 8), 4: (4, 32), 16: (1, 128)}

_NN = (((1,), (0,)), ((), ()))
_NT = (((1,), (1,)), ((), ()))


def _dot(a, b):
    return lax.dot_general(a.astype(MXU_DTYPE), b.astype(MXU_DTYPE), _NN, preferred_element_type=F32)


def _dot_nt(a, b):
    return lax.dot_general(a.astype(MXU_DTYPE), b.astype(MXU_DTYPE), _NT, preferred_element_type=F32)


def _split3(x):
    hi = x.astype(BF16)
    r1 = x - hi.astype(F32)
    mid = r1.astype(BF16)
    lo = (r1 - mid.astype(F32)).astype(BF16)
    return hi, mid, lo


def _select_cols(x, sel, terms):
    return sum(lax.dot_general(t, sel, _NN, preferred_element_type=F32) for t in _split3(x)[:terms])


def _sigmoid(z):
    return 1.0 / (1.0 + jnp.exp(-z))


def _head_expand_matrix():
    e = np.zeros((LANES, D_MODEL), np.float32)
    for h in range(N_HEADS):
        e[8 * h, HEAD_DIM * h:HEAD_DIM * (h + 1)] = 1.0
    return jnp.asarray(e, BF16)


def _head_sum_matrix():
    e = np.zeros((D_MODEL, LANES), np.float32)
    for h in range(N_HEADS):
        e[HEAD_DIM * h:HEAD_DIM * (h + 1), 8 * h:8 * (h + 1)] = 1.0
    return jnp.asarray(e, BF16)


def _attn_tables(d):
    g_n, rq = PATTERNS[d]
    q_n = g_n * rq
    gq, iq = np.arange(q_n) // rq, np.arange(q_n) % rq

    def tab(kn, base):
        k_n = g_n * kn
        gk, jk = np.arange(k_n) // kn, np.arange(k_n) % kn
        delta = g_n * (base + iq[:, None] - jk[None, :]) + gq[:, None] - gk[None, :]
        valid = (delta >= 0) & (delta <= QB)
        dist = np.where(valid, d * delta, 0).astype(np.float32)
        madd = np.where(valid, 0.0, NEG).astype(np.float32)
        return dist, madd

    d0, m0 = tab(rq if g_n == 1 else 2 * rq, 0)
    d1, m1 = tab(2 * rq, rq)
    return d0, m0, d1, m1


def _alibi_slopes():
    return jnp.exp2(-8.0 * jnp.arange(1, N_HEADS + 1, dtype=F32) / N_HEADS)


def _to_residue_major(x, tgt, after=None):
    s_n, c_n = x.shape
    lr = s_n // N_RES
    extra = [] if after is None else [after]

    def body(x_ref, t_ref, *rest):
        xo_ref, to_ref = rest[-2:]
        for r in range(N_RES):
            xo_ref[r] = x_ref[pl.ds(r, lr, stride=N_RES), :]
            to_ref[r] = t_ref[pl.ds(r, lr, stride=N_RES), :]

    nat = pl.BlockSpec((s_n, LANES), lambda j: (0, j))
    res = pl.BlockSpec((N_RES, lr, LANES), lambda j: (0, 0, j))
    xo, to = pl.pallas_call(
        body, grid=(c_n // LANES,),
        in_specs=[nat, nat] + [pl.BlockSpec((8, LANES), lambda j: (0, 0))] * len(extra),
        out_specs=[res, res],
        out_shape=[jax.ShapeDtypeStruct((N_RES, lr, c_n), F32)] * 2,
        name="perm_in",
    )(x, tgt, *extra)
    return xo.reshape(s_n, c_n), to.reshape(s_n, c_n)


def _to_natural(gx_lo, gx_hi):
    half_rows, c_n = gx_lo.shape
    lr = half_rows // (N_RES // 2)

    def body(lo_ref, hi_ref, o_ref):
        for r in range(N_RES):
            o_ref[pl.ds(r, lr, stride=N_RES), :] = lo_ref[r] if r < N_RES // 2 else hi_ref[r - N_RES // 2]

    half = pl.BlockSpec((N_RES // 2, lr, LANES), lambda j: (0, 0, j))
    return pl.pallas_call(
        body, grid=(c_n // LANES,),
        in_specs=[half, half],
        out_specs=pl.BlockSpec((2 * half_rows, LANES), lambda j: (0, j)),
        out_shape=jax.ShapeDtypeStruct((2 * half_rows, c_n), F32),
        name="perm_out",
    )(gx_lo.reshape(N_RES // 2, lr, c_n), gx_hi.reshape(N_RES // 2, lr, c_n))


def _rms_in(xp, norm_g):
    s_n, c_n = xp.shape
    tm = 512

    def body(x_ref, g_ref, u_ref, ut_ref):
        x = x_ref[...]
        r = lax.rsqrt(jnp.mean(x * x, axis=-1, keepdims=True) + EPS)
        u = x * r * g_ref[...]
        u_ref[...] = u.astype(u_ref.dtype)
        ut_ref[...] = u.T.astype(ut_ref.dtype)

    return pl.pallas_call(
        body, grid=(s_n // tm,),
        in_specs=[pl.BlockSpec((tm, c_n), lambda i: (i, 0)), pl.BlockSpec((1, c_n), lambda i: (0, 0))],
        out_specs=[pl.BlockSpec((tm, c_n), lambda i: (i, 0)), pl.BlockSpec((c_n, tm), lambda i: (0, i))],
        out_shape=[jax.ShapeDtypeStruct((s_n, c_n), ACT_DTYPE), jax.ShapeDtypeStruct((c_n, s_n), ACT_DTYPE)],
        name="rms_in",
    )(xp, norm_g)


def _in_proj(u, chip, w_own=None, w4=None, partial=None, others=()):
    s_n = u.shape[0]
    tn, cm = 512, 512
    per = SHARD_COLS // tn
    own = partial is None

    def body(chip_ref, a_ref, b_ref, *rest):
        o_ref = rest[-1]
        b = b_ref[...]
        for c in range(s_n // cm):
            o_ref[c * cm:(c + 1) * cm, :] = _dot(a_ref[c * cm:(c + 1) * cm, :], b).astype(o_ref.dtype)

    def shard(n, chip_ref):
        if own:
            return chip_ref[0]
        mask = others[-1]
        for i, m in enumerate(others[:-1]):
            mask = jnp.where(n // per == i, m, mask)
        return jnp.bitwise_xor(chip_ref[0], mask)

    w_spec = (pl.BlockSpec((D_MODEL, tn), lambda n, c: (0, n)) if own else
              pl.BlockSpec((None, D_MODEL, tn), lambda n, c: (shard(n, c), 0, n % per)))
    return pl.pallas_call(
        body,
        grid_spec=pltpu.PrefetchScalarGridSpec(
            num_scalar_prefetch=1, grid=(per if own else len(others) * per,),
            in_specs=[pl.BlockSpec((s_n, D_MODEL), lambda n, c: (0, 0)), w_spec] + ([] if own else [_ANY]),
            out_specs=pl.BlockSpec((s_n, tn), lambda n, c: (0, shard(n, c) * per + n % per))),
        out_shape=jax.ShapeDtypeStruct((s_n, IN_COLS), ACT_DTYPE),
        input_output_aliases={} if own else {3: 0},
        name="in_proj_own" if own else "in_proj_" + "_".join(str(m) for m in others),
    )(*([chip, u, w_own] if own else [chip, u, w4, partial]))


def _conv_terms(xc_ref, cg_ref, r, row, lr, cache):
    def a_of(q):
        if q not in cache:
            cache[q] = cg_ref[q].astype(F32) * xc_ref[q].astype(F32)
        return cache[q]

    def shift_down(v):
        return jnp.where(row >= 1, pltpu.roll(v, 1, 0), 0.0)

    a = a_of(r)
    am1 = a_of(r - 1) if r >= 1 else shift_down(a_of(N_RES - 1))
    am2 = a_of(r - 2) if r >= 2 else shift_down(a_of(N_RES - 2 + r))
    return a, am1, am2


def _conv_fwd(proj, conv_w):
    s_n = proj.shape[0]
    lr = s_n // N_RES
    pv = proj.reshape(N_RES, lr, IN_COLS)

    def body(xc_ref, bg_ref, cg_ref, zc_ref, w_ref, hc_ref, hct_ref):
        w = w_ref[...]
        row = lax.broadcasted_iota(jnp.int32, (lr, LANES), 0)
        products = {}
        for r in range(N_RES):
            a, am1, am2 = _conv_terms(xc_ref, cg_ref, r, row, lr, products)
            c = w[0:1] * am2 + w[1:2] * am1 + w[2:3] * a
            z = zc_ref[r].astype(F32)
            hc = z * _sigmoid(z) * bg_ref[r].astype(F32) * c
            hc_ref[r] = hc.astype(hc_ref.dtype)
            hct_ref[:, r * lr:(r + 1) * lr] = hc.T.astype(hct_ref.dtype)

    def col(part):
        return pl.BlockSpec((N_RES, lr, LANES), lambda j: (0, 0, part * 8 + j))

    hc, hct = pl.pallas_call(
        body, grid=(D_MODEL // LANES,),
        in_specs=[col(0), col(1), col(2), col(3), pl.BlockSpec((3, LANES), lambda j: (0, j))],
        out_specs=[pl.BlockSpec((N_RES, lr, LANES), lambda j: (0, 0, j)),
                   pl.BlockSpec((LANES, s_n), lambda j: (j, 0))],
        out_shape=[jax.ShapeDtypeStruct((N_RES, lr, D_MODEL), ACT_DTYPE),
                   jax.ShapeDtypeStruct((D_MODEL, s_n), ACT_DTYPE)],
        name="conv_fwd",
    )(pv, pv, pv, pv, conv_w)
    return hc.reshape(s_n, D_MODEL), hct


RES_PER_STEP = 8
CLASSES_PER_STEP = 2
ATTN_BATCH = 16

_BNT = (((2,), (2,)), ((0,), (0,)))
_BNN = (((2,), (1,)), ((0,), (0,)))


def _bdot(a, b, dims):
    return lax.dot_general(a.astype(MXU_DTYPE), b.astype(MXU_DTYPE), dims, preferred_element_type=F32)


def _pattern_view_shape(s_n, c_n, g_n, lead=()):
    lr = s_n // N_RES
    return (*lead, 4, 4, lr, c_n) if g_n == 4 else (*lead, N_RES, lr, c_n)


def _pattern_view(a, g_n, lead=()):
    return a.reshape(_pattern_view_shape(a.shape[-2], a.shape[-1], g_n, lead))


def _pattern_block(g_n, lr):
    if g_n == 4:
        return (4, CLASSES_PER_STEP, lr, LANES)
    return (16 if g_n == 16 else RES_PER_STEP, lr, LANES)


def _pattern_grid(g_n):
    return ({1: N_RES // RES_PER_STEP, 4: 4 // CLASSES_PER_STEP, 16: 1}[g_n], HP)


def _pattern_spec(g_n, lr, col_of_hp, lead=()):
    z = (0,) * len(lead)
    block = (*lead, *_pattern_block(g_n, lr))
    if g_n == 16:
        return pl.BlockSpec(block, lambda r, hp: (*z, 0, 0, col_of_hp(hp)))
    if g_n == 4:
        return pl.BlockSpec(block, lambda r, hp: (*z, 0, r, 0, col_of_hp(hp)))
    return pl.BlockSpec(block, lambda r, hp: (*z, r, 0, col_of_hp(hp)))


def _aligned(start, m):
    return start if isinstance(start, int) else pl.multiple_of(start, m)


class _Units:
    def __init__(self, g_n, rq):
        self.g_n, self.rq = g_n, rq
        self.per_res, self.paired = g_n == 1, rq == 8

    def plan(self, lr, size):
        if self.per_res:
            return [0], lr // self.rq - 1, lambda j: [pl.multiple_of(j * self.rq, self.rq)]
        if self.paired:
            per = min(size // 2, lr // 16)
            assert (lr // 16) % per == 0
            return ([i * 16 for i in range(per)], lr // 16 // per - 1,
                    lambda j: [pl.multiple_of((j * per + i) * 16, 16) for i in range(per)])
        step, classes = self.rq, range(CLASSES_PER_STEP)
        per = min(size // CLASSES_PER_STEP, lr // step)
        assert (lr // step) % per == 0
        return ([(c, i * step) for c in classes for i in range(per)], lr // step // per - 1,
                lambda j: [(c, pl.multiple_of((j * per + i) * step, step)) for c in classes for i in range(per)])

    def count(self, qs):
        return RES_PER_STEP if self.per_res else len(qs) * (2 if self.paired else 1)

    def _split(self, tiles, lo, rows):
        return tiles[:, lo:lo + rows].reshape(self.g_n * rows, LANES)

    def load_q(self, ref, qs):
        rq = self.rq
        if self.per_res:
            return ref[:, pl.ds(qs[0], rq), :]
        if self.paired:
            tiles = [ref[:, pl.ds(q, 16), :].astype(F32) for q in qs]
            return jnp.stack([self._split(t, lo, 8) for t in tiles for lo in (0, 8)])
        return jnp.stack([ref[:, c, pl.ds(q, rq), :].reshape(self.g_n * rq, LANES) for c, q in qs])

    def _key_rows(self, q, at_start):
        return (0, 2 * self.rq) if at_start and q == 0 else (_aligned(q - self.rq, self.rq), 2 * self.rq)

    def load_k(self, ref, qs, first):
        rq = self.rq
        if self.per_res:
            return ref[:, pl.ds(0, rq), :] if first else ref[:, pl.ds(_aligned(qs[0] - rq, rq), 2 * rq), :]
        if self.paired:
            out = []
            for i, q in enumerate(qs):
                if first and i == 0:
                    t = ref[:, 0:16, :].astype(F32)
                    out += [self._split(t, 0, 16)] * 2
                else:
                    t = ref[:, pl.ds(_aligned(q - 16, 16), 32), :].astype(F32)
                    out += [self._split(t, 8, 16), self._split(t, 16, 16)]
            return jnp.stack(out)
        rows = [(c, *self._key_rows(q, first)) for c, q in qs]
        return jnp.stack([ref[:, c, pl.ds(k0, n), :].reshape(self.g_n * n, LANES) for c, k0, n in rows])

    def store_q(self, ref, qs, val, add=False, lead=()):
        if self.per_res:
            pieces = [((), qs[0], self.rq, val)]
        elif self.paired:
            pieces = [((), q, 16, jnp.concatenate([val[2 * i].reshape(self.g_n, 8, LANES),
                                                   val[2 * i + 1].reshape(self.g_n, 8, LANES)], axis=1))
                      for i, q in enumerate(qs)]
        else:
            pieces = [((c,), q, self.rq, val[i].reshape(self.g_n, self.rq, LANES)) for i, (c, q) in enumerate(qs)]
        for cls, start, rows, v in pieces:
            idx = (*lead, slice(None), *cls, pl.ds(start, rows), slice(None))
            ref[idx] = (ref[idx] + v if add else v).astype(ref.dtype)

    def add_k(self, ref, qs, val, first):
        rq = self.rq
        if self.per_res:
            k0, n = (0, rq) if first else (_aligned(qs[0] - rq, rq), 2 * rq)
            ref[:, pl.ds(k0, n), :] += val
            return
        if self.paired:
            starts = [s for i, q in enumerate(qs)
                      for s in ((0, 0) if first and i == 0 else (_aligned(q - 8, 8), q))]
            rows = [((), s, 16) for s in starts]
        else:
            rows = [((c,), *self._key_rows(q, first)) for c, q in qs]
        for b, (cls, k0, n) in enumerate(rows):
            idx = (slice(None), *cls, pl.ds(k0, n), slice(None))
            ref[idx] += val[b].reshape(self.g_n, n, LANES)


def _batch_bias(un, qs, at_start, first_ref, general_ref):
    if not at_start:
        return general_ref[...][None]
    if un.per_res:
        return first_ref[...][None]
    if un.paired:
        return jnp.concatenate([first_ref[...][None]] + [general_ref[...][None]] * (un.count(qs) - 1), axis=0)
    return jnp.stack([(first_ref if q == 0 else general_ref)[...] for _, q in qs])


def _stack_heads(x, low):
    zero = jnp.zeros_like(x)
    return jnp.concatenate([jnp.where(low, x, zero), jnp.where(low, zero, x)], axis=1)


def _attn_fwd(proj, slopes, d):
    g_n, rq = PATTERNS[d]
    un = _Units(g_n, rq)
    s_n = proj.shape[0]
    lr = s_n // N_RES
    q_n = g_n * rq
    d0, m0, d1, m1 = _attn_tables(d)
    first, n_more, later = un.plan(lr, ATTN_BATCH)

    def body(sl_ref, q_ref, k_ref, v_ref, d0_ref, m0_ref, d1_ref, m1_ref, o_ref, lse_ref, b0_ref, b1_ref):
        hp = pl.program_id(1)

        @pl.when(hp == 0)
        def _():
            lse_ref[...] = jnp.zeros(lse_ref.shape, F32)

        for h in (0, 1):
            slope = sl_ref[2 * hp + h]
            b0_ref[h * q_n:(h + 1) * q_n, :] = m0_ref[...] - slope * d0_ref[...]
            b1_ref[h * q_n:(h + 1) * q_n, :] = m1_ref[...] - slope * d1_ref[...]

        lane = lax.broadcasted_iota(jnp.int32, (1, q_n, LANES), 2)
        low = lane < HEAD_DIM
        grp = lane // 8

        def batch(qs, at_start):
            qq = _stack_heads(un.load_q(q_ref, qs) * 0.125, low)
            s = _bdot(qq, un.load_k(k_ref, qs, at_start), _BNT) + _batch_bias(un, qs, at_start, b0_ref, b1_ref)
            m = jnp.max(s, axis=2, keepdims=True)
            p = jnp.exp(s - m)
            l = jnp.sum(p, axis=2, keepdims=True)
            o = _bdot(p, un.load_k(v_ref, qs, at_start), _BNN) * (1.0 / l)
            lse = m + jnp.log(l)
            un.store_q(o_ref, qs, jnp.where(low, o[:, :q_n], o[:, q_n:]))
            upd = jnp.where(grp == 2 * hp, lse[:, :q_n], 0.0) + jnp.where(grp == 2 * hp + 1, lse[:, q_n:], 0.0)
            un.store_q(lse_ref, qs, upd, add=True)

        batch(first, True)

        def more(j, carry):
            batch(later(j), False)
            return carry

        lax.fori_loop(1, 1 + n_more, more, 0)

    pv = _pattern_view(proj, g_n)
    full = lambda a: pl.BlockSpec(a.shape, lambda r, hp: (0, 0))
    o, lse = pl.pallas_call(
        body, grid=_pattern_grid(g_n),
        in_specs=[pl.BlockSpec(memory_space=pltpu.SMEM),
                  _pattern_spec(g_n, lr, lambda hp: 32 + hp),
                  _pattern_spec(g_n, lr, lambda hp: 40 + hp),
                  _pattern_spec(g_n, lr, lambda hp: 48 + hp),
                  full(d0), full(m0), full(d1), full(m1)],
        out_specs=[_pattern_spec(g_n, lr, lambda hp: hp), _pattern_spec(g_n, lr, lambda hp: 0)],
        out_shape=[jax.ShapeDtypeStruct(_pattern_view_shape(s_n, D_MODEL, g_n), ACT_DTYPE),
                   jax.ShapeDtypeStruct(_pattern_view_shape(s_n, LANES, g_n), F32)],
        scratch_shapes=[pltpu.VMEM((2 * q_n, d0.shape[1]), F32), pltpu.VMEM((2 * q_n, 2 * q_n), F32)],
        name=f"attn_fwd_d{d}",
    )(slopes, pv, pv, pv, d0, m0, d1, m1)
    return o.reshape(s_n, D_MODEL), lse.reshape(s_n, LANES)


def _attn_combine(outs, lses, proj):
    s_n = proj.shape[0]
    tm = 512

    def body(o1_ref, o2_ref, o3_ref, l1_ref, l2_ref, l3_ref, za_ref, e_ref, o_ref, lse_ref, ha_ref, hat_ref):
        ls = [l1_ref[...], l2_ref[...], l3_ref[...]]
        mx = jnp.maximum(jnp.maximum(ls[0], ls[1]), ls[2])
        den = sum(jnp.exp(l - mx) for l in ls)
        lse = mx + jnp.log(den)
        lse_ref[...] = lse
        o = jnp.zeros((tm, D_MODEL), F32)
        for l, oref in zip(ls, (o1_ref, o2_ref, o3_ref)):
            o = o + _select_cols(jnp.exp(l - lse), e_ref[...], terms=2) * oref[...].astype(F32)
        o_ref[...] = o.astype(o_ref.dtype)
        z = za_ref[...].astype(F32)
        ha = z * _sigmoid(z) * o
        ha_ref[...] = ha.astype(ha_ref.dtype)
        hat_ref[...] = ha.T.astype(hat_ref.dtype)

    row = lambda w: pl.BlockSpec((tm, w), lambda i: (i, 0))
    return pl.pallas_call(
        body, grid=(s_n // tm,),
        in_specs=[row(D_MODEL)] * 3 + [row(LANES)] * 3
        + [pl.BlockSpec((tm, D_MODEL), lambda i: (i, 7)), pl.BlockSpec((LANES, D_MODEL), lambda i: (0, 0))],
        out_specs=[row(D_MODEL), row(LANES), row(D_MODEL), pl.BlockSpec((D_MODEL, tm), lambda i: (0, i))],
        out_shape=[jax.ShapeDtypeStruct((s_n, D_MODEL), ACT_DTYPE), jax.ShapeDtypeStruct((s_n, LANES), F32),
                   jax.ShapeDtypeStruct((s_n, D_MODEL), ACT_DTYPE), jax.ShapeDtypeStruct((D_MODEL, s_n), ACT_DTYPE)],
        name="attn_combine",
    )(*outs, *lses, proj, _head_expand_matrix())


CHAIN_ROWS = 256


def _row_chains(tm):
    return [slice(r, r + CHAIN_ROWS) for r in range(0, tm, CHAIN_ROWS)]


def _gates(gc_ref, ga_ref, b_ref, rows):
    b = b_ref[...]
    gc = _sigmoid(gc_ref[rows, :].astype(F32) + b[:, :D_MODEL])
    ga = _sigmoid(ga_ref[rows, :].astype(F32) + b[:, D_MODEL:])
    return gc, ga


def _merge_loss(hc, ha, woc, woa, wo, proj, b_merge, xp, final_g, tgt):
    s_n = xp.shape[0]
    tm = 512

    def body(hc_ref, ha_ref, woc_ref, woa_ref, wo_ref, gc_ref, ga_ref, b_ref, x_ref, gf_ref, t_ref,
             yc_ref, ya_ref, dhb_ref, dgf_ref, loss_ref, dwo_ref, mgt_ref):
        i = pl.program_id(0)

        @pl.when(i == 0)
        def _():
            dgf_ref[...] = jnp.zeros(dgf_ref.shape, F32)
            loss_ref[...] = jnp.zeros(loss_ref.shape, F32)
            dwo_ref[...] = jnp.zeros(dwo_ref.shape, F32)

        gf = gf_ref[...]
        for rows in _row_chains(tm):
            yc = _dot(hc_ref[rows, :], woc_ref[...])
            ya = _dot(ha_ref[rows, :], woa_ref[...])
            gc, ga = _gates(gc_ref, ga_ref, b_ref, rows)
            mg = gc * yc + ga * ya
            yc_ref[rows, :] = yc.astype(yc_ref.dtype)
            ya_ref[rows, :] = ya.astype(ya_ref.dtype)
            mgt_ref[:, rows] = mg.T.astype(mgt_ref.dtype)
            h2 = x_ref[rows, :] + _dot(mg, wo_ref[...])
            r2 = lax.rsqrt(jnp.mean(h2 * h2, axis=-1, keepdims=True) + EPS)
            nrm = h2 * r2
            err = nrm * gf - t_ref[rows, :]
            e2 = (err * err).reshape(-1, 8, D_MODEL).sum(axis=0)
            loss_ref[...] += sum(e2[:, c * LANES:(c + 1) * LANES] for c in range(D_MODEL // LANES))
            dy = err * (1.0 / D_MODEL)
            dgf_ref[...] += jnp.sum(dy * nrm, axis=0, keepdims=True)
            dn = dy * gf
            dh2 = r2 * (dn - nrm * jnp.mean(dn * nrm, axis=-1, keepdims=True))
            dhb_ref[rows, :] = dh2.astype(dhb_ref.dtype)
        dwo_ref[...] += _dot(mgt_ref[...], dhb_ref[...])

    row = pl.BlockSpec((tm, D_MODEL), lambda i: (i, 0))
    wsp = pl.BlockSpec((D_MODEL, D_MODEL), lambda i: (0, 0), pipeline_mode=pl.Buffered(1))
    vec = lambda w: pl.BlockSpec((1, w), lambda i: (0, 0))
    act = jax.ShapeDtypeStruct((s_n, D_MODEL), ACT_DTYPE)
    return pl.pallas_call(
        body, grid=(s_n // tm,),
        in_specs=[row, row, wsp, wsp, wsp,
                  pl.BlockSpec((tm, D_MODEL), lambda i: (i, 8)), pl.BlockSpec((tm, D_MODEL), lambda i: (i, 9)),
                  vec(2 * D_MODEL), row, vec(D_MODEL), row],
        out_specs=[row, row, row, vec(D_MODEL), pl.BlockSpec((8, LANES), lambda i: (0, 0)),
                   pl.BlockSpec((D_MODEL, D_MODEL), lambda i: (0, 0))],
        out_shape=[act, act, act, jax.ShapeDtypeStruct((1, D_MODEL), F32), jax.ShapeDtypeStruct((8, LANES), F32),
                   jax.ShapeDtypeStruct((D_MODEL, D_MODEL), F32)],
        scratch_shapes=[pltpu.VMEM((D_MODEL, tm), MXU_DTYPE)],
        name="merge_loss",
    )(hc, ha, woc, woa, wo, proj, proj, b_merge, xp, final_g, tgt)


def _merge_bwd(dh2b, wo, woc, woa, yc, ya, proj, b_merge, o, hct, hat):
    s_n = dh2b.shape[0]
    tm = 512

    def body(dh_ref, wo_ref, woc_ref, woa_ref, yc_ref, ya_ref, gc_ref, ga_ref, b_ref, o_ref, za_ref, e_ref,
             hct_ref, hat_ref, dhc_ref, do_ref, dsum_ref, db3_ref, dbias_ref, dwoc_ref, dwoa_ref,
             dyc_ref, dya_ref):
        i = pl.program_id(0)

        @pl.when(i == 0)
        def _():
            dbias_ref[...] = jnp.zeros(dbias_ref.shape, F32)
            dwoc_ref[...] = jnp.zeros(dwoc_ref.shape, F32)
            dwoa_ref[...] = jnp.zeros(dwoa_ref.shape, F32)

        for rows in _row_chains(tm):
            dmg = _dot_nt(dh_ref[rows, :], wo_ref[...])
            gc, ga = _gates(gc_ref, ga_ref, b_ref, rows)
            dgc = dmg * yc_ref[rows, :].astype(F32) * gc * (1.0 - gc)
            dga = dmg * ya_ref[rows, :].astype(F32) * ga * (1.0 - ga)
            dbias_ref[:, :D_MODEL] += jnp.sum(dgc, axis=0, keepdims=True)
            dbias_ref[:, D_MODEL:] += jnp.sum(dga, axis=0, keepdims=True)
            dyc = dmg * gc
            dya = dmg * ga
            dyc_ref[rows, :] = dyc.astype(dyc_ref.dtype)
            dya_ref[rows, :] = dya.astype(dya_ref.dtype)
            dhc_ref[rows, :] = _dot_nt(dyc, woc_ref[...]).astype(dhc_ref.dtype)
            dha = _dot_nt(dya, woa_ref[...])
            z = za_ref[rows, :].astype(F32)
            sg = _sigmoid(z)
            ov = o_ref[rows, :].astype(F32)
            dout = dha * z * sg
            do_ref[rows, :] = dout.astype(do_ref.dtype)
            dsum_ref[rows, :] = _select_cols(dout * ov, e_ref[...], terms=2)
            db3_ref[0, rows, :] = (dha * ov * sg * (1.0 + z * (1.0 - sg))).astype(db3_ref.dtype)
            db3_ref[1, rows, :] = dgc.astype(db3_ref.dtype)
            db3_ref[2, rows, :] = dga.astype(db3_ref.dtype)
        dwoc_ref[...] += _dot(hct_ref[...], dyc_ref[...])
        dwoa_ref[...] += _dot(hat_ref[...], dya_ref[...])

    row = pl.BlockSpec((tm, D_MODEL), lambda i: (i, 0))
    col = pl.BlockSpec((D_MODEL, tm), lambda i: (0, i))
    wsp = pl.BlockSpec((D_MODEL, D_MODEL), lambda i: (0, 0), pipeline_mode=pl.Buffered(1))
    acc = pl.BlockSpec((D_MODEL, D_MODEL), lambda i: (0, 0))
    act = jax.ShapeDtypeStruct((s_n, D_MODEL), ACT_DTYPE)
    grad = jax.ShapeDtypeStruct((D_MODEL, D_MODEL), F32)
    return pl.pallas_call(
        body, grid=(s_n // tm,),
        in_specs=[row, wsp, wsp, wsp, row, row,
                  pl.BlockSpec((tm, D_MODEL), lambda i: (i, 8)), pl.BlockSpec((tm, D_MODEL), lambda i: (i, 9)),
                  pl.BlockSpec((1, 2 * D_MODEL), lambda i: (0, 0)), row,
                  pl.BlockSpec((tm, D_MODEL), lambda i: (i, 7)), pl.BlockSpec((D_MODEL, LANES), lambda i: (0, 0)),
                  col, col],
        out_specs=[row, row, pl.BlockSpec((tm, LANES), lambda i: (i, 0)),
                   pl.BlockSpec((3, tm, D_MODEL), lambda i: (0, i, 0)),
                   pl.BlockSpec((1, 2 * D_MODEL), lambda i: (0, 0)), acc, acc],
        out_shape=[act, act, jax.ShapeDtypeStruct((s_n, LANES), F32),
                   jax.ShapeDtypeStruct((3, s_n, D_MODEL), ACT_DTYPE),
                   jax.ShapeDtypeStruct((1, 2 * D_MODEL), F32), grad, grad],
        scratch_shapes=[pltpu.VMEM((tm, D_MODEL), MXU_DTYPE), pltpu.VMEM((tm, D_MODEL), MXU_DTYPE)],
        name="merge_bwd",
    )(dh2b, wo, woc, woa, yc, ya, proj, proj, b_merge, o, proj, _head_sum_matrix(), hct, hat)


def _behind(body, after, n_prefetch=0):
    if after is None:
        return body, [], []

    def ordered(*refs):
        body(*refs[:n_prefetch], *refs[n_prefetch + 1:])

    return ordered, [_ANY], [after]


def _conv_bwd(proj, conv_w, dhc, after=None):
    s_n = proj.shape[0]
    lr = s_n // N_RES
    pv = proj.reshape(N_RES, lr, IN_COLS)

    def body(xc_ref, bg_ref, cg_ref, zc_ref, w_ref, dhc_ref, da4_ref, dw_ref, dc_ref):
        w = w_ref[...]
        row = lax.broadcasted_iota(jnp.int32, (lr, LANES), 0)
        dw = [jnp.zeros((1, LANES), F32) for _ in range(3)]
        products = {}
        for r in range(N_RES):
            a, am1, am2 = _conv_terms(xc_ref, cg_ref, r, row, lr, products)
            c = w[0:1] * am2 + w[1:2] * am1 + w[2:3] * a
            z = zc_ref[r].astype(F32)
            sg = _sigmoid(z)
            sz = z * sg
            bg = bg_ref[r].astype(F32)
            dh = dhc_ref[r].astype(F32)
            da4_ref[1, r] = (dh * sz * c).astype(da4_ref.dtype)
            da4_ref[3, r] = (dh * bg * c * sg * (1.0 + z * (1.0 - sg))).astype(da4_ref.dtype)
            dc = dh * sz * bg
            dc_ref[r] = dc
            dw[0] = dw[0] + jnp.sum(dc * am2, axis=0, keepdims=True)
            dw[1] = dw[1] + jnp.sum(dc * am1, axis=0, keepdims=True)
            dw[2] = dw[2] + jnp.sum(dc * a, axis=0, keepdims=True)
        dw_ref[0:1, :] = dw[0]
        dw_ref[1:2, :] = dw[1]
        dw_ref[2:3, :] = dw[2]

        def shift_up(v):
            return jnp.where(row < lr - 1, pltpu.roll(v, lr - 1, 0), 0.0)

        for r in range(N_RES):
            dp1 = dc_ref[r + 1] if r + 1 < N_RES else shift_up(dc_ref[0])
            dp2 = dc_ref[r + 2] if r + 2 < N_RES else shift_up(dc_ref[r + 2 - N_RES])
            da = w[2:3] * dc_ref[r] + w[1:2] * dp1 + w[0:1] * dp2
            da4_ref[0, r] = (da * cg_ref[r].astype(F32)).astype(da4_ref.dtype)
            da4_ref[2, r] = (da * xc_ref[r].astype(F32)).astype(da4_ref.dtype)

    def col(part):
        return pl.BlockSpec((N_RES, lr, LANES), lambda j: (0, 0, part * 8 + j))

    kern, token_spec, token = _behind(body, after)
    da4, dw = pl.pallas_call(
        kern, grid=(D_MODEL // LANES,),
        in_specs=token_spec + [col(0), col(1), col(2), col(3), pl.BlockSpec((3, LANES), lambda j: (0, j)),
                               pl.BlockSpec((N_RES, lr, LANES), lambda j: (0, 0, j))],
        out_specs=[pl.BlockSpec((4, N_RES, lr, LANES), lambda j: (0, 0, 0, j)),
                   pl.BlockSpec((3, LANES), lambda j: (0, j))],
        out_shape=[jax.ShapeDtypeStruct((4, N_RES, lr, D_MODEL), ACT_DTYPE),
                   jax.ShapeDtypeStruct((3, D_MODEL), F32)],
        scratch_shapes=[pltpu.VMEM((N_RES, lr, LANES), F32)],
        name="conv_bwd",
    )(*token, pv, pv, pv, pv, conv_w, dhc.reshape(N_RES, lr, D_MODEL))
    return da4.reshape(4, s_n, D_MODEL), dw


def _attn_bwd(proj, dout, lse, dsum, slopes, d, prev=None, after=None):
    g_n, rq = PATTERNS[d]
    un = _Units(g_n, rq)
    s_n = proj.shape[0]
    lr = s_n // N_RES
    q_n = g_n * rq
    d0, m0, d1, m1 = (np.ascontiguousarray(t.T) for t in _attn_tables(d))
    first, n_more, later = un.plan(lr, ATTN_BATCH)
    bsz = un.count(first)

    def body(sl_ref, q_ref, k_ref, v_ref, do_ref, lse_ref, ds_ref, d0_ref, m0_ref, d1_ref, m1_ref, *rest):
        prev_ref = rest[0] if prev is not None else None
        out_ref, b0_ref, b1_ref, lt_ref, dt_ref, dk_ref, dv_ref = rest[-7:]
        hp = pl.program_id(1)
        for h in (0, 1):
            slope = sl_ref[2 * hp + h]
            b0_ref[:, h * q_n:(h + 1) * q_n] = m0_ref[...] - slope * d0_ref[...]
            b1_ref[:, h * q_n:(h + 1) * q_n] = m1_ref[...] - slope * d1_ref[...]
        if prev is None:
            dk_ref[...] = jnp.zeros(dk_ref.shape, F32)
            dv_ref[...] = jnp.zeros(dv_ref.shape, F32)
        else:
            out_ref[0] = prev_ref[0]
            dk_ref[...] = prev_ref[1].astype(F32)
            dv_ref[...] = prev_ref[2].astype(F32)
        low = lax.broadcasted_iota(jnp.int32, (1, q_n, LANES), 2) < HEAD_DIM
        row16 = pl.multiple_of(16 * hp, 16)

        def query_rows(stat_ref, t_ref, qs):
            tiles = un.load_q(stat_ref, qs)
            for b in range(bsz):
                t_ref[b] = tiles[b].T
            t16 = t_ref[:, pl.ds(row16, 16), :]
            return jnp.concatenate([t16[:, 0:1, :], t16[:, 8:9, :]], axis=2)

        def batch(qs, at_start):
            qq = _stack_heads(un.load_q(q_ref, qs) * 0.125, low)
            dd = _stack_heads(un.load_q(do_ref, qs), low)
            ks = un.load_k(k_ref, qs, at_start)
            vs = un.load_k(v_ref, qs, at_start)
            lrow = query_rows(lse_ref, lt_ref, qs)
            drow = query_rows(ds_ref, dt_ref, qs)
            pt = jnp.exp(_bdot(ks, qq, _BNT) + _batch_bias(un, qs, at_start, b0_ref, b1_ref) - lrow)
            dst = pt * (_bdot(vs, dd, _BNT) - drow)
            un.add_k(dv_ref, qs, _bdot(pt, dd, _BNN), at_start)
            un.add_k(dk_ref, qs, _bdot(dst, qq, _BNN), at_start)
            dq = _bdot(jnp.swapaxes(dst, 1, 2), ks, _BNN)
            un.store_q(out_ref, qs, jnp.where(low, dq[:, :q_n], dq[:, q_n:]) * 0.125, add=prev is not None,
                       lead=(0,))

        batch(first, True)

        def more(j, carry):
            batch(later(j), False)
            return carry

        lax.fori_loop(1, 1 + n_more, more, 0)
        out_ref[1] = dk_ref[...].astype(out_ref.dtype)
        out_ref[2] = dv_ref[...].astype(out_ref.dtype)

    pv = _pattern_view(proj, g_n)
    full = lambda a: pl.BlockSpec(a.shape, lambda r, hp: (0, 0))
    whole = _pattern_spec(g_n, lr, lambda hp: hp, lead=(3,))
    kern, token_spec, token = _behind(body, after)
    out = pl.pallas_call(
        kern, grid=_pattern_grid(g_n),
        in_specs=token_spec + [pl.BlockSpec(memory_space=pltpu.SMEM),
                               _pattern_spec(g_n, lr, lambda hp: 32 + hp),
                               _pattern_spec(g_n, lr, lambda hp: 40 + hp),
                               _pattern_spec(g_n, lr, lambda hp: 48 + hp),
                               _pattern_spec(g_n, lr, lambda hp: hp),
                               _pattern_spec(g_n, lr, lambda hp: 0),
                               _pattern_spec(g_n, lr, lambda hp: 0),
                               full(d0), full(m0), full(d1), full(m1)] + ([] if prev is None else [whole]),
        out_specs=whole,
        out_shape=jax.ShapeDtypeStruct(_pattern_view_shape(s_n, D_MODEL, g_n, lead=(3,)), ACT_DTYPE),
        scratch_shapes=[pltpu.VMEM((d0.shape[0], 2 * q_n), F32), pltpu.VMEM((2 * q_n, 2 * q_n), F32),
                        pltpu.VMEM((bsz, LANES, q_n), F32), pltpu.VMEM((bsz, LANES, q_n), F32),
                        pltpu.VMEM(_pattern_block(g_n, lr), F32), pltpu.VMEM(_pattern_block(g_n, lr), F32)],
        name=f"attn_bwd_d{d}",
    )(*token, slopes, pv, pv, pv, _pattern_view(dout, g_n), _pattern_view(lse, g_n), _pattern_view(dsum, g_n),
      d0, m0, d1, m1, *([] if prev is None else [_pattern_view(prev, g_n, lead=(3,))]))
    return out.reshape(3, s_n, D_MODEL)


def _part_index(step, per, lo, n):
    return jnp.clip(step // per - lo, 0, n - 1)


def _dw_in(ut, da4, dc3, db3):
    s_n = ut.shape[1]
    tn = 512
    per = D_MODEL // tn
    shard_blocks = SHARD_COLS // tn

    def body(a_ref, p0_ref, p1_ref, p2_ref, o_ref):
        part = pl.program_id(0) // per

        @pl.when(part < 4)
        def _():
            o_ref[...] = _dot(a_ref[...], p0_ref[...])

        @pl.when((part >= 4) & (part < 7))
        def _():
            o_ref[...] = _dot(a_ref[...], p1_ref[...])

        @pl.when(part >= 7)
        def _():
            o_ref[...] = _dot(a_ref[...], p2_ref[...])

    def pspec(lo, n):
        def index(j):
            part = j // per
            col = jnp.where(part < lo, 0, jnp.where(part >= lo + n, per - 1, j % per))
            return _part_index(j, per, lo, n), 0, col
        return pl.BlockSpec((None, s_n, tn), index)

    return pl.pallas_call(
        body, grid=(IN_COLS // tn,),
        in_specs=[pl.BlockSpec((D_MODEL, s_n), lambda j: (0, 0), pipeline_mode=pl.Buffered(1)),
                  pspec(0, 4), pspec(4, 3), pspec(7, 3)],
        out_specs=pl.BlockSpec((None, D_MODEL, tn), lambda j: (j // shard_blocks, 0, j % shard_blocks)),
        out_shape=jax.ShapeDtypeStruct((4, D_MODEL, SHARD_COLS), F32),
        name="dw_in",
    )(ut, da4, dc3, db3)


def _input_grad(da4, dc3, db3, w4, xp, norm_g, dh2, row0, rows, after=None):
    tm, tk = 256, 512
    per = D_MODEL // tk
    shard_blocks = SHARD_COLS // tk
    m0 = row0 // tm

    def body(p0_ref, p1_ref, p2_ref, w_ref, x_ref, g_ref, dh_ref, gx_ref, dg_ref):
        @pl.when(pl.program_id(0) == 0)
        def _():
            dg_ref[...] = jnp.zeros(dg_ref.shape, F32)

        du = None
        for k in range(IN_COLS // tk):
            part, cols = k // per, pl.ds((k % per) * tk, tk)
            ref, slot = (p0_ref, part) if part < 4 else (p1_ref, part - 4) if part < 7 else (p2_ref, part - 7)
            d = _dot_nt(ref[slot, :, cols], w_ref[k // shard_blocks, :, pl.ds((k % shard_blocks) * tk, tk)])
            du = d if du is None else du + d
        x = x_ref[...]
        r = lax.rsqrt(jnp.mean(x * x, axis=-1, keepdims=True) + EPS)
        nrm = x * r
        dg_ref[...] += jnp.sum(du * nrm, axis=0, keepdims=True)
        dn = du * g_ref[...]
        gx_ref[...] = dh_ref[...].astype(F32) + r * (dn - nrm * jnp.mean(dn * nrm, axis=-1, keepdims=True))

    def pspec(n):
        return pl.BlockSpec((n, tm, D_MODEL), lambda m: (0, m0 + m, 0))

    row_in = pl.BlockSpec((tm, D_MODEL), lambda m: (m0 + m, 0))
    vec = pl.BlockSpec((1, D_MODEL), lambda m: (0, 0))
    kern, token_spec, token = _behind(body, after)
    return pl.pallas_call(
        kern, grid=(rows // tm,),
        in_specs=token_spec + [pspec(4), pspec(3), pspec(3),
                               pl.BlockSpec(w4.shape, lambda m: (0, 0, 0), pipeline_mode=pl.Buffered(1)),
                               row_in, vec, row_in],
        out_specs=[pl.BlockSpec((tm, D_MODEL), lambda m: (m, 0)), vec],
        out_shape=[jax.ShapeDtypeStruct((rows, D_MODEL), F32), jax.ShapeDtypeStruct((1, D_MODEL), F32)],
        name="input_grad",
    )(*token, da4, dc3, db3, w4, xp, norm_g, dh2)


class _Step:
    def __init__(self, x, tgt, norm_g, chip, after=None):
        self.norm_g, self.chip = norm_g, chip
        self.slopes = _alibi_slopes()
        self.xp, self.tp = _to_residue_major(x, tgt, after)
        self.u, self.ut = _rms_in(self.xp, norm_g)

    def project_own(self, w_own):
        self.proj_own = _in_proj(self.u, self.chip, w_own=w_own)

    def project(self, w4, others):
        self.proj_own = _in_proj(self.u, self.chip, w4=w4, partial=self.proj_own, others=others)

    def mixers(self, w4, taps):
        self.w4, self.taps, self.proj = w4, taps, self.proj_own
        self.hc, self.hct = _conv_fwd(self.proj, taps)
        fwd = [_attn_fwd(self.proj, self.slopes, d) for d in PATTERNS]
        self.o, self.lse, self.ha, self.hat = _attn_combine([f[0] for f in fwd], [f[1] for f in fwd], self.proj)

    def merge_and_loss(self, woc, woa, wo, b_merge, final_g):
        self.woc, self.woa, self.wo, self.b_merge = woc, woa, wo, b_merge
        (self.yc, self.ya, self.dh2b, self.d_final_g, self.loss8, self.d_wo) = _merge_loss(
            self.hc, self.ha, woc, woa, wo, self.proj, b_merge, self.xp, final_g, self.tp)

    def out_weight_grads(self):
        (self.dhc, self.dout, self.dsum, self.db3, self.d_bias, d_woc, d_woa) = _merge_bwd(
            self.dh2b, self.wo, self.woc, self.woa, self.yc, self.ya, self.proj, self.b_merge, self.o,
            self.hct, self.hat)
        return d_woc, d_woa, self.d_wo

    def conv_grads(self, after=None):
        self.da4, self.d_taps = _conv_bwd(self.proj, self.taps, self.dhc, after)

    def in_weight_grad(self, after=None):
        self.dc3 = None
        for d in PATTERNS:
            self.dc3 = _attn_bwd(self.proj, self.dout, self.lse, self.dsum, self.slopes, d, prev=self.dc3,
                                 after=after if self.dc3 is None else None)
        return _dw_in(self.ut, self.da4, self.dc3, self.db3)

    def input_grad(self, half, after=None):
        rows = self.xp.shape[0] // 2
        return _input_grad(self.da4, self.dc3, self.db3, self.w4, self.xp, self.norm_g, self.dh2b,
                           half * rows, rows, after)


def _local_grads(x, tgt, norm_g, w4, b_merge, conv_w, woc, woa, wo, final_g):
    st = _Step(x, tgt, norm_g, jnp.zeros((1,), jnp.int32))
    st.project_own(w4[0])
    st.project(w4, (2, 1))
    st.project(w4, (3,))
    st.mixers(w4, conv_w)
    st.merge_and_loss(woc, woa, wo, b_merge, final_g)
    d_woc, d_woa, d_wo = st.out_weight_grads()
    st.conv_grads()
    d_w4 = st.in_weight_grad()
    gx_lo, dg_lo = st.input_grad(0)
    gx_hi, dg_hi = st.input_grad(1)
    return (st.loss8, _to_natural(gx_lo, gx_hi), dg_lo + dg_hi, d_w4, st.d_bias, st.d_taps, d_woc, d_woa, d_wo,
            st.d_final_g)


MESH = pl.DeviceIdType.MESH
_CHIP_FLIPS = ((1, 0), (0, 1), (1, 1))
_ANY = pl.BlockSpec(memory_space=pl.ANY)


def _place():
    return lax.axis_index("x"), lax.axis_index("y"), lax.axis_index("c")


def _flip(v, f):
    return 1 - v if f else v


def _remote(src, dst, send_sems, recv_sems, k, device):
    return pltpu.make_async_remote_copy(src_ref=src, dst_ref=dst, send_sem=send_sems.at[k], recv_sem=recv_sems.at[k],
                                        device_id=device, device_id_type=MESH)


def _place_shard(w, chip, dtype):
    rows, cols = w.shape
    tm = min(rows, 128)

    def body(chip_ref, w_ref, o_ref):
        o_ref[0] = w_ref[...].astype(o_ref.dtype)

    return pl.pallas_call(
        body,
        grid_spec=pltpu.PrefetchScalarGridSpec(
            num_scalar_prefetch=1, grid=(rows // tm,),
            in_specs=[pl.BlockSpec((tm, cols), lambda i, chip_ref: (i, 0))],
            out_specs=pl.BlockSpec((1, tm, cols), lambda i, chip_ref: (chip_ref[0], i, 0))),
        out_shape=jax.ShapeDtypeStruct((4, rows, cols), dtype),
        name="place_shard",
    )(chip, w)


def _gather_copies_to(flips, whole=()):
    def copies(arrs, _, send_sems, recv_sems):
        x, y, c = _place()
        out = []
        for a, arr in enumerate(arrs):
            h = arr.shape[1] // 2
            mine = arr.at[2 * x + y] if a in whole else arr.at[2 * x + y, pl.ds(pl.multiple_of(c * h, 8), h)]
            for i, t in enumerate(flips):
                fx, fy = _CHIP_FLIPS[t]
                out.append(_remote(mine, mine, send_sems, recv_sems, len(flips) * a + i,
                                   (_flip(x, fx), _flip(y, fy), c)))
        return out
    return copies


def _forward_to_sibling(arrs, flips=(0, 1, 2)):
    n = len(arrs)

    def body(*refs):
        outs = refs[n:2 * n]
        send_sems, recv_sems = refs[2 * n:]
        x, y, c = _place()
        sibling = (x, y, 1 - c)
        started = []
        for a in range(n):
            h = outs[a].shape[1] // 2
            rows = pl.ds(pl.multiple_of(c * h, 8), h)
            for t in flips:
                fx, fy = _CHIP_FLIPS[t]
                landed = outs[a].at[2 * _flip(x, fx) + _flip(y, fy), rows]
                cp = _remote(landed, landed, send_sems, recv_sems, 3 * a + t, sibling)
                cp.start()
                started.append(cp)
        for a in range(n):
            h = outs[a].shape[1] // 2
            rows = pl.ds(pl.multiple_of((1 - c) * h, 8), h)
            for t in flips:
                fx, fy = _CHIP_FLIPS[t]
                handed = outs[a].at[2 * _flip(x, fx) + _flip(y, fy), rows]
                _remote(handed, handed, send_sems, recv_sems, 3 * a + t, sibling).wait_recv()
        for cp in started:
            cp.wait_send()

    return pl.pallas_call(
        body, in_specs=[_ANY] * n, out_specs=[_ANY] * n,
        out_shape=[jax.ShapeDtypeStruct(s.shape, s.dtype) for s in arrs],
        input_output_aliases={a: a for a in range(n)},
        scratch_shapes=[pltpu.SemaphoreType.DMA((3 * n,)), pltpu.SemaphoreType.DMA((3 * n,))],
        name="gathered_to_sibling_" + "".join(str(t) for t in flips),
    )(*arrs)


_HBM = pl.BlockSpec(memory_space=pltpu.HBM)
_SEM = pl.BlockSpec(memory_space=pltpu.SEMAPHORE)
_EFFECT = pltpu.SideEffectType.DATAFLOW_SIDE_EFFECTING


class _SplitExchange:
    def __init__(self, name, srcs, land_shapes, n_copies, copies, riders=()):
        self.name, self.n, self.nl, self.copies = name, len(srcs), len(land_shapes), copies
        n, nb = self.n, len(srcs) + len(land_shapes)
        lands = [lax.empty(s.shape, s.dtype) for s in land_shapes]
        bufs = [pltpu.with_memory_space_constraint(a, pltpu.HBM) for a in (*srcs, *lands, *riders)]
        na = len(bufs)

        def body(*refs):
            send_sems, recv_sems = refs[na], refs[na + 1]
            for cp in copies(refs[:n], refs[n:nb], send_sems, recv_sems):
                cp.start()
            refs[-1][...] = jnp.zeros(refs[-1].shape, F32)

        outs = pl.pallas_call(
            body, name=name + "_start",
            in_specs=[_HBM] * na,
            out_specs=[_SEM, _SEM] + [_HBM] * na + [pl.BlockSpec(memory_space=pltpu.VMEM)],
            out_shape=[pltpu.SemaphoreType.DMA((n_copies,)), pltpu.SemaphoreType.DMA((n_copies,))]
            + [pltpu.HBM(b.shape, b.dtype) for b in bufs] + [jax.ShapeDtypeStruct((8, LANES), F32)],
            input_output_aliases={i: 2 + i for i in range(na)},
            compiler_params=pltpu.CompilerParams(has_side_effects=_EFFECT),
        )(*bufs)
        self.sems, self.bufs, self.riders, self.token = outs[:2], outs[2:2 + nb], outs[2 + nb:2 + na], outs[-1]

    def wait(self, done, riders=(), bufs=None):
        n, nb, copies = self.n, self.n + self.nl, self.copies
        bufs = [*(self.bufs if bufs is None else bufs),
                *[pltpu.with_memory_space_constraint(a, pltpu.HBM) for a in riders]]
        na = len(bufs)
        done = list(done) if isinstance(done, (list, tuple)) else [done]

        def body(*refs):
            send_sems, recv_sems = refs[na], refs[na + 1]
            for cp in copies(refs[:n], refs[n:nb], send_sems, recv_sems):
                cp.wait_send()
                cp.wait_recv()

        outs = pl.pallas_call(
            body, name=self.name + "_wait",
            in_specs=[_HBM] * na + [_SEM, _SEM] + [_ANY] * len(done),
            out_specs=[_HBM] * na,
            out_shape=[pltpu.HBM(b.shape, b.dtype) for b in bufs],
            input_output_aliases={i: i for i in range(na)},
            compiler_params=pltpu.CompilerParams(has_side_effects=_EFFECT),
        )(*bufs, *self.sems, *done)
        return outs[:n], outs[n:nb], outs[nb:]


def _sibling_copies(srcs, lands, send_sems, recv_sems):
    x, y, c = _place()
    out = []
    for a, (src, land) in enumerate(zip(srcs, lands)):
        h = src.shape[1] // 2
        theirs = pl.ds(pl.multiple_of((1 - c) * h, 8), h)
        out.append(_remote(src.at[:, theirs], land, send_sems, recv_sems, a, (x, y, 1 - c)))
    return out


def _grads_to_sibling(name, grads):
    shapes = [jax.ShapeDtypeStruct((4, g.shape[1] // 2, g.shape[2]), g.dtype) for g in grads]
    return _SplitExchange(name, grads, shapes, len(grads), _sibling_copies)


def _chip_copies(srcs, lands, send_sems, recv_sems):
    x, y, c = _place()
    out = []
    for a, (src, land) in enumerate(zip(srcs, lands)):
        for t, (fx, fy) in enumerate(_CHIP_FLIPS):
            tx, ty = _flip(x, fx), _flip(y, fy)
            out.append(_remote(src.at[2 * tx + ty], land.at[t], send_sems, recv_sems, 3 * a + t, (tx, ty, c)))
    return out


def _grads_to_chips(name, parts):
    shapes = [jax.ShapeDtypeStruct((3, *p.shape[1:]), p.dtype) for p in parts]
    return _SplitExchange(name, parts, shapes, 3 * len(parts), _chip_copies)


def _add_halves(g, r, where):
    n = len(g)
    _, rows, cols = g[0].shape
    h = rows // 2
    tm = min(h, 128)
    nt = h // tm

    def body(where_ref, *refs):
        for a in range(n):
            refs[2 * n + a][...] = (refs[a][...] + refs[n + a][...]).astype(BF16)

    def other(j, w):
        return jnp.bitwise_xor(w[0], j + 1)

    spec = pl.BlockSpec((1, tm, cols), lambda j, i, w: (other(j, w), i, 0))
    return pl.pallas_call(
        body,
        grid_spec=pltpu.PrefetchScalarGridSpec(
            num_scalar_prefetch=1, grid=(3, nt),
            in_specs=[pl.BlockSpec((1, tm, cols), lambda j, i, w: (other(j, w), w[1] * nt + i, 0))] * n + [spec] * n,
            out_specs=[spec] * n),
        out_shape=[jax.ShapeDtypeStruct((4, h, cols), BF16)] * n,
        name="add_sibling_grads",
    )(where, *g, *r)


def _add_chips(g, r, recv, where, after=None):
    _, h, cols = r.shape
    tm = min(h, 128)
    nt = h // tm

    def body(where_ref, g_ref, r_ref, recv_ref, out_ref):
        own = g_ref[0] + r_ref[0]
        out_ref[...] = ((own + recv_ref[0].astype(F32)) + recv_ref[1].astype(F32)) + recv_ref[2].astype(F32)

    kern, token_spec, token = _behind(body, after, n_prefetch=1)
    return pl.pallas_call(
        kern,
        grid_spec=pltpu.PrefetchScalarGridSpec(
            num_scalar_prefetch=1, grid=(nt,),
            in_specs=token_spec + [pl.BlockSpec((1, tm, cols), lambda i, w: (w[0], w[1] * nt + i, 0)),
                                   pl.BlockSpec((1, tm, cols), lambda i, w: (w[0], i, 0)),
                                   pl.BlockSpec((3, tm, cols), lambda i, w: (0, i, 0))],
            out_specs=pl.BlockSpec((tm, cols), lambda i, w: (w[1] * nt + i, 0))),
        out_shape=jax.ShapeDtypeStruct((2 * h, cols), F32),
        name="add_chip_grads",
    )(where, *token, g, r, recv)


def _half_copies(srcs, _, send_sems, recv_sems):
    x, y, c = _place()
    out = []
    for a, src in enumerate(srcs):
        h = src.shape[0] // 2
        mine = src.at[pl.ds(pl.multiple_of(c * h, 8), h)]
        out.append(_remote(mine, mine, send_sems, recv_sems, a, (x, y, 1 - c)))
    return out


def _share_halves(name, shards):
    return _SplitExchange(name, shards, [], len(shards), _half_copies)


def _reduce_small(rows):
    cols = rows[0].shape[1]
    n = len(rows)
    assert sum(r.shape[0] for r in rows) <= 8

    def body(*refs):
        ins, out_ref = refs[:n], refs[n]
        vec_ref, gath_ref, send_sems, recv_sems = refs[n + 1:]
        x, y, c = _place()
        me = 4 * x + 2 * y + c
        vec_ref[...] = jnp.zeros(vec_ref.shape, F32)
        at = 0
        for r in ins:
            vec_ref[at:at + r.shape[0], :] = r[...]
            at += r.shape[0]
        copies = []
        for k in range(1, 8):
            peer = (_flip(x, (k >> 2) & 1), _flip(y, (k >> 1) & 1), _flip(c, k & 1))
            copies.append(_remote(vec_ref, gath_ref.at[me], send_sems, recv_sems, k - 1, peer))
        for cp in copies:
            cp.start()
        gath_ref[me] = vec_ref[...]
        for cp in copies:
            cp.wait()
        tot = gath_ref[0]
        for dev in range(1, 8):
            tot = tot + gath_ref[dev]
        out_ref[...] = tot
        out_ref[7:8, :] = jnp.zeros((1, cols), F32) + jnp.sum(tot[7:8, :])

    vm = pl.BlockSpec(memory_space=pltpu.VMEM)
    return pl.pallas_call(
        body, in_specs=[vm] * n, out_specs=vm,
        out_shape=jax.ShapeDtypeStruct((8, cols), F32),
        scratch_shapes=[pltpu.VMEM((8, cols), F32), pltpu.VMEM((8, 8, cols), F32),
                        pltpu.SemaphoreType.DMA((7,)), pltpu.SemaphoreType.DMA((7,))],
        name="reduce_small",
    )(*rows)


def _adamw_tile(w_ref, g_ref, m_ref, v_ref, d_ref, m2_ref, v2_ref, gout_ref):
    gr = g_ref[...]
    m2 = ADAM_B1 * m_ref[...] + (1.0 - ADAM_B1) * gr
    v2 = ADAM_B2 * v_ref[...] + (1.0 - ADAM_B2) * (gr * gr)
    m_hat = m2 / (1.0 - ADAM_B1 ** ADAM_STEP)
    v_hat = v2 / (1.0 - ADAM_B2 ** ADAM_STEP)
    d_ref[...] = -ADAM_LR * (m_hat / (jnp.sqrt(v_hat) + ADAM_EPS) + ADAM_WD * w_ref[...])
    m2_ref[...] = m2
    v2_ref[...] = v2
    gout_ref[...] = gr


def _adamw(w, g, m, v, name):
    rows, cols = w.shape
    tm = 128 if rows % 128 == 0 else rows

    def body(*refs):
        _adamw_tile(*refs)

    spec = pl.BlockSpec((tm, cols), lambda i: (i, 0))
    sds = jax.ShapeDtypeStruct((rows, cols), F32)
    return pl.pallas_call(body, grid=(rows // tm,), in_specs=[spec] * 4, out_specs=[spec] * 4,
                          out_shape=[sds] * 4, name=name)(w, g, m, v)


def _adamw_half(w, g, m, v, half, name, other=()):
    rows, cols = w.shape
    tm = 128
    nt = rows // 2 // tm

    def body(half_ref, *refs):
        _adamw_tile(*refs[:4], *refs[4 + len(other):])

    spec = pl.BlockSpec((tm, cols), lambda i, h: (h[0] * nt + i, 0))
    sds = jax.ShapeDtypeStruct((rows, cols), F32)
    return pl.pallas_call(
        body,
        grid_spec=pltpu.PrefetchScalarGridSpec(
            num_scalar_prefetch=1, grid=(nt,), in_specs=[spec] * 4 + [_ANY] * len(other), out_specs=[spec] * 4),
        out_shape=[sds] * 4, input_output_aliases={5 + k: k for k in range(len(other))}, name=name,
    )(half, w, g, m, v, *other)


def kernel(x, norm_g, w_in, b_merge, conv_w, w_out_conv, w_out_attn, w_o, final_g, loss_target, m_norm_g, m_w_in, m_b_merge, m_conv_w, m_w_out_conv, m_w_out_attn, m_w_o, m_final_g, v_norm_g, v_w_in, v_b_merge, v_conv_w, v_w_out_conv, v_w_out_attn, v_w_o, v_final_g):
    mx, my, mc = _place()
    chip = (2 * mx + my).astype(jnp.int32)
    seq = x.shape[1]

    chip1 = chip.reshape(1)
    slots = [_place_shard(w[0], chip1, MXU_DTYPE) for w in (w_in, w_out_conv, w_out_attn, w_o)]
    taps_slot = _place_shard(jnp.pad(conv_w[0], ((0, 5), (0, 0))), chip1, F32)
    gather_near = _SplitExchange("gather_w_in_near", [slots[0], taps_slot], [], 4,
                                 _gather_copies_to((0, 1), whole=(1,)))
    st = _Step(x[0], loss_target[0], norm_g, chip1, after=gather_near.token)
    st.project_own(w_in[0])
    near, _, _ = gather_near.wait([st.ut, st.proj_own])
    gather_far = _SplitExchange("gather_w_in_far", near, [], 2, _gather_copies_to((2,), whole=(1,)))
    (w4,) = _forward_to_sibling(gather_far.bufs[:1], flips=(0, 1))
    st.project(w4, (2, 1))
    (w4, taps4), _, out_slots = gather_far.wait([st.proj_own], riders=slots[1:], bufs=[w4, gather_far.bufs[1]])
    gather_out = _SplitExchange("gather_w_out", out_slots, [], 9, _gather_copies_to((0, 1, 2)), riders=[w4])
    (w4,) = _forward_to_sibling(gather_out.riders, flips=(2,))
    st.project(w4, (3,))
    st.mixers(w4, jnp.concatenate([taps4[j, :3, :] for j in range(4)], axis=1))
    out_ws, _, _ = gather_out.wait(st.o)
    woc, woa, wo = [w.reshape(D_MODEL, D_MODEL) for w in _forward_to_sibling(out_ws)]
    st.merge_and_loss(woc, woa, wo, b_merge, final_g.reshape(1, D_MODEL))

    half = mc.astype(jnp.int32).reshape(1)
    where = jnp.stack([chip, mc.astype(jnp.int32)])
    out_grads = [g.reshape(4, -1, D_MODEL) for g in st.out_weight_grads()]
    to_sibling = _grads_to_sibling("out_grads_to_sibling", out_grads)
    st.conv_grads(after=to_sibling.token)
    out_grads, out_from_sibling, _ = to_sibling.wait(st.da4)
    to_chips = _grads_to_chips("out_grads_to_chips",
                               _add_halves(out_grads, out_from_sibling, where))
    d_w4 = st.in_weight_grad(after=to_chips.token)
    out_from_chips = to_chips.wait(st.dc3)[1]

    to_sibling = _grads_to_sibling("in_grad_to_sibling", [d_w4])
    gx_lo, dg_lo = st.input_grad(0, after=to_sibling.token)
    (d_w4,), (from_sibling,), _ = to_sibling.wait(gx_lo)
    to_chips = _grads_to_chips("in_grad_to_chips", _add_halves([d_w4], [from_sibling], where))

    out_reduced = [_add_chips(g, r, recv, where, after=to_chips.token)
                   for g, r, recv in zip(out_grads, out_from_sibling, out_from_chips)]
    share_out = _share_halves("share_out_grads", out_reduced)
    gx_hi, dg_hi = st.input_grad(1, after=share_out.token)
    grad_x = _to_natural(gx_lo, gx_hi)
    g_woc, g_woa, g_wo = share_out.wait(gx_hi)[0]

    small = _reduce_small([dg_lo + dg_hi, st.d_bias.reshape(2, D_MODEL), st.d_taps, st.d_final_g,
                           st.loss8.reshape(1, D_MODEL)])
    loss = (0.5 / D_MODEL) * small[7, 0]
    g_taps = lax.dynamic_slice(small[3:6], (0, chip * (D_MODEL // 4)), (3, D_MODEL // 4))
    upd = {
        "norm_g": _adamw(norm_g, small[0:1], m_norm_g, v_norm_g, "adamw_norm_g"),
        "b_merge": _adamw(b_merge, small[1:3].reshape(1, 2 * D_MODEL), m_b_merge, v_b_merge, "adamw_b_merge"),
        "conv_w": _adamw(conv_w[0], g_taps, m_conv_w[0], v_conv_w[0], "adamw_conv_w"),
        "w_out_conv": _adamw(w_out_conv[0], g_woc, m_w_out_conv[0], v_w_out_conv[0], "adamw_w_out_conv"),
        "w_out_attn": _adamw(w_out_attn[0], g_woa, m_w_out_attn[0], v_w_out_attn[0], "adamw_w_out_attn"),
        "w_o": _adamw(w_o[0], g_wo, m_w_o[0], v_w_o[0], "adamw_w_o"),
        "final_g": _adamw(final_g.reshape(1, D_MODEL), small[6:7], m_final_g.reshape(1, D_MODEL),
                          v_final_g.reshape(1, D_MODEL), "adamw_final_g"),
    }
    behind = [grad_x] + [u[0] for u in upd.values()]
    in_reduced = _add_chips(d_w4, from_sibling, to_chips.wait(behind)[1][0], where)

    share_in = _share_halves("share_in_grad", [in_reduced])
    w_in_args = (w_in[0], m_w_in[0], v_w_in[0])
    own_rows = _adamw_half(w_in_args[0], share_in.bufs[0], *w_in_args[1:], half, "adamw_w_in_own_rows")
    (g_w_in,) = share_in.wait(own_rows[0])[0]
    upd["w_in"] = _adamw_half(w_in_args[0], g_w_in, *w_in_args[1:], 1 - half, "adamw_w_in_sibling_rows",
                              other=own_rows)

    names = ["norm_g", "w_in", "b_merge", "conv_w", "w_out_conv", "w_out_attn", "w_o", "final_g"]
    shapes = [norm_g.shape, w_in.shape, b_merge.shape, conv_w.shape, w_out_conv.shape, w_out_attn.shape,
              w_o.shape, final_g.shape]
    outs = [loss, grad_x.reshape(1, seq, D_MODEL)]
    for k in (3, 0, 1, 2):
        outs += [upd[n][k].reshape(s) for n, s in zip(names, shapes)]
    return tuple(outs)
```

```python
import functools

import numpy as np
import jax
import jax.numpy as jnp
from jax import lax
from jax.experimental import pallas as pl
from jax.experimental.pallas import tpu as pltpu

F32 = jnp.float32
BF16 = jnp.bfloat16
MXU_DTYPE = jnp.bfloat16
ACT_DTYPE = jnp.bfloat16

D_MODEL = 1024
N_HEADS = 16
HEAD_DIM = 64
QB = 128
N_RES = 16
LANES = 128
HP = N_HEADS * HEAD_DIM // LANES
IN_COLS = 10 * D_MODEL
SHARD_COLS = IN_COLS // 4
EPS = 1e-6
NEG = -1e30

ADAM_LR, ADAM_B1, ADAM_B2, ADAM_EPS, ADAM_WD, ADAM_STEP = 0.001, 0.9, 0.999, 1e-08, 0.01, 10

PATTERNS = {1: (16, 8), 4: (4, 32), 16: (1, 128)}

_NN = (((1,), (0,)), ((), ()))
_NT = (((1,), (1,)), ((), ()))


def _dot(a, b):
    return lax.dot_general(a.astype(MXU_DTYPE), b.astype(MXU_DTYPE), _NN, preferred_element_type=F32)


def _dot_nt(a, b):
    return lax.dot_general(a.astype(MXU_DTYPE), b.astype(MXU_DTYPE), _NT, preferred_element_type=F32)


def _split3(x):
    hi = x.astype(BF16)
    r1 = x - hi.astype(F32)
    mid = r1.astype(BF16)
    lo = (r1 - mid.astype(F32)).astype(BF16)
    return hi, mid, lo


def _select_cols(x, sel, terms):
    return sum(lax.dot_general(t, sel, _NN, preferred_element_type=F32) for t in _split3(x)[:terms])


def _sigmoid(z):
    return 1.0 / (1.0 + jnp.exp(-z))


def _head_expand_matrix():
    e = np.zeros((LANES, D_MODEL), np.float32)
    for h in range(N_HEADS):
        e[8 * h, HEAD_DIM * h:HEAD_DIM * (h + 1)] = 1.0
    return jnp.asarray(e, BF16)


def _head_sum_matrix():
    e = np.zeros((D_MODEL, LANES), np.float32)
    for h in range(N_HEADS):
        e[HEAD_DIM * h:HEAD_DIM * (h + 1), 8 * h:8 * (h + 1)] = 1.0
    return jnp.asarray(e, BF16)


def _attn_tables(d):
    g_n, rq = PATTERNS[d]
    q_n = g_n * rq
    gq, iq = np.arange(q_n) // rq, np.arange(q_n) % rq

    def tab(kn, base):
        k_n = g_n * kn
        gk, jk = np.arange(k_n) // kn, np.arange(k_n) % kn
        delta = g_n * (base + iq[:, None] - jk[None, :]) + gq[:, None] - gk[None, :]
        valid = (delta >= 0) & (delta <= QB)
        dist = np.where(valid, d * delta, 0).astype(np.float32)
        madd = np.where(valid, 0.0, NEG).astype(np.float32)
        return dist, madd

    d0, m0 = tab(rq if g_n == 1 else 2 * rq, 0)
    d1, m1 = tab(2 * rq, rq)
    return d0, m0, d1, m1


def _alibi_slopes():
    return jnp.exp2(-8.0 * jnp.arange(1, N_HEADS + 1, dtype=F32) / N_HEADS)


def _to_residue_major(x, tgt, after=None):
    s_n, c_n = x.shape
    lr = s_n // N_RES
    extra = [] if after is None else [after]

    def body(x_ref, t_ref, *rest):
        xo_ref, to_ref = rest[-2:]
        for r in range(N_RES):
            xo_ref[r] = x_ref[pl.ds(r, lr, stride=N_RES), :]
            to_ref[r] = t_ref[pl.ds(r, lr, stride=N_RES), :]

    nat = pl.BlockSpec((s_n, LANES), lambda j: (0, j))
    res = pl.BlockSpec((N_RES, lr, LANES), lambda j: (0, 0, j))
    xo, to = pl.pallas_call(
        body, grid=(c_n // LANES,),
        in_specs=[nat, nat] + [pl.BlockSpec((8, LANES), lambda j: (0, 0))] * len(extra),
        out_specs=[res, res],
        out_shape=[jax.ShapeDtypeStruct((N_RES, lr, c_n), F32)] * 2,
        name="perm_in",
    )(x, tgt, *extra)
    return xo.reshape(s_n, c_n), to.reshape(s_n, c_n)


def _to_natural(gx_lo, gx_hi):
    half_rows, c_n = gx_lo.shape
    lr = half_rows // (N_RES // 2)

    def body(lo_ref, hi_ref, o_ref):
        for r in range(N_RES):
            o_ref[pl.ds(r, lr, stride=N_RES), :] = lo_ref[r] if r < N_RES // 2 else hi_ref[r - N_RES // 2]

    half = pl.BlockSpec((N_RES // 2, lr, LANES), lambda j: (0, 0, j))
    return pl.pallas_call(
        body, grid=(c_n // LANES,),
        in_specs=[half, half],
        out_specs=pl.BlockSpec((2 * half_rows, LANES), lambda j: (0, j)),
        out_shape=jax.ShapeDtypeStruct((2 * half_rows, c_n), F32),
        name="perm_out",
    )(gx_lo.reshape(N_RES // 2, lr, c_n), gx_hi.reshape(N_RES // 2, lr, c_n))


def _rms_in(xp, norm_g):
    s_n, c_n = xp.shape
    tm = 512

    def body(x_ref, g_ref, u_ref, ut_ref):
        x = x_ref[...]
        r = lax.rsqrt(jnp.mean(x * x, axis=-1, keepdims=True) + EPS)
        u = x * r * g_ref[...]
        u_ref[...] = u.astype(u_ref.dtype)
        ut_ref[...] = u.T.astype(ut_ref.dtype)

    return pl.pallas_call(
        body, grid=(s_n // tm,),
        in_specs=[pl.BlockSpec((tm, c_n), lambda i: (i, 0)), pl.BlockSpec((1, c_n), lambda i: (0, 0))],
        out_specs=[pl.BlockSpec((tm, c_n), lambda i: (i, 0)), pl.BlockSpec((c_n, tm), lambda i: (0, i))],
        out_shape=[jax.ShapeDtypeStruct((s_n, c_n), ACT_DTYPE), jax.ShapeDtypeStruct((c_n, s_n), ACT_DTYPE)],
        name="rms_in",
    )(xp, norm_g)


def _in_proj(u, chip, w_own=None, w4=None, partial=None, others=()):
    s_n = u.shape[0]
    tn, cm = 512, 512
    per = SHARD_COLS // tn
    own = partial is None

    def body(chip_ref, a_ref, b_ref, *rest):
        o_ref = rest[-1]
        b = b_ref[...]
        for c in range(s_n // cm):
            o_ref[c * cm:(c + 1) * cm, :] = _dot(a_ref[c * cm:(c + 1) * cm, :], b).astype(o_ref.dtype)

    def shard(n, chip_ref):
        if own:
            return chip_ref[0]
        mask = others[-1]
        for i, m in enumerate(others[:-1]):
            mask = jnp.where(n // per == i, m, mask)
        return jnp.bitwise_xor(chip_ref[0], mask)

    w_spec = (pl.BlockSpec((D_MODEL, tn), lambda n, c: (0, n)) if own else
              pl.BlockSpec((None, D_MODEL, tn), lambda n, c: (shard(n, c), 0, n % per)))
    return pl.pallas_call(
        body,
        grid_spec=pltpu.PrefetchScalarGridSpec(
            num_scalar_prefetch=1, grid=(per if own else len(others) * per,),
            in_specs=[pl.BlockSpec((s_n, D_MODEL), lambda n, c: (0, 0)), w_spec] + ([] if own else [_ANY]),
            out_specs=pl.BlockSpec((s_n, tn), lambda n, c: (0, shard(n, c) * per + n % per))),
        out_shape=jax.ShapeDtypeStruct((s_n, IN_COLS), ACT_DTYPE),
        input_output_aliases={} if own else {3: 0},
        name="in_proj_own" if own else "in_proj_" + "_".join(str(m) for m in others),
    )(*([chip, u, w_own] if own else [chip, u, w4, partial]))


def _conv_terms(xc_ref, cg_ref, r, row, lr, cache):
    def a_of(q):
        if q not in cache:
            cache[q] = cg_ref[q].astype(F32) * xc_ref[q].astype(F32)
        return cache[q]

    def shift_down(v):
        return jnp.where(row >= 1, pltpu.roll(v, 1, 0), 0.0)

    a = a_of(r)
    am1 = a_of(r - 1) if r >= 1 else shift_down(a_of(N_RES - 1))
    am2 = a_of(r - 2) if r >= 2 else shift_down(a_of(N_RES - 2 + r))
    return a, am1, am2


def _conv_fwd(proj, conv_w):
    s_n = proj.shape[0]
    lr = s_n // N_RES
    pv = proj.reshape(N_RES, lr, IN_COLS)

    def body(xc_ref, bg_ref, cg_ref, zc_ref, w_ref, hc_ref, hct_ref):
        w = w_ref[...]
        row = lax.broadcasted_iota(jnp.int32, (lr, LANES), 0)
        products = {}
        for r in range(N_RES):
            a, am1, am2 = _conv_terms(xc_ref, cg_ref, r, row, lr, products)
            c = w[0:1] * am2 + w[1:2] * am1 + w[2:3] * a
            z = zc_ref[r].astype(F32)
            hc = z * _sigmoid(z) * bg_ref[r].astype(F32) * c
            hc_ref[r] = hc.astype(hc_ref.dtype)
            hct_ref[:, r * lr:(r + 1) * lr] = hc.T.astype(hct_ref.dtype)

    def col(part):
        return pl.BlockSpec((N_RES, lr, LANES), lambda j: (0, 0, part * 8 + j))

    hc, hct = pl.pallas_call(
        body, grid=(D_MODEL // LANES,),
        in_specs=[col(0), col(1), col(2), col(3), pl.BlockSpec((3, LANES), lambda j: (0, j))],
        out_specs=[pl.BlockSpec((N_RES, lr, LANES), lambda j: (0, 0, j)),
                   pl.BlockSpec((LANES, s_n), lambda j: (j, 0))],
        out_shape=[jax.ShapeDtypeStruct((N_RES, lr, D_MODEL), ACT_DTYPE),
                   jax.ShapeDtypeStruct((D_MODEL, s_n), ACT_DTYPE)],
        name="conv_fwd",
    )(pv, pv, pv, pv, conv_w)
    return hc.reshape(s_n, D_MODEL), hct


RES_PER_STEP = 8
CLASSES_PER_STEP = 2
ATTN_BATCH = 16

_BNT = (((2,), (2,)), ((0,), (0,)))
_BNN = (((2,), (1,)), ((0,), (0,)))


def _bdot(a, b, dims):
    return lax.dot_general(a.astype(MXU_DTYPE), b.astype(MXU_DTYPE), dims, preferred_element_type=F32)


def _pattern_view_shape(s_n, c_n, g_n, lead=()):
    lr = s_n // N_RES
    return (*lead, 4, 4, lr, c_n) if g_n == 4 else (*lead, N_RES, lr, c_n)


def _pattern_view(a, g_n, lead=()):
    return a.reshape(_pattern_view_shape(a.shape[-2], a.shape[-1], g_n, lead))


def _pattern_block(g_n, lr):
    if g_n == 4:
        return (4, CLASSES_PER_STEP, lr, LANES)
    return (16 if g_n == 16 else RES_PER_STEP, lr, LANES)


def _pattern_grid(g_n):
    return ({1: N_RES // RES_PER_STEP, 4: 4 // CLASSES_PER_STEP, 16: 1}[g_n], HP)


def _pattern_spec(g_n, lr, col_of_hp, lead=()):
    z = (0,) * len(lead)
    block = (*lead, *_pattern_block(g_n, lr))
    if g_n == 16:
        return pl.BlockSpec(block, lambda r, hp: (*z, 0, 0, col_of_hp(hp)))
    if g_n == 4:
        return pl.BlockSpec(block, lambda r, hp: (*z, 0, r, 0, col_of_hp(hp)))
    return pl.BlockSpec(block, lambda r, hp: (*z, r, 0, col_of_hp(hp)))


def _aligned(start, m):
    return start if isinstance(start, int) else pl.multiple_of(start, m)


class _Units:
    def __init__(self, g_n, rq):
        self.g_n, self.rq = g_n, rq
        self.per_res, self.paired = g_n == 1, rq == 8

    def plan(self, lr, size):
        if self.per_res:
            return [0], lr // self.rq - 1, lambda j: [pl.multiple_of(j * self.rq, self.rq)]
        if self.paired:
            per = min(size // 2, lr // 16)
            assert (lr // 16) % per == 0
            return ([i * 16 for i in range(per)], lr // 16 // per - 1,
                    lambda j: [pl.multiple_of((j * per + i) * 16, 16) for i in range(per)])
        step, classes = self.rq, range(CLASSES_PER_STEP)
        per = min(size // CLASSES_PER_STEP, lr // step)
        assert (lr // step) % per == 0
        return ([(c, i * step) for c in classes for i in range(per)], lr // step // per - 1,
                lambda j: [(c, pl.multiple_of((j * per + i) * step, step)) for c in classes for i in range(per)])

    def count(self, qs):
        return RES_PER_STEP if self.per_res else len(qs) * (2 if self.paired else 1)

    def _split(self, tiles, lo, rows):
        return tiles[:, lo:lo + rows].reshape(self.g_n * rows, LANES)

    def load_q(self, ref, qs):
        rq = self.rq
        if self.per_res:
            return ref[:, pl.ds(qs[0], rq), :]
        if self.paired:
            tiles = [ref[:, pl.ds(q, 16), :].astype(F32) for q in qs]
            return jnp.stack([self._split(t, lo, 8) for t in tiles for lo in (0, 8)])
        return jnp.stack([ref[:, c, pl.ds(q, rq), :].reshape(self.g_n * rq, LANES) for c, q in qs])

    def _key_rows(self, q, at_start):
        return (0, 2 * self.rq) if at_start and q == 0 else (_aligned(q - self.rq, self.rq), 2 * self.rq)

    def load_k(self, ref, qs, first):
        rq = self.rq
        if self.per_res:
            return ref[:, pl.ds(0, rq), :] if first else ref[:, pl.ds(_aligned(qs[0] - rq, rq), 2 * rq), :]
        if self.paired:
            out = []
            for i, q in enumerate(qs):
                if first and i == 0:
                    t = ref[:, 0:16, :].astype(F32)
                    out += [self._split(t, 0, 16)] * 2
                else:
                    t = ref[:, pl.ds(_aligned(q - 16, 16), 32), :].astype(F32)
                    out += [self._split(t, 8, 16), self._split(t, 16, 16)]
            return jnp.stack(out)
        rows = [(c, *self._key_rows(q, first)) for c, q in qs]
        return jnp.stack([ref[:, c, pl.ds(k0, n), :].reshape(self.g_n * n, LANES) for c, k0, n in rows])

    def store_q(self, ref, qs, val, add=False, lead=()):
        if self.per_res:
            pieces = [((), qs[0], self.rq, val)]
        elif self.paired:
            pieces = [((), q, 16, jnp.concatenate([val[2 * i].reshape(self.g_n, 8, LANES),
                                                   val[2 * i + 1].reshape(self.g_n, 8, LANES)], axis=1))
                      for i, q in enumerate(qs)]
        else:
            pieces = [((c,), q, self.rq, val[i].reshape(self.g_n, self.rq, LANES)) for i, (c, q) in enumerate(qs)]
        for cls, start, rows, v in pieces:
            idx = (*lead, slice(None), *cls, pl.ds(start, rows), slice(None))
            ref[idx] = (ref[idx] + v if add else v).astype(ref.dtype)

    def add_k(self, ref, qs, val, first):
        rq = self.rq
        if self.per_res:
            k0, n = (0, rq) if first else (_aligned(qs[0] - rq, rq), 2 * rq)
            ref[:, pl.ds(k0, n), :] += val
            return
        if self.paired:
            starts = [s for i, q in enumerate(qs)
                      for s in ((0, 0) if first and i == 0 else (_aligned(q - 8, 8), q))]
            rows = [((), s, 16) for s in starts]
        else:
            rows = [((c,), *self._key_rows(q, first)) for c, q in qs]
        for b, (cls, k0, n) in enumerate(rows):
            idx = (slice(None), *cls, pl.ds(k0, n), slice(None))
            ref[idx] += val[b].reshape(self.g_n, n, LANES)


def _batch_bias(un, qs, at_start, first_ref, general_ref):
    if not at_start:
        return general_ref[...][None]
    if un.per_res:
        return first_ref[...][None]
    if un.paired:
        return jnp.concatenate([first_ref[...][None]] + [general_ref[...][None]] * (un.count(qs) - 1), axis=0)
    return jnp.stack([(first_ref if q == 0 else general_ref)[...] for _, q in qs])


def _stack_heads(x, low):
    zero = jnp.zeros_like(x)
    return jnp.concatenate([jnp.where(low, x, zero), jnp.where(low, zero, x)], axis=1)


def _attn_fwd(proj, slopes, d):
    g_n, rq = PATTERNS[d]
    un = _Units(g_n, rq)
    s_n = proj.shape[0]
    lr = s_n // N_RES
    q_n = g_n * rq
    d0, m0, d1, m1 = _attn_tables(d)
    first, n_more, later = un.plan(lr, ATTN_BATCH)

    def body(sl_ref, q_ref, k_ref, v_ref, d0_ref, m0_ref, d1_ref, m1_ref, o_ref, lse_ref, b0_ref, b1_ref):
        hp = pl.program_id(1)

        @pl.when(hp == 0)
        def _():
            lse_ref[...] = jnp.zeros(lse_ref.shape, F32)

        for h in (0, 1):
            slope = sl_ref[2 * hp + h]
            b0_ref[h * q_n:(h + 1) * q_n, :] = m0_ref[...] - slope * d0_ref[...]
            b1_ref[h * q_n:(h + 1) * q_n, :] = m1_ref[...] - slope * d1_ref[...]

        lane = lax.broadcasted_iota(jnp.int32, (1, q_n, LANES), 2)
        low = lane < HEAD_DIM
        grp = lane // 8

        def batch(qs, at_start):
            qq = _stack_heads(un.load_q(q_ref, qs) * 0.125, low)
            s = _bdot(qq, un.load_k(k_ref, qs, at_start), _BNT) + _batch_bias(un, qs, at_start, b0_ref, b1_ref)
            m = jnp.max(s, axis=2, keepdims=True)
            p = jnp.exp(s - m)
            l = jnp.sum(p, axis=2, keepdims=True)
            o = _bdot(p, un.load_k(v_ref, qs, at_start), _BNN) * (1.0 / l)
            lse = m + jnp.log(l)
            un.store_q(o_ref, qs, jnp.where(low, o[:, :q_n], o[:, q_n:]))
            upd = jnp.where(grp == 2 * hp, lse[:, :q_n], 0.0) + jnp.where(grp == 2 * hp + 1, lse[:, q_n:], 0.0)
            un.store_q(lse_ref, qs, upd, add=True)

        batch(first, True)

        def more(j, carry):
            batch(later(j), False)
            return carry

        lax.fori_loop(1, 1 + n_more, more, 0)

    pv = _pattern_view(proj, g_n)
    full = lambda a: pl.BlockSpec(a.shape, lambda r, hp: (0, 0))
    o, lse = pl.pallas_call(
        body, grid=_pattern_grid(g_n),
        in_specs=[pl.BlockSpec(memory_space=pltpu.SMEM),
                  _pattern_spec(g_n, lr, lambda hp: 32 + hp),
                  _pattern_spec(g_n, lr, lambda hp: 40 + hp),
                  _pattern_spec(g_n, lr, lambda hp: 48 + hp),
                  full(d0), full(m0), full(d1), full(m1)],
        out_specs=[_pattern_spec(g_n, lr, lambda hp: hp), _pattern_spec(g_n, lr, lambda hp: 0)],
        out_shape=[jax.ShapeDtypeStruct(_pattern_view_shape(s_n, D_MODEL, g_n), ACT_DTYPE),
                   jax.ShapeDtypeStruct(_pattern_view_shape(s_n, LANES, g_n), F32)],
        scratch_shapes=[pltpu.VMEM((2 * q_n, d0.shape[1]), F32), pltpu.VMEM((2 * q_n, 2 * q_n), F32)],
        name=f"attn_fwd_d{d}",
    )(slopes, pv, pv, pv, d0, m0, d1, m1)
    return o.reshape(s_n, D_MODEL), lse.reshape(s_n, LANES)


def _attn_combine(outs, lses, proj):
    s_n = proj.shape[0]
    tm = 512

    def body(o1_ref, o2_ref, o3_ref, l1_ref, l2_ref, l3_ref, za_ref, e_ref, o_ref, lse_ref, ha_ref, hat_ref):
        ls = [l1_ref[...], l2_ref[...], l3_ref[...]]
        mx = jnp.maximum(jnp.maximum(ls[0], ls[1]), ls[2])
        den = sum(jnp.exp(l - mx) for l in ls)
        lse = mx + jnp.log(den)
        lse_ref[...] = lse
        o = jnp.zeros((tm, D_MODEL), F32)
        for l, oref in zip(ls, (o1_ref, o2_ref, o3_ref)):
            o = o + _select_cols(jnp.exp(l - lse), e_ref[...], terms=2) * oref[...].astype(F32)
        o_ref[...] = o.astype(o_ref.dtype)
        z = za_ref[...].astype(F32)
        ha = z * _sigmoid(z) * o
        ha_ref[...] = ha.astype(ha_ref.dtype)
        hat_ref[...] = ha.T.astype(hat_ref.dtype)

    row = lambda w: pl.BlockSpec((tm, w), lambda i: (i, 0))
    return pl.pallas_call(
        body, grid=(s_n // tm,),
        in_specs=[row(D_MODEL)] * 3 + [row(LANES)] * 3
        + [pl.BlockSpec((tm, D_MODEL), lambda i: (i, 7)), pl.BlockSpec((LANES, D_MODEL), lambda i: (0, 0))],
        out_specs=[row(D_MODEL), row(LANES), row(D_MODEL), pl.BlockSpec((D_MODEL, tm), lambda i: (0, i))],
        out_shape=[jax.ShapeDtypeStruct((s_n, D_MODEL), ACT_DTYPE), jax.ShapeDtypeStruct((s_n, LANES), F32),
                   jax.ShapeDtypeStruct((s_n, D_MODEL), ACT_DTYPE), jax.ShapeDtypeStruct((D_MODEL, s_n), ACT_DTYPE)],
        name="attn_combine",
    )(*outs, *lses, proj, _head_expand_matrix())


CHAIN_ROWS = 256


def _row_chains(tm):
    return [slice(r, r + CHAIN_ROWS) for r in range(0, tm, CHAIN_ROWS)]


def _gates(gc_ref, ga_ref, b_ref, rows):
    b = b_ref[...]
    gc = _sigmoid(gc_ref[rows, :].astype(F32) + b[:, :D_MODEL])
    ga = _sigmoid(ga_ref[rows, :].astype(F32) + b[:, D_MODEL:])
    return gc, ga


def _merge_loss(hc, ha, woc, woa, wo, proj, b_merge, xp, final_g, tgt):
    s_n = xp.shape[0]
    tm = 512

    def body(hc_ref, ha_ref, woc_ref, woa_ref, wo_ref, gc_ref, ga_ref, b_ref, x_ref, gf_ref, t_ref,
             yc_ref, ya_ref, dhb_ref, dgf_ref, loss_ref, dwo_ref, mgt_ref):
        i = pl.program_id(0)

        @pl.when(i == 0)
        def _():
            dgf_ref[...] = jnp.zeros(dgf_ref.shape, F32)
            loss_ref[...] = jnp.zeros(loss_ref.shape, F32)
            dwo_ref[...] = jnp.zeros(dwo_ref.shape, F32)

        gf = gf_ref[...]
        for rows in _row_chains(tm):
            yc = _dot(hc_ref[rows, :], woc_ref[...])
            ya = _dot(ha_ref[rows, :], woa_ref[...])
            gc, ga = _gates(gc_ref, ga_ref, b_ref, rows)
            mg = gc * yc + ga * ya
            yc_ref[rows, :] = yc.astype(yc_ref.dtype)
            ya_ref[rows, :] = ya.astype(ya_ref.dtype)
            mgt_ref[:, rows] = mg.T.astype(mgt_ref.dtype)
            h2 = x_ref[rows, :] + _dot(mg, wo_ref[...])
            r2 = lax.rsqrt(jnp.mean(h2 * h2, axis=-1, keepdims=True) + EPS)
            nrm = h2 * r2
            err = nrm * gf - t_ref[rows, :]
            e2 = (err * err).reshape(-1, 8, D_MODEL).sum(axis=0)
            loss_ref[...] += sum(e2[:, c * LANES:(c + 1) * LANES] for c in range(D_MODEL // LANES))
            dy = err * (1.0 / D_MODEL)
            dgf_ref[...] += jnp.sum(dy * nrm, axis=0, keepdims=True)
            dn = dy * gf
            dh2 = r2 * (dn - nrm * jnp.mean(dn * nrm, axis=-1, keepdims=True))
            dhb_ref[rows, :] = dh2.astype(dhb_ref.dtype)
        dwo_ref[...] += _dot(mgt_ref[...], dhb_ref[...])

    row = pl.BlockSpec((tm, D_MODEL), lambda i: (i, 0))
    wsp = pl.BlockSpec((D_MODEL, D_MODEL), lambda i: (0, 0), pipeline_mode=pl.Buffered(1))
    vec = lambda w: pl.BlockSpec((1, w), lambda i: (0, 0))
    act = jax.ShapeDtypeStruct((s_n, D_MODEL), ACT_DTYPE)
    return pl.pallas_call(
        body, grid=(s_n // tm,),
        in_specs=[row, row, wsp, wsp, wsp,
                  pl.BlockSpec((tm, D_MODEL), lambda i: (i, 8)), pl.BlockSpec((tm, D_MODEL), lambda i: (i, 9)),
                  vec(2 * D_MODEL), row, vec(D_MODEL), row],
        out_specs=[row, row, row, vec(D_MODEL), pl.BlockSpec((8, LANES), lambda i: (0, 0)),
                   pl.BlockSpec((D_MODEL, D_MODEL), lambda i: (0, 0))],
        out_shape=[act, act, act, jax.ShapeDtypeStruct((1, D_MODEL), F32), jax.ShapeDtypeStruct((8, LANES), F32),
                   jax.ShapeDtypeStruct((D_MODEL, D_MODEL), F32)],
        scratch_shapes=[pltpu.VMEM((D_MODEL, tm), MXU_DTYPE)],
        name="merge_loss",
    )(hc, ha, woc, woa, wo, proj, proj, b_merge, xp, final_g, tgt)


def _merge_bwd(dh2b, wo, woc, woa, yc, ya, proj, b_merge, o, hct, hat):
    s_n = dh2b.shape[0]
    tm = 512

    def body(dh_ref, wo_ref, woc_ref, woa_ref, yc_ref, ya_ref, gc_ref, ga_ref, b_ref, o_ref, za_ref, e_ref,
             hct_ref, hat_ref, dhc_ref, do_ref, dsum_ref, db3_ref, dbias_ref, dwoc_ref, dwoa_ref,
             dyc_ref, dya_ref):
        i = pl.program_id(0)

        @pl.when(i == 0)
        def _():
            dbias_ref[...] = jnp.zeros(dbias_ref.shape, F32)
            dwoc_ref[...] = jnp.zeros(dwoc_ref.shape, F32)
            dwoa_ref[...] = jnp.zeros(dwoa_ref.shape, F32)

        for rows in _row_chains(tm):
            dmg = _dot_nt(dh_ref[rows, :], wo_ref[...])
            gc, ga = _gates(gc_ref, ga_ref, b_ref, rows)
            dgc = dmg * yc_ref[rows, :].astype(F32) * gc * (1.0 - gc)
            dga = dmg * ya_ref[rows, :].astype(F32) * ga * (1.0 - ga)
            dbias_ref[:, :D_MODEL] += jnp.sum(dgc, axis=0, keepdims=True)
            dbias_ref[:, D_MODEL:] += jnp.sum(dga, axis=0, keepdims=True)
            dyc = dmg * gc
            dya = dmg * ga
            dyc_ref[rows, :] = dyc.astype(dyc_ref.dtype)
            dya_ref[rows, :] = dya.astype(dya_ref.dtype)
            dhc_ref[rows, :] = _dot_nt(dyc, woc_ref[...]).astype(dhc_ref.dtype)
            dha = _dot_nt(dya, woa_ref[...])
            z = za_ref[rows, :].astype(F32)
            sg = _sigmoid(z)
            ov = o_ref[rows, :].astype(F32)
            dout = dha * z * sg
            do_ref[rows, :] = dout.astype(do_ref.dtype)
            dsum_ref[rows, :] = _select_cols(dout * ov, e_ref[...], terms=2)
            db3_ref[0, rows, :] = (dha * ov * sg * (1.0 + z * (1.0 - sg))).astype(db3_ref.dtype)
            db3_ref[1, rows, :] = dgc.astype(db3_ref.dtype)
            db3_ref[2, rows, :] = dga.astype(db3_ref.dtype)
        dwoc_ref[...] += _dot(hct_ref[...], dyc_ref[...])
        dwoa_ref[...] += _dot(hat_ref[...], dya_ref[...])

    row = pl.BlockSpec((tm, D_MODEL), lambda i: (i, 0))
    col = pl.BlockSpec((D_MODEL, tm), lambda i: (0, i))
    wsp = pl.BlockSpec((D_MODEL, D_MODEL), lambda i: (0, 0), pipeline_mode=pl.Buffered(1))
    acc = pl.BlockSpec((D_MODEL, D_MODEL), lambda i: (0, 0))
    act = jax.ShapeDtypeStruct((s_n, D_MODEL), ACT_DTYPE)
    grad = jax.ShapeDtypeStruct((D_MODEL, D_MODEL), F32)
    return pl.pallas_call(
        body, grid=(s_n // tm,),
        in_specs=[row, wsp, wsp, wsp, row, row,
                  pl.BlockSpec((tm, D_MODEL), lambda i: (i, 8)), pl.BlockSpec((tm, D_MODEL), lambda i: (i, 9)),
                  pl.BlockSpec((1, 2 * D_MODEL), lambda i: (0, 0)), row,
                  pl.BlockSpec((tm, D_MODEL), lambda i: (i, 7)), pl.BlockSpec((D_MODEL, LANES), lambda i: (0, 0)),
                  col, col],
        out_specs=[row, row, pl.BlockSpec((tm, LANES), lambda i: (i, 0)),
                   pl.BlockSpec((3, tm, D_MODEL), lambda i: (0, i, 0)),
                   pl.BlockSpec((1, 2 * D_MODEL), lambda i: (0, 0)), acc, acc],
        out_shape=[act, act, jax.ShapeDtypeStruct((s_n, LANES), F32),
                   jax.ShapeDtypeStruct((3, s_n, D_MODEL), ACT_DTYPE),
                   jax.ShapeDtypeStruct((1, 2 * D_MODEL), F32), grad, grad],
        scratch_shapes=[pltpu.VMEM((tm, D_MODEL), MXU_DTYPE), pltpu.VMEM((tm, D_MODEL), MXU_DTYPE)],
        name="merge_bwd",
    )(dh2b, wo, woc, woa, yc, ya, proj, proj, b_merge, o, proj, _head_sum_matrix(), hct, hat)


def _behind(body, after, n_prefetch=0):
    if after is None:
        return body, [], []

    def ordered(*refs):
        body(*refs[:n_prefetch], *refs[n_prefetch + 1:])

    return ordered, [_ANY], [after]


def _conv_bwd(proj, conv_w, dhc, after=None):
    s_n = proj.shape[0]
    lr = s_n // N_RES
    pv = proj.reshape(N_RES, lr, IN_COLS)

    def body(xc_ref, bg_ref, cg_ref, zc_ref, w_ref, dhc_ref, da4_ref, dw_ref, dc_ref):
        w = w_ref[...]
        row = lax.broadcasted_iota(jnp.int32, (lr, LANES), 0)
        dw = [jnp.zeros((1, LANES), F32) for _ in range(3)]
        products = {}
        for r in range(N_RES):
            a, am1, am2 = _conv_terms(xc_ref, cg_ref, r, row, lr, products)
            c = w[0:1] * am2 + w[1:2] * am1 + w[2:3] * a
            z = zc_ref[r].astype(F32)
            sg = _sigmoid(z)
            sz = z * sg
            bg = bg_ref[r].astype(F32)
            dh = dhc_ref[r].astype(F32)
            da4_ref[1, r] = (dh * sz * c).astype(da4_ref.dtype)
            da4_ref[3, r] = (dh * bg * c * sg * (1.0 + z * (1.0 - sg))).astype(da4_ref.dtype)
            dc = dh * sz * bg
            dc_ref[r] = dc
            dw[0] = dw[0] + jnp.sum(dc * am2, axis=0, keepdims=True)
            dw[1] = dw[1] + jnp.sum(dc * am1, axis=0, keepdims=True)
            dw[2] = dw[2] + jnp.sum(dc * a, axis=0, keepdims=True)
        dw_ref[0:1, :] = dw[0]
        dw_ref[1:2, :] = dw[1]
        dw_ref[2:3, :] = dw[2]

        def shift_up(v):
            return jnp.where(row < lr - 1, pltpu.roll(v, lr - 1, 0), 0.0)

        for r in range(N_RES):
            dp1 = dc_ref[r + 1] if r + 1 < N_RES else shift_up(dc_ref[0])
            dp2 = dc_ref[r + 2] if r + 2 < N_RES else shift_up(dc_ref[r + 2 - N_RES])
            da = w[2:3] * dc_ref[r] + w[1:2] * dp1 + w[0:1] * dp2
            da4_ref[0, r] = (da * cg_ref[r].astype(F32)).astype(da4_ref.dtype)
            da4_ref[2, r] = (da * xc_ref[r].astype(F32)).astype(da4_ref.dtype)

    def col(part):
        return pl.BlockSpec((N_RES, lr, LANES), lambda j: (0, 0, part * 8 + j))

    kern, token_spec, token = _behind(body, after)
    da4, dw = pl.pallas_call(
        kern, grid=(D_MODEL // LANES,),
        in_specs=token_spec + [col(0), col(1), col(2), col(3), pl.BlockSpec((3, LANES), lambda j: (0, j)),
                               pl.BlockSpec((N_RES, lr, LANES), lambda j: (0, 0, j))],
        out_specs=[pl.BlockSpec((4, N_RES, lr, LANES), lambda j: (0, 0, 0, j)),
                   pl.BlockSpec((3, LANES), lambda j: (0, j))],
        out_shape=[jax.ShapeDtypeStruct((4, N_RES, lr, D_MODEL), ACT_DTYPE),
                   jax.ShapeDtypeStruct((3, D_MODEL), F32)],
        scratch_shapes=[pltpu.VMEM((N_RES, lr, LANES), F32)],
        name="conv_bwd",
    )(*token, pv, pv, pv, pv, conv_w, dhc.reshape(N_RES, lr, D_MODEL))
    return da4.reshape(4, s_n, D_MODEL), dw


def _attn_bwd(proj, dout, lse, dsum, slopes, d, prev=None, after=None):
    g_n, rq = PATTERNS[d]
    un = _Units(g_n, rq)
    s_n = proj.shape[0]
    lr = s_n // N_RES
    q_n = g_n * rq
    d0, m0, d1, m1 = (np.ascontiguousarray(t.T) for t in _attn_tables(d))
    first, n_more, later = un.plan(lr, ATTN_BATCH)
    bsz = un.count(first)

    def body(sl_ref, q_ref, k_ref, v_ref, do_ref, lse_ref, ds_ref, d0_ref, m0_ref, d1_ref, m1_ref, *rest):
        prev_ref = rest[0] if prev is not None else None
        out_ref, b0_ref, b1_ref, lt_ref, dt_ref, dk_ref, dv_ref = rest[-7:]
        hp = pl.program_id(1)
        for h in (0, 1):
            slope = sl_ref[2 * hp + h]
            b0_ref[:, h * q_n:(h + 1) * q_n] = m0_ref[...] - slope * d0_ref[...]
            b1_ref[:, h * q_n:(h + 1) * q_n] = m1_ref[...] - slope * d1_ref[...]
        if prev is None:
            dk_ref[...] = jnp.zeros(dk_ref.shape, F32)
            dv_ref[...] = jnp.zeros(dv_ref.shape, F32)
        else:
            out_ref[0] = prev_ref[0]
            dk_ref[...] = prev_ref[1].astype(F32)
            dv_ref[...] = prev_ref[2].astype(F32)
        low = lax.broadcasted_iota(jnp.int32, (1, q_n, LANES), 2) < HEAD_DIM
        row16 = pl.multiple_of(16 * hp, 16)

        def query_rows(stat_ref, t_ref, qs):
            tiles = un.load_q(stat_ref, qs)
            for b in range(bsz):
                t_ref[b] = tiles[b].T
            t16 = t_ref[:, pl.ds(row16, 16), :]
            return jnp.concatenate([t16[:, 0:1, :], t16[:, 8:9, :]], axis=2)

        def batch(qs, at_start):
            qq = _stack_heads(un.load_q(q_ref, qs) * 0.125, low)
            dd = _stack_heads(un.load_q(do_ref, qs), low)
            ks = un.load_k(k_ref, qs, at_start)
            vs = un.load_k(v_ref, qs, at_start)
            lrow = query_rows(lse_ref, lt_ref, qs)
            drow = query_rows(ds_ref, dt_ref, qs)
            pt = jnp.exp(_bdot(ks, qq, _BNT) + _batch_bias(un, qs, at_start, b0_ref, b1_ref) - lrow)
            dst = pt * (_bdot(vs, dd, _BNT) - drow)
            un.add_k(dv_ref, qs, _bdot(pt, dd, _BNN), at_start)
            un.add_k(dk_ref, qs, _bdot(dst, qq, _BNN), at_start)
            dq = _bdot(jnp.swapaxes(dst, 1, 2), ks, _BNN)
            un.store_q(out_ref, qs, jnp.where(low, dq[:, :q_n], dq[:, q_n:]) * 0.125, add=prev is not None,
                       lead=(0,))

        batch(first, True)

        def more(j, carry):
            batch(later(j), False)
            return carry

        lax.fori_loop(1, 1 + n_more, more, 0)
        out_ref[1] = dk_ref[...].astype(out_ref.dtype)
        out_ref[2] = dv_ref[...].astype(out_ref.dtype)

    pv = _pattern_view(proj, g_n)
    full = lambda a: pl.BlockSpec(a.shape, lambda r, hp: (0, 0))
    whole = _pattern_spec(g_n, lr, lambda hp: hp, lead=(3,))
    kern, token_spec, token = _behind(body, after)
    out = pl.pallas_call(
        kern, grid=_pattern_grid(g_n),
        in_specs=token_spec + [pl.BlockSpec(memory_space=pltpu.SMEM),
                               _pattern_spec(g_n, lr, lambda hp: 32 + hp),
                               _pattern_spec(g_n, lr, lambda hp: 40 + hp),
                               _pattern_spec(g_n, lr, lambda hp: 48 + hp),
                               _pattern_spec(g_n, lr, lambda hp: hp),
                               _pattern_spec(g_n, lr, lambda hp: 0),
                               _pattern_spec(g_n, lr, lambda hp: 0),
                               full(d0), full(m0), full(d1), full(m1)] + ([] if prev is None else [whole]),
        out_specs=whole,
        out_shape=jax.ShapeDtypeStruct(_pattern_view_shape(s_n, D_MODEL, g_n, lead=(3,)), ACT_DTYPE),
        scratch_shapes=[pltpu.VMEM((d0.shape[0], 2 * q_n), F32), pltpu.VMEM((2 * q_n, 2 * q_n), F32),
                        pltpu.VMEM((bsz, LANES, q_n), F32), pltpu.VMEM((bsz, LANES, q_n), F32),
                        pltpu.VMEM(_pattern_block(g_n, lr), F32), pltpu.VMEM(_pattern_block(g_n, lr), F32)],
        name=f"attn_bwd_d{d}",
    )(*token, slopes, pv, pv, pv, _pattern_view(dout, g_n), _pattern_view(lse, g_n), _pattern_view(dsum, g_n),
      d0, m0, d1, m1, *([] if prev is None else [_pattern_view(prev, g_n, lead=(3,))]))
    return out.reshape(3, s_n, D_MODEL)


def _part_index(step, per, lo, n):
    return jnp.clip(step // per - lo, 0, n - 1)


def _dw_in(ut, da4, dc3, db3):
    s_n = ut.shape[1]
    tn = 512
    per = D_MODEL // tn
    shard_blocks = SHARD_COLS // tn

    def body(a_ref, p0_ref, p1_ref, p2_ref, o_ref):
        part = pl.program_id(0) // per

        @pl.when(part < 4)
        def _():
            o_ref[...] = _dot(a_ref[...], p0_ref[...])

        @pl.when((part >= 4) & (part < 7))
        def _():
            o_ref[...] = _dot(a_ref[...], p1_ref[...])

        @pl.when(part >= 7)
        def _():
            o_ref[...] = _dot(a_ref[...], p2_ref[...])

    def pspec(lo, n):
        def index(j):
            part = j // per
            col = jnp.where(part < lo, 0, jnp.where(part >= lo + n, per - 1, j % per))
            return _part_index(j, per, lo, n), 0, col
        return pl.BlockSpec((None, s_n, tn), index)

    return pl.pallas_call(
        body, grid=(IN_COLS // tn,),
        in_specs=[pl.BlockSpec((D_MODEL, s_n), lambda j: (0, 0), pipeline_mode=pl.Buffered(1)),
                  pspec(0, 4), pspec(4, 3), pspec(7, 3)],
        out_specs=pl.BlockSpec((None, D_MODEL, tn), lambda j: (j // shard_blocks, 0, j % shard_blocks)),
        out_shape=jax.ShapeDtypeStruct((4, D_MODEL, SHARD_COLS), F32),
        name="dw_in",
    )(ut, da4, dc3, db3)


def _input_grad(da4, dc3, db3, w4, xp, norm_g, dh2, row0, rows, after=None):
    tm, tk = 256, 512
    per = D_MODEL // tk
    shard_blocks = SHARD_COLS // tk
    m0 = row0 // tm

    def body(p0_ref, p1_ref, p2_ref, w_ref, x_ref, g_ref, dh_ref, gx_ref, dg_ref):
        @pl.when(pl.program_id(0) == 0)
        def _():
            dg_ref[...] = jnp.zeros(dg_ref.shape, F32)

        du = None
        for k in range(IN_COLS // tk):
            part, cols = k // per, pl.ds((k % per) * tk, tk)
            ref, slot = (p0_ref, part) if part < 4 else (p1_ref, part - 4) if part < 7 else (p2_ref, part - 7)
            d = _dot_nt(ref[slot, :, cols], w_ref[k // shard_blocks, :, pl.ds((k % shard_blocks) * tk, tk)])
            du = d if du is None else du + d
        x = x_ref[...]
        r = lax.rsqrt(jnp.mean(x * x, axis=-1, keepdims=True) + EPS)
        nrm = x * r
        dg_ref[...] += jnp.sum(du * nrm, axis=0, keepdims=True)
        dn = du * g_ref[...]
        gx_ref[...] = dh_ref[...].astype(F32) + r * (dn - nrm * jnp.mean(dn * nrm, axis=-1, keepdims=True))

    def pspec(n):
        return pl.BlockSpec((n, tm, D_MODEL), lambda m: (0, m0 + m, 0))

    row_in = pl.BlockSpec((tm, D_MODEL), lambda m: (m0 + m, 0))
    vec = pl.BlockSpec((1, D_MODEL), lambda m: (0, 0))
    kern, token_spec, token = _behind(body, after)
    return pl.pallas_call(
        kern, grid=(rows // tm,),
        in_specs=token_spec + [pspec(4), pspec(3), pspec(3),
                               pl.BlockSpec(w4.shape, lambda m: (0, 0, 0), pipeline_mode=pl.Buffered(1)),
                               row_in, vec, row_in],
        out_specs=[pl.BlockSpec((tm, D_MODEL), lambda m: (m, 0)), vec],
        out_shape=[jax.ShapeDtypeStruct((rows, D_MODEL), F32), jax.ShapeDtypeStruct((1, D_MODEL), F32)],
        name="input_grad",
    )(*token, da4, dc3, db3, w4, xp, norm_g, dh2)


class _Step:
    def __init__(self, x, tgt, norm_g, chip, after=None):
        self.norm_g, self.chip = norm_g, chip
        self.slopes = _alibi_slopes()
        self.xp, self.tp = _to_residue_major(x, tgt, after)
        self.u, self.ut = _rms_in(self.xp, norm_g)

    def project_own(self, w_own):
        self.proj_own = _in_proj(self.u, self.chip, w_own=w_own)

    def project(self, w4, others):
        self.proj_own = _in_proj(self.u, self.chip, w4=w4, partial=self.proj_own, others=others)

    def mixers(self, w4, taps):
        self.w4, self.taps, self.proj = w4, taps, self.proj_own
        self.hc, self.hct = _conv_fwd(self.proj, taps)
        fwd = [_attn_fwd(self.proj, self.slopes, d) for d in PATTERNS]
        self.o, self.lse, self.ha, self.hat = _attn_combine([f[0] for f in fwd], [f[1] for f in fwd], self.proj)

    def merge_and_loss(self, woc, woa, wo, b_merge, final_g):
        self.woc, self.woa, self.wo, self.b_merge = woc, woa, wo, b_merge
        (self.yc, self.ya, self.dh2b, self.d_final_g, self.loss8, self.d_wo) = _merge_loss(
            self.hc, self.ha, woc, woa, wo, self.proj, b_merge, self.xp, final_g, self.tp)

    def out_weight_grads(self):
        (self.dhc, self.dout, self.dsum, self.db3, self.d_bias, d_woc, d_woa) = _merge_bwd(
            self.dh2b, self.wo, self.woc, self.woa, self.yc, self.ya, self.proj, self.b_merge, self.o,
            self.hct, self.hat)
        return d_woc, d_woa, self.d_wo

    def conv_grads(self, after=None):
        self.da4, self.d_taps = _conv_bwd(self.proj, self.taps, self.dhc, after)

    def in_weight_grad(self, after=None):
        self.dc3 = None
        for d in PATTERNS:
            self.dc3 = _attn_bwd(self.proj, self.dout, self.lse, self.dsum, self.slopes, d, prev=self.dc3,
                                 after=after if self.dc3 is None else None)
        return _dw_in(self.ut, self.da4, self.dc3, self.db3)

    def input_grad(self, half, after=None):
        rows = self.xp.shape[0] // 2
        return _input_grad(self.da4, self.dc3, self.db3, self.w4, self.xp, self.norm_g, self.dh2b,
                           half * rows, rows, after)


def _local_grads(x, tgt, norm_g, w4, b_merge, conv_w, woc, woa, wo, final_g):
    st = _Step(x, tgt, norm_g, jnp.zeros((1,), jnp.int32))
    st.project_own(w4[0])
    st.project(w4, (2, 1))
    st.project(w4, (3,))
    st.mixers(w4, conv_w)
    st.merge_and_loss(woc, woa, wo, b_merge, final_g)
    d_woc, d_woa, d_wo = st.out_weight_grads()
    st.conv_grads()
    d_w4 = st.in_weight_grad()
    gx_lo, dg_lo = st.input_grad(0)
    gx_hi, dg_hi = st.input_grad(1)
    return (st.loss8, _to_natural(gx_lo, gx_hi), dg_lo + dg_hi, d_w4, st.d_bias, st.d_taps, d_woc, d_woa, d_wo,
            st.d_final_g)


MESH = pl.DeviceIdType.MESH
_CHIP_FLIPS = ((1, 0), (0, 1), (1, 1))
_ANY = pl.BlockSpec(memory_space=pl.ANY)


def _place():
    return lax.axis_index("x"), lax.axis_index("y"), lax.axis_index("c")


def _flip(v, f):
    return 1 - v if f else v


def _remote(src, dst, send_sems, recv_sems, k, device):
    return pltpu.make_async_remote_copy(src_ref=src, dst_ref=dst, send_sem=send_sems.at[k], recv_sem=recv_sems.at[k],
                                        device_id=device, device_id_type=MESH)


def _place_shard(ws, chip, dtype):
    n = len(ws)
    rows, cols = ws[0].shape
    tm = min(rows, 128)

    def body(chip_ref, *refs):
        for a in range(n):
            refs[n + a][0] = refs[a][...].astype(dtype)

    return pl.pallas_call(
        body,
        grid_spec=pltpu.PrefetchScalarGridSpec(
            num_scalar_prefetch=1, grid=(rows // tm,),
            in_specs=[pl.BlockSpec((tm, cols), lambda i, chip_ref: (i, 0))] * n,
            out_specs=[pl.BlockSpec((1, tm, cols), lambda i, chip_ref: (chip_ref[0], i, 0))] * n),
        out_shape=[jax.ShapeDtypeStruct((4, rows, cols), dtype)] * n,
        name="place_shard",
    )(chip, *ws)


def _gather_copies_to(flips, whole=()):
    def copies(arrs, _, send_sems, recv_sems):
        x, y, c = _place()
        out = []
        for a, arr in enumerate(arrs):
            h = arr.shape[1] // 2
            mine = arr.at[2 * x + y] if a in whole else arr.at[2 * x + y, pl.ds(pl.multiple_of(c * h, 8), h)]
            for i, t in enumerate(flips):
                fx, fy = _CHIP_FLIPS[t]
                out.append(_remote(mine, mine, send_sems, recv_sems, len(flips) * a + i,
                                   (_flip(x, fx), _flip(y, fy), c)))
        return out
    return copies


def _forward_to_sibling(arrs, flips=(0, 1, 2)):
    n = len(arrs)

    def body(*refs):
        outs = refs[n:2 * n]
        send_sems, recv_sems = refs[2 * n:]
        x, y, c = _place()
        sibling = (x, y, 1 - c)
        started = []
        for a in range(n):
            h = outs[a].shape[1] // 2
            rows = pl.ds(pl.multiple_of(c * h, 8), h)
            for t in flips:
                fx, fy = _CHIP_FLIPS[t]
                landed = outs[a].at[2 * _flip(x, fx) + _flip(y, fy), rows]
                cp = _remote(landed, landed, send_sems, recv_sems, 3 * a + t, sibling)
                cp.start()
                started.append(cp)
        for a in range(n):
            h = outs[a].shape[1] // 2
            rows = pl.ds(pl.multiple_of((1 - c) * h, 8), h)
            for t in flips:
                fx, fy = _CHIP_FLIPS[t]
                handed = outs[a].at[2 * _flip(x, fx) + _flip(y, fy), rows]
                _remote(handed, handed, send_sems, recv_sems, 3 * a + t, sibling).wait_recv()
        for cp in started:
            cp.wait_send()

    return pl.pallas_call(
        body, in_specs=[_ANY] * n, out_specs=[_ANY] * n,
        out_shape=[jax.ShapeDtypeStruct(s.shape, s.dtype) for s in arrs],
        input_output_aliases={a: a for a in range(n)},
        scratch_shapes=[pltpu.SemaphoreType.DMA((3 * n,)), pltpu.SemaphoreType.DMA((3 * n,))],
        name="gathered_to_sibling_" + "".join(str(t) for t in flips),
    )(*arrs)


_HBM = pl.BlockSpec(memory_space=pltpu.HBM)
_SEM = pl.BlockSpec(memory_space=pltpu.SEMAPHORE)
_EFFECT = pltpu.SideEffectType.DATAFLOW_SIDE_EFFECTING


class _SplitExchange:
    def __init__(self, name, srcs, land_shapes, n_copies, copies, riders=()):
        self.name, self.n, self.nl, self.copies = name, len(srcs), len(land_shapes), copies
        n, nb = self.n, len(srcs) + len(land_shapes)
        lands = [lax.empty(s.shape, s.dtype) for s in land_shapes]
        bufs = [pltpu.with_memory_space_constraint(a, pltpu.HBM) for a in (*srcs, *lands, *riders)]
        na = len(bufs)

        def body(*refs):
            send_sems, recv_sems = refs[na], refs[na + 1]
            for cp in copies(refs[:n], refs[n:nb], send_sems, recv_sems):
                cp.start()
            refs[-1][...] = jnp.zeros(refs[-1].shape, F32)

        outs = pl.pallas_call(
            body, name=name + "_start",
            in_specs=[_HBM] * na,
            out_specs=[_SEM, _SEM] + [_HBM] * na + [pl.BlockSpec(memory_space=pltpu.VMEM)],
            out_shape=[pltpu.SemaphoreType.DMA((n_copies,)), pltpu.SemaphoreType.DMA((n_copies,))]
            + [pltpu.HBM(b.shape, b.dtype) for b in bufs] + [jax.ShapeDtypeStruct((8, LANES), F32)],
            input_output_aliases={i: 2 + i for i in range(na)},
            compiler_params=pltpu.CompilerParams(has_side_effects=_EFFECT),
        )(*bufs)
        self.sems, self.bufs, self.riders, self.token = outs[:2], outs[2:2 + nb], outs[2 + nb:2 + na], outs[-1]

    def wait(self, done, riders=(), bufs=None):
        n, nb, copies = self.n, self.n + self.nl, self.copies
        bufs = [*(self.bufs if bufs is None else bufs),
                *[pltpu.with_memory_space_constraint(a, pltpu.HBM) for a in riders]]
        na = len(bufs)
        done = list(done) if isinstance(done, (list, tuple)) else [done]

        def body(*refs):
            send_sems, recv_sems = refs[na], refs[na + 1]
            for cp in copies(refs[:n], refs[n:nb], send_sems, recv_sems):
                cp.wait_send()
                cp.wait_recv()

        outs = pl.pallas_call(
            body, name=self.name + "_wait",
            in_specs=[_HBM] * na + [_SEM, _SEM] + [_ANY] * len(done),
            out_specs=[_HBM] * na,
            out_shape=[pltpu.HBM(b.shape, b.dtype) for b in bufs],
            input_output_aliases={i: i for i in range(na)},
            compiler_params=pltpu.CompilerParams(has_side_effects=_EFFECT),
        )(*bufs, *self.sems, *done)
        return outs[:n], outs[n:nb], outs[nb:]


def _sibling_copies(srcs, lands, send_sems, recv_sems):
    x, y, c = _place()
    out = []
    for a, (src, land) in enumerate(zip(srcs, lands)):
        h = src.shape[1] // 2
        theirs = pl.ds(pl.multiple_of((1 - c) * h, 8), h)
        out.append(_remote(src.at[:, theirs], land, send_sems, recv_sems, a, (x, y, 1 - c)))
    return out


def _grads_to_sibling(name, grads):
    shapes = [jax.ShapeDtypeStruct((4, g.shape[1] // 2, g.shape[2]), g.dtype) for g in grads]
    return _SplitExchange(name, grads, shapes, len(grads), _sibling_copies)


def _chip_copies(srcs, lands, send_sems, recv_sems):
    x, y, c = _place()
    out = []
    for a, (src, land) in enumerate(zip(srcs, lands)):
        for t, (fx, fy) in enumerate(_CHIP_FLIPS):
            tx, ty = _flip(x, fx), _flip(y, fy)
            out.append(_remote(src.at[2 * tx + ty], land.at[t], send_sems, recv_sems, 3 * a + t, (tx, ty, c)))
    return out


def _grads_to_chips(name, parts):
    shapes = [jax.ShapeDtypeStruct((3, *p.shape[1:]), p.dtype) for p in parts]
    return _SplitExchange(name, parts, shapes, 3 * len(parts), _chip_copies)


def _add_halves(g, r, where):
    n = len(g)
    _, rows, cols = g[0].shape
    h = rows // 2
    tm = min(h, 128)
    nt = h // tm

    def body(where_ref, *refs):
        for a in range(n):
            refs[2 * n + a][...] = (refs[a][...] + refs[n + a][...]).astype(BF16)

    def other(j, w):
        return jnp.bitwise_xor(w[0], j + 1)

    spec = pl.BlockSpec((1, tm, cols), lambda j, i, w: (other(j, w), i, 0))
    return pl.pallas_call(
        body,
        grid_spec=pltpu.PrefetchScalarGridSpec(
            num_scalar_prefetch=1, grid=(3, nt),
            in_specs=[pl.BlockSpec((1, tm, cols), lambda j, i, w: (other(j, w), w[1] * nt + i, 0))] * n + [spec] * n,
            out_specs=[spec] * n),
        out_shape=[jax.ShapeDtypeStruct((4, h, cols), BF16)] * n,
        name="add_sibling_grads",
    )(where, *g, *r)


def _add_chips(g, r, recv, where, after=None):
    _, h, cols = r.shape
    tm = min(h, 128)
    nt = h // tm

    def body(where_ref, g_ref, r_ref, recv_ref, out_ref):
        own = g_ref[0] + r_ref[0]
        out_ref[...] = ((own + recv_ref[0].astype(F32)) + recv_ref[1].astype(F32)) + recv_ref[2].astype(F32)

    kern, token_spec, token = _behind(body, after, n_prefetch=1)
    return pl.pallas_call(
        kern,
        grid_spec=pltpu.PrefetchScalarGridSpec(
            num_scalar_prefetch=1, grid=(nt,),
            in_specs=token_spec + [pl.BlockSpec((1, tm, cols), lambda i, w: (w[0], w[1] * nt + i, 0)),
                                   pl.BlockSpec((1, tm, cols), lambda i, w: (w[0], i, 0)),
                                   pl.BlockSpec((3, tm, cols), lambda i, w: (0, i, 0))],
            out_specs=pl.BlockSpec((tm, cols), lambda i, w: (w[1] * nt + i, 0))),
        out_shape=jax.ShapeDtypeStruct((2 * h, cols), F32),
        name="add_chip_grads",
    )(where, *token, g, r, recv)


def _half_copies(srcs, _, send_sems, recv_sems):
    x, y, c = _place()
    out = []
    for a, src in enumerate(srcs):
        h = src.shape[0] // 2
        mine = src.at[pl.ds(pl.multiple_of(c * h, 8), h)]
        out.append(_remote(mine, mine, send_sems, recv_sems, a, (x, y, 1 - c)))
    return out


def _share_halves(name, shards):
    return _SplitExchange(name, shards, [], len(shards), _half_copies)


def _reduce_small(rows):
    cols = rows[0].shape[1]
    n = len(rows)
    assert sum(r.shape[0] for r in rows) <= 8

    def body(*refs):
        ins, out_ref = refs[:n], refs[n]
        vec_ref, gath_ref, send_sems, recv_sems = refs[n + 1:]
        x, y, c = _place()
        me = 4 * x + 2 * y + c
        vec_ref[...] = jnp.zeros(vec_ref.shape, F32)
        at = 0
        for r in ins:
            vec_ref[at:at + r.shape[0], :] = r[...]
            at += r.shape[0]
        copies = []
        for k in range(1, 8):
            peer = (_flip(x, (k >> 2) & 1), _flip(y, (k >> 1) & 1), _flip(c, k & 1))
            copies.append(_remote(vec_ref, gath_ref.at[me], send_sems, recv_sems, k - 1, peer))
        for cp in copies:
            cp.start()
        gath_ref[me] = vec_ref[...]
        for cp in copies:
            cp.wait()
        tot = gath_ref[0]
        for dev in range(1, 8):
            tot = tot + gath_ref[dev]
        out_ref[...] = tot
        out_ref[7:8, :] = jnp.zeros((1, cols), F32) + jnp.sum(tot[7:8, :])

    vm = pl.BlockSpec(memory_space=pltpu.VMEM)
    return pl.pallas_call(
        body, in_specs=[vm] * n, out_specs=vm,
        out_shape=jax.ShapeDtypeStruct((8, cols), F32),
        scratch_shapes=[pltpu.VMEM((8, cols), F32), pltpu.VMEM((8, 8, cols), F32),
                        pltpu.SemaphoreType.DMA((7,)), pltpu.SemaphoreType.DMA((7,))],
        name="reduce_small",
    )(*rows)


def _adamw_tile(w_ref, g_ref, m_ref, v_ref, d_ref, m2_ref, v2_ref, gout_ref):
    gr = g_ref[...]
    m2 = ADAM_B1 * m_ref[...] + (1.0 - ADAM_B1) * gr
    v2 = ADAM_B2 * v_ref[...] + (1.0 - ADAM_B2) * (gr * gr)
    m_hat = m2 / (1.0 - ADAM_B1 ** ADAM_STEP)
    v_hat = v2 / (1.0 - ADAM_B2 ** ADAM_STEP)
    d_ref[...] = -ADAM_LR * (m_hat / (jnp.sqrt(v_hat) + ADAM_EPS) + ADAM_WD * w_ref[...])
    m2_ref[...] = m2
    v2_ref[...] = v2
    gout_ref[...] = gr


def _adamw(w, g, m, v, name):
    rows, cols = w.shape
    tm = 128 if rows % 128 == 0 else rows

    def body(*refs):
        _adamw_tile(*refs)

    spec = pl.BlockSpec((tm, cols), lambda i: (i, 0))
    sds = jax.ShapeDtypeStruct((rows, cols), F32)
    return pl.pallas_call(body, grid=(rows // tm,), in_specs=[spec] * 4, out_specs=[spec] * 4,
                          out_shape=[sds] * 4, name=name)(w, g, m, v)


def _adamw_half(w, g, m, v, half, name, other=()):
    rows, cols = w.shape
    tm = 128
    nt = rows // 2 // tm

    def body(half_ref, *refs):
        _adamw_tile(*refs[:4], *refs[4 + len(other):])

    spec = pl.BlockSpec((tm, cols), lambda i, h: (h[0] * nt + i, 0))
    sds = jax.ShapeDtypeStruct((rows, cols), F32)
    return pl.pallas_call(
        body,
        grid_spec=pltpu.PrefetchScalarGridSpec(
            num_scalar_prefetch=1, grid=(nt,), in_specs=[spec] * 4 + [_ANY] * len(other), out_specs=[spec] * 4),
        out_shape=[sds] * 4, input_output_aliases={5 + k: k for k in range(len(other))}, name=name,
    )(half, w, g, m, v, *other)


def kernel(x, norm_g, w_in, b_merge, conv_w, w_out_conv, w_out_attn, w_o, final_g, loss_target, m_norm_g, m_w_in, m_b_merge, m_conv_w, m_w_out_conv, m_w_out_attn, m_w_o, m_final_g, v_norm_g, v_w_in, v_b_merge, v_conv_w, v_w_out_conv, v_w_out_attn, v_w_o, v_final_g):
    mx, my, mc = _place()
    chip = (2 * mx + my).astype(jnp.int32)
    seq = x.shape[1]

    chip1 = chip.reshape(1)
    slots = [*_place_shard([w_in[0]], chip1, MXU_DTYPE),
             *_place_shard([w_out_conv[0], w_out_attn[0], w_o[0]], chip1, MXU_DTYPE)]
    (taps_slot,) = _place_shard([jnp.pad(conv_w[0], ((0, 5), (0, 0)))], chip1, F32)
    gather_near = _SplitExchange("gather_w_in_near", [slots[0], taps_slot], [], 4,
                                 _gather_copies_to((0, 1), whole=(1,)))
    st = _Step(x[0], loss_target[0], norm_g, chip1, after=gather_near.token)
    st.project_own(w_in[0])
    near, _, _ = gather_near.wait([st.ut, st.proj_own])
    gather_far = _SplitExchange("gather_w_in_far", near, [], 2, _gather_copies_to((2,), whole=(1,)))
    (w4,) = _forward_to_sibling(gather_far.bufs[:1], flips=(0, 1))
    st.project(w4, (2, 1))
    (w4, taps4), _, out_slots = gather_far.wait([st.proj_own], riders=slots[1:], bufs=[w4, gather_far.bufs[1]])
    gather_out = _SplitExchange("gather_w_out", out_slots, [], 9, _gather_copies_to((0, 1, 2)), riders=[w4])
    (w4,) = _forward_to_sibling(gather_out.riders, flips=(2,))
    st.project(w4, (3,))
    st.mixers(w4, jnp.concatenate([taps4[j, :3, :] for j in range(4)], axis=1))
    out_ws, _, _ = gather_out.wait(st.o)
    woc, woa, wo = [w.reshape(D_MODEL, D_MODEL) for w in _forward_to_sibling(out_ws)]
    st.merge_and_loss(woc, woa, wo, b_merge, final_g.reshape(1, D_MODEL))

    half = mc.astype(jnp.int32).reshape(1)
    where = jnp.stack([chip, mc.astype(jnp.int32)])
    out_grads = [g.reshape(4, -1, D_MODEL) for g in st.out_weight_grads()]
    to_sibling = _grads_to_sibling("out_grads_to_sibling", out_grads)
    st.conv_grads(after=to_sibling.token)
    out_grads, out_from_sibling, _ = to_sibling.wait(st.da4)
    to_chips = _grads_to_chips("out_grads_to_chips",
                               _add_halves(out_grads, out_from_sibling, where))
    d_w4 = st.in_weight_grad(after=to_chips.token)
    out_from_chips = to_chips.wait(st.dc3)[1]

    to_sibling = _grads_to_sibling("in_grad_to_sibling", [d_w4])
    gx_lo, dg_lo = st.input_grad(0, after=to_sibling.token)
    (d_w4,), (from_sibling,), _ = to_sibling.wait(gx_lo)
    to_chips = _grads_to_chips("in_grad_to_chips", _add_halves([d_w4], [from_sibling], where))

    out_reduced = [_add_chips(g, r, recv, where, after=to_chips.token)
                   for g, r, recv in zip(out_grads, out_from_sibling, out_from_chips)]
    share_out = _share_halves("share_out_grads", out_reduced)
    gx_hi, dg_hi = st.input_grad(1, after=share_out.token)
    grad_x = _to_natural(gx_lo, gx_hi)
    g_woc, g_woa, g_wo = share_out.wait(gx_hi)[0]

    small = _reduce_small([dg_lo + dg_hi, st.d_bias.reshape(2, D_MODEL), st.d_taps, st.d_final_g,
                           st.loss8.reshape(1, D_MODEL)])
    loss = (0.5 / D_MODEL) * small[7, 0]
    g_taps = lax.dynamic_slice(small[3:6], (0, chip * (D_MODEL // 4)), (3, D_MODEL // 4))
    upd = {
        "norm_g": _adamw(norm_g, small[0:1], m_norm_g, v_norm_g, "adamw_norm_g"),
        "b_merge": _adamw(b_merge, small[1:3].reshape(1, 2 * D_MODEL), m_b_merge, v_b_merge, "adamw_b_merge"),
        "conv_w": _adamw(conv_w[0], g_taps, m_conv_w[0], v_conv_w[0], "adamw_conv_w"),
        "w_out_conv": _adamw(w_out_conv[0], g_woc, m_w_out_conv[0], v_w_out_conv[0], "adamw_w_out_conv"),
        "w_out_attn": _adamw(w_out_attn[0], g_woa, m_w_out_attn[0], v_w_out_attn[0], "adamw_w_out_attn"),
        "w_o": _adamw(w_o[0], g_wo, m_w_o[0], v_w_o[0], "adamw_w_o"),
        "final_g": _adamw(final_g.reshape(1, D_MODEL), small[6:7], m_final_g.reshape(1, D_MODEL),
                          v_final_g.reshape(1, D_MODEL), "adamw_final_g"),
    }
    behind = [grad_x] + [u[0] for u in upd.values()]
    in_reduced = _add_chips(d_w4, from_sibling, to_chips.wait(behind)[1][0], where)

    share_in = _share_halves("share_in_grad", [in_reduced])
    w_in_args = (w_in[0], m_w_in[0], v_w_in[0])
    own_rows = _adamw_half(w_in_args[0], share_in.bufs[0], *w_in_args[1:], half, "adamw_w_in_own_rows")
    (g_w_in,) = share_in.wait(own_rows[0])[0]
    upd["w_in"] = _adamw_half(w_in_args[0], g_w_in, *w_in_args[1:], 1 - half, "adamw_w_in_sibling_rows",
                              other=own_rows)

    names = ["norm_g", "w_in", "b_merge", "conv_w", "w_out_conv", "w_out_attn", "w_o", "final_g"]
    shapes = [norm_g.shape, w_in.shape, b_merge.shape, conv_w.shape, w_out_conv.shape, w_out_attn.shape,
              w_o.shape, final_g.shape]
    outs = [loss, grad_x.reshape(1, seq, D_MODEL)]
    for k in (3, 0, 1, 2):
        outs += [upd[n][k].reshape(s) for n, s in zip(names, shapes)]
    return tuple(outs)
```
